```python
import jax, jax.numpy as jnp
from jax import lax
import numpy as np

D_MODEL = 1024
BATCH = 8
SEQ = 4096
DEPTH = 4

HEAD_DIM = 64
FOX_HEADS = 8
SB_HEADS = 4
POOL_GROUPS = 4
POOL_WINDOWS = (2, 4, 8, 16)
POOL_GROUP_DIM = 64
FOX_W = FOX_HEADS * HEAD_DIM
SB_W = SB_HEADS * HEAD_DIM
POOL_W = POOL_GROUPS * POOL_GROUP_DIM
D_MIX = FOX_W + POOL_W + SB_W
Q_BLOCK = 128
EPS = 1e-6
NEG = -1e30

IN_SPLITS = (
    FOX_W, FOX_W, FOX_W, FOX_W,
    FOX_HEADS,
    POOL_W, POOL_W,
    SB_W, SB_W, SB_W, SB_W,
)
D_IN = sum(IN_SPLITS)

kernel_name = "hybrid_fox_pool_stickbreak_parallel_heads"


def rms_norm(x, g):
    xf = x.astype(jnp.float32)
    y = xf * lax.rsqrt(jnp.mean(xf * xf, axis=-1, keepdims=True) + EPS)
    return (y * g.astype(jnp.float32)).astype(x.dtype)


def to_heads(t, n_heads):
    b, s, _ = t.shape
    return t.reshape(b, s, n_heads, HEAD_DIM).transpose(0, 2, 1, 3)


def from_heads(t):
    b, h, s, d = t.shape
    return t.transpose(0, 2, 1, 3).reshape(b, s, h * d)


def split_blocks(t):
    b, h, s = t.shape[:3]
    nb = s // Q_BLOCK
    t = t.reshape((b, h, nb, Q_BLOCK) + t.shape[3:])
    return jnp.moveaxis(t, 2, 0)


def merge_blocks(o):
    nb, b, h, qb, d = o.shape
    return jnp.moveaxis(o, 0, 2).reshape(b, h, nb * qb, d)


def forgetting_attention(q, k, v, log_f):
    s_len, d = q.shape[2], q.shape[3]
    c = jnp.cumsum(log_f, axis=-1)
    kpos = jnp.arange(s_len)
    scale = d ** -0.5

    def one_block(args):
        qi, ci, i = args
        qpos = i * Q_BLOCK + jnp.arange(Q_BLOCK)
        sc = jnp.einsum('bhqd,bhkd->bhqk', qi, k).astype(jnp.float32) * scale
        sc = sc + (ci[..., :, None] - c[..., None, :])
        sc = jnp.where(kpos[None, :] <= qpos[:, None], sc, NEG)
        p = jax.nn.softmax(sc, axis=-1)
        return jnp.einsum('bhqk,bhkd->bhqd', p.astype(v.dtype), v)

    nb = s_len // Q_BLOCK
    o = lax.map(one_block, (split_blocks(q), split_blocks(c), jnp.arange(nb)))
    return merge_blocks(o)


def stick_breaking_attention(q, k, v):
    s_len, d = q.shape[2], q.shape[3]
    kpos = jnp.arange(s_len)
    scale = d ** -0.5

    def one_block(args):
        qi, i = args
        qpos = i * Q_BLOCK + jnp.arange(Q_BLOCK)
        z = jnp.einsum('bhqd,bhkd->bhqk', qi, k).astype(jnp.float32) * scale
        causal = kpos[None, :] < qpos[:, None]
        log_1m_beta = jnp.where(causal, -jax.nn.softplus(z), 0.0)
        rest = lax.cumsum(log_1m_beta, axis=3, reverse=True) - log_1m_beta
        a = jnp.where(causal, jnp.exp(jax.nn.log_sigmoid(z) + rest), 0.0)
        return jnp.einsum('bhqk,bhkd->bhqd', a.astype(v.dtype), v)

    nb = s_len // Q_BLOCK
    o = lax.map(one_block, (split_blocks(q), jnp.arange(nb)))
    return merge_blocks(o)


def causal_window_mean(x, w):
    s_len = x.shape[1]
    xf = x.astype(jnp.float32)
    cs = jnp.cumsum(xf, axis=1)
    cs_prev = jnp.pad(cs, ((0, 0), (w, 0), (0, 0)))[:, :s_len]
    count = jnp.minimum(jnp.arange(s_len) + 1, w).astype(jnp.float32)
    return ((cs - cs_prev) / count[None, :, None]).astype(x.dtype)


def multiscale_pool(x, w_pool, scale):
    b, s_len, _ = x.shape
    groups = jnp.split(x, POOL_GROUPS, axis=-1)
    pooled = jnp.stack([causal_window_mean(g, w) - g for g, w in zip(groups, POOL_WINDOWS)], axis=2)
    y = jnp.einsum('bsgc,gcd->bsgd', pooled, w_pool).reshape(b, s_len, POOL_W)
    return y * scale


def hybrid_layer(x, norm_g, w_in, b_f, q_norm_g, k_norm_g, w_pool, pool_scale, w_out):
    h = rms_norm(x, norm_g)
    proj = jnp.einsum('bsd,de->bse', h, w_in)
    idx = np.cumsum(IN_SPLITS)[:-1].tolist()
    (fq, fk, fv, fg, ff, px, pg, sq, sk, sv, sg) = jnp.split(proj, idx, axis=-1)

    fq = rms_norm(to_heads(fq, FOX_HEADS), q_norm_g)
    fk = rms_norm(to_heads(fk, FOX_HEADS), k_norm_g)
    fv = to_heads(fv, FOX_HEADS)
    log_f = jax.nn.log_sigmoid((ff + b_f).astype(jnp.float32)).transpose(0, 2, 1)
    fox_out = from_heads(forgetting_attention(fq, fk, fv, log_f)) * jax.nn.silu(fg)

    pool_out = multiscale_pool(px, w_pool, pool_scale) * jax.nn.silu(pg)

    sb = stick_breaking_attention(to_heads(sq, SB_HEADS), to_heads(sk, SB_HEADS), to_heads(sv, SB_HEADS))
    sb_out = from_heads(sb) * jax.nn.silu(sg)

    mixed = jnp.concatenate([fox_out, pool_out, sb_out], axis=-1)
    return x + jnp.einsum('bse,ed->bsd', mixed, w_out)


def _fwd_setup_inputs(seed: int = 0) -> dict:
    key = jax.random.key(seed)
    ks = jax.random.split(key, 10)
    x = jax.random.normal(ks[0], (BATCH, SEQ, D_MODEL), jnp.float32)
    norm_g = 1.0 + 0.02 * jax.random.normal(ks[1], (DEPTH, D_MODEL), jnp.float32)
    w_in = jax.random.normal(ks[2], (DEPTH, D_MODEL, D_IN), jnp.float32) * D_MODEL ** -0.5
    b_f = jax.random.uniform(ks[3], (DEPTH, FOX_HEADS), jnp.float32, 1.0, 4.0)
    q_norm_g = 1.0 + 0.02 * jax.random.normal(ks[4], (DEPTH, HEAD_DIM), jnp.float32)
    k_norm_g = 1.0 + 0.02 * jax.random.normal(ks[5], (DEPTH, HEAD_DIM), jnp.float32)
    w_pool = jax.random.normal(ks[6], (DEPTH, POOL_GROUPS, POOL_GROUP_DIM, POOL_GROUP_DIM), jnp.float32) * POOL_GROUP_DIM ** -0.5
    pool_scale = 1.0 + 0.02 * jax.random.normal(ks[7], (DEPTH, POOL_W), jnp.float32)
    w_out = jax.random.normal(ks[8], (DEPTH, D_MIX, D_MODEL), jnp.float32) * D_MIX ** -0.5
    return {"x": x, "norm_g": norm_g, "w_in": w_in, "b_f": b_f, "q_norm_g": q_norm_g,
            "k_norm_g": k_norm_g, "w_pool": w_pool, "pool_scale": pool_scale, "w_out": w_out}


def _fwd_reference(x, norm_g, w_in, b_f, q_norm_g, k_norm_g, w_pool, pool_scale, w_out):
    for l in range(DEPTH):
        x = hybrid_layer(x, norm_g[l], w_in[l], b_f[l], q_norm_g[l], k_norm_g[l],
                         w_pool[l], pool_scale[l], w_out[l])
    return x


import jax as _jax
import jax.numpy as _jnp

TWIN_FORMAT = 'train_step'
FWD_PARAMS = ['x', 'norm_g', 'w_in', 'b_f', 'q_norm_g', 'k_norm_g', 'w_pool', 'pool_scale', 'w_out']
TWIN_WEIGHTS = ['norm_g', 'w_in', 'b_f', 'q_norm_g', 'k_norm_g', 'w_pool', 'pool_scale', 'w_out']
TWIN_DIFF_INPUT = 'x'
TWIN_INPUTS = ['x', 'norm_g', 'w_in', 'b_f', 'q_norm_g', 'k_norm_g', 'w_pool', 'pool_scale', 'w_out', 'loss_target', 'm_norm_g', 'm_w_in', 'm_b_f', 'm_q_norm_g', 'm_k_norm_g', 'm_w_pool', 'm_pool_scale', 'm_w_out', 'v_norm_g', 'v_w_in', 'v_b_f', 'v_q_norm_g', 'v_k_norm_g', 'v_w_pool', 'v_pool_scale', 'v_w_out']
TWIN_OUTPUTS = ['loss', 'grad_x', 'grad_norm_g', 'grad_w_in', 'grad_b_f', 'grad_q_norm_g', 'grad_k_norm_g', 'grad_w_pool', 'grad_pool_scale', 'grad_w_out', 'delta_norm_g', 'delta_w_in', 'delta_b_f', 'delta_q_norm_g', 'delta_k_norm_g', 'delta_w_pool', 'delta_pool_scale', 'delta_w_out', 'new_m_norm_g', 'new_m_w_in', 'new_m_b_f', 'new_m_q_norm_g', 'new_m_k_norm_g', 'new_m_w_pool', 'new_m_pool_scale', 'new_m_w_out', 'new_v_norm_g', 'new_v_w_in', 'new_v_b_f', 'new_v_q_norm_g', 'new_v_k_norm_g', 'new_v_w_pool', 'new_v_pool_scale', 'new_v_w_out']
TWIN_LEAF_KINDS = {'loss': 'loss', 'grad_x': 'grad_x', 'grad_norm_g': 'grad_w', 'grad_w_in': 'grad_w', 'grad_b_f': 'grad_w', 'grad_q_norm_g': 'grad_w', 'grad_k_norm_g': 'grad_w', 'grad_w_pool': 'grad_w', 'grad_pool_scale': 'grad_w', 'grad_w_out': 'grad_w', 'delta_norm_g': 'delta_w', 'delta_w_in': 'delta_w', 'delta_b_f': 'delta_w', 'delta_q_norm_g': 'delta_w', 'delta_k_norm_g': 'delta_w', 'delta_w_pool': 'delta_w', 'delta_pool_scale': 'delta_w', 'delta_w_out': 'delta_w', 'new_m_norm_g': 'new_m', 'new_m_w_in': 'new_m', 'new_m_b_f': 'new_m', 'new_m_q_norm_g': 'new_m', 'new_m_k_norm_g': 'new_m', 'new_m_w_pool': 'new_m', 'new_m_pool_scale': 'new_m', 'new_m_w_out': 'new_m', 'new_v_norm_g': 'new_v', 'new_v_w_in': 'new_v', 'new_v_b_f': 'new_v', 'new_v_q_norm_g': 'new_v', 'new_v_k_norm_g': 'new_v', 'new_v_w_pool': 'new_v', 'new_v_pool_scale': 'new_v', 'new_v_w_out': 'new_v'}


def _forward(args):
    return _fwd_reference(*[args[k] for k in FWD_PARAMS])


def _output_shape():
    def fwd():
        inp = _fwd_setup_inputs(0)
        return _fwd_reference(*[inp[k] for k in FWD_PARAMS])
    out = _jax.eval_shape(fwd)
    return out.shape, out.dtype

N_MICROBATCH = 1
ADAM_LR = 0.001
ADAM_B1 = 0.9
ADAM_B2 = 0.999
ADAM_EPS = 1e-08
ADAM_WD = 0.01
ADAM_STEP = 10
PER_EXAMPLE_BATCH_AXIS = {'x': 0, 'loss_target': 0}
SHARED_INPUTS = []
_WEIGHT_DTYPES = {'norm_g': _jnp.float32, 'w_in': _jnp.float32, 'b_f': _jnp.float32, 'q_norm_g': _jnp.float32, 'k_norm_g': _jnp.float32, 'w_pool': _jnp.float32, 'pool_scale': _jnp.float32, 'w_out': _jnp.float32}
MOMENT_SCALE = {'norm_g': 9.472511e+00, 'w_in': 1.797263e-01, 'b_f': 3.348898e+01, 'q_norm_g': 4.581384e+00, 'k_norm_g': 4.573357e+00, 'w_pool': 1.054915e+00, 'pool_scale': 9.000356e+00, 'w_out': 1.960890e-01}


def _to_microbatches(a, axis):
    t = _jnp.moveaxis(a, axis, 0)
    t = t.reshape((N_MICROBATCH, t.shape[0] // N_MICROBATCH) + t.shape[1:])
    return _jnp.moveaxis(t, 1, axis + 1)


def setup_inputs(seed: int = 0) -> dict:
    inp = _fwd_setup_inputs(seed)
    key = _jax.random.fold_in(_jax.random.key(seed), 7919)
    shape, _ = _output_shape()
    out = dict(inp)
    out["loss_target"] = _jax.random.normal(_jax.random.fold_in(key, 0), shape, _jnp.float32)
    for i, name in enumerate(TWIN_WEIGHTS):
        w = inp[name].astype(_jnp.float32)
        if MOMENT_SCALE is None:
            s = _jnp.sqrt(_jnp.mean(_jnp.square(w)) + 1e-30)
        else:
            s = MOMENT_SCALE[name]
        km, kv = _jax.random.split(_jax.random.fold_in(key, i + 1))
        out[name] = w
        out["m_" + name] = s * _jax.random.normal(km, w.shape, _jnp.float32)
        out["v_" + name] = (s * s) * _jax.random.uniform(kv, w.shape, _jnp.float32, 0.5, 1.5)
    if N_MICROBATCH > 1:
        for name, axis in PER_EXAMPLE_BATCH_AXIS.items():
            out[name] = _to_microbatches(out[name], axis)
    return {'x': out['x'], 'norm_g': out['norm_g'], 'w_in': out['w_in'], 'b_f': out['b_f'], 'q_norm_g': out['q_norm_g'], 'k_norm_g': out['k_norm_g'], 'w_pool': out['w_pool'], 'pool_scale': out['pool_scale'], 'w_out': out['w_out'], 'loss_target': out['loss_target'], 'm_norm_g': out['m_norm_g'], 'm_w_in': out['m_w_in'], 'm_b_f': out['m_b_f'], 'm_q_norm_g': out['m_q_norm_g'], 'm_k_norm_g': out['m_k_norm_g'], 'm_w_pool': out['m_w_pool'], 'm_pool_scale': out['m_pool_scale'], 'm_w_out': out['m_w_out'], 'v_norm_g': out['v_norm_g'], 'v_w_in': out['v_w_in'], 'v_b_f': out['v_b_f'], 'v_q_norm_g': out['v_q_norm_g'], 'v_k_norm_g': out['v_k_norm_g'], 'v_w_pool': out['v_w_pool'], 'v_pool_scale': out['v_pool_scale'], 'v_w_out': out['v_w_out']}


def _loss(weights, diff, rest, loss_target):
    with _jax.named_scope("forward"):
        args = {**rest, TWIN_DIFF_INPUT: diff, **{k: w.astype(_WEIGHT_DTYPES[k]) for k, w in weights.items()}}
        y = _forward(args)
    with _jax.named_scope("loss_head"):
        err = _jnp.square(y.astype(_jnp.float32) - loss_target)
        return 0.5 * _jnp.sum(_jnp.mean(err, axis=-1)) if err.ndim else 0.5 * err


def _adamw(w, g, m, v):
    m = ADAM_B1 * m + (1.0 - ADAM_B1) * g
    v = ADAM_B2 * v + (1.0 - ADAM_B2) * _jnp.square(g)
    m_hat = m / (1.0 - ADAM_B1 ** ADAM_STEP)
    v_hat = v / (1.0 - ADAM_B2 ** ADAM_STEP)
    delta = -ADAM_LR * (m_hat / (_jnp.sqrt(v_hat) + ADAM_EPS) + ADAM_WD * w)
    return delta, m, v


def reference(x, norm_g, w_in, b_f, q_norm_g, k_norm_g, w_pool, pool_scale, w_out, loss_target, m_norm_g, m_w_in, m_b_f, m_q_norm_g, m_k_norm_g, m_w_pool, m_pool_scale, m_w_out, v_norm_g, v_w_in, v_b_f, v_q_norm_g, v_k_norm_g, v_w_pool, v_pool_scale, v_w_out):
    given = dict(x=x, norm_g=norm_g, w_in=w_in, b_f=b_f, q_norm_g=q_norm_g, k_norm_g=k_norm_g, w_pool=w_pool, pool_scale=pool_scale, w_out=w_out, loss_target=loss_target, m_norm_g=m_norm_g, m_w_in=m_w_in, m_b_f=m_b_f, m_q_norm_g=m_q_norm_g, m_k_norm_g=m_k_norm_g, m_w_pool=m_w_pool, m_pool_scale=m_pool_scale, m_w_out=m_w_out, v_norm_g=v_norm_g, v_w_in=v_w_in, v_b_f=v_b_f, v_q_norm_g=v_q_norm_g, v_k_norm_g=v_k_norm_g, v_w_pool=v_w_pool, v_pool_scale=v_pool_scale, v_w_out=v_w_out)
    weights = {n: given[n] for n in TWIN_WEIGHTS}
    shared = {n: given[n] for n in SHARED_INPUTS}
    per_example = {n: given[n] for n in ['x']}
    grad_fn = _jax.value_and_grad(_loss, argnums=(0, 1))

    def one_microbatch(ex, loss_target):
        ex = dict(ex)
        diff = ex.pop(TWIN_DIFF_INPUT)
        return grad_fn(weights, diff, {**shared, **ex}, loss_target)

    if N_MICROBATCH == 1:
        loss, (grad_w, grad_x) = one_microbatch(per_example, given["loss_target"])
    else:
        def body(carry, xs):
            loss_sum, grad_sum = carry
            l_k, (gw_k, gx_k) = one_microbatch(xs[0], xs[1])
            with _jax.named_scope("update"):
                return (loss_sum + l_k, _jax.tree.map(_jnp.add, grad_sum, gw_k)), gx_k

        init = (_jnp.zeros((), _jnp.float32), _jax.tree.map(_jnp.zeros_like, weights))
        (loss, grad_w), grad_x = _jax.lax.scan(body, init, (per_example, given["loss_target"]))
    with _jax.named_scope("update"):
        delta_w, new_m, new_v = {}, {}, {}
        for n in TWIN_WEIGHTS:
            delta_w[n], new_m[n], new_v[n] = _adamw(weights[n], grad_w[n], given["m_" + n], given["v_" + n])
    return (loss, grad_x, *[grad_w[n] for n in TWIN_WEIGHTS], *[delta_w[n] for n in TWIN_WEIGHTS],
            *[new_m[n] for n in TWIN_WEIGHTS], *[new_v[n] for n in TWIN_WEIGHTS])
```

```python
import functools

import jax
import jax.numpy as jnp
from jax import lax
from jax.experimental import pallas as pl
from jax.experimental.pallas import tpu as pltpu

F32 = jnp.float32
BF16 = jnp.bfloat16

DEPTH = 4
HEAD_DIM = 64
FOX_HEADS = 8
SB_HEADS = 4
FOX_W = FOX_HEADS * HEAD_DIM
SB_W = SB_HEADS * HEAD_DIM
POOL_W = 256
POOL_WINDOWS = (2, 4, 8, 16)
POOL_HALO = 16
D_MIX = FOX_W + POOL_W + SB_W
EPS = 1e-6
NEG = -1e30
QK_SCALE = HEAD_DIM ** -0.5

ORIG_FOX = 4 * FOX_W
ORIG_FF = ORIG_FOX
ORIG_REST = ORIG_FF + FOX_HEADS
D_IN = ORIG_REST + 2 * POOL_W + 4 * SB_W

C_FQ, C_FK, C_FV, C_FG = 0, FOX_W, 2 * FOX_W, 3 * FOX_W
C_PX = 4 * FOX_W
C_PG = C_PX + POOL_W
C_SQ = C_PG + POOL_W
C_SK, C_SV, C_SG = C_SQ + SB_W, C_SQ + 2 * SB_W, C_SQ + 3 * SB_W
PM = C_SG + SB_W
LANES = 128
PW = PM + LANES
FF_STRIDE = 8

ADAM_LR = 0.001
ADAM_B1 = 0.9
ADAM_B2 = 0.999
ADAM_EPS = 1e-08
ADAM_WD = 0.01
ADAM_STEP = 10

VMEM_LIMIT = 48 * 1024 * 1024


def _cparams(**kw):
    return pltpu.CompilerParams(vmem_limit_bytes=VMEM_LIMIT, **kw)


def _dot(a, b):
    return jnp.dot(a, b, preferred_element_type=F32)


def _dot_nt(a, b):
    return lax.dot_general(a, b, (((1,), (1,)), ((), ())), preferred_element_type=F32)


def _dot_tn(a, b):
    return lax.dot_general(a, b, (((0,), (0,)), ((), ())), preferred_element_type=F32)


def _split2(x):
    hi = x.astype(BF16)
    lo = (x - hi.astype(F32)).astype(BF16)
    return hi, lo


def _split3(x):
    hi = x.astype(BF16)
    r = x - hi.astype(F32)
    mid = r.astype(BF16)
    lo = (r - mid.astype(F32)).astype(BF16)
    return hi, mid, lo


def _dot_exact_rhs(x, m):
    hi, mid, lo = _split3(x)
    return _dot(hi, m) + _dot(mid, m) + _dot(lo, m)


def _dot_exact_lhs(m, x):
    hi, mid, lo = _split3(x)
    return _dot(m, hi) + _dot(m, mid) + _dot(m, lo)


def _sigmoid(x):
    return 1.0 / (1.0 + jnp.exp(-x))


def _silu_pair(x):
    s = _sigmoid(x)
    return x * s, s * (1.0 + x * (1.0 - s))


def _iota(shape, dim):
    return lax.broadcasted_iota(jnp.int32, shape, dim)


def _ones_where(cond):
    return jnp.where(cond, 1.0, 0.0).astype(BF16)


def _head_blockdiag(w):
    return _ones_where((_iota((w, w), 0) >> 6) == (_iota((w, w), 1) >> 6))


def _group_sum(x, bd):
    hi, lo = _split2(x)
    return _dot(hi, bd) + _dot(lo, bd)


def _lane_pick(x, lane_idx, lane):
    return jnp.sum(jnp.where(lane_idx == lane, x, 0.0), axis=1, keepdims=True)


def _inproj(x, g, w_all, *, tm, tn):
    S, D = x.shape
    nj = PM // tn

    def body(x_ref, g_ref, w_ref, wff_ref, proj_ref, ff_ref, h_ref):
        @pl.when(pl.program_id(1) == 0)
        def _():
            xf = x_ref[...]
            ms = jnp.mean(xf * xf, axis=-1, keepdims=True)
            h = (xf * lax.rsqrt(ms + EPS) * g_ref[...]).astype(BF16)
            h_ref[...] = h
            ff_ref[...] = _dot(h, wff_ref[...])

        proj_ref[...] = _dot(h_ref[...], w_ref[...])

    return pl.pallas_call(
        body, name="inproj", grid=(S // tm, nj),
        in_specs=[pl.BlockSpec((tm, D), lambda i, j: (i, 0)),
                  pl.BlockSpec((1, D), lambda i, j: (0, 0)),
                  pl.BlockSpec((D, tn), lambda i, j: (0, j)),
                  pl.BlockSpec((D, LANES), lambda i, j: (0, PM // LANES))],
        out_specs=[pl.BlockSpec((tm, tn), lambda i, j: (i, j)),
                   pl.BlockSpec((tm, LANES), lambda i, j: (i, 0)),
                   pl.BlockSpec((tm, D), lambda i, j: (i, 0))],
        out_shape=[jax.ShapeDtypeStruct((S, PM), F32), jax.ShapeDtypeStruct((S, LANES), F32),
                   jax.ShapeDtypeStruct((S, D), BF16)],
        compiler_params=_cparams(dimension_semantics=("arbitrary", "arbitrary")),
    )(x, g, w_all, w_all)


def _pool_group_select(lane_group, vals):
    return jnp.where(lane_group == 0, vals[0], jnp.where(lane_group == 1, vals[1], jnp.where(lane_group == 2, vals[2], vals[3])))


def _prep(projm, ffo, qg, kg, bfp, wpd, ps, *, ts):
    S = projm.shape[0]
    nb = S // ts
    hb = ts // POOL_HALO
    npair = FOX_HEADS // 2

    def body(fq_ref, fk_ref, fv_ref, pp_ref, halo_ref, ff_ref, sq_ref, sk_ref, sv_ref,
             qg_ref, kg_ref, bf_ref, wpd_ref, ps_ref,
             qn_ref, kn_ref, v_ref, cb_ref, ct_ref, sqo_ref, sko_ref, svo_ref, pooled_ref, yp_ref, pm_ref,
             carry_ref, c_ref, buf_ref):
        i = pl.program_id(0)
        bd = _head_blockdiag(FOX_W)
        for src, g_ref, dst, scale in ((fq_ref, qg_ref, qn_ref, QK_SCALE), (fk_ref, kg_ref, kn_ref, 1.0)):
            q = src[...]
            ss = _group_sum(q * q, bd)
            qn = q * lax.rsqrt(ss * (1.0 / HEAD_DIM) + EPS) * g_ref[...]
            dst[...] = (qn * scale).astype(BF16)
        v_ref[...] = fv_ref[...].astype(BF16)
        sqo_ref[...] = (sq_ref[...] * QK_SCALE).astype(BF16)
        sko_ref[...] = sk_ref[...].astype(BF16)
        svo_ref[...] = sv_ref[...].astype(BF16)

        @pl.when(i == 0)
        def _():
            carry_ref[...] = jnp.zeros_like(carry_ref)

        z = ff_ref[...] + bf_ref[...]
        lf = jnp.minimum(z, 0.0) - jnp.log(1.0 + jnp.exp(-jnp.abs(z)))
        tri = _ones_where(_iota((ts, ts), 1) <= _iota((ts, ts), 0))
        c = _dot_exact_lhs(tri, lf) + carry_ref[...]
        c_ref[...] = c
        carry_ref[...] = c_ref[ts - 1:ts, :]
        src_lane = _iota((LANES, LANES), 0)
        dst_lane = _iota((LANES, LANES), 1)
        ct = c.T
        for p in range(npair):
            sel = _ones_where(src_lane == jnp.where(dst_lane < HEAD_DIM, FF_STRIDE * p, FF_STRIDE * p + 1))
            cb_ref[p] = _dot_exact_rhs(c, sel)
            ct_ref[p] = ct[FF_STRIDE * p:FF_STRIDE * (p + 1), :]

        x = pp_ref[:, 0:POOL_W]
        pg = pp_ref[:, POOL_W:2 * POOL_W]
        halo = jnp.where(i > 0, halo_ref[:, 0:POOL_W], 0.0)
        buf_ref[0:POOL_HALO, :] = halo
        buf_ref[POOL_HALO:POOL_HALO + ts, :] = x
        acc = x
        snaps = []
        for d in range(1, POOL_HALO):
            acc = acc + buf_ref[pl.ds(POOL_HALO - d, ts), :]
            if d + 1 in POOL_WINDOWS:
                snaps.append(acc)
        lane_group = _iota((1, POOL_W), 1) >> 6
        wsum = _pool_group_select(lane_group, snaps)
        wlen = _pool_group_select(lane_group, [float(w) for w in POOL_WINDOWS])
        tpos = (i * ts + _iota((ts, 1), 0) + 1).astype(F32)
        pooled = wsum / jnp.minimum(tpos, wlen) - x
        pb = pooled.astype(BF16)
        pooled_ref[...] = pb
        yp = _dot(pb, wpd_ref[...])
        yp_ref[...] = yp
        pm_ref[...] = (yp * ps_ref[...] * (pg * _sigmoid(pg))).astype(BF16)

    blk = lambda w, c: pl.BlockSpec((ts, w), lambda i: (i, c))
    full = lambda a: pl.BlockSpec(a.shape, lambda i: (0,) * a.ndim)
    out_shapes = [
        jax.ShapeDtypeStruct((S, FOX_W), BF16), jax.ShapeDtypeStruct((S, FOX_W), BF16), jax.ShapeDtypeStruct((S, FOX_W), BF16),
        jax.ShapeDtypeStruct((npair, S, LANES), F32), jax.ShapeDtypeStruct((npair, FF_STRIDE, S), F32),
        jax.ShapeDtypeStruct((S, SB_W), BF16), jax.ShapeDtypeStruct((S, SB_W), BF16), jax.ShapeDtypeStruct((S, SB_W), BF16),
        jax.ShapeDtypeStruct((S, POOL_W), BF16), jax.ShapeDtypeStruct((S, POOL_W), F32), jax.ShapeDtypeStruct((S, POOL_W), BF16),
    ]
    out_specs = [
        blk(FOX_W, 0), blk(FOX_W, 0), blk(FOX_W, 0),
        pl.BlockSpec((npair, ts, LANES), lambda i: (0, i, 0)), pl.BlockSpec((npair, FF_STRIDE, ts), lambda i: (0, 0, i)),
        blk(SB_W, 0), blk(SB_W, 0), blk(SB_W, 0),
        blk(POOL_W, 0), blk(POOL_W, 0), blk(POOL_W, 0),
    ]
    return pl.pallas_call(
        body, name="prep", grid=(nb,),
        in_specs=[blk(FOX_W, C_FQ // FOX_W), blk(FOX_W, C_FK // FOX_W), blk(FOX_W, C_FV // FOX_W), blk(2 * POOL_W, C_PX // (2 * POOL_W)),
                  pl.BlockSpec((POOL_HALO, 2 * POOL_W), lambda i: (jnp.maximum(i * hb - 1, 0), C_PX // (2 * POOL_W))),
                  blk(LANES, 0),
                  blk(SB_W, C_SQ // SB_W), blk(SB_W, C_SK // SB_W), blk(SB_W, C_SV // SB_W),
                  full(qg), full(kg), full(bfp), full(wpd), full(ps)],
        out_specs=out_specs, out_shape=out_shapes,
        scratch_shapes=[pltpu.VMEM((1, LANES), F32), pltpu.VMEM((ts, LANES), F32), pltpu.VMEM((ts + POOL_HALO, POOL_W), F32)],
        compiler_params=_cparams(dimension_semantics=("arbitrary",)),
    )(projm, projm, projm, projm, projm, ffo, projm, projm, projm, qg, kg, bfp, wpd, ps)


def _pair_masks(x):
    ma = _iota((1, LANES), 1) < HEAD_DIM
    zero = jnp.zeros_like(x)
    return jnp.where(ma, x, zero), jnp.where(ma, zero, x)


def _num_kblocks(qi, tq, tk):
    return (qi * tq + tq + tk - 1) // tk


def _fox_fwd(qn, kn, v, cb, ct, projm, *, tq, tk):
    S = qn.shape[0]
    npair = FOX_HEADS // 2

    def body(q_ref, k_ref, v_ref, cb_ref, ct_ref, fg_ref, o_ref, lse_ref, fm_ref):
        qi = pl.program_id(1)
        lane = _iota((1, LANES), 1)
        ma = lane < HEAD_DIM
        qh = _pair_masks(q_ref[...])
        cbv = cb_ref[0]
        ctq = (_lane_pick(cbv, lane, 0), _lane_pick(cbv, lane, HEAD_DIM))
        qpos = qi * tq + _iota((tq, tk), 0)
        kiota = _iota((tq, tk), 1)

        def step(j, carry):
            k0 = pl.multiple_of(j * tk, tk)
            kb = k_ref[pl.ds(k0, tk), :]
            vb = v_ref[pl.ds(k0, tk), :]
            mask = (k0 + kiota) <= qpos
            new = []
            for h in range(2):
                m, l, acc = carry[h]
                cs = ct_ref[0, h:h + 1, pl.ds(k0, tk)]
                s = _dot_nt(qh[h], kb) + (ctq[h] - cs)
                s = jnp.where(mask, s, NEG)
                m_new = jnp.maximum(m, jnp.max(s, axis=1, keepdims=True))
                alpha = jnp.exp(m - m_new)
                p = jnp.exp(s - m_new)
                l = alpha * l + jnp.sum(p, axis=1, keepdims=True)
                acc = alpha * acc + _dot(p.astype(BF16), vb)
                new.append((m_new, l, acc))
            return tuple(new)

        init = tuple((jnp.full((tq, 1), NEG, F32), jnp.zeros((tq, 1), F32), jnp.zeros((tq, LANES), F32)) for _ in range(2))
        (ma_, la, acca), (mb_, lb, accb) = lax.fori_loop(0, _num_kblocks(qi, tq, tk), step, init)
        o = jnp.where(ma, acca / la, accb / lb)
        o_ref[...] = o
        lse_ref[...] = jnp.where(ma, ma_ + jnp.log(la), mb_ + jnp.log(lb))
        fg = fg_ref[...]
        fm_ref[...] = (o * (fg * _sigmoid(fg))).astype(BF16)

    qblk = pl.BlockSpec((tq, LANES), lambda p, i: (i, p))
    kvblk = pl.BlockSpec((S, LANES), lambda p, i: (0, p))
    return pl.pallas_call(
        body, name="fox_fwd", grid=(npair, S // tq),
        in_specs=[qblk, kvblk, kvblk,
                  pl.BlockSpec((1, tq, LANES), lambda p, i: (p, i, 0)),
                  pl.BlockSpec((1, FF_STRIDE, S), lambda p, i: (p, 0, 0)),
                  pl.BlockSpec((tq, LANES), lambda p, i: (i, C_FG // LANES + p))],
        out_specs=[qblk, qblk, qblk],
        out_shape=[jax.ShapeDtypeStruct((S, FOX_W), F32), jax.ShapeDtypeStruct((S, FOX_W), F32), jax.ShapeDtypeStruct((S, FOX_W), BF16)],
        compiler_params=_cparams(dimension_semantics=("arbitrary", "arbitrary")),
    )(qn, kn, v, cb, ct, projm)


def _sb_scores(qh, kb, causal, tmat, r_run):
    z = _dot_nt(qh, kb)
    e = jnp.exp(-jnp.abs(z))
    sp = jnp.maximum(z, 0.0) + jnp.log(1.0 + e)
    lb = jnp.where(causal, -sp, 0.0)
    hi, lo = _split2(lb)
    rin = _dot(hi, tmat) + _dot(lo, tmat)
    a = jnp.exp(jnp.where(causal, z + lb + (rin + r_run), NEG))
    return z, e, lb, rin, a


def _sb_fwd(sq, sk, sv, projm, *, tq, tk):
    S = sq.shape[0]
    npair = SB_HEADS // 2

    def body(q_ref, k_ref, v_ref, sg_ref, o_ref, sm_ref):
        qi = pl.program_id(1)
        lane = _iota((1, LANES), 1)
        ma = lane < HEAD_DIM
        qh = _pair_masks(q_ref[...])
        qpos = qi * tq + _iota((tq, tk), 0)
        kiota = _iota((tq, tk), 1)
        col0 = kiota == 0
        tmat = _ones_where(_iota((tk, tk), 0) > _iota((tk, tk), 1))
        nk = _num_kblocks(qi, tq, tk)

        def step(jj, carry):
            k0 = pl.multiple_of((nk - 1 - jj) * tk, tk)
            kb = k_ref[pl.ds(k0, tk), :]
            vb = v_ref[pl.ds(k0, tk), :]
            causal = (k0 + kiota) < qpos
            new = []
            for h in range(2):
                r_run, acc = carry[h]
                _, _, lb, rin, a = _sb_scores(qh[h], kb, causal, tmat, r_run)
                ahi, alo = _split2(a)
                acc = acc + (_dot(ahi, vb) + _dot(alo, vb))
                r_run = r_run + jnp.sum(jnp.where(col0, rin + lb, 0.0), axis=1, keepdims=True)
                new.append((r_run, acc))
            return tuple(new)

        init = tuple((jnp.zeros((tq, 1), F32), jnp.zeros((tq, LANES), F32)) for _ in range(2))
        (_, acca), (_, accb) = lax.fori_loop(0, nk, step, init)
        o = jnp.where(ma, acca, accb)
        o_ref[...] = o
        sg = sg_ref[...]
        sm_ref[...] = (o * (sg * _sigmoid(sg))).astype(BF16)

    qblk = pl.BlockSpec((tq, LANES), lambda p, i: (i, p))
    kvblk = pl.BlockSpec((S, LANES), lambda p, i: (0, p))
    return pl.pallas_call(
        body, name="sb_fwd", grid=(npair, S // tq),
        in_specs=[qblk, kvblk, kvblk, pl.BlockSpec((tq, LANES), lambda p, i: (i, C_SG // LANES + p))],
        out_specs=[qblk, qblk],
        out_shape=[jax.ShapeDtypeStruct((S, SB_W), F32), jax.ShapeDtypeStruct((S, SB_W), BF16)],
        compiler_params=_cparams(dimension_semantics=("arbitrary", "arbitrary")),
    )(sq, sk, sv, projm)


def _outproj(x, fm, pm, sm, w_out, *, tm):
    S, D = x.shape

    def body(x_ref, fm_ref, pm_ref, sm_ref, w_ref, y_ref):
        y = x_ref[...] + _dot(fm_ref[...], w_ref[0:FOX_W, :])
        y = y + _dot(pm_ref[...], w_ref[FOX_W:FOX_W + POOL_W, :])
        y_ref[...] = y + _dot(sm_ref[...], w_ref[FOX_W + POOL_W:D_MIX, :])

    row = lambda w: pl.BlockSpec((tm, w), lambda i: (i, 0))
    return pl.pallas_call(
        body, name="outproj", grid=(S // tm,),
        in_specs=[row(D), row(FOX_W), row(POOL_W), row(SB_W), pl.BlockSpec((D_MIX, D), lambda i: (0, 0))],
        out_specs=row(D), out_shape=jax.ShapeDtypeStruct((S, D), F32),
        compiler_params=_cparams(dimension_semantics=("arbitrary",)),
    )(x, fm, pm, sm, w_out)


def _loss_head(y, target, *, tm):
    S, D = y.shape

    def body(y_ref, t_ref, dy_ref, sq_ref):
        @pl.when(pl.program_id(0) == 0)
        def _():
            sq_ref[...] = jnp.zeros_like(sq_ref)

        d = y_ref[...] - t_ref[...]
        dy_ref[...] = d * (1.0 / D)
        sq_ref[...] += jnp.sum(d * d, axis=0, keepdims=True)

    row = pl.BlockSpec((tm, D), lambda i: (i, 0))
    return pl.pallas_call(
        body, name="loss_head", grid=(S // tm,),
        in_specs=[row, row], out_specs=[row, pl.BlockSpec((1, D), lambda i: (0, 0))],
        out_shape=[jax.ShapeDtypeStruct((S, D), F32), jax.ShapeDtypeStruct((1, D), F32)],
        compiler_params=_cparams(dimension_semantics=("arbitrary",)),
    )(y, target)


def _outproj_bwd(dy, fm, pm, sm, w_out, *, tm):
    S, D = dy.shape

    def body(dy_ref, fm_ref, pm_ref, sm_ref, w_ref, dm_ref, dw_ref):
        @pl.when(pl.program_id(0) == 0)
        def _():
            dw_ref[...] = jnp.zeros_like(dw_ref)

        dyb = dy_ref[...].astype(BF16)
        dm_ref[...] = _dot_nt(dyb, w_ref[...])
        dw_ref[0:FOX_W, :] += _dot_tn(fm_ref[...], dyb)
        dw_ref[FOX_W:FOX_W + POOL_W, :] += _dot_tn(pm_ref[...], dyb)
        dw_ref[FOX_W + POOL_W:D_MIX, :] += _dot_tn(sm_ref[...], dyb)

    row = lambda w: pl.BlockSpec((tm, w), lambda i: (i, 0))
    wspec = pl.BlockSpec((D_MIX, D), lambda i: (0, 0))
    return pl.pallas_call(
        body, name="outproj_bwd", grid=(S // tm,),
        in_specs=[row(D), row(FOX_W), row(POOL_W), row(SB_W), wspec],
        out_specs=[row(D_MIX), wspec],
        out_shape=[jax.ShapeDtypeStruct((S, D_MIX), F32), jax.ShapeDtypeStruct((D_MIX, D), F32)],
        compiler_params=_cparams(dimension_semantics=("arbitrary",)),
    )(dy, fm, pm, sm, w_out)


def _fox_bwd(qn, kn, v, cb, ct, o, lse, dmix, projm, *, tq, tk):
    S = qn.shape[0]
    npair = FOX_HEADS // 2

    def body(q_ref, k_ref, v_ref, cb_ref, ct_ref, o_ref, lse_ref, dm_ref, fg_ref,
             dq_ref, dk_ref, dv_ref, dfg_ref, dct_ref, dcr_ref):
        qi = pl.program_id(1)

        @pl.when(qi == 0)
        def _():
            dk_ref[...] = jnp.zeros_like(dk_ref)
            dv_ref[...] = jnp.zeros_like(dv_ref)
            dct_ref[...] = jnp.zeros_like(dct_ref)

        lane = _iota((1, LANES), 1)
        ma = lane < HEAD_DIM
        qh = _pair_masks(q_ref[...])
        cbv = cb_ref[0]
        ctq = (_lane_pick(cbv, lane, 0), _lane_pick(cbv, lane, HEAD_DIM))
        lsev = lse_ref[...]
        lse = (_lane_pick(lsev, lane, 0), _lane_pick(lsev, lane, HEAD_DIM))
        fg = fg_ref[...]
        silu, dsilu = _silu_pair(fg)
        dm = dm_ref[...]
        ov = o_ref[...]
        do = dm * silu
        dfg_ref[...] = dm * ov * dsilu
        dd = do * ov
        dsum = (jnp.sum(jnp.where(ma, dd, 0.0), axis=1, keepdims=True), jnp.sum(jnp.where(ma, 0.0, dd), axis=1, keepdims=True))
        doh = _pair_masks(do.astype(BF16))
        qpos = qi * tq + _iota((tq, tk), 0)
        kiota = _iota((tq, tk), 1)

        def step(j, carry):
            dq = carry[0]
            rows = list(carry[1:])
            k0 = pl.multiple_of(j * tk, tk)
            kb = k_ref[pl.ds(k0, tk), :]
            vb = v_ref[pl.ds(k0, tk), :]
            kh = _pair_masks(kb)
            mask = (k0 + kiota) <= qpos
            dv_blk = jnp.zeros((tk, LANES), F32)
            dk_blk = jnp.zeros((tk, LANES), F32)
            for h in range(2):
                cs = ct_ref[0, h:h + 1, pl.ds(k0, tk)]
                s = _dot_nt(qh[h], kb) + (ctq[h] - cs)
                p = jnp.exp(jnp.where(mask, s, NEG) - lse[h])
                dp = _dot_nt(doh[h], vb)
                dsf = p * (dp - dsum[h])
                dct_ref[0, h:h + 1, pl.ds(k0, tk)] -= jnp.sum(dsf, axis=0, keepdims=True)
                rows[h] = rows[h] + jnp.sum(dsf, axis=1, keepdims=True)
                ds = dsf.astype(BF16)
                dv_blk = dv_blk + _dot_tn(p.astype(BF16), doh[h])
                dk_blk = dk_blk + _dot_tn(ds, qh[h])
                dq = dq + _dot(ds, kh[h])
            dk_ref[pl.ds(k0, tk), :] += dk_blk
            dv_ref[pl.ds(k0, tk), :] += dv_blk
            return (dq, rows[0], rows[1])

        zcol = jnp.zeros((tq, 1), F32)
        dq, rowa, rowb = lax.fori_loop(0, _num_kblocks(qi, tq, tk), step, (jnp.zeros((tq, LANES), F32), zcol, zcol))
        dq_ref[...] = dq * QK_SCALE
        dcr_ref[0] = jnp.where(ma, rowa, rowb)

    qblk = pl.BlockSpec((tq, LANES), lambda p, i: (i, p))
    kvblk = pl.BlockSpec((S, LANES), lambda p, i: (0, p))
    f32out = jax.ShapeDtypeStruct((S, FOX_W), F32)
    ctblk = pl.BlockSpec((1, FF_STRIDE, S), lambda p, i: (p, 0, 0))
    return pl.pallas_call(
        body, name="fox_bwd", grid=(npair, S // tq),
        in_specs=[qblk, kvblk, kvblk,
                  pl.BlockSpec((1, tq, LANES), lambda p, i: (p, i, 0)),
                  pl.BlockSpec((1, FF_STRIDE, S), lambda p, i: (p, 0, 0)),
                  qblk, qblk, qblk,
                  pl.BlockSpec((tq, LANES), lambda p, i: (i, C_FG // LANES + p))],
        out_specs=[qblk, kvblk, kvblk, qblk, ctblk, pl.BlockSpec((1, tq, LANES), lambda p, i: (p, i, 0))],
        out_shape=[f32out, f32out, f32out, f32out, jax.ShapeDtypeStruct((npair, FF_STRIDE, S), F32),
                   jax.ShapeDtypeStruct((npair, S, LANES), F32)],
        compiler_params=_cparams(dimension_semantics=("arbitrary", "arbitrary")),
    )(qn, kn, v, cb, ct, o, lse, dmix, projm)


def _sb_bwd(sq, sk, sv, o, dmix, projm, *, tq, tk):
    S = sq.shape[0]
    npair = SB_HEADS // 2
    mix0 = (FOX_W + POOL_W) // LANES

    def body(q_ref, k_ref, v_ref, o_ref, dm_ref, sg_ref, dq_ref, dk_ref, dv_ref, dsg_ref):
        qi = pl.program_id(1)

        @pl.when(qi == 0)
        def _():
            dk_ref[...] = jnp.zeros_like(dk_ref)
            dv_ref[...] = jnp.zeros_like(dv_ref)

        lane = _iota((1, LANES), 1)
        ma = lane < HEAD_DIM
        qh = _pair_masks(q_ref[...])
        sg = sg_ref[...]
        silu, dsilu = _silu_pair(sg)
        dm = dm_ref[...]
        ov = o_ref[...]
        do = dm * silu
        dsg_ref[...] = dm * ov * dsilu
        dob = do.astype(BF16)
        dd = dob.astype(F32) * ov
        dsum = (jnp.sum(jnp.where(ma, dd, 0.0), axis=1, keepdims=True), jnp.sum(jnp.where(ma, 0.0, dd), axis=1, keepdims=True))
        doh = _pair_masks(dob)
        qpos = qi * tq + _iota((tq, tk), 0)
        kiota = _iota((tq, tk), 1)
        col0 = kiota == 0
        rr = _iota((tk, tk), 0)
        cc = _iota((tk, tk), 1)
        tmat = _ones_where(rr > cc)
        tmat_inc = _ones_where(rr >= cc)
        nk = _num_kblocks(qi, tq, tk)

        def step(jj, carry):
            dq = carry[2]
            k0 = pl.multiple_of((nk - 1 - jj) * tk, tk)
            kb = k_ref[pl.ds(k0, tk), :]
            vb = v_ref[pl.ds(k0, tk), :]
            kh = _pair_masks(kb)
            causal = (k0 + kiota) < qpos
            dk_blk = jnp.zeros((tk, LANES), F32)
            dv_blk = jnp.zeros((tk, LANES), F32)
            new = []
            for h in range(2):
                r_run, u_run = carry[h]
                z, e, lb, rin, a = _sb_scores(qh[h], kb, causal, tmat, r_run)
                u = a * _dot_nt(doh[h], vb)
                uhi, ulo = _split2(u)
                uin = _dot(uhi, tmat_inc) + _dot(ulo, tmat_inc)
                cum_u = dsum[h] - (uin + u_run)
                r = 1.0 / (1.0 + e)
                er = e * r
                pos = z >= 0.0
                sig = jnp.where(pos, r, er)
                nsig = jnp.where(pos, er, r)
                dz = jnp.where(causal, u * nsig - sig * cum_u, 0.0).astype(BF16)
                dv_blk = dv_blk + _dot_tn(a.astype(BF16), doh[h])
                dk_blk = dk_blk + _dot_tn(dz, qh[h])
                dq = dq + _dot(dz, kh[h])
                r_run = r_run + jnp.sum(jnp.where(col0, rin + lb, 0.0), axis=1, keepdims=True)
                u_run = u_run + jnp.sum(jnp.where(col0, uin, 0.0), axis=1, keepdims=True)
                new.append((r_run, u_run))
            dk_ref[pl.ds(k0, tk), :] += dk_blk
            dv_ref[pl.ds(k0, tk), :] += dv_blk
            return (new[0], new[1], dq)

        zcol = jnp.zeros((tq, 1), F32)
        init = ((zcol, zcol), (zcol, zcol), jnp.zeros((tq, LANES), F32))
        dq = lax.fori_loop(0, nk, step, init)[2]
        dq_ref[...] = dq * QK_SCALE

    qblk = pl.BlockSpec((tq, LANES), lambda p, i: (i, p))
    kvblk = pl.BlockSpec((S, LANES), lambda p, i: (0, p))
    f32out = jax.ShapeDtypeStruct((S, SB_W), F32)
    return pl.pallas_call(
        body, name="sb_bwd", grid=(npair, S // tq),
        in_specs=[qblk, kvblk, kvblk, qblk,
                  pl.BlockSpec((tq, LANES), lambda p, i: (i, mix0 + p)),
                  pl.BlockSpec((tq, LANES), lambda p, i: (i, C_SG // LANES + p))],
        out_specs=[qblk, kvblk, kvblk, qblk],
        out_shape=[f32out, f32out, f32out, f32out],
        compiler_params=_cparams(dimension_semantics=("arbitrary", "arbitrary")),
    )(sq, sk, sv, o, dmix, projm)


def _prep_bwd(projm, ffo, dqn, dkn, dct, dcr, dv, dfg, dsq, dsk, dsv, dsg, dmix, pooled, yp, qg, kg, bfp, wpd, ps, *, ts):
    S = projm.shape[0]
    nb = S // ts
    hb = ts // POOL_HALO
    npair = FOX_HEADS // 2
    last_halo = S // POOL_HALO - 1

    def body(fq_ref, fk_ref, pp_ref, pph_ref, ff_ref,
             dqn_ref, dkn_ref, dct_ref, dcr_ref, dv_ref, dfg_ref, dsq_ref, dsk_ref, dsv_ref, dsg_ref,
             dmp_ref, dmh_ref, pooled_ref, yp_ref, qg_ref, kg_ref, bf_ref, wpd_ref, ps_ref,
             dp_ref, dqg_ref, dkg_ref, dbf_ref, dwp_ref, dps_ref,
             carry_ref, dl_ref, buf_ref, dct_s):
        i = pl.program_id(0)
        blk = nb - 1 - i

        @pl.when(i == 0)
        def _():
            carry_ref[...] = jnp.zeros_like(carry_ref)
            dqg_ref[...] = jnp.zeros_like(dqg_ref)
            dkg_ref[...] = jnp.zeros_like(dkg_ref)
            dbf_ref[...] = jnp.zeros_like(dbf_ref)
            dwp_ref[...] = jnp.zeros_like(dwp_ref)
            dps_ref[...] = jnp.zeros_like(dps_ref)

        bd = _head_blockdiag(FOX_W)
        for raw_ref, g_ref, dn, dg_ref, col in ((fq_ref, qg_ref, dqn_ref[...], dqg_ref, C_FQ), (fk_ref, kg_ref, dkn_ref[...], dkg_ref, C_FK)):
            q = raw_ref[...]
            rstd = lax.rsqrt(_group_sum(q * q, bd) * (1.0 / HEAD_DIM) + EPS)
            xhat = q * rstd
            dg_ref[...] += jnp.sum(dn * xhat, axis=0, keepdims=True)
            dyg = dn * g_ref[...]
            mean = _group_sum(dyg * xhat, bd) * (1.0 / HEAD_DIM)
            dp_ref[:, col:col + FOX_W] = (rstd * (dyg - xhat * mean)).astype(BF16)
        dp_ref[:, C_FV:C_FV + FOX_W] = dv_ref[...].astype(BF16)
        dp_ref[:, C_FG:C_FG + FOX_W] = dfg_ref[...].astype(BF16)
        dp_ref[:, C_SQ:C_SQ + SB_W] = dsq_ref[...].astype(BF16)
        dp_ref[:, C_SK:C_SK + SB_W] = dsk_ref[...].astype(BF16)
        dp_ref[:, C_SV:C_SV + SB_W] = dsv_ref[...].astype(BF16)
        dp_ref[:, C_SG:C_SG + SB_W] = dsg_ref[...].astype(BF16)

        dct_s[...] = jnp.zeros_like(dct_s)
        for p in range(npair):
            dct_s[FF_STRIDE * p:FF_STRIDE * (p + 1), :] = dct_ref[p]
        dc = dct_s[...].T
        lane = _iota((1, LANES), 1)
        for p in range(npair):
            dcr = dcr_ref[p]
            dc = dc + jnp.where(lane == FF_STRIDE * p, _lane_pick(dcr, lane, 0), 0.0)
            dc = dc + jnp.where(lane == FF_STRIDE * p + 1, _lane_pick(dcr, lane, HEAD_DIM), 0.0)
        triu = _ones_where(_iota((ts, ts), 1) >= _iota((ts, ts), 0))
        dlf = _dot_exact_lhs(triu, dc) + carry_ref[...]
        dl_ref[...] = dlf
        carry_ref[...] = dl_ref[0:1, :]
        z = ff_ref[...] + bf_ref[...]
        dff = dlf * (1.0 / (1.0 + jnp.exp(z)))
        dbf_ref[...] += jnp.sum(dff, axis=0, keepdims=True)
        dp_ref[:, PM:PW] = dff.astype(BF16)

        psv = ps_ref[...]
        wpdv = wpd_ref[...]
        lane_group = _iota((1, POOL_W), 1) >> 6
        wlen = _pool_group_select(lane_group, [float(w) for w in POOL_WINDOWS])
        pg = pp_ref[:, POOL_W:2 * POOL_W]
        silu, dsilu = _silu_pair(pg)
        dmp = dmp_ref[...]
        ypv = yp_ref[...]
        dp_ref[:, C_PG:C_PG + POOL_W] = (dmp * (ypv * psv) * dsilu).astype(BF16)
        dps_ref[...] += jnp.sum(dmp * silu * ypv, axis=0, keepdims=True)
        dyp = (dmp * psv * silu).astype(BF16)
        dwp_ref[...] += _dot_tn(pooled_ref[...], dyp)
        dpooled = _dot_nt(dyp, wpdv)
        pgh = pph_ref[:, POOL_W:2 * POOL_W]
        dyph = (dmh_ref[...] * psv * (pgh * _sigmoid(pgh))).astype(BF16)
        dpooled_h = jnp.where(blk < nb - 1, _dot_nt(dyph, wpdv), 0.0)
        tpos = (blk * ts + _iota((ts, 1), 0) + 1).astype(F32)
        ev = dpooled / jnp.minimum(tpos, wlen)
        buf_ref[0:ts, :] = ev
        buf_ref[ts:ts + POOL_HALO, :] = dpooled_h / wlen
        acc = ev
        snaps = []
        for d in range(1, POOL_HALO):
            acc = acc + buf_ref[pl.ds(d, ts), :]
            if d + 1 in POOL_WINDOWS:
                snaps.append(acc)
        dp_ref[:, C_PX:C_PX + POOL_W] = (_pool_group_select(lane_group, snaps) - dpooled).astype(BF16)

    rblk = lambda w, c: pl.BlockSpec((ts, w), lambda i: (nb - 1 - i, c))
    full = lambda a: pl.BlockSpec(a.shape, lambda i: (0,) * a.ndim)
    halo = lambda w, c: pl.BlockSpec((POOL_HALO, w), lambda i: (jnp.minimum((nb - i) * hb, last_halo), c))
    acc_spec = lambda r, w: pl.BlockSpec((r, w), lambda i: (0, 0))
    return pl.pallas_call(
        body, name="prep_bwd", grid=(nb,),
        in_specs=[rblk(FOX_W, C_FQ // FOX_W), rblk(FOX_W, C_FK // FOX_W), rblk(2 * POOL_W, C_PX // (2 * POOL_W)),
                  halo(2 * POOL_W, C_PX // (2 * POOL_W)), rblk(LANES, 0),
                  rblk(FOX_W, 0), rblk(FOX_W, 0), pl.BlockSpec((npair, FF_STRIDE, ts), lambda i: (0, 0, nb - 1 - i)),
                  pl.BlockSpec((npair, ts, LANES), lambda i: (0, nb - 1 - i, 0)), rblk(FOX_W, 0), rblk(FOX_W, 0),
                  rblk(SB_W, 0), rblk(SB_W, 0), rblk(SB_W, 0), rblk(SB_W, 0),
                  rblk(POOL_W, FOX_W // POOL_W), halo(POOL_W, FOX_W // POOL_W), rblk(POOL_W, 0), rblk(POOL_W, 0),
                  full(qg), full(kg), full(bfp), full(wpd), full(ps)],
        out_specs=[rblk(PW, 0), acc_spec(1, FOX_W), acc_spec(1, FOX_W), acc_spec(1, LANES), acc_spec(POOL_W, POOL_W), acc_spec(1, POOL_W)],
        out_shape=[jax.ShapeDtypeStruct((S, PW), BF16), jax.ShapeDtypeStruct((1, FOX_W), F32), jax.ShapeDtypeStruct((1, FOX_W), F32),
                   jax.ShapeDtypeStruct((1, LANES), F32), jax.ShapeDtypeStruct((POOL_W, POOL_W), F32), jax.ShapeDtypeStruct((1, POOL_W), F32)],
        scratch_shapes=[pltpu.VMEM((1, LANES), F32), pltpu.VMEM((ts, LANES), F32), pltpu.VMEM((ts + POOL_HALO, POOL_W), F32),
                        pltpu.VMEM((LANES, ts), F32)],
        compiler_params=_cparams(dimension_semantics=("arbitrary",)),
    )(projm, projm, projm, projm, ffo, dqn, dkn, dct, dcr, dv, dfg, dsq, dsk, dsv, dsg, dmix, dmix, pooled, yp, qg, kg, bfp, wpd, ps)


def _inproj_dw(h, dproj, *, ts, tn):
    S, D = h.shape
    nj = PM // tn

    def body(h_ref, dp_ref, dpf_ref, dw_ref, dwf_ref):
        s = pl.program_id(1)

        @pl.when(s == 0)
        def _():
            dw_ref[...] = jnp.zeros_like(dw_ref)

        @pl.when((s == 0) & (pl.program_id(0) == 0))
        def _():
            dwf_ref[...] = jnp.zeros_like(dwf_ref)

        hv = h_ref[...]
        dw_ref[...] += _dot_tn(hv, dp_ref[...])

        @pl.when(pl.program_id(0) == 0)
        def _():
            dwf_ref[...] += _dot_tn(hv, dpf_ref[...])

    return pl.pallas_call(
        body, name="inproj_dw", grid=(nj, S // ts),
        in_specs=[pl.BlockSpec((ts, D), lambda j, s: (s, 0)),
                  pl.BlockSpec((ts, tn), lambda j, s: (s, j)),
                  pl.BlockSpec((ts, LANES), lambda j, s: (s, PM // LANES))],
        out_specs=[pl.BlockSpec((D, tn), lambda j, s: (0, j)), pl.BlockSpec((D, LANES), lambda j, s: (0, 0))],
        out_shape=[jax.ShapeDtypeStruct((D, PM), F32), jax.ShapeDtypeStruct((D, LANES), F32)],
        compiler_params=_cparams(dimension_semantics=("arbitrary", "arbitrary")),
    )(h, dproj, dproj)


def _inproj_dx(dproj, w_all, x, g, dy, *, tm):
    S, D = x.shape

    def body(dp_ref, w_ref, x_ref, g_ref, dy_ref, dx_ref, dg_ref):
        @pl.when(pl.program_id(0) == 0)
        def _():
            dg_ref[...] = jnp.zeros_like(dg_ref)

        dh = _dot_nt(dp_ref[...], w_ref[...])
        xf = x_ref[...]
        rstd = lax.rsqrt(jnp.mean(xf * xf, axis=-1, keepdims=True) + EPS)
        xhat = xf * rstd
        dg_ref[...] += jnp.sum(dh * xhat, axis=0, keepdims=True)
        dyg = dh * g_ref[...]
        mean = jnp.mean(dyg * xhat, axis=-1, keepdims=True)
        dx_ref[...] = rstd * (dyg - xhat * mean) + dy_ref[...]

    row = lambda w: pl.BlockSpec((tm, w), lambda i: (i, 0))
    return pl.pallas_call(
        body, name="inproj_dx", grid=(S // tm,),
        in_specs=[row(PW), pl.BlockSpec((D, PW), lambda i: (0, 0)), row(D), pl.BlockSpec((1, D), lambda i: (0, 0)), row(D)],
        out_specs=[row(D), pl.BlockSpec((1, D), lambda i: (0, 0))],
        out_shape=[jax.ShapeDtypeStruct((S, D), F32), jax.ShapeDtypeStruct((1, D), F32)],
        compiler_params=_cparams(dimension_semantics=("arbitrary",)),
    )(dproj, w_all, x, g, dy)


def _adamw(w, g, m, v):
    R, C = w.shape
    tr = R if R <= 512 else 256

    def body(w_ref, g_ref, m_ref, v_ref, d_ref, nm_ref, nv_ref):
        gv = g_ref[...]
        nm = ADAM_B1 * m_ref[...] + (1.0 - ADAM_B1) * gv
        nv = ADAM_B2 * v_ref[...] + (1.0 - ADAM_B2) * (gv * gv)
        m_hat = nm / (1.0 - ADAM_B1 ** ADAM_STEP)
        v_hat = nv / (1.0 - ADAM_B2 ** ADAM_STEP)
        d_ref[...] = -ADAM_LR * (m_hat / (jnp.sqrt(v_hat) + ADAM_EPS) + ADAM_WD * w_ref[...])
        nm_ref[...] = nm
        nv_ref[...] = nv

    spec = pl.BlockSpec((tr, C), lambda i: (i, 0))
    shp = jax.ShapeDtypeStruct((R, C), F32)
    return pl.pallas_call(
        body, name="adamw", grid=(R // tr,), in_specs=[spec] * 4, out_specs=[spec] * 3, out_shape=[shp] * 3,
        compiler_params=_cparams(dimension_semantics=("arbitrary",)),
    )(w, g, m, v)


def _adamw_nd(w, g, m, v):
    shape = w.shape
    two_d = (-1, shape[-1])
    outs = _adamw(w.reshape(two_d), g.reshape(two_d), m.reshape(two_d), v.reshape(two_d))
    return tuple(o.reshape(shape) for o in outs)


def _add_n(name, first, others, *, emit_bf16):
    shape = first.shape
    two_d = (-1, shape[-1])
    R = first.reshape(two_d).shape[0]
    C = shape[-1]
    tr = 256 if R % 256 == 0 else R
    n = len(others)

    def body(*refs):
        acc = refs[0][...]
        for r in refs[1:1 + n]:
            acc = acc + r[...].astype(F32)
        refs[1 + n][...] = acc
        if emit_bf16:
            refs[2 + n][...] = acc.astype(BF16)

    spec = pl.BlockSpec((tr, C), lambda i: (i, 0))
    out_shape = [jax.ShapeDtypeStruct((R, C), F32)] + ([jax.ShapeDtypeStruct((R, C), BF16)] if emit_bf16 else [])
    outs = pl.pallas_call(
        body, name=name, grid=(R // tr,), in_specs=[spec] * (1 + n), out_specs=[spec] * len(out_shape), out_shape=out_shape,
        compiler_params=_cparams(dimension_semantics=("arbitrary",)),
    )(first.reshape(two_d), *[o.reshape(two_d) for o in others])
    return tuple(o.reshape(shape) for o in outs)


FLIP_C = (0, 0, 1)
FLIP_X = (1, 0, 0)
FLIP_Y = (0, 1, 0)
FLIP_XY = (1, 1, 0)
MESH = pl.DeviceIdType.MESH


def _peer(flip):
    me = (lax.axis_index("x"), lax.axis_index("y"), lax.axis_index("c"))
    return tuple(1 - a if f else a for a, f in zip(me, flip))


def _exchange(name, arrays, flips):
    n = len(arrays)

    def body(*refs):
        srcs, dsts = refs[:n], refs[n:2 * n]
        send_sems, recv_sems = refs[2 * n:]
        copies = [pltpu.make_async_remote_copy(src_ref=srcs[k], dst_ref=dsts[k], send_sem=send_sems.at[k], recv_sem=recv_sems.at[k],
                                               device_id=_peer(flips[k]), device_id_type=MESH) for k in range(n)]
        for cp in copies:
            cp.start()
        for cp in copies:
            cp.wait()

    anyspec = pl.BlockSpec(memory_space=pl.ANY)
    return pl.pallas_call(
        body, name=name, in_specs=[anyspec] * n, out_specs=[anyspec] * n,
        out_shape=[jax.ShapeDtypeStruct(a.shape, a.dtype) for a in arrays],
        scratch_shapes=[pltpu.SemaphoreType.DMA((n,)), pltpu.SemaphoreType.DMA((n,))],
    )(*arrays)


def _exchange_add(name, x, flip):
    def body(x_ref, o_ref, buf_ref, send_sem, recv_sem):
        cp = pltpu.make_async_remote_copy(src_ref=x_ref, dst_ref=buf_ref, send_sem=send_sem, recv_sem=recv_sem,
                                          device_id=_peer(flip), device_id_type=MESH)
        cp.start()
        cp.wait()
        o_ref[...] = x_ref[...] + buf_ref[...]

    vspec = pl.BlockSpec(memory_space=pltpu.VMEM)
    return pl.pallas_call(
        body, name=name, in_specs=[vspec], out_specs=vspec, out_shape=jax.ShapeDtypeStruct(x.shape, x.dtype),
        scratch_shapes=[pltpu.VMEM(x.shape, x.dtype), pltpu.SemaphoreType.DMA, pltpu.SemaphoreType.DMA],
    )(x)


def _chip_index():
    return 2 * lax.axis_index("x") + lax.axis_index("y")


def _gather_weights(w_in, w_out):
    c = lax.axis_index("c")
    j = _chip_index()
    half = DEPTH // 2
    wi = w_in.astype(BF16)
    wo = w_out.astype(BF16)
    wi_half = lax.dynamic_slice_in_dim(wi, half * c, half, axis=0)
    wo_half = lax.dynamic_slice_in_dim(wo, half * c, half, axis=0)
    flips = (FLIP_X, FLIP_Y, FLIP_XY)
    got = _exchange("gather_ici", [wi_half] * 3 + [wo_half] * 3, flips * 2)
    gi = jnp.stack(got[:3])
    go = jnp.stack(got[3:])
    si, so = _exchange("gather_d2d", [gi, go], (FLIP_C, FLIP_C))

    def place(mine, sib):
        out = jnp.zeros((DEPTH,) + mine.shape[1:], mine.dtype)
        out = lax.dynamic_update_slice_in_dim(out, mine, half * c, axis=0)
        return lax.dynamic_update_slice_in_dim(out, sib, half * (1 - c), axis=0)

    d_model = w_in.shape[1]
    shard_cols = w_in.shape[2]
    shard_rows = w_out.shape[1]
    w_in_full = jnp.zeros((DEPTH, d_model, 4 * shard_cols), BF16)
    w_out_full = jnp.zeros((DEPTH, 4 * shard_rows, w_out.shape[2]), BF16)
    w_in_full = lax.dynamic_update_slice_in_dim(w_in_full, wi, j * shard_cols, axis=2)
    w_out_full = lax.dynamic_update_slice_in_dim(w_out_full, wo, j * shard_rows, axis=1)
    for k, m in enumerate((2, 1, 3)):
        jk = j ^ m
        w_in_full = lax.dynamic_update_slice_in_dim(w_in_full, place(gi[k], si[k]), jk * shard_cols, axis=2)
        w_out_full = lax.dynamic_update_slice_in_dim(w_out_full, place(go[k], so[k]), jk * shard_rows, axis=1)
    return w_in_full, w_out_full


def _to_aligned(w_in_full):
    L, D, _ = w_in_full.shape
    npair = FOX_HEADS // 2
    ff = w_in_full[..., ORIG_FF:ORIG_REST].reshape(L, D, npair, 2)
    ff = jnp.pad(ff, ((0, 0), (0, 0), (0, 0), (0, FF_STRIDE - 2))).reshape(L, D, npair * FF_STRIDE)
    ff = jnp.pad(ff, ((0, 0), (0, 0), (0, LANES - npair * FF_STRIDE)))
    return jnp.concatenate([w_in_full[..., :ORIG_FOX], w_in_full[..., ORIG_REST:], ff], axis=-1)


def _from_aligned(dw_all):
    L, D, _ = dw_all.shape
    npair = FOX_HEADS // 2
    ff = dw_all[..., PM:PM + npair * FF_STRIDE].reshape(L, D, npair, FF_STRIDE)[..., :2].reshape(L, D, FOX_HEADS)
    return jnp.concatenate([dw_all[..., :ORIG_FOX], ff, dw_all[..., ORIG_FOX:PM]], axis=-1)


def _reduce_scatter(parts):
    c = lax.axis_index("c")
    j = _chip_index()
    half = DEPTH // 2
    n = len(parts)
    give = [lax.dynamic_slice_in_dim(p, half * (1 - c), half, axis=1).astype(BF16) for p in parts]
    keep = [lax.dynamic_slice_in_dim(p, half * c, half, axis=1) for p in parts]
    got = _exchange("rs_d2d", give, (FLIP_C,) * n)
    chip = [_add_n("rs_add_chip", k, [g], emit_bf16=True) for k, g in zip(keep, got)]
    masks = (2, 1, 3)
    flips = (FLIP_X, FLIP_Y, FLIP_XY)
    sends, sflips = [], []
    for f32_sum, bf in chip:
        for m, fl in zip(masks, flips):
            sends.append(lax.dynamic_index_in_dim(bf, j ^ m, axis=0, keepdims=False))
            sflips.append(fl)
    got = _exchange("rs_ici", sends, tuple(sflips))
    mine = []
    for a, (f32_sum, bf) in enumerate(chip):
        own = lax.dynamic_index_in_dim(f32_sum, j, axis=0, keepdims=False)
        mine.append(_add_n("rs_add_all", own, list(got[3 * a:3 * a + 3]), emit_bf16=False)[0])
    sib = _exchange("rs_share", mine, (FLIP_C,) * n)
    out = []
    for m_, s_ in zip(mine, sib):
        full = jnp.zeros((DEPTH,) + m_.shape[1:], F32)
        full = lax.dynamic_update_slice_in_dim(full, m_, half * c, axis=0)
        out.append(lax.dynamic_update_slice_in_dim(full, s_, half * (1 - c), axis=0))
    return out


def _all_reduce_small(x):
    x = _exchange_add("ar_c", x, FLIP_C)
    x = _exchange_add("ar_y", x, FLIP_Y)
    return _exchange_add("ar_x", x, FLIP_X)


def _blocks(S):
    return dict(tm=min(512, S), ts=min(512, S), tq=min(256, S), tk=min(512, S), tks=min(256, S))


def _pair_pad(vec):
    npair = FOX_HEADS // 2
    v = jnp.pad(vec.reshape(npair, 2), ((0, 0), (0, FF_STRIDE - 2))).reshape(1, npair * FF_STRIDE)
    return jnp.pad(v, ((0, 0), (0, LANES - npair * FF_STRIDE)))


def _pair_unpad(row):
    npair = FOX_HEADS // 2
    return row[0, :npair * FF_STRIDE].reshape(npair, FF_STRIDE)[:, :2].reshape(FOX_HEADS)


def _pool_blockdiag(w_pool):
    g, cg, _ = w_pool.shape
    eye = jnp.eye(g, dtype=w_pool.dtype)
    return jnp.einsum("gh,gcd->gchd", eye, w_pool).reshape(g * cg, g * cg)


def _layer_params(norm_g, b_f, q_norm_g, k_norm_g, w_pool, pool_scale):
    return dict(g=norm_g.reshape(1, -1), qg=jnp.tile(q_norm_g, FOX_HEADS).reshape(1, FOX_W), kg=jnp.tile(k_norm_g, FOX_HEADS).reshape(1, FOX_W),
                bfp=_pair_pad(b_f), wpd=_pool_blockdiag(w_pool).astype(BF16), ps=pool_scale.reshape(1, POOL_W))


def _layer_fwd(x, w_all, w_out, prm, bs):
    projm, ffo, h = _inproj(x, prm["g"], w_all, tm=bs["tm"], tn=512)
    qn, kn, v, cb, ct, sq, sk, sv, pooled, yp, pm = _prep(projm, ffo, prm["qg"], prm["kg"], prm["bfp"], prm["wpd"], prm["ps"], ts=bs["ts"])
    o, lse, fm = _fox_fwd(qn, kn, v, cb, ct, projm, tq=bs["tq"], tk=bs["tk"])
    so, sm = _sb_fwd(sq, sk, sv, projm, tq=bs["tq"], tk=bs["tks"])
    y = _outproj(x, fm, pm, sm, w_out, tm=bs["tm"])
    saved = dict(x=x, projm=projm, ffo=ffo, h=h, qn=qn, kn=kn, v=v, cb=cb, ct=ct, sq=sq, sk=sk, sv=sv, pooled=pooled, yp=yp,
                 o=o, lse=lse, so=so, fm=fm, pm=pm, sm=sm)
    return y, saved


def _layer_bwd(dy, w_all, w_out, prm, sv_, bs):
    dmix, dw_out = _outproj_bwd(dy, sv_["fm"], sv_["pm"], sv_["sm"], w_out, tm=bs["tm"])
    dqn, dkn, dv, dfg, dct, dcr = _fox_bwd(sv_["qn"], sv_["kn"], sv_["v"], sv_["cb"], sv_["ct"], sv_["o"], sv_["lse"], dmix, sv_["projm"],
                                      tq=bs["tq"], tk=bs["tk"])
    dsq, dsk, dsv, dsg = _sb_bwd(sv_["sq"], sv_["sk"], sv_["sv"], sv_["so"], dmix, sv_["projm"], tq=bs["tq"], tk=bs["tks"])
    dproj, dqg, dkg, dbf, dwp, dps = _prep_bwd(sv_["projm"], sv_["ffo"], dqn, dkn, dct, dcr, dv, dfg, dsq, dsk, dsv, dsg, dmix,
                                               sv_["pooled"], sv_["yp"], prm["qg"], prm["kg"], prm["bfp"], prm["wpd"], prm["ps"], ts=bs["ts"])
    dwm, dwf = _inproj_dw(sv_["h"], dproj, ts=bs["ts"], tn=512)
    dx, dg = _inproj_dx(dproj, w_all, sv_["x"], prm["g"], dy, tm=min(256, bs["tm"]))
    grads = dict(
        w_all=jnp.concatenate([dwm, dwf], axis=-1), w_out=dw_out, norm_g=dg[0],
        b_f=_pair_unpad(dbf), q_norm_g=dqg.reshape(FOX_HEADS, HEAD_DIM).sum(0), k_norm_g=dkg.reshape(FOX_HEADS, HEAD_DIM).sum(0),
        w_pool=jnp.stack([dwp[HEAD_DIM * g:HEAD_DIM * (g + 1), HEAD_DIM * g:HEAD_DIM * (g + 1)] for g in range(4)]),
        pool_scale=dps[0])
    return dx, grads


def _local_step(x, target, w_all, w_out, norm_g, b_f, q_norm_g, k_norm_g, w_pool, pool_scale):
    S, D = x.shape
    bs = _blocks(S)
    prms = [_layer_params(norm_g[l], b_f[l], q_norm_g[l], k_norm_g[l], w_pool[l], pool_scale[l]) for l in range(DEPTH)]
    saved = []
    y = x
    for l in range(DEPTH):
        y, s_ = _layer_fwd(y, w_all[l], w_out[l], prms[l], bs)
        saved.append(s_)
    dy, sq = _loss_head(y, target, tm=bs["tm"])
    loss = 0.5 * jnp.sum(sq) / D
    grads = [None] * DEPTH
    for l in reversed(range(DEPTH)):
        dy, grads[l] = _layer_bwd(dy, w_all[l], w_out[l], prms[l], saved[l], bs)
    stacked = {k: jnp.stack([g[k] for g in grads]) for k in grads[0]}
    return loss, dy, stacked


SMALL = ("norm_g", "b_f", "q_norm_g", "k_norm_g", "w_pool", "pool_scale")


def _pack_small(gr):
    flat = jnp.concatenate([gr[k].reshape(-1) for k in SMALL])
    pad = (-flat.shape[0]) % (8 * LANES)
    return jnp.pad(flat, (0, pad)).reshape(-1, LANES)


def _unpack_small(packed, like):
    flat = packed.reshape(-1)
    out, off = {}, 0
    for k in SMALL:
        n = like[k].size
        out[k] = flat[off:off + n].reshape(like[k].shape)
        off += n
    return out


def kernel(x, norm_g, w_in, b_f, q_norm_g, k_norm_g, w_pool, pool_scale, w_out, loss_target, m_norm_g, m_w_in, m_b_f, m_q_norm_g, m_k_norm_g, m_w_pool, m_pool_scale, m_w_out, v_norm_g, v_w_in, v_b_f, v_q_norm_g, v_k_norm_g, v_w_pool, v_pool_scale, v_w_out):
    weights = dict(norm_g=norm_g, w_in=w_in, b_f=b_f, q_norm_g=q_norm_g, k_norm_g=k_norm_g, w_pool=w_pool, pool_scale=pool_scale, w_out=w_out)
    mom_m = dict(norm_g=m_norm_g, w_in=m_w_in, b_f=m_b_f, q_norm_g=m_q_norm_g, k_norm_g=m_k_norm_g, w_pool=m_w_pool, pool_scale=m_pool_scale, w_out=m_w_out)
    mom_v = dict(norm_g=v_norm_g, w_in=v_w_in, b_f=v_b_f, q_norm_g=v_q_norm_g, k_norm_g=v_k_norm_g, w_pool=v_w_pool, pool_scale=v_pool_scale, w_out=v_w_out)
    shard_cols = w_in.shape[2]
    shard_rows = w_out.shape[1]

    w_in_full, w_out_full = _gather_weights(w_in, w_out)
    w_all = _to_aligned(w_in_full)
    loss, dx, gr = _local_step(x[0], loss_target[0], w_all, w_out_full, norm_g, b_f, q_norm_g, k_norm_g, w_pool, pool_scale)
    loss = lax.psum(loss, ("x", "y", "c"))

    dw_in_full = _from_aligned(gr["w_all"])
    d_model = dw_in_full.shape[1]
    part_in = jnp.moveaxis(dw_in_full.reshape(DEPTH, d_model, 4, shard_cols), 2, 0)
    part_out = jnp.moveaxis(gr["w_out"].reshape(DEPTH, 4, shard_rows, -1), 1, 0)
    g_w_in, g_w_out = _reduce_scatter([part_in, part_out])
    small = _unpack_small(_all_reduce_small(_pack_small(gr)), {k: weights[k] for k in SMALL})
    grad_w = dict(small, w_in=g_w_in, w_out=g_w_out)

    names = ("norm_g", "w_in", "b_f", "q_norm_g", "k_norm_g", "w_pool", "pool_scale", "w_out")
    upd = {k: _adamw_nd(weights[k], grad_w[k], mom_m[k], mom_v[k]) for k in names}
    return (loss, dx[None], *[grad_w[k] for k in names], *[upd[k][0] for k in names], *[upd[k][1] for k in names], *[upd[k][2] for k in names])
```

```python
import functools

import jax
import jax.numpy as jnp
from jax import lax
from jax.experimental import pallas as pl
from jax.experimental.pallas import tpu as pltpu

F32 = jnp.float32
BF16 = jnp.bfloat16

DEPTH = 4
HEAD_DIM = 64
FOX_HEADS = 8
SB_HEADS = 4
FOX_W = FOX_HEADS * HEAD_DIM
SB_W = SB_HEADS * HEAD_DIM
POOL_W = 256
POOL_WINDOWS = (2, 4, 8, 16)
POOL_HALO = 16
D_MIX = FOX_W + POOL_W + SB_W
EPS = 1e-6
NEG = -1e30
QK_SCALE = HEAD_DIM ** -0.5

ORIG_FOX = 4 * FOX_W
ORIG_FF = ORIG_FOX
ORIG_REST = ORIG_FF + FOX_HEADS
D_IN = ORIG_REST + 2 * POOL_W + 4 * SB_W

C_FQ, C_FK, C_FV, C_FG = 0, FOX_W, 2 * FOX_W, 3 * FOX_W
C_PX = 4 * FOX_W
C_PG = C_PX + POOL_W
C_SQ = C_PG + POOL_W
C_SK, C_SV, C_SG = C_SQ + SB_W, C_SQ + 2 * SB_W, C_SQ + 3 * SB_W
PM = C_SG + SB_W
LANES = 128
PW = PM + LANES
FF_STRIDE = 8
AUG = 3

ADAM_LR = 0.001
ADAM_B1 = 0.9
ADAM_B2 = 0.999
ADAM_EPS = 1e-08
ADAM_WD = 0.01
ADAM_STEP = 10

VMEM_LIMIT = 48 * 1024 * 1024


def _cparams(**kw):
    return pltpu.CompilerParams(vmem_limit_bytes=VMEM_LIMIT, **kw)


def _dot(a, b):
    return jnp.dot(a, b, preferred_element_type=F32)


def _dot_nt(a, b):
    return lax.dot_general(a, b, (((1,), (1,)), ((), ())), preferred_element_type=F32)


def _dot_tn(a, b):
    return lax.dot_general(a, b, (((0,), (0,)), ((), ())), preferred_element_type=F32)


def _split2(x):
    hi = x.astype(BF16)
    lo = (x - hi.astype(F32)).astype(BF16)
    return hi, lo


def _split3(x):
    hi = x.astype(BF16)
    r = x - hi.astype(F32)
    mid = r.astype(BF16)
    lo = (r - mid.astype(F32)).astype(BF16)
    return hi, mid, lo


def _dot_exact_rhs(x, m):
    hi, mid, lo = _split3(x)
    return _dot(hi, m) + _dot(mid, m) + _dot(lo, m)


def _dot_exact_lhs(m, x):
    hi, mid, lo = _split3(x)
    return _dot(m, hi) + _dot(m, mid) + _dot(m, lo)


def _sigmoid(x):
    return 1.0 / (1.0 + jnp.exp(-x))


def _silu_pair(x):
    s = _sigmoid(x)
    return x * s, s * (1.0 + x * (1.0 - s))


def _iota(shape, dim):
    return lax.broadcasted_iota(jnp.int32, shape, dim)


def _ones_where(cond):
    return jnp.where(cond, 1.0, 0.0).astype(BF16)


def _head_blockdiag(w):
    return _ones_where((_iota((w, w), 0) >> 6) == (_iota((w, w), 1) >> 6))


def _group_sum(x, bd):
    hi, lo = _split2(x)
    return _dot(hi, bd) + _dot(lo, bd)


def _lane_pick(x, lane_idx, lane):
    return jnp.sum(jnp.where(lane_idx == lane, x, 0.0), axis=1, keepdims=True)


def _inproj(x, g, w_all, *, tm, tn):
    S, D = x.shape
    nj = PM // tn

    def body(x_ref, g_ref, w_ref, wff_ref, proj_ref, ff_ref, h_ref):
        @pl.when(pl.program_id(1) == 0)
        def _():
            xf = x_ref[...]
            ms = jnp.mean(xf * xf, axis=-1, keepdims=True)
            h = (xf * lax.rsqrt(ms + EPS) * g_ref[...]).astype(BF16)
            h_ref[...] = h
            ff_ref[...] = _dot(h, wff_ref[...])

        proj_ref[...] = _dot(h_ref[...], w_ref[...])

    return pl.pallas_call(
        body, name="inproj", grid=(S // tm, nj),
        in_specs=[pl.BlockSpec((tm, D), lambda i, j: (i, 0)),
                  pl.BlockSpec((1, D), lambda i, j: (0, 0)),
                  pl.BlockSpec((D, tn), lambda i, j: (0, j)),
                  pl.BlockSpec((D, LANES), lambda i, j: (0, PM // LANES))],
        out_specs=[pl.BlockSpec((tm, tn), lambda i, j: (i, j)),
                   pl.BlockSpec((tm, LANES), lambda i, j: (i, 0)),
                   pl.BlockSpec((tm, D), lambda i, j: (i, 0))],
        out_shape=[jax.ShapeDtypeStruct((S, PM), F32), jax.ShapeDtypeStruct((S, LANES), F32),
                   jax.ShapeDtypeStruct((S, D), BF16)],
        compiler_params=_cparams(dimension_semantics=("arbitrary", "arbitrary")),
    )(x, g, w_all, w_all)


def _pool_group_select(lane_group, vals):
    return jnp.where(lane_group == 0, vals[0], jnp.where(lane_group == 1, vals[1], jnp.where(lane_group == 2, vals[2], vals[3])))


def _prep(projm, ffo, qg, kg, bfp, wpd, ps, *, ts):
    S = projm.shape[0]
    nb = S // ts
    hb = ts // POOL_HALO

    def body(fq_ref, fk_ref, fv_ref, pp_ref, halo_ref, ff_ref, sq_ref, sk_ref, sv_ref,
             qg_ref, kg_ref, bf_ref, wpd_ref, ps_ref,
             qn_ref, ka_ref, kb_ref, v_ref, sqo_ref, sko_ref, svo_ref, pooled_ref, yp_ref, pm_ref,
             carry_ref, c_ref, buf_ref):
        i = pl.program_id(0)
        bd = _head_blockdiag(FOX_W)
        normed = []
        for src, g_ref in ((fq_ref, qg_ref), (fk_ref, kg_ref)):
            q = src[...]
            ss = _group_sum(q * q, bd)
            normed.append(q * lax.rsqrt(ss * (1.0 / HEAD_DIM) + EPS) * g_ref[...])
        qn_ref[...] = (normed[0] * QK_SCALE).astype(BF16)
        kn = normed[1]
        v_ref[...] = fv_ref[...].astype(BF16)
        sqo_ref[...] = (sq_ref[...] * QK_SCALE).astype(BF16)
        sko_ref[...] = sk_ref[...].astype(BF16)
        svo_ref[...] = sv_ref[...].astype(BF16)

        @pl.when(i == 0)
        def _():
            carry_ref[...] = jnp.zeros_like(carry_ref)

        z = ff_ref[...] + bf_ref[...]
        lf = jnp.minimum(z, 0.0) - jnp.log(1.0 + jnp.exp(-jnp.abs(z)))
        tri = _ones_where(_iota((ts, ts), 1) <= _iota((ts, ts), 0))
        c = _dot_exact_lhs(tri, lf) + carry_ref[...]
        c_ref[...] = c
        carry_ref[...] = c_ref[ts - 1:ts, :]
        parts = jnp.concatenate(_split3(-c), axis=1)
        row = _iota((AUG * LANES, FOX_W), 0)
        col = _iota((AUG * LANES, FOX_W), 1)
        part, src = row >> 7, row & (LANES - 1)
        pair, off = col >> 7, col & (LANES - 1)
        sel_a = _ones_where((src == FF_STRIDE * pair) & (off == HEAD_DIM + part))
        sel_b = _ones_where((src == FF_STRIDE * pair + 1) & (off == part))
        first_half = (_iota((1, FOX_W), 1) & HEAD_DIM) == 0
        ka_ref[...] = jnp.where(first_half, kn, _dot(parts, sel_a)).astype(BF16)
        kb_ref[...] = jnp.where(first_half, _dot(parts, sel_b), kn).astype(BF16)

        x = pp_ref[:, 0:POOL_W]
        pg = pp_ref[:, POOL_W:2 * POOL_W]
        halo = jnp.where(i > 0, halo_ref[:, 0:POOL_W], 0.0)
        buf_ref[0:POOL_HALO, :] = halo
        buf_ref[POOL_HALO:POOL_HALO + ts, :] = x
        acc = x
        snaps = []
        for d in range(1, POOL_HALO):
            acc = acc + buf_ref[pl.ds(POOL_HALO - d, ts), :]
            if d + 1 in POOL_WINDOWS:
                snaps.append(acc)
        lane_group = _iota((1, POOL_W), 1) >> 6
        wsum = _pool_group_select(lane_group, snaps)
        wlen = _pool_group_select(lane_group, [float(w) for w in POOL_WINDOWS])
        tpos = (i * ts + _iota((ts, 1), 0) + 1).astype(F32)
        pooled = wsum / jnp.minimum(tpos, wlen) - x
        pb = pooled.astype(BF16)
        pooled_ref[...] = pb
        yp = _dot(pb, wpd_ref[...])
        yp_ref[...] = yp
        pm_ref[...] = (yp * ps_ref[...] * (pg * _sigmoid(pg))).astype(BF16)

    blk = lambda w, c: pl.BlockSpec((ts, w), lambda i: (i, c))
    full = lambda a: pl.BlockSpec(a.shape, lambda i: (0,) * a.ndim)
    out_shapes = [
        jax.ShapeDtypeStruct((S, FOX_W), BF16), jax.ShapeDtypeStruct((S, FOX_W), BF16), jax.ShapeDtypeStruct((S, FOX_W), BF16),
        jax.ShapeDtypeStruct((S, FOX_W), BF16),
        jax.ShapeDtypeStruct((S, SB_W), BF16), jax.ShapeDtypeStruct((S, SB_W), BF16), jax.ShapeDtypeStruct((S, SB_W), BF16),
        jax.ShapeDtypeStruct((S, POOL_W), BF16), jax.ShapeDtypeStruct((S, POOL_W), F32), jax.ShapeDtypeStruct((S, POOL_W), BF16),
    ]
    out_specs = [
        blk(FOX_W, 0), blk(FOX_W, 0), blk(FOX_W, 0), blk(FOX_W, 0),
        blk(SB_W, 0), blk(SB_W, 0), blk(SB_W, 0),
        blk(POOL_W, 0), blk(POOL_W, 0), blk(POOL_W, 0),
    ]
    return pl.pallas_call(
        body, name="prep", grid=(nb,),
        in_specs=[blk(FOX_W, C_FQ // FOX_W), blk(FOX_W, C_FK // FOX_W), blk(FOX_W, C_FV // FOX_W), blk(2 * POOL_W, C_PX // (2 * POOL_W)),
                  pl.BlockSpec((POOL_HALO, 2 * POOL_W), lambda i: (jnp.maximum(i * hb - 1, 0), C_PX // (2 * POOL_W))),
                  blk(LANES, 0),
                  blk(SB_W, C_SQ // SB_W), blk(SB_W, C_SK // SB_W), blk(SB_W, C_SV // SB_W),
                  full(qg), full(kg), full(bfp), full(wpd), full(ps)],
        out_specs=out_specs, out_shape=out_shapes,
        scratch_shapes=[pltpu.VMEM((1, LANES), F32), pltpu.VMEM((ts, LANES), F32), pltpu.VMEM((ts + POOL_HALO, POOL_W), F32)],
        compiler_params=_cparams(dimension_semantics=("arbitrary",)),
    )(projm, projm, projm, projm, projm, ffo, projm, projm, projm, qg, kg, bfp, wpd, ps)


def _pair_masks(x):
    ma = _iota((1, LANES), 1) < HEAD_DIM
    zero = jnp.zeros_like(x)
    return jnp.where(ma, x, zero), jnp.where(ma, zero, x)


def _aug_queries(q):
    lane = _iota((1, LANES), 1)
    one = jnp.ones_like(q)
    zero = jnp.zeros_like(q)
    qa = jnp.where(lane < HEAD_DIM, q, jnp.where(lane < HEAD_DIM + AUG, one, zero))
    qb = jnp.where(lane >= HEAD_DIM, q, jnp.where(lane < AUG, one, zero))
    return qa, qb


def _fox_fwd(qn, ka, kb, v, projm, *, tq, tk):
    S = qn.shape[0]
    npair = FOX_HEADS // 2

    def body(q_ref, ka_ref, kb_ref, v_ref, fg_ref, o_ref, lse_ref, fm_ref):
        qi = pl.program_id(1)
        lane = _iota((1, LANES), 1)
        ma = lane < HEAD_DIM
        qaug = _aug_queries(q_ref[...])
        k_refs = (ka_ref, kb_ref)

        def block(j, carry, masked):
            k0 = pl.multiple_of(j * tk, tk)
            vb = v_ref[pl.ds(k0, tk), :]
            if masked:
                mask = (k0 + _iota((tq, tk), 1)) <= (qi * tq + _iota((tq, tk), 0))
            scores = [_dot_nt(qaug[h], k_refs[h][pl.ds(k0, tk), :]) for h in range(2)]
            stats, ps = [], []
            for h in range(2):
                m, l, _ = carry[h]
                s = jnp.where(mask, scores[h], NEG) if masked else scores[h]
                m_new = jnp.maximum(m, jnp.max(s, axis=1, keepdims=True))
                alpha = jnp.exp(m - m_new)
                p = jnp.exp(s - m_new)
                stats.append((m_new, alpha * l + jnp.sum(p, axis=1, keepdims=True), alpha))
                ps.append(p.astype(BF16))
            pv = _dot(jnp.concatenate(ps, axis=0), vb)
            return tuple((stats[h][0], stats[h][1], stats[h][2] * carry[h][2] + pv[h * tq:(h + 1) * tq]) for h in range(2))

        init = tuple((jnp.full((tq, 1), NEG, F32), jnp.zeros((tq, 1), F32), jnp.zeros((tq, LANES), F32)) for _ in range(2))
        nfull = (qi * tq) // tk
        carry = lax.fori_loop(0, nfull, lambda j, c: block(j, c, False), init)
        (ma_, la, acca), (mb_, lb, accb) = block(nfull, carry, True)
        o = jnp.where(ma, acca / la, accb / lb)
        o_ref[...] = o
        lse_ref[...] = jnp.where(ma, ma_ + jnp.log(la), mb_ + jnp.log(lb))
        fg = fg_ref[...]
        fm_ref[...] = (o * (fg * _sigmoid(fg))).astype(BF16)

    qblk = pl.BlockSpec((tq, LANES), lambda p, i: (i, p))
    kvblk = pl.BlockSpec((S, LANES), lambda p, i: (0, p))
    return pl.pallas_call(
        body, name="fox_fwd", grid=(npair, S // tq),
        in_specs=[qblk, kvblk, kvblk, kvblk,
                  pl.BlockSpec((tq, LANES), lambda p, i: (i, C_FG // LANES + p))],
        out_specs=[qblk, qblk, qblk],
        out_shape=[jax.ShapeDtypeStruct((S, FOX_W), F32), jax.ShapeDtypeStruct((S, FOX_W), F32), jax.ShapeDtypeStruct((S, FOX_W), BF16)],
        compiler_params=_cparams(dimension_semantics=("arbitrary", "arbitrary")),
    )(qn, ka, kb, v, projm)


def _suffix_sums(x, tmat2):
    return _dot(jnp.concatenate(_split2(x), axis=1), tmat2)


def _suffix_matrix(tk, inclusive):
    rr, cc = _iota((2 * tk, tk), 0) & (tk - 1), _iota((2 * tk, tk), 1)
    return _ones_where(rr >= cc) if inclusive else _ones_where(rr > cc)


def _sb_scores(qh, kb, causal, tmat2, r_runs):
    heads = range(2)
    zs = [_dot_nt(qh[h], kb) for h in heads]
    nsps = [jnp.minimum(-z, 0.0) - jnp.log(1.0 + jnp.exp(-jnp.abs(z))) for z in zs]
    lbs = nsps if causal is None else [jnp.where(causal, n, 0.0) for n in nsps]
    rins = [_suffix_sums(lb, tmat2) for lb in lbs]
    args = [zs[h] + lbs[h] + (rins[h] + r_runs[h]) for h in heads]
    a_s = [jnp.exp(arg if causal is None else jnp.where(causal, arg, NEG)) for arg in args]
    return zs, nsps, lbs, a_s


def _sb_fwd(sq, sk, sv, projm, *, tq, tk):
    S = sq.shape[0]
    npair = SB_HEADS // 2

    def body(q_ref, k_ref, v_ref, sg_ref, o_ref, sm_ref):
        qi = pl.program_id(1)
        lane = _iota((1, LANES), 1)
        ma = lane < HEAD_DIM
        qh = _pair_masks(q_ref[...])
        tmat2 = _suffix_matrix(tk, inclusive=False)
        nfull = (qi * tq) // tk

        def block(j, carry, masked):
            k0 = pl.multiple_of(j * tk, tk)
            kb = k_ref[pl.ds(k0, tk), :]
            vb = v_ref[pl.ds(k0, tk), :]
            causal = (k0 + _iota((tq, tk), 1)) < (qi * tq + _iota((tq, tk), 0)) if masked else None
            _, _, lbs, a_s = _sb_scores(qh, kb, causal, tmat2, [carry[h][0] for h in range(2)])
            pv = _dot(jnp.concatenate([a.astype(BF16) for a in a_s], axis=0), vb)
            return tuple((carry[h][0] + jnp.sum(lbs[h], axis=1, keepdims=True), carry[h][1] + pv[h * tq:(h + 1) * tq]) for h in range(2))

        init = tuple((jnp.zeros((tq, 1), F32), jnp.zeros((tq, LANES), F32)) for _ in range(2))
        carry = block(nfull, init, True)
        (_, acca), (_, accb) = lax.fori_loop(0, nfull, lambda jj, c: block(nfull - 1 - jj, c, False), carry)
        o = jnp.where(ma, acca, accb)
        o_ref[...] = o
        sg = sg_ref[...]
        sm_ref[...] = (o * (sg * _sigmoid(sg))).astype(BF16)

    qblk = pl.BlockSpec((tq, LANES), lambda p, i: (i, p))
    kvblk = pl.BlockSpec((S, LANES), lambda p, i: (0, p))
    return pl.pallas_call(
        body, name="sb_fwd", grid=(npair, S // tq),
        in_specs=[qblk, kvblk, kvblk, pl.BlockSpec((tq, LANES), lambda p, i: (i, C_SG // LANES + p))],
        out_specs=[qblk, qblk],
        out_shape=[jax.ShapeDtypeStruct((S, SB_W), F32), jax.ShapeDtypeStruct((S, SB_W), BF16)],
        compiler_params=_cparams(dimension_semantics=("arbitrary", "arbitrary")),
    )(sq, sk, sv, projm)


def _outproj(x, fm, pm, sm, w_out, *, tm):
    S, D = x.shape

    def body(x_ref, fm_ref, pm_ref, sm_ref, w_ref, y_ref):
        y = x_ref[...] + _dot(fm_ref[...], w_ref[0:FOX_W, :])
        y = y + _dot(pm_ref[...], w_ref[FOX_W:FOX_W + POOL_W, :])
        y_ref[...] = y + _dot(sm_ref[...], w_ref[FOX_W + POOL_W:D_MIX, :])

    row = lambda w: pl.BlockSpec((tm, w), lambda i: (i, 0))
    return pl.pallas_call(
        body, name="outproj", grid=(S // tm,),
        in_specs=[row(D), row(FOX_W), row(POOL_W), row(SB_W), pl.BlockSpec((D_MIX, D), lambda i: (0, 0))],
        out_specs=row(D), out_shape=jax.ShapeDtypeStruct((S, D), F32),
        compiler_params=_cparams(dimension_semantics=("arbitrary",)),
    )(x, fm, pm, sm, w_out)


def _loss_head(y, target, *, tm):
    S, D = y.shape

    def body(y_ref, t_ref, dy_ref, sq_ref):
        @pl.when(pl.program_id(0) == 0)
        def _():
            sq_ref[...] = jnp.zeros_like(sq_ref)

        d = y_ref[...] - t_ref[...]
        dy_ref[...] = d * (1.0 / D)
        sq_ref[...] += jnp.sum(d * d, axis=0, keepdims=True)

    row = pl.BlockSpec((tm, D), lambda i: (i, 0))
    return pl.pallas_call(
        body, name="loss_head", grid=(S // tm,),
        in_specs=[row, row], out_specs=[row, pl.BlockSpec((1, D), lambda i: (0, 0))],
        out_shape=[jax.ShapeDtypeStruct((S, D), F32), jax.ShapeDtypeStruct((1, D), F32)],
        compiler_params=_cparams(dimension_semantics=("arbitrary",)),
    )(y, target)


def _outproj_bwd(dy, fm, pm, sm, w_out, *, tm):
    S, D = dy.shape

    def body(dy_ref, fm_ref, pm_ref, sm_ref, w_ref, dm_ref, dw_ref):
        @pl.when(pl.program_id(0) == 0)
        def _():
            dw_ref[...] = jnp.zeros_like(dw_ref)

        dyb = dy_ref[...].astype(BF16)
        dm_ref[...] = _dot_nt(dyb, w_ref[...])
        dw_ref[0:FOX_W, :] += _dot_tn(fm_ref[...], dyb)
        dw_ref[FOX_W:FOX_W + POOL_W, :] += _dot_tn(pm_ref[...], dyb)
        dw_ref[FOX_W + POOL_W:D_MIX, :] += _dot_tn(sm_ref[...], dyb)

    row = lambda w: pl.BlockSpec((tm, w), lambda i: (i, 0))
    wspec = pl.BlockSpec((D_MIX, D), lambda i: (0, 0))
    return pl.pallas_call(
        body, name="outproj_bwd", grid=(S // tm,),
        in_specs=[row(D), row(FOX_W), row(POOL_W), row(SB_W), wspec],
        out_specs=[row(D_MIX), wspec],
        out_shape=[jax.ShapeDtypeStruct((S, D_MIX), F32), jax.ShapeDtypeStruct((D_MIX, D), F32)],
        compiler_params=_cparams(dimension_semantics=("arbitrary",)),
    )(dy, fm, pm, sm, w_out)


def _fox_bwd(qn, ka, kb, v, o, lse, dmix, projm, *, tq, tk):
    S = qn.shape[0]
    npair = FOX_HEADS // 2

    def body(q_ref, ka_ref, kb_ref, v_ref, o_ref, lse_ref, dm_ref, fg_ref,
             dq_ref, dk_ref, dv_ref, dfg_ref, dct_ref, dcr_ref):
        qi = pl.program_id(1)

        @pl.when(qi == 0)
        def _():
            dk_ref[...] = jnp.zeros_like(dk_ref)
            dv_ref[...] = jnp.zeros_like(dv_ref)
            dct_ref[...] = jnp.zeros_like(dct_ref)

        lane = _iota((1, LANES), 1)
        ma = lane < HEAD_DIM
        qh = _pair_masks(q_ref[...])
        qaug = _aug_queries(q_ref[...])
        k_refs = (ka_ref, kb_ref)
        lsev = lse_ref[...]
        lse = (_lane_pick(lsev, lane, 0), _lane_pick(lsev, lane, HEAD_DIM))
        fg = fg_ref[...]
        silu, dsilu = _silu_pair(fg)
        dm = dm_ref[...]
        ov = o_ref[...]
        do = dm * silu
        dfg_ref[...] = dm * ov * dsilu
        dd = do * ov
        dsum = (jnp.sum(jnp.where(ma, dd, 0.0), axis=1, keepdims=True), jnp.sum(jnp.where(ma, 0.0, dd), axis=1, keepdims=True))
        doh = _pair_masks(do.astype(BF16))
        do2 = jnp.concatenate(doh, axis=0)
        q2 = jnp.concatenate(qh, axis=0)

        def block(j, carry, masked):
            dq = carry[0]
            rows = list(carry[1:])
            k0 = pl.multiple_of(j * tk, tk)
            vb = v_ref[pl.ds(k0, tk), :]
            if masked:
                mask = (k0 + _iota((tq, tk), 1)) <= (qi * tq + _iota((tq, tk), 0))
            heads = range(2)
            kaugs = [k_refs[h][pl.ds(k0, tk), :] for h in heads]
            scores = [_dot_nt(qaug[h], kaugs[h]) for h in heads]
            dps = [_dot_nt(doh[h], vb) for h in heads]
            ps, dss = [], []
            for h in heads:
                s = jnp.where(mask, scores[h], NEG) if masked else scores[h]
                p = jnp.exp(s - lse[h])
                dsf = p * (dps[h] - dsum[h])
                dct_ref[0, h:h + 1, pl.ds(k0, tk)] -= jnp.sum(dsf, axis=0, keepdims=True)
                rows[h] = rows[h] + jnp.sum(dsf, axis=1, keepdims=True)
                ps.append(p.astype(BF16))
                dss.append(dsf.astype(BF16))
            dv_ref[pl.ds(k0, tk), :] += _dot_tn(jnp.concatenate(ps, axis=0), do2)
            dk_ref[pl.ds(k0, tk), :] += _dot_tn(jnp.concatenate(dss, axis=0), q2)
            kh = jnp.concatenate([_pair_masks(kaugs[h])[h] for h in heads], axis=0)
            dq = dq + _dot(jnp.concatenate(dss, axis=1), kh)
            return (dq, rows[0], rows[1])

        zcol = jnp.zeros((tq, 1), F32)
        nfull = (qi * tq) // tk
        carry = lax.fori_loop(0, nfull, lambda j, c: block(j, c, False), (jnp.zeros((tq, LANES), F32), zcol, zcol))
        dq, rowa, rowb = block(nfull, carry, True)
        dq_ref[...] = dq * QK_SCALE
        dcr_ref[0] = jnp.where(ma, rowa, rowb)

    qblk = pl.BlockSpec((tq, LANES), lambda p, i: (i, p))
    kvblk = pl.BlockSpec((S, LANES), lambda p, i: (0, p))
    f32out = jax.ShapeDtypeStruct((S, FOX_W), F32)
    ctblk = pl.BlockSpec((1, FF_STRIDE, S), lambda p, i: (p, 0, 0))
    return pl.pallas_call(
        body, name="fox_bwd", grid=(npair, S // tq),
        in_specs=[qblk, kvblk, kvblk, kvblk, qblk, qblk, qblk,
                  pl.BlockSpec((tq, LANES), lambda p, i: (i, C_FG // LANES + p))],
        out_specs=[qblk, kvblk, kvblk, qblk, ctblk, pl.BlockSpec((1, tq, LANES), lambda p, i: (p, i, 0))],
        out_shape=[f32out, f32out, f32out, f32out, jax.ShapeDtypeStruct((npair, FF_STRIDE, S), F32),
                   jax.ShapeDtypeStruct((npair, S, LANES), F32)],
        compiler_params=_cparams(dimension_semantics=("arbitrary", "arbitrary")),
    )(qn, ka, kb, v, o, lse, dmix, projm)


def _sb_bwd(sq, sk, sv, o, dmix, projm, *, tq, tk):
    S = sq.shape[0]
    npair = SB_HEADS // 2
    mix0 = (FOX_W + POOL_W) // LANES

    def body(q_ref, k_ref, v_ref, o_ref, dm_ref, sg_ref, dq_ref, dk_ref, dv_ref, dsg_ref):
        qi = pl.program_id(1)

        @pl.when(qi == 0)
        def _():
            dk_ref[...] = jnp.zeros_like(dk_ref)
            dv_ref[...] = jnp.zeros_like(dv_ref)

        lane = _iota((1, LANES), 1)
        ma = lane < HEAD_DIM
        qh = _pair_masks(q_ref[...])
        sg = sg_ref[...]
        silu, dsilu = _silu_pair(sg)
        dm = dm_ref[...]
        ov = o_ref[...]
        do = dm * silu
        dsg_ref[...] = dm * ov * dsilu
        dob = do.astype(BF16)
        dd = dob.astype(F32) * ov
        dsum = (jnp.sum(jnp.where(ma, dd, 0.0), axis=1, keepdims=True), jnp.sum(jnp.where(ma, 0.0, dd), axis=1, keepdims=True))
        doh = _pair_masks(dob)
        do2 = jnp.concatenate(doh, axis=0)
        q2 = jnp.concatenate(qh, axis=0)
        tmat2 = _suffix_matrix(tk, inclusive=False)
        tmat2_inc = _suffix_matrix(tk, inclusive=True)
        nfull = (qi * tq) // tk

        def block(j, carry, masked):
            dq = carry[2]
            k0 = pl.multiple_of(j * tk, tk)
            kb = k_ref[pl.ds(k0, tk), :]
            vb = v_ref[pl.ds(k0, tk), :]
            kh = _pair_masks(kb)
            causal = (k0 + _iota((tq, tk), 1)) < (qi * tq + _iota((tq, tk), 0)) if masked else None
            heads = range(2)
            das = [_dot_nt(doh[h], vb) for h in heads]
            zs, nsps, lbs, a_s = _sb_scores(qh, kb, causal, tmat2, [carry[h][0] for h in heads])
            abs_ = [a.astype(BF16) for a in a_s]
            us = [abs_[h].astype(F32) * das[h] for h in heads]
            uins = [_suffix_sums(u, tmat2_inc) for u in us]
            dzs = []
            for h in heads:
                cum_u = dsum[h] - (uins[h] + carry[h][1])
                dz = us[h] * jnp.exp(nsps[h]) - jnp.exp(zs[h] + nsps[h]) * cum_u
                if masked:
                    dz = jnp.where(causal, dz, 0.0)
                dzs.append(dz.astype(BF16))
            dv_ref[pl.ds(k0, tk), :] += _dot_tn(jnp.concatenate(abs_, axis=0), do2)
            dk_ref[pl.ds(k0, tk), :] += _dot_tn(jnp.concatenate(dzs, axis=0), q2)
            dq = dq + _dot(jnp.concatenate(dzs, axis=1), jnp.concatenate(kh, axis=0))
            new = [(carry[h][0] + jnp.sum(lbs[h], axis=1, keepdims=True), carry[h][1] + jnp.sum(us[h], axis=1, keepdims=True)) for h in heads]
            return (new[0], new[1], dq)

        zcol = jnp.zeros((tq, 1), F32)
        init = ((zcol, zcol), (zcol, zcol), jnp.zeros((tq, LANES), F32))
        carry = block(nfull, init, True)
        dq = lax.fori_loop(0, nfull, lambda jj, c: block(nfull - 1 - jj, c, False), carry)[2]
        dq_ref[...] = dq * QK_SCALE

    qblk = pl.BlockSpec((tq, LANES), lambda p, i: (i, p))
    kvblk = pl.BlockSpec((S, LANES), lambda p, i: (0, p))
    f32out = jax.ShapeDtypeStruct((S, SB_W), F32)
    return pl.pallas_call(
        body, name="sb_bwd", grid=(npair, S // tq),
        in_specs=[qblk, kvblk, kvblk, qblk,
                  pl.BlockSpec((tq, LANES), lambda p, i: (i, mix0 + p)),
                  pl.BlockSpec((tq, LANES), lambda p, i: (i, C_SG // LANES + p))],
        out_specs=[qblk, kvblk, kvblk, qblk],
        out_shape=[f32out, f32out, f32out, f32out],
        compiler_params=_cparams(dimension_semantics=("arbitrary", "arbitrary")),
    )(sq, sk, sv, o, dmix, projm)


def _prep_bwd(projm, ffo, dqn, dkn, dct, dcr, dv, dfg, dsq, dsk, dsv, dsg, dmix, pooled, yp, qg, kg, bfp, wpd, ps, *, ts):
    S = projm.shape[0]
    nb = S // ts
    hb = ts // POOL_HALO
    npair = FOX_HEADS // 2
    last_halo = S // POOL_HALO - 1

    def body(fq_ref, fk_ref, pp_ref, pph_ref, ff_ref,
             dqn_ref, dkn_ref, dct_ref, dcr_ref, dv_ref, dfg_ref, dsq_ref, dsk_ref, dsv_ref, dsg_ref,
             dmp_ref, dmh_ref, pooled_ref, yp_ref, qg_ref, kg_ref, bf_ref, wpd_ref, ps_ref,
             dp_ref, dqg_ref, dkg_ref, dbf_ref, dwp_ref, dps_ref,
             carry_ref, dl_ref, buf_ref, dct_s):
        i = pl.program_id(0)
        blk = nb - 1 - i

        @pl.when(i == 0)
        def _():
            carry_ref[...] = jnp.zeros_like(carry_ref)
            dqg_ref[...] = jnp.zeros_like(dqg_ref)
            dkg_ref[...] = jnp.zeros_like(dkg_ref)
            dbf_ref[...] = jnp.zeros_like(dbf_ref)
            dwp_ref[...] = jnp.zeros_like(dwp_ref)
            dps_ref[...] = jnp.zeros_like(dps_ref)

        bd = _head_blockdiag(FOX_W)
        for raw_ref, g_ref, dn, dg_ref, col in ((fq_ref, qg_ref, dqn_ref[...], dqg_ref, C_FQ), (fk_ref, kg_ref, dkn_ref[...], dkg_ref, C_FK)):
            q = raw_ref[...]
            rstd = lax.rsqrt(_group_sum(q * q, bd) * (1.0 / HEAD_DIM) + EPS)
            xhat = q * rstd
            dg_ref[...] += jnp.sum(dn * xhat, axis=0, keepdims=True)
            dyg = dn * g_ref[...]
            mean = _group_sum(dyg * xhat, bd) * (1.0 / HEAD_DIM)
            dp_ref[:, col:col + FOX_W] = (rstd * (dyg - xhat * mean)).astype(BF16)
        dp_ref[:, C_FV:C_FV + FOX_W] = dv_ref[...].astype(BF16)
        dp_ref[:, C_FG:C_FG + FOX_W] = dfg_ref[...].astype(BF16)
        dp_ref[:, C_SQ:C_SQ + SB_W] = dsq_ref[...].astype(BF16)
        dp_ref[:, C_SK:C_SK + SB_W] = dsk_ref[...].astype(BF16)
        dp_ref[:, C_SV:C_SV + SB_W] = dsv_ref[...].astype(BF16)
        dp_ref[:, C_SG:C_SG + SB_W] = dsg_ref[...].astype(BF16)

        dct_s[...] = jnp.zeros_like(dct_s)
        for p in range(npair):
            dct_s[FF_STRIDE * p:FF_STRIDE * (p + 1), :] = dct_ref[p]
        dc = dct_s[...].T
        lane = _iota((1, LANES), 1)
        for p in range(npair):
            dcr = dcr_ref[p]
            dc = dc + jnp.where(lane == FF_STRIDE * p, _lane_pick(dcr, lane, 0), 0.0)
            dc = dc + jnp.where(lane == FF_STRIDE * p + 1, _lane_pick(dcr, lane, HEAD_DIM), 0.0)
        triu = _ones_where(_iota((ts, ts), 1) >= _iota((ts, ts), 0))
        dlf = _dot_exact_lhs(triu, dc) + carry_ref[...]
        dl_ref[...] = dlf
        carry_ref[...] = dl_ref[0:1, :]
        z = ff_ref[...] + bf_ref[...]
        dff = dlf * (1.0 / (1.0 + jnp.exp(z)))
        dbf_ref[...] += jnp.sum(dff, axis=0, keepdims=True)
        dp_ref[:, PM:PW] = dff.astype(BF16)

        psv = ps_ref[...]
        wpdv = wpd_ref[...]
        lane_group = _iota((1, POOL_W), 1) >> 6
        wlen = _pool_group_select(lane_group, [float(w) for w in POOL_WINDOWS])
        pg = pp_ref[:, POOL_W:2 * POOL_W]
        silu, dsilu = _silu_pair(pg)
        dmp = dmp_ref[...]
        ypv = yp_ref[...]
        dp_ref[:, C_PG:C_PG + POOL_W] = (dmp * (ypv * psv) * dsilu).astype(BF16)
        dps_ref[...] += jnp.sum(dmp * silu * ypv, axis=0, keepdims=True)
        dyp = (dmp * psv * silu).astype(BF16)
        dwp_ref[...] += _dot_tn(pooled_ref[...], dyp)
        dpooled = _dot_nt(dyp, wpdv)
        pgh = pph_ref[:, POOL_W:2 * POOL_W]
        dyph = (dmh_ref[...] * psv * (pgh * _sigmoid(pgh))).astype(BF16)
        dpooled_h = jnp.where(blk < nb - 1, _dot_nt(dyph, wpdv), 0.0)
        tpos = (blk * ts + _iota((ts, 1), 0) + 1).astype(F32)
        ev = dpooled / jnp.minimum(tpos, wlen)
        buf_ref[0:ts, :] = ev
        buf_ref[ts:ts + POOL_HALO, :] = dpooled_h / wlen
        acc = ev
        snaps = []
        for d in range(1, POOL_HALO):
            acc = acc + buf_ref[pl.ds(d, ts), :]
            if d + 1 in POOL_WINDOWS:
                snaps.append(acc)
        dp_ref[:, C_PX:C_PX + POOL_W] = (_pool_group_select(lane_group, snaps) - dpooled).astype(BF16)

    rblk = lambda w, c: pl.BlockSpec((ts, w), lambda i: (nb - 1 - i, c))
    full = lambda a: pl.BlockSpec(a.shape, lambda i: (0,) * a.ndim)
    halo = lambda w, c: pl.BlockSpec((POOL_HALO, w), lambda i: (jnp.minimum((nb - i) * hb, last_halo), c))
    acc_spec = lambda r, w: pl.BlockSpec((r, w), lambda i: (0, 0))
    return pl.pallas_call(
        body, name="prep_bwd", grid=(nb,),
        in_specs=[rblk(FOX_W, C_FQ // FOX_W), rblk(FOX_W, C_FK // FOX_W), rblk(2 * POOL_W, C_PX // (2 * POOL_W)),
                  halo(2 * POOL_W, C_PX // (2 * POOL_W)), rblk(LANES, 0),
                  rblk(FOX_W, 0), rblk(FOX_W, 0), pl.BlockSpec((npair, FF_STRIDE, ts), lambda i: (0, 0, nb - 1 - i)),
                  pl.BlockSpec((npair, ts, LANES), lambda i: (0, nb - 1 - i, 0)), rblk(FOX_W, 0), rblk(FOX_W, 0),
                  rblk(SB_W, 0), rblk(SB_W, 0), rblk(SB_W, 0), rblk(SB_W, 0),
                  rblk(POOL_W, FOX_W // POOL_W), halo(POOL_W, FOX_W // POOL_W), rblk(POOL_W, 0), rblk(POOL_W, 0),
                  full(qg), full(kg), full(bfp), full(wpd), full(ps)],
        out_specs=[rblk(PW, 0), acc_spec(1, FOX_W), acc_spec(1, FOX_W), acc_spec(1, LANES), acc_spec(POOL_W, POOL_W), acc_spec(1, POOL_W)],
        out_shape=[jax.ShapeDtypeStruct((S, PW), BF16), jax.ShapeDtypeStruct((1, FOX_W), F32), jax.ShapeDtypeStruct((1, FOX_W), F32),
                   jax.ShapeDtypeStruct((1, LANES), F32), jax.ShapeDtypeStruct((POOL_W, POOL_W), F32), jax.ShapeDtypeStruct((1, POOL_W), F32)],
        scratch_shapes=[pltpu.VMEM((1, LANES), F32), pltpu.VMEM((ts, LANES), F32), pltpu.VMEM((ts + POOL_HALO, POOL_W), F32),
                        pltpu.VMEM((LANES, ts), F32)],
        compiler_params=_cparams(dimension_semantics=("arbitrary",)),
    )(projm, projm, projm, projm, ffo, dqn, dkn, dct, dcr, dv, dfg, dsq, dsk, dsv, dsg, dmix, dmix, pooled, yp, qg, kg, bfp, wpd, ps)


def _inproj_dw(h, dproj, *, ts, tn):
    S, D = h.shape
    nj = PM // tn

    def body(h_ref, dp_ref, dpf_ref, dw_ref, dwf_ref):
        s = pl.program_id(1)

        @pl.when(s == 0)
        def _():
            dw_ref[...] = jnp.zeros_like(dw_ref)

        @pl.when((s == 0) & (pl.program_id(0) == 0))
        def _():
            dwf_ref[...] = jnp.zeros_like(dwf_ref)

        hv = h_ref[...]
        dw_ref[...] += _dot_tn(hv, dp_ref[...])

        @pl.when(pl.program_id(0) == 0)
        def _():
            dwf_ref[...] += _dot_tn(hv, dpf_ref[...])

    return pl.pallas_call(
        body, name="inproj_dw", grid=(nj, S // ts),
        in_specs=[pl.BlockSpec((ts, D), lambda j, s: (s, 0)),
                  pl.BlockSpec((ts, tn), lambda j, s: (s, j)),
                  pl.BlockSpec((ts, LANES), lambda j, s: (s, PM // LANES))],
        out_specs=[pl.BlockSpec((D, tn), lambda j, s: (0, j)), pl.BlockSpec((D, LANES), lambda j, s: (0, 0))],
        out_shape=[jax.ShapeDtypeStruct((D, PM), F32), jax.ShapeDtypeStruct((D, LANES), F32)],
        compiler_params=_cparams(dimension_semantics=("arbitrary", "arbitrary")),
    )(h, dproj, dproj)


def _inproj_dx(dproj, w_all, x, g, dy, *, tm):
    S, D = x.shape

    def body(dp_ref, w_ref, x_ref, g_ref, dy_ref, dx_ref, dg_ref):
        @pl.when(pl.program_id(0) == 0)
        def _():
            dg_ref[...] = jnp.zeros_like(dg_ref)

        dh = _dot_nt(dp_ref[...], w_ref[...])
        xf = x_ref[...]
        rstd = lax.rsqrt(jnp.mean(xf * xf, axis=-1, keepdims=True) + EPS)
        xhat = xf * rstd
        dg_ref[...] += jnp.sum(dh * xhat, axis=0, keepdims=True)
        dyg = dh * g_ref[...]
        mean = jnp.mean(dyg * xhat, axis=-1, keepdims=True)
        dx_ref[...] = rstd * (dyg - xhat * mean) + dy_ref[...]

    row = lambda w: pl.BlockSpec((tm, w), lambda i: (i, 0))
    return pl.pallas_call(
        body, name="inproj_dx", grid=(S // tm,),
        in_specs=[row(PW), pl.BlockSpec((D, PW), lambda i: (0, 0)), row(D), pl.BlockSpec((1, D), lambda i: (0, 0)), row(D)],
        out_specs=[row(D), pl.BlockSpec((1, D), lambda i: (0, 0))],
        out_shape=[jax.ShapeDtypeStruct((S, D), F32), jax.ShapeDtypeStruct((1, D), F32)],
        compiler_params=_cparams(dimension_semantics=("arbitrary",)),
    )(dproj, w_all, x, g, dy)


def _adamw(w, g, m, v):
    R, C = w.shape
    tr = R if R <= 512 else 256

    def body(w_ref, g_ref, m_ref, v_ref, d_ref, nm_ref, nv_ref):
        gv = g_ref[...]
        nm = ADAM_B1 * m_ref[...] + (1.0 - ADAM_B1) * gv
        nv = ADAM_B2 * v_ref[...] + (1.0 - ADAM_B2) * (gv * gv)
        m_hat = nm / (1.0 - ADAM_B1 ** ADAM_STEP)
        v_hat = nv / (1.0 - ADAM_B2 ** ADAM_STEP)
        d_ref[...] = -ADAM_LR * (m_hat / (jnp.sqrt(v_hat) + ADAM_EPS) + ADAM_WD * w_ref[...])
        nm_ref[...] = nm
        nv_ref[...] = nv

    spec = pl.BlockSpec((tr, C), lambda i: (i, 0))
    shp = jax.ShapeDtypeStruct((R, C), F32)
    return pl.pallas_call(
        body, name="adamw", grid=(R // tr,), in_specs=[spec] * 4, out_specs=[spec] * 3, out_shape=[shp] * 3,
        compiler_params=_cparams(dimension_semantics=("arbitrary",)),
    )(w, g, m, v)


def _adamw_nd(w, g, m, v):
    shape = w.shape
    two_d = (-1, shape[-1])
    outs = _adamw(w.reshape(two_d), g.reshape(two_d), m.reshape(two_d), v.reshape(two_d))
    return tuple(o.reshape(shape) for o in outs)


def _add_n(name, first, others, *, emit_bf16):
    shape = first.shape
    two_d = (-1, shape[-1])
    R = first.reshape(two_d).shape[0]
    C = shape[-1]
    tr = 256 if R % 256 == 0 else R
    n = len(others)

    def body(*refs):
        acc = refs[0][...]
        for r in refs[1:1 + n]:
            acc = acc + r[...].astype(F32)
        refs[1 + n][...] = acc
        if emit_bf16:
            refs[2 + n][...] = acc.astype(BF16)

    spec = pl.BlockSpec((tr, C), lambda i: (i, 0))
    out_shape = [jax.ShapeDtypeStruct((R, C), F32)] + ([jax.ShapeDtypeStruct((R, C), BF16)] if emit_bf16 else [])
    outs = pl.pallas_call(
        body, name=name, grid=(R // tr,), in_specs=[spec] * (1 + n), out_specs=[spec] * len(out_shape), out_shape=out_shape,
        compiler_params=_cparams(dimension_semantics=("arbitrary",)),
    )(first.reshape(two_d), *[o.reshape(two_d) for o in others])
    return tuple(o.reshape(shape) for o in outs)


FLIP_C = (0, 0, 1)
FLIP_X = (1, 0, 0)
FLIP_Y = (0, 1, 0)
FLIP_XY = (1, 1, 0)
MESH = pl.DeviceIdType.MESH


def _peer(flip):
    me = (lax.axis_index("x"), lax.axis_index("y"), lax.axis_index("c"))
    return tuple(1 - a if f else a for a, f in zip(me, flip))


def _exchange(name, arrays, flips):
    n = len(arrays)

    def body(*refs):
        srcs, dsts = refs[:n], refs[n:2 * n]
        send_sems, recv_sems = refs[2 * n:]
        copies = [pltpu.make_async_remote_copy(src_ref=srcs[k], dst_ref=dsts[k], send_sem=send_sems.at[k], recv_sem=recv_sems.at[k],
                                               device_id=_peer(flips[k]), device_id_type=MESH) for k in range(n)]
        for cp in copies:
            cp.start()
        for cp in copies:
            cp.wait()

    anyspec = pl.BlockSpec(memory_space=pl.ANY)
    return pl.pallas_call(
        body, name=name, in_specs=[anyspec] * n, out_specs=[anyspec] * n,
        out_shape=[jax.ShapeDtypeStruct(a.shape, a.dtype) for a in arrays],
        scratch_shapes=[pltpu.SemaphoreType.DMA((n,)), pltpu.SemaphoreType.DMA((n,))],
    )(*arrays)


def _exchange_add(name, x, flip):
    def body(x_ref, o_ref, buf_ref, send_sem, recv_sem):
        cp = pltpu.make_async_remote_copy(src_ref=x_ref, dst_ref=buf_ref, send_sem=send_sem, recv_sem=recv_sem,
                                          device_id=_peer(flip), device_id_type=MESH)
        cp.start()
        cp.wait()
        o_ref[...] = x_ref[...] + buf_ref[...]

    vspec = pl.BlockSpec(memory_space=pltpu.VMEM)
    return pl.pallas_call(
        body, name=name, in_specs=[vspec], out_specs=vspec, out_shape=jax.ShapeDtypeStruct(x.shape, x.dtype),
        scratch_shapes=[pltpu.VMEM(x.shape, x.dtype), pltpu.SemaphoreType.DMA, pltpu.SemaphoreType.DMA],
    )(x)


def _chip_index():
    return 2 * lax.axis_index("x") + lax.axis_index("y")


def _gather_weights(w_in, w_out):
    half = DEPTH // 2
    wi = w_in.astype(BF16)
    wo = w_out.astype(BF16)
    masks = (2, 1, 3)
    flips = (FLIP_X, FLIP_Y, FLIP_XY)
    n_first = 2 * len(masks)

    def body(wi_ref, wo_ref, gi_ref, go_ref, send_sems, recv_sems, local_sems):
        c = lax.axis_index("c")
        j = _chip_index()
        mine = pl.ds(half * c, half)
        theirs = pl.ds(half * (1 - c), half)
        srcs = (wi_ref, wo_ref)
        dsts = (gi_ref, go_ref)

        def copy(idx, src, dst, flip):
            return pltpu.make_async_remote_copy(src_ref=src, dst_ref=dst, send_sem=send_sems.at[idx], recv_sem=recv_sems.at[idx],
                                                device_id=_peer(flip), device_id_type=MESH)

        local = [pltpu.make_async_copy(srcs[a], dsts[a].at[j], local_sems.at[a]) for a in range(2)]
        for cp in local:
            cp.start()
        first = [copy(2 * k + a, srcs[a].at[mine], dsts[a].at[j, mine], flips[k]) for k in range(len(masks)) for a in range(2)]
        for cp in first:
            cp.start()
        passed = []
        for k, m in enumerate(masks):
            for a in range(2):
                slot = dsts[a].at[j ^ m, mine]
                copy(2 * k + a, slot, slot, flips[k]).wait_recv()
                fwd = copy(n_first + 2 * k + a, slot, slot, FLIP_C)
                fwd.start()
                passed.append(fwd)
        for k, m in enumerate(masks):
            for a in range(2):
                slot = dsts[a].at[j ^ m, theirs]
                copy(n_first + 2 * k + a, slot, slot, FLIP_C).wait_recv()
        for cp in first + passed:
            cp.wait_send()
        for cp in local:
            cp.wait()

    anyspec = pl.BlockSpec(memory_space=pl.ANY)
    gi, go = pl.pallas_call(
        body, name="gather_weights", in_specs=[anyspec] * 2, out_specs=[anyspec] * 2,
        out_shape=[jax.ShapeDtypeStruct((4,) + wi.shape, BF16), jax.ShapeDtypeStruct((4,) + wo.shape, BF16)],
        scratch_shapes=[pltpu.SemaphoreType.DMA((2 * n_first,)), pltpu.SemaphoreType.DMA((2 * n_first,)), pltpu.SemaphoreType.DMA((2,))],
    )(wi, wo)
    w_in_full = jnp.transpose(gi, (1, 2, 0, 3)).reshape(DEPTH, wi.shape[1], 4 * wi.shape[2])
    w_out_full = jnp.transpose(go, (1, 0, 2, 3)).reshape(DEPTH, 4 * wo.shape[1], wo.shape[2])
    return w_in_full, w_out_full


def _to_aligned(w_in_full):
    L, D, _ = w_in_full.shape
    npair = FOX_HEADS // 2
    ff = w_in_full[..., ORIG_FF:ORIG_REST].reshape(L, D, npair, 2)
    ff = jnp.pad(ff, ((0, 0), (0, 0), (0, 0), (0, FF_STRIDE - 2))).reshape(L, D, npair * FF_STRIDE)
    ff = jnp.pad(ff, ((0, 0), (0, 0), (0, LANES - npair * FF_STRIDE)))
    return jnp.concatenate([w_in_full[..., :ORIG_FOX], w_in_full[..., ORIG_REST:], ff], axis=-1)


def _from_aligned(dw_all):
    L, D, _ = dw_all.shape
    npair = FOX_HEADS // 2
    ff = dw_all[..., PM:PM + npair * FF_STRIDE].reshape(L, D, npair, FF_STRIDE)[..., :2].reshape(L, D, FOX_HEADS)
    return jnp.concatenate([dw_all[..., :ORIG_FOX], ff, dw_all[..., ORIG_FOX:PM]], axis=-1)


def _reduce_scatter(parts):
    c = lax.axis_index("c")
    j = _chip_index()
    half = DEPTH // 2
    n = len(parts)
    give = [lax.dynamic_slice_in_dim(p, half * (1 - c), half, axis=1).astype(BF16) for p in parts]
    keep = [lax.dynamic_slice_in_dim(p, half * c, half, axis=1) for p in parts]
    got = _exchange("rs_d2d", give, (FLIP_C,) * n)
    chip = [_add_n("rs_add_chip", k, [g], emit_bf16=True) for k, g in zip(keep, got)]
    masks = (2, 1, 3)
    flips = (FLIP_X, FLIP_Y, FLIP_XY)
    sends, sflips = [], []
    for f32_sum, bf in chip:
        for m, fl in zip(masks, flips):
            sends.append(lax.dynamic_index_in_dim(bf, j ^ m, axis=0, keepdims=False))
            sflips.append(fl)
    got = _exchange("rs_ici", sends, tuple(sflips))
    mine = []
    for a, (f32_sum, bf) in enumerate(chip):
        own = lax.dynamic_index_in_dim(f32_sum, j, axis=0, keepdims=False)
        mine.append(_add_n("rs_add_all", own, list(got[3 * a:3 * a + 3]), emit_bf16=False)[0])
    return _share_halves(mine)


def _share_halves(mine):
    n = len(mine)
    half = DEPTH // 2

    def body(*refs):
        srcs, dsts = refs[:n], refs[n:2 * n]
        send_sems, recv_sems, local_sems = refs[2 * n:]
        lay = pl.ds(half * lax.axis_index("c"), half)
        local = [pltpu.make_async_copy(srcs[k], dsts[k].at[lay], local_sems.at[k]) for k in range(n)]
        remote = [pltpu.make_async_remote_copy(src_ref=srcs[k], dst_ref=dsts[k].at[lay], send_sem=send_sems.at[k], recv_sem=recv_sems.at[k],
                                               device_id=_peer(FLIP_C), device_id_type=MESH) for k in range(n)]
        for cp in local + remote:
            cp.start()
        for cp in remote + local:
            cp.wait()

    anyspec = pl.BlockSpec(memory_space=pl.ANY)
    return pl.pallas_call(
        body, name="rs_share", in_specs=[anyspec] * n, out_specs=[anyspec] * n,
        out_shape=[jax.ShapeDtypeStruct((DEPTH,) + a.shape[1:], a.dtype) for a in mine],
        scratch_shapes=[pltpu.SemaphoreType.DMA((n,)), pltpu.SemaphoreType.DMA((n,)), pltpu.SemaphoreType.DMA((n,))],
    )(*mine)


def _all_reduce_small(x):
    x = _exchange_add("ar_c", x, FLIP_C)
    x = _exchange_add("ar_y", x, FLIP_Y)
    return _exchange_add("ar_x", x, FLIP_X)


def _blocks(S):
    return dict(tm=min(512, S), ts=min(512, S), tq=min(256, S), tk=min(512, S), tks=min(256, S))


def _pair_pad(vec):
    npair = FOX_HEADS // 2
    v = jnp.pad(vec.reshape(npair, 2), ((0, 0), (0, FF_STRIDE - 2))).reshape(1, npair * FF_STRIDE)
    return jnp.pad(v, ((0, 0), (0, LANES - npair * FF_STRIDE)))


def _pair_unpad(row):
    npair = FOX_HEADS // 2
    return row[0, :npair * FF_STRIDE].reshape(npair, FF_STRIDE)[:, :2].reshape(FOX_HEADS)


def _pool_blockdiag(w_pool):
    g, cg, _ = w_pool.shape
    eye = jnp.eye(g, dtype=w_pool.dtype)
    return jnp.einsum("gh,gcd->gchd", eye, w_pool).reshape(g * cg, g * cg)


def _layer_params(norm_g, b_f, q_norm_g, k_norm_g, w_pool, pool_scale):
    return dict(g=norm_g.reshape(1, -1), qg=jnp.tile(q_norm_g, FOX_HEADS).reshape(1, FOX_W), kg=jnp.tile(k_norm_g, FOX_HEADS).reshape(1, FOX_W),
                bfp=_pair_pad(b_f), wpd=_pool_blockdiag(w_pool).astype(BF16), ps=pool_scale.reshape(1, POOL_W))


def _layer_fwd(x, w_all, w_out, prm, bs):
    projm, ffo, h = _inproj(x, prm["g"], w_all, tm=bs["tm"], tn=512)
    qn, ka, kb, v, sq, sk, sv, pooled, yp, pm = _prep(projm, ffo, prm["qg"], prm["kg"], prm["bfp"], prm["wpd"], prm["ps"], ts=bs["ts"])
    o, lse, fm = _fox_fwd(qn, ka, kb, v, projm, tq=bs["tq"], tk=bs["tk"])
    so, sm = _sb_fwd(sq, sk, sv, projm, tq=bs["tq"], tk=bs["tks"])
    y = _outproj(x, fm, pm, sm, w_out, tm=bs["tm"])
    saved = dict(x=x, projm=projm, ffo=ffo, h=h, qn=qn, ka=ka, kb=kb, v=v, sq=sq, sk=sk, sv=sv, pooled=pooled, yp=yp,
                 o=o, lse=lse, so=so, fm=fm, pm=pm, sm=sm)
    return y, saved


def _layer_bwd(dy, w_all, w_out, prm, sv_, bs):
    dmix, dw_out = _outproj_bwd(dy, sv_["fm"], sv_["pm"], sv_["sm"], w_out, tm=bs["tm"])
    dqn, dkn, dv, dfg, dct, dcr = _fox_bwd(sv_["qn"], sv_["ka"], sv_["kb"], sv_["v"], sv_["o"], sv_["lse"], dmix, sv_["projm"],
                                      tq=bs["tq"], tk=bs["tk"])
    dsq, dsk, dsv, dsg = _sb_bwd(sv_["sq"], sv_["sk"], sv_["sv"], sv_["so"], dmix, sv_["projm"], tq=bs["tq"], tk=bs["tks"])
    dproj, dqg, dkg, dbf, dwp, dps = _prep_bwd(sv_["projm"], sv_["ffo"], dqn, dkn, dct, dcr, dv, dfg, dsq, dsk, dsv, dsg, dmix,
                                               sv_["pooled"], sv_["yp"], prm["qg"], prm["kg"], prm["bfp"], prm["wpd"], prm["ps"], ts=bs["ts"])
    dwm, dwf = _inproj_dw(sv_["h"], dproj, ts=bs["ts"], tn=512)
    dx, dg = _inproj_dx(dproj, w_all, sv_["x"], prm["g"], dy, tm=min(256, bs["tm"]))
    grads = dict(
        w_all=jnp.concatenate([dwm, dwf], axis=-1), w_out=dw_out, norm_g=dg[0],
        b_f=_pair_unpad(dbf), q_norm_g=dqg.reshape(FOX_HEADS, HEAD_DIM).sum(0), k_norm_g=dkg.reshape(FOX_HEADS, HEAD_DIM).sum(0),
        w_pool=jnp.stack([dwp[HEAD_DIM * g:HEAD_DIM * (g + 1), HEAD_DIM * g:HEAD_DIM * (g + 1)] for g in range(4)]),
        pool_scale=dps[0])
    return dx, grads


def _local_step(x, target, w_all, w_out, norm_g, b_f, q_norm_g, k_norm_g, w_pool, pool_scale):
    S, D = x.shape
    bs = _blocks(S)
    prms = [_layer_params(norm_g[l], b_f[l], q_norm_g[l], k_norm_g[l], w_pool[l], pool_scale[l]) for l in range(DEPTH)]
    saved = []
    y = x
    for l in range(DEPTH):
        y, s_ = _layer_fwd(y, w_all[l], w_out[l], prms[l], bs)
        saved.append(s_)
    dy, sq = _loss_head(y, target, tm=bs["tm"])
    loss = 0.5 * jnp.sum(sq) / D
    grads = [None] * DEPTH
    for l in reversed(range(DEPTH)):
        dy, grads[l] = _layer_bwd(dy, w_all[l], w_out[l], prms[l], saved[l], bs)
    stacked = {k: jnp.stack([g[k] for g in grads]) for k in grads[0]}
    return loss, dy, stacked


SMALL = ("norm_g", "b_f", "q_norm_g", "k_norm_g", "w_pool", "pool_scale")


def _pack_small(gr):
    flat = jnp.concatenate([gr[k].reshape(-1) for k in SMALL])
    pad = (-flat.shape[0]) % (8 * LANES)
    return jnp.pad(flat, (0, pad)).reshape(-1, LANES)


def _unpack_small(packed, like):
    flat = packed.reshape(-1)
    out, off = {}, 0
    for k in SMALL:
        n = like[k].size
        out[k] = flat[off:off + n].reshape(like[k].shape)
        off += n
    return out


def kernel(x, norm_g, w_in, b_f, q_norm_g, k_norm_g, w_pool, pool_scale, w_out, loss_target, m_norm_g, m_w_in, m_b_f, m_q_norm_g, m_k_norm_g, m_w_pool, m_pool_scale, m_w_out, v_norm_g, v_w_in, v_b_f, v_q_norm_g, v_k_norm_g, v_w_pool, v_pool_scale, v_w_out):
    weights = dict(norm_g=norm_g, w_in=w_in, b_f=b_f, q_norm_g=q_norm_g, k_norm_g=k_norm_g, w_pool=w_pool, pool_scale=pool_scale, w_out=w_out)
    mom_m = dict(norm_g=m_norm_g, w_in=m_w_in, b_f=m_b_f, q_norm_g=m_q_norm_g, k_norm_g=m_k_norm_g, w_pool=m_w_pool, pool_scale=m_pool_scale, w_out=m_w_out)
    mom_v = dict(norm_g=v_norm_g, w_in=v_w_in, b_f=v_b_f, q_norm_g=v_q_norm_g, k_norm_g=v_k_norm_g, w_pool=v_w_pool, pool_scale=v_pool_scale, w_out=v_w_out)
    shard_cols = w_in.shape[2]
    shard_rows = w_out.shape[1]

    w_in_full, w_out_full = _gather_weights(w_in, w_out)
    w_all = _to_aligned(w_in_full)
    loss, dx, gr = _local_step(x[0], loss_target[0], w_all, w_out_full, norm_g, b_f, q_norm_g, k_norm_g, w_pool, pool_scale)
    loss = lax.psum(loss, ("x", "y", "c"))

    dw_in_full = _from_aligned(gr["w_all"])
    d_model = dw_in_full.shape[1]
    part_in = jnp.moveaxis(dw_in_full.reshape(DEPTH, d_model, 4, shard_cols), 2, 0)
    part_out = jnp.moveaxis(gr["w_out"].reshape(DEPTH, 4, shard_rows, -1), 1, 0)
    g_w_in, g_w_out = _reduce_scatter([part_in, part_out])
    small = _unpack_small(_all_reduce_small(_pack_small(gr)), {k: weights[k] for k in SMALL})
    grad_w = dict(small, w_in=g_w_in, w_out=g_w_out)

    names = ("norm_g", "w_in", "b_f", "q_norm_g", "k_norm_g", "w_pool", "pool_scale", "w_out")
    upd = {k: _adamw_nd(weights[k], grad_w[k], mom_m[k], mom_v[k]) for k in names}
    return (loss, dx[None], *[grad_w[k] for k in names], *[upd[k][0] for k in names], *[upd[k][1] for k in names], *[upd[k][2] for k in names])
```

```python
import functools

import jax
import jax.numpy as jnp
from jax import lax
from jax.experimental import pallas as pl
from jax.experimental.pallas import tpu as pltpu

F32 = jnp.float32
BF16 = jnp.bfloat16

DEPTH = 4
HEAD_DIM = 64
FOX_HEADS = 8
SB_HEADS = 4
FOX_W = FOX_HEADS * HEAD_DIM
SB_W = SB_HEADS * HEAD_DIM
POOL_W = 256
POOL_WINDOWS = (2, 4, 8, 16)
POOL_HALO = 16
D_MIX = FOX_W + POOL_W + SB_W
EPS = 1e-6
NEG = -1e30
QK_SCALE = HEAD_DIM ** -0.5

ORIG_FOX = 4 * FOX_W
ORIG_FF = ORIG_FOX
ORIG_REST = ORIG_FF + FOX_HEADS
D_IN = ORIG_REST + 2 * POOL_W + 4 * SB_W

C_FQ, C_FK, C_FV, C_FG = 0, FOX_W, 2 * FOX_W, 3 * FOX_W
C_PX = 4 * FOX_W
C_PG = C_PX + POOL_W
C_SQ = C_PG + POOL_W
C_SK, C_SV, C_SG = C_SQ + SB_W, C_SQ + 2 * SB_W, C_SQ + 3 * SB_W
PM = C_SG + SB_W
LANES = 128
PW = PM + LANES
FF_STRIDE = 8
AUG = 3

ADAM_LR = 0.001
ADAM_B1 = 0.9
ADAM_B2 = 0.999
ADAM_EPS = 1e-08
ADAM_WD = 0.01
ADAM_STEP = 10

VMEM_LIMIT = 48 * 1024 * 1024


def _cparams(**kw):
    return pltpu.CompilerParams(vmem_limit_bytes=VMEM_LIMIT, **kw)


def _dot(a, b):
    return jnp.dot(a, b, preferred_element_type=F32)


def _dot_nt(a, b):
    return lax.dot_general(a, b, (((1,), (1,)), ((), ())), preferred_element_type=F32)


def _dot_tn(a, b):
    return lax.dot_general(a, b, (((0,), (0,)), ((), ())), preferred_element_type=F32)


def _split2(x):
    hi = x.astype(BF16)
    lo = (x - hi.astype(F32)).astype(BF16)
    return hi, lo


def _split3(x):
    hi = x.astype(BF16)
    r = x - hi.astype(F32)
    mid = r.astype(BF16)
    lo = (r - mid.astype(F32)).astype(BF16)
    return hi, mid, lo


def _dot_exact_rhs(x, m):
    hi, mid, lo = _split3(x)
    return _dot(hi, m) + _dot(mid, m) + _dot(lo, m)


def _dot_exact_lhs(m, x):
    hi, mid, lo = _split3(x)
    return _dot(m, hi) + _dot(m, mid) + _dot(m, lo)


def _sigmoid(x):
    return 1.0 / (1.0 + jnp.exp(-x))


def _silu_pair(x):
    s = _sigmoid(x)
    return x * s, s * (1.0 + x * (1.0 - s))


def _iota(shape, dim):
    return lax.broadcasted_iota(jnp.int32, shape, dim)


def _ones_where(cond):
    return jnp.where(cond, 1.0, 0.0).astype(BF16)


def _head_blockdiag(w):
    return _ones_where((_iota((w, w), 0) >> 6) == (_iota((w, w), 1) >> 6))


def _group_sum(x, bd):
    hi, lo = _split2(x)
    return _dot(hi, bd) + _dot(lo, bd)


def _lane_pick(x, lane_idx, lane):
    return jnp.sum(jnp.where(lane_idx == lane, x, 0.0), axis=1, keepdims=True)


def _inproj(x, g, w_all, *, tm, tn):
    S, D = x.shape
    nj = PM // tn

    def body(x_ref, g_ref, w_ref, wff_ref, proj_ref, ff_ref, h_ref):
        @pl.when(pl.program_id(1) == 0)
        def _():
            xf = x_ref[...]
            ms = jnp.mean(xf * xf, axis=-1, keepdims=True)
            h = (xf * lax.rsqrt(ms + EPS) * g_ref[...]).astype(BF16)
            h_ref[...] = h
            ff_ref[...] = _dot(h, wff_ref[...])

        proj_ref[...] = _dot(h_ref[...], w_ref[...])

    return pl.pallas_call(
        body, name="inproj", grid=(S // tm, nj),
        in_specs=[pl.BlockSpec((tm, D), lambda i, j: (i, 0)),
                  pl.BlockSpec((1, D), lambda i, j: (0, 0)),
                  pl.BlockSpec((D, tn), lambda i, j: (0, j)),
                  pl.BlockSpec((D, LANES), lambda i, j: (0, PM // LANES))],
        out_specs=[pl.BlockSpec((tm, tn), lambda i, j: (i, j)),
                   pl.BlockSpec((tm, LANES), lambda i, j: (i, 0)),
                   pl.BlockSpec((tm, D), lambda i, j: (i, 0))],
        out_shape=[jax.ShapeDtypeStruct((S, PM), F32), jax.ShapeDtypeStruct((S, LANES), F32),
                   jax.ShapeDtypeStruct((S, D), BF16)],
        compiler_params=_cparams(dimension_semantics=("arbitrary", "arbitrary")),
    )(x, g, w_all, w_all)


def _pool_group_select(lane_group, vals):
    return jnp.where(lane_group == 0, vals[0], jnp.where(lane_group == 1, vals[1], jnp.where(lane_group == 2, vals[2], vals[3])))


def _prep(projm, ffo, qg, kg, bfp, wpd, ps, *, ts):
    S = projm.shape[0]
    nb = S // ts
    hb = ts // POOL_HALO

    def body(fq_ref, fk_ref, fv_ref, pp_ref, halo_ref, ff_ref, sq_ref, sk_ref, sv_ref,
             qg_ref, kg_ref, bf_ref, wpd_ref, ps_ref,
             qn_ref, ka_ref, kb_ref, v_ref, sqo_ref, sko_ref, svo_ref, pooled_ref, yp_ref, pm_ref,
             carry_ref, c_ref, buf_ref):
        i = pl.program_id(0)
        bd = _head_blockdiag(FOX_W)
        normed = []
        for src, g_ref in ((fq_ref, qg_ref), (fk_ref, kg_ref)):
            q = src[...]
            ss = _group_sum(q * q, bd)
            normed.append(q * lax.rsqrt(ss * (1.0 / HEAD_DIM) + EPS) * g_ref[...])
        qn_ref[...] = (normed[0] * QK_SCALE).astype(BF16)
        kn = normed[1]
        v_ref[...] = fv_ref[...].astype(BF16)
        sqo_ref[...] = (sq_ref[...] * QK_SCALE).astype(BF16)
        sko_ref[...] = sk_ref[...].astype(BF16)
        svo_ref[...] = sv_ref[...].astype(BF16)

        @pl.when(i == 0)
        def _():
            carry_ref[...] = jnp.zeros_like(carry_ref)

        z = ff_ref[...] + bf_ref[...]
        lf = jnp.minimum(z, 0.0) - jnp.log(1.0 + jnp.exp(-jnp.abs(z)))
        tri = _ones_where(_iota((ts, ts), 1) <= _iota((ts, ts), 0))
        c = _dot_exact_lhs(tri, lf) + carry_ref[...]
        c_ref[...] = c
        carry_ref[...] = c_ref[ts - 1:ts, :]
        parts = jnp.concatenate(_split3(-c), axis=1)
        row = _iota((AUG * LANES, FOX_W), 0)
        col = _iota((AUG * LANES, FOX_W), 1)
        part, src = row >> 7, row & (LANES - 1)
        pair, off = col >> 7, col & (LANES - 1)
        sel_a = _ones_where((src == FF_STRIDE * pair) & (off == HEAD_DIM + part))
        sel_b = _ones_where((src == FF_STRIDE * pair + 1) & (off == part))
        first_half = (_iota((1, FOX_W), 1) & HEAD_DIM) == 0
        ka_ref[...] = jnp.where(first_half, kn, _dot(parts, sel_a)).astype(BF16)
        kb_ref[...] = jnp.where(first_half, _dot(parts, sel_b), kn).astype(BF16)

        x = pp_ref[:, 0:POOL_W]
        pg = pp_ref[:, POOL_W:2 * POOL_W]
        halo = jnp.where(i > 0, halo_ref[:, 0:POOL_W], 0.0)
        buf_ref[0:POOL_HALO, :] = halo
        buf_ref[POOL_HALO:POOL_HALO + ts, :] = x
        acc = x
        snaps = []
        for d in range(1, POOL_HALO):
            acc = acc + buf_ref[pl.ds(POOL_HALO - d, ts), :]
            if d + 1 in POOL_WINDOWS:
                snaps.append(acc)
        lane_group = _iota((1, POOL_W), 1) >> 6
        wsum = _pool_group_select(lane_group, snaps)
        wlen = _pool_group_select(lane_group, [float(w) for w in POOL_WINDOWS])
        tpos = (i * ts + _iota((ts, 1), 0) + 1).astype(F32)
        pooled = wsum / jnp.minimum(tpos, wlen) - x
        pb = pooled.astype(BF16)
        pooled_ref[...] = pb
        yp = _dot(pb, wpd_ref[...])
        yp_ref[...] = yp
        pm_ref[...] = (yp * ps_ref[...] * (pg * _sigmoid(pg))).astype(BF16)

    blk = lambda w, c: pl.BlockSpec((ts, w), lambda i: (i, c))
    full = lambda a: pl.BlockSpec(a.shape, lambda i: (0,) * a.ndim)
    out_shapes = [
        jax.ShapeDtypeStruct((S, FOX_W), BF16), jax.ShapeDtypeStruct((S, FOX_W), BF16), jax.ShapeDtypeStruct((S, FOX_W), BF16),
        jax.ShapeDtypeStruct((S, FOX_W), BF16),
        jax.ShapeDtypeStruct((S, SB_W), BF16), jax.ShapeDtypeStruct((S, SB_W), BF16), jax.ShapeDtypeStruct((S, SB_W), BF16),
        jax.ShapeDtypeStruct((S, POOL_W), BF16), jax.ShapeDtypeStruct((S, POOL_W), F32), jax.ShapeDtypeStruct((S, POOL_W), BF16),
    ]
    out_specs = [
        blk(FOX_W, 0), blk(FOX_W, 0), blk(FOX_W, 0), blk(FOX_W, 0),
        blk(SB_W, 0), blk(SB_W, 0), blk(SB_W, 0),
        blk(POOL_W, 0), blk(POOL_W, 0), blk(POOL_W, 0),
    ]
    return pl.pallas_call(
        body, name="prep", grid=(nb,),
        in_specs=[blk(FOX_W, C_FQ // FOX_W), blk(FOX_W, C_FK // FOX_W), blk(FOX_W, C_FV // FOX_W), blk(2 * POOL_W, C_PX // (2 * POOL_W)),
                  pl.BlockSpec((POOL_HALO, 2 * POOL_W), lambda i: (jnp.maximum(i * hb - 1, 0), C_PX // (2 * POOL_W))),
                  blk(LANES, 0),
                  blk(SB_W, C_SQ // SB_W), blk(SB_W, C_SK // SB_W), blk(SB_W, C_SV // SB_W),
                  full(qg), full(kg), full(bfp), full(wpd), full(ps)],
        out_specs=out_specs, out_shape=out_shapes,
        scratch_shapes=[pltpu.VMEM((1, LANES), F32), pltpu.VMEM((ts, LANES), F32), pltpu.VMEM((ts + POOL_HALO, POOL_W), F32)],
        compiler_params=_cparams(dimension_semantics=("arbitrary",)),
    )(projm, projm, projm, projm, projm, ffo, projm, projm, projm, qg, kg, bfp, wpd, ps)


def _pair_masks(x):
    ma = _iota((1, LANES), 1) < HEAD_DIM
    zero = jnp.zeros_like(x)
    return jnp.where(ma, x, zero), jnp.where(ma, zero, x)


def _aug_queries(q):
    lane = _iota((1, LANES), 1)
    one = jnp.ones_like(q)
    zero = jnp.zeros_like(q)
    qa = jnp.where(lane < HEAD_DIM, q, jnp.where(lane < HEAD_DIM + AUG, one, zero))
    qb = jnp.where(lane >= HEAD_DIM, q, jnp.where(lane < AUG, one, zero))
    return qa, qb


def _fox_fwd(qn, ka, kb, v, projm, *, tq, tk):
    S = qn.shape[0]
    npair = FOX_HEADS // 2

    def body(q_ref, ka_ref, kb_ref, v_ref, fg_ref, o_ref, lse_ref, fm_ref):
        qi = pl.program_id(1)
        lane = _iota((1, LANES), 1)
        ma = lane < HEAD_DIM
        qaug = _aug_queries(q_ref[...])
        k_refs = (ka_ref, kb_ref)

        def block(j, carry, masked):
            k0 = pl.multiple_of(j * tk, tk)
            vb = v_ref[pl.ds(k0, tk), :]
            if masked:
                mask = (k0 + _iota((tq, tk), 1)) <= (qi * tq + _iota((tq, tk), 0))
            scores = [_dot_nt(qaug[h], k_refs[h][pl.ds(k0, tk), :]) for h in range(2)]
            stats, ps = [], []
            for h in range(2):
                m, l, _ = carry[h]
                s = jnp.where(mask, scores[h], NEG) if masked else scores[h]
                m_new = jnp.maximum(m, jnp.max(s, axis=1, keepdims=True))
                alpha = jnp.exp(m - m_new)
                p = jnp.exp(s - m_new)
                stats.append((m_new, alpha * l + jnp.sum(p, axis=1, keepdims=True), alpha))
                ps.append(p.astype(BF16))
            pv = _dot(jnp.concatenate(ps, axis=0), vb)
            return tuple((stats[h][0], stats[h][1], stats[h][2] * carry[h][2] + pv[h * tq:(h + 1) * tq]) for h in range(2))

        init = tuple((jnp.full((tq, 1), NEG, F32), jnp.zeros((tq, 1), F32), jnp.zeros((tq, LANES), F32)) for _ in range(2))
        nfull = (qi * tq) // tk
        carry = lax.fori_loop(0, nfull, lambda j, c: block(j, c, False), init)
        for mi in range(max(1, tq // tk)):
            carry = block(nfull + mi, carry, True)
        (ma_, la, acca), (mb_, lb, accb) = carry
        o = jnp.where(ma, acca / la, accb / lb)
        o_ref[...] = o
        lse_ref[...] = jnp.where(ma, ma_ + jnp.log(la), mb_ + jnp.log(lb))
        fg = fg_ref[...]
        fm_ref[...] = (o * (fg * _sigmoid(fg))).astype(BF16)

    qblk = pl.BlockSpec((tq, LANES), lambda p, i: (i, p))
    kvblk = pl.BlockSpec((S, LANES), lambda p, i: (0, p))
    return pl.pallas_call(
        body, name="fox_fwd", grid=(npair, S // tq),
        in_specs=[qblk, kvblk, kvblk, kvblk,
                  pl.BlockSpec((tq, LANES), lambda p, i: (i, C_FG // LANES + p))],
        out_specs=[qblk, qblk, qblk],
        out_shape=[jax.ShapeDtypeStruct((S, FOX_W), F32), jax.ShapeDtypeStruct((S, FOX_W), F32), jax.ShapeDtypeStruct((S, FOX_W), BF16)],
        compiler_params=_cparams(dimension_semantics=("arbitrary", "arbitrary")),
    )(qn, ka, kb, v, projm)


def _suffix_sums(x, tmat2):
    return _dot(jnp.concatenate(_split2(x), axis=1), tmat2)


def _suffix_matrix(tk, inclusive):
    rr, cc = _iota((2 * tk, tk), 0) & (tk - 1), _iota((2 * tk, tk), 1)
    return _ones_where(rr >= cc) if inclusive else _ones_where(rr > cc)


def _sb_scores(qh, kb, causal, tmat2, r_runs):
    heads = range(2)
    zs = [_dot_nt(qh[h], kb) for h in heads]
    nsps = [jnp.minimum(-z, 0.0) - jnp.log(1.0 + jnp.exp(-jnp.abs(z))) for z in zs]
    lbs = nsps if causal is None else [jnp.where(causal, n, 0.0) for n in nsps]
    rins = [_suffix_sums(lb, tmat2) for lb in lbs]
    args = [zs[h] + lbs[h] + (rins[h] + r_runs[h]) for h in heads]
    a_s = [jnp.exp(arg if causal is None else jnp.where(causal, arg, NEG)) for arg in args]
    return zs, nsps, lbs, a_s


def _sb_fwd(sq, sk, sv, projm, *, tq, tk):
    S = sq.shape[0]
    npair = SB_HEADS // 2

    def body(q_ref, k_ref, v_ref, sg_ref, o_ref, sm_ref):
        qi = pl.program_id(1)
        lane = _iota((1, LANES), 1)
        ma = lane < HEAD_DIM
        qh = _pair_masks(q_ref[...])
        tmat2 = _suffix_matrix(tk, inclusive=False)
        nfull = (qi * tq) // tk

        def block(j, carry, masked):
            k0 = pl.multiple_of(j * tk, tk)
            kb = k_ref[pl.ds(k0, tk), :]
            vb = v_ref[pl.ds(k0, tk), :]
            causal = (k0 + _iota((tq, tk), 1)) < (qi * tq + _iota((tq, tk), 0)) if masked else None
            _, _, lbs, a_s = _sb_scores(qh, kb, causal, tmat2, [carry[h][0] for h in range(2)])
            pv = _dot(jnp.concatenate([a.astype(BF16) for a in a_s], axis=0), vb)
            return tuple((carry[h][0] + jnp.sum(lbs[h], axis=1, keepdims=True), carry[h][1] + pv[h * tq:(h + 1) * tq]) for h in range(2))

        init = tuple((jnp.zeros((tq, 1), F32), jnp.zeros((tq, LANES), F32)) for _ in range(2))
        carry = init
        for mi in reversed(range(max(1, tq // tk))):
            carry = block(nfull + mi, carry, True)
        (_, acca), (_, accb) = lax.fori_loop(0, nfull, lambda jj, c: block(nfull - 1 - jj, c, False), carry)
        o = jnp.where(ma, acca, accb)
        o_ref[...] = o
        sg = sg_ref[...]
        sm_ref[...] = (o * (sg * _sigmoid(sg))).astype(BF16)

    qblk = pl.BlockSpec((tq, LANES), lambda p, i: (i, p))
    kvblk = pl.BlockSpec((S, LANES), lambda p, i: (0, p))
    return pl.pallas_call(
        body, name="sb_fwd", grid=(npair, S // tq),
        in_specs=[qblk, kvblk, kvblk, pl.BlockSpec((tq, LANES), lambda p, i: (i, C_SG // LANES + p))],
        out_specs=[qblk, qblk],
        out_shape=[jax.ShapeDtypeStruct((S, SB_W), F32), jax.ShapeDtypeStruct((S, SB_W), BF16)],
        compiler_params=_cparams(dimension_semantics=("arbitrary", "arbitrary")),
    )(sq, sk, sv, projm)


def _outproj(x, fm, pm, sm, w_out, *, tm):
    S, D = x.shape

    def body(x_ref, fm_ref, pm_ref, sm_ref, w_ref, y_ref):
        y = x_ref[...] + _dot(fm_ref[...], w_ref[0:FOX_W, :])
        y = y + _dot(pm_ref[...], w_ref[FOX_W:FOX_W + POOL_W, :])
        y_ref[...] = y + _dot(sm_ref[...], w_ref[FOX_W + POOL_W:D_MIX, :])

    row = lambda w: pl.BlockSpec((tm, w), lambda i: (i, 0))
    return pl.pallas_call(
        body, name="outproj", grid=(S // tm,),
        in_specs=[row(D), row(FOX_W), row(POOL_W), row(SB_W), pl.BlockSpec((D_MIX, D), lambda i: (0, 0))],
        out_specs=row(D), out_shape=jax.ShapeDtypeStruct((S, D), F32),
        compiler_params=_cparams(dimension_semantics=("arbitrary",)),
    )(x, fm, pm, sm, w_out)


def _loss_head(y, target, *, tm):
    S, D = y.shape

    def body(y_ref, t_ref, dy_ref, sq_ref):
        @pl.when(pl.program_id(0) == 0)
        def _():
            sq_ref[...] = jnp.zeros_like(sq_ref)

        d = y_ref[...] - t_ref[...]
        dy_ref[...] = d * (1.0 / D)
        sq_ref[...] += jnp.sum(d * d, axis=0, keepdims=True)

    row = pl.BlockSpec((tm, D), lambda i: (i, 0))
    return pl.pallas_call(
        body, name="loss_head", grid=(S // tm,),
        in_specs=[row, row], out_specs=[row, pl.BlockSpec((1, D), lambda i: (0, 0))],
        out_shape=[jax.ShapeDtypeStruct((S, D), F32), jax.ShapeDtypeStruct((1, D), F32)],
        compiler_params=_cparams(dimension_semantics=("arbitrary",)),
    )(y, target)


def _outproj_bwd(dy, fm, pm, sm, w_out, *, tm):
    S, D = dy.shape

    def body(dy_ref, fm_ref, pm_ref, sm_ref, w_ref, dm_ref, dw_ref):
        @pl.when(pl.program_id(0) == 0)
        def _():
            dw_ref[...] = jnp.zeros_like(dw_ref)

        dyb = dy_ref[...].astype(BF16)
        dm_ref[...] = _dot_nt(dyb, w_ref[...])
        dw_ref[0:FOX_W, :] += _dot_tn(fm_ref[...], dyb)
        dw_ref[FOX_W:FOX_W + POOL_W, :] += _dot_tn(pm_ref[...], dyb)
        dw_ref[FOX_W + POOL_W:D_MIX, :] += _dot_tn(sm_ref[...], dyb)

    row = lambda w: pl.BlockSpec((tm, w), lambda i: (i, 0))
    wspec = pl.BlockSpec((D_MIX, D), lambda i: (0, 0))
    return pl.pallas_call(
        body, name="outproj_bwd", grid=(S // tm,),
        in_specs=[row(D), row(FOX_W), row(POOL_W), row(SB_W), wspec],
        out_specs=[row(D_MIX), wspec],
        out_shape=[jax.ShapeDtypeStruct((S, D_MIX), F32), jax.ShapeDtypeStruct((D_MIX, D), F32)],
        compiler_params=_cparams(dimension_semantics=("arbitrary",)),
    )(dy, fm, pm, sm, w_out)


def _fox_bwd(qn, ka, kb, v, o, lse, dmix, projm, *, tq, tk):
    S = qn.shape[0]
    npair = FOX_HEADS // 2

    def body(q_ref, ka_ref, kb_ref, v_ref, o_ref, lse_ref, dm_ref, fg_ref,
             dq_ref, dk_ref, dv_ref, dfg_ref, dct_ref, dcr_ref):
        qi = pl.program_id(1)

        @pl.when(qi == 0)
        def _():
            dk_ref[...] = jnp.zeros_like(dk_ref)
            dv_ref[...] = jnp.zeros_like(dv_ref)
            dct_ref[...] = jnp.zeros_like(dct_ref)

        lane = _iota((1, LANES), 1)
        ma = lane < HEAD_DIM
        qh = _pair_masks(q_ref[...])
        qaug = _aug_queries(q_ref[...])
        k_refs = (ka_ref, kb_ref)
        lsev = lse_ref[...]
        lse = (_lane_pick(lsev, lane, 0), _lane_pick(lsev, lane, HEAD_DIM))
        fg = fg_ref[...]
        silu, dsilu = _silu_pair(fg)
        dm = dm_ref[...]
        ov = o_ref[...]
        do = dm * silu
        dfg_ref[...] = dm * ov * dsilu
        dd = do * ov
        dsum = (jnp.sum(jnp.where(ma, dd, 0.0), axis=1, keepdims=True), jnp.sum(jnp.where(ma, 0.0, dd), axis=1, keepdims=True))
        doh = _pair_masks(do.astype(BF16))
        do2 = jnp.concatenate(doh, axis=0)
        q2 = jnp.concatenate(qh, axis=0)

        def block(j, carry, masked):
            dq = carry[0]
            rows = list(carry[1:])
            k0 = pl.multiple_of(j * tk, tk)
            vb = v_ref[pl.ds(k0, tk), :]
            if masked:
                mask = (k0 + _iota((tq, tk), 1)) <= (qi * tq + _iota((tq, tk), 0))
            heads = range(2)
            kaugs = [k_refs[h][pl.ds(k0, tk), :] for h in heads]
            scores = [_dot_nt(qaug[h], kaugs[h]) for h in heads]
            dps = [_dot_nt(doh[h], vb) for h in heads]
            ps, dss = [], []
            for h in heads:
                s = jnp.where(mask, scores[h], NEG) if masked else scores[h]
                p = jnp.exp(s - lse[h])
                dsf = p * (dps[h] - dsum[h])
                dct_ref[0, h:h + 1, pl.ds(k0, tk)] -= jnp.sum(dsf, axis=0, keepdims=True)
                rows[h] = rows[h] + jnp.sum(dsf, axis=1, keepdims=True)
                ps.append(p.astype(BF16))
                dss.append(dsf.astype(BF16))
            dv_ref[pl.ds(k0, tk), :] += _dot_tn(jnp.concatenate(ps, axis=0), do2)
            dk_ref[pl.ds(k0, tk), :] += _dot_tn(jnp.concatenate(dss, axis=0), q2)
            kh = jnp.concatenate([_pair_masks(kaugs[h])[h] for h in heads], axis=0)
            dq = dq + _dot(jnp.concatenate(dss, axis=1), kh)
            return (dq, rows[0], rows[1])

        zcol = jnp.zeros((tq, 1), F32)
        nfull = (qi * tq) // tk
        carry = lax.fori_loop(0, nfull, lambda j, c: block(j, c, False), (jnp.zeros((tq, LANES), F32), zcol, zcol))
        for mi in range(max(1, tq // tk)):
            carry = block(nfull + mi, carry, True)
        dq, rowa, rowb = carry
        dq_ref[...] = dq * QK_SCALE
        dcr_ref[0] = jnp.where(ma, rowa, rowb)

    qblk = pl.BlockSpec((tq, LANES), lambda p, i: (i, p))
    kvblk = pl.BlockSpec((S, LANES), lambda p, i: (0, p))
    f32out = jax.ShapeDtypeStruct((S, FOX_W), F32)
    ctblk = pl.BlockSpec((1, FF_STRIDE, S), lambda p, i: (p, 0, 0))
    return pl.pallas_call(
        body, name="fox_bwd", grid=(npair, S // tq),
        in_specs=[qblk, kvblk, kvblk, kvblk, qblk, qblk, qblk,
                  pl.BlockSpec((tq, LANES), lambda p, i: (i, C_FG // LANES + p))],
        out_specs=[qblk, kvblk, kvblk, qblk, ctblk, pl.BlockSpec((1, tq, LANES), lambda p, i: (p, i, 0))],
        out_shape=[f32out, f32out, f32out, f32out, jax.ShapeDtypeStruct((npair, FF_STRIDE, S), F32),
                   jax.ShapeDtypeStruct((npair, S, LANES), F32)],
        compiler_params=_cparams(dimension_semantics=("arbitrary", "arbitrary")),
    )(qn, ka, kb, v, o, lse, dmix, projm)


def _sb_bwd(sq, sk, sv, o, dmix, projm, *, tq, tk):
    S = sq.shape[0]
    npair = SB_HEADS // 2
    mix0 = (FOX_W + POOL_W) // LANES

    def body(q_ref, k_ref, v_ref, o_ref, dm_ref, sg_ref, dq_ref, dk_ref, dv_ref, dsg_ref):
        qi = pl.program_id(1)

        @pl.when(qi == 0)
        def _():
            dk_ref[...] = jnp.zeros_like(dk_ref)
            dv_ref[...] = jnp.zeros_like(dv_ref)

        lane = _iota((1, LANES), 1)
        ma = lane < HEAD_DIM
        qh = _pair_masks(q_ref[...])
        sg = sg_ref[...]
        silu, dsilu = _silu_pair(sg)
        dm = dm_ref[...]
        ov = o_ref[...]
        do = dm * silu
        dsg_ref[...] = dm * ov * dsilu
        dob = do.astype(BF16)
        dd = dob.astype(F32) * ov
        dsum = (jnp.sum(jnp.where(ma, dd, 0.0), axis=1, keepdims=True), jnp.sum(jnp.where(ma, 0.0, dd), axis=1, keepdims=True))
        doh = _pair_masks(dob)
        do2 = jnp.concatenate(doh, axis=0)
        q2 = jnp.concatenate(qh, axis=0)
        tmat2 = _suffix_matrix(tk, inclusive=False)
        tmat2_inc = _suffix_matrix(tk, inclusive=True)
        nfull = (qi * tq) // tk

        def block(j, carry, masked):
            dq = carry[2]
            k0 = pl.multiple_of(j * tk, tk)
            kb = k_ref[pl.ds(k0, tk), :]
            vb = v_ref[pl.ds(k0, tk), :]
            kh = _pair_masks(kb)
            causal = (k0 + _iota((tq, tk), 1)) < (qi * tq + _iota((tq, tk), 0)) if masked else None
            heads = range(2)
            das = [_dot_nt(doh[h], vb) for h in heads]
            zs, nsps, lbs, a_s = _sb_scores(qh, kb, causal, tmat2, [carry[h][0] for h in heads])
            abs_ = [a.astype(BF16) for a in a_s]
            us = [abs_[h].astype(F32) * das[h] for h in heads]
            uins = [_suffix_sums(u, tmat2_inc) for u in us]
            dzs = []
            for h in heads:
                cum_u = dsum[h] - (uins[h] + carry[h][1])
                dz = us[h] * jnp.exp(nsps[h]) - jnp.exp(zs[h] + nsps[h]) * cum_u
                if masked:
                    dz = jnp.where(causal, dz, 0.0)
                dzs.append(dz.astype(BF16))
            dv_ref[pl.ds(k0, tk), :] += _dot_tn(jnp.concatenate(abs_, axis=0), do2)
            dk_ref[pl.ds(k0, tk), :] += _dot_tn(jnp.concatenate(dzs, axis=0), q2)
            dq = dq + _dot(jnp.concatenate(dzs, axis=1), jnp.concatenate(kh, axis=0))
            new = [(carry[h][0] + jnp.sum(lbs[h], axis=1, keepdims=True), carry[h][1] + jnp.sum(us[h], axis=1, keepdims=True)) for h in heads]
            return (new[0], new[1], dq)

        zcol = jnp.zeros((tq, 1), F32)
        init = ((zcol, zcol), (zcol, zcol), jnp.zeros((tq, LANES), F32))
        carry = init
        for mi in reversed(range(max(1, tq // tk))):
            carry = block(nfull + mi, carry, True)
        dq = lax.fori_loop(0, nfull, lambda jj, c: block(nfull - 1 - jj, c, False), carry)[2]
        dq_ref[...] = dq * QK_SCALE

    qblk = pl.BlockSpec((tq, LANES), lambda p, i: (i, p))
    kvblk = pl.BlockSpec((S, LANES), lambda p, i: (0, p))
    f32out = jax.ShapeDtypeStruct((S, SB_W), F32)
    return pl.pallas_call(
        body, name="sb_bwd", grid=(npair, S // tq),
        in_specs=[qblk, kvblk, kvblk, qblk,
                  pl.BlockSpec((tq, LANES), lambda p, i: (i, mix0 + p)),
                  pl.BlockSpec((tq, LANES), lambda p, i: (i, C_SG // LANES + p))],
        out_specs=[qblk, kvblk, kvblk, qblk],
        out_shape=[f32out, f32out, f32out, f32out],
        compiler_params=_cparams(dimension_semantics=("arbitrary", "arbitrary")),
    )(sq, sk, sv, o, dmix, projm)


def _prep_bwd(projm, ffo, dqn, dkn, dct, dcr, dv, dfg, dsq, dsk, dsv, dsg, dmix, pooled, yp, qg, kg, bfp, wpd, ps, *, ts):
    S = projm.shape[0]
    nb = S // ts
    hb = ts // POOL_HALO
    npair = FOX_HEADS // 2
    last_halo = S // POOL_HALO - 1

    def body(fq_ref, fk_ref, pp_ref, pph_ref, ff_ref,
             dqn_ref, dkn_ref, dct_ref, dcr_ref, dv_ref, dfg_ref, dsq_ref, dsk_ref, dsv_ref, dsg_ref,
             dmp_ref, dmh_ref, pooled_ref, yp_ref, qg_ref, kg_ref, bf_ref, wpd_ref, ps_ref,
             dp_ref, dqg_ref, dkg_ref, dbf_ref, dwp_ref, dps_ref,
             carry_ref, dl_ref, buf_ref, dct_s):
        i = pl.program_id(0)
        blk = nb - 1 - i

        @pl.when(i == 0)
        def _():
            carry_ref[...] = jnp.zeros_like(carry_ref)
            dqg_ref[...] = jnp.zeros_like(dqg_ref)
            dkg_ref[...] = jnp.zeros_like(dkg_ref)
            dbf_ref[...] = jnp.zeros_like(dbf_ref)
            dwp_ref[...] = jnp.zeros_like(dwp_ref)
            dps_ref[...] = jnp.zeros_like(dps_ref)

        bd = _head_blockdiag(FOX_W)
        for raw_ref, g_ref, dn, dg_ref, col in ((fq_ref, qg_ref, dqn_ref[...], dqg_ref, C_FQ), (fk_ref, kg_ref, dkn_ref[...], dkg_ref, C_FK)):
            q = raw_ref[...]
            rstd = lax.rsqrt(_group_sum(q * q, bd) * (1.0 / HEAD_DIM) + EPS)
            xhat = q * rstd
            dg_ref[...] += jnp.sum(dn * xhat, axis=0, keepdims=True)
            dyg = dn * g_ref[...]
            mean = _group_sum(dyg * xhat, bd) * (1.0 / HEAD_DIM)
            dp_ref[:, col:col + FOX_W] = (rstd * (dyg - xhat * mean)).astype(BF16)
        dp_ref[:, C_FV:C_FV + FOX_W] = dv_ref[...].astype(BF16)
        dp_ref[:, C_FG:C_FG + FOX_W] = dfg_ref[...].astype(BF16)
        dp_ref[:, C_SQ:C_SQ + SB_W] = dsq_ref[...].astype(BF16)
        dp_ref[:, C_SK:C_SK + SB_W] = dsk_ref[...].astype(BF16)
        dp_ref[:, C_SV:C_SV + SB_W] = dsv_ref[...].astype(BF16)
        dp_ref[:, C_SG:C_SG + SB_W] = dsg_ref[...].astype(BF16)

        dct_s[...] = jnp.zeros_like(dct_s)
        for p in range(npair):
            dct_s[FF_STRIDE * p:FF_STRIDE * (p + 1), :] = dct_ref[p]
        dc = dct_s[...].T
        lane = _iota((1, LANES), 1)
        for p in range(npair):
            dcr = dcr_ref[p]
            dc = dc + jnp.where(lane == FF_STRIDE * p, _lane_pick(dcr, lane, 0), 0.0)
            dc = dc + jnp.where(lane == FF_STRIDE * p + 1, _lane_pick(dcr, lane, HEAD_DIM), 0.0)
        triu = _ones_where(_iota((ts, ts), 1) >= _iota((ts, ts), 0))
        dlf = _dot_exact_lhs(triu, dc) + carry_ref[...]
        dl_ref[...] = dlf
        carry_ref[...] = dl_ref[0:1, :]
        z = ff_ref[...] + bf_ref[...]
        dff = dlf * (1.0 / (1.0 + jnp.exp(z)))
        dbf_ref[...] += jnp.sum(dff, axis=0, keepdims=True)
        dp_ref[:, PM:PW] = dff.astype(BF16)

        psv = ps_ref[...]
        wpdv = wpd_ref[...]
        lane_group = _iota((1, POOL_W), 1) >> 6
        wlen = _pool_group_select(lane_group, [float(w) for w in POOL_WINDOWS])
        pg = pp_ref[:, POOL_W:2 * POOL_W]
        silu, dsilu = _silu_pair(pg)
        dmp = dmp_ref[...]
        ypv = yp_ref[...]
        dp_ref[:, C_PG:C_PG + POOL_W] = (dmp * (ypv * psv) * dsilu).astype(BF16)
        dps_ref[...] += jnp.sum(dmp * silu * ypv, axis=0, keepdims=True)
        dyp = (dmp * psv * silu).astype(BF16)
        dwp_ref[...] += _dot_tn(pooled_ref[...], dyp)
        dpooled = _dot_nt(dyp, wpdv)
        pgh = pph_ref[:, POOL_W:2 * POOL_W]
        dyph = (dmh_ref[...] * psv * (pgh * _sigmoid(pgh))).astype(BF16)
        dpooled_h = jnp.where(blk < nb - 1, _dot_nt(dyph, wpdv), 0.0)
        tpos = (blk * ts + _iota((ts, 1), 0) + 1).astype(F32)
        ev = dpooled / jnp.minimum(tpos, wlen)
        buf_ref[0:ts, :] = ev
        buf_ref[ts:ts + POOL_HALO, :] = dpooled_h / wlen
        acc = ev
        snaps = []
        for d in range(1, POOL_HALO):
            acc = acc + buf_ref[pl.ds(d, ts), :]
            if d + 1 in POOL_WINDOWS:
                snaps.append(acc)
        dp_ref[:, C_PX:C_PX + POOL_W] = (_pool_group_select(lane_group, snaps) - dpooled).astype(BF16)

    rblk = lambda w, c: pl.BlockSpec((ts, w), lambda i: (nb - 1 - i, c))
    full = lambda a: pl.BlockSpec(a.shape, lambda i: (0,) * a.ndim)
    halo = lambda w, c: pl.BlockSpec((POOL_HALO, w), lambda i: (jnp.minimum((nb - i) * hb, last_halo), c))
    acc_spec = lambda r, w: pl.BlockSpec((r, w), lambda i: (0, 0))
    return pl.pallas_call(
        body, name="prep_bwd", grid=(nb,),
        in_specs=[rblk(FOX_W, C_FQ // FOX_W), rblk(FOX_W, C_FK // FOX_W), rblk(2 * POOL_W, C_PX // (2 * POOL_W)),
                  halo(2 * POOL_W, C_PX // (2 * POOL_W)), rblk(LANES, 0),
                  rblk(FOX_W, 0), rblk(FOX_W, 0), pl.BlockSpec((npair, FF_STRIDE, ts), lambda i: (0, 0, nb - 1 - i)),
                  pl.BlockSpec((npair, ts, LANES), lambda i: (0, nb - 1 - i, 0)), rblk(FOX_W, 0), rblk(FOX_W, 0),
                  rblk(SB_W, 0), rblk(SB_W, 0), rblk(SB_W, 0), rblk(SB_W, 0),
                  rblk(POOL_W, FOX_W // POOL_W), halo(POOL_W, FOX_W // POOL_W), rblk(POOL_W, 0), rblk(POOL_W, 0),
                  full(qg), full(kg), full(bfp), full(wpd), full(ps)],
        out_specs=[rblk(PW, 0), acc_spec(1, FOX_W), acc_spec(1, FOX_W), acc_spec(1, LANES), acc_spec(POOL_W, POOL_W), acc_spec(1, POOL_W)],
        out_shape=[jax.ShapeDtypeStruct((S, PW), BF16), jax.ShapeDtypeStruct((1, FOX_W), F32), jax.ShapeDtypeStruct((1, FOX_W), F32),
                   jax.ShapeDtypeStruct((1, LANES), F32), jax.ShapeDtypeStruct((POOL_W, POOL_W), F32), jax.ShapeDtypeStruct((1, POOL_W), F32)],
        scratch_shapes=[pltpu.VMEM((1, LANES), F32), pltpu.VMEM((ts, LANES), F32), pltpu.VMEM((ts + POOL_HALO, POOL_W), F32),
                        pltpu.VMEM((LANES, ts), F32)],
        compiler_params=_cparams(dimension_semantics=("arbitrary",)),
    )(projm, projm, projm, projm, ffo, dqn, dkn, dct, dcr, dv, dfg, dsq, dsk, dsv, dsg, dmix, dmix, pooled, yp, qg, kg, bfp, wpd, ps)


def _inproj_dw(h, dproj, *, ts, tn):
    S, D = h.shape
    nj = PM // tn

    def body(h_ref, dp_ref, dpf_ref, dw_ref, dwf_ref):
        s = pl.program_id(1)

        @pl.when(s == 0)
        def _():
            dw_ref[...] = jnp.zeros_like(dw_ref)

        @pl.when((s == 0) & (pl.program_id(0) == 0))
        def _():
            dwf_ref[...] = jnp.zeros_like(dwf_ref)

        hv = h_ref[...]
        dw_ref[...] += _dot_tn(hv, dp_ref[...])

        @pl.when(pl.program_id(0) == 0)
        def _():
            dwf_ref[...] += _dot_tn(hv, dpf_ref[...])

    return pl.pallas_call(
        body, name="inproj_dw", grid=(nj, S // ts),
        in_specs=[pl.BlockSpec((ts, D), lambda j, s: (s, 0)),
                  pl.BlockSpec((ts, tn), lambda j, s: (s, j)),
                  pl.BlockSpec((ts, LANES), lambda j, s: (s, PM // LANES))],
        out_specs=[pl.BlockSpec((D, tn), lambda j, s: (0, j)), pl.BlockSpec((D, LANES), lambda j, s: (0, 0))],
        out_shape=[jax.ShapeDtypeStruct((D, PM), F32), jax.ShapeDtypeStruct((D, LANES), F32)],
        compiler_params=_cparams(dimension_semantics=("arbitrary", "arbitrary")),
    )(h, dproj, dproj)


def _inproj_dx(dproj, w_all, x, g, dy, *, tm):
    S, D = x.shape

    def body(dp_ref, w_ref, x_ref, g_ref, dy_ref, dx_ref, dg_ref):
        @pl.when(pl.program_id(0) == 0)
        def _():
            dg_ref[...] = jnp.zeros_like(dg_ref)

        dh = _dot_nt(dp_ref[...], w_ref[...])
        xf = x_ref[...]
        rstd = lax.rsqrt(jnp.mean(xf * xf, axis=-1, keepdims=True) + EPS)
        xhat = xf * rstd
        dg_ref[...] += jnp.sum(dh * xhat, axis=0, keepdims=True)
        dyg = dh * g_ref[...]
        mean = jnp.mean(dyg * xhat, axis=-1, keepdims=True)
        dx_ref[...] = rstd * (dyg - xhat * mean) + dy_ref[...]

    row = lambda w: pl.BlockSpec((tm, w), lambda i: (i, 0))
    return pl.pallas_call(
        body, name="inproj_dx", grid=(S // tm,),
        in_specs=[row(PW), pl.BlockSpec((D, PW), lambda i: (0, 0)), row(D), pl.BlockSpec((1, D), lambda i: (0, 0)), row(D)],
        out_specs=[row(D), pl.BlockSpec((1, D), lambda i: (0, 0))],
        out_shape=[jax.ShapeDtypeStruct((S, D), F32), jax.ShapeDtypeStruct((1, D), F32)],
        compiler_params=_cparams(dimension_semantics=("arbitrary",)),
    )(dproj, w_all, x, g, dy)


def _adamw(w, g, m, v):
    R, C = w.shape
    tr = R if R <= 512 else 256

    def body(w_ref, g_ref, m_ref, v_ref, d_ref, nm_ref, nv_ref):
        gv = g_ref[...]
        nm = ADAM_B1 * m_ref[...] + (1.0 - ADAM_B1) * gv
        nv = ADAM_B2 * v_ref[...] + (1.0 - ADAM_B2) * (gv * gv)
        m_hat = nm / (1.0 - ADAM_B1 ** ADAM_STEP)
        v_hat = nv / (1.0 - ADAM_B2 ** ADAM_STEP)
        d_ref[...] = -ADAM_LR * (m_hat / (jnp.sqrt(v_hat) + ADAM_EPS) + ADAM_WD * w_ref[...])
        nm_ref[...] = nm
        nv_ref[...] = nv

    spec = pl.BlockSpec((tr, C), lambda i: (i, 0))
    shp = jax.ShapeDtypeStruct((R, C), F32)
    return pl.pallas_call(
        body, name="adamw", grid=(R // tr,), in_specs=[spec] * 4, out_specs=[spec] * 3, out_shape=[shp] * 3,
        compiler_params=_cparams(dimension_semantics=("arbitrary",)),
    )(w, g, m, v)


def _adamw_nd(w, g, m, v):
    shape = w.shape
    two_d = (-1, shape[-1])
    outs = _adamw(w.reshape(two_d), g.reshape(two_d), m.reshape(two_d), v.reshape(two_d))
    return tuple(o.reshape(shape) for o in outs)


def _add_n(name, first, others, *, emit_bf16):
    shape = first.shape
    two_d = (-1, shape[-1])
    R = first.reshape(two_d).shape[0]
    C = shape[-1]
    tr = 256 if R % 256 == 0 else R
    n = len(others)

    def body(*refs):
        acc = refs[0][...]
        for r in refs[1:1 + n]:
            acc = acc + r[...].astype(F32)
        refs[1 + n][...] = acc
        if emit_bf16:
            refs[2 + n][...] = acc.astype(BF16)

    spec = pl.BlockSpec((tr, C), lambda i: (i, 0))
    out_shape = [jax.ShapeDtypeStruct((R, C), F32)] + ([jax.ShapeDtypeStruct((R, C), BF16)] if emit_bf16 else [])
    outs = pl.pallas_call(
        body, name=name, grid=(R // tr,), in_specs=[spec] * (1 + n), out_specs=[spec] * len(out_shape), out_shape=out_shape,
        compiler_params=_cparams(dimension_semantics=("arbitrary",)),
    )(first.reshape(two_d), *[o.reshape(two_d) for o in others])
    return tuple(o.reshape(shape) for o in outs)


FLIP_C = (0, 0, 1)
FLIP_X = (1, 0, 0)
FLIP_Y = (0, 1, 0)
FLIP_XY = (1, 1, 0)
MESH = pl.DeviceIdType.MESH


def _peer(flip):
    me = (lax.axis_index("x"), lax.axis_index("y"), lax.axis_index("c"))
    return tuple(1 - a if f else a for a, f in zip(me, flip))


def _exchange(name, arrays, flips):
    n = len(arrays)

    def body(*refs):
        srcs, dsts = refs[:n], refs[n:2 * n]
        send_sems, recv_sems = refs[2 * n:]
        copies = [pltpu.make_async_remote_copy(src_ref=srcs[k], dst_ref=dsts[k], send_sem=send_sems.at[k], recv_sem=recv_sems.at[k],
                                               device_id=_peer(flips[k]), device_id_type=MESH) for k in range(n)]
        for cp in copies:
            cp.start()
        for cp in copies:
            cp.wait()

    anyspec = pl.BlockSpec(memory_space=pl.ANY)
    return pl.pallas_call(
        body, name=name, in_specs=[anyspec] * n, out_specs=[anyspec] * n,
        out_shape=[jax.ShapeDtypeStruct(a.shape, a.dtype) for a in arrays],
        scratch_shapes=[pltpu.SemaphoreType.DMA((n,)), pltpu.SemaphoreType.DMA((n,))],
    )(*arrays)


def _exchange_add(name, x, flip):
    def body(x_ref, o_ref, buf_ref, send_sem, recv_sem):
        cp = pltpu.make_async_remote_copy(src_ref=x_ref, dst_ref=buf_ref, send_sem=send_sem, recv_sem=recv_sem,
                                          device_id=_peer(flip), device_id_type=MESH)
        cp.start()
        cp.wait()
        o_ref[...] = x_ref[...] + buf_ref[...]

    vspec = pl.BlockSpec(memory_space=pltpu.VMEM)
    return pl.pallas_call(
        body, name=name, in_specs=[vspec], out_specs=vspec, out_shape=jax.ShapeDtypeStruct(x.shape, x.dtype),
        scratch_shapes=[pltpu.VMEM(x.shape, x.dtype), pltpu.SemaphoreType.DMA, pltpu.SemaphoreType.DMA],
    )(x)


def _chip_index():
    return 2 * lax.axis_index("x") + lax.axis_index("y")


def _gather_weights(w_in, w_out):
    half = DEPTH // 2
    wi = w_in.astype(BF16)
    wo = w_out.astype(BF16)
    masks = (2, 1, 3)
    flips = (FLIP_X, FLIP_Y, FLIP_XY)
    n_first = 2 * len(masks)

    def body(wi_ref, wo_ref, gi_ref, go_ref, send_sems, recv_sems):
        c = lax.axis_index("c")
        j = _chip_index()
        mine = pl.ds(half * c, half)
        theirs = pl.ds(half * (1 - c), half)
        srcs = (wi_ref, wo_ref)
        dsts = (gi_ref, go_ref)

        def copy(idx, src, dst, flip):
            return pltpu.make_async_remote_copy(src_ref=src, dst_ref=dst, send_sem=send_sems.at[idx], recv_sem=recv_sems.at[idx],
                                                device_id=_peer(flip), device_id_type=MESH)

        first = [copy(2 * k + a, srcs[a].at[mine], dsts[a].at[j, mine], flips[k]) for k in range(len(masks)) for a in range(2)]
        for cp in first:
            cp.start()
        passed = []
        for k, m in enumerate(masks):
            for a in range(2):
                slot = dsts[a].at[j ^ m, mine]
                copy(2 * k + a, slot, slot, flips[k]).wait_recv()
                fwd = copy(n_first + 2 * k + a, slot, slot, FLIP_C)
                fwd.start()
                passed.append(fwd)
        for k, m in enumerate(masks):
            for a in range(2):
                slot = dsts[a].at[j ^ m, theirs]
                copy(n_first + 2 * k + a, slot, slot, FLIP_C).wait_recv()
        for cp in first + passed:
            cp.wait_send()

    anyspec = pl.BlockSpec(memory_space=pl.ANY)
    gi, go = pl.pallas_call(
        body, name="gather_weights", in_specs=[anyspec] * 2, out_specs=[anyspec] * 2,
        out_shape=[jax.ShapeDtypeStruct((4,) + wi.shape, BF16), jax.ShapeDtypeStruct((4,) + wo.shape, BF16)],
        scratch_shapes=[pltpu.SemaphoreType.DMA((2 * n_first,)), pltpu.SemaphoreType.DMA((2 * n_first,))],
    )(wi, wo)
    own = lax.broadcasted_iota(jnp.int32, (4, 1, 1, 1), 0) == _chip_index()
    gi = jnp.where(own, wi[None], gi)
    go = jnp.where(own, wo[None], go)
    w_in_full = jnp.transpose(gi, (1, 2, 0, 3)).reshape(DEPTH, wi.shape[1], 4 * wi.shape[2])
    w_out_full = jnp.transpose(go, (1, 0, 2, 3)).reshape(DEPTH, 4 * wo.shape[1], wo.shape[2])
    return w_in_full, w_out_full


def _to_aligned(w_in_full):
    L, D, _ = w_in_full.shape
    npair = FOX_HEADS // 2
    ff = w_in_full[..., ORIG_FF:ORIG_REST].reshape(L, D, npair, 2)
    ff = jnp.pad(ff, ((0, 0), (0, 0), (0, 0), (0, FF_STRIDE - 2))).reshape(L, D, npair * FF_STRIDE)
    ff = jnp.pad(ff, ((0, 0), (0, 0), (0, LANES - npair * FF_STRIDE)))
    return jnp.concatenate([w_in_full[..., :ORIG_FOX], w_in_full[..., ORIG_REST:], ff], axis=-1)


def _from_aligned(dw_all):
    L, D, _ = dw_all.shape
    npair = FOX_HEADS // 2
    ff = dw_all[..., PM:PM + npair * FF_STRIDE].reshape(L, D, npair, FF_STRIDE)[..., :2].reshape(L, D, FOX_HEADS)
    return jnp.concatenate([dw_all[..., :ORIG_FOX], ff, dw_all[..., ORIG_FOX:PM]], axis=-1)


def _reduce_scatter(parts):
    c = lax.axis_index("c")
    j = _chip_index()
    half = DEPTH // 2
    n = len(parts)
    give = [lax.dynamic_slice_in_dim(p, half * (1 - c), half, axis=1).astype(BF16) for p in parts]
    keep = [lax.dynamic_slice_in_dim(p, half * c, half, axis=1) for p in parts]
    got = _exchange("rs_d2d", give, (FLIP_C,) * n)
    chip = [_add_n("rs_add_chip", k, [g], emit_bf16=True) for k, g in zip(keep, got)]
    masks = (2, 1, 3)
    flips = (FLIP_X, FLIP_Y, FLIP_XY)
    sends, sflips = [], []
    for f32_sum, bf in chip:
        for m, fl in zip(masks, flips):
            sends.append(lax.dynamic_index_in_dim(bf, j ^ m, axis=0, keepdims=False))
            sflips.append(fl)
    got = _exchange("rs_ici", sends, tuple(sflips))
    mine = []
    for a, (f32_sum, bf) in enumerate(chip):
        own = lax.dynamic_index_in_dim(f32_sum, j, axis=0, keepdims=False)
        mine.append(_add_into_half("rs_add_all", own, list(got[3 * a:3 * a + 3])))
    return _share_halves(mine)


def _add_into_half(name, first, others):
    half, rows, C = first.shape
    tr = min(256, rows)
    n = len(others)

    def body(c_ref, *refs):
        acc = refs[0][...]
        for r in refs[1:1 + n]:
            acc = acc + r[...].astype(F32)
        refs[1 + n][...] = acc

    grid_spec = pltpu.PrefetchScalarGridSpec(
        num_scalar_prefetch=1, grid=(half, rows // tr),
        in_specs=[pl.BlockSpec((1, tr, C), lambda l, i, c_ref: (l, i, 0))] * (1 + n),
        out_specs=pl.BlockSpec((1, tr, C), lambda l, i, c_ref: (c_ref[0] * half + l, i, 0)))
    return pl.pallas_call(
        body, name=name, grid_spec=grid_spec, out_shape=jax.ShapeDtypeStruct((2 * half, rows, C), F32),
        compiler_params=_cparams(dimension_semantics=("arbitrary", "arbitrary")),
    )(lax.axis_index("c").astype(jnp.int32).reshape(1), first, *others)


def _share_halves(bufs):
    n = len(bufs)
    half = DEPTH // 2

    def body(*refs):
        outs = refs[n:2 * n]
        send_sems, recv_sems = refs[2 * n:]
        lay = pl.ds(half * lax.axis_index("c"), half)
        copies = [pltpu.make_async_remote_copy(src_ref=outs[k].at[lay], dst_ref=outs[k].at[lay], send_sem=send_sems.at[k],
                                               recv_sem=recv_sems.at[k], device_id=_peer(FLIP_C), device_id_type=MESH) for k in range(n)]
        for cp in copies:
            cp.start()
        for cp in copies:
            cp.wait()

    anyspec = pl.BlockSpec(memory_space=pl.ANY)
    return pl.pallas_call(
        body, name="rs_share", in_specs=[anyspec] * n, out_specs=[anyspec] * n,
        out_shape=[jax.ShapeDtypeStruct(a.shape, a.dtype) for a in bufs],
        input_output_aliases={k: k for k in range(n)},
        scratch_shapes=[pltpu.SemaphoreType.DMA((n,)), pltpu.SemaphoreType.DMA((n,))],
    )(*bufs)


def _all_reduce_small(x):
    x = _exchange_add("ar_c", x, FLIP_C)
    x = _exchange_add("ar_y", x, FLIP_Y)
    return _exchange_add("ar_x", x, FLIP_X)


def _blocks(S):
    return dict(tm=min(512, S), ts=min(512, S), tq=min(512, S), tk=min(512, S), tks=min(256, S))


def _pair_pad(vec):
    npair = FOX_HEADS // 2
    v = jnp.pad(vec.reshape(npair, 2), ((0, 0), (0, FF_STRIDE - 2))).reshape(1, npair * FF_STRIDE)
    return jnp.pad(v, ((0, 0), (0, LANES - npair * FF_STRIDE)))


def _pair_unpad(row):
    npair = FOX_HEADS // 2
    return row[0, :npair * FF_STRIDE].reshape(npair, FF_STRIDE)[:, :2].reshape(FOX_HEADS)


def _pool_blockdiag(w_pool):
    g, cg, _ = w_pool.shape
    eye = jnp.eye(g, dtype=w_pool.dtype)
    return jnp.einsum("gh,gcd->gchd", eye, w_pool).reshape(g * cg, g * cg)


def _layer_params(norm_g, b_f, q_norm_g, k_norm_g, w_pool, pool_scale):
    return dict(g=norm_g.reshape(1, -1), qg=jnp.tile(q_norm_g, FOX_HEADS).reshape(1, FOX_W), kg=jnp.tile(k_norm_g, FOX_HEADS).reshape(1, FOX_W),
                bfp=_pair_pad(b_f), wpd=_pool_blockdiag(w_pool).astype(BF16), ps=pool_scale.reshape(1, POOL_W))


def _layer_fwd(x, w_all, w_out, prm, bs):
    projm, ffo, h = _inproj(x, prm["g"], w_all, tm=bs["tm"], tn=512)
    qn, ka, kb, v, sq, sk, sv, pooled, yp, pm = _prep(projm, ffo, prm["qg"], prm["kg"], prm["bfp"], prm["wpd"], prm["ps"], ts=bs["ts"])
    o, lse, fm = _fox_fwd(qn, ka, kb, v, projm, tq=bs["tq"], tk=bs["tk"])
    so, sm = _sb_fwd(sq, sk, sv, projm, tq=bs["tq"], tk=bs["tks"])
    y = _outproj(x, fm, pm, sm, w_out, tm=bs["tm"])
    saved = dict(x=x, projm=projm, ffo=ffo, h=h, qn=qn, ka=ka, kb=kb, v=v, sq=sq, sk=sk, sv=sv, pooled=pooled, yp=yp,
                 o=o, lse=lse, so=so, fm=fm, pm=pm, sm=sm)
    return y, saved


def _layer_bwd(dy, w_all, w_out, prm, sv_, bs):
    dmix, dw_out = _outproj_bwd(dy, sv_["fm"], sv_["pm"], sv_["sm"], w_out, tm=bs["tm"])
    dqn, dkn, dv, dfg, dct, dcr = _fox_bwd(sv_["qn"], sv_["ka"], sv_["kb"], sv_["v"], sv_["o"], sv_["lse"], dmix, sv_["projm"],
                                      tq=bs["tq"], tk=bs["tk"])
    dsq, dsk, dsv, dsg = _sb_bwd(sv_["sq"], sv_["sk"], sv_["sv"], sv_["so"], dmix, sv_["projm"], tq=bs["tq"], tk=bs["tks"])
    dproj, dqg, dkg, dbf, dwp, dps = _prep_bwd(sv_["projm"], sv_["ffo"], dqn, dkn, dct, dcr, dv, dfg, dsq, dsk, dsv, dsg, dmix,
                                               sv_["pooled"], sv_["yp"], prm["qg"], prm["kg"], prm["bfp"], prm["wpd"], prm["ps"], ts=bs["ts"])
    dwm, dwf = _inproj_dw(sv_["h"], dproj, ts=bs["ts"], tn=512)
    dx, dg = _inproj_dx(dproj, w_all, sv_["x"], prm["g"], dy, tm=min(256, bs["tm"]))
    grads = dict(
        w_all=jnp.concatenate([dwm, dwf], axis=-1), w_out=dw_out, norm_g=dg[0],
        b_f=_pair_unpad(dbf), q_norm_g=dqg.reshape(FOX_HEADS, HEAD_DIM).sum(0), k_norm_g=dkg.reshape(FOX_HEADS, HEAD_DIM).sum(0),
        w_pool=jnp.stack([dwp[HEAD_DIM * g:HEAD_DIM * (g + 1), HEAD_DIM * g:HEAD_DIM * (g + 1)] for g in range(4)]),
        pool_scale=dps[0])
    return dx, grads


def _local_step(x, target, w_all, w_out, norm_g, b_f, q_norm_g, k_norm_g, w_pool, pool_scale):
    S, D = x.shape
    bs = _blocks(S)
    prms = [_layer_params(norm_g[l], b_f[l], q_norm_g[l], k_norm_g[l], w_pool[l], pool_scale[l]) for l in range(DEPTH)]
    saved = []
    y = x
    for l in range(DEPTH):
        y, s_ = _layer_fwd(y, w_all[l], w_out[l], prms[l], bs)
        saved.append(s_)
    dy, sq = _loss_head(y, target, tm=bs["tm"])
    loss = 0.5 * jnp.sum(sq) / D
    grads = [None] * DEPTH
    for l in reversed(range(DEPTH)):
        dy, grads[l] = _layer_bwd(dy, w_all[l], w_out[l], prms[l], saved[l], bs)
    stacked = {k: jnp.stack([g[k] for g in grads]) for k in grads[0]}
    return loss, dy, stacked


SMALL = ("norm_g", "b_f", "q_norm_g", "k_norm_g", "w_pool", "pool_scale")


def _pack_small(gr):
    flat = jnp.concatenate([gr[k].reshape(-1) for k in SMALL])
    pad = (-flat.shape[0]) % (8 * LANES)
    return jnp.pad(flat, (0, pad)).reshape(-1, LANES)


def _unpack_small(packed, like):
    flat = packed.reshape(-1)
    out, off = {}, 0
    for k in SMALL:
        n = like[k].size
        out[k] = flat[off:off + n].reshape(like[k].shape)
        off += n
    return out


def kernel(x, norm_g, w_in, b_f, q_norm_g, k_norm_g, w_pool, pool_scale, w_out, loss_target, m_norm_g, m_w_in, m_b_f, m_q_norm_g, m_k_norm_g, m_w_pool, m_pool_scale, m_w_out, v_norm_g, v_w_in, v_b_f, v_q_norm_g, v_k_norm_g, v_w_pool, v_pool_scale, v_w_out):
    weights = dict(norm_g=norm_g, w_in=w_in, b_f=b_f, q_norm_g=q_norm_g, k_norm_g=k_norm_g, w_pool=w_pool, pool_scale=pool_scale, w_out=w_out)
    mom_m = dict(norm_g=m_norm_g, w_in=m_w_in, b_f=m_b_f, q_norm_g=m_q_norm_g, k_norm_g=m_k_norm_g, w_pool=m_w_pool, pool_scale=m_pool_scale, w_out=m_w_out)
    mom_v = dict(norm_g=v_norm_g, w_in=v_w_in, b_f=v_b_f, q_norm_g=v_q_norm_g, k_norm_g=v_k_norm_g, w_pool=v_w_pool, pool_scale=v_pool_scale, w_out=v_w_out)
    shard_cols = w_in.shape[2]
    shard_rows = w_out.shape[1]

    w_in_full, w_out_full = _gather_weights(w_in, w_out)
    w_all = _to_aligned(w_in_full)
    loss, dx, gr = _local_step(x[0], loss_target[0], w_all, w_out_full, norm_g, b_f, q_norm_g, k_norm_g, w_pool, pool_scale)
    loss = lax.psum(loss, ("x", "y", "c"))

    dw_in_full = _from_aligned(gr["w_all"])
    d_model = dw_in_full.shape[1]
    part_in = jnp.moveaxis(dw_in_full.reshape(DEPTH, d_model, 4, shard_cols), 2, 0)
    part_out = jnp.moveaxis(gr["w_out"].reshape(DEPTH, 4, shard_rows, -1), 1, 0)
    g_w_in, g_w_out = _reduce_scatter([part_in, part_out])
    small = _unpack_small(_all_reduce_small(_pack_small(gr)), {k: weights[k] for k in SMALL})
    grad_w = dict(small, w_in=g_w_in, w_out=g_w_out)

    names = ("norm_g", "w_in", "b_f", "q_norm_g", "k_norm_g", "w_pool", "pool_scale", "w_out")
    upd = {k: _adamw_nd(weights[k], grad_w[k], mom_m[k], mom_v[k]) for k in names}
    return (loss, dx[None], *[grad_w[k] for k in names], *[upd[k][0] for k in names], *[upd[k][1] for k in names], *[upd[k][2] for k in names])
```

```python
import functools

import jax
import jax.numpy as jnp
from jax import lax
from jax.experimental import pallas as pl
from jax.experimental.pallas import tpu as pltpu

F32 = jnp.float32
BF16 = jnp.bfloat16

DEPTH = 4
HEAD_DIM = 64
FOX_HEADS = 8
SB_HEADS = 4
FOX_W = FOX_HEADS * HEAD_DIM
SB_W = SB_HEADS * HEAD_DIM
POOL_W = 256
POOL_WINDOWS = (2, 4, 8, 16)
POOL_HALO = 16
D_MIX = FOX_W + POOL_W + SB_W
EPS = 1e-6
NEG = -1e30
QK_SCALE = HEAD_DIM ** -0.5

ORIG_FOX = 4 * FOX_W
ORIG_FF = ORIG_FOX
ORIG_REST = ORIG_FF + FOX_HEADS
D_IN = ORIG_REST + 2 * POOL_W + 4 * SB_W

C_FQ, C_FK, C_FV, C_FG = 0, FOX_W, 2 * FOX_W, 3 * FOX_W
C_PX = 4 * FOX_W
C_PG = C_PX + POOL_W
C_SQ = C_PG + POOL_W
C_SK, C_SV, C_SG = C_SQ + SB_W, C_SQ + 2 * SB_W, C_SQ + 3 * SB_W
PM = C_SG + SB_W
LANES = 128
PW = PM + LANES
FF_STRIDE = 8
AUG = 3

ADAM_LR = 0.001
ADAM_B1 = 0.9
ADAM_B2 = 0.999
ADAM_EPS = 1e-08
ADAM_WD = 0.01
ADAM_STEP = 10

VMEM_LIMIT = 48 * 1024 * 1024


def _cparams(**kw):
    return pltpu.CompilerParams(vmem_limit_bytes=VMEM_LIMIT, **kw)


def _dot(a, b):
    return jnp.dot(a, b, preferred_element_type=F32)


def _dot_nt(a, b):
    return lax.dot_general(a, b, (((1,), (1,)), ((), ())), preferred_element_type=F32)


def _dot_tn(a, b):
    return lax.dot_general(a, b, (((0,), (0,)), ((), ())), preferred_element_type=F32)


def _split2(x):
    hi = x.astype(BF16)
    lo = (x - hi.astype(F32)).astype(BF16)
    return hi, lo


def _split3(x):
    hi = x.astype(BF16)
    r = x - hi.astype(F32)
    mid = r.astype(BF16)
    lo = (r - mid.astype(F32)).astype(BF16)
    return hi, mid, lo


def _dot_exact_rhs(x, m):
    hi, mid, lo = _split3(x)
    return _dot(hi, m) + _dot(mid, m) + _dot(lo, m)


def _dot_exact_lhs(m, x):
    hi, mid, lo = _split3(x)
    return _dot(m, hi) + _dot(m, mid) + _dot(m, lo)


def _sigmoid(x):
    return 1.0 / (1.0 + jnp.exp(-x))


def _silu_pair(x):
    s = _sigmoid(x)
    return x * s, s * (1.0 + x * (1.0 - s))


def _iota(shape, dim):
    return lax.broadcasted_iota(jnp.int32, shape, dim)


def _ones_where(cond):
    return jnp.where(cond, 1.0, 0.0).astype(BF16)


def _head_blockdiag(w):
    return _ones_where((_iota((w, w), 0) >> 6) == (_iota((w, w), 1) >> 6))


def _group_sum(x, bd):
    hi, lo = _split2(x)
    return _dot(hi, bd) + _dot(lo, bd)


def _lane_pick(x, lane_idx, lane):
    return jnp.sum(jnp.where(lane_idx == lane, x, 0.0), axis=1, keepdims=True)


def _inproj(x, g, w_all, *, tm, tn):
    S, D = x.shape
    nj = PM // tn

    def body(x_ref, g_ref, w_ref, wff_ref, proj_ref, ff_ref, h_ref):
        @pl.when(pl.program_id(1) == 0)
        def _():
            xf = x_ref[...]
            ms = jnp.mean(xf * xf, axis=-1, keepdims=True)
            h = (xf * lax.rsqrt(ms + EPS) * g_ref[...]).astype(BF16)
            h_ref[...] = h
            ff_ref[...] = _dot(h, wff_ref[...])

        proj_ref[...] = _dot(h_ref[...], w_ref[...])

    return pl.pallas_call(
        body, name="inproj", grid=(S // tm, nj),
        in_specs=[pl.BlockSpec((tm, D), lambda i, j: (i, 0)),
                  pl.BlockSpec((1, D), lambda i, j: (0, 0)),
                  pl.BlockSpec((D, tn), lambda i, j: (0, j)),
                  pl.BlockSpec((D, LANES), lambda i, j: (0, PM // LANES))],
        out_specs=[pl.BlockSpec((tm, tn), lambda i, j: (i, j)),
                   pl.BlockSpec((tm, LANES), lambda i, j: (i, 0)),
                   pl.BlockSpec((tm, D), lambda i, j: (i, 0))],
        out_shape=[jax.ShapeDtypeStruct((S, PM), F32), jax.ShapeDtypeStruct((S, LANES), F32),
                   jax.ShapeDtypeStruct((S, D), BF16)],
        compiler_params=_cparams(dimension_semantics=("arbitrary", "arbitrary")),
    )(x, g, w_all, w_all)


def _pool_group_select(lane_group, vals):
    return jnp.where(lane_group == 0, vals[0], jnp.where(lane_group == 1, vals[1], jnp.where(lane_group == 2, vals[2], vals[3])))


def _prep(projm, ffo, qg, kg, bfp, wpd, ps, *, ts):
    S = projm.shape[0]
    nb = S // ts
    hb = ts // POOL_HALO

    def body(fq_ref, fk_ref, fv_ref, pp_ref, halo_ref, ff_ref, sq_ref, sk_ref, sv_ref,
             qg_ref, kg_ref, bf_ref, wpd_ref, ps_ref,
             qn_ref, ka_ref, kb_ref, v_ref, sqo_ref, sko_ref, svo_ref, pooled_ref, yp_ref, pm_ref,
             carry_ref, c_ref, buf_ref):
        i = pl.program_id(0)
        bd = _head_blockdiag(FOX_W)
        normed = []
        for src, g_ref in ((fq_ref, qg_ref), (fk_ref, kg_ref)):
            q = src[...]
            ss = _group_sum(q * q, bd)
            normed.append(q * lax.rsqrt(ss * (1.0 / HEAD_DIM) + EPS) * g_ref[...])
        qn_ref[...] = (normed[0] * QK_SCALE).astype(BF16)
        kn = normed[1]
        v_ref[...] = fv_ref[...].astype(BF16)
        sqo_ref[...] = (sq_ref[...] * QK_SCALE).astype(BF16)
        sko_ref[...] = sk_ref[...].astype(BF16)
        svo_ref[...] = sv_ref[...].astype(BF16)

        @pl.when(i == 0)
        def _():
            carry_ref[...] = jnp.zeros_like(carry_ref)

        z = ff_ref[...] + bf_ref[...]
        lf = jnp.minimum(z, 0.0) - jnp.log(1.0 + jnp.exp(-jnp.abs(z)))
        tri = _ones_where(_iota((ts, ts), 1) <= _iota((ts, ts), 0))
        c = _dot_exact_lhs(tri, lf) + carry_ref[...]
        c_ref[...] = c
        carry_ref[...] = c_ref[ts - 1:ts, :]
        parts = jnp.concatenate(_split3(-c), axis=1)
        row = _iota((AUG * LANES, FOX_W), 0)
        col = _iota((AUG * LANES, FOX_W), 1)
        part, src = row >> 7, row & (LANES - 1)
        pair, off = col >> 7, col & (LANES - 1)
        sel_a = _ones_where((src == FF_STRIDE * pair) & (off == HEAD_DIM + part))
        sel_b = _ones_where((src == FF_STRIDE * pair + 1) & (off == part))
        first_half = (_iota((1, FOX_W), 1) & HEAD_DIM) == 0
        ka_ref[...] = jnp.where(first_half, kn, _dot(parts, sel_a)).astype(BF16)
        kb_ref[...] = jnp.where(first_half, _dot(parts, sel_b), kn).astype(BF16)

        x = pp_ref[:, 0:POOL_W]
        pg = pp_ref[:, POOL_W:2 * POOL_W]
        halo = jnp.where(i > 0, halo_ref[:, 0:POOL_W], 0.0)
        buf_ref[0:POOL_HALO, :] = halo
        buf_ref[POOL_HALO:POOL_HALO + ts, :] = x
        acc = x
        snaps = []
        for d in range(1, POOL_HALO):
            acc = acc + buf_ref[pl.ds(POOL_HALO - d, ts), :]
            if d + 1 in POOL_WINDOWS:
                snaps.append(acc)
        lane_group = _iota((1, POOL_W), 1) >> 6
        wsum = _pool_group_select(lane_group, snaps)
        wlen = _pool_group_select(lane_group, [float(w) for w in POOL_WINDOWS])
        tpos = (i * ts + _iota((ts, 1), 0) + 1).astype(F32)
        pooled = wsum / jnp.minimum(tpos, wlen) - x
        pb = pooled.astype(BF16)
        pooled_ref[...] = pb
        yp = _dot(pb, wpd_ref[...])
        yp_ref[...] = yp
        pm_ref[...] = (yp * ps_ref[...] * (pg * _sigmoid(pg))).astype(BF16)

    blk = lambda w, c: pl.BlockSpec((ts, w), lambda i: (i, c))
    full = lambda a: pl.BlockSpec(a.shape, lambda i: (0,) * a.ndim)
    out_shapes = [
        jax.ShapeDtypeStruct((S, FOX_W), BF16), jax.ShapeDtypeStruct((S, FOX_W), BF16), jax.ShapeDtypeStruct((S, FOX_W), BF16),
        jax.ShapeDtypeStruct((S, FOX_W), BF16),
        jax.ShapeDtypeStruct((S, SB_W), BF16), jax.ShapeDtypeStruct((S, SB_W), BF16), jax.ShapeDtypeStruct((S, SB_W), BF16),
        jax.ShapeDtypeStruct((S, POOL_W), BF16), jax.ShapeDtypeStruct((S, POOL_W), F32), jax.ShapeDtypeStruct((S, POOL_W), BF16),
    ]
    out_specs = [
        blk(FOX_W, 0), blk(FOX_W, 0), blk(FOX_W, 0), blk(FOX_W, 0),
        blk(SB_W, 0), blk(SB_W, 0), blk(SB_W, 0),
        blk(POOL_W, 0), blk(POOL_W, 0), blk(POOL_W, 0),
    ]
    return pl.pallas_call(
        body, name="prep", grid=(nb,),
        in_specs=[blk(FOX_W, C_FQ // FOX_W), blk(FOX_W, C_FK // FOX_W), blk(FOX_W, C_FV // FOX_W), blk(2 * POOL_W, C_PX // (2 * POOL_W)),
                  pl.BlockSpec((POOL_HALO, 2 * POOL_W), lambda i: (jnp.maximum(i * hb - 1, 0), C_PX // (2 * POOL_W))),
                  blk(LANES, 0),
                  blk(SB_W, C_SQ // SB_W), blk(SB_W, C_SK // SB_W), blk(SB_W, C_SV // SB_W),
                  full(qg), full(kg), full(bfp), full(wpd), full(ps)],
        out_specs=out_specs, out_shape=out_shapes,
        scratch_shapes=[pltpu.VMEM((1, LANES), F32), pltpu.VMEM((ts, LANES), F32), pltpu.VMEM((ts + POOL_HALO, POOL_W), F32)],
        compiler_params=_cparams(dimension_semantics=("arbitrary",)),
    )(projm, projm, projm, projm, projm, ffo, projm, projm, projm, qg, kg, bfp, wpd, ps)


def _pair_masks(x):
    ma = _iota((1, LANES), 1) < HEAD_DIM
    zero = jnp.zeros_like(x)
    return jnp.where(ma, x, zero), jnp.where(ma, zero, x)


def _aug_queries(q):
    lane = _iota((1, LANES), 1)
    one = jnp.ones_like(q)
    zero = jnp.zeros_like(q)
    qa = jnp.where(lane < HEAD_DIM, q, jnp.where(lane < HEAD_DIM + AUG, one, zero))
    qb = jnp.where(lane >= HEAD_DIM, q, jnp.where(lane < AUG, one, zero))
    return qa, qb


def _fox_fwd(qn, ka, kb, v, projm, *, tq, tk):
    S = qn.shape[0]
    npair = FOX_HEADS // 2

    def body(q_ref, ka_ref, kb_ref, v_ref, fg_ref, o_ref, lse_ref, fm_ref):
        qi = pl.program_id(1)
        lane = _iota((1, LANES), 1)
        ma = lane < HEAD_DIM
        qaug = _aug_queries(q_ref[...])
        k_refs = (ka_ref, kb_ref)

        def block(j, carry, masked):
            k0 = pl.multiple_of(j * tk, tk)
            vb = v_ref[pl.ds(k0, tk), :]
            if masked:
                mask = (k0 + _iota((tq, tk), 1)) <= (qi * tq + _iota((tq, tk), 0))
            scores = [_dot_nt(qaug[h], k_refs[h][pl.ds(k0, tk), :]) for h in range(2)]
            stats, ps = [], []
            for h in range(2):
                m, l, _ = carry[h]
                s = jnp.where(mask, scores[h], NEG) if masked else scores[h]
                m_new = jnp.maximum(m, jnp.max(s, axis=1, keepdims=True))
                alpha = jnp.exp(m - m_new)
                p = jnp.exp(s - m_new)
                stats.append((m_new, alpha * l + jnp.sum(p, axis=1, keepdims=True), alpha))
                ps.append(p.astype(BF16))
            pv = _dot(jnp.concatenate(ps, axis=0), vb)
            return tuple((stats[h][0], stats[h][1], stats[h][2] * carry[h][2] + pv[h * tq:(h + 1) * tq]) for h in range(2))

        init = tuple((jnp.full((tq, 1), NEG, F32), jnp.zeros((tq, 1), F32), jnp.zeros((tq, LANES), F32)) for _ in range(2))
        nfull = (qi * tq) // tk
        carry = lax.fori_loop(0, nfull, lambda j, c: block(j, c, False), init)
        for mi in range(max(1, tq // tk)):
            carry = block(nfull + mi, carry, True)
        (ma_, la, acca), (mb_, lb, accb) = carry
        o = jnp.where(ma, acca / la, accb / lb)
        o_ref[...] = o
        lse_ref[...] = jnp.where(ma, ma_ + jnp.log(la), mb_ + jnp.log(lb))
        fg = fg_ref[...]
        fm_ref[...] = (o * (fg * _sigmoid(fg))).astype(BF16)

    qblk = pl.BlockSpec((tq, LANES), lambda p, i: (i, p))
    kvblk = pl.BlockSpec((S, LANES), lambda p, i: (0, p))
    return pl.pallas_call(
        body, name="fox_fwd", grid=(npair, S // tq),
        in_specs=[qblk, kvblk, kvblk, kvblk,
                  pl.BlockSpec((tq, LANES), lambda p, i: (i, C_FG // LANES + p))],
        out_specs=[qblk, qblk, qblk],
        out_shape=[jax.ShapeDtypeStruct((S, FOX_W), F32), jax.ShapeDtypeStruct((S, FOX_W), F32), jax.ShapeDtypeStruct((S, FOX_W), BF16)],
        compiler_params=_cparams(dimension_semantics=("arbitrary", "arbitrary")),
    )(qn, ka, kb, v, projm)


def _suffix_sums(x, tmat2):
    return _dot(jnp.concatenate(_split2(x), axis=1), tmat2)


def _suffix_matrix(tk, inclusive):
    rr, cc = _iota((2 * tk, tk), 0) & (tk - 1), _iota((2 * tk, tk), 1)
    return _ones_where(rr >= cc) if inclusive else _ones_where(rr > cc)


def _sb_scores(qh, kb, causal, tmat2, r_runs):
    heads = range(2)
    zs = [_dot_nt(qh[h], kb) for h in heads]
    nsps = [jnp.minimum(-z, 0.0) - jnp.log(1.0 + jnp.exp(-jnp.abs(z))) for z in zs]
    lbs = nsps if causal is None else [jnp.where(causal, n, 0.0) for n in nsps]
    rins = [_suffix_sums(lb, tmat2) for lb in lbs]
    args = [zs[h] + lbs[h] + (rins[h] + r_runs[h]) for h in heads]
    a_s = [jnp.exp(arg if causal is None else jnp.where(causal, arg, NEG)) for arg in args]
    return zs, nsps, lbs, a_s


def _sb_fwd(sq, sk, sv, projm, *, tq, tk):
    S = sq.shape[0]
    npair = SB_HEADS // 2

    def body(q_ref, k_ref, v_ref, sg_ref, o_ref, sm_ref):
        qi = pl.program_id(1)
        lane = _iota((1, LANES), 1)
        ma = lane < HEAD_DIM
        qh = _pair_masks(q_ref[...])
        tmat2 = _suffix_matrix(tk, inclusive=False)
        nfull = (qi * tq) // tk

        def block(j, carry, masked):
            k0 = pl.multiple_of(j * tk, tk)
            kb = k_ref[pl.ds(k0, tk), :]
            vb = v_ref[pl.ds(k0, tk), :]
            causal = (k0 + _iota((tq, tk), 1)) < (qi * tq + _iota((tq, tk), 0)) if masked else None
            _, _, lbs, a_s = _sb_scores(qh, kb, causal, tmat2, [carry[h][0] for h in range(2)])
            pv = _dot(jnp.concatenate([a.astype(BF16) for a in a_s], axis=0), vb)
            return tuple((carry[h][0] + jnp.sum(lbs[h], axis=1, keepdims=True), carry[h][1] + pv[h * tq:(h + 1) * tq]) for h in range(2))

        init = tuple((jnp.zeros((tq, 1), F32), jnp.zeros((tq, LANES), F32)) for _ in range(2))
        carry = init
        for mi in reversed(range(max(1, tq // tk))):
            carry = block(nfull + mi, carry, True)
        (_, acca), (_, accb) = lax.fori_loop(0, nfull, lambda jj, c: block(nfull - 1 - jj, c, False), carry)
        o = jnp.where(ma, acca, accb)
        o_ref[...] = o
        sg = sg_ref[...]
        sm_ref[...] = (o * (sg * _sigmoid(sg))).astype(BF16)

    qblk = pl.BlockSpec((tq, LANES), lambda p, i: (i, p))
    kvblk = pl.BlockSpec((S, LANES), lambda p, i: (0, p))
    return pl.pallas_call(
        body, name="sb_fwd", grid=(npair, S // tq),
        in_specs=[qblk, kvblk, kvblk, pl.BlockSpec((tq, LANES), lambda p, i: (i, C_SG // LANES + p))],
        out_specs=[qblk, qblk],
        out_shape=[jax.ShapeDtypeStruct((S, SB_W), F32), jax.ShapeDtypeStruct((S, SB_W), BF16)],
        compiler_params=_cparams(dimension_semantics=("arbitrary", "arbitrary")),
    )(sq, sk, sv, projm)


def _outproj(x, fm, pm, sm, w_out, *, tm):
    S, D = x.shape

    def body(x_ref, fm_ref, pm_ref, sm_ref, w_ref, y_ref):
        y = x_ref[...] + _dot(fm_ref[...], w_ref[0:FOX_W, :])
        y = y + _dot(pm_ref[...], w_ref[FOX_W:FOX_W + POOL_W, :])
        y_ref[...] = y + _dot(sm_ref[...], w_ref[FOX_W + POOL_W:D_MIX, :])

    row = lambda w: pl.BlockSpec((tm, w), lambda i: (i, 0))
    return pl.pallas_call(
        body, name="outproj", grid=(S // tm,),
        in_specs=[row(D), row(FOX_W), row(POOL_W), row(SB_W), pl.BlockSpec((D_MIX, D), lambda i: (0, 0))],
        out_specs=row(D), out_shape=jax.ShapeDtypeStruct((S, D), F32),
        compiler_params=_cparams(dimension_semantics=("arbitrary",)),
    )(x, fm, pm, sm, w_out)


def _loss_head(y, target, *, tm):
    S, D = y.shape

    def body(y_ref, t_ref, dy_ref, sq_ref):
        @pl.when(pl.program_id(0) == 0)
        def _():
            sq_ref[...] = jnp.zeros_like(sq_ref)

        d = y_ref[...] - t_ref[...]
        dy_ref[...] = d * (1.0 / D)
        sq_ref[...] += jnp.sum(d * d, axis=0, keepdims=True)

    row = pl.BlockSpec((tm, D), lambda i: (i, 0))
    return pl.pallas_call(
        body, name="loss_head", grid=(S // tm,),
        in_specs=[row, row], out_specs=[row, pl.BlockSpec((1, D), lambda i: (0, 0))],
        out_shape=[jax.ShapeDtypeStruct((S, D), F32), jax.ShapeDtypeStruct((1, D), F32)],
        compiler_params=_cparams(dimension_semantics=("arbitrary",)),
    )(y, target)


def _outproj_bwd(dy, fm, pm, sm, w_out, layer, stacks, *, tm):
    S, D = dy.shape

    def body(dy_ref, fm_ref, pm_ref, sm_ref, w_ref, dm_ref, dw_ref):
        @pl.when(pl.program_id(0) == 0)
        def _():
            dw_ref[...] = jnp.zeros_like(dw_ref)

        dyb = dy_ref[...].astype(BF16)
        dm_ref[...] = _dot_nt(dyb, w_ref[...])
        dw_ref[0:FOX_W, :] += _dot_tn(fm_ref[...], dyb)
        dw_ref[FOX_W:FOX_W + POOL_W, :] += _dot_tn(pm_ref[...], dyb)
        dw_ref[FOX_W + POOL_W:D_MIX, :] += _dot_tn(sm_ref[...], dyb)

    row = lambda w: pl.BlockSpec((tm, w), lambda i: (i, 0))
    wspec = pl.BlockSpec((D_MIX, D), lambda i: (0, 0))
    return _stack_call(
        body, "outproj_bwd", (S // tm,), [row(D), row(FOX_W), row(POOL_W), row(SB_W), wspec], (dy, fm, pm, sm, w_out),
        [pl.BlockSpec((None, D_MIX, D), lambda i: (layer, 0, 0))], [(D_MIX, D)], stacks,
        plain_specs=[row(D_MIX)], plain_shapes=[jax.ShapeDtypeStruct((S, D_MIX), F32)],
        compiler_params=_cparams(dimension_semantics=("arbitrary",)))


def _fox_bwd(qn, ka, kb, v, o, lse, dmix, projm, *, tq, tk):
    S = qn.shape[0]
    npair = FOX_HEADS // 2

    def body(q_ref, ka_ref, kb_ref, v_ref, o_ref, lse_ref, dm_ref, fg_ref,
             dq_ref, dk_ref, dv_ref, dfg_ref, dct_ref, dcr_ref):
        qi = pl.program_id(1)

        @pl.when(qi == 0)
        def _():
            dk_ref[...] = jnp.zeros_like(dk_ref)
            dv_ref[...] = jnp.zeros_like(dv_ref)
            dct_ref[...] = jnp.zeros_like(dct_ref)

        lane = _iota((1, LANES), 1)
        ma = lane < HEAD_DIM
        qh = _pair_masks(q_ref[...])
        qaug = _aug_queries(q_ref[...])
        k_refs = (ka_ref, kb_ref)
        lsev = lse_ref[...]
        lse = (_lane_pick(lsev, lane, 0), _lane_pick(lsev, lane, HEAD_DIM))
        fg = fg_ref[...]
        silu, dsilu = _silu_pair(fg)
        dm = dm_ref[...]
        ov = o_ref[...]
        do = dm * silu
        dfg_ref[...] = dm * ov * dsilu
        dd = do * ov
        dsum = (jnp.sum(jnp.where(ma, dd, 0.0), axis=1, keepdims=True), jnp.sum(jnp.where(ma, 0.0, dd), axis=1, keepdims=True))
        doh = _pair_masks(do.astype(BF16))
        do2 = jnp.concatenate(doh, axis=0)
        q2 = jnp.concatenate(qh, axis=0)

        def block(j, carry, masked):
            dq = carry[0]
            rows = list(carry[1:])
            k0 = pl.multiple_of(j * tk, tk)
            vb = v_ref[pl.ds(k0, tk), :]
            if masked:
                mask = (k0 + _iota((tq, tk), 1)) <= (qi * tq + _iota((tq, tk), 0))
            heads = range(2)
            kaugs = [k_refs[h][pl.ds(k0, tk), :] for h in heads]
            scores = [_dot_nt(qaug[h], kaugs[h]) for h in heads]
            dps = [_dot_nt(doh[h], vb) for h in heads]
            ps, dss = [], []
            for h in heads:
                s = jnp.where(mask, scores[h], NEG) if masked else scores[h]
                p = jnp.exp(s - lse[h])
                dsf = p * (dps[h] - dsum[h])
                dct_ref[0, h:h + 1, pl.ds(k0, tk)] -= jnp.sum(dsf, axis=0, keepdims=True)
                rows[h] = rows[h] + jnp.sum(dsf, axis=1, keepdims=True)
                ps.append(p.astype(BF16))
                dss.append(dsf.astype(BF16))
            dv_ref[pl.ds(k0, tk), :] += _dot_tn(jnp.concatenate(ps, axis=0), do2)
            dk_ref[pl.ds(k0, tk), :] += _dot_tn(jnp.concatenate(dss, axis=0), q2)
            kh = jnp.concatenate([_pair_masks(kaugs[h])[h] for h in heads], axis=0)
            dq = dq + _dot(jnp.concatenate(dss, axis=1), kh)
            return (dq, rows[0], rows[1])

        zcol = jnp.zeros((tq, 1), F32)
        nfull = (qi * tq) // tk
        carry = lax.fori_loop(0, nfull, lambda j, c: block(j, c, False), (jnp.zeros((tq, LANES), F32), zcol, zcol))
        for mi in range(max(1, tq // tk)):
            carry = block(nfull + mi, carry, True)
        dq, rowa, rowb = carry
        dq_ref[...] = dq * QK_SCALE
        dcr_ref[0] = jnp.where(ma, rowa, rowb)

    qblk = pl.BlockSpec((tq, LANES), lambda p, i: (i, p))
    kvblk = pl.BlockSpec((S, LANES), lambda p, i: (0, p))
    f32out = jax.ShapeDtypeStruct((S, FOX_W), F32)
    ctblk = pl.BlockSpec((1, FF_STRIDE, S), lambda p, i: (p, 0, 0))
    return pl.pallas_call(
        body, name="fox_bwd", grid=(npair, S // tq),
        in_specs=[qblk, kvblk, kvblk, kvblk, qblk, qblk, qblk,
                  pl.BlockSpec((tq, LANES), lambda p, i: (i, C_FG // LANES + p))],
        out_specs=[qblk, kvblk, kvblk, qblk, ctblk, pl.BlockSpec((1, tq, LANES), lambda p, i: (p, i, 0))],
        out_shape=[f32out, f32out, f32out, f32out, jax.ShapeDtypeStruct((npair, FF_STRIDE, S), F32),
                   jax.ShapeDtypeStruct((npair, S, LANES), F32)],
        compiler_params=_cparams(dimension_semantics=("arbitrary", "arbitrary")),
    )(qn, ka, kb, v, o, lse, dmix, projm)


def _sb_bwd(sq, sk, sv, o, dmix, projm, *, tq, tk):
    S = sq.shape[0]
    npair = SB_HEADS // 2
    mix0 = (FOX_W + POOL_W) // LANES

    def body(q_ref, k_ref, v_ref, o_ref, dm_ref, sg_ref, dq_ref, dk_ref, dv_ref, dsg_ref):
        qi = pl.program_id(1)

        @pl.when(qi == 0)
        def _():
            dk_ref[...] = jnp.zeros_like(dk_ref)
            dv_ref[...] = jnp.zeros_like(dv_ref)

        lane = _iota((1, LANES), 1)
        ma = lane < HEAD_DIM
        qh = _pair_masks(q_ref[...])
        sg = sg_ref[...]
        silu, dsilu = _silu_pair(sg)
        dm = dm_ref[...]
        ov = o_ref[...]
        do = dm * silu
        dsg_ref[...] = dm * ov * dsilu
        dob = do.astype(BF16)
        dd = dob.astype(F32) * ov
        dsum = (jnp.sum(jnp.where(ma, dd, 0.0), axis=1, keepdims=True), jnp.sum(jnp.where(ma, 0.0, dd), axis=1, keepdims=True))
        doh = _pair_masks(dob)
        do2 = jnp.concatenate(doh, axis=0)
        q2 = jnp.concatenate(qh, axis=0)
        tmat2 = _suffix_matrix(tk, inclusive=False)
        tmat2_inc = _suffix_matrix(tk, inclusive=True)
        nfull = (qi * tq) // tk

        def block(j, carry, masked):
            dq = carry[2]
            k0 = pl.multiple_of(j * tk, tk)
            kb = k_ref[pl.ds(k0, tk), :]
            vb = v_ref[pl.ds(k0, tk), :]
            kh = _pair_masks(kb)
            causal = (k0 + _iota((tq, tk), 1)) < (qi * tq + _iota((tq, tk), 0)) if masked else None
            heads = range(2)
            das = [_dot_nt(doh[h], vb) for h in heads]
            zs, nsps, lbs, a_s = _sb_scores(qh, kb, causal, tmat2, [carry[h][0] for h in heads])
            abs_ = [a.astype(BF16) for a in a_s]
            us = [abs_[h].astype(F32) * das[h] for h in heads]
            uins = [_suffix_sums(u, tmat2_inc) for u in us]
            dzs = []
            for h in heads:
                cum_u = dsum[h] - (uins[h] + carry[h][1])
                dz = us[h] * jnp.exp(nsps[h]) - jnp.exp(zs[h] + nsps[h]) * cum_u
                if masked:
                    dz = jnp.where(causal, dz, 0.0)
                dzs.append(dz.astype(BF16))
            dv_ref[pl.ds(k0, tk), :] += _dot_tn(jnp.concatenate(abs_, axis=0), do2)
            dk_ref[pl.ds(k0, tk), :] += _dot_tn(jnp.concatenate(dzs, axis=0), q2)
            dq = dq + _dot(jnp.concatenate(dzs, axis=1), jnp.concatenate(kh, axis=0))
            new = [(carry[h][0] + jnp.sum(lbs[h], axis=1, keepdims=True), carry[h][1] + jnp.sum(us[h], axis=1, keepdims=True)) for h in heads]
            return (new[0], new[1], dq)

        zcol = jnp.zeros((tq, 1), F32)
        init = ((zcol, zcol), (zcol, zcol), jnp.zeros((tq, LANES), F32))
        carry = init
        for mi in reversed(range(max(1, tq // tk))):
            carry = block(nfull + mi, carry, True)
        dq = lax.fori_loop(0, nfull, lambda jj, c: block(nfull - 1 - jj, c, False), carry)[2]
        dq_ref[...] = dq * QK_SCALE

    qblk = pl.BlockSpec((tq, LANES), lambda p, i: (i, p))
    kvblk = pl.BlockSpec((S, LANES), lambda p, i: (0, p))
    f32out = jax.ShapeDtypeStruct((S, SB_W), F32)
    return pl.pallas_call(
        body, name="sb_bwd", grid=(npair, S // tq),
        in_specs=[qblk, kvblk, kvblk, qblk,
                  pl.BlockSpec((tq, LANES), lambda p, i: (i, mix0 + p)),
                  pl.BlockSpec((tq, LANES), lambda p, i: (i, C_SG // LANES + p))],
        out_specs=[qblk, kvblk, kvblk, qblk],
        out_shape=[f32out, f32out, f32out, f32out],
        compiler_params=_cparams(dimension_semantics=("arbitrary", "arbitrary")),
    )(sq, sk, sv, o, dmix, projm)


def _prep_bwd(projm, ffo, dqn, dkn, dct, dcr, dv, dfg, dsq, dsk, dsv, dsg, dmix, pooled, yp, qg, kg, bfp, wpd, ps, *, ts):
    S = projm.shape[0]
    nb = S // ts
    hb = ts // POOL_HALO
    npair = FOX_HEADS // 2
    last_halo = S // POOL_HALO - 1

    def body(fq_ref, fk_ref, pp_ref, pph_ref, ff_ref,
             dqn_ref, dkn_ref, dct_ref, dcr_ref, dv_ref, dfg_ref, dsq_ref, dsk_ref, dsv_ref, dsg_ref,
             dmp_ref, dmh_ref, pooled_ref, yp_ref, qg_ref, kg_ref, bf_ref, wpd_ref, ps_ref,
             dp_ref, dqg_ref, dkg_ref, dbf_ref, dwp_ref, dps_ref,
             carry_ref, dl_ref, buf_ref, dct_s):
        i = pl.program_id(0)
        blk = nb - 1 - i

        @pl.when(i == 0)
        def _():
            carry_ref[...] = jnp.zeros_like(carry_ref)
            dqg_ref[...] = jnp.zeros_like(dqg_ref)
            dkg_ref[...] = jnp.zeros_like(dkg_ref)
            dbf_ref[...] = jnp.zeros_like(dbf_ref)
            dwp_ref[...] = jnp.zeros_like(dwp_ref)
            dps_ref[...] = jnp.zeros_like(dps_ref)

        bd = _head_blockdiag(FOX_W)
        for raw_ref, g_ref, dn, dg_ref, col in ((fq_ref, qg_ref, dqn_ref[...], dqg_ref, C_FQ), (fk_ref, kg_ref, dkn_ref[...], dkg_ref, C_FK)):
            q = raw_ref[...]
            rstd = lax.rsqrt(_group_sum(q * q, bd) * (1.0 / HEAD_DIM) + EPS)
            xhat = q * rstd
            dg_ref[...] += jnp.sum(dn * xhat, axis=0, keepdims=True)
            dyg = dn * g_ref[...]
            mean = _group_sum(dyg * xhat, bd) * (1.0 / HEAD_DIM)
            dp_ref[:, col:col + FOX_W] = (rstd * (dyg - xhat * mean)).astype(BF16)
        dp_ref[:, C_FV:C_FV + FOX_W] = dv_ref[...].astype(BF16)
        dp_ref[:, C_FG:C_FG + FOX_W] = dfg_ref[...].astype(BF16)
        dp_ref[:, C_SQ:C_SQ + SB_W] = dsq_ref[...].astype(BF16)
        dp_ref[:, C_SK:C_SK + SB_W] = dsk_ref[...].astype(BF16)
        dp_ref[:, C_SV:C_SV + SB_W] = dsv_ref[...].astype(BF16)
        dp_ref[:, C_SG:C_SG + SB_W] = dsg_ref[...].astype(BF16)

        dct_s[...] = jnp.zeros_like(dct_s)
        for p in range(npair):
            dct_s[FF_STRIDE * p:FF_STRIDE * (p + 1), :] = dct_ref[p]
        dc = dct_s[...].T
        lane = _iota((1, LANES), 1)
        for p in range(npair):
            dcr = dcr_ref[p]
            dc = dc + jnp.where(lane == FF_STRIDE * p, _lane_pick(dcr, lane, 0), 0.0)
            dc = dc + jnp.where(lane == FF_STRIDE * p + 1, _lane_pick(dcr, lane, HEAD_DIM), 0.0)
        triu = _ones_where(_iota((ts, ts), 1) >= _iota((ts, ts), 0))
        dlf = _dot_exact_lhs(triu, dc) + carry_ref[...]
        dl_ref[...] = dlf
        carry_ref[...] = dl_ref[0:1, :]
        z = ff_ref[...] + bf_ref[...]
        dff = dlf * (1.0 / (1.0 + jnp.exp(z)))
        dbf_ref[...] += jnp.sum(dff, axis=0, keepdims=True)
        dp_ref[:, PM:PW] = dff.astype(BF16)

        psv = ps_ref[...]
        wpdv = wpd_ref[...]
        lane_group = _iota((1, POOL_W), 1) >> 6
        wlen = _pool_group_select(lane_group, [float(w) for w in POOL_WINDOWS])
        pg = pp_ref[:, POOL_W:2 * POOL_W]
        silu, dsilu = _silu_pair(pg)
        dmp = dmp_ref[...]
        ypv = yp_ref[...]
        dp_ref[:, C_PG:C_PG + POOL_W] = (dmp * (ypv * psv) * dsilu).astype(BF16)
        dps_ref[...] += jnp.sum(dmp * silu * ypv, axis=0, keepdims=True)
        dyp = (dmp * psv * silu).astype(BF16)
        dwp_ref[...] += _dot_tn(pooled_ref[...], dyp)
        dpooled = _dot_nt(dyp, wpdv)
        pgh = pph_ref[:, POOL_W:2 * POOL_W]
        dyph = (dmh_ref[...] * psv * (pgh * _sigmoid(pgh))).astype(BF16)
        dpooled_h = jnp.where(blk < nb - 1, _dot_nt(dyph, wpdv), 0.0)
        tpos = (blk * ts + _iota((ts, 1), 0) + 1).astype(F32)
        ev = dpooled / jnp.minimum(tpos, wlen)
        buf_ref[0:ts, :] = ev
        buf_ref[ts:ts + POOL_HALO, :] = dpooled_h / wlen
        acc = ev
        snaps = []
        for d in range(1, POOL_HALO):
            acc = acc + buf_ref[pl.ds(d, ts), :]
            if d + 1 in POOL_WINDOWS:
                snaps.append(acc)
        dp_ref[:, C_PX:C_PX + POOL_W] = (_pool_group_select(lane_group, snaps) - dpooled).astype(BF16)

    rblk = lambda w, c: pl.BlockSpec((ts, w), lambda i: (nb - 1 - i, c))
    full = lambda a: pl.BlockSpec(a.shape, lambda i: (0,) * a.ndim)
    halo = lambda w, c: pl.BlockSpec((POOL_HALO, w), lambda i: (jnp.minimum((nb - i) * hb, last_halo), c))
    acc_spec = lambda r, w: pl.BlockSpec((r, w), lambda i: (0, 0))
    return pl.pallas_call(
        body, name="prep_bwd", grid=(nb,),
        in_specs=[rblk(FOX_W, C_FQ // FOX_W), rblk(FOX_W, C_FK // FOX_W), rblk(2 * POOL_W, C_PX // (2 * POOL_W)),
                  halo(2 * POOL_W, C_PX // (2 * POOL_W)), rblk(LANES, 0),
                  rblk(FOX_W, 0), rblk(FOX_W, 0), pl.BlockSpec((npair, FF_STRIDE, ts), lambda i: (0, 0, nb - 1 - i)),
                  pl.BlockSpec((npair, ts, LANES), lambda i: (0, nb - 1 - i, 0)), rblk(FOX_W, 0), rblk(FOX_W, 0),
                  rblk(SB_W, 0), rblk(SB_W, 0), rblk(SB_W, 0), rblk(SB_W, 0),
                  rblk(POOL_W, FOX_W // POOL_W), halo(POOL_W, FOX_W // POOL_W), rblk(POOL_W, 0), rblk(POOL_W, 0),
                  full(qg), full(kg), full(bfp), full(wpd), full(ps)],
        out_specs=[rblk(PW, 0), acc_spec(1, FOX_W), acc_spec(1, FOX_W), acc_spec(1, LANES), acc_spec(POOL_W, POOL_W), acc_spec(1, POOL_W)],
        out_shape=[jax.ShapeDtypeStruct((S, PW), BF16), jax.ShapeDtypeStruct((1, FOX_W), F32), jax.ShapeDtypeStruct((1, FOX_W), F32),
                   jax.ShapeDtypeStruct((1, LANES), F32), jax.ShapeDtypeStruct((POOL_W, POOL_W), F32), jax.ShapeDtypeStruct((1, POOL_W), F32)],
        scratch_shapes=[pltpu.VMEM((1, LANES), F32), pltpu.VMEM((ts, LANES), F32), pltpu.VMEM((ts + POOL_HALO, POOL_W), F32),
                        pltpu.VMEM((LANES, ts), F32)],
        compiler_params=_cparams(dimension_semantics=("arbitrary",)),
    )(projm, projm, projm, projm, ffo, dqn, dkn, dct, dcr, dv, dfg, dsq, dsk, dsv, dsg, dmix, dmix, pooled, yp, qg, kg, bfp, wpd, ps)


def _stack_call(body, name, grid, in_specs, operands, slot_specs, slot_shapes, stacks, plain_specs=(), plain_shapes=(), **kw):
    out_specs = list(plain_specs) + list(slot_specs)
    out_shape = list(plain_shapes) + [jax.ShapeDtypeStruct((DEPTH,) + s, F32) for s in slot_shapes]
    if stacks is None:
        return pl.pallas_call(body, name=name, grid=grid, in_specs=in_specs, out_specs=out_specs, out_shape=out_shape, **kw)(*operands)
    n = len(operands)

    def aliased_body(*refs):
        body(*refs[:n], *refs[n + len(stacks):])

    return pl.pallas_call(
        aliased_body, name=name, grid=grid, in_specs=list(in_specs) + [pl.BlockSpec(memory_space=pl.ANY)] * len(stacks),
        out_specs=out_specs, out_shape=out_shape,
        input_output_aliases={n + k: len(plain_specs) + k for k in range(len(stacks))}, **kw)(*operands, *stacks)


def _inproj_dw(h, dproj, layer, stacks, *, ts, tn):
    S, D = h.shape
    nj = PM // tn

    def body(h_ref, dp_ref, dpf_ref, dw_ref, dwf_ref):
        s = pl.program_id(1)

        @pl.when(s == 0)
        def _():
            dw_ref[...] = jnp.zeros_like(dw_ref)

        @pl.when((s == 0) & (pl.program_id(0) == 0))
        def _():
            dwf_ref[...] = jnp.zeros_like(dwf_ref)

        hv = h_ref[...]
        dw_ref[...] += _dot_tn(hv, dp_ref[...])

        @pl.when(pl.program_id(0) == 0)
        def _():
            dwf_ref[...] += _dot_tn(hv, dpf_ref[...])

    return _stack_call(
        body, "inproj_dw", (nj, S // ts),
        [pl.BlockSpec((ts, D), lambda j, s: (s, 0)),
         pl.BlockSpec((ts, tn), lambda j, s: (s, j)),
         pl.BlockSpec((ts, LANES), lambda j, s: (s, PM // LANES))],
        (h, dproj, dproj),
        [pl.BlockSpec((None, D, tn), lambda j, s: (layer, 0, j)), pl.BlockSpec((None, D, LANES), lambda j, s: (layer, 0, 0))],
        [(D, PM), (D, LANES)], stacks,
        compiler_params=_cparams(dimension_semantics=("arbitrary", "arbitrary")))


def _inproj_dx(dproj, w_all, x, g, dy, *, tm):
    S, D = x.shape

    def body(dp_ref, w_ref, x_ref, g_ref, dy_ref, dx_ref, dg_ref):
        @pl.when(pl.program_id(0) == 0)
        def _():
            dg_ref[...] = jnp.zeros_like(dg_ref)

        dh = _dot_nt(dp_ref[...], w_ref[...])
        xf = x_ref[...]
        rstd = lax.rsqrt(jnp.mean(xf * xf, axis=-1, keepdims=True) + EPS)
        xhat = xf * rstd
        dg_ref[...] += jnp.sum(dh * xhat, axis=0, keepdims=True)
        dyg = dh * g_ref[...]
        mean = jnp.mean(dyg * xhat, axis=-1, keepdims=True)
        dx_ref[...] = rstd * (dyg - xhat * mean) + dy_ref[...]

    row = lambda w: pl.BlockSpec((tm, w), lambda i: (i, 0))
    return pl.pallas_call(
        body, name="inproj_dx", grid=(S // tm,),
        in_specs=[row(PW), pl.BlockSpec((D, PW), lambda i: (0, 0)), row(D), pl.BlockSpec((1, D), lambda i: (0, 0)), row(D)],
        out_specs=[row(D), pl.BlockSpec((1, D), lambda i: (0, 0))],
        out_shape=[jax.ShapeDtypeStruct((S, D), F32), jax.ShapeDtypeStruct((1, D), F32)],
        compiler_params=_cparams(dimension_semantics=("arbitrary",)),
    )(dproj, w_all, x, g, dy)


def _adamw(w, g, m, v):
    L, R, C = w.shape
    tr = R if R <= 512 else 256

    def body(w_ref, g_ref, m_ref, v_ref, d_ref, nm_ref, nv_ref):
        gv = g_ref[...]
        nm = ADAM_B1 * m_ref[...] + (1.0 - ADAM_B1) * gv
        nv = ADAM_B2 * v_ref[...] + (1.0 - ADAM_B2) * (gv * gv)
        m_hat = nm / (1.0 - ADAM_B1 ** ADAM_STEP)
        v_hat = nv / (1.0 - ADAM_B2 ** ADAM_STEP)
        d_ref[...] = -ADAM_LR * (m_hat / (jnp.sqrt(v_hat) + ADAM_EPS) + ADAM_WD * w_ref[...])
        nm_ref[...] = nm
        nv_ref[...] = nv

    spec = pl.BlockSpec((1, tr, C), lambda l, i: (l, i, 0))
    shp = jax.ShapeDtypeStruct((L, R, C), F32)
    return pl.pallas_call(
        body, name="adamw", grid=(L, R // tr), in_specs=[spec] * 4, out_specs=[spec] * 3, out_shape=[shp] * 3,
        compiler_params=_cparams(dimension_semantics=("arbitrary", "arbitrary")),
    )(w, g, m, v)


def _adamw_nd(w, g, m, v):
    shape = w.shape
    view = (1,) + shape if w.ndim == 2 else (shape[0], -1, shape[-1])
    outs = _adamw(w.reshape(view), g.reshape(view), m.reshape(view), v.reshape(view))
    return tuple(o.reshape(shape) for o in outs)


FLIP_C = (0, 0, 1)
FLIP_X = (1, 0, 0)
FLIP_Y = (0, 1, 0)
FLIP_XY = (1, 1, 0)
MESH = pl.DeviceIdType.MESH


def _peer(flip):
    me = (lax.axis_index("x"), lax.axis_index("y"), lax.axis_index("c"))
    return tuple(1 - a if f else a for a, f in zip(me, flip))


def _exchange(name, arrays, flips):
    n = len(arrays)

    def body(*refs):
        srcs, dsts = refs[:n], refs[n:2 * n]
        send_sems, recv_sems = refs[2 * n:]
        copies = [pltpu.make_async_remote_copy(src_ref=srcs[k], dst_ref=dsts[k], send_sem=send_sems.at[k], recv_sem=recv_sems.at[k],
                                               device_id=_peer(flips[k]), device_id_type=MESH) for k in range(n)]
        for cp in copies:
            cp.start()
        for cp in copies:
            cp.wait()

    anyspec = pl.BlockSpec(memory_space=pl.ANY)
    return pl.pallas_call(
        body, name=name, in_specs=[anyspec] * n, out_specs=[anyspec] * n,
        out_shape=[jax.ShapeDtypeStruct(a.shape, a.dtype) for a in arrays],
        scratch_shapes=[pltpu.SemaphoreType.DMA((n,)), pltpu.SemaphoreType.DMA((n,))],
    )(*arrays)


def _exchange_add(name, x, flip):
    def body(x_ref, o_ref, buf_ref, send_sem, recv_sem):
        cp = pltpu.make_async_remote_copy(src_ref=x_ref, dst_ref=buf_ref, send_sem=send_sem, recv_sem=recv_sem,
                                          device_id=_peer(flip), device_id_type=MESH)
        cp.start()
        cp.wait()
        o_ref[...] = x_ref[...] + buf_ref[...]

    vspec = pl.BlockSpec(memory_space=pltpu.VMEM)
    return pl.pallas_call(
        body, name=name, in_specs=[vspec], out_specs=vspec, out_shape=jax.ShapeDtypeStruct(x.shape, x.dtype),
        scratch_shapes=[pltpu.VMEM(x.shape, x.dtype), pltpu.SemaphoreType.DMA, pltpu.SemaphoreType.DMA],
    )(x)


def _chip_index():
    return 2 * lax.axis_index("x") + lax.axis_index("y")


def _gather_weights(w_in, w_out):
    half = DEPTH // 2
    wi = w_in.astype(BF16)
    wo = w_out.astype(BF16)
    masks = (2, 1, 3)
    flips = (FLIP_X, FLIP_Y, FLIP_XY)
    n_first = 2 * len(masks)

    def body(wi_ref, wo_ref, gi_ref, go_ref, send_sems, recv_sems):
        c = lax.axis_index("c")
        j = _chip_index()
        mine = pl.ds(half * c, half)
        theirs = pl.ds(half * (1 - c), half)
        srcs = (wi_ref, wo_ref)
        dsts = (gi_ref, go_ref)

        def copy(idx, src, dst, flip):
            return pltpu.make_async_remote_copy(src_ref=src, dst_ref=dst, send_sem=send_sems.at[idx], recv_sem=recv_sems.at[idx],
                                                device_id=_peer(flip), device_id_type=MESH)

        first = [copy(2 * k + a, srcs[a].at[mine], dsts[a].at[j, mine], flips[k]) for k in range(len(masks)) for a in range(2)]
        for cp in first:
            cp.start()
        passed = []
        for k, m in enumerate(masks):
            for a in range(2):
                slot = dsts[a].at[j ^ m, mine]
                copy(2 * k + a, slot, slot, flips[k]).wait_recv()
                fwd = copy(n_first + 2 * k + a, slot, slot, FLIP_C)
                fwd.start()
                passed.append(fwd)
        for k, m in enumerate(masks):
            for a in range(2):
                slot = dsts[a].at[j ^ m, theirs]
                copy(n_first + 2 * k + a, slot, slot, FLIP_C).wait_recv()
        for cp in first + passed:
            cp.wait_send()

    anyspec = pl.BlockSpec(memory_space=pl.ANY)
    gi, go = pl.pallas_call(
        body, name="gather_weights", in_specs=[anyspec] * 2, out_specs=[anyspec] * 2,
        out_shape=[jax.ShapeDtypeStruct((4,) + wi.shape, BF16), jax.ShapeDtypeStruct((4,) + wo.shape, BF16)],
        scratch_shapes=[pltpu.SemaphoreType.DMA((2 * n_first,)), pltpu.SemaphoreType.DMA((2 * n_first,))],
    )(wi, wo)
    own = lax.broadcasted_iota(jnp.int32, (4, 1, 1, 1), 0) == _chip_index()
    gi = jnp.where(own, wi[None], gi)
    go = jnp.where(own, wo[None], go)
    w_in_full = jnp.transpose(gi, (1, 2, 0, 3)).reshape(DEPTH, wi.shape[1], 4 * wi.shape[2])
    w_out_full = jnp.transpose(go, (1, 0, 2, 3)).reshape(DEPTH, 4 * wo.shape[1], wo.shape[2])
    return w_in_full, w_out_full


def _to_aligned(w_in_full):
    L, D, _ = w_in_full.shape
    npair = FOX_HEADS // 2
    ff = w_in_full[..., ORIG_FF:ORIG_REST].reshape(L, D, npair, 2)
    ff = jnp.pad(ff, ((0, 0), (0, 0), (0, 0), (0, FF_STRIDE - 2))).reshape(L, D, npair * FF_STRIDE)
    ff = jnp.pad(ff, ((0, 0), (0, 0), (0, LANES - npair * FF_STRIDE)))
    return jnp.concatenate([w_in_full[..., :ORIG_FOX], w_in_full[..., ORIG_REST:], ff], axis=-1)


def _from_aligned(dw_all):
    L, D, _ = dw_all.shape
    npair = FOX_HEADS // 2
    ff = dw_all[..., PM:PM + npair * FF_STRIDE].reshape(L, D, npair, FF_STRIDE)[..., :2].reshape(L, D, FOX_HEADS)
    return jnp.concatenate([dw_all[..., :ORIG_FOX], ff, dw_all[..., ORIG_FOX:PM]], axis=-1)


def _half_layers(name, stack, got):
    L, R, C = stack.shape
    half = L // 2
    tr = min(256, R)
    c = lax.axis_index("c")
    which = ((1 - c) if got is None else c).astype(jnp.int32).reshape(1)

    def body(c_ref, x_ref, *refs):
        if got is None:
            refs[0][...] = x_ref[...].astype(BF16)
        else:
            acc = x_ref[...] + refs[0][...].astype(F32)
            refs[1][...] = acc
            refs[2][...] = acc.astype(BF16)

    plain = pl.BlockSpec((1, tr, C), lambda l, i, c_ref: (l, i, 0))
    picked = pl.BlockSpec((1, tr, C), lambda l, i, c_ref: (c_ref[0] * half + l, i, 0))
    shp = lambda dt: jax.ShapeDtypeStruct((half, R, C), dt)
    grid_spec = pltpu.PrefetchScalarGridSpec(
        num_scalar_prefetch=1, grid=(half, R // tr),
        in_specs=[picked] + ([] if got is None else [plain]), out_specs=[plain] if got is None else [plain, plain])
    return pl.pallas_call(
        body, name=name, grid_spec=grid_spec, out_shape=[shp(BF16)] if got is None else [shp(F32), shp(BF16)],
        compiler_params=_cparams(dimension_semantics=("arbitrary", "arbitrary")),
    )(which, stack, *([] if got is None else [got]))


def _reduce_scatter(stack_m, stack_f, stack_o, shard_cols, shard_rows):
    j = _chip_index()
    half = DEPTH // 2
    stacks = (stack_m, stack_f, stack_o)
    give = [_half_layers("rs_give", s, None)[0] for s in stacks]
    got = _exchange("rs_d2d", give, (FLIP_C,) * len(stacks))
    (m32, mbf), (f32_, fbf), (o32, obf) = [_half_layers("rs_add_chip", s, g) for s, g in zip(stacks, got)]
    d_model = stack_m.shape[1]

    def in_shards(m, f):
        full = _from_aligned(jnp.concatenate([m, f], axis=-1))
        return jnp.moveaxis(full.reshape(half, d_model, 4, shard_cols), 2, 0)

    def out_shards(o):
        return jnp.moveaxis(o.reshape(half, 4, shard_rows, o.shape[-1]), 1, 0)

    chip = [(in_shards(m32, f32_), in_shards(mbf, fbf)), (out_shards(o32), out_shards(obf))]
    masks = (2, 1, 3)
    flips = (FLIP_X, FLIP_Y, FLIP_XY)
    sends, sflips = [], []
    for _, bf in chip:
        for m, fl in zip(masks, flips):
            sends.append(lax.dynamic_index_in_dim(bf, j ^ m, axis=0, keepdims=False))
            sflips.append(fl)
    got = _exchange("rs_ici", sends, tuple(sflips))
    mine = []
    for a, (f32_sum, _) in enumerate(chip):
        own = lax.dynamic_index_in_dim(f32_sum, j, axis=0, keepdims=False)
        mine.append(_add_into_half("rs_add_all", own, list(got[3 * a:3 * a + 3])))
    return _share_halves(mine)


def _add_into_half(name, first, others):
    half, rows, C = first.shape
    tr = min(256, rows)
    n = len(others)

    def body(c_ref, *refs):
        acc = refs[0][...]
        for r in refs[1:1 + n]:
            acc = acc + r[...].astype(F32)
        refs[1 + n][...] = acc

    grid_spec = pltpu.PrefetchScalarGridSpec(
        num_scalar_prefetch=1, grid=(half, rows // tr),
        in_specs=[pl.BlockSpec((1, tr, C), lambda l, i, c_ref: (l, i, 0))] * (1 + n),
        out_specs=pl.BlockSpec((1, tr, C), lambda l, i, c_ref: (c_ref[0] * half + l, i, 0)))
    return pl.pallas_call(
        body, name=name, grid_spec=grid_spec, out_shape=jax.ShapeDtypeStruct((2 * half, rows, C), F32),
        compiler_params=_cparams(dimension_semantics=("arbitrary", "arbitrary")),
    )(lax.axis_index("c").astype(jnp.int32).reshape(1), first, *others)


def _share_halves(bufs):
    n = len(bufs)
    half = DEPTH // 2

    def body(*refs):
        outs = refs[n:2 * n]
        send_sems, recv_sems = refs[2 * n:]
        lay = pl.ds(half * lax.axis_index("c"), half)
        copies = [pltpu.make_async_remote_copy(src_ref=outs[k].at[lay], dst_ref=outs[k].at[lay], send_sem=send_sems.at[k],
                                               recv_sem=recv_sems.at[k], device_id=_peer(FLIP_C), device_id_type=MESH) for k in range(n)]
        for cp in copies:
            cp.start()
        for cp in copies:
            cp.wait()

    anyspec = pl.BlockSpec(memory_space=pl.ANY)
    return pl.pallas_call(
        body, name="rs_share", in_specs=[anyspec] * n, out_specs=[anyspec] * n,
        out_shape=[jax.ShapeDtypeStruct(a.shape, a.dtype) for a in bufs],
        input_output_aliases={k: k for k in range(n)},
        scratch_shapes=[pltpu.SemaphoreType.DMA((n,)), pltpu.SemaphoreType.DMA((n,))],
    )(*bufs)


def _all_reduce_small(x):
    x = _exchange_add("ar_c", x, FLIP_C)
    x = _exchange_add("ar_y", x, FLIP_Y)
    return _exchange_add("ar_x", x, FLIP_X)


def _blocks(S):
    return dict(tm=min(512, S), ts=min(512, S), tq=min(512, S), tk=min(512, S), tks=min(256, S))


def _pair_pad(vec):
    npair = FOX_HEADS // 2
    v = jnp.pad(vec.reshape(npair, 2), ((0, 0), (0, FF_STRIDE - 2))).reshape(1, npair * FF_STRIDE)
    return jnp.pad(v, ((0, 0), (0, LANES - npair * FF_STRIDE)))


def _pair_unpad(row):
    npair = FOX_HEADS // 2
    return row[0, :npair * FF_STRIDE].reshape(npair, FF_STRIDE)[:, :2].reshape(FOX_HEADS)


def _pool_blockdiag(w_pool):
    g, cg, _ = w_pool.shape
    eye = jnp.eye(g, dtype=w_pool.dtype)
    return jnp.einsum("gh,gcd->gchd", eye, w_pool).reshape(g * cg, g * cg)


def _layer_params(norm_g, b_f, q_norm_g, k_norm_g, w_pool, pool_scale):
    return dict(g=norm_g.reshape(1, -1), qg=jnp.tile(q_norm_g, FOX_HEADS).reshape(1, FOX_W), kg=jnp.tile(k_norm_g, FOX_HEADS).reshape(1, FOX_W),
                bfp=_pair_pad(b_f), wpd=_pool_blockdiag(w_pool).astype(BF16), ps=pool_scale.reshape(1, POOL_W))


def _layer_fwd(x, w_all, w_out, prm, bs):
    projm, ffo, h = _inproj(x, prm["g"], w_all, tm=bs["tm"], tn=512)
    qn, ka, kb, v, sq, sk, sv, pooled, yp, pm = _prep(projm, ffo, prm["qg"], prm["kg"], prm["bfp"], prm["wpd"], prm["ps"], ts=bs["ts"])
    o, lse, fm = _fox_fwd(qn, ka, kb, v, projm, tq=bs["tq"], tk=bs["tk"])
    so, sm = _sb_fwd(sq, sk, sv, projm, tq=bs["tq"], tk=bs["tks"])
    y = _outproj(x, fm, pm, sm, w_out, tm=bs["tm"])
    saved = dict(x=x, projm=projm, ffo=ffo, h=h, qn=qn, ka=ka, kb=kb, v=v, sq=sq, sk=sk, sv=sv, pooled=pooled, yp=yp,
                 o=o, lse=lse, so=so, fm=fm, pm=pm, sm=sm)
    return y, saved


def _layer_bwd(dy, w_all, w_out, prm, sv_, bs, layer, stacks):
    dmix, stack_o = _outproj_bwd(dy, sv_["fm"], sv_["pm"], sv_["sm"], w_out, layer, None if stacks is None else stacks[2:], tm=bs["tm"])
    dqn, dkn, dv, dfg, dct, dcr = _fox_bwd(sv_["qn"], sv_["ka"], sv_["kb"], sv_["v"], sv_["o"], sv_["lse"], dmix, sv_["projm"],
                                      tq=bs["tq"], tk=bs["tk"])
    dsq, dsk, dsv, dsg = _sb_bwd(sv_["sq"], sv_["sk"], sv_["sv"], sv_["so"], dmix, sv_["projm"], tq=bs["tq"], tk=bs["tks"])
    dproj, dqg, dkg, dbf, dwp, dps = _prep_bwd(sv_["projm"], sv_["ffo"], dqn, dkn, dct, dcr, dv, dfg, dsq, dsk, dsv, dsg, dmix,
                                               sv_["pooled"], sv_["yp"], prm["qg"], prm["kg"], prm["bfp"], prm["wpd"], prm["ps"], ts=bs["ts"])
    stack_m, stack_f = _inproj_dw(sv_["h"], dproj, layer, None if stacks is None else stacks[:2], ts=bs["ts"], tn=512)
    dx, dg = _inproj_dx(dproj, w_all, sv_["x"], prm["g"], dy, tm=min(256, bs["tm"]))
    grads = dict(
        norm_g=dg[0],
        b_f=_pair_unpad(dbf), q_norm_g=dqg.reshape(FOX_HEADS, HEAD_DIM).sum(0), k_norm_g=dkg.reshape(FOX_HEADS, HEAD_DIM).sum(0),
        w_pool=jnp.stack([dwp[HEAD_DIM * g:HEAD_DIM * (g + 1), HEAD_DIM * g:HEAD_DIM * (g + 1)] for g in range(4)]),
        pool_scale=dps[0])
    return dx, grads, (stack_m, stack_f, stack_o)


def _local_step(x, target, w_all, w_out, norm_g, b_f, q_norm_g, k_norm_g, w_pool, pool_scale):
    S, D = x.shape
    bs = _blocks(S)
    prms = [_layer_params(norm_g[l], b_f[l], q_norm_g[l], k_norm_g[l], w_pool[l], pool_scale[l]) for l in range(DEPTH)]
    saved = []
    y = x
    for l in range(DEPTH):
        y, s_ = _layer_fwd(y, w_all[l], w_out[l], prms[l], bs)
        saved.append(s_)
    dy, sq = _loss_head(y, target, tm=bs["tm"])
    loss = 0.5 * jnp.sum(sq) / D
    grads = [None] * DEPTH
    stacks = None
    for l in reversed(range(DEPTH)):
        dy, grads[l], stacks = _layer_bwd(dy, w_all[l], w_out[l], prms[l], saved[l], bs, l, stacks)
    stacked = {k: jnp.stack([g[k] for g in grads]) for k in grads[0]}
    return loss, dy, stacked, stacks


SMALL = ("norm_g", "b_f", "q_norm_g", "k_norm_g", "w_pool", "pool_scale")


def _pack_small(gr):
    flat = jnp.concatenate([gr[k].reshape(-1) for k in SMALL])
    pad = (-flat.shape[0]) % (8 * LANES)
    return jnp.pad(flat, (0, pad)).reshape(-1, LANES)


def _unpack_small(packed, like):
    flat = packed.reshape(-1)
    out, off = {}, 0
    for k in SMALL:
        n = like[k].size
        out[k] = flat[off:off + n].reshape(like[k].shape)
        off += n
    return out


def kernel(x, norm_g, w_in, b_f, q_norm_g, k_norm_g, w_pool, pool_scale, w_out, loss_target, m_norm_g, m_w_in, m_b_f, m_q_norm_g, m_k_norm_g, m_w_pool, m_pool_scale, m_w_out, v_norm_g, v_w_in, v_b_f, v_q_norm_g, v_k_norm_g, v_w_pool, v_pool_scale, v_w_out):
    weights = dict(norm_g=norm_g, w_in=w_in, b_f=b_f, q_norm_g=q_norm_g, k_norm_g=k_norm_g, w_pool=w_pool, pool_scale=pool_scale, w_out=w_out)
    mom_m = dict(norm_g=m_norm_g, w_in=m_w_in, b_f=m_b_f, q_norm_g=m_q_norm_g, k_norm_g=m_k_norm_g, w_pool=m_w_pool, pool_scale=m_pool_scale, w_out=m_w_out)
    mom_v = dict(norm_g=v_norm_g, w_in=v_w_in, b_f=v_b_f, q_norm_g=v_q_norm_g, k_norm_g=v_k_norm_g, w_pool=v_w_pool, pool_scale=v_pool_scale, w_out=v_w_out)
    shard_cols = w_in.shape[2]
    shard_rows = w_out.shape[1]

    w_in_full, w_out_full = _gather_weights(w_in, w_out)
    w_all = _to_aligned(w_in_full)
    loss, dx, gr, stacks = _local_step(x[0], loss_target[0], w_all, w_out_full, norm_g, b_f, q_norm_g, k_norm_g, w_pool, pool_scale)
    loss = lax.psum(loss, ("x", "y", "c"))

    g_w_in, g_w_out = _reduce_scatter(*stacks, shard_cols, shard_rows)
    small = _unpack_small(_all_reduce_small(_pack_small(gr)), {k: weights[k] for k in SMALL})
    grad_w = dict(small, w_in=g_w_in, w_out=g_w_out)

    names = ("norm_g", "w_in", "b_f", "q_norm_g", "k_norm_g", "w_pool", "pool_scale", "w_out")
    upd = {k: _adamw_nd(weights[k], grad_w[k], mom_m[k], mom_v[k]) for k in names}
    return (loss, dx[None], *[grad_w[k] for k in names], *[upd[k][0] for k in names], *[upd[k][1] for k in names], *[upd[k][2] for k in names])
```

```python
import functools

import jax
import jax.numpy as jnp
from jax import lax
from jax.experimental import pallas as pl
from jax.experimental.pallas import tpu as pltpu

F32 = jnp.float32
BF16 = jnp.bfloat16

DEPTH = 4
HEAD_DIM = 64
FOX_HEADS = 8
SB_HEADS = 4
FOX_W = FOX_HEADS * HEAD_DIM
SB_W = SB_HEADS * HEAD_DIM
POOL_W = 256
POOL_WINDOWS = (2, 4, 8, 16)
POOL_HALO = 16
D_MIX = FOX_W + POOL_W + SB_W
EPS = 1e-6
NEG = -1e30
QK_SCALE = HEAD_DIM ** -0.5

ORIG_FOX = 4 * FOX_W
ORIG_FF = ORIG_FOX
ORIG_REST = ORIG_FF + FOX_HEADS
D_IN = ORIG_REST + 2 * POOL_W + 4 * SB_W

C_FQ, C_FK, C_FV, C_FG = 0, FOX_W, 2 * FOX_W, 3 * FOX_W
C_PX = 4 * FOX_W
C_PG = C_PX + POOL_W
C_SQ = C_PG + POOL_W
C_SK, C_SV, C_SG = C_SQ + SB_W, C_SQ + 2 * SB_W, C_SQ + 3 * SB_W
PM = C_SG + SB_W
LANES = 128
PW = PM + LANES
FF_STRIDE = 8
AUG = 3

ADAM_LR = 0.001
ADAM_B1 = 0.9
ADAM_B2 = 0.999
ADAM_EPS = 1e-08
ADAM_WD = 0.01
ADAM_STEP = 10

VMEM_LIMIT = 48 * 1024 * 1024


def _cparams(**kw):
    return pltpu.CompilerParams(vmem_limit_bytes=VMEM_LIMIT, **kw)


def _dot(a, b):
    return jnp.dot(a, b, preferred_element_type=F32)


def _dot_nt(a, b):
    return lax.dot_general(a, b, (((1,), (1,)), ((), ())), preferred_element_type=F32)


def _dot_tn(a, b):
    return lax.dot_general(a, b, (((0,), (0,)), ((), ())), preferred_element_type=F32)


def _split2(x):
    hi = x.astype(BF16)
    lo = (x - hi.astype(F32)).astype(BF16)
    return hi, lo


def _split3(x):
    hi = x.astype(BF16)
    r = x - hi.astype(F32)
    mid = r.astype(BF16)
    lo = (r - mid.astype(F32)).astype(BF16)
    return hi, mid, lo


def _dot_exact_rhs(x, m):
    hi, mid, lo = _split3(x)
    return _dot(hi, m) + _dot(mid, m) + _dot(lo, m)


def _dot_exact_lhs(m, x):
    hi, mid, lo = _split3(x)
    return _dot(m, hi) + _dot(m, mid) + _dot(m, lo)


def _sigmoid(x):
    return 1.0 / (1.0 + jnp.exp(-x))


def _silu_pair(x):
    s = _sigmoid(x)
    return x * s, s * (1.0 + x * (1.0 - s))


def _iota(shape, dim):
    return lax.broadcasted_iota(jnp.int32, shape, dim)


def _ones_where(cond):
    return jnp.where(cond, 1.0, 0.0).astype(BF16)


def _head_blockdiag(w):
    return _ones_where((_iota((w, w), 0) >> 6) == (_iota((w, w), 1) >> 6))


def _group_sum(x, bd):
    hi, lo = _split2(x)
    return _dot(hi, bd) + _dot(lo, bd)


def _lane_pick(x, lane_idx, lane):
    return jnp.sum(jnp.where(lane_idx == lane, x, 0.0), axis=1, keepdims=True)


def _inproj(x, g, wt_all, layer, *, tm, tn):
    S, D = x.shape
    nj = PM // tn

    def body(x_ref, g_ref, w_ref, wff_ref, proj_ref, ff_ref, h_ref):
        @pl.when(pl.program_id(1) == 0)
        def _():
            xf = x_ref[...]
            ms = jnp.mean(xf * xf, axis=-1, keepdims=True)
            h = (xf * lax.rsqrt(ms + EPS) * g_ref[...]).astype(BF16)
            h_ref[...] = h
            ff_ref[...] = _dot_nt(h, wff_ref[...])

        proj_ref[...] = _dot_nt(h_ref[...], w_ref[...])

    return pl.pallas_call(
        body, name="inproj", grid=(S // tm, nj),
        in_specs=[pl.BlockSpec((tm, D), lambda i, j: (i, 0)),
                  pl.BlockSpec((1, D), lambda i, j: (0, 0)),
                  pl.BlockSpec((None, tn, D), lambda i, j: (layer, j, 0)),
                  pl.BlockSpec((None, LANES, D), lambda i, j: (layer, PM // LANES, 0))],
        out_specs=[pl.BlockSpec((tm, tn), lambda i, j: (i, j)),
                   pl.BlockSpec((tm, LANES), lambda i, j: (i, 0)),
                   pl.BlockSpec((tm, D), lambda i, j: (i, 0))],
        out_shape=[jax.ShapeDtypeStruct((S, PM), F32), jax.ShapeDtypeStruct((S, LANES), F32),
                   jax.ShapeDtypeStruct((S, D), BF16)],
        compiler_params=_cparams(dimension_semantics=("arbitrary", "arbitrary")),
    )(x, g, wt_all, wt_all)


def _pool_group_select(lane_group, vals):
    return jnp.where(lane_group == 0, vals[0], jnp.where(lane_group == 1, vals[1], jnp.where(lane_group == 2, vals[2], vals[3])))


def _prep(projm, ffo, qg, kg, bfp, wpd, ps, *, ts):
    S = projm.shape[0]
    nb = S // ts
    hb = ts // POOL_HALO

    def body(fq_ref, fk_ref, fv_ref, pp_ref, halo_ref, ff_ref, sq_ref, sk_ref, sv_ref,
             qg_ref, kg_ref, bf_ref, wpd_ref, ps_ref,
             qn_ref, ka_ref, kb_ref, v_ref, sqo_ref, sko_ref, svo_ref, pooled_ref, yp_ref, pm_ref,
             carry_ref, c_ref, buf_ref):
        i = pl.program_id(0)
        bd = _head_blockdiag(FOX_W)
        normed = []
        for src, g_ref in ((fq_ref, qg_ref), (fk_ref, kg_ref)):
            q = src[...]
            ss = _group_sum(q * q, bd)
            normed.append(q * lax.rsqrt(ss * (1.0 / HEAD_DIM) + EPS) * g_ref[...])
        qn_ref[...] = (normed[0] * QK_SCALE).astype(BF16)
        kn = normed[1]
        v_ref[...] = fv_ref[...].astype(BF16)
        sqo_ref[...] = (sq_ref[...] * QK_SCALE).astype(BF16)
        sko_ref[...] = sk_ref[...].astype(BF16)
        svo_ref[...] = sv_ref[...].astype(BF16)

        @pl.when(i == 0)
        def _():
            carry_ref[...] = jnp.zeros_like(carry_ref)

        z = ff_ref[...] + bf_ref[...]
        lf = jnp.minimum(z, 0.0) - jnp.log(1.0 + jnp.exp(-jnp.abs(z)))
        tri = _ones_where(_iota((ts, ts), 1) <= _iota((ts, ts), 0))
        c = _dot_exact_lhs(tri, lf) + carry_ref[...]
        c_ref[...] = c
        carry_ref[...] = c_ref[ts - 1:ts, :]
        parts = jnp.concatenate(_split3(-c), axis=1)
        row = _iota((AUG * LANES, FOX_W), 0)
        col = _iota((AUG * LANES, FOX_W), 1)
        part, src = row >> 7, row & (LANES - 1)
        pair, off = col >> 7, col & (LANES - 1)
        sel_a = _ones_where((src == FF_STRIDE * pair) & (off == HEAD_DIM + part))
        sel_b = _ones_where((src == FF_STRIDE * pair + 1) & (off == part))
        first_half = (_iota((1, FOX_W), 1) & HEAD_DIM) == 0
        ka_ref[...] = jnp.where(first_half, kn, _dot(parts, sel_a)).astype(BF16)
        kb_ref[...] = jnp.where(first_half, _dot(parts, sel_b), kn).astype(BF16)

        x = pp_ref[:, 0:POOL_W]
        pg = pp_ref[:, POOL_W:2 * POOL_W]
        halo = jnp.where(i > 0, halo_ref[:, 0:POOL_W], 0.0)
        buf_ref[0:POOL_HALO, :] = halo
        buf_ref[POOL_HALO:POOL_HALO + ts, :] = x
        acc = x
        snaps = []
        for d in range(1, POOL_HALO):
            acc = acc + buf_ref[pl.ds(POOL_HALO - d, ts), :]
            if d + 1 in POOL_WINDOWS:
                snaps.append(acc)
        lane_group = _iota((1, POOL_W), 1) >> 6
        wsum = _pool_group_select(lane_group, snaps)
        wlen = _pool_group_select(lane_group, [float(w) for w in POOL_WINDOWS])
        tpos = (i * ts + _iota((ts, 1), 0) + 1).astype(F32)
        pooled = wsum / jnp.minimum(tpos, wlen) - x
        pb = pooled.astype(BF16)
        pooled_ref[...] = pb
        yp = _dot(pb, wpd_ref[...])
        yp_ref[...] = yp
        pm_ref[...] = (yp * ps_ref[...] * (pg * _sigmoid(pg))).astype(BF16)

    blk = lambda w, c: pl.BlockSpec((ts, w), lambda i: (i, c))
    full = lambda a: pl.BlockSpec(a.shape, lambda i: (0,) * a.ndim)
    out_shapes = [
        jax.ShapeDtypeStruct((S, FOX_W), BF16), jax.ShapeDtypeStruct((S, FOX_W), BF16), jax.ShapeDtypeStruct((S, FOX_W), BF16),
        jax.ShapeDtypeStruct((S, FOX_W), BF16),
        jax.ShapeDtypeStruct((S, SB_W), BF16), jax.ShapeDtypeStruct((S, SB_W), BF16), jax.ShapeDtypeStruct((S, SB_W), BF16),
        jax.ShapeDtypeStruct((S, POOL_W), BF16), jax.ShapeDtypeStruct((S, POOL_W), F32), jax.ShapeDtypeStruct((S, POOL_W), BF16),
    ]
    out_specs = [
        blk(FOX_W, 0), blk(FOX_W, 0), blk(FOX_W, 0), blk(FOX_W, 0),
        blk(SB_W, 0), blk(SB_W, 0), blk(SB_W, 0),
        blk(POOL_W, 0), blk(POOL_W, 0), blk(POOL_W, 0),
    ]
    return pl.pallas_call(
        body, name="prep", grid=(nb,),
        in_specs=[blk(FOX_W, C_FQ // FOX_W), blk(FOX_W, C_FK // FOX_W), blk(FOX_W, C_FV // FOX_W), blk(2 * POOL_W, C_PX // (2 * POOL_W)),
                  pl.BlockSpec((POOL_HALO, 2 * POOL_W), lambda i: (jnp.maximum(i * hb - 1, 0), C_PX // (2 * POOL_W))),
                  blk(LANES, 0),
                  blk(SB_W, C_SQ // SB_W), blk(SB_W, C_SK // SB_W), blk(SB_W, C_SV // SB_W),
                  full(qg), full(kg), full(bfp), full(wpd), full(ps)],
        out_specs=out_specs, out_shape=out_shapes,
        scratch_shapes=[pltpu.VMEM((1, LANES), F32), pltpu.VMEM((ts, LANES), F32), pltpu.VMEM((ts + POOL_HALO, POOL_W), F32)],
        compiler_params=_cparams(dimension_semantics=("arbitrary",)),
    )(projm, projm, projm, projm, projm, ffo, projm, projm, projm, qg, kg, bfp, wpd, ps)


def _pair_masks(x):
    ma = _iota((1, LANES), 1) < HEAD_DIM
    zero = jnp.zeros_like(x)
    return jnp.where(ma, x, zero), jnp.where(ma, zero, x)


def _aug_queries(q):
    lane = _iota((1, LANES), 1)
    one = jnp.ones_like(q)
    zero = jnp.zeros_like(q)
    qa = jnp.where(lane < HEAD_DIM, q, jnp.where(lane < HEAD_DIM + AUG, one, zero))
    qb = jnp.where(lane >= HEAD_DIM, q, jnp.where(lane < AUG, one, zero))
    return qa, qb


def _fox_fwd(qn, ka, kb, v, projm, *, tq, tk):
    S = qn.shape[0]
    npair = FOX_HEADS // 2

    def body(q_ref, ka_ref, kb_ref, v_ref, fg_ref, o_ref, lse_ref, fm_ref):
        qi = pl.program_id(1)
        lane = _iota((1, LANES), 1)
        ma = lane < HEAD_DIM
        qaug = _aug_queries(q_ref[...])
        k_refs = (ka_ref, kb_ref)

        def block(j, carry, masked):
            k0 = pl.multiple_of(j * tk, tk)
            vb = v_ref[pl.ds(k0, tk), :]
            if masked:
                mask = (k0 + _iota((tq, tk), 1)) <= (qi * tq + _iota((tq, tk), 0))
            scores = [_dot_nt(qaug[h], k_refs[h][pl.ds(k0, tk), :]) for h in range(2)]
            new = []
            for h in range(2):
                m, l, acc = carry[h]
                s = jnp.where(mask, scores[h], NEG) if masked else scores[h]
                m_new = jnp.maximum(m, jnp.max(s, axis=1, keepdims=True))
                alpha = jnp.exp(m - m_new)
                p = jnp.exp(s - m_new)
                new.append((m_new, alpha * l + jnp.sum(p, axis=1, keepdims=True), alpha * acc + _dot(p.astype(BF16), vb)))
            return tuple(new)

        init = tuple((jnp.full((tq, 1), NEG, F32), jnp.zeros((tq, 1), F32), jnp.zeros((tq, LANES), F32)) for _ in range(2))
        nfull = (qi * tq) // tk
        carry = lax.fori_loop(0, nfull, lambda j, c: block(j, c, False), init)
        for mi in range(max(1, tq // tk)):
            carry = block(nfull + mi, carry, True)
        (ma_, la, acca), (mb_, lb, accb) = carry
        o = jnp.where(ma, acca / la, accb / lb)
        o_ref[...] = o
        lse_ref[...] = jnp.where(ma, ma_ + jnp.log(la), mb_ + jnp.log(lb))
        fg = fg_ref[...]
        fm_ref[...] = (o * (fg * _sigmoid(fg))).astype(BF16)

    qblk = pl.BlockSpec((tq, LANES), lambda p, i: (i, p))
    kvblk = pl.BlockSpec((S, LANES), lambda p, i: (0, p))
    return pl.pallas_call(
        body, name="fox_fwd", grid=(npair, S // tq),
        in_specs=[qblk, kvblk, kvblk, kvblk,
                  pl.BlockSpec((tq, LANES), lambda p, i: (i, C_FG // LANES + p))],
        out_specs=[qblk, qblk, qblk],
        out_shape=[jax.ShapeDtypeStruct((S, FOX_W), F32), jax.ShapeDtypeStruct((S, FOX_W), F32), jax.ShapeDtypeStruct((S, FOX_W), BF16)],
        compiler_params=_cparams(dimension_semantics=("arbitrary", "arbitrary")),
    )(qn, ka, kb, v, projm)


def _suffix_sums(x, tmat2):
    return _dot(jnp.concatenate(_split2(x), axis=1), tmat2)


def _suffix_matrix(tk, inclusive):
    rr, cc = _iota((2 * tk, tk), 0) & (tk - 1), _iota((2 * tk, tk), 1)
    return _ones_where(rr >= cc) if inclusive else _ones_where(rr > cc)


def _sb_scores(qh, kb, causal, tmat2, r_runs):
    heads = range(2)
    zs = [_dot_nt(qh[h], kb) for h in heads]
    nsps = [jnp.minimum(-z, 0.0) - jnp.log(1.0 + jnp.exp(-jnp.abs(z))) for z in zs]
    lbs = nsps if causal is None else [jnp.where(causal, n, 0.0) for n in nsps]
    rins = [_suffix_sums(lb, tmat2) for lb in lbs]
    args = [zs[h] + lbs[h] + (rins[h] + r_runs[h]) for h in heads]
    a_s = [jnp.exp(arg if causal is None else jnp.where(causal, arg, NEG)) for arg in args]
    return zs, nsps, lbs, a_s


def _sb_fwd(sq, sk, sv, projm, *, tq, tk):
    S = sq.shape[0]
    npair = SB_HEADS // 2

    def body(q_ref, k_ref, v_ref, sg_ref, o_ref, sm_ref):
        qi = pl.program_id(1)
        lane = _iota((1, LANES), 1)
        ma = lane < HEAD_DIM
        qh = _pair_masks(q_ref[...])
        tmat2 = _suffix_matrix(tk, inclusive=False)
        nfull = (qi * tq) // tk

        def block(j, carry, masked):
            k0 = pl.multiple_of(j * tk, tk)
            kb = k_ref[pl.ds(k0, tk), :]
            vb = v_ref[pl.ds(k0, tk), :]
            causal = (k0 + _iota((tq, tk), 1)) < (qi * tq + _iota((tq, tk), 0)) if masked else None
            _, _, lbs, a_s = _sb_scores(qh, kb, causal, tmat2, [carry[h][0] for h in range(2)])
            pv = _dot(jnp.concatenate([a.astype(BF16) for a in a_s], axis=0), vb)
            return tuple((carry[h][0] + jnp.sum(lbs[h], axis=1, keepdims=True), carry[h][1] + pv[h * tq:(h + 1) * tq]) for h in range(2))

        init = tuple((jnp.zeros((tq, 1), F32), jnp.zeros((tq, LANES), F32)) for _ in range(2))
        carry = init
        for mi in reversed(range(max(1, tq // tk))):
            carry = block(nfull + mi, carry, True)
        (_, acca), (_, accb) = lax.fori_loop(0, nfull, lambda jj, c: block(nfull - 1 - jj, c, False), carry)
        o = jnp.where(ma, acca, accb)
        o_ref[...] = o
        sg = sg_ref[...]
        sm_ref[...] = (o * (sg * _sigmoid(sg))).astype(BF16)

    qblk = pl.BlockSpec((tq, LANES), lambda p, i: (i, p))
    kvblk = pl.BlockSpec((S, LANES), lambda p, i: (0, p))
    return pl.pallas_call(
        body, name="sb_fwd", grid=(npair, S // tq),
        in_specs=[qblk, kvblk, kvblk, pl.BlockSpec((tq, LANES), lambda p, i: (i, C_SG // LANES + p))],
        out_specs=[qblk, qblk],
        out_shape=[jax.ShapeDtypeStruct((S, SB_W), F32), jax.ShapeDtypeStruct((S, SB_W), BF16)],
        compiler_params=_cparams(dimension_semantics=("arbitrary", "arbitrary")),
    )(sq, sk, sv, projm)


def _outproj(x, fm, pm, sm, w_out, layer, *, tm):
    S, D = x.shape

    def body(x_ref, fm_ref, pm_ref, sm_ref, w_ref, y_ref):
        y = x_ref[...] + _dot(fm_ref[...], w_ref[0:FOX_W, :])
        y = y + _dot(pm_ref[...], w_ref[FOX_W:FOX_W + POOL_W, :])
        y_ref[...] = y + _dot(sm_ref[...], w_ref[FOX_W + POOL_W:D_MIX, :])

    row = lambda w: pl.BlockSpec((tm, w), lambda i: (i, 0))
    return pl.pallas_call(
        body, name="outproj", grid=(S // tm,),
        in_specs=[row(D), row(FOX_W), row(POOL_W), row(SB_W), pl.BlockSpec((None, D_MIX, D), lambda i: (layer, 0, 0))],
        out_specs=row(D), out_shape=jax.ShapeDtypeStruct((S, D), F32),
        compiler_params=_cparams(dimension_semantics=("arbitrary",)),
    )(x, fm, pm, sm, w_out)


def _loss_head(y, target, *, tm):
    S, D = y.shape

    def body(y_ref, t_ref, dy_ref, sq_ref):
        @pl.when(pl.program_id(0) == 0)
        def _():
            sq_ref[...] = jnp.zeros_like(sq_ref)

        d = y_ref[...] - t_ref[...]
        dy_ref[...] = d * (1.0 / D)
        sq_ref[...] += jnp.sum(d * d, axis=0, keepdims=True)

    row = pl.BlockSpec((tm, D), lambda i: (i, 0))
    return pl.pallas_call(
        body, name="loss_head", grid=(S // tm,),
        in_specs=[row, row], out_specs=[row, pl.BlockSpec((1, D), lambda i: (0, 0))],
        out_shape=[jax.ShapeDtypeStruct((S, D), F32), jax.ShapeDtypeStruct((1, D), F32)],
        compiler_params=_cparams(dimension_semantics=("arbitrary",)),
    )(y, target)


def _outproj_bwd(dy, fm, pm, sm, w_out, layer, stacks, *, tm):
    S, D = dy.shape

    def body(dy_ref, fm_ref, pm_ref, sm_ref, w_ref, dm_ref, dw_ref):
        @pl.when(pl.program_id(0) == 0)
        def _():
            dw_ref[...] = jnp.zeros_like(dw_ref)

        dyb = dy_ref[...].astype(BF16)
        dm_ref[...] = _dot_nt(dyb, w_ref[...])
        dw_ref[0:FOX_W, :] += _dot_tn(fm_ref[...], dyb)
        dw_ref[FOX_W:FOX_W + POOL_W, :] += _dot_tn(pm_ref[...], dyb)
        dw_ref[FOX_W + POOL_W:D_MIX, :] += _dot_tn(sm_ref[...], dyb)

    row = lambda w: pl.BlockSpec((tm, w), lambda i: (i, 0))
    wspec = pl.BlockSpec((None, D_MIX, D), lambda i: (layer, 0, 0))
    return _stack_call(
        body, "outproj_bwd", (S // tm,), [row(D), row(FOX_W), row(POOL_W), row(SB_W), wspec], (dy, fm, pm, sm, w_out),
        [pl.BlockSpec((None, D_MIX, D), lambda i: (layer, 0, 0))], [(D_MIX, D)], stacks,
        plain_specs=[row(D_MIX)], plain_shapes=[jax.ShapeDtypeStruct((S, D_MIX), F32)],
        compiler_params=_cparams(dimension_semantics=("arbitrary",)))


def _fox_bwd(qn, ka, kb, v, o, lse, dmix, projm, *, tq, tk):
    S = qn.shape[0]
    npair = FOX_HEADS // 2

    def body(q_ref, ka_ref, kb_ref, v_ref, o_ref, lse_ref, dm_ref, fg_ref,
             dq_ref, dk_ref, dv_ref, dfg_ref, dct_ref, dcr_ref):
        qi = pl.program_id(1)

        @pl.when(qi == 0)
        def _():
            dk_ref[...] = jnp.zeros_like(dk_ref)
            dv_ref[...] = jnp.zeros_like(dv_ref)
            dct_ref[...] = jnp.zeros_like(dct_ref)

        lane = _iota((1, LANES), 1)
        ma = lane < HEAD_DIM
        qh = _pair_masks(q_ref[...])
        qaug = _aug_queries(q_ref[...])
        k_refs = (ka_ref, kb_ref)
        lsev = lse_ref[...]
        lse = (_lane_pick(lsev, lane, 0), _lane_pick(lsev, lane, HEAD_DIM))
        fg = fg_ref[...]
        silu, dsilu = _silu_pair(fg)
        dm = dm_ref[...]
        ov = o_ref[...]
        do = dm * silu
        dfg_ref[...] = dm * ov * dsilu
        dd = do * ov
        dsum = (jnp.sum(jnp.where(ma, dd, 0.0), axis=1, keepdims=True), jnp.sum(jnp.where(ma, 0.0, dd), axis=1, keepdims=True))
        doh = _pair_masks(do.astype(BF16))
        do2 = jnp.concatenate(doh, axis=0)
        q2 = jnp.concatenate(qh, axis=0)

        def block(j, carry, masked):
            dq = carry[0]
            rows = list(carry[1:])
            k0 = pl.multiple_of(j * tk, tk)
            vb = v_ref[pl.ds(k0, tk), :]
            if masked:
                mask = (k0 + _iota((tq, tk), 1)) <= (qi * tq + _iota((tq, tk), 0))
            heads = range(2)
            kaugs = [k_refs[h][pl.ds(k0, tk), :] for h in heads]
            scores = [_dot_nt(qaug[h], kaugs[h]) for h in heads]
            dps = [_dot_nt(doh[h], vb) for h in heads]
            ps, dss = [], []
            for h in heads:
                s = jnp.where(mask, scores[h], NEG) if masked else scores[h]
                p = jnp.exp(s - lse[h])
                dsf = p * (dps[h] - dsum[h])
                dct_ref[0, h:h + 1, pl.ds(k0, tk)] -= jnp.sum(dsf, axis=0, keepdims=True)
                rows[h] = rows[h] + jnp.sum(dsf, axis=1, keepdims=True)
                ps.append(p.astype(BF16))
                dss.append(dsf.astype(BF16))
            dv_ref[pl.ds(k0, tk), :] += _dot_tn(jnp.concatenate(ps, axis=0), do2)
            dk_ref[pl.ds(k0, tk), :] += _dot_tn(jnp.concatenate(dss, axis=0), q2)
            kh = jnp.concatenate([_pair_masks(kaugs[h])[h] for h in heads], axis=0)
            dq = dq + _dot(jnp.concatenate(dss, axis=1), kh)
            return (dq, rows[0], rows[1])

        zcol = jnp.zeros((tq, 1), F32)
        nfull = (qi * tq) // tk
        carry = lax.fori_loop(0, nfull, lambda j, c: block(j, c, False), (jnp.zeros((tq, LANES), F32), zcol, zcol))
        for mi in range(max(1, tq // tk)):
            carry = block(nfull + mi, carry, True)
        dq, rowa, rowb = carry
        dq_ref[...] = dq * QK_SCALE
        dcr_ref[0] = jnp.where(ma, rowa, rowb)

    qblk = pl.BlockSpec((tq, LANES), lambda p, i: (i, p))
    kvblk = pl.BlockSpec((S, LANES), lambda p, i: (0, p))
    f32out = jax.ShapeDtypeStruct((S, FOX_W), F32)
    ctblk = pl.BlockSpec((1, FF_STRIDE, S), lambda p, i: (p, 0, 0))
    return pl.pallas_call(
        body, name="fox_bwd", grid=(npair, S // tq),
        in_specs=[qblk, kvblk, kvblk, kvblk, qblk, qblk, qblk,
                  pl.BlockSpec((tq, LANES), lambda p, i: (i, C_FG // LANES + p))],
        out_specs=[qblk, kvblk, kvblk, qblk, ctblk, pl.BlockSpec((1, tq, LANES), lambda p, i: (p, i, 0))],
        out_shape=[f32out, f32out, f32out, f32out, jax.ShapeDtypeStruct((npair, FF_STRIDE, S), F32),
                   jax.ShapeDtypeStruct((npair, S, LANES), F32)],
        compiler_params=_cparams(dimension_semantics=("arbitrary", "arbitrary")),
    )(qn, ka, kb, v, o, lse, dmix, projm)


def _sb_bwd(sq, sk, sv, o, dmix, projm, *, tq, tk):
    S = sq.shape[0]
    npair = SB_HEADS // 2
    mix0 = (FOX_W + POOL_W) // LANES

    def body(q_ref, k_ref, v_ref, o_ref, dm_ref, sg_ref, dq_ref, dk_ref, dv_ref, dsg_ref):
        qi = pl.program_id(1)

        @pl.when(qi == 0)
        def _():
            dk_ref[...] = jnp.zeros_like(dk_ref)
            dv_ref[...] = jnp.zeros_like(dv_ref)

        lane = _iota((1, LANES), 1)
        ma = lane < HEAD_DIM
        qh = _pair_masks(q_ref[...])
        sg = sg_ref[...]
        silu, dsilu = _silu_pair(sg)
        dm = dm_ref[...]
        ov = o_ref[...]
        do = dm * silu
        dsg_ref[...] = dm * ov * dsilu
        dob = do.astype(BF16)
        dd = dob.astype(F32) * ov
        dsum = (jnp.sum(jnp.where(ma, dd, 0.0), axis=1, keepdims=True), jnp.sum(jnp.where(ma, 0.0, dd), axis=1, keepdims=True))
        doh = _pair_masks(dob)
        do2 = jnp.concatenate(doh, axis=0)
        q2 = jnp.concatenate(qh, axis=0)
        tmat2 = _suffix_matrix(tk, inclusive=False)
        tmat2_inc = _suffix_matrix(tk, inclusive=True)
        nfull = (qi * tq) // tk

        def block(j, carry, masked):
            dq = carry[2]
            k0 = pl.multiple_of(j * tk, tk)
            kb = k_ref[pl.ds(k0, tk), :]
            vb = v_ref[pl.ds(k0, tk), :]
            kh = _pair_masks(kb)
            causal = (k0 + _iota((tq, tk), 1)) < (qi * tq + _iota((tq, tk), 0)) if masked else None
            heads = range(2)
            das = [_dot_nt(doh[h], vb) for h in heads]
            zs, nsps, lbs, a_s = _sb_scores(qh, kb, causal, tmat2, [carry[h][0] for h in heads])
            abs_ = [a.astype(BF16) for a in a_s]
            us = [abs_[h].astype(F32) * das[h] for h in heads]
            uins = [_suffix_sums(u, tmat2_inc) for u in us]
            dzs = []
            for h in heads:
                cum_u = dsum[h] - (uins[h] + carry[h][1])
                dz = us[h] * jnp.exp(nsps[h]) - jnp.exp(zs[h] + nsps[h]) * cum_u
                if masked:
                    dz = jnp.where(causal, dz, 0.0)
                dzs.append(dz.astype(BF16))
            dv_ref[pl.ds(k0, tk), :] += _dot_tn(jnp.concatenate(abs_, axis=0), do2)
            dk_ref[pl.ds(k0, tk), :] += _dot_tn(jnp.concatenate(dzs, axis=0), q2)
            dq = dq + _dot(jnp.concatenate(dzs, axis=1), jnp.concatenate(kh, axis=0))
            new = [(carry[h][0] + jnp.sum(lbs[h], axis=1, keepdims=True), carry[h][1] + jnp.sum(us[h], axis=1, keepdims=True)) for h in heads]
            return (new[0], new[1], dq)

        zcol = jnp.zeros((tq, 1), F32)
        init = ((zcol, zcol), (zcol, zcol), jnp.zeros((tq, LANES), F32))
        carry = init
        for mi in reversed(range(max(1, tq // tk))):
            carry = block(nfull + mi, carry, True)
        dq = lax.fori_loop(0, nfull, lambda jj, c: block(nfull - 1 - jj, c, False), carry)[2]
        dq_ref[...] = dq * QK_SCALE

    qblk = pl.BlockSpec((tq, LANES), lambda p, i: (i, p))
    kvblk = pl.BlockSpec((S, LANES), lambda p, i: (0, p))
    f32out = jax.ShapeDtypeStruct((S, SB_W), F32)
    return pl.pallas_call(
        body, name="sb_bwd", grid=(npair, S // tq),
        in_specs=[qblk, kvblk, kvblk, qblk,
                  pl.BlockSpec((tq, LANES), lambda p, i: (i, mix0 + p)),
                  pl.BlockSpec((tq, LANES), lambda p, i: (i, C_SG // LANES + p))],
        out_specs=[qblk, kvblk, kvblk, qblk],
        out_shape=[f32out, f32out, f32out, f32out],
        compiler_params=_cparams(dimension_semantics=("arbitrary", "arbitrary")),
    )(sq, sk, sv, o, dmix, projm)


def _prep_bwd(projm, ffo, dqn, dkn, dct, dcr, dv, dfg, dsq, dsk, dsv, dsg, dmix, pooled, yp, qg, kg, bfp, wpd, ps, *, ts):
    S = projm.shape[0]
    nb = S // ts
    hb = ts // POOL_HALO
    npair = FOX_HEADS // 2
    last_halo = S // POOL_HALO - 1

    def body(fq_ref, fk_ref, pp_ref, pph_ref, ff_ref,
             dqn_ref, dkn_ref, dct_ref, dcr_ref, dv_ref, dfg_ref, dsq_ref, dsk_ref, dsv_ref, dsg_ref,
             dmp_ref, dmh_ref, pooled_ref, yp_ref, qg_ref, kg_ref, bf_ref, wpd_ref, ps_ref,
             dp_ref, dqg_ref, dkg_ref, dbf_ref, dwp_ref, dps_ref,
             carry_ref, dl_ref, buf_ref, dct_s):
        i = pl.program_id(0)
        blk = nb - 1 - i

        @pl.when(i == 0)
        def _():
            carry_ref[...] = jnp.zeros_like(carry_ref)
            dqg_ref[...] = jnp.zeros_like(dqg_ref)
            dkg_ref[...] = jnp.zeros_like(dkg_ref)
            dbf_ref[...] = jnp.zeros_like(dbf_ref)
            dwp_ref[...] = jnp.zeros_like(dwp_ref)
            dps_ref[...] = jnp.zeros_like(dps_ref)

        bd = _head_blockdiag(FOX_W)
        for raw_ref, g_ref, dn, dg_ref, col in ((fq_ref, qg_ref, dqn_ref[...], dqg_ref, C_FQ), (fk_ref, kg_ref, dkn_ref[...], dkg_ref, C_FK)):
            q = raw_ref[...]
            rstd = lax.rsqrt(_group_sum(q * q, bd) * (1.0 / HEAD_DIM) + EPS)
            xhat = q * rstd
            dg_ref[...] += jnp.sum(dn * xhat, axis=0, keepdims=True)
            dyg = dn * g_ref[...]
            mean = _group_sum(dyg * xhat, bd) * (1.0 / HEAD_DIM)
            dp_ref[:, col:col + FOX_W] = (rstd * (dyg - xhat * mean)).astype(BF16)
        dp_ref[:, C_FV:C_FV + FOX_W] = dv_ref[...].astype(BF16)
        dp_ref[:, C_FG:C_FG + FOX_W] = dfg_ref[...].astype(BF16)
        dp_ref[:, C_SQ:C_SQ + SB_W] = dsq_ref[...].astype(BF16)
        dp_ref[:, C_SK:C_SK + SB_W] = dsk_ref[...].astype(BF16)
        dp_ref[:, C_SV:C_SV + SB_W] = dsv_ref[...].astype(BF16)
        dp_ref[:, C_SG:C_SG + SB_W] = dsg_ref[...].astype(BF16)

        dct_s[...] = jnp.zeros_like(dct_s)
        for p in range(npair):
            dct_s[FF_STRIDE * p:FF_STRIDE * (p + 1), :] = dct_ref[p]
        dc = dct_s[...].T
        lane = _iota((1, LANES), 1)
        for p in range(npair):
            dcr = dcr_ref[p]
            dc = dc + jnp.where(lane == FF_STRIDE * p, _lane_pick(dcr, lane, 0), 0.0)
            dc = dc + jnp.where(lane == FF_STRIDE * p + 1, _lane_pick(dcr, lane, HEAD_DIM), 0.0)
        triu = _ones_where(_iota((ts, ts), 1) >= _iota((ts, ts), 0))
        dlf = _dot_exact_lhs(triu, dc) + carry_ref[...]
        dl_ref[...] = dlf
        carry_ref[...] = dl_ref[0:1, :]
        z = ff_ref[...] + bf_ref[...]
        dff = dlf * (1.0 / (1.0 + jnp.exp(z)))
        dbf_ref[...] += jnp.sum(dff, axis=0, keepdims=True)
        dp_ref[:, PM:PW] = dff.astype(BF16)

        psv = ps_ref[...]
        wpdv = wpd_ref[...]
        lane_group = _iota((1, POOL_W), 1) >> 6
        wlen = _pool_group_select(lane_group, [float(w) for w in POOL_WINDOWS])
        pg = pp_ref[:, POOL_W:2 * POOL_W]
        silu, dsilu = _silu_pair(pg)
        dmp = dmp_ref[...]
        ypv = yp_ref[...]
        dp_ref[:, C_PG:C_PG + POOL_W] = (dmp * (ypv * psv) * dsilu).astype(BF16)
        dps_ref[...] += jnp.sum(dmp * silu * ypv, axis=0, keepdims=True)
        dyp = (dmp * psv * silu).astype(BF16)
        dwp_ref[...] += _dot_tn(pooled_ref[...], dyp)
        dpooled = _dot_nt(dyp, wpdv)
        pgh = pph_ref[:, POOL_W:2 * POOL_W]
        dyph = (dmh_ref[...] * psv * (pgh * _sigmoid(pgh))).astype(BF16)
        dpooled_h = jnp.where(blk < nb - 1, _dot_nt(dyph, wpdv), 0.0)
        tpos = (blk * ts + _iota((ts, 1), 0) + 1).astype(F32)
        ev = dpooled / jnp.minimum(tpos, wlen)
        buf_ref[0:ts, :] = ev
        buf_ref[ts:ts + POOL_HALO, :] = dpooled_h / wlen
        acc = ev
        snaps = []
        for d in range(1, POOL_HALO):
            acc = acc + buf_ref[pl.ds(d, ts), :]
            if d + 1 in POOL_WINDOWS:
                snaps.append(acc)
        dp_ref[:, C_PX:C_PX + POOL_W] = (_pool_group_select(lane_group, snaps) - dpooled).astype(BF16)

    rblk = lambda w, c: pl.BlockSpec((ts, w), lambda i: (nb - 1 - i, c))
    full = lambda a: pl.BlockSpec(a.shape, lambda i: (0,) * a.ndim)
    halo = lambda w, c: pl.BlockSpec((POOL_HALO, w), lambda i: (jnp.minimum((nb - i) * hb, last_halo), c))
    acc_spec = lambda r, w: pl.BlockSpec((r, w), lambda i: (0, 0))
    return pl.pallas_call(
        body, name="prep_bwd", grid=(nb,),
        in_specs=[rblk(FOX_W, C_FQ // FOX_W), rblk(FOX_W, C_FK // FOX_W), rblk(2 * POOL_W, C_PX // (2 * POOL_W)),
                  halo(2 * POOL_W, C_PX // (2 * POOL_W)), rblk(LANES, 0),
                  rblk(FOX_W, 0), rblk(FOX_W, 0), pl.BlockSpec((npair, FF_STRIDE, ts), lambda i: (0, 0, nb - 1 - i)),
                  pl.BlockSpec((npair, ts, LANES), lambda i: (0, nb - 1 - i, 0)), rblk(FOX_W, 0), rblk(FOX_W, 0),
                  rblk(SB_W, 0), rblk(SB_W, 0), rblk(SB_W, 0), rblk(SB_W, 0),
                  rblk(POOL_W, FOX_W // POOL_W), halo(POOL_W, FOX_W // POOL_W), rblk(POOL_W, 0), rblk(POOL_W, 0),
                  full(qg), full(kg), full(bfp), full(wpd), full(ps)],
        out_specs=[rblk(PW, 0), acc_spec(1, FOX_W), acc_spec(1, FOX_W), acc_spec(1, LANES), acc_spec(POOL_W, POOL_W), acc_spec(1, POOL_W)],
        out_shape=[jax.ShapeDtypeStruct((S, PW), BF16), jax.ShapeDtypeStruct((1, FOX_W), F32), jax.ShapeDtypeStruct((1, FOX_W), F32),
                   jax.ShapeDtypeStruct((1, LANES), F32), jax.ShapeDtypeStruct((POOL_W, POOL_W), F32), jax.ShapeDtypeStruct((1, POOL_W), F32)],
        scratch_shapes=[pltpu.VMEM((1, LANES), F32), pltpu.VMEM((ts, LANES), F32), pltpu.VMEM((ts + POOL_HALO, POOL_W), F32),
                        pltpu.VMEM((LANES, ts), F32)],
        compiler_params=_cparams(dimension_semantics=("arbitrary",)),
    )(projm, projm, projm, projm, ffo, dqn, dkn, dct, dcr, dv, dfg, dsq, dsk, dsv, dsg, dmix, dmix, pooled, yp, qg, kg, bfp, wpd, ps)


def _stack_call(body, name, grid, in_specs, operands, slot_specs, slot_shapes, stacks, plain_specs=(), plain_shapes=(), **kw):
    out_specs = list(plain_specs) + list(slot_specs)
    out_shape = list(plain_shapes) + [jax.ShapeDtypeStruct((DEPTH,) + s, F32) for s in slot_shapes]
    if stacks is None:
        return pl.pallas_call(body, name=name, grid=grid, in_specs=in_specs, out_specs=out_specs, out_shape=out_shape, **kw)(*operands)
    n = len(operands)

    def aliased_body(*refs):
        body(*refs[:n], *refs[n + len(stacks):])

    return pl.pallas_call(
        aliased_body, name=name, grid=grid, in_specs=list(in_specs) + [pl.BlockSpec(memory_space=pl.ANY)] * len(stacks),
        out_specs=out_specs, out_shape=out_shape,
        input_output_aliases={n + k: len(plain_specs) + k for k in range(len(stacks))}, **kw)(*operands, *stacks)


def _inproj_dw(h, dproj, layer, stacks, *, ts, tn):
    S, D = h.shape
    nj = PM // tn

    def body(h_ref, dp_ref, dpf_ref, dw_ref, dwf_ref):
        s = pl.program_id(1)

        @pl.when(s == 0)
        def _():
            dw_ref[...] = jnp.zeros_like(dw_ref)

        @pl.when((s == 0) & (pl.program_id(0) == 0))
        def _():
            dwf_ref[...] = jnp.zeros_like(dwf_ref)

        hv = h_ref[...]
        dw_ref[...] += _dot_tn(dp_ref[...], hv)

        @pl.when(pl.program_id(0) == 0)
        def _():
            dwf_ref[...] += _dot_tn(dpf_ref[...], hv)

    return _stack_call(
        body, "inproj_dw", (nj, S // ts),
        [pl.BlockSpec((ts, D), lambda j, s: (s, 0)),
         pl.BlockSpec((ts, tn), lambda j, s: (s, j)),
         pl.BlockSpec((ts, LANES), lambda j, s: (s, PM // LANES))],
        (h, dproj, dproj),
        [pl.BlockSpec((None, tn, D), lambda j, s: (layer, j, 0)), pl.BlockSpec((None, LANES, D), lambda j, s: (layer, 0, 0))],
        [(PM, D), (LANES, D)], stacks,
        compiler_params=_cparams(dimension_semantics=("arbitrary", "arbitrary")))


def _inproj_dx(dproj, wt_all, layer, x, g, dy, *, tm):
    S, D = x.shape

    def body(dp_ref, w_ref, x_ref, g_ref, dy_ref, dx_ref, dg_ref):
        @pl.when(pl.program_id(0) == 0)
        def _():
            dg_ref[...] = jnp.zeros_like(dg_ref)

        dh = _dot(dp_ref[...], w_ref[...])
        xf = x_ref[...]
        rstd = lax.rsqrt(jnp.mean(xf * xf, axis=-1, keepdims=True) + EPS)
        xhat = xf * rstd
        dg_ref[...] += jnp.sum(dh * xhat, axis=0, keepdims=True)
        dyg = dh * g_ref[...]
        mean = jnp.mean(dyg * xhat, axis=-1, keepdims=True)
        dx_ref[...] = rstd * (dyg - xhat * mean) + dy_ref[...]

    row = lambda w: pl.BlockSpec((tm, w), lambda i: (i, 0))
    return pl.pallas_call(
        body, name="inproj_dx", grid=(S // tm,),
        in_specs=[row(PW), pl.BlockSpec((None, PW, D), lambda i: (layer, 0, 0)), row(D), pl.BlockSpec((1, D), lambda i: (0, 0)), row(D)],
        out_specs=[row(D), pl.BlockSpec((1, D), lambda i: (0, 0))],
        out_shape=[jax.ShapeDtypeStruct((S, D), F32), jax.ShapeDtypeStruct((1, D), F32)],
        compiler_params=_cparams(dimension_semantics=("arbitrary",)),
    )(dproj, wt_all, x, g, dy)


def _adam_update(w, g, m, v):
    nm = ADAM_B1 * m + (1.0 - ADAM_B1) * g
    nv = ADAM_B2 * v + (1.0 - ADAM_B2) * (g * g)
    m_hat = nm / (1.0 - ADAM_B1 ** ADAM_STEP)
    v_hat = nv / (1.0 - ADAM_B2 ** ADAM_STEP)
    return -ADAM_LR * (m_hat / (jnp.sqrt(v_hat) + ADAM_EPS) + ADAM_WD * w), nm, nv


def _adamw(w, g, m, v):
    L, R, C = w.shape
    tr = R if R <= 512 else 256

    def body(w_ref, g_ref, m_ref, v_ref, d_ref, nm_ref, nv_ref):
        d_ref[...], nm_ref[...], nv_ref[...] = _adam_update(w_ref[...], g_ref[...], m_ref[...], v_ref[...])

    spec = pl.BlockSpec((1, tr, C), lambda l, i: (l, i, 0))
    shp = jax.ShapeDtypeStruct((L, R, C), F32)
    return pl.pallas_call(
        body, name="adamw", grid=(L, R // tr), in_specs=[spec] * 4, out_specs=[spec] * 3, out_shape=[shp] * 3,
        compiler_params=_cparams(dimension_semantics=("arbitrary", "arbitrary")),
    )(w, g, m, v)


def _adamw_nd(w, g, m, v):
    shape = w.shape
    view = (1,) + shape if w.ndim == 2 else (shape[0], -1, shape[-1])
    outs = _adamw(w.reshape(view), g.reshape(view), m.reshape(view), v.reshape(view))
    return tuple(o.reshape(shape) for o in outs)


FLIP_C = (0, 0, 1)
FLIP_X = (1, 0, 0)
FLIP_Y = (0, 1, 0)
FLIP_XY = (1, 1, 0)
MESH = pl.DeviceIdType.MESH


def _peer(flip):
    me = (lax.axis_index("x"), lax.axis_index("y"), lax.axis_index("c"))
    return tuple(1 - a if f else a for a, f in zip(me, flip))


def _exchange(name, arrays, flips):
    n = len(arrays)

    def body(*refs):
        srcs, dsts = refs[:n], refs[n:2 * n]
        send_sems, recv_sems = refs[2 * n:]
        copies = [pltpu.make_async_remote_copy(src_ref=srcs[k], dst_ref=dsts[k], send_sem=send_sems.at[k], recv_sem=recv_sems.at[k],
                                               device_id=_peer(flips[k]), device_id_type=MESH) for k in range(n)]
        for cp in copies:
            cp.start()
        for cp in copies:
            cp.wait()

    anyspec = pl.BlockSpec(memory_space=pl.ANY)
    return pl.pallas_call(
        body, name=name, in_specs=[anyspec] * n, out_specs=[anyspec] * n,
        out_shape=[jax.ShapeDtypeStruct(a.shape, a.dtype) for a in arrays],
        scratch_shapes=[pltpu.SemaphoreType.DMA((n,)), pltpu.SemaphoreType.DMA((n,))],
    )(*arrays)


def _exchange_add(name, x, flip):
    def body(x_ref, o_ref, buf_ref, send_sem, recv_sem):
        cp = pltpu.make_async_remote_copy(src_ref=x_ref, dst_ref=buf_ref, send_sem=send_sem, recv_sem=recv_sem,
                                          device_id=_peer(flip), device_id_type=MESH)
        cp.start()
        cp.wait()
        o_ref[...] = x_ref[...] + buf_ref[...]

    vspec = pl.BlockSpec(memory_space=pltpu.VMEM)
    return pl.pallas_call(
        body, name=name, in_specs=[vspec], out_specs=vspec, out_shape=jax.ShapeDtypeStruct(x.shape, x.dtype),
        scratch_shapes=[pltpu.VMEM(x.shape, x.dtype), pltpu.SemaphoreType.DMA, pltpu.SemaphoreType.DMA],
    )(x)


def _chip_index():
    return 2 * lax.axis_index("x") + lax.axis_index("y")


def _gather_weights(w_in_t, w_out):
    wi = w_in_t.astype(BF16)
    wo = jnp.swapaxes(w_out, 0, 1).astype(BF16)
    halves = (wi.shape[0] // 2, wo.shape[0] // 2)
    masks = (2, 1, 3)
    flips = (FLIP_X, FLIP_Y, FLIP_XY)
    n_first = 2 * len(masks)

    def body(wi_ref, wo_ref, gi_ref, go_ref, send_sems, recv_sems):
        c = lax.axis_index("c")
        j = _chip_index()
        srcs = (wi_ref, wo_ref)
        dsts = (gi_ref, go_ref)
        mine = [pl.ds(h * c, h) for h in halves]
        theirs = [pl.ds(h * (1 - c), h) for h in halves]

        def copy(idx, src, dst, flip):
            return pltpu.make_async_remote_copy(src_ref=src, dst_ref=dst, send_sem=send_sems.at[idx], recv_sem=recv_sems.at[idx],
                                                device_id=_peer(flip), device_id_type=MESH)

        first = [copy(2 * k + a, srcs[a].at[mine[a]], dsts[a].at[j, mine[a]], flips[k]) for k in range(len(masks)) for a in range(2)]
        for cp in first:
            cp.start()
        passed = []
        for k, m in enumerate(masks):
            for a in range(2):
                slot = dsts[a].at[j ^ m, mine[a]]
                copy(2 * k + a, slot, slot, flips[k]).wait_recv()
                fwd = copy(n_first + 2 * k + a, slot, slot, FLIP_C)
                fwd.start()
                passed.append(fwd)
        for k, m in enumerate(masks):
            for a in range(2):
                slot = dsts[a].at[j ^ m, theirs[a]]
                copy(n_first + 2 * k + a, slot, slot, FLIP_C).wait_recv()
        for cp in first + passed:
            cp.wait_send()

    anyspec = pl.BlockSpec(memory_space=pl.ANY)
    gi, go = pl.pallas_call(
        body, name="gather_weights", in_specs=[anyspec] * 2, out_specs=[anyspec] * 2,
        out_shape=[jax.ShapeDtypeStruct((4,) + wi.shape, BF16), jax.ShapeDtypeStruct((4,) + wo.shape, BF16)],
        scratch_shapes=[pltpu.SemaphoreType.DMA((2 * n_first,)), pltpu.SemaphoreType.DMA((2 * n_first,))],
    )(wi, wo)
    own = lax.broadcasted_iota(jnp.int32, (4, 1, 1, 1), 0) == _chip_index()
    gi = jnp.where(own, wi[None], gi)
    go = jnp.where(own, wo[None], go)
    w_in_t_full = gi.reshape((4 * wi.shape[0],) + wi.shape[1:])
    w_out_full = jnp.swapaxes(go.reshape((4 * wo.shape[0],) + wo.shape[1:]), 0, 1)
    return w_in_t_full, w_out_full


def _to_aligned(w_t):
    _, L, D = w_t.shape
    npair = FOX_HEADS // 2
    ff = w_t[ORIG_FF:ORIG_REST].reshape(npair, 2, L, D)
    ff = jnp.pad(ff, ((0, 0), (0, FF_STRIDE - 2), (0, 0), (0, 0))).reshape(npair * FF_STRIDE, L, D)
    ff = jnp.pad(ff, ((0, LANES - npair * FF_STRIDE), (0, 0), (0, 0)))
    return jnp.swapaxes(jnp.concatenate([w_t[:ORIG_FOX], w_t[ORIG_REST:], ff], axis=0), 0, 1)


def _from_aligned(dw_t):
    n, _, D = dw_t.shape
    npair = FOX_HEADS // 2
    ff = dw_t[:, PM:PM + npair * FF_STRIDE].reshape(n, npair, FF_STRIDE, D)[:, :, :2].reshape(n, FOX_HEADS, D)
    return jnp.swapaxes(jnp.concatenate([dw_t[:, :ORIG_FOX], ff, dw_t[:, ORIG_FOX:PM]], axis=1), 0, 1)


def _half_layers(name, stack, got):
    L, R, C = stack.shape
    half = L // 2
    tr = min(256, R)
    c = lax.axis_index("c")
    which = ((1 - c) if got is None else c).astype(jnp.int32).reshape(1)

    def body(c_ref, x_ref, *refs):
        if got is None:
            refs[0][...] = x_ref[...].astype(BF16)
        else:
            acc = x_ref[...] + refs[0][...].astype(F32)
            refs[1][...] = acc
            refs[2][...] = acc.astype(BF16)

    plain = pl.BlockSpec((1, tr, C), lambda l, i, c_ref: (l, i, 0))
    picked = pl.BlockSpec((1, tr, C), lambda l, i, c_ref: (c_ref[0] * half + l, i, 0))
    shp = lambda dt: jax.ShapeDtypeStruct((half, R, C), dt)
    grid_spec = pltpu.PrefetchScalarGridSpec(
        num_scalar_prefetch=1, grid=(half, R // tr),
        in_specs=[picked] + ([] if got is None else [plain]), out_specs=[plain] if got is None else [plain, plain])
    return pl.pallas_call(
        body, name=name, grid_spec=grid_spec, out_shape=[shp(BF16)] if got is None else [shp(F32), shp(BF16)],
        compiler_params=_cparams(dimension_semantics=("arbitrary", "arbitrary")),
    )(which, stack, *([] if got is None else [got]))


def _reduce_scatter(stack_m, stack_f, stack_o, shard_cols, shard_rows):
    j = _chip_index()
    half = DEPTH // 2
    stacks = (stack_m, stack_f, stack_o)
    give = [_half_layers("rs_give", s, None)[0] for s in stacks]
    got = _exchange("rs_d2d", give, (FLIP_C,) * len(stacks))
    (m32, mbf), (f32_, fbf), (o32, obf) = [_half_layers("rs_add_chip", s, g) for s, g in zip(stacks, got)]
    d_model = stack_m.shape[2]

    def in_shards(m, f):
        return _from_aligned(jnp.concatenate([m, f], axis=1)).reshape(4, shard_cols, half, d_model)

    def out_shards(o):
        return jnp.moveaxis(o.reshape(half, 4, shard_rows, o.shape[-1]), 1, 0)

    chip = [(in_shards(m32, f32_), in_shards(mbf, fbf)), (out_shards(o32), out_shards(obf))]
    masks = (2, 1, 3)
    flips = (FLIP_X, FLIP_Y, FLIP_XY)
    sends, sflips = [], []
    for _, bf in chip:
        for m, fl in zip(masks, flips):
            sends.append(lax.dynamic_index_in_dim(bf, j ^ m, axis=0, keepdims=False))
            sflips.append(fl)
    got = _exchange("rs_ici", sends, tuple(sflips))
    own_in, own_out = [lax.dynamic_index_in_dim(f32_sum, j, axis=0, keepdims=False) for f32_sum, _ in chip]
    mine_in = _add_rows("rs_add_in", own_in, list(got[0:3]))
    mine_out = _add_into_half("rs_add_out", own_out, list(got[3:6]))
    sib_in, g_out = _share_halves(mine_in, mine_out)
    return (mine_in, sib_in), g_out


def _add_rows(name, first, others):
    n = len(others)

    def body(*refs):
        acc = refs[0][...]
        for r in refs[1:1 + n]:
            acc = acc + r[...].astype(F32)
        refs[1 + n][...] = acc

    grid, spec = _row_lane_blocks(first.shape)
    return pl.pallas_call(
        body, name=name, grid=grid, in_specs=[spec(first.shape[1])] * (1 + n), out_specs=spec(first.shape[1]),
        out_shape=jax.ShapeDtypeStruct(first.shape, F32),
        compiler_params=_cparams(dimension_semantics=("arbitrary", "arbitrary")),
    )(first, *others)


def _row_lane_blocks(shape):
    rows, _, C = shape
    tr = rows // 2 if rows % 2 == 0 and rows > 64 else rows
    return (rows // tr, C // LANES), lambda n_mid: pl.BlockSpec((tr, n_mid, LANES), lambda i, k, *_: (i, 0, k))


def _add_into_half(name, first, others):
    half, rows, C = first.shape
    tr = min(256, rows)
    n = len(others)

    def body(c_ref, *refs):
        acc = refs[0][...]
        for r in refs[1:1 + n]:
            acc = acc + r[...].astype(F32)
        refs[1 + n][...] = acc

    grid_spec = pltpu.PrefetchScalarGridSpec(
        num_scalar_prefetch=1, grid=(half, rows // tr),
        in_specs=[pl.BlockSpec((1, tr, C), lambda l, i, c_ref: (l, i, 0))] * (1 + n),
        out_specs=pl.BlockSpec((1, tr, C), lambda l, i, c_ref: (c_ref[0] * half + l, i, 0)))
    return pl.pallas_call(
        body, name=name, grid_spec=grid_spec, out_shape=jax.ShapeDtypeStruct((2 * half, rows, C), F32),
        compiler_params=_cparams(dimension_semantics=("arbitrary", "arbitrary")),
    )(lax.axis_index("c").astype(jnp.int32).reshape(1), first, *others)


def _share_halves(mine, buf):
    half = DEPTH // 2

    def body(mine_ref, buf_in, sib_ref, buf_ref, send_sems, recv_sems):
        lay = pl.ds(half * lax.axis_index("c"), half)
        copies = [pltpu.make_async_remote_copy(src_ref=src, dst_ref=dst, send_sem=send_sems.at[k], recv_sem=recv_sems.at[k],
                                               device_id=_peer(FLIP_C), device_id_type=MESH)
                  for k, (src, dst) in enumerate(((mine_ref, sib_ref), (buf_ref.at[lay], buf_ref.at[lay])))]
        for cp in copies:
            cp.start()
        for cp in copies:
            cp.wait()

    anyspec = pl.BlockSpec(memory_space=pl.ANY)
    return pl.pallas_call(
        body, name="rs_share", in_specs=[anyspec] * 2, out_specs=[anyspec] * 2,
        out_shape=[jax.ShapeDtypeStruct(mine.shape, mine.dtype), jax.ShapeDtypeStruct(buf.shape, buf.dtype)],
        input_output_aliases={1: 1},
        scratch_shapes=[pltpu.SemaphoreType.DMA((2,)), pltpu.SemaphoreType.DMA((2,))],
    )(mine, buf)


def _adamw_halves(w, g_mine, g_sib, m, v):
    half = g_mine.shape[1]

    def body(c_ref, w_ref, gm_ref, gs_ref, m_ref, v_ref, g_ref, d_ref, nm_ref, nv_ref):
        first = c_ref[0] == 0
        gm, gs = gm_ref[...], gs_ref[...]
        for h, gv in enumerate((jnp.where(first, gm, gs), jnp.where(first, gs, gm))):
            lay = slice(half * h, half * (h + 1))
            g_ref[:, lay, :] = gv
            d_ref[:, lay, :], nm_ref[:, lay, :], nv_ref[:, lay, :] = _adam_update(w_ref[:, lay, :], gv, m_ref[:, lay, :], v_ref[:, lay, :])

    grid, spec = _row_lane_blocks(w.shape)
    full, part = spec(w.shape[1]), spec(half)
    grid_spec = pltpu.PrefetchScalarGridSpec(num_scalar_prefetch=1, grid=grid, in_specs=[full, part, part, full, full], out_specs=[full] * 4)
    return pl.pallas_call(
        body, name="adamw_halves", grid_spec=grid_spec, out_shape=[jax.ShapeDtypeStruct(w.shape, F32)] * 4,
        compiler_params=_cparams(dimension_semantics=("arbitrary", "arbitrary")),
    )(lax.axis_index("c").astype(jnp.int32).reshape(1), w, g_mine, g_sib, m, v)


def _all_reduce_small(x):
    x = _exchange_add("ar_c", x, FLIP_C)
    x = _exchange_add("ar_y", x, FLIP_Y)
    return _exchange_add("ar_x", x, FLIP_X)


def _blocks(S):
    return dict(tm=min(512, S), ts=min(512, S), tq=min(512, S), tk=min(512, S), tks=min(256, S))


def _pair_pad(vec):
    npair = FOX_HEADS // 2
    v = jnp.pad(vec.reshape(npair, 2), ((0, 0), (0, FF_STRIDE - 2))).reshape(1, npair * FF_STRIDE)
    return jnp.pad(v, ((0, 0), (0, LANES - npair * FF_STRIDE)))


def _pair_unpad(row):
    npair = FOX_HEADS // 2
    return row[0, :npair * FF_STRIDE].reshape(npair, FF_STRIDE)[:, :2].reshape(FOX_HEADS)


def _pool_blockdiag(w_pool):
    g, cg, _ = w_pool.shape
    eye = jnp.eye(g, dtype=w_pool.dtype)
    return jnp.einsum("gh,gcd->gchd", eye, w_pool).reshape(g * cg, g * cg)


def _layer_params(norm_g, b_f, q_norm_g, k_norm_g, w_pool, pool_scale):
    return dict(g=norm_g.reshape(1, -1), qg=jnp.tile(q_norm_g, FOX_HEADS).reshape(1, FOX_W), kg=jnp.tile(k_norm_g, FOX_HEADS).reshape(1, FOX_W),
                bfp=_pair_pad(b_f), wpd=_pool_blockdiag(w_pool).astype(BF16), ps=pool_scale.reshape(1, POOL_W))


def _layer_fwd(x, wt_all, w_out, layer, prm, bs):
    projm, ffo, h = _inproj(x, prm["g"], wt_all, layer, tm=bs["tm"], tn=512)
    qn, ka, kb, v, sq, sk, sv, pooled, yp, pm = _prep(projm, ffo, prm["qg"], prm["kg"], prm["bfp"], prm["wpd"], prm["ps"], ts=bs["ts"])
    o, lse, fm = _fox_fwd(qn, ka, kb, v, projm, tq=bs["tq"], tk=bs["tk"])
    so, sm = _sb_fwd(sq, sk, sv, projm, tq=bs["tq"], tk=bs["tks"])
    y = _outproj(x, fm, pm, sm, w_out, layer, tm=bs["tm"])
    saved = dict(x=x, projm=projm, ffo=ffo, h=h, qn=qn, ka=ka, kb=kb, v=v, sq=sq, sk=sk, sv=sv, pooled=pooled, yp=yp,
                 o=o, lse=lse, so=so, fm=fm, pm=pm, sm=sm)
    return y, saved


def _layer_bwd(dy, wt_all, w_out, prm, sv_, bs, layer, stacks):
    dmix, stack_o = _outproj_bwd(dy, sv_["fm"], sv_["pm"], sv_["sm"], w_out, layer, None if stacks is None else stacks[2:], tm=bs["tm"])
    dqn, dkn, dv, dfg, dct, dcr = _fox_bwd(sv_["qn"], sv_["ka"], sv_["kb"], sv_["v"], sv_["o"], sv_["lse"], dmix, sv_["projm"],
                                      tq=bs["tq"], tk=bs["tk"])
    dsq, dsk, dsv, dsg = _sb_bwd(sv_["sq"], sv_["sk"], sv_["sv"], sv_["so"], dmix, sv_["projm"], tq=bs["tq"], tk=bs["tks"])
    dproj, dqg, dkg, dbf, dwp, dps = _prep_bwd(sv_["projm"], sv_["ffo"], dqn, dkn, dct, dcr, dv, dfg, dsq, dsk, dsv, dsg, dmix,
                                               sv_["pooled"], sv_["yp"], prm["qg"], prm["kg"], prm["bfp"], prm["wpd"], prm["ps"], ts=bs["ts"])
    stack_m, stack_f = _inproj_dw(sv_["h"], dproj, layer, None if stacks is None else stacks[:2], ts=bs["ts"], tn=512)
    dx, dg = _inproj_dx(dproj, wt_all, layer, sv_["x"], prm["g"], dy, tm=min(256, bs["tm"]))
    grads = dict(
        norm_g=dg[0],
        b_f=_pair_unpad(dbf), q_norm_g=dqg.reshape(FOX_HEADS, HEAD_DIM).sum(0), k_norm_g=dkg.reshape(FOX_HEADS, HEAD_DIM).sum(0),
        w_pool=jnp.stack([dwp[HEAD_DIM * g:HEAD_DIM * (g + 1), HEAD_DIM * g:HEAD_DIM * (g + 1)] for g in range(4)]),
        pool_scale=dps[0])
    return dx, grads, (stack_m, stack_f, stack_o)


def _local_step(x, target, wt_all, w_out, norm_g, b_f, q_norm_g, k_norm_g, w_pool, pool_scale):
    S, D = x.shape
    bs = _blocks(S)
    prms = [_layer_params(norm_g[l], b_f[l], q_norm_g[l], k_norm_g[l], w_pool[l], pool_scale[l]) for l in range(DEPTH)]
    saved = []
    y = x
    for l in range(DEPTH):
        y, s_ = _layer_fwd(y, wt_all, w_out, l, prms[l], bs)
        saved.append(s_)
    dy, sq = _loss_head(y, target, tm=bs["tm"])
    loss = 0.5 * jnp.sum(sq) / D
    grads = [None] * DEPTH
    stacks = None
    for l in reversed(range(DEPTH)):
        dy, grads[l], stacks = _layer_bwd(dy, wt_all, w_out, prms[l], saved[l], bs, l, stacks)
    stacked = {k: jnp.stack([g[k] for g in grads]) for k in grads[0]}
    return loss, dy, stacked, stacks


SMALL = ("norm_g", "b_f", "q_norm_g", "k_norm_g", "w_pool", "pool_scale")


def _pack_small(gr):
    flat = jnp.concatenate([gr[k].reshape(-1) for k in SMALL])
    pad = (-flat.shape[0]) % (8 * LANES)
    return jnp.pad(flat, (0, pad)).reshape(-1, LANES)


def _unpack_small(packed, like):
    flat = packed.reshape(-1)
    out, off = {}, 0
    for k in SMALL:
        n = like[k].size
        out[k] = flat[off:off + n].reshape(like[k].shape)
        off += n
    return out


def kernel(x, norm_g, w_in, b_f, q_norm_g, k_norm_g, w_pool, pool_scale, w_out, loss_target, m_norm_g, m_w_in, m_b_f, m_q_norm_g, m_k_norm_g, m_w_pool, m_pool_scale, m_w_out, v_norm_g, v_w_in, v_b_f, v_q_norm_g, v_k_norm_g, v_w_pool, v_pool_scale, v_w_out):
    weights = dict(norm_g=norm_g, w_in=w_in, b_f=b_f, q_norm_g=q_norm_g, k_norm_g=k_norm_g, w_pool=w_pool, pool_scale=pool_scale, w_out=w_out)
    mom_m = dict(norm_g=m_norm_g, w_in=m_w_in, b_f=m_b_f, q_norm_g=m_q_norm_g, k_norm_g=m_k_norm_g, w_pool=m_w_pool, pool_scale=m_pool_scale, w_out=m_w_out)
    mom_v = dict(norm_g=v_norm_g, w_in=v_w_in, b_f=v_b_f, q_norm_g=v_q_norm_g, k_norm_g=v_k_norm_g, w_pool=v_w_pool, pool_scale=v_pool_scale, w_out=v_w_out)
    shard_cols = w_in.shape[2]
    shard_rows = w_out.shape[1]

    cols_first = lambda a: jnp.transpose(a, (2, 0, 1))
    w_in_t = cols_first(w_in)
    w_in_t_full, w_out_full = _gather_weights(w_in_t, w_out)
    wt_all = _to_aligned(w_in_t_full)
    loss, dx, gr, stacks = _local_step(x[0], loss_target[0], wt_all, w_out_full, norm_g, b_f, q_norm_g, k_norm_g, w_pool, pool_scale)
    loss = lax.psum(loss, ("x", "y", "c"))

    (g_in_mine, g_in_sib), g_w_out = _reduce_scatter(*stacks, shard_cols, shard_rows)
    small = _unpack_small(_all_reduce_small(_pack_small(gr)), {k: weights[k] for k in SMALL})
    grad_w = dict(small, w_out=g_w_out)

    names = ("norm_g", "w_in", "b_f", "q_norm_g", "k_norm_g", "w_pool", "pool_scale", "w_out")
    upd = {k: _adamw_nd(weights[k], grad_w[k], mom_m[k], mom_v[k]) for k in names if k != "w_in"}
    in_t = _adamw_halves(w_in_t, g_in_mine, g_in_sib, cols_first(mom_m["w_in"]), cols_first(mom_v["w_in"]))
    grad_w["w_in"], *upd["w_in"] = [jnp.transpose(a, (1, 2, 0)) for a in in_t]
    return (loss, dx[None], *[grad_w[k] for k in names], *[upd[k][0] for k in names], *[upd[k][1] for k in names], *[upd[k][2] for k in names])
```

```python
import functools

import jax
import jax.numpy as jnp
from jax import lax
from jax.experimental import pallas as pl
from jax.experimental.pallas import tpu as pltpu

F32 = jnp.float32
BF16 = jnp.bfloat16

DEPTH = 4
HEAD_DIM = 64
FOX_HEADS = 8
SB_HEADS = 4
FOX_W = FOX_HEADS * HEAD_DIM
SB_W = SB_HEADS * HEAD_DIM
POOL_W = 256
POOL_WINDOWS = (2, 4, 8, 16)
POOL_HALO = 16
D_MIX = FOX_W + POOL_W + SB_W
EPS = 1e-6
NEG = -1e30
QK_SCALE = HEAD_DIM ** -0.5

ORIG_FOX = 4 * FOX_W
ORIG_FF = ORIG_FOX
ORIG_REST = ORIG_FF + FOX_HEADS
D_IN = ORIG_REST + 2 * POOL_W + 4 * SB_W

C_FQ, C_FK, C_FV, C_FG = 0, FOX_W, 2 * FOX_W, 3 * FOX_W
C_PX = 4 * FOX_W
C_PG = C_PX + POOL_W
C_SQ = C_PG + POOL_W
C_SK, C_SV, C_SG = C_SQ + SB_W, C_SQ + 2 * SB_W, C_SQ + 3 * SB_W
PM = C_SG + SB_W
LANES = 128
PW = PM + LANES
FF_STRIDE = 8
AUG = 3

ADAM_LR = 0.001
ADAM_B1 = 0.9
ADAM_B2 = 0.999
ADAM_EPS = 1e-08
ADAM_WD = 0.01
ADAM_STEP = 10

VMEM_LIMIT = 48 * 1024 * 1024
PROJ_TN = PM // 2


def _cparams(**kw):
    return pltpu.CompilerParams(vmem_limit_bytes=VMEM_LIMIT, **kw)


def _dot(a, b):
    return jnp.dot(a, b, preferred_element_type=F32)


def _dot_nt(a, b):
    return lax.dot_general(a, b, (((1,), (1,)), ((), ())), preferred_element_type=F32)


def _dot_tn(a, b):
    return lax.dot_general(a, b, (((0,), (0,)), ((), ())), preferred_element_type=F32)


def _split2(x):
    hi = x.astype(BF16)
    lo = (x - hi.astype(F32)).astype(BF16)
    return hi, lo


def _split3(x):
    hi = x.astype(BF16)
    r = x - hi.astype(F32)
    mid = r.astype(BF16)
    lo = (r - mid.astype(F32)).astype(BF16)
    return hi, mid, lo


def _dot_exact_rhs(x, m):
    hi, mid, lo = _split3(x)
    return _dot(hi, m) + _dot(mid, m) + _dot(lo, m)


def _dot_exact_lhs(m, x):
    hi, mid, lo = _split3(x)
    return _dot(m, hi) + _dot(m, mid) + _dot(m, lo)


def _sigmoid(x):
    return 1.0 / (1.0 + jnp.exp(-x))


def _silu_pair(x):
    s = _sigmoid(x)
    return x * s, s * (1.0 + x * (1.0 - s))


def _iota(shape, dim):
    return lax.broadcasted_iota(jnp.int32, shape, dim)


def _ones_where(cond):
    return jnp.where(cond, 1.0, 0.0).astype(BF16)


def _head_blockdiag(w):
    return _ones_where((_iota((w, w), 0) >> 6) == (_iota((w, w), 1) >> 6))


def _group_sum(x, bd):
    hi, lo = _split2(x)
    return _dot(hi, bd) + _dot(lo, bd)


def _lane_pick(x, lane_idx, lane):
    return jnp.sum(jnp.where(lane_idx == lane, x, 0.0), axis=1, keepdims=True)


def _inproj(x, g, wt_all, layer, *, tm, tn):
    S, D = x.shape
    nj = PM // tn

    def body(x_ref, g_ref, w_ref, wff_ref, proj_ref, ff_ref, h_ref):
        @pl.when(pl.program_id(1) == 0)
        def _():
            xf = x_ref[...]
            ms = jnp.mean(xf * xf, axis=-1, keepdims=True)
            h = (xf * lax.rsqrt(ms + EPS) * g_ref[...]).astype(BF16)
            h_ref[...] = h
            ff_ref[...] = _dot_nt(h, wff_ref[...])

        proj_ref[...] = _dot_nt(h_ref[...], w_ref[...])

    return pl.pallas_call(
        body, name="inproj", grid=(S // tm, nj),
        in_specs=[pl.BlockSpec((tm, D), lambda i, j: (i, 0)),
                  pl.BlockSpec((1, D), lambda i, j: (0, 0)),
                  pl.BlockSpec((None, tn, D), lambda i, j: (layer, j, 0)),
                  pl.BlockSpec((None, LANES, D), lambda i, j: (layer, PM // LANES, 0))],
        out_specs=[pl.BlockSpec((tm, tn), lambda i, j: (i, j)),
                   pl.BlockSpec((tm, LANES), lambda i, j: (i, 0)),
                   pl.BlockSpec((tm, D), lambda i, j: (i, 0))],
        out_shape=[jax.ShapeDtypeStruct((S, PM), F32), jax.ShapeDtypeStruct((S, LANES), F32),
                   jax.ShapeDtypeStruct((S, D), BF16)],
        compiler_params=_cparams(dimension_semantics=("arbitrary", "arbitrary")),
    )(x, g, wt_all, wt_all)


def _pool_group_select(lane_group, vals):
    return jnp.where(lane_group == 0, vals[0], jnp.where(lane_group == 1, vals[1], jnp.where(lane_group == 2, vals[2], vals[3])))


def _prep(projm, ffo, qg, kg, bfp, wpd, ps, *, ts):
    S = projm.shape[0]
    nb = S // ts
    hb = ts // POOL_HALO

    def body(fq_ref, fk_ref, fv_ref, pp_ref, halo_ref, ff_ref, sq_ref, sk_ref, sv_ref,
             qg_ref, kg_ref, bf_ref, wpd_ref, ps_ref,
             qn_ref, ka_ref, kb_ref, v_ref, sqo_ref, sko_ref, svo_ref, pooled_ref, yp_ref, pm_ref,
             carry_ref, c_ref, buf_ref):
        i = pl.program_id(0)
        bd = _head_blockdiag(FOX_W)
        normed = []
        for src, g_ref in ((fq_ref, qg_ref), (fk_ref, kg_ref)):
            q = src[...]
            ss = _group_sum(q * q, bd)
            normed.append(q * lax.rsqrt(ss * (1.0 / HEAD_DIM) + EPS) * g_ref[...])
        qn_ref[...] = (normed[0] * QK_SCALE).astype(BF16)
        kn = normed[1]
        v_ref[...] = fv_ref[...].astype(BF16)
        sqo_ref[...] = (sq_ref[...] * QK_SCALE).astype(BF16)
        sko_ref[...] = sk_ref[...].astype(BF16)
        svo_ref[...] = sv_ref[...].astype(BF16)

        @pl.when(i == 0)
        def _():
            carry_ref[...] = jnp.zeros_like(carry_ref)

        z = ff_ref[...] + bf_ref[...]
        lf = jnp.minimum(z, 0.0) - jnp.log(1.0 + jnp.exp(-jnp.abs(z)))
        tri = _ones_where(_iota((ts, ts), 1) <= _iota((ts, ts), 0))
        c = _dot_exact_lhs(tri, lf) + carry_ref[...]
        c_ref[...] = c
        carry_ref[...] = c_ref[ts - 1:ts, :]
        parts = jnp.concatenate(_split3(-c), axis=1)
        row = _iota((AUG * LANES, FOX_W), 0)
        col = _iota((AUG * LANES, FOX_W), 1)
        part, src = row >> 7, row & (LANES - 1)
        pair, off = col >> 7, col & (LANES - 1)
        sel_a = _ones_where((src == FF_STRIDE * pair) & (off == HEAD_DIM + part))
        sel_b = _ones_where((src == FF_STRIDE * pair + 1) & (off == part))
        first_half = (_iota((1, FOX_W), 1) & HEAD_DIM) == 0
        ka_ref[...] = jnp.where(first_half, kn, _dot(parts, sel_a)).astype(BF16)
        kb_ref[...] = jnp.where(first_half, _dot(parts, sel_b), kn).astype(BF16)

        x = pp_ref[:, 0:POOL_W]
        pg = pp_ref[:, POOL_W:2 * POOL_W]
        halo = jnp.where(i > 0, halo_ref[:, 0:POOL_W], 0.0)
        buf_ref[0:POOL_HALO, :] = halo
        buf_ref[POOL_HALO:POOL_HALO + ts, :] = x
        acc = x
        snaps = []
        for d in range(1, POOL_HALO):
            acc = acc + buf_ref[pl.ds(POOL_HALO - d, ts), :]
            if d + 1 in POOL_WINDOWS:
                snaps.append(acc)
        lane_group = _iota((1, POOL_W), 1) >> 6
        wsum = _pool_group_select(lane_group, snaps)
        wlen = _pool_group_select(lane_group, [float(w) for w in POOL_WINDOWS])
        tpos = (i * ts + _iota((ts, 1), 0) + 1).astype(F32)
        pooled = wsum / jnp.minimum(tpos, wlen) - x
        pb = pooled.astype(BF16)
        pooled_ref[...] = pb
        yp = _dot(pb, wpd_ref[...])
        yp_ref[...] = yp
        pm_ref[...] = (yp * ps_ref[...] * (pg * _sigmoid(pg))).astype(BF16)

    blk = lambda w, c: pl.BlockSpec((ts, w), lambda i: (i, c))
    full = lambda a: pl.BlockSpec(a.shape, lambda i: (0,) * a.ndim)
    out_shapes = [
        jax.ShapeDtypeStruct((S, FOX_W), BF16), jax.ShapeDtypeStruct((S, FOX_W), BF16), jax.ShapeDtypeStruct((S, FOX_W), BF16),
        jax.ShapeDtypeStruct((S, FOX_W), BF16),
        jax.ShapeDtypeStruct((S, SB_W), BF16), jax.ShapeDtypeStruct((S, SB_W), BF16), jax.ShapeDtypeStruct((S, SB_W), BF16),
        jax.ShapeDtypeStruct((S, POOL_W), BF16), jax.ShapeDtypeStruct((S, POOL_W), F32), jax.ShapeDtypeStruct((S, POOL_W), BF16),
    ]
    out_specs = [
        blk(FOX_W, 0), blk(FOX_W, 0), blk(FOX_W, 0), blk(FOX_W, 0),
        blk(SB_W, 0), blk(SB_W, 0), blk(SB_W, 0),
        blk(POOL_W, 0), blk(POOL_W, 0), blk(POOL_W, 0),
    ]
    return pl.pallas_call(
        body, name="prep", grid=(nb,),
        in_specs=[blk(FOX_W, C_FQ // FOX_W), blk(FOX_W, C_FK // FOX_W), blk(FOX_W, C_FV // FOX_W), blk(2 * POOL_W, C_PX // (2 * POOL_W)),
                  pl.BlockSpec((POOL_HALO, 2 * POOL_W), lambda i: (jnp.maximum(i * hb - 1, 0), C_PX // (2 * POOL_W))),
                  blk(LANES, 0),
                  blk(SB_W, C_SQ // SB_W), blk(SB_W, C_SK // SB_W), blk(SB_W, C_SV // SB_W),
                  full(qg), full(kg), full(bfp), full(wpd), full(ps)],
        out_specs=out_specs, out_shape=out_shapes,
        scratch_shapes=[pltpu.VMEM((1, LANES), F32), pltpu.VMEM((ts, LANES), F32), pltpu.VMEM((ts + POOL_HALO, POOL_W), F32)],
        compiler_params=_cparams(dimension_semantics=("arbitrary",)),
    )(projm, projm, projm, projm, projm, ffo, projm, projm, projm, qg, kg, bfp, wpd, ps)


def _pair_masks(x):
    ma = _iota((1, LANES), 1) < HEAD_DIM
    zero = jnp.zeros_like(x)
    return jnp.where(ma, x, zero), jnp.where(ma, zero, x)


def _aug_queries(q):
    lane = _iota((1, LANES), 1)
    one = jnp.ones_like(q)
    zero = jnp.zeros_like(q)
    qa = jnp.where(lane < HEAD_DIM, q, jnp.where(lane < HEAD_DIM + AUG, one, zero))
    qb = jnp.where(lane >= HEAD_DIM, q, jnp.where(lane < AUG, one, zero))
    return qa, qb


def _fox_fwd(qn, ka, kb, v, projm, *, tq, tk):
    S = qn.shape[0]
    npair = FOX_HEADS // 2

    def body(q_ref, ka_ref, kb_ref, v_ref, fg_ref, o_ref, lse_ref, fm_ref):
        qi = pl.program_id(1)
        lane = _iota((1, LANES), 1)
        ma = lane < HEAD_DIM
        qaug = _aug_queries(q_ref[...])
        k_refs = (ka_ref, kb_ref)

        def block(j, carry, masked):
            k0 = pl.multiple_of(j * tk, tk)
            vb = v_ref[pl.ds(k0, tk), :]
            if masked:
                mask = (k0 + _iota((tq, tk), 1)) <= (qi * tq + _iota((tq, tk), 0))
            scores = [_dot_nt(qaug[h], k_refs[h][pl.ds(k0, tk), :]) for h in range(2)]
            new = []
            for h in range(2):
                m, l, acc = carry[h]
                s = jnp.where(mask, scores[h], NEG) if masked else scores[h]
                m_new = jnp.maximum(m, jnp.max(s, axis=1, keepdims=True))
                alpha = jnp.exp(m - m_new)
                p = jnp.exp(s - m_new)
                new.append((m_new, alpha * l + jnp.sum(p, axis=1, keepdims=True), alpha * acc + _dot(p.astype(BF16), vb)))
            return tuple(new)

        init = tuple((jnp.full((tq, 1), NEG, F32), jnp.zeros((tq, 1), F32), jnp.zeros((tq, LANES), F32)) for _ in range(2))
        nfull = (qi * tq) // tk
        carry = lax.fori_loop(0, nfull, lambda j, c: block(j, c, False), init)
        for mi in range(max(1, tq // tk)):
            carry = block(nfull + mi, carry, True)
        (ma_, la, acca), (mb_, lb, accb) = carry
        o = jnp.where(ma, acca / la, accb / lb)
        o_ref[...] = o
        lse_ref[...] = jnp.where(ma, ma_ + jnp.log(la), mb_ + jnp.log(lb))
        fg = fg_ref[...]
        fm_ref[...] = (o * (fg * _sigmoid(fg))).astype(BF16)

    qblk = pl.BlockSpec((tq, LANES), lambda p, i: (i, p))
    kvblk = pl.BlockSpec((S, LANES), lambda p, i: (0, p))
    return pl.pallas_call(
        body, name="fox_fwd", grid=(npair, S // tq),
        in_specs=[qblk, kvblk, kvblk, kvblk,
                  pl.BlockSpec((tq, LANES), lambda p, i: (i, C_FG // LANES + p))],
        out_specs=[qblk, qblk, qblk],
        out_shape=[jax.ShapeDtypeStruct((S, FOX_W), F32), jax.ShapeDtypeStruct((S, FOX_W), F32), jax.ShapeDtypeStruct((S, FOX_W), BF16)],
        compiler_params=_cparams(dimension_semantics=("arbitrary", "arbitrary")),
    )(qn, ka, kb, v, projm)


def _suffix_sums(x, tmat2):
    return _dot(jnp.concatenate(_split2(x), axis=1), tmat2)


def _suffix_matrix(tk, inclusive):
    rr, cc = _iota((2 * tk, tk), 0) & (tk - 1), _iota((2 * tk, tk), 1)
    return _ones_where(rr >= cc) if inclusive else _ones_where(rr > cc)


def _sb_scores(qh, kb, causal, tmat2, r_runs):
    heads = range(2)
    zs = [_dot_nt(qh[h], kb) for h in heads]
    nsps = [jnp.minimum(-z, 0.0) - jnp.log(1.0 + jnp.exp(-jnp.abs(z))) for z in zs]
    lbs = nsps if causal is None else [jnp.where(causal, n, 0.0) for n in nsps]
    rins = [_suffix_sums(lb, tmat2) for lb in lbs]
    args = [zs[h] + lbs[h] + (rins[h] + r_runs[h]) for h in heads]
    a_s = [jnp.exp(arg if causal is None else jnp.where(causal, arg, NEG)) for arg in args]
    return zs, nsps, lbs, a_s


def _sb_fwd(sq, sk, sv, projm, *, tq, tk):
    S = sq.shape[0]
    npair = SB_HEADS // 2

    def body(q_ref, k_ref, v_ref, sg_ref, o_ref, sm_ref):
        qi = pl.program_id(1)
        lane = _iota((1, LANES), 1)
        ma = lane < HEAD_DIM
        qh = _pair_masks(q_ref[...])
        tmat2 = _suffix_matrix(tk, inclusive=False)
        nfull = (qi * tq) // tk

        def block(j, carry, masked):
            k0 = pl.multiple_of(j * tk, tk)
            kb = k_ref[pl.ds(k0, tk), :]
            vb = v_ref[pl.ds(k0, tk), :]
            causal = (k0 + _iota((tq, tk), 1)) < (qi * tq + _iota((tq, tk), 0)) if masked else None
            _, _, lbs, a_s = _sb_scores(qh, kb, causal, tmat2, [carry[h][0] for h in range(2)])
            pv = _dot(jnp.concatenate([a.astype(BF16) for a in a_s], axis=0), vb)
            return tuple((carry[h][0] + jnp.sum(lbs[h], axis=1, keepdims=True), carry[h][1] + pv[h * tq:(h + 1) * tq]) for h in range(2))

        init = tuple((jnp.zeros((tq, 1), F32), jnp.zeros((tq, LANES), F32)) for _ in range(2))
        carry = init
        for mi in reversed(range(max(1, tq // tk))):
            carry = block(nfull + mi, carry, True)
        (_, acca), (_, accb) = lax.fori_loop(0, nfull, lambda jj, c: block(nfull - 1 - jj, c, False), carry)
        o = jnp.where(ma, acca, accb)
        o_ref[...] = o
        sg = sg_ref[...]
        sm_ref[...] = (o * (sg * _sigmoid(sg))).astype(BF16)

    qblk = pl.BlockSpec((tq, LANES), lambda p, i: (i, p))
    kvblk = pl.BlockSpec((S, LANES), lambda p, i: (0, p))
    return pl.pallas_call(
        body, name="sb_fwd", grid=(npair, S // tq),
        in_specs=[qblk, kvblk, kvblk, pl.BlockSpec((tq, LANES), lambda p, i: (i, C_SG // LANES + p))],
        out_specs=[qblk, qblk],
        out_shape=[jax.ShapeDtypeStruct((S, SB_W), F32), jax.ShapeDtypeStruct((S, SB_W), BF16)],
        compiler_params=_cparams(dimension_semantics=("arbitrary", "arbitrary")),
    )(sq, sk, sv, projm)


def _outproj(x, fm, pm, sm, w_out, layer, *, tm):
    S, D = x.shape

    def body(x_ref, fm_ref, pm_ref, sm_ref, w_ref, y_ref):
        y = x_ref[...] + _dot(fm_ref[...], w_ref[0:FOX_W, :])
        y = y + _dot(pm_ref[...], w_ref[FOX_W:FOX_W + POOL_W, :])
        y_ref[...] = y + _dot(sm_ref[...], w_ref[FOX_W + POOL_W:D_MIX, :])

    row = lambda w: pl.BlockSpec((tm, w), lambda i: (i, 0))
    return pl.pallas_call(
        body, name="outproj", grid=(S // tm,),
        in_specs=[row(D), row(FOX_W), row(POOL_W), row(SB_W), pl.BlockSpec((None, D_MIX, D), lambda i: (layer, 0, 0))],
        out_specs=row(D), out_shape=jax.ShapeDtypeStruct((S, D), F32),
        compiler_params=_cparams(dimension_semantics=("arbitrary",)),
    )(x, fm, pm, sm, w_out)


def _loss_head(y, target, *, tm):
    S, D = y.shape

    def body(y_ref, t_ref, dy_ref, sq_ref):
        @pl.when(pl.program_id(0) == 0)
        def _():
            sq_ref[...] = jnp.zeros_like(sq_ref)

        d = y_ref[...] - t_ref[...]
        dy_ref[...] = d * (1.0 / D)
        sq_ref[...] += jnp.sum(d * d, axis=0, keepdims=True)

    row = pl.BlockSpec((tm, D), lambda i: (i, 0))
    return pl.pallas_call(
        body, name="loss_head", grid=(S // tm,),
        in_specs=[row, row], out_specs=[row, pl.BlockSpec((1, D), lambda i: (0, 0))],
        out_shape=[jax.ShapeDtypeStruct((S, D), F32), jax.ShapeDtypeStruct((1, D), F32)],
        compiler_params=_cparams(dimension_semantics=("arbitrary",)),
    )(y, target)


def _outproj_bwd(dy, fm, pm, sm, w_out, layer, stacks, *, tm):
    S, D = dy.shape

    def body(dy_ref, fm_ref, pm_ref, sm_ref, w_ref, dm_ref, dw_ref):
        @pl.when(pl.program_id(0) == 0)
        def _():
            dw_ref[...] = jnp.zeros_like(dw_ref)

        dyb = dy_ref[...].astype(BF16)
        dm_ref[...] = _dot_nt(dyb, w_ref[...])
        dw_ref[0:FOX_W, :] += _dot_tn(fm_ref[...], dyb)
        dw_ref[FOX_W:FOX_W + POOL_W, :] += _dot_tn(pm_ref[...], dyb)
        dw_ref[FOX_W + POOL_W:D_MIX, :] += _dot_tn(sm_ref[...], dyb)

    row = lambda w: pl.BlockSpec((tm, w), lambda i: (i, 0))
    wspec = pl.BlockSpec((None, D_MIX, D), lambda i: (layer, 0, 0))
    return _stack_call(
        body, "outproj_bwd", (S // tm,), [row(D), row(FOX_W), row(POOL_W), row(SB_W), wspec], (dy, fm, pm, sm, w_out),
        [pl.BlockSpec((None, D_MIX, D), lambda i: (layer, 0, 0))], [(D_MIX, D)], stacks,
        plain_specs=[row(D_MIX)], plain_shapes=[jax.ShapeDtypeStruct((S, D_MIX), F32)],
        compiler_params=_cparams(dimension_semantics=("arbitrary",)))


def _fox_bwd(qn, ka, kb, v, o, lse, dmix, projm, *, tq, tk):
    S = qn.shape[0]
    npair = FOX_HEADS // 2

    def body(q_ref, ka_ref, kb_ref, v_ref, o_ref, lse_ref, dm_ref, fg_ref,
             dq_ref, dk_ref, dv_ref, dfg_ref, dct_ref, dcr_ref):
        qi = pl.program_id(1)

        @pl.when(qi == 0)
        def _():
            dk_ref[...] = jnp.zeros_like(dk_ref)
            dv_ref[...] = jnp.zeros_like(dv_ref)
            dct_ref[...] = jnp.zeros_like(dct_ref)

        lane = _iota((1, LANES), 1)
        ma = lane < HEAD_DIM
        qh = _pair_masks(q_ref[...])
        qaug = _aug_queries(q_ref[...])
        k_refs = (ka_ref, kb_ref)
        lsev = lse_ref[...]
        lse = (_lane_pick(lsev, lane, 0), _lane_pick(lsev, lane, HEAD_DIM))
        fg = fg_ref[...]
        silu, dsilu = _silu_pair(fg)
        dm = dm_ref[...]
        ov = o_ref[...]
        do = dm * silu
        dfg_ref[...] = dm * ov * dsilu
        dd = do * ov
        dsum = (jnp.sum(jnp.where(ma, dd, 0.0), axis=1, keepdims=True), jnp.sum(jnp.where(ma, 0.0, dd), axis=1, keepdims=True))
        doh = _pair_masks(do.astype(BF16))
        do2 = jnp.concatenate(doh, axis=0)
        q2 = jnp.concatenate(qh, axis=0)

        def block(j, carry, masked):
            dq = carry[0]
            rows = list(carry[1:])
            k0 = pl.multiple_of(j * tk, tk)
            vb = v_ref[pl.ds(k0, tk), :]
            if masked:
                mask = (k0 + _iota((tq, tk), 1)) <= (qi * tq + _iota((tq, tk), 0))
            heads = range(2)
            kaugs = [k_refs[h][pl.ds(k0, tk), :] for h in heads]
            scores = [_dot_nt(qaug[h], kaugs[h]) for h in heads]
            dps = [_dot_nt(doh[h], vb) for h in heads]
            ps, dss = [], []
            for h in heads:
                s = jnp.where(mask, scores[h], NEG) if masked else scores[h]
                p = jnp.exp(s - lse[h])
                dsf = p * (dps[h] - dsum[h])
                dct_ref[0, h:h + 1, pl.ds(k0, tk)] -= jnp.sum(dsf, axis=0, keepdims=True)
                rows[h] = rows[h] + jnp.sum(dsf, axis=1, keepdims=True)
                ps.append(p.astype(BF16))
                dss.append(dsf.astype(BF16))
            dv_ref[pl.ds(k0, tk), :] += _dot_tn(jnp.concatenate(ps, axis=0), do2)
            dk_ref[pl.ds(k0, tk), :] += _dot_tn(jnp.concatenate(dss, axis=0), q2)
            kh = jnp.concatenate([_pair_masks(kaugs[h])[h] for h in heads], axis=0)
            dq = dq + _dot(jnp.concatenate(dss, axis=1), kh)
            return (dq, rows[0], rows[1])

        zcol = jnp.zeros((tq, 1), F32)
        nfull = (qi * tq) // tk
        carry = lax.fori_loop(0, nfull, lambda j, c: block(j, c, False), (jnp.zeros((tq, LANES), F32), zcol, zcol))
        for mi in range(max(1, tq // tk)):
            carry = block(nfull + mi, carry, True)
        dq, rowa, rowb = carry
        dq_ref[...] = dq * QK_SCALE
        dcr_ref[0] = jnp.where(ma, rowa, rowb)

    qblk = pl.BlockSpec((tq, LANES), lambda p, i: (i, p))
    kvblk = pl.BlockSpec((S, LANES), lambda p, i: (0, p))
    f32out = jax.ShapeDtypeStruct((S, FOX_W), F32)
    ctblk = pl.BlockSpec((1, FF_STRIDE, S), lambda p, i: (p, 0, 0))
    return pl.pallas_call(
        body, name="fox_bwd", grid=(npair, S // tq),
        in_specs=[qblk, kvblk, kvblk, kvblk, qblk, qblk, qblk,
                  pl.BlockSpec((tq, LANES), lambda p, i: (i, C_FG // LANES + p))],
        out_specs=[qblk, kvblk, kvblk, qblk, ctblk, pl.BlockSpec((1, tq, LANES), lambda p, i: (p, i, 0))],
        out_shape=[f32out, f32out, f32out, f32out, jax.ShapeDtypeStruct((npair, FF_STRIDE, S), F32),
                   jax.ShapeDtypeStruct((npair, S, LANES), F32)],
        compiler_params=_cparams(dimension_semantics=("arbitrary", "arbitrary")),
    )(qn, ka, kb, v, o, lse, dmix, projm)


def _sb_bwd(sq, sk, sv, o, dmix, projm, *, tq, tk):
    S = sq.shape[0]
    npair = SB_HEADS // 2
    mix0 = (FOX_W + POOL_W) // LANES

    def body(q_ref, k_ref, v_ref, o_ref, dm_ref, sg_ref, dq_ref, dk_ref, dv_ref, dsg_ref):
        qi = pl.program_id(1)

        @pl.when(qi == 0)
        def _():
            dk_ref[...] = jnp.zeros_like(dk_ref)
            dv_ref[...] = jnp.zeros_like(dv_ref)

        lane = _iota((1, LANES), 1)
        ma = lane < HEAD_DIM
        qh = _pair_masks(q_ref[...])
        sg = sg_ref[...]
        silu, dsilu = _silu_pair(sg)
        dm = dm_ref[...]
        ov = o_ref[...]
        do = dm * silu
        dsg_ref[...] = dm * ov * dsilu
        dob = do.astype(BF16)
        dd = dob.astype(F32) * ov
        dsum = (jnp.sum(jnp.where(ma, dd, 0.0), axis=1, keepdims=True), jnp.sum(jnp.where(ma, 0.0, dd), axis=1, keepdims=True))
        doh = _pair_masks(dob)
        do2 = jnp.concatenate(doh, axis=0)
        q2 = jnp.concatenate(qh, axis=0)
        tmat2 = _suffix_matrix(tk, inclusive=False)
        tmat2_inc = _suffix_matrix(tk, inclusive=True)
        nfull = (qi * tq) // tk

        def block(j, carry, masked):
            dq = carry[2]
            k0 = pl.multiple_of(j * tk, tk)
            kb = k_ref[pl.ds(k0, tk), :]
            vb = v_ref[pl.ds(k0, tk), :]
            kh = _pair_masks(kb)
            causal = (k0 + _iota((tq, tk), 1)) < (qi * tq + _iota((tq, tk), 0)) if masked else None
            heads = range(2)
            das = [_dot_nt(doh[h], vb) for h in heads]
            zs, nsps, lbs, a_s = _sb_scores(qh, kb, causal, tmat2, [carry[h][0] for h in heads])
            abs_ = [a.astype(BF16) for a in a_s]
            us = [abs_[h].astype(F32) * das[h] for h in heads]
            uins = [_suffix_sums(u, tmat2_inc) for u in us]
            dzs = []
            for h in heads:
                cum_u = dsum[h] - (uins[h] + carry[h][1])
                dz = us[h] * jnp.exp(nsps[h]) - jnp.exp(zs[h] + nsps[h]) * cum_u
                if masked:
                    dz = jnp.where(causal, dz, 0.0)
                dzs.append(dz.astype(BF16))
            dv_ref[pl.ds(k0, tk), :] += _dot_tn(jnp.concatenate(abs_, axis=0), do2)
            dk_ref[pl.ds(k0, tk), :] += _dot_tn(jnp.concatenate(dzs, axis=0), q2)
            dq = dq + _dot(jnp.concatenate(dzs, axis=1), jnp.concatenate(kh, axis=0))
            new = [(carry[h][0] + jnp.sum(lbs[h], axis=1, keepdims=True), carry[h][1] + jnp.sum(us[h], axis=1, keepdims=True)) for h in heads]
            return (new[0], new[1], dq)

        zcol = jnp.zeros((tq, 1), F32)
        init = ((zcol, zcol), (zcol, zcol), jnp.zeros((tq, LANES), F32))
        carry = init
        for mi in reversed(range(max(1, tq // tk))):
            carry = block(nfull + mi, carry, True)
        dq = lax.fori_loop(0, nfull, lambda jj, c: block(nfull - 1 - jj, c, False), carry)[2]
        dq_ref[...] = dq * QK_SCALE

    qblk = pl.BlockSpec((tq, LANES), lambda p, i: (i, p))
    kvblk = pl.BlockSpec((S, LANES), lambda p, i: (0, p))
    f32out = jax.ShapeDtypeStruct((S, SB_W), F32)
    return pl.pallas_call(
        body, name="sb_bwd", grid=(npair, S // tq),
        in_specs=[qblk, kvblk, kvblk, qblk,
                  pl.BlockSpec((tq, LANES), lambda p, i: (i, mix0 + p)),
                  pl.BlockSpec((tq, LANES), lambda p, i: (i, C_SG // LANES + p))],
        out_specs=[qblk, kvblk, kvblk, qblk],
        out_shape=[f32out, f32out, f32out, f32out],
        compiler_params=_cparams(dimension_semantics=("arbitrary", "arbitrary")),
    )(sq, sk, sv, o, dmix, projm)


def _prep_bwd(projm, ffo, dqn, dkn, dct, dcr, dv, dfg, dsq, dsk, dsv, dsg, dmix, pooled, yp, qg, kg, bfp, wpd, ps, *, ts):
    S = projm.shape[0]
    nb = S // ts
    hb = ts // POOL_HALO
    npair = FOX_HEADS // 2
    last_halo = S // POOL_HALO - 1

    def body(fq_ref, fk_ref, pp_ref, pph_ref, ff_ref,
             dqn_ref, dkn_ref, dct_ref, dcr_ref, dv_ref, dfg_ref, dsq_ref, dsk_ref, dsv_ref, dsg_ref,
             dmp_ref, dmh_ref, pooled_ref, yp_ref, qg_ref, kg_ref, bf_ref, wpd_ref, ps_ref,
             dp_ref, dqg_ref, dkg_ref, dbf_ref, dwp_ref, dps_ref,
             carry_ref, dl_ref, buf_ref, dct_s):
        i = pl.program_id(0)
        blk = nb - 1 - i

        @pl.when(i == 0)
        def _():
            carry_ref[...] = jnp.zeros_like(carry_ref)
            dqg_ref[...] = jnp.zeros_like(dqg_ref)
            dkg_ref[...] = jnp.zeros_like(dkg_ref)
            dbf_ref[...] = jnp.zeros_like(dbf_ref)
            dwp_ref[...] = jnp.zeros_like(dwp_ref)
            dps_ref[...] = jnp.zeros_like(dps_ref)

        bd = _head_blockdiag(FOX_W)
        for raw_ref, g_ref, dn, dg_ref, col in ((fq_ref, qg_ref, dqn_ref[...], dqg_ref, C_FQ), (fk_ref, kg_ref, dkn_ref[...], dkg_ref, C_FK)):
            q = raw_ref[...]
            rstd = lax.rsqrt(_group_sum(q * q, bd) * (1.0 / HEAD_DIM) + EPS)
            xhat = q * rstd
            dg_ref[...] += jnp.sum(dn * xhat, axis=0, keepdims=True)
            dyg = dn * g_ref[...]
            mean = _group_sum(dyg * xhat, bd) * (1.0 / HEAD_DIM)
            dp_ref[:, col:col + FOX_W] = (rstd * (dyg - xhat * mean)).astype(BF16)
        dp_ref[:, C_FV:C_FV + FOX_W] = dv_ref[...].astype(BF16)
        dp_ref[:, C_FG:C_FG + FOX_W] = dfg_ref[...].astype(BF16)
        dp_ref[:, C_SQ:C_SQ + SB_W] = dsq_ref[...].astype(BF16)
        dp_ref[:, C_SK:C_SK + SB_W] = dsk_ref[...].astype(BF16)
        dp_ref[:, C_SV:C_SV + SB_W] = dsv_ref[...].astype(BF16)
        dp_ref[:, C_SG:C_SG + SB_W] = dsg_ref[...].astype(BF16)

        dct_s[...] = jnp.zeros_like(dct_s)
        for p in range(npair):
            dct_s[FF_STRIDE * p:FF_STRIDE * (p + 1), :] = dct_ref[p]
        dc = dct_s[...].T
        lane = _iota((1, LANES), 1)
        for p in range(npair):
            dcr = dcr_ref[p]
            dc = dc + jnp.where(lane == FF_STRIDE * p, _lane_pick(dcr, lane, 0), 0.0)
            dc = dc + jnp.where(lane == FF_STRIDE * p + 1, _lane_pick(dcr, lane, HEAD_DIM), 0.0)
        triu = _ones_where(_iota((ts, ts), 1) >= _iota((ts, ts), 0))
        dlf = _dot_exact_lhs(triu, dc) + carry_ref[...]
        dl_ref[...] = dlf
        carry_ref[...] = dl_ref[0:1, :]
        z = ff_ref[...] + bf_ref[...]
        dff = dlf * (1.0 / (1.0 + jnp.exp(z)))
        dbf_ref[...] += jnp.sum(dff, axis=0, keepdims=True)
        dp_ref[:, PM:PW] = dff.astype(BF16)

        psv = ps_ref[...]
        wpdv = wpd_ref[...]
        lane_group = _iota((1, POOL_W), 1) >> 6
        wlen = _pool_group_select(lane_group, [float(w) for w in POOL_WINDOWS])
        pg = pp_ref[:, POOL_W:2 * POOL_W]
        silu, dsilu = _silu_pair(pg)
        dmp = dmp_ref[...]
        ypv = yp_ref[...]
        dp_ref[:, C_PG:C_PG + POOL_W] = (dmp * (ypv * psv) * dsilu).astype(BF16)
        dps_ref[...] += jnp.sum(dmp * silu * ypv, axis=0, keepdims=True)
        dyp = (dmp * psv * silu).astype(BF16)
        dwp_ref[...] += _dot_tn(pooled_ref[...], dyp)
        dpooled = _dot_nt(dyp, wpdv)
        pgh = pph_ref[:, POOL_W:2 * POOL_W]
        dyph = (dmh_ref[...] * psv * (pgh * _sigmoid(pgh))).astype(BF16)
        dpooled_h = jnp.where(blk < nb - 1, _dot_nt(dyph, wpdv), 0.0)
        tpos = (blk * ts + _iota((ts, 1), 0) + 1).astype(F32)
        ev = dpooled / jnp.minimum(tpos, wlen)
        buf_ref[0:ts, :] = ev
        buf_ref[ts:ts + POOL_HALO, :] = dpooled_h / wlen
        acc = ev
        snaps = []
        for d in range(1, POOL_HALO):
            acc = acc + buf_ref[pl.ds(d, ts), :]
            if d + 1 in POOL_WINDOWS:
                snaps.append(acc)
        dp_ref[:, C_PX:C_PX + POOL_W] = (_pool_group_select(lane_group, snaps) - dpooled).astype(BF16)

    rblk = lambda w, c: pl.BlockSpec((ts, w), lambda i: (nb - 1 - i, c))
    full = lambda a: pl.BlockSpec(a.shape, lambda i: (0,) * a.ndim)
    halo = lambda w, c: pl.BlockSpec((POOL_HALO, w), lambda i: (jnp.minimum((nb - i) * hb, last_halo), c))
    acc_spec = lambda r, w: pl.BlockSpec((r, w), lambda i: (0, 0))
    return pl.pallas_call(
        body, name="prep_bwd", grid=(nb,),
        in_specs=[rblk(FOX_W, C_FQ // FOX_W), rblk(FOX_W, C_FK // FOX_W), rblk(2 * POOL_W, C_PX // (2 * POOL_W)),
                  halo(2 * POOL_W, C_PX // (2 * POOL_W)), rblk(LANES, 0),
                  rblk(FOX_W, 0), rblk(FOX_W, 0), pl.BlockSpec((npair, FF_STRIDE, ts), lambda i: (0, 0, nb - 1 - i)),
                  pl.BlockSpec((npair, ts, LANES), lambda i: (0, nb - 1 - i, 0)), rblk(FOX_W, 0), rblk(FOX_W, 0),
                  rblk(SB_W, 0), rblk(SB_W, 0), rblk(SB_W, 0), rblk(SB_W, 0),
                  rblk(POOL_W, FOX_W // POOL_W), halo(POOL_W, FOX_W // POOL_W), rblk(POOL_W, 0), rblk(POOL_W, 0),
                  full(qg), full(kg), full(bfp), full(wpd), full(ps)],
        out_specs=[rblk(PW, 0), acc_spec(1, FOX_W), acc_spec(1, FOX_W), acc_spec(1, LANES), acc_spec(POOL_W, POOL_W), acc_spec(1, POOL_W)],
        out_shape=[jax.ShapeDtypeStruct((S, PW), BF16), jax.ShapeDtypeStruct((1, FOX_W), F32), jax.ShapeDtypeStruct((1, FOX_W), F32),
                   jax.ShapeDtypeStruct((1, LANES), F32), jax.ShapeDtypeStruct((POOL_W, POOL_W), F32), jax.ShapeDtypeStruct((1, POOL_W), F32)],
        scratch_shapes=[pltpu.VMEM((1, LANES), F32), pltpu.VMEM((ts, LANES), F32), pltpu.VMEM((ts + POOL_HALO, POOL_W), F32),
                        pltpu.VMEM((LANES, ts), F32)],
        compiler_params=_cparams(dimension_semantics=("arbitrary",)),
    )(projm, projm, projm, projm, ffo, dqn, dkn, dct, dcr, dv, dfg, dsq, dsk, dsv, dsg, dmix, dmix, pooled, yp, qg, kg, bfp, wpd, ps)


def _stack_call(body, name, grid, in_specs, operands, slot_specs, slot_shapes, stacks, plain_specs=(), plain_shapes=(), **kw):
    out_specs = list(plain_specs) + list(slot_specs)
    out_shape = list(plain_shapes) + [jax.ShapeDtypeStruct((DEPTH,) + s, F32) for s in slot_shapes]
    if stacks is None:
        return pl.pallas_call(body, name=name, grid=grid, in_specs=in_specs, out_specs=out_specs, out_shape=out_shape, **kw)(*operands)
    n = len(operands)

    def aliased_body(*refs):
        body(*refs[:n], *refs[n + len(stacks):])

    return pl.pallas_call(
        aliased_body, name=name, grid=grid, in_specs=list(in_specs) + [pl.BlockSpec(memory_space=pl.ANY)] * len(stacks),
        out_specs=out_specs, out_shape=out_shape,
        input_output_aliases={n + k: len(plain_specs) + k for k in range(len(stacks))}, **kw)(*operands, *stacks)


def _inproj_dw(h, dproj, layer, stacks, *, ts, tn):
    S, D = h.shape
    nj = PM // tn

    def body(h_ref, dp_ref, dpf_ref, dw_ref, dwf_ref):
        s = pl.program_id(1)

        @pl.when(s == 0)
        def _():
            dw_ref[...] = jnp.zeros_like(dw_ref)

        @pl.when((s == 0) & (pl.program_id(0) == 0))
        def _():
            dwf_ref[...] = jnp.zeros_like(dwf_ref)

        hv = h_ref[...]
        dw_ref[...] += _dot_tn(dp_ref[...], hv)

        @pl.when(pl.program_id(0) == 0)
        def _():
            dwf_ref[...] += _dot_tn(dpf_ref[...], hv)

    return _stack_call(
        body, "inproj_dw", (nj, S // ts),
        [pl.BlockSpec((ts, D), lambda j, s: (s, 0)),
         pl.BlockSpec((ts, tn), lambda j, s: (s, j)),
         pl.BlockSpec((ts, LANES), lambda j, s: (s, PM // LANES))],
        (h, dproj, dproj),
        [pl.BlockSpec((None, tn, D), lambda j, s: (layer, j, 0)), pl.BlockSpec((None, LANES, D), lambda j, s: (layer, 0, 0))],
        [(PM, D), (LANES, D)], stacks,
        compiler_params=_cparams(dimension_semantics=("arbitrary", "arbitrary")))


def _inproj_dx(dproj, wt_all, layer, x, g, dy, *, tm):
    S, D = x.shape

    def body(dp_ref, w_ref, x_ref, g_ref, dy_ref, dx_ref, dg_ref):
        @pl.when(pl.program_id(0) == 0)
        def _():
            dg_ref[...] = jnp.zeros_like(dg_ref)

        dh = _dot(dp_ref[...], w_ref[...])
        xf = x_ref[...]
        rstd = lax.rsqrt(jnp.mean(xf * xf, axis=-1, keepdims=True) + EPS)
        xhat = xf * rstd
        dg_ref[...] += jnp.sum(dh * xhat, axis=0, keepdims=True)
        dyg = dh * g_ref[...]
        mean = jnp.mean(dyg * xhat, axis=-1, keepdims=True)
        dx_ref[...] = rstd * (dyg - xhat * mean) + dy_ref[...]

    row = lambda w: pl.BlockSpec((tm, w), lambda i: (i, 0))
    return pl.pallas_call(
        body, name="inproj_dx", grid=(S // tm,),
        in_specs=[row(PW), pl.BlockSpec((None, PW, D), lambda i: (layer, 0, 0)), row(D), pl.BlockSpec((1, D), lambda i: (0, 0)), row(D)],
        out_specs=[row(D), pl.BlockSpec((1, D), lambda i: (0, 0))],
        out_shape=[jax.ShapeDtypeStruct((S, D), F32), jax.ShapeDtypeStruct((1, D), F32)],
        compiler_params=_cparams(dimension_semantics=("arbitrary",)),
    )(dproj, wt_all, x, g, dy)


def _adam_update(w, g, m, v):
    nm = ADAM_B1 * m + (1.0 - ADAM_B1) * g
    nv = ADAM_B2 * v + (1.0 - ADAM_B2) * (g * g)
    m_hat = nm / (1.0 - ADAM_B1 ** ADAM_STEP)
    v_hat = nv / (1.0 - ADAM_B2 ** ADAM_STEP)
    return -ADAM_LR * (m_hat / (jnp.sqrt(v_hat) + ADAM_EPS) + ADAM_WD * w), nm, nv


def _adamw(w, g, m, v):
    L, R, C = w.shape
    tr = R if R <= 512 else 256

    def body(w_ref, g_ref, m_ref, v_ref, d_ref, nm_ref, nv_ref):
        d_ref[...], nm_ref[...], nv_ref[...] = _adam_update(w_ref[...], g_ref[...], m_ref[...], v_ref[...])

    spec = pl.BlockSpec((1, tr, C), lambda l, i: (l, i, 0))
    shp = jax.ShapeDtypeStruct((L, R, C), F32)
    return pl.pallas_call(
        body, name="adamw", grid=(L, R // tr), in_specs=[spec] * 4, out_specs=[spec] * 3, out_shape=[shp] * 3,
        compiler_params=_cparams(dimension_semantics=("arbitrary", "arbitrary")),
    )(w, g, m, v)


def _adamw_nd(w, g, m, v):
    shape = w.shape
    view = (1,) + shape if w.ndim == 2 else (shape[0], -1, shape[-1])
    outs = _adamw(w.reshape(view), g.reshape(view), m.reshape(view), v.reshape(view))
    return tuple(o.reshape(shape) for o in outs)


FLIP_C = (0, 0, 1)
FLIP_X = (1, 0, 0)
FLIP_Y = (0, 1, 0)
FLIP_XY = (1, 1, 0)
MESH = pl.DeviceIdType.MESH


def _peer(flip):
    me = (lax.axis_index("x"), lax.axis_index("y"), lax.axis_index("c"))
    return tuple(1 - a if f else a for a, f in zip(me, flip))


def _exchange(name, arrays, flips):
    n = len(arrays)

    def body(*refs):
        srcs, dsts = refs[:n], refs[n:2 * n]
        send_sems, recv_sems = refs[2 * n:]
        copies = [pltpu.make_async_remote_copy(src_ref=srcs[k], dst_ref=dsts[k], send_sem=send_sems.at[k], recv_sem=recv_sems.at[k],
                                               device_id=_peer(flips[k]), device_id_type=MESH) for k in range(n)]
        for cp in copies:
            cp.start()
        for cp in copies:
            cp.wait()

    anyspec = pl.BlockSpec(memory_space=pl.ANY)
    return pl.pallas_call(
        body, name=name, in_specs=[anyspec] * n, out_specs=[anyspec] * n,
        out_shape=[jax.ShapeDtypeStruct(a.shape, a.dtype) for a in arrays],
        scratch_shapes=[pltpu.SemaphoreType.DMA((n,)), pltpu.SemaphoreType.DMA((n,))],
    )(*arrays)


def _exchange_add(name, x, flip):
    def body(x_ref, o_ref, buf_ref, send_sem, recv_sem):
        cp = pltpu.make_async_remote_copy(src_ref=x_ref, dst_ref=buf_ref, send_sem=send_sem, recv_sem=recv_sem,
                                          device_id=_peer(flip), device_id_type=MESH)
        cp.start()
        cp.wait()
        o_ref[...] = x_ref[...] + buf_ref[...]

    vspec = pl.BlockSpec(memory_space=pltpu.VMEM)
    return pl.pallas_call(
        body, name=name, in_specs=[vspec], out_specs=vspec, out_shape=jax.ShapeDtypeStruct(x.shape, x.dtype),
        scratch_shapes=[pltpu.VMEM(x.shape, x.dtype), pltpu.SemaphoreType.DMA, pltpu.SemaphoreType.DMA],
    )(x)


def _chip_index():
    return 2 * lax.axis_index("x") + lax.axis_index("y")


def _gather_weights(w_in_t, w_out):
    wi = w_in_t.astype(BF16)
    wo = jnp.swapaxes(w_out, 0, 1).astype(BF16)
    halves = (wi.shape[0] // 2, wo.shape[0] // 2)
    masks = (2, 1, 3)
    flips = (FLIP_X, FLIP_Y, FLIP_XY)
    n_first = 2 * len(masks)

    def body(wi_ref, wo_ref, gi_ref, go_ref, send_sems, recv_sems):
        c = lax.axis_index("c")
        j = _chip_index()
        srcs = (wi_ref, wo_ref)
        dsts = (gi_ref, go_ref)
        mine = [pl.ds(h * c, h) for h in halves]
        theirs = [pl.ds(h * (1 - c), h) for h in halves]

        def copy(idx, src, dst, flip):
            return pltpu.make_async_remote_copy(src_ref=src, dst_ref=dst, send_sem=send_sems.at[idx], recv_sem=recv_sems.at[idx],
                                                device_id=_peer(flip), device_id_type=MESH)

        first = [copy(2 * k + a, srcs[a].at[mine[a]], dsts[a].at[j, mine[a]], flips[k]) for k in range(len(masks)) for a in range(2)]
        for cp in first:
            cp.start()
        passed = []
        for k, m in enumerate(masks):
            for a in range(2):
                slot = dsts[a].at[j ^ m, mine[a]]
                copy(2 * k + a, slot, slot, flips[k]).wait_recv()
                fwd = copy(n_first + 2 * k + a, slot, slot, FLIP_C)
                fwd.start()
                passed.append(fwd)
        for k, m in enumerate(masks):
            for a in range(2):
                slot = dsts[a].at[j ^ m, theirs[a]]
                copy(n_first + 2 * k + a, slot, slot, FLIP_C).wait_recv()
        for cp in first + passed:
            cp.wait_send()

    anyspec = pl.BlockSpec(memory_space=pl.ANY)
    gi, go = pl.pallas_call(
        body, name="gather_weights", in_specs=[anyspec] * 2, out_specs=[anyspec] * 2,
        out_shape=[jax.ShapeDtypeStruct((4,) + wi.shape, BF16), jax.ShapeDtypeStruct((4,) + wo.shape, BF16)],
        scratch_shapes=[pltpu.SemaphoreType.DMA((2 * n_first,)), pltpu.SemaphoreType.DMA((2 * n_first,))],
    )(wi, wo)
    own = lax.broadcasted_iota(jnp.int32, (4, 1, 1, 1), 0) == _chip_index()
    gi = jnp.where(own, wi[None], gi)
    go = jnp.where(own, wo[None], go)
    w_in_t_full = gi.reshape((4 * wi.shape[0],) + wi.shape[1:])
    w_out_full = jnp.swapaxes(go.reshape((4 * wo.shape[0],) + wo.shape[1:]), 0, 1)
    return w_in_t_full, w_out_full


def _to_aligned(w_t):
    _, L, D = w_t.shape
    npair = FOX_HEADS // 2
    ff = w_t[ORIG_FF:ORIG_REST].reshape(npair, 2, L, D)
    ff = jnp.pad(ff, ((0, 0), (0, FF_STRIDE - 2), (0, 0), (0, 0))).reshape(npair * FF_STRIDE, L, D)
    ff = jnp.pad(ff, ((0, LANES - npair * FF_STRIDE), (0, 0), (0, 0)))
    return jnp.swapaxes(jnp.concatenate([w_t[:ORIG_FOX], w_t[ORIG_REST:], ff], axis=0), 0, 1)


def _from_aligned(dw_t):
    n, _, D = dw_t.shape
    npair = FOX_HEADS // 2
    ff = dw_t[:, PM:PM + npair * FF_STRIDE].reshape(n, npair, FF_STRIDE, D)[:, :, :2].reshape(n, FOX_HEADS, D)
    return jnp.swapaxes(jnp.concatenate([dw_t[:, :ORIG_FOX], ff, dw_t[:, ORIG_FOX:PM]], axis=1), 0, 1)


def _half_layers(name, stack, got):
    L, R, C = stack.shape
    half = L // 2
    tr = min(256, R)
    c = lax.axis_index("c")
    which = ((1 - c) if got is None else c).astype(jnp.int32).reshape(1)

    def body(c_ref, x_ref, *refs):
        if got is None:
            refs[0][...] = x_ref[...].astype(BF16)
        else:
            acc = x_ref[...] + refs[0][...].astype(F32)
            refs[1][...] = acc
            refs[2][...] = acc.astype(BF16)

    plain = pl.BlockSpec((1, tr, C), lambda l, i, c_ref: (l, i, 0))
    picked = pl.BlockSpec((1, tr, C), lambda l, i, c_ref: (c_ref[0] * half + l, i, 0))
    shp = lambda dt: jax.ShapeDtypeStruct((half, R, C), dt)
    grid_spec = pltpu.PrefetchScalarGridSpec(
        num_scalar_prefetch=1, grid=(half, R // tr),
        in_specs=[picked] + ([] if got is None else [plain]), out_specs=[plain] if got is None else [plain, plain])
    return pl.pallas_call(
        body, name=name, grid_spec=grid_spec, out_shape=[shp(BF16)] if got is None else [shp(F32), shp(BF16)],
        compiler_params=_cparams(dimension_semantics=("arbitrary", "arbitrary")),
    )(which, stack, *([] if got is None else [got]))


def _reduce_scatter(stack_m, stack_f, stack_o, shard_cols, shard_rows):
    j = _chip_index()
    half = DEPTH // 2
    stacks = (stack_m, stack_f, stack_o)
    give = [_half_layers("rs_give", s, None)[0] for s in stacks]
    got = _exchange("rs_d2d", give, (FLIP_C,) * len(stacks))
    (m32, mbf), (f32_, fbf), (o32, obf) = [_half_layers("rs_add_chip", s, g) for s, g in zip(stacks, got)]
    d_model = stack_m.shape[2]

    def in_shards(m, f):
        return _from_aligned(jnp.concatenate([m, f], axis=1)).reshape(4, shard_cols, half, d_model)

    def out_shards(o):
        return jnp.moveaxis(o.reshape(half, 4, shard_rows, o.shape[-1]), 1, 0)

    chip = [(in_shards(m32, f32_), in_shards(mbf, fbf)), (out_shards(o32), out_shards(obf))]
    masks = (2, 1, 3)
    flips = (FLIP_X, FLIP_Y, FLIP_XY)
    sends, sflips = [], []
    for _, bf in chip:
        for m, fl in zip(masks, flips):
            sends.append(lax.dynamic_index_in_dim(bf, j ^ m, axis=0, keepdims=False))
            sflips.append(fl)
    got = _exchange("rs_ici", sends, tuple(sflips))
    own_in, own_out = [lax.dynamic_index_in_dim(f32_sum, j, axis=0, keepdims=False) for f32_sum, _ in chip]
    mine_in = _add_rows("rs_add_in", own_in, list(got[0:3]))
    mine_out = _add_into_half("rs_add_out", own_out, list(got[3:6]))
    sib_in, g_out = _share_halves(mine_in, mine_out)
    return (mine_in, sib_in), g_out


def _add_rows(name, first, others):
    n = len(others)

    def body(*refs):
        acc = refs[0][...]
        for r in refs[1:1 + n]:
            acc = acc + r[...].astype(F32)
        refs[1 + n][...] = acc

    grid, spec = _row_lane_blocks(first.shape)
    return pl.pallas_call(
        body, name=name, grid=grid, in_specs=[spec(first.shape[1])] * (1 + n), out_specs=spec(first.shape[1]),
        out_shape=jax.ShapeDtypeStruct(first.shape, F32),
        compiler_params=_cparams(dimension_semantics=("arbitrary", "arbitrary")),
    )(first, *others)


def _row_lane_blocks(shape):
    rows, _, C = shape
    tr = rows // 2 if rows % 2 == 0 and rows > 64 else rows
    return (rows // tr, C // LANES), lambda n_mid: pl.BlockSpec((tr, n_mid, LANES), lambda i, k, *_: (i, 0, k))


def _add_into_half(name, first, others):
    half, rows, C = first.shape
    tr = min(256, rows)
    n = len(others)

    def body(c_ref, *refs):
        acc = refs[0][...]
        for r in refs[1:1 + n]:
            acc = acc + r[...].astype(F32)
        refs[1 + n][...] = acc

    grid_spec = pltpu.PrefetchScalarGridSpec(
        num_scalar_prefetch=1, grid=(half, rows // tr),
        in_specs=[pl.BlockSpec((1, tr, C), lambda l, i, c_ref: (l, i, 0))] * (1 + n),
        out_specs=pl.BlockSpec((1, tr, C), lambda l, i, c_ref: (c_ref[0] * half + l, i, 0)))
    return pl.pallas_call(
        body, name=name, grid_spec=grid_spec, out_shape=jax.ShapeDtypeStruct((2 * half, rows, C), F32),
        compiler_params=_cparams(dimension_semantics=("arbitrary", "arbitrary")),
    )(lax.axis_index("c").astype(jnp.int32).reshape(1), first, *others)


def _share_halves(mine, buf):
    half = DEPTH // 2

    def body(mine_ref, buf_in, sib_ref, buf_ref, send_sems, recv_sems):
        lay = pl.ds(half * lax.axis_index("c"), half)
        copies = [pltpu.make_async_remote_copy(src_ref=src, dst_ref=dst, send_sem=send_sems.at[k], recv_sem=recv_sems.at[k],
                                               device_id=_peer(FLIP_C), device_id_type=MESH)
                  for k, (src, dst) in enumerate(((mine_ref, sib_ref), (buf_ref.at[lay], buf_ref.at[lay])))]
        for cp in copies:
            cp.start()
        for cp in copies:
            cp.wait()

    anyspec = pl.BlockSpec(memory_space=pl.ANY)
    return pl.pallas_call(
        body, name="rs_share", in_specs=[anyspec] * 2, out_specs=[anyspec] * 2,
        out_shape=[jax.ShapeDtypeStruct(mine.shape, mine.dtype), jax.ShapeDtypeStruct(buf.shape, buf.dtype)],
        input_output_aliases={1: 1},
        scratch_shapes=[pltpu.SemaphoreType.DMA((2,)), pltpu.SemaphoreType.DMA((2,))],
    )(mine, buf)


def _adamw_halves(w, g_mine, g_sib, m, v):
    half = g_mine.shape[1]

    def body(c_ref, w_ref, gm_ref, gs_ref, m_ref, v_ref, g_ref, d_ref, nm_ref, nv_ref):
        first = c_ref[0] == 0
        gm, gs = gm_ref[...], gs_ref[...]
        for h, gv in enumerate((jnp.where(first, gm, gs), jnp.where(first, gs, gm))):
            lay = slice(half * h, half * (h + 1))
            g_ref[:, lay, :] = gv
            d_ref[:, lay, :], nm_ref[:, lay, :], nv_ref[:, lay, :] = _adam_update(w_ref[:, lay, :], gv, m_ref[:, lay, :], v_ref[:, lay, :])

    grid, spec = _row_lane_blocks(w.shape)
    full, part = spec(w.shape[1]), spec(half)
    grid_spec = pltpu.PrefetchScalarGridSpec(num_scalar_prefetch=1, grid=grid, in_specs=[full, part, part, full, full], out_specs=[full] * 4)
    return pl.pallas_call(
        body, name="adamw_halves", grid_spec=grid_spec, out_shape=[jax.ShapeDtypeStruct(w.shape, F32)] * 4,
        compiler_params=_cparams(dimension_semantics=("arbitrary", "arbitrary")),
    )(lax.axis_index("c").astype(jnp.int32).reshape(1), w, g_mine, g_sib, m, v)


def _all_reduce_small(x):
    x = _exchange_add("ar_c", x, FLIP_C)
    x = _exchange_add("ar_y", x, FLIP_Y)
    return _exchange_add("ar_x", x, FLIP_X)


def _blocks(S):
    return dict(tm=min(512, S), tm_proj=min(1024, S), ts=min(512, S), tq=min(512, S), tk=min(512, S), tks=min(256, S))


def _pair_pad(vec):
    npair = FOX_HEADS // 2
    v = jnp.pad(vec.reshape(npair, 2), ((0, 0), (0, FF_STRIDE - 2))).reshape(1, npair * FF_STRIDE)
    return jnp.pad(v, ((0, 0), (0, LANES - npair * FF_STRIDE)))


def _pair_unpad(row):
    npair = FOX_HEADS // 2
    return row[0, :npair * FF_STRIDE].reshape(npair, FF_STRIDE)[:, :2].reshape(FOX_HEADS)


def _pool_blockdiag(w_pool):
    g, cg, _ = w_pool.shape
    eye = jnp.eye(g, dtype=w_pool.dtype)
    return jnp.einsum("gh,gcd->gchd", eye, w_pool).reshape(g * cg, g * cg)


def _layer_params(norm_g, b_f, q_norm_g, k_norm_g, w_pool, pool_scale):
    return dict(g=norm_g.reshape(1, -1), qg=jnp.tile(q_norm_g, FOX_HEADS).reshape(1, FOX_W), kg=jnp.tile(k_norm_g, FOX_HEADS).reshape(1, FOX_W),
                bfp=_pair_pad(b_f), wpd=_pool_blockdiag(w_pool).astype(BF16), ps=pool_scale.reshape(1, POOL_W))


def _layer_fwd(x, wt_all, w_out, layer, prm, bs):
    projm, ffo, h = _inproj(x, prm["g"], wt_all, layer, tm=bs["tm_proj"], tn=PROJ_TN)
    qn, ka, kb, v, sq, sk, sv, pooled, yp, pm = _prep(projm, ffo, prm["qg"], prm["kg"], prm["bfp"], prm["wpd"], prm["ps"], ts=bs["ts"])
    o, lse, fm = _fox_fwd(qn, ka, kb, v, projm, tq=bs["tq"], tk=bs["tk"])
    so, sm = _sb_fwd(sq, sk, sv, projm, tq=bs["tq"], tk=bs["tks"])
    y = _outproj(x, fm, pm, sm, w_out, layer, tm=bs["tm"])
    saved = dict(x=x, projm=projm, ffo=ffo, h=h, qn=qn, ka=ka, kb=kb, v=v, sq=sq, sk=sk, sv=sv, pooled=pooled, yp=yp,
                 o=o, lse=lse, so=so, fm=fm, pm=pm, sm=sm)
    return y, saved


def _layer_bwd(dy, wt_all, w_out, prm, sv_, bs, layer, stacks):
    dmix, stack_o = _outproj_bwd(dy, sv_["fm"], sv_["pm"], sv_["sm"], w_out, layer, None if stacks is None else stacks[2:], tm=bs["tm"])
    dqn, dkn, dv, dfg, dct, dcr = _fox_bwd(sv_["qn"], sv_["ka"], sv_["kb"], sv_["v"], sv_["o"], sv_["lse"], dmix, sv_["projm"],
                                      tq=bs["tq"], tk=bs["tk"])
    dsq, dsk, dsv, dsg = _sb_bwd(sv_["sq"], sv_["sk"], sv_["sv"], sv_["so"], dmix, sv_["projm"], tq=bs["tq"], tk=bs["tks"])
    dproj, dqg, dkg, dbf, dwp, dps = _prep_bwd(sv_["projm"], sv_["ffo"], dqn, dkn, dct, dcr, dv, dfg, dsq, dsk, dsv, dsg, dmix,
                                               sv_["pooled"], sv_["yp"], prm["qg"], prm["kg"], prm["bfp"], prm["wpd"], prm["ps"], ts=bs["ts"])
    stack_m, stack_f = _inproj_dw(sv_["h"], dproj, layer, None if stacks is None else stacks[:2], ts=bs["tm_proj"], tn=PROJ_TN)
    dx, dg = _inproj_dx(dproj, wt_all, layer, sv_["x"], prm["g"], dy, tm=min(256, bs["tm"]))
    grads = dict(
        norm_g=dg[0],
        b_f=_pair_unpad(dbf), q_norm_g=dqg.reshape(FOX_HEADS, HEAD_DIM).sum(0), k_norm_g=dkg.reshape(FOX_HEADS, HEAD_DIM).sum(0),
        w_pool=jnp.stack([dwp[HEAD_DIM * g:HEAD_DIM * (g + 1), HEAD_DIM * g:HEAD_DIM * (g + 1)] for g in range(4)]),
        pool_scale=dps[0])
    return dx, grads, (stack_m, stack_f, stack_o)


def _local_step(x, target, wt_all, w_out, norm_g, b_f, q_norm_g, k_norm_g, w_pool, pool_scale):
    S, D = x.shape
    bs = _blocks(S)
    prms = [_layer_params(norm_g[l], b_f[l], q_norm_g[l], k_norm_g[l], w_pool[l], pool_scale[l]) for l in range(DEPTH)]
    saved = []
    y = x
    for l in range(DEPTH):
        y, s_ = _layer_fwd(y, wt_all, w_out, l, prms[l], bs)
        saved.append(s_)
    dy, sq = _loss_head(y, target, tm=bs["tm"])
    loss = 0.5 * jnp.sum(sq) / D
    grads = [None] * DEPTH
    stacks = None
    for l in reversed(range(DEPTH)):
        dy, grads[l], stacks = _layer_bwd(dy, wt_all, w_out, prms[l], saved[l], bs, l, stacks)
    stacked = {k: jnp.stack([g[k] for g in grads]) for k in grads[0]}
    return loss, dy, stacked, stacks


SMALL = ("norm_g", "b_f", "q_norm_g", "k_norm_g", "w_pool", "pool_scale")


def _pack_small(gr):
    flat = jnp.concatenate([gr[k].reshape(-1) for k in SMALL])
    pad = (-flat.shape[0]) % (8 * LANES)
    return jnp.pad(flat, (0, pad)).reshape(-1, LANES)


def _unpack_small(packed, like):
    flat = packed.reshape(-1)
    out, off = {}, 0
    for k in SMALL:
        n = like[k].size
        out[k] = flat[off:off + n].reshape(like[k].shape)
        off += n
    return out


def kernel(x, norm_g, w_in, b_f, q_norm_g, k_norm_g, w_pool, pool_scale, w_out, loss_target, m_norm_g, m_w_in, m_b_f, m_q_norm_g, m_k_norm_g, m_w_pool, m_pool_scale, m_w_out, v_norm_g, v_w_in, v_b_f, v_q_norm_g, v_k_norm_g, v_w_pool, v_pool_scale, v_w_out):
    weights = dict(norm_g=norm_g, w_in=w_in, b_f=b_f, q_norm_g=q_norm_g, k_norm_g=k_norm_g, w_pool=w_pool, pool_scale=pool_scale, w_out=w_out)
    mom_m = dict(norm_g=m_norm_g, w_in=m_w_in, b_f=m_b_f, q_norm_g=m_q_norm_g, k_norm_g=m_k_norm_g, w_pool=m_w_pool, pool_scale=m_pool_scale, w_out=m_w_out)
    mom_v = dict(norm_g=v_norm_g, w_in=v_w_in, b_f=v_b_f, q_norm_g=v_q_norm_g, k_norm_g=v_k_norm_g, w_pool=v_w_pool, pool_scale=v_pool_scale, w_out=v_w_out)
    shard_cols = w_in.shape[2]
    shard_rows = w_out.shape[1]

    cols_first = lambda a: jnp.transpose(a, (2, 0, 1))
    w_in_t = cols_first(w_in)
    w_in_t_full, w_out_full = _gather_weights(w_in_t, w_out)
    wt_all = _to_aligned(w_in_t_full)
    loss, dx, gr, stacks = _local_step(x[0], loss_target[0], wt_all, w_out_full, norm_g, b_f, q_norm_g, k_norm_g, w_pool, pool_scale)
    loss = lax.psum(loss, ("x", "y", "c"))

    (g_in_mine, g_in_sib), g_w_out = _reduce_scatter(*stacks, shard_cols, shard_rows)
    small = _unpack_small(_all_reduce_small(_pack_small(gr)), {k: weights[k] for k in SMALL})
    grad_w = dict(small, w_out=g_w_out)

    names = ("norm_g", "w_in", "b_f", "q_norm_g", "k_norm_g", "w_pool", "pool_scale", "w_out")
    upd = {k: _adamw_nd(weights[k], grad_w[k], mom_m[k], mom_v[k]) for k in names if k != "w_in"}
    in_t = _adamw_halves(w_in_t, g_in_mine, g_in_sib, cols_first(mom_m["w_in"]), cols_first(mom_v["w_in"]))
    grad_w["w_in"], *upd["w_in"] = [jnp.transpose(a, (1, 2, 0)) for a in in_t]
    return (loss, dx[None], *[grad_w[k] for k in names], *[upd[k][0] for k in names], *[upd[k][1] for k in names], *[upd[k][2] for k in names])
```

```python
import functools

import jax
import jax.numpy as jnp
from jax import lax
from jax.experimental import pallas as pl
from jax.experimental.pallas import tpu as pltpu

F32 = jnp.float32
BF16 = jnp.bfloat16

DEPTH = 4
HEAD_DIM = 64
FOX_HEADS = 8
SB_HEADS = 4
FOX_W = FOX_HEADS * HEAD_DIM
SB_W = SB_HEADS * HEAD_DIM
POOL_W = 256
POOL_WINDOWS = (2, 4, 8, 16)
POOL_HALO = 16
D_MIX = FOX_W + POOL_W + SB_W
EPS = 1e-6
NEG = -1e30
QK_SCALE = HEAD_DIM ** -0.5

ORIG_FOX = 4 * FOX_W
ORIG_FF = ORIG_FOX
ORIG_REST = ORIG_FF + FOX_HEADS
D_IN = ORIG_REST + 2 * POOL_W + 4 * SB_W

C_FQ, C_FK, C_FV, C_FG = 0, FOX_W, 2 * FOX_W, 3 * FOX_W
C_PX = 4 * FOX_W
C_PG = C_PX + POOL_W
C_SQ = C_PG + POOL_W
C_SK, C_SV, C_SG = C_SQ + SB_W, C_SQ + 2 * SB_W, C_SQ + 3 * SB_W
PM = C_SG + SB_W
LANES = 128
PW = PM + LANES
FF_STRIDE = 8
AUG = 3

ADAM_LR = 0.001
ADAM_B1 = 0.9
ADAM_B2 = 0.999
ADAM_EPS = 1e-08
ADAM_WD = 0.01
ADAM_STEP = 10

VMEM_LIMIT = 48 * 1024 * 1024
PROJ_TN = PM // 2


def _cparams(**kw):
    return pltpu.CompilerParams(vmem_limit_bytes=VMEM_LIMIT, **kw)


def _dot(a, b):
    return jnp.dot(a, b, preferred_element_type=F32)


def _dot_nt(a, b):
    return lax.dot_general(a, b, (((1,), (1,)), ((), ())), preferred_element_type=F32)


def _dot_tn(a, b):
    return lax.dot_general(a, b, (((0,), (0,)), ((), ())), preferred_element_type=F32)


def _split2(x):
    hi = x.astype(BF16)
    lo = (x - hi.astype(F32)).astype(BF16)
    return hi, lo


def _split3(x):
    hi = x.astype(BF16)
    r = x - hi.astype(F32)
    mid = r.astype(BF16)
    lo = (r - mid.astype(F32)).astype(BF16)
    return hi, mid, lo


def _dot_exact_rhs(x, m):
    hi, mid, lo = _split3(x)
    return _dot(hi, m) + _dot(mid, m) + _dot(lo, m)


def _dot_exact_lhs(m, x):
    hi, mid, lo = _split3(x)
    return _dot(m, hi) + _dot(m, mid) + _dot(m, lo)


def _sigmoid(x):
    return 1.0 / (1.0 + jnp.exp(-x))


def _silu_pair(x):
    s = _sigmoid(x)
    return x * s, s * (1.0 + x * (1.0 - s))


def _iota(shape, dim):
    return lax.broadcasted_iota(jnp.int32, shape, dim)


def _ones_where(cond):
    return jnp.where(cond, 1.0, 0.0).astype(BF16)


def _head_blockdiag(w):
    return _ones_where((_iota((w, w), 0) >> 6) == (_iota((w, w), 1) >> 6))


def _group_sum(x, bd):
    hi, lo = _split2(x)
    return _dot(hi, bd) + _dot(lo, bd)


def _lane_pick(x, lane_idx, lane):
    return jnp.sum(jnp.where(lane_idx == lane, x, 0.0), axis=1, keepdims=True)


def _inproj(x, g, wt_all, layer, *, tm, tn):
    S, D = x.shape
    nj = PM // tn

    def body(x_ref, g_ref, w_ref, wff_ref, proj_ref, ff_ref, h_ref):
        @pl.when(pl.program_id(1) == 0)
        def _():
            xf = x_ref[...]
            ms = jnp.mean(xf * xf, axis=-1, keepdims=True)
            h = (xf * lax.rsqrt(ms + EPS) * g_ref[...]).astype(BF16)
            h_ref[...] = h
            ff_ref[...] = _dot_nt(h, wff_ref[...])

        proj_ref[...] = _dot_nt(h_ref[...], w_ref[...])

    return pl.pallas_call(
        body, name="inproj", grid=(S // tm, nj),
        in_specs=[pl.BlockSpec((tm, D), lambda i, j: (i, 0)),
                  pl.BlockSpec((1, D), lambda i, j: (0, 0)),
                  pl.BlockSpec((None, tn, D), lambda i, j: (layer, j, 0)),
                  pl.BlockSpec((None, LANES, D), lambda i, j: (layer, PM // LANES, 0))],
        out_specs=[pl.BlockSpec((tm, tn), lambda i, j: (i, j)),
                   pl.BlockSpec((tm, LANES), lambda i, j: (i, 0)),
                   pl.BlockSpec((tm, D), lambda i, j: (i, 0))],
        out_shape=[jax.ShapeDtypeStruct((S, PM), F32), jax.ShapeDtypeStruct((S, LANES), F32),
                   jax.ShapeDtypeStruct((S, D), BF16)],
        compiler_params=_cparams(dimension_semantics=("arbitrary", "arbitrary")),
    )(x, g, wt_all, wt_all)


def _pool_group_select(lane_group, vals):
    return jnp.where(lane_group == 0, vals[0], jnp.where(lane_group == 1, vals[1], jnp.where(lane_group == 2, vals[2], vals[3])))


def _prep(projm, ffo, qg, kg, bfp, wpd, ps, *, ts):
    S = projm.shape[0]
    nb = S // ts
    hb = ts // POOL_HALO

    def body(fq_ref, fk_ref, fv_ref, pp_ref, halo_ref, ff_ref, sq_ref, sk_ref, sv_ref,
             qg_ref, kg_ref, bf_ref, wpd_ref, ps_ref,
             qn_ref, ka_ref, kb_ref, v_ref, sqo_ref, sko_ref, svo_ref, pooled_ref, yp_ref, pm_ref,
             carry_ref, c_ref, buf_ref):
        i = pl.program_id(0)
        bd = _head_blockdiag(FOX_W)
        normed = []
        for src, g_ref in ((fq_ref, qg_ref), (fk_ref, kg_ref)):
            q = src[...]
            ss = _group_sum(q * q, bd)
            normed.append(q * lax.rsqrt(ss * (1.0 / HEAD_DIM) + EPS) * g_ref[...])
        qn_ref[...] = (normed[0] * QK_SCALE).astype(BF16)
        kn = normed[1]
        v_ref[...] = fv_ref[...].astype(BF16)
        sqo_ref[...] = (sq_ref[...] * QK_SCALE).astype(BF16)
        sko_ref[...] = sk_ref[...].astype(BF16)
        svo_ref[...] = sv_ref[...].astype(BF16)

        @pl.when(i == 0)
        def _():
            carry_ref[...] = jnp.zeros_like(carry_ref)

        z = ff_ref[...] + bf_ref[...]
        lf = jnp.minimum(z, 0.0) - jnp.log(1.0 + jnp.exp(-jnp.abs(z)))
        tri = _ones_where(_iota((ts, ts), 1) <= _iota((ts, ts), 0))
        c = _dot_exact_lhs(tri, lf) + carry_ref[...]
        c_ref[...] = c
        carry_ref[...] = c_ref[ts - 1:ts, :]
        parts = jnp.concatenate(_split3(-c), axis=1)
        row = _iota((AUG * LANES, FOX_W), 0)
        col = _iota((AUG * LANES, FOX_W), 1)
        part, src = row >> 7, row & (LANES - 1)
        pair, off = col >> 7, col & (LANES - 1)
        sel_a = _ones_where((src == FF_STRIDE * pair) & (off == HEAD_DIM + part))
        sel_b = _ones_where((src == FF_STRIDE * pair + 1) & (off == part))
        first_half = (_iota((1, FOX_W), 1) & HEAD_DIM) == 0
        ka_ref[...] = jnp.where(first_half, kn, _dot(parts, sel_a)).astype(BF16)
        kb_ref[...] = jnp.where(first_half, _dot(parts, sel_b), kn).astype(BF16)

        x = pp_ref[:, 0:POOL_W]
        pg = pp_ref[:, POOL_W:2 * POOL_W]
        halo = jnp.where(i > 0, halo_ref[:, 0:POOL_W], 0.0)
        buf_ref[0:POOL_HALO, :] = halo
        buf_ref[POOL_HALO:POOL_HALO + ts, :] = x
        acc = x
        snaps = []
        for d in range(1, POOL_HALO):
            acc = acc + buf_ref[pl.ds(POOL_HALO - d, ts), :]
            if d + 1 in POOL_WINDOWS:
                snaps.append(acc)
        lane_group = _iota((1, POOL_W), 1) >> 6
        wsum = _pool_group_select(lane_group, snaps)
        wlen = _pool_group_select(lane_group, [float(w) for w in POOL_WINDOWS])
        tpos = (i * ts + _iota((ts, 1), 0) + 1).astype(F32)
        pooled = wsum / jnp.minimum(tpos, wlen) - x
        pb = pooled.astype(BF16)
        pooled_ref[...] = pb
        yp = _dot(pb, wpd_ref[...])
        yp_ref[...] = yp
        pm_ref[...] = (yp * ps_ref[...] * (pg * _sigmoid(pg))).astype(BF16)

    blk = lambda w, c: pl.BlockSpec((ts, w), lambda i: (i, c))
    full = lambda a: pl.BlockSpec(a.shape, lambda i: (0,) * a.ndim)
    out_shapes = [
        jax.ShapeDtypeStruct((S, FOX_W), BF16), jax.ShapeDtypeStruct((S, FOX_W), BF16), jax.ShapeDtypeStruct((S, FOX_W), BF16),
        jax.ShapeDtypeStruct((S, FOX_W), BF16),
        jax.ShapeDtypeStruct((S, SB_W), BF16), jax.ShapeDtypeStruct((S, SB_W), BF16), jax.ShapeDtypeStruct((S, SB_W), BF16),
        jax.ShapeDtypeStruct((S, POOL_W), BF16), jax.ShapeDtypeStruct((S, POOL_W), F32), jax.ShapeDtypeStruct((S, POOL_W), BF16),
    ]
    out_specs = [
        blk(FOX_W, 0), blk(FOX_W, 0), blk(FOX_W, 0), blk(FOX_W, 0),
        blk(SB_W, 0), blk(SB_W, 0), blk(SB_W, 0),
        blk(POOL_W, 0), blk(POOL_W, 0), blk(POOL_W, 0),
    ]
    return pl.pallas_call(
        body, name="prep", grid=(nb,),
        in_specs=[blk(FOX_W, C_FQ // FOX_W), blk(FOX_W, C_FK // FOX_W), blk(FOX_W, C_FV // FOX_W), blk(2 * POOL_W, C_PX // (2 * POOL_W)),
                  pl.BlockSpec((POOL_HALO, 2 * POOL_W), lambda i: (jnp.maximum(i * hb - 1, 0), C_PX // (2 * POOL_W))),
                  blk(LANES, 0),
                  blk(SB_W, C_SQ // SB_W), blk(SB_W, C_SK // SB_W), blk(SB_W, C_SV // SB_W),
                  full(qg), full(kg), full(bfp), full(wpd), full(ps)],
        out_specs=out_specs, out_shape=out_shapes,
        scratch_shapes=[pltpu.VMEM((1, LANES), F32), pltpu.VMEM((ts, LANES), F32), pltpu.VMEM((ts + POOL_HALO, POOL_W), F32)],
        compiler_params=_cparams(dimension_semantics=("arbitrary",)),
    )(projm, projm, projm, projm, projm, ffo, projm, projm, projm, qg, kg, bfp, wpd, ps)


def _pair_masks(x):
    ma = _iota((1, LANES), 1) < HEAD_DIM
    zero = jnp.zeros_like(x)
    return jnp.where(ma, x, zero), jnp.where(ma, zero, x)


DIAG_TILE = 256


def _diag_tiles(tq, size=DIAG_TILE):
    size = min(tq, size)
    return [(t * size, size) for t in range(tq // size)]


def _put_rows(old, new, r0):
    return new if r0 == 0 else jnp.concatenate([old[:r0], new], axis=0)


def _aug_queries(q):
    lane = _iota((1, LANES), 1)
    one = jnp.ones_like(q)
    zero = jnp.zeros_like(q)
    qa = jnp.where(lane < HEAD_DIM, q, jnp.where(lane < HEAD_DIM + AUG, one, zero))
    qb = jnp.where(lane >= HEAD_DIM, q, jnp.where(lane < AUG, one, zero))
    return qa, qb


def _fox_fwd(qn, ka, kb, v, projm, *, tq, tk):
    S = qn.shape[0]
    npair = FOX_HEADS // 2

    def body(q_ref, ka_ref, kb_ref, v_ref, fg_ref, o_ref, lse_ref, fm_ref):
        qi = pl.program_id(1)
        lane = _iota((1, LANES), 1)
        ma = lane < HEAD_DIM
        qaug = _aug_queries(q_ref[...])
        k_refs = (ka_ref, kb_ref)

        def block(k0, tkl, r0, carry, masked):
            vb = v_ref[pl.ds(k0, tkl), :]
            if masked:
                mask = (k0 + _iota((tq - r0, tkl), 1)) <= (qi * tq + r0 + _iota((tq - r0, tkl), 0))
            scores = [_dot_nt(qaug[h][r0:], k_refs[h][pl.ds(k0, tkl), :]) for h in range(2)]
            new = []
            for h in range(2):
                m, l, acc = [x[r0:] for x in carry[h]]
                s = jnp.where(mask, scores[h], NEG) if masked else scores[h]
                m_new = jnp.maximum(m, jnp.max(s, axis=1, keepdims=True))
                alpha = jnp.exp(m - m_new)
                p = jnp.exp(s - m_new)
                sub = (m_new, alpha * l + jnp.sum(p, axis=1, keepdims=True), alpha * acc + _dot(p.astype(BF16), vb))
                new.append(tuple(_put_rows(old, x, r0) for old, x in zip(carry[h], sub)))
            return tuple(new)

        init = tuple((jnp.full((tq, 1), NEG, F32), jnp.zeros((tq, 1), F32), jnp.zeros((tq, LANES), F32)) for _ in range(2))
        carry = lax.fori_loop(0, (qi * tq) // tk, lambda j, c: block(pl.multiple_of(j * tk, tk), tk, 0, c, False), init)
        for off, size in _diag_tiles(tq, tq):
            carry = block(pl.multiple_of(qi * tq + off, size), size, off, carry, True)
        (ma_, la, acca), (mb_, lb, accb) = carry
        o = jnp.where(ma, acca / la, accb / lb)
        o_ref[...] = o
        lse_ref[...] = jnp.where(ma, ma_ + jnp.log(la), mb_ + jnp.log(lb))
        fg = fg_ref[...]
        fm_ref[...] = (o * (fg * _sigmoid(fg))).astype(BF16)

    qblk = pl.BlockSpec((tq, LANES), lambda p, i: (i, p))
    kvblk = pl.BlockSpec((S, LANES), lambda p, i: (0, p))
    return pl.pallas_call(
        body, name="fox_fwd", grid=(npair, S // tq),
        in_specs=[qblk, kvblk, kvblk, kvblk,
                  pl.BlockSpec((tq, LANES), lambda p, i: (i, C_FG // LANES + p))],
        out_specs=[qblk, qblk, qblk],
        out_shape=[jax.ShapeDtypeStruct((S, FOX_W), F32), jax.ShapeDtypeStruct((S, FOX_W), F32), jax.ShapeDtypeStruct((S, FOX_W), BF16)],
        compiler_params=_cparams(dimension_semantics=("arbitrary", "arbitrary")),
    )(qn, ka, kb, v, projm)


def _suffix_sums(x, tmat2):
    return _dot(jnp.concatenate(_split2(x), axis=1), tmat2)


def _suffix_matrix(tk, inclusive):
    rr, cc = _iota((2 * tk, tk), 0) & (tk - 1), _iota((2 * tk, tk), 1)
    return _ones_where(rr >= cc) if inclusive else _ones_where(rr > cc)


def _sb_scores(qh, kb, causal, tmat2, r_runs):
    heads = range(2)
    zs = [_dot_nt(qh[h], kb) for h in heads]
    nsps = [jnp.minimum(-z, 0.0) - jnp.log(1.0 + jnp.exp(-jnp.abs(z))) for z in zs]
    lbs = nsps if causal is None else [jnp.where(causal, n, 0.0) for n in nsps]
    rins = [_suffix_sums(lb, tmat2) for lb in lbs]
    args = [zs[h] + lbs[h] + (rins[h] + r_runs[h]) for h in heads]
    a_s = [jnp.exp(arg if causal is None else jnp.where(causal, arg, NEG)) for arg in args]
    return zs, nsps, lbs, a_s


def _sb_fwd(sq, sk, sv, projm, *, tq, tk):
    S = sq.shape[0]
    npair = SB_HEADS // 2

    def body(q_ref, k_ref, v_ref, sg_ref, o_ref, sm_ref):
        qi = pl.program_id(1)
        lane = _iota((1, LANES), 1)
        ma = lane < HEAD_DIM
        qh = _pair_masks(q_ref[...])
        tmat2 = _suffix_matrix(tk, inclusive=False)
        nfull = (qi * tq) // tk

        def block(k0, r0, carry, masked):
            nr = tq - r0
            kb = k_ref[pl.ds(k0, tk), :]
            vb = v_ref[pl.ds(k0, tk), :]
            causal = (k0 + _iota((nr, tk), 1)) < (qi * tq + r0 + _iota((nr, tk), 0)) if masked else None
            _, _, lbs, a_s = _sb_scores([q[r0:] for q in qh], kb, causal, tmat2, [carry[h][0][r0:] for h in range(2)])
            pv = _dot(jnp.concatenate([a.astype(BF16) for a in a_s], axis=0), vb)
            return tuple((_put_rows(carry[h][0], carry[h][0][r0:] + jnp.sum(lbs[h], axis=1, keepdims=True), r0),
                          _put_rows(carry[h][1], carry[h][1][r0:] + pv[h * nr:(h + 1) * nr], r0)) for h in range(2))

        carry = tuple((jnp.zeros((tq, 1), F32), jnp.zeros((tq, LANES), F32)) for _ in range(2))
        for off, size in reversed(_diag_tiles(tq)):
            assert size == tk
            carry = block(pl.multiple_of(qi * tq + off, tk), off, carry, True)
        (_, acca), (_, accb) = lax.fori_loop(0, nfull, lambda jj, c: block(pl.multiple_of((nfull - 1 - jj) * tk, tk), 0, c, False), carry)
        o = jnp.where(ma, acca, accb)
        o_ref[...] = o
        sg = sg_ref[...]
        sm_ref[...] = (o * (sg * _sigmoid(sg))).astype(BF16)

    qblk = pl.BlockSpec((tq, LANES), lambda p, i: (i, p))
    kvblk = pl.BlockSpec((S, LANES), lambda p, i: (0, p))
    return pl.pallas_call(
        body, name="sb_fwd", grid=(npair, S // tq),
        in_specs=[qblk, kvblk, kvblk, pl.BlockSpec((tq, LANES), lambda p, i: (i, C_SG // LANES + p))],
        out_specs=[qblk, qblk],
        out_shape=[jax.ShapeDtypeStruct((S, SB_W), F32), jax.ShapeDtypeStruct((S, SB_W), BF16)],
        compiler_params=_cparams(dimension_semantics=("arbitrary", "arbitrary")),
    )(sq, sk, sv, projm)


def _outproj(x, fm, pm, sm, w_out, layer, *, tm):
    S, D = x.shape

    def body(x_ref, fm_ref, pm_ref, sm_ref, w_ref, y_ref):
        y = x_ref[...] + _dot(fm_ref[...], w_ref[0:FOX_W, :])
        y = y + _dot(pm_ref[...], w_ref[FOX_W:FOX_W + POOL_W, :])
        y_ref[...] = y + _dot(sm_ref[...], w_ref[FOX_W + POOL_W:D_MIX, :])

    row = lambda w: pl.BlockSpec((tm, w), lambda i: (i, 0))
    return pl.pallas_call(
        body, name="outproj", grid=(S // tm,),
        in_specs=[row(D), row(FOX_W), row(POOL_W), row(SB_W), pl.BlockSpec((None, D_MIX, D), lambda i: (layer, 0, 0))],
        out_specs=row(D), out_shape=jax.ShapeDtypeStruct((S, D), F32),
        compiler_params=_cparams(dimension_semantics=("arbitrary",)),
    )(x, fm, pm, sm, w_out)


def _loss_head(y, target, *, tm):
    S, D = y.shape

    def body(y_ref, t_ref, dy_ref, sq_ref):
        @pl.when(pl.program_id(0) == 0)
        def _():
            sq_ref[...] = jnp.zeros_like(sq_ref)

        d = y_ref[...] - t_ref[...]
        dy_ref[...] = d * (1.0 / D)
        sq_ref[...] += jnp.sum(d * d, axis=0, keepdims=True)

    row = pl.BlockSpec((tm, D), lambda i: (i, 0))
    return pl.pallas_call(
        body, name="loss_head", grid=(S // tm,),
        in_specs=[row, row], out_specs=[row, pl.BlockSpec((1, D), lambda i: (0, 0))],
        out_shape=[jax.ShapeDtypeStruct((S, D), F32), jax.ShapeDtypeStruct((1, D), F32)],
        compiler_params=_cparams(dimension_semantics=("arbitrary",)),
    )(y, target)


def _outproj_bwd(dy, fm, pm, sm, w_out, layer, stacks, *, tm):
    S, D = dy.shape

    def body(dy_ref, fm_ref, pm_ref, sm_ref, w_ref, dm_ref, dw_ref):
        @pl.when(pl.program_id(0) == 0)
        def _():
            dw_ref[...] = jnp.zeros_like(dw_ref)

        dyb = dy_ref[...].astype(BF16)
        dm_ref[...] = _dot_nt(dyb, w_ref[...])
        dw_ref[0:FOX_W, :] += _dot_tn(fm_ref[...], dyb)
        dw_ref[FOX_W:FOX_W + POOL_W, :] += _dot_tn(pm_ref[...], dyb)
        dw_ref[FOX_W + POOL_W:D_MIX, :] += _dot_tn(sm_ref[...], dyb)

    row = lambda w: pl.BlockSpec((tm, w), lambda i: (i, 0))
    wspec = pl.BlockSpec((None, D_MIX, D), lambda i: (layer, 0, 0))
    return _stack_call(
        body, "outproj_bwd", (S // tm,), [row(D), row(FOX_W), row(POOL_W), row(SB_W), wspec], (dy, fm, pm, sm, w_out),
        [pl.BlockSpec((None, D_MIX, D), lambda i: (layer, 0, 0))], [(D_MIX, D)], stacks,
        plain_specs=[row(D_MIX)], plain_shapes=[jax.ShapeDtypeStruct((S, D_MIX), F32)],
        compiler_params=_cparams(dimension_semantics=("arbitrary",)))


def _fox_bwd(qn, ka, kb, v, o, lse, dmix, projm, *, tq, tk):
    S = qn.shape[0]
    npair = FOX_HEADS // 2

    def body(q_ref, ka_ref, kb_ref, v_ref, o_ref, lse_ref, dm_ref, fg_ref,
             dq_ref, dk_ref, dv_ref, dfg_ref, dct_ref, dcr_ref):
        qi = pl.program_id(1)

        @pl.when(qi == 0)
        def _():
            dk_ref[...] = jnp.zeros_like(dk_ref)
            dv_ref[...] = jnp.zeros_like(dv_ref)
            dct_ref[...] = jnp.zeros_like(dct_ref)

        lane = _iota((1, LANES), 1)
        ma = lane < HEAD_DIM
        qh = _pair_masks(q_ref[...])
        qaug = _aug_queries(q_ref[...])
        k_refs = (ka_ref, kb_ref)
        lsev = lse_ref[...]
        lse = (_lane_pick(lsev, lane, 0), _lane_pick(lsev, lane, HEAD_DIM))
        fg = fg_ref[...]
        silu, dsilu = _silu_pair(fg)
        dm = dm_ref[...]
        ov = o_ref[...]
        do = dm * silu
        dfg_ref[...] = dm * ov * dsilu
        dd = do * ov
        dsum = (jnp.sum(jnp.where(ma, dd, 0.0), axis=1, keepdims=True), jnp.sum(jnp.where(ma, 0.0, dd), axis=1, keepdims=True))
        doh = _pair_masks(do.astype(BF16))

        def block(k0, tkl, r0, carry, masked):
            vb = v_ref[pl.ds(k0, tkl), :]
            if masked:
                mask = (k0 + _iota((tq - r0, tkl), 1)) <= (qi * tq + r0 + _iota((tq - r0, tkl), 0))
            heads = range(2)
            kaugs = [k_refs[h][pl.ds(k0, tkl), :] for h in heads]
            scores = [_dot_nt(qaug[h][r0:], kaugs[h]) for h in heads]
            dps = [_dot_nt(doh[h][r0:], vb) for h in heads]
            ps, dss, rows = [], [], []
            for h in heads:
                s = jnp.where(mask, scores[h], NEG) if masked else scores[h]
                p = jnp.exp(s - lse[h][r0:])
                dsf = p * (dps[h] - dsum[h][r0:])
                dct_ref[0, h:h + 1, pl.ds(k0, tkl)] -= jnp.sum(dsf, axis=0, keepdims=True)
                rows.append(_put_rows(carry[1 + h], carry[1 + h][r0:] + jnp.sum(dsf, axis=1, keepdims=True), r0))
                ps.append(p.astype(BF16))
                dss.append(dsf.astype(BF16))
            dv_ref[pl.ds(k0, tkl), :] += _dot_tn(jnp.concatenate(ps, axis=0), jnp.concatenate([d[r0:] for d in doh], axis=0))
            dk_ref[pl.ds(k0, tkl), :] += _dot_tn(jnp.concatenate(dss, axis=0), jnp.concatenate([q[r0:] for q in qh], axis=0))
            kh = jnp.concatenate([_pair_masks(kaugs[h])[h] for h in heads], axis=0)
            dq = _put_rows(carry[0], carry[0][r0:] + _dot(jnp.concatenate(dss, axis=1), kh), r0)
            return (dq, rows[0], rows[1])

        zcol = jnp.zeros((tq, 1), F32)
        carry = lax.fori_loop(0, (qi * tq) // tk, lambda j, c: block(pl.multiple_of(j * tk, tk), tk, 0, c, False),
                              (jnp.zeros((tq, LANES), F32), zcol, zcol))
        for off, size in _diag_tiles(tq):
            carry = block(pl.multiple_of(qi * tq + off, size), size, off, carry, True)
        dq, rowa, rowb = carry
        dq_ref[...] = dq * QK_SCALE
        dcr_ref[0] = jnp.where(ma, rowa, rowb)

    qblk = pl.BlockSpec((tq, LANES), lambda p, i: (i, p))
    kvblk = pl.BlockSpec((S, LANES), lambda p, i: (0, p))
    f32out = jax.ShapeDtypeStruct((S, FOX_W), F32)
    ctblk = pl.BlockSpec((1, FF_STRIDE, S), lambda p, i: (p, 0, 0))
    return pl.pallas_call(
        body, name="fox_bwd", grid=(npair, S // tq),
        in_specs=[qblk, kvblk, kvblk, kvblk, qblk, qblk, qblk,
                  pl.BlockSpec((tq, LANES), lambda p, i: (i, C_FG // LANES + p))],
        out_specs=[qblk, kvblk, kvblk, qblk, ctblk, pl.BlockSpec((1, tq, LANES), lambda p, i: (p, i, 0))],
        out_shape=[f32out, f32out, f32out, f32out, jax.ShapeDtypeStruct((npair, FF_STRIDE, S), F32),
                   jax.ShapeDtypeStruct((npair, S, LANES), F32)],
        compiler_params=_cparams(dimension_semantics=("arbitrary", "arbitrary")),
    )(qn, ka, kb, v, o, lse, dmix, projm)


def _sb_bwd(sq, sk, sv, o, dmix, projm, *, tq, tk):
    S = sq.shape[0]
    npair = SB_HEADS // 2
    mix0 = (FOX_W + POOL_W) // LANES

    def body(q_ref, k_ref, v_ref, o_ref, dm_ref, sg_ref, dq_ref, dk_ref, dv_ref, dsg_ref):
        qi = pl.program_id(1)

        @pl.when(qi == 0)
        def _():
            dk_ref[...] = jnp.zeros_like(dk_ref)
            dv_ref[...] = jnp.zeros_like(dv_ref)

        lane = _iota((1, LANES), 1)
        ma = lane < HEAD_DIM
        qh = _pair_masks(q_ref[...])
        sg = sg_ref[...]
        silu, dsilu = _silu_pair(sg)
        dm = dm_ref[...]
        ov = o_ref[...]
        do = dm * silu
        dsg_ref[...] = dm * ov * dsilu
        dob = do.astype(BF16)
        dd = dob.astype(F32) * ov
        dsum = (jnp.sum(jnp.where(ma, dd, 0.0), axis=1, keepdims=True), jnp.sum(jnp.where(ma, 0.0, dd), axis=1, keepdims=True))
        doh = _pair_masks(dob)
        tmat2 = _suffix_matrix(tk, inclusive=False)
        tmat2_inc = _suffix_matrix(tk, inclusive=True)
        nfull = (qi * tq) // tk

        def block(k0, r0, carry, masked):
            nr = tq - r0
            kb = k_ref[pl.ds(k0, tk), :]
            vb = v_ref[pl.ds(k0, tk), :]
            kh = _pair_masks(kb)
            causal = (k0 + _iota((nr, tk), 1)) < (qi * tq + r0 + _iota((nr, tk), 0)) if masked else None
            heads = range(2)
            qs = [q[r0:] for q in qh]
            dos = [d[r0:] for d in doh]
            das = [_dot_nt(dos[h], vb) for h in heads]
            zs, nsps, lbs, a_s = _sb_scores(qs, kb, causal, tmat2, [carry[h][0][r0:] for h in heads])
            abs_ = [a.astype(BF16) for a in a_s]
            us = [abs_[h].astype(F32) * das[h] for h in heads]
            uins = [_suffix_sums(u, tmat2_inc) for u in us]
            dzs = []
            for h in heads:
                cum_u = dsum[h][r0:] - (uins[h] + carry[h][1][r0:])
                dz = us[h] * jnp.exp(nsps[h]) - jnp.exp(zs[h] + nsps[h]) * cum_u
                if masked:
                    dz = jnp.where(causal, dz, 0.0)
                dzs.append(dz.astype(BF16))
            dv_ref[pl.ds(k0, tk), :] += _dot_tn(jnp.concatenate(abs_, axis=0), jnp.concatenate(dos, axis=0))
            dk_ref[pl.ds(k0, tk), :] += _dot_tn(jnp.concatenate(dzs, axis=0), jnp.concatenate(qs, axis=0))
            dq = _put_rows(carry[2], carry[2][r0:] + _dot(jnp.concatenate(dzs, axis=1), jnp.concatenate(kh, axis=0)), r0)
            new = [(_put_rows(carry[h][0], carry[h][0][r0:] + jnp.sum(lbs[h], axis=1, keepdims=True), r0),
                    _put_rows(carry[h][1], carry[h][1][r0:] + jnp.sum(us[h], axis=1, keepdims=True), r0)) for h in heads]
            return (new[0], new[1], dq)

        zcol = jnp.zeros((tq, 1), F32)
        carry = ((zcol, zcol), (zcol, zcol), jnp.zeros((tq, LANES), F32))
        for off, size in reversed(_diag_tiles(tq)):
            assert size == tk
            carry = block(pl.multiple_of(qi * tq + off, tk), off, carry, True)
        dq = lax.fori_loop(0, nfull, lambda jj, c: block(pl.multiple_of((nfull - 1 - jj) * tk, tk), 0, c, False), carry)[2]
        dq_ref[...] = dq * QK_SCALE

    qblk = pl.BlockSpec((tq, LANES), lambda p, i: (i, p))
    kvblk = pl.BlockSpec((S, LANES), lambda p, i: (0, p))
    f32out = jax.ShapeDtypeStruct((S, SB_W), F32)
    return pl.pallas_call(
        body, name="sb_bwd", grid=(npair, S // tq),
        in_specs=[qblk, kvblk, kvblk, qblk,
                  pl.BlockSpec((tq, LANES), lambda p, i: (i, mix0 + p)),
                  pl.BlockSpec((tq, LANES), lambda p, i: (i, C_SG // LANES + p))],
        out_specs=[qblk, kvblk, kvblk, qblk],
        out_shape=[f32out, f32out, f32out, f32out],
        compiler_params=_cparams(dimension_semantics=("arbitrary", "arbitrary")),
    )(sq, sk, sv, o, dmix, projm)


def _prep_bwd(projm, ffo, dqn, dkn, dct, dcr, dv, dfg, dsq, dsk, dsv, dsg, dmix, pooled, yp, qg, kg, bfp, wpd, ps, *, ts):
    S = projm.shape[0]
    nb = S // ts
    hb = ts // POOL_HALO
    npair = FOX_HEADS // 2
    last_halo = S // POOL_HALO - 1

    def body(fq_ref, fk_ref, pp_ref, pph_ref, ff_ref,
             dqn_ref, dkn_ref, dct_ref, dcr_ref, dv_ref, dfg_ref, dsq_ref, dsk_ref, dsv_ref, dsg_ref,
             dmp_ref, dmh_ref, pooled_ref, yp_ref, qg_ref, kg_ref, bf_ref, wpd_ref, ps_ref,
             dp_ref, dqg_ref, dkg_ref, dbf_ref, dwp_ref, dps_ref,
             carry_ref, dl_ref, buf_ref, dct_s):
        i = pl.program_id(0)
        blk = nb - 1 - i

        @pl.when(i == 0)
        def _():
            carry_ref[...] = jnp.zeros_like(carry_ref)
            dqg_ref[...] = jnp.zeros_like(dqg_ref)
            dkg_ref[...] = jnp.zeros_like(dkg_ref)
            dbf_ref[...] = jnp.zeros_like(dbf_ref)
            dwp_ref[...] = jnp.zeros_like(dwp_ref)
            dps_ref[...] = jnp.zeros_like(dps_ref)

        bd = _head_blockdiag(FOX_W)
        for raw_ref, g_ref, dn, dg_ref, col in ((fq_ref, qg_ref, dqn_ref[...], dqg_ref, C_FQ), (fk_ref, kg_ref, dkn_ref[...], dkg_ref, C_FK)):
            q = raw_ref[...]
            rstd = lax.rsqrt(_group_sum(q * q, bd) * (1.0 / HEAD_DIM) + EPS)
            xhat = q * rstd
            dg_ref[...] += jnp.sum(dn * xhat, axis=0, keepdims=True)
            dyg = dn * g_ref[...]
            mean = _group_sum(dyg * xhat, bd) * (1.0 / HEAD_DIM)
            dp_ref[:, col:col + FOX_W] = (rstd * (dyg - xhat * mean)).astype(BF16)
        dp_ref[:, C_FV:C_FV + FOX_W] = dv_ref[...].astype(BF16)
        dp_ref[:, C_FG:C_FG + FOX_W] = dfg_ref[...].astype(BF16)
        dp_ref[:, C_SQ:C_SQ + SB_W] = dsq_ref[...].astype(BF16)
        dp_ref[:, C_SK:C_SK + SB_W] = dsk_ref[...].astype(BF16)
        dp_ref[:, C_SV:C_SV + SB_W] = dsv_ref[...].astype(BF16)
        dp_ref[:, C_SG:C_SG + SB_W] = dsg_ref[...].astype(BF16)

        dct_s[...] = jnp.zeros_like(dct_s)
        for p in range(npair):
            dct_s[FF_STRIDE * p:FF_STRIDE * (p + 1), :] = dct_ref[p]
        dc = dct_s[...].T
        lane = _iota((1, LANES), 1)
        for p in range(npair):
            dcr = dcr_ref[p]
            dc = dc + jnp.where(lane == FF_STRIDE * p, _lane_pick(dcr, lane, 0), 0.0)
            dc = dc + jnp.where(lane == FF_STRIDE * p + 1, _lane_pick(dcr, lane, HEAD_DIM), 0.0)
        triu = _ones_where(_iota((ts, ts), 1) >= _iota((ts, ts), 0))
        dlf = _dot_exact_lhs(triu, dc) + carry_ref[...]
        dl_ref[...] = dlf
        carry_ref[...] = dl_ref[0:1, :]
        z = ff_ref[...] + bf_ref[...]
        dff = dlf * (1.0 / (1.0 + jnp.exp(z)))
        dbf_ref[...] += jnp.sum(dff, axis=0, keepdims=True)
        dp_ref[:, PM:PW] = dff.astype(BF16)

        psv = ps_ref[...]
        wpdv = wpd_ref[...]
        lane_group = _iota((1, POOL_W), 1) >> 6
        wlen = _pool_group_select(lane_group, [float(w) for w in POOL_WINDOWS])
        pg = pp_ref[:, POOL_W:2 * POOL_W]
        silu, dsilu = _silu_pair(pg)
        dmp = dmp_ref[...]
        ypv = yp_ref[...]
        dp_ref[:, C_PG:C_PG + POOL_W] = (dmp * (ypv * psv) * dsilu).astype(BF16)
        dps_ref[...] += jnp.sum(dmp * silu * ypv, axis=0, keepdims=True)
        dyp = (dmp * psv * silu).astype(BF16)
        dwp_ref[...] += _dot_tn(pooled_ref[...], dyp)
        dpooled = _dot_nt(dyp, wpdv)
        pgh = pph_ref[:, POOL_W:2 * POOL_W]
        dyph = (dmh_ref[...] * psv * (pgh * _sigmoid(pgh))).astype(BF16)
        dpooled_h = jnp.where(blk < nb - 1, _dot_nt(dyph, wpdv), 0.0)
        tpos = (blk * ts + _iota((ts, 1), 0) + 1).astype(F32)
        ev = dpooled / jnp.minimum(tpos, wlen)
        buf_ref[0:ts, :] = ev
        buf_ref[ts:ts + POOL_HALO, :] = dpooled_h / wlen
        acc = ev
        snaps = []
        for d in range(1, POOL_HALO):
            acc = acc + buf_ref[pl.ds(d, ts), :]
            if d + 1 in POOL_WINDOWS:
                snaps.append(acc)
        dp_ref[:, C_PX:C_PX + POOL_W] = (_pool_group_select(lane_group, snaps) - dpooled).astype(BF16)

    rblk = lambda w, c: pl.BlockSpec((ts, w), lambda i: (nb - 1 - i, c))
    full = lambda a: pl.BlockSpec(a.shape, lambda i: (0,) * a.ndim)
    halo = lambda w, c: pl.BlockSpec((POOL_HALO, w), lambda i: (jnp.minimum((nb - i) * hb, last_halo), c))
    acc_spec = lambda r, w: pl.BlockSpec((r, w), lambda i: (0, 0))
    return pl.pallas_call(
        body, name="prep_bwd", grid=(nb,),
        in_specs=[rblk(FOX_W, C_FQ // FOX_W), rblk(FOX_W, C_FK // FOX_W), rblk(2 * POOL_W, C_PX // (2 * POOL_W)),
                  halo(2 * POOL_W, C_PX // (2 * POOL_W)), rblk(LANES, 0),
                  rblk(FOX_W, 0), rblk(FOX_W, 0), pl.BlockSpec((npair, FF_STRIDE, ts), lambda i: (0, 0, nb - 1 - i)),
                  pl.BlockSpec((npair, ts, LANES), lambda i: (0, nb - 1 - i, 0)), rblk(FOX_W, 0), rblk(FOX_W, 0),
                  rblk(SB_W, 0), rblk(SB_W, 0), rblk(SB_W, 0), rblk(SB_W, 0),
                  rblk(POOL_W, FOX_W // POOL_W), halo(POOL_W, FOX_W // POOL_W), rblk(POOL_W, 0), rblk(POOL_W, 0),
                  full(qg), full(kg), full(bfp), full(wpd), full(ps)],
        out_specs=[rblk(PW, 0), acc_spec(1, FOX_W), acc_spec(1, FOX_W), acc_spec(1, LANES), acc_spec(POOL_W, POOL_W), acc_spec(1, POOL_W)],
        out_shape=[jax.ShapeDtypeStruct((S, PW), BF16), jax.ShapeDtypeStruct((1, FOX_W), F32), jax.ShapeDtypeStruct((1, FOX_W), F32),
                   jax.ShapeDtypeStruct((1, LANES), F32), jax.ShapeDtypeStruct((POOL_W, POOL_W), F32), jax.ShapeDtypeStruct((1, POOL_W), F32)],
        scratch_shapes=[pltpu.VMEM((1, LANES), F32), pltpu.VMEM((ts, LANES), F32), pltpu.VMEM((ts + POOL_HALO, POOL_W), F32),
                        pltpu.VMEM((LANES, ts), F32)],
        compiler_params=_cparams(dimension_semantics=("arbitrary",)),
    )(projm, projm, projm, projm, ffo, dqn, dkn, dct, dcr, dv, dfg, dsq, dsk, dsv, dsg, dmix, dmix, pooled, yp, qg, kg, bfp, wpd, ps)


def _stack_call(body, name, grid, in_specs, operands, slot_specs, slot_shapes, stacks, plain_specs=(), plain_shapes=(), **kw):
    out_specs = list(plain_specs) + list(slot_specs)
    out_shape = list(plain_shapes) + [jax.ShapeDtypeStruct((DEPTH,) + s, F32) for s in slot_shapes]
    if stacks is None:
        return pl.pallas_call(body, name=name, grid=grid, in_specs=in_specs, out_specs=out_specs, out_shape=out_shape, **kw)(*operands)
    n = len(operands)

    def aliased_body(*refs):
        body(*refs[:n], *refs[n + len(stacks):])

    return pl.pallas_call(
        aliased_body, name=name, grid=grid, in_specs=list(in_specs) + [pl.BlockSpec(memory_space=pl.ANY)] * len(stacks),
        out_specs=out_specs, out_shape=out_shape,
        input_output_aliases={n + k: len(plain_specs) + k for k in range(len(stacks))}, **kw)(*operands, *stacks)


def _inproj_dw(h, dproj, layer, stacks, *, ts, tn):
    S, D = h.shape
    nj = PM // tn

    def body(h_ref, dp_ref, dpf_ref, dw_ref, dwf_ref):
        s = pl.program_id(1)

        @pl.when(s == 0)
        def _():
            dw_ref[...] = jnp.zeros_like(dw_ref)

        @pl.when((s == 0) & (pl.program_id(0) == 0))
        def _():
            dwf_ref[...] = jnp.zeros_like(dwf_ref)

        hv = h_ref[...]
        dw_ref[...] += _dot_tn(dp_ref[...], hv)

        @pl.when(pl.program_id(0) == 0)
        def _():
            dwf_ref[...] += _dot_tn(dpf_ref[...], hv)

    return _stack_call(
        body, "inproj_dw", (nj, S // ts),
        [pl.BlockSpec((ts, D), lambda j, s: (s, 0)),
         pl.BlockSpec((ts, tn), lambda j, s: (s, j)),
         pl.BlockSpec((ts, LANES), lambda j, s: (s, PM // LANES))],
        (h, dproj, dproj),
        [pl.BlockSpec((None, tn, D), lambda j, s: (layer, j, 0)), pl.BlockSpec((None, LANES, D), lambda j, s: (layer, 0, 0))],
        [(PM, D), (LANES, D)], stacks,
        compiler_params=_cparams(dimension_semantics=("arbitrary", "arbitrary")))


def _inproj_dx(dproj, wt_all, layer, x, g, dy, *, tm):
    S, D = x.shape

    def body(dp_ref, w_ref, x_ref, g_ref, dy_ref, dx_ref, dg_ref):
        @pl.when(pl.program_id(0) == 0)
        def _():
            dg_ref[...] = jnp.zeros_like(dg_ref)

        dh = _dot(dp_ref[...], w_ref[...])
        xf = x_ref[...]
        rstd = lax.rsqrt(jnp.mean(xf * xf, axis=-1, keepdims=True) + EPS)
        xhat = xf * rstd
        dg_ref[...] += jnp.sum(dh * xhat, axis=0, keepdims=True)
        dyg = dh * g_ref[...]
        mean = jnp.mean(dyg * xhat, axis=-1, keepdims=True)
        dx_ref[...] = rstd * (dyg - xhat * mean) + dy_ref[...]

    row = lambda w: pl.BlockSpec((tm, w), lambda i: (i, 0))
    return pl.pallas_call(
        body, name="inproj_dx", grid=(S // tm,),
        in_specs=[row(PW), pl.BlockSpec((None, PW, D), lambda i: (layer, 0, 0)), row(D), pl.BlockSpec((1, D), lambda i: (0, 0)), row(D)],
        out_specs=[row(D), pl.BlockSpec((1, D), lambda i: (0, 0))],
        out_shape=[jax.ShapeDtypeStruct((S, D), F32), jax.ShapeDtypeStruct((1, D), F32)],
        compiler_params=_cparams(dimension_semantics=("arbitrary",)),
    )(dproj, wt_all, x, g, dy)


def _adam_update(w, g, m, v):
    nm = ADAM_B1 * m + (1.0 - ADAM_B1) * g
    nv = ADAM_B2 * v + (1.0 - ADAM_B2) * (g * g)
    m_hat = nm / (1.0 - ADAM_B1 ** ADAM_STEP)
    v_hat = nv / (1.0 - ADAM_B2 ** ADAM_STEP)
    return -ADAM_LR * (m_hat / (jnp.sqrt(v_hat) + ADAM_EPS) + ADAM_WD * w), nm, nv


def _adamw(w, g, m, v):
    L, R, C = w.shape
    tr = R if R <= 512 else 256

    def body(w_ref, g_ref, m_ref, v_ref, d_ref, nm_ref, nv_ref):
        d_ref[...], nm_ref[...], nv_ref[...] = _adam_update(w_ref[...], g_ref[...], m_ref[...], v_ref[...])

    spec = pl.BlockSpec((1, tr, C), lambda l, i: (l, i, 0))
    shp = jax.ShapeDtypeStruct((L, R, C), F32)
    return pl.pallas_call(
        body, name="adamw", grid=(L, R // tr), in_specs=[spec] * 4, out_specs=[spec] * 3, out_shape=[shp] * 3,
        compiler_params=_cparams(dimension_semantics=("arbitrary", "arbitrary")),
    )(w, g, m, v)


def _adamw_nd(w, g, m, v):
    shape = w.shape
    view = (1,) + shape if w.ndim == 2 else (shape[0], -1, shape[-1])
    outs = _adamw(w.reshape(view), g.reshape(view), m.reshape(view), v.reshape(view))
    return tuple(o.reshape(shape) for o in outs)


FLIP_C = (0, 0, 1)
FLIP_X = (1, 0, 0)
FLIP_Y = (0, 1, 0)
FLIP_XY = (1, 1, 0)
MESH = pl.DeviceIdType.MESH


def _peer(flip):
    me = (lax.axis_index("x"), lax.axis_index("y"), lax.axis_index("c"))
    return tuple(1 - a if f else a for a, f in zip(me, flip))


def _exchange(name, arrays, flips):
    n = len(arrays)

    def body(*refs):
        srcs, dsts = refs[:n], refs[n:2 * n]
        send_sems, recv_sems = refs[2 * n:]
        copies = [pltpu.make_async_remote_copy(src_ref=srcs[k], dst_ref=dsts[k], send_sem=send_sems.at[k], recv_sem=recv_sems.at[k],
                                               device_id=_peer(flips[k]), device_id_type=MESH) for k in range(n)]
        for cp in copies:
            cp.start()
        for cp in copies:
            cp.wait()

    anyspec = pl.BlockSpec(memory_space=pl.ANY)
    return pl.pallas_call(
        body, name=name, in_specs=[anyspec] * n, out_specs=[anyspec] * n,
        out_shape=[jax.ShapeDtypeStruct(a.shape, a.dtype) for a in arrays],
        scratch_shapes=[pltpu.SemaphoreType.DMA((n,)), pltpu.SemaphoreType.DMA((n,))],
    )(*arrays)


def _exchange_add(name, x, flip):
    def body(x_ref, o_ref, buf_ref, send_sem, recv_sem):
        cp = pltpu.make_async_remote_copy(src_ref=x_ref, dst_ref=buf_ref, send_sem=send_sem, recv_sem=recv_sem,
                                          device_id=_peer(flip), device_id_type=MESH)
        cp.start()
        cp.wait()
        o_ref[...] = x_ref[...] + buf_ref[...]

    vspec = pl.BlockSpec(memory_space=pltpu.VMEM)
    return pl.pallas_call(
        body, name=name, in_specs=[vspec], out_specs=vspec, out_shape=jax.ShapeDtypeStruct(x.shape, x.dtype),
        scratch_shapes=[pltpu.VMEM(x.shape, x.dtype), pltpu.SemaphoreType.DMA, pltpu.SemaphoreType.DMA],
    )(x)


def _chip_index():
    return 2 * lax.axis_index("x") + lax.axis_index("y")


def _gather_weights(w_in_t, w_out):
    wi = w_in_t.astype(BF16)
    wo = jnp.swapaxes(w_out, 0, 1).astype(BF16)
    halves = (wi.shape[0] // 2, wo.shape[0] // 2)
    masks = (2, 1, 3)
    flips = (FLIP_X, FLIP_Y, FLIP_XY)
    n_first = 2 * len(masks)

    def body(wi_ref, wo_ref, gi_ref, go_ref, send_sems, recv_sems, local_sems):
        c = lax.axis_index("c")
        j = _chip_index()
        srcs = (wi_ref, wo_ref)
        dsts = (gi_ref, go_ref)
        mine = [pl.ds(h * c, h) for h in halves]
        theirs = [pl.ds(h * (1 - c), h) for h in halves]

        def copy(idx, src, dst, flip):
            return pltpu.make_async_remote_copy(src_ref=src, dst_ref=dst, send_sem=send_sems.at[idx], recv_sem=recv_sems.at[idx],
                                                device_id=_peer(flip), device_id_type=MESH)

        local = [pltpu.make_async_copy(srcs[a], dsts[a].at[j], local_sems.at[a]) for a in range(2)]
        for cp in local:
            cp.start()
        first = [copy(2 * k + a, srcs[a].at[mine[a]], dsts[a].at[j, mine[a]], flips[k]) for k in range(len(masks)) for a in range(2)]
        for cp in first:
            cp.start()
        passed = []
        for k, m in enumerate(masks):
            for a in range(2):
                slot = dsts[a].at[j ^ m, mine[a]]
                copy(2 * k + a, slot, slot, flips[k]).wait_recv()
                fwd = copy(n_first + 2 * k + a, slot, slot, FLIP_C)
                fwd.start()
                passed.append(fwd)
        for k, m in enumerate(masks):
            for a in range(2):
                slot = dsts[a].at[j ^ m, theirs[a]]
                copy(n_first + 2 * k + a, slot, slot, FLIP_C).wait_recv()
        for cp in first + passed:
            cp.wait_send()
        for cp in local:
            cp.wait()

    anyspec = pl.BlockSpec(memory_space=pl.ANY)
    gi, go = pl.pallas_call(
        body, name="gather_weights", in_specs=[anyspec] * 2, out_specs=[anyspec] * 2,
        out_shape=[jax.ShapeDtypeStruct((4,) + wi.shape, BF16), jax.ShapeDtypeStruct((4,) + wo.shape, BF16)],
        scratch_shapes=[pltpu.SemaphoreType.DMA((2 * n_first,)), pltpu.SemaphoreType.DMA((2 * n_first,)), pltpu.SemaphoreType.DMA((2,))],
    )(wi, wo)
    w_in_t_full = gi.reshape((4 * wi.shape[0],) + wi.shape[1:])
    w_out_full = jnp.swapaxes(go.reshape((4 * wo.shape[0],) + wo.shape[1:]), 0, 1)
    return w_in_t_full, w_out_full


def _to_aligned(w_t):
    _, L, D = w_t.shape
    npair = FOX_HEADS // 2
    ff = w_t[ORIG_FF:ORIG_REST].reshape(npair, 2, L, D)
    ff = jnp.pad(ff, ((0, 0), (0, FF_STRIDE - 2), (0, 0), (0, 0))).reshape(npair * FF_STRIDE, L, D)
    ff = jnp.pad(ff, ((0, LANES - npair * FF_STRIDE), (0, 0), (0, 0)))
    return jnp.swapaxes(jnp.concatenate([w_t[:ORIG_FOX], w_t[ORIG_REST:], ff], axis=0), 0, 1)


def _from_aligned(dw_t):
    n, _, D = dw_t.shape
    npair = FOX_HEADS // 2
    ff = dw_t[:, PM:PM + npair * FF_STRIDE].reshape(n, npair, FF_STRIDE, D)[:, :, :2].reshape(n, FOX_HEADS, D)
    return jnp.swapaxes(jnp.concatenate([dw_t[:, :ORIG_FOX], ff, dw_t[:, ORIG_FOX:PM]], axis=1), 0, 1)


def _half_layers(name, stack, got):
    L, R, C = stack.shape
    half = L // 2
    tr = min(256, R)
    c = lax.axis_index("c")
    which = ((1 - c) if got is None else c).astype(jnp.int32).reshape(1)

    def body(c_ref, x_ref, *refs):
        if got is None:
            refs[0][...] = x_ref[...].astype(BF16)
        else:
            acc = x_ref[...] + refs[0][...].astype(F32)
            refs[1][...] = acc
            refs[2][...] = acc.astype(BF16)

    plain = pl.BlockSpec((1, tr, C), lambda l, i, c_ref: (l, i, 0))
    picked = pl.BlockSpec((1, tr, C), lambda l, i, c_ref: (c_ref[0] * half + l, i, 0))
    shp = lambda dt: jax.ShapeDtypeStruct((half, R, C), dt)
    grid_spec = pltpu.PrefetchScalarGridSpec(
        num_scalar_prefetch=1, grid=(half, R // tr),
        in_specs=[picked] + ([] if got is None else [plain]), out_specs=[plain] if got is None else [plain, plain])
    return pl.pallas_call(
        body, name=name, grid_spec=grid_spec, out_shape=[shp(BF16)] if got is None else [shp(F32), shp(BF16)],
        compiler_params=_cparams(dimension_semantics=("arbitrary", "arbitrary")),
    )(which, stack, *([] if got is None else [got]))


def _reduce_scatter(stack_m, stack_f, stack_o, shard_cols, shard_rows):
    j = _chip_index()
    half = DEPTH // 2
    stacks = (stack_m, stack_f, stack_o)
    give = [_half_layers("rs_give", s, None)[0] for s in stacks]
    got = _exchange("rs_d2d", give, (FLIP_C,) * len(stacks))
    (m32, mbf), (f32_, fbf), (o32, obf) = [_half_layers("rs_add_chip", s, g) for s, g in zip(stacks, got)]
    d_model = stack_m.shape[2]

    def in_shards(m, f):
        return _from_aligned(jnp.concatenate([m, f], axis=1)).reshape(4, shard_cols, half, d_model)

    def out_shards(o):
        return jnp.moveaxis(o.reshape(half, 4, shard_rows, o.shape[-1]), 1, 0)

    chip = [(in_shards(m32, f32_), in_shards(mbf, fbf)), (out_shards(o32), out_shards(obf))]
    masks = (2, 1, 3)
    flips = (FLIP_X, FLIP_Y, FLIP_XY)
    sends, sflips = [], []
    for _, bf in chip:
        for m, fl in zip(masks, flips):
            sends.append(lax.dynamic_index_in_dim(bf, j ^ m, axis=0, keepdims=False))
            sflips.append(fl)
    got = _exchange("rs_ici", sends, tuple(sflips))
    own_in, own_out = [lax.dynamic_index_in_dim(f32_sum, j, axis=0, keepdims=False) for f32_sum, _ in chip]
    mine_in = _add_rows("rs_add_in", own_in, list(got[0:3]))
    mine_out = _add_into_half("rs_add_out", own_out, list(got[3:6]))
    sib_in, g_out = _share_halves(mine_in, mine_out)
    return (mine_in, sib_in), g_out


def _add_rows(name, first, others):
    n = len(others)

    def body(*refs):
        acc = refs[0][...]
        for r in refs[1:1 + n]:
            acc = acc + r[...].astype(F32)
        refs[1 + n][...] = acc

    grid, spec = _row_lane_blocks(first.shape)
    return pl.pallas_call(
        body, name=name, grid=grid, in_specs=[spec(first.shape[1])] * (1 + n), out_specs=spec(first.shape[1]),
        out_shape=jax.ShapeDtypeStruct(first.shape, F32),
        compiler_params=_cparams(dimension_semantics=("arbitrary", "arbitrary")),
    )(first, *others)


def _row_lane_blocks(shape):
    rows, _, C = shape
    tr = rows // 2 if rows % 2 == 0 and rows > 64 else rows
    return (rows // tr, C // LANES), lambda n_mid: pl.BlockSpec((tr, n_mid, LANES), lambda i, k, *_: (i, 0, k))


def _add_into_half(name, first, others):
    half, rows, C = first.shape
    tr = min(256, rows)
    n = len(others)

    def body(c_ref, *refs):
        acc = refs[0][...]
        for r in refs[1:1 + n]:
            acc = acc + r[...].astype(F32)
        refs[1 + n][...] = acc

    grid_spec = pltpu.PrefetchScalarGridSpec(
        num_scalar_prefetch=1, grid=(half, rows // tr),
        in_specs=[pl.BlockSpec((1, tr, C), lambda l, i, c_ref: (l, i, 0))] * (1 + n),
        out_specs=pl.BlockSpec((1, tr, C), lambda l, i, c_ref: (c_ref[0] * half + l, i, 0)))
    return pl.pallas_call(
        body, name=name, grid_spec=grid_spec, out_shape=jax.ShapeDtypeStruct((2 * half, rows, C), F32),
        compiler_params=_cparams(dimension_semantics=("arbitrary", "arbitrary")),
    )(lax.axis_index("c").astype(jnp.int32).reshape(1), first, *others)


def _share_halves(mine, buf):
    half = DEPTH // 2

    def body(mine_ref, buf_in, sib_ref, buf_ref, send_sems, recv_sems):
        lay = pl.ds(half * lax.axis_index("c"), half)
        copies = [pltpu.make_async_remote_copy(src_ref=src, dst_ref=dst, send_sem=send_sems.at[k], recv_sem=recv_sems.at[k],
                                               device_id=_peer(FLIP_C), device_id_type=MESH)
                  for k, (src, dst) in enumerate(((mine_ref, sib_ref), (buf_ref.at[lay], buf_ref.at[lay])))]
        for cp in copies:
            cp.start()
        for cp in copies:
            cp.wait()

    anyspec = pl.BlockSpec(memory_space=pl.ANY)
    return pl.pallas_call(
        body, name="rs_share", in_specs=[anyspec] * 2, out_specs=[anyspec] * 2,
        out_shape=[jax.ShapeDtypeStruct(mine.shape, mine.dtype), jax.ShapeDtypeStruct(buf.shape, buf.dtype)],
        input_output_aliases={1: 1},
        scratch_shapes=[pltpu.SemaphoreType.DMA((2,)), pltpu.SemaphoreType.DMA((2,))],
    )(mine, buf)


def _adamw_halves(w, g_mine, g_sib, m, v):
    half = g_mine.shape[1]

    def body(c_ref, w_ref, gm_ref, gs_ref, m_ref, v_ref, g_ref, d_ref, nm_ref, nv_ref):
        first = c_ref[0] == 0
        gm, gs = gm_ref[...], gs_ref[...]
        for h, gv in enumerate((jnp.where(first, gm, gs), jnp.where(first, gs, gm))):
            lay = slice(half * h, half * (h + 1))
            g_ref[:, lay, :] = gv
            d_ref[:, lay, :], nm_ref[:, lay, :], nv_ref[:, lay, :] = _adam_update(w_ref[:, lay, :], gv, m_ref[:, lay, :], v_ref[:, lay, :])

    grid, spec = _row_lane_blocks(w.shape)
    full, part = spec(w.shape[1]), spec(half)
    grid_spec = pltpu.PrefetchScalarGridSpec(num_scalar_prefetch=1, grid=grid, in_specs=[full, part, part, full, full], out_specs=[full] * 4)
    return pl.pallas_call(
        body, name="adamw_halves", grid_spec=grid_spec, out_shape=[jax.ShapeDtypeStruct(w.shape, F32)] * 4,
        compiler_params=_cparams(dimension_semantics=("arbitrary", "arbitrary")),
    )(lax.axis_index("c").astype(jnp.int32).reshape(1), w, g_mine, g_sib, m, v)


def _all_reduce_small(x):
    x = _exchange_add("ar_c", x, FLIP_C)
    x = _exchange_add("ar_y", x, FLIP_Y)
    return _exchange_add("ar_x", x, FLIP_X)


def _blocks(S):
    return dict(tm=min(512, S), tm_proj=min(1024, S), ts=min(512, S), tq=min(512, S), tk=min(512, S), tks=min(256, S))


def _pair_pad(vec):
    npair = FOX_HEADS // 2
    v = jnp.pad(vec.reshape(npair, 2), ((0, 0), (0, FF_STRIDE - 2))).reshape(1, npair * FF_STRIDE)
    return jnp.pad(v, ((0, 0), (0, LANES - npair * FF_STRIDE)))


def _pair_unpad(row):
    npair = FOX_HEADS // 2
    return row[0, :npair * FF_STRIDE].reshape(npair, FF_STRIDE)[:, :2].reshape(FOX_HEADS)


def _pool_blockdiag(w_pool):
    g, cg, _ = w_pool.shape
    eye = jnp.eye(g, dtype=w_pool.dtype)
    return jnp.einsum("gh,gcd->gchd", eye, w_pool).reshape(g * cg, g * cg)


def _layer_params(norm_g, b_f, q_norm_g, k_norm_g, w_pool, pool_scale):
    return dict(g=norm_g.reshape(1, -1), qg=jnp.tile(q_norm_g, FOX_HEADS).reshape(1, FOX_W), kg=jnp.tile(k_norm_g, FOX_HEADS).reshape(1, FOX_W),
                bfp=_pair_pad(b_f), wpd=_pool_blockdiag(w_pool).astype(BF16), ps=pool_scale.reshape(1, POOL_W))


def _layer_fwd(x, wt_all, w_out, layer, prm, bs):
    projm, ffo, h = _inproj(x, prm["g"], wt_all, layer, tm=bs["tm_proj"], tn=PROJ_TN)
    qn, ka, kb, v, sq, sk, sv, pooled, yp, pm = _prep(projm, ffo, prm["qg"], prm["kg"], prm["bfp"], prm["wpd"], prm["ps"], ts=bs["ts"])
    o, lse, fm = _fox_fwd(qn, ka, kb, v, projm, tq=bs["tq"], tk=bs["tk"])
    so, sm = _sb_fwd(sq, sk, sv, projm, tq=bs["tq"], tk=bs["tks"])
    y = _outproj(x, fm, pm, sm, w_out, layer, tm=bs["tm"])
    saved = dict(x=x, projm=projm, ffo=ffo, h=h, qn=qn, ka=ka, kb=kb, v=v, sq=sq, sk=sk, sv=sv, pooled=pooled, yp=yp,
                 o=o, lse=lse, so=so, fm=fm, pm=pm, sm=sm)
    return y, saved


def _layer_bwd(dy, wt_all, w_out, prm, sv_, bs, layer, stacks):
    dmix, stack_o = _outproj_bwd(dy, sv_["fm"], sv_["pm"], sv_["sm"], w_out, layer, None if stacks is None else stacks[2:], tm=bs["tm"])
    dqn, dkn, dv, dfg, dct, dcr = _fox_bwd(sv_["qn"], sv_["ka"], sv_["kb"], sv_["v"], sv_["o"], sv_["lse"], dmix, sv_["projm"],
                                      tq=bs["tq"], tk=bs["tk"])
    dsq, dsk, dsv, dsg = _sb_bwd(sv_["sq"], sv_["sk"], sv_["sv"], sv_["so"], dmix, sv_["projm"], tq=bs["tq"], tk=bs["tks"])
    dproj, dqg, dkg, dbf, dwp, dps = _prep_bwd(sv_["projm"], sv_["ffo"], dqn, dkn, dct, dcr, dv, dfg, dsq, dsk, dsv, dsg, dmix,
                                               sv_["pooled"], sv_["yp"], prm["qg"], prm["kg"], prm["bfp"], prm["wpd"], prm["ps"], ts=bs["ts"])
    stack_m, stack_f = _inproj_dw(sv_["h"], dproj, layer, None if stacks is None else stacks[:2], ts=bs["tm_proj"], tn=PROJ_TN)
    dx, dg = _inproj_dx(dproj, wt_all, layer, sv_["x"], prm["g"], dy, tm=min(256, bs["tm"]))
    grads = dict(
        norm_g=dg[0],
        b_f=_pair_unpad(dbf), q_norm_g=dqg.reshape(FOX_HEADS, HEAD_DIM).sum(0), k_norm_g=dkg.reshape(FOX_HEADS, HEAD_DIM).sum(0),
        w_pool=jnp.stack([dwp[HEAD_DIM * g:HEAD_DIM * (g + 1), HEAD_DIM * g:HEAD_DIM * (g + 1)] for g in range(4)]),
        pool_scale=dps[0])
    return dx, grads, (stack_m, stack_f, stack_o)


def _local_step(x, target, wt_all, w_out, norm_g, b_f, q_norm_g, k_norm_g, w_pool, pool_scale):
    S, D = x.shape
    bs = _blocks(S)
    prms = [_layer_params(norm_g[l], b_f[l], q_norm_g[l], k_norm_g[l], w_pool[l], pool_scale[l]) for l in range(DEPTH)]
    saved = []
    y = x
    for l in range(DEPTH):
        y, s_ = _layer_fwd(y, wt_all, w_out, l, prms[l], bs)
        saved.append(s_)
    dy, sq = _loss_head(y, target, tm=bs["tm"])
    loss = 0.5 * jnp.sum(sq) / D
    grads = [None] * DEPTH
    stacks = None
    for l in reversed(range(DEPTH)):
        dy, grads[l], stacks = _layer_bwd(dy, wt_all, w_out, prms[l], saved[l], bs, l, stacks)
    stacked = {k: jnp.stack([g[k] for g in grads]) for k in grads[0]}
    return loss, dy, stacked, stacks


SMALL = ("norm_g", "b_f", "q_norm_g", "k_norm_g", "w_pool", "pool_scale")


def _pack_small(gr):
    flat = jnp.concatenate([gr[k].reshape(-1) for k in SMALL])
    pad = (-flat.shape[0]) % (8 * LANES)
    return jnp.pad(flat, (0, pad)).reshape(-1, LANES)


def _unpack_small(packed, like):
    flat = packed.reshape(-1)
    out, off = {}, 0
    for k in SMALL:
        n = like[k].size
        out[k] = flat[off:off + n].reshape(like[k].shape)
        off += n
    return out


def kernel(x, norm_g, w_in, b_f, q_norm_g, k_norm_g, w_pool, pool_scale, w_out, loss_target, m_norm_g, m_w_in, m_b_f, m_q_norm_g, m_k_norm_g, m_w_pool, m_pool_scale, m_w_out, v_norm_g, v_w_in, v_b_f, v_q_norm_g, v_k_norm_g, v_w_pool, v_pool_scale, v_w_out):
    weights = dict(norm_g=norm_g, w_in=w_in, b_f=b_f, q_norm_g=q_norm_g, k_norm_g=k_norm_g, w_pool=w_pool, pool_scale=pool_scale, w_out=w_out)
    mom_m = dict(norm_g=m_norm_g, w_in=m_w_in, b_f=m_b_f, q_norm_g=m_q_norm_g, k_norm_g=m_k_norm_g, w_pool=m_w_pool, pool_scale=m_pool_scale, w_out=m_w_out)
    mom_v = dict(norm_g=v_norm_g, w_in=v_w_in, b_f=v_b_f, q_norm_g=v_q_norm_g, k_norm_g=v_k_norm_g, w_pool=v_w_pool, pool_scale=v_pool_scale, w_out=v_w_out)
    shard_cols = w_in.shape[2]
    shard_rows = w_out.shape[1]

    cols_first = lambda a: jnp.transpose(a, (2, 0, 1))
    w_in_t = cols_first(w_in)
    w_in_t_full, w_out_full = _gather_weights(w_in_t, w_out)
    wt_all = _to_aligned(w_in_t_full)
    loss, dx, gr, stacks = _local_step(x[0], loss_target[0], wt_all, w_out_full, norm_g, b_f, q_norm_g, k_norm_g, w_pool, pool_scale)
    loss = lax.psum(loss, ("x", "y", "c"))

    (g_in_mine, g_in_sib), g_w_out = _reduce_scatter(*stacks, shard_cols, shard_rows)
    small = _unpack_small(_all_reduce_small(_pack_small(gr)), {k: weights[k] for k in SMALL})
    grad_w = dict(small, w_out=g_w_out)

    names = ("norm_g", "w_in", "b_f", "q_norm_g", "k_norm_g", "w_pool", "pool_scale", "w_out")
    upd = {k: _adamw_nd(weights[k], grad_w[k], mom_m[k], mom_v[k]) for k in names if k != "w_in"}
    in_t = _adamw_halves(w_in_t, g_in_mine, g_in_sib, cols_first(mom_m["w_in"]), cols_first(mom_v["w_in"]))
    grad_w["w_in"], *upd["w_in"] = [jnp.transpose(a, (1, 2, 0)) for a in in_t]
    return (loss, dx[None], *[grad_w[k] for k in names], *[upd[k][0] for k in names], *[upd[k][1] for k in names], *[upd[k][2] for k in names])
```

```python
import functools

import jax
import jax.numpy as jnp
from jax import lax
from jax.experimental import pallas as pl
from jax.experimental.pallas import tpu as pltpu

F32 = jnp.float32
BF16 = jnp.bfloat16

DEPTH = 4
HEAD_DIM = 64
FOX_HEADS = 8
SB_HEADS = 4
FOX_W = FOX_HEADS * HEAD_DIM
SB_W = SB_HEADS * HEAD_DIM
POOL_W = 256
POOL_WINDOWS = (2, 4, 8, 16)
POOL_HALO = 16
D_MIX = FOX_W + POOL_W + SB_W
EPS = 1e-6
NEG = -1e30
QK_SCALE = HEAD_DIM ** -0.5

ORIG_FOX = 4 * FOX_W
ORIG_FF = ORIG_FOX
ORIG_REST = ORIG_FF + FOX_HEADS
D_IN = ORIG_REST + 2 * POOL_W + 4 * SB_W

C_FQ, C_FK, C_FV, C_FG = 0, FOX_W, 2 * FOX_W, 3 * FOX_W
C_PX = 4 * FOX_W
C_PG = C_PX + POOL_W
C_SQ = C_PG + POOL_W
C_SK, C_SV, C_SG = C_SQ + SB_W, C_SQ + 2 * SB_W, C_SQ + 3 * SB_W
PM = C_SG + SB_W
LANES = 128
PW = PM + LANES
FF_STRIDE = 8
AUG = 3

ADAM_LR = 0.001
ADAM_B1 = 0.9
ADAM_B2 = 0.999
ADAM_EPS = 1e-08
ADAM_WD = 0.01
ADAM_STEP = 10

VMEM_LIMIT = 48 * 1024 * 1024
PROJ_TN = PM // 2


def _cparams(**kw):
    return pltpu.CompilerParams(vmem_limit_bytes=VMEM_LIMIT, **kw)


def _dot(a, b):
    return jnp.dot(a, b, preferred_element_type=F32)


def _dot_nt(a, b):
    return lax.dot_general(a, b, (((1,), (1,)), ((), ())), preferred_element_type=F32)


def _dot_tn(a, b):
    return lax.dot_general(a, b, (((0,), (0,)), ((), ())), preferred_element_type=F32)


def _split2(x):
    hi = x.astype(BF16)
    lo = (x - hi.astype(F32)).astype(BF16)
    return hi, lo


def _split3(x):
    hi = x.astype(BF16)
    r = x - hi.astype(F32)
    mid = r.astype(BF16)
    lo = (r - mid.astype(F32)).astype(BF16)
    return hi, mid, lo


def _dot_exact_rhs(x, m):
    hi, mid, lo = _split3(x)
    return _dot(hi, m) + _dot(mid, m) + _dot(lo, m)


def _dot_exact_lhs(m, x):
    hi, mid, lo = _split3(x)
    return _dot(m, hi) + _dot(m, mid) + _dot(m, lo)


def _sigmoid(x):
    return 1.0 / (1.0 + jnp.exp(-x))


def _silu_pair(x):
    s = _sigmoid(x)
    return x * s, s * (1.0 + x * (1.0 - s))


def _iota(shape, dim):
    return lax.broadcasted_iota(jnp.int32, shape, dim)


def _ones_where(cond):
    return jnp.where(cond, 1.0, 0.0).astype(BF16)


def _head_blockdiag(w):
    return _ones_where((_iota((w, w), 0) >> 6) == (_iota((w, w), 1) >> 6))


def _group_sum(x, bd):
    hi, lo = _split2(x)
    return _dot(hi, bd) + _dot(lo, bd)


def _lane_pick(x, lane_idx, lane):
    return jnp.sum(jnp.where(lane_idx == lane, x, 0.0), axis=1, keepdims=True)


def _inproj(x, g, wt_all, layer, *, tm, tn):
    S, D = x.shape
    nj = PM // tn

    def body(x_ref, g_ref, w_ref, wff_ref, proj_ref, ff_ref, h_ref):
        @pl.when(pl.program_id(1) == 0)
        def _():
            xf = x_ref[...]
            ms = jnp.mean(xf * xf, axis=-1, keepdims=True)
            h = (xf * lax.rsqrt(ms + EPS) * g_ref[...]).astype(BF16)
            h_ref[...] = h
            ff_ref[...] = _dot_nt(h, wff_ref[...])

        proj_ref[...] = _dot_nt(h_ref[...], w_ref[...])

    return pl.pallas_call(
        body, name="inproj", grid=(S // tm, nj),
        in_specs=[pl.BlockSpec((tm, D), lambda i, j: (i, 0)),
                  pl.BlockSpec((1, D), lambda i, j: (0, 0)),
                  pl.BlockSpec((None, tn, D), lambda i, j: (layer, j, 0)),
                  pl.BlockSpec((None, LANES, D), lambda i, j: (layer, PM // LANES, 0))],
        out_specs=[pl.BlockSpec((tm, tn), lambda i, j: (i, j)),
                   pl.BlockSpec((tm, LANES), lambda i, j: (i, 0)),
                   pl.BlockSpec((tm, D), lambda i, j: (i, 0))],
        out_shape=[jax.ShapeDtypeStruct((S, PM), F32), jax.ShapeDtypeStruct((S, LANES), F32),
                   jax.ShapeDtypeStruct((S, D), BF16)],
        compiler_params=_cparams(dimension_semantics=("arbitrary", "arbitrary")),
    )(x, g, wt_all, wt_all)


def _pool_group_select(lane_group, vals):
    return jnp.where(lane_group == 0, vals[0], jnp.where(lane_group == 1, vals[1], jnp.where(lane_group == 2, vals[2], vals[3])))


def _prep(projm, ffo, qg, kg, bfp, wpd, ps, *, ts):
    S = projm.shape[0]
    nb = S // ts
    hb = ts // POOL_HALO

    def body(fq_ref, fk_ref, fv_ref, pp_ref, halo_ref, ff_ref, sq_ref, sk_ref, sv_ref,
             qg_ref, kg_ref, bf_ref, wpd_ref, ps_ref,
             qn_ref, ka_ref, kb_ref, v_ref, sqo_ref, sko_ref, svo_ref, pooled_ref, yp_ref, pm_ref,
             carry_ref, c_ref, buf_ref):
        i = pl.program_id(0)
        bd = _head_blockdiag(FOX_W)
        normed = []
        for src, g_ref in ((fq_ref, qg_ref), (fk_ref, kg_ref)):
            q = src[...]
            ss = _group_sum(q * q, bd)
            normed.append(q * lax.rsqrt(ss * (1.0 / HEAD_DIM) + EPS) * g_ref[...])
        qn_ref[...] = (normed[0] * QK_SCALE).astype(BF16)
        kn = normed[1]
        v_ref[...] = fv_ref[...].astype(BF16)
        sqo_ref[...] = (sq_ref[...] * QK_SCALE).astype(BF16)
        sko_ref[...] = sk_ref[...].astype(BF16)
        svo_ref[...] = sv_ref[...].astype(BF16)

        @pl.when(i == 0)
        def _():
            carry_ref[...] = jnp.zeros_like(carry_ref)

        z = ff_ref[...] + bf_ref[...]
        lf = jnp.minimum(z, 0.0) - jnp.log(1.0 + jnp.exp(-jnp.abs(z)))
        tri = _ones_where(_iota((ts, ts), 1) <= _iota((ts, ts), 0))
        c = _dot_exact_lhs(tri, lf) + carry_ref[...]
        c_ref[...] = c
        carry_ref[...] = c_ref[ts - 1:ts, :]
        parts = jnp.concatenate(_split3(-c), axis=1)
        row = _iota((AUG * LANES, FOX_W), 0)
        col = _iota((AUG * LANES, FOX_W), 1)
        part, src = row >> 7, row & (LANES - 1)
        pair, off = col >> 7, col & (LANES - 1)
        sel_a = _ones_where((src == FF_STRIDE * pair) & (off == HEAD_DIM + part))
        sel_b = _ones_where((src == FF_STRIDE * pair + 1) & (off == part))
        first_half = (_iota((1, FOX_W), 1) & HEAD_DIM) == 0
        ka_ref[...] = jnp.where(first_half, kn, _dot(parts, sel_a)).astype(BF16)
        kb_ref[...] = jnp.where(first_half, _dot(parts, sel_b), kn).astype(BF16)

        x = pp_ref[:, 0:POOL_W]
        pg = pp_ref[:, POOL_W:2 * POOL_W]
        halo = jnp.where(i > 0, halo_ref[:, 0:POOL_W], 0.0)
        buf_ref[0:POOL_HALO, :] = halo
        buf_ref[POOL_HALO:POOL_HALO + ts, :] = x
        acc = x
        snaps = []
        for d in range(1, POOL_HALO):
            acc = acc + buf_ref[pl.ds(POOL_HALO - d, ts), :]
            if d + 1 in POOL_WINDOWS:
                snaps.append(acc)
        lane_group = _iota((1, POOL_W), 1) >> 6
        wsum = _pool_group_select(lane_group, snaps)
        wlen = _pool_group_select(lane_group, [float(w) for w in POOL_WINDOWS])
        tpos = (i * ts + _iota((ts, 1), 0) + 1).astype(F32)
        pooled = wsum / jnp.minimum(tpos, wlen) - x
        pb = pooled.astype(BF16)
        pooled_ref[...] = pb
        yp = _dot(pb, wpd_ref[...])
        yp_ref[...] = yp
        pm_ref[...] = (yp * ps_ref[...] * (pg * _sigmoid(pg))).astype(BF16)

    blk = lambda w, c: pl.BlockSpec((ts, w), lambda i: (i, c))
    full = lambda a: pl.BlockSpec(a.shape, lambda i: (0,) * a.ndim)
    out_shapes = [
        jax.ShapeDtypeStruct((S, FOX_W), BF16), jax.ShapeDtypeStruct((S, FOX_W), BF16), jax.ShapeDtypeStruct((S, FOX_W), BF16),
        jax.ShapeDtypeStruct((S, FOX_W), BF16),
        jax.ShapeDtypeStruct((S, SB_W), BF16), jax.ShapeDtypeStruct((S, SB_W), BF16), jax.ShapeDtypeStruct((S, SB_W), BF16),
        jax.ShapeDtypeStruct((S, POOL_W), BF16), jax.ShapeDtypeStruct((S, POOL_W), F32), jax.ShapeDtypeStruct((S, POOL_W), BF16),
    ]
    out_specs = [
        blk(FOX_W, 0), blk(FOX_W, 0), blk(FOX_W, 0), blk(FOX_W, 0),
        blk(SB_W, 0), blk(SB_W, 0), blk(SB_W, 0),
        blk(POOL_W, 0), blk(POOL_W, 0), blk(POOL_W, 0),
    ]
    return pl.pallas_call(
        body, name="prep", grid=(nb,),
        in_specs=[blk(FOX_W, C_FQ // FOX_W), blk(FOX_W, C_FK // FOX_W), blk(FOX_W, C_FV // FOX_W), blk(2 * POOL_W, C_PX // (2 * POOL_W)),
                  pl.BlockSpec((POOL_HALO, 2 * POOL_W), lambda i: (jnp.maximum(i * hb - 1, 0), C_PX // (2 * POOL_W))),
                  blk(LANES, 0),
                  blk(SB_W, C_SQ // SB_W), blk(SB_W, C_SK // SB_W), blk(SB_W, C_SV // SB_W),
                  full(qg), full(kg), full(bfp), full(wpd), full(ps)],
        out_specs=out_specs, out_shape=out_shapes,
        scratch_shapes=[pltpu.VMEM((1, LANES), F32), pltpu.VMEM((ts, LANES), F32), pltpu.VMEM((ts + POOL_HALO, POOL_W), F32)],
        compiler_params=_cparams(dimension_semantics=("arbitrary",)),
    )(projm, projm, projm, projm, projm, ffo, projm, projm, projm, qg, kg, bfp, wpd, ps)


def _pair_masks(x):
    ma = _iota((1, LANES), 1) < HEAD_DIM
    zero = jnp.zeros_like(x)
    return jnp.where(ma, x, zero), jnp.where(ma, zero, x)


DIAG_TILE = 256


def _diag_tiles(tq, size=DIAG_TILE):
    size = min(tq, size)
    return [(t * size, size) for t in range(tq // size)]


def _put_rows(old, new, r0):
    return new if r0 == 0 else jnp.concatenate([old[:r0], new], axis=0)


def _aug_queries(q):
    lane = _iota((1, LANES), 1)
    one = jnp.ones_like(q)
    zero = jnp.zeros_like(q)
    qa = jnp.where(lane < HEAD_DIM, q, jnp.where(lane < HEAD_DIM + AUG, one, zero))
    qb = jnp.where(lane >= HEAD_DIM, q, jnp.where(lane < AUG, one, zero))
    return qa, qb


def _fox_fwd(qn, ka, kb, v, projm, *, tq, tk):
    S = qn.shape[0]
    npair = FOX_HEADS // 2

    def body(q_ref, ka_ref, kb_ref, v_ref, fg_ref, o_ref, lse_ref, fm_ref):
        qi = pl.program_id(1)
        lane = _iota((1, LANES), 1)
        ma = lane < HEAD_DIM
        qaug = _aug_queries(q_ref[...])
        k_refs = (ka_ref, kb_ref)

        def block(k0, tkl, r0, carry, masked):
            vb = v_ref[pl.ds(k0, tkl), :]
            if masked:
                mask = (k0 + _iota((tq - r0, tkl), 1)) <= (qi * tq + r0 + _iota((tq - r0, tkl), 0))
            scores = [_dot_nt(qaug[h][r0:], k_refs[h][pl.ds(k0, tkl), :]) for h in range(2)]
            new = []
            for h in range(2):
                m, l, acc = [x[r0:] for x in carry[h]]
                s = jnp.where(mask, scores[h], NEG) if masked else scores[h]
                m_new = jnp.maximum(m, jnp.max(s, axis=1, keepdims=True))
                alpha = jnp.exp(m - m_new)
                p = jnp.exp(s - m_new)
                sub = (m_new, alpha * l + jnp.sum(p, axis=1, keepdims=True), alpha * acc + _dot(p.astype(BF16), vb))
                new.append(tuple(_put_rows(old, x, r0) for old, x in zip(carry[h], sub)))
            return tuple(new)

        init = tuple((jnp.full((tq, 1), NEG, F32), jnp.zeros((tq, 1), F32), jnp.zeros((tq, LANES), F32)) for _ in range(2))
        carry = lax.fori_loop(0, (qi * tq) // tk, lambda j, c: block(pl.multiple_of(j * tk, tk), tk, 0, c, False), init)
        for off, size in _diag_tiles(tq, tq):
            carry = block(pl.multiple_of(qi * tq + off, size), size, off, carry, True)
        (ma_, la, acca), (mb_, lb, accb) = carry
        o = jnp.where(ma, acca / la, accb / lb)
        o_ref[...] = o
        lse_ref[...] = jnp.where(ma, ma_ + jnp.log(la), mb_ + jnp.log(lb))
        fg = fg_ref[...]
        fm_ref[...] = (o * (fg * _sigmoid(fg))).astype(BF16)

    qblk = pl.BlockSpec((tq, LANES), lambda p, i: (i, p))
    kvblk = pl.BlockSpec((S, LANES), lambda p, i: (0, p))
    return pl.pallas_call(
        body, name="fox_fwd", grid=(npair, S // tq),
        in_specs=[qblk, kvblk, kvblk, kvblk,
                  pl.BlockSpec((tq, LANES), lambda p, i: (i, C_FG // LANES + p))],
        out_specs=[qblk, qblk, qblk],
        out_shape=[jax.ShapeDtypeStruct((S, FOX_W), F32), jax.ShapeDtypeStruct((S, FOX_W), F32), jax.ShapeDtypeStruct((S, FOX_W), BF16)],
        compiler_params=_cparams(dimension_semantics=("arbitrary", "arbitrary")),
    )(qn, ka, kb, v, projm)


def _suffix_sums(x, tmat2):
    return _dot(jnp.concatenate(_split2(x), axis=1), tmat2)


def _suffix_matrix(tk, inclusive):
    rr, cc = _iota((2 * tk, tk), 0) & (tk - 1), _iota((2 * tk, tk), 1)
    return _ones_where(rr >= cc) if inclusive else _ones_where(rr > cc)


def _sb_scores(qh, kb, causal, tmat2, r_runs):
    heads = range(2)
    zs = [_dot_nt(qh[h], kb) for h in heads]
    nsps = [jnp.minimum(-z, 0.0) - jnp.log(1.0 + jnp.exp(-jnp.abs(z))) for z in zs]
    lbs = nsps if causal is None else [jnp.where(causal, n, 0.0) for n in nsps]
    rins = [_suffix_sums(lb, tmat2) for lb in lbs]
    args = [zs[h] + lbs[h] + (rins[h] + r_runs[h]) for h in heads]
    a_s = [jnp.exp(arg if causal is None else jnp.where(causal, arg, NEG)) for arg in args]
    return zs, nsps, lbs, a_s


def _sb_fwd(sq, sk, sv, projm, *, tq, tk):
    S = sq.shape[0]
    npair = SB_HEADS // 2

    def body(q_ref, k_ref, v_ref, sg_ref, o_ref, sm_ref):
        qi = pl.program_id(1)
        lane = _iota((1, LANES), 1)
        ma = lane < HEAD_DIM
        qh = _pair_masks(q_ref[...])
        tmat2 = _suffix_matrix(tk, inclusive=False)
        nfull = (qi * tq) // tk

        def block(k0, r0, carry, masked):
            nr = tq - r0
            kb = k_ref[pl.ds(k0, tk), :]
            vb = v_ref[pl.ds(k0, tk), :]
            causal = (k0 + _iota((nr, tk), 1)) < (qi * tq + r0 + _iota((nr, tk), 0)) if masked else None
            _, _, lbs, a_s = _sb_scores([q[r0:] for q in qh], kb, causal, tmat2, [carry[h][0][r0:] for h in range(2)])
            pv = _dot(jnp.concatenate([a.astype(BF16) for a in a_s], axis=0), vb)
            return tuple((_put_rows(carry[h][0], carry[h][0][r0:] + jnp.sum(lbs[h], axis=1, keepdims=True), r0),
                          _put_rows(carry[h][1], carry[h][1][r0:] + pv[h * nr:(h + 1) * nr], r0)) for h in range(2))

        carry = tuple((jnp.zeros((tq, 1), F32), jnp.zeros((tq, LANES), F32)) for _ in range(2))
        for off, size in reversed(_diag_tiles(tq)):
            assert size == tk
            carry = block(pl.multiple_of(qi * tq + off, tk), off, carry, True)
        (_, acca), (_, accb) = lax.fori_loop(0, nfull, lambda jj, c: block(pl.multiple_of((nfull - 1 - jj) * tk, tk), 0, c, False), carry)
        o = jnp.where(ma, acca, accb)
        o_ref[...] = o
        sg = sg_ref[...]
        sm_ref[...] = (o * (sg * _sigmoid(sg))).astype(BF16)

    qblk = pl.BlockSpec((tq, LANES), lambda p, i: (i, p))
    kvblk = pl.BlockSpec((S, LANES), lambda p, i: (0, p))
    return pl.pallas_call(
        body, name="sb_fwd", grid=(npair, S // tq),
        in_specs=[qblk, kvblk, kvblk, pl.BlockSpec((tq, LANES), lambda p, i: (i, C_SG // LANES + p))],
        out_specs=[qblk, qblk],
        out_shape=[jax.ShapeDtypeStruct((S, SB_W), F32), jax.ShapeDtypeStruct((S, SB_W), BF16)],
        compiler_params=_cparams(dimension_semantics=("arbitrary", "arbitrary")),
    )(sq, sk, sv, projm)


def _outproj(x, fm, pm, sm, w_out, layer, *, tm):
    S, D = x.shape

    def body(x_ref, fm_ref, pm_ref, sm_ref, w_ref, y_ref):
        y = x_ref[...] + _dot(fm_ref[...], w_ref[0:FOX_W, :])
        y = y + _dot(pm_ref[...], w_ref[FOX_W:FOX_W + POOL_W, :])
        y_ref[...] = y + _dot(sm_ref[...], w_ref[FOX_W + POOL_W:D_MIX, :])

    row = lambda w: pl.BlockSpec((tm, w), lambda i: (i, 0))
    return pl.pallas_call(
        body, name="outproj", grid=(S // tm,),
        in_specs=[row(D), row(FOX_W), row(POOL_W), row(SB_W), pl.BlockSpec((None, D_MIX, D), lambda i: (layer, 0, 0))],
        out_specs=row(D), out_shape=jax.ShapeDtypeStruct((S, D), F32),
        compiler_params=_cparams(dimension_semantics=("arbitrary",)),
    )(x, fm, pm, sm, w_out)


def _loss_head(y, target, *, tm):
    S, D = y.shape

    def body(y_ref, t_ref, dy_ref, sq_ref):
        @pl.when(pl.program_id(0) == 0)
        def _():
            sq_ref[...] = jnp.zeros_like(sq_ref)

        d = y_ref[...] - t_ref[...]
        dy_ref[...] = d * (1.0 / D)
        sq_ref[...] += jnp.sum(d * d, axis=0, keepdims=True)

    row = pl.BlockSpec((tm, D), lambda i: (i, 0))
    return pl.pallas_call(
        body, name="loss_head", grid=(S // tm,),
        in_specs=[row, row], out_specs=[row, pl.BlockSpec((1, D), lambda i: (0, 0))],
        out_shape=[jax.ShapeDtypeStruct((S, D), F32), jax.ShapeDtypeStruct((1, D), F32)],
        compiler_params=_cparams(dimension_semantics=("arbitrary",)),
    )(y, target)


def _outproj_bwd(dy, fm, pm, sm, w_out, layer, stacks, *, tm):
    S, D = dy.shape

    def body(dy_ref, fm_ref, pm_ref, sm_ref, w_ref, dm_ref, dw_ref):
        @pl.when(pl.program_id(0) == 0)
        def _():
            dw_ref[...] = jnp.zeros_like(dw_ref)

        dyb = dy_ref[...].astype(BF16)
        dm_ref[...] = _dot_nt(dyb, w_ref[...])
        dw_ref[0:FOX_W, :] += _dot_tn(fm_ref[...], dyb)
        dw_ref[FOX_W:FOX_W + POOL_W, :] += _dot_tn(pm_ref[...], dyb)
        dw_ref[FOX_W + POOL_W:D_MIX, :] += _dot_tn(sm_ref[...], dyb)

    row = lambda w: pl.BlockSpec((tm, w), lambda i: (i, 0))
    wspec = pl.BlockSpec((None, D_MIX, D), lambda i: (layer, 0, 0))
    return _stack_call(
        body, "outproj_bwd", (S // tm,), [row(D), row(FOX_W), row(POOL_W), row(SB_W), wspec], (dy, fm, pm, sm, w_out),
        [pl.BlockSpec((None, D_MIX, D), lambda i: (layer, 0, 0))], [(D_MIX, D)], stacks,
        plain_specs=[row(D_MIX)], plain_shapes=[jax.ShapeDtypeStruct((S, D_MIX), F32)],
        compiler_params=_cparams(dimension_semantics=("arbitrary",)))


def _fox_bwd(qn, ka, kb, v, o, lse, dmix, projm, *, tq, tk):
    S = qn.shape[0]
    npair = FOX_HEADS // 2

    def body(q_ref, ka_ref, kb_ref, v_ref, o_ref, lse_ref, dm_ref, fg_ref,
             dq_ref, dk_ref, dv_ref, dfg_ref, dct_ref, dcr_ref):
        qi = pl.program_id(1)

        @pl.when(qi == 0)
        def _():
            dk_ref[...] = jnp.zeros_like(dk_ref)
            dv_ref[...] = jnp.zeros_like(dv_ref)
            dct_ref[...] = jnp.zeros_like(dct_ref)

        lane = _iota((1, LANES), 1)
        ma = lane < HEAD_DIM
        qh = _pair_masks(q_ref[...])
        qaug = _aug_queries(q_ref[...])
        k_refs = (ka_ref, kb_ref)
        lsev = lse_ref[...]
        lse = (_lane_pick(lsev, lane, 0), _lane_pick(lsev, lane, HEAD_DIM))
        fg = fg_ref[...]
        silu, dsilu = _silu_pair(fg)
        dm = dm_ref[...]
        ov = o_ref[...]
        do = dm * silu
        dfg_ref[...] = dm * ov * dsilu
        dd = do * ov
        dsum = (jnp.sum(jnp.where(ma, dd, 0.0), axis=1, keepdims=True), jnp.sum(jnp.where(ma, 0.0, dd), axis=1, keepdims=True))
        doh = _pair_masks(do.astype(BF16))

        def block(k0, tkl, r0, carry, masked):
            vb = v_ref[pl.ds(k0, tkl), :]
            if masked:
                mask = (k0 + _iota((tq - r0, tkl), 1)) <= (qi * tq + r0 + _iota((tq - r0, tkl), 0))
            heads = range(2)
            kaugs = [k_refs[h][pl.ds(k0, tkl), :] for h in heads]
            scores = [_dot_nt(qaug[h][r0:], kaugs[h]) for h in heads]
            dps = [_dot_nt(doh[h][r0:], vb) for h in heads]
            ps, dss, rows = [], [], []
            for h in heads:
                s = jnp.where(mask, scores[h], NEG) if masked else scores[h]
                p = jnp.exp(s - lse[h][r0:])
                dsf = p * (dps[h] - dsum[h][r0:])
                dct_ref[0, h:h + 1, pl.ds(k0, tkl)] -= jnp.sum(dsf, axis=0, keepdims=True)
                rows.append(_put_rows(carry[1 + h], carry[1 + h][r0:] + jnp.sum(dsf, axis=1, keepdims=True), r0))
                ps.append(p.astype(BF16))
                dss.append(dsf.astype(BF16))
            dv_ref[pl.ds(k0, tkl), :] += _dot_tn(jnp.concatenate(ps, axis=0), jnp.concatenate([d[r0:] for d in doh], axis=0))
            dk_ref[pl.ds(k0, tkl), :] += _dot_tn(jnp.concatenate(dss, axis=0), jnp.concatenate([q[r0:] for q in qh], axis=0))
            kh = jnp.concatenate([_pair_masks(kaugs[h])[h] for h in heads], axis=0)
            dq = _put_rows(carry[0], carry[0][r0:] + _dot(jnp.concatenate(dss, axis=1), kh), r0)
            return (dq, rows[0], rows[1])

        zcol = jnp.zeros((tq, 1), F32)
        carry = lax.fori_loop(0, (qi * tq) // tk, lambda j, c: block(pl.multiple_of(j * tk, tk), tk, 0, c, False),
                              (jnp.zeros((tq, LANES), F32), zcol, zcol))
        for off, size in _diag_tiles(tq):
            carry = block(pl.multiple_of(qi * tq + off, size), size, off, carry, True)
        dq, rowa, rowb = carry
        dq_ref[...] = dq * QK_SCALE
        dcr_ref[0] = jnp.where(ma, rowa, rowb)

    qblk = pl.BlockSpec((tq, LANES), lambda p, i: (i, p))
    kvblk = pl.BlockSpec((S, LANES), lambda p, i: (0, p))
    f32out = jax.ShapeDtypeStruct((S, FOX_W), F32)
    ctblk = pl.BlockSpec((1, FF_STRIDE, S), lambda p, i: (p, 0, 0))
    return pl.pallas_call(
        body, name="fox_bwd", grid=(npair, S // tq),
        in_specs=[qblk, kvblk, kvblk, kvblk, qblk, qblk, qblk,
                  pl.BlockSpec((tq, LANES), lambda p, i: (i, C_FG // LANES + p))],
        out_specs=[qblk, kvblk, kvblk, qblk, ctblk, pl.BlockSpec((1, tq, LANES), lambda p, i: (p, i, 0))],
        out_shape=[f32out, f32out, f32out, f32out, jax.ShapeDtypeStruct((npair, FF_STRIDE, S), F32),
                   jax.ShapeDtypeStruct((npair, S, LANES), F32)],
        compiler_params=_cparams(dimension_semantics=("arbitrary", "arbitrary")),
    )(qn, ka, kb, v, o, lse, dmix, projm)


def _sb_bwd(sq, sk, sv, o, dmix, projm, *, tq, tk):
    S = sq.shape[0]
    npair = SB_HEADS // 2
    mix0 = (FOX_W + POOL_W) // LANES

    def body(q_ref, k_ref, v_ref, o_ref, dm_ref, sg_ref, dq_ref, dk_ref, dv_ref, dsg_ref):
        qi = pl.program_id(1)

        @pl.when(qi == 0)
        def _():
            dk_ref[...] = jnp.zeros_like(dk_ref)
            dv_ref[...] = jnp.zeros_like(dv_ref)

        lane = _iota((1, LANES), 1)
        ma = lane < HEAD_DIM
        qh = _pair_masks(q_ref[...])
        sg = sg_ref[...]
        silu, dsilu = _silu_pair(sg)
        dm = dm_ref[...]
        ov = o_ref[...]
        do = dm * silu
        dsg_ref[...] = dm * ov * dsilu
        dob = do.astype(BF16)
        dd = dob.astype(F32) * ov
        dsum = (jnp.sum(jnp.where(ma, dd, 0.0), axis=1, keepdims=True), jnp.sum(jnp.where(ma, 0.0, dd), axis=1, keepdims=True))
        doh = _pair_masks(dob)
        tmat2 = _suffix_matrix(tk, inclusive=False)
        tmat2_inc = _suffix_matrix(tk, inclusive=True)
        nfull = (qi * tq) // tk

        def block(k0, r0, carry, masked):
            nr = tq - r0
            kb = k_ref[pl.ds(k0, tk), :]
            vb = v_ref[pl.ds(k0, tk), :]
            kh = _pair_masks(kb)
            causal = (k0 + _iota((nr, tk), 1)) < (qi * tq + r0 + _iota((nr, tk), 0)) if masked else None
            heads = range(2)
            qs = [q[r0:] for q in qh]
            dos = [d[r0:] for d in doh]
            das = [_dot_nt(dos[h], vb) for h in heads]
            zs, nsps, lbs, a_s = _sb_scores(qs, kb, causal, tmat2, [carry[h][0][r0:] for h in heads])
            abs_ = [a.astype(BF16) for a in a_s]
            us = [abs_[h].astype(F32) * das[h] for h in heads]
            uins = [_suffix_sums(u, tmat2_inc) for u in us]
            dzs = []
            for h in heads:
                cum_u = dsum[h][r0:] - (uins[h] + carry[h][1][r0:])
                dz = us[h] * jnp.exp(nsps[h]) - jnp.exp(zs[h] + nsps[h]) * cum_u
                if masked:
                    dz = jnp.where(causal, dz, 0.0)
                dzs.append(dz.astype(BF16))
            dv_ref[pl.ds(k0, tk), :] += _dot_tn(jnp.concatenate(abs_, axis=0), jnp.concatenate(dos, axis=0))
            dk_ref[pl.ds(k0, tk), :] += _dot_tn(jnp.concatenate(dzs, axis=0), jnp.concatenate(qs, axis=0))
            dq = _put_rows(carry[2], carry[2][r0:] + _dot(jnp.concatenate(dzs, axis=1), jnp.concatenate(kh, axis=0)), r0)
            new = [(_put_rows(carry[h][0], carry[h][0][r0:] + jnp.sum(lbs[h], axis=1, keepdims=True), r0),
                    _put_rows(carry[h][1], carry[h][1][r0:] + jnp.sum(us[h], axis=1, keepdims=True), r0)) for h in heads]
            return (new[0], new[1], dq)

        zcol = jnp.zeros((tq, 1), F32)
        carry = ((zcol, zcol), (zcol, zcol), jnp.zeros((tq, LANES), F32))
        for off, size in reversed(_diag_tiles(tq)):
            assert size == tk
            carry = block(pl.multiple_of(qi * tq + off, tk), off, carry, True)
        dq = lax.fori_loop(0, nfull, lambda jj, c: block(pl.multiple_of((nfull - 1 - jj) * tk, tk), 0, c, False), carry)[2]
        dq_ref[...] = dq * QK_SCALE

    qblk = pl.BlockSpec((tq, LANES), lambda p, i: (i, p))
    kvblk = pl.BlockSpec((S, LANES), lambda p, i: (0, p))
    f32out = jax.ShapeDtypeStruct((S, SB_W), F32)
    return pl.pallas_call(
        body, name="sb_bwd", grid=(npair, S // tq),
        in_specs=[qblk, kvblk, kvblk, qblk,
                  pl.BlockSpec((tq, LANES), lambda p, i: (i, mix0 + p)),
                  pl.BlockSpec((tq, LANES), lambda p, i: (i, C_SG // LANES + p))],
        out_specs=[qblk, kvblk, kvblk, qblk],
        out_shape=[f32out, f32out, f32out, f32out],
        compiler_params=_cparams(dimension_semantics=("arbitrary", "arbitrary")),
    )(sq, sk, sv, o, dmix, projm)


def _prep_bwd(projm, ffo, dqn, dkn, dct, dcr, dv, dfg, dsq, dsk, dsv, dsg, dmix, pooled, yp, qg, kg, bfp, wpd, ps, *, ts):
    S = projm.shape[0]
    nb = S // ts
    hb = ts // POOL_HALO
    npair = FOX_HEADS // 2
    last_halo = S // POOL_HALO - 1

    def body(fq_ref, fk_ref, pp_ref, pph_ref, ff_ref,
             dqn_ref, dkn_ref, dct_ref, dcr_ref, dv_ref, dfg_ref, dsq_ref, dsk_ref, dsv_ref, dsg_ref,
             dmp_ref, dmh_ref, pooled_ref, yp_ref, qg_ref, kg_ref, bf_ref, wpd_ref, ps_ref,
             dp_ref, dqg_ref, dkg_ref, dbf_ref, dwp_ref, dps_ref,
             carry_ref, dl_ref, buf_ref, dct_s):
        i = pl.program_id(0)
        blk = nb - 1 - i

        @pl.when(i == 0)
        def _():
            carry_ref[...] = jnp.zeros_like(carry_ref)
            dqg_ref[...] = jnp.zeros_like(dqg_ref)
            dkg_ref[...] = jnp.zeros_like(dkg_ref)
            dbf_ref[...] = jnp.zeros_like(dbf_ref)
            dwp_ref[...] = jnp.zeros_like(dwp_ref)
            dps_ref[...] = jnp.zeros_like(dps_ref)

        bd = _head_blockdiag(FOX_W)
        for raw_ref, g_ref, dn, dg_ref, col in ((fq_ref, qg_ref, dqn_ref[...], dqg_ref, C_FQ), (fk_ref, kg_ref, dkn_ref[...], dkg_ref, C_FK)):
            q = raw_ref[...]
            rstd = lax.rsqrt(_group_sum(q * q, bd) * (1.0 / HEAD_DIM) + EPS)
            xhat = q * rstd
            dg_ref[...] += jnp.sum(dn * xhat, axis=0, keepdims=True)
            dyg = dn * g_ref[...]
            mean = _group_sum(dyg * xhat, bd) * (1.0 / HEAD_DIM)
            dp_ref[:, col:col + FOX_W] = (rstd * (dyg - xhat * mean)).astype(BF16)
        dp_ref[:, C_FV:C_FV + FOX_W] = dv_ref[...].astype(BF16)
        dp_ref[:, C_FG:C_FG + FOX_W] = dfg_ref[...].astype(BF16)
        dp_ref[:, C_SQ:C_SQ + SB_W] = dsq_ref[...].astype(BF16)
        dp_ref[:, C_SK:C_SK + SB_W] = dsk_ref[...].astype(BF16)
        dp_ref[:, C_SV:C_SV + SB_W] = dsv_ref[...].astype(BF16)
        dp_ref[:, C_SG:C_SG + SB_W] = dsg_ref[...].astype(BF16)

        dct_s[...] = jnp.zeros_like(dct_s)
        for p in range(npair):
            dct_s[FF_STRIDE * p:FF_STRIDE * (p + 1), :] = dct_ref[p]
        dc = dct_s[...].T
        lane = _iota((1, LANES), 1)
        for p in range(npair):
            dcr = dcr_ref[p]
            dc = dc + jnp.where(lane == FF_STRIDE * p, _lane_pick(dcr, lane, 0), 0.0)
            dc = dc + jnp.where(lane == FF_STRIDE * p + 1, _lane_pick(dcr, lane, HEAD_DIM), 0.0)
        triu = _ones_where(_iota((ts, ts), 1) >= _iota((ts, ts), 0))
        dlf = _dot_exact_lhs(triu, dc) + carry_ref[...]
        dl_ref[...] = dlf
        carry_ref[...] = dl_ref[0:1, :]
        z = ff_ref[...] + bf_ref[...]
        dff = dlf * (1.0 / (1.0 + jnp.exp(z)))
        dbf_ref[...] += jnp.sum(dff, axis=0, keepdims=True)
        dp_ref[:, PM:PW] = dff.astype(BF16)

        psv = ps_ref[...]
        wpdv = wpd_ref[...]
        lane_group = _iota((1, POOL_W), 1) >> 6
        wlen = _pool_group_select(lane_group, [float(w) for w in POOL_WINDOWS])
        pg = pp_ref[:, POOL_W:2 * POOL_W]
        silu, dsilu = _silu_pair(pg)
        dmp = dmp_ref[...]
        ypv = yp_ref[...]
        dp_ref[:, C_PG:C_PG + POOL_W] = (dmp * (ypv * psv) * dsilu).astype(BF16)
        dps_ref[...] += jnp.sum(dmp * silu * ypv, axis=0, keepdims=True)
        dyp = (dmp * psv * silu).astype(BF16)
        dwp_ref[...] += _dot_tn(pooled_ref[...], dyp)
        dpooled = _dot_nt(dyp, wpdv)
        pgh = pph_ref[:, POOL_W:2 * POOL_W]
        dyph = (dmh_ref[...] * psv * (pgh * _sigmoid(pgh))).astype(BF16)
        dpooled_h = jnp.where(blk < nb - 1, _dot_nt(dyph, wpdv), 0.0)
        tpos = (blk * ts + _iota((ts, 1), 0) + 1).astype(F32)
        ev = dpooled / jnp.minimum(tpos, wlen)
        buf_ref[0:ts, :] = ev
        buf_ref[ts:ts + POOL_HALO, :] = dpooled_h / wlen
        acc = ev
        snaps = []
        for d in range(1, POOL_HALO):
            acc = acc + buf_ref[pl.ds(d, ts), :]
            if d + 1 in POOL_WINDOWS:
                snaps.append(acc)
        dp_ref[:, C_PX:C_PX + POOL_W] = (_pool_group_select(lane_group, snaps) - dpooled).astype(BF16)

    rblk = lambda w, c: pl.BlockSpec((ts, w), lambda i: (nb - 1 - i, c))
    full = lambda a: pl.BlockSpec(a.shape, lambda i: (0,) * a.ndim)
    halo = lambda w, c: pl.BlockSpec((POOL_HALO, w), lambda i: (jnp.minimum((nb - i) * hb, last_halo), c))
    acc_spec = lambda r, w: pl.BlockSpec((r, w), lambda i: (0, 0))
    return pl.pallas_call(
        body, name="prep_bwd", grid=(nb,),
        in_specs=[rblk(FOX_W, C_FQ // FOX_W), rblk(FOX_W, C_FK // FOX_W), rblk(2 * POOL_W, C_PX // (2 * POOL_W)),
                  halo(2 * POOL_W, C_PX // (2 * POOL_W)), rblk(LANES, 0),
                  rblk(FOX_W, 0), rblk(FOX_W, 0), pl.BlockSpec((npair, FF_STRIDE, ts), lambda i: (0, 0, nb - 1 - i)),
                  pl.BlockSpec((npair, ts, LANES), lambda i: (0, nb - 1 - i, 0)), rblk(FOX_W, 0), rblk(FOX_W, 0),
                  rblk(SB_W, 0), rblk(SB_W, 0), rblk(SB_W, 0), rblk(SB_W, 0),
                  rblk(POOL_W, FOX_W // POOL_W), halo(POOL_W, FOX_W // POOL_W), rblk(POOL_W, 0), rblk(POOL_W, 0),
                  full(qg), full(kg), full(bfp), full(wpd), full(ps)],
        out_specs=[rblk(PW, 0), acc_spec(1, FOX_W), acc_spec(1, FOX_W), acc_spec(1, LANES), acc_spec(POOL_W, POOL_W), acc_spec(1, POOL_W)],
        out_shape=[jax.ShapeDtypeStruct((S, PW), BF16), jax.ShapeDtypeStruct((1, FOX_W), F32), jax.ShapeDtypeStruct((1, FOX_W), F32),
                   jax.ShapeDtypeStruct((1, LANES), F32), jax.ShapeDtypeStruct((POOL_W, POOL_W), F32), jax.ShapeDtypeStruct((1, POOL_W), F32)],
        scratch_shapes=[pltpu.VMEM((1, LANES), F32), pltpu.VMEM((ts, LANES), F32), pltpu.VMEM((ts + POOL_HALO, POOL_W), F32),
                        pltpu.VMEM((LANES, ts), F32)],
        compiler_params=_cparams(dimension_semantics=("arbitrary",)),
    )(projm, projm, projm, projm, ffo, dqn, dkn, dct, dcr, dv, dfg, dsq, dsk, dsv, dsg, dmix, dmix, pooled, yp, qg, kg, bfp, wpd, ps)


def _stack_call(body, name, grid, in_specs, operands, slot_specs, slot_shapes, stacks, plain_specs=(), plain_shapes=(), **kw):
    out_specs = list(plain_specs) + list(slot_specs)
    out_shape = list(plain_shapes) + [jax.ShapeDtypeStruct((DEPTH,) + s, F32) for s in slot_shapes]
    if stacks is None:
        return pl.pallas_call(body, name=name, grid=grid, in_specs=in_specs, out_specs=out_specs, out_shape=out_shape, **kw)(*operands)
    n = len(operands)

    def aliased_body(*refs):
        body(*refs[:n], *refs[n + len(stacks):])

    return pl.pallas_call(
        aliased_body, name=name, grid=grid, in_specs=list(in_specs) + [pl.BlockSpec(memory_space=pl.ANY)] * len(stacks),
        out_specs=out_specs, out_shape=out_shape,
        input_output_aliases={n + k: len(plain_specs) + k for k in range(len(stacks))}, **kw)(*operands, *stacks)


def _inproj_dw(h, dproj, layer, stacks, *, ts, tn):
    S, D = h.shape
    nj = PM // tn

    def body(h_ref, dp_ref, dpf_ref, dw_ref, dwf_ref):
        s = pl.program_id(1)

        @pl.when(s == 0)
        def _():
            dw_ref[...] = jnp.zeros_like(dw_ref)

        @pl.when((s == 0) & (pl.program_id(0) == 0))
        def _():
            dwf_ref[...] = jnp.zeros_like(dwf_ref)

        hv = h_ref[...]
        dw_ref[...] += _dot_tn(dp_ref[...], hv)

        @pl.when(pl.program_id(0) == 0)
        def _():
            dwf_ref[...] += _dot_tn(dpf_ref[...], hv)

    return _stack_call(
        body, "inproj_dw", (nj, S // ts),
        [pl.BlockSpec((ts, D), lambda j, s: (s, 0)),
         pl.BlockSpec((ts, tn), lambda j, s: (s, j)),
         pl.BlockSpec((ts, LANES), lambda j, s: (s, PM // LANES))],
        (h, dproj, dproj),
        [pl.BlockSpec((None, tn, D), lambda j, s: (layer, j, 0)), pl.BlockSpec((None, LANES, D), lambda j, s: (layer, 0, 0))],
        [(PM, D), (LANES, D)], stacks,
        compiler_params=_cparams(dimension_semantics=("arbitrary", "arbitrary")))


def _inproj_dx(dproj, wt_all, layer, x, g, dy, *, tm):
    S, D = x.shape

    def body(dp_ref, w_ref, x_ref, g_ref, dy_ref, dx_ref, dg_ref):
        @pl.when(pl.program_id(0) == 0)
        def _():
            dg_ref[...] = jnp.zeros_like(dg_ref)

        dh = _dot(dp_ref[...], w_ref[...])
        xf = x_ref[...]
        rstd = lax.rsqrt(jnp.mean(xf * xf, axis=-1, keepdims=True) + EPS)
        xhat = xf * rstd
        dg_ref[...] += jnp.sum(dh * xhat, axis=0, keepdims=True)
        dyg = dh * g_ref[...]
        mean = jnp.mean(dyg * xhat, axis=-1, keepdims=True)
        dx_ref[...] = rstd * (dyg - xhat * mean) + dy_ref[...]

    row = lambda w: pl.BlockSpec((tm, w), lambda i: (i, 0))
    return pl.pallas_call(
        body, name="inproj_dx", grid=(S // tm,),
        in_specs=[row(PW), pl.BlockSpec((None, PW, D), lambda i: (layer, 0, 0)), row(D), pl.BlockSpec((1, D), lambda i: (0, 0)), row(D)],
        out_specs=[row(D), pl.BlockSpec((1, D), lambda i: (0, 0))],
        out_shape=[jax.ShapeDtypeStruct((S, D), F32), jax.ShapeDtypeStruct((1, D), F32)],
        compiler_params=_cparams(dimension_semantics=("arbitrary",)),
    )(dproj, wt_all, x, g, dy)


def _adam_update(w, g, m, v):
    nm = ADAM_B1 * m + (1.0 - ADAM_B1) * g
    nv = ADAM_B2 * v + (1.0 - ADAM_B2) * (g * g)
    m_hat = nm / (1.0 - ADAM_B1 ** ADAM_STEP)
    v_hat = nv / (1.0 - ADAM_B2 ** ADAM_STEP)
    return -ADAM_LR * (m_hat / (jnp.sqrt(v_hat) + ADAM_EPS) + ADAM_WD * w), nm, nv


def _adamw(w, g, m, v):
    L, R, C = w.shape
    tr = R if R <= 512 else 256

    def body(w_ref, g_ref, m_ref, v_ref, d_ref, nm_ref, nv_ref):
        d_ref[...], nm_ref[...], nv_ref[...] = _adam_update(w_ref[...], g_ref[...], m_ref[...], v_ref[...])

    spec = pl.BlockSpec((1, tr, C), lambda l, i: (l, i, 0))
    shp = jax.ShapeDtypeStruct((L, R, C), F32)
    return pl.pallas_call(
        body, name="adamw", grid=(L, R // tr), in_specs=[spec] * 4, out_specs=[spec] * 3, out_shape=[shp] * 3,
        compiler_params=_cparams(dimension_semantics=("arbitrary", "arbitrary")),
    )(w, g, m, v)


def _adamw_nd(w, g, m, v):
    shape = w.shape
    view = (1,) + shape if w.ndim == 2 else (shape[0], -1, shape[-1])
    outs = _adamw(w.reshape(view), g.reshape(view), m.reshape(view), v.reshape(view))
    return tuple(o.reshape(shape) for o in outs)


FLIP_C = (0, 0, 1)
FLIP_X = (1, 0, 0)
FLIP_Y = (0, 1, 0)
FLIP_XY = (1, 1, 0)
MESH = pl.DeviceIdType.MESH


def _peer(flip):
    me = (lax.axis_index("x"), lax.axis_index("y"), lax.axis_index("c"))
    return tuple(1 - a if f else a for a, f in zip(me, flip))


def _exchange(name, arrays, flips):
    n = len(arrays)

    def body(*refs):
        srcs, dsts = refs[:n], refs[n:2 * n]
        send_sems, recv_sems = refs[2 * n:]
        copies = [pltpu.make_async_remote_copy(src_ref=srcs[k], dst_ref=dsts[k], send_sem=send_sems.at[k], recv_sem=recv_sems.at[k],
                                               device_id=_peer(flips[k]), device_id_type=MESH) for k in range(n)]
        for cp in copies:
            cp.start()
        for cp in copies:
            cp.wait()

    anyspec = pl.BlockSpec(memory_space=pl.ANY)
    return pl.pallas_call(
        body, name=name, in_specs=[anyspec] * n, out_specs=[anyspec] * n,
        out_shape=[jax.ShapeDtypeStruct(a.shape, a.dtype) for a in arrays],
        scratch_shapes=[pltpu.SemaphoreType.DMA((n,)), pltpu.SemaphoreType.DMA((n,))],
    )(*arrays)


def _exchange_add(name, x, flip):
    def body(x_ref, o_ref, buf_ref, send_sem, recv_sem):
        cp = pltpu.make_async_remote_copy(src_ref=x_ref, dst_ref=buf_ref, send_sem=send_sem, recv_sem=recv_sem,
                                          device_id=_peer(flip), device_id_type=MESH)
        cp.start()
        cp.wait()
        o_ref[...] = x_ref[...] + buf_ref[...]

    vspec = pl.BlockSpec(memory_space=pltpu.VMEM)
    return pl.pallas_call(
        body, name=name, in_specs=[vspec], out_specs=vspec, out_shape=jax.ShapeDtypeStruct(x.shape, x.dtype),
        scratch_shapes=[pltpu.VMEM(x.shape, x.dtype), pltpu.SemaphoreType.DMA, pltpu.SemaphoreType.DMA],
    )(x)


def _chip_index():
    return 2 * lax.axis_index("x") + lax.axis_index("y")


def _gather_weights(w_in_t, w_out):
    wi = w_in_t.astype(BF16)
    wo = jnp.swapaxes(w_out, 0, 1).astype(BF16)
    halves = (wi.shape[0] // 2, wo.shape[0] // 2)
    masks = (2, 1, 3)
    flips = (FLIP_X, FLIP_Y, FLIP_XY)
    n_first = 2 * len(masks)

    def body(wi_ref, wo_ref, gi_ref, go_ref, send_sems, recv_sems):
        c = lax.axis_index("c")
        j = _chip_index()
        srcs = (wi_ref, wo_ref)
        dsts = (gi_ref, go_ref)
        mine = [pl.ds(h * c, h) for h in halves]
        theirs = [pl.ds(h * (1 - c), h) for h in halves]

        def copy(idx, src, dst, flip):
            return pltpu.make_async_remote_copy(src_ref=src, dst_ref=dst, send_sem=send_sems.at[idx], recv_sem=recv_sems.at[idx],
                                                device_id=_peer(flip), device_id_type=MESH)

        first = [copy(2 * k + a, srcs[a].at[mine[a]], dsts[a].at[j, mine[a]], flips[k]) for k in range(len(masks)) for a in range(2)]
        for cp in first:
            cp.start()
        passed = []
        for k, m in enumerate(masks):
            for a in range(2):
                slot = dsts[a].at[j ^ m, mine[a]]
                copy(2 * k + a, slot, slot, flips[k]).wait_recv()
                fwd = copy(n_first + 2 * k + a, slot, slot, FLIP_C)
                fwd.start()
                passed.append(fwd)
        for k, m in enumerate(masks):
            for a in range(2):
                slot = dsts[a].at[j ^ m, theirs[a]]
                copy(n_first + 2 * k + a, slot, slot, FLIP_C).wait_recv()
        for cp in first + passed:
            cp.wait_send()

    anyspec = pl.BlockSpec(memory_space=pl.ANY)
    gi, go = pl.pallas_call(
        body, name="gather_weights", in_specs=[anyspec] * 2, out_specs=[anyspec] * 2,
        out_shape=[jax.ShapeDtypeStruct((4,) + wi.shape, BF16), jax.ShapeDtypeStruct((4,) + wo.shape, BF16)],
        scratch_shapes=[pltpu.SemaphoreType.DMA((2 * n_first,)), pltpu.SemaphoreType.DMA((2 * n_first,))],
    )(wi, wo)
    own = lax.broadcasted_iota(jnp.int32, (4, 1, 1, 1), 0) == _chip_index()
    gi = jnp.where(own, wi[None], gi)
    go = jnp.where(own, wo[None], go)
    w_in_t_full = gi.reshape((4 * wi.shape[0],) + wi.shape[1:])
    w_out_full = jnp.swapaxes(go.reshape((4 * wo.shape[0],) + wo.shape[1:]), 0, 1)
    return w_in_t_full, w_out_full


def _to_aligned(w_t):
    _, L, D = w_t.shape
    npair = FOX_HEADS // 2
    ff = w_t[ORIG_FF:ORIG_REST].reshape(npair, 2, L, D)
    ff = jnp.pad(ff, ((0, 0), (0, FF_STRIDE - 2), (0, 0), (0, 0))).reshape(npair * FF_STRIDE, L, D)
    ff = jnp.pad(ff, ((0, LANES - npair * FF_STRIDE), (0, 0), (0, 0)))
    return jnp.swapaxes(jnp.concatenate([w_t[:ORIG_FOX], w_t[ORIG_REST:], ff], axis=0), 0, 1)


def _from_aligned(dw_t):
    n, _, D = dw_t.shape
    npair = FOX_HEADS // 2
    ff = dw_t[:, PM:PM + npair * FF_STRIDE].reshape(n, npair, FF_STRIDE, D)[:, :, :2].reshape(n, FOX_HEADS, D)
    return jnp.swapaxes(jnp.concatenate([dw_t[:, :ORIG_FOX], ff, dw_t[:, ORIG_FOX:PM]], axis=1), 0, 1)


def _half_layers(name, stack, got):
    L, R, C = stack.shape
    half = L // 2
    tr = min(256, R)
    c = lax.axis_index("c")
    which = ((1 - c) if got is None else c).astype(jnp.int32).reshape(1)

    def body(c_ref, x_ref, *refs):
        if got is None:
            refs[0][...] = x_ref[...].astype(BF16)
        else:
            acc = x_ref[...] + refs[0][...].astype(F32)
            refs[1][...] = acc
            refs[2][...] = acc.astype(BF16)

    plain = pl.BlockSpec((1, tr, C), lambda l, i, c_ref: (l, i, 0))
    picked = pl.BlockSpec((1, tr, C), lambda l, i, c_ref: (c_ref[0] * half + l, i, 0))
    shp = lambda dt: jax.ShapeDtypeStruct((half, R, C), dt)
    grid_spec = pltpu.PrefetchScalarGridSpec(
        num_scalar_prefetch=1, grid=(half, R // tr),
        in_specs=[picked] + ([] if got is None else [plain]), out_specs=[plain] if got is None else [plain, plain])
    return pl.pallas_call(
        body, name=name, grid_spec=grid_spec, out_shape=[shp(BF16)] if got is None else [shp(F32), shp(BF16)],
        compiler_params=_cparams(dimension_semantics=("arbitrary", "arbitrary")),
    )(which, stack, *([] if got is None else [got]))


def _reduce_scatter(stack_m, stack_f, stack_o, shard_cols, shard_rows):
    j = _chip_index()
    half = DEPTH // 2
    stacks = (stack_m, stack_f, stack_o)
    give = [_half_layers("rs_give", s, None)[0] for s in stacks]
    got = _exchange("rs_d2d", give, (FLIP_C,) * len(stacks))
    (m32, mbf), (f32_, fbf), (o32, obf) = [_half_layers("rs_add_chip", s, g) for s, g in zip(stacks, got)]
    d_model = stack_m.shape[2]

    def in_shards(m, f):
        return _from_aligned(jnp.concatenate([m, f], axis=1)).reshape(4, shard_cols, half, d_model)

    def out_shards(o):
        return jnp.moveaxis(o.reshape(half, 4, shard_rows, o.shape[-1]), 1, 0)

    chip = [(in_shards(m32, f32_), in_shards(mbf, fbf)), (out_shards(o32), out_shards(obf))]
    masks = (2, 1, 3)
    flips = (FLIP_X, FLIP_Y, FLIP_XY)
    sends, sflips = [], []
    for _, bf in chip:
        for m, fl in zip(masks, flips):
            sends.append(lax.dynamic_index_in_dim(bf, j ^ m, axis=0, keepdims=False))
            sflips.append(fl)
    got = _exchange("rs_ici", sends, tuple(sflips))
    own_in, own_out = [lax.dynamic_index_in_dim(f32_sum, j, axis=0, keepdims=False) for f32_sum, _ in chip]
    mine_in = _add_rows("rs_add_in", own_in, list(got[0:3]))
    mine_out = _add_into_half("rs_add_out", own_out, list(got[3:6]))
    sib_in, g_out = _share_halves(mine_in, mine_out)
    return (mine_in, sib_in), g_out


def _add_rows(name, first, others):
    n = len(others)

    def body(*refs):
        acc = refs[0][...]
        for r in refs[1:1 + n]:
            acc = acc + r[...].astype(F32)
        refs[1 + n][...] = acc

    grid, spec = _row_lane_blocks(first.shape)
    return pl.pallas_call(
        body, name=name, grid=grid, in_specs=[spec(first.shape[1])] * (1 + n), out_specs=spec(first.shape[1]),
        out_shape=jax.ShapeDtypeStruct(first.shape, F32),
        compiler_params=_cparams(dimension_semantics=("arbitrary", "arbitrary")),
    )(first, *others)


def _row_lane_blocks(shape):
    rows, _, C = shape
    tr = rows // 2 if rows % 2 == 0 and rows > 64 else rows
    return (rows // tr, C // LANES), lambda n_mid: pl.BlockSpec((tr, n_mid, LANES), lambda i, k, *_: (i, 0, k))


def _add_into_half(name, first, others):
    half, rows, C = first.shape
    tr = min(256, rows)
    n = len(others)

    def body(c_ref, *refs):
        acc = refs[0][...]
        for r in refs[1:1 + n]:
            acc = acc + r[...].astype(F32)
        refs[1 + n][...] = acc

    grid_spec = pltpu.PrefetchScalarGridSpec(
        num_scalar_prefetch=1, grid=(half, rows // tr),
        in_specs=[pl.BlockSpec((1, tr, C), lambda l, i, c_ref: (l, i, 0))] * (1 + n),
        out_specs=pl.BlockSpec((1, tr, C), lambda l, i, c_ref: (c_ref[0] * half + l, i, 0)))
    return pl.pallas_call(
        body, name=name, grid_spec=grid_spec, out_shape=jax.ShapeDtypeStruct((2 * half, rows, C), F32),
        compiler_params=_cparams(dimension_semantics=("arbitrary", "arbitrary")),
    )(lax.axis_index("c").astype(jnp.int32).reshape(1), first, *others)


def _share_halves(mine, buf):
    half = DEPTH // 2

    def body(mine_ref, buf_in, sib_ref, buf_ref, send_sems, recv_sems):
        lay = pl.ds(half * lax.axis_index("c"), half)
        copies = [pltpu.make_async_remote_copy(src_ref=src, dst_ref=dst, send_sem=send_sems.at[k], recv_sem=recv_sems.at[k],
                                               device_id=_peer(FLIP_C), device_id_type=MESH)
                  for k, (src, dst) in enumerate(((mine_ref, sib_ref), (buf_ref.at[lay], buf_ref.at[lay])))]
        for cp in copies:
            cp.start()
        for cp in copies:
            cp.wait()

    anyspec = pl.BlockSpec(memory_space=pl.ANY)
    return pl.pallas_call(
        body, name="rs_share", in_specs=[anyspec] * 2, out_specs=[anyspec] * 2,
        out_shape=[jax.ShapeDtypeStruct(mine.shape, mine.dtype), jax.ShapeDtypeStruct(buf.shape, buf.dtype)],
        input_output_aliases={1: 1},
        scratch_shapes=[pltpu.SemaphoreType.DMA((2,)), pltpu.SemaphoreType.DMA((2,))],
    )(mine, buf)


def _adamw_halves(w, g_mine, g_sib, m, v):
    half = g_mine.shape[1]

    def body(c_ref, w_ref, gm_ref, gs_ref, m_ref, v_ref, g_ref, d_ref, nm_ref, nv_ref):
        first = c_ref[0] == 0
        gm, gs = gm_ref[...], gs_ref[...]
        for h, gv in enumerate((jnp.where(first, gm, gs), jnp.where(first, gs, gm))):
            lay = slice(half * h, half * (h + 1))
            g_ref[:, lay, :] = gv
            d_ref[:, lay, :], nm_ref[:, lay, :], nv_ref[:, lay, :] = _adam_update(w_ref[:, lay, :], gv, m_ref[:, lay, :], v_ref[:, lay, :])

    grid, spec = _row_lane_blocks(w.shape)
    full, part = spec(w.shape[1]), spec(half)
    grid_spec = pltpu.PrefetchScalarGridSpec(num_scalar_prefetch=1, grid=grid, in_specs=[full, part, part, full, full], out_specs=[full] * 4)
    return pl.pallas_call(
        body, name="adamw_halves", grid_spec=grid_spec, out_shape=[jax.ShapeDtypeStruct(w.shape, F32)] * 4,
        compiler_params=_cparams(dimension_semantics=("arbitrary", "arbitrary")),
    )(lax.axis_index("c").astype(jnp.int32).reshape(1), w, g_mine, g_sib, m, v)


def _all_reduce_small(x):
    x = _exchange_add("ar_c", x, FLIP_C)
    x = _exchange_add("ar_y", x, FLIP_Y)
    return _exchange_add("ar_x", x, FLIP_X)


def _blocks(S):
    return dict(tm=min(512, S), tm_proj=min(1024, S), ts=min(512, S), tq=min(512, S), tq_big=min(1024, S), tk=min(512, S), tks=min(256, S))


def _pair_pad(vec):
    npair = FOX_HEADS // 2
    v = jnp.pad(vec.reshape(npair, 2), ((0, 0), (0, FF_STRIDE - 2))).reshape(1, npair * FF_STRIDE)
    return jnp.pad(v, ((0, 0), (0, LANES - npair * FF_STRIDE)))


def _pair_unpad(row):
    npair = FOX_HEADS // 2
    return row[0, :npair * FF_STRIDE].reshape(npair, FF_STRIDE)[:, :2].reshape(FOX_HEADS)


def _pool_blockdiag(w_pool):
    g, cg, _ = w_pool.shape
    eye = jnp.eye(g, dtype=w_pool.dtype)
    return jnp.einsum("gh,gcd->gchd", eye, w_pool).reshape(g * cg, g * cg)


def _layer_params(norm_g, b_f, q_norm_g, k_norm_g, w_pool, pool_scale):
    return dict(g=norm_g.reshape(1, -1), qg=jnp.tile(q_norm_g, FOX_HEADS).reshape(1, FOX_W), kg=jnp.tile(k_norm_g, FOX_HEADS).reshape(1, FOX_W),
                bfp=_pair_pad(b_f), wpd=_pool_blockdiag(w_pool).astype(BF16), ps=pool_scale.reshape(1, POOL_W))


def _layer_fwd(x, wt_all, w_out, layer, prm, bs):
    projm, ffo, h = _inproj(x, prm["g"], wt_all, layer, tm=bs["tm_proj"], tn=PROJ_TN)
    qn, ka, kb, v, sq, sk, sv, pooled, yp, pm = _prep(projm, ffo, prm["qg"], prm["kg"], prm["bfp"], prm["wpd"], prm["ps"], ts=bs["ts"])
    o, lse, fm = _fox_fwd(qn, ka, kb, v, projm, tq=bs["tq"], tk=bs["tk"])
    so, sm = _sb_fwd(sq, sk, sv, projm, tq=bs["tq_big"], tk=bs["tks"])
    y = _outproj(x, fm, pm, sm, w_out, layer, tm=bs["tm"])
    saved = dict(x=x, projm=projm, ffo=ffo, h=h, qn=qn, ka=ka, kb=kb, v=v, sq=sq, sk=sk, sv=sv, pooled=pooled, yp=yp,
                 o=o, lse=lse, so=so, fm=fm, pm=pm, sm=sm)
    return y, saved


def _layer_bwd(dy, wt_all, w_out, prm, sv_, bs, layer, stacks):
    dmix, stack_o = _outproj_bwd(dy, sv_["fm"], sv_["pm"], sv_["sm"], w_out, layer, None if stacks is None else stacks[2:], tm=bs["tm"])
    dqn, dkn, dv, dfg, dct, dcr = _fox_bwd(sv_["qn"], sv_["ka"], sv_["kb"], sv_["v"], sv_["o"], sv_["lse"], dmix, sv_["projm"],
                                      tq=bs["tq_big"], tk=bs["tk"])
    dsq, dsk, dsv, dsg = _sb_bwd(sv_["sq"], sv_["sk"], sv_["sv"], sv_["so"], dmix, sv_["projm"], tq=bs["tq"], tk=bs["tks"])
    dproj, dqg, dkg, dbf, dwp, dps = _prep_bwd(sv_["projm"], sv_["ffo"], dqn, dkn, dct, dcr, dv, dfg, dsq, dsk, dsv, dsg, dmix,
                                               sv_["pooled"], sv_["yp"], prm["qg"], prm["kg"], prm["bfp"], prm["wpd"], prm["ps"], ts=bs["ts"])
    stack_m, stack_f = _inproj_dw(sv_["h"], dproj, layer, None if stacks is None else stacks[:2], ts=bs["tm_proj"], tn=PROJ_TN)
    dx, dg = _inproj_dx(dproj, wt_all, layer, sv_["x"], prm["g"], dy, tm=min(256, bs["tm"]))
    grads = dict(
        norm_g=dg[0],
        b_f=_pair_unpad(dbf), q_norm_g=dqg.reshape(FOX_HEADS, HEAD_DIM).sum(0), k_norm_g=dkg.reshape(FOX_HEADS, HEAD_DIM).sum(0),
        w_pool=jnp.stack([dwp[HEAD_DIM * g:HEAD_DIM * (g + 1), HEAD_DIM * g:HEAD_DIM * (g + 1)] for g in range(4)]),
        pool_scale=dps[0])
    return dx, grads, (stack_m, stack_f, stack_o)


def _local_step(x, target, wt_all, w_out, norm_g, b_f, q_norm_g, k_norm_g, w_pool, pool_scale):
    S, D = x.shape
    bs = _blocks(S)
    prms = [_layer_params(norm_g[l], b_f[l], q_norm_g[l], k_norm_g[l], w_pool[l], pool_scale[l]) for l in range(DEPTH)]
    saved = []
    y = x
    for l in range(DEPTH):
        y, s_ = _layer_fwd(y, wt_all, w_out, l, prms[l], bs)
        saved.append(s_)
    dy, sq = _loss_head(y, target, tm=bs["tm"])
    loss = 0.5 * jnp.sum(sq) / D
    grads = [None] * DEPTH
    stacks = None
    for l in reversed(range(DEPTH)):
        dy, grads[l], stacks = _layer_bwd(dy, wt_all, w_out, prms[l], saved[l], bs, l, stacks)
    stacked = {k: jnp.stack([g[k] for g in grads]) for k in grads[0]}
    return loss, dy, stacked, stacks


SMALL = ("norm_g", "b_f", "q_norm_g", "k_norm_g", "w_pool", "pool_scale")


def _pack_small(gr):
    flat = jnp.concatenate([gr[k].reshape(-1) for k in SMALL])
    pad = (-flat.shape[0]) % (8 * LANES)
    return jnp.pad(flat, (0, pad)).reshape(-1, LANES)


def _unpack_small(packed, like):
    flat = packed.reshape(-1)
    out, off = {}, 0
    for k in SMALL:
        n = like[k].size
        out[k] = flat[off:off + n].reshape(like[k].shape)
        off += n
    return out


def kernel(x, norm_g, w_in, b_f, q_norm_g, k_norm_g, w_pool, pool_scale, w_out, loss_target, m_norm_g, m_w_in, m_b_f, m_q_norm_g, m_k_norm_g, m_w_pool, m_pool_scale, m_w_out, v_norm_g, v_w_in, v_b_f, v_q_norm_g, v_k_norm_g, v_w_pool, v_pool_scale, v_w_out):
    weights = dict(norm_g=norm_g, w_in=w_in, b_f=b_f, q_norm_g=q_norm_g, k_norm_g=k_norm_g, w_pool=w_pool, pool_scale=pool_scale, w_out=w_out)
    mom_m = dict(norm_g=m_norm_g, w_in=m_w_in, b_f=m_b_f, q_norm_g=m_q_norm_g, k_norm_g=m_k_norm_g, w_pool=m_w_pool, pool_scale=m_pool_scale, w_out=m_w_out)
    mom_v = dict(norm_g=v_norm_g, w_in=v_w_in, b_f=v_b_f, q_norm_g=v_q_norm_g, k_norm_g=v_k_norm_g, w_pool=v_w_pool, pool_scale=v_pool_scale, w_out=v_w_out)
    shard_cols = w_in.shape[2]
    shard_rows = w_out.shape[1]

    cols_first = lambda a: jnp.transpose(a, (2, 0, 1))
    w_in_t = cols_first(w_in)
    w_in_t_full, w_out_full = _gather_weights(w_in_t, w_out)
    wt_all = _to_aligned(w_in_t_full)
    loss, dx, gr, stacks = _local_step(x[0], loss_target[0], wt_all, w_out_full, norm_g, b_f, q_norm_g, k_norm_g, w_pool, pool_scale)
    loss = lax.psum(loss, ("x", "y", "c"))

    (g_in_mine, g_in_sib), g_w_out = _reduce_scatter(*stacks, shard_cols, shard_rows)
    small = _unpack_small(_all_reduce_small(_pack_small(gr)), {k: weights[k] for k in SMALL})
    grad_w = dict(small, w_out=g_w_out)

    names = ("norm_g", "w_in", "b_f", "q_norm_g", "k_norm_g", "w_pool", "pool_scale", "w_out")
    upd = {k: _adamw_nd(weights[k], grad_w[k], mom_m[k], mom_v[k]) for k in names if k != "w_in"}
    in_t = _adamw_halves(w_in_t, g_in_mine, g_in_sib, cols_first(mom_m["w_in"]), cols_first(mom_v["w_in"]))
    grad_w["w_in"], *upd["w_in"] = [jnp.transpose(a, (1, 2, 0)) for a in in_t]
    return (loss, dx[None], *[grad_w[k] for k in names], *[upd[k][0] for k in names], *[upd[k][1] for k in names], *[upd[k][2] for k in names])
```

```python
import functools

import jax
import jax.numpy as jnp
from jax import lax
from jax.experimental import pallas as pl
from jax.experimental.pallas import tpu as pltpu

F32 = jnp.float32
BF16 = jnp.bfloat16

DEPTH = 4
HEAD_DIM = 64
FOX_HEADS = 8
SB_HEADS = 4
FOX_W = FOX_HEADS * HEAD_DIM
SB_W = SB_HEADS * HEAD_DIM
POOL_W = 256
POOL_WINDOWS = (2, 4, 8, 16)
POOL_HALO = 16
D_MIX = FOX_W + POOL_W + SB_W
EPS = 1e-6
NEG = -1e30
QK_SCALE = HEAD_DIM ** -0.5

ORIG_FOX = 4 * FOX_W
ORIG_FF = ORIG_FOX
ORIG_REST = ORIG_FF + FOX_HEADS
D_IN = ORIG_REST + 2 * POOL_W + 4 * SB_W

C_FQ, C_FK, C_FV, C_FG = 0, FOX_W, 2 * FOX_W, 3 * FOX_W
C_PX = 4 * FOX_W
C_PG = C_PX + POOL_W
C_SQ = C_PG + POOL_W
C_SK, C_SV, C_SG = C_SQ + SB_W, C_SQ + 2 * SB_W, C_SQ + 3 * SB_W
PM = C_SG + SB_W
LANES = 128
PW = PM + LANES
FF_STRIDE = 8
AUG = 3

ADAM_LR = 0.001
ADAM_B1 = 0.9
ADAM_B2 = 0.999
ADAM_EPS = 1e-08
ADAM_WD = 0.01
ADAM_STEP = 10

VMEM_LIMIT = 48 * 1024 * 1024
PROJ_TN = PM // 2


def _cparams(**kw):
    return pltpu.CompilerParams(vmem_limit_bytes=VMEM_LIMIT, **kw)


def _dot(a, b):
    return jnp.dot(a, b, preferred_element_type=F32)


def _dot_nt(a, b):
    return lax.dot_general(a, b, (((1,), (1,)), ((), ())), preferred_element_type=F32)


def _dot_tn(a, b):
    return lax.dot_general(a, b, (((0,), (0,)), ((), ())), preferred_element_type=F32)


def _split2(x):
    hi = x.astype(BF16)
    lo = (x - hi.astype(F32)).astype(BF16)
    return hi, lo


def _split3(x):
    hi = x.astype(BF16)
    r = x - hi.astype(F32)
    mid = r.astype(BF16)
    lo = (r - mid.astype(F32)).astype(BF16)
    return hi, mid, lo


def _dot_exact_rhs(x, m):
    hi, mid, lo = _split3(x)
    return _dot(hi, m) + _dot(mid, m) + _dot(lo, m)


def _dot_exact_lhs(m, x):
    hi, mid, lo = _split3(x)
    return _dot(m, hi) + _dot(m, mid) + _dot(m, lo)


def _sigmoid(x):
    return 1.0 / (1.0 + jnp.exp(-x))


def _silu_pair(x):
    s = _sigmoid(x)
    return x * s, s * (1.0 + x * (1.0 - s))


def _iota(shape, dim):
    return lax.broadcasted_iota(jnp.int32, shape, dim)


def _ones_where(cond):
    return jnp.where(cond, 1.0, 0.0).astype(BF16)


def _head_blockdiag(w):
    return _ones_where((_iota((w, w), 0) >> 6) == (_iota((w, w), 1) >> 6))


def _group_sum(x, bd):
    hi, lo = _split2(x)
    return _dot(hi, bd) + _dot(lo, bd)


def _lane_pick(x, lane_idx, lane):
    return jnp.sum(jnp.where(lane_idx == lane, x, 0.0), axis=1, keepdims=True)


def _inproj(x, g, wt_all, layer, *, tm, tn):
    S, D = x.shape
    nj = PM // tn

    def body(x_ref, g_ref, w_ref, wff_ref, proj_ref, ff_ref, h_ref):
        @pl.when(pl.program_id(1) == 0)
        def _():
            xf = x_ref[...]
            ms = jnp.mean(xf * xf, axis=-1, keepdims=True)
            h = (xf * lax.rsqrt(ms + EPS) * g_ref[...]).astype(BF16)
            h_ref[...] = h
            ff_ref[...] = _dot_nt(h, wff_ref[...])

        proj_ref[...] = _dot_nt(h_ref[...], w_ref[...])

    return pl.pallas_call(
        body, name="inproj", grid=(S // tm, nj),
        in_specs=[pl.BlockSpec((tm, D), lambda i, j: (i, 0)),
                  pl.BlockSpec((1, D), lambda i, j: (0, 0)),
                  pl.BlockSpec((None, tn, D), lambda i, j: (layer, j, 0)),
                  pl.BlockSpec((None, LANES, D), lambda i, j: (layer, PM // LANES, 0))],
        out_specs=[pl.BlockSpec((tm, tn), lambda i, j: (i, j)),
                   pl.BlockSpec((tm, LANES), lambda i, j: (i, 0)),
                   pl.BlockSpec((tm, D), lambda i, j: (i, 0))],
        out_shape=[jax.ShapeDtypeStruct((S, PM), F32), jax.ShapeDtypeStruct((S, LANES), F32),
                   jax.ShapeDtypeStruct((S, D), BF16)],
        compiler_params=_cparams(dimension_semantics=("arbitrary", "arbitrary")),
    )(x, g, wt_all, wt_all)


def _pool_group_select(lane_group, vals):
    return jnp.where(lane_group == 0, vals[0], jnp.where(lane_group == 1, vals[1], jnp.where(lane_group == 2, vals[2], vals[3])))


def _prep(projm, ffo, qg, kg, bfp, wpd, ps, *, ts):
    S = projm.shape[0]
    nb = S // ts
    hb = ts // POOL_HALO

    def body(fq_ref, fk_ref, fv_ref, pp_ref, halo_ref, ff_ref, sq_ref, sk_ref, sv_ref,
             qg_ref, kg_ref, bf_ref, wpd_ref, ps_ref,
             qn_ref, ka_ref, kb_ref, v_ref, sqo_ref, sko_ref, svo_ref, pooled_ref, yp_ref, pm_ref,
             carry_ref, c_ref, buf_ref):
        i = pl.program_id(0)
        bd = _head_blockdiag(FOX_W)
        normed = []
        for src, g_ref in ((fq_ref, qg_ref), (fk_ref, kg_ref)):
            q = src[...]
            ss = _group_sum(q * q, bd)
            normed.append(q * lax.rsqrt(ss * (1.0 / HEAD_DIM) + EPS) * g_ref[...])
        qn_ref[...] = (normed[0] * QK_SCALE).astype(BF16)
        kn = normed[1]
        v_ref[...] = fv_ref[...].astype(BF16)
        sqo_ref[...] = (sq_ref[...] * QK_SCALE).astype(BF16)
        sko_ref[...] = sk_ref[...].astype(BF16)
        svo_ref[...] = sv_ref[...].astype(BF16)

        @pl.when(i == 0)
        def _():
            carry_ref[...] = jnp.zeros_like(carry_ref)

        z = ff_ref[...] + bf_ref[...]
        lf = jnp.minimum(z, 0.0) - jnp.log(1.0 + jnp.exp(-jnp.abs(z)))
        tri = _ones_where(_iota((ts, ts), 1) <= _iota((ts, ts), 0))
        c = _dot_exact_lhs(tri, lf) + carry_ref[...]
        c_ref[...] = c
        carry_ref[...] = c_ref[ts - 1:ts, :]
        parts = jnp.concatenate(_split3(-c), axis=1)
        row = _iota((AUG * LANES, FOX_W), 0)
        col = _iota((AUG * LANES, FOX_W), 1)
        part, src = row >> 7, row & (LANES - 1)
        pair, off = col >> 7, col & (LANES - 1)
        sel_a = _ones_where((src == FF_STRIDE * pair) & (off == HEAD_DIM + part))
        sel_b = _ones_where((src == FF_STRIDE * pair + 1) & (off == part))
        first_half = (_iota((1, FOX_W), 1) & HEAD_DIM) == 0
        ka_ref[...] = jnp.where(first_half, kn, _dot(parts, sel_a)).astype(BF16)
        kb_ref[...] = jnp.where(first_half, _dot(parts, sel_b), kn).astype(BF16)

        x = pp_ref[:, 0:POOL_W]
        pg = pp_ref[:, POOL_W:2 * POOL_W]
        halo = jnp.where(i > 0, halo_ref[:, 0:POOL_W], 0.0)
        buf_ref[0:POOL_HALO, :] = halo
        buf_ref[POOL_HALO:POOL_HALO + ts, :] = x
        acc = x
        snaps = []
        for d in range(1, POOL_HALO):
            acc = acc + buf_ref[pl.ds(POOL_HALO - d, ts), :]
            if d + 1 in POOL_WINDOWS:
                snaps.append(acc)
        lane_group = _iota((1, POOL_W), 1) >> 6
        wsum = _pool_group_select(lane_group, snaps)
        wlen = _pool_group_select(lane_group, [float(w) for w in POOL_WINDOWS])
        tpos = (i * ts + _iota((ts, 1), 0) + 1).astype(F32)
        pooled = wsum / jnp.minimum(tpos, wlen) - x
        pb = pooled.astype(BF16)
        pooled_ref[...] = pb
        yp = _dot(pb, wpd_ref[...])
        yp_ref[...] = yp
        pm_ref[...] = (yp * ps_ref[...] * (pg * _sigmoid(pg))).astype(BF16)

    blk = lambda w, c: pl.BlockSpec((ts, w), lambda i: (i, c))
    full = lambda a: pl.BlockSpec(a.shape, lambda i: (0,) * a.ndim)
    out_shapes = [
        jax.ShapeDtypeStruct((S, FOX_W), BF16), jax.ShapeDtypeStruct((S, FOX_W), BF16), jax.ShapeDtypeStruct((S, FOX_W), BF16),
        jax.ShapeDtypeStruct((S, FOX_W), BF16),
        jax.ShapeDtypeStruct((S, SB_W), BF16), jax.ShapeDtypeStruct((S, SB_W), BF16), jax.ShapeDtypeStruct((S, SB_W), BF16),
        jax.ShapeDtypeStruct((S, POOL_W), BF16), jax.ShapeDtypeStruct((S, POOL_W), F32), jax.ShapeDtypeStruct((S, POOL_W), BF16),
    ]
    out_specs = [
        blk(FOX_W, 0), blk(FOX_W, 0), blk(FOX_W, 0), blk(FOX_W, 0),
        blk(SB_W, 0), blk(SB_W, 0), blk(SB_W, 0),
        blk(POOL_W, 0), blk(POOL_W, 0), blk(POOL_W, 0),
    ]
    return pl.pallas_call(
        body, name="prep", grid=(nb,),
        in_specs=[blk(FOX_W, C_FQ // FOX_W), blk(FOX_W, C_FK // FOX_W), blk(FOX_W, C_FV // FOX_W), blk(2 * POOL_W, C_PX // (2 * POOL_W)),
                  pl.BlockSpec((POOL_HALO, 2 * POOL_W), lambda i: (jnp.maximum(i * hb - 1, 0), C_PX // (2 * POOL_W))),
                  blk(LANES, 0),
                  blk(SB_W, C_SQ // SB_W), blk(SB_W, C_SK // SB_W), blk(SB_W, C_SV // SB_W),
                  full(qg), full(kg), full(bfp), full(wpd), full(ps)],
        out_specs=out_specs, out_shape=out_shapes,
        scratch_shapes=[pltpu.VMEM((1, LANES), F32), pltpu.VMEM((ts, LANES), F32), pltpu.VMEM((ts + POOL_HALO, POOL_W), F32)],
        compiler_params=_cparams(dimension_semantics=("arbitrary",)),
    )(projm, projm, projm, projm, projm, ffo, projm, projm, projm, qg, kg, bfp, wpd, ps)


def _pair_masks(x):
    ma = _iota((1, LANES), 1) < HEAD_DIM
    zero = jnp.zeros_like(x)
    return jnp.where(ma, x, zero), jnp.where(ma, zero, x)


DIAG_TILE = 256


def _diag_tiles(tq, size=DIAG_TILE):
    size = min(tq, size)
    return [(t * size, size) for t in range(tq // size)]


def _put_rows(old, new, r0):
    return new if r0 == 0 else jnp.concatenate([old[:r0], new], axis=0)


def _aug_queries(q):
    lane = _iota((1, LANES), 1)
    one = jnp.ones_like(q)
    zero = jnp.zeros_like(q)
    qa = jnp.where(lane < HEAD_DIM, q, jnp.where(lane < HEAD_DIM + AUG, one, zero))
    qb = jnp.where(lane >= HEAD_DIM, q, jnp.where(lane < AUG, one, zero))
    return qa, qb


def _fox_fwd(qn, ka, kb, v, projm, *, tq, tk):
    S = qn.shape[0]
    npair = FOX_HEADS // 2

    def body(q_ref, ka_ref, kb_ref, v_ref, fg_ref, o_ref, lse_ref, fm_ref):
        qi = pl.program_id(1)
        lane = _iota((1, LANES), 1)
        ma = lane < HEAD_DIM
        qaug = _aug_queries(q_ref[...])
        k_refs = (ka_ref, kb_ref)

        def block(k0, tkl, r0, carry, masked):
            vb = v_ref[pl.ds(k0, tkl), :]
            if masked:
                mask = (k0 + _iota((tq - r0, tkl), 1)) <= (qi * tq + r0 + _iota((tq - r0, tkl), 0))
            scores = [_dot_nt(qaug[h][r0:], k_refs[h][pl.ds(k0, tkl), :]) for h in range(2)]
            new = []
            for h in range(2):
                m, l, acc = [x[r0:] for x in carry[h]]
                s = jnp.where(mask, scores[h], NEG) if masked else scores[h]
                m_new = jnp.maximum(m, jnp.max(s, axis=1, keepdims=True))
                alpha = jnp.exp(m - m_new)
                p = jnp.exp(s - m_new)
                sub = (m_new, alpha * l + jnp.sum(p, axis=1, keepdims=True), alpha * acc + _dot(p.astype(BF16), vb))
                new.append(tuple(_put_rows(old, x, r0) for old, x in zip(carry[h], sub)))
            return tuple(new)

        init = tuple((jnp.full((tq, 1), NEG, F32), jnp.zeros((tq, 1), F32), jnp.zeros((tq, LANES), F32)) for _ in range(2))
        carry = lax.fori_loop(0, (qi * tq) // tk, lambda j, c: block(pl.multiple_of(j * tk, tk), tk, 0, c, False), init)
        for off, size in _diag_tiles(tq, tq):
            carry = block(pl.multiple_of(qi * tq + off, size), size, off, carry, True)
        (ma_, la, acca), (mb_, lb, accb) = carry
        o = jnp.where(ma, acca / la, accb / lb)
        o_ref[...] = o
        lse_ref[...] = jnp.where(ma, ma_ + jnp.log(la), mb_ + jnp.log(lb))
        fg = fg_ref[...]
        fm_ref[...] = (o * (fg * _sigmoid(fg))).astype(BF16)

    qblk = pl.BlockSpec((tq, LANES), lambda p, i: (i, p))
    kvblk = pl.BlockSpec((S, LANES), lambda p, i: (0, p))
    return pl.pallas_call(
        body, name="fox_fwd", grid=(npair, S // tq),
        in_specs=[qblk, kvblk, kvblk, kvblk,
                  pl.BlockSpec((tq, LANES), lambda p, i: (i, C_FG // LANES + p))],
        out_specs=[qblk, qblk, qblk],
        out_shape=[jax.ShapeDtypeStruct((S, FOX_W), F32), jax.ShapeDtypeStruct((S, FOX_W), F32), jax.ShapeDtypeStruct((S, FOX_W), BF16)],
        compiler_params=_cparams(dimension_semantics=("arbitrary", "arbitrary")),
    )(qn, ka, kb, v, projm)


def _suffix_sums(x, tmat2):
    return _dot(jnp.concatenate(_split2(x), axis=1), tmat2)


def _suffix_matrix(tk, inclusive):
    rr, cc = _iota((2 * tk, tk), 0) & (tk - 1), _iota((2 * tk, tk), 1)
    return _ones_where(rr >= cc) if inclusive else _ones_where(rr > cc)


def _sb_scores(qh, kb, causal, tmat2, r_runs):
    heads = range(2)
    zs = [_dot_nt(qh[h], kb) for h in heads]
    nsps = [jnp.minimum(-z, 0.0) - jnp.log(1.0 + jnp.exp(-jnp.abs(z))) for z in zs]
    lbs = nsps if causal is None else [jnp.where(causal, n, 0.0) for n in nsps]
    rins = [_suffix_sums(lb, tmat2) for lb in lbs]
    args = [zs[h] + lbs[h] + (rins[h] + r_runs[h]) for h in heads]
    a_s = [jnp.exp(arg if causal is None else jnp.where(causal, arg, NEG)) for arg in args]
    return zs, nsps, lbs, a_s


SB_DEAD = -120.0


def _sb_walk_left(nfull, tk, block, carry, running_sums):
    def alive(state):
        jj, c = state
        ra, rb = running_sums(c)
        return (jj < nfull) & (jnp.max(jnp.maximum(ra, rb)) >= SB_DEAD)

    def step(state):
        jj, c = state
        return jj + 1, block(pl.multiple_of((nfull - 1 - jj) * tk, tk), 0, c, False)

    return lax.while_loop(alive, step, (jnp.int32(0), carry))[1]


def _sb_fwd(sq, sk, sv, projm, *, tq, tk):
    S = sq.shape[0]
    npair = SB_HEADS // 2

    def body(q_ref, k_ref, v_ref, sg_ref, o_ref, sm_ref):
        qi = pl.program_id(1)
        lane = _iota((1, LANES), 1)
        ma = lane < HEAD_DIM
        qh = _pair_masks(q_ref[...])
        tmat2 = _suffix_matrix(tk, inclusive=False)
        nfull = (qi * tq) // tk

        def block(k0, r0, carry, masked):
            nr = tq - r0
            kb = k_ref[pl.ds(k0, tk), :]
            vb = v_ref[pl.ds(k0, tk), :]
            causal = (k0 + _iota((nr, tk), 1)) < (qi * tq + r0 + _iota((nr, tk), 0)) if masked else None
            _, _, lbs, a_s = _sb_scores([q[r0:] for q in qh], kb, causal, tmat2, [carry[h][0][r0:] for h in range(2)])
            pv = _dot(jnp.concatenate([a.astype(BF16) for a in a_s], axis=0), vb)
            return tuple((_put_rows(carry[h][0], carry[h][0][r0:] + jnp.sum(lbs[h], axis=1, keepdims=True), r0),
                          _put_rows(carry[h][1], carry[h][1][r0:] + pv[h * nr:(h + 1) * nr], r0)) for h in range(2))

        carry = tuple((jnp.zeros((tq, 1), F32), jnp.zeros((tq, LANES), F32)) for _ in range(2))
        for off, size in reversed(_diag_tiles(tq)):
            assert size == tk
            carry = block(pl.multiple_of(qi * tq + off, tk), off, carry, True)
        (_, acca), (_, accb) = _sb_walk_left(nfull, tk, block, carry, lambda c: (c[0][0], c[1][0]))
        o = jnp.where(ma, acca, accb)
        o_ref[...] = o
        sg = sg_ref[...]
        sm_ref[...] = (o * (sg * _sigmoid(sg))).astype(BF16)

    qblk = pl.BlockSpec((tq, LANES), lambda p, i: (i, p))
    kvblk = pl.BlockSpec((S, LANES), lambda p, i: (0, p))
    return pl.pallas_call(
        body, name="sb_fwd", grid=(npair, S // tq),
        in_specs=[qblk, kvblk, kvblk, pl.BlockSpec((tq, LANES), lambda p, i: (i, C_SG // LANES + p))],
        out_specs=[qblk, qblk],
        out_shape=[jax.ShapeDtypeStruct((S, SB_W), F32), jax.ShapeDtypeStruct((S, SB_W), BF16)],
        compiler_params=_cparams(dimension_semantics=("arbitrary", "arbitrary")),
    )(sq, sk, sv, projm)


def _outproj(x, fm, pm, sm, w_out, layer, *, tm):
    S, D = x.shape

    def body(x_ref, fm_ref, pm_ref, sm_ref, w_ref, y_ref):
        y = x_ref[...] + _dot(fm_ref[...], w_ref[0:FOX_W, :])
        y = y + _dot(pm_ref[...], w_ref[FOX_W:FOX_W + POOL_W, :])
        y_ref[...] = y + _dot(sm_ref[...], w_ref[FOX_W + POOL_W:D_MIX, :])

    row = lambda w: pl.BlockSpec((tm, w), lambda i: (i, 0))
    return pl.pallas_call(
        body, name="outproj", grid=(S // tm,),
        in_specs=[row(D), row(FOX_W), row(POOL_W), row(SB_W), pl.BlockSpec((None, D_MIX, D), lambda i: (layer, 0, 0))],
        out_specs=row(D), out_shape=jax.ShapeDtypeStruct((S, D), F32),
        compiler_params=_cparams(dimension_semantics=("arbitrary",)),
    )(x, fm, pm, sm, w_out)


def _loss_head(y, target, *, tm):
    S, D = y.shape

    def body(y_ref, t_ref, dy_ref, sq_ref):
        @pl.when(pl.program_id(0) == 0)
        def _():
            sq_ref[...] = jnp.zeros_like(sq_ref)

        d = y_ref[...] - t_ref[...]
        dy_ref[...] = d * (1.0 / D)
        sq_ref[...] += jnp.sum(d * d, axis=0, keepdims=True)

    row = pl.BlockSpec((tm, D), lambda i: (i, 0))
    return pl.pallas_call(
        body, name="loss_head", grid=(S // tm,),
        in_specs=[row, row], out_specs=[row, pl.BlockSpec((1, D), lambda i: (0, 0))],
        out_shape=[jax.ShapeDtypeStruct((S, D), F32), jax.ShapeDtypeStruct((1, D), F32)],
        compiler_params=_cparams(dimension_semantics=("arbitrary",)),
    )(y, target)


def _outproj_bwd(dy, fm, pm, sm, w_out, layer, stacks, *, tm):
    S, D = dy.shape

    def body(dy_ref, fm_ref, pm_ref, sm_ref, w_ref, dm_ref, dw_ref):
        @pl.when(pl.program_id(0) == 0)
        def _():
            dw_ref[...] = jnp.zeros_like(dw_ref)

        dyb = dy_ref[...].astype(BF16)
        dm_ref[...] = _dot_nt(dyb, w_ref[...])
        dw_ref[0:FOX_W, :] += _dot_tn(fm_ref[...], dyb)
        dw_ref[FOX_W:FOX_W + POOL_W, :] += _dot_tn(pm_ref[...], dyb)
        dw_ref[FOX_W + POOL_W:D_MIX, :] += _dot_tn(sm_ref[...], dyb)

    row = lambda w: pl.BlockSpec((tm, w), lambda i: (i, 0))
    wspec = pl.BlockSpec((None, D_MIX, D), lambda i: (layer, 0, 0))
    return _stack_call(
        body, "outproj_bwd", (S // tm,), [row(D), row(FOX_W), row(POOL_W), row(SB_W), wspec], (dy, fm, pm, sm, w_out),
        [pl.BlockSpec((None, D_MIX, D), lambda i: (layer, 0, 0))], [(D_MIX, D)], stacks,
        plain_specs=[row(D_MIX)], plain_shapes=[jax.ShapeDtypeStruct((S, D_MIX), F32)],
        compiler_params=_cparams(dimension_semantics=("arbitrary",)))


def _fox_bwd(qn, ka, kb, v, o, lse, dmix, projm, *, tq, tk):
    S = qn.shape[0]
    npair = FOX_HEADS // 2

    def body(q_ref, ka_ref, kb_ref, v_ref, o_ref, lse_ref, dm_ref, fg_ref,
             dq_ref, dk_ref, dv_ref, dfg_ref, dct_ref, dcr_ref):
        qi = pl.program_id(1)

        @pl.when(qi == 0)
        def _():
            dk_ref[...] = jnp.zeros_like(dk_ref)
            dv_ref[...] = jnp.zeros_like(dv_ref)
            dct_ref[...] = jnp.zeros_like(dct_ref)

        lane = _iota((1, LANES), 1)
        ma = lane < HEAD_DIM
        qh = _pair_masks(q_ref[...])
        qaug = _aug_queries(q_ref[...])
        k_refs = (ka_ref, kb_ref)
        lsev = lse_ref[...]
        lse = (_lane_pick(lsev, lane, 0), _lane_pick(lsev, lane, HEAD_DIM))
        fg = fg_ref[...]
        silu, dsilu = _silu_pair(fg)
        dm = dm_ref[...]
        ov = o_ref[...]
        do = dm * silu
        dfg_ref[...] = dm * ov * dsilu
        dd = do * ov
        dsum = (jnp.sum(jnp.where(ma, dd, 0.0), axis=1, keepdims=True), jnp.sum(jnp.where(ma, 0.0, dd), axis=1, keepdims=True))
        doh = _pair_masks(do.astype(BF16))

        def block(k0, tkl, r0, carry, masked):
            vb = v_ref[pl.ds(k0, tkl), :]
            if masked:
                mask = (k0 + _iota((tq - r0, tkl), 1)) <= (qi * tq + r0 + _iota((tq - r0, tkl), 0))
            heads = range(2)
            kaugs = [k_refs[h][pl.ds(k0, tkl), :] for h in heads]
            scores = [_dot_nt(qaug[h][r0:], kaugs[h]) for h in heads]
            dps = [_dot_nt(doh[h][r0:], vb) for h in heads]
            ps, dss, rows = [], [], []
            for h in heads:
                s = jnp.where(mask, scores[h], NEG) if masked else scores[h]
                p = jnp.exp(s - lse[h][r0:])
                dsf = p * (dps[h] - dsum[h][r0:])
                dct_ref[0, h:h + 1, pl.ds(k0, tkl)] -= jnp.sum(dsf, axis=0, keepdims=True)
                rows.append(_put_rows(carry[1 + h], carry[1 + h][r0:] + jnp.sum(dsf, axis=1, keepdims=True), r0))
                ps.append(p.astype(BF16))
                dss.append(dsf.astype(BF16))
            dv_ref[pl.ds(k0, tkl), :] += _dot_tn(jnp.concatenate(ps, axis=0), jnp.concatenate([d[r0:] for d in doh], axis=0))
            dk_ref[pl.ds(k0, tkl), :] += _dot_tn(jnp.concatenate(dss, axis=0), jnp.concatenate([q[r0:] for q in qh], axis=0))
            kh = jnp.concatenate([_pair_masks(kaugs[h])[h] for h in heads], axis=0)
            dq = _put_rows(carry[0], carry[0][r0:] + _dot(jnp.concatenate(dss, axis=1), kh), r0)
            return (dq, rows[0], rows[1])

        zcol = jnp.zeros((tq, 1), F32)
        carry = lax.fori_loop(0, (qi * tq) // tk, lambda j, c: block(pl.multiple_of(j * tk, tk), tk, 0, c, False),
                              (jnp.zeros((tq, LANES), F32), zcol, zcol))
        for off, size in _diag_tiles(tq):
            carry = block(pl.multiple_of(qi * tq + off, size), size, off, carry, True)
        dq, rowa, rowb = carry
        dq_ref[...] = dq * QK_SCALE
        dcr_ref[0] = jnp.where(ma, rowa, rowb)

    qblk = pl.BlockSpec((tq, LANES), lambda p, i: (i, p))
    kvblk = pl.BlockSpec((S, LANES), lambda p, i: (0, p))
    f32out = jax.ShapeDtypeStruct((S, FOX_W), F32)
    ctblk = pl.BlockSpec((1, FF_STRIDE, S), lambda p, i: (p, 0, 0))
    return pl.pallas_call(
        body, name="fox_bwd", grid=(npair, S // tq),
        in_specs=[qblk, kvblk, kvblk, kvblk, qblk, qblk, qblk,
                  pl.BlockSpec((tq, LANES), lambda p, i: (i, C_FG // LANES + p))],
        out_specs=[qblk, kvblk, kvblk, qblk, ctblk, pl.BlockSpec((1, tq, LANES), lambda p, i: (p, i, 0))],
        out_shape=[f32out, f32out, f32out, f32out, jax.ShapeDtypeStruct((npair, FF_STRIDE, S), F32),
                   jax.ShapeDtypeStruct((npair, S, LANES), F32)],
        compiler_params=_cparams(dimension_semantics=("arbitrary", "arbitrary")),
    )(qn, ka, kb, v, o, lse, dmix, projm)


def _sb_bwd(sq, sk, sv, o, dmix, projm, *, tq, tk):
    S = sq.shape[0]
    npair = SB_HEADS // 2
    mix0 = (FOX_W + POOL_W) // LANES

    def body(q_ref, k_ref, v_ref, o_ref, dm_ref, sg_ref, dq_ref, dk_ref, dv_ref, dsg_ref):
        qi = pl.program_id(1)

        @pl.when(qi == 0)
        def _():
            dk_ref[...] = jnp.zeros_like(dk_ref)
            dv_ref[...] = jnp.zeros_like(dv_ref)

        lane = _iota((1, LANES), 1)
        ma = lane < HEAD_DIM
        qh = _pair_masks(q_ref[...])
        sg = sg_ref[...]
        silu, dsilu = _silu_pair(sg)
        dm = dm_ref[...]
        ov = o_ref[...]
        do = dm * silu
        dsg_ref[...] = dm * ov * dsilu
        dob = do.astype(BF16)
        dd = dob.astype(F32) * ov
        dsum = (jnp.sum(jnp.where(ma, dd, 0.0), axis=1, keepdims=True), jnp.sum(jnp.where(ma, 0.0, dd), axis=1, keepdims=True))
        doh = _pair_masks(dob)
        tmat2 = _suffix_matrix(tk, inclusive=False)
        tmat2_inc = _suffix_matrix(tk, inclusive=True)
        nfull = (qi * tq) // tk

        def block(k0, r0, carry, masked):
            nr = tq - r0
            kb = k_ref[pl.ds(k0, tk), :]
            vb = v_ref[pl.ds(k0, tk), :]
            kh = _pair_masks(kb)
            causal = (k0 + _iota((nr, tk), 1)) < (qi * tq + r0 + _iota((nr, tk), 0)) if masked else None
            heads = range(2)
            qs = [q[r0:] for q in qh]
            dos = [d[r0:] for d in doh]
            das = [_dot_nt(dos[h], vb) for h in heads]
            zs, nsps, lbs, a_s = _sb_scores(qs, kb, causal, tmat2, [carry[h][0][r0:] for h in heads])
            abs_ = [a.astype(BF16) for a in a_s]
            us = [abs_[h].astype(F32) * das[h] for h in heads]
            uins = [_suffix_sums(u, tmat2_inc) for u in us]
            dzs = []
            for h in heads:
                cum_u = dsum[h][r0:] - (uins[h] + carry[h][1][r0:])
                dz = us[h] * jnp.exp(nsps[h]) - jnp.exp(zs[h] + nsps[h]) * cum_u
                if masked:
                    dz = jnp.where(causal, dz, 0.0)
                dzs.append(dz.astype(BF16))
            dv_ref[pl.ds(k0, tk), :] += _dot_tn(jnp.concatenate(abs_, axis=0), jnp.concatenate(dos, axis=0))
            dk_ref[pl.ds(k0, tk), :] += _dot_tn(jnp.concatenate(dzs, axis=0), jnp.concatenate(qs, axis=0))
            dq = _put_rows(carry[2], carry[2][r0:] + _dot(jnp.concatenate(dzs, axis=1), jnp.concatenate(kh, axis=0)), r0)
            new = [(_put_rows(carry[h][0], carry[h][0][r0:] + jnp.sum(lbs[h], axis=1, keepdims=True), r0),
                    _put_rows(carry[h][1], carry[h][1][r0:] + jnp.sum(us[h], axis=1, keepdims=True), r0)) for h in heads]
            return (new[0], new[1], dq)

        zcol = jnp.zeros((tq, 1), F32)
        carry = ((zcol, zcol), (zcol, zcol), jnp.zeros((tq, LANES), F32))
        for off, size in reversed(_diag_tiles(tq)):
            assert size == tk
            carry = block(pl.multiple_of(qi * tq + off, tk), off, carry, True)
        dq = _sb_walk_left(nfull, tk, block, carry, lambda c: (c[0][0], c[1][0]))[2]
        dq_ref[...] = dq * QK_SCALE

    qblk = pl.BlockSpec((tq, LANES), lambda p, i: (i, p))
    kvblk = pl.BlockSpec((S, LANES), lambda p, i: (0, p))
    f32out = jax.ShapeDtypeStruct((S, SB_W), F32)
    return pl.pallas_call(
        body, name="sb_bwd", grid=(npair, S // tq),
        in_specs=[qblk, kvblk, kvblk, qblk,
                  pl.BlockSpec((tq, LANES), lambda p, i: (i, mix0 + p)),
                  pl.BlockSpec((tq, LANES), lambda p, i: (i, C_SG // LANES + p))],
        out_specs=[qblk, kvblk, kvblk, qblk],
        out_shape=[f32out, f32out, f32out, f32out],
        compiler_params=_cparams(dimension_semantics=("arbitrary", "arbitrary")),
    )(sq, sk, sv, o, dmix, projm)


def _prep_bwd(projm, ffo, dqn, dkn, dct, dcr, dv, dfg, dsq, dsk, dsv, dsg, dmix, pooled, yp, qg, kg, bfp, wpd, ps, *, ts):
    S = projm.shape[0]
    nb = S // ts
    hb = ts // POOL_HALO
    npair = FOX_HEADS // 2
    last_halo = S // POOL_HALO - 1

    def body(fq_ref, fk_ref, pp_ref, pph_ref, ff_ref,
             dqn_ref, dkn_ref, dct_ref, dcr_ref, dv_ref, dfg_ref, dsq_ref, dsk_ref, dsv_ref, dsg_ref,
             dmp_ref, dmh_ref, pooled_ref, yp_ref, qg_ref, kg_ref, bf_ref, wpd_ref, ps_ref,
             dp_ref, dqg_ref, dkg_ref, dbf_ref, dwp_ref, dps_ref,
             carry_ref, dl_ref, buf_ref, dct_s):
        i = pl.program_id(0)
        blk = nb - 1 - i

        @pl.when(i == 0)
        def _():
            carry_ref[...] = jnp.zeros_like(carry_ref)
            dqg_ref[...] = jnp.zeros_like(dqg_ref)
            dkg_ref[...] = jnp.zeros_like(dkg_ref)
            dbf_ref[...] = jnp.zeros_like(dbf_ref)
            dwp_ref[...] = jnp.zeros_like(dwp_ref)
            dps_ref[...] = jnp.zeros_like(dps_ref)

        bd = _head_blockdiag(FOX_W)
        for raw_ref, g_ref, dn, dg_ref, col in ((fq_ref, qg_ref, dqn_ref[...], dqg_ref, C_FQ), (fk_ref, kg_ref, dkn_ref[...], dkg_ref, C_FK)):
            q = raw_ref[...]
            rstd = lax.rsqrt(_group_sum(q * q, bd) * (1.0 / HEAD_DIM) + EPS)
            xhat = q * rstd
            dg_ref[...] += jnp.sum(dn * xhat, axis=0, keepdims=True)
            dyg = dn * g_ref[...]
            mean = _group_sum(dyg * xhat, bd) * (1.0 / HEAD_DIM)
            dp_ref[:, col:col + FOX_W] = (rstd * (dyg - xhat * mean)).astype(BF16)
        dp_ref[:, C_FV:C_FV + FOX_W] = dv_ref[...].astype(BF16)
        dp_ref[:, C_FG:C_FG + FOX_W] = dfg_ref[...].astype(BF16)
        dp_ref[:, C_SQ:C_SQ + SB_W] = dsq_ref[...].astype(BF16)
        dp_ref[:, C_SK:C_SK + SB_W] = dsk_ref[...].astype(BF16)
        dp_ref[:, C_SV:C_SV + SB_W] = dsv_ref[...].astype(BF16)
        dp_ref[:, C_SG:C_SG + SB_W] = dsg_ref[...].astype(BF16)

        dct_s[...] = jnp.zeros_like(dct_s)
        for p in range(npair):
            dct_s[FF_STRIDE * p:FF_STRIDE * (p + 1), :] = dct_ref[p]
        dc = dct_s[...].T
        lane = _iota((1, LANES), 1)
        for p in range(npair):
            dcr = dcr_ref[p]
            dc = dc + jnp.where(lane == FF_STRIDE * p, _lane_pick(dcr, lane, 0), 0.0)
            dc = dc + jnp.where(lane == FF_STRIDE * p + 1, _lane_pick(dcr, lane, HEAD_DIM), 0.0)
        triu = _ones_where(_iota((ts, ts), 1) >= _iota((ts, ts), 0))
        dlf = _dot_exact_lhs(triu, dc) + carry_ref[...]
        dl_ref[...] = dlf
        carry_ref[...] = dl_ref[0:1, :]
        z = ff_ref[...] + bf_ref[...]
        dff = dlf * (1.0 / (1.0 + jnp.exp(z)))
        dbf_ref[...] += jnp.sum(dff, axis=0, keepdims=True)
        dp_ref[:, PM:PW] = dff.astype(BF16)

        psv = ps_ref[...]
        wpdv = wpd_ref[...]
        lane_group = _iota((1, POOL_W), 1) >> 6
        wlen = _pool_group_select(lane_group, [float(w) for w in POOL_WINDOWS])
        pg = pp_ref[:, POOL_W:2 * POOL_W]
        silu, dsilu = _silu_pair(pg)
        dmp = dmp_ref[...]
        ypv = yp_ref[...]
        dp_ref[:, C_PG:C_PG + POOL_W] = (dmp * (ypv * psv) * dsilu).astype(BF16)
        dps_ref[...] += jnp.sum(dmp * silu * ypv, axis=0, keepdims=True)
        dyp = (dmp * psv * silu).astype(BF16)
        dwp_ref[...] += _dot_tn(pooled_ref[...], dyp)
        dpooled = _dot_nt(dyp, wpdv)
        pgh = pph_ref[:, POOL_W:2 * POOL_W]
        dyph = (dmh_ref[...] * psv * (pgh * _sigmoid(pgh))).astype(BF16)
        dpooled_h = jnp.where(blk < nb - 1, _dot_nt(dyph, wpdv), 0.0)
        tpos = (blk * ts + _iota((ts, 1), 0) + 1).astype(F32)
        ev = dpooled / jnp.minimum(tpos, wlen)
        buf_ref[0:ts, :] = ev
        buf_ref[ts:ts + POOL_HALO, :] = dpooled_h / wlen
        acc = ev
        snaps = []
        for d in range(1, POOL_HALO):
            acc = acc + buf_ref[pl.ds(d, ts), :]
            if d + 1 in POOL_WINDOWS:
                snaps.append(acc)
        dp_ref[:, C_PX:C_PX + POOL_W] = (_pool_group_select(lane_group, snaps) - dpooled).astype(BF16)

    rblk = lambda w, c: pl.BlockSpec((ts, w), lambda i: (nb - 1 - i, c))
    full = lambda a: pl.BlockSpec(a.shape, lambda i: (0,) * a.ndim)
    halo = lambda w, c: pl.BlockSpec((POOL_HALO, w), lambda i: (jnp.minimum((nb - i) * hb, last_halo), c))
    acc_spec = lambda r, w: pl.BlockSpec((r, w), lambda i: (0, 0))
    return pl.pallas_call(
        body, name="prep_bwd", grid=(nb,),
        in_specs=[rblk(FOX_W, C_FQ // FOX_W), rblk(FOX_W, C_FK // FOX_W), rblk(2 * POOL_W, C_PX // (2 * POOL_W)),
                  halo(2 * POOL_W, C_PX // (2 * POOL_W)), rblk(LANES, 0),
                  rblk(FOX_W, 0), rblk(FOX_W, 0), pl.BlockSpec((npair, FF_STRIDE, ts), lambda i: (0, 0, nb - 1 - i)),
                  pl.BlockSpec((npair, ts, LANES), lambda i: (0, nb - 1 - i, 0)), rblk(FOX_W, 0), rblk(FOX_W, 0),
                  rblk(SB_W, 0), rblk(SB_W, 0), rblk(SB_W, 0), rblk(SB_W, 0),
                  rblk(POOL_W, FOX_W // POOL_W), halo(POOL_W, FOX_W // POOL_W), rblk(POOL_W, 0), rblk(POOL_W, 0),
                  full(qg), full(kg), full(bfp), full(wpd), full(ps)],
        out_specs=[rblk(PW, 0), acc_spec(1, FOX_W), acc_spec(1, FOX_W), acc_spec(1, LANES), acc_spec(POOL_W, POOL_W), acc_spec(1, POOL_W)],
        out_shape=[jax.ShapeDtypeStruct((S, PW), BF16), jax.ShapeDtypeStruct((1, FOX_W), F32), jax.ShapeDtypeStruct((1, FOX_W), F32),
                   jax.ShapeDtypeStruct((1, LANES), F32), jax.ShapeDtypeStruct((POOL_W, POOL_W), F32), jax.ShapeDtypeStruct((1, POOL_W), F32)],
        scratch_shapes=[pltpu.VMEM((1, LANES), F32), pltpu.VMEM((ts, LANES), F32), pltpu.VMEM((ts + POOL_HALO, POOL_W), F32),
                        pltpu.VMEM((LANES, ts), F32)],
        compiler_params=_cparams(dimension_semantics=("arbitrary",)),
    )(projm, projm, projm, projm, ffo, dqn, dkn, dct, dcr, dv, dfg, dsq, dsk, dsv, dsg, dmix, dmix, pooled, yp, qg, kg, bfp, wpd, ps)


def _stack_call(body, name, grid, in_specs, operands, slot_specs, slot_shapes, stacks, plain_specs=(), plain_shapes=(), **kw):
    out_specs = list(plain_specs) + list(slot_specs)
    out_shape = list(plain_shapes) + [jax.ShapeDtypeStruct((DEPTH,) + s, F32) for s in slot_shapes]
    if stacks is None:
        return pl.pallas_call(body, name=name, grid=grid, in_specs=in_specs, out_specs=out_specs, out_shape=out_shape, **kw)(*operands)
    n = len(operands)

    def aliased_body(*refs):
        body(*refs[:n], *refs[n + len(stacks):])

    return pl.pallas_call(
        aliased_body, name=name, grid=grid, in_specs=list(in_specs) + [pl.BlockSpec(memory_space=pl.ANY)] * len(stacks),
        out_specs=out_specs, out_shape=out_shape,
        input_output_aliases={n + k: len(plain_specs) + k for k in range(len(stacks))}, **kw)(*operands, *stacks)


def _inproj_dw(h, dproj, layer, stacks, *, ts, tn):
    S, D = h.shape
    nj = PM // tn

    def body(h_ref, dp_ref, dpf_ref, dw_ref, dwf_ref):
        s = pl.program_id(1)

        @pl.when(s == 0)
        def _():
            dw_ref[...] = jnp.zeros_like(dw_ref)

        @pl.when((s == 0) & (pl.program_id(0) == 0))
        def _():
            dwf_ref[...] = jnp.zeros_like(dwf_ref)

        hv = h_ref[...]
        dw_ref[...] += _dot_tn(dp_ref[...], hv)

        @pl.when(pl.program_id(0) == 0)
        def _():
            dwf_ref[...] += _dot_tn(dpf_ref[...], hv)

    return _stack_call(
        body, "inproj_dw", (nj, S // ts),
        [pl.BlockSpec((ts, D), lambda j, s: (s, 0)),
         pl.BlockSpec((ts, tn), lambda j, s: (s, j)),
         pl.BlockSpec((ts, LANES), lambda j, s: (s, PM // LANES))],
        (h, dproj, dproj),
        [pl.BlockSpec((None, tn, D), lambda j, s: (layer, j, 0)), pl.BlockSpec((None, LANES, D), lambda j, s: (layer, 0, 0))],
        [(PM, D), (LANES, D)], stacks,
        compiler_params=_cparams(dimension_semantics=("arbitrary", "arbitrary")))


def _inproj_dx(dproj, wt_all, layer, x, g, dy, *, tm):
    S, D = x.shape

    def body(dp_ref, w_ref, x_ref, g_ref, dy_ref, dx_ref, dg_ref):
        @pl.when(pl.program_id(0) == 0)
        def _():
            dg_ref[...] = jnp.zeros_like(dg_ref)

        dh = _dot(dp_ref[...], w_ref[...])
        xf = x_ref[...]
        rstd = lax.rsqrt(jnp.mean(xf * xf, axis=-1, keepdims=True) + EPS)
        xhat = xf * rstd
        dg_ref[...] += jnp.sum(dh * xhat, axis=0, keepdims=True)
        dyg = dh * g_ref[...]
        mean = jnp.mean(dyg * xhat, axis=-1, keepdims=True)
        dx_ref[...] = rstd * (dyg - xhat * mean) + dy_ref[...]

    row = lambda w: pl.BlockSpec((tm, w), lambda i: (i, 0))
    return pl.pallas_call(
        body, name="inproj_dx", grid=(S // tm,),
        in_specs=[row(PW), pl.BlockSpec((None, PW, D), lambda i: (layer, 0, 0)), row(D), pl.BlockSpec((1, D), lambda i: (0, 0)), row(D)],
        out_specs=[row(D), pl.BlockSpec((1, D), lambda i: (0, 0))],
        out_shape=[jax.ShapeDtypeStruct((S, D), F32), jax.ShapeDtypeStruct((1, D), F32)],
        compiler_params=_cparams(dimension_semantics=("arbitrary",)),
    )(dproj, wt_all, x, g, dy)


def _adam_update(w, g, m, v):
    nm = ADAM_B1 * m + (1.0 - ADAM_B1) * g
    nv = ADAM_B2 * v + (1.0 - ADAM_B2) * (g * g)
    m_hat = nm / (1.0 - ADAM_B1 ** ADAM_STEP)
    v_hat = nv / (1.0 - ADAM_B2 ** ADAM_STEP)
    return -ADAM_LR * (m_hat / (jnp.sqrt(v_hat) + ADAM_EPS) + ADAM_WD * w), nm, nv


def _adamw(w, g, m, v):
    L, R, C = w.shape
    tr = R if R <= 512 else 256

    def body(w_ref, g_ref, m_ref, v_ref, d_ref, nm_ref, nv_ref):
        d_ref[...], nm_ref[...], nv_ref[...] = _adam_update(w_ref[...], g_ref[...], m_ref[...], v_ref[...])

    spec = pl.BlockSpec((1, tr, C), lambda l, i: (l, i, 0))
    shp = jax.ShapeDtypeStruct((L, R, C), F32)
    return pl.pallas_call(
        body, name="adamw", grid=(L, R // tr), in_specs=[spec] * 4, out_specs=[spec] * 3, out_shape=[shp] * 3,
        compiler_params=_cparams(dimension_semantics=("arbitrary", "arbitrary")),
    )(w, g, m, v)


def _adamw_nd(w, g, m, v):
    shape = w.shape
    view = (1,) + shape if w.ndim == 2 else (shape[0], -1, shape[-1])
    outs = _adamw(w.reshape(view), g.reshape(view), m.reshape(view), v.reshape(view))
    return tuple(o.reshape(shape) for o in outs)


FLIP_C = (0, 0, 1)
FLIP_X = (1, 0, 0)
FLIP_Y = (0, 1, 0)
FLIP_XY = (1, 1, 0)
MESH = pl.DeviceIdType.MESH


def _peer(flip):
    me = (lax.axis_index("x"), lax.axis_index("y"), lax.axis_index("c"))
    return tuple(1 - a if f else a for a, f in zip(me, flip))


def _exchange(name, arrays, flips):
    n = len(arrays)

    def body(*refs):
        srcs, dsts = refs[:n], refs[n:2 * n]
        send_sems, recv_sems = refs[2 * n:]
        copies = [pltpu.make_async_remote_copy(src_ref=srcs[k], dst_ref=dsts[k], send_sem=send_sems.at[k], recv_sem=recv_sems.at[k],
                                               device_id=_peer(flips[k]), device_id_type=MESH) for k in range(n)]
        for cp in copies:
            cp.start()
        for cp in copies:
            cp.wait()

    anyspec = pl.BlockSpec(memory_space=pl.ANY)
    return pl.pallas_call(
        body, name=name, in_specs=[anyspec] * n, out_specs=[anyspec] * n,
        out_shape=[jax.ShapeDtypeStruct(a.shape, a.dtype) for a in arrays],
        scratch_shapes=[pltpu.SemaphoreType.DMA((n,)), pltpu.SemaphoreType.DMA((n,))],
    )(*arrays)


def _exchange_add(name, x, flip):
    def body(x_ref, o_ref, buf_ref, send_sem, recv_sem):
        cp = pltpu.make_async_remote_copy(src_ref=x_ref, dst_ref=buf_ref, send_sem=send_sem, recv_sem=recv_sem,
                                          device_id=_peer(flip), device_id_type=MESH)
        cp.start()
        cp.wait()
        o_ref[...] = x_ref[...] + buf_ref[...]

    vspec = pl.BlockSpec(memory_space=pltpu.VMEM)
    return pl.pallas_call(
        body, name=name, in_specs=[vspec], out_specs=vspec, out_shape=jax.ShapeDtypeStruct(x.shape, x.dtype),
        scratch_shapes=[pltpu.VMEM(x.shape, x.dtype), pltpu.SemaphoreType.DMA, pltpu.SemaphoreType.DMA],
    )(x)


def _chip_index():
    return 2 * lax.axis_index("x") + lax.axis_index("y")


def _gather_weights(w_in_t, w_out):
    wi = w_in_t.astype(BF16)
    wo = jnp.swapaxes(w_out, 0, 1).astype(BF16)
    halves = (wi.shape[0] // 2, wo.shape[0] // 2)
    masks = (2, 1, 3)
    flips = (FLIP_X, FLIP_Y, FLIP_XY)
    n_first = 2 * len(masks)

    def body(wi_ref, wo_ref, gi_ref, go_ref, send_sems, recv_sems):
        c = lax.axis_index("c")
        j = _chip_index()
        srcs = (wi_ref, wo_ref)
        dsts = (gi_ref, go_ref)
        mine = [pl.ds(h * c, h) for h in halves]
        theirs = [pl.ds(h * (1 - c), h) for h in halves]

        def copy(idx, src, dst, flip):
            return pltpu.make_async_remote_copy(src_ref=src, dst_ref=dst, send_sem=send_sems.at[idx], recv_sem=recv_sems.at[idx],
                                                device_id=_peer(flip), device_id_type=MESH)

        first = [copy(2 * k + a, srcs[a].at[mine[a]], dsts[a].at[j, mine[a]], flips[k]) for k in range(len(masks)) for a in range(2)]
        for cp in first:
            cp.start()
        passed = []
        for k, m in enumerate(masks):
            for a in range(2):
                slot = dsts[a].at[j ^ m, mine[a]]
                copy(2 * k + a, slot, slot, flips[k]).wait_recv()
                fwd = copy(n_first + 2 * k + a, slot, slot, FLIP_C)
                fwd.start()
                passed.append(fwd)
        for k, m in enumerate(masks):
            for a in range(2):
                slot = dsts[a].at[j ^ m, theirs[a]]
                copy(n_first + 2 * k + a, slot, slot, FLIP_C).wait_recv()
        for cp in first + passed:
            cp.wait_send()

    anyspec = pl.BlockSpec(memory_space=pl.ANY)
    gi, go = pl.pallas_call(
        body, name="gather_weights", in_specs=[anyspec] * 2, out_specs=[anyspec] * 2,
        out_shape=[jax.ShapeDtypeStruct((4,) + wi.shape, BF16), jax.ShapeDtypeStruct((4,) + wo.shape, BF16)],
        scratch_shapes=[pltpu.SemaphoreType.DMA((2 * n_first,)), pltpu.SemaphoreType.DMA((2 * n_first,))],
    )(wi, wo)
    own = lax.broadcasted_iota(jnp.int32, (4, 1, 1, 1), 0) == _chip_index()
    gi = jnp.where(own, wi[None], gi)
    go = jnp.where(own, wo[None], go)
    w_in_t_full = gi.reshape((4 * wi.shape[0],) + wi.shape[1:])
    w_out_full = jnp.swapaxes(go.reshape((4 * wo.shape[0],) + wo.shape[1:]), 0, 1)
    return w_in_t_full, w_out_full


def _to_aligned(w_t):
    _, L, D = w_t.shape
    npair = FOX_HEADS // 2
    ff = w_t[ORIG_FF:ORIG_REST].reshape(npair, 2, L, D)
    ff = jnp.pad(ff, ((0, 0), (0, FF_STRIDE - 2), (0, 0), (0, 0))).reshape(npair * FF_STRIDE, L, D)
    ff = jnp.pad(ff, ((0, LANES - npair * FF_STRIDE), (0, 0), (0, 0)))
    return jnp.swapaxes(jnp.concatenate([w_t[:ORIG_FOX], w_t[ORIG_REST:], ff], axis=0), 0, 1)


def _from_aligned(dw_t):
    n, _, D = dw_t.shape
    npair = FOX_HEADS // 2
    ff = dw_t[:, PM:PM + npair * FF_STRIDE].reshape(n, npair, FF_STRIDE, D)[:, :, :2].reshape(n, FOX_HEADS, D)
    return jnp.swapaxes(jnp.concatenate([dw_t[:, :ORIG_FOX], ff, dw_t[:, ORIG_FOX:PM]], axis=1), 0, 1)


def _half_layers(name, stack, got):
    L, R, C = stack.shape
    half = L // 2
    tr = min(256, R)
    c = lax.axis_index("c")
    which = ((1 - c) if got is None else c).astype(jnp.int32).reshape(1)

    def body(c_ref, x_ref, *refs):
        if got is None:
            refs[0][...] = x_ref[...].astype(BF16)
        else:
            acc = x_ref[...] + refs[0][...].astype(F32)
            refs[1][...] = acc
            refs[2][...] = acc.astype(BF16)

    plain = pl.BlockSpec((1, tr, C), lambda l, i, c_ref: (l, i, 0))
    picked = pl.BlockSpec((1, tr, C), lambda l, i, c_ref: (c_ref[0] * half + l, i, 0))
    shp = lambda dt: jax.ShapeDtypeStruct((half, R, C), dt)
    grid_spec = pltpu.PrefetchScalarGridSpec(
        num_scalar_prefetch=1, grid=(half, R // tr),
        in_specs=[picked] + ([] if got is None else [plain]), out_specs=[plain] if got is None else [plain, plain])
    return pl.pallas_call(
        body, name=name, grid_spec=grid_spec, out_shape=[shp(BF16)] if got is None else [shp(F32), shp(BF16)],
        compiler_params=_cparams(dimension_semantics=("arbitrary", "arbitrary")),
    )(which, stack, *([] if got is None else [got]))


def _reduce_scatter(stack_m, stack_f, stack_o, shard_cols, shard_rows):
    j = _chip_index()
    half = DEPTH // 2
    stacks = (stack_m, stack_f, stack_o)
    give = [_half_layers("rs_give", s, None)[0] for s in stacks]
    got = _exchange("rs_d2d", give, (FLIP_C,) * len(stacks))
    (m32, mbf), (f32_, fbf), (o32, obf) = [_half_layers("rs_add_chip", s, g) for s, g in zip(stacks, got)]
    d_model = stack_m.shape[2]

    def in_shards(m, f):
        return _from_aligned(jnp.concatenate([m, f], axis=1)).reshape(4, shard_cols, half, d_model)

    def out_shards(o):
        return jnp.moveaxis(o.reshape(half, 4, shard_rows, o.shape[-1]), 1, 0)

    chip = [(in_shards(m32, f32_), in_shards(mbf, fbf)), (out_shards(o32), out_shards(obf))]
    masks = (2, 1, 3)
    flips = (FLIP_X, FLIP_Y, FLIP_XY)
    sends, sflips = [], []
    for _, bf in chip:
        for m, fl in zip(masks, flips):
            sends.append(lax.dynamic_index_in_dim(bf, j ^ m, axis=0, keepdims=False))
            sflips.append(fl)
    got = _exchange("rs_ici", sends, tuple(sflips))
    own_in, own_out = [lax.dynamic_index_in_dim(f32_sum, j, axis=0, keepdims=False) for f32_sum, _ in chip]
    mine_in = _add_rows("rs_add_in", own_in, list(got[0:3]))
    mine_out = _add_into_half("rs_add_out", own_out, list(got[3:6]))
    sib_in, g_out = _share_halves(mine_in, mine_out)
    return (mine_in, sib_in), g_out


def _add_rows(name, first, others):
    n = len(others)

    def body(*refs):
        acc = refs[0][...]
        for r in refs[1:1 + n]:
            acc = acc + r[...].astype(F32)
        refs[1 + n][...] = acc

    grid, spec = _row_lane_blocks(first.shape)
    return pl.pallas_call(
        body, name=name, grid=grid, in_specs=[spec(first.shape[1])] * (1 + n), out_specs=spec(first.shape[1]),
        out_shape=jax.ShapeDtypeStruct(first.shape, F32),
        compiler_params=_cparams(dimension_semantics=("arbitrary", "arbitrary")),
    )(first, *others)


def _row_lane_blocks(shape):
    rows, _, C = shape
    tr = rows // 2 if rows % 2 == 0 and rows > 64 else rows
    return (rows // tr, C // LANES), lambda n_mid: pl.BlockSpec((tr, n_mid, LANES), lambda i, k, *_: (i, 0, k))


def _add_into_half(name, first, others):
    half, rows, C = first.shape
    tr = min(256, rows)
    n = len(others)

    def body(c_ref, *refs):
        acc = refs[0][...]
        for r in refs[1:1 + n]:
            acc = acc + r[...].astype(F32)
        refs[1 + n][...] = acc

    grid_spec = pltpu.PrefetchScalarGridSpec(
        num_scalar_prefetch=1, grid=(half, rows // tr),
        in_specs=[pl.BlockSpec((1, tr, C), lambda l, i, c_ref: (l, i, 0))] * (1 + n),
        out_specs=pl.BlockSpec((1, tr, C), lambda l, i, c_ref: (c_ref[0] * half + l, i, 0)))
    return pl.pallas_call(
        body, name=name, grid_spec=grid_spec, out_shape=jax.ShapeDtypeStruct((2 * half, rows, C), F32),
        compiler_params=_cparams(dimension_semantics=("arbitrary", "arbitrary")),
    )(lax.axis_index("c").astype(jnp.int32).reshape(1), first, *others)


def _share_halves(mine, buf):
    half = DEPTH // 2

    def body(mine_ref, buf_in, sib_ref, buf_ref, send_sems, recv_sems):
        lay = pl.ds(half * lax.axis_index("c"), half)
        copies = [pltpu.make_async_remote_copy(src_ref=src, dst_ref=dst, send_sem=send_sems.at[k], recv_sem=recv_sems.at[k],
                                               device_id=_peer(FLIP_C), device_id_type=MESH)
                  for k, (src, dst) in enumerate(((mine_ref, sib_ref), (buf_ref.at[lay], buf_ref.at[lay])))]
        for cp in copies:
            cp.start()
        for cp in copies:
            cp.wait()

    anyspec = pl.BlockSpec(memory_space=pl.ANY)
    return pl.pallas_call(
        body, name="rs_share", in_specs=[anyspec] * 2, out_specs=[anyspec] * 2,
        out_shape=[jax.ShapeDtypeStruct(mine.shape, mine.dtype), jax.ShapeDtypeStruct(buf.shape, buf.dtype)],
        input_output_aliases={1: 1},
        scratch_shapes=[pltpu.SemaphoreType.DMA((2,)), pltpu.SemaphoreType.DMA((2,))],
    )(mine, buf)


def _adamw_halves(w, g_mine, g_sib, m, v):
    half = g_mine.shape[1]

    def body(c_ref, w_ref, gm_ref, gs_ref, m_ref, v_ref, g_ref, d_ref, nm_ref, nv_ref):
        first = c_ref[0] == 0
        gm, gs = gm_ref[...], gs_ref[...]
        for h, gv in enumerate((jnp.where(first, gm, gs), jnp.where(first, gs, gm))):
            lay = slice(half * h, half * (h + 1))
            g_ref[:, lay, :] = gv
            d_ref[:, lay, :], nm_ref[:, lay, :], nv_ref[:, lay, :] = _adam_update(w_ref[:, lay, :], gv, m_ref[:, lay, :], v_ref[:, lay, :])

    grid, spec = _row_lane_blocks(w.shape)
    full, part = spec(w.shape[1]), spec(half)
    grid_spec = pltpu.PrefetchScalarGridSpec(num_scalar_prefetch=1, grid=grid, in_specs=[full, part, part, full, full], out_specs=[full] * 4)
    return pl.pallas_call(
        body, name="adamw_halves", grid_spec=grid_spec, out_shape=[jax.ShapeDtypeStruct(w.shape, F32)] * 4,
        compiler_params=_cparams(dimension_semantics=("arbitrary", "arbitrary")),
    )(lax.axis_index("c").astype(jnp.int32).reshape(1), w, g_mine, g_sib, m, v)


def _all_reduce_small(x):
    x = _exchange_add("ar_c", x, FLIP_C)
    x = _exchange_add("ar_y", x, FLIP_Y)
    return _exchange_add("ar_x", x, FLIP_X)


def _blocks(S):
    return dict(tm=min(512, S), tm_proj=min(1024, S), ts=min(512, S), tq=min(512, S), tq_big=min(1024, S), tk=min(512, S), tks=min(256, S))


def _pair_pad(vec):
    npair = FOX_HEADS // 2
    v = jnp.pad(vec.reshape(npair, 2), ((0, 0), (0, FF_STRIDE - 2))).reshape(1, npair * FF_STRIDE)
    return jnp.pad(v, ((0, 0), (0, LANES - npair * FF_STRIDE)))


def _pair_unpad(row):
    npair = FOX_HEADS // 2
    return row[0, :npair * FF_STRIDE].reshape(npair, FF_STRIDE)[:, :2].reshape(FOX_HEADS)


def _pool_blockdiag(w_pool):
    g, cg, _ = w_pool.shape
    eye = jnp.eye(g, dtype=w_pool.dtype)
    return jnp.einsum("gh,gcd->gchd", eye, w_pool).reshape(g * cg, g * cg)


def _layer_params(norm_g, b_f, q_norm_g, k_norm_g, w_pool, pool_scale):
    return dict(g=norm_g.reshape(1, -1), qg=jnp.tile(q_norm_g, FOX_HEADS).reshape(1, FOX_W), kg=jnp.tile(k_norm_g, FOX_HEADS).reshape(1, FOX_W),
                bfp=_pair_pad(b_f), wpd=_pool_blockdiag(w_pool).astype(BF16), ps=pool_scale.reshape(1, POOL_W))


def _layer_fwd(x, wt_all, w_out, layer, prm, bs):
    projm, ffo, h = _inproj(x, prm["g"], wt_all, layer, tm=bs["tm_proj"], tn=PROJ_TN)
    qn, ka, kb, v, sq, sk, sv, pooled, yp, pm = _prep(projm, ffo, prm["qg"], prm["kg"], prm["bfp"], prm["wpd"], prm["ps"], ts=bs["ts"])
    o, lse, fm = _fox_fwd(qn, ka, kb, v, projm, tq=bs["tq"], tk=bs["tk"])
    so, sm = _sb_fwd(sq, sk, sv, projm, tq=bs["tq_big"], tk=bs["tks"])
    y = _outproj(x, fm, pm, sm, w_out, layer, tm=bs["tm"])
    saved = dict(x=x, projm=projm, ffo=ffo, h=h, qn=qn, ka=ka, kb=kb, v=v, sq=sq, sk=sk, sv=sv, pooled=pooled, yp=yp,
                 o=o, lse=lse, so=so, fm=fm, pm=pm, sm=sm)
    return y, saved


def _layer_bwd(dy, wt_all, w_out, prm, sv_, bs, layer, stacks):
    dmix, stack_o = _outproj_bwd(dy, sv_["fm"], sv_["pm"], sv_["sm"], w_out, layer, None if stacks is None else stacks[2:], tm=bs["tm"])
    dqn, dkn, dv, dfg, dct, dcr = _fox_bwd(sv_["qn"], sv_["ka"], sv_["kb"], sv_["v"], sv_["o"], sv_["lse"], dmix, sv_["projm"],
                                      tq=bs["tq_big"], tk=bs["tk"])
    dsq, dsk, dsv, dsg = _sb_bwd(sv_["sq"], sv_["sk"], sv_["sv"], sv_["so"], dmix, sv_["projm"], tq=bs["tq"], tk=bs["tks"])
    dproj, dqg, dkg, dbf, dwp, dps = _prep_bwd(sv_["projm"], sv_["ffo"], dqn, dkn, dct, dcr, dv, dfg, dsq, dsk, dsv, dsg, dmix,
                                               sv_["pooled"], sv_["yp"], prm["qg"], prm["kg"], prm["bfp"], prm["wpd"], prm["ps"], ts=bs["ts"])
    stack_m, stack_f = _inproj_dw(sv_["h"], dproj, layer, None if stacks is None else stacks[:2], ts=bs["tm_proj"], tn=PROJ_TN)
    dx, dg = _inproj_dx(dproj, wt_all, layer, sv_["x"], prm["g"], dy, tm=min(256, bs["tm"]))
    grads = dict(
        norm_g=dg[0],
        b_f=_pair_unpad(dbf), q_norm_g=dqg.reshape(FOX_HEADS, HEAD_DIM).sum(0), k_norm_g=dkg.reshape(FOX_HEADS, HEAD_DIM).sum(0),
        w_pool=jnp.stack([dwp[HEAD_DIM * g:HEAD_DIM * (g + 1), HEAD_DIM * g:HEAD_DIM * (g + 1)] for g in range(4)]),
        pool_scale=dps[0])
    return dx, grads, (stack_m, stack_f, stack_o)


def _local_step(x, target, wt_all, w_out, norm_g, b_f, q_norm_g, k_norm_g, w_pool, pool_scale):
    S, D = x.shape
    bs = _blocks(S)
    prms = [_layer_params(norm_g[l], b_f[l], q_norm_g[l], k_norm_g[l], w_pool[l], pool_scale[l]) for l in range(DEPTH)]
    saved = []
    y = x
    for l in range(DEPTH):
        y, s_ = _layer_fwd(y, wt_all, w_out, l, prms[l], bs)
        saved.append(s_)
    dy, sq = _loss_head(y, target, tm=bs["tm"])
    loss = 0.5 * jnp.sum(sq) / D
    grads = [None] * DEPTH
    stacks = None
    for l in reversed(range(DEPTH)):
        dy, grads[l], stacks = _layer_bwd(dy, wt_all, w_out, prms[l], saved[l], bs, l, stacks)
    stacked = {k: jnp.stack([g[k] for g in grads]) for k in grads[0]}
    return loss, dy, stacked, stacks


SMALL = ("norm_g", "b_f", "q_norm_g", "k_norm_g", "w_pool", "pool_scale")


def _pack_small(gr):
    flat = jnp.concatenate([gr[k].reshape(-1) for k in SMALL])
    pad = (-flat.shape[0]) % (8 * LANES)
    return jnp.pad(flat, (0, pad)).reshape(-1, LANES)


def _unpack_small(packed, like):
    flat = packed.reshape(-1)
    out, off = {}, 0
    for k in SMALL:
        n = like[k].size
        out[k] = flat[off:off + n].reshape(like[k].shape)
        off += n
    return out


def kernel(x, norm_g, w_in, b_f, q_norm_g, k_norm_g, w_pool, pool_scale, w_out, loss_target, m_norm_g, m_w_in, m_b_f, m_q_norm_g, m_k_norm_g, m_w_pool, m_pool_scale, m_w_out, v_norm_g, v_w_in, v_b_f, v_q_norm_g, v_k_norm_g, v_w_pool, v_pool_scale, v_w_out):
    weights = dict(norm_g=norm_g, w_in=w_in, b_f=b_f, q_norm_g=q_norm_g, k_norm_g=k_norm_g, w_pool=w_pool, pool_scale=pool_scale, w_out=w_out)
    mom_m = dict(norm_g=m_norm_g, w_in=m_w_in, b_f=m_b_f, q_norm_g=m_q_norm_g, k_norm_g=m_k_norm_g, w_pool=m_w_pool, pool_scale=m_pool_scale, w_out=m_w_out)
    mom_v = dict(norm_g=v_norm_g, w_in=v_w_in, b_f=v_b_f, q_norm_g=v_q_norm_g, k_norm_g=v_k_norm_g, w_pool=v_w_pool, pool_scale=v_pool_scale, w_out=v_w_out)
    shard_cols = w_in.shape[2]
    shard_rows = w_out.shape[1]

    cols_first = lambda a: jnp.transpose(a, (2, 0, 1))
    w_in_t = cols_first(w_in)
    w_in_t_full, w_out_full = _gather_weights(w_in_t, w_out)
    wt_all = _to_aligned(w_in_t_full)
    loss, dx, gr, stacks = _local_step(x[0], loss_target[0], wt_all, w_out_full, norm_g, b_f, q_norm_g, k_norm_g, w_pool, pool_scale)
    loss = lax.psum(loss, ("x", "y", "c"))

    (g_in_mine, g_in_sib), g_w_out = _reduce_scatter(*stacks, shard_cols, shard_rows)
    small = _unpack_small(_all_reduce_small(_pack_small(gr)), {k: weights[k] for k in SMALL})
    grad_w = dict(small, w_out=g_w_out)

    names = ("norm_g", "w_in", "b_f", "q_norm_g", "k_norm_g", "w_pool", "pool_scale", "w_out")
    upd = {k: _adamw_nd(weights[k], grad_w[k], mom_m[k], mom_v[k]) for k in names if k != "w_in"}
    in_t = _adamw_halves(w_in_t, g_in_mine, g_in_sib, cols_first(mom_m["w_in"]), cols_first(mom_v["w_in"]))
    grad_w["w_in"], *upd["w_in"] = [jnp.transpose(a, (1, 2, 0)) for a in in_t]
    return (loss, dx[None], *[grad_w[k] for k in names], *[upd[k][0] for k in names], *[upd[k][1] for k in names], *[upd[k][2] for k in names])
```

```python
import functools

import jax
import jax.numpy as jnp
from jax import lax
from jax.experimental import pallas as pl
from jax.experimental.pallas import tpu as pltpu

F32 = jnp.float32
BF16 = jnp.bfloat16

DEPTH = 4
HEAD_DIM = 64
FOX_HEADS = 8
SB_HEADS = 4
FOX_W = FOX_HEADS * HEAD_DIM
SB_W = SB_HEADS * HEAD_DIM
POOL_W = 256
POOL_WINDOWS = (2, 4, 8, 16)
POOL_HALO = 16
D_MIX = FOX_W + POOL_W + SB_W
EPS = 1e-6
NEG = -1e30
QK_SCALE = HEAD_DIM ** -0.5

ORIG_FOX = 4 * FOX_W
ORIG_FF = ORIG_FOX
ORIG_REST = ORIG_FF + FOX_HEADS
D_IN = ORIG_REST + 2 * POOL_W + 4 * SB_W

C_FQ, C_FK, C_FV, C_FG = 0, FOX_W, 2 * FOX_W, 3 * FOX_W
C_PX = 4 * FOX_W
C_PG = C_PX + POOL_W
C_SQ = C_PG + POOL_W
C_SK, C_SV, C_SG = C_SQ + SB_W, C_SQ + 2 * SB_W, C_SQ + 3 * SB_W
PM = C_SG + SB_W
LANES = 128
PW = PM + LANES
FF_STRIDE = 8
AUG = 3

ADAM_LR = 0.001
ADAM_B1 = 0.9
ADAM_B2 = 0.999
ADAM_EPS = 1e-08
ADAM_WD = 0.01
ADAM_STEP = 10

VMEM_LIMIT = 48 * 1024 * 1024
PROJ_TN = PM // 2


def _cparams(**kw):
    return pltpu.CompilerParams(vmem_limit_bytes=VMEM_LIMIT, **kw)


def _dot(a, b):
    return jnp.dot(a, b, preferred_element_type=F32)


def _dot_nt(a, b):
    return lax.dot_general(a, b, (((1,), (1,)), ((), ())), preferred_element_type=F32)


def _dot_tn(a, b):
    return lax.dot_general(a, b, (((0,), (0,)), ((), ())), preferred_element_type=F32)


def _split2(x):
    hi = x.astype(BF16)
    lo = (x - hi.astype(F32)).astype(BF16)
    return hi, lo


def _split3(x):
    hi = x.astype(BF16)
    r = x - hi.astype(F32)
    mid = r.astype(BF16)
    lo = (r - mid.astype(F32)).astype(BF16)
    return hi, mid, lo


def _dot_exact_rhs(x, m):
    hi, mid, lo = _split3(x)
    return _dot(hi, m) + _dot(mid, m) + _dot(lo, m)


def _dot_exact_lhs(m, x):
    hi, mid, lo = _split3(x)
    return _dot(m, hi) + _dot(m, mid) + _dot(m, lo)


def _sigmoid(x):
    return 1.0 / (1.0 + jnp.exp(-x))


def _silu_pair(x):
    s = _sigmoid(x)
    return x * s, s * (1.0 + x * (1.0 - s))


def _iota(shape, dim):
    return lax.broadcasted_iota(jnp.int32, shape, dim)


def _ones_where(cond):
    return jnp.where(cond, 1.0, 0.0).astype(BF16)


def _head_blockdiag(w):
    return _ones_where((_iota((w, w), 0) >> 6) == (_iota((w, w), 1) >> 6))


def _group_sum(x, bd):
    hi, lo = _split2(x)
    return _dot(hi, bd) + _dot(lo, bd)


def _lane_pick(x, lane_idx, lane):
    return jnp.sum(jnp.where(lane_idx == lane, x, 0.0), axis=1, keepdims=True)


def _inproj(x, g, wt_all, layer, *, tm, tn):
    S, D = x.shape
    nj = PM // tn

    def body(x_ref, g_ref, w_ref, wff_ref, proj_ref, ff_ref, h_ref):
        @pl.when(pl.program_id(1) == 0)
        def _():
            xf = x_ref[...]
            ms = jnp.mean(xf * xf, axis=-1, keepdims=True)
            h = (xf * lax.rsqrt(ms + EPS) * g_ref[...]).astype(BF16)
            h_ref[...] = h
            ff_ref[...] = _dot_nt(h, wff_ref[...])

        proj_ref[...] = _dot_nt(h_ref[...], w_ref[...])

    return pl.pallas_call(
        body, name="inproj", grid=(S // tm, nj),
        in_specs=[pl.BlockSpec((tm, D), lambda i, j: (i, 0)),
                  pl.BlockSpec((1, D), lambda i, j: (0, 0)),
                  pl.BlockSpec((None, tn, D), lambda i, j: (layer, j, 0)),
                  pl.BlockSpec((None, LANES, D), lambda i, j: (layer, PM // LANES, 0))],
        out_specs=[pl.BlockSpec((tm, tn), lambda i, j: (i, j)),
                   pl.BlockSpec((tm, LANES), lambda i, j: (i, 0)),
                   pl.BlockSpec((tm, D), lambda i, j: (i, 0))],
        out_shape=[jax.ShapeDtypeStruct((S, PM), F32), jax.ShapeDtypeStruct((S, LANES), F32),
                   jax.ShapeDtypeStruct((S, D), BF16)],
        compiler_params=_cparams(dimension_semantics=("arbitrary", "arbitrary")),
    )(x, g, wt_all, wt_all)


def _pool_group_select(lane_group, vals):
    return jnp.where(lane_group == 0, vals[0], jnp.where(lane_group == 1, vals[1], jnp.where(lane_group == 2, vals[2], vals[3])))


def _prep(projm, ffo, qg, kg, bfp, wpd, ps, *, ts):
    S = projm.shape[0]
    nb = S // ts
    hb = ts // POOL_HALO

    def body(fq_ref, fk_ref, fv_ref, pp_ref, halo_ref, ff_ref, sq_ref, sk_ref, sv_ref,
             qg_ref, kg_ref, bf_ref, wpd_ref, ps_ref,
             qn_ref, ka_ref, kb_ref, v_ref, sqo_ref, sko_ref, svo_ref, pooled_ref, yp_ref, pm_ref,
             carry_ref, c_ref, buf_ref):
        i = pl.program_id(0)
        bd = _head_blockdiag(FOX_W)
        normed = []
        for src, g_ref in ((fq_ref, qg_ref), (fk_ref, kg_ref)):
            q = src[...]
            ss = _group_sum(q * q, bd)
            normed.append(q * lax.rsqrt(ss * (1.0 / HEAD_DIM) + EPS) * g_ref[...])
        qn_ref[...] = (normed[0] * QK_SCALE).astype(BF16)
        kn = normed[1]
        v_ref[...] = fv_ref[...].astype(BF16)
        sqo_ref[...] = (sq_ref[...] * QK_SCALE).astype(BF16)
        sko_ref[...] = sk_ref[...].astype(BF16)
        svo_ref[...] = sv_ref[...].astype(BF16)

        @pl.when(i == 0)
        def _():
            carry_ref[...] = jnp.zeros_like(carry_ref)

        z = ff_ref[...] + bf_ref[...]
        lf = jnp.minimum(z, 0.0) - jnp.log(1.0 + jnp.exp(-jnp.abs(z)))
        tri = _ones_where(_iota((ts, ts), 1) <= _iota((ts, ts), 0))
        c = _dot_exact_lhs(tri, lf) + carry_ref[...]
        c_ref[...] = c
        carry_ref[...] = c_ref[ts - 1:ts, :]
        parts = jnp.concatenate(_split3(-c), axis=1)
        row = _iota((AUG * LANES, FOX_W), 0)
        col = _iota((AUG * LANES, FOX_W), 1)
        part, src = row >> 7, row & (LANES - 1)
        pair, off = col >> 7, col & (LANES - 1)
        sel_a = _ones_where((src == FF_STRIDE * pair) & (off == HEAD_DIM + part))
        sel_b = _ones_where((src == FF_STRIDE * pair + 1) & (off == part))
        first_half = (_iota((1, FOX_W), 1) & HEAD_DIM) == 0
        ka_ref[...] = jnp.where(first_half, kn, _dot(parts, sel_a)).astype(BF16)
        kb_ref[...] = jnp.where(first_half, _dot(parts, sel_b), kn).astype(BF16)

        x = pp_ref[:, 0:POOL_W]
        pg = pp_ref[:, POOL_W:2 * POOL_W]
        halo = jnp.where(i > 0, halo_ref[:, 0:POOL_W], 0.0)
        buf_ref[0:POOL_HALO, :] = halo
        buf_ref[POOL_HALO:POOL_HALO + ts, :] = x
        acc = x
        snaps = []
        for d in range(1, POOL_HALO):
            acc = acc + buf_ref[pl.ds(POOL_HALO - d, ts), :]
            if d + 1 in POOL_WINDOWS:
                snaps.append(acc)
        lane_group = _iota((1, POOL_W), 1) >> 6
        wsum = _pool_group_select(lane_group, snaps)
        wlen = _pool_group_select(lane_group, [float(w) for w in POOL_WINDOWS])
        tpos = (i * ts + _iota((ts, 1), 0) + 1).astype(F32)
        pooled = wsum / jnp.minimum(tpos, wlen) - x
        pb = pooled.astype(BF16)
        pooled_ref[...] = pb
        yp = _dot(pb, wpd_ref[...])
        yp_ref[...] = yp
        pm_ref[...] = (yp * ps_ref[...] * (pg * _sigmoid(pg))).astype(BF16)

    blk = lambda w, c: pl.BlockSpec((ts, w), lambda i: (i, c))
    full = lambda a: pl.BlockSpec(a.shape, lambda i: (0,) * a.ndim)
    out_shapes = [
        jax.ShapeDtypeStruct((S, FOX_W), BF16), jax.ShapeDtypeStruct((S, FOX_W), BF16), jax.ShapeDtypeStruct((S, FOX_W), BF16),
        jax.ShapeDtypeStruct((S, FOX_W), BF16),
        jax.ShapeDtypeStruct((S, SB_W), BF16), jax.ShapeDtypeStruct((S, SB_W), BF16), jax.ShapeDtypeStruct((S, SB_W), BF16),
        jax.ShapeDtypeStruct((S, POOL_W), BF16), jax.ShapeDtypeStruct((S, POOL_W), F32), jax.ShapeDtypeStruct((S, POOL_W), BF16),
    ]
    out_specs = [
        blk(FOX_W, 0), blk(FOX_W, 0), blk(FOX_W, 0), blk(FOX_W, 0),
        blk(SB_W, 0), blk(SB_W, 0), blk(SB_W, 0),
        blk(POOL_W, 0), blk(POOL_W, 0), blk(POOL_W, 0),
    ]
    return pl.pallas_call(
        body, name="prep", grid=(nb,),
        in_specs=[blk(FOX_W, C_FQ // FOX_W), blk(FOX_W, C_FK // FOX_W), blk(FOX_W, C_FV // FOX_W), blk(2 * POOL_W, C_PX // (2 * POOL_W)),
                  pl.BlockSpec((POOL_HALO, 2 * POOL_W), lambda i: (jnp.maximum(i * hb - 1, 0), C_PX // (2 * POOL_W))),
                  blk(LANES, 0),
                  blk(SB_W, C_SQ // SB_W), blk(SB_W, C_SK // SB_W), blk(SB_W, C_SV // SB_W),
                  full(qg), full(kg), full(bfp), full(wpd), full(ps)],
        out_specs=out_specs, out_shape=out_shapes,
        scratch_shapes=[pltpu.VMEM((1, LANES), F32), pltpu.VMEM((ts, LANES), F32), pltpu.VMEM((ts + POOL_HALO, POOL_W), F32)],
        compiler_params=_cparams(dimension_semantics=("arbitrary",)),
    )(projm, projm, projm, projm, projm, ffo, projm, projm, projm, qg, kg, bfp, wpd, ps)


def _pair_masks(x):
    ma = _iota((1, LANES), 1) < HEAD_DIM
    zero = jnp.zeros_like(x)
    return jnp.where(ma, x, zero), jnp.where(ma, zero, x)


DIAG_TILE = 256


def _diag_tiles(tq, size=DIAG_TILE):
    size = min(tq, size)
    return [(t * size, size) for t in range(tq // size)]


def _put_rows(old, new, r0):
    return new if r0 == 0 else jnp.concatenate([old[:r0], new], axis=0)


def _aug_queries(q):
    lane = _iota((1, LANES), 1)
    one = jnp.ones_like(q)
    zero = jnp.zeros_like(q)
    qa = jnp.where(lane < HEAD_DIM, q, jnp.where(lane < HEAD_DIM + AUG, one, zero))
    qb = jnp.where(lane >= HEAD_DIM, q, jnp.where(lane < AUG, one, zero))
    return qa, qb


EXP_DEAD = -120.0
PACK = 16


def _fox_walk_left(nfull, tk, block, carry, k_refs, qk_bound, row_floor):
    lane = _iota((1, LANES), 1)

    def score_bound(h, j):
        k0 = pl.multiple_of(jnp.maximum(j, 0) * tk + tk - PACK, PACK)
        last = k_refs[h][pl.ds(k0, PACK), :].astype(F32)
        lo = HEAD_DIM if h == 0 else 0
        negc = jnp.sum(jnp.where((lane >= lo) & (lane < lo + AUG), last, 0.0), axis=1, keepdims=True)
        return qk_bound + jnp.max(negc)

    def alive(state):
        jj, c = state
        j = nfull - 1 - jj
        floors = row_floor(c)
        return (jj < nfull) & ((score_bound(0, j) - floors[0] >= EXP_DEAD) | (score_bound(1, j) - floors[1] >= EXP_DEAD))

    def step(state):
        jj, c = state
        return jj + 1, block(pl.multiple_of((nfull - 1 - jj) * tk, tk), tk, 0, c, False)

    return lax.while_loop(alive, step, (jnp.int32(0), carry))[1]


def _fox_fwd(qn, ka, kb, v, projm, qkb, *, tq, tk):
    S = qn.shape[0]
    npair = FOX_HEADS // 2

    def body(q_ref, ka_ref, kb_ref, v_ref, fg_ref, qkb_ref, o_ref, lse_ref, fm_ref):
        qi = pl.program_id(1)
        lane = _iota((1, LANES), 1)
        ma = lane < HEAD_DIM
        qaug = _aug_queries(q_ref[...])
        k_refs = (ka_ref, kb_ref)

        def block(k0, tkl, r0, carry, masked):
            vb = v_ref[pl.ds(k0, tkl), :]
            if masked:
                mask = (k0 + _iota((tq - r0, tkl), 1)) <= (qi * tq + r0 + _iota((tq - r0, tkl), 0))
            scores = [_dot_nt(qaug[h][r0:], k_refs[h][pl.ds(k0, tkl), :]) for h in range(2)]
            new = []
            for h in range(2):
                m, l, acc = [x[r0:] for x in carry[h]]
                s = jnp.where(mask, scores[h], NEG) if masked else scores[h]
                m_new = jnp.maximum(m, jnp.max(s, axis=1, keepdims=True))
                alpha = jnp.exp(m - m_new)
                p = jnp.exp(s - m_new)
                sub = (m_new, alpha * l + jnp.sum(p, axis=1, keepdims=True), alpha * acc + _dot(p.astype(BF16), vb))
                new.append(tuple(_put_rows(old, x, r0) for old, x in zip(carry[h], sub)))
            return tuple(new)

        carry = tuple((jnp.full((tq, 1), NEG, F32), jnp.zeros((tq, 1), F32), jnp.zeros((tq, LANES), F32)) for _ in range(2))
        for off, size in _diag_tiles(tq, tq):
            carry = block(pl.multiple_of(qi * tq + off, size), size, off, carry, True)
        carry = _fox_walk_left((qi * tq) // tk, tk, block, carry, k_refs, jnp.max(qkb_ref[...]),
                               lambda c: (jnp.min(c[0][0]), jnp.min(c[1][0])))
        (ma_, la, acca), (mb_, lb, accb) = carry
        o = jnp.where(ma, acca / la, accb / lb)
        o_ref[...] = o
        lse_ref[...] = jnp.where(ma, ma_ + jnp.log(la), mb_ + jnp.log(lb))
        fg = fg_ref[...]
        fm_ref[...] = (o * (fg * _sigmoid(fg))).astype(BF16)

    qblk = pl.BlockSpec((tq, LANES), lambda p, i: (i, p))
    kvblk = pl.BlockSpec((S, LANES), lambda p, i: (0, p))
    return pl.pallas_call(
        body, name="fox_fwd", grid=(npair, S // tq),
        in_specs=[qblk, kvblk, kvblk, kvblk,
                  pl.BlockSpec((tq, LANES), lambda p, i: (i, C_FG // LANES + p)),
                  pl.BlockSpec((1, LANES), lambda p, i: (0, 0))],
        out_specs=[qblk, qblk, qblk],
        out_shape=[jax.ShapeDtypeStruct((S, FOX_W), F32), jax.ShapeDtypeStruct((S, FOX_W), F32), jax.ShapeDtypeStruct((S, FOX_W), BF16)],
        compiler_params=_cparams(dimension_semantics=("arbitrary", "arbitrary")),
    )(qn, ka, kb, v, projm, qkb)


def _suffix_sums(x, tmat2):
    return _dot(jnp.concatenate(_split2(x), axis=1), tmat2)


def _suffix_matrix(tk, inclusive):
    rr, cc = _iota((2 * tk, tk), 0) & (tk - 1), _iota((2 * tk, tk), 1)
    return _ones_where(rr >= cc) if inclusive else _ones_where(rr > cc)


def _sb_scores(qh, kb, causal, tmat2, r_runs):
    heads = range(2)
    zs = [_dot_nt(qh[h], kb) for h in heads]
    nsps = [jnp.minimum(-z, 0.0) - jnp.log(1.0 + jnp.exp(-jnp.abs(z))) for z in zs]
    lbs = nsps if causal is None else [jnp.where(causal, n, 0.0) for n in nsps]
    rins = [_suffix_sums(lb, tmat2) for lb in lbs]
    args = [zs[h] + lbs[h] + (rins[h] + r_runs[h]) for h in heads]
    a_s = [jnp.exp(arg if causal is None else jnp.where(causal, arg, NEG)) for arg in args]
    return zs, nsps, lbs, a_s


def _sb_walk_left(nfull, tk, block, carry, running_sums):
    def alive(state):
        jj, c = state
        ra, rb = running_sums(c)
        return (jj < nfull) & (jnp.max(jnp.maximum(ra, rb)) >= EXP_DEAD)

    def step(state):
        jj, c = state
        return jj + 1, block(pl.multiple_of((nfull - 1 - jj) * tk, tk), 0, c, False)

    return lax.while_loop(alive, step, (jnp.int32(0), carry))[1]


def _sb_fwd(sq, sk, sv, projm, *, tq, tk):
    S = sq.shape[0]
    npair = SB_HEADS // 2

    def body(q_ref, k_ref, v_ref, sg_ref, o_ref, sm_ref):
        qi = pl.program_id(1)
        lane = _iota((1, LANES), 1)
        ma = lane < HEAD_DIM
        qh = _pair_masks(q_ref[...])
        tmat2 = _suffix_matrix(tk, inclusive=False)
        nfull = (qi * tq) // tk

        def block(k0, r0, carry, masked):
            nr = tq - r0
            kb = k_ref[pl.ds(k0, tk), :]
            vb = v_ref[pl.ds(k0, tk), :]
            causal = (k0 + _iota((nr, tk), 1)) < (qi * tq + r0 + _iota((nr, tk), 0)) if masked else None
            _, _, lbs, a_s = _sb_scores([q[r0:] for q in qh], kb, causal, tmat2, [carry[h][0][r0:] for h in range(2)])
            pv = _dot(jnp.concatenate([a.astype(BF16) for a in a_s], axis=0), vb)
            return tuple((_put_rows(carry[h][0], carry[h][0][r0:] + jnp.sum(lbs[h], axis=1, keepdims=True), r0),
                          _put_rows(carry[h][1], carry[h][1][r0:] + pv[h * nr:(h + 1) * nr], r0)) for h in range(2))

        carry = tuple((jnp.zeros((tq, 1), F32), jnp.zeros((tq, LANES), F32)) for _ in range(2))
        for off, size in reversed(_diag_tiles(tq)):
            assert size == tk
            carry = block(pl.multiple_of(qi * tq + off, tk), off, carry, True)
        (_, acca), (_, accb) = _sb_walk_left(nfull, tk, block, carry, lambda c: (c[0][0], c[1][0]))
        o = jnp.where(ma, acca, accb)
        o_ref[...] = o
        sg = sg_ref[...]
        sm_ref[...] = (o * (sg * _sigmoid(sg))).astype(BF16)

    qblk = pl.BlockSpec((tq, LANES), lambda p, i: (i, p))
    kvblk = pl.BlockSpec((S, LANES), lambda p, i: (0, p))
    return pl.pallas_call(
        body, name="sb_fwd", grid=(npair, S // tq),
        in_specs=[qblk, kvblk, kvblk, pl.BlockSpec((tq, LANES), lambda p, i: (i, C_SG // LANES + p))],
        out_specs=[qblk, qblk],
        out_shape=[jax.ShapeDtypeStruct((S, SB_W), F32), jax.ShapeDtypeStruct((S, SB_W), BF16)],
        compiler_params=_cparams(dimension_semantics=("arbitrary", "arbitrary")),
    )(sq, sk, sv, projm)


def _outproj(x, fm, pm, sm, w_out, layer, *, tm):
    S, D = x.shape

    def body(x_ref, fm_ref, pm_ref, sm_ref, w_ref, y_ref):
        y = x_ref[...] + _dot(fm_ref[...], w_ref[0:FOX_W, :])
        y = y + _dot(pm_ref[...], w_ref[FOX_W:FOX_W + POOL_W, :])
        y_ref[...] = y + _dot(sm_ref[...], w_ref[FOX_W + POOL_W:D_MIX, :])

    row = lambda w: pl.BlockSpec((tm, w), lambda i: (i, 0))
    return pl.pallas_call(
        body, name="outproj", grid=(S // tm,),
        in_specs=[row(D), row(FOX_W), row(POOL_W), row(SB_W), pl.BlockSpec((None, D_MIX, D), lambda i: (layer, 0, 0))],
        out_specs=row(D), out_shape=jax.ShapeDtypeStruct((S, D), F32),
        compiler_params=_cparams(dimension_semantics=("arbitrary",)),
    )(x, fm, pm, sm, w_out)


def _loss_head(y, target, *, tm):
    S, D = y.shape

    def body(y_ref, t_ref, dy_ref, sq_ref):
        @pl.when(pl.program_id(0) == 0)
        def _():
            sq_ref[...] = jnp.zeros_like(sq_ref)

        d = y_ref[...] - t_ref[...]
        dy_ref[...] = d * (1.0 / D)
        sq_ref[...] += jnp.sum(d * d, axis=0, keepdims=True)

    row = pl.BlockSpec((tm, D), lambda i: (i, 0))
    return pl.pallas_call(
        body, name="loss_head", grid=(S // tm,),
        in_specs=[row, row], out_specs=[row, pl.BlockSpec((1, D), lambda i: (0, 0))],
        out_shape=[jax.ShapeDtypeStruct((S, D), F32), jax.ShapeDtypeStruct((1, D), F32)],
        compiler_params=_cparams(dimension_semantics=("arbitrary",)),
    )(y, target)


def _outproj_bwd(dy, fm, pm, sm, w_out, layer, stacks, *, tm):
    S, D = dy.shape

    def body(dy_ref, fm_ref, pm_ref, sm_ref, w_ref, dm_ref, dw_ref):
        @pl.when(pl.program_id(0) == 0)
        def _():
            dw_ref[...] = jnp.zeros_like(dw_ref)

        dyb = dy_ref[...].astype(BF16)
        dm_ref[...] = _dot_nt(dyb, w_ref[...])
        dw_ref[0:FOX_W, :] += _dot_tn(fm_ref[...], dyb)
        dw_ref[FOX_W:FOX_W + POOL_W, :] += _dot_tn(pm_ref[...], dyb)
        dw_ref[FOX_W + POOL_W:D_MIX, :] += _dot_tn(sm_ref[...], dyb)

    row = lambda w: pl.BlockSpec((tm, w), lambda i: (i, 0))
    wspec = pl.BlockSpec((None, D_MIX, D), lambda i: (layer, 0, 0))
    return _stack_call(
        body, "outproj_bwd", (S // tm,), [row(D), row(FOX_W), row(POOL_W), row(SB_W), wspec], (dy, fm, pm, sm, w_out),
        [pl.BlockSpec((None, D_MIX, D), lambda i: (layer, 0, 0))], [(D_MIX, D)], stacks,
        plain_specs=[row(D_MIX)], plain_shapes=[jax.ShapeDtypeStruct((S, D_MIX), F32)],
        compiler_params=_cparams(dimension_semantics=("arbitrary",)))


def _fox_bwd(qn, ka, kb, v, o, lse, dmix, projm, qkb, *, tq, tk):
    S = qn.shape[0]
    npair = FOX_HEADS // 2

    def body(q_ref, ka_ref, kb_ref, v_ref, o_ref, lse_ref, dm_ref, fg_ref, qkb_ref,
             dq_ref, dk_ref, dv_ref, dfg_ref, dct_ref, dcr_ref):
        qi = pl.program_id(1)

        @pl.when(qi == 0)
        def _():
            dk_ref[...] = jnp.zeros_like(dk_ref)
            dv_ref[...] = jnp.zeros_like(dv_ref)
            dct_ref[...] = jnp.zeros_like(dct_ref)

        lane = _iota((1, LANES), 1)
        ma = lane < HEAD_DIM
        qh = _pair_masks(q_ref[...])
        qaug = _aug_queries(q_ref[...])
        k_refs = (ka_ref, kb_ref)
        lsev = lse_ref[...]
        lse = (_lane_pick(lsev, lane, 0), _lane_pick(lsev, lane, HEAD_DIM))
        fg = fg_ref[...]
        silu, dsilu = _silu_pair(fg)
        dm = dm_ref[...]
        ov = o_ref[...]
        do = dm * silu
        dfg_ref[...] = dm * ov * dsilu
        dd = do * ov
        dsum = (jnp.sum(jnp.where(ma, dd, 0.0), axis=1, keepdims=True), jnp.sum(jnp.where(ma, 0.0, dd), axis=1, keepdims=True))
        doh = _pair_masks(do.astype(BF16))

        def block(k0, tkl, r0, carry, masked):
            vb = v_ref[pl.ds(k0, tkl), :]
            if masked:
                mask = (k0 + _iota((tq - r0, tkl), 1)) <= (qi * tq + r0 + _iota((tq - r0, tkl), 0))
            heads = range(2)
            kaugs = [k_refs[h][pl.ds(k0, tkl), :] for h in heads]
            scores = [_dot_nt(qaug[h][r0:], kaugs[h]) for h in heads]
            dps = [_dot_nt(doh[h][r0:], vb) for h in heads]
            ps, dss, rows = [], [], []
            for h in heads:
                s = jnp.where(mask, scores[h], NEG) if masked else scores[h]
                p = jnp.exp(s - lse[h][r0:])
                dsf = p * (dps[h] - dsum[h][r0:])
                dct_ref[0, h:h + 1, pl.ds(k0, tkl)] -= jnp.sum(dsf, axis=0, keepdims=True)
                rows.append(_put_rows(carry[1 + h], carry[1 + h][r0:] + jnp.sum(dsf, axis=1, keepdims=True), r0))
                ps.append(p.astype(BF16))
                dss.append(dsf.astype(BF16))
            dv_ref[pl.ds(k0, tkl), :] += _dot_tn(jnp.concatenate(ps, axis=0), jnp.concatenate([d[r0:] for d in doh], axis=0))
            dk_ref[pl.ds(k0, tkl), :] += _dot_tn(jnp.concatenate(dss, axis=0), jnp.concatenate([q[r0:] for q in qh], axis=0))
            kh = jnp.concatenate([_pair_masks(kaugs[h])[h] for h in heads], axis=0)
            dq = _put_rows(carry[0], carry[0][r0:] + _dot(jnp.concatenate(dss, axis=1), kh), r0)
            return (dq, rows[0], rows[1])

        zcol = jnp.zeros((tq, 1), F32)
        carry = (jnp.zeros((tq, LANES), F32), zcol, zcol)
        for off, size in _diag_tiles(tq):
            carry = block(pl.multiple_of(qi * tq + off, size), size, off, carry, True)
        floors = (jnp.min(lse[0]), jnp.min(lse[1]))
        dq, rowa, rowb = _fox_walk_left((qi * tq) // tk, tk, block, carry, k_refs, jnp.max(qkb_ref[...]), lambda c: floors)
        dq_ref[...] = dq * QK_SCALE
        dcr_ref[0] = jnp.where(ma, rowa, rowb)

    qblk = pl.BlockSpec((tq, LANES), lambda p, i: (i, p))
    kvblk = pl.BlockSpec((S, LANES), lambda p, i: (0, p))
    f32out = jax.ShapeDtypeStruct((S, FOX_W), F32)
    ctblk = pl.BlockSpec((1, FF_STRIDE, S), lambda p, i: (p, 0, 0))
    return pl.pallas_call(
        body, name="fox_bwd", grid=(npair, S // tq),
        in_specs=[qblk, kvblk, kvblk, kvblk, qblk, qblk, qblk,
                  pl.BlockSpec((tq, LANES), lambda p, i: (i, C_FG // LANES + p)),
                  pl.BlockSpec((1, LANES), lambda p, i: (0, 0))],
        out_specs=[qblk, kvblk, kvblk, qblk, ctblk, pl.BlockSpec((1, tq, LANES), lambda p, i: (p, i, 0))],
        out_shape=[f32out, f32out, f32out, f32out, jax.ShapeDtypeStruct((npair, FF_STRIDE, S), F32),
                   jax.ShapeDtypeStruct((npair, S, LANES), F32)],
        compiler_params=_cparams(dimension_semantics=("arbitrary", "arbitrary")),
    )(qn, ka, kb, v, o, lse, dmix, projm, qkb)


def _sb_bwd(sq, sk, sv, o, dmix, projm, *, tq, tk):
    S = sq.shape[0]
    npair = SB_HEADS // 2
    mix0 = (FOX_W + POOL_W) // LANES

    def body(q_ref, k_ref, v_ref, o_ref, dm_ref, sg_ref, dq_ref, dk_ref, dv_ref, dsg_ref):
        qi = pl.program_id(1)

        @pl.when(qi == 0)
        def _():
            dk_ref[...] = jnp.zeros_like(dk_ref)
            dv_ref[...] = jnp.zeros_like(dv_ref)

        lane = _iota((1, LANES), 1)
        ma = lane < HEAD_DIM
        qh = _pair_masks(q_ref[...])
        sg = sg_ref[...]
        silu, dsilu = _silu_pair(sg)
        dm = dm_ref[...]
        ov = o_ref[...]
        do = dm * silu
        dsg_ref[...] = dm * ov * dsilu
        dob = do.astype(BF16)
        dd = dob.astype(F32) * ov
        dsum = (jnp.sum(jnp.where(ma, dd, 0.0), axis=1, keepdims=True), jnp.sum(jnp.where(ma, 0.0, dd), axis=1, keepdims=True))
        doh = _pair_masks(dob)
        tmat2 = _suffix_matrix(tk, inclusive=False)
        tmat2_inc = _suffix_matrix(tk, inclusive=True)
        nfull = (qi * tq) // tk

        def block(k0, r0, carry, masked):
            nr = tq - r0
            kb = k_ref[pl.ds(k0, tk), :]
            vb = v_ref[pl.ds(k0, tk), :]
            kh = _pair_masks(kb)
            causal = (k0 + _iota((nr, tk), 1)) < (qi * tq + r0 + _iota((nr, tk), 0)) if masked else None
            heads = range(2)
            qs = [q[r0:] for q in qh]
            dos = [d[r0:] for d in doh]
            das = [_dot_nt(dos[h], vb) for h in heads]
            zs, nsps, lbs, a_s = _sb_scores(qs, kb, causal, tmat2, [carry[h][0][r0:] for h in heads])
            abs_ = [a.astype(BF16) for a in a_s]
            us = [abs_[h].astype(F32) * das[h] for h in heads]
            uins = [_suffix_sums(u, tmat2_inc) for u in us]
            dzs = []
            for h in heads:
                cum_u = dsum[h][r0:] - (uins[h] + carry[h][1][r0:])
                dz = us[h] * jnp.exp(nsps[h]) - jnp.exp(zs[h] + nsps[h]) * cum_u
                if masked:
                    dz = jnp.where(causal, dz, 0.0)
                dzs.append(dz.astype(BF16))
            dv_ref[pl.ds(k0, tk), :] += _dot_tn(jnp.concatenate(abs_, axis=0), jnp.concatenate(dos, axis=0))
            dk_ref[pl.ds(k0, tk), :] += _dot_tn(jnp.concatenate(dzs, axis=0), jnp.concatenate(qs, axis=0))
            dq = _put_rows(carry[2], carry[2][r0:] + _dot(jnp.concatenate(dzs, axis=1), jnp.concatenate(kh, axis=0)), r0)
            new = [(_put_rows(carry[h][0], carry[h][0][r0:] + jnp.sum(lbs[h], axis=1, keepdims=True), r0),
                    _put_rows(carry[h][1], carry[h][1][r0:] + jnp.sum(us[h], axis=1, keepdims=True), r0)) for h in heads]
            return (new[0], new[1], dq)

        zcol = jnp.zeros((tq, 1), F32)
        carry = ((zcol, zcol), (zcol, zcol), jnp.zeros((tq, LANES), F32))
        for off, size in reversed(_diag_tiles(tq)):
            assert size == tk
            carry = block(pl.multiple_of(qi * tq + off, tk), off, carry, True)
        dq = _sb_walk_left(nfull, tk, block, carry, lambda c: (c[0][0], c[1][0]))[2]
        dq_ref[...] = dq * QK_SCALE

    qblk = pl.BlockSpec((tq, LANES), lambda p, i: (i, p))
    kvblk = pl.BlockSpec((S, LANES), lambda p, i: (0, p))
    f32out = jax.ShapeDtypeStruct((S, SB_W), F32)
    return pl.pallas_call(
        body, name="sb_bwd", grid=(npair, S // tq),
        in_specs=[qblk, kvblk, kvblk, qblk,
                  pl.BlockSpec((tq, LANES), lambda p, i: (i, mix0 + p)),
                  pl.BlockSpec((tq, LANES), lambda p, i: (i, C_SG // LANES + p))],
        out_specs=[qblk, kvblk, kvblk, qblk],
        out_shape=[f32out, f32out, f32out, f32out],
        compiler_params=_cparams(dimension_semantics=("arbitrary", "arbitrary")),
    )(sq, sk, sv, o, dmix, projm)


def _prep_bwd(projm, ffo, dqn, dkn, dct, dcr, dv, dfg, dsq, dsk, dsv, dsg, dmix, pooled, yp, qg, kg, bfp, wpd, ps, *, ts):
    S = projm.shape[0]
    nb = S // ts
    hb = ts // POOL_HALO
    npair = FOX_HEADS // 2
    last_halo = S // POOL_HALO - 1

    def body(fq_ref, fk_ref, pp_ref, pph_ref, ff_ref,
             dqn_ref, dkn_ref, dct_ref, dcr_ref, dv_ref, dfg_ref, dsq_ref, dsk_ref, dsv_ref, dsg_ref,
             dmp_ref, dmh_ref, pooled_ref, yp_ref, qg_ref, kg_ref, bf_ref, wpd_ref, ps_ref,
             dp_ref, dqg_ref, dkg_ref, dbf_ref, dwp_ref, dps_ref,
             carry_ref, dl_ref, buf_ref, dct_s):
        i = pl.program_id(0)
        blk = nb - 1 - i

        @pl.when(i == 0)
        def _():
            carry_ref[...] = jnp.zeros_like(carry_ref)
            dqg_ref[...] = jnp.zeros_like(dqg_ref)
            dkg_ref[...] = jnp.zeros_like(dkg_ref)
            dbf_ref[...] = jnp.zeros_like(dbf_ref)
            dwp_ref[...] = jnp.zeros_like(dwp_ref)
            dps_ref[...] = jnp.zeros_like(dps_ref)

        bd = _head_blockdiag(FOX_W)
        for raw_ref, g_ref, dn, dg_ref, col in ((fq_ref, qg_ref, dqn_ref[...], dqg_ref, C_FQ), (fk_ref, kg_ref, dkn_ref[...], dkg_ref, C_FK)):
            q = raw_ref[...]
            rstd = lax.rsqrt(_group_sum(q * q, bd) * (1.0 / HEAD_DIM) + EPS)
            xhat = q * rstd
            dg_ref[...] += jnp.sum(dn * xhat, axis=0, keepdims=True)
            dyg = dn * g_ref[...]
            mean = _group_sum(dyg * xhat, bd) * (1.0 / HEAD_DIM)
            dp_ref[:, col:col + FOX_W] = (rstd * (dyg - xhat * mean)).astype(BF16)
        dp_ref[:, C_FV:C_FV + FOX_W] = dv_ref[...].astype(BF16)
        dp_ref[:, C_FG:C_FG + FOX_W] = dfg_ref[...].astype(BF16)
        dp_ref[:, C_SQ:C_SQ + SB_W] = dsq_ref[...].astype(BF16)
        dp_ref[:, C_SK:C_SK + SB_W] = dsk_ref[...].astype(BF16)
        dp_ref[:, C_SV:C_SV + SB_W] = dsv_ref[...].astype(BF16)
        dp_ref[:, C_SG:C_SG + SB_W] = dsg_ref[...].astype(BF16)

        dct_s[...] = jnp.zeros_like(dct_s)
        for p in range(npair):
            dct_s[FF_STRIDE * p:FF_STRIDE * (p + 1), :] = dct_ref[p]
        dc = dct_s[...].T
        lane = _iota((1, LANES), 1)
        for p in range(npair):
            dcr = dcr_ref[p]
            dc = dc + jnp.where(lane == FF_STRIDE * p, _lane_pick(dcr, lane, 0), 0.0)
            dc = dc + jnp.where(lane == FF_STRIDE * p + 1, _lane_pick(dcr, lane, HEAD_DIM), 0.0)
        triu = _ones_where(_iota((ts, ts), 1) >= _iota((ts, ts), 0))
        dlf = _dot_exact_lhs(triu, dc) + carry_ref[...]
        dl_ref[...] = dlf
        carry_ref[...] = dl_ref[0:1, :]
        z = ff_ref[...] + bf_ref[...]
        dff = dlf * (1.0 / (1.0 + jnp.exp(z)))
        dbf_ref[...] += jnp.sum(dff, axis=0, keepdims=True)
        dp_ref[:, PM:PW] = dff.astype(BF16)

        psv = ps_ref[...]
        wpdv = wpd_ref[...]
        lane_group = _iota((1, POOL_W), 1) >> 6
        wlen = _pool_group_select(lane_group, [float(w) for w in POOL_WINDOWS])
        pg = pp_ref[:, POOL_W:2 * POOL_W]
        silu, dsilu = _silu_pair(pg)
        dmp = dmp_ref[...]
        ypv = yp_ref[...]
        dp_ref[:, C_PG:C_PG + POOL_W] = (dmp * (ypv * psv) * dsilu).astype(BF16)
        dps_ref[...] += jnp.sum(dmp * silu * ypv, axis=0, keepdims=True)
        dyp = (dmp * psv * silu).astype(BF16)
        dwp_ref[...] += _dot_tn(pooled_ref[...], dyp)
        dpooled = _dot_nt(dyp, wpdv)
        pgh = pph_ref[:, POOL_W:2 * POOL_W]
        dyph = (dmh_ref[...] * psv * (pgh * _sigmoid(pgh))).astype(BF16)
        dpooled_h = jnp.where(blk < nb - 1, _dot_nt(dyph, wpdv), 0.0)
        tpos = (blk * ts + _iota((ts, 1), 0) + 1).astype(F32)
        ev = dpooled / jnp.minimum(tpos, wlen)
        buf_ref[0:ts, :] = ev
        buf_ref[ts:ts + POOL_HALO, :] = dpooled_h / wlen
        acc = ev
        snaps = []
        for d in range(1, POOL_HALO):
            acc = acc + buf_ref[pl.ds(d, ts), :]
            if d + 1 in POOL_WINDOWS:
                snaps.append(acc)
        dp_ref[:, C_PX:C_PX + POOL_W] = (_pool_group_select(lane_group, snaps) - dpooled).astype(BF16)

    rblk = lambda w, c: pl.BlockSpec((ts, w), lambda i: (nb - 1 - i, c))
    full = lambda a: pl.BlockSpec(a.shape, lambda i: (0,) * a.ndim)
    halo = lambda w, c: pl.BlockSpec((POOL_HALO, w), lambda i: (jnp.minimum((nb - i) * hb, last_halo), c))
    acc_spec = lambda r, w: pl.BlockSpec((r, w), lambda i: (0, 0))
    return pl.pallas_call(
        body, name="prep_bwd", grid=(nb,),
        in_specs=[rblk(FOX_W, C_FQ // FOX_W), rblk(FOX_W, C_FK // FOX_W), rblk(2 * POOL_W, C_PX // (2 * POOL_W)),
                  halo(2 * POOL_W, C_PX // (2 * POOL_W)), rblk(LANES, 0),
                  rblk(FOX_W, 0), rblk(FOX_W, 0), pl.BlockSpec((npair, FF_STRIDE, ts), lambda i: (0, 0, nb - 1 - i)),
                  pl.BlockSpec((npair, ts, LANES), lambda i: (0, nb - 1 - i, 0)), rblk(FOX_W, 0), rblk(FOX_W, 0),
                  rblk(SB_W, 0), rblk(SB_W, 0), rblk(SB_W, 0), rblk(SB_W, 0),
                  rblk(POOL_W, FOX_W // POOL_W), halo(POOL_W, FOX_W // POOL_W), rblk(POOL_W, 0), rblk(POOL_W, 0),
                  full(qg), full(kg), full(bfp), full(wpd), full(ps)],
        out_specs=[rblk(PW, 0), acc_spec(1, FOX_W), acc_spec(1, FOX_W), acc_spec(1, LANES), acc_spec(POOL_W, POOL_W), acc_spec(1, POOL_W)],
        out_shape=[jax.ShapeDtypeStruct((S, PW), BF16), jax.ShapeDtypeStruct((1, FOX_W), F32), jax.ShapeDtypeStruct((1, FOX_W), F32),
                   jax.ShapeDtypeStruct((1, LANES), F32), jax.ShapeDtypeStruct((POOL_W, POOL_W), F32), jax.ShapeDtypeStruct((1, POOL_W), F32)],
        scratch_shapes=[pltpu.VMEM((1, LANES), F32), pltpu.VMEM((ts, LANES), F32), pltpu.VMEM((ts + POOL_HALO, POOL_W), F32),
                        pltpu.VMEM((LANES, ts), F32)],
        compiler_params=_cparams(dimension_semantics=("arbitrary",)),
    )(projm, projm, projm, projm, ffo, dqn, dkn, dct, dcr, dv, dfg, dsq, dsk, dsv, dsg, dmix, dmix, pooled, yp, qg, kg, bfp, wpd, ps)


def _stack_call(body, name, grid, in_specs, operands, slot_specs, slot_shapes, stacks, plain_specs=(), plain_shapes=(), **kw):
    out_specs = list(plain_specs) + list(slot_specs)
    out_shape = list(plain_shapes) + [jax.ShapeDtypeStruct((DEPTH,) + s, F32) for s in slot_shapes]
    if stacks is None:
        return pl.pallas_call(body, name=name, grid=grid, in_specs=in_specs, out_specs=out_specs, out_shape=out_shape, **kw)(*operands)
    n = len(operands)

    def aliased_body(*refs):
        body(*refs[:n], *refs[n + len(stacks):])

    return pl.pallas_call(
        aliased_body, name=name, grid=grid, in_specs=list(in_specs) + [pl.BlockSpec(memory_space=pl.ANY)] * len(stacks),
        out_specs=out_specs, out_shape=out_shape,
        input_output_aliases={n + k: len(plain_specs) + k for k in range(len(stacks))}, **kw)(*operands, *stacks)


def _inproj_dw(h, dproj, layer, stacks, *, ts, tn):
    S, D = h.shape
    nj = PM // tn

    def body(h_ref, dp_ref, dpf_ref, dw_ref, dwf_ref):
        s = pl.program_id(1)

        @pl.when(s == 0)
        def _():
            dw_ref[...] = jnp.zeros_like(dw_ref)

        @pl.when((s == 0) & (pl.program_id(0) == 0))
        def _():
            dwf_ref[...] = jnp.zeros_like(dwf_ref)

        hv = h_ref[...]
        dw_ref[...] += _dot_tn(dp_ref[...], hv)

        @pl.when(pl.program_id(0) == 0)
        def _():
            dwf_ref[...] += _dot_tn(dpf_ref[...], hv)

    return _stack_call(
        body, "inproj_dw", (nj, S // ts),
        [pl.BlockSpec((ts, D), lambda j, s: (s, 0)),
         pl.BlockSpec((ts, tn), lambda j, s: (s, j)),
         pl.BlockSpec((ts, LANES), lambda j, s: (s, PM // LANES))],
        (h, dproj, dproj),
        [pl.BlockSpec((None, tn, D), lambda j, s: (layer, j, 0)), pl.BlockSpec((None, LANES, D), lambda j, s: (layer, 0, 0))],
        [(PM, D), (LANES, D)], stacks,
        compiler_params=_cparams(dimension_semantics=("arbitrary", "arbitrary")))


def _inproj_dx(dproj, wt_all, layer, x, g, dy, *, tm):
    S, D = x.shape

    def body(dp_ref, w_ref, x_ref, g_ref, dy_ref, dx_ref, dg_ref):
        @pl.when(pl.program_id(0) == 0)
        def _():
            dg_ref[...] = jnp.zeros_like(dg_ref)

        dh = _dot(dp_ref[...], w_ref[...])
        xf = x_ref[...]
        rstd = lax.rsqrt(jnp.mean(xf * xf, axis=-1, keepdims=True) + EPS)
        xhat = xf * rstd
        dg_ref[...] += jnp.sum(dh * xhat, axis=0, keepdims=True)
        dyg = dh * g_ref[...]
        mean = jnp.mean(dyg * xhat, axis=-1, keepdims=True)
        dx_ref[...] = rstd * (dyg - xhat * mean) + dy_ref[...]

    row = lambda w: pl.BlockSpec((tm, w), lambda i: (i, 0))
    return pl.pallas_call(
        body, name="inproj_dx", grid=(S // tm,),
        in_specs=[row(PW), pl.BlockSpec((None, PW, D), lambda i: (layer, 0, 0)), row(D), pl.BlockSpec((1, D), lambda i: (0, 0)), row(D)],
        out_specs=[row(D), pl.BlockSpec((1, D), lambda i: (0, 0))],
        out_shape=[jax.ShapeDtypeStruct((S, D), F32), jax.ShapeDtypeStruct((1, D), F32)],
        compiler_params=_cparams(dimension_semantics=("arbitrary",)),
    )(dproj, wt_all, x, g, dy)


def _adam_update(w, g, m, v):
    nm = ADAM_B1 * m + (1.0 - ADAM_B1) * g
    nv = ADAM_B2 * v + (1.0 - ADAM_B2) * (g * g)
    m_hat = nm / (1.0 - ADAM_B1 ** ADAM_STEP)
    v_hat = nv / (1.0 - ADAM_B2 ** ADAM_STEP)
    return -ADAM_LR * (m_hat / (jnp.sqrt(v_hat) + ADAM_EPS) + ADAM_WD * w), nm, nv


def _adamw(w, g, m, v):
    L, R, C = w.shape
    tr = R if R <= 512 else 256

    def body(w_ref, g_ref, m_ref, v_ref, d_ref, nm_ref, nv_ref):
        d_ref[...], nm_ref[...], nv_ref[...] = _adam_update(w_ref[...], g_ref[...], m_ref[...], v_ref[...])

    spec = pl.BlockSpec((1, tr, C), lambda l, i: (l, i, 0))
    shp = jax.ShapeDtypeStruct((L, R, C), F32)
    return pl.pallas_call(
        body, name="adamw", grid=(L, R // tr), in_specs=[spec] * 4, out_specs=[spec] * 3, out_shape=[shp] * 3,
        compiler_params=_cparams(dimension_semantics=("arbitrary", "arbitrary")),
    )(w, g, m, v)


def _adamw_nd(w, g, m, v):
    shape = w.shape
    view = (1,) + shape if w.ndim == 2 else (shape[0], -1, shape[-1])
    outs = _adamw(w.reshape(view), g.reshape(view), m.reshape(view), v.reshape(view))
    return tuple(o.reshape(shape) for o in outs)


FLIP_C = (0, 0, 1)
FLIP_X = (1, 0, 0)
FLIP_Y = (0, 1, 0)
FLIP_XY = (1, 1, 0)
MESH = pl.DeviceIdType.MESH


def _peer(flip):
    me = (lax.axis_index("x"), lax.axis_index("y"), lax.axis_index("c"))
    return tuple(1 - a if f else a for a, f in zip(me, flip))


def _exchange(name, arrays, flips):
    n = len(arrays)

    def body(*refs):
        srcs, dsts = refs[:n], refs[n:2 * n]
        send_sems, recv_sems = refs[2 * n:]
        copies = [pltpu.make_async_remote_copy(src_ref=srcs[k], dst_ref=dsts[k], send_sem=send_sems.at[k], recv_sem=recv_sems.at[k],
                                               device_id=_peer(flips[k]), device_id_type=MESH) for k in range(n)]
        for cp in copies:
            cp.start()
        for cp in copies:
            cp.wait()

    anyspec = pl.BlockSpec(memory_space=pl.ANY)
    return pl.pallas_call(
        body, name=name, in_specs=[anyspec] * n, out_specs=[anyspec] * n,
        out_shape=[jax.ShapeDtypeStruct(a.shape, a.dtype) for a in arrays],
        scratch_shapes=[pltpu.SemaphoreType.DMA((n,)), pltpu.SemaphoreType.DMA((n,))],
    )(*arrays)


def _exchange_add(name, x, flip):
    def body(x_ref, o_ref, buf_ref, send_sem, recv_sem):
        cp = pltpu.make_async_remote_copy(src_ref=x_ref, dst_ref=buf_ref, send_sem=send_sem, recv_sem=recv_sem,
                                          device_id=_peer(flip), device_id_type=MESH)
        cp.start()
        cp.wait()
        o_ref[...] = x_ref[...] + buf_ref[...]

    vspec = pl.BlockSpec(memory_space=pltpu.VMEM)
    return pl.pallas_call(
        body, name=name, in_specs=[vspec], out_specs=vspec, out_shape=jax.ShapeDtypeStruct(x.shape, x.dtype),
        scratch_shapes=[pltpu.VMEM(x.shape, x.dtype), pltpu.SemaphoreType.DMA, pltpu.SemaphoreType.DMA],
    )(x)


def _chip_index():
    return 2 * lax.axis_index("x") + lax.axis_index("y")


def _gather_weights(w_in_t, w_out):
    wi = w_in_t.astype(BF16)
    wo = jnp.swapaxes(w_out, 0, 1).astype(BF16)
    halves = (wi.shape[0] // 2, wo.shape[0] // 2)
    masks = (2, 1, 3)
    flips = (FLIP_X, FLIP_Y, FLIP_XY)
    n_first = 2 * len(masks)

    def body(wi_ref, wo_ref, gi_ref, go_ref, send_sems, recv_sems):
        c = lax.axis_index("c")
        j = _chip_index()
        srcs = (wi_ref, wo_ref)
        dsts = (gi_ref, go_ref)
        mine = [pl.ds(h * c, h) for h in halves]
        theirs = [pl.ds(h * (1 - c), h) for h in halves]

        def copy(idx, src, dst, flip):
            return pltpu.make_async_remote_copy(src_ref=src, dst_ref=dst, send_sem=send_sems.at[idx], recv_sem=recv_sems.at[idx],
                                                device_id=_peer(flip), device_id_type=MESH)

        first = [copy(2 * k + a, srcs[a].at[mine[a]], dsts[a].at[j, mine[a]], flips[k]) for k in range(len(masks)) for a in range(2)]
        for cp in first:
            cp.start()
        passed = []
        for k, m in enumerate(masks):
            for a in range(2):
                slot = dsts[a].at[j ^ m, mine[a]]
                copy(2 * k + a, slot, slot, flips[k]).wait_recv()
                fwd = copy(n_first + 2 * k + a, slot, slot, FLIP_C)
                fwd.start()
                passed.append(fwd)
        for k, m in enumerate(masks):
            for a in range(2):
                slot = dsts[a].at[j ^ m, theirs[a]]
                copy(n_first + 2 * k + a, slot, slot, FLIP_C).wait_recv()
        for cp in first + passed:
            cp.wait_send()

    anyspec = pl.BlockSpec(memory_space=pl.ANY)
    gi, go = pl.pallas_call(
        body, name="gather_weights", in_specs=[anyspec] * 2, out_specs=[anyspec] * 2,
        out_shape=[jax.ShapeDtypeStruct((4,) + wi.shape, BF16), jax.ShapeDtypeStruct((4,) + wo.shape, BF16)],
        scratch_shapes=[pltpu.SemaphoreType.DMA((2 * n_first,)), pltpu.SemaphoreType.DMA((2 * n_first,))],
    )(wi, wo)
    own = lax.broadcasted_iota(jnp.int32, (4, 1, 1, 1), 0) == _chip_index()
    gi = jnp.where(own, wi[None], gi)
    go = jnp.where(own, wo[None], go)
    w_in_t_full = gi.reshape((4 * wi.shape[0],) + wi.shape[1:])
    w_out_full = jnp.swapaxes(go.reshape((4 * wo.shape[0],) + wo.shape[1:]), 0, 1)
    return w_in_t_full, w_out_full


def _to_aligned(w_t):
    _, L, D = w_t.shape
    npair = FOX_HEADS // 2
    ff = w_t[ORIG_FF:ORIG_REST].reshape(npair, 2, L, D)
    ff = jnp.pad(ff, ((0, 0), (0, FF_STRIDE - 2), (0, 0), (0, 0))).reshape(npair * FF_STRIDE, L, D)
    ff = jnp.pad(ff, ((0, LANES - npair * FF_STRIDE), (0, 0), (0, 0)))
    return jnp.swapaxes(jnp.concatenate([w_t[:ORIG_FOX], w_t[ORIG_REST:], ff], axis=0), 0, 1)


def _from_aligned(dw_t):
    n, _, D = dw_t.shape
    npair = FOX_HEADS // 2
    ff = dw_t[:, PM:PM + npair * FF_STRIDE].reshape(n, npair, FF_STRIDE, D)[:, :, :2].reshape(n, FOX_HEADS, D)
    return jnp.swapaxes(jnp.concatenate([dw_t[:, :ORIG_FOX], ff, dw_t[:, ORIG_FOX:PM]], axis=1), 0, 1)


def _half_layers(name, stack, got):
    L, R, C = stack.shape
    half = L // 2
    tr = min(256, R)
    c = lax.axis_index("c")
    which = ((1 - c) if got is None else c).astype(jnp.int32).reshape(1)

    def body(c_ref, x_ref, *refs):
        if got is None:
            refs[0][...] = x_ref[...].astype(BF16)
        else:
            acc = x_ref[...] + refs[0][...].astype(F32)
            refs[1][...] = acc
            refs[2][...] = acc.astype(BF16)

    plain = pl.BlockSpec((1, tr, C), lambda l, i, c_ref: (l, i, 0))
    picked = pl.BlockSpec((1, tr, C), lambda l, i, c_ref: (c_ref[0] * half + l, i, 0))
    shp = lambda dt: jax.ShapeDtypeStruct((half, R, C), dt)
    grid_spec = pltpu.PrefetchScalarGridSpec(
        num_scalar_prefetch=1, grid=(half, R // tr),
        in_specs=[picked] + ([] if got is None else [plain]), out_specs=[plain] if got is None else [plain, plain])
    return pl.pallas_call(
        body, name=name, grid_spec=grid_spec, out_shape=[shp(BF16)] if got is None else [shp(F32), shp(BF16)],
        compiler_params=_cparams(dimension_semantics=("arbitrary", "arbitrary")),
    )(which, stack, *([] if got is None else [got]))


def _reduce_scatter(stack_m, stack_f, stack_o, shard_cols, shard_rows):
    j = _chip_index()
    half = DEPTH // 2
    stacks = (stack_m, stack_f, stack_o)
    give = [_half_layers("rs_give", s, None)[0] for s in stacks]
    got = _exchange("rs_d2d", give, (FLIP_C,) * len(stacks))
    (m32, mbf), (f32_, fbf), (o32, obf) = [_half_layers("rs_add_chip", s, g) for s, g in zip(stacks, got)]
    d_model = stack_m.shape[2]

    def in_shards(m, f):
        return _from_aligned(jnp.concatenate([m, f], axis=1)).reshape(4, shard_cols, half, d_model)

    def out_shards(o):
        return jnp.moveaxis(o.reshape(half, 4, shard_rows, o.shape[-1]), 1, 0)

    chip = [(in_shards(m32, f32_), in_shards(mbf, fbf)), (out_shards(o32), out_shards(obf))]
    masks = (2, 1, 3)
    flips = (FLIP_X, FLIP_Y, FLIP_XY)
    sends, sflips = [], []
    for _, bf in chip:
        for m, fl in zip(masks, flips):
            sends.append(lax.dynamic_index_in_dim(bf, j ^ m, axis=0, keepdims=False))
            sflips.append(fl)
    got = _exchange("rs_ici", sends, tuple(sflips))
    own_in, own_out = [lax.dynamic_index_in_dim(f32_sum, j, axis=0, keepdims=False) for f32_sum, _ in chip]
    mine_in = _add_rows("rs_add_in", own_in, list(got[0:3]))
    mine_out = _add_into_half("rs_add_out", own_out, list(got[3:6]))
    sib_in, g_out = _share_halves(mine_in, mine_out)
    return (mine_in, sib_in), g_out


def _add_rows(name, first, others):
    n = len(others)

    def body(*refs):
        acc = refs[0][...]
        for r in refs[1:1 + n]:
            acc = acc + r[...].astype(F32)
        refs[1 + n][...] = acc

    grid, spec = _row_lane_blocks(first.shape)
    return pl.pallas_call(
        body, name=name, grid=grid, in_specs=[spec(first.shape[1])] * (1 + n), out_specs=spec(first.shape[1]),
        out_shape=jax.ShapeDtypeStruct(first.shape, F32),
        compiler_params=_cparams(dimension_semantics=("arbitrary", "arbitrary")),
    )(first, *others)


def _row_lane_blocks(shape):
    rows, _, C = shape
    tr = rows // 2 if rows % 2 == 0 and rows > 64 else rows
    return (rows // tr, C // LANES), lambda n_mid: pl.BlockSpec((tr, n_mid, LANES), lambda i, k, *_: (i, 0, k))


def _add_into_half(name, first, others):
    half, rows, C = first.shape
    tr = min(256, rows)
    n = len(others)

    def body(c_ref, *refs):
        acc = refs[0][...]
        for r in refs[1:1 + n]:
            acc = acc + r[...].astype(F32)
        refs[1 + n][...] = acc

    grid_spec = pltpu.PrefetchScalarGridSpec(
        num_scalar_prefetch=1, grid=(half, rows // tr),
        in_specs=[pl.BlockSpec((1, tr, C), lambda l, i, c_ref: (l, i, 0))] * (1 + n),
        out_specs=pl.BlockSpec((1, tr, C), lambda l, i, c_ref: (c_ref[0] * half + l, i, 0)))
    return pl.pallas_call(
        body, name=name, grid_spec=grid_spec, out_shape=jax.ShapeDtypeStruct((2 * half, rows, C), F32),
        compiler_params=_cparams(dimension_semantics=("arbitrary", "arbitrary")),
    )(lax.axis_index("c").astype(jnp.int32).reshape(1), first, *others)


def _share_halves(mine, buf):
    half = DEPTH // 2

    def body(mine_ref, buf_in, sib_ref, buf_ref, send_sems, recv_sems):
        lay = pl.ds(half * lax.axis_index("c"), half)
        copies = [pltpu.make_async_remote_copy(src_ref=src, dst_ref=dst, send_sem=send_sems.at[k], recv_sem=recv_sems.at[k],
                                               device_id=_peer(FLIP_C), device_id_type=MESH)
                  for k, (src, dst) in enumerate(((mine_ref, sib_ref), (buf_ref.at[lay], buf_ref.at[lay])))]
        for cp in copies:
            cp.start()
        for cp in copies:
            cp.wait()

    anyspec = pl.BlockSpec(memory_space=pl.ANY)
    return pl.pallas_call(
        body, name="rs_share", in_specs=[anyspec] * 2, out_specs=[anyspec] * 2,
        out_shape=[jax.ShapeDtypeStruct(mine.shape, mine.dtype), jax.ShapeDtypeStruct(buf.shape, buf.dtype)],
        input_output_aliases={1: 1},
        scratch_shapes=[pltpu.SemaphoreType.DMA((2,)), pltpu.SemaphoreType.DMA((2,))],
    )(mine, buf)


def _adamw_halves(w, g_mine, g_sib, m, v):
    half = g_mine.shape[1]

    def body(c_ref, w_ref, gm_ref, gs_ref, m_ref, v_ref, g_ref, d_ref, nm_ref, nv_ref):
        first = c_ref[0] == 0
        gm, gs = gm_ref[...], gs_ref[...]
        for h, gv in enumerate((jnp.where(first, gm, gs), jnp.where(first, gs, gm))):
            lay = slice(half * h, half * (h + 1))
            g_ref[:, lay, :] = gv
            d_ref[:, lay, :], nm_ref[:, lay, :], nv_ref[:, lay, :] = _adam_update(w_ref[:, lay, :], gv, m_ref[:, lay, :], v_ref[:, lay, :])

    grid, spec = _row_lane_blocks(w.shape)
    full, part = spec(w.shape[1]), spec(half)
    grid_spec = pltpu.PrefetchScalarGridSpec(num_scalar_prefetch=1, grid=grid, in_specs=[full, part, part, full, full], out_specs=[full] * 4)
    return pl.pallas_call(
        body, name="adamw_halves", grid_spec=grid_spec, out_shape=[jax.ShapeDtypeStruct(w.shape, F32)] * 4,
        compiler_params=_cparams(dimension_semantics=("arbitrary", "arbitrary")),
    )(lax.axis_index("c").astype(jnp.int32).reshape(1), w, g_mine, g_sib, m, v)


def _all_reduce_small(x):
    x = _exchange_add("ar_c", x, FLIP_C)
    x = _exchange_add("ar_y", x, FLIP_Y)
    return _exchange_add("ar_x", x, FLIP_X)


def _blocks(S):
    return dict(tm=min(512, S), tm_proj=min(1024, S), ts=min(512, S), tq=min(512, S), tq_big=min(1024, S), tk=min(512, S), tks=min(256, S))


def _pair_pad(vec):
    npair = FOX_HEADS // 2
    v = jnp.pad(vec.reshape(npair, 2), ((0, 0), (0, FF_STRIDE - 2))).reshape(1, npair * FF_STRIDE)
    return jnp.pad(v, ((0, 0), (0, LANES - npair * FF_STRIDE)))


def _pair_unpad(row):
    npair = FOX_HEADS // 2
    return row[0, :npair * FF_STRIDE].reshape(npair, FF_STRIDE)[:, :2].reshape(FOX_HEADS)


def _pool_blockdiag(w_pool):
    g, cg, _ = w_pool.shape
    eye = jnp.eye(g, dtype=w_pool.dtype)
    return jnp.einsum("gh,gcd->gchd", eye, w_pool).reshape(g * cg, g * cg)


QK_BOUND_SLACK = 1.05


def _layer_params(norm_g, b_f, q_norm_g, k_norm_g, w_pool, pool_scale):
    qk_bound = QK_BOUND_SLACK * HEAD_DIM * QK_SCALE * jnp.max(jnp.abs(q_norm_g)) * jnp.max(jnp.abs(k_norm_g))
    return dict(g=norm_g.reshape(1, -1), qg=jnp.tile(q_norm_g, FOX_HEADS).reshape(1, FOX_W), kg=jnp.tile(k_norm_g, FOX_HEADS).reshape(1, FOX_W),
                bfp=_pair_pad(b_f), wpd=_pool_blockdiag(w_pool).astype(BF16), ps=pool_scale.reshape(1, POOL_W),
                qkb=jnp.full((1, LANES), qk_bound, F32))


def _layer_fwd(x, wt_all, w_out, layer, prm, bs):
    projm, ffo, h = _inproj(x, prm["g"], wt_all, layer, tm=bs["tm_proj"], tn=PROJ_TN)
    qn, ka, kb, v, sq, sk, sv, pooled, yp, pm = _prep(projm, ffo, prm["qg"], prm["kg"], prm["bfp"], prm["wpd"], prm["ps"], ts=bs["ts"])
    o, lse, fm = _fox_fwd(qn, ka, kb, v, projm, prm["qkb"], tq=bs["tq"], tk=bs["tk"])
    so, sm = _sb_fwd(sq, sk, sv, projm, tq=bs["tq_big"], tk=bs["tks"])
    y = _outproj(x, fm, pm, sm, w_out, layer, tm=bs["tm"])
    saved = dict(x=x, projm=projm, ffo=ffo, h=h, qn=qn, ka=ka, kb=kb, v=v, sq=sq, sk=sk, sv=sv, pooled=pooled, yp=yp,
                 o=o, lse=lse, so=so, fm=fm, pm=pm, sm=sm)
    return y, saved


def _layer_bwd(dy, wt_all, w_out, prm, sv_, bs, layer, stacks):
    dmix, stack_o = _outproj_bwd(dy, sv_["fm"], sv_["pm"], sv_["sm"], w_out, layer, None if stacks is None else stacks[2:], tm=bs["tm"])
    dqn, dkn, dv, dfg, dct, dcr = _fox_bwd(sv_["qn"], sv_["ka"], sv_["kb"], sv_["v"], sv_["o"], sv_["lse"], dmix, sv_["projm"],
                                      prm["qkb"], tq=bs["tq_big"], tk=bs["tk"])
    dsq, dsk, dsv, dsg = _sb_bwd(sv_["sq"], sv_["sk"], sv_["sv"], sv_["so"], dmix, sv_["projm"], tq=bs["tq"], tk=bs["tks"])
    dproj, dqg, dkg, dbf, dwp, dps = _prep_bwd(sv_["projm"], sv_["ffo"], dqn, dkn, dct, dcr, dv, dfg, dsq, dsk, dsv, dsg, dmix,
                                               sv_["pooled"], sv_["yp"], prm["qg"], prm["kg"], prm["bfp"], prm["wpd"], prm["ps"], ts=bs["ts"])
    stack_m, stack_f = _inproj_dw(sv_["h"], dproj, layer, None if stacks is None else stacks[:2], ts=bs["tm_proj"], tn=PROJ_TN)
    dx, dg = _inproj_dx(dproj, wt_all, layer, sv_["x"], prm["g"], dy, tm=min(256, bs["tm"]))
    grads = dict(
        norm_g=dg[0],
        b_f=_pair_unpad(dbf), q_norm_g=dqg.reshape(FOX_HEADS, HEAD_DIM).sum(0), k_norm_g=dkg.reshape(FOX_HEADS, HEAD_DIM).sum(0),
        w_pool=jnp.stack([dwp[HEAD_DIM * g:HEAD_DIM * (g + 1), HEAD_DIM * g:HEAD_DIM * (g + 1)] for g in range(4)]),
        pool_scale=dps[0])
    return dx, grads, (stack_m, stack_f, stack_o)


def _local_step(x, target, wt_all, w_out, norm_g, b_f, q_norm_g, k_norm_g, w_pool, pool_scale):
    S, D = x.shape
    bs = _blocks(S)
    prms = [_layer_params(norm_g[l], b_f[l], q_norm_g[l], k_norm_g[l], w_pool[l], pool_scale[l]) for l in range(DEPTH)]
    saved = []
    y = x
    for l in range(DEPTH):
        y, s_ = _layer_fwd(y, wt_all, w_out, l, prms[l], bs)
        saved.append(s_)
    dy, sq = _loss_head(y, target, tm=bs["tm"])
    loss = 0.5 * jnp.sum(sq) / D
    grads = [None] * DEPTH
    stacks = None
    for l in reversed(range(DEPTH)):
        dy, grads[l], stacks = _layer_bwd(dy, wt_all, w_out, prms[l], saved[l], bs, l, stacks)
    stacked = {k: jnp.stack([g[k] for g in grads]) for k in grads[0]}
    return loss, dy, stacked, stacks


SMALL = ("norm_g", "b_f", "q_norm_g", "k_norm_g", "w_pool", "pool_scale")


def _pack_small(gr):
    flat = jnp.concatenate([gr[k].reshape(-1) for k in SMALL])
    pad = (-flat.shape[0]) % (8 * LANES)
    return jnp.pad(flat, (0, pad)).reshape(-1, LANES)


def _unpack_small(packed, like):
    flat = packed.reshape(-1)
    out, off = {}, 0
    for k in SMALL:
        n = like[k].size
        out[k] = flat[off:off + n].reshape(like[k].shape)
        off += n
    return out


def kernel(x, norm_g, w_in, b_f, q_norm_g, k_norm_g, w_pool, pool_scale, w_out, loss_target, m_norm_g, m_w_in, m_b_f, m_q_norm_g, m_k_norm_g, m_w_pool, m_pool_scale, m_w_out, v_norm_g, v_w_in, v_b_f, v_q_norm_g, v_k_norm_g, v_w_pool, v_pool_scale, v_w_out):
    weights = dict(norm_g=norm_g, w_in=w_in, b_f=b_f, q_norm_g=q_norm_g, k_norm_g=k_norm_g, w_pool=w_pool, pool_scale=pool_scale, w_out=w_out)
    mom_m = dict(norm_g=m_norm_g, w_in=m_w_in, b_f=m_b_f, q_norm_g=m_q_norm_g, k_norm_g=m_k_norm_g, w_pool=m_w_pool, pool_scale=m_pool_scale, w_out=m_w_out)
    mom_v = dict(norm_g=v_norm_g, w_in=v_w_in, b_f=v_b_f, q_norm_g=v_q_norm_g, k_norm_g=v_k_norm_g, w_pool=v_w_pool, pool_scale=v_pool_scale, w_out=v_w_out)
    shard_cols = w_in.shape[2]
    shard_rows = w_out.shape[1]

    cols_first = lambda a: jnp.transpose(a, (2, 0, 1))
    w_in_t = cols_first(w_in)
    w_in_t_full, w_out_full = _gather_weights(w_in_t, w_out)
    wt_all = _to_aligned(w_in_t_full)
    loss, dx, gr, stacks = _local_step(x[0], loss_target[0], wt_all, w_out_full, norm_g, b_f, q_norm_g, k_norm_g, w_pool, pool_scale)
    loss = lax.psum(loss, ("x", "y", "c"))

    (g_in_mine, g_in_sib), g_w_out = _reduce_scatter(*stacks, shard_cols, shard_rows)
    small = _unpack_small(_all_reduce_small(_pack_small(gr)), {k: weights[k] for k in SMALL})
    grad_w = dict(small, w_out=g_w_out)

    names = ("norm_g", "w_in", "b_f", "q_norm_g", "k_norm_g", "w_pool", "pool_scale", "w_out")
    upd = {k: _adamw_nd(weights[k], grad_w[k], mom_m[k], mom_v[k]) for k in names if k != "w_in"}
    in_t = _adamw_halves(w_in_t, g_in_mine, g_in_sib, cols_first(mom_m["w_in"]), cols_first(mom_v["w_in"]))
    grad_w["w_in"], *upd["w_in"] = [jnp.transpose(a, (1, 2, 0)) for a in in_t]
    return (loss, dx[None], *[grad_w[k] for k in names], *[upd[k][0] for k in names], *[upd[k][1] for k in names], *[upd[k][2] for k in names])
```

```python
import functools

import jax
import jax.numpy as jnp
from jax import lax
from jax.experimental import pallas as pl
from jax.experimental.pallas import tpu as pltpu

F32 = jnp.float32
BF16 = jnp.bfloat16

DEPTH = 4
HEAD_DIM = 64
FOX_HEADS = 8
SB_HEADS = 4
FOX_W = FOX_HEADS * HEAD_DIM
SB_W = SB_HEADS * HEAD_DIM
POOL_W = 256
POOL_WINDOWS = (2, 4, 8, 16)
POOL_HALO = 16
D_MIX = FOX_W + POOL_W + SB_W
EPS = 1e-6
NEG = -1e30
QK_SCALE = HEAD_DIM ** -0.5

ORIG_FOX = 4 * FOX_W
ORIG_FF = ORIG_FOX
ORIG_REST = ORIG_FF + FOX_HEADS
D_IN = ORIG_REST + 2 * POOL_W + 4 * SB_W

C_FQ, C_FK, C_FV, C_FG = 0, FOX_W, 2 * FOX_W, 3 * FOX_W
C_PX = 4 * FOX_W
C_PG = C_PX + POOL_W
C_SQ = C_PG + POOL_W
C_SK, C_SV, C_SG = C_SQ + SB_W, C_SQ + 2 * SB_W, C_SQ + 3 * SB_W
PM = C_SG + SB_W
LANES = 128
PW = PM + LANES
FF_STRIDE = 8
AUG = 3

ADAM_LR = 0.001
ADAM_B1 = 0.9
ADAM_B2 = 0.999
ADAM_EPS = 1e-08
ADAM_WD = 0.01
ADAM_STEP = 10

VMEM_LIMIT = 48 * 1024 * 1024
PROJ_TN = PM // 2


def _cparams(**kw):
    return pltpu.CompilerParams(vmem_limit_bytes=VMEM_LIMIT, **kw)


def _dot(a, b):
    return jnp.dot(a, b, preferred_element_type=F32)


def _dot_nt(a, b):
    return lax.dot_general(a, b, (((1,), (1,)), ((), ())), preferred_element_type=F32)


def _dot_tn(a, b):
    return lax.dot_general(a, b, (((0,), (0,)), ((), ())), preferred_element_type=F32)


def _split2(x):
    hi = x.astype(BF16)
    lo = (x - hi.astype(F32)).astype(BF16)
    return hi, lo


def _split3(x):
    hi = x.astype(BF16)
    r = x - hi.astype(F32)
    mid = r.astype(BF16)
    lo = (r - mid.astype(F32)).astype(BF16)
    return hi, mid, lo


def _dot_exact_rhs(x, m):
    hi, mid, lo = _split3(x)
    return _dot(hi, m) + _dot(mid, m) + _dot(lo, m)


def _dot_exact_lhs(m, x):
    hi, mid, lo = _split3(x)
    return _dot(m, hi) + _dot(m, mid) + _dot(m, lo)


def _sigmoid(x):
    return 1.0 / (1.0 + jnp.exp(-x))


def _silu_pair(x):
    s = _sigmoid(x)
    return x * s, s * (1.0 + x * (1.0 - s))


def _iota(shape, dim):
    return lax.broadcasted_iota(jnp.int32, shape, dim)


def _ones_where(cond):
    return jnp.where(cond, 1.0, 0.0).astype(BF16)


def _head_blockdiag(w):
    return _ones_where((_iota((w, w), 0) >> 6) == (_iota((w, w), 1) >> 6))


def _group_sum(x, bd):
    hi, lo = _split2(x)
    return _dot(hi, bd) + _dot(lo, bd)


def _lane_pick(x, lane_idx, lane):
    return jnp.sum(jnp.where(lane_idx == lane, x, 0.0), axis=1, keepdims=True)


def _inproj(x, g, wt_all, layer, *, tm, tn):
    S, D = x.shape
    nj = PM // tn

    def body(x_ref, g_ref, w_ref, wff_ref, proj_ref, ff_ref, h_ref):
        @pl.when(pl.program_id(1) == 0)
        def _():
            xf = x_ref[...]
            ms = jnp.mean(xf * xf, axis=-1, keepdims=True)
            h = (xf * lax.rsqrt(ms + EPS) * g_ref[...]).astype(BF16)
            h_ref[...] = h
            ff_ref[...] = _dot_nt(h, wff_ref[...])

        proj_ref[...] = _dot_nt(h_ref[...], w_ref[...])

    return pl.pallas_call(
        body, name="inproj", grid=(S // tm, nj),
        in_specs=[pl.BlockSpec((tm, D), lambda i, j: (i, 0)),
                  pl.BlockSpec((1, D), lambda i, j: (0, 0)),
                  pl.BlockSpec((None, tn, D), lambda i, j: (layer, j, 0)),
                  pl.BlockSpec((None, LANES, D), lambda i, j: (layer, PM // LANES, 0))],
        out_specs=[pl.BlockSpec((tm, tn), lambda i, j: (i, j)),
                   pl.BlockSpec((tm, LANES), lambda i, j: (i, 0)),
                   pl.BlockSpec((tm, D), lambda i, j: (i, 0))],
        out_shape=[jax.ShapeDtypeStruct((S, PM), F32), jax.ShapeDtypeStruct((S, LANES), F32),
                   jax.ShapeDtypeStruct((S, D), BF16)],
        compiler_params=_cparams(dimension_semantics=("arbitrary", "arbitrary")),
    )(x, g, wt_all, wt_all)


def _pool_group_select(lane_group, vals):
    return jnp.where(lane_group == 0, vals[0], jnp.where(lane_group == 1, vals[1], jnp.where(lane_group == 2, vals[2], vals[3])))


def _prep(projm, ffo, qg, kg, bfp, wpd, ps, *, ts):
    S = projm.shape[0]
    nb = S // ts
    hb = ts // POOL_HALO

    def body(fq_ref, fk_ref, fv_ref, pp_ref, halo_ref, ff_ref, sq_ref, sk_ref, sv_ref,
             qg_ref, kg_ref, bf_ref, wpd_ref, ps_ref,
             qn_ref, ka_ref, kb_ref, v_ref, sqo_ref, sko_ref, svo_ref, pooled_ref, yp_ref, pm_ref,
             carry_ref, c_ref, buf_ref):
        i = pl.program_id(0)
        bd = _head_blockdiag(FOX_W)
        normed = []
        for src, g_ref in ((fq_ref, qg_ref), (fk_ref, kg_ref)):
            q = src[...]
            ss = _group_sum(q * q, bd)
            normed.append(q * lax.rsqrt(ss * (1.0 / HEAD_DIM) + EPS) * g_ref[...])
        qn_ref[...] = (normed[0] * QK_SCALE).astype(BF16)
        kn = normed[1]
        v_ref[...] = fv_ref[...].astype(BF16)
        sqo_ref[...] = (sq_ref[...] * QK_SCALE).astype(BF16)
        sko_ref[...] = sk_ref[...].astype(BF16)
        svo_ref[...] = sv_ref[...].astype(BF16)

        @pl.when(i == 0)
        def _():
            carry_ref[...] = jnp.zeros_like(carry_ref)

        z = ff_ref[...] + bf_ref[...]
        lf = jnp.minimum(z, 0.0) - jnp.log(1.0 + jnp.exp(-jnp.abs(z)))
        tri = _ones_where(_iota((ts, ts), 1) <= _iota((ts, ts), 0))
        c = _dot_exact_lhs(tri, lf) + carry_ref[...]
        c_ref[...] = c
        carry_ref[...] = c_ref[ts - 1:ts, :]
        parts = jnp.concatenate(_split3(-c), axis=1)
        row = _iota((AUG * LANES, FOX_W), 0)
        col = _iota((AUG * LANES, FOX_W), 1)
        part, src = row >> 7, row & (LANES - 1)
        pair, off = col >> 7, col & (LANES - 1)
        sel_a = _ones_where((src == FF_STRIDE * pair) & (off == HEAD_DIM + part))
        sel_b = _ones_where((src == FF_STRIDE * pair + 1) & (off == part))
        first_half = (_iota((1, FOX_W), 1) & HEAD_DIM) == 0
        ka_ref[...] = jnp.where(first_half, kn, _dot(parts, sel_a)).astype(BF16)
        kb_ref[...] = jnp.where(first_half, _dot(parts, sel_b), kn).astype(BF16)

        x = pp_ref[:, 0:POOL_W]
        pg = pp_ref[:, POOL_W:2 * POOL_W]
        halo = jnp.where(i > 0, halo_ref[:, 0:POOL_W], 0.0)
        buf_ref[0:POOL_HALO, :] = halo
        buf_ref[POOL_HALO:POOL_HALO + ts, :] = x
        acc = x
        snaps = []
        for d in range(1, POOL_HALO):
            acc = acc + buf_ref[pl.ds(POOL_HALO - d, ts), :]
            if d + 1 in POOL_WINDOWS:
                snaps.append(acc)
        lane_group = _iota((1, POOL_W), 1) >> 6
        wsum = _pool_group_select(lane_group, snaps)
        wlen = _pool_group_select(lane_group, [float(w) for w in POOL_WINDOWS])
        tpos = (i * ts + _iota((ts, 1), 0) + 1).astype(F32)
        pooled = wsum / jnp.minimum(tpos, wlen) - x
        pb = pooled.astype(BF16)
        pooled_ref[...] = pb
        yp = _dot(pb, wpd_ref[...])
        yp_ref[...] = yp
        pm_ref[...] = (yp * ps_ref[...] * (pg * _sigmoid(pg))).astype(BF16)

    blk = lambda w, c: pl.BlockSpec((ts, w), lambda i: (i, c))
    full = lambda a: pl.BlockSpec(a.shape, lambda i: (0,) * a.ndim)
    out_shapes = [
        jax.ShapeDtypeStruct((S, FOX_W), BF16), jax.ShapeDtypeStruct((S, FOX_W), BF16), jax.ShapeDtypeStruct((S, FOX_W), BF16),
        jax.ShapeDtypeStruct((S, FOX_W), BF16),
        jax.ShapeDtypeStruct((S, SB_W), BF16), jax.ShapeDtypeStruct((S, SB_W), BF16), jax.ShapeDtypeStruct((S, SB_W), BF16),
        jax.ShapeDtypeStruct((S, POOL_W), BF16), jax.ShapeDtypeStruct((S, POOL_W), F32), jax.ShapeDtypeStruct((S, POOL_W), BF16),
    ]
    out_specs = [
        blk(FOX_W, 0), blk(FOX_W, 0), blk(FOX_W, 0), blk(FOX_W, 0),
        blk(SB_W, 0), blk(SB_W, 0), blk(SB_W, 0),
        blk(POOL_W, 0), blk(POOL_W, 0), blk(POOL_W, 0),
    ]
    return pl.pallas_call(
        body, name="prep", grid=(nb,),
        in_specs=[blk(FOX_W, C_FQ // FOX_W), blk(FOX_W, C_FK // FOX_W), blk(FOX_W, C_FV // FOX_W), blk(2 * POOL_W, C_PX // (2 * POOL_W)),
                  pl.BlockSpec((POOL_HALO, 2 * POOL_W), lambda i: (jnp.maximum(i * hb - 1, 0), C_PX // (2 * POOL_W))),
                  blk(LANES, 0),
                  blk(SB_W, C_SQ // SB_W), blk(SB_W, C_SK // SB_W), blk(SB_W, C_SV // SB_W),
                  full(qg), full(kg), full(bfp), full(wpd), full(ps)],
        out_specs=out_specs, out_shape=out_shapes,
        scratch_shapes=[pltpu.VMEM((1, LANES), F32), pltpu.VMEM((ts, LANES), F32), pltpu.VMEM((ts + POOL_HALO, POOL_W), F32)],
        compiler_params=_cparams(dimension_semantics=("arbitrary",)),
    )(projm, projm, projm, projm, projm, ffo, projm, projm, projm, qg, kg, bfp, wpd, ps)


def _pair_masks(x):
    ma = _iota((1, LANES), 1) < HEAD_DIM
    zero = jnp.zeros_like(x)
    return jnp.where(ma, x, zero), jnp.where(ma, zero, x)


DIAG_TILE = 256


def _diag_tiles(tq, size=DIAG_TILE):
    size = min(tq, size)
    return [(t * size, size) for t in range(tq // size)]


def _put_rows(old, new, r0):
    return new if r0 == 0 else jnp.concatenate([old[:r0], new], axis=0)


def _aug_queries(q):
    lane = _iota((1, LANES), 1)
    one = jnp.ones_like(q)
    zero = jnp.zeros_like(q)
    qa = jnp.where(lane < HEAD_DIM, q, jnp.where(lane < HEAD_DIM + AUG, one, zero))
    qb = jnp.where(lane >= HEAD_DIM, q, jnp.where(lane < AUG, one, zero))
    return qa, qb


EXP_DEAD = -105.0
PACK = 16


def _fox_walk_left(nfull, tk, block, carry, k_refs, qk_bound, row_floor):
    lane = _iota((1, LANES), 1)

    def score_bound(h, j):
        k0 = pl.multiple_of(jnp.maximum(j, 0) * tk + tk - PACK, PACK)
        last = k_refs[h][pl.ds(k0, PACK), :].astype(F32)
        lo = HEAD_DIM if h == 0 else 0
        negc = jnp.sum(jnp.where((lane >= lo) & (lane < lo + AUG), last, 0.0), axis=1, keepdims=True)
        return qk_bound + jnp.max(negc)

    def alive(state):
        jj, c = state
        j = nfull - 1 - jj
        floors = row_floor(c)
        return (jj < nfull) & ((score_bound(0, j) - floors[0] >= EXP_DEAD) | (score_bound(1, j) - floors[1] >= EXP_DEAD))

    def step(state):
        jj, c = state
        return jj + 1, block(pl.multiple_of((nfull - 1 - jj) * tk, tk), tk, 0, c, False)

    return lax.while_loop(alive, step, (jnp.int32(0), carry))[1]


def _fox_fwd(qn, ka, kb, v, projm, qkb, *, tq, tk):
    S = qn.shape[0]
    npair = FOX_HEADS // 2

    def body(q_ref, ka_ref, kb_ref, v_ref, fg_ref, qkb_ref, o_ref, lse_ref, fm_ref):
        qi = pl.program_id(1)
        lane = _iota((1, LANES), 1)
        ma = lane < HEAD_DIM
        qaug = _aug_queries(q_ref[...])
        k_refs = (ka_ref, kb_ref)

        def block(k0, tkl, r0, carry, masked):
            vb = v_ref[pl.ds(k0, tkl), :]
            if masked:
                mask = (k0 + _iota((tq - r0, tkl), 1)) <= (qi * tq + r0 + _iota((tq - r0, tkl), 0))
            scores = [_dot_nt(qaug[h][r0:], k_refs[h][pl.ds(k0, tkl), :]) for h in range(2)]
            new = []
            for h in range(2):
                m, l, acc = [x[r0:] for x in carry[h]]
                s = jnp.where(mask, scores[h], NEG) if masked else scores[h]
                m_new = jnp.maximum(m, jnp.max(s, axis=1, keepdims=True))
                alpha = jnp.exp(m - m_new)
                p = jnp.exp(s - m_new)
                sub = (m_new, alpha * l + jnp.sum(p, axis=1, keepdims=True), alpha * acc + _dot(p.astype(BF16), vb))
                new.append(tuple(_put_rows(old, x, r0) for old, x in zip(carry[h], sub)))
            return tuple(new)

        carry = tuple((jnp.full((tq, 1), NEG, F32), jnp.zeros((tq, 1), F32), jnp.zeros((tq, LANES), F32)) for _ in range(2))
        for off, size in _diag_tiles(tq, tq):
            carry = block(pl.multiple_of(qi * tq + off, size), size, off, carry, True)
        carry = _fox_walk_left((qi * tq) // tk, tk, block, carry, k_refs, jnp.max(qkb_ref[...]),
                               lambda c: (jnp.min(c[0][0]), jnp.min(c[1][0])))
        (ma_, la, acca), (mb_, lb, accb) = carry
        o = jnp.where(ma, acca / la, accb / lb)
        o_ref[...] = o
        lse_ref[...] = jnp.where(ma, ma_ + jnp.log(la), mb_ + jnp.log(lb))
        fg = fg_ref[...]
        fm_ref[...] = (o * (fg * _sigmoid(fg))).astype(BF16)

    qblk = pl.BlockSpec((tq, LANES), lambda p, i: (i, p))
    kvblk = pl.BlockSpec((S, LANES), lambda p, i: (0, p))
    return pl.pallas_call(
        body, name="fox_fwd", grid=(npair, S // tq),
        in_specs=[qblk, kvblk, kvblk, kvblk,
                  pl.BlockSpec((tq, LANES), lambda p, i: (i, C_FG // LANES + p)),
                  pl.BlockSpec((1, LANES), lambda p, i: (0, 0))],
        out_specs=[qblk, qblk, qblk],
        out_shape=[jax.ShapeDtypeStruct((S, FOX_W), F32), jax.ShapeDtypeStruct((S, FOX_W), F32), jax.ShapeDtypeStruct((S, FOX_W), BF16)],
        compiler_params=_cparams(dimension_semantics=("arbitrary", "arbitrary")),
    )(qn, ka, kb, v, projm, qkb)


def _suffix_sums(x, tmat2):
    return _dot(jnp.concatenate(_split2(x), axis=1), tmat2)


def _suffix_matrix(tk, inclusive):
    rr, cc = _iota((2 * tk, tk), 0) & (tk - 1), _iota((2 * tk, tk), 1)
    return _ones_where(rr >= cc) if inclusive else _ones_where(rr > cc)


def _sb_scores(qh, kb, causal, tmat2, r_runs):
    heads = range(2)
    zs = [_dot_nt(qh[h], kb) for h in heads]
    nsps = [jnp.minimum(-z, 0.0) - jnp.log(1.0 + jnp.exp(-jnp.abs(z))) for z in zs]
    lbs = nsps if causal is None else [jnp.where(causal, n, 0.0) for n in nsps]
    rins = [_suffix_sums(lb, tmat2) for lb in lbs]
    args = [zs[h] + lbs[h] + (rins[h] + r_runs[h]) for h in heads]
    a_s = [jnp.exp(arg if causal is None else jnp.where(causal, arg, NEG)) for arg in args]
    return zs, nsps, lbs, a_s


def _sb_walk_left(nfull, tk, block, carry, running_sums):
    def alive(state):
        jj, c = state
        ra, rb = running_sums(c)
        return (jj < nfull) & (jnp.max(jnp.maximum(ra, rb)) >= EXP_DEAD)

    def step(state):
        jj, c = state
        return jj + 1, block(pl.multiple_of((nfull - 1 - jj) * tk, tk), 0, c, False)

    return lax.while_loop(alive, step, (jnp.int32(0), carry))[1]


def _sb_fwd(sq, sk, sv, projm, *, tq, tk):
    S = sq.shape[0]
    npair = SB_HEADS // 2

    def body(q_ref, k_ref, v_ref, sg_ref, o_ref, sm_ref):
        qi = pl.program_id(1)
        lane = _iota((1, LANES), 1)
        ma = lane < HEAD_DIM
        qh = _pair_masks(q_ref[...])
        tmat2 = _suffix_matrix(tk, inclusive=False)
        nfull = (qi * tq) // tk

        def block(k0, r0, carry, masked):
            nr = tq - r0
            kb = k_ref[pl.ds(k0, tk), :]
            vb = v_ref[pl.ds(k0, tk), :]
            causal = (k0 + _iota((nr, tk), 1)) < (qi * tq + r0 + _iota((nr, tk), 0)) if masked else None
            _, _, lbs, a_s = _sb_scores([q[r0:] for q in qh], kb, causal, tmat2, [carry[h][0][r0:] for h in range(2)])
            pv = _dot(jnp.concatenate([a.astype(BF16) for a in a_s], axis=0), vb)
            return tuple((_put_rows(carry[h][0], carry[h][0][r0:] + jnp.sum(lbs[h], axis=1, keepdims=True), r0),
                          _put_rows(carry[h][1], carry[h][1][r0:] + pv[h * nr:(h + 1) * nr], r0)) for h in range(2))

        carry = tuple((jnp.zeros((tq, 1), F32), jnp.zeros((tq, LANES), F32)) for _ in range(2))
        for off, size in reversed(_diag_tiles(tq)):
            assert size == tk
            carry = block(pl.multiple_of(qi * tq + off, tk), off, carry, True)
        (_, acca), (_, accb) = _sb_walk_left(nfull, tk, block, carry, lambda c: (c[0][0], c[1][0]))
        o = jnp.where(ma, acca, accb)
        o_ref[...] = o
        sg = sg_ref[...]
        sm_ref[...] = (o * (sg * _sigmoid(sg))).astype(BF16)

    qblk = pl.BlockSpec((tq, LANES), lambda p, i: (i, p))
    kvblk = pl.BlockSpec((S, LANES), lambda p, i: (0, p))
    return pl.pallas_call(
        body, name="sb_fwd", grid=(npair, S // tq),
        in_specs=[qblk, kvblk, kvblk, pl.BlockSpec((tq, LANES), lambda p, i: (i, C_SG // LANES + p))],
        out_specs=[qblk, qblk],
        out_shape=[jax.ShapeDtypeStruct((S, SB_W), F32), jax.ShapeDtypeStruct((S, SB_W), BF16)],
        compiler_params=_cparams(dimension_semantics=("arbitrary", "arbitrary")),
    )(sq, sk, sv, projm)


def _outproj(x, fm, pm, sm, w_out, layer, *, tm):
    S, D = x.shape

    def body(x_ref, fm_ref, pm_ref, sm_ref, w_ref, y_ref):
        y = x_ref[...] + _dot(fm_ref[...], w_ref[0:FOX_W, :])
        y = y + _dot(pm_ref[...], w_ref[FOX_W:FOX_W + POOL_W, :])
        y_ref[...] = y + _dot(sm_ref[...], w_ref[FOX_W + POOL_W:D_MIX, :])

    row = lambda w: pl.BlockSpec((tm, w), lambda i: (i, 0))
    return pl.pallas_call(
        body, name="outproj", grid=(S // tm,),
        in_specs=[row(D), row(FOX_W), row(POOL_W), row(SB_W), pl.BlockSpec((None, D_MIX, D), lambda i: (layer, 0, 0))],
        out_specs=row(D), out_shape=jax.ShapeDtypeStruct((S, D), F32),
        compiler_params=_cparams(dimension_semantics=("arbitrary",)),
    )(x, fm, pm, sm, w_out)


def _loss_head(y, target, *, tm):
    S, D = y.shape

    def body(y_ref, t_ref, dy_ref, sq_ref):
        @pl.when(pl.program_id(0) == 0)
        def _():
            sq_ref[...] = jnp.zeros_like(sq_ref)

        d = y_ref[...] - t_ref[...]
        dy_ref[...] = d * (1.0 / D)
        sq_ref[...] += jnp.sum(d * d, axis=0, keepdims=True)

    row = pl.BlockSpec((tm, D), lambda i: (i, 0))
    return pl.pallas_call(
        body, name="loss_head", grid=(S // tm,),
        in_specs=[row, row], out_specs=[row, pl.BlockSpec((1, D), lambda i: (0, 0))],
        out_shape=[jax.ShapeDtypeStruct((S, D), F32), jax.ShapeDtypeStruct((1, D), F32)],
        compiler_params=_cparams(dimension_semantics=("arbitrary",)),
    )(y, target)


def _outproj_bwd(dy, fm, pm, sm, w_out, layer, stacks, *, tm):
    S, D = dy.shape

    def body(dy_ref, fm_ref, pm_ref, sm_ref, w_ref, dm_ref, dw_ref):
        @pl.when(pl.program_id(0) == 0)
        def _():
            dw_ref[...] = jnp.zeros_like(dw_ref)

        dyb = dy_ref[...].astype(BF16)
        dm_ref[...] = _dot_nt(dyb, w_ref[...])
        dw_ref[0:FOX_W, :] += _dot_tn(fm_ref[...], dyb)
        dw_ref[FOX_W:FOX_W + POOL_W, :] += _dot_tn(pm_ref[...], dyb)
        dw_ref[FOX_W + POOL_W:D_MIX, :] += _dot_tn(sm_ref[...], dyb)

    row = lambda w: pl.BlockSpec((tm, w), lambda i: (i, 0))
    wspec = pl.BlockSpec((None, D_MIX, D), lambda i: (layer, 0, 0))
    return _stack_call(
        body, "outproj_bwd", (S // tm,), [row(D), row(FOX_W), row(POOL_W), row(SB_W), wspec], (dy, fm, pm, sm, w_out),
        [pl.BlockSpec((None, D_MIX, D), lambda i: (layer, 0, 0))], [(D_MIX, D)], stacks,
        plain_specs=[row(D_MIX)], plain_shapes=[jax.ShapeDtypeStruct((S, D_MIX), F32)],
        compiler_params=_cparams(dimension_semantics=("arbitrary",)))


def _fox_bwd(qn, ka, kb, v, o, lse, dmix, projm, qkb, *, tq, tk):
    S = qn.shape[0]
    npair = FOX_HEADS // 2

    def body(q_ref, ka_ref, kb_ref, v_ref, o_ref, lse_ref, dm_ref, fg_ref, qkb_ref,
             dq_ref, dk_ref, dv_ref, dfg_ref, dct_ref, dcr_ref):
        qi = pl.program_id(1)

        @pl.when(qi == 0)
        def _():
            dk_ref[...] = jnp.zeros_like(dk_ref)
            dv_ref[...] = jnp.zeros_like(dv_ref)
            dct_ref[...] = jnp.zeros_like(dct_ref)

        lane = _iota((1, LANES), 1)
        ma = lane < HEAD_DIM
        qh = _pair_masks(q_ref[...])
        qaug = _aug_queries(q_ref[...])
        k_refs = (ka_ref, kb_ref)
        lsev = lse_ref[...]
        lse = (_lane_pick(lsev, lane, 0), _lane_pick(lsev, lane, HEAD_DIM))
        fg = fg_ref[...]
        silu, dsilu = _silu_pair(fg)
        dm = dm_ref[...]
        ov = o_ref[...]
        do = dm * silu
        dfg_ref[...] = dm * ov * dsilu
        dd = do * ov
        dsum = (jnp.sum(jnp.where(ma, dd, 0.0), axis=1, keepdims=True), jnp.sum(jnp.where(ma, 0.0, dd), axis=1, keepdims=True))
        doh = _pair_masks(do.astype(BF16))

        def block(k0, tkl, r0, carry, masked):
            vb = v_ref[pl.ds(k0, tkl), :]
            if masked:
                mask = (k0 + _iota((tq - r0, tkl), 1)) <= (qi * tq + r0 + _iota((tq - r0, tkl), 0))
            heads = range(2)
            kaugs = [k_refs[h][pl.ds(k0, tkl), :] for h in heads]
            scores = [_dot_nt(qaug[h][r0:], kaugs[h]) for h in heads]
            dps = [_dot_nt(doh[h][r0:], vb) for h in heads]
            ps, dss, rows = [], [], []
            for h in heads:
                s = jnp.where(mask, scores[h], NEG) if masked else scores[h]
                p = jnp.exp(s - lse[h][r0:])
                dsf = p * (dps[h] - dsum[h][r0:])
                dct_ref[0, h:h + 1, pl.ds(k0, tkl)] -= jnp.sum(dsf, axis=0, keepdims=True)
                rows.append(_put_rows(carry[1 + h], carry[1 + h][r0:] + jnp.sum(dsf, axis=1, keepdims=True), r0))
                ps.append(p.astype(BF16))
                dss.append(dsf.astype(BF16))
            dv_ref[pl.ds(k0, tkl), :] += _dot_tn(jnp.concatenate(ps, axis=0), jnp.concatenate([d[r0:] for d in doh], axis=0))
            dk_ref[pl.ds(k0, tkl), :] += _dot_tn(jnp.concatenate(dss, axis=0), jnp.concatenate([q[r0:] for q in qh], axis=0))
            kh = jnp.concatenate([_pair_masks(kaugs[h])[h] for h in heads], axis=0)
            dq = _put_rows(carry[0], carry[0][r0:] + _dot(jnp.concatenate(dss, axis=1), kh), r0)
            return (dq, rows[0], rows[1])

        zcol = jnp.zeros((tq, 1), F32)
        carry = (jnp.zeros((tq, LANES), F32), zcol, zcol)
        for off, size in _diag_tiles(tq):
            carry = block(pl.multiple_of(qi * tq + off, size), size, off, carry, True)
        floors = (jnp.min(lse[0]), jnp.min(lse[1]))
        dq, rowa, rowb = _fox_walk_left((qi * tq) // tk, tk, block, carry, k_refs, jnp.max(qkb_ref[...]), lambda c: floors)
        dq_ref[...] = dq * QK_SCALE
        dcr_ref[0] = jnp.where(ma, rowa, rowb)

    qblk = pl.BlockSpec((tq, LANES), lambda p, i: (i, p))
    kvblk = pl.BlockSpec((S, LANES), lambda p, i: (0, p))
    f32out = jax.ShapeDtypeStruct((S, FOX_W), F32)
    ctblk = pl.BlockSpec((1, FF_STRIDE, S), lambda p, i: (p, 0, 0))
    return pl.pallas_call(
        body, name="fox_bwd", grid=(npair, S // tq),
        in_specs=[qblk, kvblk, kvblk, kvblk, qblk, qblk, qblk,
                  pl.BlockSpec((tq, LANES), lambda p, i: (i, C_FG // LANES + p)),
                  pl.BlockSpec((1, LANES), lambda p, i: (0, 0))],
        out_specs=[qblk, kvblk, kvblk, qblk, ctblk, pl.BlockSpec((1, tq, LANES), lambda p, i: (p, i, 0))],
        out_shape=[f32out, f32out, f32out, f32out, jax.ShapeDtypeStruct((npair, FF_STRIDE, S), F32),
                   jax.ShapeDtypeStruct((npair, S, LANES), F32)],
        compiler_params=_cparams(dimension_semantics=("arbitrary", "arbitrary")),
    )(qn, ka, kb, v, o, lse, dmix, projm, qkb)


def _sb_bwd(sq, sk, sv, o, dmix, projm, *, tq, tk):
    S = sq.shape[0]
    npair = SB_HEADS // 2
    mix0 = (FOX_W + POOL_W) // LANES

    def body(q_ref, k_ref, v_ref, o_ref, dm_ref, sg_ref, dq_ref, dk_ref, dv_ref, dsg_ref):
        qi = pl.program_id(1)

        @pl.when(qi == 0)
        def _():
            dk_ref[...] = jnp.zeros_like(dk_ref)
            dv_ref[...] = jnp.zeros_like(dv_ref)

        lane = _iota((1, LANES), 1)
        ma = lane < HEAD_DIM
        qh = _pair_masks(q_ref[...])
        sg = sg_ref[...]
        silu, dsilu = _silu_pair(sg)
        dm = dm_ref[...]
        ov = o_ref[...]
        do = dm * silu
        dsg_ref[...] = dm * ov * dsilu
        dob = do.astype(BF16)
        dd = dob.astype(F32) * ov
        dsum = (jnp.sum(jnp.where(ma, dd, 0.0), axis=1, keepdims=True), jnp.sum(jnp.where(ma, 0.0, dd), axis=1, keepdims=True))
        doh = _pair_masks(dob)
        tmat2 = _suffix_matrix(tk, inclusive=False)
        tmat2_inc = _suffix_matrix(tk, inclusive=True)
        nfull = (qi * tq) // tk

        def block(k0, r0, carry, masked):
            nr = tq - r0
            kb = k_ref[pl.ds(k0, tk), :]
            vb = v_ref[pl.ds(k0, tk), :]
            kh = _pair_masks(kb)
            causal = (k0 + _iota((nr, tk), 1)) < (qi * tq + r0 + _iota((nr, tk), 0)) if masked else None
            heads = range(2)
            qs = [q[r0:] for q in qh]
            dos = [d[r0:] for d in doh]
            das = [_dot_nt(dos[h], vb) for h in heads]
            zs, nsps, lbs, a_s = _sb_scores(qs, kb, causal, tmat2, [carry[h][0][r0:] for h in heads])
            abs_ = [a.astype(BF16) for a in a_s]
            us = [abs_[h].astype(F32) * das[h] for h in heads]
            uins = [_suffix_sums(u, tmat2_inc) for u in us]
            dzs = []
            for h in heads:
                cum_u = dsum[h][r0:] - (uins[h] + carry[h][1][r0:])
                dz = us[h] * jnp.exp(nsps[h]) - jnp.exp(zs[h] + nsps[h]) * cum_u
                if masked:
                    dz = jnp.where(causal, dz, 0.0)
                dzs.append(dz.astype(BF16))
            dv_ref[pl.ds(k0, tk), :] += _dot_tn(jnp.concatenate(abs_, axis=0), jnp.concatenate(dos, axis=0))
            dk_ref[pl.ds(k0, tk), :] += _dot_tn(jnp.concatenate(dzs, axis=0), jnp.concatenate(qs, axis=0))
            dq = _put_rows(carry[2], carry[2][r0:] + _dot(jnp.concatenate(dzs, axis=1), jnp.concatenate(kh, axis=0)), r0)
            new = [(_put_rows(carry[h][0], carry[h][0][r0:] + jnp.sum(lbs[h], axis=1, keepdims=True), r0),
                    _put_rows(carry[h][1], carry[h][1][r0:] + jnp.sum(us[h], axis=1, keepdims=True), r0)) for h in heads]
            return (new[0], new[1], dq)

        zcol = jnp.zeros((tq, 1), F32)
        carry = ((zcol, zcol), (zcol, zcol), jnp.zeros((tq, LANES), F32))
        for off, size in reversed(_diag_tiles(tq)):
            assert size == tk
            carry = block(pl.multiple_of(qi * tq + off, tk), off, carry, True)
        dq = _sb_walk_left(nfull, tk, block, carry, lambda c: (c[0][0], c[1][0]))[2]
        dq_ref[...] = dq * QK_SCALE

    qblk = pl.BlockSpec((tq, LANES), lambda p, i: (i, p))
    kvblk = pl.BlockSpec((S, LANES), lambda p, i: (0, p))
    f32out = jax.ShapeDtypeStruct((S, SB_W), F32)
    return pl.pallas_call(
        body, name="sb_bwd", grid=(npair, S // tq),
        in_specs=[qblk, kvblk, kvblk, qblk,
                  pl.BlockSpec((tq, LANES), lambda p, i: (i, mix0 + p)),
                  pl.BlockSpec((tq, LANES), lambda p, i: (i, C_SG // LANES + p))],
        out_specs=[qblk, kvblk, kvblk, qblk],
        out_shape=[f32out, f32out, f32out, f32out],
        compiler_params=_cparams(dimension_semantics=("arbitrary", "arbitrary")),
    )(sq, sk, sv, o, dmix, projm)


def _prep_bwd(projm, ffo, dqn, dkn, dct, dcr, dv, dfg, dsq, dsk, dsv, dsg, dmix, pooled, yp, qg, kg, bfp, wpd, ps, *, ts):
    S = projm.shape[0]
    nb = S // ts
    hb = ts // POOL_HALO
    npair = FOX_HEADS // 2
    last_halo = S // POOL_HALO - 1

    def body(fq_ref, fk_ref, pp_ref, pph_ref, ff_ref,
             dqn_ref, dkn_ref, dct_ref, dcr_ref, dv_ref, dfg_ref, dsq_ref, dsk_ref, dsv_ref, dsg_ref,
             dmp_ref, dmh_ref, pooled_ref, yp_ref, qg_ref, kg_ref, bf_ref, wpd_ref, ps_ref,
             dp_ref, dqg_ref, dkg_ref, dbf_ref, dwp_ref, dps_ref,
             carry_ref, dl_ref, buf_ref, dct_s):
        i = pl.program_id(0)
        blk = nb - 1 - i

        @pl.when(i == 0)
        def _():
            carry_ref[...] = jnp.zeros_like(carry_ref)
            dqg_ref[...] = jnp.zeros_like(dqg_ref)
            dkg_ref[...] = jnp.zeros_like(dkg_ref)
            dbf_ref[...] = jnp.zeros_like(dbf_ref)
            dwp_ref[...] = jnp.zeros_like(dwp_ref)
            dps_ref[...] = jnp.zeros_like(dps_ref)

        bd = _head_blockdiag(FOX_W)
        for raw_ref, g_ref, dn, dg_ref, col in ((fq_ref, qg_ref, dqn_ref[...], dqg_ref, C_FQ), (fk_ref, kg_ref, dkn_ref[...], dkg_ref, C_FK)):
            q = raw_ref[...]
            rstd = lax.rsqrt(_group_sum(q * q, bd) * (1.0 / HEAD_DIM) + EPS)
            xhat = q * rstd
            dg_ref[...] += jnp.sum(dn * xhat, axis=0, keepdims=True)
            dyg = dn * g_ref[...]
            mean = _group_sum(dyg * xhat, bd) * (1.0 / HEAD_DIM)
            dp_ref[:, col:col + FOX_W] = (rstd * (dyg - xhat * mean)).astype(BF16)
        dp_ref[:, C_FV:C_FV + FOX_W] = dv_ref[...].astype(BF16)
        dp_ref[:, C_FG:C_FG + FOX_W] = dfg_ref[...].astype(BF16)
        dp_ref[:, C_SQ:C_SQ + SB_W] = dsq_ref[...].astype(BF16)
        dp_ref[:, C_SK:C_SK + SB_W] = dsk_ref[...].astype(BF16)
        dp_ref[:, C_SV:C_SV + SB_W] = dsv_ref[...].astype(BF16)
        dp_ref[:, C_SG:C_SG + SB_W] = dsg_ref[...].astype(BF16)

        dct_s[...] = jnp.zeros_like(dct_s)
        for p in range(npair):
            dct_s[FF_STRIDE * p:FF_STRIDE * (p + 1), :] = dct_ref[p]
        dc = dct_s[...].T
        lane = _iota((1, LANES), 1)
        for p in range(npair):
            dcr = dcr_ref[p]
            dc = dc + jnp.where(lane == FF_STRIDE * p, _lane_pick(dcr, lane, 0), 0.0)
            dc = dc + jnp.where(lane == FF_STRIDE * p + 1, _lane_pick(dcr, lane, HEAD_DIM), 0.0)
        triu = _ones_where(_iota((ts, ts), 1) >= _iota((ts, ts), 0))
        dlf = _dot_exact_lhs(triu, dc) + carry_ref[...]
        dl_ref[...] = dlf
        carry_ref[...] = dl_ref[0:1, :]
        z = ff_ref[...] + bf_ref[...]
        dff = dlf * (1.0 / (1.0 + jnp.exp(z)))
        dbf_ref[...] += jnp.sum(dff, axis=0, keepdims=True)
        dp_ref[:, PM:PW] = dff.astype(BF16)

        psv = ps_ref[...]
        wpdv = wpd_ref[...]
        lane_group = _iota((1, POOL_W), 1) >> 6
        wlen = _pool_group_select(lane_group, [float(w) for w in POOL_WINDOWS])
        pg = pp_ref[:, POOL_W:2 * POOL_W]
        silu, dsilu = _silu_pair(pg)
        dmp = dmp_ref[...]
        ypv = yp_ref[...]
        dp_ref[:, C_PG:C_PG + POOL_W] = (dmp * (ypv * psv) * dsilu).astype(BF16)
        dps_ref[...] += jnp.sum(dmp * silu * ypv, axis=0, keepdims=True)
        dyp = (dmp * psv * silu).astype(BF16)
        dwp_ref[...] += _dot_tn(pooled_ref[...], dyp)
        dpooled = _dot_nt(dyp, wpdv)
        pgh = pph_ref[:, POOL_W:2 * POOL_W]
        dyph = (dmh_ref[...] * psv * (pgh * _sigmoid(pgh))).astype(BF16)
        dpooled_h = jnp.where(blk < nb - 1, _dot_nt(dyph, wpdv), 0.0)
        tpos = (blk * ts + _iota((ts, 1), 0) + 1).astype(F32)
        ev = dpooled / jnp.minimum(tpos, wlen)
        buf_ref[0:ts, :] = ev
        buf_ref[ts:ts + POOL_HALO, :] = dpooled_h / wlen
        acc = ev
        snaps = []
        for d in range(1, POOL_HALO):
            acc = acc + buf_ref[pl.ds(d, ts), :]
            if d + 1 in POOL_WINDOWS:
                snaps.append(acc)
        dp_ref[:, C_PX:C_PX + POOL_W] = (_pool_group_select(lane_group, snaps) - dpooled).astype(BF16)

    rblk = lambda w, c: pl.BlockSpec((ts, w), lambda i: (nb - 1 - i, c))
    full = lambda a: pl.BlockSpec(a.shape, lambda i: (0,) * a.ndim)
    halo = lambda w, c: pl.BlockSpec((POOL_HALO, w), lambda i: (jnp.minimum((nb - i) * hb, last_halo), c))
    acc_spec = lambda r, w: pl.BlockSpec((r, w), lambda i: (0, 0))
    return pl.pallas_call(
        body, name="prep_bwd", grid=(nb,),
        in_specs=[rblk(FOX_W, C_FQ // FOX_W), rblk(FOX_W, C_FK // FOX_W), rblk(2 * POOL_W, C_PX // (2 * POOL_W)),
                  halo(2 * POOL_W, C_PX // (2 * POOL_W)), rblk(LANES, 0),
                  rblk(FOX_W, 0), rblk(FOX_W, 0), pl.BlockSpec((npair, FF_STRIDE, ts), lambda i: (0, 0, nb - 1 - i)),
                  pl.BlockSpec((npair, ts, LANES), lambda i: (0, nb - 1 - i, 0)), rblk(FOX_W, 0), rblk(FOX_W, 0),
                  rblk(SB_W, 0), rblk(SB_W, 0), rblk(SB_W, 0), rblk(SB_W, 0),
                  rblk(POOL_W, FOX_W // POOL_W), halo(POOL_W, FOX_W // POOL_W), rblk(POOL_W, 0), rblk(POOL_W, 0),
                  full(qg), full(kg), full(bfp), full(wpd), full(ps)],
        out_specs=[rblk(PW, 0), acc_spec(1, FOX_W), acc_spec(1, FOX_W), acc_spec(1, LANES), acc_spec(POOL_W, POOL_W), acc_spec(1, POOL_W)],
        out_shape=[jax.ShapeDtypeStruct((S, PW), BF16), jax.ShapeDtypeStruct((1, FOX_W), F32), jax.ShapeDtypeStruct((1, FOX_W), F32),
                   jax.ShapeDtypeStruct((1, LANES), F32), jax.ShapeDtypeStruct((POOL_W, POOL_W), F32), jax.ShapeDtypeStruct((1, POOL_W), F32)],
        scratch_shapes=[pltpu.VMEM((1, LANES), F32), pltpu.VMEM((ts, LANES), F32), pltpu.VMEM((ts + POOL_HALO, POOL_W), F32),
                        pltpu.VMEM((LANES, ts), F32)],
        compiler_params=_cparams(dimension_semantics=("arbitrary",)),
    )(projm, projm, projm, projm, ffo, dqn, dkn, dct, dcr, dv, dfg, dsq, dsk, dsv, dsg, dmix, dmix, pooled, yp, qg, kg, bfp, wpd, ps)


def _stack_call(body, name, grid, in_specs, operands, slot_specs, slot_shapes, stacks, plain_specs=(), plain_shapes=(), **kw):
    out_specs = list(plain_specs) + list(slot_specs)
    out_shape = list(plain_shapes) + [jax.ShapeDtypeStruct((DEPTH,) + s, F32) for s in slot_shapes]
    if stacks is None:
        return pl.pallas_call(body, name=name, grid=grid, in_specs=in_specs, out_specs=out_specs, out_shape=out_shape, **kw)(*operands)
    n = len(operands)

    def aliased_body(*refs):
        body(*refs[:n], *refs[n + len(stacks):])

    return pl.pallas_call(
        aliased_body, name=name, grid=grid, in_specs=list(in_specs) + [pl.BlockSpec(memory_space=pl.ANY)] * len(stacks),
        out_specs=out_specs, out_shape=out_shape,
        input_output_aliases={n + k: len(plain_specs) + k for k in range(len(stacks))}, **kw)(*operands, *stacks)


def _inproj_dw(h, dproj, layer, stacks, *, ts, tn):
    S, D = h.shape
    nj = PM // tn

    def body(h_ref, dp_ref, dpf_ref, dw_ref, dwf_ref):
        s = pl.program_id(1)

        @pl.when(s == 0)
        def _():
            dw_ref[...] = jnp.zeros_like(dw_ref)

        @pl.when((s == 0) & (pl.program_id(0) == 0))
        def _():
            dwf_ref[...] = jnp.zeros_like(dwf_ref)

        hv = h_ref[...]
        dw_ref[...] += _dot_tn(dp_ref[...], hv)

        @pl.when(pl.program_id(0) == 0)
        def _():
            dwf_ref[...] += _dot_tn(dpf_ref[...], hv)

    return _stack_call(
        body, "inproj_dw", (nj, S // ts),
        [pl.BlockSpec((ts, D), lambda j, s: (s, 0)),
         pl.BlockSpec((ts, tn), lambda j, s: (s, j)),
         pl.BlockSpec((ts, LANES), lambda j, s: (s, PM // LANES))],
        (h, dproj, dproj),
        [pl.BlockSpec((None, tn, D), lambda j, s: (layer, j, 0)), pl.BlockSpec((None, LANES, D), lambda j, s: (layer, 0, 0))],
        [(PM, D), (LANES, D)], stacks,
        compiler_params=_cparams(dimension_semantics=("arbitrary", "arbitrary")))


def _inproj_dx(dproj, wt_all, layer, x, g, dy, *, tm):
    S, D = x.shape

    def body(dp_ref, w_ref, x_ref, g_ref, dy_ref, dx_ref, dg_ref):
        @pl.when(pl.program_id(0) == 0)
        def _():
            dg_ref[...] = jnp.zeros_like(dg_ref)

        dh = _dot(dp_ref[...], w_ref[...])
        xf = x_ref[...]
        rstd = lax.rsqrt(jnp.mean(xf * xf, axis=-1, keepdims=True) + EPS)
        xhat = xf * rstd
        dg_ref[...] += jnp.sum(dh * xhat, axis=0, keepdims=True)
        dyg = dh * g_ref[...]
        mean = jnp.mean(dyg * xhat, axis=-1, keepdims=True)
        dx_ref[...] = rstd * (dyg - xhat * mean) + dy_ref[...]

    row = lambda w: pl.BlockSpec((tm, w), lambda i: (i, 0))
    return pl.pallas_call(
        body, name="inproj_dx", grid=(S // tm,),
        in_specs=[row(PW), pl.BlockSpec((None, PW, D), lambda i: (layer, 0, 0)), row(D), pl.BlockSpec((1, D), lambda i: (0, 0)), row(D)],
        out_specs=[row(D), pl.BlockSpec((1, D), lambda i: (0, 0))],
        out_shape=[jax.ShapeDtypeStruct((S, D), F32), jax.ShapeDtypeStruct((1, D), F32)],
        compiler_params=_cparams(dimension_semantics=("arbitrary",)),
    )(dproj, wt_all, x, g, dy)


def _adam_update(w, g, m, v):
    nm = ADAM_B1 * m + (1.0 - ADAM_B1) * g
    nv = ADAM_B2 * v + (1.0 - ADAM_B2) * (g * g)
    m_hat = nm / (1.0 - ADAM_B1 ** ADAM_STEP)
    v_hat = nv / (1.0 - ADAM_B2 ** ADAM_STEP)
    return -ADAM_LR * (m_hat / (jnp.sqrt(v_hat) + ADAM_EPS) + ADAM_WD * w), nm, nv


def _adamw(w, g, m, v):
    L, R, C = w.shape
    tr = R if R <= 512 else 256

    def body(w_ref, g_ref, m_ref, v_ref, d_ref, nm_ref, nv_ref):
        d_ref[...], nm_ref[...], nv_ref[...] = _adam_update(w_ref[...], g_ref[...], m_ref[...], v_ref[...])

    spec = pl.BlockSpec((1, tr, C), lambda l, i: (l, i, 0))
    shp = jax.ShapeDtypeStruct((L, R, C), F32)
    return pl.pallas_call(
        body, name="adamw", grid=(L, R // tr), in_specs=[spec] * 4, out_specs=[spec] * 3, out_shape=[shp] * 3,
        compiler_params=_cparams(dimension_semantics=("arbitrary", "arbitrary")),
    )(w, g, m, v)


def _adamw_nd(w, g, m, v):
    shape = w.shape
    view = (1,) + shape if w.ndim == 2 else (shape[0], -1, shape[-1])
    outs = _adamw(w.reshape(view), g.reshape(view), m.reshape(view), v.reshape(view))
    return tuple(o.reshape(shape) for o in outs)


FLIP_C = (0, 0, 1)
FLIP_X = (1, 0, 0)
FLIP_Y = (0, 1, 0)
FLIP_XY = (1, 1, 0)
MESH = pl.DeviceIdType.MESH


def _peer(flip):
    me = (lax.axis_index("x"), lax.axis_index("y"), lax.axis_index("c"))
    return tuple(1 - a if f else a for a, f in zip(me, flip))


def _exchange(name, arrays, flips):
    n = len(arrays)

    def body(*refs):
        srcs, dsts = refs[:n], refs[n:2 * n]
        send_sems, recv_sems = refs[2 * n:]
        copies = [pltpu.make_async_remote_copy(src_ref=srcs[k], dst_ref=dsts[k], send_sem=send_sems.at[k], recv_sem=recv_sems.at[k],
                                               device_id=_peer(flips[k]), device_id_type=MESH) for k in range(n)]
        for cp in copies:
            cp.start()
        for cp in copies:
            cp.wait()

    anyspec = pl.BlockSpec(memory_space=pl.ANY)
    return pl.pallas_call(
        body, name=name, in_specs=[anyspec] * n, out_specs=[anyspec] * n,
        out_shape=[jax.ShapeDtypeStruct(a.shape, a.dtype) for a in arrays],
        scratch_shapes=[pltpu.SemaphoreType.DMA((n,)), pltpu.SemaphoreType.DMA((n,))],
    )(*arrays)


def _exchange_add(name, x, flip):
    def body(x_ref, o_ref, buf_ref, send_sem, recv_sem):
        cp = pltpu.make_async_remote_copy(src_ref=x_ref, dst_ref=buf_ref, send_sem=send_sem, recv_sem=recv_sem,
                                          device_id=_peer(flip), device_id_type=MESH)
        cp.start()
        cp.wait()
        o_ref[...] = x_ref[...] + buf_ref[...]

    vspec = pl.BlockSpec(memory_space=pltpu.VMEM)
    return pl.pallas_call(
        body, name=name, in_specs=[vspec], out_specs=vspec, out_shape=jax.ShapeDtypeStruct(x.shape, x.dtype),
        scratch_shapes=[pltpu.VMEM(x.shape, x.dtype), pltpu.SemaphoreType.DMA, pltpu.SemaphoreType.DMA],
    )(x)


def _chip_index():
    return 2 * lax.axis_index("x") + lax.axis_index("y")


def _gather_weights(w_in_t, w_out):
    wi = w_in_t.astype(BF16)
    wo = jnp.swapaxes(w_out, 0, 1).astype(BF16)
    halves = (wi.shape[0] // 2, wo.shape[0] // 2)
    masks = (2, 1, 3)
    flips = (FLIP_X, FLIP_Y, FLIP_XY)
    n_first = 2 * len(masks)

    def body(wi_ref, wo_ref, gi_ref, go_ref, send_sems, recv_sems):
        c = lax.axis_index("c")
        j = _chip_index()
        srcs = (wi_ref, wo_ref)
        dsts = (gi_ref, go_ref)
        mine = [pl.ds(h * c, h) for h in halves]
        theirs = [pl.ds(h * (1 - c), h) for h in halves]

        def copy(idx, src, dst, flip):
            return pltpu.make_async_remote_copy(src_ref=src, dst_ref=dst, send_sem=send_sems.at[idx], recv_sem=recv_sems.at[idx],
                                                device_id=_peer(flip), device_id_type=MESH)

        first = [copy(2 * k + a, srcs[a].at[mine[a]], dsts[a].at[j, mine[a]], flips[k]) for k in range(len(masks)) for a in range(2)]
        for cp in first:
            cp.start()
        own = [copy(2 * n_first + a, srcs[a], dsts[a].at[j], FLIP_C) for a in range(2)]
        for cp in own:
            cp.start()
        passed = []
        for k, m in enumerate(masks):
            for a in range(2):
                slot = dsts[a].at[j ^ m, mine[a]]
                copy(2 * k + a, slot, slot, flips[k]).wait_recv()
                fwd = copy(n_first + 2 * k + a, slot, slot, FLIP_C)
                fwd.start()
                passed.append(fwd)
        for k, m in enumerate(masks):
            for a in range(2):
                slot = dsts[a].at[j ^ m, theirs[a]]
                copy(n_first + 2 * k + a, slot, slot, FLIP_C).wait_recv()
        for cp in own:
            cp.wait()
        for cp in first + passed:
            cp.wait_send()

    anyspec = pl.BlockSpec(memory_space=pl.ANY)
    n_sems = 2 * n_first + 2
    gi, go = pl.pallas_call(
        body, name="gather_weights", in_specs=[anyspec] * 2, out_specs=[anyspec] * 2,
        out_shape=[jax.ShapeDtypeStruct((4,) + wi.shape, BF16), jax.ShapeDtypeStruct((4,) + wo.shape, BF16)],
        scratch_shapes=[pltpu.SemaphoreType.DMA((n_sems,)), pltpu.SemaphoreType.DMA((n_sems,))],
    )(wi, wo)
    w_in_t_full = gi.reshape((4 * wi.shape[0],) + wi.shape[1:])
    w_out_full = jnp.swapaxes(go.reshape((4 * wo.shape[0],) + wo.shape[1:]), 0, 1)
    return w_in_t_full, w_out_full


def _to_aligned(w_t):
    _, L, D = w_t.shape
    npair = FOX_HEADS // 2
    ff = w_t[ORIG_FF:ORIG_REST].reshape(npair, 2, L, D)
    ff = jnp.pad(ff, ((0, 0), (0, FF_STRIDE - 2), (0, 0), (0, 0))).reshape(npair * FF_STRIDE, L, D)
    ff = jnp.pad(ff, ((0, LANES - npair * FF_STRIDE), (0, 0), (0, 0)))
    return jnp.swapaxes(jnp.concatenate([w_t[:ORIG_FOX], w_t[ORIG_REST:], ff], axis=0), 0, 1)


def _from_aligned(dw_t):
    n, _, D = dw_t.shape
    npair = FOX_HEADS // 2
    ff = dw_t[:, PM:PM + npair * FF_STRIDE].reshape(n, npair, FF_STRIDE, D)[:, :, :2].reshape(n, FOX_HEADS, D)
    return jnp.swapaxes(jnp.concatenate([dw_t[:, :ORIG_FOX], ff, dw_t[:, ORIG_FOX:PM]], axis=1), 0, 1)


def _half_layers(name, stack, got):
    L, R, C = stack.shape
    half = L // 2
    tr = min(256, R)
    c = lax.axis_index("c")
    which = ((1 - c) if got is None else c).astype(jnp.int32).reshape(1)

    def body(c_ref, x_ref, *refs):
        if got is None:
            refs[0][...] = x_ref[...].astype(BF16)
        else:
            acc = x_ref[...] + refs[0][...].astype(F32)
            refs[1][...] = acc
            refs[2][...] = acc.astype(BF16)

    plain = pl.BlockSpec((1, tr, C), lambda l, i, c_ref: (l, i, 0))
    picked = pl.BlockSpec((1, tr, C), lambda l, i, c_ref: (c_ref[0] * half + l, i, 0))
    shp = lambda dt: jax.ShapeDtypeStruct((half, R, C), dt)
    grid_spec = pltpu.PrefetchScalarGridSpec(
        num_scalar_prefetch=1, grid=(half, R // tr),
        in_specs=[picked] + ([] if got is None else [plain]), out_specs=[plain] if got is None else [plain, plain])
    return pl.pallas_call(
        body, name=name, grid_spec=grid_spec, out_shape=[shp(BF16)] if got is None else [shp(F32), shp(BF16)],
        compiler_params=_cparams(dimension_semantics=("arbitrary", "arbitrary")),
    )(which, stack, *([] if got is None else [got]))


def _reduce_scatter(stack_m, stack_f, stack_o, shard_cols, shard_rows):
    j = _chip_index()
    half = DEPTH // 2
    stacks = (stack_m, stack_f, stack_o)
    give = [_half_layers("rs_give", s, None)[0] for s in stacks]
    got = _exchange("rs_d2d", give, (FLIP_C,) * len(stacks))
    (m32, mbf), (f32_, fbf), (o32, obf) = [_half_layers("rs_add_chip", s, g) for s, g in zip(stacks, got)]
    d_model = stack_m.shape[2]

    def in_shards(m, f):
        return _from_aligned(jnp.concatenate([m, f], axis=1)).reshape(4, shard_cols, half, d_model)

    def out_shards(o):
        return jnp.moveaxis(o.reshape(half, 4, shard_rows, o.shape[-1]), 1, 0)

    chip = [(in_shards(m32, f32_), in_shards(mbf, fbf)), (out_shards(o32), out_shards(obf))]
    masks = (2, 1, 3)
    flips = (FLIP_X, FLIP_Y, FLIP_XY)
    sends, sflips = [], []
    for _, bf in chip:
        for m, fl in zip(masks, flips):
            sends.append(lax.dynamic_index_in_dim(bf, j ^ m, axis=0, keepdims=False))
            sflips.append(fl)
    got = _exchange("rs_ici", sends, tuple(sflips))
    own_in, own_out = [lax.dynamic_index_in_dim(f32_sum, j, axis=0, keepdims=False) for f32_sum, _ in chip]
    mine_in = _add_rows("rs_add_in", own_in, list(got[0:3]))
    mine_out = _add_into_half("rs_add_out", own_out, list(got[3:6]))
    sib_in, g_out = _share_halves(mine_in, mine_out)
    return (mine_in, sib_in), g_out


def _add_rows(name, first, others):
    n = len(others)

    def body(*refs):
        acc = refs[0][...]
        for r in refs[1:1 + n]:
            acc = acc + r[...].astype(F32)
        refs[1 + n][...] = acc

    grid, spec = _row_lane_blocks(first.shape)
    return pl.pallas_call(
        body, name=name, grid=grid, in_specs=[spec(first.shape[1])] * (1 + n), out_specs=spec(first.shape[1]),
        out_shape=jax.ShapeDtypeStruct(first.shape, F32),
        compiler_params=_cparams(dimension_semantics=("arbitrary", "arbitrary")),
    )(first, *others)


def _row_lane_blocks(shape):
    rows, _, C = shape
    tr = rows // 2 if rows % 2 == 0 and rows > 64 else rows
    return (rows // tr, C // LANES), lambda n_mid: pl.BlockSpec((tr, n_mid, LANES), lambda i, k, *_: (i, 0, k))


def _add_into_half(name, first, others):
    half, rows, C = first.shape
    tr = min(256, rows)
    n = len(others)

    def body(c_ref, *refs):
        acc = refs[0][...]
        for r in refs[1:1 + n]:
            acc = acc + r[...].astype(F32)
        refs[1 + n][...] = acc

    grid_spec = pltpu.PrefetchScalarGridSpec(
        num_scalar_prefetch=1, grid=(half, rows // tr),
        in_specs=[pl.BlockSpec((1, tr, C), lambda l, i, c_ref: (l, i, 0))] * (1 + n),
        out_specs=pl.BlockSpec((1, tr, C), lambda l, i, c_ref: (c_ref[0] * half + l, i, 0)))
    return pl.pallas_call(
        body, name=name, grid_spec=grid_spec, out_shape=jax.ShapeDtypeStruct((2 * half, rows, C), F32),
        compiler_params=_cparams(dimension_semantics=("arbitrary", "arbitrary")),
    )(lax.axis_index("c").astype(jnp.int32).reshape(1), first, *others)


def _share_halves(mine, buf):
    half = DEPTH // 2

    def body(mine_ref, buf_in, sib_ref, buf_ref, send_sems, recv_sems):
        lay = pl.ds(half * lax.axis_index("c"), half)
        copies = [pltpu.make_async_remote_copy(src_ref=src, dst_ref=dst, send_sem=send_sems.at[k], recv_sem=recv_sems.at[k],
                                               device_id=_peer(FLIP_C), device_id_type=MESH)
                  for k, (src, dst) in enumerate(((mine_ref, sib_ref), (buf_ref.at[lay], buf_ref.at[lay])))]
        for cp in copies:
            cp.start()
        for cp in copies:
            cp.wait()

    anyspec = pl.BlockSpec(memory_space=pl.ANY)
    return pl.pallas_call(
        body, name="rs_share", in_specs=[anyspec] * 2, out_specs=[anyspec] * 2,
        out_shape=[jax.ShapeDtypeStruct(mine.shape, mine.dtype), jax.ShapeDtypeStruct(buf.shape, buf.dtype)],
        input_output_aliases={1: 1},
        scratch_shapes=[pltpu.SemaphoreType.DMA((2,)), pltpu.SemaphoreType.DMA((2,))],
    )(mine, buf)


def _adamw_halves(w, g_mine, g_sib, m, v):
    half = g_mine.shape[1]

    def body(c_ref, w_ref, gm_ref, gs_ref, m_ref, v_ref, g_ref, d_ref, nm_ref, nv_ref):
        first = c_ref[0] == 0
        gm, gs = gm_ref[...], gs_ref[...]
        for h, gv in enumerate((jnp.where(first, gm, gs), jnp.where(first, gs, gm))):
            lay = slice(half * h, half * (h + 1))
            g_ref[:, lay, :] = gv
            d_ref[:, lay, :], nm_ref[:, lay, :], nv_ref[:, lay, :] = _adam_update(w_ref[:, lay, :], gv, m_ref[:, lay, :], v_ref[:, lay, :])

    grid, spec = _row_lane_blocks(w.shape)
    full, part = spec(w.shape[1]), spec(half)
    grid_spec = pltpu.PrefetchScalarGridSpec(num_scalar_prefetch=1, grid=grid, in_specs=[full, part, part, full, full], out_specs=[full] * 4)
    return pl.pallas_call(
        body, name="adamw_halves", grid_spec=grid_spec, out_shape=[jax.ShapeDtypeStruct(w.shape, F32)] * 4,
        compiler_params=_cparams(dimension_semantics=("arbitrary", "arbitrary")),
    )(lax.axis_index("c").astype(jnp.int32).reshape(1), w, g_mine, g_sib, m, v)


def _all_reduce_small(x):
    x = _exchange_add("ar_c", x, FLIP_C)
    x = _exchange_add("ar_y", x, FLIP_Y)
    return _exchange_add("ar_x", x, FLIP_X)


def _blocks(S):
    return dict(tm=min(512, S), tm_proj=min(1024, S), ts=min(512, S), tq=min(512, S), tq_big=min(1024, S), tk=min(512, S), tks=min(256, S))


def _pair_pad(vec):
    npair = FOX_HEADS // 2
    v = jnp.pad(vec.reshape(npair, 2), ((0, 0), (0, FF_STRIDE - 2))).reshape(1, npair * FF_STRIDE)
    return jnp.pad(v, ((0, 0), (0, LANES - npair * FF_STRIDE)))


def _pair_unpad(row):
    npair = FOX_HEADS // 2
    return row[0, :npair * FF_STRIDE].reshape(npair, FF_STRIDE)[:, :2].reshape(FOX_HEADS)


def _pool_blockdiag(w_pool):
    g, cg, _ = w_pool.shape
    eye = jnp.eye(g, dtype=w_pool.dtype)
    return jnp.einsum("gh,gcd->gchd", eye, w_pool).reshape(g * cg, g * cg)


QK_BOUND_SLACK = 1.05


def _layer_params(norm_g, b_f, q_norm_g, k_norm_g, w_pool, pool_scale):
    qk_bound = QK_BOUND_SLACK * HEAD_DIM * QK_SCALE * jnp.max(jnp.abs(q_norm_g)) * jnp.max(jnp.abs(k_norm_g))
    return dict(g=norm_g.reshape(1, -1), qg=jnp.tile(q_norm_g, FOX_HEADS).reshape(1, FOX_W), kg=jnp.tile(k_norm_g, FOX_HEADS).reshape(1, FOX_W),
                bfp=_pair_pad(b_f), wpd=_pool_blockdiag(w_pool).astype(BF16), ps=pool_scale.reshape(1, POOL_W),
                qkb=jnp.full((1, LANES), qk_bound, F32))


def _layer_fwd(x, wt_all, w_out, layer, prm, bs):
    projm, ffo, h = _inproj(x, prm["g"], wt_all, layer, tm=bs["tm_proj"], tn=PROJ_TN)
    qn, ka, kb, v, sq, sk, sv, pooled, yp, pm = _prep(projm, ffo, prm["qg"], prm["kg"], prm["bfp"], prm["wpd"], prm["ps"], ts=bs["ts"])
    o, lse, fm = _fox_fwd(qn, ka, kb, v, projm, prm["qkb"], tq=bs["tq"], tk=bs["tk"])
    so, sm = _sb_fwd(sq, sk, sv, projm, tq=bs["tq_big"], tk=bs["tks"])
    y = _outproj(x, fm, pm, sm, w_out, layer, tm=bs["tm"])
    saved = dict(x=x, projm=projm, ffo=ffo, h=h, qn=qn, ka=ka, kb=kb, v=v, sq=sq, sk=sk, sv=sv, pooled=pooled, yp=yp,
                 o=o, lse=lse, so=so, fm=fm, pm=pm, sm=sm)
    return y, saved


def _layer_bwd(dy, wt_all, w_out, prm, sv_, bs, layer, stacks):
    dmix, stack_o = _outproj_bwd(dy, sv_["fm"], sv_["pm"], sv_["sm"], w_out, layer, None if stacks is None else stacks[2:], tm=bs["tm"])
    dqn, dkn, dv, dfg, dct, dcr = _fox_bwd(sv_["qn"], sv_["ka"], sv_["kb"], sv_["v"], sv_["o"], sv_["lse"], dmix, sv_["projm"],
                                      prm["qkb"], tq=bs["tq_big"], tk=bs["tk"])
    dsq, dsk, dsv, dsg = _sb_bwd(sv_["sq"], sv_["sk"], sv_["sv"], sv_["so"], dmix, sv_["projm"], tq=bs["tq"], tk=bs["tks"])
    dproj, dqg, dkg, dbf, dwp, dps = _prep_bwd(sv_["projm"], sv_["ffo"], dqn, dkn, dct, dcr, dv, dfg, dsq, dsk, dsv, dsg, dmix,
                                               sv_["pooled"], sv_["yp"], prm["qg"], prm["kg"], prm["bfp"], prm["wpd"], prm["ps"], ts=bs["ts"])
    stack_m, stack_f = _inproj_dw(sv_["h"], dproj, layer, None if stacks is None else stacks[:2], ts=bs["tm_proj"], tn=PROJ_TN)
    dx, dg = _inproj_dx(dproj, wt_all, layer, sv_["x"], prm["g"], dy, tm=min(256, bs["tm"]))
    grads = dict(
        norm_g=dg[0],
        b_f=_pair_unpad(dbf), q_norm_g=dqg.reshape(FOX_HEADS, HEAD_DIM).sum(0), k_norm_g=dkg.reshape(FOX_HEADS, HEAD_DIM).sum(0),
        w_pool=jnp.stack([dwp[HEAD_DIM * g:HEAD_DIM * (g + 1), HEAD_DIM * g:HEAD_DIM * (g + 1)] for g in range(4)]),
        pool_scale=dps[0])
    return dx, grads, (stack_m, stack_f, stack_o)


def _local_step(x, target, wt_all, w_out, norm_g, b_f, q_norm_g, k_norm_g, w_pool, pool_scale):
    S, D = x.shape
    bs = _blocks(S)
    prms = [_layer_params(norm_g[l], b_f[l], q_norm_g[l], k_norm_g[l], w_pool[l], pool_scale[l]) for l in range(DEPTH)]
    saved = []
    y = x
    for l in range(DEPTH):
        y, s_ = _layer_fwd(y, wt_all, w_out, l, prms[l], bs)
        saved.append(s_)
    dy, sq = _loss_head(y, target, tm=bs["tm"])
    loss = 0.5 * jnp.sum(sq) / D
    grads = [None] * DEPTH
    stacks = None
    for l in reversed(range(DEPTH)):
        dy, grads[l], stacks = _layer_bwd(dy, wt_all, w_out, prms[l], saved[l], bs, l, stacks)
    stacked = {k: jnp.stack([g[k] for g in grads]) for k in grads[0]}
    return loss, dy, stacked, stacks


SMALL = ("norm_g", "b_f", "q_norm_g", "k_norm_g", "w_pool", "pool_scale")


def _pack_small(gr):
    flat = jnp.concatenate([gr[k].reshape(-1) for k in SMALL])
    pad = (-flat.shape[0]) % (8 * LANES)
    return jnp.pad(flat, (0, pad)).reshape(-1, LANES)


def _unpack_small(packed, like):
    flat = packed.reshape(-1)
    out, off = {}, 0
    for k in SMALL:
        n = like[k].size
        out[k] = flat[off:off + n].reshape(like[k].shape)
        off += n
    return out


def kernel(x, norm_g, w_in, b_f, q_norm_g, k_norm_g, w_pool, pool_scale, w_out, loss_target, m_norm_g, m_w_in, m_b_f, m_q_norm_g, m_k_norm_g, m_w_pool, m_pool_scale, m_w_out, v_norm_g, v_w_in, v_b_f, v_q_norm_g, v_k_norm_g, v_w_pool, v_pool_scale, v_w_out):
    weights = dict(norm_g=norm_g, w_in=w_in, b_f=b_f, q_norm_g=q_norm_g, k_norm_g=k_norm_g, w_pool=w_pool, pool_scale=pool_scale, w_out=w_out)
    mom_m = dict(norm_g=m_norm_g, w_in=m_w_in, b_f=m_b_f, q_norm_g=m_q_norm_g, k_norm_g=m_k_norm_g, w_pool=m_w_pool, pool_scale=m_pool_scale, w_out=m_w_out)
    mom_v = dict(norm_g=v_norm_g, w_in=v_w_in, b_f=v_b_f, q_norm_g=v_q_norm_g, k_norm_g=v_k_norm_g, w_pool=v_w_pool, pool_scale=v_pool_scale, w_out=v_w_out)
    shard_cols = w_in.shape[2]
    shard_rows = w_out.shape[1]

    cols_first = lambda a: jnp.transpose(a, (2, 0, 1))
    w_in_t = cols_first(w_in)
    w_in_t_full, w_out_full = _gather_weights(w_in_t, w_out)
    wt_all = _to_aligned(w_in_t_full)
    loss, dx, gr, stacks = _local_step(x[0], loss_target[0], wt_all, w_out_full, norm_g, b_f, q_norm_g, k_norm_g, w_pool, pool_scale)
    loss = lax.psum(loss, ("x", "y", "c"))

    (g_in_mine, g_in_sib), g_w_out = _reduce_scatter(*stacks, shard_cols, shard_rows)
    small = _unpack_small(_all_reduce_small(_pack_small(gr)), {k: weights[k] for k in SMALL})
    grad_w = dict(small, w_out=g_w_out)

    names = ("norm_g", "w_in", "b_f", "q_norm_g", "k_norm_g", "w_pool", "pool_scale", "w_out")
    upd = {k: _adamw_nd(weights[k], grad_w[k], mom_m[k], mom_v[k]) for k in names if k != "w_in"}
    in_t = _adamw_halves(w_in_t, g_in_mine, g_in_sib, cols_first(mom_m["w_in"]), cols_first(mom_v["w_in"]))
    grad_w["w_in"], *upd["w_in"] = [jnp.transpose(a, (1, 2, 0)) for a in in_t]
    return (loss, dx[None], *[grad_w[k] for k in names], *[upd[k][0] for k in names], *[upd[k][1] for k in names], *[upd[k][2] for k in names])
```

```python
import functools

import jax
import jax.numpy as jnp
from jax import lax
from jax.experimental import pallas as pl
from jax.experimental.pallas import tpu as pltpu

F32 = jnp.float32
BF16 = jnp.bfloat16

DEPTH = 4
HEAD_DIM = 64
FOX_HEADS = 8
SB_HEADS = 4
FOX_W = FOX_HEADS * HEAD_DIM
SB_W = SB_HEADS * HEAD_DIM
POOL_W = 256
POOL_WINDOWS = (2, 4, 8, 16)
POOL_HALO = 16
D_MIX = FOX_W + POOL_W + SB_W
EPS = 1e-6
NEG = -1e30
QK_SCALE = HEAD_DIM ** -0.5

ORIG_FOX = 4 * FOX_W
ORIG_FF = ORIG_FOX
ORIG_REST = ORIG_FF + FOX_HEADS
D_IN = ORIG_REST + 2 * POOL_W + 4 * SB_W

C_FQ, C_FK, C_FV, C_FG = 0, FOX_W, 2 * FOX_W, 3 * FOX_W
C_PX = 4 * FOX_W
C_PG = C_PX + POOL_W
C_SQ = C_PG + POOL_W
C_SK, C_SV, C_SG = C_SQ + SB_W, C_SQ + 2 * SB_W, C_SQ + 3 * SB_W
PM = C_SG + SB_W
LANES = 128
PW = PM + LANES
FF_STRIDE = 8
AUG = 3

ADAM_LR = 0.001
ADAM_B1 = 0.9
ADAM_B2 = 0.999
ADAM_EPS = 1e-08
ADAM_WD = 0.01
ADAM_STEP = 10

VMEM_LIMIT = 48 * 1024 * 1024
PROJ_TN = PM // 2


def _cparams(**kw):
    return pltpu.CompilerParams(vmem_limit_bytes=VMEM_LIMIT, **kw)


def _dot(a, b):
    return jnp.dot(a, b, preferred_element_type=F32)


def _dot_nt(a, b):
    return lax.dot_general(a, b, (((1,), (1,)), ((), ())), preferred_element_type=F32)


def _dot_tn(a, b):
    return lax.dot_general(a, b, (((0,), (0,)), ((), ())), preferred_element_type=F32)


def _split2(x):
    hi = x.astype(BF16)
    lo = (x - hi.astype(F32)).astype(BF16)
    return hi, lo


def _split3(x):
    hi = x.astype(BF16)
    r = x - hi.astype(F32)
    mid = r.astype(BF16)
    lo = (r - mid.astype(F32)).astype(BF16)
    return hi, mid, lo


def _dot_exact_rhs(x, m):
    hi, mid, lo = _split3(x)
    return _dot(hi, m) + _dot(mid, m) + _dot(lo, m)


def _dot_exact_lhs(m, x):
    hi, mid, lo = _split3(x)
    return _dot(m, hi) + _dot(m, mid) + _dot(m, lo)


def _sigmoid(x):
    return 1.0 / (1.0 + jnp.exp(-x))


def _silu_pair(x):
    s = _sigmoid(x)
    return x * s, s * (1.0 + x * (1.0 - s))


def _iota(shape, dim):
    return lax.broadcasted_iota(jnp.int32, shape, dim)


def _ones_where(cond):
    return jnp.where(cond, 1.0, 0.0).astype(BF16)


GROUP_SLAB = 256


def _head_blockdiag():
    rows, cols = _iota((2 * GROUP_SLAB, GROUP_SLAB), 0) & (GROUP_SLAB - 1), _iota((2 * GROUP_SLAB, GROUP_SLAB), 1)
    return _ones_where((rows >> 6) == (cols >> 6))


def _group_sum(x, bd):
    hi, lo = _split2(x)
    slabs = [_dot(jnp.concatenate([hi[:, s:s + GROUP_SLAB], lo[:, s:s + GROUP_SLAB]], axis=1), bd) for s in range(0, x.shape[1], GROUP_SLAB)]
    return jnp.concatenate(slabs, axis=1)


def _lane_pick(x, lane_idx, lane):
    return jnp.sum(jnp.where(lane_idx == lane, x, 0.0), axis=1, keepdims=True)


def _inproj(x, g, wt_all, layer, *, tm, tn):
    S, D = x.shape
    nj = PM // tn

    def body(x_ref, g_ref, w_ref, wff_ref, proj_ref, ff_ref, h_ref):
        @pl.when(pl.program_id(1) == 0)
        def _():
            xf = x_ref[...]
            ms = jnp.mean(xf * xf, axis=-1, keepdims=True)
            h = (xf * lax.rsqrt(ms + EPS) * g_ref[...]).astype(BF16)
            h_ref[...] = h
            ff_ref[...] = _dot_nt(h, wff_ref[...])

        proj_ref[...] = _dot_nt(h_ref[...], w_ref[...])

    return pl.pallas_call(
        body, name="inproj", grid=(S // tm, nj),
        in_specs=[pl.BlockSpec((tm, D), lambda i, j: (i, 0)),
                  pl.BlockSpec((1, D), lambda i, j: (0, 0)),
                  pl.BlockSpec((None, tn, D), lambda i, j: (layer, j, 0)),
                  pl.BlockSpec((None, LANES, D), lambda i, j: (layer, PM // LANES, 0))],
        out_specs=[pl.BlockSpec((tm, tn), lambda i, j: (i, j)),
                   pl.BlockSpec((tm, LANES), lambda i, j: (i, 0)),
                   pl.BlockSpec((tm, D), lambda i, j: (i, 0))],
        out_shape=[jax.ShapeDtypeStruct((S, PM), F32), jax.ShapeDtypeStruct((S, LANES), F32),
                   jax.ShapeDtypeStruct((S, D), BF16)],
        compiler_params=_cparams(dimension_semantics=("arbitrary", "arbitrary")),
    )(x, g, wt_all, wt_all)


def _pool_group_select(lane_group, vals):
    return jnp.where(lane_group == 0, vals[0], jnp.where(lane_group == 1, vals[1], jnp.where(lane_group == 2, vals[2], vals[3])))


def _prep(projm, ffo, qg, kg, bfp, wpd, ps, *, ts):
    S = projm.shape[0]
    nb = S // ts
    hb = ts // POOL_HALO

    def body(fq_ref, fk_ref, fv_ref, pp_ref, halo_ref, ff_ref, sq_ref, sk_ref, sv_ref,
             qg_ref, kg_ref, bf_ref, wpd_ref, ps_ref,
             qn_ref, ka_ref, kb_ref, v_ref, sqo_ref, sko_ref, svo_ref, pooled_ref, yp_ref, pm_ref,
             carry_ref, c_ref, buf_ref):
        i = pl.program_id(0)
        bd = _head_blockdiag()
        normed = []
        for src, g_ref in ((fq_ref, qg_ref), (fk_ref, kg_ref)):
            q = src[...]
            ss = _group_sum(q * q, bd)
            normed.append(q * lax.rsqrt(ss * (1.0 / HEAD_DIM) + EPS) * g_ref[...])
        qn_ref[...] = (normed[0] * QK_SCALE).astype(BF16)
        kn = normed[1]
        v_ref[...] = fv_ref[...].astype(BF16)
        sqo_ref[...] = (sq_ref[...] * QK_SCALE).astype(BF16)
        sko_ref[...] = sk_ref[...].astype(BF16)
        svo_ref[...] = sv_ref[...].astype(BF16)

        @pl.when(i == 0)
        def _():
            carry_ref[...] = jnp.zeros_like(carry_ref)

        z = ff_ref[...] + bf_ref[...]
        lf = jnp.minimum(z, 0.0) - jnp.log(1.0 + jnp.exp(-jnp.abs(z)))
        tri = _ones_where(_iota((ts, ts), 1) <= _iota((ts, ts), 0))
        c = _dot_exact_lhs(tri, lf) + carry_ref[...]
        c_ref[...] = c
        carry_ref[...] = c_ref[ts - 1:ts, :]
        parts = jnp.concatenate(_split3(-c), axis=1)
        row = _iota((AUG * LANES, FOX_W), 0)
        col = _iota((AUG * LANES, FOX_W), 1)
        part, src = row >> 7, row & (LANES - 1)
        pair, off = col >> 7, col & (LANES - 1)
        sel_a = _ones_where((src == FF_STRIDE * pair) & (off == HEAD_DIM + part))
        sel_b = _ones_where((src == FF_STRIDE * pair + 1) & (off == part))
        first_half = (_iota((1, FOX_W), 1) & HEAD_DIM) == 0
        ka_ref[...] = jnp.where(first_half, kn, _dot(parts, sel_a)).astype(BF16)
        kb_ref[...] = jnp.where(first_half, _dot(parts, sel_b), kn).astype(BF16)

        x = pp_ref[:, 0:POOL_W]
        pg = pp_ref[:, POOL_W:2 * POOL_W]
        halo = jnp.where(i > 0, halo_ref[:, 0:POOL_W], 0.0)
        buf_ref[0:POOL_HALO, :] = halo
        buf_ref[POOL_HALO:POOL_HALO + ts, :] = x
        acc = x
        snaps = []
        for d in range(1, POOL_HALO):
            acc = acc + buf_ref[pl.ds(POOL_HALO - d, ts), :]
            if d + 1 in POOL_WINDOWS:
                snaps.append(acc)
        lane_group = _iota((1, POOL_W), 1) >> 6
        wsum = _pool_group_select(lane_group, snaps)
        wlen = _pool_group_select(lane_group, [float(w) for w in POOL_WINDOWS])
        tpos = (i * ts + _iota((ts, 1), 0) + 1).astype(F32)
        pooled = wsum / jnp.minimum(tpos, wlen) - x
        pb = pooled.astype(BF16)
        pooled_ref[...] = pb
        yp = _dot(pb, wpd_ref[...])
        yp_ref[...] = yp
        pm_ref[...] = (yp * ps_ref[...] * (pg * _sigmoid(pg))).astype(BF16)

    blk = lambda w, c: pl.BlockSpec((ts, w), lambda i: (i, c))
    full = lambda a: pl.BlockSpec(a.shape, lambda i: (0,) * a.ndim)
    out_shapes = [
        jax.ShapeDtypeStruct((S, FOX_W), BF16), jax.ShapeDtypeStruct((S, FOX_W), BF16), jax.ShapeDtypeStruct((S, FOX_W), BF16),
        jax.ShapeDtypeStruct((S, FOX_W), BF16),
        jax.ShapeDtypeStruct((S, SB_W), BF16), jax.ShapeDtypeStruct((S, SB_W), BF16), jax.ShapeDtypeStruct((S, SB_W), BF16),
        jax.ShapeDtypeStruct((S, POOL_W), BF16), jax.ShapeDtypeStruct((S, POOL_W), F32), jax.ShapeDtypeStruct((S, POOL_W), BF16),
    ]
    out_specs = [
        blk(FOX_W, 0), blk(FOX_W, 0), blk(FOX_W, 0), blk(FOX_W, 0),
        blk(SB_W, 0), blk(SB_W, 0), blk(SB_W, 0),
        blk(POOL_W, 0), blk(POOL_W, 0), blk(POOL_W, 0),
    ]
    return pl.pallas_call(
        body, name="prep", grid=(nb,),
        in_specs=[blk(FOX_W, C_FQ // FOX_W), blk(FOX_W, C_FK // FOX_W), blk(FOX_W, C_FV // FOX_W), blk(2 * POOL_W, C_PX // (2 * POOL_W)),
                  pl.BlockSpec((POOL_HALO, 2 * POOL_W), lambda i: (jnp.maximum(i * hb - 1, 0), C_PX // (2 * POOL_W))),
                  blk(LANES, 0),
                  blk(SB_W, C_SQ // SB_W), blk(SB_W, C_SK // SB_W), blk(SB_W, C_SV // SB_W),
                  full(qg), full(kg), full(bfp), full(wpd), full(ps)],
        out_specs=out_specs, out_shape=out_shapes,
        scratch_shapes=[pltpu.VMEM((1, LANES), F32), pltpu.VMEM((ts, LANES), F32), pltpu.VMEM((ts + POOL_HALO, POOL_W), F32)],
        compiler_params=_cparams(dimension_semantics=("arbitrary",)),
    )(projm, projm, projm, projm, projm, ffo, projm, projm, projm, qg, kg, bfp, wpd, ps)


def _pair_masks(x):
    ma = _iota((1, LANES), 1) < HEAD_DIM
    zero = jnp.zeros_like(x)
    return jnp.where(ma, x, zero), jnp.where(ma, zero, x)


DIAG_TILE = 256


def _diag_tiles(tq, size=DIAG_TILE):
    size = min(tq, size)
    return [(t * size, size) for t in range(tq // size)]


def _put_rows(old, new, r0):
    return new if r0 == 0 else jnp.concatenate([old[:r0], new], axis=0)


def _aug_queries(q):
    lane = _iota((1, LANES), 1)
    one = jnp.ones_like(q)
    zero = jnp.zeros_like(q)
    qa = jnp.where(lane < HEAD_DIM, q, jnp.where(lane < HEAD_DIM + AUG, one, zero))
    qb = jnp.where(lane >= HEAD_DIM, q, jnp.where(lane < AUG, one, zero))
    return qa, qb


EXP_DEAD = -105.0
PACK = 16


def _fox_walk_left(nfull, tk, block, carry, k_refs, qk_bound, row_floor):
    lane = _iota((1, LANES), 1)

    def score_bound(h, j):
        k0 = pl.multiple_of(jnp.maximum(j, 0) * tk + tk - PACK, PACK)
        last = k_refs[h][pl.ds(k0, PACK), :].astype(F32)
        lo = HEAD_DIM if h == 0 else 0
        negc = jnp.sum(jnp.where((lane >= lo) & (lane < lo + AUG), last, 0.0), axis=1, keepdims=True)
        return qk_bound + jnp.max(negc)

    def alive(state):
        jj, c = state
        j = nfull - 1 - jj
        floors = row_floor(c)
        return (jj < nfull) & ((score_bound(0, j) - floors[0] >= EXP_DEAD) | (score_bound(1, j) - floors[1] >= EXP_DEAD))

    def step(state):
        jj, c = state
        return jj + 1, block(pl.multiple_of((nfull - 1 - jj) * tk, tk), tk, 0, c, False)

    return lax.while_loop(alive, step, (jnp.int32(0), carry))[1]


def _fox_fwd(qn, ka, kb, v, projm, qkb, *, tq, tk):
    S = qn.shape[0]
    npair = FOX_HEADS // 2

    def body(q_ref, ka_ref, kb_ref, v_ref, fg_ref, qkb_ref, o_ref, lse_ref, fm_ref):
        qi = pl.program_id(1)
        lane = _iota((1, LANES), 1)
        ma = lane < HEAD_DIM
        qaug = _aug_queries(q_ref[...])
        k_refs = (ka_ref, kb_ref)

        def block(k0, tkl, r0, carry, masked):
            vb = v_ref[pl.ds(k0, tkl), :]
            if masked:
                mask = (k0 + _iota((tq - r0, tkl), 1)) <= (qi * tq + r0 + _iota((tq - r0, tkl), 0))
            scores = [_dot_nt(qaug[h][r0:], k_refs[h][pl.ds(k0, tkl), :]) for h in range(2)]
            new = []
            for h in range(2):
                m, l, acc = [x[r0:] for x in carry[h]]
                s = jnp.where(mask, scores[h], NEG) if masked else scores[h]
                m_new = jnp.maximum(m, jnp.max(s, axis=1, keepdims=True))
                alpha = jnp.exp(m - m_new)
                p = jnp.exp(s - m_new)
                sub = (m_new, alpha * l + jnp.sum(p, axis=1, keepdims=True), alpha * acc + _dot(p.astype(BF16), vb))
                new.append(tuple(_put_rows(old, x, r0) for old, x in zip(carry[h], sub)))
            return tuple(new)

        carry = tuple((jnp.full((tq, 1), NEG, F32), jnp.zeros((tq, 1), F32), jnp.zeros((tq, LANES), F32)) for _ in range(2))
        for off, size in _diag_tiles(tq, tq):
            carry = block(pl.multiple_of(qi * tq + off, size), size, off, carry, True)
        carry = _fox_walk_left((qi * tq) // tk, tk, block, carry, k_refs, jnp.max(qkb_ref[...]),
                               lambda c: (jnp.min(c[0][0]), jnp.min(c[1][0])))
        (ma_, la, acca), (mb_, lb, accb) = carry
        o = jnp.where(ma, acca / la, accb / lb)
        o_ref[...] = o
        lse_ref[...] = jnp.where(ma, ma_ + jnp.log(la), mb_ + jnp.log(lb))
        fg = fg_ref[...]
        fm_ref[...] = (o * (fg * _sigmoid(fg))).astype(BF16)

    qblk = pl.BlockSpec((tq, LANES), lambda p, i: (i, p))
    kvblk = pl.BlockSpec((S, LANES), lambda p, i: (0, p))
    return pl.pallas_call(
        body, name="fox_fwd", grid=(npair, S // tq),
        in_specs=[qblk, kvblk, kvblk, kvblk,
                  pl.BlockSpec((tq, LANES), lambda p, i: (i, C_FG // LANES + p)),
                  pl.BlockSpec((1, LANES), lambda p, i: (0, 0))],
        out_specs=[qblk, qblk, qblk],
        out_shape=[jax.ShapeDtypeStruct((S, FOX_W), F32), jax.ShapeDtypeStruct((S, FOX_W), F32), jax.ShapeDtypeStruct((S, FOX_W), BF16)],
        compiler_params=_cparams(dimension_semantics=("arbitrary", "arbitrary")),
    )(qn, ka, kb, v, projm, qkb)


def _suffix_sums(x, tmat2):
    return _dot(jnp.concatenate(_split2(x), axis=1), tmat2)


def _suffix_matrix(tk, inclusive):
    rr, cc = _iota((2 * tk, tk), 0) & (tk - 1), _iota((2 * tk, tk), 1)
    return _ones_where(rr >= cc) if inclusive else _ones_where(rr > cc)


def _sb_scores(qh, kb, causal, tmat2, r_runs):
    heads = range(2)
    zs = [_dot_nt(qh[h], kb) for h in heads]
    nsps = [jnp.minimum(-z, 0.0) - jnp.log(1.0 + jnp.exp(-jnp.abs(z))) for z in zs]
    lbs = nsps if causal is None else [jnp.where(causal, n, 0.0) for n in nsps]
    rins = [_suffix_sums(lb, tmat2) for lb in lbs]
    args = [zs[h] + lbs[h] + (rins[h] + r_runs[h]) for h in heads]
    a_s = [jnp.exp(arg if causal is None else jnp.where(causal, arg, NEG)) for arg in args]
    return zs, nsps, lbs, a_s


def _sb_walk_left(nfull, tk, block, carry, running_sums):
    def alive(state):
        jj, c = state
        ra, rb = running_sums(c)
        return (jj < nfull) & (jnp.max(jnp.maximum(ra, rb)) >= EXP_DEAD)

    def step(state):
        jj, c = state
        return jj + 1, block(pl.multiple_of((nfull - 1 - jj) * tk, tk), 0, c, False)

    return lax.while_loop(alive, step, (jnp.int32(0), carry))[1]


def _sb_fwd(sq, sk, sv, projm, *, tq, tk):
    S = sq.shape[0]
    npair = SB_HEADS // 2

    def body(q_ref, k_ref, v_ref, sg_ref, o_ref, sm_ref):
        qi = pl.program_id(1)
        lane = _iota((1, LANES), 1)
        ma = lane < HEAD_DIM
        qh = _pair_masks(q_ref[...])
        tmat2 = _suffix_matrix(tk, inclusive=False)
        nfull = (qi * tq) // tk

        def block(k0, r0, carry, masked):
            nr = tq - r0
            kb = k_ref[pl.ds(k0, tk), :]
            vb = v_ref[pl.ds(k0, tk), :]
            causal = (k0 + _iota((nr, tk), 1)) < (qi * tq + r0 + _iota((nr, tk), 0)) if masked else None
            _, _, lbs, a_s = _sb_scores([q[r0:] for q in qh], kb, causal, tmat2, [carry[h][0][r0:] for h in range(2)])
            pv = _dot(jnp.concatenate([a.astype(BF16) for a in a_s], axis=0), vb)
            return tuple((_put_rows(carry[h][0], carry[h][0][r0:] + jnp.sum(lbs[h], axis=1, keepdims=True), r0),
                          _put_rows(carry[h][1], carry[h][1][r0:] + pv[h * nr:(h + 1) * nr], r0)) for h in range(2))

        carry = tuple((jnp.zeros((tq, 1), F32), jnp.zeros((tq, LANES), F32)) for _ in range(2))
        for off, size in reversed(_diag_tiles(tq)):
            assert size == tk
            carry = block(pl.multiple_of(qi * tq + off, tk), off, carry, True)
        (_, acca), (_, accb) = _sb_walk_left(nfull, tk, block, carry, lambda c: (c[0][0], c[1][0]))
        o = jnp.where(ma, acca, accb)
        o_ref[...] = o
        sg = sg_ref[...]
        sm_ref[...] = (o * (sg * _sigmoid(sg))).astype(BF16)

    qblk = pl.BlockSpec((tq, LANES), lambda p, i: (i, p))
    kvblk = pl.BlockSpec((S, LANES), lambda p, i: (0, p))
    return pl.pallas_call(
        body, name="sb_fwd", grid=(npair, S // tq),
        in_specs=[qblk, kvblk, kvblk, pl.BlockSpec((tq, LANES), lambda p, i: (i, C_SG // LANES + p))],
        out_specs=[qblk, qblk],
        out_shape=[jax.ShapeDtypeStruct((S, SB_W), F32), jax.ShapeDtypeStruct((S, SB_W), BF16)],
        compiler_params=_cparams(dimension_semantics=("arbitrary", "arbitrary")),
    )(sq, sk, sv, projm)


def _outproj(x, fm, pm, sm, w_out, layer, *, tm):
    S, D = x.shape

    def body(x_ref, fm_ref, pm_ref, sm_ref, w_ref, y_ref):
        y = x_ref[...] + _dot(fm_ref[...], w_ref[0:FOX_W, :])
        y = y + _dot(pm_ref[...], w_ref[FOX_W:FOX_W + POOL_W, :])
        y_ref[...] = y + _dot(sm_ref[...], w_ref[FOX_W + POOL_W:D_MIX, :])

    row = lambda w: pl.BlockSpec((tm, w), lambda i: (i, 0))
    return pl.pallas_call(
        body, name="outproj", grid=(S // tm,),
        in_specs=[row(D), row(FOX_W), row(POOL_W), row(SB_W), pl.BlockSpec((None, D_MIX, D), lambda i: (layer, 0, 0))],
        out_specs=row(D), out_shape=jax.ShapeDtypeStruct((S, D), F32),
        compiler_params=_cparams(dimension_semantics=("arbitrary",)),
    )(x, fm, pm, sm, w_out)


def _loss_head(y, target, *, tm):
    S, D = y.shape

    def body(y_ref, t_ref, dy_ref, sq_ref):
        @pl.when(pl.program_id(0) == 0)
        def _():
            sq_ref[...] = jnp.zeros_like(sq_ref)

        d = y_ref[...] - t_ref[...]
        dy_ref[...] = d * (1.0 / D)
        sq_ref[...] += jnp.sum(d * d, axis=0, keepdims=True)

    row = pl.BlockSpec((tm, D), lambda i: (i, 0))
    return pl.pallas_call(
        body, name="loss_head", grid=(S // tm,),
        in_specs=[row, row], out_specs=[row, pl.BlockSpec((1, D), lambda i: (0, 0))],
        out_shape=[jax.ShapeDtypeStruct((S, D), F32), jax.ShapeDtypeStruct((1, D), F32)],
        compiler_params=_cparams(dimension_semantics=("arbitrary",)),
    )(y, target)


def _outproj_bwd(dy, fm, pm, sm, w_out, layer, stacks, *, tm):
    S, D = dy.shape

    def body(dy_ref, fm_ref, pm_ref, sm_ref, w_ref, dm_ref, dw_ref):
        @pl.when(pl.program_id(0) == 0)
        def _():
            dw_ref[...] = jnp.zeros_like(dw_ref)

        dyb = dy_ref[...].astype(BF16)
        dm_ref[...] = _dot_nt(dyb, w_ref[...])
        dw_ref[0:FOX_W, :] += _dot_tn(fm_ref[...], dyb)
        dw_ref[FOX_W:FOX_W + POOL_W, :] += _dot_tn(pm_ref[...], dyb)
        dw_ref[FOX_W + POOL_W:D_MIX, :] += _dot_tn(sm_ref[...], dyb)

    row = lambda w: pl.BlockSpec((tm, w), lambda i: (i, 0))
    wspec = pl.BlockSpec((None, D_MIX, D), lambda i: (layer, 0, 0))
    return _stack_call(
        body, "outproj_bwd", (S // tm,), [row(D), row(FOX_W), row(POOL_W), row(SB_W), wspec], (dy, fm, pm, sm, w_out),
        [pl.BlockSpec((None, D_MIX, D), lambda i: (layer, 0, 0))], [(D_MIX, D)], stacks,
        plain_specs=[row(D_MIX)], plain_shapes=[jax.ShapeDtypeStruct((S, D_MIX), F32)],
        compiler_params=_cparams(dimension_semantics=("arbitrary",)))


def _fox_bwd(qn, ka, kb, v, o, lse, dmix, projm, qkb, *, tq, tk):
    S = qn.shape[0]
    npair = FOX_HEADS // 2

    def body(q_ref, ka_ref, kb_ref, v_ref, o_ref, lse_ref, dm_ref, fg_ref, qkb_ref,
             dq_ref, dk_ref, dv_ref, dfg_ref, dct_ref, dcr_ref):
        qi = pl.program_id(1)

        @pl.when(qi == 0)
        def _():
            dk_ref[...] = jnp.zeros_like(dk_ref)
            dv_ref[...] = jnp.zeros_like(dv_ref)
            dct_ref[...] = jnp.zeros_like(dct_ref)

        lane = _iota((1, LANES), 1)
        ma = lane < HEAD_DIM
        qh = _pair_masks(q_ref[...])
        qaug = _aug_queries(q_ref[...])
        k_refs = (ka_ref, kb_ref)
        lsev = lse_ref[...]
        lse = (_lane_pick(lsev, lane, 0), _lane_pick(lsev, lane, HEAD_DIM))
        fg = fg_ref[...]
        silu, dsilu = _silu_pair(fg)
        dm = dm_ref[...]
        ov = o_ref[...]
        do = dm * silu
        dfg_ref[...] = dm * ov * dsilu
        dd = do * ov
        dsum = (jnp.sum(jnp.where(ma, dd, 0.0), axis=1, keepdims=True), jnp.sum(jnp.where(ma, 0.0, dd), axis=1, keepdims=True))
        doh = _pair_masks(do.astype(BF16))

        def block(k0, tkl, r0, carry, masked):
            vb = v_ref[pl.ds(k0, tkl), :]
            if masked:
                mask = (k0 + _iota((tq - r0, tkl), 1)) <= (qi * tq + r0 + _iota((tq - r0, tkl), 0))
            heads = range(2)
            kaugs = [k_refs[h][pl.ds(k0, tkl), :] for h in heads]
            scores = [_dot_nt(qaug[h][r0:], kaugs[h]) for h in heads]
            dps = [_dot_nt(doh[h][r0:], vb) for h in heads]
            ps, dss, rows = [], [], []
            for h in heads:
                s = jnp.where(mask, scores[h], NEG) if masked else scores[h]
                p = jnp.exp(s - lse[h][r0:])
                dsf = p * (dps[h] - dsum[h][r0:])
                dct_ref[0, h:h + 1, pl.ds(k0, tkl)] -= jnp.sum(dsf, axis=0, keepdims=True)
                rows.append(_put_rows(carry[1 + h], carry[1 + h][r0:] + jnp.sum(dsf, axis=1, keepdims=True), r0))
                ps.append(p.astype(BF16))
                dss.append(dsf.astype(BF16))
            dv_ref[pl.ds(k0, tkl), :] += _dot_tn(jnp.concatenate(ps, axis=0), jnp.concatenate([d[r0:] for d in doh], axis=0))
            dk_ref[pl.ds(k0, tkl), :] += _dot_tn(jnp.concatenate(dss, axis=0), jnp.concatenate([q[r0:] for q in qh], axis=0))
            kh = jnp.concatenate([_pair_masks(kaugs[h])[h] for h in heads], axis=0)
            dq = _put_rows(carry[0], carry[0][r0:] + _dot(jnp.concatenate(dss, axis=1), kh), r0)
            return (dq, rows[0], rows[1])

        zcol = jnp.zeros((tq, 1), F32)
        carry = (jnp.zeros((tq, LANES), F32), zcol, zcol)
        for off, size in _diag_tiles(tq):
            carry = block(pl.multiple_of(qi * tq + off, size), size, off, carry, True)
        floors = (jnp.min(lse[0]), jnp.min(lse[1]))
        dq, rowa, rowb = _fox_walk_left((qi * tq) // tk, tk, block, carry, k_refs, jnp.max(qkb_ref[...]), lambda c: floors)
        dq_ref[...] = dq * QK_SCALE
        dcr_ref[0] = jnp.where(ma, rowa, rowb)

    qblk = pl.BlockSpec((tq, LANES), lambda p, i: (i, p))
    kvblk = pl.BlockSpec((S, LANES), lambda p, i: (0, p))
    f32out = jax.ShapeDtypeStruct((S, FOX_W), F32)
    ctblk = pl.BlockSpec((1, FF_STRIDE, S), lambda p, i: (p, 0, 0))
    return pl.pallas_call(
        body, name="fox_bwd", grid=(npair, S // tq),
        in_specs=[qblk, kvblk, kvblk, kvblk, qblk, qblk, qblk,
                  pl.BlockSpec((tq, LANES), lambda p, i: (i, C_FG // LANES + p)),
                  pl.BlockSpec((1, LANES), lambda p, i: (0, 0))],
        out_specs=[qblk, kvblk, kvblk, qblk, ctblk, pl.BlockSpec((1, tq, LANES), lambda p, i: (p, i, 0))],
        out_shape=[f32out, f32out, f32out, f32out, jax.ShapeDtypeStruct((npair, FF_STRIDE, S), F32),
                   jax.ShapeDtypeStruct((npair, S, LANES), F32)],
        compiler_params=_cparams(dimension_semantics=("arbitrary", "arbitrary")),
    )(qn, ka, kb, v, o, lse, dmix, projm, qkb)


def _sb_bwd(sq, sk, sv, o, dmix, projm, *, tq, tk):
    S = sq.shape[0]
    npair = SB_HEADS // 2
    mix0 = (FOX_W + POOL_W) // LANES

    def body(q_ref, k_ref, v_ref, o_ref, dm_ref, sg_ref, dq_ref, dk_ref, dv_ref, dsg_ref):
        qi = pl.program_id(1)

        @pl.when(qi == 0)
        def _():
            dk_ref[...] = jnp.zeros_like(dk_ref)
            dv_ref[...] = jnp.zeros_like(dv_ref)

        lane = _iota((1, LANES), 1)
        ma = lane < HEAD_DIM
        qh = _pair_masks(q_ref[...])
        sg = sg_ref[...]
        silu, dsilu = _silu_pair(sg)
        dm = dm_ref[...]
        ov = o_ref[...]
        do = dm * silu
        dsg_ref[...] = dm * ov * dsilu
        dob = do.astype(BF16)
        dd = dob.astype(F32) * ov
        dsum = (jnp.sum(jnp.where(ma, dd, 0.0), axis=1, keepdims=True), jnp.sum(jnp.where(ma, 0.0, dd), axis=1, keepdims=True))
        doh = _pair_masks(dob)
        tmat2 = _suffix_matrix(tk, inclusive=False)
        tmat2_inc = _suffix_matrix(tk, inclusive=True)
        nfull = (qi * tq) // tk

        def block(k0, r0, carry, masked):
            nr = tq - r0
            kb = k_ref[pl.ds(k0, tk), :]
            vb = v_ref[pl.ds(k0, tk), :]
            kh = _pair_masks(kb)
            causal = (k0 + _iota((nr, tk), 1)) < (qi * tq + r0 + _iota((nr, tk), 0)) if masked else None
            heads = range(2)
            qs = [q[r0:] for q in qh]
            dos = [d[r0:] for d in doh]
            das = [_dot_nt(dos[h], vb) for h in heads]
            zs, nsps, lbs, a_s = _sb_scores(qs, kb, causal, tmat2, [carry[h][0][r0:] for h in heads])
            abs_ = [a.astype(BF16) for a in a_s]
            us = [abs_[h].astype(F32) * das[h] for h in heads]
            uins = [_suffix_sums(u, tmat2_inc) for u in us]
            dzs = []
            for h in heads:
                cum_u = dsum[h][r0:] - (uins[h] + carry[h][1][r0:])
                dz = us[h] * jnp.exp(nsps[h]) - jnp.exp(zs[h] + nsps[h]) * cum_u
                if masked:
                    dz = jnp.where(causal, dz, 0.0)
                dzs.append(dz.astype(BF16))
            dv_ref[pl.ds(k0, tk), :] += _dot_tn(jnp.concatenate(abs_, axis=0), jnp.concatenate(dos, axis=0))
            dk_ref[pl.ds(k0, tk), :] += _dot_tn(jnp.concatenate(dzs, axis=0), jnp.concatenate(qs, axis=0))
            dq = _put_rows(carry[2], carry[2][r0:] + _dot(jnp.concatenate(dzs, axis=1), jnp.concatenate(kh, axis=0)), r0)
            new = [(_put_rows(carry[h][0], carry[h][0][r0:] + jnp.sum(lbs[h], axis=1, keepdims=True), r0),
                    _put_rows(carry[h][1], carry[h][1][r0:] + jnp.sum(us[h], axis=1, keepdims=True), r0)) for h in heads]
            return (new[0], new[1], dq)

        zcol = jnp.zeros((tq, 1), F32)
        carry = ((zcol, zcol), (zcol, zcol), jnp.zeros((tq, LANES), F32))
        for off, size in reversed(_diag_tiles(tq)):
            assert size == tk
            carry = block(pl.multiple_of(qi * tq + off, tk), off, carry, True)
        dq = _sb_walk_left(nfull, tk, block, carry, lambda c: (c[0][0], c[1][0]))[2]
        dq_ref[...] = dq * QK_SCALE

    qblk = pl.BlockSpec((tq, LANES), lambda p, i: (i, p))
    kvblk = pl.BlockSpec((S, LANES), lambda p, i: (0, p))
    f32out = jax.ShapeDtypeStruct((S, SB_W), F32)
    return pl.pallas_call(
        body, name="sb_bwd", grid=(npair, S // tq),
        in_specs=[qblk, kvblk, kvblk, qblk,
                  pl.BlockSpec((tq, LANES), lambda p, i: (i, mix0 + p)),
                  pl.BlockSpec((tq, LANES), lambda p, i: (i, C_SG // LANES + p))],
        out_specs=[qblk, kvblk, kvblk, qblk],
        out_shape=[f32out, f32out, f32out, f32out],
        compiler_params=_cparams(dimension_semantics=("arbitrary", "arbitrary")),
    )(sq, sk, sv, o, dmix, projm)


def _prep_bwd(projm, ffo, dqn, dkn, dct, dcr, dv, dfg, dsq, dsk, dsv, dsg, dmix, pooled, yp, qg, kg, bfp, wpd, ps, *, ts):
    S = projm.shape[0]
    nb = S // ts
    hb = ts // POOL_HALO
    npair = FOX_HEADS // 2
    last_halo = S // POOL_HALO - 1

    def body(fq_ref, fk_ref, pp_ref, pph_ref, ff_ref,
             dqn_ref, dkn_ref, dct_ref, dcr_ref, dv_ref, dfg_ref, dsq_ref, dsk_ref, dsv_ref, dsg_ref,
             dmp_ref, dmh_ref, pooled_ref, yp_ref, qg_ref, kg_ref, bf_ref, wpd_ref, ps_ref,
             dp_ref, dqg_ref, dkg_ref, dbf_ref, dwp_ref, dps_ref,
             carry_ref, dl_ref, buf_ref, dct_s):
        i = pl.program_id(0)
        blk = nb - 1 - i

        @pl.when(i == 0)
        def _():
            carry_ref[...] = jnp.zeros_like(carry_ref)
            dqg_ref[...] = jnp.zeros_like(dqg_ref)
            dkg_ref[...] = jnp.zeros_like(dkg_ref)
            dbf_ref[...] = jnp.zeros_like(dbf_ref)
            dwp_ref[...] = jnp.zeros_like(dwp_ref)
            dps_ref[...] = jnp.zeros_like(dps_ref)

        bd = _head_blockdiag()
        for raw_ref, g_ref, dn, dg_ref, col in ((fq_ref, qg_ref, dqn_ref[...], dqg_ref, C_FQ), (fk_ref, kg_ref, dkn_ref[...], dkg_ref, C_FK)):
            q = raw_ref[...]
            rstd = lax.rsqrt(_group_sum(q * q, bd) * (1.0 / HEAD_DIM) + EPS)
            xhat = q * rstd
            dg_ref[...] += jnp.sum(dn * xhat, axis=0, keepdims=True)
            dyg = dn * g_ref[...]
            mean = _group_sum(dyg * xhat, bd) * (1.0 / HEAD_DIM)
            dp_ref[:, col:col + FOX_W] = (rstd * (dyg - xhat * mean)).astype(BF16)
        dp_ref[:, C_FV:C_FV + FOX_W] = dv_ref[...].astype(BF16)
        dp_ref[:, C_FG:C_FG + FOX_W] = dfg_ref[...].astype(BF16)
        dp_ref[:, C_SQ:C_SQ + SB_W] = dsq_ref[...].astype(BF16)
        dp_ref[:, C_SK:C_SK + SB_W] = dsk_ref[...].astype(BF16)
        dp_ref[:, C_SV:C_SV + SB_W] = dsv_ref[...].astype(BF16)
        dp_ref[:, C_SG:C_SG + SB_W] = dsg_ref[...].astype(BF16)

        dct_s[...] = jnp.zeros_like(dct_s)
        for p in range(npair):
            dct_s[FF_STRIDE * p:FF_STRIDE * (p + 1), :] = dct_ref[p]
        dc = dct_s[...].T
        lane = _iota((1, LANES), 1)
        for p in range(npair):
            dcr = dcr_ref[p]
            dc = dc + jnp.where(lane == FF_STRIDE * p, _lane_pick(dcr, lane, 0), 0.0)
            dc = dc + jnp.where(lane == FF_STRIDE * p + 1, _lane_pick(dcr, lane, HEAD_DIM), 0.0)
        triu = _ones_where(_iota((ts, ts), 1) >= _iota((ts, ts), 0))
        dlf = _dot_exact_lhs(triu, dc) + carry_ref[...]
        dl_ref[...] = dlf
        carry_ref[...] = dl_ref[0:1, :]
        z = ff_ref[...] + bf_ref[...]
        dff = dlf * (1.0 / (1.0 + jnp.exp(z)))
        dbf_ref[...] += jnp.sum(dff, axis=0, keepdims=True)
        dp_ref[:, PM:PW] = dff.astype(BF16)

        psv = ps_ref[...]
        wpdv = wpd_ref[...]
        lane_group = _iota((1, POOL_W), 1) >> 6
        wlen = _pool_group_select(lane_group, [float(w) for w in POOL_WINDOWS])
        pg = pp_ref[:, POOL_W:2 * POOL_W]
        silu, dsilu = _silu_pair(pg)
        dmp = dmp_ref[...]
        ypv = yp_ref[...]
        dp_ref[:, C_PG:C_PG + POOL_W] = (dmp * (ypv * psv) * dsilu).astype(BF16)
        dps_ref[...] += jnp.sum(dmp * silu * ypv, axis=0, keepdims=True)
        dyp = (dmp * psv * silu).astype(BF16)
        dwp_ref[...] += _dot_tn(pooled_ref[...], dyp)
        dpooled = _dot_nt(dyp, wpdv)
        pgh = pph_ref[:, POOL_W:2 * POOL_W]
        dyph = (dmh_ref[...] * psv * (pgh * _sigmoid(pgh))).astype(BF16)
        dpooled_h = jnp.where(blk < nb - 1, _dot_nt(dyph, wpdv), 0.0)
        tpos = (blk * ts + _iota((ts, 1), 0) + 1).astype(F32)
        ev = dpooled / jnp.minimum(tpos, wlen)
        buf_ref[0:ts, :] = ev
        buf_ref[ts:ts + POOL_HALO, :] = dpooled_h / wlen
        acc = ev
        snaps = []
        for d in range(1, POOL_HALO):
            acc = acc + buf_ref[pl.ds(d, ts), :]
            if d + 1 in POOL_WINDOWS:
                snaps.append(acc)
        dp_ref[:, C_PX:C_PX + POOL_W] = (_pool_group_select(lane_group, snaps) - dpooled).astype(BF16)

    rblk = lambda w, c: pl.BlockSpec((ts, w), lambda i: (nb - 1 - i, c))
    full = lambda a: pl.BlockSpec(a.shape, lambda i: (0,) * a.ndim)
    halo = lambda w, c: pl.BlockSpec((POOL_HALO, w), lambda i: (jnp.minimum((nb - i) * hb, last_halo), c))
    acc_spec = lambda r, w: pl.BlockSpec((r, w), lambda i: (0, 0))
    return pl.pallas_call(
        body, name="prep_bwd", grid=(nb,),
        in_specs=[rblk(FOX_W, C_FQ // FOX_W), rblk(FOX_W, C_FK // FOX_W), rblk(2 * POOL_W, C_PX // (2 * POOL_W)),
                  halo(2 * POOL_W, C_PX // (2 * POOL_W)), rblk(LANES, 0),
                  rblk(FOX_W, 0), rblk(FOX_W, 0), pl.BlockSpec((npair, FF_STRIDE, ts), lambda i: (0, 0, nb - 1 - i)),
                  pl.BlockSpec((npair, ts, LANES), lambda i: (0, nb - 1 - i, 0)), rblk(FOX_W, 0), rblk(FOX_W, 0),
                  rblk(SB_W, 0), rblk(SB_W, 0), rblk(SB_W, 0), rblk(SB_W, 0),
                  rblk(POOL_W, FOX_W // POOL_W), halo(POOL_W, FOX_W // POOL_W), rblk(POOL_W, 0), rblk(POOL_W, 0),
                  full(qg), full(kg), full(bfp), full(wpd), full(ps)],
        out_specs=[rblk(PW, 0), acc_spec(1, FOX_W), acc_spec(1, FOX_W), acc_spec(1, LANES), acc_spec(POOL_W, POOL_W), acc_spec(1, POOL_W)],
        out_shape=[jax.ShapeDtypeStruct((S, PW), BF16), jax.ShapeDtypeStruct((1, FOX_W), F32), jax.ShapeDtypeStruct((1, FOX_W), F32),
                   jax.ShapeDtypeStruct((1, LANES), F32), jax.ShapeDtypeStruct((POOL_W, POOL_W), F32), jax.ShapeDtypeStruct((1, POOL_W), F32)],
        scratch_shapes=[pltpu.VMEM((1, LANES), F32), pltpu.VMEM((ts, LANES), F32), pltpu.VMEM((ts + POOL_HALO, POOL_W), F32),
                        pltpu.VMEM((LANES, ts), F32)],
        compiler_params=_cparams(dimension_semantics=("arbitrary",)),
    )(projm, projm, projm, projm, ffo, dqn, dkn, dct, dcr, dv, dfg, dsq, dsk, dsv, dsg, dmix, dmix, pooled, yp, qg, kg, bfp, wpd, ps)


def _stack_call(body, name, grid, in_specs, operands, slot_specs, slot_shapes, stacks, plain_specs=(), plain_shapes=(), **kw):
    out_specs = list(plain_specs) + list(slot_specs)
    out_shape = list(plain_shapes) + [jax.ShapeDtypeStruct((DEPTH,) + s, F32) for s in slot_shapes]
    if stacks is None:
        return pl.pallas_call(body, name=name, grid=grid, in_specs=in_specs, out_specs=out_specs, out_shape=out_shape, **kw)(*operands)
    n = len(operands)

    def aliased_body(*refs):
        body(*refs[:n], *refs[n + len(stacks):])

    return pl.pallas_call(
        aliased_body, name=name, grid=grid, in_specs=list(in_specs) + [pl.BlockSpec(memory_space=pl.ANY)] * len(stacks),
        out_specs=out_specs, out_shape=out_shape,
        input_output_aliases={n + k: len(plain_specs) + k for k in range(len(stacks))}, **kw)(*operands, *stacks)


def _inproj_dw(h, dproj, layer, stacks, *, ts, tn):
    S, D = h.shape
    nj = PM // tn

    def body(h_ref, dp_ref, dpf_ref, dw_ref, dwf_ref):
        s = pl.program_id(1)

        @pl.when(s == 0)
        def _():
            dw_ref[...] = jnp.zeros_like(dw_ref)

        @pl.when((s == 0) & (pl.program_id(0) == 0))
        def _():
            dwf_ref[...] = jnp.zeros_like(dwf_ref)

        hv = h_ref[...]
        dw_ref[...] += _dot_tn(dp_ref[...], hv)

        @pl.when(pl.program_id(0) == 0)
        def _():
            dwf_ref[...] += _dot_tn(dpf_ref[...], hv)

    return _stack_call(
        body, "inproj_dw", (nj, S // ts),
        [pl.BlockSpec((ts, D), lambda j, s: (s, 0)),
         pl.BlockSpec((ts, tn), lambda j, s: (s, j)),
         pl.BlockSpec((ts, LANES), lambda j, s: (s, PM // LANES))],
        (h, dproj, dproj),
        [pl.BlockSpec((None, tn, D), lambda j, s: (layer, j, 0)), pl.BlockSpec((None, LANES, D), lambda j, s: (layer, 0, 0))],
        [(PM, D), (LANES, D)], stacks,
        compiler_params=_cparams(dimension_semantics=("arbitrary", "arbitrary")))


def _inproj_dx(dproj, wt_all, layer, x, g, dy, *, tm):
    S, D = x.shape

    def body(dp_ref, w_ref, x_ref, g_ref, dy_ref, dx_ref, dg_ref):
        @pl.when(pl.program_id(0) == 0)
        def _():
            dg_ref[...] = jnp.zeros_like(dg_ref)

        dh = _dot(dp_ref[...], w_ref[...])
        xf = x_ref[...]
        rstd = lax.rsqrt(jnp.mean(xf * xf, axis=-1, keepdims=True) + EPS)
        xhat = xf * rstd
        dg_ref[...] += jnp.sum(dh * xhat, axis=0, keepdims=True)
        dyg = dh * g_ref[...]
        mean = jnp.mean(dyg * xhat, axis=-1, keepdims=True)
        dx_ref[...] = rstd * (dyg - xhat * mean) + dy_ref[...]

    row = lambda w: pl.BlockSpec((tm, w), lambda i: (i, 0))
    return pl.pallas_call(
        body, name="inproj_dx", grid=(S // tm,),
        in_specs=[row(PW), pl.BlockSpec((None, PW, D), lambda i: (layer, 0, 0)), row(D), pl.BlockSpec((1, D), lambda i: (0, 0)), row(D)],
        out_specs=[row(D), pl.BlockSpec((1, D), lambda i: (0, 0))],
        out_shape=[jax.ShapeDtypeStruct((S, D), F32), jax.ShapeDtypeStruct((1, D), F32)],
        compiler_params=_cparams(dimension_semantics=("arbitrary",)),
    )(dproj, wt_all, x, g, dy)


def _adam_update(w, g, m, v):
    nm = ADAM_B1 * m + (1.0 - ADAM_B1) * g
    nv = ADAM_B2 * v + (1.0 - ADAM_B2) * (g * g)
    m_hat = nm / (1.0 - ADAM_B1 ** ADAM_STEP)
    v_hat = nv / (1.0 - ADAM_B2 ** ADAM_STEP)
    return -ADAM_LR * (m_hat / (jnp.sqrt(v_hat) + ADAM_EPS) + ADAM_WD * w), nm, nv


def _adamw(w, g, m, v):
    L, R, C = w.shape
    tr = R if R <= 512 else 256

    def body(w_ref, g_ref, m_ref, v_ref, d_ref, nm_ref, nv_ref):
        d_ref[...], nm_ref[...], nv_ref[...] = _adam_update(w_ref[...], g_ref[...], m_ref[...], v_ref[...])

    spec = pl.BlockSpec((1, tr, C), lambda l, i: (l, i, 0))
    shp = jax.ShapeDtypeStruct((L, R, C), F32)
    return pl.pallas_call(
        body, name="adamw", grid=(L, R // tr), in_specs=[spec] * 4, out_specs=[spec] * 3, out_shape=[shp] * 3,
        compiler_params=_cparams(dimension_semantics=("arbitrary", "arbitrary")),
    )(w, g, m, v)


def _adamw_nd(w, g, m, v):
    shape = w.shape
    view = (1,) + shape if w.ndim == 2 else (shape[0], -1, shape[-1])
    outs = _adamw(w.reshape(view), g.reshape(view), m.reshape(view), v.reshape(view))
    return tuple(o.reshape(shape) for o in outs)


FLIP_C = (0, 0, 1)
FLIP_X = (1, 0, 0)
FLIP_Y = (0, 1, 0)
FLIP_XY = (1, 1, 0)
MESH = pl.DeviceIdType.MESH


def _peer(flip):
    me = (lax.axis_index("x"), lax.axis_index("y"), lax.axis_index("c"))
    return tuple(1 - a if f else a for a, f in zip(me, flip))


def _exchange(name, arrays, flips):
    n = len(arrays)

    def body(*refs):
        srcs, dsts = refs[:n], refs[n:2 * n]
        send_sems, recv_sems = refs[2 * n:]
        copies = [pltpu.make_async_remote_copy(src_ref=srcs[k], dst_ref=dsts[k], send_sem=send_sems.at[k], recv_sem=recv_sems.at[k],
                                               device_id=_peer(flips[k]), device_id_type=MESH) for k in range(n)]
        for cp in copies:
            cp.start()
        for cp in copies:
            cp.wait()

    anyspec = pl.BlockSpec(memory_space=pl.ANY)
    return pl.pallas_call(
        body, name=name, in_specs=[anyspec] * n, out_specs=[anyspec] * n,
        out_shape=[jax.ShapeDtypeStruct(a.shape, a.dtype) for a in arrays],
        scratch_shapes=[pltpu.SemaphoreType.DMA((n,)), pltpu.SemaphoreType.DMA((n,))],
    )(*arrays)


def _exchange_add(name, x, flip):
    def body(x_ref, o_ref, buf_ref, send_sem, recv_sem):
        cp = pltpu.make_async_remote_copy(src_ref=x_ref, dst_ref=buf_ref, send_sem=send_sem, recv_sem=recv_sem,
                                          device_id=_peer(flip), device_id_type=MESH)
        cp.start()
        cp.wait()
        o_ref[...] = x_ref[...] + buf_ref[...]

    vspec = pl.BlockSpec(memory_space=pltpu.VMEM)
    return pl.pallas_call(
        body, name=name, in_specs=[vspec], out_specs=vspec, out_shape=jax.ShapeDtypeStruct(x.shape, x.dtype),
        scratch_shapes=[pltpu.VMEM(x.shape, x.dtype), pltpu.SemaphoreType.DMA, pltpu.SemaphoreType.DMA],
    )(x)


def _chip_index():
    return 2 * lax.axis_index("x") + lax.axis_index("y")


def _gather_weights(w_in_t, w_out):
    wi = w_in_t.astype(BF16)
    wo = jnp.swapaxes(w_out, 0, 1).astype(BF16)
    halves = (wi.shape[0] // 2, wo.shape[0] // 2)
    ARR = 2
    TO_X, TO_Y, ON_Y, ON_X, SIB_X, SIB_Y, SIB_D0, SIB_D1, OWN = [ARR * k for k in range(9)]
    n_sems = ARR * 9

    def body(wi_ref, wo_ref, gi_ref, go_ref, send_sems, recv_sems):
        c = lax.axis_index("c")
        j = _chip_index()
        srcs = (wi_ref, wo_ref)
        dsts = (gi_ref, go_ref)
        def cuts(core):
            return [(pl.ds(h * core, h), pl.ds(h * core, h // 2), pl.ds(h * core + h // 2, h - h // 2)) for h in halves]
        mine, theirs = cuts(c), cuts(1 - c)
        HALF, Q0, Q1 = 0, 1, 2

        def copy(idx, src, dst, flip):
            return pltpu.make_async_remote_copy(src_ref=src, dst_ref=dst, send_sem=send_sems.at[idx], recv_sem=recv_sems.at[idx],
                                                device_id=_peer(flip), device_id_type=MESH)

        def slot(a, shard, cut):
            return dsts[a].at[shard, cut]

        jx, jy, jd = j ^ 2, j ^ 1, j ^ 3
        sends = []

        def start(cp):
            cp.start()
            sends.append(cp)

        for a in range(ARR):
            start(copy(TO_X + a, srcs[a].at[mine[a][HALF]], slot(a, j, mine[a][HALF]), FLIP_X))
            start(copy(TO_Y + a, srcs[a].at[mine[a][HALF]], slot(a, j, mine[a][HALF]), FLIP_Y))
        own = [copy(OWN + a, srcs[a], dsts[a].at[j], FLIP_C) for a in range(ARR)]
        for cp in own:
            cp.start()
        for a in range(ARR):
            copy(TO_X + a, slot(a, jx, mine[a][HALF]), slot(a, jx, mine[a][HALF]), FLIP_X).wait_recv()
            start(copy(ON_Y + a, slot(a, jx, mine[a][Q0]), slot(a, jx, mine[a][Q0]), FLIP_Y))
            start(copy(SIB_X + a, slot(a, jx, mine[a][HALF]), slot(a, jx, mine[a][HALF]), FLIP_C))
        for a in range(ARR):
            copy(TO_Y + a, slot(a, jy, mine[a][HALF]), slot(a, jy, mine[a][HALF]), FLIP_Y).wait_recv()
            start(copy(ON_X + a, slot(a, jy, mine[a][Q1]), slot(a, jy, mine[a][Q1]), FLIP_X))
            start(copy(SIB_Y + a, slot(a, jy, mine[a][HALF]), slot(a, jy, mine[a][HALF]), FLIP_C))
        for a in range(ARR):
            copy(ON_Y + a, slot(a, jd, mine[a][Q0]), slot(a, jd, mine[a][Q0]), FLIP_Y).wait_recv()
            start(copy(SIB_D0 + a, slot(a, jd, mine[a][Q0]), slot(a, jd, mine[a][Q0]), FLIP_C))
        for a in range(ARR):
            copy(ON_X + a, slot(a, jd, mine[a][Q1]), slot(a, jd, mine[a][Q1]), FLIP_X).wait_recv()
            start(copy(SIB_D1 + a, slot(a, jd, mine[a][Q1]), slot(a, jd, mine[a][Q1]), FLIP_C))
        for a in range(ARR):
            for idx, shard, cut in ((SIB_X, jx, HALF), (SIB_Y, jy, HALF), (SIB_D0, jd, Q0), (SIB_D1, jd, Q1)):
                copy(idx + a, slot(a, shard, theirs[a][cut]), slot(a, shard, theirs[a][cut]), FLIP_C).wait_recv()
        for cp in own:
            cp.wait()
        for cp in sends:
            cp.wait_send()

    anyspec = pl.BlockSpec(memory_space=pl.ANY)
    gi, go = pl.pallas_call(
        body, name="gather_weights", in_specs=[anyspec] * 2, out_specs=[anyspec] * 2,
        out_shape=[jax.ShapeDtypeStruct((4,) + wi.shape, BF16), jax.ShapeDtypeStruct((4,) + wo.shape, BF16)],
        scratch_shapes=[pltpu.SemaphoreType.DMA((n_sems,)), pltpu.SemaphoreType.DMA((n_sems,))],
    )(wi, wo)
    w_in_t_full = gi.reshape((4 * wi.shape[0],) + wi.shape[1:])
    w_out_full = jnp.swapaxes(go.reshape((4 * wo.shape[0],) + wo.shape[1:]), 0, 1)
    return w_in_t_full, w_out_full


def _to_aligned(w_t):
    _, L, D = w_t.shape
    npair = FOX_HEADS // 2
    ff = w_t[ORIG_FF:ORIG_REST].reshape(npair, 2, L, D)
    ff = jnp.pad(ff, ((0, 0), (0, FF_STRIDE - 2), (0, 0), (0, 0))).reshape(npair * FF_STRIDE, L, D)
    ff = jnp.pad(ff, ((0, LANES - npair * FF_STRIDE), (0, 0), (0, 0)))
    return jnp.swapaxes(jnp.concatenate([w_t[:ORIG_FOX], w_t[ORIG_REST:], ff], axis=0), 0, 1)


def _from_aligned(dw_t):
    n, _, D = dw_t.shape
    npair = FOX_HEADS // 2
    ff = dw_t[:, PM:PM + npair * FF_STRIDE].reshape(n, npair, FF_STRIDE, D)[:, :, :2].reshape(n, FOX_HEADS, D)
    return jnp.swapaxes(jnp.concatenate([dw_t[:, :ORIG_FOX], ff, dw_t[:, ORIG_FOX:PM]], axis=1), 0, 1)


def _half_layers(name, stack, got):
    L, R, C = stack.shape
    half = L // 2
    tr = min(256, R)
    c = lax.axis_index("c")
    which = ((1 - c) if got is None else c).astype(jnp.int32).reshape(1)

    def body(c_ref, x_ref, *refs):
        if got is None:
            refs[0][...] = x_ref[...].astype(BF16)
        else:
            acc = x_ref[...] + refs[0][...].astype(F32)
            refs[1][...] = acc
            refs[2][...] = acc.astype(BF16)

    plain = pl.BlockSpec((1, tr, C), lambda l, i, c_ref: (l, i, 0))
    picked = pl.BlockSpec((1, tr, C), lambda l, i, c_ref: (c_ref[0] * half + l, i, 0))
    shp = lambda dt: jax.ShapeDtypeStruct((half, R, C), dt)
    grid_spec = pltpu.PrefetchScalarGridSpec(
        num_scalar_prefetch=1, grid=(half, R // tr),
        in_specs=[picked] + ([] if got is None else [plain]), out_specs=[plain] if got is None else [plain, plain])
    return pl.pallas_call(
        body, name=name, grid_spec=grid_spec, out_shape=[shp(BF16)] if got is None else [shp(F32), shp(BF16)],
        compiler_params=_cparams(dimension_semantics=("arbitrary", "arbitrary")),
    )(which, stack, *([] if got is None else [got]))


def _reduce_scatter(stack_m, stack_f, stack_o, shard_cols, shard_rows):
    j = _chip_index()
    half = DEPTH // 2
    stacks = (stack_m, stack_f, stack_o)
    give = [_half_layers("rs_give", s, None)[0] for s in stacks]
    got = _exchange("rs_d2d", give, (FLIP_C,) * len(stacks))
    (m32, mbf), (f32_, fbf), (o32, obf) = [_half_layers("rs_add_chip", s, g) for s, g in zip(stacks, got)]
    d_model = stack_m.shape[2]

    def in_shards(m, f):
        return _from_aligned(jnp.concatenate([m, f], axis=1)).reshape(4, shard_cols, half, d_model)

    def out_shards(o):
        return jnp.moveaxis(o.reshape(half, 4, shard_rows, o.shape[-1]), 1, 0)

    chip = [(in_shards(m32, f32_), in_shards(mbf, fbf)), (out_shards(o32), out_shards(obf))]
    masks = (2, 1, 3)
    flips = (FLIP_X, FLIP_Y, FLIP_XY)
    sends, sflips = [], []
    for _, bf in chip:
        for m, fl in zip(masks, flips):
            sends.append(lax.dynamic_index_in_dim(bf, j ^ m, axis=0, keepdims=False))
            sflips.append(fl)
    got = _exchange("rs_ici", sends, tuple(sflips))
    own_in, own_out = [lax.dynamic_index_in_dim(f32_sum, j, axis=0, keepdims=False) for f32_sum, _ in chip]
    mine_in = _add_rows("rs_add_in", own_in, list(got[0:3]))
    mine_out = _add_into_half("rs_add_out", own_out, list(got[3:6]))
    sib_in, g_out = _share_halves(mine_in, mine_out)
    return (mine_in, sib_in), g_out


def _add_rows(name, first, others):
    n = len(others)

    def body(*refs):
        acc = refs[0][...]
        for r in refs[1:1 + n]:
            acc = acc + r[...].astype(F32)
        refs[1 + n][...] = acc

    grid, spec = _row_lane_blocks(first.shape)
    return pl.pallas_call(
        body, name=name, grid=grid, in_specs=[spec(first.shape[1])] * (1 + n), out_specs=spec(first.shape[1]),
        out_shape=jax.ShapeDtypeStruct(first.shape, F32),
        compiler_params=_cparams(dimension_semantics=("arbitrary", "arbitrary")),
    )(first, *others)


def _row_lane_blocks(shape):
    rows, _, C = shape
    tr = rows // 2 if rows % 2 == 0 and rows > 64 else rows
    return (rows // tr, C // LANES), lambda n_mid: pl.BlockSpec((tr, n_mid, LANES), lambda i, k, *_: (i, 0, k))


def _add_into_half(name, first, others):
    half, rows, C = first.shape
    tr = min(256, rows)
    n = len(others)

    def body(c_ref, *refs):
        acc = refs[0][...]
        for r in refs[1:1 + n]:
            acc = acc + r[...].astype(F32)
        refs[1 + n][...] = acc

    grid_spec = pltpu.PrefetchScalarGridSpec(
        num_scalar_prefetch=1, grid=(half, rows // tr),
        in_specs=[pl.BlockSpec((1, tr, C), lambda l, i, c_ref: (l, i, 0))] * (1 + n),
        out_specs=pl.BlockSpec((1, tr, C), lambda l, i, c_ref: (c_ref[0] * half + l, i, 0)))
    return pl.pallas_call(
        body, name=name, grid_spec=grid_spec, out_shape=jax.ShapeDtypeStruct((2 * half, rows, C), F32),
        compiler_params=_cparams(dimension_semantics=("arbitrary", "arbitrary")),
    )(lax.axis_index("c").astype(jnp.int32).reshape(1), first, *others)


def _share_halves(mine, buf):
    half = DEPTH // 2

    def body(mine_ref, buf_in, sib_ref, buf_ref, send_sems, recv_sems):
        lay = pl.ds(half * lax.axis_index("c"), half)
        copies = [pltpu.make_async_remote_copy(src_ref=src, dst_ref=dst, send_sem=send_sems.at[k], recv_sem=recv_sems.at[k],
                                               device_id=_peer(FLIP_C), device_id_type=MESH)
                  for k, (src, dst) in enumerate(((mine_ref, sib_ref), (buf_ref.at[lay], buf_ref.at[lay])))]
        for cp in copies:
            cp.start()
        for cp in copies:
            cp.wait()

    anyspec = pl.BlockSpec(memory_space=pl.ANY)
    return pl.pallas_call(
        body, name="rs_share", in_specs=[anyspec] * 2, out_specs=[anyspec] * 2,
        out_shape=[jax.ShapeDtypeStruct(mine.shape, mine.dtype), jax.ShapeDtypeStruct(buf.shape, buf.dtype)],
        input_output_aliases={1: 1},
        scratch_shapes=[pltpu.SemaphoreType.DMA((2,)), pltpu.SemaphoreType.DMA((2,))],
    )(mine, buf)


def _adamw_halves(w, g_mine, g_sib, m, v):
    half = g_mine.shape[1]

    def body(c_ref, w_ref, gm_ref, gs_ref, m_ref, v_ref, g_ref, d_ref, nm_ref, nv_ref):
        first = c_ref[0] == 0
        gm, gs = gm_ref[...], gs_ref[...]
        for h, gv in enumerate((jnp.where(first, gm, gs), jnp.where(first, gs, gm))):
            lay = slice(half * h, half * (h + 1))
            g_ref[:, lay, :] = gv
            d_ref[:, lay, :], nm_ref[:, lay, :], nv_ref[:, lay, :] = _adam_update(w_ref[:, lay, :], gv, m_ref[:, lay, :], v_ref[:, lay, :])

    grid, spec = _row_lane_blocks(w.shape)
    full, part = spec(w.shape[1]), spec(half)
    grid_spec = pltpu.PrefetchScalarGridSpec(num_scalar_prefetch=1, grid=grid, in_specs=[full, part, part, full, full], out_specs=[full] * 4)
    return pl.pallas_call(
        body, name="adamw_halves", grid_spec=grid_spec, out_shape=[jax.ShapeDtypeStruct(w.shape, F32)] * 4,
        compiler_params=_cparams(dimension_semantics=("arbitrary", "arbitrary")),
    )(lax.axis_index("c").astype(jnp.int32).reshape(1), w, g_mine, g_sib, m, v)


def _all_reduce_small(x):
    x = _exchange_add("ar_c", x, FLIP_C)
    x = _exchange_add("ar_y", x, FLIP_Y)
    return _exchange_add("ar_x", x, FLIP_X)


def _blocks(S):
    return dict(tm=min(512, S), tm_proj=min(1024, S), ts=min(512, S), tq=min(512, S), tq_big=min(1024, S), tk=min(512, S), tks=min(256, S))


def _pair_pad(vec):
    npair = FOX_HEADS // 2
    v = jnp.pad(vec.reshape(npair, 2), ((0, 0), (0, FF_STRIDE - 2))).reshape(1, npair * FF_STRIDE)
    return jnp.pad(v, ((0, 0), (0, LANES - npair * FF_STRIDE)))


def _pair_unpad(row):
    npair = FOX_HEADS // 2
    return row[0, :npair * FF_STRIDE].reshape(npair, FF_STRIDE)[:, :2].reshape(FOX_HEADS)


def _pool_blockdiag(w_pool):
    g, cg, _ = w_pool.shape
    eye = jnp.eye(g, dtype=w_pool.dtype)
    return jnp.einsum("gh,gcd->gchd", eye, w_pool).reshape(g * cg, g * cg)


QK_BOUND_SLACK = 1.05


def _layer_params(norm_g, b_f, q_norm_g, k_norm_g, w_pool, pool_scale):
    qk_bound = QK_BOUND_SLACK * HEAD_DIM * QK_SCALE * jnp.max(jnp.abs(q_norm_g)) * jnp.max(jnp.abs(k_norm_g))
    return dict(g=norm_g.reshape(1, -1), qg=jnp.tile(q_norm_g, FOX_HEADS).reshape(1, FOX_W), kg=jnp.tile(k_norm_g, FOX_HEADS).reshape(1, FOX_W),
                bfp=_pair_pad(b_f), wpd=_pool_blockdiag(w_pool).astype(BF16), ps=pool_scale.reshape(1, POOL_W),
                qkb=jnp.full((1, LANES), qk_bound, F32))


def _layer_fwd(x, wt_all, w_out, layer, prm, bs):
    projm, ffo, h = _inproj(x, prm["g"], wt_all, layer, tm=bs["tm_proj"], tn=PROJ_TN)
    qn, ka, kb, v, sq, sk, sv, pooled, yp, pm = _prep(projm, ffo, prm["qg"], prm["kg"], prm["bfp"], prm["wpd"], prm["ps"], ts=bs["ts"])
    o, lse, fm = _fox_fwd(qn, ka, kb, v, projm, prm["qkb"], tq=bs["tq"], tk=bs["tk"])
    so, sm = _sb_fwd(sq, sk, sv, projm, tq=bs["tq_big"], tk=bs["tks"])
    y = _outproj(x, fm, pm, sm, w_out, layer, tm=bs["tm"])
    saved = dict(x=x, projm=projm, ffo=ffo, h=h, qn=qn, ka=ka, kb=kb, v=v, sq=sq, sk=sk, sv=sv, pooled=pooled, yp=yp,
                 o=o, lse=lse, so=so, fm=fm, pm=pm, sm=sm)
    return y, saved


def _layer_bwd(dy, wt_all, w_out, prm, sv_, bs, layer, stacks):
    dmix, stack_o = _outproj_bwd(dy, sv_["fm"], sv_["pm"], sv_["sm"], w_out, layer, None if stacks is None else stacks[2:], tm=bs["tm"])
    dqn, dkn, dv, dfg, dct, dcr = _fox_bwd(sv_["qn"], sv_["ka"], sv_["kb"], sv_["v"], sv_["o"], sv_["lse"], dmix, sv_["projm"],
                                      prm["qkb"], tq=bs["tq_big"], tk=bs["tk"])
    dsq, dsk, dsv, dsg = _sb_bwd(sv_["sq"], sv_["sk"], sv_["sv"], sv_["so"], dmix, sv_["projm"], tq=bs["tq"], tk=bs["tks"])
    dproj, dqg, dkg, dbf, dwp, dps = _prep_bwd(sv_["projm"], sv_["ffo"], dqn, dkn, dct, dcr, dv, dfg, dsq, dsk, dsv, dsg, dmix,
                                               sv_["pooled"], sv_["yp"], prm["qg"], prm["kg"], prm["bfp"], prm["wpd"], prm["ps"], ts=bs["ts"])
    stack_m, stack_f = _inproj_dw(sv_["h"], dproj, layer, None if stacks is None else stacks[:2], ts=bs["tm_proj"], tn=PROJ_TN)
    dx, dg = _inproj_dx(dproj, wt_all, layer, sv_["x"], prm["g"], dy, tm=min(256, bs["tm"]))
    grads = dict(
        norm_g=dg[0],
        b_f=_pair_unpad(dbf), q_norm_g=dqg.reshape(FOX_HEADS, HEAD_DIM).sum(0), k_norm_g=dkg.reshape(FOX_HEADS, HEAD_DIM).sum(0),
        w_pool=jnp.stack([dwp[HEAD_DIM * g:HEAD_DIM * (g + 1), HEAD_DIM * g:HEAD_DIM * (g + 1)] for g in range(4)]),
        pool_scale=dps[0])
    return dx, grads, (stack_m, stack_f, stack_o)


def _local_step(x, target, wt_all, w_out, norm_g, b_f, q_norm_g, k_norm_g, w_pool, pool_scale):
    S, D = x.shape
    bs = _blocks(S)
    prms = [_layer_params(norm_g[l], b_f[l], q_norm_g[l], k_norm_g[l], w_pool[l], pool_scale[l]) for l in range(DEPTH)]
    saved = []
    y = x
    for l in range(DEPTH):
        y, s_ = _layer_fwd(y, wt_all, w_out, l, prms[l], bs)
        saved.append(s_)
    dy, sq = _loss_head(y, target, tm=bs["tm"])
    loss = 0.5 * jnp.sum(sq) / D
    grads = [None] * DEPTH
    stacks = None
    for l in reversed(range(DEPTH)):
        dy, grads[l], stacks = _layer_bwd(dy, wt_all, w_out, prms[l], saved[l], bs, l, stacks)
    stacked = {k: jnp.stack([g[k] for g in grads]) for k in grads[0]}
    return loss, dy, stacked, stacks


SMALL = ("norm_g", "b_f", "q_norm_g", "k_norm_g", "w_pool", "pool_scale")


def _pack_small(gr):
    flat = jnp.concatenate([gr[k].reshape(-1) for k in SMALL])
    pad = (-flat.shape[0]) % (8 * LANES)
    return jnp.pad(flat, (0, pad)).reshape(-1, LANES)


def _unpack_small(packed, like):
    flat = packed.reshape(-1)
    out, off = {}, 0
    for k in SMALL:
        n = like[k].size
        out[k] = flat[off:off + n].reshape(like[k].shape)
        off += n
    return out


def kernel(x, norm_g, w_in, b_f, q_norm_g, k_norm_g, w_pool, pool_scale, w_out, loss_target, m_norm_g, m_w_in, m_b_f, m_q_norm_g, m_k_norm_g, m_w_pool, m_pool_scale, m_w_out, v_norm_g, v_w_in, v_b_f, v_q_norm_g, v_k_norm_g, v_w_pool, v_pool_scale, v_w_out):
    weights = dict(norm_g=norm_g, w_in=w_in, b_f=b_f, q_norm_g=q_norm_g, k_norm_g=k_norm_g, w_pool=w_pool, pool_scale=pool_scale, w_out=w_out)
    mom_m = dict(norm_g=m_norm_g, w_in=m_w_in, b_f=m_b_f, q_norm_g=m_q_norm_g, k_norm_g=m_k_norm_g, w_pool=m_w_pool, pool_scale=m_pool_scale, w_out=m_w_out)
    mom_v = dict(norm_g=v_norm_g, w_in=v_w_in, b_f=v_b_f, q_norm_g=v_q_norm_g, k_norm_g=v_k_norm_g, w_pool=v_w_pool, pool_scale=v_pool_scale, w_out=v_w_out)
    shard_cols = w_in.shape[2]
    shard_rows = w_out.shape[1]

    cols_first = lambda a: jnp.transpose(a, (2, 0, 1))
    w_in_t = cols_first(w_in)
    w_in_t_full, w_out_full = _gather_weights(w_in_t, w_out)
    wt_all = _to_aligned(w_in_t_full)
    loss, dx, gr, stacks = _local_step(x[0], loss_target[0], wt_all, w_out_full, norm_g, b_f, q_norm_g, k_norm_g, w_pool, pool_scale)
    loss = lax.psum(loss, ("x", "y", "c"))

    (g_in_mine, g_in_sib), g_w_out = _reduce_scatter(*stacks, shard_cols, shard_rows)
    small = _unpack_small(_all_reduce_small(_pack_small(gr)), {k: weights[k] for k in SMALL})
    grad_w = dict(small, w_out=g_w_out)

    names = ("norm_g", "w_in", "b_f", "q_norm_g", "k_norm_g", "w_pool", "pool_scale", "w_out")
    upd = {k: _adamw_nd(weights[k], grad_w[k], mom_m[k], mom_v[k]) for k in names if k != "w_in"}
    in_t = _adamw_halves(w_in_t, g_in_mine, g_in_sib, cols_first(mom_m["w_in"]), cols_first(mom_v["w_in"]))
    grad_w["w_in"], *upd["w_in"] = [jnp.transpose(a, (1, 2, 0)) for a in in_t]
    return (loss, dx[None], *[grad_w[k] for k in names], *[upd[k][0] for k in names], *[upd[k][1] for k in names], *[upd[k][2] for k in names])
```

```python
import functools

import jax
import jax.numpy as jnp
from jax import lax
from jax.experimental import pallas as pl
from jax.experimental.pallas import tpu as pltpu

F32 = jnp.float32
BF16 = jnp.bfloat16

DEPTH = 4
HEAD_DIM = 64
FOX_HEADS = 8
SB_HEADS = 4
FOX_W = FOX_HEADS * HEAD_DIM
SB_W = SB_HEADS * HEAD_DIM
POOL_W = 256
POOL_WINDOWS = (2, 4, 8, 16)
POOL_HALO = 16
D_MIX = FOX_W + POOL_W + SB_W
EPS = 1e-6
NEG = -1e30
QK_SCALE = HEAD_DIM ** -0.5

ORIG_FOX = 4 * FOX_W
ORIG_FF = ORIG_FOX
ORIG_REST = ORIG_FF + FOX_HEADS
D_IN = ORIG_REST + 2 * POOL_W + 4 * SB_W

C_FQ, C_FK, C_FV, C_FG = 0, FOX_W, 2 * FOX_W, 3 * FOX_W
C_PX = 4 * FOX_W
C_PG = C_PX + POOL_W
C_SQ = C_PG + POOL_W
C_SK, C_SV, C_SG = C_SQ + SB_W, C_SQ + 2 * SB_W, C_SQ + 3 * SB_W
PM = C_SG + SB_W
LANES = 128
PW = PM + LANES
FF_STRIDE = 8
AUG = 3

ADAM_LR = 0.001
ADAM_B1 = 0.9
ADAM_B2 = 0.999
ADAM_EPS = 1e-08
ADAM_WD = 0.01
ADAM_STEP = 10

VMEM_LIMIT = 48 * 1024 * 1024
PROJ_TN = PM // 2


def _cparams(**kw):
    return pltpu.CompilerParams(vmem_limit_bytes=VMEM_LIMIT, **kw)


def _dot(a, b):
    return jnp.dot(a, b, preferred_element_type=F32)


def _dot_nt(a, b):
    return lax.dot_general(a, b, (((1,), (1,)), ((), ())), preferred_element_type=F32)


def _dot_tn(a, b):
    return lax.dot_general(a, b, (((0,), (0,)), ((), ())), preferred_element_type=F32)


def _split2(x):
    hi = x.astype(BF16)
    lo = (x - hi.astype(F32)).astype(BF16)
    return hi, lo


def _split3(x):
    hi = x.astype(BF16)
    r = x - hi.astype(F32)
    mid = r.astype(BF16)
    lo = (r - mid.astype(F32)).astype(BF16)
    return hi, mid, lo


def _dot_exact_rhs(x, m):
    hi, mid, lo = _split3(x)
    return _dot(hi, m) + _dot(mid, m) + _dot(lo, m)


def _dot_exact_lhs(m, x):
    hi, mid, lo = _split3(x)
    return _dot(m, hi) + _dot(m, mid) + _dot(m, lo)


def _sigmoid(x):
    return 1.0 / (1.0 + jnp.exp(-x))


def _silu_pair(x):
    s = _sigmoid(x)
    return x * s, s * (1.0 + x * (1.0 - s))


def _iota(shape, dim):
    return lax.broadcasted_iota(jnp.int32, shape, dim)


def _ones_where(cond):
    return jnp.where(cond, 1.0, 0.0).astype(BF16)


GROUP_SLAB = 256


def _head_blockdiag():
    rows, cols = _iota((2 * GROUP_SLAB, GROUP_SLAB), 0) & (GROUP_SLAB - 1), _iota((2 * GROUP_SLAB, GROUP_SLAB), 1)
    return _ones_where((rows >> 6) == (cols >> 6))


def _group_sum(x, bd):
    hi, lo = _split2(x)
    slabs = [_dot(jnp.concatenate([hi[:, s:s + GROUP_SLAB], lo[:, s:s + GROUP_SLAB]], axis=1), bd) for s in range(0, x.shape[1], GROUP_SLAB)]
    return jnp.concatenate(slabs, axis=1)


def _lane_pick(x, lane_idx, lane):
    return jnp.sum(jnp.where(lane_idx == lane, x, 0.0), axis=1, keepdims=True)


def _inproj(x, g, wt_all, layer, *, tm, tn):
    S, D = x.shape
    nj = PM // tn

    def body(x_ref, g_ref, w_ref, wff_ref, proj_ref, ff_ref, h_ref):
        @pl.when(pl.program_id(1) == 0)
        def _():
            xf = x_ref[...]
            ms = jnp.mean(xf * xf, axis=-1, keepdims=True)
            h = (xf * lax.rsqrt(ms + EPS) * g_ref[...]).astype(BF16)
            h_ref[...] = h
            ff_ref[...] = _dot_nt(h, wff_ref[...])

        proj_ref[...] = _dot_nt(h_ref[...], w_ref[...])

    return pl.pallas_call(
        body, name="inproj", grid=(S // tm, nj),
        in_specs=[pl.BlockSpec((tm, D), lambda i, j: (i, 0)),
                  pl.BlockSpec((1, D), lambda i, j: (0, 0)),
                  pl.BlockSpec((None, tn, D), lambda i, j: (layer, j, 0)),
                  pl.BlockSpec((None, LANES, D), lambda i, j: (layer, PM // LANES, 0))],
        out_specs=[pl.BlockSpec((tm, tn), lambda i, j: (i, j)),
                   pl.BlockSpec((tm, LANES), lambda i, j: (i, 0)),
                   pl.BlockSpec((tm, D), lambda i, j: (i, 0))],
        out_shape=[jax.ShapeDtypeStruct((S, PM), F32), jax.ShapeDtypeStruct((S, LANES), F32),
                   jax.ShapeDtypeStruct((S, D), BF16)],
        compiler_params=_cparams(dimension_semantics=("arbitrary", "arbitrary")),
    )(x, g, wt_all, wt_all)


def _pool_group_select(lane_group, vals):
    return jnp.where(lane_group == 0, vals[0], jnp.where(lane_group == 1, vals[1], jnp.where(lane_group == 2, vals[2], vals[3])))


def _prep(projm, ffo, qg, kg, bfp, wpd, ps, *, ts):
    S = projm.shape[0]
    nb = S // ts
    hb = ts // POOL_HALO

    def body(fq_ref, fk_ref, fv_ref, pp_ref, halo_ref, ff_ref, sq_ref, sk_ref, sv_ref,
             qg_ref, kg_ref, bf_ref, wpd_ref, ps_ref,
             qn_ref, ka_ref, kb_ref, v_ref, sqo_ref, sko_ref, svo_ref, pooled_ref, yp_ref, pm_ref,
             carry_ref, c_ref, buf_ref):
        i = pl.program_id(0)
        bd = _head_blockdiag()
        normed = []
        for src, g_ref in ((fq_ref, qg_ref), (fk_ref, kg_ref)):
            q = src[...]
            ss = _group_sum(q * q, bd)
            normed.append(q * lax.rsqrt(ss * (1.0 / HEAD_DIM) + EPS) * g_ref[...])
        qn_ref[...] = (normed[0] * QK_SCALE).astype(BF16)
        kn = normed[1]
        v_ref[...] = fv_ref[...].astype(BF16)
        sqo_ref[...] = (sq_ref[...] * QK_SCALE).astype(BF16)
        sko_ref[...] = sk_ref[...].astype(BF16)
        svo_ref[...] = sv_ref[...].astype(BF16)

        @pl.when(i == 0)
        def _():
            carry_ref[...] = jnp.zeros_like(carry_ref)

        z = ff_ref[...] + bf_ref[...]
        lf = jnp.minimum(z, 0.0) - jnp.log(1.0 + jnp.exp(-jnp.abs(z)))
        tri = _ones_where(_iota((ts, ts), 1) <= _iota((ts, ts), 0))
        c = _dot_exact_lhs(tri, lf) + carry_ref[...]
        c_ref[...] = c
        carry_ref[...] = c_ref[ts - 1:ts, :]
        parts = jnp.concatenate(_split3(-c), axis=1)
        row = _iota((AUG * LANES, FOX_W), 0)
        col = _iota((AUG * LANES, FOX_W), 1)
        part, src = row >> 7, row & (LANES - 1)
        pair, off = col >> 7, col & (LANES - 1)
        sel_a = _ones_where((src == FF_STRIDE * pair) & (off == HEAD_DIM + part))
        sel_b = _ones_where((src == FF_STRIDE * pair + 1) & (off == part))
        first_half = (_iota((1, FOX_W), 1) & HEAD_DIM) == 0
        ka_ref[...] = jnp.where(first_half, kn, _dot(parts, sel_a)).astype(BF16)
        kb_ref[...] = jnp.where(first_half, _dot(parts, sel_b), kn).astype(BF16)

        x = pp_ref[:, 0:POOL_W]
        pg = pp_ref[:, POOL_W:2 * POOL_W]
        halo = jnp.where(i > 0, halo_ref[:, 0:POOL_W], 0.0)
        buf_ref[0:POOL_HALO, :] = halo
        buf_ref[POOL_HALO:POOL_HALO + ts, :] = x
        acc = x
        snaps = []
        for d in range(1, POOL_HALO):
            acc = acc + buf_ref[pl.ds(POOL_HALO - d, ts), :]
            if d + 1 in POOL_WINDOWS:
                snaps.append(acc)
        lane_group = _iota((1, POOL_W), 1) >> 6
        wsum = _pool_group_select(lane_group, snaps)
        wlen = _pool_group_select(lane_group, [float(w) for w in POOL_WINDOWS])
        tpos = (i * ts + _iota((ts, 1), 0) + 1).astype(F32)
        pooled = wsum / jnp.minimum(tpos, wlen) - x
        pb = pooled.astype(BF16)
        pooled_ref[...] = pb
        yp = _dot(pb, wpd_ref[...])
        yp_ref[...] = yp
        pm_ref[...] = (yp * ps_ref[...] * (pg * _sigmoid(pg))).astype(BF16)

    blk = lambda w, c: pl.BlockSpec((ts, w), lambda i: (i, c))
    full = lambda a: pl.BlockSpec(a.shape, lambda i: (0,) * a.ndim)
    out_shapes = [
        jax.ShapeDtypeStruct((S, FOX_W), BF16), jax.ShapeDtypeStruct((S, FOX_W), BF16), jax.ShapeDtypeStruct((S, FOX_W), BF16),
        jax.ShapeDtypeStruct((S, FOX_W), BF16),
        jax.ShapeDtypeStruct((S, SB_W), BF16), jax.ShapeDtypeStruct((S, SB_W), BF16), jax.ShapeDtypeStruct((S, SB_W), BF16),
        jax.ShapeDtypeStruct((S, POOL_W), BF16), jax.ShapeDtypeStruct((S, POOL_W), F32), jax.ShapeDtypeStruct((S, POOL_W), BF16),
    ]
    out_specs = [
        blk(FOX_W, 0), blk(FOX_W, 0), blk(FOX_W, 0), blk(FOX_W, 0),
        blk(SB_W, 0), blk(SB_W, 0), blk(SB_W, 0),
        blk(POOL_W, 0), blk(POOL_W, 0), blk(POOL_W, 0),
    ]
    return pl.pallas_call(
        body, name="prep", grid=(nb,),
        in_specs=[blk(FOX_W, C_FQ // FOX_W), blk(FOX_W, C_FK // FOX_W), blk(FOX_W, C_FV // FOX_W), blk(2 * POOL_W, C_PX // (2 * POOL_W)),
                  pl.BlockSpec((POOL_HALO, 2 * POOL_W), lambda i: (jnp.maximum(i * hb - 1, 0), C_PX // (2 * POOL_W))),
                  blk(LANES, 0),
                  blk(SB_W, C_SQ // SB_W), blk(SB_W, C_SK // SB_W), blk(SB_W, C_SV // SB_W),
                  full(qg), full(kg), full(bfp), full(wpd), full(ps)],
        out_specs=out_specs, out_shape=out_shapes,
        scratch_shapes=[pltpu.VMEM((1, LANES), F32), pltpu.VMEM((ts, LANES), F32), pltpu.VMEM((ts + POOL_HALO, POOL_W), F32)],
        compiler_params=_cparams(dimension_semantics=("arbitrary",)),
    )(projm, projm, projm, projm, projm, ffo, projm, projm, projm, qg, kg, bfp, wpd, ps)


def _pair_masks(x):
    ma = _iota((1, LANES), 1) < HEAD_DIM
    zero = jnp.zeros_like(x)
    return jnp.where(ma, x, zero), jnp.where(ma, zero, x)


DIAG_TILE = 256


def _diag_tiles(tq, size=DIAG_TILE):
    size = min(tq, size)
    return [(t * size, size) for t in range(tq // size)]


def _put_rows(old, new, r0):
    return new if r0 == 0 else jnp.concatenate([old[:r0], new], axis=0)


def _aug_queries(q):
    lane = _iota((1, LANES), 1)
    one = jnp.ones_like(q)
    zero = jnp.zeros_like(q)
    qa = jnp.where(lane < HEAD_DIM, q, jnp.where(lane < HEAD_DIM + AUG, one, zero))
    qb = jnp.where(lane >= HEAD_DIM, q, jnp.where(lane < AUG, one, zero))
    return qa, qb


EXP_DEAD = -105.0
PACK = 16


def _fox_walk_left(nfull, tk, block, carry, k_refs, qk_bound, row_floor):
    lane = _iota((1, LANES), 1)

    def score_bound(h, j):
        k0 = pl.multiple_of(jnp.maximum(j, 0) * tk + tk - PACK, PACK)
        last = k_refs[h][pl.ds(k0, PACK), :].astype(F32)
        lo = HEAD_DIM if h == 0 else 0
        negc = jnp.sum(jnp.where((lane >= lo) & (lane < lo + AUG), last, 0.0), axis=1, keepdims=True)
        return qk_bound + jnp.max(negc)

    def alive(state):
        jj, c = state
        j = nfull - 1 - jj
        floors = row_floor(c)
        return (jj < nfull) & ((score_bound(0, j) - floors[0] >= EXP_DEAD) | (score_bound(1, j) - floors[1] >= EXP_DEAD))

    def step(state):
        jj, c = state
        return jj + 1, block(pl.multiple_of((nfull - 1 - jj) * tk, tk), tk, 0, c, False)

    return lax.while_loop(alive, step, (jnp.int32(0), carry))[1]


def _fox_fwd(qn, ka, kb, v, projm, qkb, *, tq, tk):
    S = qn.shape[0]
    npair = FOX_HEADS // 2

    def body(q_ref, ka_ref, kb_ref, v_ref, fg_ref, qkb_ref, o_ref, lse_ref, fm_ref):
        qi = pl.program_id(1)
        lane = _iota((1, LANES), 1)
        ma = lane < HEAD_DIM
        qaug = _aug_queries(q_ref[...])
        k_refs = (ka_ref, kb_ref)

        def block(k0, tkl, r0, carry, masked):
            vb = v_ref[pl.ds(k0, tkl), :]
            if masked:
                mask = (k0 + _iota((tq - r0, tkl), 1)) <= (qi * tq + r0 + _iota((tq - r0, tkl), 0))
            scores = [_dot_nt(qaug[h][r0:], k_refs[h][pl.ds(k0, tkl), :]) for h in range(2)]
            new = []
            for h in range(2):
                m, l, acc = [x[r0:] for x in carry[h]]
                s = jnp.where(mask, scores[h], NEG) if masked else scores[h]
                m_new = jnp.maximum(m, jnp.max(s, axis=1, keepdims=True))
                alpha = jnp.exp(m - m_new)
                p = jnp.exp(s - m_new)
                sub = (m_new, alpha * l + jnp.sum(p, axis=1, keepdims=True), alpha * acc + _dot(p.astype(BF16), vb))
                new.append(tuple(_put_rows(old, x, r0) for old, x in zip(carry[h], sub)))
            return tuple(new)

        carry = tuple((jnp.full((tq, 1), NEG, F32), jnp.zeros((tq, 1), F32), jnp.zeros((tq, LANES), F32)) for _ in range(2))
        for off, size in _diag_tiles(tq, tq):
            carry = block(pl.multiple_of(qi * tq + off, size), size, off, carry, True)
        carry = _fox_walk_left((qi * tq) // tk, tk, block, carry, k_refs, jnp.max(qkb_ref[...]),
                               lambda c: (jnp.min(c[0][0]), jnp.min(c[1][0])))
        (ma_, la, acca), (mb_, lb, accb) = carry
        o = jnp.where(ma, acca / la, accb / lb)
        o_ref[...] = o
        lse_ref[...] = jnp.where(ma, ma_ + jnp.log(la), mb_ + jnp.log(lb))
        fg = fg_ref[...]
        fm_ref[...] = (o * (fg * _sigmoid(fg))).astype(BF16)

    qblk = pl.BlockSpec((tq, LANES), lambda p, i: (i, p))
    kvblk = pl.BlockSpec((S, LANES), lambda p, i: (0, p))
    return pl.pallas_call(
        body, name="fox_fwd", grid=(npair, S // tq),
        in_specs=[qblk, kvblk, kvblk, kvblk,
                  pl.BlockSpec((tq, LANES), lambda p, i: (i, C_FG // LANES + p)),
                  pl.BlockSpec((1, LANES), lambda p, i: (0, 0))],
        out_specs=[qblk, qblk, qblk],
        out_shape=[jax.ShapeDtypeStruct((S, FOX_W), F32), jax.ShapeDtypeStruct((S, FOX_W), F32), jax.ShapeDtypeStruct((S, FOX_W), BF16)],
        compiler_params=_cparams(dimension_semantics=("arbitrary", "arbitrary")),
    )(qn, ka, kb, v, projm, qkb)


def _suffix_sums(x, tmat2):
    return _dot(jnp.concatenate(_split2(x), axis=1), tmat2)


def _suffix_matrix(tk, inclusive):
    rr, cc = _iota((2 * tk, tk), 0) & (tk - 1), _iota((2 * tk, tk), 1)
    return _ones_where(rr >= cc) if inclusive else _ones_where(rr > cc)


def _sb_scores(qh, kb, causal, tmat2, r_runs):
    heads = range(2)
    zs = [_dot_nt(qh[h], kb) for h in heads]
    nsps = [jnp.minimum(-z, 0.0) - jnp.log(1.0 + jnp.exp(-jnp.abs(z))) for z in zs]
    lbs = nsps if causal is None else [jnp.where(causal, n, 0.0) for n in nsps]
    rins = [_suffix_sums(lb, tmat2) for lb in lbs]
    args = [zs[h] + lbs[h] + (rins[h] + r_runs[h]) for h in heads]
    a_s = [jnp.exp(arg if causal is None else jnp.where(causal, arg, NEG)) for arg in args]
    return zs, nsps, lbs, a_s


def _sb_walk_left(nfull, tk, block, carry, running_sums):
    def alive(state):
        jj, c = state
        ra, rb = running_sums(c)
        return (jj < nfull) & (jnp.max(jnp.maximum(ra, rb)) >= EXP_DEAD)

    def step(state):
        jj, c = state
        return jj + 1, block(pl.multiple_of((nfull - 1 - jj) * tk, tk), 0, c, False)

    return lax.while_loop(alive, step, (jnp.int32(0), carry))[1]


def _sb_fwd(sq, sk, sv, projm, *, tq, tk):
    S = sq.shape[0]
    npair = SB_HEADS // 2

    def body(q_ref, k_ref, v_ref, sg_ref, o_ref, sm_ref):
        qi = pl.program_id(1)
        lane = _iota((1, LANES), 1)
        ma = lane < HEAD_DIM
        qh = _pair_masks(q_ref[...])
        tmat2 = _suffix_matrix(tk, inclusive=False)
        nfull = (qi * tq) // tk

        def block(k0, r0, carry, masked):
            nr = tq - r0
            kb = k_ref[pl.ds(k0, tk), :]
            vb = v_ref[pl.ds(k0, tk), :]
            causal = (k0 + _iota((nr, tk), 1)) < (qi * tq + r0 + _iota((nr, tk), 0)) if masked else None
            _, _, lbs, a_s = _sb_scores([q[r0:] for q in qh], kb, causal, tmat2, [carry[h][0][r0:] for h in range(2)])
            pv = _dot(jnp.concatenate([a.astype(BF16) for a in a_s], axis=0), vb)
            return tuple((_put_rows(carry[h][0], carry[h][0][r0:] + jnp.sum(lbs[h], axis=1, keepdims=True), r0),
                          _put_rows(carry[h][1], carry[h][1][r0:] + pv[h * nr:(h + 1) * nr], r0)) for h in range(2))

        carry = tuple((jnp.zeros((tq, 1), F32), jnp.zeros((tq, LANES), F32)) for _ in range(2))
        for off, size in reversed(_diag_tiles(tq)):
            assert size == tk
            carry = block(pl.multiple_of(qi * tq + off, tk), off, carry, True)
        (_, acca), (_, accb) = _sb_walk_left(nfull, tk, block, carry, lambda c: (c[0][0], c[1][0]))
        o = jnp.where(ma, acca, accb)
        o_ref[...] = o
        sg = sg_ref[...]
        sm_ref[...] = (o * (sg * _sigmoid(sg))).astype(BF16)

    qblk = pl.BlockSpec((tq, LANES), lambda p, i: (i, p))
    kvblk = pl.BlockSpec((S, LANES), lambda p, i: (0, p))
    return pl.pallas_call(
        body, name="sb_fwd", grid=(npair, S // tq),
        in_specs=[qblk, kvblk, kvblk, pl.BlockSpec((tq, LANES), lambda p, i: (i, C_SG // LANES + p))],
        out_specs=[qblk, qblk],
        out_shape=[jax.ShapeDtypeStruct((S, SB_W), F32), jax.ShapeDtypeStruct((S, SB_W), BF16)],
        compiler_params=_cparams(dimension_semantics=("arbitrary", "arbitrary")),
    )(sq, sk, sv, projm)


def _outproj(x, fm, pm, sm, w_out, layer, *, tm):
    S, D = x.shape

    def body(x_ref, fm_ref, pm_ref, sm_ref, w_ref, y_ref):
        y = x_ref[...] + _dot(fm_ref[...], w_ref[0:FOX_W, :])
        y = y + _dot(pm_ref[...], w_ref[FOX_W:FOX_W + POOL_W, :])
        y_ref[...] = y + _dot(sm_ref[...], w_ref[FOX_W + POOL_W:D_MIX, :])

    row = lambda w: pl.BlockSpec((tm, w), lambda i: (i, 0))
    return pl.pallas_call(
        body, name="outproj", grid=(S // tm,),
        in_specs=[row(D), row(FOX_W), row(POOL_W), row(SB_W), pl.BlockSpec((None, D_MIX, D), lambda i: (layer, 0, 0))],
        out_specs=row(D), out_shape=jax.ShapeDtypeStruct((S, D), F32),
        compiler_params=_cparams(dimension_semantics=("arbitrary",)),
    )(x, fm, pm, sm, w_out)


def _loss_head(y, target, *, tm):
    S, D = y.shape

    def body(y_ref, t_ref, dy_ref, sq_ref):
        @pl.when(pl.program_id(0) == 0)
        def _():
            sq_ref[...] = jnp.zeros_like(sq_ref)

        d = y_ref[...] - t_ref[...]
        dy_ref[...] = d * (1.0 / D)
        sq_ref[...] += jnp.sum(d * d, axis=0, keepdims=True)

    row = pl.BlockSpec((tm, D), lambda i: (i, 0))
    return pl.pallas_call(
        body, name="loss_head", grid=(S // tm,),
        in_specs=[row, row], out_specs=[row, pl.BlockSpec((1, D), lambda i: (0, 0))],
        out_shape=[jax.ShapeDtypeStruct((S, D), F32), jax.ShapeDtypeStruct((1, D), F32)],
        compiler_params=_cparams(dimension_semantics=("arbitrary",)),
    )(y, target)


def _outproj_bwd(dy, fm, pm, sm, w_out, layer, stacks, *, tm):
    S, D = dy.shape

    def body(dy_ref, fm_ref, pm_ref, sm_ref, w_ref, dm_ref, dw_ref):
        @pl.when(pl.program_id(0) == 0)
        def _():
            dw_ref[...] = jnp.zeros_like(dw_ref)

        dyb = dy_ref[...].astype(BF16)
        dm_ref[...] = _dot_nt(dyb, w_ref[...])
        dw_ref[0:FOX_W, :] += _dot_tn(fm_ref[...], dyb)
        dw_ref[FOX_W:FOX_W + POOL_W, :] += _dot_tn(pm_ref[...], dyb)
        dw_ref[FOX_W + POOL_W:D_MIX, :] += _dot_tn(sm_ref[...], dyb)

    row = lambda w: pl.BlockSpec((tm, w), lambda i: (i, 0))
    wspec = pl.BlockSpec((None, D_MIX, D), lambda i: (layer, 0, 0))
    return _stack_call(
        body, "outproj_bwd", (S // tm,), [row(D), row(FOX_W), row(POOL_W), row(SB_W), wspec], (dy, fm, pm, sm, w_out),
        [pl.BlockSpec((None, D_MIX, D), lambda i: (layer, 0, 0))], [(D_MIX, D)], stacks,
        plain_specs=[row(D_MIX)], plain_shapes=[jax.ShapeDtypeStruct((S, D_MIX), F32)],
        compiler_params=_cparams(dimension_semantics=("arbitrary",)))


def _fox_bwd(qn, ka, kb, v, o, lse, dmix, projm, qkb, *, tq, tk):
    S = qn.shape[0]
    npair = FOX_HEADS // 2

    def body(q_ref, ka_ref, kb_ref, v_ref, o_ref, lse_ref, dm_ref, fg_ref, qkb_ref,
             dq_ref, dk_ref, dv_ref, dfg_ref, dct_ref, dcr_ref):
        qi = pl.program_id(1)

        @pl.when(qi == 0)
        def _():
            dk_ref[...] = jnp.zeros_like(dk_ref)
            dv_ref[...] = jnp.zeros_like(dv_ref)
            dct_ref[...] = jnp.zeros_like(dct_ref)

        lane = _iota((1, LANES), 1)
        ma = lane < HEAD_DIM
        qh = _pair_masks(q_ref[...])
        qaug = _aug_queries(q_ref[...])
        k_refs = (ka_ref, kb_ref)
        lsev = lse_ref[...]
        lse = (_lane_pick(lsev, lane, 0), _lane_pick(lsev, lane, HEAD_DIM))
        fg = fg_ref[...]
        silu, dsilu = _silu_pair(fg)
        dm = dm_ref[...]
        ov = o_ref[...]
        do = dm * silu
        dfg_ref[...] = dm * ov * dsilu
        dd = do * ov
        dsum = (jnp.sum(jnp.where(ma, dd, 0.0), axis=1, keepdims=True), jnp.sum(jnp.where(ma, 0.0, dd), axis=1, keepdims=True))
        doh = _pair_masks(do.astype(BF16))

        def block(k0, tkl, r0, carry, masked):
            vb = v_ref[pl.ds(k0, tkl), :]
            if masked:
                mask = (k0 + _iota((tq - r0, tkl), 1)) <= (qi * tq + r0 + _iota((tq - r0, tkl), 0))
            heads = range(2)
            kaugs = [k_refs[h][pl.ds(k0, tkl), :] for h in heads]
            scores = [_dot_nt(qaug[h][r0:], kaugs[h]) for h in heads]
            dps = [_dot_nt(doh[h][r0:], vb) for h in heads]
            ps, dss, rows = [], [], []
            for h in heads:
                s = jnp.where(mask, scores[h], NEG) if masked else scores[h]
                p = jnp.exp(s - lse[h][r0:])
                dsf = p * (dps[h] - dsum[h][r0:])
                dct_ref[0, h:h + 1, pl.ds(k0, tkl)] -= jnp.sum(dsf, axis=0, keepdims=True)
                rows.append(_put_rows(carry[1 + h], carry[1 + h][r0:] + jnp.sum(dsf, axis=1, keepdims=True), r0))
                ps.append(p.astype(BF16))
                dss.append(dsf.astype(BF16))
            dv_ref[pl.ds(k0, tkl), :] += _dot_tn(jnp.concatenate(ps, axis=0), jnp.concatenate([d[r0:] for d in doh], axis=0))
            dk_ref[pl.ds(k0, tkl), :] += _dot_tn(jnp.concatenate(dss, axis=0), jnp.concatenate([q[r0:] for q in qh], axis=0))
            kh = jnp.concatenate([_pair_masks(kaugs[h])[h] for h in heads], axis=0)
            dq = _put_rows(carry[0], carry[0][r0:] + _dot(jnp.concatenate(dss, axis=1), kh), r0)
            return (dq, rows[0], rows[1])

        zcol = jnp.zeros((tq, 1), F32)
        carry = (jnp.zeros((tq, LANES), F32), zcol, zcol)
        for off, size in _diag_tiles(tq):
            carry = block(pl.multiple_of(qi * tq + off, size), size, off, carry, True)
        floors = (jnp.min(lse[0]), jnp.min(lse[1]))
        dq, rowa, rowb = _fox_walk_left((qi * tq) // tk, tk, block, carry, k_refs, jnp.max(qkb_ref[...]), lambda c: floors)
        dq_ref[...] = dq * QK_SCALE
        dcr_ref[0] = jnp.where(ma, rowa, rowb)

    qblk = pl.BlockSpec((tq, LANES), lambda p, i: (i, p))
    kvblk = pl.BlockSpec((S, LANES), lambda p, i: (0, p))
    f32out = jax.ShapeDtypeStruct((S, FOX_W), F32)
    ctblk = pl.BlockSpec((1, FF_STRIDE, S), lambda p, i: (p, 0, 0))
    return pl.pallas_call(
        body, name="fox_bwd", grid=(npair, S // tq),
        in_specs=[qblk, kvblk, kvblk, kvblk, qblk, qblk, qblk,
                  pl.BlockSpec((tq, LANES), lambda p, i: (i, C_FG // LANES + p)),
                  pl.BlockSpec((1, LANES), lambda p, i: (0, 0))],
        out_specs=[qblk, kvblk, kvblk, qblk, ctblk, pl.BlockSpec((1, tq, LANES), lambda p, i: (p, i, 0))],
        out_shape=[f32out, f32out, f32out, f32out, jax.ShapeDtypeStruct((npair, FF_STRIDE, S), F32),
                   jax.ShapeDtypeStruct((npair, S, LANES), F32)],
        compiler_params=_cparams(dimension_semantics=("arbitrary", "arbitrary")),
    )(qn, ka, kb, v, o, lse, dmix, projm, qkb)


def _sb_bwd(sq, sk, sv, o, dmix, projm, *, tq, tk):
    S = sq.shape[0]
    npair = SB_HEADS // 2
    mix0 = (FOX_W + POOL_W) // LANES

    def body(q_ref, k_ref, v_ref, o_ref, dm_ref, sg_ref, dq_ref, dk_ref, dv_ref, dsg_ref):
        qi = pl.program_id(1)

        @pl.when(qi == 0)
        def _():
            dk_ref[...] = jnp.zeros_like(dk_ref)
            dv_ref[...] = jnp.zeros_like(dv_ref)

        lane = _iota((1, LANES), 1)
        ma = lane < HEAD_DIM
        qh = _pair_masks(q_ref[...])
        sg = sg_ref[...]
        silu, dsilu = _silu_pair(sg)
        dm = dm_ref[...]
        ov = o_ref[...]
        do = dm * silu
        dsg_ref[...] = dm * ov * dsilu
        dob = do.astype(BF16)
        dd = dob.astype(F32) * ov
        dsum = (jnp.sum(jnp.where(ma, dd, 0.0), axis=1, keepdims=True), jnp.sum(jnp.where(ma, 0.0, dd), axis=1, keepdims=True))
        doh = _pair_masks(dob)
        tmat2 = _suffix_matrix(tk, inclusive=False)
        tmat2_inc = _suffix_matrix(tk, inclusive=True)
        nfull = (qi * tq) // tk

        def block(k0, r0, carry, masked):
            nr = tq - r0
            kb = k_ref[pl.ds(k0, tk), :]
            vb = v_ref[pl.ds(k0, tk), :]
            kh = _pair_masks(kb)
            causal = (k0 + _iota((nr, tk), 1)) < (qi * tq + r0 + _iota((nr, tk), 0)) if masked else None
            heads = range(2)
            qs = [q[r0:] for q in qh]
            dos = [d[r0:] for d in doh]
            das = [_dot_nt(dos[h], vb) for h in heads]
            zs, nsps, lbs, a_s = _sb_scores(qs, kb, causal, tmat2, [carry[h][0][r0:] for h in heads])
            abs_ = [a.astype(BF16) for a in a_s]
            us = [abs_[h].astype(F32) * das[h] for h in heads]
            uins = [_suffix_sums(u, tmat2_inc) for u in us]
            dzs = []
            for h in heads:
                cum_u = dsum[h][r0:] - (uins[h] + carry[h][1][r0:])
                dz = us[h] * jnp.exp(nsps[h]) - jnp.exp(zs[h] + nsps[h]) * cum_u
                if masked:
                    dz = jnp.where(causal, dz, 0.0)
                dzs.append(dz.astype(BF16))
            dv_ref[pl.ds(k0, tk), :] += _dot_tn(jnp.concatenate(abs_, axis=0), jnp.concatenate(dos, axis=0))
            dk_ref[pl.ds(k0, tk), :] += _dot_tn(jnp.concatenate(dzs, axis=0), jnp.concatenate(qs, axis=0))
            dq = _put_rows(carry[2], carry[2][r0:] + _dot(jnp.concatenate(dzs, axis=1), jnp.concatenate(kh, axis=0)), r0)
            new = [(_put_rows(carry[h][0], carry[h][0][r0:] + jnp.sum(lbs[h], axis=1, keepdims=True), r0),
                    _put_rows(carry[h][1], carry[h][1][r0:] + jnp.sum(us[h], axis=1, keepdims=True), r0)) for h in heads]
            return (new[0], new[1], dq)

        zcol = jnp.zeros((tq, 1), F32)
        carry = ((zcol, zcol), (zcol, zcol), jnp.zeros((tq, LANES), F32))
        for off, size in reversed(_diag_tiles(tq)):
            assert size == tk
            carry = block(pl.multiple_of(qi * tq + off, tk), off, carry, True)
        dq = _sb_walk_left(nfull, tk, block, carry, lambda c: (c[0][0], c[1][0]))[2]
        dq_ref[...] = dq * QK_SCALE

    qblk = pl.BlockSpec((tq, LANES), lambda p, i: (i, p))
    kvblk = pl.BlockSpec((S, LANES), lambda p, i: (0, p))
    f32out = jax.ShapeDtypeStruct((S, SB_W), F32)
    return pl.pallas_call(
        body, name="sb_bwd", grid=(npair, S // tq),
        in_specs=[qblk, kvblk, kvblk, qblk,
                  pl.BlockSpec((tq, LANES), lambda p, i: (i, mix0 + p)),
                  pl.BlockSpec((tq, LANES), lambda p, i: (i, C_SG // LANES + p))],
        out_specs=[qblk, kvblk, kvblk, qblk],
        out_shape=[f32out, f32out, f32out, f32out],
        compiler_params=_cparams(dimension_semantics=("arbitrary", "arbitrary")),
    )(sq, sk, sv, o, dmix, projm)


def _prep_bwd(projm, ffo, dqn, dkn, dct, dcr, dv, dfg, dsq, dsk, dsv, dsg, dmix, pooled, yp, qg, kg, bfp, wpd, ps, *, ts):
    S = projm.shape[0]
    nb = S // ts
    hb = ts // POOL_HALO
    npair = FOX_HEADS // 2
    last_halo = S // POOL_HALO - 1

    def body(fq_ref, fk_ref, pp_ref, pph_ref, ff_ref,
             dqn_ref, dkn_ref, dct_ref, dcr_ref, dv_ref, dfg_ref, dsq_ref, dsk_ref, dsv_ref, dsg_ref,
             dmp_ref, dmh_ref, pooled_ref, yp_ref, qg_ref, kg_ref, bf_ref, wpd_ref, ps_ref,
             dp_ref, dqg_ref, dkg_ref, dbf_ref, dwp_ref, dps_ref,
             carry_ref, dl_ref, buf_ref, dct_s):
        i = pl.program_id(0)
        blk = nb - 1 - i

        @pl.when(i == 0)
        def _():
            carry_ref[...] = jnp.zeros_like(carry_ref)
            dqg_ref[...] = jnp.zeros_like(dqg_ref)
            dkg_ref[...] = jnp.zeros_like(dkg_ref)
            dbf_ref[...] = jnp.zeros_like(dbf_ref)
            dwp_ref[...] = jnp.zeros_like(dwp_ref)
            dps_ref[...] = jnp.zeros_like(dps_ref)

        bd = _head_blockdiag()
        for raw_ref, g_ref, dn, dg_ref, col in ((fq_ref, qg_ref, dqn_ref[...], dqg_ref, C_FQ), (fk_ref, kg_ref, dkn_ref[...], dkg_ref, C_FK)):
            q = raw_ref[...]
            rstd = lax.rsqrt(_group_sum(q * q, bd) * (1.0 / HEAD_DIM) + EPS)
            xhat = q * rstd
            dg_ref[...] += jnp.sum(dn * xhat, axis=0, keepdims=True)
            dyg = dn * g_ref[...]
            mean = _group_sum(dyg * xhat, bd) * (1.0 / HEAD_DIM)
            dp_ref[:, col:col + FOX_W] = (rstd * (dyg - xhat * mean)).astype(BF16)
        dp_ref[:, C_FV:C_FV + FOX_W] = dv_ref[...].astype(BF16)
        dp_ref[:, C_FG:C_FG + FOX_W] = dfg_ref[...].astype(BF16)
        dp_ref[:, C_SQ:C_SQ + SB_W] = dsq_ref[...].astype(BF16)
        dp_ref[:, C_SK:C_SK + SB_W] = dsk_ref[...].astype(BF16)
        dp_ref[:, C_SV:C_SV + SB_W] = dsv_ref[...].astype(BF16)
        dp_ref[:, C_SG:C_SG + SB_W] = dsg_ref[...].astype(BF16)

        dct_s[...] = jnp.zeros_like(dct_s)
        for p in range(npair):
            dct_s[FF_STRIDE * p:FF_STRIDE * (p + 1), :] = dct_ref[p]
        dc = dct_s[...].T
        lane = _iota((1, LANES), 1)
        for p in range(npair):
            dcr = dcr_ref[p]
            dc = dc + jnp.where(lane == FF_STRIDE * p, _lane_pick(dcr, lane, 0), 0.0)
            dc = dc + jnp.where(lane == FF_STRIDE * p + 1, _lane_pick(dcr, lane, HEAD_DIM), 0.0)
        triu = _ones_where(_iota((ts, ts), 1) >= _iota((ts, ts), 0))
        dlf = _dot_exact_lhs(triu, dc) + carry_ref[...]
        dl_ref[...] = dlf
        carry_ref[...] = dl_ref[0:1, :]
        z = ff_ref[...] + bf_ref[...]
        dff = dlf * (1.0 / (1.0 + jnp.exp(z)))
        dbf_ref[...] += jnp.sum(dff, axis=0, keepdims=True)
        dp_ref[:, PM:PW] = dff.astype(BF16)

        psv = ps_ref[...]
        wpdv = wpd_ref[...]
        lane_group = _iota((1, POOL_W), 1) >> 6
        wlen = _pool_group_select(lane_group, [float(w) for w in POOL_WINDOWS])
        pg = pp_ref[:, POOL_W:2 * POOL_W]
        silu, dsilu = _silu_pair(pg)
        dmp = dmp_ref[...]
        ypv = yp_ref[...]
        dp_ref[:, C_PG:C_PG + POOL_W] = (dmp * (ypv * psv) * dsilu).astype(BF16)
        dps_ref[...] += jnp.sum(dmp * silu * ypv, axis=0, keepdims=True)
        dyp = (dmp * psv * silu).astype(BF16)
        dwp_ref[...] += _dot_tn(pooled_ref[...], dyp)
        dpooled = _dot_nt(dyp, wpdv)
        pgh = pph_ref[:, POOL_W:2 * POOL_W]
        dyph = (dmh_ref[...] * psv * (pgh * _sigmoid(pgh))).astype(BF16)
        dpooled_h = jnp.where(blk < nb - 1, _dot_nt(dyph, wpdv), 0.0)
        tpos = (blk * ts + _iota((ts, 1), 0) + 1).astype(F32)
        ev = dpooled / jnp.minimum(tpos, wlen)
        buf_ref[0:ts, :] = ev
        buf_ref[ts:ts + POOL_HALO, :] = dpooled_h / wlen
        acc = ev
        snaps = []
        for d in range(1, POOL_HALO):
            acc = acc + buf_ref[pl.ds(d, ts), :]
            if d + 1 in POOL_WINDOWS:
                snaps.append(acc)
        dp_ref[:, C_PX:C_PX + POOL_W] = (_pool_group_select(lane_group, snaps) - dpooled).astype(BF16)

    rblk = lambda w, c: pl.BlockSpec((ts, w), lambda i: (nb - 1 - i, c))
    full = lambda a: pl.BlockSpec(a.shape, lambda i: (0,) * a.ndim)
    halo = lambda w, c: pl.BlockSpec((POOL_HALO, w), lambda i: (jnp.minimum((nb - i) * hb, last_halo), c))
    acc_spec = lambda r, w: pl.BlockSpec((r, w), lambda i: (0, 0))
    return pl.pallas_call(
        body, name="prep_bwd", grid=(nb,),
        in_specs=[rblk(FOX_W, C_FQ // FOX_W), rblk(FOX_W, C_FK // FOX_W), rblk(2 * POOL_W, C_PX // (2 * POOL_W)),
                  halo(2 * POOL_W, C_PX // (2 * POOL_W)), rblk(LANES, 0),
                  rblk(FOX_W, 0), rblk(FOX_W, 0), pl.BlockSpec((npair, FF_STRIDE, ts), lambda i: (0, 0, nb - 1 - i)),
                  pl.BlockSpec((npair, ts, LANES), lambda i: (0, nb - 1 - i, 0)), rblk(FOX_W, 0), rblk(FOX_W, 0),
                  rblk(SB_W, 0), rblk(SB_W, 0), rblk(SB_W, 0), rblk(SB_W, 0),
                  rblk(POOL_W, FOX_W // POOL_W), halo(POOL_W, FOX_W // POOL_W), rblk(POOL_W, 0), rblk(POOL_W, 0),
                  full(qg), full(kg), full(bfp), full(wpd), full(ps)],
        out_specs=[rblk(PW, 0), acc_spec(1, FOX_W), acc_spec(1, FOX_W), acc_spec(1, LANES), acc_spec(POOL_W, POOL_W), acc_spec(1, POOL_W)],
        out_shape=[jax.ShapeDtypeStruct((S, PW), BF16), jax.ShapeDtypeStruct((1, FOX_W), F32), jax.ShapeDtypeStruct((1, FOX_W), F32),
                   jax.ShapeDtypeStruct((1, LANES), F32), jax.ShapeDtypeStruct((POOL_W, POOL_W), F32), jax.ShapeDtypeStruct((1, POOL_W), F32)],
        scratch_shapes=[pltpu.VMEM((1, LANES), F32), pltpu.VMEM((ts, LANES), F32), pltpu.VMEM((ts + POOL_HALO, POOL_W), F32),
                        pltpu.VMEM((LANES, ts), F32)],
        compiler_params=_cparams(dimension_semantics=("arbitrary",)),
    )(projm, projm, projm, projm, ffo, dqn, dkn, dct, dcr, dv, dfg, dsq, dsk, dsv, dsg, dmix, dmix, pooled, yp, qg, kg, bfp, wpd, ps)


def _stack_call(body, name, grid, in_specs, operands, slot_specs, slot_shapes, stacks, plain_specs=(), plain_shapes=(), **kw):
    out_specs = list(plain_specs) + list(slot_specs)
    out_shape = list(plain_shapes) + [jax.ShapeDtypeStruct((DEPTH,) + s, F32) for s in slot_shapes]
    if stacks is None:
        return pl.pallas_call(body, name=name, grid=grid, in_specs=in_specs, out_specs=out_specs, out_shape=out_shape, **kw)(*operands)
    n = len(operands)

    def aliased_body(*refs):
        body(*refs[:n], *refs[n + len(stacks):])

    return pl.pallas_call(
        aliased_body, name=name, grid=grid, in_specs=list(in_specs) + [pl.BlockSpec(memory_space=pl.ANY)] * len(stacks),
        out_specs=out_specs, out_shape=out_shape,
        input_output_aliases={n + k: len(plain_specs) + k for k in range(len(stacks))}, **kw)(*operands, *stacks)


def _inproj_dw(h, dproj, layer, stacks, *, ts, tn):
    S, D = h.shape
    nj = PM // tn

    def body(h_ref, dp_ref, dpf_ref, dw_ref, dwf_ref):
        s = pl.program_id(1)

        @pl.when(s == 0)
        def _():
            dw_ref[...] = jnp.zeros_like(dw_ref)

        @pl.when((s == 0) & (pl.program_id(0) == 0))
        def _():
            dwf_ref[...] = jnp.zeros_like(dwf_ref)

        hv = h_ref[...]
        dw_ref[...] += _dot_tn(dp_ref[...], hv)

        @pl.when(pl.program_id(0) == 0)
        def _():
            dwf_ref[...] += _dot_tn(dpf_ref[...], hv)

    return _stack_call(
        body, "inproj_dw", (nj, S // ts),
        [pl.BlockSpec((ts, D), lambda j, s: (s, 0)),
         pl.BlockSpec((ts, tn), lambda j, s: (s, j)),
         pl.BlockSpec((ts, LANES), lambda j, s: (s, PM // LANES))],
        (h, dproj, dproj),
        [pl.BlockSpec((None, tn, D), lambda j, s: (layer, j, 0)), pl.BlockSpec((None, LANES, D), lambda j, s: (layer, 0, 0))],
        [(PM, D), (LANES, D)], stacks,
        compiler_params=_cparams(dimension_semantics=("arbitrary", "arbitrary")))


def _inproj_dx(dproj, wt_all, layer, x, g, dy, *, tm):
    S, D = x.shape

    def body(dp_ref, w_ref, x_ref, g_ref, dy_ref, dx_ref, dg_ref):
        @pl.when(pl.program_id(0) == 0)
        def _():
            dg_ref[...] = jnp.zeros_like(dg_ref)

        dh = _dot(dp_ref[...], w_ref[...])
        xf = x_ref[...]
        rstd = lax.rsqrt(jnp.mean(xf * xf, axis=-1, keepdims=True) + EPS)
        xhat = xf * rstd
        dg_ref[...] += jnp.sum(dh * xhat, axis=0, keepdims=True)
        dyg = dh * g_ref[...]
        mean = jnp.mean(dyg * xhat, axis=-1, keepdims=True)
        dx_ref[...] = rstd * (dyg - xhat * mean) + dy_ref[...]

    row = lambda w: pl.BlockSpec((tm, w), lambda i: (i, 0))
    return pl.pallas_call(
        body, name="inproj_dx", grid=(S // tm,),
        in_specs=[row(PW), pl.BlockSpec((None, PW, D), lambda i: (layer, 0, 0)), row(D), pl.BlockSpec((1, D), lambda i: (0, 0)), row(D)],
        out_specs=[row(D), pl.BlockSpec((1, D), lambda i: (0, 0))],
        out_shape=[jax.ShapeDtypeStruct((S, D), F32), jax.ShapeDtypeStruct((1, D), F32)],
        compiler_params=_cparams(dimension_semantics=("arbitrary",)),
    )(dproj, wt_all, x, g, dy)


def _adam_update(w, g, m, v):
    nm = ADAM_B1 * m + (1.0 - ADAM_B1) * g
    nv = ADAM_B2 * v + (1.0 - ADAM_B2) * (g * g)
    m_hat = nm / (1.0 - ADAM_B1 ** ADAM_STEP)
    v_hat = nv / (1.0 - ADAM_B2 ** ADAM_STEP)
    return -ADAM_LR * (m_hat / (jnp.sqrt(v_hat) + ADAM_EPS) + ADAM_WD * w), nm, nv


def _adamw(w, g, m, v):
    L, R, C = w.shape
    tr = R if R <= 512 else 256

    def body(w_ref, g_ref, m_ref, v_ref, d_ref, nm_ref, nv_ref):
        d_ref[...], nm_ref[...], nv_ref[...] = _adam_update(w_ref[...], g_ref[...], m_ref[...], v_ref[...])

    spec = pl.BlockSpec((1, tr, C), lambda l, i: (l, i, 0))
    shp = jax.ShapeDtypeStruct((L, R, C), F32)
    return pl.pallas_call(
        body, name="adamw", grid=(L, R // tr), in_specs=[spec] * 4, out_specs=[spec] * 3, out_shape=[shp] * 3,
        compiler_params=_cparams(dimension_semantics=("arbitrary", "arbitrary")),
    )(w, g, m, v)


def _adamw_nd(w, g, m, v):
    shape = w.shape
    view = (1,) + shape if w.ndim == 2 else (shape[0], -1, shape[-1])
    outs = _adamw(w.reshape(view), g.reshape(view), m.reshape(view), v.reshape(view))
    return tuple(o.reshape(shape) for o in outs)


FLIP_C = (0, 0, 1)
FLIP_X = (1, 0, 0)
FLIP_Y = (0, 1, 0)
FLIP_XY = (1, 1, 0)
MESH = pl.DeviceIdType.MESH


def _peer(flip):
    me = (lax.axis_index("x"), lax.axis_index("y"), lax.axis_index("c"))
    return tuple(1 - a if f else a for a, f in zip(me, flip))


def _exchange(name, arrays, flips):
    n = len(arrays)

    def body(*refs):
        srcs, dsts = refs[:n], refs[n:2 * n]
        send_sems, recv_sems = refs[2 * n:]
        copies = [pltpu.make_async_remote_copy(src_ref=srcs[k], dst_ref=dsts[k], send_sem=send_sems.at[k], recv_sem=recv_sems.at[k],
                                               device_id=_peer(flips[k]), device_id_type=MESH) for k in range(n)]
        for cp in copies:
            cp.start()
        for cp in copies:
            cp.wait()

    anyspec = pl.BlockSpec(memory_space=pl.ANY)
    return pl.pallas_call(
        body, name=name, in_specs=[anyspec] * n, out_specs=[anyspec] * n,
        out_shape=[jax.ShapeDtypeStruct(a.shape, a.dtype) for a in arrays],
        scratch_shapes=[pltpu.SemaphoreType.DMA((n,)), pltpu.SemaphoreType.DMA((n,))],
    )(*arrays)


def _exchange_add(name, x, flip):
    def body(x_ref, o_ref, buf_ref, send_sem, recv_sem):
        cp = pltpu.make_async_remote_copy(src_ref=x_ref, dst_ref=buf_ref, send_sem=send_sem, recv_sem=recv_sem,
                                          device_id=_peer(flip), device_id_type=MESH)
        cp.start()
        cp.wait()
        o_ref[...] = x_ref[...] + buf_ref[...]

    vspec = pl.BlockSpec(memory_space=pltpu.VMEM)
    return pl.pallas_call(
        body, name=name, in_specs=[vspec], out_specs=vspec, out_shape=jax.ShapeDtypeStruct(x.shape, x.dtype),
        scratch_shapes=[pltpu.VMEM(x.shape, x.dtype), pltpu.SemaphoreType.DMA, pltpu.SemaphoreType.DMA],
    )(x)


def _chip_index():
    return 2 * lax.axis_index("x") + lax.axis_index("y")


def _gather_weights(w_in_t, w_out):
    wi = w_in_t.astype(BF16)
    wo = jnp.swapaxes(w_out, 0, 1).astype(BF16)
    halves = (wi.shape[0] // 2, wo.shape[0] // 2)
    ARR = 2
    TO_X, TO_Y, ON_Y, ON_X, SIB_X, SIB_Y, SIB_D0, SIB_D1, OWN = [ARR * k for k in range(9)]
    n_sems = ARR * 9

    def body(wi_ref, wo_ref, gi_ref, go_ref, send_sems, recv_sems):
        c = lax.axis_index("c")
        j = _chip_index()
        srcs = (wi_ref, wo_ref)
        dsts = (gi_ref, go_ref)
        def cuts(core):
            return [(pl.ds(h * core, h), pl.ds(h * core, h // 2), pl.ds(h * core + h // 2, h - h // 2)) for h in halves]
        mine, theirs = cuts(c), cuts(1 - c)
        HALF, Q0, Q1 = 0, 1, 2

        def copy(idx, src, dst, flip):
            return pltpu.make_async_remote_copy(src_ref=src, dst_ref=dst, send_sem=send_sems.at[idx], recv_sem=recv_sems.at[idx],
                                                device_id=_peer(flip), device_id_type=MESH)

        def slot(a, shard, cut):
            return dsts[a].at[shard, cut]

        jx, jy, jd = j ^ 2, j ^ 1, j ^ 3
        sends = []

        def start(cp):
            cp.start()
            sends.append(cp)

        for a in range(ARR):
            start(copy(TO_X + a, srcs[a].at[mine[a][HALF]], slot(a, j, mine[a][HALF]), FLIP_X))
            start(copy(TO_Y + a, srcs[a].at[mine[a][HALF]], slot(a, j, mine[a][HALF]), FLIP_Y))
        own = [copy(OWN + a, srcs[a], dsts[a].at[j], FLIP_C) for a in range(ARR)]
        for cp in own:
            cp.start()
        for a in range(ARR):
            copy(TO_X + a, slot(a, jx, mine[a][HALF]), slot(a, jx, mine[a][HALF]), FLIP_X).wait_recv()
            start(copy(ON_Y + a, slot(a, jx, mine[a][Q0]), slot(a, jx, mine[a][Q0]), FLIP_Y))
            start(copy(SIB_X + a, slot(a, jx, mine[a][HALF]), slot(a, jx, mine[a][HALF]), FLIP_C))
        for a in range(ARR):
            copy(TO_Y + a, slot(a, jy, mine[a][HALF]), slot(a, jy, mine[a][HALF]), FLIP_Y).wait_recv()
            start(copy(ON_X + a, slot(a, jy, mine[a][Q1]), slot(a, jy, mine[a][Q1]), FLIP_X))
            start(copy(SIB_Y + a, slot(a, jy, mine[a][HALF]), slot(a, jy, mine[a][HALF]), FLIP_C))
        for a in range(ARR):
            copy(ON_Y + a, slot(a, jd, mine[a][Q0]), slot(a, jd, mine[a][Q0]), FLIP_Y).wait_recv()
            start(copy(SIB_D0 + a, slot(a, jd, mine[a][Q0]), slot(a, jd, mine[a][Q0]), FLIP_C))
        for a in range(ARR):
            copy(ON_X + a, slot(a, jd, mine[a][Q1]), slot(a, jd, mine[a][Q1]), FLIP_X).wait_recv()
            start(copy(SIB_D1 + a, slot(a, jd, mine[a][Q1]), slot(a, jd, mine[a][Q1]), FLIP_C))
        for a in range(ARR):
            for idx, shard, cut in ((SIB_X, jx, HALF), (SIB_Y, jy, HALF), (SIB_D0, jd, Q0), (SIB_D1, jd, Q1)):
                copy(idx + a, slot(a, shard, theirs[a][cut]), slot(a, shard, theirs[a][cut]), FLIP_C).wait_recv()
        for cp in own:
            cp.wait()
        for cp in sends:
            cp.wait_send()

    anyspec = pl.BlockSpec(memory_space=pl.ANY)
    gi, go = pl.pallas_call(
        body, name="gather_weights", in_specs=[anyspec] * 2, out_specs=[anyspec] * 2,
        out_shape=[jax.ShapeDtypeStruct((4,) + wi.shape, BF16), jax.ShapeDtypeStruct((4,) + wo.shape, BF16)],
        scratch_shapes=[pltpu.SemaphoreType.DMA((n_sems,)), pltpu.SemaphoreType.DMA((n_sems,))],
    )(wi, wo)
    w_in_t_full = gi.reshape((4 * wi.shape[0],) + wi.shape[1:])
    w_out_full = jnp.swapaxes(go.reshape((4 * wo.shape[0],) + wo.shape[1:]), 0, 1)
    return w_in_t_full, w_out_full


def _to_aligned(w_t):
    _, L, D = w_t.shape
    npair = FOX_HEADS // 2
    ff = w_t[ORIG_FF:ORIG_REST].reshape(npair, 2, L, D)
    ff = jnp.pad(ff, ((0, 0), (0, FF_STRIDE - 2), (0, 0), (0, 0))).reshape(npair * FF_STRIDE, L, D)
    ff = jnp.pad(ff, ((0, LANES - npair * FF_STRIDE), (0, 0), (0, 0)))
    return jnp.swapaxes(jnp.concatenate([w_t[:ORIG_FOX], w_t[ORIG_REST:], ff], axis=0), 0, 1)


def _from_aligned(dw_t):
    n, _, D = dw_t.shape
    npair = FOX_HEADS // 2
    ff = dw_t[:, PM:PM + npair * FF_STRIDE].reshape(n, npair, FF_STRIDE, D)[:, :, :2].reshape(n, FOX_HEADS, D)
    return jnp.swapaxes(jnp.concatenate([dw_t[:, :ORIG_FOX], ff, dw_t[:, ORIG_FOX:PM]], axis=1), 0, 1)


def _half_layers(name, stack, got):
    L, R, C = stack.shape
    half = L // 2
    tr = min(256, R)
    c = lax.axis_index("c")
    which = ((1 - c) if got is None else c).astype(jnp.int32).reshape(1)

    def body(c_ref, x_ref, *refs):
        if got is None:
            refs[0][...] = x_ref[...].astype(BF16)
        else:
            acc = x_ref[...] + refs[0][...].astype(F32)
            refs[1][...] = acc
            refs[2][...] = acc.astype(BF16)

    plain = pl.BlockSpec((1, tr, C), lambda l, i, c_ref: (l, i, 0))
    picked = pl.BlockSpec((1, tr, C), lambda l, i, c_ref: (c_ref[0] * half + l, i, 0))
    shp = lambda dt: jax.ShapeDtypeStruct((half, R, C), dt)
    grid_spec = pltpu.PrefetchScalarGridSpec(
        num_scalar_prefetch=1, grid=(half, R // tr),
        in_specs=[picked] + ([] if got is None else [plain]), out_specs=[plain] if got is None else [plain, plain])
    return pl.pallas_call(
        body, name=name, grid_spec=grid_spec, out_shape=[shp(BF16)] if got is None else [shp(F32), shp(BF16)],
        compiler_params=_cparams(dimension_semantics=("arbitrary", "arbitrary")),
    )(which, stack, *([] if got is None else [got]))


def _reduce_scatter(stack_m, stack_f, stack_o, shard_cols, shard_rows):
    j = _chip_index()
    half = DEPTH // 2
    stacks = (stack_m, stack_f, stack_o)
    give = [_half_layers("rs_give", s, None)[0] for s in stacks]
    got = _exchange("rs_d2d", give, (FLIP_C,) * len(stacks))
    (m32, mbf), (f32_, fbf), (o32, obf) = [_half_layers("rs_add_chip", s, g) for s, g in zip(stacks, got)]
    d_model = stack_m.shape[2]

    def in_shards(m, f):
        return _from_aligned(jnp.concatenate([m, f], axis=1)).reshape(4, shard_cols, half, d_model)

    def out_shards(o):
        return jnp.moveaxis(o.reshape(half, 4, shard_rows, o.shape[-1]), 1, 0)

    chip = [(in_shards(m32, f32_), in_shards(mbf, fbf), 0), (out_shards(o32), out_shards(obf), 1)]
    shard = lambda a, idx: lax.dynamic_index_in_dim(a, idx, axis=0, keepdims=False)
    via = []
    for _, bf, axis in chip:
        diag = shard(bf, j ^ 3)
        cut = diag.shape[axis] // 2
        via += [lax.slice_in_dim(diag, 0, cut, axis=axis), lax.slice_in_dim(diag, cut, 2 * cut, axis=axis)]
    handed = _exchange("rs_via", via, (FLIP_X, FLIP_Y) * len(chip))
    sends = []
    for a, (f32_sum, _, axis) in enumerate(chip):
        sends.append(_add_half_along("rs_add_via", shard(f32_sum, j ^ 2), handed[2 * a + 1], axis, 1))
        sends.append(_add_half_along("rs_add_via", shard(f32_sum, j ^ 1), handed[2 * a], axis, 0))
    got = _exchange("rs_ici", sends, (FLIP_X, FLIP_Y) * len(chip))
    own_in, own_out = [shard(f32_sum, j) for f32_sum, _, _ in chip]
    mine_in = _add_rows("rs_add_in", own_in, list(got[0:2]))
    mine_out = _add_into_half("rs_add_out", own_out, list(got[2:4]))
    sib_in, g_out = _share_halves(mine_in, mine_out)
    return (mine_in, sib_in), g_out


def _add_half_along(name, base, extra, axis, which):
    lanes = min(ROW_LANE_CHUNK, base.shape[2])
    assert base.shape[axis] == 2 * extra.shape[axis]
    blk = tuple(base.shape[d] // 2 if d == axis else base.shape[d] for d in range(2)) + (lanes,)

    def body(b_ref, e_ref, o_ref):
        x = b_ref[...]
        o_ref[...] = jnp.where(pl.program_id(0) == which, x + e_ref[...].astype(F32), x).astype(BF16)

    at = lambda i, k: (i, 0, k) if axis == 0 else (0, i, k)
    return pl.pallas_call(
        body, name=name, grid=(2, base.shape[2] // lanes),
        in_specs=[pl.BlockSpec(blk, at), pl.BlockSpec(blk, lambda i, k: (0, 0, k))], out_specs=pl.BlockSpec(blk, at),
        out_shape=jax.ShapeDtypeStruct(base.shape, BF16),
        compiler_params=_cparams(dimension_semantics=("arbitrary", "arbitrary")),
    )(base, extra)


def _add_rows(name, first, others):
    n = len(others)

    def body(*refs):
        acc = refs[0][...]
        for r in refs[1:1 + n]:
            acc = acc + r[...].astype(F32)
        refs[1 + n][...] = acc

    grid, spec = _row_lane_blocks(first.shape)
    return pl.pallas_call(
        body, name=name, grid=grid, in_specs=[spec(first.shape[1])] * (1 + n), out_specs=spec(first.shape[1]),
        out_shape=jax.ShapeDtypeStruct(first.shape, F32),
        compiler_params=_cparams(dimension_semantics=("arbitrary", "arbitrary")),
    )(first, *others)


ROW_LANE_CHUNK = 256


def _row_lane_blocks(shape):
    rows, _, C = shape
    tr = rows // 2 if rows % 2 == 0 and rows > 64 else rows
    lanes = min(ROW_LANE_CHUNK, C)
    return (rows // tr, C // lanes), lambda n_mid: pl.BlockSpec((tr, n_mid, lanes), lambda i, k, *_: (i, 0, k))


def _add_into_half(name, first, others):
    half, rows, C = first.shape
    tr = min(256, rows)
    n = len(others)

    def body(c_ref, *refs):
        acc = refs[0][...]
        for r in refs[1:1 + n]:
            acc = acc + r[...].astype(F32)
        refs[1 + n][...] = acc

    grid_spec = pltpu.PrefetchScalarGridSpec(
        num_scalar_prefetch=1, grid=(half, rows // tr),
        in_specs=[pl.BlockSpec((1, tr, C), lambda l, i, c_ref: (l, i, 0))] * (1 + n),
        out_specs=pl.BlockSpec((1, tr, C), lambda l, i, c_ref: (c_ref[0] * half + l, i, 0)))
    return pl.pallas_call(
        body, name=name, grid_spec=grid_spec, out_shape=jax.ShapeDtypeStruct((2 * half, rows, C), F32),
        compiler_params=_cparams(dimension_semantics=("arbitrary", "arbitrary")),
    )(lax.axis_index("c").astype(jnp.int32).reshape(1), first, *others)


def _share_halves(mine, buf):
    half = DEPTH // 2

    def body(mine_ref, buf_in, sib_ref, buf_ref, send_sems, recv_sems):
        lay = pl.ds(half * lax.axis_index("c"), half)
        copies = [pltpu.make_async_remote_copy(src_ref=src, dst_ref=dst, send_sem=send_sems.at[k], recv_sem=recv_sems.at[k],
                                               device_id=_peer(FLIP_C), device_id_type=MESH)
                  for k, (src, dst) in enumerate(((mine_ref, sib_ref), (buf_ref.at[lay], buf_ref.at[lay])))]
        for cp in copies:
            cp.start()
        for cp in copies:
            cp.wait()

    anyspec = pl.BlockSpec(memory_space=pl.ANY)
    return pl.pallas_call(
        body, name="rs_share", in_specs=[anyspec] * 2, out_specs=[anyspec] * 2,
        out_shape=[jax.ShapeDtypeStruct(mine.shape, mine.dtype), jax.ShapeDtypeStruct(buf.shape, buf.dtype)],
        input_output_aliases={1: 1},
        scratch_shapes=[pltpu.SemaphoreType.DMA((2,)), pltpu.SemaphoreType.DMA((2,))],
    )(mine, buf)


def _adamw_halves(w, g_mine, g_sib, m, v):
    half = g_mine.shape[1]

    def body(c_ref, w_ref, gm_ref, gs_ref, m_ref, v_ref, g_ref, d_ref, nm_ref, nv_ref):
        first = c_ref[0] == 0
        gm, gs = gm_ref[...], gs_ref[...]
        for h, gv in enumerate((jnp.where(first, gm, gs), jnp.where(first, gs, gm))):
            lay = slice(half * h, half * (h + 1))
            g_ref[:, lay, :] = gv
            d_ref[:, lay, :], nm_ref[:, lay, :], nv_ref[:, lay, :] = _adam_update(w_ref[:, lay, :], gv, m_ref[:, lay, :], v_ref[:, lay, :])

    grid, spec = _row_lane_blocks(w.shape)
    full, part = spec(w.shape[1]), spec(half)
    grid_spec = pltpu.PrefetchScalarGridSpec(num_scalar_prefetch=1, grid=grid, in_specs=[full, part, part, full, full], out_specs=[full] * 4)
    return pl.pallas_call(
        body, name="adamw_halves", grid_spec=grid_spec, out_shape=[jax.ShapeDtypeStruct(w.shape, F32)] * 4,
        compiler_params=_cparams(dimension_semantics=("arbitrary", "arbitrary")),
    )(lax.axis_index("c").astype(jnp.int32).reshape(1), w, g_mine, g_sib, m, v)


def _all_reduce_small(x):
    x = _exchange_add("ar_c", x, FLIP_C)
    x = _exchange_add("ar_y", x, FLIP_Y)
    return _exchange_add("ar_x", x, FLIP_X)


def _blocks(S):
    return dict(tm=min(512, S), tm_proj=min(1024, S), ts=min(512, S), tq=min(512, S), tq_big=min(1024, S), tk=min(512, S), tks=min(256, S))


def _pair_pad(vec):
    npair = FOX_HEADS // 2
    v = jnp.pad(vec.reshape(npair, 2), ((0, 0), (0, FF_STRIDE - 2))).reshape(1, npair * FF_STRIDE)
    return jnp.pad(v, ((0, 0), (0, LANES - npair * FF_STRIDE)))


def _pair_unpad(row):
    npair = FOX_HEADS // 2
    return row[0, :npair * FF_STRIDE].reshape(npair, FF_STRIDE)[:, :2].reshape(FOX_HEADS)


def _pool_blockdiag(w_pool):
    g, cg, _ = w_pool.shape
    eye = jnp.eye(g, dtype=w_pool.dtype)
    return jnp.einsum("gh,gcd->gchd", eye, w_pool).reshape(g * cg, g * cg)


QK_BOUND_SLACK = 1.05


def _layer_params(norm_g, b_f, q_norm_g, k_norm_g, w_pool, pool_scale):
    qk_bound = QK_BOUND_SLACK * HEAD_DIM * QK_SCALE * jnp.max(jnp.abs(q_norm_g)) * jnp.max(jnp.abs(k_norm_g))
    return dict(g=norm_g.reshape(1, -1), qg=jnp.tile(q_norm_g, FOX_HEADS).reshape(1, FOX_W), kg=jnp.tile(k_norm_g, FOX_HEADS).reshape(1, FOX_W),
                bfp=_pair_pad(b_f), wpd=_pool_blockdiag(w_pool).astype(BF16), ps=pool_scale.reshape(1, POOL_W),
                qkb=jnp.full((1, LANES), qk_bound, F32))


def _layer_fwd(x, wt_all, w_out, layer, prm, bs):
    projm, ffo, h = _inproj(x, prm["g"], wt_all, layer, tm=bs["tm_proj"], tn=PROJ_TN)
    qn, ka, kb, v, sq, sk, sv, pooled, yp, pm = _prep(projm, ffo, prm["qg"], prm["kg"], prm["bfp"], prm["wpd"], prm["ps"], ts=bs["ts"])
    o, lse, fm = _fox_fwd(qn, ka, kb, v, projm, prm["qkb"], tq=bs["tq"], tk=bs["tk"])
    so, sm = _sb_fwd(sq, sk, sv, projm, tq=bs["tq_big"], tk=bs["tks"])
    y = _outproj(x, fm, pm, sm, w_out, layer, tm=bs["tm"])
    saved = dict(x=x, projm=projm, ffo=ffo, h=h, qn=qn, ka=ka, kb=kb, v=v, sq=sq, sk=sk, sv=sv, pooled=pooled, yp=yp,
                 o=o, lse=lse, so=so, fm=fm, pm=pm, sm=sm)
    return y, saved


def _layer_bwd(dy, wt_all, w_out, prm, sv_, bs, layer, stacks):
    dmix, stack_o = _outproj_bwd(dy, sv_["fm"], sv_["pm"], sv_["sm"], w_out, layer, None if stacks is None else stacks[2:], tm=bs["tm"])
    dqn, dkn, dv, dfg, dct, dcr = _fox_bwd(sv_["qn"], sv_["ka"], sv_["kb"], sv_["v"], sv_["o"], sv_["lse"], dmix, sv_["projm"],
                                      prm["qkb"], tq=bs["tq_big"], tk=bs["tk"])
    dsq, dsk, dsv, dsg = _sb_bwd(sv_["sq"], sv_["sk"], sv_["sv"], sv_["so"], dmix, sv_["projm"], tq=bs["tq"], tk=bs["tks"])
    dproj, dqg, dkg, dbf, dwp, dps = _prep_bwd(sv_["projm"], sv_["ffo"], dqn, dkn, dct, dcr, dv, dfg, dsq, dsk, dsv, dsg, dmix,
                                               sv_["pooled"], sv_["yp"], prm["qg"], prm["kg"], prm["bfp"], prm["wpd"], prm["ps"], ts=bs["ts"])
    stack_m, stack_f = _inproj_dw(sv_["h"], dproj, layer, None if stacks is None else stacks[:2], ts=bs["tm_proj"], tn=PROJ_TN)
    dx, dg = _inproj_dx(dproj, wt_all, layer, sv_["x"], prm["g"], dy, tm=min(256, bs["tm"]))
    grads = dict(
        norm_g=dg[0],
        b_f=_pair_unpad(dbf), q_norm_g=dqg.reshape(FOX_HEADS, HEAD_DIM).sum(0), k_norm_g=dkg.reshape(FOX_HEADS, HEAD_DIM).sum(0),
        w_pool=jnp.stack([dwp[HEAD_DIM * g:HEAD_DIM * (g + 1), HEAD_DIM * g:HEAD_DIM * (g + 1)] for g in range(4)]),
        pool_scale=dps[0])
    return dx, grads, (stack_m, stack_f, stack_o)


def _local_step(x, target, wt_all, w_out, norm_g, b_f, q_norm_g, k_norm_g, w_pool, pool_scale):
    S, D = x.shape
    bs = _blocks(S)
    prms = [_layer_params(norm_g[l], b_f[l], q_norm_g[l], k_norm_g[l], w_pool[l], pool_scale[l]) for l in range(DEPTH)]
    saved = []
    y = x
    for l in range(DEPTH):
        y, s_ = _layer_fwd(y, wt_all, w_out, l, prms[l], bs)
        saved.append(s_)
    dy, sq = _loss_head(y, target, tm=bs["tm"])
    loss = 0.5 * jnp.sum(sq) / D
    grads = [None] * DEPTH
    stacks = None
    for l in reversed(range(DEPTH)):
        dy, grads[l], stacks = _layer_bwd(dy, wt_all, w_out, prms[l], saved[l], bs, l, stacks)
    stacked = {k: jnp.stack([g[k] for g in grads]) for k in grads[0]}
    return loss, dy, stacked, stacks


SMALL = ("norm_g", "b_f", "q_norm_g", "k_norm_g", "w_pool", "pool_scale")


def _pack_small(gr):
    flat = jnp.concatenate([gr[k].reshape(-1) for k in SMALL])
    pad = (-flat.shape[0]) % (8 * LANES)
    return jnp.pad(flat, (0, pad)).reshape(-1, LANES)


def _unpack_small(packed, like):
    flat = packed.reshape(-1)
    out, off = {}, 0
    for k in SMALL:
        n = like[k].size
        out[k] = flat[off:off + n].reshape(like[k].shape)
        off += n
    return out


def kernel(x, norm_g, w_in, b_f, q_norm_g, k_norm_g, w_pool, pool_scale, w_out, loss_target, m_norm_g, m_w_in, m_b_f, m_q_norm_g, m_k_norm_g, m_w_pool, m_pool_scale, m_w_out, v_norm_g, v_w_in, v_b_f, v_q_norm_g, v_k_norm_g, v_w_pool, v_pool_scale, v_w_out):
    weights = dict(norm_g=norm_g, w_in=w_in, b_f=b_f, q_norm_g=q_norm_g, k_norm_g=k_norm_g, w_pool=w_pool, pool_scale=pool_scale, w_out=w_out)
    mom_m = dict(norm_g=m_norm_g, w_in=m_w_in, b_f=m_b_f, q_norm_g=m_q_norm_g, k_norm_g=m_k_norm_g, w_pool=m_w_pool, pool_scale=m_pool_scale, w_out=m_w_out)
    mom_v = dict(norm_g=v_norm_g, w_in=v_w_in, b_f=v_b_f, q_norm_g=v_q_norm_g, k_norm_g=v_k_norm_g, w_pool=v_w_pool, pool_scale=v_pool_scale, w_out=v_w_out)
    shard_cols = w_in.shape[2]
    shard_rows = w_out.shape[1]

    cols_first = lambda a: jnp.transpose(a, (2, 0, 1))
    w_in_t = cols_first(w_in)
    w_in_t_full, w_out_full = _gather_weights(w_in_t, w_out)
    wt_all = _to_aligned(w_in_t_full)
    loss, dx, gr, stacks = _local_step(x[0], loss_target[0], wt_all, w_out_full, norm_g, b_f, q_norm_g, k_norm_g, w_pool, pool_scale)
    loss = lax.psum(loss, ("x", "y", "c"))

    (g_in_mine, g_in_sib), g_w_out = _reduce_scatter(*stacks, shard_cols, shard_rows)
    small = _unpack_small(_all_reduce_small(_pack_small(gr)), {k: weights[k] for k in SMALL})
    grad_w = dict(small, w_out=g_w_out)

    names = ("norm_g", "w_in", "b_f", "q_norm_g", "k_norm_g", "w_pool", "pool_scale", "w_out")
    upd = {k: _adamw_nd(weights[k], grad_w[k], mom_m[k], mom_v[k]) for k in names if k != "w_in"}
    in_t = _adamw_halves(w_in_t, g_in_mine, g_in_sib, cols_first(mom_m["w_in"]), cols_first(mom_v["w_in"]))
    grad_w["w_in"], *upd["w_in"] = [jnp.transpose(a, (1, 2, 0)) for a in in_t]
    return (loss, dx[None], *[grad_w[k] for k in names], *[upd[k][0] for k in names], *[upd[k][1] for k in names], *[upd[k][2] for k in names])
```

```python
import functools

import jax
import jax.numpy as jnp
from jax import lax
from jax.experimental import pallas as pl
from jax.experimental.pallas import tpu as pltpu

F32 = jnp.float32
BF16 = jnp.bfloat16

DEPTH = 4
HEAD_DIM = 64
FOX_HEADS = 8
SB_HEADS = 4
FOX_W = FOX_HEADS * HEAD_DIM
SB_W = SB_HEADS * HEAD_DIM
POOL_W = 256
POOL_WINDOWS = (2, 4, 8, 16)
POOL_HALO = 16
D_MIX = FOX_W + POOL_W + SB_W
EPS = 1e-6
NEG = -1e30
QK_SCALE = HEAD_DIM ** -0.5

ORIG_FOX = 4 * FOX_W
ORIG_FF = ORIG_FOX
ORIG_REST = ORIG_FF + FOX_HEADS
D_IN = ORIG_REST + 2 * POOL_W + 4 * SB_W

C_FQ, C_FK, C_FV, C_FG = 0, FOX_W, 2 * FOX_W, 3 * FOX_W
C_PX = 4 * FOX_W
C_PG = C_PX + POOL_W
C_SQ = C_PG + POOL_W
C_SK, C_SV, C_SG = C_SQ + SB_W, C_SQ + 2 * SB_W, C_SQ + 3 * SB_W
PM = C_SG + SB_W
LANES = 128
PW = PM + LANES
FF_STRIDE = 8
AUG = 3

ADAM_LR = 0.001
ADAM_B1 = 0.9
ADAM_B2 = 0.999
ADAM_EPS = 1e-08
ADAM_WD = 0.01
ADAM_STEP = 10

VMEM_LIMIT = 48 * 1024 * 1024
PROJ_TN = PM // 2


def _cparams(**kw):
    return pltpu.CompilerParams(vmem_limit_bytes=VMEM_LIMIT, **kw)


def _dot(a, b):
    return jnp.dot(a, b, preferred_element_type=F32)


def _dot_nt(a, b):
    return lax.dot_general(a, b, (((1,), (1,)), ((), ())), preferred_element_type=F32)


def _dot_tn(a, b):
    return lax.dot_general(a, b, (((0,), (0,)), ((), ())), preferred_element_type=F32)


def _split2(x):
    hi = x.astype(BF16)
    lo = (x - hi.astype(F32)).astype(BF16)
    return hi, lo


def _split3(x):
    hi = x.astype(BF16)
    r = x - hi.astype(F32)
    mid = r.astype(BF16)
    lo = (r - mid.astype(F32)).astype(BF16)
    return hi, mid, lo


def _dot_exact_rhs(x, m):
    hi, mid, lo = _split3(x)
    return _dot(hi, m) + _dot(mid, m) + _dot(lo, m)


def _dot_exact_lhs(m, x):
    hi, mid, lo = _split3(x)
    return _dot(m, hi) + _dot(m, mid) + _dot(m, lo)


def _sigmoid(x):
    return 1.0 / (1.0 + jnp.exp(-x))


def _silu_pair(x):
    s = _sigmoid(x)
    return x * s, s * (1.0 + x * (1.0 - s))


def _iota(shape, dim):
    return lax.broadcasted_iota(jnp.int32, shape, dim)


def _ones_where(cond):
    return jnp.where(cond, 1.0, 0.0).astype(BF16)


GROUP_SLAB = 256


def _head_blockdiag():
    rows, cols = _iota((2 * GROUP_SLAB, GROUP_SLAB), 0) & (GROUP_SLAB - 1), _iota((2 * GROUP_SLAB, GROUP_SLAB), 1)
    return _ones_where((rows >> 6) == (cols >> 6))


def _group_sum(x, bd):
    hi, lo = _split2(x)
    slabs = [_dot(jnp.concatenate([hi[:, s:s + GROUP_SLAB], lo[:, s:s + GROUP_SLAB]], axis=1), bd) for s in range(0, x.shape[1], GROUP_SLAB)]
    return jnp.concatenate(slabs, axis=1)


def _lane_pick(x, lane_idx, lane):
    return jnp.sum(jnp.where(lane_idx == lane, x, 0.0), axis=1, keepdims=True)


def _inproj(x, g, wt_all, layer, *, tm, tn):
    S, D = x.shape
    nj = PM // tn

    def body(x_ref, g_ref, w_ref, wff_ref, proj_ref, ff_ref, h_ref):
        @pl.when(pl.program_id(1) == 0)
        def _():
            xf = x_ref[...]
            ms = jnp.mean(xf * xf, axis=-1, keepdims=True)
            h = (xf * lax.rsqrt(ms + EPS) * g_ref[...]).astype(BF16)
            h_ref[...] = h
            ff_ref[...] = _dot_nt(h, wff_ref[...])

        proj_ref[...] = _dot_nt(h_ref[...], w_ref[...])

    return pl.pallas_call(
        body, name="inproj", grid=(S // tm, nj),
        in_specs=[pl.BlockSpec((tm, D), lambda i, j: (i, 0)),
                  pl.BlockSpec((1, D), lambda i, j: (0, 0)),
                  pl.BlockSpec((None, tn, D), lambda i, j: (layer, j, 0)),
                  pl.BlockSpec((None, LANES, D), lambda i, j: (layer, PM // LANES, 0))],
        out_specs=[pl.BlockSpec((tm, tn), lambda i, j: (i, j)),
                   pl.BlockSpec((tm, LANES), lambda i, j: (i, 0)),
                   pl.BlockSpec((tm, D), lambda i, j: (i, 0))],
        out_shape=[jax.ShapeDtypeStruct((S, PM), F32), jax.ShapeDtypeStruct((S, LANES), F32),
                   jax.ShapeDtypeStruct((S, D), BF16)],
        compiler_params=_cparams(dimension_semantics=("arbitrary", "arbitrary")),
    )(x, g, wt_all, wt_all)


def _pool_group_select(lane_group, vals):
    return jnp.where(lane_group == 0, vals[0], jnp.where(lane_group == 1, vals[1], jnp.where(lane_group == 2, vals[2], vals[3])))


def _prep(projm, ffo, qg, kg, bfp, wpd, ps, *, ts):
    S = projm.shape[0]
    nb = S // ts
    hb = ts // POOL_HALO

    def body(fq_ref, fk_ref, fv_ref, pp_ref, halo_ref, ff_ref, sq_ref, sk_ref, sv_ref,
             qg_ref, kg_ref, bf_ref, wpd_ref, ps_ref,
             qn_ref, ka_ref, kb_ref, v_ref, sqo_ref, sko_ref, svo_ref, pooled_ref, yp_ref, pm_ref,
             carry_ref, c_ref, buf_ref):
        i = pl.program_id(0)
        bd = _head_blockdiag()
        normed = []
        for src, g_ref in ((fq_ref, qg_ref), (fk_ref, kg_ref)):
            q = src[...]
            ss = _group_sum(q * q, bd)
            normed.append(q * lax.rsqrt(ss * (1.0 / HEAD_DIM) + EPS) * g_ref[...])
        qn_ref[...] = (normed[0] * QK_SCALE).astype(BF16)
        kn = normed[1]
        v_ref[...] = fv_ref[...].astype(BF16)
        sqo_ref[...] = (sq_ref[...] * QK_SCALE).astype(BF16)
        sko_ref[...] = sk_ref[...].astype(BF16)
        svo_ref[...] = sv_ref[...].astype(BF16)

        @pl.when(i == 0)
        def _():
            carry_ref[...] = jnp.zeros_like(carry_ref)

        z = ff_ref[...] + bf_ref[...]
        lf = jnp.minimum(z, 0.0) - jnp.log(1.0 + jnp.exp(-jnp.abs(z)))
        tri = _ones_where(_iota((ts, ts), 1) <= _iota((ts, ts), 0))
        c = _dot_exact_lhs(tri, lf) + carry_ref[...]
        c_ref[...] = c
        carry_ref[...] = c_ref[ts - 1:ts, :]
        parts = jnp.concatenate(_split3(-c), axis=1)
        row = _iota((AUG * LANES, FOX_W), 0)
        col = _iota((AUG * LANES, FOX_W), 1)
        part, src = row >> 7, row & (LANES - 1)
        pair, off = col >> 7, col & (LANES - 1)
        sel_a = _ones_where((src == FF_STRIDE * pair) & (off == HEAD_DIM + part))
        sel_b = _ones_where((src == FF_STRIDE * pair + 1) & (off == part))
        first_half = (_iota((1, FOX_W), 1) & HEAD_DIM) == 0
        ka_ref[...] = jnp.where(first_half, kn, _dot(parts, sel_a)).astype(BF16)
        kb_ref[...] = jnp.where(first_half, _dot(parts, sel_b), kn).astype(BF16)

        x = pp_ref[:, 0:POOL_W]
        pg = pp_ref[:, POOL_W:2 * POOL_W]
        halo = jnp.where(i > 0, halo_ref[:, 0:POOL_W], 0.0)
        buf_ref[0:POOL_HALO, :] = halo
        buf_ref[POOL_HALO:POOL_HALO + ts, :] = x
        acc = x
        snaps = []
        for d in range(1, POOL_HALO):
            acc = acc + buf_ref[pl.ds(POOL_HALO - d, ts), :]
            if d + 1 in POOL_WINDOWS:
                snaps.append(acc)
        lane_group = _iota((1, POOL_W), 1) >> 6
        wsum = _pool_group_select(lane_group, snaps)
        wlen = _pool_group_select(lane_group, [float(w) for w in POOL_WINDOWS])
        tpos = (i * ts + _iota((ts, 1), 0) + 1).astype(F32)
        pooled = wsum / jnp.minimum(tpos, wlen) - x
        pb = pooled.astype(BF16)
        pooled_ref[...] = pb
        yp = _dot(pb, wpd_ref[...])
        yp_ref[...] = yp
        pm_ref[...] = (yp * ps_ref[...] * (pg * _sigmoid(pg))).astype(BF16)

    blk = lambda w, c: pl.BlockSpec((ts, w), lambda i: (i, c))
    full = lambda a: pl.BlockSpec(a.shape, lambda i: (0,) * a.ndim)
    out_shapes = [
        jax.ShapeDtypeStruct((S, FOX_W), BF16), jax.ShapeDtypeStruct((S, FOX_W), BF16), jax.ShapeDtypeStruct((S, FOX_W), BF16),
        jax.ShapeDtypeStruct((S, FOX_W), BF16),
        jax.ShapeDtypeStruct((S, SB_W), BF16), jax.ShapeDtypeStruct((S, SB_W), BF16), jax.ShapeDtypeStruct((S, SB_W), BF16),
        jax.ShapeDtypeStruct((S, POOL_W), BF16), jax.ShapeDtypeStruct((S, POOL_W), F32), jax.ShapeDtypeStruct((S, POOL_W), BF16),
    ]
    out_specs = [
        blk(FOX_W, 0), blk(FOX_W, 0), blk(FOX_W, 0), blk(FOX_W, 0),
        blk(SB_W, 0), blk(SB_W, 0), blk(SB_W, 0),
        blk(POOL_W, 0), blk(POOL_W, 0), blk(POOL_W, 0),
    ]
    return pl.pallas_call(
        body, name="prep", grid=(nb,),
        in_specs=[blk(FOX_W, C_FQ // FOX_W), blk(FOX_W, C_FK // FOX_W), blk(FOX_W, C_FV // FOX_W), blk(2 * POOL_W, C_PX // (2 * POOL_W)),
                  pl.BlockSpec((POOL_HALO, 2 * POOL_W), lambda i: (jnp.maximum(i * hb - 1, 0), C_PX // (2 * POOL_W))),
                  blk(LANES, 0),
                  blk(SB_W, C_SQ // SB_W), blk(SB_W, C_SK // SB_W), blk(SB_W, C_SV // SB_W),
                  full(qg), full(kg), full(bfp), full(wpd), full(ps)],
        out_specs=out_specs, out_shape=out_shapes,
        scratch_shapes=[pltpu.VMEM((1, LANES), F32), pltpu.VMEM((ts, LANES), F32), pltpu.VMEM((ts + POOL_HALO, POOL_W), F32)],
        compiler_params=_cparams(dimension_semantics=("arbitrary",)),
    )(projm, projm, projm, projm, projm, ffo, projm, projm, projm, qg, kg, bfp, wpd, ps)


def _pair_masks(x):
    ma = _iota((1, LANES), 1) < HEAD_DIM
    zero = jnp.zeros_like(x)
    return jnp.where(ma, x, zero), jnp.where(ma, zero, x)


DIAG_TILE = 256


def _diag_tiles(tq, size=DIAG_TILE):
    size = min(tq, size)
    return [(t * size, size) for t in range(tq // size)]


def _put_rows(old, new, r0):
    return new if r0 == 0 else jnp.concatenate([old[:r0], new], axis=0)


def _aug_queries(q):
    lane = _iota((1, LANES), 1)
    one = jnp.ones_like(q)
    zero = jnp.zeros_like(q)
    qa = jnp.where(lane < HEAD_DIM, q, jnp.where(lane < HEAD_DIM + AUG, one, zero))
    qb = jnp.where(lane >= HEAD_DIM, q, jnp.where(lane < AUG, one, zero))
    return qa, qb


EXP_DEAD = -105.0
PACK = 16


def _fox_walk_left(nfull, tk, block, carry, k_refs, qk_bound, row_floor):
    lane = _iota((1, LANES), 1)

    def alive(h, jj, c):
        k0 = pl.multiple_of(jnp.maximum(nfull - 1 - jj, 0) * tk + tk - PACK, PACK)
        last = k_refs[h][pl.ds(k0, PACK), :].astype(F32)
        lo = HEAD_DIM if h == 0 else 0
        negc = jnp.sum(jnp.where((lane >= lo) & (lane < lo + AUG), last, 0.0), axis=1, keepdims=True)
        return qk_bound + jnp.max(negc) - row_floor(c)[h] >= EXP_DEAD

    def walk(heads, jj0, c0):
        def go_on(state):
            jj, c = state
            ok = jj < nfull
            for h in heads:
                ok = ok & alive(h, jj, c)
            return ok

        def step(state):
            jj, c = state
            return jj + 1, block(pl.multiple_of((nfull - 1 - jj) * tk, tk), tk, 0, c, False, heads)

        return lax.while_loop(go_on, step, (jj0, c0))

    jj_pair, carry = walk((0, 1), jnp.int32(0), carry)
    carry = walk((0,), jj_pair, carry)[1]
    return walk((1,), jj_pair, carry)[1]


def _fox_fwd(qn, ka, kb, v, projm, qkb, *, tq, tk):
    S = qn.shape[0]
    npair = FOX_HEADS // 2

    def body(q_ref, ka_ref, kb_ref, v_ref, fg_ref, qkb_ref, o_ref, lse_ref, fm_ref):
        qi = pl.program_id(1)
        lane = _iota((1, LANES), 1)
        ma = lane < HEAD_DIM
        qaug = _aug_queries(q_ref[...])
        k_refs = (ka_ref, kb_ref)

        def block(k0, tkl, r0, carry, masked, heads=(0, 1)):
            vb = v_ref[pl.ds(k0, tkl), :]
            if masked:
                mask = (k0 + _iota((tq - r0, tkl), 1)) <= (qi * tq + r0 + _iota((tq - r0, tkl), 0))
            scores = {h: _dot_nt(qaug[h][r0:], k_refs[h][pl.ds(k0, tkl), :]) for h in heads}
            new = list(carry)
            for h in heads:
                m, l, acc = [x[r0:] for x in carry[h]]
                s = jnp.where(mask, scores[h], NEG) if masked else scores[h]
                m_new = jnp.maximum(m, jnp.max(s, axis=1, keepdims=True))
                alpha = jnp.exp(m - m_new)
                p = jnp.exp(s - m_new)
                sub = (m_new, alpha * l + jnp.sum(p, axis=1, keepdims=True), alpha * acc + _dot(p.astype(BF16), vb))
                new[h] = tuple(_put_rows(old, x, r0) for old, x in zip(carry[h], sub))
            return tuple(new)

        carry = tuple((jnp.full((tq, 1), NEG, F32), jnp.zeros((tq, 1), F32), jnp.zeros((tq, LANES), F32)) for _ in range(2))
        for off, size in _diag_tiles(tq, tq):
            carry = block(pl.multiple_of(qi * tq + off, size), size, off, carry, True)
        carry = _fox_walk_left((qi * tq) // tk, tk, block, carry, k_refs, jnp.max(qkb_ref[...]),
                               lambda c: (jnp.min(c[0][0]), jnp.min(c[1][0])))
        (ma_, la, acca), (mb_, lb, accb) = carry
        o = jnp.where(ma, acca / la, accb / lb)
        o_ref[...] = o
        lse_ref[...] = jnp.where(ma, ma_ + jnp.log(la), mb_ + jnp.log(lb))
        fg = fg_ref[...]
        fm_ref[...] = (o * (fg * _sigmoid(fg))).astype(BF16)

    qblk = pl.BlockSpec((tq, LANES), lambda p, i: (i, p))
    kvblk = pl.BlockSpec((S, LANES), lambda p, i: (0, p))
    return pl.pallas_call(
        body, name="fox_fwd", grid=(npair, S // tq),
        in_specs=[qblk, kvblk, kvblk, kvblk,
                  pl.BlockSpec((tq, LANES), lambda p, i: (i, C_FG // LANES + p)),
                  pl.BlockSpec((1, LANES), lambda p, i: (0, 0))],
        out_specs=[qblk, qblk, qblk],
        out_shape=[jax.ShapeDtypeStruct((S, FOX_W), F32), jax.ShapeDtypeStruct((S, FOX_W), F32), jax.ShapeDtypeStruct((S, FOX_W), BF16)],
        compiler_params=_cparams(dimension_semantics=("arbitrary", "arbitrary")),
    )(qn, ka, kb, v, projm, qkb)


def _suffix_sums(x, tmat2):
    return _dot(jnp.concatenate(_split2(x), axis=1), tmat2)


def _suffix_matrix(tk, inclusive):
    rr, cc = _iota((2 * tk, tk), 0) & (tk - 1), _iota((2 * tk, tk), 1)
    return _ones_where(rr >= cc) if inclusive else _ones_where(rr > cc)


def _sb_scores(qh, kb, causal, tmat2, r_runs):
    heads = range(2)
    zs = [_dot_nt(qh[h], kb) for h in heads]
    nsps = [jnp.minimum(-z, 0.0) - jnp.log(1.0 + jnp.exp(-jnp.abs(z))) for z in zs]
    lbs = nsps if causal is None else [jnp.where(causal, n, 0.0) for n in nsps]
    rins = [_suffix_sums(lb, tmat2) for lb in lbs]
    args = [zs[h] + lbs[h] + (rins[h] + r_runs[h]) for h in heads]
    a_s = [jnp.exp(arg if causal is None else jnp.where(causal, arg, NEG)) for arg in args]
    return zs, nsps, lbs, a_s


def _sb_walk_left(nfull, tk, block, carry, running_sums):
    def alive(state):
        jj, c = state
        ra, rb = running_sums(c)
        return (jj < nfull) & (jnp.max(jnp.maximum(ra, rb)) >= EXP_DEAD)

    def step(state):
        jj, c = state
        return jj + 1, block(pl.multiple_of((nfull - 1 - jj) * tk, tk), 0, c, False)

    return lax.while_loop(alive, step, (jnp.int32(0), carry))[1]


def _sb_fwd(sq, sk, sv, projm, *, tq, tk):
    S = sq.shape[0]
    npair = SB_HEADS // 2

    def body(q_ref, k_ref, v_ref, sg_ref, o_ref, sm_ref):
        qi = pl.program_id(1)
        lane = _iota((1, LANES), 1)
        ma = lane < HEAD_DIM
        qh = _pair_masks(q_ref[...])
        tmat2 = _suffix_matrix(tk, inclusive=False)
        nfull = (qi * tq) // tk

        def block(k0, r0, carry, masked):
            nr = tq - r0
            kb = k_ref[pl.ds(k0, tk), :]
            vb = v_ref[pl.ds(k0, tk), :]
            causal = (k0 + _iota((nr, tk), 1)) < (qi * tq + r0 + _iota((nr, tk), 0)) if masked else None
            _, _, lbs, a_s = _sb_scores([q[r0:] for q in qh], kb, causal, tmat2, [carry[h][0][r0:] for h in range(2)])
            pv = _dot(jnp.concatenate([a.astype(BF16) for a in a_s], axis=0), vb)
            return tuple((_put_rows(carry[h][0], carry[h][0][r0:] + jnp.sum(lbs[h], axis=1, keepdims=True), r0),
                          _put_rows(carry[h][1], carry[h][1][r0:] + pv[h * nr:(h + 1) * nr], r0)) for h in range(2))

        carry = tuple((jnp.zeros((tq, 1), F32), jnp.zeros((tq, LANES), F32)) for _ in range(2))
        for off, size in reversed(_diag_tiles(tq)):
            assert size == tk
            carry = block(pl.multiple_of(qi * tq + off, tk), off, carry, True)
        (_, acca), (_, accb) = _sb_walk_left(nfull, tk, block, carry, lambda c: (c[0][0], c[1][0]))
        o = jnp.where(ma, acca, accb)
        o_ref[...] = o
        sg = sg_ref[...]
        sm_ref[...] = (o * (sg * _sigmoid(sg))).astype(BF16)

    qblk = pl.BlockSpec((tq, LANES), lambda p, i: (i, p))
    kvblk = pl.BlockSpec((S, LANES), lambda p, i: (0, p))
    return pl.pallas_call(
        body, name="sb_fwd", grid=(npair, S // tq),
        in_specs=[qblk, kvblk, kvblk, pl.BlockSpec((tq, LANES), lambda p, i: (i, C_SG // LANES + p))],
        out_specs=[qblk, qblk],
        out_shape=[jax.ShapeDtypeStruct((S, SB_W), F32), jax.ShapeDtypeStruct((S, SB_W), BF16)],
        compiler_params=_cparams(dimension_semantics=("arbitrary", "arbitrary")),
    )(sq, sk, sv, projm)


def _outproj(x, fm, pm, sm, w_out, layer, *, tm):
    S, D = x.shape

    def body(x_ref, fm_ref, pm_ref, sm_ref, w_ref, y_ref):
        y = x_ref[...] + _dot(fm_ref[...], w_ref[0:FOX_W, :])
        y = y + _dot(pm_ref[...], w_ref[FOX_W:FOX_W + POOL_W, :])
        y_ref[...] = y + _dot(sm_ref[...], w_ref[FOX_W + POOL_W:D_MIX, :])

    row = lambda w: pl.BlockSpec((tm, w), lambda i: (i, 0))
    return pl.pallas_call(
        body, name="outproj", grid=(S // tm,),
        in_specs=[row(D), row(FOX_W), row(POOL_W), row(SB_W), pl.BlockSpec((None, D_MIX, D), lambda i: (layer, 0, 0))],
        out_specs=row(D), out_shape=jax.ShapeDtypeStruct((S, D), F32),
        compiler_params=_cparams(dimension_semantics=("arbitrary",)),
    )(x, fm, pm, sm, w_out)


def _loss_head(y, target, *, tm):
    S, D = y.shape

    def body(y_ref, t_ref, dy_ref, sq_ref):
        @pl.when(pl.program_id(0) == 0)
        def _():
            sq_ref[...] = jnp.zeros_like(sq_ref)

        d = y_ref[...] - t_ref[...]
        dy_ref[...] = d * (1.0 / D)
        sq_ref[...] += jnp.sum(d * d, axis=0, keepdims=True)

    row = pl.BlockSpec((tm, D), lambda i: (i, 0))
    return pl.pallas_call(
        body, name="loss_head", grid=(S // tm,),
        in_specs=[row, row], out_specs=[row, pl.BlockSpec((1, D), lambda i: (0, 0))],
        out_shape=[jax.ShapeDtypeStruct((S, D), F32), jax.ShapeDtypeStruct((1, D), F32)],
        compiler_params=_cparams(dimension_semantics=("arbitrary",)),
    )(y, target)


def _outproj_bwd(dy, fm, pm, sm, w_out, layer, stacks, *, tm):
    S, D = dy.shape

    def body(dy_ref, fm_ref, pm_ref, sm_ref, w_ref, dm_ref, dw_ref):
        @pl.when(pl.program_id(0) == 0)
        def _():
            dw_ref[...] = jnp.zeros_like(dw_ref)

        dyb = dy_ref[...].astype(BF16)
        dm_ref[...] = _dot_nt(dyb, w_ref[...])
        dw_ref[0:FOX_W, :] += _dot_tn(fm_ref[...], dyb)
        dw_ref[FOX_W:FOX_W + POOL_W, :] += _dot_tn(pm_ref[...], dyb)
        dw_ref[FOX_W + POOL_W:D_MIX, :] += _dot_tn(sm_ref[...], dyb)

    row = lambda w: pl.BlockSpec((tm, w), lambda i: (i, 0))
    wspec = pl.BlockSpec((None, D_MIX, D), lambda i: (layer, 0, 0))
    return _stack_call(
        body, "outproj_bwd", (S // tm,), [row(D), row(FOX_W), row(POOL_W), row(SB_W), wspec], (dy, fm, pm, sm, w_out),
        [pl.BlockSpec((None, D_MIX, D), lambda i: (layer, 0, 0))], [(D_MIX, D)], stacks,
        plain_specs=[row(D_MIX)], plain_shapes=[jax.ShapeDtypeStruct((S, D_MIX), F32)],
        compiler_params=_cparams(dimension_semantics=("arbitrary",)))


def _fox_bwd(qn, ka, kb, v, o, lse, dmix, projm, qkb, *, tq, tk):
    S = qn.shape[0]
    npair = FOX_HEADS // 2

    def body(q_ref, ka_ref, kb_ref, v_ref, o_ref, lse_ref, dm_ref, fg_ref, qkb_ref,
             dq_ref, dk_ref, dv_ref, dfg_ref, dct_ref, dcr_ref):
        qi = pl.program_id(1)

        @pl.when(qi == 0)
        def _():
            dk_ref[...] = jnp.zeros_like(dk_ref)
            dv_ref[...] = jnp.zeros_like(dv_ref)
            dct_ref[...] = jnp.zeros_like(dct_ref)

        lane = _iota((1, LANES), 1)
        ma = lane < HEAD_DIM
        qh = _pair_masks(q_ref[...])
        qaug = _aug_queries(q_ref[...])
        k_refs = (ka_ref, kb_ref)
        lsev = lse_ref[...]
        lse = (_lane_pick(lsev, lane, 0), _lane_pick(lsev, lane, HEAD_DIM))
        fg = fg_ref[...]
        silu, dsilu = _silu_pair(fg)
        dm = dm_ref[...]
        ov = o_ref[...]
        do = dm * silu
        dfg_ref[...] = dm * ov * dsilu
        dd = do * ov
        dsum = (jnp.sum(jnp.where(ma, dd, 0.0), axis=1, keepdims=True), jnp.sum(jnp.where(ma, 0.0, dd), axis=1, keepdims=True))
        doh = _pair_masks(do.astype(BF16))

        def block(k0, tkl, r0, carry, masked, heads=(0, 1)):
            vb = v_ref[pl.ds(k0, tkl), :]
            if masked:
                mask = (k0 + _iota((tq - r0, tkl), 1)) <= (qi * tq + r0 + _iota((tq - r0, tkl), 0))
            kaugs = {h: k_refs[h][pl.ds(k0, tkl), :] for h in heads}
            scores = {h: _dot_nt(qaug[h][r0:], kaugs[h]) for h in heads}
            dps = {h: _dot_nt(doh[h][r0:], vb) for h in heads}
            ps, dss = [], []
            rows = [carry[1], carry[2]]
            for h in heads:
                s = jnp.where(mask, scores[h], NEG) if masked else scores[h]
                p = jnp.exp(s - lse[h][r0:])
                dsf = p * (dps[h] - dsum[h][r0:])
                dct_ref[0, h:h + 1, pl.ds(k0, tkl)] -= jnp.sum(dsf, axis=0, keepdims=True)
                rows[h] = _put_rows(carry[1 + h], carry[1 + h][r0:] + jnp.sum(dsf, axis=1, keepdims=True), r0)
                ps.append(p.astype(BF16))
                dss.append(dsf.astype(BF16))
            dv_ref[pl.ds(k0, tkl), :] += _dot_tn(jnp.concatenate(ps, axis=0), jnp.concatenate([doh[h][r0:] for h in heads], axis=0))
            dk_ref[pl.ds(k0, tkl), :] += _dot_tn(jnp.concatenate(dss, axis=0), jnp.concatenate([qh[h][r0:] for h in heads], axis=0))
            kh = jnp.concatenate([_pair_masks(kaugs[h])[h] for h in heads], axis=0)
            dq = _put_rows(carry[0], carry[0][r0:] + _dot(jnp.concatenate(dss, axis=1), kh), r0)
            return (dq, rows[0], rows[1])

        zcol = jnp.zeros((tq, 1), F32)
        carry = (jnp.zeros((tq, LANES), F32), zcol, zcol)
        for off, size in _diag_tiles(tq):
            carry = block(pl.multiple_of(qi * tq + off, size), size, off, carry, True)
        floors = (jnp.min(lse[0]), jnp.min(lse[1]))
        dq, rowa, rowb = _fox_walk_left((qi * tq) // tk, tk, block, carry, k_refs, jnp.max(qkb_ref[...]), lambda c: floors)
        dq_ref[...] = dq * QK_SCALE
        dcr_ref[0] = jnp.where(ma, rowa, rowb)

    qblk = pl.BlockSpec((tq, LANES), lambda p, i: (i, p))
    kvblk = pl.BlockSpec((S, LANES), lambda p, i: (0, p))
    f32out = jax.ShapeDtypeStruct((S, FOX_W), F32)
    ctblk = pl.BlockSpec((1, FF_STRIDE, S), lambda p, i: (p, 0, 0))
    return pl.pallas_call(
        body, name="fox_bwd", grid=(npair, S // tq),
        in_specs=[qblk, kvblk, kvblk, kvblk, qblk, qblk, qblk,
                  pl.BlockSpec((tq, LANES), lambda p, i: (i, C_FG // LANES + p)),
                  pl.BlockSpec((1, LANES), lambda p, i: (0, 0))],
        out_specs=[qblk, kvblk, kvblk, qblk, ctblk, pl.BlockSpec((1, tq, LANES), lambda p, i: (p, i, 0))],
        out_shape=[f32out, f32out, f32out, f32out, jax.ShapeDtypeStruct((npair, FF_STRIDE, S), F32),
                   jax.ShapeDtypeStruct((npair, S, LANES), F32)],
        compiler_params=_cparams(dimension_semantics=("arbitrary", "arbitrary")),
    )(qn, ka, kb, v, o, lse, dmix, projm, qkb)


def _sb_bwd(sq, sk, sv, o, dmix, projm, *, tq, tk):
    S = sq.shape[0]
    npair = SB_HEADS // 2
    mix0 = (FOX_W + POOL_W) // LANES

    def body(q_ref, k_ref, v_ref, o_ref, dm_ref, sg_ref, dq_ref, dk_ref, dv_ref, dsg_ref):
        qi = pl.program_id(1)

        @pl.when(qi == 0)
        def _():
            dk_ref[...] = jnp.zeros_like(dk_ref)
            dv_ref[...] = jnp.zeros_like(dv_ref)

        lane = _iota((1, LANES), 1)
        ma = lane < HEAD_DIM
        qh = _pair_masks(q_ref[...])
        sg = sg_ref[...]
        silu, dsilu = _silu_pair(sg)
        dm = dm_ref[...]
        ov = o_ref[...]
        do = dm * silu
        dsg_ref[...] = dm * ov * dsilu
        dob = do.astype(BF16)
        dd = dob.astype(F32) * ov
        dsum = (jnp.sum(jnp.where(ma, dd, 0.0), axis=1, keepdims=True), jnp.sum(jnp.where(ma, 0.0, dd), axis=1, keepdims=True))
        doh = _pair_masks(dob)
        tmat2 = _suffix_matrix(tk, inclusive=False)
        tmat2_inc = _suffix_matrix(tk, inclusive=True)
        nfull = (qi * tq) // tk

        def block(k0, r0, carry, masked):
            nr = tq - r0
            kb = k_ref[pl.ds(k0, tk), :]
            vb = v_ref[pl.ds(k0, tk), :]
            kh = _pair_masks(kb)
            causal = (k0 + _iota((nr, tk), 1)) < (qi * tq + r0 + _iota((nr, tk), 0)) if masked else None
            heads = range(2)
            qs = [q[r0:] for q in qh]
            dos = [d[r0:] for d in doh]
            das = [_dot_nt(dos[h], vb) for h in heads]
            zs, nsps, lbs, a_s = _sb_scores(qs, kb, causal, tmat2, [carry[h][0][r0:] for h in heads])
            abs_ = [a.astype(BF16) for a in a_s]
            us = [abs_[h].astype(F32) * das[h] for h in heads]
            uins = [_suffix_sums(u, tmat2_inc) for u in us]
            dzs = []
            for h in heads:
                cum_u = dsum[h][r0:] - (uins[h] + carry[h][1][r0:])
                dz = us[h] * jnp.exp(nsps[h]) - jnp.exp(zs[h] + nsps[h]) * cum_u
                if masked:
                    dz = jnp.where(causal, dz, 0.0)
                dzs.append(dz.astype(BF16))
            dv_ref[pl.ds(k0, tk), :] += _dot_tn(jnp.concatenate(abs_, axis=0), jnp.concatenate(dos, axis=0))
            dk_ref[pl.ds(k0, tk), :] += _dot_tn(jnp.concatenate(dzs, axis=0), jnp.concatenate(qs, axis=0))
            dq = _put_rows(carry[2], carry[2][r0:] + _dot(jnp.concatenate(dzs, axis=1), jnp.concatenate(kh, axis=0)), r0)
            new = [(_put_rows(carry[h][0], carry[h][0][r0:] + jnp.sum(lbs[h], axis=1, keepdims=True), r0),
                    _put_rows(carry[h][1], carry[h][1][r0:] + jnp.sum(us[h], axis=1, keepdims=True), r0)) for h in heads]
            return (new[0], new[1], dq)

        zcol = jnp.zeros((tq, 1), F32)
        carry = ((zcol, zcol), (zcol, zcol), jnp.zeros((tq, LANES), F32))
        for off, size in reversed(_diag_tiles(tq)):
            assert size == tk
            carry = block(pl.multiple_of(qi * tq + off, tk), off, carry, True)
        dq = _sb_walk_left(nfull, tk, block, carry, lambda c: (c[0][0], c[1][0]))[2]
        dq_ref[...] = dq * QK_SCALE

    qblk = pl.BlockSpec((tq, LANES), lambda p, i: (i, p))
    kvblk = pl.BlockSpec((S, LANES), lambda p, i: (0, p))
    f32out = jax.ShapeDtypeStruct((S, SB_W), F32)
    return pl.pallas_call(
        body, name="sb_bwd", grid=(npair, S // tq),
        in_specs=[qblk, kvblk, kvblk, qblk,
                  pl.BlockSpec((tq, LANES), lambda p, i: (i, mix0 + p)),
                  pl.BlockSpec((tq, LANES), lambda p, i: (i, C_SG // LANES + p))],
        out_specs=[qblk, kvblk, kvblk, qblk],
        out_shape=[f32out, f32out, f32out, f32out],
        compiler_params=_cparams(dimension_semantics=("arbitrary", "arbitrary")),
    )(sq, sk, sv, o, dmix, projm)


def _prep_bwd(projm, ffo, dqn, dkn, dct, dcr, dv, dfg, dsq, dsk, dsv, dsg, dmix, pooled, yp, qg, kg, bfp, wpd, ps, *, ts):
    S = projm.shape[0]
    nb = S // ts
    hb = ts // POOL_HALO
    npair = FOX_HEADS // 2
    last_halo = S // POOL_HALO - 1

    def body(fq_ref, fk_ref, pp_ref, pph_ref, ff_ref,
             dqn_ref, dkn_ref, dct_ref, dcr_ref, dv_ref, dfg_ref, dsq_ref, dsk_ref, dsv_ref, dsg_ref,
             dmp_ref, dmh_ref, pooled_ref, yp_ref, qg_ref, kg_ref, bf_ref, wpd_ref, ps_ref,
             dp_ref, dqg_ref, dkg_ref, dbf_ref, dwp_ref, dps_ref,
             carry_ref, dl_ref, buf_ref, dct_s):
        i = pl.program_id(0)
        blk = nb - 1 - i

        @pl.when(i == 0)
        def _():
            carry_ref[...] = jnp.zeros_like(carry_ref)
            dqg_ref[...] = jnp.zeros_like(dqg_ref)
            dkg_ref[...] = jnp.zeros_like(dkg_ref)
            dbf_ref[...] = jnp.zeros_like(dbf_ref)
            dwp_ref[...] = jnp.zeros_like(dwp_ref)
            dps_ref[...] = jnp.zeros_like(dps_ref)

        bd = _head_blockdiag()
        for raw_ref, g_ref, dn, dg_ref, col in ((fq_ref, qg_ref, dqn_ref[...], dqg_ref, C_FQ), (fk_ref, kg_ref, dkn_ref[...], dkg_ref, C_FK)):
            q = raw_ref[...]
            rstd = lax.rsqrt(_group_sum(q * q, bd) * (1.0 / HEAD_DIM) + EPS)
            xhat = q * rstd
            dg_ref[...] += jnp.sum(dn * xhat, axis=0, keepdims=True)
            dyg = dn * g_ref[...]
            mean = _group_sum(dyg * xhat, bd) * (1.0 / HEAD_DIM)
            dp_ref[:, col:col + FOX_W] = (rstd * (dyg - xhat * mean)).astype(BF16)
        dp_ref[:, C_FV:C_FV + FOX_W] = dv_ref[...].astype(BF16)
        dp_ref[:, C_FG:C_FG + FOX_W] = dfg_ref[...].astype(BF16)
        dp_ref[:, C_SQ:C_SQ + SB_W] = dsq_ref[...].astype(BF16)
        dp_ref[:, C_SK:C_SK + SB_W] = dsk_ref[...].astype(BF16)
        dp_ref[:, C_SV:C_SV + SB_W] = dsv_ref[...].astype(BF16)
        dp_ref[:, C_SG:C_SG + SB_W] = dsg_ref[...].astype(BF16)

        dct_s[...] = jnp.zeros_like(dct_s)
        for p in range(npair):
            dct_s[FF_STRIDE * p:FF_STRIDE * (p + 1), :] = dct_ref[p]
        dc = dct_s[...].T
        lane = _iota((1, LANES), 1)
        for p in range(npair):
            dcr = dcr_ref[p]
            dc = dc + jnp.where(lane == FF_STRIDE * p, _lane_pick(dcr, lane, 0), 0.0)
            dc = dc + jnp.where(lane == FF_STRIDE * p + 1, _lane_pick(dcr, lane, HEAD_DIM), 0.0)
        triu = _ones_where(_iota((ts, ts), 1) >= _iota((ts, ts), 0))
        dlf = _dot_exact_lhs(triu, dc) + carry_ref[...]
        dl_ref[...] = dlf
        carry_ref[...] = dl_ref[0:1, :]
        z = ff_ref[...] + bf_ref[...]
        dff = dlf * (1.0 / (1.0 + jnp.exp(z)))
        dbf_ref[...] += jnp.sum(dff, axis=0, keepdims=True)
        dp_ref[:, PM:PW] = dff.astype(BF16)

        psv = ps_ref[...]
        wpdv = wpd_ref[...]
        lane_group = _iota((1, POOL_W), 1) >> 6
        wlen = _pool_group_select(lane_group, [float(w) for w in POOL_WINDOWS])
        pg = pp_ref[:, POOL_W:2 * POOL_W]
        silu, dsilu = _silu_pair(pg)
        dmp = dmp_ref[...]
        ypv = yp_ref[...]
        dp_ref[:, C_PG:C_PG + POOL_W] = (dmp * (ypv * psv) * dsilu).astype(BF16)
        dps_ref[...] += jnp.sum(dmp * silu * ypv, axis=0, keepdims=True)
        dyp = (dmp * psv * silu).astype(BF16)
        dwp_ref[...] += _dot_tn(pooled_ref[...], dyp)
        dpooled = _dot_nt(dyp, wpdv)
        pgh = pph_ref[:, POOL_W:2 * POOL_W]
        dyph = (dmh_ref[...] * psv * (pgh * _sigmoid(pgh))).astype(BF16)
        dpooled_h = jnp.where(blk < nb - 1, _dot_nt(dyph, wpdv), 0.0)
        tpos = (blk * ts + _iota((ts, 1), 0) + 1).astype(F32)
        ev = dpooled / jnp.minimum(tpos, wlen)
        buf_ref[0:ts, :] = ev
        buf_ref[ts:ts + POOL_HALO, :] = dpooled_h / wlen
        acc = ev
        snaps = []
        for d in range(1, POOL_HALO):
            acc = acc + buf_ref[pl.ds(d, ts), :]
            if d + 1 in POOL_WINDOWS:
                snaps.append(acc)
        dp_ref[:, C_PX:C_PX + POOL_W] = (_pool_group_select(lane_group, snaps) - dpooled).astype(BF16)

    rblk = lambda w, c: pl.BlockSpec((ts, w), lambda i: (nb - 1 - i, c))
    full = lambda a: pl.BlockSpec(a.shape, lambda i: (0,) * a.ndim)
    halo = lambda w, c: pl.BlockSpec((POOL_HALO, w), lambda i: (jnp.minimum((nb - i) * hb, last_halo), c))
    acc_spec = lambda r, w: pl.BlockSpec((r, w), lambda i: (0, 0))
    return pl.pallas_call(
        body, name="prep_bwd", grid=(nb,),
        in_specs=[rblk(FOX_W, C_FQ // FOX_W), rblk(FOX_W, C_FK // FOX_W), rblk(2 * POOL_W, C_PX // (2 * POOL_W)),
                  halo(2 * POOL_W, C_PX // (2 * POOL_W)), rblk(LANES, 0),
                  rblk(FOX_W, 0), rblk(FOX_W, 0), pl.BlockSpec((npair, FF_STRIDE, ts), lambda i: (0, 0, nb - 1 - i)),
                  pl.BlockSpec((npair, ts, LANES), lambda i: (0, nb - 1 - i, 0)), rblk(FOX_W, 0), rblk(FOX_W, 0),
                  rblk(SB_W, 0), rblk(SB_W, 0), rblk(SB_W, 0), rblk(SB_W, 0),
                  rblk(POOL_W, FOX_W // POOL_W), halo(POOL_W, FOX_W // POOL_W), rblk(POOL_W, 0), rblk(POOL_W, 0),
                  full(qg), full(kg), full(bfp), full(wpd), full(ps)],
        out_specs=[rblk(PW, 0), acc_spec(1, FOX_W), acc_spec(1, FOX_W), acc_spec(1, LANES), acc_spec(POOL_W, POOL_W), acc_spec(1, POOL_W)],
        out_shape=[jax.ShapeDtypeStruct((S, PW), BF16), jax.ShapeDtypeStruct((1, FOX_W), F32), jax.ShapeDtypeStruct((1, FOX_W), F32),
                   jax.ShapeDtypeStruct((1, LANES), F32), jax.ShapeDtypeStruct((POOL_W, POOL_W), F32), jax.ShapeDtypeStruct((1, POOL_W), F32)],
        scratch_shapes=[pltpu.VMEM((1, LANES), F32), pltpu.VMEM((ts, LANES), F32), pltpu.VMEM((ts + POOL_HALO, POOL_W), F32),
                        pltpu.VMEM((LANES, ts), F32)],
        compiler_params=_cparams(dimension_semantics=("arbitrary",)),
    )(projm, projm, projm, projm, ffo, dqn, dkn, dct, dcr, dv, dfg, dsq, dsk, dsv, dsg, dmix, dmix, pooled, yp, qg, kg, bfp, wpd, ps)


def _stack_call(body, name, grid, in_specs, operands, slot_specs, slot_shapes, stacks, plain_specs=(), plain_shapes=(), **kw):
    out_specs = list(plain_specs) + list(slot_specs)
    out_shape = list(plain_shapes) + [jax.ShapeDtypeStruct((DEPTH,) + s, F32) for s in slot_shapes]
    if stacks is None:
        return pl.pallas_call(body, name=name, grid=grid, in_specs=in_specs, out_specs=out_specs, out_shape=out_shape, **kw)(*operands)
    n = len(operands)

    def aliased_body(*refs):
        body(*refs[:n], *refs[n + len(stacks):])

    return pl.pallas_call(
        aliased_body, name=name, grid=grid, in_specs=list(in_specs) + [pl.BlockSpec(memory_space=pl.ANY)] * len(stacks),
        out_specs=out_specs, out_shape=out_shape,
        input_output_aliases={n + k: len(plain_specs) + k for k in range(len(stacks))}, **kw)(*operands, *stacks)


def _inproj_dw(h, dproj, layer, stacks, *, ts, tn):
    S, D = h.shape
    nj = PM // tn

    def body(h_ref, dp_ref, dpf_ref, dw_ref, dwf_ref):
        s = pl.program_id(1)

        @pl.when(s == 0)
        def _():
            dw_ref[...] = jnp.zeros_like(dw_ref)

        @pl.when((s == 0) & (pl.program_id(0) == 0))
        def _():
            dwf_ref[...] = jnp.zeros_like(dwf_ref)

        hv = h_ref[...]
        dw_ref[...] += _dot_tn(dp_ref[...], hv)

        @pl.when(pl.program_id(0) == 0)
        def _():
            dwf_ref[...] += _dot_tn(dpf_ref[...], hv)

    return _stack_call(
        body, "inproj_dw", (nj, S // ts),
        [pl.BlockSpec((ts, D), lambda j, s: (s, 0)),
         pl.BlockSpec((ts, tn), lambda j, s: (s, j)),
         pl.BlockSpec((ts, LANES), lambda j, s: (s, PM // LANES))],
        (h, dproj, dproj),
        [pl.BlockSpec((None, tn, D), lambda j, s: (layer, j, 0)), pl.BlockSpec((None, LANES, D), lambda j, s: (layer, 0, 0))],
        [(PM, D), (LANES, D)], stacks,
        compiler_params=_cparams(dimension_semantics=("arbitrary", "arbitrary")))


def _inproj_dx(dproj, wt_all, layer, x, g, dy, *, tm):
    S, D = x.shape

    def body(dp_ref, w_ref, x_ref, g_ref, dy_ref, dx_ref, dg_ref):
        @pl.when(pl.program_id(0) == 0)
        def _():
            dg_ref[...] = jnp.zeros_like(dg_ref)

        dh = _dot(dp_ref[...], w_ref[...])
        xf = x_ref[...]
        rstd = lax.rsqrt(jnp.mean(xf * xf, axis=-1, keepdims=True) + EPS)
        xhat = xf * rstd
        dg_ref[...] += jnp.sum(dh * xhat, axis=0, keepdims=True)
        dyg = dh * g_ref[...]
        mean = jnp.mean(dyg * xhat, axis=-1, keepdims=True)
        dx_ref[...] = rstd * (dyg - xhat * mean) + dy_ref[...]

    row = lambda w: pl.BlockSpec((tm, w), lambda i: (i, 0))
    return pl.pallas_call(
        body, name="inproj_dx", grid=(S // tm,),
        in_specs=[row(PW), pl.BlockSpec((None, PW, D), lambda i: (layer, 0, 0)), row(D), pl.BlockSpec((1, D), lambda i: (0, 0)), row(D)],
        out_specs=[row(D), pl.BlockSpec((1, D), lambda i: (0, 0))],
        out_shape=[jax.ShapeDtypeStruct((S, D), F32), jax.ShapeDtypeStruct((1, D), F32)],
        compiler_params=_cparams(dimension_semantics=("arbitrary",)),
    )(dproj, wt_all, x, g, dy)


def _adam_update(w, g, m, v):
    nm = ADAM_B1 * m + (1.0 - ADAM_B1) * g
    nv = ADAM_B2 * v + (1.0 - ADAM_B2) * (g * g)
    m_hat = nm / (1.0 - ADAM_B1 ** ADAM_STEP)
    v_hat = nv / (1.0 - ADAM_B2 ** ADAM_STEP)
    return -ADAM_LR * (m_hat / (jnp.sqrt(v_hat) + ADAM_EPS) + ADAM_WD * w), nm, nv


def _adamw(w, g, m, v):
    L, R, C = w.shape
    tr = R if R <= 512 else 256

    def body(w_ref, g_ref, m_ref, v_ref, d_ref, nm_ref, nv_ref):
        d_ref[...], nm_ref[...], nv_ref[...] = _adam_update(w_ref[...], g_ref[...], m_ref[...], v_ref[...])

    spec = pl.BlockSpec((1, tr, C), lambda l, i: (l, i, 0))
    shp = jax.ShapeDtypeStruct((L, R, C), F32)
    return pl.pallas_call(
        body, name="adamw", grid=(L, R // tr), in_specs=[spec] * 4, out_specs=[spec] * 3, out_shape=[shp] * 3,
        compiler_params=_cparams(dimension_semantics=("arbitrary", "arbitrary")),
    )(w, g, m, v)


def _adamw_nd(w, g, m, v):
    shape = w.shape
    view = (1,) + shape if w.ndim == 2 else (shape[0], -1, shape[-1])
    outs = _adamw(w.reshape(view), g.reshape(view), m.reshape(view), v.reshape(view))
    return tuple(o.reshape(shape) for o in outs)


FLIP_C = (0, 0, 1)
FLIP_X = (1, 0, 0)
FLIP_Y = (0, 1, 0)
FLIP_XY = (1, 1, 0)
MESH = pl.DeviceIdType.MESH


def _peer(flip):
    me = (lax.axis_index("x"), lax.axis_index("y"), lax.axis_index("c"))
    return tuple(1 - a if f else a for a, f in zip(me, flip))


def _exchange(name, arrays, flips):
    n = len(arrays)

    def body(*refs):
        srcs, dsts = refs[:n], refs[n:2 * n]
        send_sems, recv_sems = refs[2 * n:]
        copies = [pltpu.make_async_remote_copy(src_ref=srcs[k], dst_ref=dsts[k], send_sem=send_sems.at[k], recv_sem=recv_sems.at[k],
                                               device_id=_peer(flips[k]), device_id_type=MESH) for k in range(n)]
        for cp in copies:
            cp.start()
        for cp in copies:
            cp.wait()

    anyspec = pl.BlockSpec(memory_space=pl.ANY)
    return pl.pallas_call(
        body, name=name, in_specs=[anyspec] * n, out_specs=[anyspec] * n,
        out_shape=[jax.ShapeDtypeStruct(a.shape, a.dtype) for a in arrays],
        scratch_shapes=[pltpu.SemaphoreType.DMA((n,)), pltpu.SemaphoreType.DMA((n,))],
    )(*arrays)


def _exchange_add(name, x, flip):
    def body(x_ref, o_ref, buf_ref, send_sem, recv_sem):
        cp = pltpu.make_async_remote_copy(src_ref=x_ref, dst_ref=buf_ref, send_sem=send_sem, recv_sem=recv_sem,
                                          device_id=_peer(flip), device_id_type=MESH)
        cp.start()
        cp.wait()
        o_ref[...] = x_ref[...] + buf_ref[...]

    vspec = pl.BlockSpec(memory_space=pltpu.VMEM)
    return pl.pallas_call(
        body, name=name, in_specs=[vspec], out_specs=vspec, out_shape=jax.ShapeDtypeStruct(x.shape, x.dtype),
        scratch_shapes=[pltpu.VMEM(x.shape, x.dtype), pltpu.SemaphoreType.DMA, pltpu.SemaphoreType.DMA],
    )(x)


def _chip_index():
    return 2 * lax.axis_index("x") + lax.axis_index("y")


def _gather_weights(w_in_t, w_out):
    wi = w_in_t.astype(BF16)
    wo = jnp.swapaxes(w_out, 0, 1).astype(BF16)
    halves = (wi.shape[0] // 2, wo.shape[0] // 2)
    ARR = 2
    TO_X, TO_Y, ON_Y, ON_X, SIB_X, SIB_Y, SIB_D0, SIB_D1, OWN = [ARR * k for k in range(9)]
    n_sems = ARR * 9

    def body(wi_ref, wo_ref, gi_ref, go_ref, send_sems, recv_sems):
        c = lax.axis_index("c")
        j = _chip_index()
        srcs = (wi_ref, wo_ref)
        dsts = (gi_ref, go_ref)
        def cuts(core):
            return [(pl.ds(h * core, h), pl.ds(h * core, h // 2), pl.ds(h * core + h // 2, h - h // 2)) for h in halves]
        mine, theirs = cuts(c), cuts(1 - c)
        HALF, Q0, Q1 = 0, 1, 2

        def copy(idx, src, dst, flip):
            return pltpu.make_async_remote_copy(src_ref=src, dst_ref=dst, send_sem=send_sems.at[idx], recv_sem=recv_sems.at[idx],
                                                device_id=_peer(flip), device_id_type=MESH)

        def slot(a, shard, cut):
            return dsts[a].at[shard, cut]

        jx, jy, jd = j ^ 2, j ^ 1, j ^ 3
        sends = []

        def start(cp):
            cp.start()
            sends.append(cp)

        for a in range(ARR):
            start(copy(TO_X + a, srcs[a].at[mine[a][HALF]], slot(a, j, mine[a][HALF]), FLIP_X))
            start(copy(TO_Y + a, srcs[a].at[mine[a][HALF]], slot(a, j, mine[a][HALF]), FLIP_Y))
        own = [copy(OWN + a, srcs[a], dsts[a].at[j], FLIP_C) for a in range(ARR)]
        for cp in own:
            cp.start()
        for a in range(ARR):
            copy(TO_X + a, slot(a, jx, mine[a][HALF]), slot(a, jx, mine[a][HALF]), FLIP_X).wait_recv()
            start(copy(ON_Y + a, slot(a, jx, mine[a][Q0]), slot(a, jx, mine[a][Q0]), FLIP_Y))
            start(copy(SIB_X + a, slot(a, jx, mine[a][HALF]), slot(a, jx, mine[a][HALF]), FLIP_C))
        for a in range(ARR):
            copy(TO_Y + a, slot(a, jy, mine[a][HALF]), slot(a, jy, mine[a][HALF]), FLIP_Y).wait_recv()
            start(copy(ON_X + a, slot(a, jy, mine[a][Q1]), slot(a, jy, mine[a][Q1]), FLIP_X))
            start(copy(SIB_Y + a, slot(a, jy, mine[a][HALF]), slot(a, jy, mine[a][HALF]), FLIP_C))
        for a in range(ARR):
            copy(ON_Y + a, slot(a, jd, mine[a][Q0]), slot(a, jd, mine[a][Q0]), FLIP_Y).wait_recv()
            start(copy(SIB_D0 + a, slot(a, jd, mine[a][Q0]), slot(a, jd, mine[a][Q0]), FLIP_C))
        for a in range(ARR):
            copy(ON_X + a, slot(a, jd, mine[a][Q1]), slot(a, jd, mine[a][Q1]), FLIP_X).wait_recv()
            start(copy(SIB_D1 + a, slot(a, jd, mine[a][Q1]), slot(a, jd, mine[a][Q1]), FLIP_C))
        for a in range(ARR):
            for idx, shard, cut in ((SIB_X, jx, HALF), (SIB_Y, jy, HALF), (SIB_D0, jd, Q0), (SIB_D1, jd, Q1)):
                copy(idx + a, slot(a, shard, theirs[a][cut]), slot(a, shard, theirs[a][cut]), FLIP_C).wait_recv()
        for cp in own:
            cp.wait()
        for cp in sends:
            cp.wait_send()

    anyspec = pl.BlockSpec(memory_space=pl.ANY)
    gi, go = pl.pallas_call(
        body, name="gather_weights", in_specs=[anyspec] * 2, out_specs=[anyspec] * 2,
        out_shape=[jax.ShapeDtypeStruct((4,) + wi.shape, BF16), jax.ShapeDtypeStruct((4,) + wo.shape, BF16)],
        scratch_shapes=[pltpu.SemaphoreType.DMA((n_sems,)), pltpu.SemaphoreType.DMA((n_sems,))],
    )(wi, wo)
    w_in_t_full = gi.reshape((4 * wi.shape[0],) + wi.shape[1:])
    w_out_full = jnp.swapaxes(go.reshape((4 * wo.shape[0],) + wo.shape[1:]), 0, 1)
    return w_in_t_full, w_out_full


def _to_aligned(w_t):
    _, L, D = w_t.shape
    npair = FOX_HEADS // 2
    ff = w_t[ORIG_FF:ORIG_REST].reshape(npair, 2, L, D)
    ff = jnp.pad(ff, ((0, 0), (0, FF_STRIDE - 2), (0, 0), (0, 0))).reshape(npair * FF_STRIDE, L, D)
    ff = jnp.pad(ff, ((0, LANES - npair * FF_STRIDE), (0, 0), (0, 0)))
    return jnp.swapaxes(jnp.concatenate([w_t[:ORIG_FOX], w_t[ORIG_REST:], ff], axis=0), 0, 1)


def _from_aligned(dw_t):
    n, _, D = dw_t.shape
    npair = FOX_HEADS // 2
    ff = dw_t[:, PM:PM + npair * FF_STRIDE].reshape(n, npair, FF_STRIDE, D)[:, :, :2].reshape(n, FOX_HEADS, D)
    return jnp.swapaxes(jnp.concatenate([dw_t[:, :ORIG_FOX], ff, dw_t[:, ORIG_FOX:PM]], axis=1), 0, 1)


def _half_layers(name, stack, got):
    L, R, C = stack.shape
    half = L // 2
    tr = min(256, R)
    c = lax.axis_index("c")
    which = ((1 - c) if got is None else c).astype(jnp.int32).reshape(1)

    def body(c_ref, x_ref, *refs):
        if got is None:
            refs[0][...] = x_ref[...].astype(BF16)
        else:
            acc = x_ref[...] + refs[0][...].astype(F32)
            refs[1][...] = acc
            refs[2][...] = acc.astype(BF16)

    plain = pl.BlockSpec((1, tr, C), lambda l, i, c_ref: (l, i, 0))
    picked = pl.BlockSpec((1, tr, C), lambda l, i, c_ref: (c_ref[0] * half + l, i, 0))
    shp = lambda dt: jax.ShapeDtypeStruct((half, R, C), dt)
    grid_spec = pltpu.PrefetchScalarGridSpec(
        num_scalar_prefetch=1, grid=(half, R // tr),
        in_specs=[picked] + ([] if got is None else [plain]), out_specs=[plain] if got is None else [plain, plain])
    return pl.pallas_call(
        body, name=name, grid_spec=grid_spec, out_shape=[shp(BF16)] if got is None else [shp(F32), shp(BF16)],
        compiler_params=_cparams(dimension_semantics=("arbitrary", "arbitrary")),
    )(which, stack, *([] if got is None else [got]))


def _reduce_scatter(stack_m, stack_f, stack_o, shard_cols, shard_rows):
    j = _chip_index()
    half = DEPTH // 2
    stacks = (stack_m, stack_f, stack_o)
    give = [_half_layers("rs_give", s, None)[0] for s in stacks]
    got = _exchange("rs_d2d", give, (FLIP_C,) * len(stacks))
    (m32, mbf), (f32_, fbf), (o32, obf) = [_half_layers("rs_add_chip", s, g) for s, g in zip(stacks, got)]
    d_model = stack_m.shape[2]

    def in_shards(m, f):
        return _from_aligned(jnp.concatenate([m, f], axis=1)).reshape(4, shard_cols, half, d_model)

    def out_shards(o):
        return jnp.moveaxis(o.reshape(half, 4, shard_rows, o.shape[-1]), 1, 0)

    chip = [(in_shards(m32, f32_), in_shards(mbf, fbf), 0), (out_shards(o32), out_shards(obf), 1)]
    shard = lambda a, idx: lax.dynamic_index_in_dim(a, idx, axis=0, keepdims=False)
    via = []
    for _, bf, axis in chip:
        diag = shard(bf, j ^ 3)
        cut = diag.shape[axis] // 2
        via += [lax.slice_in_dim(diag, 0, cut, axis=axis), lax.slice_in_dim(diag, cut, 2 * cut, axis=axis)]
    handed = _exchange("rs_via", via, (FLIP_X, FLIP_Y) * len(chip))
    sends = []
    for a, (f32_sum, _, axis) in enumerate(chip):
        sends.append(_add_half_along("rs_add_via", shard(f32_sum, j ^ 2), handed[2 * a + 1], axis, 1))
        sends.append(_add_half_along("rs_add_via", shard(f32_sum, j ^ 1), handed[2 * a], axis, 0))
    got = _exchange("rs_ici", sends, (FLIP_X, FLIP_Y) * len(chip))
    own_in, own_out = [shard(f32_sum, j) for f32_sum, _, _ in chip]
    mine_in = _add_rows("rs_add_in", own_in, list(got[0:2]))
    mine_out = _add_into_half("rs_add_out", own_out, list(got[2:4]))
    sib_in, g_out = _share_halves(mine_in, mine_out)
    return (mine_in, sib_in), g_out


def _add_half_along(name, base, extra, axis, which):
    lanes = min(ROW_LANE_CHUNK, base.shape[2])
    assert base.shape[axis] == 2 * extra.shape[axis]
    blk = tuple(base.shape[d] // 2 if d == axis else base.shape[d] for d in range(2)) + (lanes,)

    def body(b_ref, e_ref, o_ref):
        x = b_ref[...]
        o_ref[...] = jnp.where(pl.program_id(0) == which, x + e_ref[...].astype(F32), x).astype(BF16)

    at = lambda i, k: (i, 0, k) if axis == 0 else (0, i, k)
    return pl.pallas_call(
        body, name=name, grid=(2, base.shape[2] // lanes),
        in_specs=[pl.BlockSpec(blk, at), pl.BlockSpec(blk, lambda i, k: (0, 0, k))], out_specs=pl.BlockSpec(blk, at),
        out_shape=jax.ShapeDtypeStruct(base.shape, BF16),
        compiler_params=_cparams(dimension_semantics=("arbitrary", "arbitrary")),
    )(base, extra)


def _add_rows(name, first, others):
    n = len(others)

    def body(*refs):
        acc = refs[0][...]
        for r in refs[1:1 + n]:
            acc = acc + r[...].astype(F32)
        refs[1 + n][...] = acc

    grid, spec = _row_lane_blocks(first.shape)
    return pl.pallas_call(
        body, name=name, grid=grid, in_specs=[spec(first.shape[1])] * (1 + n), out_specs=spec(first.shape[1]),
        out_shape=jax.ShapeDtypeStruct(first.shape, F32),
        compiler_params=_cparams(dimension_semantics=("arbitrary", "arbitrary")),
    )(first, *others)


ROW_LANE_CHUNK = 256


def _row_lane_blocks(shape):
    rows, _, C = shape
    tr = rows // 2 if rows % 2 == 0 and rows > 64 else rows
    lanes = min(ROW_LANE_CHUNK, C)
    return (rows // tr, C // lanes), lambda n_mid: pl.BlockSpec((tr, n_mid, lanes), lambda i, k, *_: (i, 0, k))


def _add_into_half(name, first, others):
    half, rows, C = first.shape
    tr = min(256, rows)
    n = len(others)

    def body(c_ref, *refs):
        acc = refs[0][...]
        for r in refs[1:1 + n]:
            acc = acc + r[...].astype(F32)
        refs[1 + n][...] = acc

    grid_spec = pltpu.PrefetchScalarGridSpec(
        num_scalar_prefetch=1, grid=(half, rows // tr),
        in_specs=[pl.BlockSpec((1, tr, C), lambda l, i, c_ref: (l, i, 0))] * (1 + n),
        out_specs=pl.BlockSpec((1, tr, C), lambda l, i, c_ref: (c_ref[0] * half + l, i, 0)))
    return pl.pallas_call(
        body, name=name, grid_spec=grid_spec, out_shape=jax.ShapeDtypeStruct((2 * half, rows, C), F32),
        compiler_params=_cparams(dimension_semantics=("arbitrary", "arbitrary")),
    )(lax.axis_index("c").astype(jnp.int32).reshape(1), first, *others)


def _share_halves(mine, buf):
    half = DEPTH // 2

    def body(mine_ref, buf_in, sib_ref, buf_ref, send_sems, recv_sems):
        lay = pl.ds(half * lax.axis_index("c"), half)
        copies = [pltpu.make_async_remote_copy(src_ref=src, dst_ref=dst, send_sem=send_sems.at[k], recv_sem=recv_sems.at[k],
                                               device_id=_peer(FLIP_C), device_id_type=MESH)
                  for k, (src, dst) in enumerate(((mine_ref, sib_ref), (buf_ref.at[lay], buf_ref.at[lay])))]
        for cp in copies:
            cp.start()
        for cp in copies:
            cp.wait()

    anyspec = pl.BlockSpec(memory_space=pl.ANY)
    return pl.pallas_call(
        body, name="rs_share", in_specs=[anyspec] * 2, out_specs=[anyspec] * 2,
        out_shape=[jax.ShapeDtypeStruct(mine.shape, mine.dtype), jax.ShapeDtypeStruct(buf.shape, buf.dtype)],
        input_output_aliases={1: 1},
        scratch_shapes=[pltpu.SemaphoreType.DMA((2,)), pltpu.SemaphoreType.DMA((2,))],
    )(mine, buf)


def _adamw_halves(w, g_mine, g_sib, m, v):
    half = g_mine.shape[1]

    def body(c_ref, w_ref, gm_ref, gs_ref, m_ref, v_ref, g_ref, d_ref, nm_ref, nv_ref):
        first = c_ref[0] == 0
        gm, gs = gm_ref[...], gs_ref[...]
        for h, gv in enumerate((jnp.where(first, gm, gs), jnp.where(first, gs, gm))):
            lay = slice(half * h, half * (h + 1))
            g_ref[:, lay, :] = gv
            d_ref[:, lay, :], nm_ref[:, lay, :], nv_ref[:, lay, :] = _adam_update(w_ref[:, lay, :], gv, m_ref[:, lay, :], v_ref[:, lay, :])

    grid, spec = _row_lane_blocks(w.shape)
    full, part = spec(w.shape[1]), spec(half)
    grid_spec = pltpu.PrefetchScalarGridSpec(num_scalar_prefetch=1, grid=grid, in_specs=[full, part, part, full, full], out_specs=[full] * 4)
    return pl.pallas_call(
        body, name="adamw_halves", grid_spec=grid_spec, out_shape=[jax.ShapeDtypeStruct(w.shape, F32)] * 4,
        compiler_params=_cparams(dimension_semantics=("arbitrary", "arbitrary")),
    )(lax.axis_index("c").astype(jnp.int32).reshape(1), w, g_mine, g_sib, m, v)


def _all_reduce_small(x):
    x = _exchange_add("ar_c", x, FLIP_C)
    x = _exchange_add("ar_y", x, FLIP_Y)
    return _exchange_add("ar_x", x, FLIP_X)


def _blocks(S):
    return dict(tm=min(512, S), tm_proj=min(1024, S), ts=min(512, S), tq=min(512, S), tq_big=min(1024, S), tk=min(512, S), tks=min(256, S))


def _pair_pad(vec):
    npair = FOX_HEADS // 2
    v = jnp.pad(vec.reshape(npair, 2), ((0, 0), (0, FF_STRIDE - 2))).reshape(1, npair * FF_STRIDE)
    return jnp.pad(v, ((0, 0), (0, LANES - npair * FF_STRIDE)))


def _pair_unpad(row):
    npair = FOX_HEADS // 2
    return row[0, :npair * FF_STRIDE].reshape(npair, FF_STRIDE)[:, :2].reshape(FOX_HEADS)


def _pool_blockdiag(w_pool):
    g, cg, _ = w_pool.shape
    eye = jnp.eye(g, dtype=w_pool.dtype)
    return jnp.einsum("gh,gcd->gchd", eye, w_pool).reshape(g * cg, g * cg)


QK_BOUND_SLACK = 1.05


def _layer_params(norm_g, b_f, q_norm_g, k_norm_g, w_pool, pool_scale):
    qk_bound = QK_BOUND_SLACK * HEAD_DIM * QK_SCALE * jnp.max(jnp.abs(q_norm_g)) * jnp.max(jnp.abs(k_norm_g))
    return dict(g=norm_g.reshape(1, -1), qg=jnp.tile(q_norm_g, FOX_HEADS).reshape(1, FOX_W), kg=jnp.tile(k_norm_g, FOX_HEADS).reshape(1, FOX_W),
                bfp=_pair_pad(b_f), wpd=_pool_blockdiag(w_pool).astype(BF16), ps=pool_scale.reshape(1, POOL_W),
                qkb=jnp.full((1, LANES), qk_bound, F32))


def _layer_fwd(x, wt_all, w_out, layer, prm, bs):
    projm, ffo, h = _inproj(x, prm["g"], wt_all, layer, tm=bs["tm_proj"], tn=PROJ_TN)
    qn, ka, kb, v, sq, sk, sv, pooled, yp, pm = _prep(projm, ffo, prm["qg"], prm["kg"], prm["bfp"], prm["wpd"], prm["ps"], ts=bs["ts"])
    o, lse, fm = _fox_fwd(qn, ka, kb, v, projm, prm["qkb"], tq=bs["tq"], tk=bs["tk"])
    so, sm = _sb_fwd(sq, sk, sv, projm, tq=bs["tq_big"], tk=bs["tks"])
    y = _outproj(x, fm, pm, sm, w_out, layer, tm=bs["tm"])
    saved = dict(x=x, projm=projm, ffo=ffo, h=h, qn=qn, ka=ka, kb=kb, v=v, sq=sq, sk=sk, sv=sv, pooled=pooled, yp=yp,
                 o=o, lse=lse, so=so, fm=fm, pm=pm, sm=sm)
    return y, saved


def _layer_bwd(dy, wt_all, w_out, prm, sv_, bs, layer, stacks):
    dmix, stack_o = _outproj_bwd(dy, sv_["fm"], sv_["pm"], sv_["sm"], w_out, layer, None if stacks is None else stacks[2:], tm=bs["tm"])
    dqn, dkn, dv, dfg, dct, dcr = _fox_bwd(sv_["qn"], sv_["ka"], sv_["kb"], sv_["v"], sv_["o"], sv_["lse"], dmix, sv_["projm"],
                                      prm["qkb"], tq=bs["tq_big"], tk=bs["tk"])
    dsq, dsk, dsv, dsg = _sb_bwd(sv_["sq"], sv_["sk"], sv_["sv"], sv_["so"], dmix, sv_["projm"], tq=bs["tq"], tk=bs["tks"])
    dproj, dqg, dkg, dbf, dwp, dps = _prep_bwd(sv_["projm"], sv_["ffo"], dqn, dkn, dct, dcr, dv, dfg, dsq, dsk, dsv, dsg, dmix,
                                               sv_["pooled"], sv_["yp"], prm["qg"], prm["kg"], prm["bfp"], prm["wpd"], prm["ps"], ts=bs["ts"])
    stack_m, stack_f = _inproj_dw(sv_["h"], dproj, layer, None if stacks is None else stacks[:2], ts=bs["tm_proj"], tn=PROJ_TN)
    dx, dg = _inproj_dx(dproj, wt_all, layer, sv_["x"], prm["g"], dy, tm=min(256, bs["tm"]))
    grads = dict(
        norm_g=dg[0],
        b_f=_pair_unpad(dbf), q_norm_g=dqg.reshape(FOX_HEADS, HEAD_DIM).sum(0), k_norm_g=dkg.reshape(FOX_HEADS, HEAD_DIM).sum(0),
        w_pool=jnp.stack([dwp[HEAD_DIM * g:HEAD_DIM * (g + 1), HEAD_DIM * g:HEAD_DIM * (g + 1)] for g in range(4)]),
        pool_scale=dps[0])
    return dx, grads, (stack_m, stack_f, stack_o)


def _local_step(x, target, wt_all, w_out, norm_g, b_f, q_norm_g, k_norm_g, w_pool, pool_scale):
    S, D = x.shape
    bs = _blocks(S)
    prms = [_layer_params(norm_g[l], b_f[l], q_norm_g[l], k_norm_g[l], w_pool[l], pool_scale[l]) for l in range(DEPTH)]
    saved = []
    y = x
    for l in range(DEPTH):
        y, s_ = _layer_fwd(y, wt_all, w_out, l, prms[l], bs)
        saved.append(s_)
    dy, sq = _loss_head(y, target, tm=bs["tm"])
    loss = 0.5 * jnp.sum(sq) / D
    grads = [None] * DEPTH
    stacks = None
    for l in reversed(range(DEPTH)):
        dy, grads[l], stacks = _layer_bwd(dy, wt_all, w_out, prms[l], saved[l], bs, l, stacks)
    stacked = {k: jnp.stack([g[k] for g in grads]) for k in grads[0]}
    return loss, dy, stacked, stacks


SMALL = ("norm_g", "b_f", "q_norm_g", "k_norm_g", "w_pool", "pool_scale")


def _pack_small(gr):
    flat = jnp.concatenate([gr[k].reshape(-1) for k in SMALL])
    pad = (-flat.shape[0]) % (8 * LANES)
    return jnp.pad(flat, (0, pad)).reshape(-1, LANES)


def _unpack_small(packed, like):
    flat = packed.reshape(-1)
    out, off = {}, 0
    for k in SMALL:
        n = like[k].size
        out[k] = flat[off:off + n].reshape(like[k].shape)
        off += n
    return out


def kernel(x, norm_g, w_in, b_f, q_norm_g, k_norm_g, w_pool, pool_scale, w_out, loss_target, m_norm_g, m_w_in, m_b_f, m_q_norm_g, m_k_norm_g, m_w_pool, m_pool_scale, m_w_out, v_norm_g, v_w_in, v_b_f, v_q_norm_g, v_k_norm_g, v_w_pool, v_pool_scale, v_w_out):
    weights = dict(norm_g=norm_g, w_in=w_in, b_f=b_f, q_norm_g=q_norm_g, k_norm_g=k_norm_g, w_pool=w_pool, pool_scale=pool_scale, w_out=w_out)
    mom_m = dict(norm_g=m_norm_g, w_in=m_w_in, b_f=m_b_f, q_norm_g=m_q_norm_g, k_norm_g=m_k_norm_g, w_pool=m_w_pool, pool_scale=m_pool_scale, w_out=m_w_out)
    mom_v = dict(norm_g=v_norm_g, w_in=v_w_in, b_f=v_b_f, q_norm_g=v_q_norm_g, k_norm_g=v_k_norm_g, w_pool=v_w_pool, pool_scale=v_pool_scale, w_out=v_w_out)
    shard_cols = w_in.shape[2]
    shard_rows = w_out.shape[1]

    cols_first = lambda a: jnp.transpose(a, (2, 0, 1))
    w_in_t = cols_first(w_in)
    w_in_t_full, w_out_full = _gather_weights(w_in_t, w_out)
    wt_all = _to_aligned(w_in_t_full)
    loss, dx, gr, stacks = _local_step(x[0], loss_target[0], wt_all, w_out_full, norm_g, b_f, q_norm_g, k_norm_g, w_pool, pool_scale)
    loss = lax.psum(loss, ("x", "y", "c"))

    (g_in_mine, g_in_sib), g_w_out = _reduce_scatter(*stacks, shard_cols, shard_rows)
    small = _unpack_small(_all_reduce_small(_pack_small(gr)), {k: weights[k] for k in SMALL})
    grad_w = dict(small, w_out=g_w_out)

    names = ("norm_g", "w_in", "b_f", "q_norm_g", "k_norm_g", "w_pool", "pool_scale", "w_out")
    upd = {k: _adamw_nd(weights[k], grad_w[k], mom_m[k], mom_v[k]) for k in names if k != "w_in"}
    in_t = _adamw_halves(w_in_t, g_in_mine, g_in_sib, cols_first(mom_m["w_in"]), cols_first(mom_v["w_in"]))
    grad_w["w_in"], *upd["w_in"] = [jnp.transpose(a, (1, 2, 0)) for a in in_t]
    return (loss, dx[None], *[grad_w[k] for k in names], *[upd[k][0] for k in names], *[upd[k][1] for k in names], *[upd[k][2] for k in names])
```

```python
import functools

import jax
import jax.numpy as jnp
from jax import lax
from jax.experimental import pallas as pl
from jax.experimental.pallas import tpu as pltpu

F32 = jnp.float32
BF16 = jnp.bfloat16

DEPTH = 4
HEAD_DIM = 64
FOX_HEADS = 8
SB_HEADS = 4
FOX_W = FOX_HEADS * HEAD_DIM
SB_W = SB_HEADS * HEAD_DIM
POOL_W = 256
POOL_WINDOWS = (2, 4, 8, 16)
POOL_HALO = 16
D_MIX = FOX_W + POOL_W + SB_W
EPS = 1e-6
NEG = -1e30
QK_SCALE = HEAD_DIM ** -0.5

ORIG_FOX = 4 * FOX_W
ORIG_FF = ORIG_FOX
ORIG_REST = ORIG_FF + FOX_HEADS
D_IN = ORIG_REST + 2 * POOL_W + 4 * SB_W

C_FQ, C_FK, C_FV, C_FG = 0, FOX_W, 2 * FOX_W, 3 * FOX_W
C_PX = 4 * FOX_W
C_PG = C_PX + POOL_W
C_SQ = C_PG + POOL_W
C_SK, C_SV, C_SG = C_SQ + SB_W, C_SQ + 2 * SB_W, C_SQ + 3 * SB_W
PM = C_SG + SB_W
LANES = 128
PW = PM + LANES
FF_STRIDE = 8
AUG = 3

ADAM_LR = 0.001
ADAM_B1 = 0.9
ADAM_B2 = 0.999
ADAM_EPS = 1e-08
ADAM_WD = 0.01
ADAM_STEP = 10

VMEM_LIMIT = 48 * 1024 * 1024
PROJ_TN = PM // 2


def _cparams(**kw):
    return pltpu.CompilerParams(vmem_limit_bytes=VMEM_LIMIT, **kw)


def _dot(a, b):
    return jnp.dot(a, b, preferred_element_type=F32)


def _dot_nt(a, b):
    return lax.dot_general(a, b, (((1,), (1,)), ((), ())), preferred_element_type=F32)


def _dot_tn(a, b):
    return lax.dot_general(a, b, (((0,), (0,)), ((), ())), preferred_element_type=F32)


def _split2(x):
    hi = x.astype(BF16)
    lo = (x - hi.astype(F32)).astype(BF16)
    return hi, lo


def _split3(x):
    hi = x.astype(BF16)
    r = x - hi.astype(F32)
    mid = r.astype(BF16)
    lo = (r - mid.astype(F32)).astype(BF16)
    return hi, mid, lo


def _dot_exact_rhs(x, m):
    hi, mid, lo = _split3(x)
    return _dot(hi, m) + _dot(mid, m) + _dot(lo, m)


def _dot_exact_lhs(m, x):
    hi, mid, lo = _split3(x)
    return _dot(m, hi) + _dot(m, mid) + _dot(m, lo)


def _sigmoid(x):
    return 1.0 / (1.0 + jnp.exp(-x))


def _silu_pair(x):
    s = _sigmoid(x)
    return x * s, s * (1.0 + x * (1.0 - s))


def _iota(shape, dim):
    return lax.broadcasted_iota(jnp.int32, shape, dim)


def _ones_where(cond):
    return jnp.where(cond, 1.0, 0.0).astype(BF16)


GROUP_SLAB = 256


def _head_blockdiag():
    rows, cols = _iota((2 * GROUP_SLAB, GROUP_SLAB), 0) & (GROUP_SLAB - 1), _iota((2 * GROUP_SLAB, GROUP_SLAB), 1)
    return _ones_where((rows >> 6) == (cols >> 6))


def _group_sum(x, bd):
    hi, lo = _split2(x)
    slabs = [_dot(jnp.concatenate([hi[:, s:s + GROUP_SLAB], lo[:, s:s + GROUP_SLAB]], axis=1), bd) for s in range(0, x.shape[1], GROUP_SLAB)]
    return jnp.concatenate(slabs, axis=1)


def _lane_pick(x, lane_idx, lane):
    return jnp.sum(jnp.where(lane_idx == lane, x, 0.0), axis=1, keepdims=True)


def _inproj(x, g, wt_all, layer, *, tm, tn):
    S, D = x.shape
    nj = PM // tn

    def body(x_ref, g_ref, w_ref, wff_ref, proj_ref, ff_ref, h_ref):
        @pl.when(pl.program_id(1) == 0)
        def _():
            xf = x_ref[...]
            ms = jnp.mean(xf * xf, axis=-1, keepdims=True)
            h = (xf * lax.rsqrt(ms + EPS) * g_ref[...]).astype(BF16)
            h_ref[...] = h
            ff_ref[...] = _dot_nt(h, wff_ref[...])

        proj_ref[...] = _dot_nt(h_ref[...], w_ref[...])

    return pl.pallas_call(
        body, name="inproj", grid=(S // tm, nj),
        in_specs=[pl.BlockSpec((tm, D), lambda i, j: (i, 0)),
                  pl.BlockSpec((1, D), lambda i, j: (0, 0)),
                  pl.BlockSpec((None, tn, D), lambda i, j: (layer, j, 0)),
                  pl.BlockSpec((None, LANES, D), lambda i, j: (layer, PM // LANES, 0))],
        out_specs=[pl.BlockSpec((tm, tn), lambda i, j: (i, j)),
                   pl.BlockSpec((tm, LANES), lambda i, j: (i, 0)),
                   pl.BlockSpec((tm, D), lambda i, j: (i, 0))],
        out_shape=[jax.ShapeDtypeStruct((S, PM), F32), jax.ShapeDtypeStruct((S, LANES), F32),
                   jax.ShapeDtypeStruct((S, D), BF16)],
        compiler_params=_cparams(dimension_semantics=("arbitrary", "arbitrary")),
    )(x, g, wt_all, wt_all)


def _pool_group_select(lane_group, vals):
    return jnp.where(lane_group == 0, vals[0], jnp.where(lane_group == 1, vals[1], jnp.where(lane_group == 2, vals[2], vals[3])))


def _prep(projm, ffo, qg, kg, bfp, wpd, ps, *, ts):
    S = projm.shape[0]
    nb = S // ts
    hb = ts // POOL_HALO

    def body(fq_ref, fk_ref, fv_ref, pp_ref, halo_ref, ff_ref, sq_ref, sk_ref, sv_ref,
             qg_ref, kg_ref, bf_ref, wpd_ref, ps_ref,
             qn_ref, ka_ref, kb_ref, v_ref, sqo_ref, sko_ref, svo_ref, pooled_ref, yp_ref, pm_ref,
             carry_ref, c_ref, buf_ref):
        i = pl.program_id(0)
        bd = _head_blockdiag()
        normed = []
        for src, g_ref in ((fq_ref, qg_ref), (fk_ref, kg_ref)):
            q = src[...]
            ss = _group_sum(q * q, bd)
            normed.append(q * lax.rsqrt(ss * (1.0 / HEAD_DIM) + EPS) * g_ref[...])
        qn_ref[...] = (normed[0] * QK_SCALE).astype(BF16)
        kn = normed[1]
        v_ref[...] = fv_ref[...].astype(BF16)
        sqo_ref[...] = (sq_ref[...] * QK_SCALE).astype(BF16)
        sko_ref[...] = sk_ref[...].astype(BF16)
        svo_ref[...] = sv_ref[...].astype(BF16)

        @pl.when(i == 0)
        def _():
            carry_ref[...] = jnp.zeros_like(carry_ref)

        z = ff_ref[...] + bf_ref[...]
        lf = jnp.minimum(z, 0.0) - jnp.log(1.0 + jnp.exp(-jnp.abs(z)))
        tri = _ones_where(_iota((ts, ts), 1) <= _iota((ts, ts), 0))
        c = _dot_exact_lhs(tri, lf) + carry_ref[...]
        c_ref[...] = c
        carry_ref[...] = c_ref[ts - 1:ts, :]
        parts = jnp.concatenate(_split3(-c), axis=1)
        row = _iota((AUG * LANES, FOX_W), 0)
        col = _iota((AUG * LANES, FOX_W), 1)
        part, src = row >> 7, row & (LANES - 1)
        pair, off = col >> 7, col & (LANES - 1)
        sel_a = _ones_where((src == FF_STRIDE * pair) & (off == HEAD_DIM + part))
        sel_b = _ones_where((src == FF_STRIDE * pair + 1) & (off == part))
        first_half = (_iota((1, FOX_W), 1) & HEAD_DIM) == 0
        ka_ref[...] = jnp.where(first_half, kn, _dot(parts, sel_a)).astype(BF16)
        kb_ref[...] = jnp.where(first_half, _dot(parts, sel_b), kn).astype(BF16)

        x = pp_ref[:, 0:POOL_W]
        pg = pp_ref[:, POOL_W:2 * POOL_W]
        halo = jnp.where(i > 0, halo_ref[:, 0:POOL_W], 0.0)
        buf_ref[0:POOL_HALO, :] = halo
        buf_ref[POOL_HALO:POOL_HALO + ts, :] = x
        acc = x
        snaps = []
        for d in range(1, POOL_HALO):
            acc = acc + buf_ref[pl.ds(POOL_HALO - d, ts), :]
            if d + 1 in POOL_WINDOWS:
                snaps.append(acc)
        lane_group = _iota((1, POOL_W), 1) >> 6
        wsum = _pool_group_select(lane_group, snaps)
        wlen = _pool_group_select(lane_group, [float(w) for w in POOL_WINDOWS])
        tpos = (i * ts + _iota((ts, 1), 0) + 1).astype(F32)
        pooled = wsum / jnp.minimum(tpos, wlen) - x
        pb = pooled.astype(BF16)
        pooled_ref[...] = pb
        yp = _dot(pb, wpd_ref[...])
        yp_ref[...] = yp
        pm_ref[...] = (yp * ps_ref[...] * (pg * _sigmoid(pg))).astype(BF16)

    blk = lambda w, c: pl.BlockSpec((ts, w), lambda i: (i, c))
    full = lambda a: pl.BlockSpec(a.shape, lambda i: (0,) * a.ndim)
    out_shapes = [
        jax.ShapeDtypeStruct((S, FOX_W), BF16), jax.ShapeDtypeStruct((S, FOX_W), BF16), jax.ShapeDtypeStruct((S, FOX_W), BF16),
        jax.ShapeDtypeStruct((S, FOX_W), BF16),
        jax.ShapeDtypeStruct((S, SB_W), BF16), jax.ShapeDtypeStruct((S, SB_W), BF16), jax.ShapeDtypeStruct((S, SB_W), BF16),
        jax.ShapeDtypeStruct((S, POOL_W), BF16), jax.ShapeDtypeStruct((S, POOL_W), F32), jax.ShapeDtypeStruct((S, POOL_W), BF16),
    ]
    out_specs = [
        blk(FOX_W, 0), blk(FOX_W, 0), blk(FOX_W, 0), blk(FOX_W, 0),
        blk(SB_W, 0), blk(SB_W, 0), blk(SB_W, 0),
        blk(POOL_W, 0), blk(POOL_W, 0), blk(POOL_W, 0),
    ]
    return pl.pallas_call(
        body, name="prep", grid=(nb,),
        in_specs=[blk(FOX_W, C_FQ // FOX_W), blk(FOX_W, C_FK // FOX_W), blk(FOX_W, C_FV // FOX_W), blk(2 * POOL_W, C_PX // (2 * POOL_W)),
                  pl.BlockSpec((POOL_HALO, 2 * POOL_W), lambda i: (jnp.maximum(i * hb - 1, 0), C_PX // (2 * POOL_W))),
                  blk(LANES, 0),
                  blk(SB_W, C_SQ // SB_W), blk(SB_W, C_SK // SB_W), blk(SB_W, C_SV // SB_W),
                  full(qg), full(kg), full(bfp), full(wpd), full(ps)],
        out_specs=out_specs, out_shape=out_shapes,
        scratch_shapes=[pltpu.VMEM((1, LANES), F32), pltpu.VMEM((ts, LANES), F32), pltpu.VMEM((ts + POOL_HALO, POOL_W), F32)],
        compiler_params=_cparams(dimension_semantics=("arbitrary",)),
    )(projm, projm, projm, projm, projm, ffo, projm, projm, projm, qg, kg, bfp, wpd, ps)


def _pair_masks(x):
    ma = _iota((1, LANES), 1) < HEAD_DIM
    zero = jnp.zeros_like(x)
    return jnp.where(ma, x, zero), jnp.where(ma, zero, x)


DIAG_TILE = 256


def _diag_tiles(tq, size=DIAG_TILE):
    size = min(tq, size)
    return [(t * size, size) for t in range(tq // size)]


def _put_rows(old, new, r0):
    return new if r0 == 0 else jnp.concatenate([old[:r0], new], axis=0)


def _aug_queries(q):
    lane = _iota((1, LANES), 1)
    one = jnp.ones_like(q)
    zero = jnp.zeros_like(q)
    qa = jnp.where(lane < HEAD_DIM, q, jnp.where(lane < HEAD_DIM + AUG, one, zero))
    qb = jnp.where(lane >= HEAD_DIM, q, jnp.where(lane < AUG, one, zero))
    return qa, qb


EXP_DEAD = -105.0
PACK = 16


def _fox_walk_left(nfull, tk, block, carry, k_refs, qk_bound, row_floor):
    lane = _iota((1, LANES), 1)

    def alive(h, jj, c):
        k0 = pl.multiple_of(jnp.maximum(nfull - 1 - jj, 0) * tk + tk - PACK, PACK)
        last = k_refs[h][pl.ds(k0, PACK), :].astype(F32)
        lo = HEAD_DIM if h == 0 else 0
        negc = jnp.sum(jnp.where((lane >= lo) & (lane < lo + AUG), last, 0.0), axis=1, keepdims=True)
        return qk_bound + jnp.max(negc) - row_floor(c)[h] >= EXP_DEAD

    def walk(heads, jj0, c0):
        def go_on(state):
            jj, c = state
            ok = jj < nfull
            for h in heads:
                ok = ok & alive(h, jj, c)
            return ok

        def step(state):
            jj, c = state
            return jj + 1, block(pl.multiple_of((nfull - 1 - jj) * tk, tk), tk, 0, c, False, heads)

        return lax.while_loop(go_on, step, (jj0, c0))

    jj_pair, carry = walk((0, 1), jnp.int32(0), carry)
    carry = walk((0,), jj_pair, carry)[1]
    return walk((1,), jj_pair, carry)[1]


def _fox_fwd(qn, ka, kb, v, projm, qkb, *, tq, tk):
    S = qn.shape[0]
    npair = FOX_HEADS // 2

    def body(q_ref, ka_ref, kb_ref, v_ref, fg_ref, qkb_ref, o_ref, lse_ref, fm_ref):
        qi = pl.program_id(1)
        lane = _iota((1, LANES), 1)
        ma = lane < HEAD_DIM
        qaug = _aug_queries(q_ref[...])
        k_refs = (ka_ref, kb_ref)

        def block(k0, tkl, r0, carry, masked, heads=(0, 1)):
            vb = v_ref[pl.ds(k0, tkl), :]
            if masked:
                mask = (k0 + _iota((tq - r0, tkl), 1)) <= (qi * tq + r0 + _iota((tq - r0, tkl), 0))
            scores = {h: _dot_nt(qaug[h][r0:], k_refs[h][pl.ds(k0, tkl), :]) for h in heads}
            new = list(carry)
            for h in heads:
                m, l, acc = [x[r0:] for x in carry[h]]
                s = jnp.where(mask, scores[h], NEG) if masked else scores[h]
                m_new = jnp.maximum(m, jnp.max(s, axis=1, keepdims=True))
                alpha = jnp.exp(m - m_new)
                p = jnp.exp(s - m_new)
                sub = (m_new, alpha * l + jnp.sum(p, axis=1, keepdims=True), alpha * acc + _dot(p.astype(BF16), vb))
                new[h] = tuple(_put_rows(old, x, r0) for old, x in zip(carry[h], sub))
            return tuple(new)

        carry = tuple((jnp.full((tq, 1), NEG, F32), jnp.zeros((tq, 1), F32), jnp.zeros((tq, LANES), F32)) for _ in range(2))
        for off, size in _diag_tiles(tq, tk):
            carry = block(pl.multiple_of(qi * tq + off, size), size, off, carry, True)
        carry = _fox_walk_left((qi * tq) // tk, tk, block, carry, k_refs, jnp.max(qkb_ref[...]),
                               lambda c: (jnp.min(c[0][0]), jnp.min(c[1][0])))
        (ma_, la, acca), (mb_, lb, accb) = carry
        o = jnp.where(ma, acca / la, accb / lb)
        o_ref[...] = o
        lse_ref[...] = jnp.where(ma, ma_ + jnp.log(la), mb_ + jnp.log(lb))
        fg = fg_ref[...]
        fm_ref[...] = (o * (fg * _sigmoid(fg))).astype(BF16)

    qblk = pl.BlockSpec((tq, LANES), lambda p, i: (i, p))
    kvblk = pl.BlockSpec((S, LANES), lambda p, i: (0, p))
    return pl.pallas_call(
        body, name="fox_fwd", grid=(npair, S // tq),
        in_specs=[qblk, kvblk, kvblk, kvblk,
                  pl.BlockSpec((tq, LANES), lambda p, i: (i, C_FG // LANES + p)),
                  pl.BlockSpec((1, LANES), lambda p, i: (0, 0))],
        out_specs=[qblk, qblk, qblk],
        out_shape=[jax.ShapeDtypeStruct((S, FOX_W), F32), jax.ShapeDtypeStruct((S, FOX_W), F32), jax.ShapeDtypeStruct((S, FOX_W), BF16)],
        compiler_params=_cparams(dimension_semantics=("arbitrary", "arbitrary")),
    )(qn, ka, kb, v, projm, qkb)


def _suffix_sums(x, tmat2):
    return _dot(jnp.concatenate(_split2(x), axis=1), tmat2)


def _suffix_matrix(tk, inclusive):
    rr, cc = _iota((2 * tk, tk), 0) & (tk - 1), _iota((2 * tk, tk), 1)
    return _ones_where(rr >= cc) if inclusive else _ones_where(rr > cc)


def _sb_scores(qh, kb, causal, tmat2, r_runs):
    heads = range(2)
    zs = [_dot_nt(qh[h], kb) for h in heads]
    nsps = [jnp.minimum(-z, 0.0) - jnp.log(1.0 + jnp.exp(-jnp.abs(z))) for z in zs]
    lbs = nsps if causal is None else [jnp.where(causal, n, 0.0) for n in nsps]
    rins = [_suffix_sums(lb, tmat2) for lb in lbs]
    args = [zs[h] + lbs[h] + (rins[h] + r_runs[h]) for h in heads]
    a_s = [jnp.exp(arg if causal is None else jnp.where(causal, arg, NEG)) for arg in args]
    return zs, nsps, lbs, a_s


def _sb_walk_left(nfull, tk, block, carry, running_sums):
    def alive(state):
        jj, c = state
        ra, rb = running_sums(c)
        return (jj < nfull) & (jnp.max(jnp.maximum(ra, rb)) >= EXP_DEAD)

    def step(state):
        jj, c = state
        return jj + 1, block(pl.multiple_of((nfull - 1 - jj) * tk, tk), 0, c, False)

    return lax.while_loop(alive, step, (jnp.int32(0), carry))[1]


def _sb_fwd(sq, sk, sv, projm, *, tq, tk):
    S = sq.shape[0]
    npair = SB_HEADS // 2

    def body(q_ref, k_ref, v_ref, sg_ref, o_ref, sm_ref):
        qi = pl.program_id(1)
        lane = _iota((1, LANES), 1)
        ma = lane < HEAD_DIM
        qh = _pair_masks(q_ref[...])
        tmat2 = _suffix_matrix(tk, inclusive=False)
        nfull = (qi * tq) // tk

        def block(k0, r0, carry, masked):
            nr = tq - r0
            kb = k_ref[pl.ds(k0, tk), :]
            vb = v_ref[pl.ds(k0, tk), :]
            causal = (k0 + _iota((nr, tk), 1)) < (qi * tq + r0 + _iota((nr, tk), 0)) if masked else None
            _, _, lbs, a_s = _sb_scores([q[r0:] for q in qh], kb, causal, tmat2, [carry[h][0][r0:] for h in range(2)])
            pv = _dot(jnp.concatenate([a.astype(BF16) for a in a_s], axis=0), vb)
            return tuple((_put_rows(carry[h][0], carry[h][0][r0:] + jnp.sum(lbs[h], axis=1, keepdims=True), r0),
                          _put_rows(carry[h][1], carry[h][1][r0:] + pv[h * nr:(h + 1) * nr], r0)) for h in range(2))

        carry = tuple((jnp.zeros((tq, 1), F32), jnp.zeros((tq, LANES), F32)) for _ in range(2))
        for off, size in reversed(_diag_tiles(tq)):
            assert size == tk
            carry = block(pl.multiple_of(qi * tq + off, tk), off, carry, True)
        (_, acca), (_, accb) = _sb_walk_left(nfull, tk, block, carry, lambda c: (c[0][0], c[1][0]))
        o = jnp.where(ma, acca, accb)
        o_ref[...] = o
        sg = sg_ref[...]
        sm_ref[...] = (o * (sg * _sigmoid(sg))).astype(BF16)

    qblk = pl.BlockSpec((tq, LANES), lambda p, i: (i, p))
    kvblk = pl.BlockSpec((S, LANES), lambda p, i: (0, p))
    return pl.pallas_call(
        body, name="sb_fwd", grid=(npair, S // tq),
        in_specs=[qblk, kvblk, kvblk, pl.BlockSpec((tq, LANES), lambda p, i: (i, C_SG // LANES + p))],
        out_specs=[qblk, qblk],
        out_shape=[jax.ShapeDtypeStruct((S, SB_W), F32), jax.ShapeDtypeStruct((S, SB_W), BF16)],
        compiler_params=_cparams(dimension_semantics=("arbitrary", "arbitrary")),
    )(sq, sk, sv, projm)


def _outproj(x, fm, pm, sm, w_out, layer, *, tm):
    S, D = x.shape

    def body(x_ref, fm_ref, pm_ref, sm_ref, w_ref, y_ref):
        y = x_ref[...] + _dot(fm_ref[...], w_ref[0:FOX_W, :])
        y = y + _dot(pm_ref[...], w_ref[FOX_W:FOX_W + POOL_W, :])
        y_ref[...] = y + _dot(sm_ref[...], w_ref[FOX_W + POOL_W:D_MIX, :])

    row = lambda w: pl.BlockSpec((tm, w), lambda i: (i, 0))
    return pl.pallas_call(
        body, name="outproj", grid=(S // tm,),
        in_specs=[row(D), row(FOX_W), row(POOL_W), row(SB_W), pl.BlockSpec((None, D_MIX, D), lambda i: (layer, 0, 0))],
        out_specs=row(D), out_shape=jax.ShapeDtypeStruct((S, D), F32),
        compiler_params=_cparams(dimension_semantics=("arbitrary",)),
    )(x, fm, pm, sm, w_out)


def _loss_head(y, target, *, tm):
    S, D = y.shape

    def body(y_ref, t_ref, dy_ref, sq_ref):
        @pl.when(pl.program_id(0) == 0)
        def _():
            sq_ref[...] = jnp.zeros_like(sq_ref)

        d = y_ref[...] - t_ref[...]
        dy_ref[...] = d * (1.0 / D)
        sq_ref[...] += jnp.sum(d * d, axis=0, keepdims=True)

    row = pl.BlockSpec((tm, D), lambda i: (i, 0))
    return pl.pallas_call(
        body, name="loss_head", grid=(S // tm,),
        in_specs=[row, row], out_specs=[row, pl.BlockSpec((1, D), lambda i: (0, 0))],
        out_shape=[jax.ShapeDtypeStruct((S, D), F32), jax.ShapeDtypeStruct((1, D), F32)],
        compiler_params=_cparams(dimension_semantics=("arbitrary",)),
    )(y, target)


def _outproj_bwd(dy, fm, pm, sm, w_out, layer, stacks, *, tm):
    S, D = dy.shape

    def body(dy_ref, fm_ref, pm_ref, sm_ref, w_ref, dm_ref, dw_ref):
        @pl.when(pl.program_id(0) == 0)
        def _():
            dw_ref[...] = jnp.zeros_like(dw_ref)

        dyb = dy_ref[...].astype(BF16)
        dm_ref[...] = _dot_nt(dyb, w_ref[...])
        dw_ref[0:FOX_W, :] += _dot_tn(fm_ref[...], dyb)
        dw_ref[FOX_W:FOX_W + POOL_W, :] += _dot_tn(pm_ref[...], dyb)
        dw_ref[FOX_W + POOL_W:D_MIX, :] += _dot_tn(sm_ref[...], dyb)

    row = lambda w: pl.BlockSpec((tm, w), lambda i: (i, 0))
    wspec = pl.BlockSpec((None, D_MIX, D), lambda i: (layer, 0, 0))
    return _stack_call(
        body, "outproj_bwd", (S // tm,), [row(D), row(FOX_W), row(POOL_W), row(SB_W), wspec], (dy, fm, pm, sm, w_out),
        [pl.BlockSpec((None, D_MIX, D), lambda i: (layer, 0, 0))], [(D_MIX, D)], stacks,
        plain_specs=[row(D_MIX)], plain_shapes=[jax.ShapeDtypeStruct((S, D_MIX), F32)],
        compiler_params=_cparams(dimension_semantics=("arbitrary",)))


def _fox_bwd(qn, ka, kb, v, o, lse, dmix, projm, qkb, *, tq, tk):
    S = qn.shape[0]
    npair = FOX_HEADS // 2

    def body(q_ref, ka_ref, kb_ref, v_ref, o_ref, lse_ref, dm_ref, fg_ref, qkb_ref,
             dq_ref, dk_ref, dv_ref, dfg_ref, dct_ref, dcr_ref):
        qi = pl.program_id(1)

        @pl.when(qi == 0)
        def _():
            dk_ref[...] = jnp.zeros_like(dk_ref)
            dv_ref[...] = jnp.zeros_like(dv_ref)
            dct_ref[...] = jnp.zeros_like(dct_ref)

        lane = _iota((1, LANES), 1)
        ma = lane < HEAD_DIM
        qh = _pair_masks(q_ref[...])
        qaug = _aug_queries(q_ref[...])
        k_refs = (ka_ref, kb_ref)
        lsev = lse_ref[...]
        lse = (_lane_pick(lsev, lane, 0), _lane_pick(lsev, lane, HEAD_DIM))
        fg = fg_ref[...]
        silu, dsilu = _silu_pair(fg)
        dm = dm_ref[...]
        ov = o_ref[...]
        do = dm * silu
        dfg_ref[...] = dm * ov * dsilu
        dd = do * ov
        dsum = (jnp.sum(jnp.where(ma, dd, 0.0), axis=1, keepdims=True), jnp.sum(jnp.where(ma, 0.0, dd), axis=1, keepdims=True))
        doh = _pair_masks(do.astype(BF16))

        def block(k0, tkl, r0, carry, masked, heads=(0, 1)):
            vb = v_ref[pl.ds(k0, tkl), :]
            if masked:
                mask = (k0 + _iota((tq - r0, tkl), 1)) <= (qi * tq + r0 + _iota((tq - r0, tkl), 0))
            kaugs = {h: k_refs[h][pl.ds(k0, tkl), :] for h in heads}
            scores = {h: _dot_nt(qaug[h][r0:], kaugs[h]) for h in heads}
            dps = {h: _dot_nt(doh[h][r0:], vb) for h in heads}
            ps, dss = [], []
            rows = [carry[1], carry[2]]
            for h in heads:
                s = jnp.where(mask, scores[h], NEG) if masked else scores[h]
                p = jnp.exp(s - lse[h][r0:])
                dsf = p * (dps[h] - dsum[h][r0:])
                dct_ref[0, h:h + 1, pl.ds(k0, tkl)] -= jnp.sum(dsf, axis=0, keepdims=True)
                rows[h] = _put_rows(carry[1 + h], carry[1 + h][r0:] + jnp.sum(dsf, axis=1, keepdims=True), r0)
                ps.append(p.astype(BF16))
                dss.append(dsf.astype(BF16))
            dv_ref[pl.ds(k0, tkl), :] += _dot_tn(jnp.concatenate(ps, axis=0), jnp.concatenate([doh[h][r0:] for h in heads], axis=0))
            dk_ref[pl.ds(k0, tkl), :] += _dot_tn(jnp.concatenate(dss, axis=0), jnp.concatenate([qh[h][r0:] for h in heads], axis=0))
            kh = jnp.concatenate([_pair_masks(kaugs[h])[h] for h in heads], axis=0)
            dq = _put_rows(carry[0], carry[0][r0:] + _dot(jnp.concatenate(dss, axis=1), kh), r0)
            return (dq, rows[0], rows[1])

        zcol = jnp.zeros((tq, 1), F32)
        carry = (jnp.zeros((tq, LANES), F32), zcol, zcol)
        for off, size in _diag_tiles(tq):
            carry = block(pl.multiple_of(qi * tq + off, size), size, off, carry, True)
        floors = (jnp.min(lse[0]), jnp.min(lse[1]))
        dq, rowa, rowb = _fox_walk_left((qi * tq) // tk, tk, block, carry, k_refs, jnp.max(qkb_ref[...]), lambda c: floors)
        dq_ref[...] = dq * QK_SCALE
        dcr_ref[0] = jnp.where(ma, rowa, rowb)

    qblk = pl.BlockSpec((tq, LANES), lambda p, i: (i, p))
    kvblk = pl.BlockSpec((S, LANES), lambda p, i: (0, p))
    f32out = jax.ShapeDtypeStruct((S, FOX_W), F32)
    ctblk = pl.BlockSpec((1, FF_STRIDE, S), lambda p, i: (p, 0, 0))
    return pl.pallas_call(
        body, name="fox_bwd", grid=(npair, S // tq),
        in_specs=[qblk, kvblk, kvblk, kvblk, qblk, qblk, qblk,
                  pl.BlockSpec((tq, LANES), lambda p, i: (i, C_FG // LANES + p)),
                  pl.BlockSpec((1, LANES), lambda p, i: (0, 0))],
        out_specs=[qblk, kvblk, kvblk, qblk, ctblk, pl.BlockSpec((1, tq, LANES), lambda p, i: (p, i, 0))],
        out_shape=[f32out, f32out, f32out, f32out, jax.ShapeDtypeStruct((npair, FF_STRIDE, S), F32),
                   jax.ShapeDtypeStruct((npair, S, LANES), F32)],
        compiler_params=_cparams(dimension_semantics=("arbitrary", "arbitrary")),
    )(qn, ka, kb, v, o, lse, dmix, projm, qkb)


def _sb_bwd(sq, sk, sv, o, dmix, projm, *, tq, tk):
    S = sq.shape[0]
    npair = SB_HEADS // 2
    mix0 = (FOX_W + POOL_W) // LANES

    def body(q_ref, k_ref, v_ref, o_ref, dm_ref, sg_ref, dq_ref, dk_ref, dv_ref, dsg_ref):
        qi = pl.program_id(1)

        @pl.when(qi == 0)
        def _():
            dk_ref[...] = jnp.zeros_like(dk_ref)
            dv_ref[...] = jnp.zeros_like(dv_ref)

        lane = _iota((1, LANES), 1)
        ma = lane < HEAD_DIM
        qh = _pair_masks(q_ref[...])
        sg = sg_ref[...]
        silu, dsilu = _silu_pair(sg)
        dm = dm_ref[...]
        ov = o_ref[...]
        do = dm * silu
        dsg_ref[...] = dm * ov * dsilu
        dob = do.astype(BF16)
        dd = dob.astype(F32) * ov
        dsum = (jnp.sum(jnp.where(ma, dd, 0.0), axis=1, keepdims=True), jnp.sum(jnp.where(ma, 0.0, dd), axis=1, keepdims=True))
        doh = _pair_masks(dob)
        tmat2 = _suffix_matrix(tk, inclusive=False)
        tmat2_inc = _suffix_matrix(tk, inclusive=True)
        nfull = (qi * tq) // tk

        def block(k0, r0, carry, masked):
            nr = tq - r0
            kb = k_ref[pl.ds(k0, tk), :]
            vb = v_ref[pl.ds(k0, tk), :]
            kh = _pair_masks(kb)
            causal = (k0 + _iota((nr, tk), 1)) < (qi * tq + r0 + _iota((nr, tk), 0)) if masked else None
            heads = range(2)
            qs = [q[r0:] for q in qh]
            dos = [d[r0:] for d in doh]
            das = [_dot_nt(dos[h], vb) for h in heads]
            zs, nsps, lbs, a_s = _sb_scores(qs, kb, causal, tmat2, [carry[h][0][r0:] for h in heads])
            abs_ = [a.astype(BF16) for a in a_s]
            us = [abs_[h].astype(F32) * das[h] for h in heads]
            uins = [_suffix_sums(u, tmat2_inc) for u in us]
            dzs = []
            for h in heads:
                cum_u = dsum[h][r0:] - (uins[h] + carry[h][1][r0:])
                dz = us[h] * jnp.exp(nsps[h]) - jnp.exp(zs[h] + nsps[h]) * cum_u
                if masked:
                    dz = jnp.where(causal, dz, 0.0)
                dzs.append(dz.astype(BF16))
            dv_ref[pl.ds(k0, tk), :] += _dot_tn(jnp.concatenate(abs_, axis=0), jnp.concatenate(dos, axis=0))
            dk_ref[pl.ds(k0, tk), :] += _dot_tn(jnp.concatenate(dzs, axis=0), jnp.concatenate(qs, axis=0))
            dq = _put_rows(carry[2], carry[2][r0:] + _dot(jnp.concatenate(dzs, axis=1), jnp.concatenate(kh, axis=0)), r0)
            new = [(_put_rows(carry[h][0], carry[h][0][r0:] + jnp.sum(lbs[h], axis=1, keepdims=True), r0),
                    _put_rows(carry[h][1], carry[h][1][r0:] + jnp.sum(us[h], axis=1, keepdims=True), r0)) for h in heads]
            return (new[0], new[1], dq)

        zcol = jnp.zeros((tq, 1), F32)
        carry = ((zcol, zcol), (zcol, zcol), jnp.zeros((tq, LANES), F32))
        for off, size in reversed(_diag_tiles(tq)):
            assert size == tk
            carry = block(pl.multiple_of(qi * tq + off, tk), off, carry, True)
        dq = _sb_walk_left(nfull, tk, block, carry, lambda c: (c[0][0], c[1][0]))[2]
        dq_ref[...] = dq * QK_SCALE

    qblk = pl.BlockSpec((tq, LANES), lambda p, i: (i, p))
    kvblk = pl.BlockSpec((S, LANES), lambda p, i: (0, p))
    f32out = jax.ShapeDtypeStruct((S, SB_W), F32)
    return pl.pallas_call(
        body, name="sb_bwd", grid=(npair, S // tq),
        in_specs=[qblk, kvblk, kvblk, qblk,
                  pl.BlockSpec((tq, LANES), lambda p, i: (i, mix0 + p)),
                  pl.BlockSpec((tq, LANES), lambda p, i: (i, C_SG // LANES + p))],
        out_specs=[qblk, kvblk, kvblk, qblk],
        out_shape=[f32out, f32out, f32out, f32out],
        compiler_params=_cparams(dimension_semantics=("arbitrary", "arbitrary")),
    )(sq, sk, sv, o, dmix, projm)


def _prep_bwd(projm, ffo, dqn, dkn, dct, dcr, dv, dfg, dsq, dsk, dsv, dsg, dmix, pooled, yp, qg, kg, bfp, wpd, ps, *, ts):
    S = projm.shape[0]
    nb = S // ts
    hb = ts // POOL_HALO
    npair = FOX_HEADS // 2
    last_halo = S // POOL_HALO - 1

    def body(fq_ref, fk_ref, pp_ref, pph_ref, ff_ref,
             dqn_ref, dkn_ref, dct_ref, dcr_ref, dv_ref, dfg_ref, dsq_ref, dsk_ref, dsv_ref, dsg_ref,
             dmp_ref, dmh_ref, pooled_ref, yp_ref, qg_ref, kg_ref, bf_ref, wpd_ref, ps_ref,
             dp_ref, dqg_ref, dkg_ref, dbf_ref, dwp_ref, dps_ref,
             carry_ref, dl_ref, buf_ref, dct_s):
        i = pl.program_id(0)
        blk = nb - 1 - i

        @pl.when(i == 0)
        def _():
            carry_ref[...] = jnp.zeros_like(carry_ref)
            dqg_ref[...] = jnp.zeros_like(dqg_ref)
            dkg_ref[...] = jnp.zeros_like(dkg_ref)
            dbf_ref[...] = jnp.zeros_like(dbf_ref)
            dwp_ref[...] = jnp.zeros_like(dwp_ref)
            dps_ref[...] = jnp.zeros_like(dps_ref)

        bd = _head_blockdiag()
        for raw_ref, g_ref, dn, dg_ref, col in ((fq_ref, qg_ref, dqn_ref[...], dqg_ref, C_FQ), (fk_ref, kg_ref, dkn_ref[...], dkg_ref, C_FK)):
            q = raw_ref[...]
            rstd = lax.rsqrt(_group_sum(q * q, bd) * (1.0 / HEAD_DIM) + EPS)
            xhat = q * rstd
            dg_ref[...] += jnp.sum(dn * xhat, axis=0, keepdims=True)
            dyg = dn * g_ref[...]
            mean = _group_sum(dyg * xhat, bd) * (1.0 / HEAD_DIM)
            dp_ref[:, col:col + FOX_W] = (rstd * (dyg - xhat * mean)).astype(BF16)
        dp_ref[:, C_FV:C_FV + FOX_W] = dv_ref[...].astype(BF16)
        dp_ref[:, C_FG:C_FG + FOX_W] = dfg_ref[...].astype(BF16)
        dp_ref[:, C_SQ:C_SQ + SB_W] = dsq_ref[...].astype(BF16)
        dp_ref[:, C_SK:C_SK + SB_W] = dsk_ref[...].astype(BF16)
        dp_ref[:, C_SV:C_SV + SB_W] = dsv_ref[...].astype(BF16)
        dp_ref[:, C_SG:C_SG + SB_W] = dsg_ref[...].astype(BF16)

        dct_s[...] = jnp.zeros_like(dct_s)
        for p in range(npair):
            dct_s[FF_STRIDE * p:FF_STRIDE * (p + 1), :] = dct_ref[p]
        dc = dct_s[...].T
        lane = _iota((1, LANES), 1)
        for p in range(npair):
            dcr = dcr_ref[p]
            dc = dc + jnp.where(lane == FF_STRIDE * p, _lane_pick(dcr, lane, 0), 0.0)
            dc = dc + jnp.where(lane == FF_STRIDE * p + 1, _lane_pick(dcr, lane, HEAD_DIM), 0.0)
        triu = _ones_where(_iota((ts, ts), 1) >= _iota((ts, ts), 0))
        dlf = _dot_exact_lhs(triu, dc) + carry_ref[...]
        dl_ref[...] = dlf
        carry_ref[...] = dl_ref[0:1, :]
        z = ff_ref[...] + bf_ref[...]
        dff = dlf * (1.0 / (1.0 + jnp.exp(z)))
        dbf_ref[...] += jnp.sum(dff, axis=0, keepdims=True)
        dp_ref[:, PM:PW] = dff.astype(BF16)

        psv = ps_ref[...]
        wpdv = wpd_ref[...]
        lane_group = _iota((1, POOL_W), 1) >> 6
        wlen = _pool_group_select(lane_group, [float(w) for w in POOL_WINDOWS])
        pg = pp_ref[:, POOL_W:2 * POOL_W]
        silu, dsilu = _silu_pair(pg)
        dmp = dmp_ref[...]
        ypv = yp_ref[...]
        dp_ref[:, C_PG:C_PG + POOL_W] = (dmp * (ypv * psv) * dsilu).astype(BF16)
        dps_ref[...] += jnp.sum(dmp * silu * ypv, axis=0, keepdims=True)
        dyp = (dmp * psv * silu).astype(BF16)
        dwp_ref[...] += _dot_tn(pooled_ref[...], dyp)
        dpooled = _dot_nt(dyp, wpdv)
        pgh = pph_ref[:, POOL_W:2 * POOL_W]
        dyph = (dmh_ref[...] * psv * (pgh * _sigmoid(pgh))).astype(BF16)
        dpooled_h = jnp.where(blk < nb - 1, _dot_nt(dyph, wpdv), 0.0)
        tpos = (blk * ts + _iota((ts, 1), 0) + 1).astype(F32)
        ev = dpooled / jnp.minimum(tpos, wlen)
        buf_ref[0:ts, :] = ev
        buf_ref[ts:ts + POOL_HALO, :] = dpooled_h / wlen
        acc = ev
        snaps = []
        for d in range(1, POOL_HALO):
            acc = acc + buf_ref[pl.ds(d, ts), :]
            if d + 1 in POOL_WINDOWS:
                snaps.append(acc)
        dp_ref[:, C_PX:C_PX + POOL_W] = (_pool_group_select(lane_group, snaps) - dpooled).astype(BF16)

    rblk = lambda w, c: pl.BlockSpec((ts, w), lambda i: (nb - 1 - i, c))
    full = lambda a: pl.BlockSpec(a.shape, lambda i: (0,) * a.ndim)
    halo = lambda w, c: pl.BlockSpec((POOL_HALO, w), lambda i: (jnp.minimum((nb - i) * hb, last_halo), c))
    acc_spec = lambda r, w: pl.BlockSpec((r, w), lambda i: (0, 0))
    return pl.pallas_call(
        body, name="prep_bwd", grid=(nb,),
        in_specs=[rblk(FOX_W, C_FQ // FOX_W), rblk(FOX_W, C_FK // FOX_W), rblk(2 * POOL_W, C_PX // (2 * POOL_W)),
                  halo(2 * POOL_W, C_PX // (2 * POOL_W)), rblk(LANES, 0),
                  rblk(FOX_W, 0), rblk(FOX_W, 0), pl.BlockSpec((npair, FF_STRIDE, ts), lambda i: (0, 0, nb - 1 - i)),
                  pl.BlockSpec((npair, ts, LANES), lambda i: (0, nb - 1 - i, 0)), rblk(FOX_W, 0), rblk(FOX_W, 0),
                  rblk(SB_W, 0), rblk(SB_W, 0), rblk(SB_W, 0), rblk(SB_W, 0),
                  rblk(POOL_W, FOX_W // POOL_W), halo(POOL_W, FOX_W // POOL_W), rblk(POOL_W, 0), rblk(POOL_W, 0),
                  full(qg), full(kg), full(bfp), full(wpd), full(ps)],
        out_specs=[rblk(PW, 0), acc_spec(1, FOX_W), acc_spec(1, FOX_W), acc_spec(1, LANES), acc_spec(POOL_W, POOL_W), acc_spec(1, POOL_W)],
        out_shape=[jax.ShapeDtypeStruct((S, PW), BF16), jax.ShapeDtypeStruct((1, FOX_W), F32), jax.ShapeDtypeStruct((1, FOX_W), F32),
                   jax.ShapeDtypeStruct((1, LANES), F32), jax.ShapeDtypeStruct((POOL_W, POOL_W), F32), jax.ShapeDtypeStruct((1, POOL_W), F32)],
        scratch_shapes=[pltpu.VMEM((1, LANES), F32), pltpu.VMEM((ts, LANES), F32), pltpu.VMEM((ts + POOL_HALO, POOL_W), F32),
                        pltpu.VMEM((LANES, ts), F32)],
        compiler_params=_cparams(dimension_semantics=("arbitrary",)),
    )(projm, projm, projm, projm, ffo, dqn, dkn, dct, dcr, dv, dfg, dsq, dsk, dsv, dsg, dmix, dmix, pooled, yp, qg, kg, bfp, wpd, ps)


def _stack_call(body, name, grid, in_specs, operands, slot_specs, slot_shapes, stacks, plain_specs=(), plain_shapes=(), **kw):
    out_specs = list(plain_specs) + list(slot_specs)
    out_shape = list(plain_shapes) + [jax.ShapeDtypeStruct((DEPTH,) + s, F32) for s in slot_shapes]
    if stacks is None:
        return pl.pallas_call(body, name=name, grid=grid, in_specs=in_specs, out_specs=out_specs, out_shape=out_shape, **kw)(*operands)
    n = len(operands)

    def aliased_body(*refs):
        body(*refs[:n], *refs[n + len(stacks):])

    return pl.pallas_call(
        aliased_body, name=name, grid=grid, in_specs=list(in_specs) + [pl.BlockSpec(memory_space=pl.ANY)] * len(stacks),
        out_specs=out_specs, out_shape=out_shape,
        input_output_aliases={n + k: len(plain_specs) + k for k in range(len(stacks))}, **kw)(*operands, *stacks)


def _inproj_dw(h, dproj, layer, stacks, *, ts, tn):
    S, D = h.shape
    nj = PM // tn

    def body(h_ref, dp_ref, dpf_ref, dw_ref, dwf_ref):
        s = pl.program_id(1)

        @pl.when(s == 0)
        def _():
            dw_ref[...] = jnp.zeros_like(dw_ref)

        @pl.when((s == 0) & (pl.program_id(0) == 0))
        def _():
            dwf_ref[...] = jnp.zeros_like(dwf_ref)

        hv = h_ref[...]
        dw_ref[...] += _dot_tn(dp_ref[...], hv)

        @pl.when(pl.program_id(0) == 0)
        def _():
            dwf_ref[...] += _dot_tn(dpf_ref[...], hv)

    return _stack_call(
        body, "inproj_dw", (nj, S // ts),
        [pl.BlockSpec((ts, D), lambda j, s: (s, 0)),
         pl.BlockSpec((ts, tn), lambda j, s: (s, j)),
         pl.BlockSpec((ts, LANES), lambda j, s: (s, PM // LANES))],
        (h, dproj, dproj),
        [pl.BlockSpec((None, tn, D), lambda j, s: (layer, j, 0)), pl.BlockSpec((None, LANES, D), lambda j, s: (layer, 0, 0))],
        [(PM, D), (LANES, D)], stacks,
        compiler_params=_cparams(dimension_semantics=("arbitrary", "arbitrary")))


def _inproj_dx(dproj, wt_all, layer, x, g, dy, *, tm):
    S, D = x.shape

    def body(dp_ref, w_ref, x_ref, g_ref, dy_ref, dx_ref, dg_ref):
        @pl.when(pl.program_id(0) == 0)
        def _():
            dg_ref[...] = jnp.zeros_like(dg_ref)

        dh = _dot(dp_ref[...], w_ref[...])
        xf = x_ref[...]
        rstd = lax.rsqrt(jnp.mean(xf * xf, axis=-1, keepdims=True) + EPS)
        xhat = xf * rstd
        dg_ref[...] += jnp.sum(dh * xhat, axis=0, keepdims=True)
        dyg = dh * g_ref[...]
        mean = jnp.mean(dyg * xhat, axis=-1, keepdims=True)
        dx_ref[...] = rstd * (dyg - xhat * mean) + dy_ref[...]

    row = lambda w: pl.BlockSpec((tm, w), lambda i: (i, 0))
    return pl.pallas_call(
        body, name="inproj_dx", grid=(S // tm,),
        in_specs=[row(PW), pl.BlockSpec((None, PW, D), lambda i: (layer, 0, 0)), row(D), pl.BlockSpec((1, D), lambda i: (0, 0)), row(D)],
        out_specs=[row(D), pl.BlockSpec((1, D), lambda i: (0, 0))],
        out_shape=[jax.ShapeDtypeStruct((S, D), F32), jax.ShapeDtypeStruct((1, D), F32)],
        compiler_params=_cparams(dimension_semantics=("arbitrary",)),
    )(dproj, wt_all, x, g, dy)


def _adam_update(w, g, m, v):
    nm = ADAM_B1 * m + (1.0 - ADAM_B1) * g
    nv = ADAM_B2 * v + (1.0 - ADAM_B2) * (g * g)
    m_hat = nm / (1.0 - ADAM_B1 ** ADAM_STEP)
    v_hat = nv / (1.0 - ADAM_B2 ** ADAM_STEP)
    return -ADAM_LR * (m_hat / (jnp.sqrt(v_hat) + ADAM_EPS) + ADAM_WD * w), nm, nv


def _adamw(w, g, m, v):
    L, R, C = w.shape
    tr = R if R <= 512 else 256

    def body(w_ref, g_ref, m_ref, v_ref, d_ref, nm_ref, nv_ref):
        d_ref[...], nm_ref[...], nv_ref[...] = _adam_update(w_ref[...], g_ref[...], m_ref[...], v_ref[...])

    spec = pl.BlockSpec((1, tr, C), lambda l, i: (l, i, 0))
    shp = jax.ShapeDtypeStruct((L, R, C), F32)
    return pl.pallas_call(
        body, name="adamw", grid=(L, R // tr), in_specs=[spec] * 4, out_specs=[spec] * 3, out_shape=[shp] * 3,
        compiler_params=_cparams(dimension_semantics=("arbitrary", "arbitrary")),
    )(w, g, m, v)


def _adamw_nd(w, g, m, v):
    shape = w.shape
    view = (1,) + shape if w.ndim == 2 else (shape[0], -1, shape[-1])
    outs = _adamw(w.reshape(view), g.reshape(view), m.reshape(view), v.reshape(view))
    return tuple(o.reshape(shape) for o in outs)


FLIP_C = (0, 0, 1)
FLIP_X = (1, 0, 0)
FLIP_Y = (0, 1, 0)
FLIP_XY = (1, 1, 0)
MESH = pl.DeviceIdType.MESH


def _peer(flip):
    me = (lax.axis_index("x"), lax.axis_index("y"), lax.axis_index("c"))
    return tuple(1 - a if f else a for a, f in zip(me, flip))


def _exchange(name, arrays, flips):
    n = len(arrays)

    def body(*refs):
        srcs, dsts = refs[:n], refs[n:2 * n]
        send_sems, recv_sems = refs[2 * n:]
        copies = [pltpu.make_async_remote_copy(src_ref=srcs[k], dst_ref=dsts[k], send_sem=send_sems.at[k], recv_sem=recv_sems.at[k],
                                               device_id=_peer(flips[k]), device_id_type=MESH) for k in range(n)]
        for cp in copies:
            cp.start()
        for cp in copies:
            cp.wait()

    anyspec = pl.BlockSpec(memory_space=pl.ANY)
    return pl.pallas_call(
        body, name=name, in_specs=[anyspec] * n, out_specs=[anyspec] * n,
        out_shape=[jax.ShapeDtypeStruct(a.shape, a.dtype) for a in arrays],
        scratch_shapes=[pltpu.SemaphoreType.DMA((n,)), pltpu.SemaphoreType.DMA((n,))],
    )(*arrays)


def _exchange_add(name, x, flip):
    def body(x_ref, o_ref, buf_ref, send_sem, recv_sem):
        cp = pltpu.make_async_remote_copy(src_ref=x_ref, dst_ref=buf_ref, send_sem=send_sem, recv_sem=recv_sem,
                                          device_id=_peer(flip), device_id_type=MESH)
        cp.start()
        cp.wait()
        o_ref[...] = x_ref[...] + buf_ref[...]

    vspec = pl.BlockSpec(memory_space=pltpu.VMEM)
    return pl.pallas_call(
        body, name=name, in_specs=[vspec], out_specs=vspec, out_shape=jax.ShapeDtypeStruct(x.shape, x.dtype),
        scratch_shapes=[pltpu.VMEM(x.shape, x.dtype), pltpu.SemaphoreType.DMA, pltpu.SemaphoreType.DMA],
    )(x)


def _chip_index():
    return 2 * lax.axis_index("x") + lax.axis_index("y")


def _gather_weights(w_in_t, w_out):
    wi = w_in_t.astype(BF16)
    wo = jnp.swapaxes(w_out, 0, 1).astype(BF16)
    halves = (wi.shape[0] // 2, wo.shape[0] // 2)
    ARR = 2
    TO_X, TO_Y, ON_Y, ON_X, SIB_X, SIB_Y, SIB_D0, SIB_D1, OWN = [ARR * k for k in range(9)]
    n_sems = ARR * 9

    def body(wi_ref, wo_ref, gi_ref, go_ref, send_sems, recv_sems):
        c = lax.axis_index("c")
        j = _chip_index()
        srcs = (wi_ref, wo_ref)
        dsts = (gi_ref, go_ref)
        def cuts(core):
            return [(pl.ds(h * core, h), pl.ds(h * core, h // 2), pl.ds(h * core + h // 2, h - h // 2)) for h in halves]
        mine, theirs = cuts(c), cuts(1 - c)
        HALF, Q0, Q1 = 0, 1, 2

        def copy(idx, src, dst, flip):
            return pltpu.make_async_remote_copy(src_ref=src, dst_ref=dst, send_sem=send_sems.at[idx], recv_sem=recv_sems.at[idx],
                                                device_id=_peer(flip), device_id_type=MESH)

        def slot(a, shard, cut):
            return dsts[a].at[shard, cut]

        jx, jy, jd = j ^ 2, j ^ 1, j ^ 3
        sends = []

        def start(cp):
            cp.start()
            sends.append(cp)

        for a in range(ARR):
            start(copy(TO_X + a, srcs[a].at[mine[a][HALF]], slot(a, j, mine[a][HALF]), FLIP_X))
            start(copy(TO_Y + a, srcs[a].at[mine[a][HALF]], slot(a, j, mine[a][HALF]), FLIP_Y))
        own = [copy(OWN + a, srcs[a], dsts[a].at[j], FLIP_C) for a in range(ARR)]
        for cp in own:
            cp.start()
        for a in range(ARR):
            copy(TO_X + a, slot(a, jx, mine[a][HALF]), slot(a, jx, mine[a][HALF]), FLIP_X).wait_recv()
            start(copy(ON_Y + a, slot(a, jx, mine[a][Q0]), slot(a, jx, mine[a][Q0]), FLIP_Y))
            start(copy(SIB_X + a, slot(a, jx, mine[a][HALF]), slot(a, jx, mine[a][HALF]), FLIP_C))
        for a in range(ARR):
            copy(TO_Y + a, slot(a, jy, mine[a][HALF]), slot(a, jy, mine[a][HALF]), FLIP_Y).wait_recv()
            start(copy(ON_X + a, slot(a, jy, mine[a][Q1]), slot(a, jy, mine[a][Q1]), FLIP_X))
            start(copy(SIB_Y + a, slot(a, jy, mine[a][HALF]), slot(a, jy, mine[a][HALF]), FLIP_C))
        for a in range(ARR):
            copy(ON_Y + a, slot(a, jd, mine[a][Q0]), slot(a, jd, mine[a][Q0]), FLIP_Y).wait_recv()
            start(copy(SIB_D0 + a, slot(a, jd, mine[a][Q0]), slot(a, jd, mine[a][Q0]), FLIP_C))
        for a in range(ARR):
            copy(ON_X + a, slot(a, jd, mine[a][Q1]), slot(a, jd, mine[a][Q1]), FLIP_X).wait_recv()
            start(copy(SIB_D1 + a, slot(a, jd, mine[a][Q1]), slot(a, jd, mine[a][Q1]), FLIP_C))
        for a in range(ARR):
            for idx, shard, cut in ((SIB_X, jx, HALF), (SIB_Y, jy, HALF), (SIB_D0, jd, Q0), (SIB_D1, jd, Q1)):
                copy(idx + a, slot(a, shard, theirs[a][cut]), slot(a, shard, theirs[a][cut]), FLIP_C).wait_recv()
        for cp in own:
            cp.wait()
        for cp in sends:
            cp.wait_send()

    anyspec = pl.BlockSpec(memory_space=pl.ANY)
    gi, go = pl.pallas_call(
        body, name="gather_weights", in_specs=[anyspec] * 2, out_specs=[anyspec] * 2,
        out_shape=[jax.ShapeDtypeStruct((4,) + wi.shape, BF16), jax.ShapeDtypeStruct((4,) + wo.shape, BF16)],
        scratch_shapes=[pltpu.SemaphoreType.DMA((n_sems,)), pltpu.SemaphoreType.DMA((n_sems,))],
    )(wi, wo)
    w_in_t_full = gi.reshape((4 * wi.shape[0],) + wi.shape[1:])
    w_out_full = jnp.swapaxes(go.reshape((4 * wo.shape[0],) + wo.shape[1:]), 0, 1)
    return w_in_t_full, w_out_full


def _to_aligned(w_t):
    _, L, D = w_t.shape
    npair = FOX_HEADS // 2
    ff = w_t[ORIG_FF:ORIG_REST].reshape(npair, 2, L, D)
    ff = jnp.pad(ff, ((0, 0), (0, FF_STRIDE - 2), (0, 0), (0, 0))).reshape(npair * FF_STRIDE, L, D)
    ff = jnp.pad(ff, ((0, LANES - npair * FF_STRIDE), (0, 0), (0, 0)))
    return jnp.swapaxes(jnp.concatenate([w_t[:ORIG_FOX], w_t[ORIG_REST:], ff], axis=0), 0, 1)


def _from_aligned(dw_t):
    n, _, D = dw_t.shape
    npair = FOX_HEADS // 2
    ff = dw_t[:, PM:PM + npair * FF_STRIDE].reshape(n, npair, FF_STRIDE, D)[:, :, :2].reshape(n, FOX_HEADS, D)
    return jnp.swapaxes(jnp.concatenate([dw_t[:, :ORIG_FOX], ff, dw_t[:, ORIG_FOX:PM]], axis=1), 0, 1)


def _half_layers(name, stack, got):
    L, R, C = stack.shape
    half = L // 2
    tr = min(256, R)
    c = lax.axis_index("c")
    which = ((1 - c) if got is None else c).astype(jnp.int32).reshape(1)

    def body(c_ref, x_ref, *refs):
        if got is None:
            refs[0][...] = x_ref[...].astype(BF16)
        else:
            acc = x_ref[...] + refs[0][...].astype(F32)
            refs[1][...] = acc
            refs[2][...] = acc.astype(BF16)

    plain = pl.BlockSpec((1, tr, C), lambda l, i, c_ref: (l, i, 0))
    picked = pl.BlockSpec((1, tr, C), lambda l, i, c_ref: (c_ref[0] * half + l, i, 0))
    shp = lambda dt: jax.ShapeDtypeStruct((half, R, C), dt)
    grid_spec = pltpu.PrefetchScalarGridSpec(
        num_scalar_prefetch=1, grid=(half, R // tr),
        in_specs=[picked] + ([] if got is None else [plain]), out_specs=[plain] if got is None else [plain, plain])
    return pl.pallas_call(
        body, name=name, grid_spec=grid_spec, out_shape=[shp(BF16)] if got is None else [shp(F32), shp(BF16)],
        compiler_params=_cparams(dimension_semantics=("arbitrary", "arbitrary")),
    )(which, stack, *([] if got is None else [got]))


def _reduce_scatter(stack_m, stack_f, stack_o, shard_cols, shard_rows):
    j = _chip_index()
    half = DEPTH // 2
    stacks = (stack_m, stack_f, stack_o)
    give = [_half_layers("rs_give", s, None)[0] for s in stacks]
    got = _exchange("rs_d2d", give, (FLIP_C,) * len(stacks))
    (m32, mbf), (f32_, fbf), (o32, obf) = [_half_layers("rs_add_chip", s, g) for s, g in zip(stacks, got)]
    d_model = stack_m.shape[2]

    def in_shards(m, f):
        return _from_aligned(jnp.concatenate([m, f], axis=1)).reshape(4, shard_cols, half, d_model)

    def out_shards(o):
        return jnp.moveaxis(o.reshape(half, 4, shard_rows, o.shape[-1]), 1, 0)

    chip = [(in_shards(m32, f32_), in_shards(mbf, fbf), 0), (out_shards(o32), out_shards(obf), 1)]
    shard = lambda a, idx: lax.dynamic_index_in_dim(a, idx, axis=0, keepdims=False)
    via = []
    for _, bf, axis in chip:
        diag = shard(bf, j ^ 3)
        cut = diag.shape[axis] // 2
        via += [lax.slice_in_dim(diag, 0, cut, axis=axis), lax.slice_in_dim(diag, cut, 2 * cut, axis=axis)]
    handed = _exchange("rs_via", via, (FLIP_X, FLIP_Y) * len(chip))
    sends = []
    for a, (f32_sum, _, axis) in enumerate(chip):
        sends.append(_add_half_along("rs_add_via", shard(f32_sum, j ^ 2), handed[2 * a + 1], axis, 1))
        sends.append(_add_half_along("rs_add_via", shard(f32_sum, j ^ 1), handed[2 * a], axis, 0))
    got = _exchange("rs_ici", sends, (FLIP_X, FLIP_Y) * len(chip))
    own_in, own_out = [shard(f32_sum, j) for f32_sum, _, _ in chip]
    mine_in = _add_rows("rs_add_in", own_in, list(got[0:2]))
    mine_out = _add_into_half("rs_add_out", own_out, list(got[2:4]))
    sib_in, g_out = _share_halves(mine_in, mine_out)
    return (mine_in, sib_in), g_out


def _add_half_along(name, base, extra, axis, which):
    lanes = min(ROW_LANE_CHUNK, base.shape[2])
    assert base.shape[axis] == 2 * extra.shape[axis]
    blk = tuple(base.shape[d] // 2 if d == axis else base.shape[d] for d in range(2)) + (lanes,)

    def body(b_ref, e_ref, o_ref):
        x = b_ref[...]
        o_ref[...] = jnp.where(pl.program_id(0) == which, x + e_ref[...].astype(F32), x).astype(BF16)

    at = lambda i, k: (i, 0, k) if axis == 0 else (0, i, k)
    return pl.pallas_call(
        body, name=name, grid=(2, base.shape[2] // lanes),
        in_specs=[pl.BlockSpec(blk, at), pl.BlockSpec(blk, lambda i, k: (0, 0, k))], out_specs=pl.BlockSpec(blk, at),
        out_shape=jax.ShapeDtypeStruct(base.shape, BF16),
        compiler_params=_cparams(dimension_semantics=("arbitrary", "arbitrary")),
    )(base, extra)


def _add_rows(name, first, others):
    n = len(others)

    def body(*refs):
        acc = refs[0][...]
        for r in refs[1:1 + n]:
            acc = acc + r[...].astype(F32)
        refs[1 + n][...] = acc

    grid, spec = _row_lane_blocks(first.shape)
    return pl.pallas_call(
        body, name=name, grid=grid, in_specs=[spec(first.shape[1])] * (1 + n), out_specs=spec(first.shape[1]),
        out_shape=jax.ShapeDtypeStruct(first.shape, F32),
        compiler_params=_cparams(dimension_semantics=("arbitrary", "arbitrary")),
    )(first, *others)


ROW_LANE_CHUNK = 256


def _row_lane_blocks(shape):
    rows, _, C = shape
    tr = rows // 2 if rows % 2 == 0 and rows > 64 else rows
    lanes = min(ROW_LANE_CHUNK, C)
    return (rows // tr, C // lanes), lambda n_mid: pl.BlockSpec((tr, n_mid, lanes), lambda i, k, *_: (i, 0, k))


def _add_into_half(name, first, others):
    half, rows, C = first.shape
    tr = min(256, rows)
    n = len(others)

    def body(c_ref, *refs):
        acc = refs[0][...]
        for r in refs[1:1 + n]:
            acc = acc + r[...].astype(F32)
        refs[1 + n][...] = acc

    grid_spec = pltpu.PrefetchScalarGridSpec(
        num_scalar_prefetch=1, grid=(half, rows // tr),
        in_specs=[pl.BlockSpec((1, tr, C), lambda l, i, c_ref: (l, i, 0))] * (1 + n),
        out_specs=pl.BlockSpec((1, tr, C), lambda l, i, c_ref: (c_ref[0] * half + l, i, 0)))
    return pl.pallas_call(
        body, name=name, grid_spec=grid_spec, out_shape=jax.ShapeDtypeStruct((2 * half, rows, C), F32),
        compiler_params=_cparams(dimension_semantics=("arbitrary", "arbitrary")),
    )(lax.axis_index("c").astype(jnp.int32).reshape(1), first, *others)


def _share_halves(mine, buf):
    half = DEPTH // 2

    def body(mine_ref, buf_in, sib_ref, buf_ref, send_sems, recv_sems):
        lay = pl.ds(half * lax.axis_index("c"), half)
        copies = [pltpu.make_async_remote_copy(src_ref=src, dst_ref=dst, send_sem=send_sems.at[k], recv_sem=recv_sems.at[k],
                                               device_id=_peer(FLIP_C), device_id_type=MESH)
                  for k, (src, dst) in enumerate(((mine_ref, sib_ref), (buf_ref.at[lay], buf_ref.at[lay])))]
        for cp in copies:
            cp.start()
        for cp in copies:
            cp.wait()

    anyspec = pl.BlockSpec(memory_space=pl.ANY)
    return pl.pallas_call(
        body, name="rs_share", in_specs=[anyspec] * 2, out_specs=[anyspec] * 2,
        out_shape=[jax.ShapeDtypeStruct(mine.shape, mine.dtype), jax.ShapeDtypeStruct(buf.shape, buf.dtype)],
        input_output_aliases={1: 1},
        scratch_shapes=[pltpu.SemaphoreType.DMA((2,)), pltpu.SemaphoreType.DMA((2,))],
    )(mine, buf)


def _adamw_halves(w, g_mine, g_sib, m, v):
    half = g_mine.shape[1]

    def body(c_ref, w_ref, gm_ref, gs_ref, m_ref, v_ref, g_ref, d_ref, nm_ref, nv_ref):
        first = c_ref[0] == 0
        gm, gs = gm_ref[...], gs_ref[...]
        for h, gv in enumerate((jnp.where(first, gm, gs), jnp.where(first, gs, gm))):
            lay = slice(half * h, half * (h + 1))
            g_ref[:, lay, :] = gv
            d_ref[:, lay, :], nm_ref[:, lay, :], nv_ref[:, lay, :] = _adam_update(w_ref[:, lay, :], gv, m_ref[:, lay, :], v_ref[:, lay, :])

    grid, spec = _row_lane_blocks(w.shape)
    full, part = spec(w.shape[1]), spec(half)
    grid_spec = pltpu.PrefetchScalarGridSpec(num_scalar_prefetch=1, grid=grid, in_specs=[full, part, part, full, full], out_specs=[full] * 4)
    return pl.pallas_call(
        body, name="adamw_halves", grid_spec=grid_spec, out_shape=[jax.ShapeDtypeStruct(w.shape, F32)] * 4,
        compiler_params=_cparams(dimension_semantics=("arbitrary", "arbitrary")),
    )(lax.axis_index("c").astype(jnp.int32).reshape(1), w, g_mine, g_sib, m, v)


def _all_reduce_small(x):
    x = _exchange_add("ar_c", x, FLIP_C)
    x = _exchange_add("ar_y", x, FLIP_Y)
    return _exchange_add("ar_x", x, FLIP_X)


def _blocks(S):
    return dict(tm=min(512, S), tm_proj=min(1024, S), ts=min(512, S), tq=min(512, S), tq_big=min(1024, S), tk=min(512, S), tks=min(256, S))


def _pair_pad(vec):
    npair = FOX_HEADS // 2
    v = jnp.pad(vec.reshape(npair, 2), ((0, 0), (0, FF_STRIDE - 2))).reshape(1, npair * FF_STRIDE)
    return jnp.pad(v, ((0, 0), (0, LANES - npair * FF_STRIDE)))


def _pair_unpad(row):
    npair = FOX_HEADS // 2
    return row[0, :npair * FF_STRIDE].reshape(npair, FF_STRIDE)[:, :2].reshape(FOX_HEADS)


def _pool_blockdiag(w_pool):
    g, cg, _ = w_pool.shape
    eye = jnp.eye(g, dtype=w_pool.dtype)
    return jnp.einsum("gh,gcd->gchd", eye, w_pool).reshape(g * cg, g * cg)


QK_BOUND_SLACK = 1.05


def _layer_params(norm_g, b_f, q_norm_g, k_norm_g, w_pool, pool_scale):
    qk_bound = QK_BOUND_SLACK * HEAD_DIM * QK_SCALE * jnp.max(jnp.abs(q_norm_g)) * jnp.max(jnp.abs(k_norm_g))
    return dict(g=norm_g.reshape(1, -1), qg=jnp.tile(q_norm_g, FOX_HEADS).reshape(1, FOX_W), kg=jnp.tile(k_norm_g, FOX_HEADS).reshape(1, FOX_W),
                bfp=_pair_pad(b_f), wpd=_pool_blockdiag(w_pool).astype(BF16), ps=pool_scale.reshape(1, POOL_W),
                qkb=jnp.full((1, LANES), qk_bound, F32))


def _layer_fwd(x, wt_all, w_out, layer, prm, bs):
    projm, ffo, h = _inproj(x, prm["g"], wt_all, layer, tm=bs["tm_proj"], tn=PROJ_TN)
    qn, ka, kb, v, sq, sk, sv, pooled, yp, pm = _prep(projm, ffo, prm["qg"], prm["kg"], prm["bfp"], prm["wpd"], prm["ps"], ts=bs["ts"])
    o, lse, fm = _fox_fwd(qn, ka, kb, v, projm, prm["qkb"], tq=bs["tq_big"], tk=bs["tk"])
    so, sm = _sb_fwd(sq, sk, sv, projm, tq=bs["tq_big"], tk=bs["tks"])
    y = _outproj(x, fm, pm, sm, w_out, layer, tm=bs["tm"])
    saved = dict(x=x, projm=projm, ffo=ffo, h=h, qn=qn, ka=ka, kb=kb, v=v, sq=sq, sk=sk, sv=sv, pooled=pooled, yp=yp,
                 o=o, lse=lse, so=so, fm=fm, pm=pm, sm=sm)
    return y, saved


def _layer_bwd(dy, wt_all, w_out, prm, sv_, bs, layer, stacks):
    dmix, stack_o = _outproj_bwd(dy, sv_["fm"], sv_["pm"], sv_["sm"], w_out, layer, None if stacks is None else stacks[2:], tm=bs["tm"])
    dqn, dkn, dv, dfg, dct, dcr = _fox_bwd(sv_["qn"], sv_["ka"], sv_["kb"], sv_["v"], sv_["o"], sv_["lse"], dmix, sv_["projm"],
                                      prm["qkb"], tq=bs["tq_big"], tk=bs["tk"])
    dsq, dsk, dsv, dsg = _sb_bwd(sv_["sq"], sv_["sk"], sv_["sv"], sv_["so"], dmix, sv_["projm"], tq=bs["tq"], tk=bs["tks"])
    dproj, dqg, dkg, dbf, dwp, dps = _prep_bwd(sv_["projm"], sv_["ffo"], dqn, dkn, dct, dcr, dv, dfg, dsq, dsk, dsv, dsg, dmix,
                                               sv_["pooled"], sv_["yp"], prm["qg"], prm["kg"], prm["bfp"], prm["wpd"], prm["ps"], ts=bs["ts"])
    stack_m, stack_f = _inproj_dw(sv_["h"], dproj, layer, None if stacks is None else stacks[:2], ts=bs["tm_proj"], tn=PROJ_TN)
    dx, dg = _inproj_dx(dproj, wt_all, layer, sv_["x"], prm["g"], dy, tm=min(256, bs["tm"]))
    grads = dict(
        norm_g=dg[0],
        b_f=_pair_unpad(dbf), q_norm_g=dqg.reshape(FOX_HEADS, HEAD_DIM).sum(0), k_norm_g=dkg.reshape(FOX_HEADS, HEAD_DIM).sum(0),
        w_pool=jnp.stack([dwp[HEAD_DIM * g:HEAD_DIM * (g + 1), HEAD_DIM * g:HEAD_DIM * (g + 1)] for g in range(4)]),
        pool_scale=dps[0])
    return dx, grads, (stack_m, stack_f, stack_o)


def _local_step(x, target, wt_all, w_out, norm_g, b_f, q_norm_g, k_norm_g, w_pool, pool_scale):
    S, D = x.shape
    bs = _blocks(S)
    prms = [_layer_params(norm_g[l], b_f[l], q_norm_g[l], k_norm_g[l], w_pool[l], pool_scale[l]) for l in range(DEPTH)]
    saved = []
    y = x
    for l in range(DEPTH):
        y, s_ = _layer_fwd(y, wt_all, w_out, l, prms[l], bs)
        saved.append(s_)
    dy, sq = _loss_head(y, target, tm=bs["tm"])
    loss = 0.5 * jnp.sum(sq) / D
    grads = [None] * DEPTH
    stacks = None
    for l in reversed(range(DEPTH)):
        dy, grads[l], stacks = _layer_bwd(dy, wt_all, w_out, prms[l], saved[l], bs, l, stacks)
    stacked = {k: jnp.stack([g[k] for g in grads]) for k in grads[0]}
    return loss, dy, stacked, stacks


SMALL = ("norm_g", "b_f", "q_norm_g", "k_norm_g", "w_pool", "pool_scale")


def _pack_small(gr):
    flat = jnp.concatenate([gr[k].reshape(-1) for k in SMALL])
    pad = (-flat.shape[0]) % (8 * LANES)
    return jnp.pad(flat, (0, pad)).reshape(-1, LANES)


def _unpack_small(packed, like):
    flat = packed.reshape(-1)
    out, off = {}, 0
    for k in SMALL:
        n = like[k].size
        out[k] = flat[off:off + n].reshape(like[k].shape)
        off += n
    return out


def kernel(x, norm_g, w_in, b_f, q_norm_g, k_norm_g, w_pool, pool_scale, w_out, loss_target, m_norm_g, m_w_in, m_b_f, m_q_norm_g, m_k_norm_g, m_w_pool, m_pool_scale, m_w_out, v_norm_g, v_w_in, v_b_f, v_q_norm_g, v_k_norm_g, v_w_pool, v_pool_scale, v_w_out):
    weights = dict(norm_g=norm_g, w_in=w_in, b_f=b_f, q_norm_g=q_norm_g, k_norm_g=k_norm_g, w_pool=w_pool, pool_scale=pool_scale, w_out=w_out)
    mom_m = dict(norm_g=m_norm_g, w_in=m_w_in, b_f=m_b_f, q_norm_g=m_q_norm_g, k_norm_g=m_k_norm_g, w_pool=m_w_pool, pool_scale=m_pool_scale, w_out=m_w_out)
    mom_v = dict(norm_g=v_norm_g, w_in=v_w_in, b_f=v_b_f, q_norm_g=v_q_norm_g, k_norm_g=v_k_norm_g, w_pool=v_w_pool, pool_scale=v_pool_scale, w_out=v_w_out)
    shard_cols = w_in.shape[2]
    shard_rows = w_out.shape[1]

    cols_first = lambda a: jnp.transpose(a, (2, 0, 1))
    w_in_t = cols_first(w_in)
    w_in_t_full, w_out_full = _gather_weights(w_in_t, w_out)
    wt_all = _to_aligned(w_in_t_full)
    loss, dx, gr, stacks = _local_step(x[0], loss_target[0], wt_all, w_out_full, norm_g, b_f, q_norm_g, k_norm_g, w_pool, pool_scale)
    loss = lax.psum(loss, ("x", "y", "c"))

    (g_in_mine, g_in_sib), g_w_out = _reduce_scatter(*stacks, shard_cols, shard_rows)
    small = _unpack_small(_all_reduce_small(_pack_small(gr)), {k: weights[k] for k in SMALL})
    grad_w = dict(small, w_out=g_w_out)

    names = ("norm_g", "w_in", "b_f", "q_norm_g", "k_norm_g", "w_pool", "pool_scale", "w_out")
    upd = {k: _adamw_nd(weights[k], grad_w[k], mom_m[k], mom_v[k]) for k in names if k != "w_in"}
    in_t = _adamw_halves(w_in_t, g_in_mine, g_in_sib, cols_first(mom_m["w_in"]), cols_first(mom_v["w_in"]))
    grad_w["w_in"], *upd["w_in"] = [jnp.transpose(a, (1, 2, 0)) for a in in_t]
    return (loss, dx[None], *[grad_w[k] for k in names], *[upd[k][0] for k in names], *[upd[k][1] for k in names], *[upd[k][2] for k in names])
```

```python
import functools

import jax
import jax.numpy as jnp
from jax import lax
from jax.experimental import pallas as pl
from jax.experimental.pallas import tpu as pltpu

F32 = jnp.float32
BF16 = jnp.bfloat16

DEPTH = 4
HEAD_DIM = 64
FOX_HEADS = 8
SB_HEADS = 4
FOX_W = FOX_HEADS * HEAD_DIM
SB_W = SB_HEADS * HEAD_DIM
POOL_W = 256
POOL_WINDOWS = (2, 4, 8, 16)
POOL_HALO = 16
D_MIX = FOX_W + POOL_W + SB_W
EPS = 1e-6
NEG = -1e30
QK_SCALE = HEAD_DIM ** -0.5

ORIG_FOX = 4 * FOX_W
ORIG_FF = ORIG_FOX
ORIG_REST = ORIG_FF + FOX_HEADS
D_IN = ORIG_REST + 2 * POOL_W + 4 * SB_W

C_FQ, C_FK, C_FV, C_FG = 0, FOX_W, 2 * FOX_W, 3 * FOX_W
C_PX = 4 * FOX_W
C_PG = C_PX + POOL_W
C_SQ = C_PG + POOL_W
C_SK, C_SV, C_SG = C_SQ + SB_W, C_SQ + 2 * SB_W, C_SQ + 3 * SB_W
PM = C_SG + SB_W
LANES = 128
PW = PM + LANES
FF_STRIDE = 8
AUG = 3

ADAM_LR = 0.001
ADAM_B1 = 0.9
ADAM_B2 = 0.999
ADAM_EPS = 1e-08
ADAM_WD = 0.01
ADAM_STEP = 10

VMEM_LIMIT = 48 * 1024 * 1024
PROJ_TN = PM // 2


def _cparams(**kw):
    return pltpu.CompilerParams(vmem_limit_bytes=VMEM_LIMIT, **kw)


def _dot(a, b):
    return jnp.dot(a, b, preferred_element_type=F32)


def _dot_nt(a, b):
    return lax.dot_general(a, b, (((1,), (1,)), ((), ())), preferred_element_type=F32)


def _dot_tn(a, b):
    return lax.dot_general(a, b, (((0,), (0,)), ((), ())), preferred_element_type=F32)


def _split2(x):
    hi = x.astype(BF16)
    lo = (x - hi.astype(F32)).astype(BF16)
    return hi, lo


def _split3(x):
    hi = x.astype(BF16)
    r = x - hi.astype(F32)
    mid = r.astype(BF16)
    lo = (r - mid.astype(F32)).astype(BF16)
    return hi, mid, lo


def _dot_exact_rhs(x, m):
    hi, mid, lo = _split3(x)
    return _dot(hi, m) + _dot(mid, m) + _dot(lo, m)


def _dot_exact_lhs(m, x):
    hi, mid, lo = _split3(x)
    return _dot(m, hi) + _dot(m, mid) + _dot(m, lo)


def _sigmoid(x):
    return 1.0 / (1.0 + jnp.exp(-x))


def _silu_pair(x):
    s = _sigmoid(x)
    return x * s, s * (1.0 + x * (1.0 - s))


def _iota(shape, dim):
    return lax.broadcasted_iota(jnp.int32, shape, dim)


def _ones_where(cond):
    return jnp.where(cond, 1.0, 0.0).astype(BF16)


GROUP_SLAB = 256


def _head_blockdiag():
    rows, cols = _iota((2 * GROUP_SLAB, GROUP_SLAB), 0) & (GROUP_SLAB - 1), _iota((2 * GROUP_SLAB, GROUP_SLAB), 1)
    return _ones_where((rows >> 6) == (cols >> 6))


def _group_sum(x, bd):
    hi, lo = _split2(x)
    slabs = [_dot(jnp.concatenate([hi[:, s:s + GROUP_SLAB], lo[:, s:s + GROUP_SLAB]], axis=1), bd) for s in range(0, x.shape[1], GROUP_SLAB)]
    return jnp.concatenate(slabs, axis=1)


def _lane_pick(x, lane_idx, lane):
    return jnp.sum(jnp.where(lane_idx == lane, x, 0.0), axis=1, keepdims=True)


def _inproj(x, g, wt_all, layer, *, tm, tn):
    S, D = x.shape
    nj = PM // tn

    def body(x_ref, g_ref, w_ref, wff_ref, proj_ref, ff_ref, h_ref):
        @pl.when(pl.program_id(1) == 0)
        def _():
            xf = x_ref[...]
            ms = jnp.mean(xf * xf, axis=-1, keepdims=True)
            h = (xf * lax.rsqrt(ms + EPS) * g_ref[...]).astype(BF16)
            h_ref[...] = h
            ff_ref[...] = _dot_nt(h, wff_ref[...])

        proj_ref[...] = _dot_nt(h_ref[...], w_ref[...])

    return pl.pallas_call(
        body, name="inproj", grid=(S // tm, nj),
        in_specs=[pl.BlockSpec((tm, D), lambda i, j: (i, 0)),
                  pl.BlockSpec((1, D), lambda i, j: (0, 0)),
                  pl.BlockSpec((None, tn, D), lambda i, j: (layer, j, 0)),
                  pl.BlockSpec((None, LANES, D), lambda i, j: (layer, PM // LANES, 0))],
        out_specs=[pl.BlockSpec((tm, tn), lambda i, j: (i, j)),
                   pl.BlockSpec((tm, LANES), lambda i, j: (i, 0)),
                   pl.BlockSpec((tm, D), lambda i, j: (i, 0))],
        out_shape=[jax.ShapeDtypeStruct((S, PM), F32), jax.ShapeDtypeStruct((S, LANES), F32),
                   jax.ShapeDtypeStruct((S, D), BF16)],
        compiler_params=_cparams(dimension_semantics=("arbitrary", "arbitrary")),
    )(x, g, wt_all, wt_all)


def _pool_group_select(lane_group, vals):
    return jnp.where(lane_group == 0, vals[0], jnp.where(lane_group == 1, vals[1], jnp.where(lane_group == 2, vals[2], vals[3])))


def _prep(projm, ffo, qg, kg, bfp, wpd, ps, *, ts):
    S = projm.shape[0]
    nb = S // ts
    hb = ts // POOL_HALO

    def body(fq_ref, fk_ref, fv_ref, pp_ref, halo_ref, ff_ref, sq_ref, sk_ref, sv_ref,
             qg_ref, kg_ref, bf_ref, wpd_ref, ps_ref,
             qn_ref, ka_ref, kb_ref, v_ref, sqo_ref, sko_ref, svo_ref, pooled_ref, yp_ref, pm_ref,
             carry_ref, c_ref, buf_ref):
        i = pl.program_id(0)
        bd = _head_blockdiag()
        normed = []
        for src, g_ref in ((fq_ref, qg_ref), (fk_ref, kg_ref)):
            q = src[...]
            ss = _group_sum(q * q, bd)
            normed.append(q * lax.rsqrt(ss * (1.0 / HEAD_DIM) + EPS) * g_ref[...])
        qn_ref[...] = (normed[0] * QK_SCALE).astype(BF16)
        kn = normed[1]
        v_ref[...] = fv_ref[...].astype(BF16)
        sqo_ref[...] = (sq_ref[...] * QK_SCALE).astype(BF16)
        sko_ref[...] = sk_ref[...].astype(BF16)
        svo_ref[...] = sv_ref[...].astype(BF16)

        @pl.when(i == 0)
        def _():
            carry_ref[...] = jnp.zeros_like(carry_ref)

        z = ff_ref[...] + bf_ref[...]
        lf = jnp.minimum(z, 0.0) - jnp.log(1.0 + jnp.exp(-jnp.abs(z)))
        tri = _ones_where(_iota((ts, ts), 1) <= _iota((ts, ts), 0))
        c = _dot_exact_lhs(tri, lf) + carry_ref[...]
        c_ref[...] = c
        carry_ref[...] = c_ref[ts - 1:ts, :]
        parts = jnp.concatenate(_split3(-c), axis=1)
        row = _iota((AUG * LANES, FOX_W), 0)
        col = _iota((AUG * LANES, FOX_W), 1)
        part, src = row >> 7, row & (LANES - 1)
        pair, off = col >> 7, col & (LANES - 1)
        sel_a = _ones_where((src == FF_STRIDE * pair) & (off == HEAD_DIM + part))
        sel_b = _ones_where((src == FF_STRIDE * pair + 1) & (off == part))
        first_half = (_iota((1, FOX_W), 1) & HEAD_DIM) == 0
        ka_ref[...] = jnp.where(first_half, kn, _dot(parts, sel_a)).astype(BF16)
        kb_ref[...] = jnp.where(first_half, _dot(parts, sel_b), kn).astype(BF16)

        x = pp_ref[:, 0:POOL_W]
        pg = pp_ref[:, POOL_W:2 * POOL_W]
        halo = jnp.where(i > 0, halo_ref[:, 0:POOL_W], 0.0)
        buf_ref[0:POOL_HALO, :] = halo
        buf_ref[POOL_HALO:POOL_HALO + ts, :] = x
        acc = x
        snaps = []
        for d in range(1, POOL_HALO):
            acc = acc + buf_ref[pl.ds(POOL_HALO - d, ts), :]
            if d + 1 in POOL_WINDOWS:
                snaps.append(acc)
        lane_group = _iota((1, POOL_W), 1) >> 6
        wsum = _pool_group_select(lane_group, snaps)
        wlen = _pool_group_select(lane_group, [float(w) for w in POOL_WINDOWS])
        tpos = (i * ts + _iota((ts, 1), 0) + 1).astype(F32)
        pooled = wsum / jnp.minimum(tpos, wlen) - x
        pb = pooled.astype(BF16)
        pooled_ref[...] = pb
        yp = _dot(pb, wpd_ref[...])
        yp_ref[...] = yp
        pm_ref[...] = (yp * ps_ref[...] * (pg * _sigmoid(pg))).astype(BF16)

    blk = lambda w, c: pl.BlockSpec((ts, w), lambda i: (i, c))
    full = lambda a: pl.BlockSpec(a.shape, lambda i: (0,) * a.ndim)
    out_shapes = [
        jax.ShapeDtypeStruct((S, FOX_W), BF16), jax.ShapeDtypeStruct((S, FOX_W), BF16), jax.ShapeDtypeStruct((S, FOX_W), BF16),
        jax.ShapeDtypeStruct((S, FOX_W), BF16),
        jax.ShapeDtypeStruct((S, SB_W), BF16), jax.ShapeDtypeStruct((S, SB_W), BF16), jax.ShapeDtypeStruct((S, SB_W), BF16),
        jax.ShapeDtypeStruct((S, POOL_W), BF16), jax.ShapeDtypeStruct((S, POOL_W), F32), jax.ShapeDtypeStruct((S, POOL_W), BF16),
    ]
    out_specs = [
        blk(FOX_W, 0), blk(FOX_W, 0), blk(FOX_W, 0), blk(FOX_W, 0),
        blk(SB_W, 0), blk(SB_W, 0), blk(SB_W, 0),
        blk(POOL_W, 0), blk(POOL_W, 0), blk(POOL_W, 0),
    ]
    return pl.pallas_call(
        body, name="prep", grid=(nb,),
        in_specs=[blk(FOX_W, C_FQ // FOX_W), blk(FOX_W, C_FK // FOX_W), blk(FOX_W, C_FV // FOX_W), blk(2 * POOL_W, C_PX // (2 * POOL_W)),
                  pl.BlockSpec((POOL_HALO, 2 * POOL_W), lambda i: (jnp.maximum(i * hb - 1, 0), C_PX // (2 * POOL_W))),
                  blk(LANES, 0),
                  blk(SB_W, C_SQ // SB_W), blk(SB_W, C_SK // SB_W), blk(SB_W, C_SV // SB_W),
                  full(qg), full(kg), full(bfp), full(wpd), full(ps)],
        out_specs=out_specs, out_shape=out_shapes,
        scratch_shapes=[pltpu.VMEM((1, LANES), F32), pltpu.VMEM((ts, LANES), F32), pltpu.VMEM((ts + POOL_HALO, POOL_W), F32)],
        compiler_params=_cparams(dimension_semantics=("arbitrary",)),
    )(projm, projm, projm, projm, projm, ffo, projm, projm, projm, qg, kg, bfp, wpd, ps)


def _pair_masks(x):
    ma = _iota((1, LANES), 1) < HEAD_DIM
    zero = jnp.zeros_like(x)
    return jnp.where(ma, x, zero), jnp.where(ma, zero, x)


DIAG_TILE = 256


def _diag_tiles(tq, size=DIAG_TILE):
    size = min(tq, size)
    return [(t * size, size) for t in range(tq // size)]


def _put_rows(old, new, r0):
    return new if r0 == 0 else jnp.concatenate([old[:r0], new], axis=0)


def _aug_queries(q):
    lane = _iota((1, LANES), 1)
    one = jnp.ones_like(q)
    zero = jnp.zeros_like(q)
    qa = jnp.where(lane < HEAD_DIM, q, jnp.where(lane < HEAD_DIM + AUG, one, zero))
    qb = jnp.where(lane >= HEAD_DIM, q, jnp.where(lane < AUG, one, zero))
    return qa, qb


EXP_DEAD = -105.0
PACK = 16


def _fox_walk_left(nfull, tk, block, carry, k_refs, qk_bound, row_floor):
    lane = _iota((1, LANES), 1)

    def alive(h, jj, c):
        k0 = pl.multiple_of(jnp.maximum(nfull - 1 - jj, 0) * tk + tk - PACK, PACK)
        last = k_refs[h][pl.ds(k0, PACK), :].astype(F32)
        lo = HEAD_DIM if h == 0 else 0
        negc = jnp.sum(jnp.where((lane >= lo) & (lane < lo + AUG), last, 0.0), axis=1, keepdims=True)
        return qk_bound + jnp.max(negc) - row_floor(c)[h] >= EXP_DEAD

    def walk(heads, jj0, c0):
        def go_on(state):
            jj, c = state
            ok = jj < nfull
            for h in heads:
                ok = ok & alive(h, jj, c)
            return ok

        def step(state):
            jj, c = state
            return jj + 1, block(pl.multiple_of((nfull - 1 - jj) * tk, tk), tk, 0, c, False, heads)

        return lax.while_loop(go_on, step, (jj0, c0))

    jj_pair, carry = walk((0, 1), jnp.int32(0), carry)
    carry = walk((0,), jj_pair, carry)[1]
    return walk((1,), jj_pair, carry)[1]


def _fox_fwd(qn, ka, kb, v, projm, qkb, *, tq, tk):
    S = qn.shape[0]
    npair = FOX_HEADS // 2

    def body(q_ref, ka_ref, kb_ref, v_ref, fg_ref, qkb_ref, o_ref, lse_ref, fm_ref):
        qi = pl.program_id(1)
        lane = _iota((1, LANES), 1)
        ma = lane < HEAD_DIM
        qaug = _aug_queries(q_ref[...])
        k_refs = (ka_ref, kb_ref)

        def block(k0, tkl, r0, carry, masked, heads=(0, 1)):
            vb = v_ref[pl.ds(k0, tkl), :]
            if masked:
                mask = (k0 + _iota((tq - r0, tkl), 1)) <= (qi * tq + r0 + _iota((tq - r0, tkl), 0))
            scores = {h: _dot_nt(qaug[h][r0:], k_refs[h][pl.ds(k0, tkl), :]) for h in heads}
            new = list(carry)
            for h in heads:
                m, l, acc = [x[r0:] for x in carry[h]]
                s = jnp.where(mask, scores[h], NEG) if masked else scores[h]
                m_new = jnp.maximum(m, jnp.max(s, axis=1, keepdims=True))
                alpha = jnp.exp(m - m_new)
                p = jnp.exp(s - m_new)
                sub = (m_new, alpha * l + jnp.sum(p, axis=1, keepdims=True), alpha * acc + _dot(p.astype(BF16), vb))
                new[h] = tuple(_put_rows(old, x, r0) for old, x in zip(carry[h], sub))
            return tuple(new)

        carry = tuple((jnp.full((tq, 1), NEG, F32), jnp.zeros((tq, 1), F32), jnp.zeros((tq, LANES), F32)) for _ in range(2))
        for off, size in _diag_tiles(tq, tq):
            carry = block(pl.multiple_of(qi * tq + off, size), size, off, carry, True)
        carry = _fox_walk_left((qi * tq) // tk, tk, block, carry, k_refs, jnp.max(qkb_ref[...]),
                               lambda c: (jnp.min(c[0][0]), jnp.min(c[1][0])))
        (ma_, la, acca), (mb_, lb, accb) = carry
        o = jnp.where(ma, acca / la, accb / lb)
        o_ref[...] = o
        lse_ref[...] = jnp.where(ma, ma_ + jnp.log(la), mb_ + jnp.log(lb))
        fg = fg_ref[...]
        fm_ref[...] = (o * (fg * _sigmoid(fg))).astype(BF16)

    qblk = pl.BlockSpec((tq, LANES), lambda p, i: (i, p))
    kvblk = pl.BlockSpec((S, LANES), lambda p, i: (0, p))
    return pl.pallas_call(
        body, name="fox_fwd", grid=(npair, S // tq),
        in_specs=[qblk, kvblk, kvblk, kvblk,
                  pl.BlockSpec((tq, LANES), lambda p, i: (i, C_FG // LANES + p)),
                  pl.BlockSpec((1, LANES), lambda p, i: (0, 0))],
        out_specs=[qblk, qblk, qblk],
        out_shape=[jax.ShapeDtypeStruct((S, FOX_W), F32), jax.ShapeDtypeStruct((S, FOX_W), F32), jax.ShapeDtypeStruct((S, FOX_W), BF16)],
        compiler_params=_cparams(dimension_semantics=("arbitrary", "arbitrary")),
    )(qn, ka, kb, v, projm, qkb)


def _suffix_sums(x, tmat2):
    return _dot(jnp.concatenate(_split2(x), axis=1), tmat2)


def _suffix_matrix(tk, inclusive):
    rr, cc = _iota((2 * tk, tk), 0) & (tk - 1), _iota((2 * tk, tk), 1)
    return _ones_where(rr >= cc) if inclusive else _ones_where(rr > cc)


def _sb_scores(qh, kb, causal, tmat2, r_runs):
    heads = range(2)
    zs = [_dot_nt(qh[h], kb) for h in heads]
    nsps = [jnp.minimum(-z, 0.0) - jnp.log(1.0 + jnp.exp(-jnp.abs(z))) for z in zs]
    lbs = nsps if causal is None else [jnp.where(causal, n, 0.0) for n in nsps]
    rins = [_suffix_sums(lb, tmat2) for lb in lbs]
    args = [zs[h] + lbs[h] + (rins[h] + r_runs[h]) for h in heads]
    a_s = [jnp.exp(arg if causal is None else jnp.where(causal, arg, NEG)) for arg in args]
    return zs, nsps, lbs, a_s


def _sb_walk_left(nfull, tk, block, carry, running_sums):
    def alive(state):
        jj, c = state
        ra, rb = running_sums(c)
        return (jj < nfull) & (jnp.max(jnp.maximum(ra, rb)) >= EXP_DEAD)

    def step(state):
        jj, c = state
        return jj + 1, block(pl.multiple_of((nfull - 1 - jj) * tk, tk), 0, c, False)

    return lax.while_loop(alive, step, (jnp.int32(0), carry))[1]


def _sb_fwd(sq, sk, sv, projm, *, tq, tk):
    S = sq.shape[0]
    npair = SB_HEADS // 2

    def body(q_ref, k_ref, v_ref, sg_ref, o_ref, sm_ref):
        qi = pl.program_id(1)
        lane = _iota((1, LANES), 1)
        ma = lane < HEAD_DIM
        qh = _pair_masks(q_ref[...])
        tmat2 = _suffix_matrix(tk, inclusive=False)
        nfull = (qi * tq) // tk

        def block(k0, r0, carry, masked):
            nr = tq - r0
            kb = k_ref[pl.ds(k0, tk), :]
            vb = v_ref[pl.ds(k0, tk), :]
            causal = (k0 + _iota((nr, tk), 1)) < (qi * tq + r0 + _iota((nr, tk), 0)) if masked else None
            _, _, lbs, a_s = _sb_scores([q[r0:] for q in qh], kb, causal, tmat2, [carry[h][0][r0:] for h in range(2)])
            pv = _dot(jnp.concatenate([a.astype(BF16) for a in a_s], axis=0), vb)
            return tuple((_put_rows(carry[h][0], carry[h][0][r0:] + jnp.sum(lbs[h], axis=1, keepdims=True), r0),
                          _put_rows(carry[h][1], carry[h][1][r0:] + pv[h * nr:(h + 1) * nr], r0)) for h in range(2))

        carry = tuple((jnp.zeros((tq, 1), F32), jnp.zeros((tq, LANES), F32)) for _ in range(2))
        for off, size in reversed(_diag_tiles(tq)):
            assert size == tk
            carry = block(pl.multiple_of(qi * tq + off, tk), off, carry, True)
        (_, acca), (_, accb) = _sb_walk_left(nfull, tk, block, carry, lambda c: (c[0][0], c[1][0]))
        o = jnp.where(ma, acca, accb)
        o_ref[...] = o
        sg = sg_ref[...]
        sm_ref[...] = (o * (sg * _sigmoid(sg))).astype(BF16)

    qblk = pl.BlockSpec((tq, LANES), lambda p, i: (i, p))
    kvblk = pl.BlockSpec((S, LANES), lambda p, i: (0, p))
    return pl.pallas_call(
        body, name="sb_fwd", grid=(npair, S // tq),
        in_specs=[qblk, kvblk, kvblk, pl.BlockSpec((tq, LANES), lambda p, i: (i, C_SG // LANES + p))],
        out_specs=[qblk, qblk],
        out_shape=[jax.ShapeDtypeStruct((S, SB_W), F32), jax.ShapeDtypeStruct((S, SB_W), BF16)],
        compiler_params=_cparams(dimension_semantics=("arbitrary", "arbitrary")),
    )(sq, sk, sv, projm)


def _outproj(x, fm, pm, sm, w_out, layer, *, tm):
    S, D = x.shape

    def body(x_ref, fm_ref, pm_ref, sm_ref, w_ref, y_ref):
        y = x_ref[...] + _dot(fm_ref[...], w_ref[0:FOX_W, :])
        y = y + _dot(pm_ref[...], w_ref[FOX_W:FOX_W + POOL_W, :])
        y_ref[...] = y + _dot(sm_ref[...], w_ref[FOX_W + POOL_W:D_MIX, :])

    row = lambda w: pl.BlockSpec((tm, w), lambda i: (i, 0))
    return pl.pallas_call(
        body, name="outproj", grid=(S // tm,),
        in_specs=[row(D), row(FOX_W), row(POOL_W), row(SB_W), pl.BlockSpec((None, D_MIX, D), lambda i: (layer, 0, 0))],
        out_specs=row(D), out_shape=jax.ShapeDtypeStruct((S, D), F32),
        compiler_params=_cparams(dimension_semantics=("arbitrary",)),
    )(x, fm, pm, sm, w_out)


def _loss_head(y, target, *, tm):
    S, D = y.shape

    def body(y_ref, t_ref, dy_ref, sq_ref):
        @pl.when(pl.program_id(0) == 0)
        def _():
            sq_ref[...] = jnp.zeros_like(sq_ref)

        d = y_ref[...] - t_ref[...]
        dy_ref[...] = d * (1.0 / D)
        sq_ref[...] += jnp.sum(d * d, axis=0, keepdims=True)

    row = pl.BlockSpec((tm, D), lambda i: (i, 0))
    return pl.pallas_call(
        body, name="loss_head", grid=(S // tm,),
        in_specs=[row, row], out_specs=[row, pl.BlockSpec((1, D), lambda i: (0, 0))],
        out_shape=[jax.ShapeDtypeStruct((S, D), F32), jax.ShapeDtypeStruct((1, D), F32)],
        compiler_params=_cparams(dimension_semantics=("arbitrary",)),
    )(y, target)


def _outproj_bwd(dy, fm, pm, sm, w_out, layer, stacks, *, tm):
    S, D = dy.shape

    def body(dy_ref, fm_ref, pm_ref, sm_ref, w_ref, dm_ref, dw_ref):
        @pl.when(pl.program_id(0) == 0)
        def _():
            dw_ref[...] = jnp.zeros_like(dw_ref)

        dyb = dy_ref[...].astype(BF16)
        dm_ref[...] = _dot_nt(dyb, w_ref[...])
        dw_ref[0:FOX_W, :] += _dot_tn(fm_ref[...], dyb)
        dw_ref[FOX_W:FOX_W + POOL_W, :] += _dot_tn(pm_ref[...], dyb)
        dw_ref[FOX_W + POOL_W:D_MIX, :] += _dot_tn(sm_ref[...], dyb)

    row = lambda w: pl.BlockSpec((tm, w), lambda i: (i, 0))
    wspec = pl.BlockSpec((None, D_MIX, D), lambda i: (layer, 0, 0))
    return _stack_call(
        body, "outproj_bwd", (S // tm,), [row(D), row(FOX_W), row(POOL_W), row(SB_W), wspec], (dy, fm, pm, sm, w_out),
        [pl.BlockSpec((None, D_MIX, D), lambda i: (layer, 0, 0))], [(D_MIX, D)], stacks,
        plain_specs=[row(D_MIX)], plain_shapes=[jax.ShapeDtypeStruct((S, D_MIX), F32)],
        compiler_params=_cparams(dimension_semantics=("arbitrary",)))


def _fox_bwd(qn, ka, kb, v, o, lse, dmix, projm, qkb, *, tq, tk):
    S = qn.shape[0]
    npair = FOX_HEADS // 2

    def body(q_ref, ka_ref, kb_ref, v_ref, o_ref, lse_ref, dm_ref, fg_ref, qkb_ref,
             dq_ref, dk_ref, dv_ref, dfg_ref, dct_ref, dcr_ref):
        qi = pl.program_id(1)

        @pl.when(qi == 0)
        def _():
            dk_ref[...] = jnp.zeros_like(dk_ref)
            dv_ref[...] = jnp.zeros_like(dv_ref)
            dct_ref[...] = jnp.zeros_like(dct_ref)

        lane = _iota((1, LANES), 1)
        ma = lane < HEAD_DIM
        qh = _pair_masks(q_ref[...])
        qaug = _aug_queries(q_ref[...])
        k_refs = (ka_ref, kb_ref)
        lsev = lse_ref[...]
        lse = (_lane_pick(lsev, lane, 0), _lane_pick(lsev, lane, HEAD_DIM))
        fg = fg_ref[...]
        silu, dsilu = _silu_pair(fg)
        dm = dm_ref[...]
        ov = o_ref[...]
        do = dm * silu
        dfg_ref[...] = dm * ov * dsilu
        dd = do * ov
        dsum = (jnp.sum(jnp.where(ma, dd, 0.0), axis=1, keepdims=True), jnp.sum(jnp.where(ma, 0.0, dd), axis=1, keepdims=True))
        doh = _pair_masks(do.astype(BF16))

        def block(k0, tkl, r0, carry, masked, heads=(0, 1)):
            vb = v_ref[pl.ds(k0, tkl), :]
            if masked:
                mask = (k0 + _iota((tq - r0, tkl), 1)) <= (qi * tq + r0 + _iota((tq - r0, tkl), 0))
            kaugs = {h: k_refs[h][pl.ds(k0, tkl), :] for h in heads}
            scores = {h: _dot_nt(qaug[h][r0:], kaugs[h]) for h in heads}
            dps = {h: _dot_nt(doh[h][r0:], vb) for h in heads}
            ps, dss = [], []
            rows = [carry[1], carry[2]]
            for h in heads:
                s = jnp.where(mask, scores[h], NEG) if masked else scores[h]
                p = jnp.exp(s - lse[h][r0:])
                dsf = p * (dps[h] - dsum[h][r0:])
                dct_ref[0, h:h + 1, pl.ds(k0, tkl)] -= jnp.sum(dsf, axis=0, keepdims=True)
                rows[h] = _put_rows(carry[1 + h], carry[1 + h][r0:] + jnp.sum(dsf, axis=1, keepdims=True), r0)
                ps.append(p.astype(BF16))
                dss.append(dsf.astype(BF16))
            dv_ref[pl.ds(k0, tkl), :] += _dot_tn(jnp.concatenate(ps, axis=0), jnp.concatenate([doh[h][r0:] for h in heads], axis=0))
            dk_ref[pl.ds(k0, tkl), :] += _dot_tn(jnp.concatenate(dss, axis=0), jnp.concatenate([qh[h][r0:] for h in heads], axis=0))
            kh = jnp.concatenate([_pair_masks(kaugs[h])[h] for h in heads], axis=0)
            dq = _put_rows(carry[0], carry[0][r0:] + _dot(jnp.concatenate(dss, axis=1), kh), r0)
            return (dq, rows[0], rows[1])

        zcol = jnp.zeros((tq, 1), F32)
        carry = (jnp.zeros((tq, LANES), F32), zcol, zcol)
        for off, size in _diag_tiles(tq):
            carry = block(pl.multiple_of(qi * tq + off, size), size, off, carry, True)
        floors = (jnp.min(lse[0]), jnp.min(lse[1]))
        dq, rowa, rowb = _fox_walk_left((qi * tq) // tk, tk, block, carry, k_refs, jnp.max(qkb_ref[...]), lambda c: floors)
        dq_ref[...] = dq * QK_SCALE
        dcr_ref[0] = jnp.where(ma, rowa, rowb)

    qblk = pl.BlockSpec((tq, LANES), lambda p, i: (i, p))
    kvblk = pl.BlockSpec((S, LANES), lambda p, i: (0, p))
    f32out = jax.ShapeDtypeStruct((S, FOX_W), F32)
    ctblk = pl.BlockSpec((1, FF_STRIDE, S), lambda p, i: (p, 0, 0))
    return pl.pallas_call(
        body, name="fox_bwd", grid=(npair, S // tq),
        in_specs=[qblk, kvblk, kvblk, kvblk, qblk, qblk, qblk,
                  pl.BlockSpec((tq, LANES), lambda p, i: (i, C_FG // LANES + p)),
                  pl.BlockSpec((1, LANES), lambda p, i: (0, 0))],
        out_specs=[qblk, kvblk, kvblk, qblk, ctblk, pl.BlockSpec((1, tq, LANES), lambda p, i: (p, i, 0))],
        out_shape=[f32out, f32out, f32out, f32out, jax.ShapeDtypeStruct((npair, FF_STRIDE, S), F32),
                   jax.ShapeDtypeStruct((npair, S, LANES), F32)],
        compiler_params=_cparams(dimension_semantics=("arbitrary", "arbitrary")),
    )(qn, ka, kb, v, o, lse, dmix, projm, qkb)


def _sb_bwd(sq, sk, sv, o, dmix, projm, *, tq, tk):
    S = sq.shape[0]
    npair = SB_HEADS // 2
    mix0 = (FOX_W + POOL_W) // LANES

    def body(q_ref, k_ref, v_ref, o_ref, dm_ref, sg_ref, dq_ref, dk_ref, dv_ref, dsg_ref):
        qi = pl.program_id(1)

        @pl.when(qi == 0)
        def _():
            dk_ref[...] = jnp.zeros_like(dk_ref)
            dv_ref[...] = jnp.zeros_like(dv_ref)

        lane = _iota((1, LANES), 1)
        ma = lane < HEAD_DIM
        qh = _pair_masks(q_ref[...])
        sg = sg_ref[...]
        silu, dsilu = _silu_pair(sg)
        dm = dm_ref[...]
        ov = o_ref[...]
        do = dm * silu
        dsg_ref[...] = dm * ov * dsilu
        dob = do.astype(BF16)
        dd = dob.astype(F32) * ov
        dsum = (jnp.sum(jnp.where(ma, dd, 0.0), axis=1, keepdims=True), jnp.sum(jnp.where(ma, 0.0, dd), axis=1, keepdims=True))
        doh = _pair_masks(dob)
        tmat2 = _suffix_matrix(tk, inclusive=False)
        tmat2_inc = _suffix_matrix(tk, inclusive=True)
        nfull = (qi * tq) // tk

        def block(k0, r0, carry, masked):
            nr = tq - r0
            kb = k_ref[pl.ds(k0, tk), :]
            vb = v_ref[pl.ds(k0, tk), :]
            kh = _pair_masks(kb)
            causal = (k0 + _iota((nr, tk), 1)) < (qi * tq + r0 + _iota((nr, tk), 0)) if masked else None
            heads = range(2)
            qs = [q[r0:] for q in qh]
            dos = [d[r0:] for d in doh]
            das = [_dot_nt(dos[h], vb) for h in heads]
            zs, nsps, lbs, a_s = _sb_scores(qs, kb, causal, tmat2, [carry[h][0][r0:] for h in heads])
            abs_ = [a.astype(BF16) for a in a_s]
            us = [abs_[h].astype(F32) * das[h] for h in heads]
            uins = [_suffix_sums(u, tmat2_inc) for u in us]
            dzs = []
            for h in heads:
                cum_u = dsum[h][r0:] - (uins[h] + carry[h][1][r0:])
                dz = us[h] * jnp.exp(nsps[h]) - jnp.exp(zs[h] + nsps[h]) * cum_u
                if masked:
                    dz = jnp.where(causal, dz, 0.0)
                dzs.append(dz.astype(BF16))
            dv_ref[pl.ds(k0, tk), :] += _dot_tn(jnp.concatenate(abs_, axis=0), jnp.concatenate(dos, axis=0))
            dk_ref[pl.ds(k0, tk), :] += _dot_tn(jnp.concatenate(dzs, axis=0), jnp.concatenate(qs, axis=0))
            dq = _put_rows(carry[2], carry[2][r0:] + _dot(jnp.concatenate(dzs, axis=1), jnp.concatenate(kh, axis=0)), r0)
            new = [(_put_rows(carry[h][0], carry[h][0][r0:] + jnp.sum(lbs[h], axis=1, keepdims=True), r0),
                    _put_rows(carry[h][1], carry[h][1][r0:] + jnp.sum(us[h], axis=1, keepdims=True), r0)) for h in heads]
            return (new[0], new[1], dq)

        zcol = jnp.zeros((tq, 1), F32)
        carry = ((zcol, zcol), (zcol, zcol), jnp.zeros((tq, LANES), F32))
        for off, size in reversed(_diag_tiles(tq)):
            assert size == tk
            carry = block(pl.multiple_of(qi * tq + off, tk), off, carry, True)
        dq = _sb_walk_left(nfull, tk, block, carry, lambda c: (c[0][0], c[1][0]))[2]
        dq_ref[...] = dq * QK_SCALE

    qblk = pl.BlockSpec((tq, LANES), lambda p, i: (i, p))
    kvblk = pl.BlockSpec((S, LANES), lambda p, i: (0, p))
    f32out = jax.ShapeDtypeStruct((S, SB_W), F32)
    return pl.pallas_call(
        body, name="sb_bwd", grid=(npair, S // tq),
        in_specs=[qblk, kvblk, kvblk, qblk,
                  pl.BlockSpec((tq, LANES), lambda p, i: (i, mix0 + p)),
                  pl.BlockSpec((tq, LANES), lambda p, i: (i, C_SG // LANES + p))],
        out_specs=[qblk, kvblk, kvblk, qblk],
        out_shape=[f32out, f32out, f32out, f32out],
        compiler_params=_cparams(dimension_semantics=("arbitrary", "arbitrary")),
    )(sq, sk, sv, o, dmix, projm)


def _prep_bwd(projm, ffo, dqn, dkn, dct, dcr, dv, dfg, dsq, dsk, dsv, dsg, dmix, pooled, yp, qg, kg, bfp, wpd, ps, *, ts):
    S = projm.shape[0]
    nb = S // ts
    hb = ts // POOL_HALO
    npair = FOX_HEADS // 2
    last_halo = S // POOL_HALO - 1

    def body(fq_ref, fk_ref, pp_ref, pph_ref, ff_ref,
             dqn_ref, dkn_ref, dct_ref, dcr_ref, dv_ref, dfg_ref, dsq_ref, dsk_ref, dsv_ref, dsg_ref,
             dmp_ref, dmh_ref, pooled_ref, yp_ref, qg_ref, kg_ref, bf_ref, wpd_ref, ps_ref,
             dp_ref, dqg_ref, dkg_ref, dbf_ref, dwp_ref, dps_ref,
             carry_ref, dl_ref, buf_ref, dct_s):
        i = pl.program_id(0)
        blk = nb - 1 - i

        @pl.when(i == 0)
        def _():
            carry_ref[...] = jnp.zeros_like(carry_ref)
            dqg_ref[...] = jnp.zeros_like(dqg_ref)
            dkg_ref[...] = jnp.zeros_like(dkg_ref)
            dbf_ref[...] = jnp.zeros_like(dbf_ref)
            dwp_ref[...] = jnp.zeros_like(dwp_ref)
            dps_ref[...] = jnp.zeros_like(dps_ref)

        bd = _head_blockdiag()
        for raw_ref, g_ref, dn, dg_ref, col in ((fq_ref, qg_ref, dqn_ref[...], dqg_ref, C_FQ), (fk_ref, kg_ref, dkn_ref[...], dkg_ref, C_FK)):
            q = raw_ref[...]
            rstd = lax.rsqrt(_group_sum(q * q, bd) * (1.0 / HEAD_DIM) + EPS)
            xhat = q * rstd
            dg_ref[...] += jnp.sum(dn * xhat, axis=0, keepdims=True)
            dyg = dn * g_ref[...]
            mean = _group_sum(dyg * xhat, bd) * (1.0 / HEAD_DIM)
            dp_ref[:, col:col + FOX_W] = (rstd * (dyg - xhat * mean)).astype(BF16)
        dp_ref[:, C_FV:C_FV + FOX_W] = dv_ref[...].astype(BF16)
        dp_ref[:, C_FG:C_FG + FOX_W] = dfg_ref[...].astype(BF16)
        dp_ref[:, C_SQ:C_SQ + SB_W] = dsq_ref[...].astype(BF16)
        dp_ref[:, C_SK:C_SK + SB_W] = dsk_ref[...].astype(BF16)
        dp_ref[:, C_SV:C_SV + SB_W] = dsv_ref[...].astype(BF16)
        dp_ref[:, C_SG:C_SG + SB_W] = dsg_ref[...].astype(BF16)

        dct_s[...] = jnp.zeros_like(dct_s)
        for p in range(npair):
            dct_s[FF_STRIDE * p:FF_STRIDE * (p + 1), :] = dct_ref[p]
        dc = dct_s[...].T
        lane = _iota((1, LANES), 1)
        for p in range(npair):
            dcr = dcr_ref[p]
            dc = dc + jnp.where(lane == FF_STRIDE * p, _lane_pick(dcr, lane, 0), 0.0)
            dc = dc + jnp.where(lane == FF_STRIDE * p + 1, _lane_pick(dcr, lane, HEAD_DIM), 0.0)
        triu = _ones_where(_iota((ts, ts), 1) >= _iota((ts, ts), 0))
        dlf = _dot_exact_lhs(triu, dc) + carry_ref[...]
        dl_ref[...] = dlf
        carry_ref[...] = dl_ref[0:1, :]
        z = ff_ref[...] + bf_ref[...]
        dff = dlf * (1.0 / (1.0 + jnp.exp(z)))
        dbf_ref[...] += jnp.sum(dff, axis=0, keepdims=True)
        dp_ref[:, PM:PW] = dff.astype(BF16)

        psv = ps_ref[...]
        wpdv = wpd_ref[...]
        lane_group = _iota((1, POOL_W), 1) >> 6
        wlen = _pool_group_select(lane_group, [float(w) for w in POOL_WINDOWS])
        pg = pp_ref[:, POOL_W:2 * POOL_W]
        silu, dsilu = _silu_pair(pg)
        dmp = dmp_ref[...]
        ypv = yp_ref[...]
        dp_ref[:, C_PG:C_PG + POOL_W] = (dmp * (ypv * psv) * dsilu).astype(BF16)
        dps_ref[...] += jnp.sum(dmp * silu * ypv, axis=0, keepdims=True)
        dyp = (dmp * psv * silu).astype(BF16)
        dwp_ref[...] += _dot_tn(pooled_ref[...], dyp)
        dpooled = _dot_nt(dyp, wpdv)
        pgh = pph_ref[:, POOL_W:2 * POOL_W]
        dyph = (dmh_ref[...] * psv * (pgh * _sigmoid(pgh))).astype(BF16)
        dpooled_h = jnp.where(blk < nb - 1, _dot_nt(dyph, wpdv), 0.0)
        tpos = (blk * ts + _iota((ts, 1), 0) + 1).astype(F32)
        ev = dpooled / jnp.minimum(tpos, wlen)
        buf_ref[0:ts, :] = ev
        buf_ref[ts:ts + POOL_HALO, :] = dpooled_h / wlen
        acc = ev
        snaps = []
        for d in range(1, POOL_HALO):
            acc = acc + buf_ref[pl.ds(d, ts), :]
            if d + 1 in POOL_WINDOWS:
                snaps.append(acc)
        dp_ref[:, C_PX:C_PX + POOL_W] = (_pool_group_select(lane_group, snaps) - dpooled).astype(BF16)

    rblk = lambda w, c: pl.BlockSpec((ts, w), lambda i: (nb - 1 - i, c))
    full = lambda a: pl.BlockSpec(a.shape, lambda i: (0,) * a.ndim)
    halo = lambda w, c: pl.BlockSpec((POOL_HALO, w), lambda i: (jnp.minimum((nb - i) * hb, last_halo), c))
    acc_spec = lambda r, w: pl.BlockSpec((r, w), lambda i: (0, 0))
    return pl.pallas_call(
        body, name="prep_bwd", grid=(nb,),
        in_specs=[rblk(FOX_W, C_FQ // FOX_W), rblk(FOX_W, C_FK // FOX_W), rblk(2 * POOL_W, C_PX // (2 * POOL_W)),
                  halo(2 * POOL_W, C_PX // (2 * POOL_W)), rblk(LANES, 0),
                  rblk(FOX_W, 0), rblk(FOX_W, 0), pl.BlockSpec((npair, FF_STRIDE, ts), lambda i: (0, 0, nb - 1 - i)),
                  pl.BlockSpec((npair, ts, LANES), lambda i: (0, nb - 1 - i, 0)), rblk(FOX_W, 0), rblk(FOX_W, 0),
                  rblk(SB_W, 0), rblk(SB_W, 0), rblk(SB_W, 0), rblk(SB_W, 0),
                  rblk(POOL_W, FOX_W // POOL_W), halo(POOL_W, FOX_W // POOL_W), rblk(POOL_W, 0), rblk(POOL_W, 0),
                  full(qg), full(kg), full(bfp), full(wpd), full(ps)],
        out_specs=[rblk(PW, 0), acc_spec(1, FOX_W), acc_spec(1, FOX_W), acc_spec(1, LANES), acc_spec(POOL_W, POOL_W), acc_spec(1, POOL_W)],
        out_shape=[jax.ShapeDtypeStruct((S, PW), BF16), jax.ShapeDtypeStruct((1, FOX_W), F32), jax.ShapeDtypeStruct((1, FOX_W), F32),
                   jax.ShapeDtypeStruct((1, LANES), F32), jax.ShapeDtypeStruct((POOL_W, POOL_W), F32), jax.ShapeDtypeStruct((1, POOL_W), F32)],
        scratch_shapes=[pltpu.VMEM((1, LANES), F32), pltpu.VMEM((ts, LANES), F32), pltpu.VMEM((ts + POOL_HALO, POOL_W), F32),
                        pltpu.VMEM((LANES, ts), F32)],
        compiler_params=_cparams(dimension_semantics=("arbitrary",)),
    )(projm, projm, projm, projm, ffo, dqn, dkn, dct, dcr, dv, dfg, dsq, dsk, dsv, dsg, dmix, dmix, pooled, yp, qg, kg, bfp, wpd, ps)


def _stack_call(body, name, grid, in_specs, operands, slot_specs, slot_shapes, stacks, plain_specs=(), plain_shapes=(), **kw):
    out_specs = list(plain_specs) + list(slot_specs)
    out_shape = list(plain_shapes) + [jax.ShapeDtypeStruct((DEPTH,) + s, F32) for s in slot_shapes]
    if stacks is None:
        return pl.pallas_call(body, name=name, grid=grid, in_specs=in_specs, out_specs=out_specs, out_shape=out_shape, **kw)(*operands)
    n = len(operands)

    def aliased_body(*refs):
        body(*refs[:n], *refs[n + len(stacks):])

    return pl.pallas_call(
        aliased_body, name=name, grid=grid, in_specs=list(in_specs) + [pl.BlockSpec(memory_space=pl.ANY)] * len(stacks),
        out_specs=out_specs, out_shape=out_shape,
        input_output_aliases={n + k: len(plain_specs) + k for k in range(len(stacks))}, **kw)(*operands, *stacks)


def _inproj_dw(h, dproj, layer, stacks, *, ts, tn):
    S, D = h.shape
    nj = PM // tn

    def body(h_ref, dp_ref, dpf_ref, dw_ref, dwf_ref):
        s = pl.program_id(1)

        @pl.when(s == 0)
        def _():
            dw_ref[...] = jnp.zeros_like(dw_ref)

        @pl.when((s == 0) & (pl.program_id(0) == 0))
        def _():
            dwf_ref[...] = jnp.zeros_like(dwf_ref)

        hv = h_ref[...]
        dw_ref[...] += _dot_tn(dp_ref[...], hv)

        @pl.when(pl.program_id(0) == 0)
        def _():
            dwf_ref[...] += _dot_tn(dpf_ref[...], hv)

    return _stack_call(
        body, "inproj_dw", (nj, S // ts),
        [pl.BlockSpec((ts, D), lambda j, s: (s, 0)),
         pl.BlockSpec((ts, tn), lambda j, s: (s, j)),
         pl.BlockSpec((ts, LANES), lambda j, s: (s, PM // LANES))],
        (h, dproj, dproj),
        [pl.BlockSpec((None, tn, D), lambda j, s: (layer, j, 0)), pl.BlockSpec((None, LANES, D), lambda j, s: (layer, 0, 0))],
        [(PM, D), (LANES, D)], stacks,
        compiler_params=_cparams(dimension_semantics=("arbitrary", "arbitrary")))


def _inproj_dx(dproj, wt_all, layer, x, g, dy, *, tm):
    S, D = x.shape

    def body(dp_ref, w_ref, x_ref, g_ref, dy_ref, dx_ref, dg_ref):
        @pl.when(pl.program_id(0) == 0)
        def _():
            dg_ref[...] = jnp.zeros_like(dg_ref)

        dh = _dot(dp_ref[...], w_ref[...])
        xf = x_ref[...]
        rstd = lax.rsqrt(jnp.mean(xf * xf, axis=-1, keepdims=True) + EPS)
        xhat = xf * rstd
        dg_ref[...] += jnp.sum(dh * xhat, axis=0, keepdims=True)
        dyg = dh * g_ref[...]
        mean = jnp.mean(dyg * xhat, axis=-1, keepdims=True)
        dx_ref[...] = rstd * (dyg - xhat * mean) + dy_ref[...]

    row = lambda w: pl.BlockSpec((tm, w), lambda i: (i, 0))
    return pl.pallas_call(
        body, name="inproj_dx", grid=(S // tm,),
        in_specs=[row(PW), pl.BlockSpec((None, PW, D), lambda i: (layer, 0, 0)), row(D), pl.BlockSpec((1, D), lambda i: (0, 0)), row(D)],
        out_specs=[row(D), pl.BlockSpec((1, D), lambda i: (0, 0))],
        out_shape=[jax.ShapeDtypeStruct((S, D), F32), jax.ShapeDtypeStruct((1, D), F32)],
        compiler_params=_cparams(dimension_semantics=("arbitrary",)),
    )(dproj, wt_all, x, g, dy)


def _adam_update(w, g, m, v):
    nm = ADAM_B1 * m + (1.0 - ADAM_B1) * g
    nv = ADAM_B2 * v + (1.0 - ADAM_B2) * (g * g)
    m_hat = nm / (1.0 - ADAM_B1 ** ADAM_STEP)
    v_hat = nv / (1.0 - ADAM_B2 ** ADAM_STEP)
    return -ADAM_LR * (m_hat / (jnp.sqrt(v_hat) + ADAM_EPS) + ADAM_WD * w), nm, nv


def _adamw(w, g, m, v):
    L, R, C = w.shape
    tr = R if R <= 512 else 256

    def body(w_ref, g_ref, m_ref, v_ref, d_ref, nm_ref, nv_ref):
        d_ref[...], nm_ref[...], nv_ref[...] = _adam_update(w_ref[...], g_ref[...], m_ref[...], v_ref[...])

    spec = pl.BlockSpec((1, tr, C), lambda l, i: (l, i, 0))
    shp = jax.ShapeDtypeStruct((L, R, C), F32)
    return pl.pallas_call(
        body, name="adamw", grid=(L, R // tr), in_specs=[spec] * 4, out_specs=[spec] * 3, out_shape=[shp] * 3,
        compiler_params=_cparams(dimension_semantics=("arbitrary", "arbitrary")),
    )(w, g, m, v)


def _adamw_nd(w, g, m, v):
    shape = w.shape
    view = (1,) + shape if w.ndim == 2 else (shape[0], -1, shape[-1])
    outs = _adamw(w.reshape(view), g.reshape(view), m.reshape(view), v.reshape(view))
    return tuple(o.reshape(shape) for o in outs)


FLIP_C = (0, 0, 1)
FLIP_X = (1, 0, 0)
FLIP_Y = (0, 1, 0)
FLIP_XY = (1, 1, 0)
MESH = pl.DeviceIdType.MESH


def _peer(flip):
    me = (lax.axis_index("x"), lax.axis_index("y"), lax.axis_index("c"))
    return tuple(1 - a if f else a for a, f in zip(me, flip))


def _exchange(name, arrays, flips):
    n = len(arrays)

    def body(*refs):
        srcs, dsts = refs[:n], refs[n:2 * n]
        send_sems, recv_sems = refs[2 * n:]
        copies = [pltpu.make_async_remote_copy(src_ref=srcs[k], dst_ref=dsts[k], send_sem=send_sems.at[k], recv_sem=recv_sems.at[k],
                                               device_id=_peer(flips[k]), device_id_type=MESH) for k in range(n)]
        for cp in copies:
            cp.start()
        for cp in copies:
            cp.wait()

    anyspec = pl.BlockSpec(memory_space=pl.ANY)
    return pl.pallas_call(
        body, name=name, in_specs=[anyspec] * n, out_specs=[anyspec] * n,
        out_shape=[jax.ShapeDtypeStruct(a.shape, a.dtype) for a in arrays],
        scratch_shapes=[pltpu.SemaphoreType.DMA((n,)), pltpu.SemaphoreType.DMA((n,))],
    )(*arrays)


def _exchange_add(name, x, flip):
    def body(x_ref, o_ref, buf_ref, send_sem, recv_sem):
        cp = pltpu.make_async_remote_copy(src_ref=x_ref, dst_ref=buf_ref, send_sem=send_sem, recv_sem=recv_sem,
                                          device_id=_peer(flip), device_id_type=MESH)
        cp.start()
        cp.wait()
        o_ref[...] = x_ref[...] + buf_ref[...]

    vspec = pl.BlockSpec(memory_space=pltpu.VMEM)
    return pl.pallas_call(
        body, name=name, in_specs=[vspec], out_specs=vspec, out_shape=jax.ShapeDtypeStruct(x.shape, x.dtype),
        scratch_shapes=[pltpu.VMEM(x.shape, x.dtype), pltpu.SemaphoreType.DMA, pltpu.SemaphoreType.DMA],
    )(x)


def _chip_index():
    return 2 * lax.axis_index("x") + lax.axis_index("y")


def _gather_weights(w_in_t, w_out):
    wi = w_in_t.astype(BF16)
    wo = jnp.swapaxes(w_out, 0, 1).astype(BF16)
    halves = (wi.shape[0] // 2, wo.shape[0] // 2)
    ARR = 2
    TO_X, TO_Y, ON_Y, ON_X, SIB_X, SIB_Y, SIB_D0, SIB_D1, OWN = [ARR * k for k in range(9)]
    n_sems = ARR * 9

    def body(wi_ref, wo_ref, gi_ref, go_ref, send_sems, recv_sems):
        c = lax.axis_index("c")
        j = _chip_index()
        srcs = (wi_ref, wo_ref)
        dsts = (gi_ref, go_ref)
        def cuts(core):
            return [(pl.ds(h * core, h), pl.ds(h * core, h // 2), pl.ds(h * core + h // 2, h - h // 2)) for h in halves]
        mine, theirs = cuts(c), cuts(1 - c)
        HALF, Q0, Q1 = 0, 1, 2

        def copy(idx, src, dst, flip):
            return pltpu.make_async_remote_copy(src_ref=src, dst_ref=dst, send_sem=send_sems.at[idx], recv_sem=recv_sems.at[idx],
                                                device_id=_peer(flip), device_id_type=MESH)

        def slot(a, shard, cut):
            return dsts[a].at[shard, cut]

        jx, jy, jd = j ^ 2, j ^ 1, j ^ 3
        sends = []

        def start(cp):
            cp.start()
            sends.append(cp)

        for a in range(ARR):
            start(copy(TO_X + a, srcs[a].at[mine[a][HALF]], slot(a, j, mine[a][HALF]), FLIP_X))
            start(copy(TO_Y + a, srcs[a].at[mine[a][HALF]], slot(a, j, mine[a][HALF]), FLIP_Y))
        own = [copy(OWN + a, srcs[a], dsts[a].at[j], FLIP_C) for a in range(ARR)]
        for cp in own:
            cp.start()
        for a in range(ARR):
            copy(TO_X + a, slot(a, jx, mine[a][HALF]), slot(a, jx, mine[a][HALF]), FLIP_X).wait_recv()
            start(copy(ON_Y + a, slot(a, jx, mine[a][Q0]), slot(a, jx, mine[a][Q0]), FLIP_Y))
            start(copy(SIB_X + a, slot(a, jx, mine[a][HALF]), slot(a, jx, mine[a][HALF]), FLIP_C))
        for a in range(ARR):
            copy(TO_Y + a, slot(a, jy, mine[a][HALF]), slot(a, jy, mine[a][HALF]), FLIP_Y).wait_recv()
            start(copy(ON_X + a, slot(a, jy, mine[a][Q1]), slot(a, jy, mine[a][Q1]), FLIP_X))
            start(copy(SIB_Y + a, slot(a, jy, mine[a][HALF]), slot(a, jy, mine[a][HALF]), FLIP_C))
        for a in range(ARR):
            copy(ON_Y + a, slot(a, jd, mine[a][Q0]), slot(a, jd, mine[a][Q0]), FLIP_Y).wait_recv()
            start(copy(SIB_D0 + a, slot(a, jd, mine[a][Q0]), slot(a, jd, mine[a][Q0]), FLIP_C))
        for a in range(ARR):
            copy(ON_X + a, slot(a, jd, mine[a][Q1]), slot(a, jd, mine[a][Q1]), FLIP_X).wait_recv()
            start(copy(SIB_D1 + a, slot(a, jd, mine[a][Q1]), slot(a, jd, mine[a][Q1]), FLIP_C))
        for a in range(ARR):
            for idx, shard, cut in ((SIB_X, jx, HALF), (SIB_Y, jy, HALF), (SIB_D0, jd, Q0), (SIB_D1, jd, Q1)):
                copy(idx + a, slot(a, shard, theirs[a][cut]), slot(a, shard, theirs[a][cut]), FLIP_C).wait_recv()
        for cp in own:
            cp.wait()
        for cp in sends:
            cp.wait_send()

    anyspec = pl.BlockSpec(memory_space=pl.ANY)
    gi, go = pl.pallas_call(
        body, name="gather_weights", in_specs=[anyspec] * 2, out_specs=[anyspec] * 2,
        out_shape=[jax.ShapeDtypeStruct((4,) + wi.shape, BF16), jax.ShapeDtypeStruct((4,) + wo.shape, BF16)],
        scratch_shapes=[pltpu.SemaphoreType.DMA((n_sems,)), pltpu.SemaphoreType.DMA((n_sems,))],
    )(wi, wo)
    w_in_t_full = gi.reshape((4 * wi.shape[0],) + wi.shape[1:])
    w_out_full = jnp.swapaxes(go.reshape((4 * wo.shape[0],) + wo.shape[1:]), 0, 1)
    return w_in_t_full, w_out_full


def _to_aligned(w_t):
    _, L, D = w_t.shape
    npair = FOX_HEADS // 2
    ff = w_t[ORIG_FF:ORIG_REST].reshape(npair, 2, L, D)
    ff = jnp.pad(ff, ((0, 0), (0, FF_STRIDE - 2), (0, 0), (0, 0))).reshape(npair * FF_STRIDE, L, D)
    ff = jnp.pad(ff, ((0, LANES - npair * FF_STRIDE), (0, 0), (0, 0)))
    return jnp.swapaxes(jnp.concatenate([w_t[:ORIG_FOX], w_t[ORIG_REST:], ff], axis=0), 0, 1)


def _from_aligned(dw_t):
    n, _, D = dw_t.shape
    npair = FOX_HEADS // 2
    ff = dw_t[:, PM:PM + npair * FF_STRIDE].reshape(n, npair, FF_STRIDE, D)[:, :, :2].reshape(n, FOX_HEADS, D)
    return jnp.swapaxes(jnp.concatenate([dw_t[:, :ORIG_FOX], ff, dw_t[:, ORIG_FOX:PM]], axis=1), 0, 1)


def _half_layers(name, stack, got):
    L, R, C = stack.shape
    half = L // 2
    tr = min(256, R)
    c = lax.axis_index("c")
    which = ((1 - c) if got is None else c).astype(jnp.int32).reshape(1)

    def body(c_ref, x_ref, *refs):
        if got is None:
            refs[0][...] = x_ref[...].astype(BF16)
        else:
            acc = x_ref[...] + refs[0][...].astype(F32)
            refs[1][...] = acc
            refs[2][...] = acc.astype(BF16)

    plain = pl.BlockSpec((1, tr, C), lambda l, i, c_ref: (l, i, 0))
    picked = pl.BlockSpec((1, tr, C), lambda l, i, c_ref: (c_ref[0] * half + l, i, 0))
    shp = lambda dt: jax.ShapeDtypeStruct((half, R, C), dt)
    grid_spec = pltpu.PrefetchScalarGridSpec(
        num_scalar_prefetch=1, grid=(half, R // tr),
        in_specs=[picked] + ([] if got is None else [plain]), out_specs=[plain] if got is None else [plain, plain])
    return pl.pallas_call(
        body, name=name, grid_spec=grid_spec, out_shape=[shp(BF16)] if got is None else [shp(F32), shp(BF16)],
        compiler_params=_cparams(dimension_semantics=("arbitrary", "arbitrary")),
    )(which, stack, *([] if got is None else [got]))


def _reduce_scatter(stack_m, stack_f, stack_o, shard_cols, shard_rows):
    j = _chip_index()
    half = DEPTH // 2
    stacks = (stack_m, stack_f, stack_o)
    give = [_half_layers("rs_give", s, None)[0] for s in stacks]
    got = _exchange("rs_d2d", give, (FLIP_C,) * len(stacks))
    (m32, mbf), (f32_, fbf), (o32, obf) = [_half_layers("rs_add_chip", s, g) for s, g in zip(stacks, got)]
    d_model = stack_m.shape[2]

    def in_shards(m, f):
        return _from_aligned(jnp.concatenate([m, f], axis=1)).reshape(4, shard_cols, half, d_model)

    def out_shards(o):
        return jnp.moveaxis(o.reshape(half, 4, shard_rows, o.shape[-1]), 1, 0)

    chip = [(in_shards(m32, f32_), in_shards(mbf, fbf), 0), (out_shards(o32), out_shards(obf), 1)]
    shard = lambda a, idx: lax.dynamic_index_in_dim(a, idx, axis=0, keepdims=False)
    via = []
    for _, bf, axis in chip:
        diag = shard(bf, j ^ 3)
        cut = diag.shape[axis] // 2
        via += [lax.slice_in_dim(diag, 0, cut, axis=axis), lax.slice_in_dim(diag, cut, 2 * cut, axis=axis)]
    handed = _exchange("rs_via", via, (FLIP_X, FLIP_Y) * len(chip))
    sends = []
    for a, (f32_sum, _, axis) in enumerate(chip):
        sends.append(_add_half_along("rs_add_via", shard(f32_sum, j ^ 2), handed[2 * a + 1], axis, 1))
        sends.append(_add_half_along("rs_add_via", shard(f32_sum, j ^ 1), handed[2 * a], axis, 0))
    got = _exchange("rs_ici", sends, (FLIP_X, FLIP_Y) * len(chip))
    own_in, own_out = [shard(f32_sum, j) for f32_sum, _, _ in chip]
    mine_in = _add_rows("rs_add_in", own_in, list(got[0:2]))
    mine_out = _add_into_half("rs_add_out", own_out, list(got[2:4]))
    sib_in, g_out = _share_halves(mine_in, mine_out)
    return (mine_in, sib_in), g_out


def _add_half_along(name, base, extra, axis, which):
    lanes = min(ROW_LANE_CHUNK, base.shape[2])
    assert base.shape[axis] == 2 * extra.shape[axis]
    blk = tuple(base.shape[d] // 2 if d == axis else base.shape[d] for d in range(2)) + (lanes,)

    def body(b_ref, e_ref, o_ref):
        x = b_ref[...]
        o_ref[...] = jnp.where(pl.program_id(0) == which, x + e_ref[...].astype(F32), x).astype(BF16)

    at = lambda i, k: (i, 0, k) if axis == 0 else (0, i, k)
    return pl.pallas_call(
        body, name=name, grid=(2, base.shape[2] // lanes),
        in_specs=[pl.BlockSpec(blk, at), pl.BlockSpec(blk, lambda i, k: (0, 0, k))], out_specs=pl.BlockSpec(blk, at),
        out_shape=jax.ShapeDtypeStruct(base.shape, BF16),
        compiler_params=_cparams(dimension_semantics=("arbitrary", "arbitrary")),
    )(base, extra)


def _add_rows(name, first, others):
    n = len(others)

    def body(*refs):
        acc = refs[0][...]
        for r in refs[1:1 + n]:
            acc = acc + r[...].astype(F32)
        refs[1 + n][...] = acc

    grid, spec = _row_lane_blocks(first.shape)
    return pl.pallas_call(
        body, name=name, grid=grid, in_specs=[spec(first.shape[1])] * (1 + n), out_specs=spec(first.shape[1]),
        out_shape=jax.ShapeDtypeStruct(first.shape, F32),
        compiler_params=_cparams(dimension_semantics=("arbitrary", "arbitrary")),
    )(first, *others)


ROW_LANE_CHUNK = 256


def _row_lane_blocks(shape):
    rows, _, C = shape
    tr = rows // 2 if rows % 2 == 0 and rows > 64 else rows
    lanes = min(ROW_LANE_CHUNK, C)
    return (rows // tr, C // lanes), lambda n_mid: pl.BlockSpec((tr, n_mid, lanes), lambda i, k, *_: (i, 0, k))


def _add_into_half(name, first, others):
    half, rows, C = first.shape
    tr = min(256, rows)
    n = len(others)

    def body(c_ref, *refs):
        acc = refs[0][...]
        for r in refs[1:1 + n]:
            acc = acc + r[...].astype(F32)
        refs[1 + n][...] = acc

    grid_spec = pltpu.PrefetchScalarGridSpec(
        num_scalar_prefetch=1, grid=(half, rows // tr),
        in_specs=[pl.BlockSpec((1, tr, C), lambda l, i, c_ref: (l, i, 0))] * (1 + n),
        out_specs=pl.BlockSpec((1, tr, C), lambda l, i, c_ref: (c_ref[0] * half + l, i, 0)))
    return pl.pallas_call(
        body, name=name, grid_spec=grid_spec, out_shape=jax.ShapeDtypeStruct((2 * half, rows, C), F32),
        compiler_params=_cparams(dimension_semantics=("arbitrary", "arbitrary")),
    )(lax.axis_index("c").astype(jnp.int32).reshape(1), first, *others)


def _share_halves(mine, buf):
    half = DEPTH // 2

    def body(mine_ref, buf_in, sib_ref, buf_ref, send_sems, recv_sems):
        lay = pl.ds(half * lax.axis_index("c"), half)
        copies = [pltpu.make_async_remote_copy(src_ref=src, dst_ref=dst, send_sem=send_sems.at[k], recv_sem=recv_sems.at[k],
                                               device_id=_peer(FLIP_C), device_id_type=MESH)
                  for k, (src, dst) in enumerate(((mine_ref, sib_ref), (buf_ref.at[lay], buf_ref.at[lay])))]
        for cp in copies:
            cp.start()
        for cp in copies:
            cp.wait()

    anyspec = pl.BlockSpec(memory_space=pl.ANY)
    return pl.pallas_call(
        body, name="rs_share", in_specs=[anyspec] * 2, out_specs=[anyspec] * 2,
        out_shape=[jax.ShapeDtypeStruct(mine.shape, mine.dtype), jax.ShapeDtypeStruct(buf.shape, buf.dtype)],
        input_output_aliases={1: 1},
        scratch_shapes=[pltpu.SemaphoreType.DMA((2,)), pltpu.SemaphoreType.DMA((2,))],
    )(mine, buf)


def _adamw_halves(w, g_mine, g_sib, m, v):
    half = g_mine.shape[1]

    def body(c_ref, w_ref, gm_ref, gs_ref, m_ref, v_ref, g_ref, d_ref, nm_ref, nv_ref):
        first = c_ref[0] == 0
        gm, gs = gm_ref[...], gs_ref[...]
        for h, gv in enumerate((jnp.where(first, gm, gs), jnp.where(first, gs, gm))):
            lay = slice(half * h, half * (h + 1))
            g_ref[:, lay, :] = gv
            d_ref[:, lay, :], nm_ref[:, lay, :], nv_ref[:, lay, :] = _adam_update(w_ref[:, lay, :], gv, m_ref[:, lay, :], v_ref[:, lay, :])

    grid, spec = _row_lane_blocks(w.shape)
    full, part = spec(w.shape[1]), spec(half)
    grid_spec = pltpu.PrefetchScalarGridSpec(num_scalar_prefetch=1, grid=grid, in_specs=[full, part, part, full, full], out_specs=[full] * 4)
    return pl.pallas_call(
        body, name="adamw_halves", grid_spec=grid_spec, out_shape=[jax.ShapeDtypeStruct(w.shape, F32)] * 4,
        compiler_params=_cparams(dimension_semantics=("arbitrary", "arbitrary")),
    )(lax.axis_index("c").astype(jnp.int32).reshape(1), w, g_mine, g_sib, m, v)


def _all_reduce_small(x):
    x = _exchange_add("ar_c", x, FLIP_C)
    x = _exchange_add("ar_y", x, FLIP_Y)
    return _exchange_add("ar_x", x, FLIP_X)


def _blocks(S):
    return dict(tm=min(512, S), tm_proj=min(1024, S), ts=min(512, S), tq=min(512, S), tq_big=min(1024, S), tk=min(512, S), tks=min(256, S))


def _pair_pad(vec):
    npair = FOX_HEADS // 2
    v = jnp.pad(vec.reshape(npair, 2), ((0, 0), (0, FF_STRIDE - 2))).reshape(1, npair * FF_STRIDE)
    return jnp.pad(v, ((0, 0), (0, LANES - npair * FF_STRIDE)))


def _pair_unpad(row):
    npair = FOX_HEADS // 2
    return row[0, :npair * FF_STRIDE].reshape(npair, FF_STRIDE)[:, :2].reshape(FOX_HEADS)


def _pool_blockdiag(w_pool):
    g, cg, _ = w_pool.shape
    eye = jnp.eye(g, dtype=w_pool.dtype)
    return jnp.einsum("gh,gcd->gchd", eye, w_pool).reshape(g * cg, g * cg)


QK_BOUND_SLACK = 1.05


def _layer_params(norm_g, b_f, q_norm_g, k_norm_g, w_pool, pool_scale):
    qk_bound = QK_BOUND_SLACK * HEAD_DIM * QK_SCALE * jnp.max(jnp.abs(q_norm_g)) * jnp.max(jnp.abs(k_norm_g))
    return dict(g=norm_g.reshape(1, -1), qg=jnp.tile(q_norm_g, FOX_HEADS).reshape(1, FOX_W), kg=jnp.tile(k_norm_g, FOX_HEADS).reshape(1, FOX_W),
                bfp=_pair_pad(b_f), wpd=_pool_blockdiag(w_pool).astype(BF16), ps=pool_scale.reshape(1, POOL_W),
                qkb=jnp.full((1, LANES), qk_bound, F32))


def _layer_fwd(x, wt_all, w_out, layer, prm, bs):
    projm, ffo, h = _inproj(x, prm["g"], wt_all, layer, tm=bs["tm_proj"], tn=PROJ_TN)
    qn, ka, kb, v, sq, sk, sv, pooled, yp, pm = _prep(projm, ffo, prm["qg"], prm["kg"], prm["bfp"], prm["wpd"], prm["ps"], ts=bs["ts"])
    o, lse, fm = _fox_fwd(qn, ka, kb, v, projm, prm["qkb"], tq=bs["tq"], tk=bs["tk"])
    so, sm = _sb_fwd(sq, sk, sv, projm, tq=bs["tq_big"], tk=bs["tks"])
    y = _outproj(x, fm, pm, sm, w_out, layer, tm=bs["tm"])
    saved = dict(x=x, projm=projm, ffo=ffo, h=h, qn=qn, ka=ka, kb=kb, v=v, sq=sq, sk=sk, sv=sv, pooled=pooled, yp=yp,
                 o=o, lse=lse, so=so, fm=fm, pm=pm, sm=sm)
    return y, saved


def _layer_bwd(dy, wt_all, w_out, prm, sv_, bs, layer, stacks):
    dmix, stack_o = _outproj_bwd(dy, sv_["fm"], sv_["pm"], sv_["sm"], w_out, layer, None if stacks is None else stacks[2:], tm=bs["tm"])
    dqn, dkn, dv, dfg, dct, dcr = _fox_bwd(sv_["qn"], sv_["ka"], sv_["kb"], sv_["v"], sv_["o"], sv_["lse"], dmix, sv_["projm"],
                                      prm["qkb"], tq=bs["tq"], tk=bs["tk"])
    dsq, dsk, dsv, dsg = _sb_bwd(sv_["sq"], sv_["sk"], sv_["sv"], sv_["so"], dmix, sv_["projm"], tq=bs["tq"], tk=bs["tks"])
    dproj, dqg, dkg, dbf, dwp, dps = _prep_bwd(sv_["projm"], sv_["ffo"], dqn, dkn, dct, dcr, dv, dfg, dsq, dsk, dsv, dsg, dmix,
                                               sv_["pooled"], sv_["yp"], prm["qg"], prm["kg"], prm["bfp"], prm["wpd"], prm["ps"], ts=bs["ts"])
    stack_m, stack_f = _inproj_dw(sv_["h"], dproj, layer, None if stacks is None else stacks[:2], ts=bs["tm_proj"], tn=PROJ_TN)
    dx, dg = _inproj_dx(dproj, wt_all, layer, sv_["x"], prm["g"], dy, tm=min(256, bs["tm"]))
    grads = dict(
        norm_g=dg[0],
        b_f=_pair_unpad(dbf), q_norm_g=dqg.reshape(FOX_HEADS, HEAD_DIM).sum(0), k_norm_g=dkg.reshape(FOX_HEADS, HEAD_DIM).sum(0),
        w_pool=jnp.stack([dwp[HEAD_DIM * g:HEAD_DIM * (g + 1), HEAD_DIM * g:HEAD_DIM * (g + 1)] for g in range(4)]),
        pool_scale=dps[0])
    return dx, grads, (stack_m, stack_f, stack_o)


def _local_step(x, target, wt_all, w_out, norm_g, b_f, q_norm_g, k_norm_g, w_pool, pool_scale):
    S, D = x.shape
    bs = _blocks(S)
    prms = [_layer_params(norm_g[l], b_f[l], q_norm_g[l], k_norm_g[l], w_pool[l], pool_scale[l]) for l in range(DEPTH)]
    saved = []
    y = x
    for l in range(DEPTH):
        y, s_ = _layer_fwd(y, wt_all, w_out, l, prms[l], bs)
        saved.append(s_)
    dy, sq = _loss_head(y, target, tm=bs["tm"])
    loss = 0.5 * jnp.sum(sq) / D
    grads = [None] * DEPTH
    stacks = None
    for l in reversed(range(DEPTH)):
        dy, grads[l], stacks = _layer_bwd(dy, wt_all, w_out, prms[l], saved[l], bs, l, stacks)
    stacked = {k: jnp.stack([g[k] for g in grads]) for k in grads[0]}
    return loss, dy, stacked, stacks


SMALL = ("norm_g", "b_f", "q_norm_g", "k_norm_g", "w_pool", "pool_scale")


def _pack_small(gr):
    flat = jnp.concatenate([gr[k].reshape(-1) for k in SMALL])
    pad = (-flat.shape[0]) % (8 * LANES)
    return jnp.pad(flat, (0, pad)).reshape(-1, LANES)


def _unpack_small(packed, like):
    flat = packed.reshape(-1)
    out, off = {}, 0
    for k in SMALL:
        n = like[k].size
        out[k] = flat[off:off + n].reshape(like[k].shape)
        off += n
    return out


def kernel(x, norm_g, w_in, b_f, q_norm_g, k_norm_g, w_pool, pool_scale, w_out, loss_target, m_norm_g, m_w_in, m_b_f, m_q_norm_g, m_k_norm_g, m_w_pool, m_pool_scale, m_w_out, v_norm_g, v_w_in, v_b_f, v_q_norm_g, v_k_norm_g, v_w_pool, v_pool_scale, v_w_out):
    weights = dict(norm_g=norm_g, w_in=w_in, b_f=b_f, q_norm_g=q_norm_g, k_norm_g=k_norm_g, w_pool=w_pool, pool_scale=pool_scale, w_out=w_out)
    mom_m = dict(norm_g=m_norm_g, w_in=m_w_in, b_f=m_b_f, q_norm_g=m_q_norm_g, k_norm_g=m_k_norm_g, w_pool=m_w_pool, pool_scale=m_pool_scale, w_out=m_w_out)
    mom_v = dict(norm_g=v_norm_g, w_in=v_w_in, b_f=v_b_f, q_norm_g=v_q_norm_g, k_norm_g=v_k_norm_g, w_pool=v_w_pool, pool_scale=v_pool_scale, w_out=v_w_out)
    shard_cols = w_in.shape[2]
    shard_rows = w_out.shape[1]

    cols_first = lambda a: jnp.transpose(a, (2, 0, 1))
    w_in_t = cols_first(w_in)
    w_in_t_full, w_out_full = _gather_weights(w_in_t, w_out)
    wt_all = _to_aligned(w_in_t_full)
    loss, dx, gr, stacks = _local_step(x[0], loss_target[0], wt_all, w_out_full, norm_g, b_f, q_norm_g, k_norm_g, w_pool, pool_scale)
    loss = lax.psum(loss, ("x", "y", "c"))

    (g_in_mine, g_in_sib), g_w_out = _reduce_scatter(*stacks, shard_cols, shard_rows)
    small = _unpack_small(_all_reduce_small(_pack_small(gr)), {k: weights[k] for k in SMALL})
    grad_w = dict(small, w_out=g_w_out)

    names = ("norm_g", "w_in", "b_f", "q_norm_g", "k_norm_g", "w_pool", "pool_scale", "w_out")
    upd = {k: _adamw_nd(weights[k], grad_w[k], mom_m[k], mom_v[k]) for k in names if k != "w_in"}
    in_t = _adamw_halves(w_in_t, g_in_mine, g_in_sib, cols_first(mom_m["w_in"]), cols_first(mom_v["w_in"]))
    grad_w["w_in"], *upd["w_in"] = [jnp.transpose(a, (1, 2, 0)) for a in in_t]
    return (loss, dx[None], *[grad_w[k] for k in names], *[upd[k][0] for k in names], *[upd[k][1] for k in names], *[upd[k][2] for k in names])
```

```python
import functools

import jax
import jax.numpy as jnp
from jax import lax
from jax.experimental import pallas as pl
from jax.experimental.pallas import tpu as pltpu

F32 = jnp.float32
BF16 = jnp.bfloat16

DEPTH = 4
HEAD_DIM = 64
FOX_HEADS = 8
SB_HEADS = 4
FOX_W = FOX_HEADS * HEAD_DIM
SB_W = SB_HEADS * HEAD_DIM
POOL_W = 256
POOL_WINDOWS = (2, 4, 8, 16)
POOL_HALO = 16
D_MIX = FOX_W + POOL_W + SB_W
EPS = 1e-6
NEG = -1e30
QK_SCALE = HEAD_DIM ** -0.5

ORIG_FOX = 4 * FOX_W
ORIG_FF = ORIG_FOX
ORIG_REST = ORIG_FF + FOX_HEADS
D_IN = ORIG_REST + 2 * POOL_W + 4 * SB_W

C_FQ, C_FK, C_FV, C_FG = 0, FOX_W, 2 * FOX_W, 3 * FOX_W
C_PX = 4 * FOX_W
C_PG = C_PX + POOL_W
C_SQ = C_PG + POOL_W
C_SK, C_SV, C_SG = C_SQ + SB_W, C_SQ + 2 * SB_W, C_SQ + 3 * SB_W
PM = C_SG + SB_W
LANES = 128
PW = PM + LANES
FF_STRIDE = 8
AUG = 3

ADAM_LR = 0.001
ADAM_B1 = 0.9
ADAM_B2 = 0.999
ADAM_EPS = 1e-08
ADAM_WD = 0.01
ADAM_STEP = 10

VMEM_LIMIT = 48 * 1024 * 1024
PROJ_TN = PM // 2


def _cparams(**kw):
    return pltpu.CompilerParams(vmem_limit_bytes=VMEM_LIMIT, **kw)


def _dot(a, b):
    return jnp.dot(a, b, preferred_element_type=F32)


def _dot_nt(a, b):
    return lax.dot_general(a, b, (((1,), (1,)), ((), ())), preferred_element_type=F32)


def _dot_tn(a, b):
    return lax.dot_general(a, b, (((0,), (0,)), ((), ())), preferred_element_type=F32)


def _split2(x):
    hi = x.astype(BF16)
    lo = (x - hi.astype(F32)).astype(BF16)
    return hi, lo


def _split3(x):
    hi = x.astype(BF16)
    r = x - hi.astype(F32)
    mid = r.astype(BF16)
    lo = (r - mid.astype(F32)).astype(BF16)
    return hi, mid, lo


def _dot_exact_rhs(x, m):
    hi, mid, lo = _split3(x)
    return _dot(hi, m) + _dot(mid, m) + _dot(lo, m)


def _dot_exact_lhs(m, x):
    hi, mid, lo = _split3(x)
    return _dot(m, hi) + _dot(m, mid) + _dot(m, lo)


def _sigmoid(x):
    return 1.0 / (1.0 + jnp.exp(-x))


def _silu_pair(x):
    s = _sigmoid(x)
    return x * s, s * (1.0 + x * (1.0 - s))


def _iota(shape, dim):
    return lax.broadcasted_iota(jnp.int32, shape, dim)


def _ones_where(cond):
    return jnp.where(cond, 1.0, 0.0).astype(BF16)


GROUP_SLAB = 256


def _head_blockdiag():
    rows, cols = _iota((2 * GROUP_SLAB, GROUP_SLAB), 0) & (GROUP_SLAB - 1), _iota((2 * GROUP_SLAB, GROUP_SLAB), 1)
    return _ones_where((rows >> 6) == (cols >> 6))


def _group_sum(x, bd):
    hi, lo = _split2(x)
    slabs = [_dot(jnp.concatenate([hi[:, s:s + GROUP_SLAB], lo[:, s:s + GROUP_SLAB]], axis=1), bd) for s in range(0, x.shape[1], GROUP_SLAB)]
    return jnp.concatenate(slabs, axis=1)


def _lane_pick(x, lane_idx, lane):
    return jnp.sum(jnp.where(lane_idx == lane, x, 0.0), axis=1, keepdims=True)


def _inproj(x, g, wt_all, layer, *, tm, tn):
    S, D = x.shape
    nj = PM // tn

    def body(x_ref, g_ref, w_ref, wff_ref, proj_ref, ff_ref, h_ref):
        @pl.when(pl.program_id(1) == 0)
        def _():
            xf = x_ref[...]
            ms = jnp.mean(xf * xf, axis=-1, keepdims=True)
            h = (xf * lax.rsqrt(ms + EPS) * g_ref[...]).astype(BF16)
            h_ref[...] = h
            ff_ref[...] = _dot_nt(h, wff_ref[...])

        proj_ref[...] = _dot_nt(h_ref[...], w_ref[...])

    return pl.pallas_call(
        body, name="inproj", grid=(S // tm, nj),
        in_specs=[pl.BlockSpec((tm, D), lambda i, j: (i, 0)),
                  pl.BlockSpec((1, D), lambda i, j: (0, 0)),
                  pl.BlockSpec((None, tn, D), lambda i, j: (layer, j, 0)),
                  pl.BlockSpec((None, LANES, D), lambda i, j: (layer, PM // LANES, 0))],
        out_specs=[pl.BlockSpec((tm, tn), lambda i, j: (i, j)),
                   pl.BlockSpec((tm, LANES), lambda i, j: (i, 0)),
                   pl.BlockSpec((tm, D), lambda i, j: (i, 0))],
        out_shape=[jax.ShapeDtypeStruct((S, PM), F32), jax.ShapeDtypeStruct((S, LANES), F32),
                   jax.ShapeDtypeStruct((S, D), BF16)],
        compiler_params=_cparams(dimension_semantics=("arbitrary", "arbitrary")),
    )(x, g, wt_all, wt_all)


def _pool_group_select(lane_group, vals):
    return jnp.where(lane_group == 0, vals[0], jnp.where(lane_group == 1, vals[1], jnp.where(lane_group == 2, vals[2], vals[3])))


def _prep(projm, ffo, qg, kg, bfp, wpd, ps, *, ts):
    S = projm.shape[0]
    nb = S // ts
    hb = ts // POOL_HALO

    def body(fq_ref, fk_ref, fv_ref, pp_ref, halo_ref, ff_ref, sq_ref, sk_ref, sv_ref,
             qg_ref, kg_ref, bf_ref, wpd_ref, ps_ref,
             qn_ref, ka_ref, kb_ref, v_ref, sqo_ref, sko_ref, svo_ref, pooled_ref, yp_ref, pm_ref,
             carry_ref, c_ref, buf_ref):
        i = pl.program_id(0)
        bd = _head_blockdiag()
        normed = []
        for src, g_ref in ((fq_ref, qg_ref), (fk_ref, kg_ref)):
            q = src[...]
            ss = _group_sum(q * q, bd)
            normed.append(q * lax.rsqrt(ss * (1.0 / HEAD_DIM) + EPS) * g_ref[...])
        qn_ref[...] = (normed[0] * QK_SCALE).astype(BF16)
        kn = normed[1]
        v_ref[...] = fv_ref[...].astype(BF16)
        sqo_ref[...] = (sq_ref[...] * QK_SCALE).astype(BF16)
        sko_ref[...] = sk_ref[...].astype(BF16)
        svo_ref[...] = sv_ref[...].astype(BF16)

        @pl.when(i == 0)
        def _():
            carry_ref[...] = jnp.zeros_like(carry_ref)

        z = ff_ref[...] + bf_ref[...]
        lf = jnp.minimum(z, 0.0) - jnp.log(1.0 + jnp.exp(-jnp.abs(z)))
        tri = _ones_where(_iota((ts, ts), 1) <= _iota((ts, ts), 0))
        c = _dot_exact_lhs(tri, lf) + carry_ref[...]
        c_ref[...] = c
        carry_ref[...] = c_ref[ts - 1:ts, :]
        parts = jnp.concatenate(_split3(-c), axis=1)
        row = _iota((AUG * LANES, FOX_W), 0)
        col = _iota((AUG * LANES, FOX_W), 1)
        part, src = row >> 7, row & (LANES - 1)
        pair, off = col >> 7, col & (LANES - 1)
        sel_a = _ones_where((src == FF_STRIDE * pair) & (off == HEAD_DIM + part))
        sel_b = _ones_where((src == FF_STRIDE * pair + 1) & (off == part))
        first_half = (_iota((1, FOX_W), 1) & HEAD_DIM) == 0
        ka_ref[...] = jnp.where(first_half, kn, _dot(parts, sel_a)).astype(BF16)
        kb_ref[...] = jnp.where(first_half, _dot(parts, sel_b), kn).astype(BF16)

        x = pp_ref[:, 0:POOL_W]
        pg = pp_ref[:, POOL_W:2 * POOL_W]
        halo = jnp.where(i > 0, halo_ref[:, 0:POOL_W], 0.0)
        buf_ref[0:POOL_HALO, :] = halo
        buf_ref[POOL_HALO:POOL_HALO + ts, :] = x
        acc = x
        snaps = []
        for d in range(1, POOL_HALO):
            acc = acc + buf_ref[pl.ds(POOL_HALO - d, ts), :]
            if d + 1 in POOL_WINDOWS:
                snaps.append(acc)
        lane_group = _iota((1, POOL_W), 1) >> 6
        wsum = _pool_group_select(lane_group, snaps)
        wlen = _pool_group_select(lane_group, [float(w) for w in POOL_WINDOWS])
        tpos = (i * ts + _iota((ts, 1), 0) + 1).astype(F32)
        pooled = wsum / jnp.minimum(tpos, wlen) - x
        pb = pooled.astype(BF16)
        pooled_ref[...] = pb
        yp = _dot(pb, wpd_ref[...])
        yp_ref[...] = yp
        pm_ref[...] = (yp * ps_ref[...] * (pg * _sigmoid(pg))).astype(BF16)

    blk = lambda w, c: pl.BlockSpec((ts, w), lambda i: (i, c))
    full = lambda a: pl.BlockSpec(a.shape, lambda i: (0,) * a.ndim)
    out_shapes = [
        jax.ShapeDtypeStruct((S, FOX_W), BF16), jax.ShapeDtypeStruct((S, FOX_W), BF16), jax.ShapeDtypeStruct((S, FOX_W), BF16),
        jax.ShapeDtypeStruct((S, FOX_W), BF16),
        jax.ShapeDtypeStruct((S, SB_W), BF16), jax.ShapeDtypeStruct((S, SB_W), BF16), jax.ShapeDtypeStruct((S, SB_W), BF16),
        jax.ShapeDtypeStruct((S, POOL_W), BF16), jax.ShapeDtypeStruct((S, POOL_W), F32), jax.ShapeDtypeStruct((S, POOL_W), BF16),
    ]
    out_specs = [
        blk(FOX_W, 0), blk(FOX_W, 0), blk(FOX_W, 0), blk(FOX_W, 0),
        blk(SB_W, 0), blk(SB_W, 0), blk(SB_W, 0),
        blk(POOL_W, 0), blk(POOL_W, 0), blk(POOL_W, 0),
    ]
    return pl.pallas_call(
        body, name="prep", grid=(nb,),
        in_specs=[blk(FOX_W, C_FQ // FOX_W), blk(FOX_W, C_FK // FOX_W), blk(FOX_W, C_FV // FOX_W), blk(2 * POOL_W, C_PX // (2 * POOL_W)),
                  pl.BlockSpec((POOL_HALO, 2 * POOL_W), lambda i: (jnp.maximum(i * hb - 1, 0), C_PX // (2 * POOL_W))),
                  blk(LANES, 0),
                  blk(SB_W, C_SQ // SB_W), blk(SB_W, C_SK // SB_W), blk(SB_W, C_SV // SB_W),
                  full(qg), full(kg), full(bfp), full(wpd), full(ps)],
        out_specs=out_specs, out_shape=out_shapes,
        scratch_shapes=[pltpu.VMEM((1, LANES), F32), pltpu.VMEM((ts, LANES), F32), pltpu.VMEM((ts + POOL_HALO, POOL_W), F32)],
        compiler_params=_cparams(dimension_semantics=("arbitrary",)),
    )(projm, projm, projm, projm, projm, ffo, projm, projm, projm, qg, kg, bfp, wpd, ps)


def _pair_masks(x):
    ma = _iota((1, LANES), 1) < HEAD_DIM
    zero = jnp.zeros_like(x)
    return jnp.where(ma, x, zero), jnp.where(ma, zero, x)


DIAG_TILE = 256


def _diag_tiles(tq, size=DIAG_TILE):
    size = min(tq, size)
    return [(t * size, size) for t in range(tq // size)]


def _put_rows(old, new, r0):
    return new if r0 == 0 else jnp.concatenate([old[:r0], new], axis=0)


def _aug_queries(q):
    lane = _iota((1, LANES), 1)
    one = jnp.ones_like(q)
    zero = jnp.zeros_like(q)
    qa = jnp.where(lane < HEAD_DIM, q, jnp.where(lane < HEAD_DIM + AUG, one, zero))
    qb = jnp.where(lane >= HEAD_DIM, q, jnp.where(lane < AUG, one, zero))
    return qa, qb


EXP_DEAD = -105.0
PACK = 16


def _fox_walk_left(nfull, tk, block, carry, k_refs, qk_bound, row_floor):
    lane = _iota((1, LANES), 1)

    def alive(h, jj, c):
        k0 = pl.multiple_of(jnp.maximum(nfull - 1 - jj, 0) * tk + tk - PACK, PACK)
        last = k_refs[h][pl.ds(k0, PACK), :].astype(F32)
        lo = HEAD_DIM if h == 0 else 0
        negc = jnp.sum(jnp.where((lane >= lo) & (lane < lo + AUG), last, 0.0), axis=1, keepdims=True)
        return qk_bound + jnp.max(negc) - row_floor(c)[h] >= EXP_DEAD

    def walk(heads, jj0, c0):
        def go_on(state):
            jj, c = state
            ok = jj < nfull
            for h in heads:
                ok = ok & alive(h, jj, c)
            return ok

        def step(state):
            jj, c = state
            return jj + 1, block(pl.multiple_of((nfull - 1 - jj) * tk, tk), tk, 0, c, False, heads)

        return lax.while_loop(go_on, step, (jj0, c0))

    jj_pair, carry = walk((0, 1), jnp.int32(0), carry)
    carry = walk((0,), jj_pair, carry)[1]
    return walk((1,), jj_pair, carry)[1]


def _fox_fwd(qn, ka, kb, v, projm, qkb, *, tq, tk):
    S = qn.shape[0]
    npair = FOX_HEADS // 2

    def body(q_ref, ka_ref, kb_ref, v_ref, fg_ref, qkb_ref, o_ref, lse_ref, fm_ref):
        qi = pl.program_id(1)
        lane = _iota((1, LANES), 1)
        ma = lane < HEAD_DIM
        qaug = _aug_queries(q_ref[...])
        k_refs = (ka_ref, kb_ref)

        def block(k0, tkl, r0, carry, masked, heads=(0, 1)):
            vb = v_ref[pl.ds(k0, tkl), :]
            if masked:
                mask = (k0 + _iota((tq - r0, tkl), 1)) <= (qi * tq + r0 + _iota((tq - r0, tkl), 0))
            scores = {h: _dot_nt(qaug[h][r0:], k_refs[h][pl.ds(k0, tkl), :]) for h in heads}
            new = list(carry)
            for h in heads:
                m, l, acc = [x[r0:] for x in carry[h]]
                s = jnp.where(mask, scores[h], NEG) if masked else scores[h]
                m_new = jnp.maximum(m, jnp.max(s, axis=1, keepdims=True))
                alpha = jnp.exp(m - m_new)
                p = jnp.exp(s - m_new)
                sub = (m_new, alpha * l + jnp.sum(p, axis=1, keepdims=True), alpha * acc + _dot(p.astype(BF16), vb))
                new[h] = tuple(_put_rows(old, x, r0) for old, x in zip(carry[h], sub))
            return tuple(new)

        carry = tuple((jnp.full((tq, 1), NEG, F32), jnp.zeros((tq, 1), F32), jnp.zeros((tq, LANES), F32)) for _ in range(2))
        for off, size in _diag_tiles(tq, tq):
            carry = block(pl.multiple_of(qi * tq + off, size), size, off, carry, True)
        carry = _fox_walk_left((qi * tq) // tk, tk, block, carry, k_refs, jnp.max(qkb_ref[...]),
                               lambda c: (jnp.min(c[0][0]), jnp.min(c[1][0])))
        (ma_, la, acca), (mb_, lb, accb) = carry
        o = jnp.where(ma, acca / la, accb / lb)
        o_ref[...] = o
        lse_ref[...] = jnp.where(ma, ma_ + jnp.log(la), mb_ + jnp.log(lb))
        fg = fg_ref[...]
        fm_ref[...] = (o * (fg * _sigmoid(fg))).astype(BF16)

    qblk = pl.BlockSpec((tq, LANES), lambda p, i: (i, p))
    kvblk = pl.BlockSpec((S, LANES), lambda p, i: (0, p))
    return pl.pallas_call(
        body, name="fox_fwd", grid=(npair, S // tq),
        in_specs=[qblk, kvblk, kvblk, kvblk,
                  pl.BlockSpec((tq, LANES), lambda p, i: (i, C_FG // LANES + p)),
                  pl.BlockSpec((1, LANES), lambda p, i: (0, 0))],
        out_specs=[qblk, qblk, qblk],
        out_shape=[jax.ShapeDtypeStruct((S, FOX_W), F32), jax.ShapeDtypeStruct((S, FOX_W), F32), jax.ShapeDtypeStruct((S, FOX_W), BF16)],
        compiler_params=_cparams(dimension_semantics=("arbitrary", "arbitrary")),
    )(qn, ka, kb, v, projm, qkb)


def _suffix_sums(x, tmat2):
    return _dot(jnp.concatenate(_split2(x), axis=1), tmat2)


def _suffix_matrix(tk, inclusive):
    rr, cc = _iota((2 * tk, tk), 0) & (tk - 1), _iota((2 * tk, tk), 1)
    return _ones_where(rr >= cc) if inclusive else _ones_where(rr > cc)


def _sb_scores(qh, kb, causal, tmat2, r_runs):
    heads = range(2)
    zs = [_dot_nt(qh[h], kb) for h in heads]
    nsps = [jnp.minimum(-z, 0.0) - jnp.log(1.0 + jnp.exp(-jnp.abs(z))) for z in zs]
    lbs = nsps if causal is None else [jnp.where(causal, n, 0.0) for n in nsps]
    rins = [_suffix_sums(lb, tmat2) for lb in lbs]
    args = [zs[h] + lbs[h] + (rins[h] + r_runs[h]) for h in heads]
    a_s = [jnp.exp(arg if causal is None else jnp.where(causal, arg, NEG)) for arg in args]
    return zs, nsps, lbs, a_s


def _sb_walk_left(nfull, tk, block, carry, running_sums):
    def alive(state):
        jj, c = state
        ra, rb = running_sums(c)
        return (jj < nfull) & (jnp.max(jnp.maximum(ra, rb)) >= EXP_DEAD)

    def step(state):
        jj, c = state
        return jj + 1, block(pl.multiple_of((nfull - 1 - jj) * tk, tk), 0, c, False)

    return lax.while_loop(alive, step, (jnp.int32(0), carry))[1]


def _sb_fwd(sq, sk, sv, projm, *, tq, tk):
    S = sq.shape[0]
    npair = SB_HEADS // 2

    def body(q_ref, k_ref, v_ref, sg_ref, o_ref, sm_ref):
        qi = pl.program_id(1)
        lane = _iota((1, LANES), 1)
        ma = lane < HEAD_DIM
        qh = _pair_masks(q_ref[...])
        tmat2 = _suffix_matrix(tk, inclusive=False)
        nfull = (qi * tq) // tk

        def block(k0, r0, carry, masked):
            nr = tq - r0
            kb = k_ref[pl.ds(k0, tk), :]
            vb = v_ref[pl.ds(k0, tk), :]
            causal = (k0 + _iota((nr, tk), 1)) < (qi * tq + r0 + _iota((nr, tk), 0)) if masked else None
            _, _, lbs, a_s = _sb_scores([q[r0:] for q in qh], kb, causal, tmat2, [carry[h][0][r0:] for h in range(2)])
            pv = _dot(jnp.concatenate([a.astype(BF16) for a in a_s], axis=0), vb)
            return tuple((_put_rows(carry[h][0], carry[h][0][r0:] + jnp.sum(lbs[h], axis=1, keepdims=True), r0),
                          _put_rows(carry[h][1], carry[h][1][r0:] + pv[h * nr:(h + 1) * nr], r0)) for h in range(2))

        carry = tuple((jnp.zeros((tq, 1), F32), jnp.zeros((tq, LANES), F32)) for _ in range(2))
        for off, size in reversed(_diag_tiles(tq)):
            assert size == tk
            carry = block(pl.multiple_of(qi * tq + off, tk), off, carry, True)
        (_, acca), (_, accb) = _sb_walk_left(nfull, tk, block, carry, lambda c: (c[0][0], c[1][0]))
        o = jnp.where(ma, acca, accb)
        o_ref[...] = o
        sg = sg_ref[...]
        sm_ref[...] = (o * (sg * _sigmoid(sg))).astype(BF16)

    qblk = pl.BlockSpec((tq, LANES), lambda p, i: (i, p))
    kvblk = pl.BlockSpec((S, LANES), lambda p, i: (0, p))
    return pl.pallas_call(
        body, name="sb_fwd", grid=(npair, S // tq),
        in_specs=[qblk, kvblk, kvblk, pl.BlockSpec((tq, LANES), lambda p, i: (i, C_SG // LANES + p))],
        out_specs=[qblk, qblk],
        out_shape=[jax.ShapeDtypeStruct((S, SB_W), F32), jax.ShapeDtypeStruct((S, SB_W), BF16)],
        compiler_params=_cparams(dimension_semantics=("arbitrary", "arbitrary")),
    )(sq, sk, sv, projm)


def _outproj(x, fm, pm, sm, w_out, layer, *, tm):
    S, D = x.shape

    def body(x_ref, fm_ref, pm_ref, sm_ref, w_ref, y_ref):
        y = x_ref[...] + _dot(fm_ref[...], w_ref[0:FOX_W, :])
        y = y + _dot(pm_ref[...], w_ref[FOX_W:FOX_W + POOL_W, :])
        y_ref[...] = y + _dot(sm_ref[...], w_ref[FOX_W + POOL_W:D_MIX, :])

    row = lambda w: pl.BlockSpec((tm, w), lambda i: (i, 0))
    return pl.pallas_call(
        body, name="outproj", grid=(S // tm,),
        in_specs=[row(D), row(FOX_W), row(POOL_W), row(SB_W), pl.BlockSpec((None, D_MIX, D), lambda i: (layer, 0, 0))],
        out_specs=row(D), out_shape=jax.ShapeDtypeStruct((S, D), F32),
        compiler_params=_cparams(dimension_semantics=("arbitrary",)),
    )(x, fm, pm, sm, w_out)


def _loss_head(y, target, *, tm):
    S, D = y.shape

    def body(y_ref, t_ref, dy_ref, sq_ref):
        @pl.when(pl.program_id(0) == 0)
        def _():
            sq_ref[...] = jnp.zeros_like(sq_ref)

        d = y_ref[...] - t_ref[...]
        dy_ref[...] = d * (1.0 / D)
        sq_ref[...] += jnp.sum(d * d, axis=0, keepdims=True)

    row = pl.BlockSpec((tm, D), lambda i: (i, 0))
    return pl.pallas_call(
        body, name="loss_head", grid=(S // tm,),
        in_specs=[row, row], out_specs=[row, pl.BlockSpec((1, D), lambda i: (0, 0))],
        out_shape=[jax.ShapeDtypeStruct((S, D), F32), jax.ShapeDtypeStruct((1, D), F32)],
        compiler_params=_cparams(dimension_semantics=("arbitrary",)),
    )(y, target)


def _outproj_bwd(dy, fm, pm, sm, w_out, layer, stacks, *, tm):
    S, D = dy.shape

    def body(dy_ref, fm_ref, pm_ref, sm_ref, w_ref, dm_ref, dw_ref):
        @pl.when(pl.program_id(0) == 0)
        def _():
            dw_ref[...] = jnp.zeros_like(dw_ref)

        dyb = dy_ref[...].astype(BF16)
        dm_ref[...] = _dot_nt(dyb, w_ref[...])
        dw_ref[0:FOX_W, :] += _dot_tn(fm_ref[...], dyb)
        dw_ref[FOX_W:FOX_W + POOL_W, :] += _dot_tn(pm_ref[...], dyb)
        dw_ref[FOX_W + POOL_W:D_MIX, :] += _dot_tn(sm_ref[...], dyb)

    row = lambda w: pl.BlockSpec((tm, w), lambda i: (i, 0))
    wspec = pl.BlockSpec((None, D_MIX, D), lambda i: (layer, 0, 0))
    return _stack_call(
        body, "outproj_bwd", (S // tm,), [row(D), row(FOX_W), row(POOL_W), row(SB_W), wspec], (dy, fm, pm, sm, w_out),
        [pl.BlockSpec((None, D_MIX, D), lambda i: (layer, 0, 0))], [(D_MIX, D)], stacks,
        plain_specs=[row(D_MIX)], plain_shapes=[jax.ShapeDtypeStruct((S, D_MIX), F32)],
        compiler_params=_cparams(dimension_semantics=("arbitrary",)))


def _fox_bwd(qn, ka, kb, v, o, lse, dmix, projm, qkb, *, tq, tk):
    S = qn.shape[0]
    npair = FOX_HEADS // 2

    def body(q_ref, ka_ref, kb_ref, v_ref, o_ref, lse_ref, dm_ref, fg_ref, qkb_ref,
             dq_ref, dk_ref, dv_ref, dfg_ref, dct_ref, dcr_ref):
        qi = pl.program_id(1)

        @pl.when(qi == 0)
        def _():
            dk_ref[...] = jnp.zeros_like(dk_ref)
            dv_ref[...] = jnp.zeros_like(dv_ref)
            dct_ref[...] = jnp.zeros_like(dct_ref)

        lane = _iota((1, LANES), 1)
        ma = lane < HEAD_DIM
        qh = _pair_masks(q_ref[...])
        qaug = _aug_queries(q_ref[...])
        k_refs = (ka_ref, kb_ref)
        lsev = lse_ref[...]
        lse = (_lane_pick(lsev, lane, 0), _lane_pick(lsev, lane, HEAD_DIM))
        fg = fg_ref[...]
        silu, dsilu = _silu_pair(fg)
        dm = dm_ref[...]
        ov = o_ref[...]
        do = dm * silu
        dfg_ref[...] = dm * ov * dsilu
        dd = do * ov
        dsum = (jnp.sum(jnp.where(ma, dd, 0.0), axis=1, keepdims=True), jnp.sum(jnp.where(ma, 0.0, dd), axis=1, keepdims=True))
        doh = _pair_masks(do.astype(BF16))

        def block(k0, tkl, r0, carry, masked, heads=(0, 1)):
            vb = v_ref[pl.ds(k0, tkl), :]
            if masked:
                mask = (k0 + _iota((tq - r0, tkl), 1)) <= (qi * tq + r0 + _iota((tq - r0, tkl), 0))
            kaugs = {h: k_refs[h][pl.ds(k0, tkl), :] for h in heads}
            scores = {h: _dot_nt(qaug[h][r0:], kaugs[h]) for h in heads}
            dps = {h: _dot_nt(doh[h][r0:], vb) for h in heads}
            ps, dss = [], []
            rows = [carry[1], carry[2]]
            for h in heads:
                s = jnp.where(mask, scores[h], NEG) if masked else scores[h]
                p = jnp.exp(s - lse[h][r0:])
                dsf = p * (dps[h] - dsum[h][r0:])
                dct_ref[0, h:h + 1, pl.ds(k0, tkl)] -= jnp.sum(dsf, axis=0, keepdims=True)
                rows[h] = _put_rows(carry[1 + h], carry[1 + h][r0:] + jnp.sum(dsf, axis=1, keepdims=True), r0)
                ps.append(p.astype(BF16))
                dss.append(dsf.astype(BF16))
            dv_ref[pl.ds(k0, tkl), :] += _dot_tn(jnp.concatenate(ps, axis=0), jnp.concatenate([doh[h][r0:] for h in heads], axis=0))
            dk_ref[pl.ds(k0, tkl), :] += _dot_tn(jnp.concatenate(dss, axis=0), jnp.concatenate([qh[h][r0:] for h in heads], axis=0))
            kh = jnp.concatenate([_pair_masks(kaugs[h])[h] for h in heads], axis=0)
            dq = _put_rows(carry[0], carry[0][r0:] + _dot(jnp.concatenate(dss, axis=1), kh), r0)
            return (dq, rows[0], rows[1])

        zcol = jnp.zeros((tq, 1), F32)
        carry = (jnp.zeros((tq, LANES), F32), zcol, zcol)
        for off, size in _diag_tiles(tq):
            carry = block(pl.multiple_of(qi * tq + off, size), size, off, carry, True)
        floors = (jnp.min(lse[0]), jnp.min(lse[1]))
        dq, rowa, rowb = _fox_walk_left((qi * tq) // tk, tk, block, carry, k_refs, jnp.max(qkb_ref[...]), lambda c: floors)
        dq_ref[...] = dq * QK_SCALE
        dcr_ref[0] = jnp.where(ma, rowa, rowb)

    qblk = pl.BlockSpec((tq, LANES), lambda p, i: (i, p))
    kvblk = pl.BlockSpec((S, LANES), lambda p, i: (0, p))
    f32out = jax.ShapeDtypeStruct((S, FOX_W), F32)
    ctblk = pl.BlockSpec((1, FF_STRIDE, S), lambda p, i: (p, 0, 0))
    return pl.pallas_call(
        body, name="fox_bwd", grid=(npair, S // tq),
        in_specs=[qblk, kvblk, kvblk, kvblk, qblk, qblk, qblk,
                  pl.BlockSpec((tq, LANES), lambda p, i: (i, C_FG // LANES + p)),
                  pl.BlockSpec((1, LANES), lambda p, i: (0, 0))],
        out_specs=[qblk, kvblk, kvblk, qblk, ctblk, pl.BlockSpec((1, tq, LANES), lambda p, i: (p, i, 0))],
        out_shape=[f32out, f32out, f32out, f32out, jax.ShapeDtypeStruct((npair, FF_STRIDE, S), F32),
                   jax.ShapeDtypeStruct((npair, S, LANES), F32)],
        compiler_params=_cparams(dimension_semantics=("arbitrary", "arbitrary")),
    )(qn, ka, kb, v, o, lse, dmix, projm, qkb)


def _sb_bwd(sq, sk, sv, o, dmix, projm, *, tq, tk):
    S = sq.shape[0]
    npair = SB_HEADS // 2
    mix0 = (FOX_W + POOL_W) // LANES

    def body(q_ref, k_ref, v_ref, o_ref, dm_ref, sg_ref, dq_ref, dk_ref, dv_ref, dsg_ref):
        qi = pl.program_id(1)

        @pl.when(qi == 0)
        def _():
            dk_ref[...] = jnp.zeros_like(dk_ref)
            dv_ref[...] = jnp.zeros_like(dv_ref)

        lane = _iota((1, LANES), 1)
        ma = lane < HEAD_DIM
        qh = _pair_masks(q_ref[...])
        sg = sg_ref[...]
        silu, dsilu = _silu_pair(sg)
        dm = dm_ref[...]
        ov = o_ref[...]
        do = dm * silu
        dsg_ref[...] = dm * ov * dsilu
        dob = do.astype(BF16)
        dd = dob.astype(F32) * ov
        dsum = (jnp.sum(jnp.where(ma, dd, 0.0), axis=1, keepdims=True), jnp.sum(jnp.where(ma, 0.0, dd), axis=1, keepdims=True))
        doh = _pair_masks(dob)
        tmat2 = _suffix_matrix(tk, inclusive=False)
        tmat2_inc = _suffix_matrix(tk, inclusive=True)
        nfull = (qi * tq) // tk

        def block(k0, r0, carry, masked):
            nr = tq - r0
            kb = k_ref[pl.ds(k0, tk), :]
            vb = v_ref[pl.ds(k0, tk), :]
            kh = _pair_masks(kb)
            causal = (k0 + _iota((nr, tk), 1)) < (qi * tq + r0 + _iota((nr, tk), 0)) if masked else None
            heads = range(2)
            qs = [q[r0:] for q in qh]
            dos = [d[r0:] for d in doh]
            das = [_dot_nt(dos[h], vb) for h in heads]
            zs, nsps, lbs, a_s = _sb_scores(qs, kb, causal, tmat2, [carry[h][0][r0:] for h in heads])
            abs_ = [a.astype(BF16) for a in a_s]
            us = [abs_[h].astype(F32) * das[h] for h in heads]
            uins = [_suffix_sums(u, tmat2_inc) for u in us]
            dzs = []
            for h in heads:
                cum_u = dsum[h][r0:] - (uins[h] + carry[h][1][r0:])
                dz = us[h] * jnp.exp(nsps[h]) - jnp.exp(zs[h] + nsps[h]) * cum_u
                if masked:
                    dz = jnp.where(causal, dz, 0.0)
                dzs.append(dz.astype(BF16))
            dv_ref[pl.ds(k0, tk), :] += _dot_tn(jnp.concatenate(abs_, axis=0), jnp.concatenate(dos, axis=0))
            dk_ref[pl.ds(k0, tk), :] += _dot_tn(jnp.concatenate(dzs, axis=0), jnp.concatenate(qs, axis=0))
            dq = _put_rows(carry[2], carry[2][r0:] + _dot(jnp.concatenate(dzs, axis=1), jnp.concatenate(kh, axis=0)), r0)
            new = [(_put_rows(carry[h][0], carry[h][0][r0:] + jnp.sum(lbs[h], axis=1, keepdims=True), r0),
                    _put_rows(carry[h][1], carry[h][1][r0:] + jnp.sum(us[h], axis=1, keepdims=True), r0)) for h in heads]
            return (new[0], new[1], dq)

        zcol = jnp.zeros((tq, 1), F32)
        carry = ((zcol, zcol), (zcol, zcol), jnp.zeros((tq, LANES), F32))
        for off, size in reversed(_diag_tiles(tq)):
            assert size == tk
            carry = block(pl.multiple_of(qi * tq + off, tk), off, carry, True)
        dq = _sb_walk_left(nfull, tk, block, carry, lambda c: (c[0][0], c[1][0]))[2]
        dq_ref[...] = dq * QK_SCALE

    qblk = pl.BlockSpec((tq, LANES), lambda p, i: (i, p))
    kvblk = pl.BlockSpec((S, LANES), lambda p, i: (0, p))
    f32out = jax.ShapeDtypeStruct((S, SB_W), F32)
    return pl.pallas_call(
        body, name="sb_bwd", grid=(npair, S // tq),
        in_specs=[qblk, kvblk, kvblk, qblk,
                  pl.BlockSpec((tq, LANES), lambda p, i: (i, mix0 + p)),
                  pl.BlockSpec((tq, LANES), lambda p, i: (i, C_SG // LANES + p))],
        out_specs=[qblk, kvblk, kvblk, qblk],
        out_shape=[f32out, f32out, f32out, f32out],
        compiler_params=_cparams(dimension_semantics=("arbitrary", "arbitrary")),
    )(sq, sk, sv, o, dmix, projm)


def _prep_bwd(projm, ffo, dqn, dkn, dct, dcr, dv, dfg, dsq, dsk, dsv, dsg, dmix, pooled, yp, qg, kg, bfp, wpd, ps, *, ts):
    S = projm.shape[0]
    nb = S // ts
    hb = ts // POOL_HALO
    npair = FOX_HEADS // 2
    last_halo = S // POOL_HALO - 1

    def body(fq_ref, fk_ref, pp_ref, pph_ref, ff_ref,
             dqn_ref, dkn_ref, dct_ref, dcr_ref, dv_ref, dfg_ref, dsq_ref, dsk_ref, dsv_ref, dsg_ref,
             dmp_ref, dmh_ref, pooled_ref, yp_ref, qg_ref, kg_ref, bf_ref, wpd_ref, ps_ref,
             dp_ref, dqg_ref, dkg_ref, dbf_ref, dwp_ref, dps_ref,
             carry_ref, dl_ref, buf_ref, dct_s):
        i = pl.program_id(0)
        blk = nb - 1 - i

        @pl.when(i == 0)
        def _():
            carry_ref[...] = jnp.zeros_like(carry_ref)
            dqg_ref[...] = jnp.zeros_like(dqg_ref)
            dkg_ref[...] = jnp.zeros_like(dkg_ref)
            dbf_ref[...] = jnp.zeros_like(dbf_ref)
            dwp_ref[...] = jnp.zeros_like(dwp_ref)
            dps_ref[...] = jnp.zeros_like(dps_ref)

        bd = _head_blockdiag()
        for raw_ref, g_ref, dn, dg_ref, col in ((fq_ref, qg_ref, dqn_ref[...], dqg_ref, C_FQ), (fk_ref, kg_ref, dkn_ref[...], dkg_ref, C_FK)):
            q = raw_ref[...]
            rstd = lax.rsqrt(_group_sum(q * q, bd) * (1.0 / HEAD_DIM) + EPS)
            xhat = q * rstd
            dg_ref[...] += jnp.sum(dn * xhat, axis=0, keepdims=True)
            dyg = dn * g_ref[...]
            mean = _group_sum(dyg * xhat, bd) * (1.0 / HEAD_DIM)
            dp_ref[:, col:col + FOX_W] = (rstd * (dyg - xhat * mean)).astype(BF16)
        dp_ref[:, C_FV:C_FV + FOX_W] = dv_ref[...].astype(BF16)
        dp_ref[:, C_FG:C_FG + FOX_W] = dfg_ref[...].astype(BF16)
        dp_ref[:, C_SQ:C_SQ + SB_W] = dsq_ref[...].astype(BF16)
        dp_ref[:, C_SK:C_SK + SB_W] = dsk_ref[...].astype(BF16)
        dp_ref[:, C_SV:C_SV + SB_W] = dsv_ref[...].astype(BF16)
        dp_ref[:, C_SG:C_SG + SB_W] = dsg_ref[...].astype(BF16)

        dct_s[...] = jnp.zeros_like(dct_s)
        for p in range(npair):
            dct_s[FF_STRIDE * p:FF_STRIDE * (p + 1), :] = dct_ref[p]
        dc = dct_s[...].T
        lane = _iota((1, LANES), 1)
        for p in range(npair):
            dcr = dcr_ref[p]
            dc = dc + jnp.where(lane == FF_STRIDE * p, _lane_pick(dcr, lane, 0), 0.0)
            dc = dc + jnp.where(lane == FF_STRIDE * p + 1, _lane_pick(dcr, lane, HEAD_DIM), 0.0)
        triu = _ones_where(_iota((ts, ts), 1) >= _iota((ts, ts), 0))
        dlf = _dot_exact_lhs(triu, dc) + carry_ref[...]
        dl_ref[...] = dlf
        carry_ref[...] = dl_ref[0:1, :]
        z = ff_ref[...] + bf_ref[...]
        dff = dlf * (1.0 / (1.0 + jnp.exp(z)))
        dbf_ref[...] += jnp.sum(dff, axis=0, keepdims=True)
        dp_ref[:, PM:PW] = dff.astype(BF16)

        psv = ps_ref[...]
        wpdv = wpd_ref[...]
        lane_group = _iota((1, POOL_W), 1) >> 6
        wlen = _pool_group_select(lane_group, [float(w) for w in POOL_WINDOWS])
        pg = pp_ref[:, POOL_W:2 * POOL_W]
        silu, dsilu = _silu_pair(pg)
        dmp = dmp_ref[...]
        ypv = yp_ref[...]
        dp_ref[:, C_PG:C_PG + POOL_W] = (dmp * (ypv * psv) * dsilu).astype(BF16)
        dps_ref[...] += jnp.sum(dmp * silu * ypv, axis=0, keepdims=True)
        dyp = (dmp * psv * silu).astype(BF16)
        dwp_ref[...] += _dot_tn(pooled_ref[...], dyp)
        dpooled = _dot_nt(dyp, wpdv)
        pgh = pph_ref[:, POOL_W:2 * POOL_W]
        dyph = (dmh_ref[...] * psv * (pgh * _sigmoid(pgh))).astype(BF16)
        dpooled_h = jnp.where(blk < nb - 1, _dot_nt(dyph, wpdv), 0.0)
        tpos = (blk * ts + _iota((ts, 1), 0) + 1).astype(F32)
        ev = dpooled / jnp.minimum(tpos, wlen)
        buf_ref[0:ts, :] = ev
        buf_ref[ts:ts + POOL_HALO, :] = dpooled_h / wlen
        acc = ev
        snaps = []
        for d in range(1, POOL_HALO):
            acc = acc + buf_ref[pl.ds(d, ts), :]
            if d + 1 in POOL_WINDOWS:
                snaps.append(acc)
        dp_ref[:, C_PX:C_PX + POOL_W] = (_pool_group_select(lane_group, snaps) - dpooled).astype(BF16)

    rblk = lambda w, c: pl.BlockSpec((ts, w), lambda i: (nb - 1 - i, c))
    full = lambda a: pl.BlockSpec(a.shape, lambda i: (0,) * a.ndim)
    halo = lambda w, c: pl.BlockSpec((POOL_HALO, w), lambda i: (jnp.minimum((nb - i) * hb, last_halo), c))
    acc_spec = lambda r, w: pl.BlockSpec((r, w), lambda i: (0, 0))
    return pl.pallas_call(
        body, name="prep_bwd", grid=(nb,),
        in_specs=[rblk(FOX_W, C_FQ // FOX_W), rblk(FOX_W, C_FK // FOX_W), rblk(2 * POOL_W, C_PX // (2 * POOL_W)),
                  halo(2 * POOL_W, C_PX // (2 * POOL_W)), rblk(LANES, 0),
                  rblk(FOX_W, 0), rblk(FOX_W, 0), pl.BlockSpec((npair, FF_STRIDE, ts), lambda i: (0, 0, nb - 1 - i)),
                  pl.BlockSpec((npair, ts, LANES), lambda i: (0, nb - 1 - i, 0)), rblk(FOX_W, 0), rblk(FOX_W, 0),
                  rblk(SB_W, 0), rblk(SB_W, 0), rblk(SB_W, 0), rblk(SB_W, 0),
                  rblk(POOL_W, FOX_W // POOL_W), halo(POOL_W, FOX_W // POOL_W), rblk(POOL_W, 0), rblk(POOL_W, 0),
                  full(qg), full(kg), full(bfp), full(wpd), full(ps)],
        out_specs=[rblk(PW, 0), acc_spec(1, FOX_W), acc_spec(1, FOX_W), acc_spec(1, LANES), acc_spec(POOL_W, POOL_W), acc_spec(1, POOL_W)],
        out_shape=[jax.ShapeDtypeStruct((S, PW), BF16), jax.ShapeDtypeStruct((1, FOX_W), F32), jax.ShapeDtypeStruct((1, FOX_W), F32),
                   jax.ShapeDtypeStruct((1, LANES), F32), jax.ShapeDtypeStruct((POOL_W, POOL_W), F32), jax.ShapeDtypeStruct((1, POOL_W), F32)],
        scratch_shapes=[pltpu.VMEM((1, LANES), F32), pltpu.VMEM((ts, LANES), F32), pltpu.VMEM((ts + POOL_HALO, POOL_W), F32),
                        pltpu.VMEM((LANES, ts), F32)],
        compiler_params=_cparams(dimension_semantics=("arbitrary",)),
    )(projm, projm, projm, projm, ffo, dqn, dkn, dct, dcr, dv, dfg, dsq, dsk, dsv, dsg, dmix, dmix, pooled, yp, qg, kg, bfp, wpd, ps)


def _stack_call(body, name, grid, in_specs, operands, slot_specs, slot_shapes, stacks, plain_specs=(), plain_shapes=(), **kw):
    out_specs = list(plain_specs) + list(slot_specs)
    out_shape = list(plain_shapes) + [jax.ShapeDtypeStruct((DEPTH,) + s, F32) for s in slot_shapes]
    if stacks is None:
        return pl.pallas_call(body, name=name, grid=grid, in_specs=in_specs, out_specs=out_specs, out_shape=out_shape, **kw)(*operands)
    n = len(operands)

    def aliased_body(*refs):
        body(*refs[:n], *refs[n + len(stacks):])

    return pl.pallas_call(
        aliased_body, name=name, grid=grid, in_specs=list(in_specs) + [pl.BlockSpec(memory_space=pl.ANY)] * len(stacks),
        out_specs=out_specs, out_shape=out_shape,
        input_output_aliases={n + k: len(plain_specs) + k for k in range(len(stacks))}, **kw)(*operands, *stacks)


def _inproj_dw(h, dproj, layer, stacks, *, ts, tn):
    S, D = h.shape
    nj = PM // tn

    def body(h_ref, dp_ref, dpf_ref, dw_ref, dwf_ref):
        s = pl.program_id(1)

        @pl.when(s == 0)
        def _():
            dw_ref[...] = jnp.zeros_like(dw_ref)

        @pl.when((s == 0) & (pl.program_id(0) == 0))
        def _():
            dwf_ref[...] = jnp.zeros_like(dwf_ref)

        hv = h_ref[...]
        dw_ref[...] += _dot_tn(dp_ref[...], hv)

        @pl.when(pl.program_id(0) == 0)
        def _():
            dwf_ref[...] += _dot_tn(dpf_ref[...], hv)

    return _stack_call(
        body, "inproj_dw", (nj, S // ts),
        [pl.BlockSpec((ts, D), lambda j, s: (s, 0)),
         pl.BlockSpec((ts, tn), lambda j, s: (s, j)),
         pl.BlockSpec((ts, LANES), lambda j, s: (s, PM // LANES))],
        (h, dproj, dproj),
        [pl.BlockSpec((None, tn, D), lambda j, s: (layer, j, 0)), pl.BlockSpec((None, LANES, D), lambda j, s: (layer, 0, 0))],
        [(PM, D), (LANES, D)], stacks,
        compiler_params=_cparams(dimension_semantics=("arbitrary", "arbitrary")))


def _inproj_dx(dproj, wt_all, layer, x, g, dy, *, tm):
    S, D = x.shape

    def body(dp_ref, w_ref, x_ref, g_ref, dy_ref, dx_ref, dg_ref):
        @pl.when(pl.program_id(0) == 0)
        def _():
            dg_ref[...] = jnp.zeros_like(dg_ref)

        dh = _dot(dp_ref[...], w_ref[...])
        xf = x_ref[...]
        rstd = lax.rsqrt(jnp.mean(xf * xf, axis=-1, keepdims=True) + EPS)
        xhat = xf * rstd
        dg_ref[...] += jnp.sum(dh * xhat, axis=0, keepdims=True)
        dyg = dh * g_ref[...]
        mean = jnp.mean(dyg * xhat, axis=-1, keepdims=True)
        dx_ref[...] = rstd * (dyg - xhat * mean) + dy_ref[...]

    row = lambda w: pl.BlockSpec((tm, w), lambda i: (i, 0))
    return pl.pallas_call(
        body, name="inproj_dx", grid=(S // tm,),
        in_specs=[row(PW), pl.BlockSpec((None, PW, D), lambda i: (layer, 0, 0)), row(D), pl.BlockSpec((1, D), lambda i: (0, 0)), row(D)],
        out_specs=[row(D), pl.BlockSpec((1, D), lambda i: (0, 0))],
        out_shape=[jax.ShapeDtypeStruct((S, D), F32), jax.ShapeDtypeStruct((1, D), F32)],
        compiler_params=_cparams(dimension_semantics=("arbitrary",)),
    )(dproj, wt_all, x, g, dy)


def _adam_update(w, g, m, v):
    nm = ADAM_B1 * m + (1.0 - ADAM_B1) * g
    nv = ADAM_B2 * v + (1.0 - ADAM_B2) * (g * g)
    m_hat = nm / (1.0 - ADAM_B1 ** ADAM_STEP)
    v_hat = nv / (1.0 - ADAM_B2 ** ADAM_STEP)
    return -ADAM_LR * (m_hat / (jnp.sqrt(v_hat) + ADAM_EPS) + ADAM_WD * w), nm, nv


def _adamw(w, g, m, v):
    L, R, C = w.shape
    tr = R if R <= 512 else 256

    def body(w_ref, g_ref, m_ref, v_ref, d_ref, nm_ref, nv_ref):
        d_ref[...], nm_ref[...], nv_ref[...] = _adam_update(w_ref[...], g_ref[...], m_ref[...], v_ref[...])

    spec = pl.BlockSpec((1, tr, C), lambda l, i: (l, i, 0))
    shp = jax.ShapeDtypeStruct((L, R, C), F32)
    return pl.pallas_call(
        body, name="adamw", grid=(L, R // tr), in_specs=[spec] * 4, out_specs=[spec] * 3, out_shape=[shp] * 3,
        compiler_params=_cparams(dimension_semantics=("arbitrary", "arbitrary")),
    )(w, g, m, v)


def _adamw_nd(w, g, m, v):
    shape = w.shape
    view = (1,) + shape if w.ndim == 2 else (shape[0], -1, shape[-1])
    outs = _adamw(w.reshape(view), g.reshape(view), m.reshape(view), v.reshape(view))
    return tuple(o.reshape(shape) for o in outs)


FLIP_C = (0, 0, 1)
FLIP_X = (1, 0, 0)
FLIP_Y = (0, 1, 0)
FLIP_XY = (1, 1, 0)
MESH = pl.DeviceIdType.MESH


def _peer(flip):
    me = (lax.axis_index("x"), lax.axis_index("y"), lax.axis_index("c"))
    return tuple(1 - a if f else a for a, f in zip(me, flip))


def _exchange(name, arrays, flips):
    n = len(arrays)

    def body(*refs):
        srcs, dsts = refs[:n], refs[n:2 * n]
        send_sems, recv_sems = refs[2 * n:]
        copies = [pltpu.make_async_remote_copy(src_ref=srcs[k], dst_ref=dsts[k], send_sem=send_sems.at[k], recv_sem=recv_sems.at[k],
                                               device_id=_peer(flips[k]), device_id_type=MESH) for k in range(n)]
        for cp in copies:
            cp.start()
        for cp in copies:
            cp.wait()

    anyspec = pl.BlockSpec(memory_space=pl.ANY)
    return pl.pallas_call(
        body, name=name, in_specs=[anyspec] * n, out_specs=[anyspec] * n,
        out_shape=[jax.ShapeDtypeStruct(a.shape, a.dtype) for a in arrays],
        scratch_shapes=[pltpu.SemaphoreType.DMA((n,)), pltpu.SemaphoreType.DMA((n,))],
    )(*arrays)


def _exchange_add(name, x, flip):
    def body(x_ref, o_ref, buf_ref, send_sem, recv_sem):
        cp = pltpu.make_async_remote_copy(src_ref=x_ref, dst_ref=buf_ref, send_sem=send_sem, recv_sem=recv_sem,
                                          device_id=_peer(flip), device_id_type=MESH)
        cp.start()
        cp.wait()
        o_ref[...] = x_ref[...] + buf_ref[...]

    vspec = pl.BlockSpec(memory_space=pltpu.VMEM)
    return pl.pallas_call(
        body, name=name, in_specs=[vspec], out_specs=vspec, out_shape=jax.ShapeDtypeStruct(x.shape, x.dtype),
        scratch_shapes=[pltpu.VMEM(x.shape, x.dtype), pltpu.SemaphoreType.DMA, pltpu.SemaphoreType.DMA],
    )(x)


def _chip_index():
    return 2 * lax.axis_index("x") + lax.axis_index("y")


def _gather_weights(w_in_t, w_out):
    wi = w_in_t.astype(BF16)
    wo = jnp.swapaxes(w_out, 0, 1).astype(BF16)
    halves = (wi.shape[0] // 2, wo.shape[0] // 2)
    ARR = 2
    TO_X, TO_Y, ON_Y, ON_X, SIB_X, SIB_Y, SIB_D0, SIB_D1, OWN = [ARR * k for k in range(9)]
    n_sems = ARR * 9

    def body(wi_ref, wo_ref, gi_ref, go_ref, send_sems, recv_sems):
        c = lax.axis_index("c")
        j = _chip_index()
        srcs = (wi_ref, wo_ref)
        dsts = (gi_ref, go_ref)
        def cuts(core):
            return [(pl.ds(h * core, h), pl.ds(h * core, h // 2), pl.ds(h * core + h // 2, h - h // 2)) for h in halves]
        mine, theirs = cuts(c), cuts(1 - c)
        HALF, Q0, Q1 = 0, 1, 2

        def copy(idx, src, dst, flip):
            return pltpu.make_async_remote_copy(src_ref=src, dst_ref=dst, send_sem=send_sems.at[idx], recv_sem=recv_sems.at[idx],
                                                device_id=_peer(flip), device_id_type=MESH)

        def slot(a, shard, cut):
            return dsts[a].at[shard, cut]

        jx, jy, jd = j ^ 2, j ^ 1, j ^ 3
        sends = []

        def start(cp):
            cp.start()
            sends.append(cp)

        for a in range(ARR):
            start(copy(TO_X + a, srcs[a].at[mine[a][HALF]], slot(a, j, mine[a][HALF]), FLIP_X))
            start(copy(TO_Y + a, srcs[a].at[mine[a][HALF]], slot(a, j, mine[a][HALF]), FLIP_Y))
        own = [copy(OWN + a, srcs[a], dsts[a].at[j], FLIP_C) for a in range(ARR)]
        for cp in own:
            cp.start()
        for a in range(ARR):
            copy(TO_X + a, slot(a, jx, mine[a][HALF]), slot(a, jx, mine[a][HALF]), FLIP_X).wait_recv()
            start(copy(ON_Y + a, slot(a, jx, mine[a][Q0]), slot(a, jx, mine[a][Q0]), FLIP_Y))
            start(copy(SIB_X + a, slot(a, jx, mine[a][HALF]), slot(a, jx, mine[a][HALF]), FLIP_C))
        for a in range(ARR):
            copy(TO_Y + a, slot(a, jy, mine[a][HALF]), slot(a, jy, mine[a][HALF]), FLIP_Y).wait_recv()
            start(copy(ON_X + a, slot(a, jy, mine[a][Q1]), slot(a, jy, mine[a][Q1]), FLIP_X))
            start(copy(SIB_Y + a, slot(a, jy, mine[a][HALF]), slot(a, jy, mine[a][HALF]), FLIP_C))
        for a in range(ARR):
            copy(ON_Y + a, slot(a, jd, mine[a][Q0]), slot(a, jd, mine[a][Q0]), FLIP_Y).wait_recv()
            start(copy(SIB_D0 + a, slot(a, jd, mine[a][Q0]), slot(a, jd, mine[a][Q0]), FLIP_C))
        for a in range(ARR):
            copy(ON_X + a, slot(a, jd, mine[a][Q1]), slot(a, jd, mine[a][Q1]), FLIP_X).wait_recv()
            start(copy(SIB_D1 + a, slot(a, jd, mine[a][Q1]), slot(a, jd, mine[a][Q1]), FLIP_C))
        for a in range(ARR):
            for idx, shard, cut in ((SIB_X, jx, HALF), (SIB_Y, jy, HALF), (SIB_D0, jd, Q0), (SIB_D1, jd, Q1)):
                copy(idx + a, slot(a, shard, theirs[a][cut]), slot(a, shard, theirs[a][cut]), FLIP_C).wait_recv()
        for cp in own:
            cp.wait()
        for cp in sends:
            cp.wait_send()

    anyspec = pl.BlockSpec(memory_space=pl.ANY)
    gi, go = pl.pallas_call(
        body, name="gather_weights", in_specs=[anyspec] * 2, out_specs=[anyspec] * 2,
        out_shape=[jax.ShapeDtypeStruct((4,) + wi.shape, BF16), jax.ShapeDtypeStruct((4,) + wo.shape, BF16)],
        scratch_shapes=[pltpu.SemaphoreType.DMA((n_sems,)), pltpu.SemaphoreType.DMA((n_sems,))],
    )(wi, wo)
    w_in_t_full = gi.reshape((4 * wi.shape[0],) + wi.shape[1:])
    w_out_full = jnp.swapaxes(go.reshape((4 * wo.shape[0],) + wo.shape[1:]), 0, 1)
    return w_in_t_full, w_out_full


def _to_aligned(w_t):
    _, L, D = w_t.shape
    npair = FOX_HEADS // 2
    ff = w_t[ORIG_FF:ORIG_REST].reshape(npair, 2, L, D)
    ff = jnp.pad(ff, ((0, 0), (0, FF_STRIDE - 2), (0, 0), (0, 0))).reshape(npair * FF_STRIDE, L, D)
    ff = jnp.pad(ff, ((0, LANES - npair * FF_STRIDE), (0, 0), (0, 0)))
    return jnp.swapaxes(jnp.concatenate([w_t[:ORIG_FOX], w_t[ORIG_REST:], ff], axis=0), 0, 1)


def _from_aligned(dw_t):
    n, _, D = dw_t.shape
    npair = FOX_HEADS // 2
    ff = dw_t[:, PM:PM + npair * FF_STRIDE].reshape(n, npair, FF_STRIDE, D)[:, :, :2].reshape(n, FOX_HEADS, D)
    return jnp.swapaxes(jnp.concatenate([dw_t[:, :ORIG_FOX], ff, dw_t[:, ORIG_FOX:PM]], axis=1), 0, 1)


def _half_layers(name, stack, got):
    L, R, C = stack.shape
    half = L // 2
    tr = min(256, R)
    c = lax.axis_index("c")
    which = ((1 - c) if got is None else c).astype(jnp.int32).reshape(1)

    def body(c_ref, x_ref, *refs):
        if got is None:
            refs[0][...] = x_ref[...].astype(BF16)
        else:
            acc = x_ref[...] + refs[0][...].astype(F32)
            refs[1][...] = acc
            refs[2][...] = acc.astype(BF16)

    plain = pl.BlockSpec((1, tr, C), lambda l, i, c_ref: (l, i, 0))
    picked = pl.BlockSpec((1, tr, C), lambda l, i, c_ref: (c_ref[0] * half + l, i, 0))
    shp = lambda dt: jax.ShapeDtypeStruct((half, R, C), dt)
    grid_spec = pltpu.PrefetchScalarGridSpec(
        num_scalar_prefetch=1, grid=(half, R // tr),
        in_specs=[picked] + ([] if got is None else [plain]), out_specs=[plain] if got is None else [plain, plain])
    return pl.pallas_call(
        body, name=name, grid_spec=grid_spec, out_shape=[shp(BF16)] if got is None else [shp(F32), shp(BF16)],
        compiler_params=_cparams(dimension_semantics=("arbitrary", "arbitrary")),
    )(which, stack, *([] if got is None else [got]))


def _reduce_scatter(stack_m, stack_f, stack_o, shard_cols, shard_rows):
    j = _chip_index()
    half = DEPTH // 2
    stacks = (stack_m, stack_f, stack_o)
    give = [_half_layers("rs_give", s, None)[0] for s in stacks]
    got = _exchange("rs_d2d", give, (FLIP_C,) * len(stacks))
    (m32, mbf), (f32_, fbf), (o32, obf) = [_half_layers("rs_add_chip", s, g) for s, g in zip(stacks, got)]
    d_model = stack_m.shape[2]

    def in_shards(m, f):
        return _from_aligned(jnp.concatenate([m, f], axis=1)).reshape(4, shard_cols, half, d_model)

    def out_shards(o):
        return jnp.moveaxis(o.reshape(half, 4, shard_rows, o.shape[-1]), 1, 0)

    chip = [(in_shards(m32, f32_), in_shards(mbf, fbf), 0), (out_shards(o32), out_shards(obf), 1)]
    shard = lambda a, idx: lax.dynamic_index_in_dim(a, idx, axis=0, keepdims=False)
    via = []
    for _, bf, axis in chip:
        diag = shard(bf, j ^ 3)
        cut = diag.shape[axis] // 2
        via += [lax.slice_in_dim(diag, 0, cut, axis=axis), lax.slice_in_dim(diag, cut, 2 * cut, axis=axis)]
    handed = _exchange("rs_via", via, (FLIP_X, FLIP_Y) * len(chip))
    sends = []
    for a, (f32_sum, _, axis) in enumerate(chip):
        sends.append(_add_half_along("rs_add_via", shard(f32_sum, j ^ 2), handed[2 * a + 1], axis, 1))
        sends.append(_add_half_along("rs_add_via", shard(f32_sum, j ^ 1), handed[2 * a], axis, 0))
    got = _exchange("rs_ici", sends, (FLIP_X, FLIP_Y) * len(chip))
    own_in, own_out = [shard(f32_sum, j) for f32_sum, _, _ in chip]
    mine_in = _add_rows("rs_add_in", own_in, list(got[0:2]))
    mine_out = _add_into_half("rs_add_out", own_out, list(got[2:4]))
    sib_in, g_out = _share_halves(mine_in, mine_out)
    return (mine_in, sib_in), g_out


def _add_half_along(name, base, extra, axis, which):
    lanes = min(ROW_LANE_CHUNK, base.shape[2])
    assert base.shape[axis] == 2 * extra.shape[axis]
    blk = tuple(base.shape[d] // 2 if d == axis else base.shape[d] for d in range(2)) + (lanes,)

    def body(b_ref, e_ref, o_ref):
        x = b_ref[...]
        o_ref[...] = jnp.where(pl.program_id(0) == which, x + e_ref[...].astype(F32), x).astype(BF16)

    at = lambda i, k: (i, 0, k) if axis == 0 else (0, i, k)
    return pl.pallas_call(
        body, name=name, grid=(2, base.shape[2] // lanes),
        in_specs=[pl.BlockSpec(blk, at), pl.BlockSpec(blk, lambda i, k: (0, 0, k))], out_specs=pl.BlockSpec(blk, at),
        out_shape=jax.ShapeDtypeStruct(base.shape, BF16),
        compiler_params=_cparams(dimension_semantics=("arbitrary", "arbitrary")),
    )(base, extra)


def _add_rows(name, first, others):
    n = len(others)

    def body(*refs):
        acc = refs[0][...]
        for r in refs[1:1 + n]:
            acc = acc + r[...].astype(F32)
        refs[1 + n][...] = acc

    grid, spec = _row_lane_blocks(first.shape)
    return pl.pallas_call(
        body, name=name, grid=grid, in_specs=[spec(first.shape[1])] * (1 + n), out_specs=spec(first.shape[1]),
        out_shape=jax.ShapeDtypeStruct(first.shape, F32),
        compiler_params=_cparams(dimension_semantics=("arbitrary", "arbitrary")),
    )(first, *others)


ROW_LANE_CHUNK = 256


def _row_lane_blocks(shape):
    rows, _, C = shape
    tr = rows // 2 if rows % 2 == 0 and rows > 64 else rows
    lanes = min(ROW_LANE_CHUNK, C)
    return (rows // tr, C // lanes), lambda n_mid: pl.BlockSpec((tr, n_mid, lanes), lambda i, k, *_: (i, 0, k))


def _add_into_half(name, first, others):
    half, rows, C = first.shape
    tr = min(256, rows)
    n = len(others)

    def body(c_ref, *refs):
        acc = refs[0][...]
        for r in refs[1:1 + n]:
            acc = acc + r[...].astype(F32)
        refs[1 + n][...] = acc

    grid_spec = pltpu.PrefetchScalarGridSpec(
        num_scalar_prefetch=1, grid=(half, rows // tr),
        in_specs=[pl.BlockSpec((1, tr, C), lambda l, i, c_ref: (l, i, 0))] * (1 + n),
        out_specs=pl.BlockSpec((1, tr, C), lambda l, i, c_ref: (c_ref[0] * half + l, i, 0)))
    return pl.pallas_call(
        body, name=name, grid_spec=grid_spec, out_shape=jax.ShapeDtypeStruct((2 * half, rows, C), F32),
        compiler_params=_cparams(dimension_semantics=("arbitrary", "arbitrary")),
    )(lax.axis_index("c").astype(jnp.int32).reshape(1), first, *others)


def _share_halves(mine, buf):
    half = DEPTH // 2

    def body(mine_ref, buf_in, sib_ref, buf_ref, send_sems, recv_sems):
        lay = pl.ds(half * lax.axis_index("c"), half)
        copies = [pltpu.make_async_remote_copy(src_ref=src, dst_ref=dst, send_sem=send_sems.at[k], recv_sem=recv_sems.at[k],
                                               device_id=_peer(FLIP_C), device_id_type=MESH)
                  for k, (src, dst) in enumerate(((mine_ref, sib_ref), (buf_ref.at[lay], buf_ref.at[lay])))]
        for cp in copies:
            cp.start()
        for cp in copies:
            cp.wait()

    anyspec = pl.BlockSpec(memory_space=pl.ANY)
    return pl.pallas_call(
        body, name="rs_share", in_specs=[anyspec] * 2, out_specs=[anyspec] * 2,
        out_shape=[jax.ShapeDtypeStruct(mine.shape, mine.dtype), jax.ShapeDtypeStruct(buf.shape, buf.dtype)],
        input_output_aliases={1: 1},
        scratch_shapes=[pltpu.SemaphoreType.DMA((2,)), pltpu.SemaphoreType.DMA((2,))],
    )(mine, buf)


def _adamw_halves(w, g_mine, g_sib, m, v):
    half = g_mine.shape[1]

    def body(c_ref, w_ref, gm_ref, gs_ref, m_ref, v_ref, g_ref, d_ref, nm_ref, nv_ref):
        first = c_ref[0] == 0
        gm, gs = gm_ref[...], gs_ref[...]
        for h, gv in enumerate((jnp.where(first, gm, gs), jnp.where(first, gs, gm))):
            lay = slice(half * h, half * (h + 1))
            g_ref[:, lay, :] = gv
            d_ref[:, lay, :], nm_ref[:, lay, :], nv_ref[:, lay, :] = _adam_update(w_ref[:, lay, :], gv, m_ref[:, lay, :], v_ref[:, lay, :])

    grid, spec = _row_lane_blocks(w.shape)
    full, part = spec(w.shape[1]), spec(half)
    grid_spec = pltpu.PrefetchScalarGridSpec(num_scalar_prefetch=1, grid=grid, in_specs=[full, part, part, full, full], out_specs=[full] * 4)
    return pl.pallas_call(
        body, name="adamw_halves", grid_spec=grid_spec, out_shape=[jax.ShapeDtypeStruct(w.shape, F32)] * 4,
        compiler_params=_cparams(dimension_semantics=("arbitrary", "arbitrary")),
    )(lax.axis_index("c").astype(jnp.int32).reshape(1), w, g_mine, g_sib, m, v)


def _all_reduce_small(x):
    x = _exchange_add("ar_c", x, FLIP_C)
    x = _exchange_add("ar_y", x, FLIP_Y)
    return _exchange_add("ar_x", x, FLIP_X)


def _blocks(S):
    return dict(tm=min(512, S), tm_proj=min(1024, S), ts=min(512, S), tq=min(512, S), tq_big=min(1024, S), tk=min(512, S), tks=min(256, S))


def _pair_pad(vec):
    npair = FOX_HEADS // 2
    v = jnp.pad(vec.reshape(npair, 2), ((0, 0), (0, FF_STRIDE - 2))).reshape(1, npair * FF_STRIDE)
    return jnp.pad(v, ((0, 0), (0, LANES - npair * FF_STRIDE)))


def _pair_unpad(row):
    npair = FOX_HEADS // 2
    return row[0, :npair * FF_STRIDE].reshape(npair, FF_STRIDE)[:, :2].reshape(FOX_HEADS)


def _pool_blockdiag(w_pool):
    g, cg, _ = w_pool.shape
    eye = jnp.eye(g, dtype=w_pool.dtype)
    return jnp.einsum("gh,gcd->gchd", eye, w_pool).reshape(g * cg, g * cg)


QK_BOUND_SLACK = 1.05


def _layer_params(norm_g, b_f, q_norm_g, k_norm_g, w_pool, pool_scale):
    qk_bound = QK_BOUND_SLACK * HEAD_DIM * QK_SCALE * jnp.max(jnp.abs(q_norm_g)) * jnp.max(jnp.abs(k_norm_g))
    return dict(g=norm_g.reshape(1, -1), qg=jnp.tile(q_norm_g, FOX_HEADS).reshape(1, FOX_W), kg=jnp.tile(k_norm_g, FOX_HEADS).reshape(1, FOX_W),
                bfp=_pair_pad(b_f), wpd=_pool_blockdiag(w_pool).astype(BF16), ps=pool_scale.reshape(1, POOL_W),
                qkb=jnp.full((1, LANES), qk_bound, F32))


def _layer_fwd(x, wt_all, w_out, layer, prm, bs):
    projm, ffo, h = _inproj(x, prm["g"], wt_all, layer, tm=bs["tm_proj"], tn=PROJ_TN)
    qn, ka, kb, v, sq, sk, sv, pooled, yp, pm = _prep(projm, ffo, prm["qg"], prm["kg"], prm["bfp"], prm["wpd"], prm["ps"], ts=bs["ts"])
    o, lse, fm = _fox_fwd(qn, ka, kb, v, projm, prm["qkb"], tq=bs["tq"], tk=bs["tk"])
    so, sm = _sb_fwd(sq, sk, sv, projm, tq=bs["tq"], tk=bs["tks"])
    y = _outproj(x, fm, pm, sm, w_out, layer, tm=bs["tm"])
    saved = dict(x=x, projm=projm, ffo=ffo, h=h, qn=qn, ka=ka, kb=kb, v=v, sq=sq, sk=sk, sv=sv, pooled=pooled, yp=yp,
                 o=o, lse=lse, so=so, fm=fm, pm=pm, sm=sm)
    return y, saved


def _layer_bwd(dy, wt_all, w_out, prm, sv_, bs, layer, stacks):
    dmix, stack_o = _outproj_bwd(dy, sv_["fm"], sv_["pm"], sv_["sm"], w_out, layer, None if stacks is None else stacks[2:], tm=bs["tm"])
    dqn, dkn, dv, dfg, dct, dcr = _fox_bwd(sv_["qn"], sv_["ka"], sv_["kb"], sv_["v"], sv_["o"], sv_["lse"], dmix, sv_["projm"],
                                      prm["qkb"], tq=bs["tq_big"], tk=bs["tk"])
    dsq, dsk, dsv, dsg = _sb_bwd(sv_["sq"], sv_["sk"], sv_["sv"], sv_["so"], dmix, sv_["projm"], tq=bs["tks"], tk=bs["tks"])
    dproj, dqg, dkg, dbf, dwp, dps = _prep_bwd(sv_["projm"], sv_["ffo"], dqn, dkn, dct, dcr, dv, dfg, dsq, dsk, dsv, dsg, dmix,
                                               sv_["pooled"], sv_["yp"], prm["qg"], prm["kg"], prm["bfp"], prm["wpd"], prm["ps"], ts=bs["ts"])
    stack_m, stack_f = _inproj_dw(sv_["h"], dproj, layer, None if stacks is None else stacks[:2], ts=bs["tm_proj"], tn=PROJ_TN)
    dx, dg = _inproj_dx(dproj, wt_all, layer, sv_["x"], prm["g"], dy, tm=min(256, bs["tm"]))
    grads = dict(
        norm_g=dg[0],
        b_f=_pair_unpad(dbf), q_norm_g=dqg.reshape(FOX_HEADS, HEAD_DIM).sum(0), k_norm_g=dkg.reshape(FOX_HEADS, HEAD_DIM).sum(0),
        w_pool=jnp.stack([dwp[HEAD_DIM * g:HEAD_DIM * (g + 1), HEAD_DIM * g:HEAD_DIM * (g + 1)] for g in range(4)]),
        pool_scale=dps[0])
    return dx, grads, (stack_m, stack_f, stack_o)


def _local_step(x, target, wt_all, w_out, norm_g, b_f, q_norm_g, k_norm_g, w_pool, pool_scale):
    S, D = x.shape
    bs = _blocks(S)
    prms = [_layer_params(norm_g[l], b_f[l], q_norm_g[l], k_norm_g[l], w_pool[l], pool_scale[l]) for l in range(DEPTH)]
    saved = []
    y = x
    for l in range(DEPTH):
        y, s_ = _layer_fwd(y, wt_all, w_out, l, prms[l], bs)
        saved.append(s_)
    dy, sq = _loss_head(y, target, tm=bs["tm"])
    loss = 0.5 * jnp.sum(sq) / D
    grads = [None] * DEPTH
    stacks = None
    for l in reversed(range(DEPTH)):
        dy, grads[l], stacks = _layer_bwd(dy, wt_all, w_out, prms[l], saved[l], bs, l, stacks)
    stacked = {k: jnp.stack([g[k] for g in grads]) for k in grads[0]}
    return loss, dy, stacked, stacks


SMALL = ("norm_g", "b_f", "q_norm_g", "k_norm_g", "w_pool", "pool_scale")


def _pack_small(gr):
    flat = jnp.concatenate([gr[k].reshape(-1) for k in SMALL])
    pad = (-flat.shape[0]) % (8 * LANES)
    return jnp.pad(flat, (0, pad)).reshape(-1, LANES)


def _unpack_small(packed, like):
    flat = packed.reshape(-1)
    out, off = {}, 0
    for k in SMALL:
        n = like[k].size
        out[k] = flat[off:off + n].reshape(like[k].shape)
        off += n
    return out


def kernel(x, norm_g, w_in, b_f, q_norm_g, k_norm_g, w_pool, pool_scale, w_out, loss_target, m_norm_g, m_w_in, m_b_f, m_q_norm_g, m_k_norm_g, m_w_pool, m_pool_scale, m_w_out, v_norm_g, v_w_in, v_b_f, v_q_norm_g, v_k_norm_g, v_w_pool, v_pool_scale, v_w_out):
    weights = dict(norm_g=norm_g, w_in=w_in, b_f=b_f, q_norm_g=q_norm_g, k_norm_g=k_norm_g, w_pool=w_pool, pool_scale=pool_scale, w_out=w_out)
    mom_m = dict(norm_g=m_norm_g, w_in=m_w_in, b_f=m_b_f, q_norm_g=m_q_norm_g, k_norm_g=m_k_norm_g, w_pool=m_w_pool, pool_scale=m_pool_scale, w_out=m_w_out)
    mom_v = dict(norm_g=v_norm_g, w_in=v_w_in, b_f=v_b_f, q_norm_g=v_q_norm_g, k_norm_g=v_k_norm_g, w_pool=v_w_pool, pool_scale=v_pool_scale, w_out=v_w_out)
    shard_cols = w_in.shape[2]
    shard_rows = w_out.shape[1]

    cols_first = lambda a: jnp.transpose(a, (2, 0, 1))
    w_in_t = cols_first(w_in)
    w_in_t_full, w_out_full = _gather_weights(w_in_t, w_out)
    wt_all = _to_aligned(w_in_t_full)
    loss, dx, gr, stacks = _local_step(x[0], loss_target[0], wt_all, w_out_full, norm_g, b_f, q_norm_g, k_norm_g, w_pool, pool_scale)
    loss = lax.psum(loss, ("x", "y", "c"))

    (g_in_mine, g_in_sib), g_w_out = _reduce_scatter(*stacks, shard_cols, shard_rows)
    small = _unpack_small(_all_reduce_small(_pack_small(gr)), {k: weights[k] for k in SMALL})
    grad_w = dict(small, w_out=g_w_out)

    names = ("norm_g", "w_in", "b_f", "q_norm_g", "k_norm_g", "w_pool", "pool_scale", "w_out")
    upd = {k: _adamw_nd(weights[k], grad_w[k], mom_m[k], mom_v[k]) for k in names if k != "w_in"}
    in_t = _adamw_halves(w_in_t, g_in_mine, g_in_sib, cols_first(mom_m["w_in"]), cols_first(mom_v["w_in"]))
    grad_w["w_in"], *upd["w_in"] = [jnp.transpose(a, (1, 2, 0)) for a in in_t]
    return (loss, dx[None], *[grad_w[k] for k in names], *[upd[k][0] for k in names], *[upd[k][1] for k in names], *[upd[k][2] for k in names])
```

```python
import functools

import jax
import jax.numpy as jnp
from jax import lax
from jax.experimental import pallas as pl
from jax.experimental.pallas import tpu as pltpu

F32 = jnp.float32
BF16 = jnp.bfloat16

DEPTH = 4
HEAD_DIM = 64
FOX_HEADS = 8
SB_HEADS = 4
FOX_W = FOX_HEADS * HEAD_DIM
SB_W = SB_HEADS * HEAD_DIM
POOL_W = 256
POOL_WINDOWS = (2, 4, 8, 16)
POOL_HALO = 16
D_MIX = FOX_W + POOL_W + SB_W
EPS = 1e-6
NEG = -1e30
QK_SCALE = HEAD_DIM ** -0.5

ORIG_FOX = 4 * FOX_W
ORIG_FF = ORIG_FOX
ORIG_REST = ORIG_FF + FOX_HEADS
D_IN = ORIG_REST + 2 * POOL_W + 4 * SB_W

C_FQ, C_FK, C_FV, C_FG = 0, FOX_W, 2 * FOX_W, 3 * FOX_W
C_PX = 4 * FOX_W
C_PG = C_PX + POOL_W
C_SQ = C_PG + POOL_W
C_SK, C_SV, C_SG = C_SQ + SB_W, C_SQ + 2 * SB_W, C_SQ + 3 * SB_W
PM = C_SG + SB_W
LANES = 128
PW = PM + LANES
FF_STRIDE = 8
AUG = 3

ADAM_LR = 0.001
ADAM_B1 = 0.9
ADAM_B2 = 0.999
ADAM_EPS = 1e-08
ADAM_WD = 0.01
ADAM_STEP = 10

VMEM_LIMIT = 48 * 1024 * 1024
PROJ_TN = PM // 2


def _cparams(**kw):
    return pltpu.CompilerParams(vmem_limit_bytes=VMEM_LIMIT, **kw)


def _dot(a, b):
    return jnp.dot(a, b, preferred_element_type=F32)


def _dot_nt(a, b):
    return lax.dot_general(a, b, (((1,), (1,)), ((), ())), preferred_element_type=F32)


def _dot_tn(a, b):
    return lax.dot_general(a, b, (((0,), (0,)), ((), ())), preferred_element_type=F32)


def _split2(x):
    hi = x.astype(BF16)
    lo = (x - hi.astype(F32)).astype(BF16)
    return hi, lo


def _split3(x):
    hi = x.astype(BF16)
    r = x - hi.astype(F32)
    mid = r.astype(BF16)
    lo = (r - mid.astype(F32)).astype(BF16)
    return hi, mid, lo


def _dot_exact_rhs(x, m):
    hi, mid, lo = _split3(x)
    return _dot(hi, m) + _dot(mid, m) + _dot(lo, m)


def _dot_exact_lhs(m, x):
    hi, mid, lo = _split3(x)
    return _dot(m, hi) + _dot(m, mid) + _dot(m, lo)


def _sigmoid(x):
    return 1.0 / (1.0 + jnp.exp(-x))


def _silu_pair(x):
    s = _sigmoid(x)
    return x * s, s * (1.0 + x * (1.0 - s))


def _iota(shape, dim):
    return lax.broadcasted_iota(jnp.int32, shape, dim)


def _ones_where(cond):
    return jnp.where(cond, 1.0, 0.0).astype(BF16)


GROUP_SLAB = 256


def _head_blockdiag():
    rows, cols = _iota((2 * GROUP_SLAB, GROUP_SLAB), 0) & (GROUP_SLAB - 1), _iota((2 * GROUP_SLAB, GROUP_SLAB), 1)
    return _ones_where((rows >> 6) == (cols >> 6))


def _group_sum(x, bd):
    hi, lo = _split2(x)
    slabs = [_dot(jnp.concatenate([hi[:, s:s + GROUP_SLAB], lo[:, s:s + GROUP_SLAB]], axis=1), bd) for s in range(0, x.shape[1], GROUP_SLAB)]
    return jnp.concatenate(slabs, axis=1)


def _lane_pick(x, lane_idx, lane):
    return jnp.sum(jnp.where(lane_idx == lane, x, 0.0), axis=1, keepdims=True)


def _inproj(x, g, wt_all, layer, *, tm, tn):
    S, D = x.shape
    nj = PM // tn

    def body(x_ref, g_ref, w_ref, wff_ref, proj_ref, ff_ref, h_ref):
        @pl.when(pl.program_id(1) == 0)
        def _():
            xf = x_ref[...]
            ms = jnp.mean(xf * xf, axis=-1, keepdims=True)
            h = (xf * lax.rsqrt(ms + EPS) * g_ref[...]).astype(BF16)
            h_ref[...] = h
            ff_ref[...] = _dot_nt(h, wff_ref[...])

        proj_ref[...] = _dot_nt(h_ref[...], w_ref[...])

    return pl.pallas_call(
        body, name="inproj", grid=(S // tm, nj),
        in_specs=[pl.BlockSpec((tm, D), lambda i, j: (i, 0)),
                  pl.BlockSpec((1, D), lambda i, j: (0, 0)),
                  pl.BlockSpec((None, tn, D), lambda i, j: (layer, j, 0)),
                  pl.BlockSpec((None, LANES, D), lambda i, j: (layer, PM // LANES, 0))],
        out_specs=[pl.BlockSpec((tm, tn), lambda i, j: (i, j)),
                   pl.BlockSpec((tm, LANES), lambda i, j: (i, 0)),
                   pl.BlockSpec((tm, D), lambda i, j: (i, 0))],
        out_shape=[jax.ShapeDtypeStruct((S, PM), F32), jax.ShapeDtypeStruct((S, LANES), F32),
                   jax.ShapeDtypeStruct((S, D), BF16)],
        compiler_params=_cparams(dimension_semantics=("arbitrary", "arbitrary")),
    )(x, g, wt_all, wt_all)


def _pool_group_select(lane_group, vals):
    return jnp.where(lane_group == 0, vals[0], jnp.where(lane_group == 1, vals[1], jnp.where(lane_group == 2, vals[2], vals[3])))


def _prep(projm, ffo, qg, kg, bfp, wpd, ps, *, ts):
    S = projm.shape[0]
    nb = S // ts
    hb = ts // POOL_HALO

    def body(fq_ref, fk_ref, fv_ref, pp_ref, halo_ref, ff_ref, sq_ref, sk_ref, sv_ref,
             qg_ref, kg_ref, bf_ref, wpd_ref, ps_ref,
             qn_ref, ka_ref, kb_ref, v_ref, sqo_ref, sko_ref, svo_ref, pooled_ref, yp_ref, pm_ref,
             carry_ref, c_ref, buf_ref):
        i = pl.program_id(0)
        bd = _head_blockdiag()
        normed = []
        for src, g_ref in ((fq_ref, qg_ref), (fk_ref, kg_ref)):
            q = src[...]
            ss = _group_sum(q * q, bd)
            normed.append(q * lax.rsqrt(ss * (1.0 / HEAD_DIM) + EPS) * g_ref[...])
        qn_ref[...] = (normed[0] * QK_SCALE).astype(BF16)
        kn = normed[1]
        v_ref[...] = fv_ref[...].astype(BF16)
        sqo_ref[...] = (sq_ref[...] * QK_SCALE).astype(BF16)
        sko_ref[...] = sk_ref[...].astype(BF16)
        svo_ref[...] = sv_ref[...].astype(BF16)

        @pl.when(i == 0)
        def _():
            carry_ref[...] = jnp.zeros_like(carry_ref)

        z = ff_ref[...] + bf_ref[...]
        lf = jnp.minimum(z, 0.0) - jnp.log(1.0 + jnp.exp(-jnp.abs(z)))
        tri = _ones_where(_iota((ts, ts), 1) <= _iota((ts, ts), 0))
        c = _dot_exact_lhs(tri, lf) + carry_ref[...]
        c_ref[...] = c
        carry_ref[...] = c_ref[ts - 1:ts, :]
        parts = jnp.concatenate(_split3(-c), axis=1)
        row = _iota((AUG * LANES, FOX_W), 0)
        col = _iota((AUG * LANES, FOX_W), 1)
        part, src = row >> 7, row & (LANES - 1)
        pair, off = col >> 7, col & (LANES - 1)
        sel_a = _ones_where((src == FF_STRIDE * pair) & (off == HEAD_DIM + part))
        sel_b = _ones_where((src == FF_STRIDE * pair + 1) & (off == part))
        first_half = (_iota((1, FOX_W), 1) & HEAD_DIM) == 0
        ka_ref[...] = jnp.where(first_half, kn, _dot(parts, sel_a)).astype(BF16)
        kb_ref[...] = jnp.where(first_half, _dot(parts, sel_b), kn).astype(BF16)

        x = pp_ref[:, 0:POOL_W]
        pg = pp_ref[:, POOL_W:2 * POOL_W]
        halo = jnp.where(i > 0, halo_ref[:, 0:POOL_W], 0.0)
        buf_ref[0:POOL_HALO, :] = halo
        buf_ref[POOL_HALO:POOL_HALO + ts, :] = x
        acc = x
        snaps = []
        for d in range(1, POOL_HALO):
            acc = acc + buf_ref[pl.ds(POOL_HALO - d, ts), :]
            if d + 1 in POOL_WINDOWS:
                snaps.append(acc)
        lane_group = _iota((1, POOL_W), 1) >> 6
        wsum = _pool_group_select(lane_group, snaps)
        wlen = _pool_group_select(lane_group, [float(w) for w in POOL_WINDOWS])
        tpos = (i * ts + _iota((ts, 1), 0) + 1).astype(F32)
        pooled = wsum / jnp.minimum(tpos, wlen) - x
        pb = pooled.astype(BF16)
        pooled_ref[...] = pb
        yp = _dot(pb, wpd_ref[...])
        yp_ref[...] = yp
        pm_ref[...] = (yp * ps_ref[...] * (pg * _sigmoid(pg))).astype(BF16)

    blk = lambda w, c: pl.BlockSpec((ts, w), lambda i: (i, c))
    full = lambda a: pl.BlockSpec(a.shape, lambda i: (0,) * a.ndim)
    out_shapes = [
        jax.ShapeDtypeStruct((S, FOX_W), BF16), jax.ShapeDtypeStruct((S, FOX_W), BF16), jax.ShapeDtypeStruct((S, FOX_W), BF16),
        jax.ShapeDtypeStruct((S, FOX_W), BF16),
        jax.ShapeDtypeStruct((S, SB_W), BF16), jax.ShapeDtypeStruct((S, SB_W), BF16), jax.ShapeDtypeStruct((S, SB_W), BF16),
        jax.ShapeDtypeStruct((S, POOL_W), BF16), jax.ShapeDtypeStruct((S, POOL_W), F32), jax.ShapeDtypeStruct((S, POOL_W), BF16),
    ]
    out_specs = [
        blk(FOX_W, 0), blk(FOX_W, 0), blk(FOX_W, 0), blk(FOX_W, 0),
        blk(SB_W, 0), blk(SB_W, 0), blk(SB_W, 0),
        blk(POOL_W, 0), blk(POOL_W, 0), blk(POOL_W, 0),
    ]
    return pl.pallas_call(
        body, name="prep", grid=(nb,),
        in_specs=[blk(FOX_W, C_FQ // FOX_W), blk(FOX_W, C_FK // FOX_W), blk(FOX_W, C_FV // FOX_W), blk(2 * POOL_W, C_PX // (2 * POOL_W)),
                  pl.BlockSpec((POOL_HALO, 2 * POOL_W), lambda i: (jnp.maximum(i * hb - 1, 0), C_PX // (2 * POOL_W))),
                  blk(LANES, 0),
                  blk(SB_W, C_SQ // SB_W), blk(SB_W, C_SK // SB_W), blk(SB_W, C_SV // SB_W),
                  full(qg), full(kg), full(bfp), full(wpd), full(ps)],
        out_specs=out_specs, out_shape=out_shapes,
        scratch_shapes=[pltpu.VMEM((1, LANES), F32), pltpu.VMEM((ts, LANES), F32), pltpu.VMEM((ts + POOL_HALO, POOL_W), F32)],
        compiler_params=_cparams(dimension_semantics=("arbitrary",)),
    )(projm, projm, projm, projm, projm, ffo, projm, projm, projm, qg, kg, bfp, wpd, ps)


def _pair_masks(x):
    ma = _iota((1, LANES), 1) < HEAD_DIM
    zero = jnp.zeros_like(x)
    return jnp.where(ma, x, zero), jnp.where(ma, zero, x)


DIAG_TILE = 256


def _diag_tiles(tq, size=DIAG_TILE):
    size = min(tq, size)
    return [(t * size, size) for t in range(tq // size)]


def _put_rows(old, new, r0):
    return new if r0 == 0 else jnp.concatenate([old[:r0], new], axis=0)


def _aug_queries(q):
    lane = _iota((1, LANES), 1)
    one = jnp.ones_like(q)
    zero = jnp.zeros_like(q)
    qa = jnp.where(lane < HEAD_DIM, q, jnp.where(lane < HEAD_DIM + AUG, one, zero))
    qb = jnp.where(lane >= HEAD_DIM, q, jnp.where(lane < AUG, one, zero))
    return qa, qb


EXP_DEAD = -105.0
PACK = 16


def _fox_walk_left(nfull, tk, block, carry, k_refs, qk_bound, row_floor):
    lane = _iota((1, LANES), 1)

    def alive(h, jj, c):
        k0 = pl.multiple_of(jnp.maximum(nfull - 1 - jj, 0) * tk + tk - PACK, PACK)
        last = k_refs[h][pl.ds(k0, PACK), :].astype(F32)
        lo = HEAD_DIM if h == 0 else 0
        negc = jnp.sum(jnp.where((lane >= lo) & (lane < lo + AUG), last, 0.0), axis=1, keepdims=True)
        return qk_bound + jnp.max(negc) - row_floor(c)[h] >= EXP_DEAD

    def walk(heads, jj0, c0):
        def go_on(state):
            jj, c = state
            ok = jj < nfull
            for h in heads:
                ok = ok & alive(h, jj, c)
            return ok

        def step(state):
            jj, c = state
            return jj + 1, block(pl.multiple_of((nfull - 1 - jj) * tk, tk), tk, 0, c, False, heads)

        return lax.while_loop(go_on, step, (jj0, c0))

    jj_pair, carry = walk((0, 1), jnp.int32(0), carry)
    carry = walk((0,), jj_pair, carry)[1]
    return walk((1,), jj_pair, carry)[1]


def _fox_fwd(qn, ka, kb, v, projm, qkb, *, tq, tk):
    S = qn.shape[0]
    npair = FOX_HEADS // 2

    def body(q_ref, ka_ref, kb_ref, v_ref, fg_ref, qkb_ref, o_ref, lse_ref, fm_ref):
        qi = pl.program_id(1)
        lane = _iota((1, LANES), 1)
        ma = lane < HEAD_DIM
        qaug = _aug_queries(q_ref[...])
        k_refs = (ka_ref, kb_ref)

        def block(k0, tkl, r0, carry, masked, heads=(0, 1)):
            vb = v_ref[pl.ds(k0, tkl), :]
            if masked:
                mask = (k0 + _iota((tq - r0, tkl), 1)) <= (qi * tq + r0 + _iota((tq - r0, tkl), 0))
            scores = {h: _dot_nt(qaug[h][r0:], k_refs[h][pl.ds(k0, tkl), :]) for h in heads}
            new = list(carry)
            for h in heads:
                m, l, acc = [x[r0:] for x in carry[h]]
                s = jnp.where(mask, scores[h], NEG) if masked else scores[h]
                m_new = jnp.maximum(m, jnp.max(s, axis=1, keepdims=True))
                alpha = jnp.exp(m - m_new)
                p = jnp.exp(s - m_new)
                sub = (m_new, alpha * l + jnp.sum(p, axis=1, keepdims=True), alpha * acc + _dot(p.astype(BF16), vb))
                new[h] = tuple(_put_rows(old, x, r0) for old, x in zip(carry[h], sub))
            return tuple(new)

        carry = tuple((jnp.full((tq, 1), NEG, F32), jnp.zeros((tq, 1), F32), jnp.zeros((tq, LANES), F32)) for _ in range(2))
        for off, size in _diag_tiles(tq, tq):
            carry = block(pl.multiple_of(qi * tq + off, size), size, off, carry, True)
        carry = _fox_walk_left((qi * tq) // tk, tk, block, carry, k_refs, jnp.max(qkb_ref[...]),
                               lambda c: (jnp.min(c[0][0]), jnp.min(c[1][0])))
        (ma_, la, acca), (mb_, lb, accb) = carry
        o = jnp.where(ma, acca / la, accb / lb)
        o_ref[...] = o
        lse_ref[...] = jnp.where(ma, ma_ + jnp.log(la), mb_ + jnp.log(lb))
        fg = fg_ref[...]
        fm_ref[...] = (o * (fg * _sigmoid(fg))).astype(BF16)

    qblk = pl.BlockSpec((tq, LANES), lambda p, i: (i, p))
    kvblk = pl.BlockSpec((S, LANES), lambda p, i: (0, p))
    return pl.pallas_call(
        body, name="fox_fwd", grid=(npair, S // tq),
        in_specs=[qblk, kvblk, kvblk, kvblk,
                  pl.BlockSpec((tq, LANES), lambda p, i: (i, C_FG // LANES + p)),
                  pl.BlockSpec((1, LANES), lambda p, i: (0, 0))],
        out_specs=[qblk, qblk, qblk],
        out_shape=[jax.ShapeDtypeStruct((S, FOX_W), F32), jax.ShapeDtypeStruct((S, FOX_W), F32), jax.ShapeDtypeStruct((S, FOX_W), BF16)],
        compiler_params=_cparams(dimension_semantics=("arbitrary", "arbitrary")),
    )(qn, ka, kb, v, projm, qkb)


def _suffix_sums(x, tmat2):
    return _dot(jnp.concatenate(_split2(x), axis=1), tmat2)


def _suffix_matrix(tk, inclusive):
    rr, cc = _iota((2 * tk, tk), 0) & (tk - 1), _iota((2 * tk, tk), 1)
    return _ones_where(rr >= cc) if inclusive else _ones_where(rr > cc)


def _sb_scores(qh, kb, causal, tmat2, r_runs):
    heads = range(2)
    zs = [_dot_nt(qh[h], kb) for h in heads]
    nsps = [jnp.minimum(-z, 0.0) - jnp.log(1.0 + jnp.exp(-jnp.abs(z))) for z in zs]
    lbs = nsps if causal is None else [jnp.where(causal, n, 0.0) for n in nsps]
    rins = [_suffix_sums(lb, tmat2) for lb in lbs]
    args = [zs[h] + lbs[h] + (rins[h] + r_runs[h]) for h in heads]
    a_s = [jnp.exp(arg if causal is None else jnp.where(causal, arg, NEG)) for arg in args]
    return zs, nsps, lbs, a_s


def _sb_walk_left(nfull, tk, block, carry, running_sums):
    def alive(state):
        jj, c = state
        ra, rb = running_sums(c)
        return (jj < nfull) & (jnp.max(jnp.maximum(ra, rb)) >= EXP_DEAD)

    def step(state):
        jj, c = state
        return jj + 1, block(pl.multiple_of((nfull - 1 - jj) * tk, tk), 0, c, False)

    return lax.while_loop(alive, step, (jnp.int32(0), carry))[1]


def _sb_fwd(sq, sk, sv, projm, *, tq, tk):
    S = sq.shape[0]
    npair = SB_HEADS // 2

    def body(q_ref, k_ref, v_ref, sg_ref, o_ref, sm_ref):
        qi = pl.program_id(1)
        lane = _iota((1, LANES), 1)
        ma = lane < HEAD_DIM
        qh = _pair_masks(q_ref[...])
        tmat2 = _suffix_matrix(tk, inclusive=False)
        nfull = (qi * tq) // tk

        def block(k0, r0, carry, masked):
            nr = tq - r0
            kb = k_ref[pl.ds(k0, tk), :]
            vb = v_ref[pl.ds(k0, tk), :]
            causal = (k0 + _iota((nr, tk), 1)) < (qi * tq + r0 + _iota((nr, tk), 0)) if masked else None
            _, _, lbs, a_s = _sb_scores([q[r0:] for q in qh], kb, causal, tmat2, [carry[h][0][r0:] for h in range(2)])
            pv = _dot(jnp.concatenate([a.astype(BF16) for a in a_s], axis=0), vb)
            return tuple((_put_rows(carry[h][0], carry[h][0][r0:] + jnp.sum(lbs[h], axis=1, keepdims=True), r0),
                          _put_rows(carry[h][1], carry[h][1][r0:] + pv[h * nr:(h + 1) * nr], r0)) for h in range(2))

        carry = tuple((jnp.zeros((tq, 1), F32), jnp.zeros((tq, LANES), F32)) for _ in range(2))
        for off, size in reversed(_diag_tiles(tq)):
            assert size == tk
            carry = block(pl.multiple_of(qi * tq + off, tk), off, carry, True)
        (_, acca), (_, accb) = _sb_walk_left(nfull, tk, block, carry, lambda c: (c[0][0], c[1][0]))
        o = jnp.where(ma, acca, accb)
        o_ref[...] = o
        sg = sg_ref[...]
        sm_ref[...] = (o * (sg * _sigmoid(sg))).astype(BF16)

    qblk = pl.BlockSpec((tq, LANES), lambda p, i: (i, p))
    kvblk = pl.BlockSpec((S, LANES), lambda p, i: (0, p))
    return pl.pallas_call(
        body, name="sb_fwd", grid=(npair, S // tq),
        in_specs=[qblk, kvblk, kvblk, pl.BlockSpec((tq, LANES), lambda p, i: (i, C_SG // LANES + p))],
        out_specs=[qblk, qblk],
        out_shape=[jax.ShapeDtypeStruct((S, SB_W), F32), jax.ShapeDtypeStruct((S, SB_W), BF16)],
        compiler_params=_cparams(dimension_semantics=("arbitrary", "arbitrary")),
    )(sq, sk, sv, projm)


def _outproj(x, fm, pm, sm, w_out, layer, *, tm):
    S, D = x.shape

    def body(x_ref, fm_ref, pm_ref, sm_ref, w_ref, y_ref):
        y = x_ref[...] + _dot(fm_ref[...], w_ref[0:FOX_W, :])
        y = y + _dot(pm_ref[...], w_ref[FOX_W:FOX_W + POOL_W, :])
        y_ref[...] = y + _dot(sm_ref[...], w_ref[FOX_W + POOL_W:D_MIX, :])

    row = lambda w: pl.BlockSpec((tm, w), lambda i: (i, 0))
    return pl.pallas_call(
        body, name="outproj", grid=(S // tm,),
        in_specs=[row(D), row(FOX_W), row(POOL_W), row(SB_W), pl.BlockSpec((None, D_MIX, D), lambda i: (layer, 0, 0))],
        out_specs=row(D), out_shape=jax.ShapeDtypeStruct((S, D), F32),
        compiler_params=_cparams(dimension_semantics=("arbitrary",)),
    )(x, fm, pm, sm, w_out)


def _loss_head(y, target, *, tm):
    S, D = y.shape

    def body(y_ref, t_ref, dy_ref, sq_ref):
        @pl.when(pl.program_id(0) == 0)
        def _():
            sq_ref[...] = jnp.zeros_like(sq_ref)

        d = y_ref[...] - t_ref[...]
        dy_ref[...] = d * (1.0 / D)
        sq_ref[...] += jnp.sum(d * d, axis=0, keepdims=True)

    row = pl.BlockSpec((tm, D), lambda i: (i, 0))
    return pl.pallas_call(
        body, name="loss_head", grid=(S // tm,),
        in_specs=[row, row], out_specs=[row, pl.BlockSpec((1, D), lambda i: (0, 0))],
        out_shape=[jax.ShapeDtypeStruct((S, D), F32), jax.ShapeDtypeStruct((1, D), F32)],
        compiler_params=_cparams(dimension_semantics=("arbitrary",)),
    )(y, target)


def _outproj_bwd(dy, fm, pm, sm, w_out, layer, stacks, *, tm):
    S, D = dy.shape

    def body(dy_ref, fm_ref, pm_ref, sm_ref, w_ref, dm_ref, dw_ref):
        @pl.when(pl.program_id(0) == 0)
        def _():
            dw_ref[...] = jnp.zeros_like(dw_ref)

        dyb = dy_ref[...].astype(BF16)
        dm_ref[...] = _dot_nt(dyb, w_ref[...])
        dw_ref[0:FOX_W, :] += _dot_tn(fm_ref[...], dyb)
        dw_ref[FOX_W:FOX_W + POOL_W, :] += _dot_tn(pm_ref[...], dyb)
        dw_ref[FOX_W + POOL_W:D_MIX, :] += _dot_tn(sm_ref[...], dyb)

    row = lambda w: pl.BlockSpec((tm, w), lambda i: (i, 0))
    wspec = pl.BlockSpec((None, D_MIX, D), lambda i: (layer, 0, 0))
    return _stack_call(
        body, "outproj_bwd", (S // tm,), [row(D), row(FOX_W), row(POOL_W), row(SB_W), wspec], (dy, fm, pm, sm, w_out),
        [pl.BlockSpec((None, D_MIX, D), lambda i: (layer, 0, 0))], [(D_MIX, D)], stacks,
        plain_specs=[row(D_MIX)], plain_shapes=[jax.ShapeDtypeStruct((S, D_MIX), F32)],
        compiler_params=_cparams(dimension_semantics=("arbitrary",)))


def _fox_bwd(qn, ka, kb, v, o, lse, dmix, projm, qkb, *, tq, tk):
    S = qn.shape[0]
    npair = FOX_HEADS // 2

    def body(q_ref, ka_ref, kb_ref, v_ref, o_ref, lse_ref, dm_ref, fg_ref, qkb_ref,
             dq_ref, dk_ref, dv_ref, dfg_ref, dct_ref, dcr_ref):
        qi = pl.program_id(1)

        @pl.when(qi == 0)
        def _():
            dk_ref[...] = jnp.zeros_like(dk_ref)
            dv_ref[...] = jnp.zeros_like(dv_ref)
            dct_ref[...] = jnp.zeros_like(dct_ref)

        lane = _iota((1, LANES), 1)
        ma = lane < HEAD_DIM
        qh = _pair_masks(q_ref[...])
        qaug = _aug_queries(q_ref[...])
        k_refs = (ka_ref, kb_ref)
        lsev = lse_ref[...]
        lse = (_lane_pick(lsev, lane, 0), _lane_pick(lsev, lane, HEAD_DIM))
        fg = fg_ref[...]
        silu, dsilu = _silu_pair(fg)
        dm = dm_ref[...]
        ov = o_ref[...]
        do = dm * silu
        dfg_ref[...] = dm * ov * dsilu
        dd = do * ov
        dsum = (jnp.sum(jnp.where(ma, dd, 0.0), axis=1, keepdims=True), jnp.sum(jnp.where(ma, 0.0, dd), axis=1, keepdims=True))
        doh = _pair_masks(do.astype(BF16))

        def block(k0, tkl, r0, carry, masked, heads=(0, 1)):
            vb = v_ref[pl.ds(k0, tkl), :]
            if masked:
                mask = (k0 + _iota((tq - r0, tkl), 1)) <= (qi * tq + r0 + _iota((tq - r0, tkl), 0))
            kaugs = {h: k_refs[h][pl.ds(k0, tkl), :] for h in heads}
            scores = {h: _dot_nt(qaug[h][r0:], kaugs[h]) for h in heads}
            dps = {h: _dot_nt(doh[h][r0:], vb) for h in heads}
            ps, dss = [], []
            rows = [carry[1], carry[2]]
            for h in heads:
                s = jnp.where(mask, scores[h], NEG) if masked else scores[h]
                p = jnp.exp(s - lse[h][r0:])
                dsf = p * (dps[h] - dsum[h][r0:])
                dct_ref[0, h:h + 1, pl.ds(k0, tkl)] -= jnp.sum(dsf, axis=0, keepdims=True)
                rows[h] = _put_rows(carry[1 + h], carry[1 + h][r0:] + jnp.sum(dsf, axis=1, keepdims=True), r0)
                ps.append(p.astype(BF16))
                dss.append(dsf.astype(BF16))
            dv_ref[pl.ds(k0, tkl), :] += _dot_tn(jnp.concatenate(ps, axis=0), jnp.concatenate([doh[h][r0:] for h in heads], axis=0))
            dk_ref[pl.ds(k0, tkl), :] += _dot_tn(jnp.concatenate(dss, axis=0), jnp.concatenate([qh[h][r0:] for h in heads], axis=0))
            kh = jnp.concatenate([_pair_masks(kaugs[h])[h] for h in heads], axis=0)
            dq = _put_rows(carry[0], carry[0][r0:] + _dot(jnp.concatenate(dss, axis=1), kh), r0)
            return (dq, rows[0], rows[1])

        zcol = jnp.zeros((tq, 1), F32)
        carry = (jnp.zeros((tq, LANES), F32), zcol, zcol)
        for off, size in _diag_tiles(tq):
            carry = block(pl.multiple_of(qi * tq + off, size), size, off, carry, True)
        floors = (jnp.min(lse[0]), jnp.min(lse[1]))
        dq, rowa, rowb = _fox_walk_left((qi * tq) // tk, tk, block, carry, k_refs, jnp.max(qkb_ref[...]), lambda c: floors)
        dq_ref[...] = dq * QK_SCALE
        dcr_ref[0] = jnp.where(ma, rowa, rowb)

    qblk = pl.BlockSpec((tq, LANES), lambda p, i: (i, p))
    kvblk = pl.BlockSpec((S, LANES), lambda p, i: (0, p))
    f32out = jax.ShapeDtypeStruct((S, FOX_W), F32)
    ctblk = pl.BlockSpec((1, FF_STRIDE, S), lambda p, i: (p, 0, 0))
    return pl.pallas_call(
        body, name="fox_bwd", grid=(npair, S // tq),
        in_specs=[qblk, kvblk, kvblk, kvblk, qblk, qblk, qblk,
                  pl.BlockSpec((tq, LANES), lambda p, i: (i, C_FG // LANES + p)),
                  pl.BlockSpec((1, LANES), lambda p, i: (0, 0))],
        out_specs=[qblk, kvblk, kvblk, qblk, ctblk, pl.BlockSpec((1, tq, LANES), lambda p, i: (p, i, 0))],
        out_shape=[f32out, f32out, f32out, f32out, jax.ShapeDtypeStruct((npair, FF_STRIDE, S), F32),
                   jax.ShapeDtypeStruct((npair, S, LANES), F32)],
        compiler_params=_cparams(dimension_semantics=("arbitrary", "arbitrary")),
    )(qn, ka, kb, v, o, lse, dmix, projm, qkb)


def _sb_bwd(sq, sk, sv, o, dmix, projm, *, tq, tk):
    S = sq.shape[0]
    npair = SB_HEADS // 2
    mix0 = (FOX_W + POOL_W) // LANES

    def body(q_ref, k_ref, v_ref, o_ref, dm_ref, sg_ref, dq_ref, dk_ref, dv_ref, dsg_ref):
        qi = pl.program_id(1)

        @pl.when(qi == 0)
        def _():
            dk_ref[...] = jnp.zeros_like(dk_ref)
            dv_ref[...] = jnp.zeros_like(dv_ref)

        lane = _iota((1, LANES), 1)
        ma = lane < HEAD_DIM
        qh = _pair_masks(q_ref[...])
        sg = sg_ref[...]
        silu, dsilu = _silu_pair(sg)
        dm = dm_ref[...]
        ov = o_ref[...]
        do = dm * silu
        dsg_ref[...] = dm * ov * dsilu
        dob = do.astype(BF16)
        dd = dob.astype(F32) * ov
        dsum = (jnp.sum(jnp.where(ma, dd, 0.0), axis=1, keepdims=True), jnp.sum(jnp.where(ma, 0.0, dd), axis=1, keepdims=True))
        doh = _pair_masks(dob)
        tmat2 = _suffix_matrix(tk, inclusive=False)
        tmat2_inc = _suffix_matrix(tk, inclusive=True)
        nfull = (qi * tq) // tk

        def block(k0, r0, carry, masked):
            nr = tq - r0
            kb = k_ref[pl.ds(k0, tk), :]
            vb = v_ref[pl.ds(k0, tk), :]
            kh = _pair_masks(kb)
            causal = (k0 + _iota((nr, tk), 1)) < (qi * tq + r0 + _iota((nr, tk), 0)) if masked else None
            heads = range(2)
            qs = [q[r0:] for q in qh]
            dos = [d[r0:] for d in doh]
            das = [_dot_nt(dos[h], vb) for h in heads]
            zs, nsps, lbs, a_s = _sb_scores(qs, kb, causal, tmat2, [carry[h][0][r0:] for h in heads])
            abs_ = [a.astype(BF16) for a in a_s]
            us = [abs_[h].astype(F32) * das[h] for h in heads]
            uins = [_suffix_sums(u, tmat2_inc) for u in us]
            dzs = []
            for h in heads:
                cum_u = dsum[h][r0:] - (uins[h] + carry[h][1][r0:])
                dz = us[h] * jnp.exp(nsps[h]) - jnp.exp(zs[h] + nsps[h]) * cum_u
                if masked:
                    dz = jnp.where(causal, dz, 0.0)
                dzs.append(dz.astype(BF16))
            dv_ref[pl.ds(k0, tk), :] += _dot_tn(jnp.concatenate(abs_, axis=0), jnp.concatenate(dos, axis=0))
            dk_ref[pl.ds(k0, tk), :] += _dot_tn(jnp.concatenate(dzs, axis=0), jnp.concatenate(qs, axis=0))
            dq = _put_rows(carry[2], carry[2][r0:] + _dot(jnp.concatenate(dzs, axis=1), jnp.concatenate(kh, axis=0)), r0)
            new = [(_put_rows(carry[h][0], carry[h][0][r0:] + jnp.sum(lbs[h], axis=1, keepdims=True), r0),
                    _put_rows(carry[h][1], carry[h][1][r0:] + jnp.sum(us[h], axis=1, keepdims=True), r0)) for h in heads]
            return (new[0], new[1], dq)

        zcol = jnp.zeros((tq, 1), F32)
        carry = ((zcol, zcol), (zcol, zcol), jnp.zeros((tq, LANES), F32))
        for off, size in reversed(_diag_tiles(tq)):
            assert size == tk
            carry = block(pl.multiple_of(qi * tq + off, tk), off, carry, True)
        dq = _sb_walk_left(nfull, tk, block, carry, lambda c: (c[0][0], c[1][0]))[2]
        dq_ref[...] = dq * QK_SCALE

    qblk = pl.BlockSpec((tq, LANES), lambda p, i: (i, p))
    kvblk = pl.BlockSpec((S, LANES), lambda p, i: (0, p))
    f32out = jax.ShapeDtypeStruct((S, SB_W), F32)
    return pl.pallas_call(
        body, name="sb_bwd", grid=(npair, S // tq),
        in_specs=[qblk, kvblk, kvblk, qblk,
                  pl.BlockSpec((tq, LANES), lambda p, i: (i, mix0 + p)),
                  pl.BlockSpec((tq, LANES), lambda p, i: (i, C_SG // LANES + p))],
        out_specs=[qblk, kvblk, kvblk, qblk],
        out_shape=[f32out, f32out, f32out, f32out],
        compiler_params=_cparams(dimension_semantics=("arbitrary", "arbitrary")),
    )(sq, sk, sv, o, dmix, projm)


def _prep_bwd(projm, ffo, dqn, dkn, dct, dcr, dv, dfg, dsq, dsk, dsv, dsg, dmix, pooled, yp, qg, kg, bfp, wpd, ps, *, ts):
    S = projm.shape[0]
    nb = S // ts
    hb = ts // POOL_HALO
    npair = FOX_HEADS // 2
    last_halo = S // POOL_HALO - 1

    def body(fq_ref, fk_ref, pp_ref, pph_ref, ff_ref,
             dqn_ref, dkn_ref, dct_ref, dcr_ref, dv_ref, dfg_ref, dsq_ref, dsk_ref, dsv_ref, dsg_ref,
             dmp_ref, dmh_ref, pooled_ref, yp_ref, qg_ref, kg_ref, bf_ref, wpd_ref, ps_ref,
             dp_ref, dqg_ref, dkg_ref, dbf_ref, dwp_ref, dps_ref,
             carry_ref, dl_ref, buf_ref, dct_s):
        i = pl.program_id(0)
        blk = nb - 1 - i

        @pl.when(i == 0)
        def _():
            carry_ref[...] = jnp.zeros_like(carry_ref)
            dqg_ref[...] = jnp.zeros_like(dqg_ref)
            dkg_ref[...] = jnp.zeros_like(dkg_ref)
            dbf_ref[...] = jnp.zeros_like(dbf_ref)
            dwp_ref[...] = jnp.zeros_like(dwp_ref)
            dps_ref[...] = jnp.zeros_like(dps_ref)

        bd = _head_blockdiag()
        for raw_ref, g_ref, dn, dg_ref, col in ((fq_ref, qg_ref, dqn_ref[...], dqg_ref, C_FQ), (fk_ref, kg_ref, dkn_ref[...], dkg_ref, C_FK)):
            q = raw_ref[...]
            rstd = lax.rsqrt(_group_sum(q * q, bd) * (1.0 / HEAD_DIM) + EPS)
            xhat = q * rstd
            dg_ref[...] += jnp.sum(dn * xhat, axis=0, keepdims=True)
            dyg = dn * g_ref[...]
            mean = _group_sum(dyg * xhat, bd) * (1.0 / HEAD_DIM)
            dp_ref[:, col:col + FOX_W] = (rstd * (dyg - xhat * mean)).astype(BF16)
        dp_ref[:, C_FV:C_FV + FOX_W] = dv_ref[...].astype(BF16)
        dp_ref[:, C_FG:C_FG + FOX_W] = dfg_ref[...].astype(BF16)
        dp_ref[:, C_SQ:C_SQ + SB_W] = dsq_ref[...].astype(BF16)
        dp_ref[:, C_SK:C_SK + SB_W] = dsk_ref[...].astype(BF16)
        dp_ref[:, C_SV:C_SV + SB_W] = dsv_ref[...].astype(BF16)
        dp_ref[:, C_SG:C_SG + SB_W] = dsg_ref[...].astype(BF16)

        dct_s[...] = jnp.zeros_like(dct_s)
        for p in range(npair):
            dct_s[FF_STRIDE * p:FF_STRIDE * (p + 1), :] = dct_ref[p]
        dc = dct_s[...].T
        lane = _iota((1, LANES), 1)
        for p in range(npair):
            dcr = dcr_ref[p]
            dc = dc + jnp.where(lane == FF_STRIDE * p, _lane_pick(dcr, lane, 0), 0.0)
            dc = dc + jnp.where(lane == FF_STRIDE * p + 1, _lane_pick(dcr, lane, HEAD_DIM), 0.0)
        triu = _ones_where(_iota((ts, ts), 1) >= _iota((ts, ts), 0))
        dlf = _dot_exact_lhs(triu, dc) + carry_ref[...]
        dl_ref[...] = dlf
        carry_ref[...] = dl_ref[0:1, :]
        z = ff_ref[...] + bf_ref[...]
        dff = dlf * (1.0 / (1.0 + jnp.exp(z)))
        dbf_ref[...] += jnp.sum(dff, axis=0, keepdims=True)
        dp_ref[:, PM:PW] = dff.astype(BF16)

        psv = ps_ref[...]
        wpdv = wpd_ref[...]
        lane_group = _iota((1, POOL_W), 1) >> 6
        wlen = _pool_group_select(lane_group, [float(w) for w in POOL_WINDOWS])
        pg = pp_ref[:, POOL_W:2 * POOL_W]
        silu, dsilu = _silu_pair(pg)
        dmp = dmp_ref[...]
        ypv = yp_ref[...]
        dp_ref[:, C_PG:C_PG + POOL_W] = (dmp * (ypv * psv) * dsilu).astype(BF16)
        dps_ref[...] += jnp.sum(dmp * silu * ypv, axis=0, keepdims=True)
        dyp = (dmp * psv * silu).astype(BF16)
        dwp_ref[...] += _dot_tn(pooled_ref[...], dyp)
        dpooled = _dot_nt(dyp, wpdv)
        pgh = pph_ref[:, POOL_W:2 * POOL_W]
        dyph = (dmh_ref[...] * psv * (pgh * _sigmoid(pgh))).astype(BF16)
        dpooled_h = jnp.where(blk < nb - 1, _dot_nt(dyph, wpdv), 0.0)
        tpos = (blk * ts + _iota((ts, 1), 0) + 1).astype(F32)
        ev = dpooled / jnp.minimum(tpos, wlen)
        buf_ref[0:ts, :] = ev
        buf_ref[ts:ts + POOL_HALO, :] = dpooled_h / wlen
        acc = ev
        snaps = []
        for d in range(1, POOL_HALO):
            acc = acc + buf_ref[pl.ds(d, ts), :]
            if d + 1 in POOL_WINDOWS:
                snaps.append(acc)
        dp_ref[:, C_PX:C_PX + POOL_W] = (_pool_group_select(lane_group, snaps) - dpooled).astype(BF16)

    rblk = lambda w, c: pl.BlockSpec((ts, w), lambda i: (nb - 1 - i, c))
    full = lambda a: pl.BlockSpec(a.shape, lambda i: (0,) * a.ndim)
    halo = lambda w, c: pl.BlockSpec((POOL_HALO, w), lambda i: (jnp.minimum((nb - i) * hb, last_halo), c))
    acc_spec = lambda r, w: pl.BlockSpec((r, w), lambda i: (0, 0))
    return pl.pallas_call(
        body, name="prep_bwd", grid=(nb,),
        in_specs=[rblk(FOX_W, C_FQ // FOX_W), rblk(FOX_W, C_FK // FOX_W), rblk(2 * POOL_W, C_PX // (2 * POOL_W)),
                  halo(2 * POOL_W, C_PX // (2 * POOL_W)), rblk(LANES, 0),
                  rblk(FOX_W, 0), rblk(FOX_W, 0), pl.BlockSpec((npair, FF_STRIDE, ts), lambda i: (0, 0, nb - 1 - i)),
                  pl.BlockSpec((npair, ts, LANES), lambda i: (0, nb - 1 - i, 0)), rblk(FOX_W, 0), rblk(FOX_W, 0),
                  rblk(SB_W, 0), rblk(SB_W, 0), rblk(SB_W, 0), rblk(SB_W, 0),
                  rblk(POOL_W, FOX_W // POOL_W), halo(POOL_W, FOX_W // POOL_W), rblk(POOL_W, 0), rblk(POOL_W, 0),
                  full(qg), full(kg), full(bfp), full(wpd), full(ps)],
        out_specs=[rblk(PW, 0), acc_spec(1, FOX_W), acc_spec(1, FOX_W), acc_spec(1, LANES), acc_spec(POOL_W, POOL_W), acc_spec(1, POOL_W)],
        out_shape=[jax.ShapeDtypeStruct((S, PW), BF16), jax.ShapeDtypeStruct((1, FOX_W), F32), jax.ShapeDtypeStruct((1, FOX_W), F32),
                   jax.ShapeDtypeStruct((1, LANES), F32), jax.ShapeDtypeStruct((POOL_W, POOL_W), F32), jax.ShapeDtypeStruct((1, POOL_W), F32)],
        scratch_shapes=[pltpu.VMEM((1, LANES), F32), pltpu.VMEM((ts, LANES), F32), pltpu.VMEM((ts + POOL_HALO, POOL_W), F32),
                        pltpu.VMEM((LANES, ts), F32)],
        compiler_params=_cparams(dimension_semantics=("arbitrary",)),
    )(projm, projm, projm, projm, ffo, dqn, dkn, dct, dcr, dv, dfg, dsq, dsk, dsv, dsg, dmix, dmix, pooled, yp, qg, kg, bfp, wpd, ps)


def _stack_call(body, name, grid, in_specs, operands, slot_specs, slot_shapes, stacks, plain_specs=(), plain_shapes=(), **kw):
    out_specs = list(plain_specs) + list(slot_specs)
    out_shape = list(plain_shapes) + [jax.ShapeDtypeStruct((DEPTH,) + s, F32) for s in slot_shapes]
    if stacks is None:
        return pl.pallas_call(body, name=name, grid=grid, in_specs=in_specs, out_specs=out_specs, out_shape=out_shape, **kw)(*operands)
    n = len(operands)

    def aliased_body(*refs):
        body(*refs[:n], *refs[n + len(stacks):])

    return pl.pallas_call(
        aliased_body, name=name, grid=grid, in_specs=list(in_specs) + [pl.BlockSpec(memory_space=pl.ANY)] * len(stacks),
        out_specs=out_specs, out_shape=out_shape,
        input_output_aliases={n + k: len(plain_specs) + k for k in range(len(stacks))}, **kw)(*operands, *stacks)


def _inproj_dw(h, dproj, layer, stacks, *, ts, tn):
    S, D = h.shape
    nj = PM // tn

    def body(h_ref, dp_ref, dpf_ref, dw_ref, dwf_ref):
        s = pl.program_id(1)

        @pl.when(s == 0)
        def _():
            dw_ref[...] = jnp.zeros_like(dw_ref)

        @pl.when((s == 0) & (pl.program_id(0) == 0))
        def _():
            dwf_ref[...] = jnp.zeros_like(dwf_ref)

        hv = h_ref[...]
        dw_ref[...] += _dot_tn(dp_ref[...], hv)

        @pl.when(pl.program_id(0) == 0)
        def _():
            dwf_ref[...] += _dot_tn(dpf_ref[...], hv)

    return _stack_call(
        body, "inproj_dw", (nj, S // ts),
        [pl.BlockSpec((ts, D), lambda j, s: (s, 0)),
         pl.BlockSpec((ts, tn), lambda j, s: (s, j)),
         pl.BlockSpec((ts, LANES), lambda j, s: (s, PM // LANES))],
        (h, dproj, dproj),
        [pl.BlockSpec((None, tn, D), lambda j, s: (layer, j, 0)), pl.BlockSpec((None, LANES, D), lambda j, s: (layer, 0, 0))],
        [(PM, D), (LANES, D)], stacks,
        compiler_params=_cparams(dimension_semantics=("arbitrary", "arbitrary")))


def _inproj_dx(dproj, wt_all, layer, x, g, dy, *, tm):
    S, D = x.shape

    def body(dp_ref, w_ref, x_ref, g_ref, dy_ref, dx_ref, dg_ref):
        @pl.when(pl.program_id(0) == 0)
        def _():
            dg_ref[...] = jnp.zeros_like(dg_ref)

        dh = _dot(dp_ref[...], w_ref[...])
        xf = x_ref[...]
        rstd = lax.rsqrt(jnp.mean(xf * xf, axis=-1, keepdims=True) + EPS)
        xhat = xf * rstd
        dg_ref[...] += jnp.sum(dh * xhat, axis=0, keepdims=True)
        dyg = dh * g_ref[...]
        mean = jnp.mean(dyg * xhat, axis=-1, keepdims=True)
        dx_ref[...] = rstd * (dyg - xhat * mean) + dy_ref[...]

    row = lambda w: pl.BlockSpec((tm, w), lambda i: (i, 0))
    return pl.pallas_call(
        body, name="inproj_dx", grid=(S // tm,),
        in_specs=[row(PW), pl.BlockSpec((None, PW, D), lambda i: (layer, 0, 0)), row(D), pl.BlockSpec((1, D), lambda i: (0, 0)), row(D)],
        out_specs=[row(D), pl.BlockSpec((1, D), lambda i: (0, 0))],
        out_shape=[jax.ShapeDtypeStruct((S, D), F32), jax.ShapeDtypeStruct((1, D), F32)],
        compiler_params=_cparams(dimension_semantics=("arbitrary",)),
    )(dproj, wt_all, x, g, dy)


def _adam_update(w, g, m, v):
    nm = ADAM_B1 * m + (1.0 - ADAM_B1) * g
    nv = ADAM_B2 * v + (1.0 - ADAM_B2) * (g * g)
    m_hat = nm / (1.0 - ADAM_B1 ** ADAM_STEP)
    v_hat = nv / (1.0 - ADAM_B2 ** ADAM_STEP)
    return -ADAM_LR * (m_hat / (jnp.sqrt(v_hat) + ADAM_EPS) + ADAM_WD * w), nm, nv


def _adamw(w, g, m, v):
    L, R, C = w.shape
    tr = R if R <= 512 else 256

    def body(w_ref, g_ref, m_ref, v_ref, d_ref, nm_ref, nv_ref):
        d_ref[...], nm_ref[...], nv_ref[...] = _adam_update(w_ref[...], g_ref[...], m_ref[...], v_ref[...])

    spec = pl.BlockSpec((1, tr, C), lambda l, i: (l, i, 0))
    shp = jax.ShapeDtypeStruct((L, R, C), F32)
    return pl.pallas_call(
        body, name="adamw", grid=(L, R // tr), in_specs=[spec] * 4, out_specs=[spec] * 3, out_shape=[shp] * 3,
        compiler_params=_cparams(dimension_semantics=("arbitrary", "arbitrary")),
    )(w, g, m, v)


def _adamw_nd(w, g, m, v):
    shape = w.shape
    view = (1,) + shape if w.ndim == 2 else (shape[0], -1, shape[-1])
    outs = _adamw(w.reshape(view), g.reshape(view), m.reshape(view), v.reshape(view))
    return tuple(o.reshape(shape) for o in outs)


FLIP_C = (0, 0, 1)
FLIP_X = (1, 0, 0)
FLIP_Y = (0, 1, 0)
FLIP_XY = (1, 1, 0)
MESH = pl.DeviceIdType.MESH


def _peer(flip):
    me = (lax.axis_index("x"), lax.axis_index("y"), lax.axis_index("c"))
    return tuple(1 - a if f else a for a, f in zip(me, flip))


def _exchange(name, arrays, flips):
    n = len(arrays)

    def body(*refs):
        srcs, dsts = refs[:n], refs[n:2 * n]
        send_sems, recv_sems = refs[2 * n:]
        copies = [pltpu.make_async_remote_copy(src_ref=srcs[k], dst_ref=dsts[k], send_sem=send_sems.at[k], recv_sem=recv_sems.at[k],
                                               device_id=_peer(flips[k]), device_id_type=MESH) for k in range(n)]
        for cp in copies:
            cp.start()
        for cp in copies:
            cp.wait()

    anyspec = pl.BlockSpec(memory_space=pl.ANY)
    return pl.pallas_call(
        body, name=name, in_specs=[anyspec] * n, out_specs=[anyspec] * n,
        out_shape=[jax.ShapeDtypeStruct(a.shape, a.dtype) for a in arrays],
        scratch_shapes=[pltpu.SemaphoreType.DMA((n,)), pltpu.SemaphoreType.DMA((n,))],
    )(*arrays)


def _exchange_add(name, x, flip):
    def body(x_ref, o_ref, buf_ref, send_sem, recv_sem):
        cp = pltpu.make_async_remote_copy(src_ref=x_ref, dst_ref=buf_ref, send_sem=send_sem, recv_sem=recv_sem,
                                          device_id=_peer(flip), device_id_type=MESH)
        cp.start()
        cp.wait()
        o_ref[...] = x_ref[...] + buf_ref[...]

    vspec = pl.BlockSpec(memory_space=pltpu.VMEM)
    return pl.pallas_call(
        body, name=name, in_specs=[vspec], out_specs=vspec, out_shape=jax.ShapeDtypeStruct(x.shape, x.dtype),
        scratch_shapes=[pltpu.VMEM(x.shape, x.dtype), pltpu.SemaphoreType.DMA, pltpu.SemaphoreType.DMA],
    )(x)


def _chip_index():
    return 2 * lax.axis_index("x") + lax.axis_index("y")


def _gather_weights(w_in_t, w_out):
    wi = w_in_t.astype(BF16)
    wo = jnp.swapaxes(w_out, 0, 1).astype(BF16)
    halves = (wi.shape[0] // 2, wo.shape[0] // 2)
    ARR = 2
    TO_X, TO_Y, ON_Y, ON_X, SIB_X, SIB_Y, SIB_D0, SIB_D1, OWN = [ARR * k for k in range(9)]
    n_sems = ARR * 9

    def body(wi_ref, wo_ref, gi_ref, go_ref, send_sems, recv_sems):
        c = lax.axis_index("c")
        j = _chip_index()
        srcs = (wi_ref, wo_ref)
        dsts = (gi_ref, go_ref)
        def cuts(core):
            return [(pl.ds(h * core, h), pl.ds(h * core, h // 2), pl.ds(h * core + h // 2, h - h // 2)) for h in halves]
        mine, theirs = cuts(c), cuts(1 - c)
        HALF, Q0, Q1 = 0, 1, 2

        def copy(idx, src, dst, flip):
            return pltpu.make_async_remote_copy(src_ref=src, dst_ref=dst, send_sem=send_sems.at[idx], recv_sem=recv_sems.at[idx],
                                                device_id=_peer(flip), device_id_type=MESH)

        def slot(a, shard, cut):
            return dsts[a].at[shard, cut]

        jx, jy, jd = j ^ 2, j ^ 1, j ^ 3
        sends = []

        def start(cp):
            cp.start()
            sends.append(cp)

        for a in range(ARR):
            start(copy(TO_X + a, srcs[a].at[mine[a][HALF]], slot(a, j, mine[a][HALF]), FLIP_X))
            start(copy(TO_Y + a, srcs[a].at[mine[a][HALF]], slot(a, j, mine[a][HALF]), FLIP_Y))
        own = [copy(OWN + a, srcs[a], dsts[a].at[j], FLIP_C) for a in range(ARR)]
        for cp in own:
            cp.start()
        for a in range(ARR):
            copy(TO_X + a, slot(a, jx, mine[a][HALF]), slot(a, jx, mine[a][HALF]), FLIP_X).wait_recv()
            start(copy(ON_Y + a, slot(a, jx, mine[a][Q0]), slot(a, jx, mine[a][Q0]), FLIP_Y))
            start(copy(SIB_X + a, slot(a, jx, mine[a][HALF]), slot(a, jx, mine[a][HALF]), FLIP_C))
        for a in range(ARR):
            copy(TO_Y + a, slot(a, jy, mine[a][HALF]), slot(a, jy, mine[a][HALF]), FLIP_Y).wait_recv()
            start(copy(ON_X + a, slot(a, jy, mine[a][Q1]), slot(a, jy, mine[a][Q1]), FLIP_X))
            start(copy(SIB_Y + a, slot(a, jy, mine[a][HALF]), slot(a, jy, mine[a][HALF]), FLIP_C))
        for a in range(ARR):
            copy(ON_Y + a, slot(a, jd, mine[a][Q0]), slot(a, jd, mine[a][Q0]), FLIP_Y).wait_recv()
            start(copy(SIB_D0 + a, slot(a, jd, mine[a][Q0]), slot(a, jd, mine[a][Q0]), FLIP_C))
        for a in range(ARR):
            copy(ON_X + a, slot(a, jd, mine[a][Q1]), slot(a, jd, mine[a][Q1]), FLIP_X).wait_recv()
            start(copy(SIB_D1 + a, slot(a, jd, mine[a][Q1]), slot(a, jd, mine[a][Q1]), FLIP_C))
        for a in range(ARR):
            for idx, shard, cut in ((SIB_X, jx, HALF), (SIB_Y, jy, HALF), (SIB_D0, jd, Q0), (SIB_D1, jd, Q1)):
                copy(idx + a, slot(a, shard, theirs[a][cut]), slot(a, shard, theirs[a][cut]), FLIP_C).wait_recv()
        for cp in own:
            cp.wait()
        for cp in sends:
            cp.wait_send()

    anyspec = pl.BlockSpec(memory_space=pl.ANY)
    gi, go = pl.pallas_call(
        body, name="gather_weights", in_specs=[anyspec] * 2, out_specs=[anyspec] * 2,
        out_shape=[jax.ShapeDtypeStruct((4,) + wi.shape, BF16), jax.ShapeDtypeStruct((4,) + wo.shape, BF16)],
        scratch_shapes=[pltpu.SemaphoreType.DMA((n_sems,)), pltpu.SemaphoreType.DMA((n_sems,))],
    )(wi, wo)
    w_in_t_full = gi.reshape((4 * wi.shape[0],) + wi.shape[1:])
    w_out_full = jnp.swapaxes(go.reshape((4 * wo.shape[0],) + wo.shape[1:]), 0, 1)
    return w_in_t_full, w_out_full


def _to_aligned(w_t):
    _, L, D = w_t.shape
    npair = FOX_HEADS // 2
    ff = w_t[ORIG_FF:ORIG_REST].reshape(npair, 2, L, D)
    ff = jnp.pad(ff, ((0, 0), (0, FF_STRIDE - 2), (0, 0), (0, 0))).reshape(npair * FF_STRIDE, L, D)
    ff = jnp.pad(ff, ((0, LANES - npair * FF_STRIDE), (0, 0), (0, 0)))
    return jnp.swapaxes(jnp.concatenate([w_t[:ORIG_FOX], w_t[ORIG_REST:], ff], axis=0), 0, 1)


def _from_aligned(dw_t):
    n, _, D = dw_t.shape
    npair = FOX_HEADS // 2
    ff = dw_t[:, PM:PM + npair * FF_STRIDE].reshape(n, npair, FF_STRIDE, D)[:, :, :2].reshape(n, FOX_HEADS, D)
    return jnp.swapaxes(jnp.concatenate([dw_t[:, :ORIG_FOX], ff, dw_t[:, ORIG_FOX:PM]], axis=1), 0, 1)


def _half_layers(name, stack, got):
    L, R, C = stack.shape
    half = L // 2
    tr = min(256, R)
    c = lax.axis_index("c")
    which = ((1 - c) if got is None else c).astype(jnp.int32).reshape(1)

    def body(c_ref, x_ref, *refs):
        if got is None:
            refs[0][...] = x_ref[...].astype(BF16)
        else:
            acc = x_ref[...] + refs[0][...].astype(F32)
            refs[1][...] = acc
            refs[2][...] = acc.astype(BF16)

    plain = pl.BlockSpec((1, tr, C), lambda l, i, c_ref: (l, i, 0))
    picked = pl.BlockSpec((1, tr, C), lambda l, i, c_ref: (c_ref[0] * half + l, i, 0))
    shp = lambda dt: jax.ShapeDtypeStruct((half, R, C), dt)
    grid_spec = pltpu.PrefetchScalarGridSpec(
        num_scalar_prefetch=1, grid=(half, R // tr),
        in_specs=[picked] + ([] if got is None else [plain]), out_specs=[plain] if got is None else [plain, plain])
    return pl.pallas_call(
        body, name=name, grid_spec=grid_spec, out_shape=[shp(BF16)] if got is None else [shp(F32), shp(BF16)],
        compiler_params=_cparams(dimension_semantics=("arbitrary", "arbitrary")),
    )(which, stack, *([] if got is None else [got]))


def _reduce_scatter(stack_m, stack_f, stack_o, shard_cols, shard_rows):
    j = _chip_index()
    half = DEPTH // 2
    stacks = (stack_m, stack_f, stack_o)
    give = [_half_layers("rs_give", s, None)[0] for s in stacks]
    got = _exchange("rs_d2d", give, (FLIP_C,) * len(stacks))
    (m32, mbf), (f32_, fbf), (o32, obf) = [_half_layers("rs_add_chip", s, g) for s, g in zip(stacks, got)]
    d_model = stack_m.shape[2]

    def in_shards(m, f):
        return _from_aligned(jnp.concatenate([m, f], axis=1)).reshape(4, shard_cols, half, d_model)

    def out_shards(o):
        return jnp.moveaxis(o.reshape(half, 4, shard_rows, o.shape[-1]), 1, 0)

    chip = [(in_shards(m32, f32_), in_shards(mbf, fbf), 0), (out_shards(o32), out_shards(obf), 1)]
    shard = lambda a, idx: lax.dynamic_index_in_dim(a, idx, axis=0, keepdims=False)
    via = []
    for _, bf, axis in chip:
        diag = shard(bf, j ^ 3)
        cut = diag.shape[axis] // 2
        via += [lax.slice_in_dim(diag, 0, cut, axis=axis), lax.slice_in_dim(diag, cut, 2 * cut, axis=axis)]
    handed = _exchange("rs_via", via, (FLIP_X, FLIP_Y) * len(chip))
    sends = []
    for a, (f32_sum, _, axis) in enumerate(chip):
        sends.append(_add_half_along("rs_add_via", shard(f32_sum, j ^ 2), handed[2 * a + 1], axis, 1))
        sends.append(_add_half_along("rs_add_via", shard(f32_sum, j ^ 1), handed[2 * a], axis, 0))
    got = _exchange("rs_ici", sends, (FLIP_X, FLIP_Y) * len(chip))
    own_in, own_out = [shard(f32_sum, j) for f32_sum, _, _ in chip]
    mine_in = _add_rows("rs_add_in", own_in, list(got[0:2]))
    mine_out = _add_into_half("rs_add_out", own_out, list(got[2:4]))
    sib_in, g_out = _share_halves(mine_in, mine_out)
    return (mine_in, sib_in), g_out


def _add_half_along(name, base, extra, axis, which):
    lanes = min(ROW_LANE_CHUNK, base.shape[2])
    assert base.shape[axis] == 2 * extra.shape[axis]
    blk = tuple(base.shape[d] // 2 if d == axis else base.shape[d] for d in range(2)) + (lanes,)

    def body(b_ref, e_ref, o_ref):
        x = b_ref[...]
        o_ref[...] = jnp.where(pl.program_id(0) == which, x + e_ref[...].astype(F32), x).astype(BF16)

    at = lambda i, k: (i, 0, k) if axis == 0 else (0, i, k)
    return pl.pallas_call(
        body, name=name, grid=(2, base.shape[2] // lanes),
        in_specs=[pl.BlockSpec(blk, at), pl.BlockSpec(blk, lambda i, k: (0, 0, k))], out_specs=pl.BlockSpec(blk, at),
        out_shape=jax.ShapeDtypeStruct(base.shape, BF16),
        compiler_params=_cparams(dimension_semantics=("arbitrary", "arbitrary")),
    )(base, extra)


def _add_rows(name, first, others):
    n = len(others)

    def body(*refs):
        acc = refs[0][...]
        for r in refs[1:1 + n]:
            acc = acc + r[...].astype(F32)
        refs[1 + n][...] = acc

    grid, spec = _row_lane_blocks(first.shape)
    return pl.pallas_call(
        body, name=name, grid=grid, in_specs=[spec(first.shape[1])] * (1 + n), out_specs=spec(first.shape[1]),
        out_shape=jax.ShapeDtypeStruct(first.shape, F32),
        compiler_params=_cparams(dimension_semantics=("arbitrary", "arbitrary")),
    )(first, *others)


ROW_LANE_CHUNK = 256


def _row_lane_blocks(shape):
    rows, _, C = shape
    tr = rows // 2 if rows % 2 == 0 and rows > 64 else rows
    lanes = min(ROW_LANE_CHUNK, C)
    return (rows // tr, C // lanes), lambda n_mid: pl.BlockSpec((tr, n_mid, lanes), lambda i, k, *_: (i, 0, k))


def _add_into_half(name, first, others):
    half, rows, C = first.shape
    tr = min(256, rows)
    n = len(others)

    def body(c_ref, *refs):
        acc = refs[0][...]
        for r in refs[1:1 + n]:
            acc = acc + r[...].astype(F32)
        refs[1 + n][...] = acc

    grid_spec = pltpu.PrefetchScalarGridSpec(
        num_scalar_prefetch=1, grid=(half, rows // tr),
        in_specs=[pl.BlockSpec((1, tr, C), lambda l, i, c_ref: (l, i, 0))] * (1 + n),
        out_specs=pl.BlockSpec((1, tr, C), lambda l, i, c_ref: (c_ref[0] * half + l, i, 0)))
    return pl.pallas_call(
        body, name=name, grid_spec=grid_spec, out_shape=jax.ShapeDtypeStruct((2 * half, rows, C), F32),
        compiler_params=_cparams(dimension_semantics=("arbitrary", "arbitrary")),
    )(lax.axis_index("c").astype(jnp.int32).reshape(1), first, *others)


def _share_halves(mine, buf):
    half = DEPTH // 2

    def body(mine_ref, buf_in, sib_ref, buf_ref, send_sems, recv_sems):
        lay = pl.ds(half * lax.axis_index("c"), half)
        copies = [pltpu.make_async_remote_copy(src_ref=src, dst_ref=dst, send_sem=send_sems.at[k], recv_sem=recv_sems.at[k],
                                               device_id=_peer(FLIP_C), device_id_type=MESH)
                  for k, (src, dst) in enumerate(((mine_ref, sib_ref), (buf_ref.at[lay], buf_ref.at[lay])))]
        for cp in copies:
            cp.start()
        for cp in copies:
            cp.wait()

    anyspec = pl.BlockSpec(memory_space=pl.ANY)
    return pl.pallas_call(
        body, name="rs_share", in_specs=[anyspec] * 2, out_specs=[anyspec] * 2,
        out_shape=[jax.ShapeDtypeStruct(mine.shape, mine.dtype), jax.ShapeDtypeStruct(buf.shape, buf.dtype)],
        input_output_aliases={1: 1},
        scratch_shapes=[pltpu.SemaphoreType.DMA((2,)), pltpu.SemaphoreType.DMA((2,))],
    )(mine, buf)


def _adamw_halves(w, g_mine, g_sib, m, v):
    half = g_mine.shape[1]

    def body(c_ref, w_ref, gm_ref, gs_ref, m_ref, v_ref, g_ref, d_ref, nm_ref, nv_ref):
        first = c_ref[0] == 0
        gm, gs = gm_ref[...], gs_ref[...]
        for h, gv in enumerate((jnp.where(first, gm, gs), jnp.where(first, gs, gm))):
            lay = slice(half * h, half * (h + 1))
            g_ref[:, lay, :] = gv
            d_ref[:, lay, :], nm_ref[:, lay, :], nv_ref[:, lay, :] = _adam_update(w_ref[:, lay, :], gv, m_ref[:, lay, :], v_ref[:, lay, :])

    grid, spec = _row_lane_blocks(w.shape)
    full, part = spec(w.shape[1]), spec(half)
    grid_spec = pltpu.PrefetchScalarGridSpec(num_scalar_prefetch=1, grid=grid, in_specs=[full, part, part, full, full], out_specs=[full] * 4)
    return pl.pallas_call(
        body, name="adamw_halves", grid_spec=grid_spec, out_shape=[jax.ShapeDtypeStruct(w.shape, F32)] * 4,
        compiler_params=_cparams(dimension_semantics=("arbitrary", "arbitrary")),
    )(lax.axis_index("c").astype(jnp.int32).reshape(1), w, g_mine, g_sib, m, v)


def _all_reduce_small(x):
    x = _exchange_add("ar_c", x, FLIP_C)
    x = _exchange_add("ar_y", x, FLIP_Y)
    return _exchange_add("ar_x", x, FLIP_X)


def _blocks(S):
    return dict(tm=min(512, S), tm_proj=min(1024, S), ts=min(512, S), tq=min(512, S), tq_big=min(1024, S), tk=min(512, S), tks=min(256, S))


def _pair_pad(vec):
    npair = FOX_HEADS // 2
    v = jnp.pad(vec.reshape(npair, 2), ((0, 0), (0, FF_STRIDE - 2))).reshape(1, npair * FF_STRIDE)
    return jnp.pad(v, ((0, 0), (0, LANES - npair * FF_STRIDE)))


def _pair_unpad(row):
    npair = FOX_HEADS // 2
    return row[0, :npair * FF_STRIDE].reshape(npair, FF_STRIDE)[:, :2].reshape(FOX_HEADS)


def _pool_blockdiag(w_pool):
    g, cg, _ = w_pool.shape
    eye = jnp.eye(g, dtype=w_pool.dtype)
    return jnp.einsum("gh,gcd->gchd", eye, w_pool).reshape(g * cg, g * cg)


QK_BOUND_SLACK = 1.05


def _layer_params(norm_g, b_f, q_norm_g, k_norm_g, w_pool, pool_scale):
    qk_bound = QK_BOUND_SLACK * HEAD_DIM * QK_SCALE * jnp.max(jnp.abs(q_norm_g)) * jnp.max(jnp.abs(k_norm_g))
    return dict(g=norm_g.reshape(1, -1), qg=jnp.tile(q_norm_g, FOX_HEADS).reshape(1, FOX_W), kg=jnp.tile(k_norm_g, FOX_HEADS).reshape(1, FOX_W),
                bfp=_pair_pad(b_f), wpd=_pool_blockdiag(w_pool).astype(BF16), ps=pool_scale.reshape(1, POOL_W),
                qkb=jnp.full((1, LANES), qk_bound, F32))


def _layer_fwd(x, wt_all, w_out, layer, prm, bs):
    projm, ffo, h = _inproj(x, prm["g"], wt_all, layer, tm=bs["tm_proj"], tn=PROJ_TN)
    qn, ka, kb, v, sq, sk, sv, pooled, yp, pm = _prep(projm, ffo, prm["qg"], prm["kg"], prm["bfp"], prm["wpd"], prm["ps"], ts=bs["ts"])
    o, lse, fm = _fox_fwd(qn, ka, kb, v, projm, prm["qkb"], tq=bs["tq"], tk=bs["tks"])
    so, sm = _sb_fwd(sq, sk, sv, projm, tq=bs["tks"], tk=bs["tks"])
    y = _outproj(x, fm, pm, sm, w_out, layer, tm=bs["tm"])
    saved = dict(x=x, projm=projm, ffo=ffo, h=h, qn=qn, ka=ka, kb=kb, v=v, sq=sq, sk=sk, sv=sv, pooled=pooled, yp=yp,
                 o=o, lse=lse, so=so, fm=fm, pm=pm, sm=sm)
    return y, saved


def _layer_bwd(dy, wt_all, w_out, prm, sv_, bs, layer, stacks):
    dmix, stack_o = _outproj_bwd(dy, sv_["fm"], sv_["pm"], sv_["sm"], w_out, layer, None if stacks is None else stacks[2:], tm=bs["tm"])
    dqn, dkn, dv, dfg, dct, dcr = _fox_bwd(sv_["qn"], sv_["ka"], sv_["kb"], sv_["v"], sv_["o"], sv_["lse"], dmix, sv_["projm"],
                                      prm["qkb"], tq=bs["tq_big"], tk=bs["tk"])
    dsq, dsk, dsv, dsg = _sb_bwd(sv_["sq"], sv_["sk"], sv_["sv"], sv_["so"], dmix, sv_["projm"], tq=bs["tks"], tk=bs["tks"])
    dproj, dqg, dkg, dbf, dwp, dps = _prep_bwd(sv_["projm"], sv_["ffo"], dqn, dkn, dct, dcr, dv, dfg, dsq, dsk, dsv, dsg, dmix,
                                               sv_["pooled"], sv_["yp"], prm["qg"], prm["kg"], prm["bfp"], prm["wpd"], prm["ps"], ts=bs["ts"])
    stack_m, stack_f = _inproj_dw(sv_["h"], dproj, layer, None if stacks is None else stacks[:2], ts=bs["tm_proj"], tn=PROJ_TN)
    dx, dg = _inproj_dx(dproj, wt_all, layer, sv_["x"], prm["g"], dy, tm=min(256, bs["tm"]))
    grads = dict(
        norm_g=dg[0],
        b_f=_pair_unpad(dbf), q_norm_g=dqg.reshape(FOX_HEADS, HEAD_DIM).sum(0), k_norm_g=dkg.reshape(FOX_HEADS, HEAD_DIM).sum(0),
        w_pool=jnp.stack([dwp[HEAD_DIM * g:HEAD_DIM * (g + 1), HEAD_DIM * g:HEAD_DIM * (g + 1)] for g in range(4)]),
        pool_scale=dps[0])
    return dx, grads, (stack_m, stack_f, stack_o)


def _local_step(x, target, wt_all, w_out, norm_g, b_f, q_norm_g, k_norm_g, w_pool, pool_scale):
    S, D = x.shape
    bs = _blocks(S)
    prms = [_layer_params(norm_g[l], b_f[l], q_norm_g[l], k_norm_g[l], w_pool[l], pool_scale[l]) for l in range(DEPTH)]
    saved = []
    y = x
    for l in range(DEPTH):
        y, s_ = _layer_fwd(y, wt_all, w_out, l, prms[l], bs)
        saved.append(s_)
    dy, sq = _loss_head(y, target, tm=bs["tm"])
    loss = 0.5 * jnp.sum(sq) / D
    grads = [None] * DEPTH
    stacks = None
    for l in reversed(range(DEPTH)):
        dy, grads[l], stacks = _layer_bwd(dy, wt_all, w_out, prms[l], saved[l], bs, l, stacks)
    stacked = {k: jnp.stack([g[k] for g in grads]) for k in grads[0]}
    return loss, dy, stacked, stacks


SMALL = ("norm_g", "b_f", "q_norm_g", "k_norm_g", "w_pool", "pool_scale")


def _pack_small(gr):
    flat = jnp.concatenate([gr[k].reshape(-1) for k in SMALL])
    pad = (-flat.shape[0]) % (8 * LANES)
    return jnp.pad(flat, (0, pad)).reshape(-1, LANES)


def _unpack_small(packed, like):
    flat = packed.reshape(-1)
    out, off = {}, 0
    for k in SMALL:
        n = like[k].size
        out[k] = flat[off:off + n].reshape(like[k].shape)
        off += n
    return out


def kernel(x, norm_g, w_in, b_f, q_norm_g, k_norm_g, w_pool, pool_scale, w_out, loss_target, m_norm_g, m_w_in, m_b_f, m_q_norm_g, m_k_norm_g, m_w_pool, m_pool_scale, m_w_out, v_norm_g, v_w_in, v_b_f, v_q_norm_g, v_k_norm_g, v_w_pool, v_pool_scale, v_w_out):
    weights = dict(norm_g=norm_g, w_in=w_in, b_f=b_f, q_norm_g=q_norm_g, k_norm_g=k_norm_g, w_pool=w_pool, pool_scale=pool_scale, w_out=w_out)
    mom_m = dict(norm_g=m_norm_g, w_in=m_w_in, b_f=m_b_f, q_norm_g=m_q_norm_g, k_norm_g=m_k_norm_g, w_pool=m_w_pool, pool_scale=m_pool_scale, w_out=m_w_out)
    mom_v = dict(norm_g=v_norm_g, w_in=v_w_in, b_f=v_b_f, q_norm_g=v_q_norm_g, k_norm_g=v_k_norm_g, w_pool=v_w_pool, pool_scale=v_pool_scale, w_out=v_w_out)
    shard_cols = w_in.shape[2]
    shard_rows = w_out.shape[1]

    cols_first = lambda a: jnp.transpose(a, (2, 0, 1))
    w_in_t = cols_first(w_in)
    w_in_t_full, w_out_full = _gather_weights(w_in_t, w_out)
    wt_all = _to_aligned(w_in_t_full)
    loss, dx, gr, stacks = _local_step(x[0], loss_target[0], wt_all, w_out_full, norm_g, b_f, q_norm_g, k_norm_g, w_pool, pool_scale)
    loss = lax.psum(loss, ("x", "y", "c"))

    (g_in_mine, g_in_sib), g_w_out = _reduce_scatter(*stacks, shard_cols, shard_rows)
    small = _unpack_small(_all_reduce_small(_pack_small(gr)), {k: weights[k] for k in SMALL})
    grad_w = dict(small, w_out=g_w_out)

    names = ("norm_g", "w_in", "b_f", "q_norm_g", "k_norm_g", "w_pool", "pool_scale", "w_out")
    upd = {k: _adamw_nd(weights[k], grad_w[k], mom_m[k], mom_v[k]) for k in names if k != "w_in"}
    in_t = _adamw_halves(w_in_t, g_in_mine, g_in_sib, cols_first(mom_m["w_in"]), cols_first(mom_v["w_in"]))
    grad_w["w_in"], *upd["w_in"] = [jnp.transpose(a, (1, 2, 0)) for a in in_t]
    return (loss, dx[None], *[grad_w[k] for k in names], *[upd[k][0] for k in names], *[upd[k][1] for k in names], *[upd[k][2] for k in names])
```

```python
import functools

import jax
import jax.numpy as jnp
from jax import lax
from jax.experimental import pallas as pl
from jax.experimental.pallas import tpu as pltpu

F32 = jnp.float32
BF16 = jnp.bfloat16

DEPTH = 4
HEAD_DIM = 64
FOX_HEADS = 8
SB_HEADS = 4
FOX_W = FOX_HEADS * HEAD_DIM
SB_W = SB_HEADS * HEAD_DIM
POOL_W = 256
POOL_WINDOWS = (2, 4, 8, 16)
POOL_HALO = 16
D_MIX = FOX_W + POOL_W + SB_W
EPS = 1e-6
NEG = -1e30
QK_SCALE = HEAD_DIM ** -0.5

ORIG_FOX = 4 * FOX_W
ORIG_FF = ORIG_FOX
ORIG_REST = ORIG_FF + FOX_HEADS
D_IN = ORIG_REST + 2 * POOL_W + 4 * SB_W

C_FQ, C_FK, C_FV, C_FG = 0, FOX_W, 2 * FOX_W, 3 * FOX_W
C_PX = 4 * FOX_W
C_PG = C_PX + POOL_W
C_SQ = C_PG + POOL_W
C_SK, C_SV, C_SG = C_SQ + SB_W, C_SQ + 2 * SB_W, C_SQ + 3 * SB_W
PM = C_SG + SB_W
LANES = 128
LANE_SHIFT = 7
HEAD_SHIFT = 6
PW = PM + LANES
FF_STRIDE = 8
AUG = 3

ADAM_LR = 0.001
ADAM_B1 = 0.9
ADAM_B2 = 0.999
ADAM_EPS = 1e-08
ADAM_WD = 0.01
ADAM_STEP = 10

VMEM_LIMIT = 48 * 1024 * 1024
PROJ_TN = PM // 2


def _cparams(**kw):
    return pltpu.CompilerParams(vmem_limit_bytes=VMEM_LIMIT, **kw)


def _dot(a, b):
    return jnp.dot(a, b, preferred_element_type=F32)


def _dot_nt(a, b):
    return lax.dot_general(a, b, (((1,), (1,)), ((), ())), preferred_element_type=F32)


def _dot_tn(a, b):
    return lax.dot_general(a, b, (((0,), (0,)), ((), ())), preferred_element_type=F32)


def _split2(x):
    hi = x.astype(BF16)
    lo = (x - hi.astype(F32)).astype(BF16)
    return hi, lo


def _split3(x):
    hi = x.astype(BF16)
    r = x - hi.astype(F32)
    mid = r.astype(BF16)
    lo = (r - mid.astype(F32)).astype(BF16)
    return hi, mid, lo


def _dot_exact_rhs(x, m):
    hi, mid, lo = _split3(x)
    return _dot(hi, m) + _dot(mid, m) + _dot(lo, m)


def _dot_exact_lhs(m, x):
    hi, mid, lo = _split3(x)
    return _dot(m, hi) + _dot(m, mid) + _dot(m, lo)


def _sigmoid(x):
    return 1.0 / (1.0 + jnp.exp(-x))


def _silu_pair(x):
    s = _sigmoid(x)
    return x * s, s * (1.0 + x * (1.0 - s))


def _iota(shape, dim):
    return lax.broadcasted_iota(jnp.int32, shape, dim)


def _ones_where(cond):
    return jnp.where(cond, 1.0, 0.0).astype(BF16)


GROUP_SLAB = 256


def _head_blockdiag():
    rows, cols = _iota((2 * GROUP_SLAB, GROUP_SLAB), 0) & (GROUP_SLAB - 1), _iota((2 * GROUP_SLAB, GROUP_SLAB), 1)
    return _ones_where((rows >> HEAD_SHIFT) == (cols >> HEAD_SHIFT))


def _group_sum(x, bd):
    hi, lo = _split2(x)
    slabs = [_dot(jnp.concatenate([hi[:, s:s + GROUP_SLAB], lo[:, s:s + GROUP_SLAB]], axis=1), bd) for s in range(0, x.shape[1], GROUP_SLAB)]
    return jnp.concatenate(slabs, axis=1)


def _lane_pick(x, lane_idx, lane):
    return jnp.sum(jnp.where(lane_idx == lane, x, 0.0), axis=1, keepdims=True)


def _inproj(x, g, wt_all, layer, *, tm, tn):
    S, D = x.shape
    nj = PM // tn

    def body(x_ref, g_ref, w_ref, wff_ref, proj_ref, ff_ref, h_ref):
        @pl.when(pl.program_id(1) == 0)
        def _():
            xf = x_ref[...]
            ms = jnp.mean(xf * xf, axis=-1, keepdims=True)
            h = (xf * lax.rsqrt(ms + EPS) * g_ref[...]).astype(BF16)
            h_ref[...] = h
            ff_ref[...] = _dot_nt(h, wff_ref[...])

        proj_ref[...] = _dot_nt(h_ref[...], w_ref[...])

    return pl.pallas_call(
        body, name="inproj", grid=(S // tm, nj),
        in_specs=[pl.BlockSpec((tm, D), lambda i, j: (i, 0)),
                  pl.BlockSpec((1, D), lambda i, j: (0, 0)),
                  pl.BlockSpec((None, tn, D), lambda i, j: (layer, j, 0)),
                  pl.BlockSpec((None, LANES, D), lambda i, j: (layer, PM // LANES, 0))],
        out_specs=[pl.BlockSpec((tm, tn), lambda i, j: (i, j)),
                   pl.BlockSpec((tm, LANES), lambda i, j: (i, 0)),
                   pl.BlockSpec((tm, D), lambda i, j: (i, 0))],
        out_shape=[jax.ShapeDtypeStruct((S, PM), F32), jax.ShapeDtypeStruct((S, LANES), F32),
                   jax.ShapeDtypeStruct((S, D), BF16)],
        compiler_params=_cparams(dimension_semantics=("arbitrary", "arbitrary")),
    )(x, g, wt_all, wt_all)


def _pool_group_select(lane_group, vals):
    return jnp.where(lane_group == 0, vals[0], jnp.where(lane_group == 1, vals[1], jnp.where(lane_group == 2, vals[2], vals[3])))


def _prep(projm, ffo, qg, kg, bfp, wpd, ps, *, ts):
    S = projm.shape[0]
    nb = S // ts
    hb = ts // POOL_HALO

    def body(fq_ref, fk_ref, fv_ref, pp_ref, halo_ref, ff_ref, sq_ref, sk_ref, sv_ref,
             qg_ref, kg_ref, bf_ref, wpd_ref, ps_ref,
             qn_ref, ka_ref, kb_ref, v_ref, sqo_ref, sko_ref, svo_ref, pooled_ref, yp_ref, pm_ref,
             carry_ref, c_ref, buf_ref):
        i = pl.program_id(0)
        bd = _head_blockdiag()
        normed = []
        for src, g_ref in ((fq_ref, qg_ref), (fk_ref, kg_ref)):
            q = src[...]
            ss = _group_sum(q * q, bd)
            normed.append(q * lax.rsqrt(ss * (1.0 / HEAD_DIM) + EPS) * g_ref[...])
        qn_ref[...] = (normed[0] * QK_SCALE).astype(BF16)
        kn = normed[1]
        v_ref[...] = fv_ref[...].astype(BF16)
        sqo_ref[...] = (sq_ref[...] * QK_SCALE).astype(BF16)
        sko_ref[...] = sk_ref[...].astype(BF16)
        svo_ref[...] = sv_ref[...].astype(BF16)

        @pl.when(i == 0)
        def _():
            carry_ref[...] = jnp.zeros_like(carry_ref)

        z = ff_ref[...] + bf_ref[...]
        lf = jnp.minimum(z, 0.0) - jnp.log(1.0 + jnp.exp(-jnp.abs(z)))
        tri = _ones_where(_iota((ts, ts), 1) <= _iota((ts, ts), 0))
        c = _dot_exact_lhs(tri, lf) + carry_ref[...]
        c_ref[...] = c
        carry_ref[...] = c_ref[ts - 1:ts, :]
        parts = jnp.concatenate(_split3(-c), axis=1)
        row = _iota((AUG * LANES, FOX_W), 0)
        col = _iota((AUG * LANES, FOX_W), 1)
        part, src = row >> LANE_SHIFT, row & (LANES - 1)
        pair, off = col >> LANE_SHIFT, col & (LANES - 1)
        sel_a = _ones_where((src == FF_STRIDE * pair) & (off == HEAD_DIM + part))
        sel_b = _ones_where((src == FF_STRIDE * pair + 1) & (off == part))
        first_half = (_iota((1, FOX_W), 1) & HEAD_DIM) == 0
        ka_ref[...] = jnp.where(first_half, kn, _dot(parts, sel_a)).astype(BF16)
        kb_ref[...] = jnp.where(first_half, _dot(parts, sel_b), kn).astype(BF16)

        x = pp_ref[:, 0:POOL_W]
        pg = pp_ref[:, POOL_W:2 * POOL_W]
        halo = jnp.where(i > 0, halo_ref[:, 0:POOL_W], 0.0)
        buf_ref[0:POOL_HALO, :] = halo
        buf_ref[POOL_HALO:POOL_HALO + ts, :] = x
        acc = x
        snaps = []
        for d in range(1, POOL_HALO):
            acc = acc + buf_ref[pl.ds(POOL_HALO - d, ts), :]
            if d + 1 in POOL_WINDOWS:
                snaps.append(acc)
        lane_group = _iota((1, POOL_W), 1) >> HEAD_SHIFT
        wsum = _pool_group_select(lane_group, snaps)
        wlen = _pool_group_select(lane_group, [float(w) for w in POOL_WINDOWS])
        tpos = (i * ts + _iota((ts, 1), 0) + 1).astype(F32)
        pooled = wsum / jnp.minimum(tpos, wlen) - x
        pb = pooled.astype(BF16)
        pooled_ref[...] = pb
        yp = _dot(pb, wpd_ref[...])
        yp_ref[...] = yp
        pm_ref[...] = (yp * ps_ref[...] * (pg * _sigmoid(pg))).astype(BF16)

    blk = lambda w, c: pl.BlockSpec((ts, w), lambda i: (i, c))
    full = lambda a: pl.BlockSpec(a.shape, lambda i: (0,) * a.ndim)
    out_shapes = [
        jax.ShapeDtypeStruct((S, FOX_W), BF16), jax.ShapeDtypeStruct((S, FOX_W), BF16), jax.ShapeDtypeStruct((S, FOX_W), BF16),
        jax.ShapeDtypeStruct((S, FOX_W), BF16),
        jax.ShapeDtypeStruct((S, SB_W), BF16), jax.ShapeDtypeStruct((S, SB_W), BF16), jax.ShapeDtypeStruct((S, SB_W), BF16),
        jax.ShapeDtypeStruct((S, POOL_W), BF16), jax.ShapeDtypeStruct((S, POOL_W), F32), jax.ShapeDtypeStruct((S, POOL_W), BF16),
    ]
    out_specs = [
        blk(FOX_W, 0), blk(FOX_W, 0), blk(FOX_W, 0), blk(FOX_W, 0),
        blk(SB_W, 0), blk(SB_W, 0), blk(SB_W, 0),
        blk(POOL_W, 0), blk(POOL_W, 0), blk(POOL_W, 0),
    ]
    return pl.pallas_call(
        body, name="prep", grid=(nb,),
        in_specs=[blk(FOX_W, C_FQ // FOX_W), blk(FOX_W, C_FK // FOX_W), blk(FOX_W, C_FV // FOX_W), blk(2 * POOL_W, C_PX // (2 * POOL_W)),
                  pl.BlockSpec((POOL_HALO, 2 * POOL_W), lambda i: (jnp.maximum(i * hb - 1, 0), C_PX // (2 * POOL_W))),
                  blk(LANES, 0),
                  blk(SB_W, C_SQ // SB_W), blk(SB_W, C_SK // SB_W), blk(SB_W, C_SV // SB_W),
                  full(qg), full(kg), full(bfp), full(wpd), full(ps)],
        out_specs=out_specs, out_shape=out_shapes,
        scratch_shapes=[pltpu.VMEM((1, LANES), F32), pltpu.VMEM((ts, LANES), F32), pltpu.VMEM((ts + POOL_HALO, POOL_W), F32)],
        compiler_params=_cparams(dimension_semantics=("arbitrary",)),
    )(projm, projm, projm, projm, projm, ffo, projm, projm, projm, qg, kg, bfp, wpd, ps)


def _pair_masks(x):
    ma = _iota((1, LANES), 1) < HEAD_DIM
    zero = jnp.zeros_like(x)
    return jnp.where(ma, x, zero), jnp.where(ma, zero, x)


DIAG_TILE = 256


def _diag_tiles(tq, size=DIAG_TILE):
    size = min(tq, size)
    return [(t * size, size) for t in range(tq // size)]


def _put_rows(old, new, r0):
    return new if r0 == 0 else jnp.concatenate([old[:r0], new], axis=0)


def _aug_queries(q):
    lane = _iota((1, LANES), 1)
    one = jnp.ones_like(q)
    zero = jnp.zeros_like(q)
    qa = jnp.where(lane < HEAD_DIM, q, jnp.where(lane < HEAD_DIM + AUG, one, zero))
    qb = jnp.where(lane >= HEAD_DIM, q, jnp.where(lane < AUG, one, zero))
    return qa, qb


EXP_DEAD = -105.0
PACK = 16


def _fox_walk_left(nfull, tk, block, carry, k_refs, qk_bound, row_floor):
    lane = _iota((1, LANES), 1)

    def alive(h, jj, c):
        k0 = pl.multiple_of(jnp.maximum(nfull - 1 - jj, 0) * tk + tk - PACK, PACK)
        last = k_refs[h][pl.ds(k0, PACK), :].astype(F32)
        lo = HEAD_DIM if h == 0 else 0
        negc = jnp.sum(jnp.where((lane >= lo) & (lane < lo + AUG), last, 0.0), axis=1, keepdims=True)
        return qk_bound + jnp.max(negc) - row_floor(c)[h] >= EXP_DEAD

    def walk(heads, jj0, c0):
        def go_on(state):
            jj, c = state
            ok = jj < nfull
            for h in heads:
                ok = ok & alive(h, jj, c)
            return ok

        def step(state):
            jj, c = state
            return jj + 1, block(pl.multiple_of((nfull - 1 - jj) * tk, tk), tk, 0, c, False, heads)

        return lax.while_loop(go_on, step, (jj0, c0))

    jj_pair, carry = walk((0, 1), jnp.int32(0), carry)
    carry = walk((0,), jj_pair, carry)[1]
    return walk((1,), jj_pair, carry)[1]


def _fox_fwd(qn, ka, kb, v, projm, qkb, *, tq, tk):
    S = qn.shape[0]
    npair = FOX_HEADS // 2

    def body(q_ref, ka_ref, kb_ref, v_ref, fg_ref, qkb_ref, o_ref, lse_ref, fm_ref):
        qi = pl.program_id(1)
        lane = _iota((1, LANES), 1)
        ma = lane < HEAD_DIM
        qaug = _aug_queries(q_ref[...])
        k_refs = (ka_ref, kb_ref)

        def block(k0, tkl, r0, carry, masked, heads=(0, 1)):
            vb = v_ref[pl.ds(k0, tkl), :]
            if masked:
                mask = (k0 + _iota((tq - r0, tkl), 1)) <= (qi * tq + r0 + _iota((tq - r0, tkl), 0))
            scores = {h: _dot_nt(qaug[h][r0:], k_refs[h][pl.ds(k0, tkl), :]) for h in heads}
            new = list(carry)
            for h in heads:
                m, l, acc = [x[r0:] for x in carry[h]]
                s = jnp.where(mask, scores[h], NEG) if masked else scores[h]
                m_new = jnp.maximum(m, jnp.max(s, axis=1, keepdims=True))
                alpha = jnp.exp(m - m_new)
                p = jnp.exp(s - m_new)
                sub = (m_new, alpha * l + jnp.sum(p, axis=1, keepdims=True), alpha * acc + _dot(p.astype(BF16), vb))
                new[h] = tuple(_put_rows(old, x, r0) for old, x in zip(carry[h], sub))
            return tuple(new)

        carry = tuple((jnp.full((tq, 1), NEG, F32), jnp.zeros((tq, 1), F32), jnp.zeros((tq, LANES), F32)) for _ in range(2))
        for off, size in _diag_tiles(tq, tq):
            carry = block(pl.multiple_of(qi * tq + off, size), size, off, carry, True)
        carry = _fox_walk_left((qi * tq) // tk, tk, block, carry, k_refs, jnp.max(qkb_ref[...]),
                               lambda c: (jnp.min(c[0][0]), jnp.min(c[1][0])))
        (ma_, la, acca), (mb_, lb, accb) = carry
        o = jnp.where(ma, acca / la, accb / lb)
        o_ref[...] = o
        lse_ref[...] = jnp.where(ma, ma_ + jnp.log(la), mb_ + jnp.log(lb))
        fg = fg_ref[...]
        fm_ref[...] = (o * (fg * _sigmoid(fg))).astype(BF16)

    qblk = pl.BlockSpec((tq, LANES), lambda p, i: (i, p))
    kvblk = pl.BlockSpec((S, LANES), lambda p, i: (0, p))
    return pl.pallas_call(
        body, name="fox_fwd", grid=(npair, S // tq),
        in_specs=[qblk, kvblk, kvblk, kvblk,
                  pl.BlockSpec((tq, LANES), lambda p, i: (i, C_FG // LANES + p)),
                  pl.BlockSpec((1, LANES), lambda p, i: (0, 0))],
        out_specs=[qblk, qblk, qblk],
        out_shape=[jax.ShapeDtypeStruct((S, FOX_W), F32), jax.ShapeDtypeStruct((S, FOX_W), F32), jax.ShapeDtypeStruct((S, FOX_W), BF16)],
        compiler_params=_cparams(dimension_semantics=("arbitrary", "arbitrary")),
    )(qn, ka, kb, v, projm, qkb)


def _suffix_sums(x, tmat2):
    return _dot(jnp.concatenate(_split2(x), axis=1), tmat2)


def _suffix_matrix(tk, inclusive):
    rr, cc = _iota((2 * tk, tk), 0) & (tk - 1), _iota((2 * tk, tk), 1)
    return _ones_where(rr >= cc) if inclusive else _ones_where(rr > cc)


def _sb_scores(qh, kb, causal, tmat2, r_runs):
    heads = range(2)
    zs = [_dot_nt(qh[h], kb) for h in heads]
    nsps = [jnp.minimum(-z, 0.0) - jnp.log(1.0 + jnp.exp(-jnp.abs(z))) for z in zs]
    lbs = nsps if causal is None else [jnp.where(causal, n, 0.0) for n in nsps]
    rins = [_suffix_sums(lb, tmat2) for lb in lbs]
    args = [zs[h] + lbs[h] + (rins[h] + r_runs[h]) for h in heads]
    a_s = [jnp.exp(arg if causal is None else jnp.where(causal, arg, NEG)) for arg in args]
    return zs, nsps, lbs, a_s


def _sb_walk_left(nfull, tk, block, carry, running_sums):
    def alive(state):
        jj, c = state
        ra, rb = running_sums(c)
        return (jj < nfull) & (jnp.max(jnp.maximum(ra, rb)) >= EXP_DEAD)

    def step(state):
        jj, c = state
        return jj + 1, block(pl.multiple_of((nfull - 1 - jj) * tk, tk), 0, c, False)

    return lax.while_loop(alive, step, (jnp.int32(0), carry))[1]


def _sb_fwd(sq, sk, sv, projm, *, tq, tk):
    S = sq.shape[0]
    npair = SB_HEADS // 2

    def body(q_ref, k_ref, v_ref, sg_ref, o_ref, sm_ref):
        qi = pl.program_id(1)
        lane = _iota((1, LANES), 1)
        ma = lane < HEAD_DIM
        qh = _pair_masks(q_ref[...])
        tmat2 = _suffix_matrix(tk, inclusive=False)
        nfull = (qi * tq) // tk

        def block(k0, r0, carry, masked):
            nr = tq - r0
            kb = k_ref[pl.ds(k0, tk), :]
            vb = v_ref[pl.ds(k0, tk), :]
            causal = (k0 + _iota((nr, tk), 1)) < (qi * tq + r0 + _iota((nr, tk), 0)) if masked else None
            _, _, lbs, a_s = _sb_scores([q[r0:] for q in qh], kb, causal, tmat2, [carry[h][0][r0:] for h in range(2)])
            pv = _dot(jnp.concatenate([a.astype(BF16) for a in a_s], axis=0), vb)
            return tuple((_put_rows(carry[h][0], carry[h][0][r0:] + jnp.sum(lbs[h], axis=1, keepdims=True), r0),
                          _put_rows(carry[h][1], carry[h][1][r0:] + pv[h * nr:(h + 1) * nr], r0)) for h in range(2))

        carry = tuple((jnp.zeros((tq, 1), F32), jnp.zeros((tq, LANES), F32)) for _ in range(2))
        for off, size in reversed(_diag_tiles(tq)):
            assert size == tk
            carry = block(pl.multiple_of(qi * tq + off, tk), off, carry, True)
        (_, acca), (_, accb) = _sb_walk_left(nfull, tk, block, carry, lambda c: (c[0][0], c[1][0]))
        o = jnp.where(ma, acca, accb)
        o_ref[...] = o
        sg = sg_ref[...]
        sm_ref[...] = (o * (sg * _sigmoid(sg))).astype(BF16)

    qblk = pl.BlockSpec((tq, LANES), lambda p, i: (i, p))
    kvblk = pl.BlockSpec((S, LANES), lambda p, i: (0, p))
    return pl.pallas_call(
        body, name="sb_fwd", grid=(npair, S // tq),
        in_specs=[qblk, kvblk, kvblk, pl.BlockSpec((tq, LANES), lambda p, i: (i, C_SG // LANES + p))],
        out_specs=[qblk, qblk],
        out_shape=[jax.ShapeDtypeStruct((S, SB_W), F32), jax.ShapeDtypeStruct((S, SB_W), BF16)],
        compiler_params=_cparams(dimension_semantics=("arbitrary", "arbitrary")),
    )(sq, sk, sv, projm)


def _outproj(x, fm, pm, sm, w_out, layer, *, tm):
    S, D = x.shape

    def body(x_ref, fm_ref, pm_ref, sm_ref, w_ref, y_ref):
        y = x_ref[...] + _dot(fm_ref[...], w_ref[0:FOX_W, :])
        y = y + _dot(pm_ref[...], w_ref[FOX_W:FOX_W + POOL_W, :])
        y_ref[...] = y + _dot(sm_ref[...], w_ref[FOX_W + POOL_W:D_MIX, :])

    row = lambda w: pl.BlockSpec((tm, w), lambda i: (i, 0))
    return pl.pallas_call(
        body, name="outproj", grid=(S // tm,),
        in_specs=[row(D), row(FOX_W), row(POOL_W), row(SB_W), pl.BlockSpec((None, D_MIX, D), lambda i: (layer, 0, 0))],
        out_specs=row(D), out_shape=jax.ShapeDtypeStruct((S, D), F32),
        compiler_params=_cparams(dimension_semantics=("arbitrary",)),
    )(x, fm, pm, sm, w_out)


def _loss_head(y, target, *, tm):
    S, D = y.shape

    def body(y_ref, t_ref, dy_ref, sq_ref):
        @pl.when(pl.program_id(0) == 0)
        def _():
            sq_ref[...] = jnp.zeros_like(sq_ref)

        d = y_ref[...] - t_ref[...]
        dy_ref[...] = d * (1.0 / D)
        sq_ref[...] += jnp.sum(d * d, axis=0, keepdims=True)

    row = pl.BlockSpec((tm, D), lambda i: (i, 0))
    return pl.pallas_call(
        body, name="loss_head", grid=(S // tm,),
        in_specs=[row, row], out_specs=[row, pl.BlockSpec((1, D), lambda i: (0, 0))],
        out_shape=[jax.ShapeDtypeStruct((S, D), F32), jax.ShapeDtypeStruct((1, D), F32)],
        compiler_params=_cparams(dimension_semantics=("arbitrary",)),
    )(y, target)


def _outproj_bwd(dy, fm, pm, sm, w_out, layer, stacks, *, tm):
    S, D = dy.shape

    def body(dy_ref, fm_ref, pm_ref, sm_ref, w_ref, dm_ref, dw_ref):
        @pl.when(pl.program_id(0) == 0)
        def _():
            dw_ref[...] = jnp.zeros_like(dw_ref)

        dyb = dy_ref[...].astype(BF16)
        dm_ref[...] = _dot_nt(dyb, w_ref[...])
        dw_ref[0:FOX_W, :] += _dot_tn(fm_ref[...], dyb)
        dw_ref[FOX_W:FOX_W + POOL_W, :] += _dot_tn(pm_ref[...], dyb)
        dw_ref[FOX_W + POOL_W:D_MIX, :] += _dot_tn(sm_ref[...], dyb)

    row = lambda w: pl.BlockSpec((tm, w), lambda i: (i, 0))
    wspec = pl.BlockSpec((None, D_MIX, D), lambda i: (layer, 0, 0))
    return _stack_call(
        body, "outproj_bwd", (S // tm,), [row(D), row(FOX_W), row(POOL_W), row(SB_W), wspec], (dy, fm, pm, sm, w_out),
        [pl.BlockSpec((None, D_MIX, D), lambda i: (layer, 0, 0))], [(D_MIX, D)], stacks,
        plain_specs=[row(D_MIX)], plain_shapes=[jax.ShapeDtypeStruct((S, D_MIX), F32)],
        compiler_params=_cparams(dimension_semantics=("arbitrary",)))


def _fox_bwd(qn, ka, kb, v, o, lse, dmix, projm, qkb, *, tq, tk):
    S = qn.shape[0]
    npair = FOX_HEADS // 2

    def body(q_ref, ka_ref, kb_ref, v_ref, o_ref, lse_ref, dm_ref, fg_ref, qkb_ref,
             dq_ref, dk_ref, dv_ref, dfg_ref, dct_ref, dcr_ref):
        qi = pl.program_id(1)

        @pl.when(qi == 0)
        def _():
            dk_ref[...] = jnp.zeros_like(dk_ref)
            dv_ref[...] = jnp.zeros_like(dv_ref)
            dct_ref[...] = jnp.zeros_like(dct_ref)

        lane = _iota((1, LANES), 1)
        ma = lane < HEAD_DIM
        qh = _pair_masks(q_ref[...])
        qaug = _aug_queries(q_ref[...])
        k_refs = (ka_ref, kb_ref)
        lsev = lse_ref[...]
        lse = (_lane_pick(lsev, lane, 0), _lane_pick(lsev, lane, HEAD_DIM))
        fg = fg_ref[...]
        silu, dsilu = _silu_pair(fg)
        dm = dm_ref[...]
        ov = o_ref[...]
        do = dm * silu
        dfg_ref[...] = dm * ov * dsilu
        dd = do * ov
        dsum = (jnp.sum(jnp.where(ma, dd, 0.0), axis=1, keepdims=True), jnp.sum(jnp.where(ma, 0.0, dd), axis=1, keepdims=True))
        doh = _pair_masks(do.astype(BF16))

        def block(k0, tkl, r0, carry, masked, heads=(0, 1)):
            vb = v_ref[pl.ds(k0, tkl), :]
            if masked:
                mask = (k0 + _iota((tq - r0, tkl), 1)) <= (qi * tq + r0 + _iota((tq - r0, tkl), 0))
            kaugs = {h: k_refs[h][pl.ds(k0, tkl), :] for h in heads}
            scores = {h: _dot_nt(qaug[h][r0:], kaugs[h]) for h in heads}
            dps = {h: _dot_nt(doh[h][r0:], vb) for h in heads}
            ps, dss = [], []
            rows = [carry[1], carry[2]]
            for h in heads:
                s = jnp.where(mask, scores[h], NEG) if masked else scores[h]
                p = jnp.exp(s - lse[h][r0:])
                dsf = p * (dps[h] - dsum[h][r0:])
                dct_ref[0, h:h + 1, pl.ds(k0, tkl)] -= jnp.sum(dsf, axis=0, keepdims=True)
                rows[h] = _put_rows(carry[1 + h], carry[1 + h][r0:] + jnp.sum(dsf, axis=1, keepdims=True), r0)
                ps.append(p.astype(BF16))
                dss.append(dsf.astype(BF16))
            dv_ref[pl.ds(k0, tkl), :] += _dot_tn(jnp.concatenate(ps, axis=0), jnp.concatenate([doh[h][r0:] for h in heads], axis=0))
            dk_ref[pl.ds(k0, tkl), :] += _dot_tn(jnp.concatenate(dss, axis=0), jnp.concatenate([qh[h][r0:] for h in heads], axis=0))
            kh = jnp.concatenate([_pair_masks(kaugs[h])[h] for h in heads], axis=0)
            dq = _put_rows(carry[0], carry[0][r0:] + _dot(jnp.concatenate(dss, axis=1), kh), r0)
            return (dq, rows[0], rows[1])

        zcol = jnp.zeros((tq, 1), F32)
        carry = (jnp.zeros((tq, LANES), F32), zcol, zcol)
        for off, size in _diag_tiles(tq):
            carry = block(pl.multiple_of(qi * tq + off, size), size, off, carry, True)
        floors = (jnp.min(lse[0]), jnp.min(lse[1]))
        dq, rowa, rowb = _fox_walk_left((qi * tq) // tk, tk, block, carry, k_refs, jnp.max(qkb_ref[...]), lambda c: floors)
        dq_ref[...] = dq * QK_SCALE
        dcr_ref[0] = jnp.where(ma, rowa, rowb)

    qblk = pl.BlockSpec((tq, LANES), lambda p, i: (i, p))
    kvblk = pl.BlockSpec((S, LANES), lambda p, i: (0, p))
    f32out = jax.ShapeDtypeStruct((S, FOX_W), F32)
    ctblk = pl.BlockSpec((1, FF_STRIDE, S), lambda p, i: (p, 0, 0))
    return pl.pallas_call(
        body, name="fox_bwd", grid=(npair, S // tq),
        in_specs=[qblk, kvblk, kvblk, kvblk, qblk, qblk, qblk,
                  pl.BlockSpec((tq, LANES), lambda p, i: (i, C_FG // LANES + p)),
                  pl.BlockSpec((1, LANES), lambda p, i: (0, 0))],
        out_specs=[qblk, kvblk, kvblk, qblk, ctblk, pl.BlockSpec((1, tq, LANES), lambda p, i: (p, i, 0))],
        out_shape=[f32out, f32out, f32out, f32out, jax.ShapeDtypeStruct((npair, FF_STRIDE, S), F32),
                   jax.ShapeDtypeStruct((npair, S, LANES), F32)],
        compiler_params=_cparams(dimension_semantics=("arbitrary", "arbitrary")),
    )(qn, ka, kb, v, o, lse, dmix, projm, qkb)


def _sb_bwd(sq, sk, sv, o, dmix, projm, *, tq, tk):
    S = sq.shape[0]
    npair = SB_HEADS // 2
    mix0 = (FOX_W + POOL_W) // LANES

    def body(q_ref, k_ref, v_ref, o_ref, dm_ref, sg_ref, dq_ref, dk_ref, dv_ref, dsg_ref):
        qi = pl.program_id(1)

        @pl.when(qi == 0)
        def _():
            dk_ref[...] = jnp.zeros_like(dk_ref)
            dv_ref[...] = jnp.zeros_like(dv_ref)

        lane = _iota((1, LANES), 1)
        ma = lane < HEAD_DIM
        qh = _pair_masks(q_ref[...])
        sg = sg_ref[...]
        silu, dsilu = _silu_pair(sg)
        dm = dm_ref[...]
        ov = o_ref[...]
        do = dm * silu
        dsg_ref[...] = dm * ov * dsilu
        dob = do.astype(BF16)
        dd = dob.astype(F32) * ov
        dsum = (jnp.sum(jnp.where(ma, dd, 0.0), axis=1, keepdims=True), jnp.sum(jnp.where(ma, 0.0, dd), axis=1, keepdims=True))
        doh = _pair_masks(dob)
        tmat2 = _suffix_matrix(tk, inclusive=False)
        tmat2_inc = _suffix_matrix(tk, inclusive=True)
        nfull = (qi * tq) // tk

        def block(k0, r0, carry, masked):
            nr = tq - r0
            kb = k_ref[pl.ds(k0, tk), :]
            vb = v_ref[pl.ds(k0, tk), :]
            kh = _pair_masks(kb)
            causal = (k0 + _iota((nr, tk), 1)) < (qi * tq + r0 + _iota((nr, tk), 0)) if masked else None
            heads = range(2)
            qs = [q[r0:] for q in qh]
            dos = [d[r0:] for d in doh]
            das = [_dot_nt(dos[h], vb) for h in heads]
            zs, nsps, lbs, a_s = _sb_scores(qs, kb, causal, tmat2, [carry[h][0][r0:] for h in heads])
            abs_ = [a.astype(BF16) for a in a_s]
            us = [abs_[h].astype(F32) * das[h] for h in heads]
            uins = [_suffix_sums(u, tmat2_inc) for u in us]
            dzs = []
            for h in heads:
                cum_u = dsum[h][r0:] - (uins[h] + carry[h][1][r0:])
                dz = us[h] * jnp.exp(nsps[h]) - jnp.exp(zs[h] + nsps[h]) * cum_u
                if masked:
                    dz = jnp.where(causal, dz, 0.0)
                dzs.append(dz.astype(BF16))
            dv_ref[pl.ds(k0, tk), :] += _dot_tn(jnp.concatenate(abs_, axis=0), jnp.concatenate(dos, axis=0))
            dk_ref[pl.ds(k0, tk), :] += _dot_tn(jnp.concatenate(dzs, axis=0), jnp.concatenate(qs, axis=0))
            dq = _put_rows(carry[2], carry[2][r0:] + _dot(jnp.concatenate(dzs, axis=1), jnp.concatenate(kh, axis=0)), r0)
            new = [(_put_rows(carry[h][0], carry[h][0][r0:] + jnp.sum(lbs[h], axis=1, keepdims=True), r0),
                    _put_rows(carry[h][1], carry[h][1][r0:] + jnp.sum(us[h], axis=1, keepdims=True), r0)) for h in heads]
            return (new[0], new[1], dq)

        zcol = jnp.zeros((tq, 1), F32)
        carry = ((zcol, zcol), (zcol, zcol), jnp.zeros((tq, LANES), F32))
        for off, size in reversed(_diag_tiles(tq)):
            assert size == tk
            carry = block(pl.multiple_of(qi * tq + off, tk), off, carry, True)
        dq = _sb_walk_left(nfull, tk, block, carry, lambda c: (c[0][0], c[1][0]))[2]
        dq_ref[...] = dq * QK_SCALE

    qblk = pl.BlockSpec((tq, LANES), lambda p, i: (i, p))
    kvblk = pl.BlockSpec((S, LANES), lambda p, i: (0, p))
    f32out = jax.ShapeDtypeStruct((S, SB_W), F32)
    return pl.pallas_call(
        body, name="sb_bwd", grid=(npair, S // tq),
        in_specs=[qblk, kvblk, kvblk, qblk,
                  pl.BlockSpec((tq, LANES), lambda p, i: (i, mix0 + p)),
                  pl.BlockSpec((tq, LANES), lambda p, i: (i, C_SG // LANES + p))],
        out_specs=[qblk, kvblk, kvblk, qblk],
        out_shape=[f32out, f32out, f32out, f32out],
        compiler_params=_cparams(dimension_semantics=("arbitrary", "arbitrary")),
    )(sq, sk, sv, o, dmix, projm)


def _prep_bwd(projm, ffo, dqn, dkn, dct, dcr, dv, dfg, dsq, dsk, dsv, dsg, dmix, pooled, yp, qg, kg, bfp, wpd, ps, *, ts):
    S = projm.shape[0]
    nb = S // ts
    hb = ts // POOL_HALO
    npair = FOX_HEADS // 2
    last_halo = S // POOL_HALO - 1

    def body(fq_ref, fk_ref, pp_ref, pph_ref, ff_ref,
             dqn_ref, dkn_ref, dct_ref, dcr_ref, dv_ref, dfg_ref, dsq_ref, dsk_ref, dsv_ref, dsg_ref,
             dmp_ref, dmh_ref, pooled_ref, yp_ref, qg_ref, kg_ref, bf_ref, wpd_ref, ps_ref,
             dp_ref, dqg_ref, dkg_ref, dbf_ref, dwp_ref, dps_ref,
             carry_ref, dl_ref, buf_ref, dct_s):
        i = pl.program_id(0)
        blk = nb - 1 - i

        @pl.when(i == 0)
        def _():
            carry_ref[...] = jnp.zeros_like(carry_ref)
            dqg_ref[...] = jnp.zeros_like(dqg_ref)
            dkg_ref[...] = jnp.zeros_like(dkg_ref)
            dbf_ref[...] = jnp.zeros_like(dbf_ref)
            dwp_ref[...] = jnp.zeros_like(dwp_ref)
            dps_ref[...] = jnp.zeros_like(dps_ref)

        bd = _head_blockdiag()
        for raw_ref, g_ref, dn, dg_ref, col in ((fq_ref, qg_ref, dqn_ref[...], dqg_ref, C_FQ), (fk_ref, kg_ref, dkn_ref[...], dkg_ref, C_FK)):
            q = raw_ref[...]
            rstd = lax.rsqrt(_group_sum(q * q, bd) * (1.0 / HEAD_DIM) + EPS)
            xhat = q * rstd
            dg_ref[...] += jnp.sum(dn * xhat, axis=0, keepdims=True)
            dyg = dn * g_ref[...]
            mean = _group_sum(dyg * xhat, bd) * (1.0 / HEAD_DIM)
            dp_ref[:, col:col + FOX_W] = (rstd * (dyg - xhat * mean)).astype(BF16)
        dp_ref[:, C_FV:C_FV + FOX_W] = dv_ref[...].astype(BF16)
        dp_ref[:, C_FG:C_FG + FOX_W] = dfg_ref[...].astype(BF16)
        dp_ref[:, C_SQ:C_SQ + SB_W] = dsq_ref[...].astype(BF16)
        dp_ref[:, C_SK:C_SK + SB_W] = dsk_ref[...].astype(BF16)
        dp_ref[:, C_SV:C_SV + SB_W] = dsv_ref[...].astype(BF16)
        dp_ref[:, C_SG:C_SG + SB_W] = dsg_ref[...].astype(BF16)

        dct_s[...] = jnp.zeros_like(dct_s)
        for p in range(npair):
            dct_s[FF_STRIDE * p:FF_STRIDE * (p + 1), :] = dct_ref[p]
        dc = dct_s[...].T
        lane = _iota((1, LANES), 1)
        for p in range(npair):
            dcr = dcr_ref[p]
            dc = dc + jnp.where(lane == FF_STRIDE * p, _lane_pick(dcr, lane, 0), 0.0)
            dc = dc + jnp.where(lane == FF_STRIDE * p + 1, _lane_pick(dcr, lane, HEAD_DIM), 0.0)
        triu = _ones_where(_iota((ts, ts), 1) >= _iota((ts, ts), 0))
        dlf = _dot_exact_lhs(triu, dc) + carry_ref[...]
        dl_ref[...] = dlf
        carry_ref[...] = dl_ref[0:1, :]
        z = ff_ref[...] + bf_ref[...]
        dff = dlf * (1.0 / (1.0 + jnp.exp(z)))
        dbf_ref[...] += jnp.sum(dff, axis=0, keepdims=True)
        dp_ref[:, PM:PW] = dff.astype(BF16)

        psv = ps_ref[...]
        wpdv = wpd_ref[...]
        lane_group = _iota((1, POOL_W), 1) >> HEAD_SHIFT
        wlen = _pool_group_select(lane_group, [float(w) for w in POOL_WINDOWS])
        pg = pp_ref[:, POOL_W:2 * POOL_W]
        silu, dsilu = _silu_pair(pg)
        dmp = dmp_ref[...]
        ypv = yp_ref[...]
        dp_ref[:, C_PG:C_PG + POOL_W] = (dmp * (ypv * psv) * dsilu).astype(BF16)
        dps_ref[...] += jnp.sum(dmp * silu * ypv, axis=0, keepdims=True)
        dyp = (dmp * psv * silu).astype(BF16)
        dwp_ref[...] += _dot_tn(pooled_ref[...], dyp)
        dpooled = _dot_nt(dyp, wpdv)
        pgh = pph_ref[:, POOL_W:2 * POOL_W]
        dyph = (dmh_ref[...] * psv * (pgh * _sigmoid(pgh))).astype(BF16)
        dpooled_h = jnp.where(blk < nb - 1, _dot_nt(dyph, wpdv), 0.0)
        tpos = (blk * ts + _iota((ts, 1), 0) + 1).astype(F32)
        ev = dpooled / jnp.minimum(tpos, wlen)
        buf_ref[0:ts, :] = ev
        buf_ref[ts:ts + POOL_HALO, :] = dpooled_h / wlen
        acc = ev
        snaps = []
        for d in range(1, POOL_HALO):
            acc = acc + buf_ref[pl.ds(d, ts), :]
            if d + 1 in POOL_WINDOWS:
                snaps.append(acc)
        dp_ref[:, C_PX:C_PX + POOL_W] = (_pool_group_select(lane_group, snaps) - dpooled).astype(BF16)

    rblk = lambda w, c: pl.BlockSpec((ts, w), lambda i: (nb - 1 - i, c))
    full = lambda a: pl.BlockSpec(a.shape, lambda i: (0,) * a.ndim)
    halo = lambda w, c: pl.BlockSpec((POOL_HALO, w), lambda i: (jnp.minimum((nb - i) * hb, last_halo), c))
    acc_spec = lambda r, w: pl.BlockSpec((r, w), lambda i: (0, 0))
    return pl.pallas_call(
        body, name="prep_bwd", grid=(nb,),
        in_specs=[rblk(FOX_W, C_FQ // FOX_W), rblk(FOX_W, C_FK // FOX_W), rblk(2 * POOL_W, C_PX // (2 * POOL_W)),
                  halo(2 * POOL_W, C_PX // (2 * POOL_W)), rblk(LANES, 0),
                  rblk(FOX_W, 0), rblk(FOX_W, 0), pl.BlockSpec((npair, FF_STRIDE, ts), lambda i: (0, 0, nb - 1 - i)),
                  pl.BlockSpec((npair, ts, LANES), lambda i: (0, nb - 1 - i, 0)), rblk(FOX_W, 0), rblk(FOX_W, 0),
                  rblk(SB_W, 0), rblk(SB_W, 0), rblk(SB_W, 0), rblk(SB_W, 0),
                  rblk(POOL_W, FOX_W // POOL_W), halo(POOL_W, FOX_W // POOL_W), rblk(POOL_W, 0), rblk(POOL_W, 0),
                  full(qg), full(kg), full(bfp), full(wpd), full(ps)],
        out_specs=[rblk(PW, 0), acc_spec(1, FOX_W), acc_spec(1, FOX_W), acc_spec(1, LANES), acc_spec(POOL_W, POOL_W), acc_spec(1, POOL_W)],
        out_shape=[jax.ShapeDtypeStruct((S, PW), BF16), jax.ShapeDtypeStruct((1, FOX_W), F32), jax.ShapeDtypeStruct((1, FOX_W), F32),
                   jax.ShapeDtypeStruct((1, LANES), F32), jax.ShapeDtypeStruct((POOL_W, POOL_W), F32), jax.ShapeDtypeStruct((1, POOL_W), F32)],
        scratch_shapes=[pltpu.VMEM((1, LANES), F32), pltpu.VMEM((ts, LANES), F32), pltpu.VMEM((ts + POOL_HALO, POOL_W), F32),
                        pltpu.VMEM((LANES, ts), F32)],
        compiler_params=_cparams(dimension_semantics=("arbitrary",)),
    )(projm, projm, projm, projm, ffo, dqn, dkn, dct, dcr, dv, dfg, dsq, dsk, dsv, dsg, dmix, dmix, pooled, yp, qg, kg, bfp, wpd, ps)


def _stack_call(body, name, grid, in_specs, operands, slot_specs, slot_shapes, stacks, plain_specs=(), plain_shapes=(), **kw):
    out_specs = list(plain_specs) + list(slot_specs)
    out_shape = list(plain_shapes) + [jax.ShapeDtypeStruct((DEPTH,) + s, F32) for s in slot_shapes]
    if stacks is None:
        return pl.pallas_call(body, name=name, grid=grid, in_specs=in_specs, out_specs=out_specs, out_shape=out_shape, **kw)(*operands)
    n = len(operands)

    def aliased_body(*refs):
        body(*refs[:n], *refs[n + len(stacks):])

    return pl.pallas_call(
        aliased_body, name=name, grid=grid, in_specs=list(in_specs) + [pl.BlockSpec(memory_space=pl.ANY)] * len(stacks),
        out_specs=out_specs, out_shape=out_shape,
        input_output_aliases={n + k: len(plain_specs) + k for k in range(len(stacks))}, **kw)(*operands, *stacks)


def _inproj_dw(h, dproj, layer, stacks, *, ts, tn):
    S, D = h.shape
    nj = PM // tn

    def body(h_ref, dp_ref, dpf_ref, dw_ref, dwf_ref):
        s = pl.program_id(1)

        @pl.when(s == 0)
        def _():
            dw_ref[...] = jnp.zeros_like(dw_ref)

        @pl.when((s == 0) & (pl.program_id(0) == 0))
        def _():
            dwf_ref[...] = jnp.zeros_like(dwf_ref)

        hv = h_ref[...]
        dw_ref[...] += _dot_tn(dp_ref[...], hv)

        @pl.when(pl.program_id(0) == 0)
        def _():
            dwf_ref[...] += _dot_tn(dpf_ref[...], hv)

    return _stack_call(
        body, "inproj_dw", (nj, S // ts),
        [pl.BlockSpec((ts, D), lambda j, s: (s, 0)),
         pl.BlockSpec((ts, tn), lambda j, s: (s, j)),
         pl.BlockSpec((ts, LANES), lambda j, s: (s, PM // LANES))],
        (h, dproj, dproj),
        [pl.BlockSpec((None, tn, D), lambda j, s: (layer, j, 0)), pl.BlockSpec((None, LANES, D), lambda j, s: (layer, 0, 0))],
        [(PM, D), (LANES, D)], stacks,
        compiler_params=_cparams(dimension_semantics=("arbitrary", "arbitrary")))


def _inproj_dx(dproj, wt_all, layer, x, g, dy, *, tm):
    S, D = x.shape

    def body(dp_ref, w_ref, x_ref, g_ref, dy_ref, dx_ref, dg_ref):
        @pl.when(pl.program_id(0) == 0)
        def _():
            dg_ref[...] = jnp.zeros_like(dg_ref)

        dh = _dot(dp_ref[...], w_ref[...])
        xf = x_ref[...]
        rstd = lax.rsqrt(jnp.mean(xf * xf, axis=-1, keepdims=True) + EPS)
        xhat = xf * rstd
        dg_ref[...] += jnp.sum(dh * xhat, axis=0, keepdims=True)
        dyg = dh * g_ref[...]
        mean = jnp.mean(dyg * xhat, axis=-1, keepdims=True)
        dx_ref[...] = rstd * (dyg - xhat * mean) + dy_ref[...]

    row = lambda w: pl.BlockSpec((tm, w), lambda i: (i, 0))
    return pl.pallas_call(
        body, name="inproj_dx", grid=(S // tm,),
        in_specs=[row(PW), pl.BlockSpec((None, PW, D), lambda i: (layer, 0, 0)), row(D), pl.BlockSpec((1, D), lambda i: (0, 0)), row(D)],
        out_specs=[row(D), pl.BlockSpec((1, D), lambda i: (0, 0))],
        out_shape=[jax.ShapeDtypeStruct((S, D), F32), jax.ShapeDtypeStruct((1, D), F32)],
        compiler_params=_cparams(dimension_semantics=("arbitrary",)),
    )(dproj, wt_all, x, g, dy)


def _adam_update(w, g, m, v):
    nm = ADAM_B1 * m + (1.0 - ADAM_B1) * g
    nv = ADAM_B2 * v + (1.0 - ADAM_B2) * (g * g)
    m_hat = nm / (1.0 - ADAM_B1 ** ADAM_STEP)
    v_hat = nv / (1.0 - ADAM_B2 ** ADAM_STEP)
    return -ADAM_LR * (m_hat / (jnp.sqrt(v_hat) + ADAM_EPS) + ADAM_WD * w), nm, nv


def _adamw(w, g, m, v):
    L, R, C = w.shape
    tr = R if R <= 512 else 256

    def body(w_ref, g_ref, m_ref, v_ref, d_ref, nm_ref, nv_ref):
        d_ref[...], nm_ref[...], nv_ref[...] = _adam_update(w_ref[...], g_ref[...], m_ref[...], v_ref[...])

    spec = pl.BlockSpec((1, tr, C), lambda l, i: (l, i, 0))
    shp = jax.ShapeDtypeStruct((L, R, C), F32)
    return pl.pallas_call(
        body, name="adamw", grid=(L, R // tr), in_specs=[spec] * 4, out_specs=[spec] * 3, out_shape=[shp] * 3,
        compiler_params=_cparams(dimension_semantics=("arbitrary", "arbitrary")),
    )(w, g, m, v)


def _adamw_nd(w, g, m, v):
    shape = w.shape
    view = (1,) + shape if w.ndim == 2 else (shape[0], -1, shape[-1])
    outs = _adamw(w.reshape(view), g.reshape(view), m.reshape(view), v.reshape(view))
    return tuple(o.reshape(shape) for o in outs)


FLIP_C = (0, 0, 1)
FLIP_X = (1, 0, 0)
FLIP_Y = (0, 1, 0)
FLIP_XY = (1, 1, 0)
MESH = pl.DeviceIdType.MESH


def _peer(flip):
    me = (lax.axis_index("x"), lax.axis_index("y"), lax.axis_index("c"))
    return tuple(1 - a if f else a for a, f in zip(me, flip))


def _exchange(name, arrays, flips):
    n = len(arrays)

    def body(*refs):
        srcs, dsts = refs[:n], refs[n:2 * n]
        send_sems, recv_sems = refs[2 * n:]
        copies = [pltpu.make_async_remote_copy(src_ref=srcs[k], dst_ref=dsts[k], send_sem=send_sems.at[k], recv_sem=recv_sems.at[k],
                                               device_id=_peer(flips[k]), device_id_type=MESH) for k in range(n)]
        for cp in copies:
            cp.start()
        for cp in copies:
            cp.wait()

    anyspec = pl.BlockSpec(memory_space=pl.ANY)
    return pl.pallas_call(
        body, name=name, in_specs=[anyspec] * n, out_specs=[anyspec] * n,
        out_shape=[jax.ShapeDtypeStruct(a.shape, a.dtype) for a in arrays],
        scratch_shapes=[pltpu.SemaphoreType.DMA((n,)), pltpu.SemaphoreType.DMA((n,))],
    )(*arrays)


def _exchange_add(name, x, flip):
    def body(x_ref, o_ref, buf_ref, send_sem, recv_sem):
        cp = pltpu.make_async_remote_copy(src_ref=x_ref, dst_ref=buf_ref, send_sem=send_sem, recv_sem=recv_sem,
                                          device_id=_peer(flip), device_id_type=MESH)
        cp.start()
        cp.wait()
        o_ref[...] = x_ref[...] + buf_ref[...]

    vspec = pl.BlockSpec(memory_space=pltpu.VMEM)
    return pl.pallas_call(
        body, name=name, in_specs=[vspec], out_specs=vspec, out_shape=jax.ShapeDtypeStruct(x.shape, x.dtype),
        scratch_shapes=[pltpu.VMEM(x.shape, x.dtype), pltpu.SemaphoreType.DMA, pltpu.SemaphoreType.DMA],
    )(x)


def _chip_index():
    return 2 * lax.axis_index("x") + lax.axis_index("y")


def _gather_weights(w_in_t, w_out):
    wi = w_in_t.astype(BF16)
    wo = jnp.swapaxes(w_out, 0, 1).astype(BF16)
    halves = (wi.shape[0] // 2, wo.shape[0] // 2)
    ARR = 2
    TO_X, TO_Y, ON_Y, ON_X, SIB_X, SIB_Y, SIB_D0, SIB_D1, OWN = [ARR * k for k in range(9)]
    n_sems = ARR * 9

    def body(wi_ref, wo_ref, gi_ref, go_ref, send_sems, recv_sems):
        c = lax.axis_index("c")
        j = _chip_index()
        srcs = (wi_ref, wo_ref)
        dsts = (gi_ref, go_ref)
        def cuts(core):
            return [(pl.ds(h * core, h), pl.ds(h * core, h // 2), pl.ds(h * core + h // 2, h - h // 2)) for h in halves]
        mine, theirs = cuts(c), cuts(1 - c)
        HALF, Q0, Q1 = 0, 1, 2

        def copy(idx, src, dst, flip):
            return pltpu.make_async_remote_copy(src_ref=src, dst_ref=dst, send_sem=send_sems.at[idx], recv_sem=recv_sems.at[idx],
                                                device_id=_peer(flip), device_id_type=MESH)

        def slot(a, shard, cut):
            return dsts[a].at[shard, cut]

        jx, jy, jd = j ^ 2, j ^ 1, j ^ 3
        sends = []

        def start(cp):
            cp.start()
            sends.append(cp)

        for a in range(ARR):
            start(copy(TO_X + a, srcs[a].at[mine[a][HALF]], slot(a, j, mine[a][HALF]), FLIP_X))
            start(copy(TO_Y + a, srcs[a].at[mine[a][HALF]], slot(a, j, mine[a][HALF]), FLIP_Y))
        own = [copy(OWN + a, srcs[a], dsts[a].at[j], FLIP_C) for a in range(ARR)]
        for cp in own:
            cp.start()
        for a in range(ARR):
            copy(TO_X + a, slot(a, jx, mine[a][HALF]), slot(a, jx, mine[a][HALF]), FLIP_X).wait_recv()
            start(copy(ON_Y + a, slot(a, jx, mine[a][Q0]), slot(a, jx, mine[a][Q0]), FLIP_Y))
            start(copy(SIB_X + a, slot(a, jx, mine[a][HALF]), slot(a, jx, mine[a][HALF]), FLIP_C))
        for a in range(ARR):
            copy(TO_Y + a, slot(a, jy, mine[a][HALF]), slot(a, jy, mine[a][HALF]), FLIP_Y).wait_recv()
            start(copy(ON_X + a, slot(a, jy, mine[a][Q1]), slot(a, jy, mine[a][Q1]), FLIP_X))
            start(copy(SIB_Y + a, slot(a, jy, mine[a][HALF]), slot(a, jy, mine[a][HALF]), FLIP_C))
        for a in range(ARR):
            copy(ON_Y + a, slot(a, jd, mine[a][Q0]), slot(a, jd, mine[a][Q0]), FLIP_Y).wait_recv()
            start(copy(SIB_D0 + a, slot(a, jd, mine[a][Q0]), slot(a, jd, mine[a][Q0]), FLIP_C))
        for a in range(ARR):
            copy(ON_X + a, slot(a, jd, mine[a][Q1]), slot(a, jd, mine[a][Q1]), FLIP_X).wait_recv()
            start(copy(SIB_D1 + a, slot(a, jd, mine[a][Q1]), slot(a, jd, mine[a][Q1]), FLIP_C))
        for a in range(ARR):
            for idx, shard, cut in ((SIB_X, jx, HALF), (SIB_Y, jy, HALF), (SIB_D0, jd, Q0), (SIB_D1, jd, Q1)):
                copy(idx + a, slot(a, shard, theirs[a][cut]), slot(a, shard, theirs[a][cut]), FLIP_C).wait_recv()
        for cp in own:
            cp.wait()
        for cp in sends:
            cp.wait_send()

    anyspec = pl.BlockSpec(memory_space=pl.ANY)
    gi, go = pl.pallas_call(
        body, name="gather_weights", in_specs=[anyspec] * 2, out_specs=[anyspec] * 2,
        out_shape=[jax.ShapeDtypeStruct((4,) + wi.shape, BF16), jax.ShapeDtypeStruct((4,) + wo.shape, BF16)],
        scratch_shapes=[pltpu.SemaphoreType.DMA((n_sems,)), pltpu.SemaphoreType.DMA((n_sems,))],
    )(wi, wo)
    w_in_t_full = gi.reshape((4 * wi.shape[0],) + wi.shape[1:])
    w_out_full = jnp.swapaxes(go.reshape((4 * wo.shape[0],) + wo.shape[1:]), 0, 1)
    return w_in_t_full, w_out_full


def _to_aligned(w_t):
    _, L, D = w_t.shape
    npair = FOX_HEADS // 2
    ff = w_t[ORIG_FF:ORIG_REST].reshape(npair, 2, L, D)
    ff = jnp.pad(ff, ((0, 0), (0, FF_STRIDE - 2), (0, 0), (0, 0))).reshape(npair * FF_STRIDE, L, D)
    ff = jnp.pad(ff, ((0, LANES - npair * FF_STRIDE), (0, 0), (0, 0)))
    return jnp.swapaxes(jnp.concatenate([w_t[:ORIG_FOX], w_t[ORIG_REST:], ff], axis=0), 0, 1)


def _from_aligned(dw_t):
    n, _, D = dw_t.shape
    npair = FOX_HEADS // 2
    ff = dw_t[:, PM:PM + npair * FF_STRIDE].reshape(n, npair, FF_STRIDE, D)[:, :, :2].reshape(n, FOX_HEADS, D)
    return jnp.swapaxes(jnp.concatenate([dw_t[:, :ORIG_FOX], ff, dw_t[:, ORIG_FOX:PM]], axis=1), 0, 1)


def _half_layers(name, stack, got):
    L, R, C = stack.shape
    half = L // 2
    tr = min(256, R)
    c = lax.axis_index("c")
    which = ((1 - c) if got is None else c).astype(jnp.int32).reshape(1)

    def body(c_ref, x_ref, *refs):
        if got is None:
            refs[0][...] = x_ref[...].astype(BF16)
        else:
            acc = x_ref[...] + refs[0][...].astype(F32)
            refs[1][...] = acc
            refs[2][...] = acc.astype(BF16)

    plain = pl.BlockSpec((1, tr, C), lambda l, i, c_ref: (l, i, 0))
    picked = pl.BlockSpec((1, tr, C), lambda l, i, c_ref: (c_ref[0] * half + l, i, 0))
    shp = lambda dt: jax.ShapeDtypeStruct((half, R, C), dt)
    grid_spec = pltpu.PrefetchScalarGridSpec(
        num_scalar_prefetch=1, grid=(half, R // tr),
        in_specs=[picked] + ([] if got is None else [plain]), out_specs=[plain] if got is None else [plain, plain])
    return pl.pallas_call(
        body, name=name, grid_spec=grid_spec, out_shape=[shp(BF16)] if got is None else [shp(F32), shp(BF16)],
        compiler_params=_cparams(dimension_semantics=("arbitrary", "arbitrary")),
    )(which, stack, *([] if got is None else [got]))


def _reduce_scatter(stack_m, stack_f, stack_o, shard_cols, shard_rows):
    j = _chip_index()
    half = DEPTH // 2
    stacks = (stack_m, stack_f, stack_o)
    give = [_half_layers("rs_give", s, None)[0] for s in stacks]
    got = _exchange("rs_d2d", give, (FLIP_C,) * len(stacks))
    (m32, mbf), (f32_, fbf), (o32, obf) = [_half_layers("rs_add_chip", s, g) for s, g in zip(stacks, got)]
    d_model = stack_m.shape[2]

    def in_shards(m, f):
        return _from_aligned(jnp.concatenate([m, f], axis=1)).reshape(4, shard_cols, half, d_model)

    def out_shards(o):
        return jnp.moveaxis(o.reshape(half, 4, shard_rows, o.shape[-1]), 1, 0)

    chip = [(in_shards(m32, f32_), in_shards(mbf, fbf), 0), (out_shards(o32), out_shards(obf), 1)]
    shard = lambda a, idx: lax.dynamic_index_in_dim(a, idx, axis=0, keepdims=False)
    via = []
    for _, bf, axis in chip:
        diag = shard(bf, j ^ 3)
        cut = diag.shape[axis] // 2
        via += [lax.slice_in_dim(diag, 0, cut, axis=axis), lax.slice_in_dim(diag, cut, 2 * cut, axis=axis)]
    handed = _exchange("rs_via", via, (FLIP_X, FLIP_Y) * len(chip))
    sends = []
    for a, (f32_sum, _, axis) in enumerate(chip):
        sends.append(_add_half_along("rs_add_via", shard(f32_sum, j ^ 2), handed[2 * a + 1], axis, 1))
        sends.append(_add_half_along("rs_add_via", shard(f32_sum, j ^ 1), handed[2 * a], axis, 0))
    got = _exchange("rs_ici", sends, (FLIP_X, FLIP_Y) * len(chip))
    own_in, own_out = [shard(f32_sum, j) for f32_sum, _, _ in chip]
    mine_in = _add_rows("rs_add_in", own_in, list(got[0:2]))
    mine_out = _add_into_half("rs_add_out", own_out, list(got[2:4]))
    sib_in, g_out = _share_halves(mine_in, mine_out)
    return (mine_in, sib_in), g_out


def _add_half_along(name, base, extra, axis, which):
    lanes = min(ROW_LANE_CHUNK, base.shape[2])
    assert base.shape[axis] == 2 * extra.shape[axis]
    blk = tuple(base.shape[d] // 2 if d == axis else base.shape[d] for d in range(2)) + (lanes,)

    def body(b_ref, e_ref, o_ref):
        x = b_ref[...]
        o_ref[...] = jnp.where(pl.program_id(0) == which, x + e_ref[...].astype(F32), x).astype(BF16)

    at = lambda i, k: (i, 0, k) if axis == 0 else (0, i, k)
    return pl.pallas_call(
        body, name=name, grid=(2, base.shape[2] // lanes),
        in_specs=[pl.BlockSpec(blk, at), pl.BlockSpec(blk, lambda i, k: (0, 0, k))], out_specs=pl.BlockSpec(blk, at),
        out_shape=jax.ShapeDtypeStruct(base.shape, BF16),
        compiler_params=_cparams(dimension_semantics=("arbitrary", "arbitrary")),
    )(base, extra)


def _add_rows(name, first, others):
    n = len(others)

    def body(*refs):
        acc = refs[0][...]
        for r in refs[1:1 + n]:
            acc = acc + r[...].astype(F32)
        refs[1 + n][...] = acc

    grid, spec = _row_lane_blocks(first.shape)
    return pl.pallas_call(
        body, name=name, grid=grid, in_specs=[spec(first.shape[1])] * (1 + n), out_specs=spec(first.shape[1]),
        out_shape=jax.ShapeDtypeStruct(first.shape, F32),
        compiler_params=_cparams(dimension_semantics=("arbitrary", "arbitrary")),
    )(first, *others)


ROW_LANE_CHUNK = 256


def _row_lane_blocks(shape):
    rows, _, C = shape
    tr = rows // 2 if rows % 2 == 0 and rows > 64 else rows
    lanes = min(ROW_LANE_CHUNK, C)
    return (rows // tr, C // lanes), lambda n_mid: pl.BlockSpec((tr, n_mid, lanes), lambda i, k, *_: (i, 0, k))


def _add_into_half(name, first, others):
    half, rows, C = first.shape
    tr = min(256, rows)
    n = len(others)

    def body(c_ref, *refs):
        acc = refs[0][...]
        for r in refs[1:1 + n]:
            acc = acc + r[...].astype(F32)
        refs[1 + n][...] = acc

    grid_spec = pltpu.PrefetchScalarGridSpec(
        num_scalar_prefetch=1, grid=(half, rows // tr),
        in_specs=[pl.BlockSpec((1, tr, C), lambda l, i, c_ref: (l, i, 0))] * (1 + n),
        out_specs=pl.BlockSpec((1, tr, C), lambda l, i, c_ref: (c_ref[0] * half + l, i, 0)))
    return pl.pallas_call(
        body, name=name, grid_spec=grid_spec, out_shape=jax.ShapeDtypeStruct((2 * half, rows, C), F32),
        compiler_params=_cparams(dimension_semantics=("arbitrary", "arbitrary")),
    )(lax.axis_index("c").astype(jnp.int32).reshape(1), first, *others)


def _share_halves(mine, buf):
    half = DEPTH // 2

    def body(mine_ref, buf_in, sib_ref, buf_ref, send_sems, recv_sems):
        lay = pl.ds(half * lax.axis_index("c"), half)
        copies = [pltpu.make_async_remote_copy(src_ref=src, dst_ref=dst, send_sem=send_sems.at[k], recv_sem=recv_sems.at[k],
                                               device_id=_peer(FLIP_C), device_id_type=MESH)
                  for k, (src, dst) in enumerate(((mine_ref, sib_ref), (buf_ref.at[lay], buf_ref.at[lay])))]
        for cp in copies:
            cp.start()
        for cp in copies:
            cp.wait()

    anyspec = pl.BlockSpec(memory_space=pl.ANY)
    return pl.pallas_call(
        body, name="rs_share", in_specs=[anyspec] * 2, out_specs=[anyspec] * 2,
        out_shape=[jax.ShapeDtypeStruct(mine.shape, mine.dtype), jax.ShapeDtypeStruct(buf.shape, buf.dtype)],
        input_output_aliases={1: 1},
        scratch_shapes=[pltpu.SemaphoreType.DMA((2,)), pltpu.SemaphoreType.DMA((2,))],
    )(mine, buf)


def _adamw_halves(w, g_mine, g_sib, m, v):
    half = g_mine.shape[1]

    def body(c_ref, w_ref, gm_ref, gs_ref, m_ref, v_ref, g_ref, d_ref, nm_ref, nv_ref):
        first = c_ref[0] == 0
        gm, gs = gm_ref[...], gs_ref[...]
        for h, gv in enumerate((jnp.where(first, gm, gs), jnp.where(first, gs, gm))):
            lay = slice(half * h, half * (h + 1))
            g_ref[:, lay, :] = gv
            d_ref[:, lay, :], nm_ref[:, lay, :], nv_ref[:, lay, :] = _adam_update(w_ref[:, lay, :], gv, m_ref[:, lay, :], v_ref[:, lay, :])

    grid, spec = _row_lane_blocks(w.shape)
    full, part = spec(w.shape[1]), spec(half)
    grid_spec = pltpu.PrefetchScalarGridSpec(num_scalar_prefetch=1, grid=grid, in_specs=[full, part, part, full, full], out_specs=[full] * 4)
    return pl.pallas_call(
        body, name="adamw_halves", grid_spec=grid_spec, out_shape=[jax.ShapeDtypeStruct(w.shape, F32)] * 4,
        compiler_params=_cparams(dimension_semantics=("arbitrary", "arbitrary")),
    )(lax.axis_index("c").astype(jnp.int32).reshape(1), w, g_mine, g_sib, m, v)


def _all_reduce_small(x):
    x = _exchange_add("ar_c", x, FLIP_C)
    x = _exchange_add("ar_y", x, FLIP_Y)
    return _exchange_add("ar_x", x, FLIP_X)


def _blocks(S):
    return dict(tm=min(512, S), tm_proj=min(1024, S), ts=min(512, S), tq=min(512, S), tq_big=min(1024, S), tk=min(512, S), tks=min(256, S))


def _pair_pad(vec):
    npair = FOX_HEADS // 2
    v = jnp.pad(vec.reshape(npair, 2), ((0, 0), (0, FF_STRIDE - 2))).reshape(1, npair * FF_STRIDE)
    return jnp.pad(v, ((0, 0), (0, LANES - npair * FF_STRIDE)))


def _pair_unpad(row):
    npair = FOX_HEADS // 2
    return row[0, :npair * FF_STRIDE].reshape(npair, FF_STRIDE)[:, :2].reshape(FOX_HEADS)


def _pool_blockdiag(w_pool):
    g, cg, _ = w_pool.shape
    eye = jnp.eye(g, dtype=w_pool.dtype)
    return jnp.einsum("gh,gcd->gchd", eye, w_pool).reshape(g * cg, g * cg)


QK_BOUND_SLACK = 1.05


def _layer_params(norm_g, b_f, q_norm_g, k_norm_g, w_pool, pool_scale):
    qk_bound = QK_BOUND_SLACK * HEAD_DIM * QK_SCALE * jnp.max(jnp.abs(q_norm_g)) * jnp.max(jnp.abs(k_norm_g))
    return dict(g=norm_g.reshape(1, -1), qg=jnp.tile(q_norm_g, FOX_HEADS).reshape(1, FOX_W), kg=jnp.tile(k_norm_g, FOX_HEADS).reshape(1, FOX_W),
                bfp=_pair_pad(b_f), wpd=_pool_blockdiag(w_pool).astype(BF16), ps=pool_scale.reshape(1, POOL_W),
                qkb=jnp.full((1, LANES), qk_bound, F32))


def _layer_fwd(x, wt_all, w_out, layer, prm, bs):
    projm, ffo, h = _inproj(x, prm["g"], wt_all, layer, tm=bs["tm_proj"], tn=PROJ_TN)
    qn, ka, kb, v, sq, sk, sv, pooled, yp, pm = _prep(projm, ffo, prm["qg"], prm["kg"], prm["bfp"], prm["wpd"], prm["ps"], ts=bs["ts"])
    o, lse, fm = _fox_fwd(qn, ka, kb, v, projm, prm["qkb"], tq=bs["tq"], tk=bs["tk"])
    so, sm = _sb_fwd(sq, sk, sv, projm, tq=bs["tq"], tk=bs["tks"])
    y = _outproj(x, fm, pm, sm, w_out, layer, tm=bs["tm_proj"])
    saved = dict(x=x, projm=projm, ffo=ffo, h=h, qn=qn, ka=ka, kb=kb, v=v, sq=sq, sk=sk, sv=sv, pooled=pooled, yp=yp,
                 o=o, lse=lse, so=so, fm=fm, pm=pm, sm=sm)
    return y, saved


def _layer_bwd(dy, wt_all, w_out, prm, sv_, bs, layer, stacks):
    dmix, stack_o = _outproj_bwd(dy, sv_["fm"], sv_["pm"], sv_["sm"], w_out, layer, None if stacks is None else stacks[2:], tm=bs["tm_proj"])
    dqn, dkn, dv, dfg, dct, dcr = _fox_bwd(sv_["qn"], sv_["ka"], sv_["kb"], sv_["v"], sv_["o"], sv_["lse"], dmix, sv_["projm"],
                                      prm["qkb"], tq=bs["tq_big"], tk=bs["tk"])
    dsq, dsk, dsv, dsg = _sb_bwd(sv_["sq"], sv_["sk"], sv_["sv"], sv_["so"], dmix, sv_["projm"], tq=bs["tks"], tk=bs["tks"])
    dproj, dqg, dkg, dbf, dwp, dps = _prep_bwd(sv_["projm"], sv_["ffo"], dqn, dkn, dct, dcr, dv, dfg, dsq, dsk, dsv, dsg, dmix,
                                               sv_["pooled"], sv_["yp"], prm["qg"], prm["kg"], prm["bfp"], prm["wpd"], prm["ps"], ts=bs["ts"])
    stack_m, stack_f = _inproj_dw(sv_["h"], dproj, layer, None if stacks is None else stacks[:2], ts=bs["tm_proj"], tn=PROJ_TN)
    dx, dg = _inproj_dx(dproj, wt_all, layer, sv_["x"], prm["g"], dy, tm=bs["tm"])
    grads = dict(
        norm_g=dg[0],
        b_f=_pair_unpad(dbf), q_norm_g=dqg.reshape(FOX_HEADS, HEAD_DIM).sum(0), k_norm_g=dkg.reshape(FOX_HEADS, HEAD_DIM).sum(0),
        w_pool=jnp.stack([dwp[HEAD_DIM * g:HEAD_DIM * (g + 1), HEAD_DIM * g:HEAD_DIM * (g + 1)] for g in range(4)]),
        pool_scale=dps[0])
    return dx, grads, (stack_m, stack_f, stack_o)


def _local_step(x, target, wt_all, w_out, norm_g, b_f, q_norm_g, k_norm_g, w_pool, pool_scale):
    S, D = x.shape
    bs = _blocks(S)
    prms = [_layer_params(norm_g[l], b_f[l], q_norm_g[l], k_norm_g[l], w_pool[l], pool_scale[l]) for l in range(DEPTH)]
    saved = []
    y = x
    for l in range(DEPTH):
        y, s_ = _layer_fwd(y, wt_all, w_out, l, prms[l], bs)
        saved.append(s_)
    dy, sq = _loss_head(y, target, tm=bs["tm"])
    loss = 0.5 * jnp.sum(sq) / D
    grads = [None] * DEPTH
    stacks = None
    for l in reversed(range(DEPTH)):
        dy, grads[l], stacks = _layer_bwd(dy, wt_all, w_out, prms[l], saved[l], bs, l, stacks)
    stacked = {k: jnp.stack([g[k] for g in grads]) for k in grads[0]}
    return loss, dy, stacked, stacks


SMALL = ("norm_g", "b_f", "q_norm_g", "k_norm_g", "w_pool", "pool_scale")


def _pack_small(gr):
    flat = jnp.concatenate([gr[k].reshape(-1) for k in SMALL])
    pad = (-flat.shape[0]) % (8 * LANES)
    return jnp.pad(flat, (0, pad)).reshape(-1, LANES)


def _unpack_small(packed, like):
    flat = packed.reshape(-1)
    out, off = {}, 0
    for k in SMALL:
        n = like[k].size
        out[k] = flat[off:off + n].reshape(like[k].shape)
        off += n
    return out


def kernel(x, norm_g, w_in, b_f, q_norm_g, k_norm_g, w_pool, pool_scale, w_out, loss_target, m_norm_g, m_w_in, m_b_f, m_q_norm_g, m_k_norm_g, m_w_pool, m_pool_scale, m_w_out, v_norm_g, v_w_in, v_b_f, v_q_norm_g, v_k_norm_g, v_w_pool, v_pool_scale, v_w_out):
    weights = dict(norm_g=norm_g, w_in=w_in, b_f=b_f, q_norm_g=q_norm_g, k_norm_g=k_norm_g, w_pool=w_pool, pool_scale=pool_scale, w_out=w_out)
    mom_m = dict(norm_g=m_norm_g, w_in=m_w_in, b_f=m_b_f, q_norm_g=m_q_norm_g, k_norm_g=m_k_norm_g, w_pool=m_w_pool, pool_scale=m_pool_scale, w_out=m_w_out)
    mom_v = dict(norm_g=v_norm_g, w_in=v_w_in, b_f=v_b_f, q_norm_g=v_q_norm_g, k_norm_g=v_k_norm_g, w_pool=v_w_pool, pool_scale=v_pool_scale, w_out=v_w_out)
    shard_cols = w_in.shape[2]
    shard_rows = w_out.shape[1]

    cols_first = lambda a: jnp.transpose(a, (2, 0, 1))
    w_in_t = cols_first(w_in)
    w_in_t_full, w_out_full = _gather_weights(w_in_t, w_out)
    wt_all = _to_aligned(w_in_t_full)
    loss, dx, gr, stacks = _local_step(x[0], loss_target[0], wt_all, w_out_full, norm_g, b_f, q_norm_g, k_norm_g, w_pool, pool_scale)
    loss = lax.psum(loss, ("x", "y", "c"))

    (g_in_mine, g_in_sib), g_w_out = _reduce_scatter(*stacks, shard_cols, shard_rows)
    small = _unpack_small(_all_reduce_small(_pack_small(gr)), {k: weights[k] for k in SMALL})
    grad_w = dict(small, w_out=g_w_out)

    names = ("norm_g", "w_in", "b_f", "q_norm_g", "k_norm_g", "w_pool", "pool_scale", "w_out")
    upd = {k: _adamw_nd(weights[k], grad_w[k], mom_m[k], mom_v[k]) for k in names if k != "w_in"}
    in_t = _adamw_halves(w_in_t, g_in_mine, g_in_sib, cols_first(mom_m["w_in"]), cols_first(mom_v["w_in"]))
    grad_w["w_in"], *upd["w_in"] = [jnp.transpose(a, (1, 2, 0)) for a in in_t]
    return (loss, dx[None], *[grad_w[k] for k in names], *[upd[k][0] for k in names], *[upd[k][1] for k in names], *[upd[k][2] for k in names])
```

```python
import functools

import jax
import jax.numpy as jnp
from jax import lax
from jax.experimental import pallas as pl
from jax.experimental.pallas import tpu as pltpu

F32 = jnp.float32
BF16 = jnp.bfloat16

DEPTH = 4
HEAD_DIM = 64
FOX_HEADS = 8
SB_HEADS = 4
FOX_W = FOX_HEADS * HEAD_DIM
SB_W = SB_HEADS * HEAD_DIM
POOL_W = 256
POOL_WINDOWS = (2, 4, 8, 16)
POOL_HALO = 16
D_MIX = FOX_W + POOL_W + SB_W
EPS = 1e-6
NEG = -1e30
QK_SCALE = HEAD_DIM ** -0.5

ORIG_FOX = 4 * FOX_W
ORIG_FF = ORIG_FOX
ORIG_REST = ORIG_FF + FOX_HEADS
D_IN = ORIG_REST + 2 * POOL_W + 4 * SB_W

C_FQ, C_FK, C_FV, C_FG = 0, FOX_W, 2 * FOX_W, 3 * FOX_W
C_PX = 4 * FOX_W
C_PG = C_PX + POOL_W
C_SQ = C_PG + POOL_W
C_SK, C_SV, C_SG = C_SQ + SB_W, C_SQ + 2 * SB_W, C_SQ + 3 * SB_W
PM = C_SG + SB_W
LANES = 128
LANE_SHIFT = 7
HEAD_SHIFT = 6
PW = PM + LANES
FF_STRIDE = 8
AUG = 3

ADAM_LR = 0.001
ADAM_B1 = 0.9
ADAM_B2 = 0.999
ADAM_EPS = 1e-08
ADAM_WD = 0.01
ADAM_STEP = 10

VMEM_LIMIT = 48 * 1024 * 1024
PROJ_TN = PM // 2


def _cparams(**kw):
    return pltpu.CompilerParams(vmem_limit_bytes=VMEM_LIMIT, **kw)


def _dot(a, b):
    return jnp.dot(a, b, preferred_element_type=F32)


def _dot_nt(a, b):
    return lax.dot_general(a, b, (((1,), (1,)), ((), ())), preferred_element_type=F32)


def _dot_tn(a, b):
    return lax.dot_general(a, b, (((0,), (0,)), ((), ())), preferred_element_type=F32)


def _split2(x):
    hi = x.astype(BF16)
    lo = (x - hi.astype(F32)).astype(BF16)
    return hi, lo


def _split3(x):
    hi = x.astype(BF16)
    r = x - hi.astype(F32)
    mid = r.astype(BF16)
    lo = (r - mid.astype(F32)).astype(BF16)
    return hi, mid, lo


def _dot_exact_rhs(x, m):
    hi, mid, lo = _split3(x)
    return _dot(hi, m) + _dot(mid, m) + _dot(lo, m)


def _dot_exact_lhs(m, x):
    hi, mid, lo = _split3(x)
    return _dot(m, hi) + _dot(m, mid) + _dot(m, lo)


def _sigmoid(x):
    return 1.0 / (1.0 + jnp.exp(-x))


def _silu_pair(x):
    s = _sigmoid(x)
    return x * s, s * (1.0 + x * (1.0 - s))


def _iota(shape, dim):
    return lax.broadcasted_iota(jnp.int32, shape, dim)


def _ones_where(cond):
    return jnp.where(cond, 1.0, 0.0).astype(BF16)


GROUP_SLAB = 256


def _head_blockdiag():
    rows, cols = _iota((2 * GROUP_SLAB, GROUP_SLAB), 0) & (GROUP_SLAB - 1), _iota((2 * GROUP_SLAB, GROUP_SLAB), 1)
    return _ones_where((rows >> HEAD_SHIFT) == (cols >> HEAD_SHIFT))


def _group_sum(x, bd):
    hi, lo = _split2(x)
    slabs = [_dot(jnp.concatenate([hi[:, s:s + GROUP_SLAB], lo[:, s:s + GROUP_SLAB]], axis=1), bd) for s in range(0, x.shape[1], GROUP_SLAB)]
    return jnp.concatenate(slabs, axis=1)


def _lane_pick(x, lane_idx, lane):
    return jnp.sum(jnp.where(lane_idx == lane, x, 0.0), axis=1, keepdims=True)


def _inproj(x, g, wt_all, layer, *, tm, tn):
    S, D = x.shape
    nj = PM // tn

    def body(x_ref, g_ref, w_ref, wff_ref, proj_ref, ff_ref, h_ref):
        @pl.when(pl.program_id(1) == 0)
        def _():
            xf = x_ref[...]
            ms = jnp.mean(xf * xf, axis=-1, keepdims=True)
            h = (xf * lax.rsqrt(ms + EPS) * g_ref[...]).astype(BF16)
            h_ref[...] = h
            ff_ref[...] = _dot_nt(h, wff_ref[...])

        proj_ref[...] = _dot_nt(h_ref[...], w_ref[...])

    return pl.pallas_call(
        body, name="inproj", grid=(S // tm, nj),
        in_specs=[pl.BlockSpec((tm, D), lambda i, j: (i, 0)),
                  pl.BlockSpec((1, D), lambda i, j: (0, 0)),
                  pl.BlockSpec((None, tn, D), lambda i, j: (layer, j, 0)),
                  pl.BlockSpec((None, LANES, D), lambda i, j: (layer, PM // LANES, 0))],
        out_specs=[pl.BlockSpec((tm, tn), lambda i, j: (i, j)),
                   pl.BlockSpec((tm, LANES), lambda i, j: (i, 0)),
                   pl.BlockSpec((tm, D), lambda i, j: (i, 0))],
        out_shape=[jax.ShapeDtypeStruct((S, PM), F32), jax.ShapeDtypeStruct((S, LANES), F32),
                   jax.ShapeDtypeStruct((S, D), BF16)],
        compiler_params=_cparams(dimension_semantics=("arbitrary", "arbitrary")),
    )(x, g, wt_all, wt_all)


def _pool_group_select(lane_group, vals):
    return jnp.where(lane_group == 0, vals[0], jnp.where(lane_group == 1, vals[1], jnp.where(lane_group == 2, vals[2], vals[3])))


def _prep(projm, ffo, qg, kg, bfp, wpd, ps, *, ts):
    S = projm.shape[0]
    nb = S // ts
    hb = ts // POOL_HALO

    def body(fq_ref, fk_ref, fv_ref, pp_ref, halo_ref, ff_ref, sq_ref, sk_ref, sv_ref,
             qg_ref, kg_ref, bf_ref, wpd_ref, ps_ref,
             qn_ref, ka_ref, kb_ref, v_ref, sqo_ref, sko_ref, svo_ref, pooled_ref, yp_ref, pm_ref,
             carry_ref, c_ref, buf_ref):
        i = pl.program_id(0)
        bd = _head_blockdiag()
        normed = []
        for src, g_ref in ((fq_ref, qg_ref), (fk_ref, kg_ref)):
            q = src[...]
            ss = _group_sum(q * q, bd)
            normed.append(q * lax.rsqrt(ss * (1.0 / HEAD_DIM) + EPS) * g_ref[...])
        qn_ref[...] = (normed[0] * QK_SCALE).astype(BF16)
        kn = normed[1]
        v_ref[...] = fv_ref[...].astype(BF16)
        sqo_ref[...] = (sq_ref[...] * QK_SCALE).astype(BF16)
        sko_ref[...] = sk_ref[...].astype(BF16)
        svo_ref[...] = sv_ref[...].astype(BF16)

        @pl.when(i == 0)
        def _():
            carry_ref[...] = jnp.zeros_like(carry_ref)

        z = ff_ref[...] + bf_ref[...]
        lf = jnp.minimum(z, 0.0) - jnp.log(1.0 + jnp.exp(-jnp.abs(z)))
        tri = _ones_where(_iota((ts, ts), 1) <= _iota((ts, ts), 0))
        c = _dot_exact_lhs(tri, lf) + carry_ref[...]
        c_ref[...] = c
        carry_ref[...] = c_ref[ts - 1:ts, :]
        parts = jnp.concatenate(_split3(-c), axis=1)
        row = _iota((AUG * LANES, FOX_W), 0)
        col = _iota((AUG * LANES, FOX_W), 1)
        part, src = row >> LANE_SHIFT, row & (LANES - 1)
        pair, off = col >> LANE_SHIFT, col & (LANES - 1)
        sel_a = _ones_where((src == FF_STRIDE * pair) & (off == HEAD_DIM + part))
        sel_b = _ones_where((src == FF_STRIDE * pair + 1) & (off == part))
        first_half = (_iota((1, FOX_W), 1) & HEAD_DIM) == 0
        ka_ref[...] = jnp.where(first_half, kn, _dot(parts, sel_a)).astype(BF16)
        kb_ref[...] = jnp.where(first_half, _dot(parts, sel_b), kn).astype(BF16)

        x = pp_ref[:, 0:POOL_W]
        pg = pp_ref[:, POOL_W:2 * POOL_W]
        halo = jnp.where(i > 0, halo_ref[:, 0:POOL_W], 0.0)
        buf_ref[0:POOL_HALO, :] = halo
        buf_ref[POOL_HALO:POOL_HALO + ts, :] = x
        acc = x
        snaps = []
        for d in range(1, POOL_HALO):
            acc = acc + buf_ref[pl.ds(POOL_HALO - d, ts), :]
            if d + 1 in POOL_WINDOWS:
                snaps.append(acc)
        lane_group = _iota((1, POOL_W), 1) >> HEAD_SHIFT
        wsum = _pool_group_select(lane_group, snaps)
        wlen = _pool_group_select(lane_group, [float(w) for w in POOL_WINDOWS])
        tpos = (i * ts + _iota((ts, 1), 0) + 1).astype(F32)
        pooled = wsum / jnp.minimum(tpos, wlen) - x
        pb = pooled.astype(BF16)
        pooled_ref[...] = pb
        yp = _dot(pb, wpd_ref[...])
        yp_ref[...] = yp
        pm_ref[...] = (yp * ps_ref[...] * (pg * _sigmoid(pg))).astype(BF16)

    blk = lambda w, c: pl.BlockSpec((ts, w), lambda i: (i, c))
    full = lambda a: pl.BlockSpec(a.shape, lambda i: (0,) * a.ndim)
    out_shapes = [
        jax.ShapeDtypeStruct((S, FOX_W), BF16), jax.ShapeDtypeStruct((S, FOX_W), BF16), jax.ShapeDtypeStruct((S, FOX_W), BF16),
        jax.ShapeDtypeStruct((S, FOX_W), BF16),
        jax.ShapeDtypeStruct((S, SB_W), BF16), jax.ShapeDtypeStruct((S, SB_W), BF16), jax.ShapeDtypeStruct((S, SB_W), BF16),
        jax.ShapeDtypeStruct((S, POOL_W), BF16), jax.ShapeDtypeStruct((S, POOL_W), F32), jax.ShapeDtypeStruct((S, POOL_W), BF16),
    ]
    out_specs = [
        blk(FOX_W, 0), blk(FOX_W, 0), blk(FOX_W, 0), blk(FOX_W, 0),
        blk(SB_W, 0), blk(SB_W, 0), blk(SB_W, 0),
        blk(POOL_W, 0), blk(POOL_W, 0), blk(POOL_W, 0),
    ]
    return pl.pallas_call(
        body, name="prep", grid=(nb,),
        in_specs=[blk(FOX_W, C_FQ // FOX_W), blk(FOX_W, C_FK // FOX_W), blk(FOX_W, C_FV // FOX_W), blk(2 * POOL_W, C_PX // (2 * POOL_W)),
                  pl.BlockSpec((POOL_HALO, 2 * POOL_W), lambda i: (jnp.maximum(i * hb - 1, 0), C_PX // (2 * POOL_W))),
                  blk(LANES, 0),
                  blk(SB_W, C_SQ // SB_W), blk(SB_W, C_SK // SB_W), blk(SB_W, C_SV // SB_W),
                  full(qg), full(kg), full(bfp), full(wpd), full(ps)],
        out_specs=out_specs, out_shape=out_shapes,
        scratch_shapes=[pltpu.VMEM((1, LANES), F32), pltpu.VMEM((ts, LANES), F32), pltpu.VMEM((ts + POOL_HALO, POOL_W), F32)],
        compiler_params=_cparams(dimension_semantics=("arbitrary",)),
    )(projm, projm, projm, projm, projm, ffo, projm, projm, projm, qg, kg, bfp, wpd, ps)


def _pair_masks(x):
    ma = _iota((1, LANES), 1) < HEAD_DIM
    zero = jnp.zeros_like(x)
    return jnp.where(ma, x, zero), jnp.where(ma, zero, x)


DIAG_TILE = 256


def _diag_tiles(tq, size=DIAG_TILE):
    size = min(tq, size)
    return [(t * size, size) for t in range(tq // size)]


def _put_rows(old, new, r0):
    return new if r0 == 0 else jnp.concatenate([old[:r0], new], axis=0)


def _aug_queries(q):
    lane = _iota((1, LANES), 1)
    one = jnp.ones_like(q)
    zero = jnp.zeros_like(q)
    qa = jnp.where(lane < HEAD_DIM, q, jnp.where(lane < HEAD_DIM + AUG, one, zero))
    qb = jnp.where(lane >= HEAD_DIM, q, jnp.where(lane < AUG, one, zero))
    return qa, qb


EXP_DEAD = -105.0
PACK = 16


def _fox_walk_left(nfull, tk, block, carry, k_refs, qk_bound, row_floor, lone_span=1):
    lane = _iota((1, LANES), 1)

    def alive(h, jj, c):
        k0 = pl.multiple_of(jnp.maximum(nfull - 1 - jj, 0) * tk + tk - PACK, PACK)
        last = k_refs[h][pl.ds(k0, PACK), :].astype(F32)
        lo = HEAD_DIM if h == 0 else 0
        negc = jnp.sum(jnp.where((lane >= lo) & (lane < lo + AUG), last, 0.0), axis=1, keepdims=True)
        return qk_bound + jnp.max(negc) - row_floor(c)[h] >= EXP_DEAD

    def walk(heads, jj0, c0, span=1):
        def go_on(state):
            jj, c = state
            ok = jj + span <= nfull
            for h in heads:
                ok = ok & alive(h, jj, c)
            return ok

        def step(state):
            jj, c = state
            return jj + span, block(pl.multiple_of((nfull - span - jj) * tk, tk), span * tk, 0, c, False, heads)

        return lax.while_loop(go_on, step, (jj0, c0))

    jj_pair, carry = walk((0, 1), jnp.int32(0), carry)
    for h in (0, 1):
        jj_h = jj_pair
        if lone_span > 1:
            jj_h, carry = walk((h,), jj_h, carry, lone_span)
        carry = walk((h,), jj_h, carry)[1]
    return carry


def _fox_fwd(qn, ka, kb, v, projm, qkb, *, tq, tk):
    S = qn.shape[0]
    npair = FOX_HEADS // 2

    def body(q_ref, ka_ref, kb_ref, v_ref, fg_ref, qkb_ref, o_ref, lse_ref, fm_ref):
        qi = pl.program_id(1)
        lane = _iota((1, LANES), 1)
        ma = lane < HEAD_DIM
        qaug = _aug_queries(q_ref[...])
        k_refs = (ka_ref, kb_ref)

        def block(k0, tkl, r0, carry, masked, heads=(0, 1)):
            vb = v_ref[pl.ds(k0, tkl), :]
            if masked:
                mask = (k0 + _iota((tq - r0, tkl), 1)) <= (qi * tq + r0 + _iota((tq - r0, tkl), 0))
            scores = {h: _dot_nt(qaug[h][r0:], k_refs[h][pl.ds(k0, tkl), :]) for h in heads}
            new = list(carry)
            for h in heads:
                m, l, acc = [x[r0:] for x in carry[h]]
                s = jnp.where(mask, scores[h], NEG) if masked else scores[h]
                m_new = jnp.maximum(m, jnp.max(s, axis=1, keepdims=True))
                alpha = jnp.exp(m - m_new)
                p = jnp.exp(s - m_new)
                sub = (m_new, alpha * l + jnp.sum(p, axis=1, keepdims=True), alpha * acc + _dot(p.astype(BF16), vb))
                new[h] = tuple(_put_rows(old, x, r0) for old, x in zip(carry[h], sub))
            return tuple(new)

        carry = tuple((jnp.full((tq, 1), NEG, F32), jnp.zeros((tq, 1), F32), jnp.zeros((tq, LANES), F32)) for _ in range(2))
        for off, size in _diag_tiles(tq, tq):
            carry = block(pl.multiple_of(qi * tq + off, size), size, off, carry, True)
        carry = _fox_walk_left((qi * tq) // tk, tk, block, carry, k_refs, jnp.max(qkb_ref[...]),
                               lambda c: (jnp.min(c[0][0]), jnp.min(c[1][0])), lone_span=2)
        (ma_, la, acca), (mb_, lb, accb) = carry
        o = jnp.where(ma, acca / la, accb / lb)
        o_ref[...] = o
        lse_ref[...] = jnp.where(ma, ma_ + jnp.log(la), mb_ + jnp.log(lb))
        fg = fg_ref[...]
        fm_ref[...] = (o * (fg * _sigmoid(fg))).astype(BF16)

    qblk = pl.BlockSpec((tq, LANES), lambda p, i: (i, p))
    kvblk = pl.BlockSpec((S, LANES), lambda p, i: (0, p))
    return pl.pallas_call(
        body, name="fox_fwd", grid=(npair, S // tq),
        in_specs=[qblk, kvblk, kvblk, kvblk,
                  pl.BlockSpec((tq, LANES), lambda p, i: (i, C_FG // LANES + p)),
                  pl.BlockSpec((1, LANES), lambda p, i: (0, 0))],
        out_specs=[qblk, qblk, qblk],
        out_shape=[jax.ShapeDtypeStruct((S, FOX_W), F32), jax.ShapeDtypeStruct((S, FOX_W), F32), jax.ShapeDtypeStruct((S, FOX_W), BF16)],
        compiler_params=_cparams(dimension_semantics=("arbitrary", "arbitrary")),
    )(qn, ka, kb, v, projm, qkb)


def _suffix_sums(x, tmat2):
    return _dot(jnp.concatenate(_split2(x), axis=1), tmat2)


def _suffix_matrix(tk, inclusive):
    rr, cc = _iota((2 * tk, tk), 0) & (tk - 1), _iota((2 * tk, tk), 1)
    return _ones_where(rr >= cc) if inclusive else _ones_where(rr > cc)


def _sb_scores(qh, kb, causal, tmat2, r_runs):
    heads = range(2)
    zs = [_dot_nt(qh[h], kb) for h in heads]
    nsps = [jnp.minimum(-z, 0.0) - jnp.log(1.0 + jnp.exp(-jnp.abs(z))) for z in zs]
    lbs = nsps if causal is None else [jnp.where(causal, n, 0.0) for n in nsps]
    rins = [_suffix_sums(lb, tmat2) for lb in lbs]
    args = [zs[h] + lbs[h] + (rins[h] + r_runs[h]) for h in heads]
    a_s = [jnp.exp(arg if causal is None else jnp.where(causal, arg, NEG)) for arg in args]
    return zs, nsps, lbs, a_s


def _sb_walk_left(nfull, tk, block, carry, running_sums):
    def alive(state):
        jj, c = state
        ra, rb = running_sums(c)
        return (jj < nfull) & (jnp.max(jnp.maximum(ra, rb)) >= EXP_DEAD)

    def step(state):
        jj, c = state
        return jj + 1, block(pl.multiple_of((nfull - 1 - jj) * tk, tk), 0, c, False)

    return lax.while_loop(alive, step, (jnp.int32(0), carry))[1]


def _sb_fwd(sq, sk, sv, projm, *, tq, tk):
    S = sq.shape[0]
    npair = SB_HEADS // 2

    def body(q_ref, k_ref, v_ref, sg_ref, o_ref, sm_ref):
        qi = pl.program_id(1)
        lane = _iota((1, LANES), 1)
        ma = lane < HEAD_DIM
        qh = _pair_masks(q_ref[...])
        tmat2 = _suffix_matrix(tk, inclusive=False)
        nfull = (qi * tq) // tk

        def block(k0, r0, carry, masked):
            nr = tq - r0
            kb = k_ref[pl.ds(k0, tk), :]
            vb = v_ref[pl.ds(k0, tk), :]
            causal = (k0 + _iota((nr, tk), 1)) < (qi * tq + r0 + _iota((nr, tk), 0)) if masked else None
            _, _, lbs, a_s = _sb_scores([q[r0:] for q in qh], kb, causal, tmat2, [carry[h][0][r0:] for h in range(2)])
            pv = _dot(jnp.concatenate([a.astype(BF16) for a in a_s], axis=0), vb)
            return tuple((_put_rows(carry[h][0], carry[h][0][r0:] + jnp.sum(lbs[h], axis=1, keepdims=True), r0),
                          _put_rows(carry[h][1], carry[h][1][r0:] + pv[h * nr:(h + 1) * nr], r0)) for h in range(2))

        carry = tuple((jnp.zeros((tq, 1), F32), jnp.zeros((tq, LANES), F32)) for _ in range(2))
        for off, size in reversed(_diag_tiles(tq)):
            assert size == tk
            carry = block(pl.multiple_of(qi * tq + off, tk), off, carry, True)
        (_, acca), (_, accb) = _sb_walk_left(nfull, tk, block, carry, lambda c: (c[0][0], c[1][0]))
        o = jnp.where(ma, acca, accb)
        o_ref[...] = o
        sg = sg_ref[...]
        sm_ref[...] = (o * (sg * _sigmoid(sg))).astype(BF16)

    qblk = pl.BlockSpec((tq, LANES), lambda p, i: (i, p))
    kvblk = pl.BlockSpec((S, LANES), lambda p, i: (0, p))
    return pl.pallas_call(
        body, name="sb_fwd", grid=(npair, S // tq),
        in_specs=[qblk, kvblk, kvblk, pl.BlockSpec((tq, LANES), lambda p, i: (i, C_SG // LANES + p))],
        out_specs=[qblk, qblk],
        out_shape=[jax.ShapeDtypeStruct((S, SB_W), F32), jax.ShapeDtypeStruct((S, SB_W), BF16)],
        compiler_params=_cparams(dimension_semantics=("arbitrary", "arbitrary")),
    )(sq, sk, sv, projm)


def _outproj(x, fm, pm, sm, w_out, layer, *, tm):
    S, D = x.shape

    def body(x_ref, fm_ref, pm_ref, sm_ref, w_ref, y_ref):
        y = x_ref[...] + _dot(fm_ref[...], w_ref[0:FOX_W, :])
        y = y + _dot(pm_ref[...], w_ref[FOX_W:FOX_W + POOL_W, :])
        y_ref[...] = y + _dot(sm_ref[...], w_ref[FOX_W + POOL_W:D_MIX, :])

    row = lambda w: pl.BlockSpec((tm, w), lambda i: (i, 0))
    return pl.pallas_call(
        body, name="outproj", grid=(S // tm,),
        in_specs=[row(D), row(FOX_W), row(POOL_W), row(SB_W), pl.BlockSpec((None, D_MIX, D), lambda i: (layer, 0, 0))],
        out_specs=row(D), out_shape=jax.ShapeDtypeStruct((S, D), F32),
        compiler_params=_cparams(dimension_semantics=("arbitrary",)),
    )(x, fm, pm, sm, w_out)


def _loss_head(y, target, *, tm):
    S, D = y.shape

    def body(y_ref, t_ref, dy_ref, sq_ref):
        @pl.when(pl.program_id(0) == 0)
        def _():
            sq_ref[...] = jnp.zeros_like(sq_ref)

        d = y_ref[...] - t_ref[...]
        dy_ref[...] = d * (1.0 / D)
        sq_ref[...] += jnp.sum(d * d, axis=0, keepdims=True)

    row = pl.BlockSpec((tm, D), lambda i: (i, 0))
    return pl.pallas_call(
        body, name="loss_head", grid=(S // tm,),
        in_specs=[row, row], out_specs=[row, pl.BlockSpec((1, D), lambda i: (0, 0))],
        out_shape=[jax.ShapeDtypeStruct((S, D), F32), jax.ShapeDtypeStruct((1, D), F32)],
        compiler_params=_cparams(dimension_semantics=("arbitrary",)),
    )(y, target)


def _outproj_bwd(dy, fm, pm, sm, w_out, layer, stacks, *, tm):
    S, D = dy.shape

    def body(dy_ref, fm_ref, pm_ref, sm_ref, w_ref, dm_ref, dw_ref):
        @pl.when(pl.program_id(0) == 0)
        def _():
            dw_ref[...] = jnp.zeros_like(dw_ref)

        dyb = dy_ref[...].astype(BF16)
        dm_ref[...] = _dot_nt(dyb, w_ref[...])
        dw_ref[0:FOX_W, :] += _dot_tn(fm_ref[...], dyb)
        dw_ref[FOX_W:FOX_W + POOL_W, :] += _dot_tn(pm_ref[...], dyb)
        dw_ref[FOX_W + POOL_W:D_MIX, :] += _dot_tn(sm_ref[...], dyb)

    row = lambda w: pl.BlockSpec((tm, w), lambda i: (i, 0))
    wspec = pl.BlockSpec((None, D_MIX, D), lambda i: (layer, 0, 0))
    return _stack_call(
        body, "outproj_bwd", (S // tm,), [row(D), row(FOX_W), row(POOL_W), row(SB_W), wspec], (dy, fm, pm, sm, w_out),
        [pl.BlockSpec((None, D_MIX, D), lambda i: (layer, 0, 0))], [(D_MIX, D)], stacks,
        plain_specs=[row(D_MIX)], plain_shapes=[jax.ShapeDtypeStruct((S, D_MIX), F32)],
        compiler_params=_cparams(dimension_semantics=("arbitrary",)))


def _fox_bwd(qn, ka, kb, v, o, lse, dmix, projm, qkb, *, tq, tk):
    S = qn.shape[0]
    npair = FOX_HEADS // 2

    def body(q_ref, ka_ref, kb_ref, v_ref, o_ref, lse_ref, dm_ref, fg_ref, qkb_ref,
             dq_ref, dk_ref, dv_ref, dfg_ref, dct_ref, dcr_ref):
        qi = pl.program_id(1)

        @pl.when(qi == 0)
        def _():
            dk_ref[...] = jnp.zeros_like(dk_ref)
            dv_ref[...] = jnp.zeros_like(dv_ref)
            dct_ref[...] = jnp.zeros_like(dct_ref)

        lane = _iota((1, LANES), 1)
        ma = lane < HEAD_DIM
        qh = _pair_masks(q_ref[...])
        qaug = _aug_queries(q_ref[...])
        k_refs = (ka_ref, kb_ref)
        lsev = lse_ref[...]
        lse = (_lane_pick(lsev, lane, 0), _lane_pick(lsev, lane, HEAD_DIM))
        fg = fg_ref[...]
        silu, dsilu = _silu_pair(fg)
        dm = dm_ref[...]
        ov = o_ref[...]
        do = dm * silu
        dfg_ref[...] = dm * ov * dsilu
        dd = do * ov
        dsum = (jnp.sum(jnp.where(ma, dd, 0.0), axis=1, keepdims=True), jnp.sum(jnp.where(ma, 0.0, dd), axis=1, keepdims=True))
        doh = _pair_masks(do.astype(BF16))

        def block(k0, tkl, r0, carry, masked, heads=(0, 1)):
            vb = v_ref[pl.ds(k0, tkl), :]
            if masked:
                mask = (k0 + _iota((tq - r0, tkl), 1)) <= (qi * tq + r0 + _iota((tq - r0, tkl), 0))
            kaugs = {h: k_refs[h][pl.ds(k0, tkl), :] for h in heads}
            scores = {h: _dot_nt(qaug[h][r0:], kaugs[h]) for h in heads}
            dps = {h: _dot_nt(doh[h][r0:], vb) for h in heads}
            ps, dss = [], []
            rows = [carry[1], carry[2]]
            for h in heads:
                s = jnp.where(mask, scores[h], NEG) if masked else scores[h]
                p = jnp.exp(s - lse[h][r0:])
                dsf = p * (dps[h] - dsum[h][r0:])
                dct_ref[0, h:h + 1, pl.ds(k0, tkl)] -= jnp.sum(dsf, axis=0, keepdims=True)
                rows[h] = _put_rows(carry[1 + h], carry[1 + h][r0:] + jnp.sum(dsf, axis=1, keepdims=True), r0)
                ps.append(p.astype(BF16))
                dss.append(dsf.astype(BF16))
            dv_ref[pl.ds(k0, tkl), :] += _dot_tn(jnp.concatenate(ps, axis=0), jnp.concatenate([doh[h][r0:] for h in heads], axis=0))
            dk_ref[pl.ds(k0, tkl), :] += _dot_tn(jnp.concatenate(dss, axis=0), jnp.concatenate([qh[h][r0:] for h in heads], axis=0))
            kh = jnp.concatenate([_pair_masks(kaugs[h])[h] for h in heads], axis=0)
            dq = _put_rows(carry[0], carry[0][r0:] + _dot(jnp.concatenate(dss, axis=1), kh), r0)
            return (dq, rows[0], rows[1])

        zcol = jnp.zeros((tq, 1), F32)
        carry = (jnp.zeros((tq, LANES), F32), zcol, zcol)
        for off, size in _diag_tiles(tq):
            carry = block(pl.multiple_of(qi * tq + off, size), size, off, carry, True)
        floors = (jnp.min(lse[0]), jnp.min(lse[1]))
        dq, rowa, rowb = _fox_walk_left((qi * tq) // tk, tk, block, carry, k_refs, jnp.max(qkb_ref[...]), lambda c: floors)
        dq_ref[...] = dq * QK_SCALE
        dcr_ref[0] = jnp.where(ma, rowa, rowb)

    qblk = pl.BlockSpec((tq, LANES), lambda p, i: (i, p))
    kvblk = pl.BlockSpec((S, LANES), lambda p, i: (0, p))
    f32out = jax.ShapeDtypeStruct((S, FOX_W), F32)
    ctblk = pl.BlockSpec((1, FF_STRIDE, S), lambda p, i: (p, 0, 0))
    return pl.pallas_call(
        body, name="fox_bwd", grid=(npair, S // tq),
        in_specs=[qblk, kvblk, kvblk, kvblk, qblk, qblk, qblk,
                  pl.BlockSpec((tq, LANES), lambda p, i: (i, C_FG // LANES + p)),
                  pl.BlockSpec((1, LANES), lambda p, i: (0, 0))],
        out_specs=[qblk, kvblk, kvblk, qblk, ctblk, pl.BlockSpec((1, tq, LANES), lambda p, i: (p, i, 0))],
        out_shape=[f32out, f32out, f32out, f32out, jax.ShapeDtypeStruct((npair, FF_STRIDE, S), F32),
                   jax.ShapeDtypeStruct((npair, S, LANES), F32)],
        compiler_params=_cparams(dimension_semantics=("arbitrary", "arbitrary")),
    )(qn, ka, kb, v, o, lse, dmix, projm, qkb)


def _sb_bwd(sq, sk, sv, o, dmix, projm, *, tq, tk):
    S = sq.shape[0]
    npair = SB_HEADS // 2
    mix0 = (FOX_W + POOL_W) // LANES

    def body(q_ref, k_ref, v_ref, o_ref, dm_ref, sg_ref, dq_ref, dk_ref, dv_ref, dsg_ref):
        qi = pl.program_id(1)

        @pl.when(qi == 0)
        def _():
            dk_ref[...] = jnp.zeros_like(dk_ref)
            dv_ref[...] = jnp.zeros_like(dv_ref)

        lane = _iota((1, LANES), 1)
        ma = lane < HEAD_DIM
        qh = _pair_masks(q_ref[...])
        sg = sg_ref[...]
        silu, dsilu = _silu_pair(sg)
        dm = dm_ref[...]
        ov = o_ref[...]
        do = dm * silu
        dsg_ref[...] = dm * ov * dsilu
        dob = do.astype(BF16)
        dd = dob.astype(F32) * ov
        dsum = (jnp.sum(jnp.where(ma, dd, 0.0), axis=1, keepdims=True), jnp.sum(jnp.where(ma, 0.0, dd), axis=1, keepdims=True))
        doh = _pair_masks(dob)
        tmat2 = _suffix_matrix(tk, inclusive=False)
        tmat2_inc = _suffix_matrix(tk, inclusive=True)
        nfull = (qi * tq) // tk

        def block(k0, r0, carry, masked):
            nr = tq - r0
            kb = k_ref[pl.ds(k0, tk), :]
            vb = v_ref[pl.ds(k0, tk), :]
            kh = _pair_masks(kb)
            causal = (k0 + _iota((nr, tk), 1)) < (qi * tq + r0 + _iota((nr, tk), 0)) if masked else None
            heads = range(2)
            qs = [q[r0:] for q in qh]
            dos = [d[r0:] for d in doh]
            das = [_dot_nt(dos[h], vb) for h in heads]
            zs, nsps, lbs, a_s = _sb_scores(qs, kb, causal, tmat2, [carry[h][0][r0:] for h in heads])
            abs_ = [a.astype(BF16) for a in a_s]
            us = [abs_[h].astype(F32) * das[h] for h in heads]
            uins = [_suffix_sums(u, tmat2_inc) for u in us]
            dzs = []
            for h in heads:
                cum_u = dsum[h][r0:] - (uins[h] + carry[h][1][r0:])
                dz = us[h] * jnp.exp(nsps[h]) - jnp.exp(zs[h] + nsps[h]) * cum_u
                if masked:
                    dz = jnp.where(causal, dz, 0.0)
                dzs.append(dz.astype(BF16))
            dv_ref[pl.ds(k0, tk), :] += _dot_tn(jnp.concatenate(abs_, axis=0), jnp.concatenate(dos, axis=0))
            dk_ref[pl.ds(k0, tk), :] += _dot_tn(jnp.concatenate(dzs, axis=0), jnp.concatenate(qs, axis=0))
            dq = _put_rows(carry[2], carry[2][r0:] + _dot(jnp.concatenate(dzs, axis=1), jnp.concatenate(kh, axis=0)), r0)
            new = [(_put_rows(carry[h][0], carry[h][0][r0:] + jnp.sum(lbs[h], axis=1, keepdims=True), r0),
                    _put_rows(carry[h][1], carry[h][1][r0:] + jnp.sum(us[h], axis=1, keepdims=True), r0)) for h in heads]
            return (new[0], new[1], dq)

        zcol = jnp.zeros((tq, 1), F32)
        carry = ((zcol, zcol), (zcol, zcol), jnp.zeros((tq, LANES), F32))
        for off, size in reversed(_diag_tiles(tq)):
            assert size == tk
            carry = block(pl.multiple_of(qi * tq + off, tk), off, carry, True)
        dq = _sb_walk_left(nfull, tk, block, carry, lambda c: (c[0][0], c[1][0]))[2]
        dq_ref[...] = dq * QK_SCALE

    qblk = pl.BlockSpec((tq, LANES), lambda p, i: (i, p))
    kvblk = pl.BlockSpec((S, LANES), lambda p, i: (0, p))
    f32out = jax.ShapeDtypeStruct((S, SB_W), F32)
    return pl.pallas_call(
        body, name="sb_bwd", grid=(npair, S // tq),
        in_specs=[qblk, kvblk, kvblk, qblk,
                  pl.BlockSpec((tq, LANES), lambda p, i: (i, mix0 + p)),
                  pl.BlockSpec((tq, LANES), lambda p, i: (i, C_SG // LANES + p))],
        out_specs=[qblk, kvblk, kvblk, qblk],
        out_shape=[f32out, f32out, f32out, f32out],
        compiler_params=_cparams(dimension_semantics=("arbitrary", "arbitrary")),
    )(sq, sk, sv, o, dmix, projm)


def _prep_bwd(projm, ffo, dqn, dkn, dct, dcr, dv, dfg, dsq, dsk, dsv, dsg, dmix, pooled, yp, qg, kg, bfp, wpd, ps, *, ts):
    S = projm.shape[0]
    nb = S // ts
    hb = ts // POOL_HALO
    npair = FOX_HEADS // 2
    last_halo = S // POOL_HALO - 1

    def body(fq_ref, fk_ref, pp_ref, pph_ref, ff_ref,
             dqn_ref, dkn_ref, dct_ref, dcr_ref, dv_ref, dfg_ref, dsq_ref, dsk_ref, dsv_ref, dsg_ref,
             dmp_ref, dmh_ref, pooled_ref, yp_ref, qg_ref, kg_ref, bf_ref, wpd_ref, ps_ref,
             dp_ref, dqg_ref, dkg_ref, dbf_ref, dwp_ref, dps_ref,
             carry_ref, dl_ref, buf_ref, dct_s):
        i = pl.program_id(0)
        blk = nb - 1 - i

        @pl.when(i == 0)
        def _():
            carry_ref[...] = jnp.zeros_like(carry_ref)
            dqg_ref[...] = jnp.zeros_like(dqg_ref)
            dkg_ref[...] = jnp.zeros_like(dkg_ref)
            dbf_ref[...] = jnp.zeros_like(dbf_ref)
            dwp_ref[...] = jnp.zeros_like(dwp_ref)
            dps_ref[...] = jnp.zeros_like(dps_ref)

        bd = _head_blockdiag()
        for raw_ref, g_ref, dn, dg_ref, col in ((fq_ref, qg_ref, dqn_ref[...], dqg_ref, C_FQ), (fk_ref, kg_ref, dkn_ref[...], dkg_ref, C_FK)):
            q = raw_ref[...]
            rstd = lax.rsqrt(_group_sum(q * q, bd) * (1.0 / HEAD_DIM) + EPS)
            xhat = q * rstd
            dg_ref[...] += jnp.sum(dn * xhat, axis=0, keepdims=True)
            dyg = dn * g_ref[...]
            mean = _group_sum(dyg * xhat, bd) * (1.0 / HEAD_DIM)
            dp_ref[:, col:col + FOX_W] = (rstd * (dyg - xhat * mean)).astype(BF16)
        dp_ref[:, C_FV:C_FV + FOX_W] = dv_ref[...].astype(BF16)
        dp_ref[:, C_FG:C_FG + FOX_W] = dfg_ref[...].astype(BF16)
        dp_ref[:, C_SQ:C_SQ + SB_W] = dsq_ref[...].astype(BF16)
        dp_ref[:, C_SK:C_SK + SB_W] = dsk_ref[...].astype(BF16)
        dp_ref[:, C_SV:C_SV + SB_W] = dsv_ref[...].astype(BF16)
        dp_ref[:, C_SG:C_SG + SB_W] = dsg_ref[...].astype(BF16)

        dct_s[...] = jnp.zeros_like(dct_s)
        for p in range(npair):
            dct_s[FF_STRIDE * p:FF_STRIDE * (p + 1), :] = dct_ref[p]
        dc = dct_s[...].T
        lane = _iota((1, LANES), 1)
        for p in range(npair):
            dcr = dcr_ref[p]
            dc = dc + jnp.where(lane == FF_STRIDE * p, _lane_pick(dcr, lane, 0), 0.0)
            dc = dc + jnp.where(lane == FF_STRIDE * p + 1, _lane_pick(dcr, lane, HEAD_DIM), 0.0)
        triu = _ones_where(_iota((ts, ts), 1) >= _iota((ts, ts), 0))
        dlf = _dot_exact_lhs(triu, dc) + carry_ref[...]
        dl_ref[...] = dlf
        carry_ref[...] = dl_ref[0:1, :]
        z = ff_ref[...] + bf_ref[...]
        dff = dlf * (1.0 / (1.0 + jnp.exp(z)))
        dbf_ref[...] += jnp.sum(dff, axis=0, keepdims=True)
        dp_ref[:, PM:PW] = dff.astype(BF16)

        psv = ps_ref[...]
        wpdv = wpd_ref[...]
        lane_group = _iota((1, POOL_W), 1) >> HEAD_SHIFT
        wlen = _pool_group_select(lane_group, [float(w) for w in POOL_WINDOWS])
        pg = pp_ref[:, POOL_W:2 * POOL_W]
        silu, dsilu = _silu_pair(pg)
        dmp = dmp_ref[...]
        ypv = yp_ref[...]
        dp_ref[:, C_PG:C_PG + POOL_W] = (dmp * (ypv * psv) * dsilu).astype(BF16)
        dps_ref[...] += jnp.sum(dmp * silu * ypv, axis=0, keepdims=True)
        dyp = (dmp * psv * silu).astype(BF16)
        dwp_ref[...] += _dot_tn(pooled_ref[...], dyp)
        dpooled = _dot_nt(dyp, wpdv)
        pgh = pph_ref[:, POOL_W:2 * POOL_W]
        dyph = (dmh_ref[...] * psv * (pgh * _sigmoid(pgh))).astype(BF16)
        dpooled_h = jnp.where(blk < nb - 1, _dot_nt(dyph, wpdv), 0.0)
        tpos = (blk * ts + _iota((ts, 1), 0) + 1).astype(F32)
        ev = dpooled / jnp.minimum(tpos, wlen)
        buf_ref[0:ts, :] = ev
        buf_ref[ts:ts + POOL_HALO, :] = dpooled_h / wlen
        acc = ev
        snaps = []
        for d in range(1, POOL_HALO):
            acc = acc + buf_ref[pl.ds(d, ts), :]
            if d + 1 in POOL_WINDOWS:
                snaps.append(acc)
        dp_ref[:, C_PX:C_PX + POOL_W] = (_pool_group_select(lane_group, snaps) - dpooled).astype(BF16)

    rblk = lambda w, c: pl.BlockSpec((ts, w), lambda i: (nb - 1 - i, c))
    full = lambda a: pl.BlockSpec(a.shape, lambda i: (0,) * a.ndim)
    halo = lambda w, c: pl.BlockSpec((POOL_HALO, w), lambda i: (jnp.minimum((nb - i) * hb, last_halo), c))
    acc_spec = lambda r, w: pl.BlockSpec((r, w), lambda i: (0, 0))
    return pl.pallas_call(
        body, name="prep_bwd", grid=(nb,),
        in_specs=[rblk(FOX_W, C_FQ // FOX_W), rblk(FOX_W, C_FK // FOX_W), rblk(2 * POOL_W, C_PX // (2 * POOL_W)),
                  halo(2 * POOL_W, C_PX // (2 * POOL_W)), rblk(LANES, 0),
                  rblk(FOX_W, 0), rblk(FOX_W, 0), pl.BlockSpec((npair, FF_STRIDE, ts), lambda i: (0, 0, nb - 1 - i)),
                  pl.BlockSpec((npair, ts, LANES), lambda i: (0, nb - 1 - i, 0)), rblk(FOX_W, 0), rblk(FOX_W, 0),
                  rblk(SB_W, 0), rblk(SB_W, 0), rblk(SB_W, 0), rblk(SB_W, 0),
                  rblk(POOL_W, FOX_W // POOL_W), halo(POOL_W, FOX_W // POOL_W), rblk(POOL_W, 0), rblk(POOL_W, 0),
                  full(qg), full(kg), full(bfp), full(wpd), full(ps)],
        out_specs=[rblk(PW, 0), acc_spec(1, FOX_W), acc_spec(1, FOX_W), acc_spec(1, LANES), acc_spec(POOL_W, POOL_W), acc_spec(1, POOL_W)],
        out_shape=[jax.ShapeDtypeStruct((S, PW), BF16), jax.ShapeDtypeStruct((1, FOX_W), F32), jax.ShapeDtypeStruct((1, FOX_W), F32),
                   jax.ShapeDtypeStruct((1, LANES), F32), jax.ShapeDtypeStruct((POOL_W, POOL_W), F32), jax.ShapeDtypeStruct((1, POOL_W), F32)],
        scratch_shapes=[pltpu.VMEM((1, LANES), F32), pltpu.VMEM((ts, LANES), F32), pltpu.VMEM((ts + POOL_HALO, POOL_W), F32),
                        pltpu.VMEM((LANES, ts), F32)],
        compiler_params=_cparams(dimension_semantics=("arbitrary",)),
    )(projm, projm, projm, projm, ffo, dqn, dkn, dct, dcr, dv, dfg, dsq, dsk, dsv, dsg, dmix, dmix, pooled, yp, qg, kg, bfp, wpd, ps)


def _stack_call(body, name, grid, in_specs, operands, slot_specs, slot_shapes, stacks, plain_specs=(), plain_shapes=(), **kw):
    out_specs = list(plain_specs) + list(slot_specs)
    out_shape = list(plain_shapes) + [jax.ShapeDtypeStruct((DEPTH,) + s, F32) for s in slot_shapes]
    if stacks is None:
        return pl.pallas_call(body, name=name, grid=grid, in_specs=in_specs, out_specs=out_specs, out_shape=out_shape, **kw)(*operands)
    n = len(operands)

    def aliased_body(*refs):
        body(*refs[:n], *refs[n + len(stacks):])

    return pl.pallas_call(
        aliased_body, name=name, grid=grid, in_specs=list(in_specs) + [pl.BlockSpec(memory_space=pl.ANY)] * len(stacks),
        out_specs=out_specs, out_shape=out_shape,
        input_output_aliases={n + k: len(plain_specs) + k for k in range(len(stacks))}, **kw)(*operands, *stacks)


def _inproj_dw(h, dproj, layer, stacks, *, ts, tn):
    S, D = h.shape
    nj = PM // tn

    def body(h_ref, dp_ref, dpf_ref, dw_ref, dwf_ref):
        s = pl.program_id(1)

        @pl.when(s == 0)
        def _():
            dw_ref[...] = jnp.zeros_like(dw_ref)

        @pl.when((s == 0) & (pl.program_id(0) == 0))
        def _():
            dwf_ref[...] = jnp.zeros_like(dwf_ref)

        hv = h_ref[...]
        dw_ref[...] += _dot_tn(dp_ref[...], hv)

        @pl.when(pl.program_id(0) == 0)
        def _():
            dwf_ref[...] += _dot_tn(dpf_ref[...], hv)

    return _stack_call(
        body, "inproj_dw", (nj, S // ts),
        [pl.BlockSpec((ts, D), lambda j, s: (s, 0)),
         pl.BlockSpec((ts, tn), lambda j, s: (s, j)),
         pl.BlockSpec((ts, LANES), lambda j, s: (s, PM // LANES))],
        (h, dproj, dproj),
        [pl.BlockSpec((None, tn, D), lambda j, s: (layer, j, 0)), pl.BlockSpec((None, LANES, D), lambda j, s: (layer, 0, 0))],
        [(PM, D), (LANES, D)], stacks,
        compiler_params=_cparams(dimension_semantics=("arbitrary", "arbitrary")))


def _inproj_dx(dproj, wt_all, layer, x, g, dy, *, tm):
    S, D = x.shape

    def body(dp_ref, w_ref, x_ref, g_ref, dy_ref, dx_ref, dg_ref):
        @pl.when(pl.program_id(0) == 0)
        def _():
            dg_ref[...] = jnp.zeros_like(dg_ref)

        dh = _dot(dp_ref[...], w_ref[...])
        xf = x_ref[...]
        rstd = lax.rsqrt(jnp.mean(xf * xf, axis=-1, keepdims=True) + EPS)
        xhat = xf * rstd
        dg_ref[...] += jnp.sum(dh * xhat, axis=0, keepdims=True)
        dyg = dh * g_ref[...]
        mean = jnp.mean(dyg * xhat, axis=-1, keepdims=True)
        dx_ref[...] = rstd * (dyg - xhat * mean) + dy_ref[...]

    row = lambda w: pl.BlockSpec((tm, w), lambda i: (i, 0))
    return pl.pallas_call(
        body, name="inproj_dx", grid=(S // tm,),
        in_specs=[row(PW), pl.BlockSpec((None, PW, D), lambda i: (layer, 0, 0)), row(D), pl.BlockSpec((1, D), lambda i: (0, 0)), row(D)],
        out_specs=[row(D), pl.BlockSpec((1, D), lambda i: (0, 0))],
        out_shape=[jax.ShapeDtypeStruct((S, D), F32), jax.ShapeDtypeStruct((1, D), F32)],
        compiler_params=_cparams(dimension_semantics=("arbitrary",)),
    )(dproj, wt_all, x, g, dy)


def _adam_update(w, g, m, v):
    nm = ADAM_B1 * m + (1.0 - ADAM_B1) * g
    nv = ADAM_B2 * v + (1.0 - ADAM_B2) * (g * g)
    m_hat = nm / (1.0 - ADAM_B1 ** ADAM_STEP)
    v_hat = nv / (1.0 - ADAM_B2 ** ADAM_STEP)
    return -ADAM_LR * (m_hat / (jnp.sqrt(v_hat) + ADAM_EPS) + ADAM_WD * w), nm, nv


def _adamw(w, g, m, v):
    L, R, C = w.shape
    tr = R if R <= 512 else 256

    def body(w_ref, g_ref, m_ref, v_ref, d_ref, nm_ref, nv_ref):
        d_ref[...], nm_ref[...], nv_ref[...] = _adam_update(w_ref[...], g_ref[...], m_ref[...], v_ref[...])

    spec = pl.BlockSpec((1, tr, C), lambda l, i: (l, i, 0))
    shp = jax.ShapeDtypeStruct((L, R, C), F32)
    return pl.pallas_call(
        body, name="adamw", grid=(L, R // tr), in_specs=[spec] * 4, out_specs=[spec] * 3, out_shape=[shp] * 3,
        compiler_params=_cparams(dimension_semantics=("arbitrary", "arbitrary")),
    )(w, g, m, v)


def _adamw_nd(w, g, m, v):
    shape = w.shape
    view = (1,) + shape if w.ndim == 2 else (shape[0], -1, shape[-1])
    outs = _adamw(w.reshape(view), g.reshape(view), m.reshape(view), v.reshape(view))
    return tuple(o.reshape(shape) for o in outs)


FLIP_C = (0, 0, 1)
FLIP_X = (1, 0, 0)
FLIP_Y = (0, 1, 0)
FLIP_XY = (1, 1, 0)
MESH = pl.DeviceIdType.MESH


def _peer(flip):
    me = (lax.axis_index("x"), lax.axis_index("y"), lax.axis_index("c"))
    return tuple(1 - a if f else a for a, f in zip(me, flip))


def _exchange(name, arrays, flips):
    n = len(arrays)

    def body(*refs):
        srcs, dsts = refs[:n], refs[n:2 * n]
        send_sems, recv_sems = refs[2 * n:]
        copies = [pltpu.make_async_remote_copy(src_ref=srcs[k], dst_ref=dsts[k], send_sem=send_sems.at[k], recv_sem=recv_sems.at[k],
                                               device_id=_peer(flips[k]), device_id_type=MESH) for k in range(n)]
        for cp in copies:
            cp.start()
        for cp in copies:
            cp.wait()

    anyspec = pl.BlockSpec(memory_space=pl.ANY)
    return pl.pallas_call(
        body, name=name, in_specs=[anyspec] * n, out_specs=[anyspec] * n,
        out_shape=[jax.ShapeDtypeStruct(a.shape, a.dtype) for a in arrays],
        scratch_shapes=[pltpu.SemaphoreType.DMA((n,)), pltpu.SemaphoreType.DMA((n,))],
    )(*arrays)


def _exchange_add(name, x, flip):
    def body(x_ref, o_ref, buf_ref, send_sem, recv_sem):
        cp = pltpu.make_async_remote_copy(src_ref=x_ref, dst_ref=buf_ref, send_sem=send_sem, recv_sem=recv_sem,
                                          device_id=_peer(flip), device_id_type=MESH)
        cp.start()
        cp.wait()
        o_ref[...] = x_ref[...] + buf_ref[...]

    vspec = pl.BlockSpec(memory_space=pltpu.VMEM)
    return pl.pallas_call(
        body, name=name, in_specs=[vspec], out_specs=vspec, out_shape=jax.ShapeDtypeStruct(x.shape, x.dtype),
        scratch_shapes=[pltpu.VMEM(x.shape, x.dtype), pltpu.SemaphoreType.DMA, pltpu.SemaphoreType.DMA],
    )(x)


def _chip_index():
    return 2 * lax.axis_index("x") + lax.axis_index("y")


def _gather_weights(w_in_t, w_out):
    wi = w_in_t.astype(BF16)
    wo = jnp.swapaxes(w_out, 0, 1).astype(BF16)
    halves = (wi.shape[0] // 2, wo.shape[0] // 2)
    ARR = 2
    TO_X, TO_Y, ON_Y, ON_X, SIB_X, SIB_Y, SIB_D0, SIB_D1, OWN = [ARR * k for k in range(9)]
    n_sems = ARR * 9

    def body(wi_ref, wo_ref, gi_ref, go_ref, send_sems, recv_sems):
        c = lax.axis_index("c")
        j = _chip_index()
        srcs = (wi_ref, wo_ref)
        dsts = (gi_ref, go_ref)
        def cuts(core):
            return [(pl.ds(h * core, h), pl.ds(h * core, h // 2), pl.ds(h * core + h // 2, h - h // 2)) for h in halves]
        mine, theirs = cuts(c), cuts(1 - c)
        HALF, Q0, Q1 = 0, 1, 2

        def copy(idx, src, dst, flip):
            return pltpu.make_async_remote_copy(src_ref=src, dst_ref=dst, send_sem=send_sems.at[idx], recv_sem=recv_sems.at[idx],
                                                device_id=_peer(flip), device_id_type=MESH)

        def slot(a, shard, cut):
            return dsts[a].at[shard, cut]

        jx, jy, jd = j ^ 2, j ^ 1, j ^ 3
        sends = []

        def start(cp):
            cp.start()
            sends.append(cp)

        for a in range(ARR):
            start(copy(TO_X + a, srcs[a].at[mine[a][HALF]], slot(a, j, mine[a][HALF]), FLIP_X))
            start(copy(TO_Y + a, srcs[a].at[mine[a][HALF]], slot(a, j, mine[a][HALF]), FLIP_Y))
        own = [copy(OWN + a, srcs[a], dsts[a].at[j], FLIP_C) for a in range(ARR)]
        for cp in own:
            cp.start()
        for a in range(ARR):
            copy(TO_X + a, slot(a, jx, mine[a][HALF]), slot(a, jx, mine[a][HALF]), FLIP_X).wait_recv()
            start(copy(ON_Y + a, slot(a, jx, mine[a][Q0]), slot(a, jx, mine[a][Q0]), FLIP_Y))
            start(copy(SIB_X + a, slot(a, jx, mine[a][HALF]), slot(a, jx, mine[a][HALF]), FLIP_C))
        for a in range(ARR):
            copy(TO_Y + a, slot(a, jy, mine[a][HALF]), slot(a, jy, mine[a][HALF]), FLIP_Y).wait_recv()
            start(copy(ON_X + a, slot(a, jy, mine[a][Q1]), slot(a, jy, mine[a][Q1]), FLIP_X))
            start(copy(SIB_Y + a, slot(a, jy, mine[a][HALF]), slot(a, jy, mine[a][HALF]), FLIP_C))
        for a in range(ARR):
            copy(ON_Y + a, slot(a, jd, mine[a][Q0]), slot(a, jd, mine[a][Q0]), FLIP_Y).wait_recv()
            start(copy(SIB_D0 + a, slot(a, jd, mine[a][Q0]), slot(a, jd, mine[a][Q0]), FLIP_C))
        for a in range(ARR):
            copy(ON_X + a, slot(a, jd, mine[a][Q1]), slot(a, jd, mine[a][Q1]), FLIP_X).wait_recv()
            start(copy(SIB_D1 + a, slot(a, jd, mine[a][Q1]), slot(a, jd, mine[a][Q1]), FLIP_C))
        for a in range(ARR):
            for idx, shard, cut in ((SIB_X, jx, HALF), (SIB_Y, jy, HALF), (SIB_D0, jd, Q0), (SIB_D1, jd, Q1)):
                copy(idx + a, slot(a, shard, theirs[a][cut]), slot(a, shard, theirs[a][cut]), FLIP_C).wait_recv()
        for cp in own:
            cp.wait()
        for cp in sends:
            cp.wait_send()

    anyspec = pl.BlockSpec(memory_space=pl.ANY)
    gi, go = pl.pallas_call(
        body, name="gather_weights", in_specs=[anyspec] * 2, out_specs=[anyspec] * 2,
        out_shape=[jax.ShapeDtypeStruct((4,) + wi.shape, BF16), jax.ShapeDtypeStruct((4,) + wo.shape, BF16)],
        scratch_shapes=[pltpu.SemaphoreType.DMA((n_sems,)), pltpu.SemaphoreType.DMA((n_sems,))],
    )(wi, wo)
    w_in_t_full = gi.reshape((4 * wi.shape[0],) + wi.shape[1:])
    w_out_full = jnp.swapaxes(go.reshape((4 * wo.shape[0],) + wo.shape[1:]), 0, 1)
    return w_in_t_full, w_out_full


def _to_aligned(w_t):
    _, L, D = w_t.shape
    npair = FOX_HEADS // 2
    ff = w_t[ORIG_FF:ORIG_REST].reshape(npair, 2, L, D)
    ff = jnp.pad(ff, ((0, 0), (0, FF_STRIDE - 2), (0, 0), (0, 0))).reshape(npair * FF_STRIDE, L, D)
    ff = jnp.pad(ff, ((0, LANES - npair * FF_STRIDE), (0, 0), (0, 0)))
    return jnp.swapaxes(jnp.concatenate([w_t[:ORIG_FOX], w_t[ORIG_REST:], ff], axis=0), 0, 1)


def _from_aligned(dw_t):
    n, _, D = dw_t.shape
    npair = FOX_HEADS // 2
    ff = dw_t[:, PM:PM + npair * FF_STRIDE].reshape(n, npair, FF_STRIDE, D)[:, :, :2].reshape(n, FOX_HEADS, D)
    return jnp.swapaxes(jnp.concatenate([dw_t[:, :ORIG_FOX], ff, dw_t[:, ORIG_FOX:PM]], axis=1), 0, 1)


def _half_layers(name, stack, got):
    L, R, C = stack.shape
    half = L // 2
    tr = min(256, R)
    c = lax.axis_index("c")
    which = ((1 - c) if got is None else c).astype(jnp.int32).reshape(1)

    def body(c_ref, x_ref, *refs):
        if got is None:
            refs[0][...] = x_ref[...].astype(BF16)
        else:
            acc = x_ref[...] + refs[0][...].astype(F32)
            refs[1][...] = acc
            refs[2][...] = acc.astype(BF16)

    plain = pl.BlockSpec((1, tr, C), lambda l, i, c_ref: (l, i, 0))
    picked = pl.BlockSpec((1, tr, C), lambda l, i, c_ref: (c_ref[0] * half + l, i, 0))
    shp = lambda dt: jax.ShapeDtypeStruct((half, R, C), dt)
    grid_spec = pltpu.PrefetchScalarGridSpec(
        num_scalar_prefetch=1, grid=(half, R // tr),
        in_specs=[picked] + ([] if got is None else [plain]), out_specs=[plain] if got is None else [plain, plain])
    return pl.pallas_call(
        body, name=name, grid_spec=grid_spec, out_shape=[shp(BF16)] if got is None else [shp(F32), shp(BF16)],
        compiler_params=_cparams(dimension_semantics=("arbitrary", "arbitrary")),
    )(which, stack, *([] if got is None else [got]))


def _reduce_scatter(stack_m, stack_f, stack_o, shard_cols, shard_rows):
    j = _chip_index()
    half = DEPTH // 2
    stacks = (stack_m, stack_f, stack_o)
    give = [_half_layers("rs_give", s, None)[0] for s in stacks]
    got = _exchange("rs_d2d", give, (FLIP_C,) * len(stacks))
    (m32, mbf), (f32_, fbf), (o32, obf) = [_half_layers("rs_add_chip", s, g) for s, g in zip(stacks, got)]
    d_model = stack_m.shape[2]

    def in_shards(m, f):
        return _from_aligned(jnp.concatenate([m, f], axis=1)).reshape(4, shard_cols, half, d_model)

    def out_shards(o):
        return jnp.moveaxis(o.reshape(half, 4, shard_rows, o.shape[-1]), 1, 0)

    chip = [(in_shards(m32, f32_), in_shards(mbf, fbf), 0), (out_shards(o32), out_shards(obf), 1)]
    shard = lambda a, idx: lax.dynamic_index_in_dim(a, idx, axis=0, keepdims=False)
    via = []
    for _, bf, axis in chip:
        diag = shard(bf, j ^ 3)
        cut = diag.shape[axis] // 2
        via += [lax.slice_in_dim(diag, 0, cut, axis=axis), lax.slice_in_dim(diag, cut, 2 * cut, axis=axis)]
    handed = _exchange("rs_via", via, (FLIP_X, FLIP_Y) * len(chip))
    sends = []
    for a, (f32_sum, _, axis) in enumerate(chip):
        sends.append(_add_half_along("rs_add_via", shard(f32_sum, j ^ 2), handed[2 * a + 1], axis, 1))
        sends.append(_add_half_along("rs_add_via", shard(f32_sum, j ^ 1), handed[2 * a], axis, 0))
    got = _exchange("rs_ici", sends, (FLIP_X, FLIP_Y) * len(chip))
    own_in, own_out = [shard(f32_sum, j) for f32_sum, _, _ in chip]
    mine_in = _add_rows("rs_add_in", own_in, list(got[0:2]))
    mine_out = _add_into_half("rs_add_out", own_out, list(got[2:4]))
    sib_in, g_out = _share_halves(mine_in, mine_out)
    return (mine_in, sib_in), g_out


def _add_half_along(name, base, extra, axis, which):
    lanes = min(ROW_LANE_CHUNK, base.shape[2])
    assert base.shape[axis] == 2 * extra.shape[axis]
    blk = tuple(base.shape[d] // 2 if d == axis else base.shape[d] for d in range(2)) + (lanes,)

    def body(b_ref, e_ref, o_ref):
        x = b_ref[...]
        o_ref[...] = jnp.where(pl.program_id(0) == which, x + e_ref[...].astype(F32), x).astype(BF16)

    at = lambda i, k: (i, 0, k) if axis == 0 else (0, i, k)
    return pl.pallas_call(
        body, name=name, grid=(2, base.shape[2] // lanes),
        in_specs=[pl.BlockSpec(blk, at), pl.BlockSpec(blk, lambda i, k: (0, 0, k))], out_specs=pl.BlockSpec(blk, at),
        out_shape=jax.ShapeDtypeStruct(base.shape, BF16),
        compiler_params=_cparams(dimension_semantics=("arbitrary", "arbitrary")),
    )(base, extra)


def _add_rows(name, first, others):
    n = len(others)

    def body(*refs):
        acc = refs[0][...]
        for r in refs[1:1 + n]:
            acc = acc + r[...].astype(F32)
        refs[1 + n][...] = acc

    grid, spec = _row_lane_blocks(first.shape)
    return pl.pallas_call(
        body, name=name, grid=grid, in_specs=[spec(first.shape[1])] * (1 + n), out_specs=spec(first.shape[1]),
        out_shape=jax.ShapeDtypeStruct(first.shape, F32),
        compiler_params=_cparams(dimension_semantics=("arbitrary", "arbitrary")),
    )(first, *others)


ROW_LANE_CHUNK = 256


def _row_lane_blocks(shape):
    rows, _, C = shape
    tr = rows // 2 if rows % 2 == 0 and rows > 64 else rows
    lanes = min(ROW_LANE_CHUNK, C)
    return (rows // tr, C // lanes), lambda n_mid: pl.BlockSpec((tr, n_mid, lanes), lambda i, k, *_: (i, 0, k))


def _add_into_half(name, first, others):
    half, rows, C = first.shape
    tr = min(256, rows)
    n = len(others)

    def body(c_ref, *refs):
        acc = refs[0][...]
        for r in refs[1:1 + n]:
            acc = acc + r[...].astype(F32)
        refs[1 + n][...] = acc

    grid_spec = pltpu.PrefetchScalarGridSpec(
        num_scalar_prefetch=1, grid=(half, rows // tr),
        in_specs=[pl.BlockSpec((1, tr, C), lambda l, i, c_ref: (l, i, 0))] * (1 + n),
        out_specs=pl.BlockSpec((1, tr, C), lambda l, i, c_ref: (c_ref[0] * half + l, i, 0)))
    return pl.pallas_call(
        body, name=name, grid_spec=grid_spec, out_shape=jax.ShapeDtypeStruct((2 * half, rows, C), F32),
        compiler_params=_cparams(dimension_semantics=("arbitrary", "arbitrary")),
    )(lax.axis_index("c").astype(jnp.int32).reshape(1), first, *others)


def _share_halves(mine, buf):
    half = DEPTH // 2

    def body(mine_ref, buf_in, sib_ref, buf_ref, send_sems, recv_sems):
        lay = pl.ds(half * lax.axis_index("c"), half)
        copies = [pltpu.make_async_remote_copy(src_ref=src, dst_ref=dst, send_sem=send_sems.at[k], recv_sem=recv_sems.at[k],
                                               device_id=_peer(FLIP_C), device_id_type=MESH)
                  for k, (src, dst) in enumerate(((mine_ref, sib_ref), (buf_ref.at[lay], buf_ref.at[lay])))]
        for cp in copies:
            cp.start()
        for cp in copies:
            cp.wait()

    anyspec = pl.BlockSpec(memory_space=pl.ANY)
    return pl.pallas_call(
        body, name="rs_share", in_specs=[anyspec] * 2, out_specs=[anyspec] * 2,
        out_shape=[jax.ShapeDtypeStruct(mine.shape, mine.dtype), jax.ShapeDtypeStruct(buf.shape, buf.dtype)],
        input_output_aliases={1: 1},
        scratch_shapes=[pltpu.SemaphoreType.DMA((2,)), pltpu.SemaphoreType.DMA((2,))],
    )(mine, buf)


def _adamw_halves(w, g_mine, g_sib, m, v):
    half = g_mine.shape[1]

    def body(c_ref, w_ref, gm_ref, gs_ref, m_ref, v_ref, g_ref, d_ref, nm_ref, nv_ref):
        first = c_ref[0] == 0
        gm, gs = gm_ref[...], gs_ref[...]
        for h, gv in enumerate((jnp.where(first, gm, gs), jnp.where(first, gs, gm))):
            lay = slice(half * h, half * (h + 1))
            g_ref[:, lay, :] = gv
            d_ref[:, lay, :], nm_ref[:, lay, :], nv_ref[:, lay, :] = _adam_update(w_ref[:, lay, :], gv, m_ref[:, lay, :], v_ref[:, lay, :])

    grid, spec = _row_lane_blocks(w.shape)
    full, part = spec(w.shape[1]), spec(half)
    grid_spec = pltpu.PrefetchScalarGridSpec(num_scalar_prefetch=1, grid=grid, in_specs=[full, part, part, full, full], out_specs=[full] * 4)
    return pl.pallas_call(
        body, name="adamw_halves", grid_spec=grid_spec, out_shape=[jax.ShapeDtypeStruct(w.shape, F32)] * 4,
        compiler_params=_cparams(dimension_semantics=("arbitrary", "arbitrary")),
    )(lax.axis_index("c").astype(jnp.int32).reshape(1), w, g_mine, g_sib, m, v)


def _all_reduce_small(x):
    x = _exchange_add("ar_c", x, FLIP_C)
    x = _exchange_add("ar_y", x, FLIP_Y)
    return _exchange_add("ar_x", x, FLIP_X)


def _blocks(S):
    return dict(tm=min(512, S), tm_proj=min(1024, S), ts=min(512, S), tq=min(512, S), tq_big=min(1024, S), tk=min(512, S), tks=min(256, S))


def _pair_pad(vec):
    npair = FOX_HEADS // 2
    v = jnp.pad(vec.reshape(npair, 2), ((0, 0), (0, FF_STRIDE - 2))).reshape(1, npair * FF_STRIDE)
    return jnp.pad(v, ((0, 0), (0, LANES - npair * FF_STRIDE)))


def _pair_unpad(row):
    npair = FOX_HEADS // 2
    return row[0, :npair * FF_STRIDE].reshape(npair, FF_STRIDE)[:, :2].reshape(FOX_HEADS)


def _pool_blockdiag(w_pool):
    g, cg, _ = w_pool.shape
    eye = jnp.eye(g, dtype=w_pool.dtype)
    return jnp.einsum("gh,gcd->gchd", eye, w_pool).reshape(g * cg, g * cg)


QK_BOUND_SLACK = 1.05


def _layer_params(norm_g, b_f, q_norm_g, k_norm_g, w_pool, pool_scale):
    qk_bound = QK_BOUND_SLACK * HEAD_DIM * QK_SCALE * jnp.max(jnp.abs(q_norm_g)) * jnp.max(jnp.abs(k_norm_g))
    return dict(g=norm_g.reshape(1, -1), qg=jnp.tile(q_norm_g, FOX_HEADS).reshape(1, FOX_W), kg=jnp.tile(k_norm_g, FOX_HEADS).reshape(1, FOX_W),
                bfp=_pair_pad(b_f), wpd=_pool_blockdiag(w_pool).astype(BF16), ps=pool_scale.reshape(1, POOL_W),
                qkb=jnp.full((1, LANES), qk_bound, F32))


def _layer_fwd(x, wt_all, w_out, layer, prm, bs):
    projm, ffo, h = _inproj(x, prm["g"], wt_all, layer, tm=bs["tm_proj"], tn=PROJ_TN)
    qn, ka, kb, v, sq, sk, sv, pooled, yp, pm = _prep(projm, ffo, prm["qg"], prm["kg"], prm["bfp"], prm["wpd"], prm["ps"], ts=bs["ts"])
    o, lse, fm = _fox_fwd(qn, ka, kb, v, projm, prm["qkb"], tq=bs["tq"], tk=bs["tk"])
    so, sm = _sb_fwd(sq, sk, sv, projm, tq=bs["tq"], tk=bs["tks"])
    y = _outproj(x, fm, pm, sm, w_out, layer, tm=bs["tm_proj"])
    saved = dict(x=x, projm=projm, ffo=ffo, h=h, qn=qn, ka=ka, kb=kb, v=v, sq=sq, sk=sk, sv=sv, pooled=pooled, yp=yp,
                 o=o, lse=lse, so=so, fm=fm, pm=pm, sm=sm)
    return y, saved


def _layer_bwd(dy, wt_all, w_out, prm, sv_, bs, layer, stacks):
    dmix, stack_o = _outproj_bwd(dy, sv_["fm"], sv_["pm"], sv_["sm"], w_out, layer, None if stacks is None else stacks[2:], tm=bs["tm_proj"])
    dqn, dkn, dv, dfg, dct, dcr = _fox_bwd(sv_["qn"], sv_["ka"], sv_["kb"], sv_["v"], sv_["o"], sv_["lse"], dmix, sv_["projm"],
                                      prm["qkb"], tq=bs["tq_big"], tk=bs["tk"])
    dsq, dsk, dsv, dsg = _sb_bwd(sv_["sq"], sv_["sk"], sv_["sv"], sv_["so"], dmix, sv_["projm"], tq=bs["tks"], tk=bs["tks"])
    dproj, dqg, dkg, dbf, dwp, dps = _prep_bwd(sv_["projm"], sv_["ffo"], dqn, dkn, dct, dcr, dv, dfg, dsq, dsk, dsv, dsg, dmix,
                                               sv_["pooled"], sv_["yp"], prm["qg"], prm["kg"], prm["bfp"], prm["wpd"], prm["ps"], ts=bs["ts"])
    stack_m, stack_f = _inproj_dw(sv_["h"], dproj, layer, None if stacks is None else stacks[:2], ts=bs["tm_proj"], tn=PROJ_TN)
    dx, dg = _inproj_dx(dproj, wt_all, layer, sv_["x"], prm["g"], dy, tm=bs["tm"])
    grads = dict(
        norm_g=dg[0],
        b_f=_pair_unpad(dbf), q_norm_g=dqg.reshape(FOX_HEADS, HEAD_DIM).sum(0), k_norm_g=dkg.reshape(FOX_HEADS, HEAD_DIM).sum(0),
        w_pool=jnp.stack([dwp[HEAD_DIM * g:HEAD_DIM * (g + 1), HEAD_DIM * g:HEAD_DIM * (g + 1)] for g in range(4)]),
        pool_scale=dps[0])
    return dx, grads, (stack_m, stack_f, stack_o)


def _local_step(x, target, wt_all, w_out, norm_g, b_f, q_norm_g, k_norm_g, w_pool, pool_scale):
    S, D = x.shape
    bs = _blocks(S)
    prms = [_layer_params(norm_g[l], b_f[l], q_norm_g[l], k_norm_g[l], w_pool[l], pool_scale[l]) for l in range(DEPTH)]
    saved = []
    y = x
    for l in range(DEPTH):
        y, s_ = _layer_fwd(y, wt_all, w_out, l, prms[l], bs)
        saved.append(s_)
    dy, sq = _loss_head(y, target, tm=bs["tm"])
    loss = 0.5 * jnp.sum(sq) / D
    grads = [None] * DEPTH
    stacks = None
    for l in reversed(range(DEPTH)):
        dy, grads[l], stacks = _layer_bwd(dy, wt_all, w_out, prms[l], saved[l], bs, l, stacks)
    stacked = {k: jnp.stack([g[k] for g in grads]) for k in grads[0]}
    return loss, dy, stacked, stacks


SMALL = ("norm_g", "b_f", "q_norm_g", "k_norm_g", "w_pool", "pool_scale")


def _pack_small(gr):
    flat = jnp.concatenate([gr[k].reshape(-1) for k in SMALL])
    pad = (-flat.shape[0]) % (8 * LANES)
    return jnp.pad(flat, (0, pad)).reshape(-1, LANES)


def _unpack_small(packed, like):
    flat = packed.reshape(-1)
    out, off = {}, 0
    for k in SMALL:
        n = like[k].size
        out[k] = flat[off:off + n].reshape(like[k].shape)
        off += n
    return out


def kernel(x, norm_g, w_in, b_f, q_norm_g, k_norm_g, w_pool, pool_scale, w_out, loss_target, m_norm_g, m_w_in, m_b_f, m_q_norm_g, m_k_norm_g, m_w_pool, m_pool_scale, m_w_out, v_norm_g, v_w_in, v_b_f, v_q_norm_g, v_k_norm_g, v_w_pool, v_pool_scale, v_w_out):
    weights = dict(norm_g=norm_g, w_in=w_in, b_f=b_f, q_norm_g=q_norm_g, k_norm_g=k_norm_g, w_pool=w_pool, pool_scale=pool_scale, w_out=w_out)
    mom_m = dict(norm_g=m_norm_g, w_in=m_w_in, b_f=m_b_f, q_norm_g=m_q_norm_g, k_norm_g=m_k_norm_g, w_pool=m_w_pool, pool_scale=m_pool_scale, w_out=m_w_out)
    mom_v = dict(norm_g=v_norm_g, w_in=v_w_in, b_f=v_b_f, q_norm_g=v_q_norm_g, k_norm_g=v_k_norm_g, w_pool=v_w_pool, pool_scale=v_pool_scale, w_out=v_w_out)
    shard_cols = w_in.shape[2]
    shard_rows = w_out.shape[1]

    cols_first = lambda a: jnp.transpose(a, (2, 0, 1))
    w_in_t = cols_first(w_in)
    w_in_t_full, w_out_full = _gather_weights(w_in_t, w_out)
    wt_all = _to_aligned(w_in_t_full)
    loss, dx, gr, stacks = _local_step(x[0], loss_target[0], wt_all, w_out_full, norm_g, b_f, q_norm_g, k_norm_g, w_pool, pool_scale)
    loss = lax.psum(loss, ("x", "y", "c"))

    (g_in_mine, g_in_sib), g_w_out = _reduce_scatter(*stacks, shard_cols, shard_rows)
    small = _unpack_small(_all_reduce_small(_pack_small(gr)), {k: weights[k] for k in SMALL})
    grad_w = dict(small, w_out=g_w_out)

    names = ("norm_g", "w_in", "b_f", "q_norm_g", "k_norm_g", "w_pool", "pool_scale", "w_out")
    upd = {k: _adamw_nd(weights[k], grad_w[k], mom_m[k], mom_v[k]) for k in names if k != "w_in"}
    in_t = _adamw_halves(w_in_t, g_in_mine, g_in_sib, cols_first(mom_m["w_in"]), cols_first(mom_v["w_in"]))
    grad_w["w_in"], *upd["w_in"] = [jnp.transpose(a, (1, 2, 0)) for a in in_t]
    return (loss, dx[None], *[grad_w[k] for k in names], *[upd[k][0] for k in names], *[upd[k][1] for k in names], *[upd[k][2] for k in names])
```

```python
import functools

import jax
import jax.numpy as jnp
from jax import lax
from jax.experimental import pallas as pl
from jax.experimental.pallas import tpu as pltpu

F32 = jnp.float32
BF16 = jnp.bfloat16

DEPTH = 4
HEAD_DIM = 64
FOX_HEADS = 8
SB_HEADS = 4
FOX_W = FOX_HEADS * HEAD_DIM
SB_W = SB_HEADS * HEAD_DIM
POOL_W = 256
POOL_WINDOWS = (2, 4, 8, 16)
POOL_HALO = 16
D_MIX = FOX_W + POOL_W + SB_W
EPS = 1e-6
NEG = -1e30
QK_SCALE = HEAD_DIM ** -0.5

ORIG_FOX = 4 * FOX_W
ORIG_FF = ORIG_FOX
ORIG_REST = ORIG_FF + FOX_HEADS
D_IN = ORIG_REST + 2 * POOL_W + 4 * SB_W

C_FQ, C_FK, C_FV, C_FG = 0, FOX_W, 2 * FOX_W, 3 * FOX_W
C_PX = 4 * FOX_W
C_PG = C_PX + POOL_W
C_SQ = C_PG + POOL_W
C_SK, C_SV, C_SG = C_SQ + SB_W, C_SQ + 2 * SB_W, C_SQ + 3 * SB_W
PM = C_SG + SB_W
LANES = 128
LANE_SHIFT = 7
HEAD_SHIFT = 6
PW = PM + LANES
FF_STRIDE = 8
AUG = 3

ADAM_LR = 0.001
ADAM_B1 = 0.9
ADAM_B2 = 0.999
ADAM_EPS = 1e-08
ADAM_WD = 0.01
ADAM_STEP = 10

VMEM_LIMIT = 48 * 1024 * 1024
PROJ_TN = PM // 2


def _cparams(**kw):
    return pltpu.CompilerParams(vmem_limit_bytes=VMEM_LIMIT, **kw)


def _dot(a, b):
    return jnp.dot(a, b, preferred_element_type=F32)


def _dot_nt(a, b):
    return lax.dot_general(a, b, (((1,), (1,)), ((), ())), preferred_element_type=F32)


def _dot_tn(a, b):
    return lax.dot_general(a, b, (((0,), (0,)), ((), ())), preferred_element_type=F32)


def _split2(x):
    hi = x.astype(BF16)
    lo = (x - hi.astype(F32)).astype(BF16)
    return hi, lo


def _split3(x):
    hi = x.astype(BF16)
    r = x - hi.astype(F32)
    mid = r.astype(BF16)
    lo = (r - mid.astype(F32)).astype(BF16)
    return hi, mid, lo


def _dot_exact_rhs(x, m):
    hi, mid, lo = _split3(x)
    return _dot(hi, m) + _dot(mid, m) + _dot(lo, m)


def _dot_exact_lhs(m, x):
    hi, mid, lo = _split3(x)
    return _dot(m, hi) + _dot(m, mid) + _dot(m, lo)


def _sigmoid(x):
    return 1.0 / (1.0 + jnp.exp(-x))


def _silu_pair(x):
    s = _sigmoid(x)
    return x * s, s * (1.0 + x * (1.0 - s))


def _iota(shape, dim):
    return lax.broadcasted_iota(jnp.int32, shape, dim)


def _ones_where(cond):
    return jnp.where(cond, 1.0, 0.0).astype(BF16)


GROUP_SLAB = 256


def _head_blockdiag():
    rows, cols = _iota((2 * GROUP_SLAB, GROUP_SLAB), 0) & (GROUP_SLAB - 1), _iota((2 * GROUP_SLAB, GROUP_SLAB), 1)
    return _ones_where((rows >> HEAD_SHIFT) == (cols >> HEAD_SHIFT))


def _group_sum(x, bd):
    hi, lo = _split2(x)
    slabs = [_dot(jnp.concatenate([hi[:, s:s + GROUP_SLAB], lo[:, s:s + GROUP_SLAB]], axis=1), bd) for s in range(0, x.shape[1], GROUP_SLAB)]
    return jnp.concatenate(slabs, axis=1)


def _lane_pick(x, lane_idx, lane):
    return jnp.sum(jnp.where(lane_idx == lane, x, 0.0), axis=1, keepdims=True)


def _inproj(x, g, wt_all, layer, *, tm, tn):
    S, D = x.shape
    nj = PM // tn

    def body(x_ref, g_ref, w_ref, wff_ref, proj_ref, ff_ref, h_ref):
        @pl.when(pl.program_id(1) == 0)
        def _():
            xf = x_ref[...]
            ms = jnp.mean(xf * xf, axis=-1, keepdims=True)
            h = (xf * lax.rsqrt(ms + EPS) * g_ref[...]).astype(BF16)
            h_ref[...] = h
            ff_ref[...] = _dot_nt(h, wff_ref[...])

        proj_ref[...] = _dot_nt(h_ref[...], w_ref[...])

    return pl.pallas_call(
        body, name="inproj", grid=(S // tm, nj),
        in_specs=[pl.BlockSpec((tm, D), lambda i, j: (i, 0)),
                  pl.BlockSpec((1, D), lambda i, j: (0, 0)),
                  pl.BlockSpec((None, tn, D), lambda i, j: (layer, j, 0)),
                  pl.BlockSpec((None, LANES, D), lambda i, j: (layer, PM // LANES, 0))],
        out_specs=[pl.BlockSpec((tm, tn), lambda i, j: (i, j)),
                   pl.BlockSpec((tm, LANES), lambda i, j: (i, 0)),
                   pl.BlockSpec((tm, D), lambda i, j: (i, 0))],
        out_shape=[jax.ShapeDtypeStruct((S, PM), F32), jax.ShapeDtypeStruct((S, LANES), F32),
                   jax.ShapeDtypeStruct((S, D), BF16)],
        compiler_params=_cparams(dimension_semantics=("arbitrary", "arbitrary")),
    )(x, g, wt_all, wt_all)


def _pool_group_select(lane_group, vals):
    return jnp.where(lane_group == 0, vals[0], jnp.where(lane_group == 1, vals[1], jnp.where(lane_group == 2, vals[2], vals[3])))


def _prep(projm, ffo, qg, kg, bfp, wpd, ps, *, ts):
    S = projm.shape[0]
    nb = S // ts
    hb = ts // POOL_HALO

    def body(fq_ref, fk_ref, fv_ref, pp_ref, halo_ref, ff_ref, sq_ref, sk_ref, sv_ref,
             qg_ref, kg_ref, bf_ref, wpd_ref, ps_ref,
             qn_ref, ka_ref, kb_ref, v_ref, sqo_ref, sko_ref, svo_ref, pooled_ref, yp_ref, pm_ref,
             carry_ref, c_ref, buf_ref):
        i = pl.program_id(0)
        bd = _head_blockdiag()
        normed = []
        for src, g_ref in ((fq_ref, qg_ref), (fk_ref, kg_ref)):
            q = src[...]
            ss = _group_sum(q * q, bd)
            normed.append(q * lax.rsqrt(ss * (1.0 / HEAD_DIM) + EPS) * g_ref[...])
        qn_ref[...] = (normed[0] * QK_SCALE).astype(BF16)
        kn = normed[1]
        v_ref[...] = fv_ref[...].astype(BF16)
        sqo_ref[...] = (sq_ref[...] * QK_SCALE).astype(BF16)
        sko_ref[...] = sk_ref[...].astype(BF16)
        svo_ref[...] = sv_ref[...].astype(BF16)

        @pl.when(i == 0)
        def _():
            carry_ref[...] = jnp.zeros_like(carry_ref)

        z = ff_ref[...] + bf_ref[...]
        lf = jnp.minimum(z, 0.0) - jnp.log(1.0 + jnp.exp(-jnp.abs(z)))
        tri = _ones_where(_iota((ts, ts), 1) <= _iota((ts, ts), 0))
        c = _dot_exact_lhs(tri, lf) + carry_ref[...]
        c_ref[...] = c
        carry_ref[...] = c_ref[ts - 1:ts, :]
        parts = jnp.concatenate(_split3(-c), axis=1)
        row = _iota((AUG * LANES, FOX_W), 0)
        col = _iota((AUG * LANES, FOX_W), 1)
        part, src = row >> LANE_SHIFT, row & (LANES - 1)
        pair, off = col >> LANE_SHIFT, col & (LANES - 1)
        sel_a = _ones_where((src == FF_STRIDE * pair) & (off == HEAD_DIM + part))
        sel_b = _ones_where((src == FF_STRIDE * pair + 1) & (off == part))
        first_half = (_iota((1, FOX_W), 1) & HEAD_DIM) == 0
        ka_ref[...] = jnp.where(first_half, kn, _dot(parts, sel_a)).astype(BF16)
        kb_ref[...] = jnp.where(first_half, _dot(parts, sel_b), kn).astype(BF16)

        x = pp_ref[:, 0:POOL_W]
        pg = pp_ref[:, POOL_W:2 * POOL_W]
        halo = jnp.where(i > 0, halo_ref[:, 0:POOL_W], 0.0)
        buf_ref[0:POOL_HALO, :] = halo
        buf_ref[POOL_HALO:POOL_HALO + ts, :] = x
        acc = x
        snaps = []
        for d in range(1, POOL_HALO):
            acc = acc + buf_ref[pl.ds(POOL_HALO - d, ts), :]
            if d + 1 in POOL_WINDOWS:
                snaps.append(acc)
        lane_group = _iota((1, POOL_W), 1) >> HEAD_SHIFT
        wsum = _pool_group_select(lane_group, snaps)
        wlen = _pool_group_select(lane_group, [float(w) for w in POOL_WINDOWS])
        tpos = (i * ts + _iota((ts, 1), 0) + 1).astype(F32)
        pooled = wsum / jnp.minimum(tpos, wlen) - x
        pb = pooled.astype(BF16)
        pooled_ref[...] = pb
        yp = _dot(pb, wpd_ref[...])
        yp_ref[...] = yp
        pm_ref[...] = (yp * ps_ref[...] * (pg * _sigmoid(pg))).astype(BF16)

    blk = lambda w, c: pl.BlockSpec((ts, w), lambda i: (i, c))
    full = lambda a: pl.BlockSpec(a.shape, lambda i: (0,) * a.ndim)
    out_shapes = [
        jax.ShapeDtypeStruct((S, FOX_W), BF16), jax.ShapeDtypeStruct((S, FOX_W), BF16), jax.ShapeDtypeStruct((S, FOX_W), BF16),
        jax.ShapeDtypeStruct((S, FOX_W), BF16),
        jax.ShapeDtypeStruct((S, SB_W), BF16), jax.ShapeDtypeStruct((S, SB_W), BF16), jax.ShapeDtypeStruct((S, SB_W), BF16),
        jax.ShapeDtypeStruct((S, POOL_W), BF16), jax.ShapeDtypeStruct((S, POOL_W), F32), jax.ShapeDtypeStruct((S, POOL_W), BF16),
    ]
    out_specs = [
        blk(FOX_W, 0), blk(FOX_W, 0), blk(FOX_W, 0), blk(FOX_W, 0),
        blk(SB_W, 0), blk(SB_W, 0), blk(SB_W, 0),
        blk(POOL_W, 0), blk(POOL_W, 0), blk(POOL_W, 0),
    ]
    return pl.pallas_call(
        body, name="prep", grid=(nb,),
        in_specs=[blk(FOX_W, C_FQ // FOX_W), blk(FOX_W, C_FK // FOX_W), blk(FOX_W, C_FV // FOX_W), blk(2 * POOL_W, C_PX // (2 * POOL_W)),
                  pl.BlockSpec((POOL_HALO, 2 * POOL_W), lambda i: (jnp.maximum(i * hb - 1, 0), C_PX // (2 * POOL_W))),
                  blk(LANES, 0),
                  blk(SB_W, C_SQ // SB_W), blk(SB_W, C_SK // SB_W), blk(SB_W, C_SV // SB_W),
                  full(qg), full(kg), full(bfp), full(wpd), full(ps)],
        out_specs=out_specs, out_shape=out_shapes,
        scratch_shapes=[pltpu.VMEM((1, LANES), F32), pltpu.VMEM((ts, LANES), F32), pltpu.VMEM((ts + POOL_HALO, POOL_W), F32)],
        compiler_params=_cparams(dimension_semantics=("arbitrary",)),
    )(projm, projm, projm, projm, projm, ffo, projm, projm, projm, qg, kg, bfp, wpd, ps)


def _pair_masks(x):
    ma = _iota((1, LANES), 1) < HEAD_DIM
    zero = jnp.zeros_like(x)
    return jnp.where(ma, x, zero), jnp.where(ma, zero, x)


DIAG_TILE = 256


def _diag_tiles(tq, size=DIAG_TILE):
    size = min(tq, size)
    return [(t * size, size) for t in range(tq // size)]


def _put_rows(old, new, r0):
    return new if r0 == 0 else jnp.concatenate([old[:r0], new], axis=0)


def _aug_queries(q):
    lane = _iota((1, LANES), 1)
    one = jnp.ones_like(q)
    zero = jnp.zeros_like(q)
    qa = jnp.where(lane < HEAD_DIM, q, jnp.where(lane < HEAD_DIM + AUG, one, zero))
    qb = jnp.where(lane >= HEAD_DIM, q, jnp.where(lane < AUG, one, zero))
    return qa, qb


EXP_DEAD = -105.0
PACK = 16


def _fox_walk_left(nfull, tk, block, carry, k_refs, qk_bound, row_floor):
    lane = _iota((1, LANES), 1)

    def alive(h, jj, c):
        k0 = pl.multiple_of(jnp.maximum(nfull - 1 - jj, 0) * tk + tk - PACK, PACK)
        last = k_refs[h][pl.ds(k0, PACK), :].astype(F32)
        lo = HEAD_DIM if h == 0 else 0
        negc = jnp.sum(jnp.where((lane >= lo) & (lane < lo + AUG), last, 0.0), axis=1, keepdims=True)
        return qk_bound + jnp.max(negc) - row_floor(c)[h] >= EXP_DEAD

    def walk(heads, jj0, c0):
        def go_on(state):
            jj, c = state
            ok = jj < nfull
            for h in heads:
                ok = ok & alive(h, jj, c)
            return ok

        def step(state):
            jj, c = state
            return jj + 1, block(pl.multiple_of((nfull - 1 - jj) * tk, tk), tk, 0, c, False, heads)

        return lax.while_loop(go_on, step, (jj0, c0))

    jj_pair, carry = walk((0, 1), jnp.int32(0), carry)
    carry = walk((0,), jj_pair, carry)[1]
    return walk((1,), jj_pair, carry)[1]


def _fox_fwd(qn, ka, kb, v, projm, qkb, *, tq, tk):
    S = qn.shape[0]
    npair = FOX_HEADS // 2

    def body(q_ref, ka_ref, kb_ref, v_ref, fg_ref, qkb_ref, o_ref, lse_ref, fm_ref):
        qi = pl.program_id(1)
        lane = _iota((1, LANES), 1)
        ma = lane < HEAD_DIM
        qaug = _aug_queries(q_ref[...])
        k_refs = (ka_ref, kb_ref)

        def block(k0, tkl, r0, carry, masked, heads=(0, 1)):
            vb = v_ref[pl.ds(k0, tkl), :]
            if masked:
                mask = (k0 + _iota((tq - r0, tkl), 1)) <= (qi * tq + r0 + _iota((tq - r0, tkl), 0))
            scores = {h: _dot_nt(qaug[h][r0:], k_refs[h][pl.ds(k0, tkl), :]) for h in heads}
            new = list(carry)
            for h in heads:
                m, l, acc = [x[r0:] for x in carry[h]]
                s = jnp.where(mask, scores[h], NEG) if masked else scores[h]
                m_new = jnp.maximum(m, jnp.max(s, axis=1, keepdims=True))
                alpha = jnp.exp(m - m_new)
                p = jnp.exp(s - m_new)
                sub = (m_new, alpha * l + jnp.sum(p, axis=1, keepdims=True), alpha * acc + _dot(p.astype(BF16), vb))
                new[h] = tuple(_put_rows(old, x, r0) for old, x in zip(carry[h], sub))
            return tuple(new)

        carry = tuple((jnp.full((tq, 1), NEG, F32), jnp.zeros((tq, 1), F32), jnp.zeros((tq, LANES), F32)) for _ in range(2))
        for off, size in _diag_tiles(tq, tq):
            carry = block(pl.multiple_of(qi * tq + off, size), size, off, carry, True)
        carry = _fox_walk_left((qi * tq) // tk, tk, block, carry, k_refs, jnp.max(qkb_ref[...]),
                               lambda c: (jnp.min(c[0][0]), jnp.min(c[1][0])))
        (ma_, la, acca), (mb_, lb, accb) = carry
        o = jnp.where(ma, acca / la, accb / lb)
        o_ref[...] = o
        lse_ref[...] = jnp.where(ma, ma_ + jnp.log(la), mb_ + jnp.log(lb))
        fg = fg_ref[...]
        fm_ref[...] = (o * (fg * _sigmoid(fg))).astype(BF16)

    qblk = pl.BlockSpec((tq, LANES), lambda p, i: (i, p))
    kvblk = pl.BlockSpec((S, LANES), lambda p, i: (0, p))
    return pl.pallas_call(
        body, name="fox_fwd", grid=(npair, S // tq),
        in_specs=[qblk, kvblk, kvblk, kvblk,
                  pl.BlockSpec((tq, LANES), lambda p, i: (i, C_FG // LANES + p)),
                  pl.BlockSpec((1, LANES), lambda p, i: (0, 0))],
        out_specs=[qblk, qblk, qblk],
        out_shape=[jax.ShapeDtypeStruct((S, FOX_W), F32), jax.ShapeDtypeStruct((S, FOX_W), F32), jax.ShapeDtypeStruct((S, FOX_W), BF16)],
        compiler_params=_cparams(dimension_semantics=("arbitrary", "arbitrary")),
    )(qn, ka, kb, v, projm, qkb)


def _suffix_sums(x, tmat2):
    return _dot(jnp.concatenate(_split2(x), axis=1), tmat2)


def _suffix_matrix(tk, inclusive):
    rr, cc = _iota((2 * tk, tk), 0) & (tk - 1), _iota((2 * tk, tk), 1)
    return _ones_where(rr >= cc) if inclusive else _ones_where(rr > cc)


def _sb_scores(qh, kb, causal, tmat2, r_runs):
    heads = range(2)
    zs = [_dot_nt(qh[h], kb) for h in heads]
    nsps = [jnp.minimum(-z, 0.0) - jnp.log(1.0 + jnp.exp(-jnp.abs(z))) for z in zs]
    lbs = nsps if causal is None else [jnp.where(causal, n, 0.0) for n in nsps]
    rins = [_suffix_sums(lb, tmat2) for lb in lbs]
    args = [zs[h] + lbs[h] + (rins[h] + r_runs[h]) for h in heads]
    a_s = [jnp.exp(arg if causal is None else jnp.where(causal, arg, NEG)) for arg in args]
    return zs, nsps, lbs, a_s


def _sb_walk_left(nfull, tk, block, carry, running_sums):
    def alive(state):
        jj, c = state
        ra, rb = running_sums(c)
        return (jj < nfull) & (jnp.max(jnp.maximum(ra, rb)) >= EXP_DEAD)

    def step(state):
        jj, c = state
        return jj + 1, block(pl.multiple_of((nfull - 1 - jj) * tk, tk), 0, c, False)

    return lax.while_loop(alive, step, (jnp.int32(0), carry))[1]


def _sb_fwd(sq, sk, sv, projm, *, tq, tk):
    S = sq.shape[0]
    npair = SB_HEADS // 2

    def body(q_ref, k_ref, v_ref, sg_ref, o_ref, sm_ref):
        qi = pl.program_id(1)
        lane = _iota((1, LANES), 1)
        ma = lane < HEAD_DIM
        qh = _pair_masks(q_ref[...])
        tmat2 = _suffix_matrix(tk, inclusive=False)
        nfull = (qi * tq) // tk

        def block(k0, r0, carry, masked):
            nr = tq - r0
            kb = k_ref[pl.ds(k0, tk), :]
            vb = v_ref[pl.ds(k0, tk), :]
            causal = (k0 + _iota((nr, tk), 1)) < (qi * tq + r0 + _iota((nr, tk), 0)) if masked else None
            _, _, lbs, a_s = _sb_scores([q[r0:] for q in qh], kb, causal, tmat2, [carry[h][0][r0:] for h in range(2)])
            pv = _dot(jnp.concatenate([a.astype(BF16) for a in a_s], axis=0), vb)
            return tuple((_put_rows(carry[h][0], carry[h][0][r0:] + jnp.sum(lbs[h], axis=1, keepdims=True), r0),
                          _put_rows(carry[h][1], carry[h][1][r0:] + pv[h * nr:(h + 1) * nr], r0)) for h in range(2))

        carry = tuple((jnp.zeros((tq, 1), F32), jnp.zeros((tq, LANES), F32)) for _ in range(2))
        for off, size in reversed(_diag_tiles(tq)):
            assert size == tk
            carry = block(pl.multiple_of(qi * tq + off, tk), off, carry, True)
        (_, acca), (_, accb) = _sb_walk_left(nfull, tk, block, carry, lambda c: (c[0][0], c[1][0]))
        o = jnp.where(ma, acca, accb)
        o_ref[...] = o
        sg = sg_ref[...]
        sm_ref[...] = (o * (sg * _sigmoid(sg))).astype(BF16)

    qblk = pl.BlockSpec((tq, LANES), lambda p, i: (i, p))
    kvblk = pl.BlockSpec((S, LANES), lambda p, i: (0, p))
    return pl.pallas_call(
        body, name="sb_fwd", grid=(npair, S // tq),
        in_specs=[qblk, kvblk, kvblk, pl.BlockSpec((tq, LANES), lambda p, i: (i, C_SG // LANES + p))],
        out_specs=[qblk, qblk],
        out_shape=[jax.ShapeDtypeStruct((S, SB_W), F32), jax.ShapeDtypeStruct((S, SB_W), BF16)],
        compiler_params=_cparams(dimension_semantics=("arbitrary", "arbitrary")),
    )(sq, sk, sv, projm)


def _outproj(x, fm, pm, sm, w_out, layer, *, tm):
    S, D = x.shape

    def body(x_ref, fm_ref, pm_ref, sm_ref, w_ref, y_ref):
        y = x_ref[...] + _dot(fm_ref[...], w_ref[0:FOX_W, :])
        y = y + _dot(pm_ref[...], w_ref[FOX_W:FOX_W + POOL_W, :])
        y_ref[...] = y + _dot(sm_ref[...], w_ref[FOX_W + POOL_W:D_MIX, :])

    row = lambda w: pl.BlockSpec((tm, w), lambda i: (i, 0))
    return pl.pallas_call(
        body, name="outproj", grid=(S // tm,),
        in_specs=[row(D), row(FOX_W), row(POOL_W), row(SB_W), pl.BlockSpec((None, D_MIX, D), lambda i: (layer, 0, 0))],
        out_specs=row(D), out_shape=jax.ShapeDtypeStruct((S, D), F32),
        compiler_params=_cparams(dimension_semantics=("arbitrary",)),
    )(x, fm, pm, sm, w_out)


def _loss_head(y, target, *, tm):
    S, D = y.shape

    def body(y_ref, t_ref, dy_ref, sq_ref):
        @pl.when(pl.program_id(0) == 0)
        def _():
            sq_ref[...] = jnp.zeros_like(sq_ref)

        d = y_ref[...] - t_ref[...]
        dy_ref[...] = d * (1.0 / D)
        sq_ref[...] += jnp.sum(d * d, axis=0, keepdims=True)

    row = pl.BlockSpec((tm, D), lambda i: (i, 0))
    return pl.pallas_call(
        body, name="loss_head", grid=(S // tm,),
        in_specs=[row, row], out_specs=[row, pl.BlockSpec((1, D), lambda i: (0, 0))],
        out_shape=[jax.ShapeDtypeStruct((S, D), F32), jax.ShapeDtypeStruct((1, D), F32)],
        compiler_params=_cparams(dimension_semantics=("arbitrary",)),
    )(y, target)


def _outproj_bwd(dy, fm, pm, sm, w_out, layer, stacks, *, tm):
    S, D = dy.shape

    def body(dy_ref, fm_ref, pm_ref, sm_ref, w_ref, dm_ref, dw_ref):
        @pl.when(pl.program_id(0) == 0)
        def _():
            dw_ref[...] = jnp.zeros_like(dw_ref)

        dyb = dy_ref[...].astype(BF16)
        dm_ref[...] = _dot_nt(dyb, w_ref[...])
        dw_ref[0:FOX_W, :] += _dot_tn(fm_ref[...], dyb)
        dw_ref[FOX_W:FOX_W + POOL_W, :] += _dot_tn(pm_ref[...], dyb)
        dw_ref[FOX_W + POOL_W:D_MIX, :] += _dot_tn(sm_ref[...], dyb)

    row = lambda w: pl.BlockSpec((tm, w), lambda i: (i, 0))
    wspec = pl.BlockSpec((None, D_MIX, D), lambda i: (layer, 0, 0))
    return _stack_call(
        body, "outproj_bwd", (S // tm,), [row(D), row(FOX_W), row(POOL_W), row(SB_W), wspec], (dy, fm, pm, sm, w_out),
        [pl.BlockSpec((None, D_MIX, D), lambda i: (layer, 0, 0))], [(D_MIX, D)], stacks,
        plain_specs=[row(D_MIX)], plain_shapes=[jax.ShapeDtypeStruct((S, D_MIX), F32)],
        compiler_params=_cparams(dimension_semantics=("arbitrary",)))


def _fox_bwd(qn, ka, kb, v, o, lse, dmix, projm, qkb, *, tq, tk):
    S = qn.shape[0]
    npair = FOX_HEADS // 2

    def body(q_ref, ka_ref, kb_ref, v_ref, o_ref, lse_ref, dm_ref, fg_ref, qkb_ref,
             dq_ref, dk_ref, dv_ref, dfg_ref, dct_ref, dcr_ref):
        qi = pl.program_id(1)

        @pl.when(qi == 0)
        def _():
            dk_ref[...] = jnp.zeros_like(dk_ref)
            dv_ref[...] = jnp.zeros_like(dv_ref)
            dct_ref[...] = jnp.zeros_like(dct_ref)

        lane = _iota((1, LANES), 1)
        ma = lane < HEAD_DIM
        qh = _pair_masks(q_ref[...])
        qaug = _aug_queries(q_ref[...])
        k_refs = (ka_ref, kb_ref)
        lsev = lse_ref[...]
        lse = (_lane_pick(lsev, lane, 0), _lane_pick(lsev, lane, HEAD_DIM))
        fg = fg_ref[...]
        silu, dsilu = _silu_pair(fg)
        dm = dm_ref[...]
        ov = o_ref[...]
        do = dm * silu
        dfg_ref[...] = dm * ov * dsilu
        dd = do * ov
        dsum = (jnp.sum(jnp.where(ma, dd, 0.0), axis=1, keepdims=True), jnp.sum(jnp.where(ma, 0.0, dd), axis=1, keepdims=True))
        doh = _pair_masks(do.astype(BF16))

        def block(k0, tkl, r0, carry, masked, heads=(0, 1)):
            vb = v_ref[pl.ds(k0, tkl), :]
            if masked:
                mask = (k0 + _iota((tq - r0, tkl), 1)) <= (qi * tq + r0 + _iota((tq - r0, tkl), 0))
            kaugs = {h: k_refs[h][pl.ds(k0, tkl), :] for h in heads}
            scores = {h: _dot_nt(qaug[h][r0:], kaugs[h]) for h in heads}
            dps = {h: _dot_nt(doh[h][r0:], vb) for h in heads}
            ps, dss = [], []
            rows = [carry[1], carry[2]]
            for h in heads:
                s = jnp.where(mask, scores[h], NEG) if masked else scores[h]
                p = jnp.exp(s - lse[h][r0:])
                dsf = p * (dps[h] - dsum[h][r0:])
                dct_ref[0, h:h + 1, pl.ds(k0, tkl)] -= jnp.sum(dsf, axis=0, keepdims=True)
                rows[h] = _put_rows(carry[1 + h], carry[1 + h][r0:] + jnp.sum(dsf, axis=1, keepdims=True), r0)
                ps.append(p.astype(BF16))
                dss.append(dsf.astype(BF16))
            dv_ref[pl.ds(k0, tkl), :] += _dot_tn(jnp.concatenate(ps, axis=0), jnp.concatenate([doh[h][r0:] for h in heads], axis=0))
            dk_ref[pl.ds(k0, tkl), :] += _dot_tn(jnp.concatenate(dss, axis=0), jnp.concatenate([qh[h][r0:] for h in heads], axis=0))
            kh = jnp.concatenate([_pair_masks(kaugs[h])[h] for h in heads], axis=0)
            dq = _put_rows(carry[0], carry[0][r0:] + _dot(jnp.concatenate(dss, axis=1), kh), r0)
            return (dq, rows[0], rows[1])

        zcol = jnp.zeros((tq, 1), F32)
        carry = (jnp.zeros((tq, LANES), F32), zcol, zcol)
        for off, size in _diag_tiles(tq):
            carry = block(pl.multiple_of(qi * tq + off, size), size, off, carry, True)
        floors = (jnp.min(lse[0]), jnp.min(lse[1]))
        dq, rowa, rowb = _fox_walk_left((qi * tq) // tk, tk, block, carry, k_refs, jnp.max(qkb_ref[...]), lambda c: floors)
        dq_ref[...] = dq * QK_SCALE
        dcr_ref[0] = jnp.where(ma, rowa, rowb)

    qblk = pl.BlockSpec((tq, LANES), lambda p, i: (i, p))
    kvblk = pl.BlockSpec((S, LANES), lambda p, i: (0, p))
    f32out = jax.ShapeDtypeStruct((S, FOX_W), F32)
    ctblk = pl.BlockSpec((1, FF_STRIDE, S), lambda p, i: (p, 0, 0))
    return pl.pallas_call(
        body, name="fox_bwd", grid=(npair, S // tq),
        in_specs=[qblk, kvblk, kvblk, kvblk, qblk, qblk, qblk,
                  pl.BlockSpec((tq, LANES), lambda p, i: (i, C_FG // LANES + p)),
                  pl.BlockSpec((1, LANES), lambda p, i: (0, 0))],
        out_specs=[qblk, kvblk, kvblk, qblk, ctblk, pl.BlockSpec((1, tq, LANES), lambda p, i: (p, i, 0))],
        out_shape=[f32out, f32out, f32out, f32out, jax.ShapeDtypeStruct((npair, FF_STRIDE, S), F32),
                   jax.ShapeDtypeStruct((npair, S, LANES), F32)],
        compiler_params=_cparams(dimension_semantics=("arbitrary", "arbitrary")),
    )(qn, ka, kb, v, o, lse, dmix, projm, qkb)


def _sb_bwd(sq, sk, sv, o, dmix, projm, *, tq, tk):
    S = sq.shape[0]
    npair = SB_HEADS // 2
    mix0 = (FOX_W + POOL_W) // LANES

    def body(q_ref, k_ref, v_ref, o_ref, dm_ref, sg_ref, dq_ref, dk_ref, dv_ref, dsg_ref):
        qi = pl.program_id(1)

        @pl.when(qi == 0)
        def _():
            dk_ref[...] = jnp.zeros_like(dk_ref)
            dv_ref[...] = jnp.zeros_like(dv_ref)

        lane = _iota((1, LANES), 1)
        ma = lane < HEAD_DIM
        qh = _pair_masks(q_ref[...])
        sg = sg_ref[...]
        silu, dsilu = _silu_pair(sg)
        dm = dm_ref[...]
        ov = o_ref[...]
        do = dm * silu
        dsg_ref[...] = dm * ov * dsilu
        dob = do.astype(BF16)
        dd = dob.astype(F32) * ov
        dsum = (jnp.sum(jnp.where(ma, dd, 0.0), axis=1, keepdims=True), jnp.sum(jnp.where(ma, 0.0, dd), axis=1, keepdims=True))
        doh = _pair_masks(dob)
        tmat2 = _suffix_matrix(tk, inclusive=False)
        tmat2_inc = _suffix_matrix(tk, inclusive=True)
        nfull = (qi * tq) // tk

        def block(k0, r0, carry, masked):
            nr = tq - r0
            kb = k_ref[pl.ds(k0, tk), :]
            vb = v_ref[pl.ds(k0, tk), :]
            kh = _pair_masks(kb)
            causal = (k0 + _iota((nr, tk), 1)) < (qi * tq + r0 + _iota((nr, tk), 0)) if masked else None
            heads = range(2)
            qs = [q[r0:] for q in qh]
            dos = [d[r0:] for d in doh]
            das = [_dot_nt(dos[h], vb) for h in heads]
            zs, nsps, lbs, a_s = _sb_scores(qs, kb, causal, tmat2, [carry[h][0][r0:] for h in heads])
            abs_ = [a.astype(BF16) for a in a_s]
            us = [abs_[h].astype(F32) * das[h] for h in heads]
            uins = [_suffix_sums(u, tmat2_inc) for u in us]
            dzs = []
            for h in heads:
                cum_u = dsum[h][r0:] - (uins[h] + carry[h][1][r0:])
                dz = us[h] * jnp.exp(nsps[h]) - jnp.exp(zs[h] + nsps[h]) * cum_u
                if masked:
                    dz = jnp.where(causal, dz, 0.0)
                dzs.append(dz.astype(BF16))
            dv_ref[pl.ds(k0, tk), :] += _dot_tn(jnp.concatenate(abs_, axis=0), jnp.concatenate(dos, axis=0))
            dk_ref[pl.ds(k0, tk), :] += _dot_tn(jnp.concatenate(dzs, axis=0), jnp.concatenate(qs, axis=0))
            dq = _put_rows(carry[2], carry[2][r0:] + _dot(jnp.concatenate(dzs, axis=1), jnp.concatenate(kh, axis=0)), r0)
            new = [(_put_rows(carry[h][0], carry[h][0][r0:] + jnp.sum(lbs[h], axis=1, keepdims=True), r0),
                    _put_rows(carry[h][1], carry[h][1][r0:] + jnp.sum(us[h], axis=1, keepdims=True), r0)) for h in heads]
            return (new[0], new[1], dq)

        zcol = jnp.zeros((tq, 1), F32)
        carry = ((zcol, zcol), (zcol, zcol), jnp.zeros((tq, LANES), F32))
        for off, size in reversed(_diag_tiles(tq)):
            assert size == tk
            carry = block(pl.multiple_of(qi * tq + off, tk), off, carry, True)
        dq = _sb_walk_left(nfull, tk, block, carry, lambda c: (c[0][0], c[1][0]))[2]
        dq_ref[...] = dq * QK_SCALE

    qblk = pl.BlockSpec((tq, LANES), lambda p, i: (i, p))
    kvblk = pl.BlockSpec((S, LANES), lambda p, i: (0, p))
    f32out = jax.ShapeDtypeStruct((S, SB_W), F32)
    return pl.pallas_call(
        body, name="sb_bwd", grid=(npair, S // tq),
        in_specs=[qblk, kvblk, kvblk, qblk,
                  pl.BlockSpec((tq, LANES), lambda p, i: (i, mix0 + p)),
                  pl.BlockSpec((tq, LANES), lambda p, i: (i, C_SG // LANES + p))],
        out_specs=[qblk, kvblk, kvblk, qblk],
        out_shape=[f32out, f32out, f32out, f32out],
        compiler_params=_cparams(dimension_semantics=("arbitrary", "arbitrary")),
    )(sq, sk, sv, o, dmix, projm)


def _prep_bwd(projm, ffo, dqn, dkn, dct, dcr, dv, dfg, dsq, dsk, dsv, dsg, dmix, pooled, yp, qg, kg, bfp, wpd, ps, *, ts):
    S = projm.shape[0]
    nb = S // ts
    hb = ts // POOL_HALO
    npair = FOX_HEADS // 2
    last_halo = S // POOL_HALO - 1

    def body(fq_ref, fk_ref, pp_ref, pph_ref, ff_ref,
             dqn_ref, dkn_ref, dct_ref, dcr_ref, dv_ref, dfg_ref, dsq_ref, dsk_ref, dsv_ref, dsg_ref,
             dmp_ref, dmh_ref, pooled_ref, yp_ref, qg_ref, kg_ref, bf_ref, wpd_ref, ps_ref,
             dp_ref, dqg_ref, dkg_ref, dbf_ref, dwp_ref, dps_ref,
             carry_ref, dl_ref, buf_ref, dct_s):
        i = pl.program_id(0)
        blk = nb - 1 - i

        @pl.when(i == 0)
        def _():
            carry_ref[...] = jnp.zeros_like(carry_ref)
            dqg_ref[...] = jnp.zeros_like(dqg_ref)
            dkg_ref[...] = jnp.zeros_like(dkg_ref)
            dbf_ref[...] = jnp.zeros_like(dbf_ref)
            dwp_ref[...] = jnp.zeros_like(dwp_ref)
            dps_ref[...] = jnp.zeros_like(dps_ref)

        bd = _head_blockdiag()
        for raw_ref, g_ref, dn, dg_ref, col in ((fq_ref, qg_ref, dqn_ref[...], dqg_ref, C_FQ), (fk_ref, kg_ref, dkn_ref[...], dkg_ref, C_FK)):
            q = raw_ref[...]
            rstd = lax.rsqrt(_group_sum(q * q, bd) * (1.0 / HEAD_DIM) + EPS)
            xhat = q * rstd
            dg_ref[...] += jnp.sum(dn * xhat, axis=0, keepdims=True)
            dyg = dn * g_ref[...]
            mean = _group_sum(dyg * xhat, bd) * (1.0 / HEAD_DIM)
            dp_ref[:, col:col + FOX_W] = (rstd * (dyg - xhat * mean)).astype(BF16)
        dp_ref[:, C_FV:C_FV + FOX_W] = dv_ref[...].astype(BF16)
        dp_ref[:, C_FG:C_FG + FOX_W] = dfg_ref[...].astype(BF16)
        dp_ref[:, C_SQ:C_SQ + SB_W] = dsq_ref[...].astype(BF16)
        dp_ref[:, C_SK:C_SK + SB_W] = dsk_ref[...].astype(BF16)
        dp_ref[:, C_SV:C_SV + SB_W] = dsv_ref[...].astype(BF16)
        dp_ref[:, C_SG:C_SG + SB_W] = dsg_ref[...].astype(BF16)

        dct_s[...] = jnp.zeros_like(dct_s)
        for p in range(npair):
            dct_s[FF_STRIDE * p:FF_STRIDE * (p + 1), :] = dct_ref[p]
        dc = dct_s[...].T
        lane = _iota((1, LANES), 1)
        for p in range(npair):
            dcr = dcr_ref[p]
            dc = dc + jnp.where(lane == FF_STRIDE * p, _lane_pick(dcr, lane, 0), 0.0)
            dc = dc + jnp.where(lane == FF_STRIDE * p + 1, _lane_pick(dcr, lane, HEAD_DIM), 0.0)
        triu = _ones_where(_iota((ts, ts), 1) >= _iota((ts, ts), 0))
        dlf = _dot_exact_lhs(triu, dc) + carry_ref[...]
        dl_ref[...] = dlf
        carry_ref[...] = dl_ref[0:1, :]
        z = ff_ref[...] + bf_ref[...]
        dff = dlf * (1.0 / (1.0 + jnp.exp(z)))
        dbf_ref[...] += jnp.sum(dff, axis=0, keepdims=True)
        dp_ref[:, PM:PW] = dff.astype(BF16)

        psv = ps_ref[...]
        wpdv = wpd_ref[...]
        lane_group = _iota((1, POOL_W), 1) >> HEAD_SHIFT
        wlen = _pool_group_select(lane_group, [float(w) for w in POOL_WINDOWS])
        pg = pp_ref[:, POOL_W:2 * POOL_W]
        silu, dsilu = _silu_pair(pg)
        dmp = dmp_ref[...]
        ypv = yp_ref[...]
        dp_ref[:, C_PG:C_PG + POOL_W] = (dmp * (ypv * psv) * dsilu).astype(BF16)
        dps_ref[...] += jnp.sum(dmp * silu * ypv, axis=0, keepdims=True)
        dyp = (dmp * psv * silu).astype(BF16)
        dwp_ref[...] += _dot_tn(pooled_ref[...], dyp)
        dpooled = _dot_nt(dyp, wpdv)
        pgh = pph_ref[:, POOL_W:2 * POOL_W]
        dyph = (dmh_ref[...] * psv * (pgh * _sigmoid(pgh))).astype(BF16)
        dpooled_h = jnp.where(blk < nb - 1, _dot_nt(dyph, wpdv), 0.0)
        tpos = (blk * ts + _iota((ts, 1), 0) + 1).astype(F32)
        ev = dpooled / jnp.minimum(tpos, wlen)
        buf_ref[0:ts, :] = ev
        buf_ref[ts:ts + POOL_HALO, :] = dpooled_h / wlen
        acc = ev
        snaps = []
        for d in range(1, POOL_HALO):
            acc = acc + buf_ref[pl.ds(d, ts), :]
            if d + 1 in POOL_WINDOWS:
                snaps.append(acc)
        dp_ref[:, C_PX:C_PX + POOL_W] = (_pool_group_select(lane_group, snaps) - dpooled).astype(BF16)

    rblk = lambda w, c: pl.BlockSpec((ts, w), lambda i: (nb - 1 - i, c))
    full = lambda a: pl.BlockSpec(a.shape, lambda i: (0,) * a.ndim)
    halo = lambda w, c: pl.BlockSpec((POOL_HALO, w), lambda i: (jnp.minimum((nb - i) * hb, last_halo), c))
    acc_spec = lambda r, w: pl.BlockSpec((r, w), lambda i: (0, 0))
    return pl.pallas_call(
        body, name="prep_bwd", grid=(nb,),
        in_specs=[rblk(FOX_W, C_FQ // FOX_W), rblk(FOX_W, C_FK // FOX_W), rblk(2 * POOL_W, C_PX // (2 * POOL_W)),
                  halo(2 * POOL_W, C_PX // (2 * POOL_W)), rblk(LANES, 0),
                  rblk(FOX_W, 0), rblk(FOX_W, 0), pl.BlockSpec((npair, FF_STRIDE, ts), lambda i: (0, 0, nb - 1 - i)),
                  pl.BlockSpec((npair, ts, LANES), lambda i: (0, nb - 1 - i, 0)), rblk(FOX_W, 0), rblk(FOX_W, 0),
                  rblk(SB_W, 0), rblk(SB_W, 0), rblk(SB_W, 0), rblk(SB_W, 0),
                  rblk(POOL_W, FOX_W // POOL_W), halo(POOL_W, FOX_W // POOL_W), rblk(POOL_W, 0), rblk(POOL_W, 0),
                  full(qg), full(kg), full(bfp), full(wpd), full(ps)],
        out_specs=[rblk(PW, 0), acc_spec(1, FOX_W), acc_spec(1, FOX_W), acc_spec(1, LANES), acc_spec(POOL_W, POOL_W), acc_spec(1, POOL_W)],
        out_shape=[jax.ShapeDtypeStruct((S, PW), BF16), jax.ShapeDtypeStruct((1, FOX_W), F32), jax.ShapeDtypeStruct((1, FOX_W), F32),
                   jax.ShapeDtypeStruct((1, LANES), F32), jax.ShapeDtypeStruct((POOL_W, POOL_W), F32), jax.ShapeDtypeStruct((1, POOL_W), F32)],
        scratch_shapes=[pltpu.VMEM((1, LANES), F32), pltpu.VMEM((ts, LANES), F32), pltpu.VMEM((ts + POOL_HALO, POOL_W), F32),
                        pltpu.VMEM((LANES, ts), F32)],
        compiler_params=_cparams(dimension_semantics=("arbitrary",)),
    )(projm, projm, projm, projm, ffo, dqn, dkn, dct, dcr, dv, dfg, dsq, dsk, dsv, dsg, dmix, dmix, pooled, yp, qg, kg, bfp, wpd, ps)


def _stack_call(body, name, grid, in_specs, operands, slot_specs, slot_shapes, stacks, plain_specs=(), plain_shapes=(), **kw):
    out_specs = list(plain_specs) + list(slot_specs)
    out_shape = list(plain_shapes) + [jax.ShapeDtypeStruct((DEPTH,) + s, F32) for s in slot_shapes]
    if stacks is None:
        return pl.pallas_call(body, name=name, grid=grid, in_specs=in_specs, out_specs=out_specs, out_shape=out_shape, **kw)(*operands)
    n = len(operands)

    def aliased_body(*refs):
        body(*refs[:n], *refs[n + len(stacks):])

    return pl.pallas_call(
        aliased_body, name=name, grid=grid, in_specs=list(in_specs) + [pl.BlockSpec(memory_space=pl.ANY)] * len(stacks),
        out_specs=out_specs, out_shape=out_shape,
        input_output_aliases={n + k: len(plain_specs) + k for k in range(len(stacks))}, **kw)(*operands, *stacks)


def _inproj_dw(h, dproj, layer, stacks, *, ts, tn):
    S, D = h.shape
    nj = PM // tn

    def body(h_ref, dp_ref, dpf_ref, dw_ref, dwf_ref):
        s = pl.program_id(1)

        @pl.when(s == 0)
        def _():
            dw_ref[...] = jnp.zeros_like(dw_ref)

        @pl.when((s == 0) & (pl.program_id(0) == 0))
        def _():
            dwf_ref[...] = jnp.zeros_like(dwf_ref)

        hv = h_ref[...]
        dw_ref[...] += _dot_tn(dp_ref[...], hv)

        @pl.when(pl.program_id(0) == 0)
        def _():
            dwf_ref[...] += _dot_tn(dpf_ref[...], hv)

    return _stack_call(
        body, "inproj_dw", (nj, S // ts),
        [pl.BlockSpec((ts, D), lambda j, s: (s, 0)),
         pl.BlockSpec((ts, tn), lambda j, s: (s, j)),
         pl.BlockSpec((ts, LANES), lambda j, s: (s, PM // LANES))],
        (h, dproj, dproj),
        [pl.BlockSpec((None, tn, D), lambda j, s: (layer, j, 0)), pl.BlockSpec((None, LANES, D), lambda j, s: (layer, 0, 0))],
        [(PM, D), (LANES, D)], stacks,
        compiler_params=_cparams(dimension_semantics=("arbitrary", "arbitrary")))


def _inproj_dx(dproj, wt_all, layer, x, g, dy, *, tm):
    S, D = x.shape

    def body(dp_ref, w_ref, x_ref, g_ref, dy_ref, dx_ref, dg_ref):
        @pl.when(pl.program_id(0) == 0)
        def _():
            dg_ref[...] = jnp.zeros_like(dg_ref)

        dh = _dot(dp_ref[...], w_ref[...])
        xf = x_ref[...]
        rstd = lax.rsqrt(jnp.mean(xf * xf, axis=-1, keepdims=True) + EPS)
        xhat = xf * rstd
        dg_ref[...] += jnp.sum(dh * xhat, axis=0, keepdims=True)
        dyg = dh * g_ref[...]
        mean = jnp.mean(dyg * xhat, axis=-1, keepdims=True)
        dx_ref[...] = rstd * (dyg - xhat * mean) + dy_ref[...]

    row = lambda w: pl.BlockSpec((tm, w), lambda i: (i, 0))
    return pl.pallas_call(
        body, name="inproj_dx", grid=(S // tm,),
        in_specs=[row(PW), pl.BlockSpec((None, PW, D), lambda i: (layer, 0, 0)), row(D), pl.BlockSpec((1, D), lambda i: (0, 0)), row(D)],
        out_specs=[row(D), pl.BlockSpec((1, D), lambda i: (0, 0))],
        out_shape=[jax.ShapeDtypeStruct((S, D), F32), jax.ShapeDtypeStruct((1, D), F32)],
        compiler_params=_cparams(dimension_semantics=("arbitrary",)),
    )(dproj, wt_all, x, g, dy)


def _adam_update(w, g, m, v):
    nm = ADAM_B1 * m + (1.0 - ADAM_B1) * g
    nv = ADAM_B2 * v + (1.0 - ADAM_B2) * (g * g)
    m_hat = nm / (1.0 - ADAM_B1 ** ADAM_STEP)
    v_hat = nv / (1.0 - ADAM_B2 ** ADAM_STEP)
    return -ADAM_LR * (m_hat / (jnp.sqrt(v_hat) + ADAM_EPS) + ADAM_WD * w), nm, nv


def _adamw(w, g, m, v):
    L, R, C = w.shape
    tr = R if R <= 512 else 256

    def body(w_ref, g_ref, m_ref, v_ref, d_ref, nm_ref, nv_ref):
        d_ref[...], nm_ref[...], nv_ref[...] = _adam_update(w_ref[...], g_ref[...], m_ref[...], v_ref[...])

    spec = pl.BlockSpec((1, tr, C), lambda l, i: (l, i, 0))
    shp = jax.ShapeDtypeStruct((L, R, C), F32)
    return pl.pallas_call(
        body, name="adamw", grid=(L, R // tr), in_specs=[spec] * 4, out_specs=[spec] * 3, out_shape=[shp] * 3,
        compiler_params=_cparams(dimension_semantics=("arbitrary", "arbitrary")),
    )(w, g, m, v)


def _adamw_nd(w, g, m, v):
    shape = w.shape
    view = (1,) + shape if w.ndim == 2 else (shape[0], -1, shape[-1])
    outs = _adamw(w.reshape(view), g.reshape(view), m.reshape(view), v.reshape(view))
    return tuple(o.reshape(shape) for o in outs)


FLIP_C = (0, 0, 1)
FLIP_X = (1, 0, 0)
FLIP_Y = (0, 1, 0)
FLIP_XY = (1, 1, 0)
MESH = pl.DeviceIdType.MESH


def _peer(flip):
    me = (lax.axis_index("x"), lax.axis_index("y"), lax.axis_index("c"))
    return tuple(1 - a if f else a for a, f in zip(me, flip))


def _exchange(name, arrays, flips):
    n = len(arrays)

    def body(*refs):
        srcs, dsts = refs[:n], refs[n:2 * n]
        send_sems, recv_sems = refs[2 * n:]
        copies = [pltpu.make_async_remote_copy(src_ref=srcs[k], dst_ref=dsts[k], send_sem=send_sems.at[k], recv_sem=recv_sems.at[k],
                                               device_id=_peer(flips[k]), device_id_type=MESH) for k in range(n)]
        for cp in copies:
            cp.start()
        for cp in copies:
            cp.wait()

    anyspec = pl.BlockSpec(memory_space=pl.ANY)
    return pl.pallas_call(
        body, name=name, in_specs=[anyspec] * n, out_specs=[anyspec] * n,
        out_shape=[jax.ShapeDtypeStruct(a.shape, a.dtype) for a in arrays],
        scratch_shapes=[pltpu.SemaphoreType.DMA((n,)), pltpu.SemaphoreType.DMA((n,))],
    )(*arrays)


def _exchange_add(name, x, flip):
    def body(x_ref, o_ref, buf_ref, send_sem, recv_sem):
        cp = pltpu.make_async_remote_copy(src_ref=x_ref, dst_ref=buf_ref, send_sem=send_sem, recv_sem=recv_sem,
                                          device_id=_peer(flip), device_id_type=MESH)
        cp.start()
        cp.wait()
        o_ref[...] = x_ref[...] + buf_ref[...]

    vspec = pl.BlockSpec(memory_space=pltpu.VMEM)
    return pl.pallas_call(
        body, name=name, in_specs=[vspec], out_specs=vspec, out_shape=jax.ShapeDtypeStruct(x.shape, x.dtype),
        scratch_shapes=[pltpu.VMEM(x.shape, x.dtype), pltpu.SemaphoreType.DMA, pltpu.SemaphoreType.DMA],
    )(x)


def _chip_index():
    return 2 * lax.axis_index("x") + lax.axis_index("y")


def _gather_weights(w_in_t, w_out):
    wi = w_in_t.astype(BF16)
    wo = jnp.swapaxes(w_out, 0, 1).astype(BF16)
    halves = (wi.shape[0] // 2, wo.shape[0] // 2)
    ARR = 2
    TO_X, TO_Y, ON_Y, ON_X, SIB_X, SIB_Y, SIB_D0, SIB_D1, OWN = [ARR * k for k in range(9)]
    n_sems = ARR * 9

    def body(wi_ref, wo_ref, gi_ref, go_ref, send_sems, recv_sems):
        c = lax.axis_index("c")
        j = _chip_index()
        srcs = (wi_ref, wo_ref)
        dsts = (gi_ref, go_ref)
        def cuts(core):
            return [(pl.ds(h * core, h), pl.ds(h * core, h // 2), pl.ds(h * core + h // 2, h - h // 2)) for h in halves]
        mine, theirs = cuts(c), cuts(1 - c)
        HALF, Q0, Q1 = 0, 1, 2

        def copy(idx, src, dst, flip):
            return pltpu.make_async_remote_copy(src_ref=src, dst_ref=dst, send_sem=send_sems.at[idx], recv_sem=recv_sems.at[idx],
                                                device_id=_peer(flip), device_id_type=MESH)

        def slot(a, shard, cut):
            return dsts[a].at[shard, cut]

        jx, jy, jd = j ^ 2, j ^ 1, j ^ 3
        sends = []

        def start(cp):
            cp.start()
            sends.append(cp)

        for a in range(ARR):
            start(copy(TO_X + a, srcs[a].at[mine[a][HALF]], slot(a, j, mine[a][HALF]), FLIP_X))
            start(copy(TO_Y + a, srcs[a].at[mine[a][HALF]], slot(a, j, mine[a][HALF]), FLIP_Y))
        own = [copy(OWN + a, srcs[a], dsts[a].at[j], FLIP_C) for a in range(ARR)]
        for cp in own:
            cp.start()
        for a in range(ARR):
            copy(TO_X + a, slot(a, jx, mine[a][HALF]), slot(a, jx, mine[a][HALF]), FLIP_X).wait_recv()
            start(copy(ON_Y + a, slot(a, jx, mine[a][Q0]), slot(a, jx, mine[a][Q0]), FLIP_Y))
            start(copy(SIB_X + a, slot(a, jx, mine[a][HALF]), slot(a, jx, mine[a][HALF]), FLIP_C))
        for a in range(ARR):
            copy(TO_Y + a, slot(a, jy, mine[a][HALF]), slot(a, jy, mine[a][HALF]), FLIP_Y).wait_recv()
            start(copy(ON_X + a, slot(a, jy, mine[a][Q1]), slot(a, jy, mine[a][Q1]), FLIP_X))
            start(copy(SIB_Y + a, slot(a, jy, mine[a][HALF]), slot(a, jy, mine[a][HALF]), FLIP_C))
        for a in range(ARR):
            copy(ON_Y + a, slot(a, jd, mine[a][Q0]), slot(a, jd, mine[a][Q0]), FLIP_Y).wait_recv()
            start(copy(SIB_D0 + a, slot(a, jd, mine[a][Q0]), slot(a, jd, mine[a][Q0]), FLIP_C))
        for a in range(ARR):
            copy(ON_X + a, slot(a, jd, mine[a][Q1]), slot(a, jd, mine[a][Q1]), FLIP_X).wait_recv()
            start(copy(SIB_D1 + a, slot(a, jd, mine[a][Q1]), slot(a, jd, mine[a][Q1]), FLIP_C))
        for a in range(ARR):
            for idx, shard, cut in ((SIB_X, jx, HALF), (SIB_Y, jy, HALF), (SIB_D0, jd, Q0), (SIB_D1, jd, Q1)):
                copy(idx + a, slot(a, shard, theirs[a][cut]), slot(a, shard, theirs[a][cut]), FLIP_C).wait_recv()
        for cp in own:
            cp.wait()
        for cp in sends:
            cp.wait_send()

    anyspec = pl.BlockSpec(memory_space=pl.ANY)
    gi, go = pl.pallas_call(
        body, name="gather_weights", in_specs=[anyspec] * 2, out_specs=[anyspec] * 2,
        out_shape=[jax.ShapeDtypeStruct((4,) + wi.shape, BF16), jax.ShapeDtypeStruct((4,) + wo.shape, BF16)],
        scratch_shapes=[pltpu.SemaphoreType.DMA((n_sems,)), pltpu.SemaphoreType.DMA((n_sems,))],
    )(wi, wo)
    w_in_t_full = gi.reshape((4 * wi.shape[0],) + wi.shape[1:])
    w_out_full = jnp.swapaxes(go.reshape((4 * wo.shape[0],) + wo.shape[1:]), 0, 1)
    return w_in_t_full, w_out_full


def _to_aligned(w_t):
    _, L, D = w_t.shape
    npair = FOX_HEADS // 2
    ff = w_t[ORIG_FF:ORIG_REST].reshape(npair, 2, L, D)
    ff = jnp.pad(ff, ((0, 0), (0, FF_STRIDE - 2), (0, 0), (0, 0))).reshape(npair * FF_STRIDE, L, D)
    ff = jnp.pad(ff, ((0, LANES - npair * FF_STRIDE), (0, 0), (0, 0)))
    return jnp.swapaxes(jnp.concatenate([w_t[:ORIG_FOX], w_t[ORIG_REST:], ff], axis=0), 0, 1)


def _from_aligned(dw_t):
    n, _, D = dw_t.shape
    npair = FOX_HEADS // 2
    ff = dw_t[:, PM:PM + npair * FF_STRIDE].reshape(n, npair, FF_STRIDE, D)[:, :, :2].reshape(n, FOX_HEADS, D)
    return jnp.swapaxes(jnp.concatenate([dw_t[:, :ORIG_FOX], ff, dw_t[:, ORIG_FOX:PM]], axis=1), 0, 1)


RELAY_ROWS = 256


def _rows_first(m, f):
    n, _, D = m.shape
    npair = FOX_HEADS // 2
    first_late = ORIG_FOX // RELAY_ROWS

    def body(m_ref, f_ref, out_ref, buf_ref, ff_ref, sem, ff_sem):
        i = pl.program_id(0)
        for l in range(n):
            buf_ref[:, l, :] = m_ref[l]
        start = pl.multiple_of(i * RELAY_ROWS, FOX_HEADS) + jnp.where(i >= first_late, FOX_HEADS, 0)
        main = pltpu.make_async_copy(buf_ref, out_ref.at[pl.ds(start, RELAY_ROWS)], sem)
        main.start()

        @pl.when(i == 0)
        def _():
            for l in range(n):
                for p in range(npair):
                    ff_ref[2 * p:2 * p + 2, l, :] = f_ref[l, FF_STRIDE * p:FF_STRIDE * p + 2, :]
            ff = pltpu.make_async_copy(ff_ref, out_ref.at[pl.ds(ORIG_FF, FOX_HEADS)], ff_sem)
            ff.start()
            ff.wait()

        main.wait()

    return pl.pallas_call(
        body, name="rs_rows_first", grid=(PM // RELAY_ROWS,),
        in_specs=[pl.BlockSpec((n, RELAY_ROWS, D), lambda i: (0, i, 0)), pl.BlockSpec((n, LANES, D), lambda i: (0, 0, 0))],
        out_specs=pl.BlockSpec(memory_space=pl.ANY), out_shape=jax.ShapeDtypeStruct((D_IN, n, D), F32),
        scratch_shapes=[pltpu.VMEM((RELAY_ROWS, n, D), F32), pltpu.VMEM((FOX_HEADS, n, D), F32),
                        pltpu.SemaphoreType.DMA, pltpu.SemaphoreType.DMA],
        compiler_params=_cparams(dimension_semantics=("arbitrary",)),
    )(m, f)


def _half_layers(name, stack, got, also_bf16=True):
    L, R, C = stack.shape
    half = L // 2
    tr = min(256, R)
    c = lax.axis_index("c")
    which = ((1 - c) if got is None else c).astype(jnp.int32).reshape(1)

    def body(c_ref, x_ref, *refs):
        if got is None:
            refs[0][...] = x_ref[...].astype(BF16)
        else:
            acc = x_ref[...] + refs[0][...].astype(F32)
            refs[1][...] = acc
            if also_bf16:
                refs[2][...] = acc.astype(BF16)

    plain = pl.BlockSpec((1, tr, C), lambda l, i, c_ref: (l, i, 0))
    picked = pl.BlockSpec((1, tr, C), lambda l, i, c_ref: (c_ref[0] * half + l, i, 0))
    shp = lambda dt: jax.ShapeDtypeStruct((half, R, C), dt)
    out_shape = [shp(BF16)] if got is None else [shp(F32)] + ([shp(BF16)] if also_bf16 else [])
    grid_spec = pltpu.PrefetchScalarGridSpec(
        num_scalar_prefetch=1, grid=(half, R // tr),
        in_specs=[picked] + ([] if got is None else [plain]), out_specs=[plain] * len(out_shape))
    return pl.pallas_call(
        body, name=name, grid_spec=grid_spec, out_shape=out_shape,
        compiler_params=_cparams(dimension_semantics=("arbitrary", "arbitrary")),
    )(which, stack, *([] if got is None else [got]))


def _reduce_scatter(stack_m, stack_f, stack_o, shard_cols, shard_rows):
    j = _chip_index()
    half = DEPTH // 2
    stacks = (stack_m, stack_f, stack_o)
    give = [_half_layers("rs_give", s, None)[0] for s in stacks]
    got = _exchange("rs_d2d", give, (FLIP_C,) * len(stacks))
    (m32,), (f32_,) = [_half_layers("rs_add_chip", s, g, also_bf16=False) for s, g in zip(stacks[:2], got[:2])]
    o32, obf = _half_layers("rs_add_chip", stack_o, got[2])
    d_model = stack_m.shape[2]
    in32 = _rows_first(m32, f32_).reshape(4, shard_cols, half, d_model)

    def out_shards(o):
        return jnp.moveaxis(o.reshape(half, 4, shard_rows, o.shape[-1]), 1, 0)

    chip = [(in32, in32.astype(BF16), 0), (out_shards(o32), out_shards(obf), 1)]
    shard = lambda a, idx: lax.dynamic_index_in_dim(a, idx, axis=0, keepdims=False)
    via = []
    for _, bf, axis in chip:
        diag = shard(bf, j ^ 3)
        cut = diag.shape[axis] // 2
        via += [lax.slice_in_dim(diag, 0, cut, axis=axis), lax.slice_in_dim(diag, cut, 2 * cut, axis=axis)]
    handed = _exchange("rs_via", via, (FLIP_X, FLIP_Y) * len(chip))
    sends = []
    for a, (f32_sum, _, axis) in enumerate(chip):
        sends.append(_add_half_along("rs_add_via", shard(f32_sum, j ^ 2), handed[2 * a + 1], axis, 1))
        sends.append(_add_half_along("rs_add_via", shard(f32_sum, j ^ 1), handed[2 * a], axis, 0))
    got = _exchange("rs_ici", sends, (FLIP_X, FLIP_Y) * len(chip))
    own_in, own_out = [shard(f32_sum, j) for f32_sum, _, _ in chip]
    mine_in = _add_rows("rs_add_in", own_in, list(got[0:2]))
    mine_out = _add_into_half("rs_add_out", own_out, list(got[2:4]))
    sib_in, g_out = _share_halves(mine_in, mine_out)
    return (mine_in, sib_in), g_out


def _add_half_along(name, base, extra, axis, which):
    lanes = min(ROW_LANE_CHUNK, base.shape[2])
    assert base.shape[axis] == 2 * extra.shape[axis]
    blk = tuple(base.shape[d] // 2 if d == axis else base.shape[d] for d in range(2)) + (lanes,)

    def body(b_ref, e_ref, o_ref):
        x = b_ref[...]
        o_ref[...] = jnp.where(pl.program_id(0) == which, x + e_ref[...].astype(F32), x).astype(BF16)

    at = lambda i, k: (i, 0, k) if axis == 0 else (0, i, k)
    return pl.pallas_call(
        body, name=name, grid=(2, base.shape[2] // lanes),
        in_specs=[pl.BlockSpec(blk, at), pl.BlockSpec(blk, lambda i, k: (0, 0, k))], out_specs=pl.BlockSpec(blk, at),
        out_shape=jax.ShapeDtypeStruct(base.shape, BF16),
        compiler_params=_cparams(dimension_semantics=("arbitrary", "arbitrary")),
    )(base, extra)


def _add_rows(name, first, others):
    n = len(others)

    def body(*refs):
        acc = refs[0][...]
        for r in refs[1:1 + n]:
            acc = acc + r[...].astype(F32)
        refs[1 + n][...] = acc

    grid, spec = _row_lane_blocks(first.shape)
    return pl.pallas_call(
        body, name=name, grid=grid, in_specs=[spec(first.shape[1])] * (1 + n), out_specs=spec(first.shape[1]),
        out_shape=jax.ShapeDtypeStruct(first.shape, F32),
        compiler_params=_cparams(dimension_semantics=("arbitrary", "arbitrary")),
    )(first, *others)


ROW_LANE_CHUNK = 256


def _row_lane_blocks(shape):
    rows, _, C = shape
    tr = rows // 2 if rows % 2 == 0 and rows > 64 else rows
    lanes = min(ROW_LANE_CHUNK, C)
    return (rows // tr, C // lanes), lambda n_mid: pl.BlockSpec((tr, n_mid, lanes), lambda i, k, *_: (i, 0, k))


def _add_into_half(name, first, others):
    half, rows, C = first.shape
    tr = min(256, rows)
    n = len(others)

    def body(c_ref, *refs):
        acc = refs[0][...]
        for r in refs[1:1 + n]:
            acc = acc + r[...].astype(F32)
        refs[1 + n][...] = acc

    grid_spec = pltpu.PrefetchScalarGridSpec(
        num_scalar_prefetch=1, grid=(half, rows // tr),
        in_specs=[pl.BlockSpec((1, tr, C), lambda l, i, c_ref: (l, i, 0))] * (1 + n),
        out_specs=pl.BlockSpec((1, tr, C), lambda l, i, c_ref: (c_ref[0] * half + l, i, 0)))
    return pl.pallas_call(
        body, name=name, grid_spec=grid_spec, out_shape=jax.ShapeDtypeStruct((2 * half, rows, C), F32),
        compiler_params=_cparams(dimension_semantics=("arbitrary", "arbitrary")),
    )(lax.axis_index("c").astype(jnp.int32).reshape(1), first, *others)


def _share_halves(mine, buf):
    half = DEPTH // 2

    def body(mine_ref, buf_in, sib_ref, buf_ref, send_sems, recv_sems):
        lay = pl.ds(half * lax.axis_index("c"), half)
        copies = [pltpu.make_async_remote_copy(src_ref=src, dst_ref=dst, send_sem=send_sems.at[k], recv_sem=recv_sems.at[k],
                                               device_id=_peer(FLIP_C), device_id_type=MESH)
                  for k, (src, dst) in enumerate(((mine_ref, sib_ref), (buf_ref.at[lay], buf_ref.at[lay])))]
        for cp in copies:
            cp.start()
        for cp in copies:
            cp.wait()

    anyspec = pl.BlockSpec(memory_space=pl.ANY)
    return pl.pallas_call(
        body, name="rs_share", in_specs=[anyspec] * 2, out_specs=[anyspec] * 2,
        out_shape=[jax.ShapeDtypeStruct(mine.shape, mine.dtype), jax.ShapeDtypeStruct(buf.shape, buf.dtype)],
        input_output_aliases={1: 1},
        scratch_shapes=[pltpu.SemaphoreType.DMA((2,)), pltpu.SemaphoreType.DMA((2,))],
    )(mine, buf)


def _adamw_halves(w, g_mine, g_sib, m, v):
    half = g_mine.shape[1]

    def body(c_ref, w_ref, gm_ref, gs_ref, m_ref, v_ref, g_ref, d_ref, nm_ref, nv_ref):
        first = c_ref[0] == 0
        gm, gs = gm_ref[...], gs_ref[...]
        for h, gv in enumerate((jnp.where(first, gm, gs), jnp.where(first, gs, gm))):
            lay = slice(half * h, half * (h + 1))
            g_ref[:, lay, :] = gv
            d_ref[:, lay, :], nm_ref[:, lay, :], nv_ref[:, lay, :] = _adam_update(w_ref[:, lay, :], gv, m_ref[:, lay, :], v_ref[:, lay, :])

    grid, spec = _row_lane_blocks(w.shape)
    full, part = spec(w.shape[1]), spec(half)
    grid_spec = pltpu.PrefetchScalarGridSpec(num_scalar_prefetch=1, grid=grid, in_specs=[full, part, part, full, full], out_specs=[full] * 4)
    return pl.pallas_call(
        body, name="adamw_halves", grid_spec=grid_spec, out_shape=[jax.ShapeDtypeStruct(w.shape, F32)] * 4,
        compiler_params=_cparams(dimension_semantics=("arbitrary", "arbitrary")),
    )(lax.axis_index("c").astype(jnp.int32).reshape(1), w, g_mine, g_sib, m, v)


def _all_reduce_small(x):
    x = _exchange_add("ar_c", x, FLIP_C)
    x = _exchange_add("ar_y", x, FLIP_Y)
    return _exchange_add("ar_x", x, FLIP_X)


def _blocks(S):
    return dict(tm=min(512, S), tm_proj=min(1024, S), ts=min(512, S), tq=min(512, S), tq_big=min(1024, S), tk=min(512, S), tks=min(256, S))


def _pair_pad(vec):
    npair = FOX_HEADS // 2
    v = jnp.pad(vec.reshape(npair, 2), ((0, 0), (0, FF_STRIDE - 2))).reshape(1, npair * FF_STRIDE)
    return jnp.pad(v, ((0, 0), (0, LANES - npair * FF_STRIDE)))


def _pair_unpad(row):
    npair = FOX_HEADS // 2
    return row[0, :npair * FF_STRIDE].reshape(npair, FF_STRIDE)[:, :2].reshape(FOX_HEADS)


def _pool_blockdiag(w_pool):
    g, cg, _ = w_pool.shape
    eye = jnp.eye(g, dtype=w_pool.dtype)
    return jnp.einsum("gh,gcd->gchd", eye, w_pool).reshape(g * cg, g * cg)


QK_BOUND_SLACK = 1.05


def _layer_params(norm_g, b_f, q_norm_g, k_norm_g, w_pool, pool_scale):
    qk_bound = QK_BOUND_SLACK * HEAD_DIM * QK_SCALE * jnp.max(jnp.abs(q_norm_g)) * jnp.max(jnp.abs(k_norm_g))
    return dict(g=norm_g.reshape(1, -1), qg=jnp.tile(q_norm_g, FOX_HEADS).reshape(1, FOX_W), kg=jnp.tile(k_norm_g, FOX_HEADS).reshape(1, FOX_W),
                bfp=_pair_pad(b_f), wpd=_pool_blockdiag(w_pool).astype(BF16), ps=pool_scale.reshape(1, POOL_W),
                qkb=jnp.full((1, LANES), qk_bound, F32))


def _layer_fwd(x, wt_all, w_out, layer, prm, bs):
    projm, ffo, h = _inproj(x, prm["g"], wt_all, layer, tm=bs["tm_proj"], tn=PROJ_TN)
    qn, ka, kb, v, sq, sk, sv, pooled, yp, pm = _prep(projm, ffo, prm["qg"], prm["kg"], prm["bfp"], prm["wpd"], prm["ps"], ts=bs["ts"])
    o, lse, fm = _fox_fwd(qn, ka, kb, v, projm, prm["qkb"], tq=bs["tq"], tk=bs["tk"])
    so, sm = _sb_fwd(sq, sk, sv, projm, tq=bs["tq"], tk=bs["tks"])
    y = _outproj(x, fm, pm, sm, w_out, layer, tm=bs["tm_proj"])
    saved = dict(x=x, projm=projm, ffo=ffo, h=h, qn=qn, ka=ka, kb=kb, v=v, sq=sq, sk=sk, sv=sv, pooled=pooled, yp=yp,
                 o=o, lse=lse, so=so, fm=fm, pm=pm, sm=sm)
    return y, saved


def _layer_bwd(dy, wt_all, w_out, prm, sv_, bs, layer, stacks):
    dmix, stack_o = _outproj_bwd(dy, sv_["fm"], sv_["pm"], sv_["sm"], w_out, layer, None if stacks is None else stacks[2:], tm=bs["tm_proj"])
    dqn, dkn, dv, dfg, dct, dcr = _fox_bwd(sv_["qn"], sv_["ka"], sv_["kb"], sv_["v"], sv_["o"], sv_["lse"], dmix, sv_["projm"],
                                      prm["qkb"], tq=bs["tq_big"], tk=bs["tk"])
    dsq, dsk, dsv, dsg = _sb_bwd(sv_["sq"], sv_["sk"], sv_["sv"], sv_["so"], dmix, sv_["projm"], tq=bs["tks"], tk=bs["tks"])
    dproj, dqg, dkg, dbf, dwp, dps = _prep_bwd(sv_["projm"], sv_["ffo"], dqn, dkn, dct, dcr, dv, dfg, dsq, dsk, dsv, dsg, dmix,
                                               sv_["pooled"], sv_["yp"], prm["qg"], prm["kg"], prm["bfp"], prm["wpd"], prm["ps"], ts=bs["ts"])
    stack_m, stack_f = _inproj_dw(sv_["h"], dproj, layer, None if stacks is None else stacks[:2], ts=bs["tm_proj"], tn=PROJ_TN)
    dx, dg = _inproj_dx(dproj, wt_all, layer, sv_["x"], prm["g"], dy, tm=bs["tm"])
    grads = dict(
        norm_g=dg[0],
        b_f=_pair_unpad(dbf), q_norm_g=dqg.reshape(FOX_HEADS, HEAD_DIM).sum(0), k_norm_g=dkg.reshape(FOX_HEADS, HEAD_DIM).sum(0),
        w_pool=jnp.stack([dwp[HEAD_DIM * g:HEAD_DIM * (g + 1), HEAD_DIM * g:HEAD_DIM * (g + 1)] for g in range(4)]),
        pool_scale=dps[0])
    return dx, grads, (stack_m, stack_f, stack_o)


def _local_step(x, target, wt_all, w_out, norm_g, b_f, q_norm_g, k_norm_g, w_pool, pool_scale):
    S, D = x.shape
    bs = _blocks(S)
    prms = [_layer_params(norm_g[l], b_f[l], q_norm_g[l], k_norm_g[l], w_pool[l], pool_scale[l]) for l in range(DEPTH)]
    saved = []
    y = x
    for l in range(DEPTH):
        y, s_ = _layer_fwd(y, wt_all, w_out, l, prms[l], bs)
        saved.append(s_)
    dy, sq = _loss_head(y, target, tm=bs["tm"])
    loss = 0.5 * jnp.sum(sq) / D
    grads = [None] * DEPTH
    stacks = None
    for l in reversed(range(DEPTH)):
        dy, grads[l], stacks = _layer_bwd(dy, wt_all, w_out, prms[l], saved[l], bs, l, stacks)
    stacked = {k: jnp.stack([g[k] for g in grads]) for k in grads[0]}
    return loss, dy, stacked, stacks


SMALL = ("norm_g", "b_f", "q_norm_g", "k_norm_g", "w_pool", "pool_scale")


def _pack_small(gr):
    flat = jnp.concatenate([gr[k].reshape(-1) for k in SMALL])
    pad = (-flat.shape[0]) % (8 * LANES)
    return jnp.pad(flat, (0, pad)).reshape(-1, LANES)


def _unpack_small(packed, like):
    flat = packed.reshape(-1)
    out, off = {}, 0
    for k in SMALL:
        n = like[k].size
        out[k] = flat[off:off + n].reshape(like[k].shape)
        off += n
    return out


def kernel(x, norm_g, w_in, b_f, q_norm_g, k_norm_g, w_pool, pool_scale, w_out, loss_target, m_norm_g, m_w_in, m_b_f, m_q_norm_g, m_k_norm_g, m_w_pool, m_pool_scale, m_w_out, v_norm_g, v_w_in, v_b_f, v_q_norm_g, v_k_norm_g, v_w_pool, v_pool_scale, v_w_out):
    weights = dict(norm_g=norm_g, w_in=w_in, b_f=b_f, q_norm_g=q_norm_g, k_norm_g=k_norm_g, w_pool=w_pool, pool_scale=pool_scale, w_out=w_out)
    mom_m = dict(norm_g=m_norm_g, w_in=m_w_in, b_f=m_b_f, q_norm_g=m_q_norm_g, k_norm_g=m_k_norm_g, w_pool=m_w_pool, pool_scale=m_pool_scale, w_out=m_w_out)
    mom_v = dict(norm_g=v_norm_g, w_in=v_w_in, b_f=v_b_f, q_norm_g=v_q_norm_g, k_norm_g=v_k_norm_g, w_pool=v_w_pool, pool_scale=v_pool_scale, w_out=v_w_out)
    shard_cols = w_in.shape[2]
    shard_rows = w_out.shape[1]

    cols_first = lambda a: jnp.transpose(a, (2, 0, 1))
    w_in_t = cols_first(w_in)
    w_in_t_full, w_out_full = _gather_weights(w_in_t, w_out)
    wt_all = _to_aligned(w_in_t_full)
    loss, dx, gr, stacks = _local_step(x[0], loss_target[0], wt_all, w_out_full, norm_g, b_f, q_norm_g, k_norm_g, w_pool, pool_scale)
    loss = lax.psum(loss, ("x", "y", "c"))

    (g_in_mine, g_in_sib), g_w_out = _reduce_scatter(*stacks, shard_cols, shard_rows)
    small = _unpack_small(_all_reduce_small(_pack_small(gr)), {k: weights[k] for k in SMALL})
    grad_w = dict(small, w_out=g_w_out)

    names = ("norm_g", "w_in", "b_f", "q_norm_g", "k_norm_g", "w_pool", "pool_scale", "w_out")
    upd = {k: _adamw_nd(weights[k], grad_w[k], mom_m[k], mom_v[k]) for k in names if k != "w_in"}
    in_t = _adamw_halves(w_in_t, g_in_mine, g_in_sib, cols_first(mom_m["w_in"]), cols_first(mom_v["w_in"]))
    grad_w["w_in"], *upd["w_in"] = [jnp.transpose(a, (1, 2, 0)) for a in in_t]
    return (loss, dx[None], *[grad_w[k] for k in names], *[upd[k][0] for k in names], *[upd[k][1] for k in names], *[upd[k][2] for k in names])
```

```python
import functools

import jax
import jax.numpy as jnp
from jax import lax
from jax.experimental import pallas as pl
from jax.experimental.pallas import tpu as pltpu

F32 = jnp.float32
BF16 = jnp.bfloat16

DEPTH = 4
HEAD_DIM = 64
FOX_HEADS = 8
SB_HEADS = 4
FOX_W = FOX_HEADS * HEAD_DIM
SB_W = SB_HEADS * HEAD_DIM
POOL_W = 256
POOL_WINDOWS = (2, 4, 8, 16)
POOL_HALO = 16
D_MIX = FOX_W + POOL_W + SB_W
EPS = 1e-6
NEG = -1e30
QK_SCALE = HEAD_DIM ** -0.5

ORIG_FOX = 4 * FOX_W
ORIG_FF = ORIG_FOX
ORIG_REST = ORIG_FF + FOX_HEADS
D_IN = ORIG_REST + 2 * POOL_W + 4 * SB_W

C_FQ, C_FK, C_FV, C_FG = 0, FOX_W, 2 * FOX_W, 3 * FOX_W
C_PX = 4 * FOX_W
C_PG = C_PX + POOL_W
C_SQ = C_PG + POOL_W
C_SK, C_SV, C_SG = C_SQ + SB_W, C_SQ + 2 * SB_W, C_SQ + 3 * SB_W
PM = C_SG + SB_W
LANES = 128
LANE_SHIFT = 7
HEAD_SHIFT = 6
PW = PM + LANES
FF_STRIDE = 8
AUG = 3

ADAM_LR = 0.001
ADAM_B1 = 0.9
ADAM_B2 = 0.999
ADAM_EPS = 1e-08
ADAM_WD = 0.01
ADAM_STEP = 10

VMEM_LIMIT = 48 * 1024 * 1024
PROJ_TN = PM // 2


def _cparams(**kw):
    return pltpu.CompilerParams(vmem_limit_bytes=VMEM_LIMIT, **kw)


def _dot(a, b):
    return jnp.dot(a, b, preferred_element_type=F32)


def _dot_nt(a, b):
    return lax.dot_general(a, b, (((1,), (1,)), ((), ())), preferred_element_type=F32)


def _dot_tn(a, b):
    return lax.dot_general(a, b, (((0,), (0,)), ((), ())), preferred_element_type=F32)


def _split2(x):
    hi = x.astype(BF16)
    lo = (x - hi.astype(F32)).astype(BF16)
    return hi, lo


def _split3(x):
    hi = x.astype(BF16)
    r = x - hi.astype(F32)
    mid = r.astype(BF16)
    lo = (r - mid.astype(F32)).astype(BF16)
    return hi, mid, lo


def _dot_exact_rhs(x, m):
    hi, mid, lo = _split3(x)
    return _dot(hi, m) + _dot(mid, m) + _dot(lo, m)


def _dot_exact_lhs(m, x):
    hi, mid, lo = _split3(x)
    return _dot(m, hi) + _dot(m, mid) + _dot(m, lo)


def _sigmoid(x):
    return 1.0 / (1.0 + jnp.exp(-x))


def _silu_pair(x):
    s = _sigmoid(x)
    return x * s, s * (1.0 + x * (1.0 - s))


def _iota(shape, dim):
    return lax.broadcasted_iota(jnp.int32, shape, dim)


def _ones_where(cond):
    return jnp.where(cond, 1.0, 0.0).astype(BF16)


GROUP_SLAB = 256


def _head_blockdiag():
    rows, cols = _iota((2 * GROUP_SLAB, GROUP_SLAB), 0) & (GROUP_SLAB - 1), _iota((2 * GROUP_SLAB, GROUP_SLAB), 1)
    return _ones_where((rows >> HEAD_SHIFT) == (cols >> HEAD_SHIFT))


def _group_sum(x, bd):
    hi, lo = _split2(x)
    slabs = [_dot(jnp.concatenate([hi[:, s:s + GROUP_SLAB], lo[:, s:s + GROUP_SLAB]], axis=1), bd) for s in range(0, x.shape[1], GROUP_SLAB)]
    return jnp.concatenate(slabs, axis=1)


def _lane_pick(x, lane_idx, lane):
    return jnp.sum(jnp.where(lane_idx == lane, x, 0.0), axis=1, keepdims=True)


def _inproj(x, g, wt_all, layer, *, tm, tn):
    S, D = x.shape
    nj = PM // tn

    def body(x_ref, g_ref, w_ref, wff_ref, proj_ref, ff_ref, h_ref):
        @pl.when(pl.program_id(1) == 0)
        def _():
            xf = x_ref[...]
            ms = jnp.mean(xf * xf, axis=-1, keepdims=True)
            h = (xf * lax.rsqrt(ms + EPS) * g_ref[...]).astype(BF16)
            h_ref[...] = h
            ff_ref[...] = _dot_nt(h, wff_ref[...])

        proj_ref[...] = _dot_nt(h_ref[...], w_ref[...])

    return pl.pallas_call(
        body, name="inproj", grid=(S // tm, nj),
        in_specs=[pl.BlockSpec((tm, D), lambda i, j: (i, 0)),
                  pl.BlockSpec((1, D), lambda i, j: (0, 0)),
                  pl.BlockSpec((None, tn, D), lambda i, j: (layer, j, 0)),
                  pl.BlockSpec((None, LANES, D), lambda i, j: (layer, PM // LANES, 0))],
        out_specs=[pl.BlockSpec((tm, tn), lambda i, j: (i, j)),
                   pl.BlockSpec((tm, LANES), lambda i, j: (i, 0)),
                   pl.BlockSpec((tm, D), lambda i, j: (i, 0))],
        out_shape=[jax.ShapeDtypeStruct((S, PM), F32), jax.ShapeDtypeStruct((S, LANES), F32),
                   jax.ShapeDtypeStruct((S, D), BF16)],
        compiler_params=_cparams(dimension_semantics=("arbitrary", "arbitrary")),
    )(x, g, wt_all, wt_all)


def _pool_group_select(lane_group, vals):
    return jnp.where(lane_group == 0, vals[0], jnp.where(lane_group == 1, vals[1], jnp.where(lane_group == 2, vals[2], vals[3])))


def _prep(projm, ffo, qg, kg, bfp, wpd, ps, *, ts):
    S = projm.shape[0]
    nb = S // ts
    hb = ts // POOL_HALO

    def body(fq_ref, fk_ref, fv_ref, pp_ref, halo_ref, ff_ref, sq_ref, sk_ref, sv_ref,
             qg_ref, kg_ref, bf_ref, wpd_ref, ps_ref,
             qn_ref, ka_ref, kb_ref, v_ref, sqo_ref, sko_ref, svo_ref, pooled_ref, yp_ref, pm_ref,
             carry_ref, c_ref, buf_ref):
        i = pl.program_id(0)
        bd = _head_blockdiag()
        normed = []
        for src, g_ref in ((fq_ref, qg_ref), (fk_ref, kg_ref)):
            q = src[...]
            ss = _group_sum(q * q, bd)
            normed.append(q * lax.rsqrt(ss * (1.0 / HEAD_DIM) + EPS) * g_ref[...])
        qn_ref[...] = (normed[0] * QK_SCALE).astype(BF16)
        kn = normed[1]
        v_ref[...] = fv_ref[...].astype(BF16)
        sqo_ref[...] = (sq_ref[...] * QK_SCALE).astype(BF16)
        sko_ref[...] = sk_ref[...].astype(BF16)
        svo_ref[...] = sv_ref[...].astype(BF16)

        @pl.when(i == 0)
        def _():
            carry_ref[...] = jnp.zeros_like(carry_ref)

        z = ff_ref[...] + bf_ref[...]
        lf = jnp.minimum(z, 0.0) - jnp.log(1.0 + jnp.exp(-jnp.abs(z)))
        tri = _ones_where(_iota((ts, ts), 1) <= _iota((ts, ts), 0))
        c = _dot_exact_lhs(tri, lf) + carry_ref[...]
        c_ref[...] = c
        carry_ref[...] = c_ref[ts - 1:ts, :]
        parts = jnp.concatenate(_split3(-c), axis=1)
        row = _iota((AUG * LANES, FOX_W), 0)
        col = _iota((AUG * LANES, FOX_W), 1)
        part, src = row >> LANE_SHIFT, row & (LANES - 1)
        pair, off = col >> LANE_SHIFT, col & (LANES - 1)
        sel_a = _ones_where((src == FF_STRIDE * pair) & (off == HEAD_DIM + part))
        sel_b = _ones_where((src == FF_STRIDE * pair + 1) & (off == part))
        first_half = (_iota((1, FOX_W), 1) & HEAD_DIM) == 0
        ka_ref[...] = jnp.where(first_half, kn, _dot(parts, sel_a)).astype(BF16)
        kb_ref[...] = jnp.where(first_half, _dot(parts, sel_b), kn).astype(BF16)

        x = pp_ref[:, 0:POOL_W]
        pg = pp_ref[:, POOL_W:2 * POOL_W]
        halo = jnp.where(i > 0, halo_ref[:, 0:POOL_W], 0.0)
        buf_ref[0:POOL_HALO, :] = halo
        buf_ref[POOL_HALO:POOL_HALO + ts, :] = x
        acc = x
        snaps = []
        for d in range(1, POOL_HALO):
            acc = acc + buf_ref[pl.ds(POOL_HALO - d, ts), :]
            if d + 1 in POOL_WINDOWS:
                snaps.append(acc)
        lane_group = _iota((1, POOL_W), 1) >> HEAD_SHIFT
        wsum = _pool_group_select(lane_group, snaps)
        wlen = _pool_group_select(lane_group, [float(w) for w in POOL_WINDOWS])
        tpos = (i * ts + _iota((ts, 1), 0) + 1).astype(F32)
        pooled = wsum / jnp.minimum(tpos, wlen) - x
        pb = pooled.astype(BF16)
        pooled_ref[...] = pb
        yp = _dot(pb, wpd_ref[...])
        yp_ref[...] = yp
        pm_ref[...] = (yp * ps_ref[...] * (pg * _sigmoid(pg))).astype(BF16)

    blk = lambda w, c: pl.BlockSpec((ts, w), lambda i: (i, c))
    full = lambda a: pl.BlockSpec(a.shape, lambda i: (0,) * a.ndim)
    out_shapes = [
        jax.ShapeDtypeStruct((S, FOX_W), BF16), jax.ShapeDtypeStruct((S, FOX_W), BF16), jax.ShapeDtypeStruct((S, FOX_W), BF16),
        jax.ShapeDtypeStruct((S, FOX_W), BF16),
        jax.ShapeDtypeStruct((S, SB_W), BF16), jax.ShapeDtypeStruct((S, SB_W), BF16), jax.ShapeDtypeStruct((S, SB_W), BF16),
        jax.ShapeDtypeStruct((S, POOL_W), BF16), jax.ShapeDtypeStruct((S, POOL_W), F32), jax.ShapeDtypeStruct((S, POOL_W), BF16),
    ]
    out_specs = [
        blk(FOX_W, 0), blk(FOX_W, 0), blk(FOX_W, 0), blk(FOX_W, 0),
        blk(SB_W, 0), blk(SB_W, 0), blk(SB_W, 0),
        blk(POOL_W, 0), blk(POOL_W, 0), blk(POOL_W, 0),
    ]
    return pl.pallas_call(
        body, name="prep", grid=(nb,),
        in_specs=[blk(FOX_W, C_FQ // FOX_W), blk(FOX_W, C_FK // FOX_W), blk(FOX_W, C_FV // FOX_W), blk(2 * POOL_W, C_PX // (2 * POOL_W)),
                  pl.BlockSpec((POOL_HALO, 2 * POOL_W), lambda i: (jnp.maximum(i * hb - 1, 0), C_PX // (2 * POOL_W))),
                  blk(LANES, 0),
                  blk(SB_W, C_SQ // SB_W), blk(SB_W, C_SK // SB_W), blk(SB_W, C_SV // SB_W),
                  full(qg), full(kg), full(bfp), full(wpd), full(ps)],
        out_specs=out_specs, out_shape=out_shapes,
        scratch_shapes=[pltpu.VMEM((1, LANES), F32), pltpu.VMEM((ts, LANES), F32), pltpu.VMEM((ts + POOL_HALO, POOL_W), F32)],
        compiler_params=_cparams(dimension_semantics=("arbitrary",)),
    )(projm, projm, projm, projm, projm, ffo, projm, projm, projm, qg, kg, bfp, wpd, ps)


def _pair_masks(x):
    ma = _iota((1, LANES), 1) < HEAD_DIM
    zero = jnp.zeros_like(x)
    return jnp.where(ma, x, zero), jnp.where(ma, zero, x)


DIAG_TILE = 256


def _diag_tiles(tq, size=DIAG_TILE):
    size = min(tq, size)
    return [(t * size, size) for t in range(tq // size)]


def _put_rows(old, new, r0):
    return new if r0 == 0 else jnp.concatenate([old[:r0], new], axis=0)


def _aug_queries(q):
    lane = _iota((1, LANES), 1)
    one = jnp.ones_like(q)
    zero = jnp.zeros_like(q)
    qa = jnp.where(lane < HEAD_DIM, q, jnp.where(lane < HEAD_DIM + AUG, one, zero))
    qb = jnp.where(lane >= HEAD_DIM, q, jnp.where(lane < AUG, one, zero))
    return qa, qb


EXP_DEAD = -105.0
PACK = 16


def _fox_walk_left(nfull, tk, block, carry, k_refs, qk_bound, row_floor):
    lane = _iota((1, LANES), 1)

    def alive(h, jj, c):
        k0 = pl.multiple_of(jnp.maximum(nfull - 1 - jj, 0) * tk + tk - PACK, PACK)
        last = k_refs[h][pl.ds(k0, PACK), :].astype(F32)
        lo = HEAD_DIM if h == 0 else 0
        negc = jnp.sum(jnp.where((lane >= lo) & (lane < lo + AUG), last, 0.0), axis=1, keepdims=True)
        return qk_bound + jnp.max(negc) - row_floor(c)[h] >= EXP_DEAD

    def walk(heads, jj0, c0):
        def go_on(state):
            jj, c = state
            ok = jj < nfull
            for h in heads:
                ok = ok & alive(h, jj, c)
            return ok

        def step(state):
            jj, c = state
            return jj + 1, block(pl.multiple_of((nfull - 1 - jj) * tk, tk), tk, 0, c, False, heads)

        return lax.while_loop(go_on, step, (jj0, c0))

    jj_pair, carry = walk((0, 1), jnp.int32(0), carry)
    carry = walk((0,), jj_pair, carry)[1]
    return walk((1,), jj_pair, carry)[1]


def _fox_fwd(qn, ka, kb, v, projm, qkb, *, tq, tk):
    S = qn.shape[0]
    npair = FOX_HEADS // 2

    def body(q_ref, ka_ref, kb_ref, v_ref, fg_ref, qkb_ref, o_ref, lse_ref, fm_ref):
        qi = pl.program_id(1)
        lane = _iota((1, LANES), 1)
        ma = lane < HEAD_DIM
        qaug = _aug_queries(q_ref[...])
        k_refs = (ka_ref, kb_ref)

        def block(k0, tkl, r0, carry, masked, heads=(0, 1)):
            vb = v_ref[pl.ds(k0, tkl), :]
            if masked:
                mask = (k0 + _iota((tq - r0, tkl), 1)) <= (qi * tq + r0 + _iota((tq - r0, tkl), 0))
            scores = {h: _dot_nt(qaug[h][r0:], k_refs[h][pl.ds(k0, tkl), :]) for h in heads}
            new = list(carry)
            for h in heads:
                m, l, acc = [x[r0:] for x in carry[h]]
                s = jnp.where(mask, scores[h], NEG) if masked else scores[h]
                m_new = jnp.maximum(m, jnp.max(s, axis=1, keepdims=True))
                alpha = jnp.exp(m - m_new)
                p = jnp.exp(s - m_new)
                sub = (m_new, alpha * l + jnp.sum(p, axis=1, keepdims=True), alpha * acc + _dot(p.astype(BF16), vb))
                new[h] = tuple(_put_rows(old, x, r0) for old, x in zip(carry[h], sub))
            return tuple(new)

        carry = tuple((jnp.full((tq, 1), NEG, F32), jnp.zeros((tq, 1), F32), jnp.zeros((tq, LANES), F32)) for _ in range(2))
        for off, size in _diag_tiles(tq, tq):
            carry = block(pl.multiple_of(qi * tq + off, size), size, off, carry, True)
        carry = _fox_walk_left((qi * tq) // tk, tk, block, carry, k_refs, jnp.max(qkb_ref[...]),
                               lambda c: (jnp.min(c[0][0]), jnp.min(c[1][0])))
        (ma_, la, acca), (mb_, lb, accb) = carry
        o = jnp.where(ma, acca / la, accb / lb)
        o_ref[...] = o
        lse_ref[...] = jnp.where(ma, ma_ + jnp.log(la), mb_ + jnp.log(lb))
        fg = fg_ref[...]
        fm_ref[...] = (o * (fg * _sigmoid(fg))).astype(BF16)

    qblk = pl.BlockSpec((tq, LANES), lambda p, i: (i, p))
    kvblk = pl.BlockSpec((S, LANES), lambda p, i: (0, p))
    return pl.pallas_call(
        body, name="fox_fwd", grid=(npair, S // tq),
        in_specs=[qblk, kvblk, kvblk, kvblk,
                  pl.BlockSpec((tq, LANES), lambda p, i: (i, C_FG // LANES + p)),
                  pl.BlockSpec((1, LANES), lambda p, i: (0, 0))],
        out_specs=[qblk, qblk, qblk],
        out_shape=[jax.ShapeDtypeStruct((S, FOX_W), F32), jax.ShapeDtypeStruct((S, FOX_W), F32), jax.ShapeDtypeStruct((S, FOX_W), BF16)],
        compiler_params=_cparams(dimension_semantics=("arbitrary", "arbitrary")),
    )(qn, ka, kb, v, projm, qkb)


def _suffix_sums(x, tmat2):
    return _dot(jnp.concatenate(_split2(x), axis=1), tmat2)


def _suffix_matrix(tk, inclusive):
    rr, cc = _iota((2 * tk, tk), 0) & (tk - 1), _iota((2 * tk, tk), 1)
    return _ones_where(rr >= cc) if inclusive else _ones_where(rr > cc)


def _sb_scores(qh, kb, causal, tmat2, r_runs):
    heads = range(2)
    zs = [_dot_nt(qh[h], kb) for h in heads]
    nsps = [jnp.minimum(-z, 0.0) - jnp.log(1.0 + jnp.exp(-jnp.abs(z))) for z in zs]
    lbs = nsps if causal is None else [jnp.where(causal, n, 0.0) for n in nsps]
    rins = [_suffix_sums(lb, tmat2) for lb in lbs]
    args = [zs[h] + lbs[h] + (rins[h] + r_runs[h]) for h in heads]
    a_s = [jnp.exp(arg if causal is None else jnp.where(causal, arg, NEG)) for arg in args]
    return zs, nsps, lbs, a_s


def _sb_walk_left(nfull, tk, block, carry, running_sums):
    def alive(state):
        jj, c = state
        ra, rb = running_sums(c)
        return (jj < nfull) & (jnp.max(jnp.maximum(ra, rb)) >= EXP_DEAD)

    def step(state):
        jj, c = state
        return jj + 1, block(pl.multiple_of((nfull - 1 - jj) * tk, tk), 0, c, False)

    return lax.while_loop(alive, step, (jnp.int32(0), carry))[1]


def _sb_fwd(sq, sk, sv, projm, *, tq, tk):
    S = sq.shape[0]
    npair = SB_HEADS // 2

    def body(q_ref, k_ref, v_ref, sg_ref, o_ref, sm_ref):
        qi = pl.program_id(1)
        lane = _iota((1, LANES), 1)
        ma = lane < HEAD_DIM
        qh = _pair_masks(q_ref[...])
        tmat2 = _suffix_matrix(tk, inclusive=False)
        nfull = (qi * tq) // tk

        def block(k0, r0, carry, masked):
            nr = tq - r0
            kb = k_ref[pl.ds(k0, tk), :]
            vb = v_ref[pl.ds(k0, tk), :]
            causal = (k0 + _iota((nr, tk), 1)) < (qi * tq + r0 + _iota((nr, tk), 0)) if masked else None
            _, _, lbs, a_s = _sb_scores([q[r0:] for q in qh], kb, causal, tmat2, [carry[h][0][r0:] for h in range(2)])
            pv = _dot(jnp.concatenate([a.astype(BF16) for a in a_s], axis=0), vb)
            return tuple((_put_rows(carry[h][0], carry[h][0][r0:] + jnp.sum(lbs[h], axis=1, keepdims=True), r0),
                          _put_rows(carry[h][1], carry[h][1][r0:] + pv[h * nr:(h + 1) * nr], r0)) for h in range(2))

        carry = tuple((jnp.zeros((tq, 1), F32), jnp.zeros((tq, LANES), F32)) for _ in range(2))
        for off, size in reversed(_diag_tiles(tq)):
            assert size == tk
            carry = block(pl.multiple_of(qi * tq + off, tk), off, carry, True)
        (_, acca), (_, accb) = _sb_walk_left(nfull, tk, block, carry, lambda c: (c[0][0], c[1][0]))
        o = jnp.where(ma, acca, accb)
        o_ref[...] = o
        sg = sg_ref[...]
        sm_ref[...] = (o * (sg * _sigmoid(sg))).astype(BF16)

    qblk = pl.BlockSpec((tq, LANES), lambda p, i: (i, p))
    kvblk = pl.BlockSpec((S, LANES), lambda p, i: (0, p))
    return pl.pallas_call(
        body, name="sb_fwd", grid=(npair, S // tq),
        in_specs=[qblk, kvblk, kvblk, pl.BlockSpec((tq, LANES), lambda p, i: (i, C_SG // LANES + p))],
        out_specs=[qblk, qblk],
        out_shape=[jax.ShapeDtypeStruct((S, SB_W), F32), jax.ShapeDtypeStruct((S, SB_W), BF16)],
        compiler_params=_cparams(dimension_semantics=("arbitrary", "arbitrary")),
    )(sq, sk, sv, projm)


def _outproj(x, fm, pm, sm, w_out, layer, *, tm):
    S, D = x.shape

    def body(x_ref, fm_ref, pm_ref, sm_ref, w_ref, y_ref):
        y = x_ref[...] + _dot(fm_ref[...], w_ref[0:FOX_W, :])
        y = y + _dot(pm_ref[...], w_ref[FOX_W:FOX_W + POOL_W, :])
        y_ref[...] = y + _dot(sm_ref[...], w_ref[FOX_W + POOL_W:D_MIX, :])

    row = lambda w: pl.BlockSpec((tm, w), lambda i: (i, 0))
    return pl.pallas_call(
        body, name="outproj", grid=(S // tm,),
        in_specs=[row(D), row(FOX_W), row(POOL_W), row(SB_W), pl.BlockSpec((None, D_MIX, D), lambda i: (layer, 0, 0))],
        out_specs=row(D), out_shape=jax.ShapeDtypeStruct((S, D), F32),
        compiler_params=_cparams(dimension_semantics=("arbitrary",)),
    )(x, fm, pm, sm, w_out)


def _loss_head(y, target, *, tm):
    S, D = y.shape

    def body(y_ref, t_ref, dy_ref, sq_ref):
        @pl.when(pl.program_id(0) == 0)
        def _():
            sq_ref[...] = jnp.zeros_like(sq_ref)

        d = y_ref[...] - t_ref[...]
        dy_ref[...] = d * (1.0 / D)
        sq_ref[...] += jnp.sum(d * d, axis=0, keepdims=True)

    row = pl.BlockSpec((tm, D), lambda i: (i, 0))
    return pl.pallas_call(
        body, name="loss_head", grid=(S // tm,),
        in_specs=[row, row], out_specs=[row, pl.BlockSpec((1, D), lambda i: (0, 0))],
        out_shape=[jax.ShapeDtypeStruct((S, D), F32), jax.ShapeDtypeStruct((1, D), F32)],
        compiler_params=_cparams(dimension_semantics=("arbitrary",)),
    )(y, target)


def _outproj_bwd(dy, fm, pm, sm, w_out, layer, stacks, *, tm):
    S, D = dy.shape

    def body(dy_ref, fm_ref, pm_ref, sm_ref, w_ref, dm_ref, dw_ref):
        @pl.when(pl.program_id(0) == 0)
        def _():
            dw_ref[...] = jnp.zeros_like(dw_ref)

        dyb = dy_ref[...].astype(BF16)
        dm_ref[...] = _dot_nt(dyb, w_ref[...])
        dw_ref[0:FOX_W, :] += _dot_tn(fm_ref[...], dyb)
        dw_ref[FOX_W:FOX_W + POOL_W, :] += _dot_tn(pm_ref[...], dyb)
        dw_ref[FOX_W + POOL_W:D_MIX, :] += _dot_tn(sm_ref[...], dyb)

    row = lambda w: pl.BlockSpec((tm, w), lambda i: (i, 0))
    wspec = pl.BlockSpec((None, D_MIX, D), lambda i: (layer, 0, 0))
    return _stack_call(
        body, "outproj_bwd", (S // tm,), [row(D), row(FOX_W), row(POOL_W), row(SB_W), wspec], (dy, fm, pm, sm, w_out),
        [pl.BlockSpec((None, D_MIX, D), lambda i: (layer, 0, 0))], [(D_MIX, D)], stacks,
        plain_specs=[row(D_MIX)], plain_shapes=[jax.ShapeDtypeStruct((S, D_MIX), F32)],
        compiler_params=_cparams(dimension_semantics=("arbitrary",)))


def _fox_bwd(qn, ka, kb, v, o, lse, dmix, projm, qkb, *, tq, tk):
    S = qn.shape[0]
    npair = FOX_HEADS // 2

    def body(q_ref, ka_ref, kb_ref, v_ref, o_ref, lse_ref, dm_ref, fg_ref, qkb_ref,
             dq_ref, dk_ref, dv_ref, dfg_ref, dct_ref, dcr_ref):
        qi = pl.program_id(1)

        @pl.when(qi == 0)
        def _():
            dk_ref[...] = jnp.zeros_like(dk_ref)
            dv_ref[...] = jnp.zeros_like(dv_ref)
            dct_ref[...] = jnp.zeros_like(dct_ref)

        lane = _iota((1, LANES), 1)
        ma = lane < HEAD_DIM
        qh = _pair_masks(q_ref[...])
        qaug = _aug_queries(q_ref[...])
        k_refs = (ka_ref, kb_ref)
        lsev = lse_ref[...]
        lse = (_lane_pick(lsev, lane, 0), _lane_pick(lsev, lane, HEAD_DIM))
        fg = fg_ref[...]
        silu, dsilu = _silu_pair(fg)
        dm = dm_ref[...]
        ov = o_ref[...]
        do = dm * silu
        dfg_ref[...] = dm * ov * dsilu
        dd = do * ov
        dsum = (jnp.sum(jnp.where(ma, dd, 0.0), axis=1, keepdims=True), jnp.sum(jnp.where(ma, 0.0, dd), axis=1, keepdims=True))
        doh = _pair_masks(do.astype(BF16))

        def block(k0, tkl, r0, carry, masked, heads=(0, 1)):
            vb = v_ref[pl.ds(k0, tkl), :]
            if masked:
                mask = (k0 + _iota((tq - r0, tkl), 1)) <= (qi * tq + r0 + _iota((tq - r0, tkl), 0))
            kaugs = {h: k_refs[h][pl.ds(k0, tkl), :] for h in heads}
            scores = {h: _dot_nt(qaug[h][r0:], kaugs[h]) for h in heads}
            dps = {h: _dot_nt(doh[h][r0:], vb) for h in heads}
            ps, dss = [], []
            rows = [carry[1], carry[2]]
            for h in heads:
                s = jnp.where(mask, scores[h], NEG) if masked else scores[h]
                p = jnp.exp(s - lse[h][r0:])
                dsf = p * (dps[h] - dsum[h][r0:])
                dct_ref[0, h:h + 1, pl.ds(k0, tkl)] -= jnp.sum(dsf, axis=0, keepdims=True)
                rows[h] = _put_rows(carry[1 + h], carry[1 + h][r0:] + jnp.sum(dsf, axis=1, keepdims=True), r0)
                ps.append(p.astype(BF16))
                dss.append(dsf.astype(BF16))
            dv_ref[pl.ds(k0, tkl), :] += _dot_tn(jnp.concatenate(ps, axis=0), jnp.concatenate([doh[h][r0:] for h in heads], axis=0))
            dk_ref[pl.ds(k0, tkl), :] += _dot_tn(jnp.concatenate(dss, axis=0), jnp.concatenate([qh[h][r0:] for h in heads], axis=0))
            kh = jnp.concatenate([_pair_masks(kaugs[h])[h] for h in heads], axis=0)
            dq = _put_rows(carry[0], carry[0][r0:] + _dot(jnp.concatenate(dss, axis=1), kh), r0)
            return (dq, rows[0], rows[1])

        zcol = jnp.zeros((tq, 1), F32)
        carry = (jnp.zeros((tq, LANES), F32), zcol, zcol)
        for off, size in _diag_tiles(tq):
            carry = block(pl.multiple_of(qi * tq + off, size), size, off, carry, True)
        floors = (jnp.min(lse[0]), jnp.min(lse[1]))
        dq, rowa, rowb = _fox_walk_left((qi * tq) // tk, tk, block, carry, k_refs, jnp.max(qkb_ref[...]), lambda c: floors)
        dq_ref[...] = dq * QK_SCALE
        dcr_ref[0] = jnp.where(ma, rowa, rowb)

    qblk = pl.BlockSpec((tq, LANES), lambda p, i: (i, p))
    kvblk = pl.BlockSpec((S, LANES), lambda p, i: (0, p))
    f32out = jax.ShapeDtypeStruct((S, FOX_W), F32)
    ctblk = pl.BlockSpec((1, FF_STRIDE, S), lambda p, i: (p, 0, 0))
    return pl.pallas_call(
        body, name="fox_bwd", grid=(npair, S // tq),
        in_specs=[qblk, kvblk, kvblk, kvblk, qblk, qblk, qblk,
                  pl.BlockSpec((tq, LANES), lambda p, i: (i, C_FG // LANES + p)),
                  pl.BlockSpec((1, LANES), lambda p, i: (0, 0))],
        out_specs=[qblk, kvblk, kvblk, qblk, ctblk, pl.BlockSpec((1, tq, LANES), lambda p, i: (p, i, 0))],
        out_shape=[f32out, f32out, f32out, f32out, jax.ShapeDtypeStruct((npair, FF_STRIDE, S), F32),
                   jax.ShapeDtypeStruct((npair, S, LANES), F32)],
        compiler_params=_cparams(dimension_semantics=("arbitrary", "arbitrary")),
    )(qn, ka, kb, v, o, lse, dmix, projm, qkb)


def _sb_bwd(sq, sk, sv, o, dmix, projm, *, tq, tk):
    S = sq.shape[0]
    npair = SB_HEADS // 2
    mix0 = (FOX_W + POOL_W) // LANES

    def body(q_ref, k_ref, v_ref, o_ref, dm_ref, sg_ref, dq_ref, dk_ref, dv_ref, dsg_ref):
        qi = pl.program_id(1)

        @pl.when(qi == 0)
        def _():
            dk_ref[...] = jnp.zeros_like(dk_ref)
            dv_ref[...] = jnp.zeros_like(dv_ref)

        lane = _iota((1, LANES), 1)
        ma = lane < HEAD_DIM
        qh = _pair_masks(q_ref[...])
        sg = sg_ref[...]
        silu, dsilu = _silu_pair(sg)
        dm = dm_ref[...]
        ov = o_ref[...]
        do = dm * silu
        dsg_ref[...] = dm * ov * dsilu
        dob = do.astype(BF16)
        dd = dob.astype(F32) * ov
        dsum = (jnp.sum(jnp.where(ma, dd, 0.0), axis=1, keepdims=True), jnp.sum(jnp.where(ma, 0.0, dd), axis=1, keepdims=True))
        doh = _pair_masks(dob)
        tmat2 = _suffix_matrix(tk, inclusive=False)
        tmat2_inc = _suffix_matrix(tk, inclusive=True)
        nfull = (qi * tq) // tk

        def block(k0, r0, carry, masked):
            nr = tq - r0
            kb = k_ref[pl.ds(k0, tk), :]
            vb = v_ref[pl.ds(k0, tk), :]
            kh = _pair_masks(kb)
            causal = (k0 + _iota((nr, tk), 1)) < (qi * tq + r0 + _iota((nr, tk), 0)) if masked else None
            heads = range(2)
            qs = [q[r0:] for q in qh]
            dos = [d[r0:] for d in doh]
            das = [_dot_nt(dos[h], vb) for h in heads]
            zs, nsps, lbs, a_s = _sb_scores(qs, kb, causal, tmat2, [carry[h][0][r0:] for h in heads])
            abs_ = [a.astype(BF16) for a in a_s]
            us = [abs_[h].astype(F32) * das[h] for h in heads]
            uins = [_suffix_sums(u, tmat2_inc) for u in us]
            dzs = []
            for h in heads:
                cum_u = dsum[h][r0:] - (uins[h] + carry[h][1][r0:])
                dz = us[h] * jnp.exp(nsps[h]) - jnp.exp(zs[h] + nsps[h]) * cum_u
                if masked:
                    dz = jnp.where(causal, dz, 0.0)
                dzs.append(dz.astype(BF16))
            dv_ref[pl.ds(k0, tk), :] += _dot_tn(jnp.concatenate(abs_, axis=0), jnp.concatenate(dos, axis=0))
            dk_ref[pl.ds(k0, tk), :] += _dot_tn(jnp.concatenate(dzs, axis=0), jnp.concatenate(qs, axis=0))
            dq = _put_rows(carry[2], carry[2][r0:] + _dot(jnp.concatenate(dzs, axis=1), jnp.concatenate(kh, axis=0)), r0)
            new = [(_put_rows(carry[h][0], carry[h][0][r0:] + jnp.sum(lbs[h], axis=1, keepdims=True), r0),
                    _put_rows(carry[h][1], carry[h][1][r0:] + jnp.sum(us[h], axis=1, keepdims=True), r0)) for h in heads]
            return (new[0], new[1], dq)

        zcol = jnp.zeros((tq, 1), F32)
        carry = ((zcol, zcol), (zcol, zcol), jnp.zeros((tq, LANES), F32))
        for off, size in reversed(_diag_tiles(tq)):
            assert size == tk
            carry = block(pl.multiple_of(qi * tq + off, tk), off, carry, True)
        dq = _sb_walk_left(nfull, tk, block, carry, lambda c: (c[0][0], c[1][0]))[2]
        dq_ref[...] = dq * QK_SCALE

    qblk = pl.BlockSpec((tq, LANES), lambda p, i: (i, p))
    kvblk = pl.BlockSpec((S, LANES), lambda p, i: (0, p))
    f32out = jax.ShapeDtypeStruct((S, SB_W), F32)
    return pl.pallas_call(
        body, name="sb_bwd", grid=(npair, S // tq),
        in_specs=[qblk, kvblk, kvblk, qblk,
                  pl.BlockSpec((tq, LANES), lambda p, i: (i, mix0 + p)),
                  pl.BlockSpec((tq, LANES), lambda p, i: (i, C_SG // LANES + p))],
        out_specs=[qblk, kvblk, kvblk, qblk],
        out_shape=[f32out, f32out, f32out, f32out],
        compiler_params=_cparams(dimension_semantics=("arbitrary", "arbitrary")),
    )(sq, sk, sv, o, dmix, projm)


def _prep_bwd(projm, ffo, dqn, dkn, dct, dcr, dv, dfg, dsq, dsk, dsv, dsg, dmix, pooled, yp, qg, kg, bfp, wpd, ps, *, ts):
    S = projm.shape[0]
    nb = S // ts
    hb = ts // POOL_HALO
    npair = FOX_HEADS // 2
    last_halo = S // POOL_HALO - 1

    def body(fq_ref, fk_ref, pp_ref, pph_ref, ff_ref,
             dqn_ref, dkn_ref, dct_ref, dcr_ref, dv_ref, dfg_ref, dsq_ref, dsk_ref, dsv_ref, dsg_ref,
             dmp_ref, dmh_ref, pooled_ref, yp_ref, qg_ref, kg_ref, bf_ref, wpd_ref, ps_ref,
             dp_ref, dqg_ref, dkg_ref, dbf_ref, dwp_ref, dps_ref,
             carry_ref, dl_ref, buf_ref, dct_s):
        i = pl.program_id(0)
        blk = nb - 1 - i

        @pl.when(i == 0)
        def _():
            carry_ref[...] = jnp.zeros_like(carry_ref)
            dqg_ref[...] = jnp.zeros_like(dqg_ref)
            dkg_ref[...] = jnp.zeros_like(dkg_ref)
            dbf_ref[...] = jnp.zeros_like(dbf_ref)
            dwp_ref[...] = jnp.zeros_like(dwp_ref)
            dps_ref[...] = jnp.zeros_like(dps_ref)

        bd = _head_blockdiag()
        for raw_ref, g_ref, dn, dg_ref, col in ((fq_ref, qg_ref, dqn_ref[...], dqg_ref, C_FQ), (fk_ref, kg_ref, dkn_ref[...], dkg_ref, C_FK)):
            q = raw_ref[...]
            rstd = lax.rsqrt(_group_sum(q * q, bd) * (1.0 / HEAD_DIM) + EPS)
            xhat = q * rstd
            dg_ref[...] += jnp.sum(dn * xhat, axis=0, keepdims=True)
            dyg = dn * g_ref[...]
            mean = _group_sum(dyg * xhat, bd) * (1.0 / HEAD_DIM)
            dp_ref[:, col:col + FOX_W] = (rstd * (dyg - xhat * mean)).astype(BF16)
        dp_ref[:, C_FV:C_FV + FOX_W] = dv_ref[...].astype(BF16)
        dp_ref[:, C_FG:C_FG + FOX_W] = dfg_ref[...].astype(BF16)
        dp_ref[:, C_SQ:C_SQ + SB_W] = dsq_ref[...].astype(BF16)
        dp_ref[:, C_SK:C_SK + SB_W] = dsk_ref[...].astype(BF16)
        dp_ref[:, C_SV:C_SV + SB_W] = dsv_ref[...].astype(BF16)
        dp_ref[:, C_SG:C_SG + SB_W] = dsg_ref[...].astype(BF16)

        dct_s[...] = jnp.zeros_like(dct_s)
        for p in range(npair):
            dct_s[FF_STRIDE * p:FF_STRIDE * (p + 1), :] = dct_ref[p]
        dc = dct_s[...].T
        lane = _iota((1, LANES), 1)
        for p in range(npair):
            dcr = dcr_ref[p]
            dc = dc + jnp.where(lane == FF_STRIDE * p, _lane_pick(dcr, lane, 0), 0.0)
            dc = dc + jnp.where(lane == FF_STRIDE * p + 1, _lane_pick(dcr, lane, HEAD_DIM), 0.0)
        triu = _ones_where(_iota((ts, ts), 1) >= _iota((ts, ts), 0))
        dlf = _dot_exact_lhs(triu, dc) + carry_ref[...]
        dl_ref[...] = dlf
        carry_ref[...] = dl_ref[0:1, :]
        z = ff_ref[...] + bf_ref[...]
        dff = dlf * (1.0 / (1.0 + jnp.exp(z)))
        dbf_ref[...] += jnp.sum(dff, axis=0, keepdims=True)
        dp_ref[:, PM:PW] = dff.astype(BF16)

        psv = ps_ref[...]
        wpdv = wpd_ref[...]
        lane_group = _iota((1, POOL_W), 1) >> HEAD_SHIFT
        wlen = _pool_group_select(lane_group, [float(w) for w in POOL_WINDOWS])
        pg = pp_ref[:, POOL_W:2 * POOL_W]
        silu, dsilu = _silu_pair(pg)
        dmp = dmp_ref[...]
        ypv = yp_ref[...]
        dp_ref[:, C_PG:C_PG + POOL_W] = (dmp * (ypv * psv) * dsilu).astype(BF16)
        dps_ref[...] += jnp.sum(dmp * silu * ypv, axis=0, keepdims=True)
        dyp = (dmp * psv * silu).astype(BF16)
        dwp_ref[...] += _dot_tn(pooled_ref[...], dyp)
        dpooled = _dot_nt(dyp, wpdv)
        pgh = pph_ref[:, POOL_W:2 * POOL_W]
        dyph = (dmh_ref[...] * psv * (pgh * _sigmoid(pgh))).astype(BF16)
        dpooled_h = jnp.where(blk < nb - 1, _dot_nt(dyph, wpdv), 0.0)
        tpos = (blk * ts + _iota((ts, 1), 0) + 1).astype(F32)
        ev = dpooled / jnp.minimum(tpos, wlen)
        buf_ref[0:ts, :] = ev
        buf_ref[ts:ts + POOL_HALO, :] = dpooled_h / wlen
        acc = ev
        snaps = []
        for d in range(1, POOL_HALO):
            acc = acc + buf_ref[pl.ds(d, ts), :]
            if d + 1 in POOL_WINDOWS:
                snaps.append(acc)
        dp_ref[:, C_PX:C_PX + POOL_W] = (_pool_group_select(lane_group, snaps) - dpooled).astype(BF16)

    rblk = lambda w, c: pl.BlockSpec((ts, w), lambda i: (nb - 1 - i, c))
    full = lambda a: pl.BlockSpec(a.shape, lambda i: (0,) * a.ndim)
    halo = lambda w, c: pl.BlockSpec((POOL_HALO, w), lambda i: (jnp.minimum((nb - i) * hb, last_halo), c))
    acc_spec = lambda r, w: pl.BlockSpec((r, w), lambda i: (0, 0))
    return pl.pallas_call(
        body, name="prep_bwd", grid=(nb,),
        in_specs=[rblk(FOX_W, C_FQ // FOX_W), rblk(FOX_W, C_FK // FOX_W), rblk(2 * POOL_W, C_PX // (2 * POOL_W)),
                  halo(2 * POOL_W, C_PX // (2 * POOL_W)), rblk(LANES, 0),
                  rblk(FOX_W, 0), rblk(FOX_W, 0), pl.BlockSpec((npair, FF_STRIDE, ts), lambda i: (0, 0, nb - 1 - i)),
                  pl.BlockSpec((npair, ts, LANES), lambda i: (0, nb - 1 - i, 0)), rblk(FOX_W, 0), rblk(FOX_W, 0),
                  rblk(SB_W, 0), rblk(SB_W, 0), rblk(SB_W, 0), rblk(SB_W, 0),
                  rblk(POOL_W, FOX_W // POOL_W), halo(POOL_W, FOX_W // POOL_W), rblk(POOL_W, 0), rblk(POOL_W, 0),
                  full(qg), full(kg), full(bfp), full(wpd), full(ps)],
        out_specs=[rblk(PW, 0), acc_spec(1, FOX_W), acc_spec(1, FOX_W), acc_spec(1, LANES), acc_spec(POOL_W, POOL_W), acc_spec(1, POOL_W)],
        out_shape=[jax.ShapeDtypeStruct((S, PW), BF16), jax.ShapeDtypeStruct((1, FOX_W), F32), jax.ShapeDtypeStruct((1, FOX_W), F32),
                   jax.ShapeDtypeStruct((1, LANES), F32), jax.ShapeDtypeStruct((POOL_W, POOL_W), F32), jax.ShapeDtypeStruct((1, POOL_W), F32)],
        scratch_shapes=[pltpu.VMEM((1, LANES), F32), pltpu.VMEM((ts, LANES), F32), pltpu.VMEM((ts + POOL_HALO, POOL_W), F32),
                        pltpu.VMEM((LANES, ts), F32)],
        compiler_params=_cparams(dimension_semantics=("arbitrary",)),
    )(projm, projm, projm, projm, ffo, dqn, dkn, dct, dcr, dv, dfg, dsq, dsk, dsv, dsg, dmix, dmix, pooled, yp, qg, kg, bfp, wpd, ps)


def _stack_call(body, name, grid, in_specs, operands, slot_specs, slot_shapes, stacks, plain_specs=(), plain_shapes=(), **kw):
    out_specs = list(plain_specs) + list(slot_specs)
    out_shape = list(plain_shapes) + [jax.ShapeDtypeStruct((DEPTH,) + s, F32) for s in slot_shapes]
    if stacks is None:
        return pl.pallas_call(body, name=name, grid=grid, in_specs=in_specs, out_specs=out_specs, out_shape=out_shape, **kw)(*operands)
    n = len(operands)

    def aliased_body(*refs):
        body(*refs[:n], *refs[n + len(stacks):])

    return pl.pallas_call(
        aliased_body, name=name, grid=grid, in_specs=list(in_specs) + [pl.BlockSpec(memory_space=pl.ANY)] * len(stacks),
        out_specs=out_specs, out_shape=out_shape,
        input_output_aliases={n + k: len(plain_specs) + k for k in range(len(stacks))}, **kw)(*operands, *stacks)


def _inproj_dw(h, dproj, layer, stacks, *, ts, tn):
    S, D = h.shape
    nj = PM // tn

    def body(h_ref, dp_ref, dpf_ref, dw_ref, dwf_ref):
        s = pl.program_id(1)

        @pl.when(s == 0)
        def _():
            dw_ref[...] = jnp.zeros_like(dw_ref)

        @pl.when((s == 0) & (pl.program_id(0) == 0))
        def _():
            dwf_ref[...] = jnp.zeros_like(dwf_ref)

        hv = h_ref[...]
        dw_ref[...] += _dot_tn(dp_ref[...], hv)

        @pl.when(pl.program_id(0) == 0)
        def _():
            dwf_ref[...] += _dot_tn(dpf_ref[...], hv)

    return _stack_call(
        body, "inproj_dw", (nj, S // ts),
        [pl.BlockSpec((ts, D), lambda j, s: (s, 0)),
         pl.BlockSpec((ts, tn), lambda j, s: (s, j)),
         pl.BlockSpec((ts, LANES), lambda j, s: (s, PM // LANES))],
        (h, dproj, dproj),
        [pl.BlockSpec((None, tn, D), lambda j, s: (layer, j, 0)), pl.BlockSpec((None, LANES, D), lambda j, s: (layer, 0, 0))],
        [(PM, D), (LANES, D)], stacks,
        compiler_params=_cparams(dimension_semantics=("arbitrary", "arbitrary")))


def _inproj_dx(dproj, wt_all, layer, x, g, dy, *, tm):
    S, D = x.shape

    def body(dp_ref, w_ref, x_ref, g_ref, dy_ref, dx_ref, dg_ref):
        @pl.when(pl.program_id(0) == 0)
        def _():
            dg_ref[...] = jnp.zeros_like(dg_ref)

        dh = _dot(dp_ref[...], w_ref[...])
        xf = x_ref[...]
        rstd = lax.rsqrt(jnp.mean(xf * xf, axis=-1, keepdims=True) + EPS)
        xhat = xf * rstd
        dg_ref[...] += jnp.sum(dh * xhat, axis=0, keepdims=True)
        dyg = dh * g_ref[...]
        mean = jnp.mean(dyg * xhat, axis=-1, keepdims=True)
        dx_ref[...] = rstd * (dyg - xhat * mean) + dy_ref[...]

    row = lambda w: pl.BlockSpec((tm, w), lambda i: (i, 0))
    return pl.pallas_call(
        body, name="inproj_dx", grid=(S // tm,),
        in_specs=[row(PW), pl.BlockSpec((None, PW, D), lambda i: (layer, 0, 0)), row(D), pl.BlockSpec((1, D), lambda i: (0, 0)), row(D)],
        out_specs=[row(D), pl.BlockSpec((1, D), lambda i: (0, 0))],
        out_shape=[jax.ShapeDtypeStruct((S, D), F32), jax.ShapeDtypeStruct((1, D), F32)],
        compiler_params=_cparams(dimension_semantics=("arbitrary",)),
    )(dproj, wt_all, x, g, dy)


def _adam_update(w, g, m, v):
    nm = ADAM_B1 * m + (1.0 - ADAM_B1) * g
    nv = ADAM_B2 * v + (1.0 - ADAM_B2) * (g * g)
    m_hat = nm / (1.0 - ADAM_B1 ** ADAM_STEP)
    v_hat = nv / (1.0 - ADAM_B2 ** ADAM_STEP)
    return -ADAM_LR * (m_hat / (jnp.sqrt(v_hat) + ADAM_EPS) + ADAM_WD * w), nm, nv


def _adamw(w, g, m, v):
    L, R, C = w.shape
    tr = R if R <= 512 else 256

    def body(w_ref, g_ref, m_ref, v_ref, d_ref, nm_ref, nv_ref):
        d_ref[...], nm_ref[...], nv_ref[...] = _adam_update(w_ref[...], g_ref[...], m_ref[...], v_ref[...])

    spec = pl.BlockSpec((1, tr, C), lambda l, i: (l, i, 0))
    shp = jax.ShapeDtypeStruct((L, R, C), F32)
    return pl.pallas_call(
        body, name="adamw", grid=(L, R // tr), in_specs=[spec] * 4, out_specs=[spec] * 3, out_shape=[shp] * 3,
        compiler_params=_cparams(dimension_semantics=("arbitrary", "arbitrary")),
    )(w, g, m, v)


def _adamw_nd(w, g, m, v):
    shape = w.shape
    view = (1,) + shape if w.ndim == 2 else (shape[0], -1, shape[-1])
    outs = _adamw(w.reshape(view), g.reshape(view), m.reshape(view), v.reshape(view))
    return tuple(o.reshape(shape) for o in outs)


FLIP_C = (0, 0, 1)
FLIP_X = (1, 0, 0)
FLIP_Y = (0, 1, 0)
FLIP_XY = (1, 1, 0)
MESH = pl.DeviceIdType.MESH


def _peer(flip):
    me = (lax.axis_index("x"), lax.axis_index("y"), lax.axis_index("c"))
    return tuple(1 - a if f else a for a, f in zip(me, flip))


def _exchange(name, arrays, flips):
    n = len(arrays)

    def body(*refs):
        srcs, dsts = refs[:n], refs[n:2 * n]
        send_sems, recv_sems = refs[2 * n:]
        copies = [pltpu.make_async_remote_copy(src_ref=srcs[k], dst_ref=dsts[k], send_sem=send_sems.at[k], recv_sem=recv_sems.at[k],
                                               device_id=_peer(flips[k]), device_id_type=MESH) for k in range(n)]
        for cp in copies:
            cp.start()
        for cp in copies:
            cp.wait()

    anyspec = pl.BlockSpec(memory_space=pl.ANY)
    return pl.pallas_call(
        body, name=name, in_specs=[anyspec] * n, out_specs=[anyspec] * n,
        out_shape=[jax.ShapeDtypeStruct(a.shape, a.dtype) for a in arrays],
        scratch_shapes=[pltpu.SemaphoreType.DMA((n,)), pltpu.SemaphoreType.DMA((n,))],
    )(*arrays)


def _exchange_add(name, x, flip):
    def body(x_ref, o_ref, buf_ref, send_sem, recv_sem):
        cp = pltpu.make_async_remote_copy(src_ref=x_ref, dst_ref=buf_ref, send_sem=send_sem, recv_sem=recv_sem,
                                          device_id=_peer(flip), device_id_type=MESH)
        cp.start()
        cp.wait()
        o_ref[...] = x_ref[...] + buf_ref[...]

    vspec = pl.BlockSpec(memory_space=pltpu.VMEM)
    return pl.pallas_call(
        body, name=name, in_specs=[vspec], out_specs=vspec, out_shape=jax.ShapeDtypeStruct(x.shape, x.dtype),
        scratch_shapes=[pltpu.VMEM(x.shape, x.dtype), pltpu.SemaphoreType.DMA, pltpu.SemaphoreType.DMA],
    )(x)


def _chip_index():
    return 2 * lax.axis_index("x") + lax.axis_index("y")


def _gather_weights(w_in_t, w_out):
    wi = w_in_t.astype(BF16)
    wo = jnp.swapaxes(w_out, 0, 1).astype(BF16)
    halves = (wi.shape[0] // 2, wo.shape[0] // 2)
    ARR = 2
    TO_X, TO_Y, ON_Y, ON_X, SIB_X, SIB_Y, SIB_D0, SIB_D1, OWN = [ARR * k for k in range(9)]
    n_sems = ARR * 9

    def body(wi_ref, wo_ref, gi_ref, go_ref, send_sems, recv_sems):
        c = lax.axis_index("c")
        j = _chip_index()
        srcs = (wi_ref, wo_ref)
        dsts = (gi_ref, go_ref)
        def cuts(core):
            return [(pl.ds(h * core, h), pl.ds(h * core, h // 2), pl.ds(h * core + h // 2, h - h // 2)) for h in halves]
        mine, theirs = cuts(c), cuts(1 - c)
        HALF, Q0, Q1 = 0, 1, 2

        def copy(idx, src, dst, flip):
            return pltpu.make_async_remote_copy(src_ref=src, dst_ref=dst, send_sem=send_sems.at[idx], recv_sem=recv_sems.at[idx],
                                                device_id=_peer(flip), device_id_type=MESH)

        def slot(a, shard, cut):
            return dsts[a].at[shard, cut]

        jx, jy, jd = j ^ 2, j ^ 1, j ^ 3
        sends = []

        def start(cp):
            cp.start()
            sends.append(cp)

        for a in range(ARR):
            start(copy(TO_X + a, srcs[a].at[mine[a][HALF]], slot(a, j, mine[a][HALF]), FLIP_X))
            start(copy(TO_Y + a, srcs[a].at[mine[a][HALF]], slot(a, j, mine[a][HALF]), FLIP_Y))
        own = [copy(OWN + a, srcs[a], dsts[a].at[j], FLIP_C) for a in range(ARR)]
        for cp in own:
            cp.start()
        for a in range(ARR):
            copy(TO_X + a, slot(a, jx, mine[a][HALF]), slot(a, jx, mine[a][HALF]), FLIP_X).wait_recv()
            start(copy(ON_Y + a, slot(a, jx, mine[a][Q0]), slot(a, jx, mine[a][Q0]), FLIP_Y))
            start(copy(SIB_X + a, slot(a, jx, mine[a][HALF]), slot(a, jx, mine[a][HALF]), FLIP_C))
        for a in range(ARR):
            copy(TO_Y + a, slot(a, jy, mine[a][HALF]), slot(a, jy, mine[a][HALF]), FLIP_Y).wait_recv()
            start(copy(ON_X + a, slot(a, jy, mine[a][Q1]), slot(a, jy, mine[a][Q1]), FLIP_X))
            start(copy(SIB_Y + a, slot(a, jy, mine[a][HALF]), slot(a, jy, mine[a][HALF]), FLIP_C))
        for a in range(ARR):
            copy(ON_Y + a, slot(a, jd, mine[a][Q0]), slot(a, jd, mine[a][Q0]), FLIP_Y).wait_recv()
            start(copy(SIB_D0 + a, slot(a, jd, mine[a][Q0]), slot(a, jd, mine[a][Q0]), FLIP_C))
        for a in range(ARR):
            copy(ON_X + a, slot(a, jd, mine[a][Q1]), slot(a, jd, mine[a][Q1]), FLIP_X).wait_recv()
            start(copy(SIB_D1 + a, slot(a, jd, mine[a][Q1]), slot(a, jd, mine[a][Q1]), FLIP_C))
        for a in range(ARR):
            for idx, shard, cut in ((SIB_X, jx, HALF), (SIB_Y, jy, HALF), (SIB_D0, jd, Q0), (SIB_D1, jd, Q1)):
                copy(idx + a, slot(a, shard, theirs[a][cut]), slot(a, shard, theirs[a][cut]), FLIP_C).wait_recv()
        for cp in own:
            cp.wait()
        for cp in sends:
            cp.wait_send()

    anyspec = pl.BlockSpec(memory_space=pl.ANY)
    gi, go = pl.pallas_call(
        body, name="gather_weights", in_specs=[anyspec] * 2, out_specs=[anyspec] * 2,
        out_shape=[jax.ShapeDtypeStruct((4,) + wi.shape, BF16), jax.ShapeDtypeStruct((4,) + wo.shape, BF16)],
        scratch_shapes=[pltpu.SemaphoreType.DMA((n_sems,)), pltpu.SemaphoreType.DMA((n_sems,))],
    )(wi, wo)
    w_in_t_full = gi.reshape((4 * wi.shape[0],) + wi.shape[1:])
    w_out_full = jnp.swapaxes(go.reshape((4 * wo.shape[0],) + wo.shape[1:]), 0, 1)
    return w_in_t_full, w_out_full


def _to_aligned(w_t):
    _, L, D = w_t.shape
    npair = FOX_HEADS // 2
    ff = w_t[ORIG_FF:ORIG_REST].reshape(npair, 2, L, D)
    ff = jnp.pad(ff, ((0, 0), (0, FF_STRIDE - 2), (0, 0), (0, 0))).reshape(npair * FF_STRIDE, L, D)
    ff = jnp.pad(ff, ((0, LANES - npair * FF_STRIDE), (0, 0), (0, 0)))
    return jnp.swapaxes(jnp.concatenate([w_t[:ORIG_FOX], w_t[ORIG_REST:], ff], axis=0), 0, 1)


def _from_aligned(dw_t):
    n, _, D = dw_t.shape
    npair = FOX_HEADS // 2
    ff = dw_t[:, PM:PM + npair * FF_STRIDE].reshape(n, npair, FF_STRIDE, D)[:, :, :2].reshape(n, FOX_HEADS, D)
    return jnp.swapaxes(jnp.concatenate([dw_t[:, :ORIG_FOX], ff, dw_t[:, ORIG_FOX:PM]], axis=1), 0, 1)


RELAY_ROWS = 256


def _rows_first(m, f):
    n, _, D = m.shape
    npair = FOX_HEADS // 2
    first_late = ORIG_FOX // RELAY_ROWS

    def body(m_ref, f_ref, out_ref, buf_ref, ff_ref, sem, ff_sem):
        i = pl.program_id(0)
        for l in range(n):
            buf_ref[:, l, :] = m_ref[l]
        start = pl.multiple_of(i * RELAY_ROWS, FOX_HEADS) + jnp.where(i >= first_late, FOX_HEADS, 0)
        main = pltpu.make_async_copy(buf_ref, out_ref.at[pl.ds(start, RELAY_ROWS)], sem)
        main.start()

        @pl.when(i == 0)
        def _():
            for l in range(n):
                for p in range(npair):
                    ff_ref[2 * p:2 * p + 2, l, :] = f_ref[l, FF_STRIDE * p:FF_STRIDE * p + 2, :]
            ff = pltpu.make_async_copy(ff_ref, out_ref.at[pl.ds(ORIG_FF, FOX_HEADS)], ff_sem)
            ff.start()
            ff.wait()

        main.wait()

    return pl.pallas_call(
        body, name="rs_rows_first", grid=(PM // RELAY_ROWS,),
        in_specs=[pl.BlockSpec((n, RELAY_ROWS, D), lambda i: (0, i, 0)), pl.BlockSpec((n, LANES, D), lambda i: (0, 0, 0))],
        out_specs=pl.BlockSpec(memory_space=pl.ANY), out_shape=jax.ShapeDtypeStruct((D_IN, n, D), F32),
        scratch_shapes=[pltpu.VMEM((RELAY_ROWS, n, D), F32), pltpu.VMEM((FOX_HEADS, n, D), F32),
                        pltpu.SemaphoreType.DMA, pltpu.SemaphoreType.DMA],
        compiler_params=_cparams(dimension_semantics=("arbitrary",)),
    )(m, f)


def _half_layers(name, stack, got, also_bf16=True):
    L, R, C = stack.shape
    half = L // 2
    tr = min(256, R)
    c = lax.axis_index("c")
    which = ((1 - c) if got is None else c).astype(jnp.int32).reshape(1)

    def body(c_ref, x_ref, *refs):
        if got is None:
            refs[0][...] = x_ref[...].astype(BF16)
        else:
            acc = x_ref[...] + refs[0][...].astype(F32)
            refs[1][...] = acc
            if also_bf16:
                refs[2][...] = acc.astype(BF16)

    plain = pl.BlockSpec((1, tr, C), lambda l, i, c_ref: (l, i, 0))
    picked = pl.BlockSpec((1, tr, C), lambda l, i, c_ref: (c_ref[0] * half + l, i, 0))
    shp = lambda dt: jax.ShapeDtypeStruct((half, R, C), dt)
    out_shape = [shp(BF16)] if got is None else [shp(F32)] + ([shp(BF16)] if also_bf16 else [])
    grid_spec = pltpu.PrefetchScalarGridSpec(
        num_scalar_prefetch=1, grid=(half, R // tr),
        in_specs=[picked] + ([] if got is None else [plain]), out_specs=[plain] * len(out_shape))
    return pl.pallas_call(
        body, name=name, grid_spec=grid_spec, out_shape=out_shape,
        compiler_params=_cparams(dimension_semantics=("arbitrary", "arbitrary")),
    )(which, stack, *([] if got is None else [got]))


def _reduce_scatter(stack_m, stack_f, stack_o, shard_cols, shard_rows):
    j = _chip_index()
    half = DEPTH // 2
    stacks = (stack_m, stack_f, stack_o)
    give = [_half_layers("rs_give", s, None)[0] for s in stacks]
    got = _exchange("rs_d2d", give, (FLIP_C,) * len(stacks))
    (m32,), (f32_,) = [_half_layers("rs_add_chip", s, g, also_bf16=False) for s, g in zip(stacks[:2], got[:2])]
    o32, obf = _half_layers("rs_add_chip", stack_o, got[2])
    d_model = stack_m.shape[2]
    in32 = _rows_first(m32, f32_).reshape(4, shard_cols, half, d_model)

    def out_shards(o):
        return jnp.moveaxis(o.reshape(half, 4, shard_rows, o.shape[-1]), 1, 0)

    chip = [(in32, in32.astype(BF16), 0), (out_shards(o32), out_shards(obf), 1)]
    shard = lambda a, idx: lax.dynamic_index_in_dim(a, idx, axis=0, keepdims=False)
    via = []
    for _, bf, axis in chip:
        diag = shard(bf, j ^ 3)
        cut = diag.shape[axis] // 2
        via += [lax.slice_in_dim(diag, 0, cut, axis=axis), lax.slice_in_dim(diag, cut, 2 * cut, axis=axis)]
    handed = _exchange("rs_via", via, (FLIP_X, FLIP_Y) * len(chip))
    sends = []
    for a, (f32_sum, _, axis) in enumerate(chip):
        sends.append(_add_half_along("rs_add_via", f32_sum, handed[2 * a + 1], axis, 1, pick=j ^ 2))
        sends.append(_add_half_along("rs_add_via", f32_sum, handed[2 * a], axis, 0, pick=j ^ 1))
    got = _exchange("rs_ici", sends, (FLIP_X, FLIP_Y) * len(chip))
    mine_in = _add_rows("rs_add_in", chip[0][0], list(got[0:2]), pick=j)
    mine_out = _add_into_half("rs_add_out", shard(chip[1][0], j), list(got[2:4]))
    sib_in, g_out = _share_halves(mine_in, mine_out)
    return (mine_in, sib_in), g_out


def _picked(spec, pick):
    return pl.BlockSpec((None,) + tuple(spec.block_shape), lambda *a: (a[-1][0],) + tuple(spec.index_map(*a[:-1])))


def _add_half_along(name, base, extra, axis, which, pick=None):
    shape = base.shape if pick is None else base.shape[1:]
    lanes = min(ROW_LANE_CHUNK, shape[2])
    assert shape[axis] == 2 * extra.shape[axis]
    blk = tuple(shape[d] // 2 if d == axis else shape[d] for d in range(2)) + (lanes,)

    def body(*refs):
        b_ref, e_ref, o_ref = refs[-3:]
        x = b_ref[...]
        o_ref[...] = jnp.where(pl.program_id(0) == which, x + e_ref[...].astype(F32), x).astype(BF16)

    at = lambda i, k, *_: (i, 0, k) if axis == 0 else (0, i, k)
    bspec, espec = pl.BlockSpec(blk, at), pl.BlockSpec(blk, lambda i, k, *_: (0, 0, k))
    kw = dict(out_shape=jax.ShapeDtypeStruct(shape, BF16), name=name, compiler_params=_cparams(dimension_semantics=("arbitrary", "arbitrary")))
    grid = (2, shape[2] // lanes)
    if pick is None:
        return pl.pallas_call(body, grid=grid, in_specs=[bspec, espec], out_specs=bspec, **kw)(base, extra)
    grid_spec = pltpu.PrefetchScalarGridSpec(num_scalar_prefetch=1, grid=grid, in_specs=[_picked(bspec, pick), espec], out_specs=bspec)
    return pl.pallas_call(body, grid_spec=grid_spec, **kw)(pick.astype(jnp.int32).reshape(1), base, extra)


def _add_rows(name, first, others, pick=None):
    n = len(others)
    shape = first.shape if pick is None else first.shape[1:]

    def body(*refs):
        refs = refs[-(n + 2):]
        acc = refs[0][...]
        for r in refs[1:1 + n]:
            acc = acc + r[...].astype(F32)
        refs[1 + n][...] = acc

    grid, spec = _row_lane_blocks(shape)
    sp = spec(shape[1])
    kw = dict(out_shape=jax.ShapeDtypeStruct(shape, F32), name=name, compiler_params=_cparams(dimension_semantics=("arbitrary", "arbitrary")))
    if pick is None:
        return pl.pallas_call(body, grid=grid, in_specs=[sp] * (1 + n), out_specs=sp, **kw)(first, *others)
    grid_spec = pltpu.PrefetchScalarGridSpec(num_scalar_prefetch=1, grid=grid, in_specs=[_picked(sp, pick)] + [sp] * n, out_specs=sp)
    return pl.pallas_call(body, grid_spec=grid_spec, **kw)(pick.astype(jnp.int32).reshape(1), first, *others)


ROW_LANE_CHUNK = 256


def _row_lane_blocks(shape):
    rows, _, C = shape
    tr = rows // 2 if rows % 2 == 0 and rows > 64 else rows
    lanes = min(ROW_LANE_CHUNK, C)
    return (rows // tr, C // lanes), lambda n_mid: pl.BlockSpec((tr, n_mid, lanes), lambda i, k, *_: (i, 0, k))


def _add_into_half(name, first, others):
    half, rows, C = first.shape
    tr = min(256, rows)
    n = len(others)

    def body(c_ref, *refs):
        acc = refs[0][...]
        for r in refs[1:1 + n]:
            acc = acc + r[...].astype(F32)
        refs[1 + n][...] = acc

    grid_spec = pltpu.PrefetchScalarGridSpec(
        num_scalar_prefetch=1, grid=(half, rows // tr),
        in_specs=[pl.BlockSpec((1, tr, C), lambda l, i, c_ref: (l, i, 0))] * (1 + n),
        out_specs=pl.BlockSpec((1, tr, C), lambda l, i, c_ref: (c_ref[0] * half + l, i, 0)))
    return pl.pallas_call(
        body, name=name, grid_spec=grid_spec, out_shape=jax.ShapeDtypeStruct((2 * half, rows, C), F32),
        compiler_params=_cparams(dimension_semantics=("arbitrary", "arbitrary")),
    )(lax.axis_index("c").astype(jnp.int32).reshape(1), first, *others)


def _share_halves(mine, buf):
    half = DEPTH // 2

    def body(mine_ref, buf_in, sib_ref, buf_ref, send_sems, recv_sems):
        lay = pl.ds(half * lax.axis_index("c"), half)
        copies = [pltpu.make_async_remote_copy(src_ref=src, dst_ref=dst, send_sem=send_sems.at[k], recv_sem=recv_sems.at[k],
                                               device_id=_peer(FLIP_C), device_id_type=MESH)
                  for k, (src, dst) in enumerate(((mine_ref, sib_ref), (buf_ref.at[lay], buf_ref.at[lay])))]
        for cp in copies:
            cp.start()
        for cp in copies:
            cp.wait()

    anyspec = pl.BlockSpec(memory_space=pl.ANY)
    return pl.pallas_call(
        body, name="rs_share", in_specs=[anyspec] * 2, out_specs=[anyspec] * 2,
        out_shape=[jax.ShapeDtypeStruct(mine.shape, mine.dtype), jax.ShapeDtypeStruct(buf.shape, buf.dtype)],
        input_output_aliases={1: 1},
        scratch_shapes=[pltpu.SemaphoreType.DMA((2,)), pltpu.SemaphoreType.DMA((2,))],
    )(mine, buf)


def _adamw_halves(w, g_mine, g_sib, m, v):
    half = g_mine.shape[1]

    def body(c_ref, w_ref, gm_ref, gs_ref, m_ref, v_ref, g_ref, d_ref, nm_ref, nv_ref):
        first = c_ref[0] == 0
        gm, gs = gm_ref[...], gs_ref[...]
        for h, gv in enumerate((jnp.where(first, gm, gs), jnp.where(first, gs, gm))):
            lay = slice(half * h, half * (h + 1))
            g_ref[:, lay, :] = gv
            d_ref[:, lay, :], nm_ref[:, lay, :], nv_ref[:, lay, :] = _adam_update(w_ref[:, lay, :], gv, m_ref[:, lay, :], v_ref[:, lay, :])

    grid, spec = _row_lane_blocks(w.shape)
    full, part = spec(w.shape[1]), spec(half)
    grid_spec = pltpu.PrefetchScalarGridSpec(num_scalar_prefetch=1, grid=grid, in_specs=[full, part, part, full, full], out_specs=[full] * 4)
    return pl.pallas_call(
        body, name="adamw_halves", grid_spec=grid_spec, out_shape=[jax.ShapeDtypeStruct(w.shape, F32)] * 4,
        compiler_params=_cparams(dimension_semantics=("arbitrary", "arbitrary")),
    )(lax.axis_index("c").astype(jnp.int32).reshape(1), w, g_mine, g_sib, m, v)


def _all_reduce_small(x):
    x = _exchange_add("ar_c", x, FLIP_C)
    x = _exchange_add("ar_y", x, FLIP_Y)
    return _exchange_add("ar_x", x, FLIP_X)


def _blocks(S):
    return dict(tm=min(512, S), tm_proj=min(1024, S), ts=min(512, S), tq=min(512, S), tq_big=min(1024, S), tk=min(512, S), tks=min(256, S))


def _pair_pad(vec):
    npair = FOX_HEADS // 2
    v = jnp.pad(vec.reshape(npair, 2), ((0, 0), (0, FF_STRIDE - 2))).reshape(1, npair * FF_STRIDE)
    return jnp.pad(v, ((0, 0), (0, LANES - npair * FF_STRIDE)))


def _pair_unpad(row):
    npair = FOX_HEADS // 2
    return row[0, :npair * FF_STRIDE].reshape(npair, FF_STRIDE)[:, :2].reshape(FOX_HEADS)


def _pool_blockdiag(w_pool):
    g, cg, _ = w_pool.shape
    eye = jnp.eye(g, dtype=w_pool.dtype)
    return jnp.einsum("gh,gcd->gchd", eye, w_pool).reshape(g * cg, g * cg)


QK_BOUND_SLACK = 1.05


def _layer_params(norm_g, b_f, q_norm_g, k_norm_g, w_pool, pool_scale):
    qk_bound = QK_BOUND_SLACK * HEAD_DIM * QK_SCALE * jnp.max(jnp.abs(q_norm_g)) * jnp.max(jnp.abs(k_norm_g))
    return dict(g=norm_g.reshape(1, -1), qg=jnp.tile(q_norm_g, FOX_HEADS).reshape(1, FOX_W), kg=jnp.tile(k_norm_g, FOX_HEADS).reshape(1, FOX_W),
                bfp=_pair_pad(b_f), wpd=_pool_blockdiag(w_pool).astype(BF16), ps=pool_scale.reshape(1, POOL_W),
                qkb=jnp.full((1, LANES), qk_bound, F32))


def _layer_fwd(x, wt_all, w_out, layer, prm, bs):
    projm, ffo, h = _inproj(x, prm["g"], wt_all, layer, tm=bs["tm_proj"], tn=PROJ_TN)
    qn, ka, kb, v, sq, sk, sv, pooled, yp, pm = _prep(projm, ffo, prm["qg"], prm["kg"], prm["bfp"], prm["wpd"], prm["ps"], ts=bs["ts"])
    o, lse, fm = _fox_fwd(qn, ka, kb, v, projm, prm["qkb"], tq=bs["tq"], tk=bs["tk"])
    so, sm = _sb_fwd(sq, sk, sv, projm, tq=bs["tq"], tk=bs["tks"])
    y = _outproj(x, fm, pm, sm, w_out, layer, tm=bs["tm_proj"])
    saved = dict(x=x, projm=projm, ffo=ffo, h=h, qn=qn, ka=ka, kb=kb, v=v, sq=sq, sk=sk, sv=sv, pooled=pooled, yp=yp,
                 o=o, lse=lse, so=so, fm=fm, pm=pm, sm=sm)
    return y, saved


def _layer_bwd(dy, wt_all, w_out, prm, sv_, bs, layer, stacks):
    dmix, stack_o = _outproj_bwd(dy, sv_["fm"], sv_["pm"], sv_["sm"], w_out, layer, None if stacks is None else stacks[2:], tm=bs["tm_proj"])
    dqn, dkn, dv, dfg, dct, dcr = _fox_bwd(sv_["qn"], sv_["ka"], sv_["kb"], sv_["v"], sv_["o"], sv_["lse"], dmix, sv_["projm"],
                                      prm["qkb"], tq=bs["tq_big"], tk=bs["tk"])
    dsq, dsk, dsv, dsg = _sb_bwd(sv_["sq"], sv_["sk"], sv_["sv"], sv_["so"], dmix, sv_["projm"], tq=bs["tks"], tk=bs["tks"])
    dproj, dqg, dkg, dbf, dwp, dps = _prep_bwd(sv_["projm"], sv_["ffo"], dqn, dkn, dct, dcr, dv, dfg, dsq, dsk, dsv, dsg, dmix,
                                               sv_["pooled"], sv_["yp"], prm["qg"], prm["kg"], prm["bfp"], prm["wpd"], prm["ps"], ts=bs["ts"])
    stack_m, stack_f = _inproj_dw(sv_["h"], dproj, layer, None if stacks is None else stacks[:2], ts=bs["tm_proj"], tn=PROJ_TN)
    dx, dg = _inproj_dx(dproj, wt_all, layer, sv_["x"], prm["g"], dy, tm=bs["tm"])
    grads = dict(
        norm_g=dg[0],
        b_f=_pair_unpad(dbf), q_norm_g=dqg.reshape(FOX_HEADS, HEAD_DIM).sum(0), k_norm_g=dkg.reshape(FOX_HEADS, HEAD_DIM).sum(0),
        w_pool=jnp.stack([dwp[HEAD_DIM * g:HEAD_DIM * (g + 1), HEAD_DIM * g:HEAD_DIM * (g + 1)] for g in range(4)]),
        pool_scale=dps[0])
    return dx, grads, (stack_m, stack_f, stack_o)


def _local_step(x, target, wt_all, w_out, norm_g, b_f, q_norm_g, k_norm_g, w_pool, pool_scale):
    S, D = x.shape
    bs = _blocks(S)
    prms = [_layer_params(norm_g[l], b_f[l], q_norm_g[l], k_norm_g[l], w_pool[l], pool_scale[l]) for l in range(DEPTH)]
    saved = []
    y = x
    for l in range(DEPTH):
        y, s_ = _layer_fwd(y, wt_all, w_out, l, prms[l], bs)
        saved.append(s_)
    dy, sq = _loss_head(y, target, tm=bs["tm"])
    loss = 0.5 * jnp.sum(sq) / D
    grads = [None] * DEPTH
    stacks = None
    for l in reversed(range(DEPTH)):
        dy, grads[l], stacks = _layer_bwd(dy, wt_all, w_out, prms[l], saved[l], bs, l, stacks)
    stacked = {k: jnp.stack([g[k] for g in grads]) for k in grads[0]}
    return loss, dy, stacked, stacks


SMALL = ("norm_g", "b_f", "q_norm_g", "k_norm_g", "w_pool", "pool_scale")


def _pack_small(gr):
    flat = jnp.concatenate([gr[k].reshape(-1) for k in SMALL])
    pad = (-flat.shape[0]) % (8 * LANES)
    return jnp.pad(flat, (0, pad)).reshape(-1, LANES)


def _unpack_small(packed, like):
    flat = packed.reshape(-1)
    out, off = {}, 0
    for k in SMALL:
        n = like[k].size
        out[k] = flat[off:off + n].reshape(like[k].shape)
        off += n
    return out


def kernel(x, norm_g, w_in, b_f, q_norm_g, k_norm_g, w_pool, pool_scale, w_out, loss_target, m_norm_g, m_w_in, m_b_f, m_q_norm_g, m_k_norm_g, m_w_pool, m_pool_scale, m_w_out, v_norm_g, v_w_in, v_b_f, v_q_norm_g, v_k_norm_g, v_w_pool, v_pool_scale, v_w_out):
    weights = dict(norm_g=norm_g, w_in=w_in, b_f=b_f, q_norm_g=q_norm_g, k_norm_g=k_norm_g, w_pool=w_pool, pool_scale=pool_scale, w_out=w_out)
    mom_m = dict(norm_g=m_norm_g, w_in=m_w_in, b_f=m_b_f, q_norm_g=m_q_norm_g, k_norm_g=m_k_norm_g, w_pool=m_w_pool, pool_scale=m_pool_scale, w_out=m_w_out)
    mom_v = dict(norm_g=v_norm_g, w_in=v_w_in, b_f=v_b_f, q_norm_g=v_q_norm_g, k_norm_g=v_k_norm_g, w_pool=v_w_pool, pool_scale=v_pool_scale, w_out=v_w_out)
    shard_cols = w_in.shape[2]
    shard_rows = w_out.shape[1]

    cols_first = lambda a: jnp.transpose(a, (2, 0, 1))
    w_in_t = cols_first(w_in)
    w_in_t_full, w_out_full = _gather_weights(w_in_t, w_out)
    wt_all = _to_aligned(w_in_t_full)
    loss, dx, gr, stacks = _local_step(x[0], loss_target[0], wt_all, w_out_full, norm_g, b_f, q_norm_g, k_norm_g, w_pool, pool_scale)
    loss = lax.psum(loss, ("x", "y", "c"))

    (g_in_mine, g_in_sib), g_w_out = _reduce_scatter(*stacks, shard_cols, shard_rows)
    small = _unpack_small(_all_reduce_small(_pack_small(gr)), {k: weights[k] for k in SMALL})
    grad_w = dict(small, w_out=g_w_out)

    names = ("norm_g", "w_in", "b_f", "q_norm_g", "k_norm_g", "w_pool", "pool_scale", "w_out")
    upd = {k: _adamw_nd(weights[k], grad_w[k], mom_m[k], mom_v[k]) for k in names if k != "w_in"}
    in_t = _adamw_halves(w_in_t, g_in_mine, g_in_sib, cols_first(mom_m["w_in"]), cols_first(mom_v["w_in"]))
    grad_w["w_in"], *upd["w_in"] = [jnp.transpose(a, (1, 2, 0)) for a in in_t]
    return (loss, dx[None], *[grad_w[k] for k in names], *[upd[k][0] for k in names], *[upd[k][1] for k in names], *[upd[k][2] for k in names])
```

```python
import functools

import jax
import jax.numpy as jnp
from jax import lax
from jax.experimental import pallas as pl
from jax.experimental.pallas import tpu as pltpu

F32 = jnp.float32
BF16 = jnp.bfloat16

DEPTH = 4
HEAD_DIM = 64
FOX_HEADS = 8
SB_HEADS = 4
FOX_W = FOX_HEADS * HEAD_DIM
SB_W = SB_HEADS * HEAD_DIM
POOL_W = 256
POOL_WINDOWS = (2, 4, 8, 16)
POOL_HALO = 16
D_MIX = FOX_W + POOL_W + SB_W
EPS = 1e-6
NEG = -1e30
QK_SCALE = HEAD_DIM ** -0.5

ORIG_FOX = 4 * FOX_W
ORIG_FF = ORIG_FOX
ORIG_REST = ORIG_FF + FOX_HEADS
D_IN = ORIG_REST + 2 * POOL_W + 4 * SB_W

C_FQ, C_FK, C_FV, C_FG = 0, FOX_W, 2 * FOX_W, 3 * FOX_W
C_PX = 4 * FOX_W
C_PG = C_PX + POOL_W
C_SQ = C_PG + POOL_W
C_SK, C_SV, C_SG = C_SQ + SB_W, C_SQ + 2 * SB_W, C_SQ + 3 * SB_W
PM = C_SG + SB_W
LANES = 128
LANE_SHIFT = 7
HEAD_SHIFT = 6
PW = PM + LANES
FF_STRIDE = 8
AUG = 3

ADAM_LR = 0.001
ADAM_B1 = 0.9
ADAM_B2 = 0.999
ADAM_EPS = 1e-08
ADAM_WD = 0.01
ADAM_STEP = 10

VMEM_LIMIT = 48 * 1024 * 1024
PROJ_TN = PM // 2


def _cparams(**kw):
    return pltpu.CompilerParams(vmem_limit_bytes=VMEM_LIMIT, **kw)


def _dot(a, b):
    return jnp.dot(a, b, preferred_element_type=F32)


def _dot_nt(a, b):
    return lax.dot_general(a, b, (((1,), (1,)), ((), ())), preferred_element_type=F32)


def _dot_tn(a, b):
    return lax.dot_general(a, b, (((0,), (0,)), ((), ())), preferred_element_type=F32)


def _split2(x):
    hi = x.astype(BF16)
    lo = (x - hi.astype(F32)).astype(BF16)
    return hi, lo


def _split3(x):
    hi = x.astype(BF16)
    r = x - hi.astype(F32)
    mid = r.astype(BF16)
    lo = (r - mid.astype(F32)).astype(BF16)
    return hi, mid, lo


def _dot_exact_rhs(x, m):
    hi, mid, lo = _split3(x)
    return _dot(hi, m) + _dot(mid, m) + _dot(lo, m)


def _dot_exact_lhs(m, x):
    hi, mid, lo = _split3(x)
    return _dot(m, hi) + _dot(m, mid) + _dot(m, lo)


def _sigmoid(x):
    return 1.0 / (1.0 + jnp.exp(-x))


def _silu_pair(x):
    s = _sigmoid(x)
    return x * s, s * (1.0 + x * (1.0 - s))


def _iota(shape, dim):
    return lax.broadcasted_iota(jnp.int32, shape, dim)


def _ones_where(cond):
    return jnp.where(cond, 1.0, 0.0).astype(BF16)


GROUP_SLAB = 256


def _head_blockdiag():
    rows, cols = _iota((2 * GROUP_SLAB, GROUP_SLAB), 0) & (GROUP_SLAB - 1), _iota((2 * GROUP_SLAB, GROUP_SLAB), 1)
    return _ones_where((rows >> HEAD_SHIFT) == (cols >> HEAD_SHIFT))


def _group_sum(x, bd):
    hi, lo = _split2(x)
    slabs = [_dot(jnp.concatenate([hi[:, s:s + GROUP_SLAB], lo[:, s:s + GROUP_SLAB]], axis=1), bd) for s in range(0, x.shape[1], GROUP_SLAB)]
    return jnp.concatenate(slabs, axis=1)


def _lane_pick(x, lane_idx, lane):
    return jnp.sum(jnp.where(lane_idx == lane, x, 0.0), axis=1, keepdims=True)


def _inproj(x, g, wt_all, layer, *, tm, tn):
    S, D = x.shape
    nj = PM // tn

    def body(x_ref, g_ref, w_ref, wff_ref, proj_ref, ff_ref, h_ref):
        @pl.when(pl.program_id(1) == 0)
        def _():
            xf = x_ref[...]
            ms = jnp.mean(xf * xf, axis=-1, keepdims=True)
            h = (xf * lax.rsqrt(ms + EPS) * g_ref[...]).astype(BF16)
            h_ref[...] = h
            ff_ref[...] = _dot_nt(h, wff_ref[...])

        proj_ref[...] = _dot_nt(h_ref[...], w_ref[...])

    return pl.pallas_call(
        body, name="inproj", grid=(S // tm, nj),
        in_specs=[pl.BlockSpec((tm, D), lambda i, j: (i, 0)),
                  pl.BlockSpec((1, D), lambda i, j: (0, 0)),
                  pl.BlockSpec((None, tn, D), lambda i, j: (layer, j, 0)),
                  pl.BlockSpec((None, LANES, D), lambda i, j: (layer, PM // LANES, 0))],
        out_specs=[pl.BlockSpec((tm, tn), lambda i, j: (i, j)),
                   pl.BlockSpec((tm, LANES), lambda i, j: (i, 0)),
                   pl.BlockSpec((tm, D), lambda i, j: (i, 0))],
        out_shape=[jax.ShapeDtypeStruct((S, PM), F32), jax.ShapeDtypeStruct((S, LANES), F32),
                   jax.ShapeDtypeStruct((S, D), BF16)],
        compiler_params=_cparams(dimension_semantics=("arbitrary", "arbitrary")),
    )(x, g, wt_all, wt_all)


def _pool_group_select(lane_group, vals):
    return jnp.where(lane_group == 0, vals[0], jnp.where(lane_group == 1, vals[1], jnp.where(lane_group == 2, vals[2], vals[3])))


def _prep(projm, ffo, qg, kg, bfp, wpd, ps, *, ts):
    S = projm.shape[0]
    nb = S // ts
    hb = ts // POOL_HALO

    def body(fq_ref, fk_ref, fv_ref, pp_ref, halo_ref, ff_ref, sq_ref, sk_ref, sv_ref,
             qg_ref, kg_ref, bf_ref, wpd_ref, ps_ref,
             qn_ref, ka_ref, kb_ref, v_ref, sqo_ref, sko_ref, svo_ref, pooled_ref, yp_ref, pm_ref,
             carry_ref, c_ref, buf_ref):
        i = pl.program_id(0)
        bd = _head_blockdiag()
        normed = []
        for src, g_ref in ((fq_ref, qg_ref), (fk_ref, kg_ref)):
            q = src[...]
            ss = _group_sum(q * q, bd)
            normed.append(q * lax.rsqrt(ss * (1.0 / HEAD_DIM) + EPS) * g_ref[...])
        qn_ref[...] = (normed[0] * QK_SCALE).astype(BF16)
        kn = normed[1]
        v_ref[...] = fv_ref[...].astype(BF16)
        sqo_ref[...] = (sq_ref[...] * QK_SCALE).astype(BF16)
        sko_ref[...] = sk_ref[...].astype(BF16)
        svo_ref[...] = sv_ref[...].astype(BF16)

        @pl.when(i == 0)
        def _():
            carry_ref[...] = jnp.zeros_like(carry_ref)

        z = ff_ref[...] + bf_ref[...]
        lf = jnp.minimum(z, 0.0) - jnp.log(1.0 + jnp.exp(-jnp.abs(z)))
        tri = _ones_where(_iota((ts, ts), 1) <= _iota((ts, ts), 0))
        c = _dot_exact_lhs(tri, lf) + carry_ref[...]
        c_ref[...] = c
        carry_ref[...] = c_ref[ts - 1:ts, :]
        parts = jnp.concatenate(_split3(-c), axis=1)
        row = _iota((AUG * LANES, FOX_W), 0)
        col = _iota((AUG * LANES, FOX_W), 1)
        part, src = row >> LANE_SHIFT, row & (LANES - 1)
        pair, off = col >> LANE_SHIFT, col & (LANES - 1)
        sel_a = _ones_where((src == FF_STRIDE * pair) & (off == HEAD_DIM + part))
        sel_b = _ones_where((src == FF_STRIDE * pair + 1) & (off == part))
        first_half = (_iota((1, FOX_W), 1) & HEAD_DIM) == 0
        ka_ref[...] = jnp.where(first_half, kn, _dot(parts, sel_a)).astype(BF16)
        kb_ref[...] = jnp.where(first_half, _dot(parts, sel_b), kn).astype(BF16)

        x = pp_ref[:, 0:POOL_W]
        pg = pp_ref[:, POOL_W:2 * POOL_W]
        halo = jnp.where(i > 0, halo_ref[:, 0:POOL_W], 0.0)
        buf_ref[0:POOL_HALO, :] = halo
        buf_ref[POOL_HALO:POOL_HALO + ts, :] = x
        acc = x
        snaps = []
        for d in range(1, POOL_HALO):
            acc = acc + buf_ref[pl.ds(POOL_HALO - d, ts), :]
            if d + 1 in POOL_WINDOWS:
                snaps.append(acc)
        lane_group = _iota((1, POOL_W), 1) >> HEAD_SHIFT
        wsum = _pool_group_select(lane_group, snaps)
        wlen = _pool_group_select(lane_group, [float(w) for w in POOL_WINDOWS])
        tpos = (i * ts + _iota((ts, 1), 0) + 1).astype(F32)
        pooled = wsum / jnp.minimum(tpos, wlen) - x
        pb = pooled.astype(BF16)
        pooled_ref[...] = pb
        yp = _dot(pb, wpd_ref[...])
        yp_ref[...] = yp
        pm_ref[...] = (yp * ps_ref[...] * (pg * _sigmoid(pg))).astype(BF16)

    blk = lambda w, c: pl.BlockSpec((ts, w), lambda i: (i, c))
    full = lambda a: pl.BlockSpec(a.shape, lambda i: (0,) * a.ndim)
    out_shapes = [
        jax.ShapeDtypeStruct((S, FOX_W), BF16), jax.ShapeDtypeStruct((S, FOX_W), BF16), jax.ShapeDtypeStruct((S, FOX_W), BF16),
        jax.ShapeDtypeStruct((S, FOX_W), BF16),
        jax.ShapeDtypeStruct((S, SB_W), BF16), jax.ShapeDtypeStruct((S, SB_W), BF16), jax.ShapeDtypeStruct((S, SB_W), BF16),
        jax.ShapeDtypeStruct((S, POOL_W), BF16), jax.ShapeDtypeStruct((S, POOL_W), F32), jax.ShapeDtypeStruct((S, POOL_W), BF16),
    ]
    out_specs = [
        blk(FOX_W, 0), blk(FOX_W, 0), blk(FOX_W, 0), blk(FOX_W, 0),
        blk(SB_W, 0), blk(SB_W, 0), blk(SB_W, 0),
        blk(POOL_W, 0), blk(POOL_W, 0), blk(POOL_W, 0),
    ]
    return pl.pallas_call(
        body, name="prep", grid=(nb,),
        in_specs=[blk(FOX_W, C_FQ // FOX_W), blk(FOX_W, C_FK // FOX_W), blk(FOX_W, C_FV // FOX_W), blk(2 * POOL_W, C_PX // (2 * POOL_W)),
                  pl.BlockSpec((POOL_HALO, 2 * POOL_W), lambda i: (jnp.maximum(i * hb - 1, 0), C_PX // (2 * POOL_W))),
                  blk(LANES, 0),
                  blk(SB_W, C_SQ // SB_W), blk(SB_W, C_SK // SB_W), blk(SB_W, C_SV // SB_W),
                  full(qg), full(kg), full(bfp), full(wpd), full(ps)],
        out_specs=out_specs, out_shape=out_shapes,
        scratch_shapes=[pltpu.VMEM((1, LANES), F32), pltpu.VMEM((ts, LANES), F32), pltpu.VMEM((ts + POOL_HALO, POOL_W), F32)],
        compiler_params=_cparams(dimension_semantics=("arbitrary",)),
    )(projm, projm, projm, projm, projm, ffo, projm, projm, projm, qg, kg, bfp, wpd, ps)


def _pair_masks(x):
    ma = _iota((1, LANES), 1) < HEAD_DIM
    zero = jnp.zeros_like(x)
    return jnp.where(ma, x, zero), jnp.where(ma, zero, x)


DIAG_TILE = 256


def _diag_tiles(tq, size=DIAG_TILE):
    size = min(tq, size)
    return [(t * size, size) for t in range(tq // size)]


def _put_rows(old, new, r0):
    return new if r0 == 0 else jnp.concatenate([old[:r0], new], axis=0)


def _aug_queries(q):
    lane = _iota((1, LANES), 1)
    one = jnp.ones_like(q)
    zero = jnp.zeros_like(q)
    qa = jnp.where(lane < HEAD_DIM, q, jnp.where(lane < HEAD_DIM + AUG, one, zero))
    qb = jnp.where(lane >= HEAD_DIM, q, jnp.where(lane < AUG, one, zero))
    return qa, qb


EXP_DEAD = -105.0
PACK = 16


def _fox_walk_left(nfull, tk, block, carry, k_refs, qk_bound, row_floor):
    lane = _iota((1, LANES), 1)

    def alive(h, jj, c):
        k0 = pl.multiple_of(jnp.maximum(nfull - 1 - jj, 0) * tk + tk - PACK, PACK)
        last = k_refs[h][pl.ds(k0, PACK), :].astype(F32)
        lo = HEAD_DIM if h == 0 else 0
        negc = jnp.sum(jnp.where((lane >= lo) & (lane < lo + AUG), last, 0.0), axis=1, keepdims=True)
        return qk_bound + jnp.max(negc) - row_floor(c)[h] >= EXP_DEAD

    def walk(heads, jj0, c0):
        def go_on(state):
            jj, c = state
            ok = jj < nfull
            for h in heads:
                ok = ok & alive(h, jj, c)
            return ok

        def step(state):
            jj, c = state
            return jj + 1, block(pl.multiple_of((nfull - 1 - jj) * tk, tk), tk, 0, c, False, heads)

        return lax.while_loop(go_on, step, (jj0, c0))

    jj_pair, carry = walk((0, 1), jnp.int32(0), carry)
    carry = walk((0,), jj_pair, carry)[1]
    return walk((1,), jj_pair, carry)[1]


def _fox_fwd(qn, ka, kb, v, projm, qkb, *, tq, tk):
    S = qn.shape[0]
    npair = FOX_HEADS // 2

    def body(q_ref, ka_ref, kb_ref, v_ref, fg_ref, qkb_ref, o_ref, lse_ref, fm_ref):
        qi = pl.program_id(1)
        lane = _iota((1, LANES), 1)
        ma = lane < HEAD_DIM
        qaug = _aug_queries(q_ref[...])
        k_refs = (ka_ref, kb_ref)

        def block(k0, tkl, r0, carry, masked, heads=(0, 1)):
            vb = v_ref[pl.ds(k0, tkl), :]
            if masked:
                mask = (k0 + _iota((tq - r0, tkl), 1)) <= (qi * tq + r0 + _iota((tq - r0, tkl), 0))
            scores = {h: _dot_nt(qaug[h][r0:], k_refs[h][pl.ds(k0, tkl), :]) for h in heads}
            new = list(carry)
            for h in heads:
                m, l, acc = [x[r0:] for x in carry[h]]
                s = jnp.where(mask, scores[h], NEG) if masked else scores[h]
                m_new = jnp.maximum(m, jnp.max(s, axis=1, keepdims=True))
                alpha = jnp.exp(m - m_new)
                p = jnp.exp(s - m_new)
                sub = (m_new, alpha * l + jnp.sum(p, axis=1, keepdims=True), alpha * acc + _dot(p.astype(BF16), vb))
                new[h] = tuple(_put_rows(old, x, r0) for old, x in zip(carry[h], sub))
            return tuple(new)

        carry = tuple((jnp.full((tq, 1), NEG, F32), jnp.zeros((tq, 1), F32), jnp.zeros((tq, LANES), F32)) for _ in range(2))
        for off, size in _diag_tiles(tq, tq):
            carry = block(pl.multiple_of(qi * tq + off, size), size, off, carry, True)
        carry = _fox_walk_left((qi * tq) // tk, tk, block, carry, k_refs, jnp.max(qkb_ref[...]),
                               lambda c: (jnp.min(c[0][0]), jnp.min(c[1][0])))
        (ma_, la, acca), (mb_, lb, accb) = carry
        o = jnp.where(ma, acca / la, accb / lb)
        o_ref[...] = o
        lse_ref[...] = jnp.where(ma, ma_ + jnp.log(la), mb_ + jnp.log(lb))
        fg = fg_ref[...]
        fm_ref[...] = (o * (fg * _sigmoid(fg))).astype(BF16)

    qblk = pl.BlockSpec((tq, LANES), lambda p, i: (i, p))
    kvblk = pl.BlockSpec((S, LANES), lambda p, i: (0, p))
    return pl.pallas_call(
        body, name="fox_fwd", grid=(npair, S // tq),
        in_specs=[qblk, kvblk, kvblk, kvblk,
                  pl.BlockSpec((tq, LANES), lambda p, i: (i, C_FG // LANES + p)),
                  pl.BlockSpec((1, LANES), lambda p, i: (0, 0))],
        out_specs=[qblk, qblk, qblk],
        out_shape=[jax.ShapeDtypeStruct((S, FOX_W), F32), jax.ShapeDtypeStruct((S, FOX_W), F32), jax.ShapeDtypeStruct((S, FOX_W), BF16)],
        compiler_params=_cparams(dimension_semantics=("arbitrary", "arbitrary")),
    )(qn, ka, kb, v, projm, qkb)


def _suffix_sums(x, tmat2):
    return _dot(jnp.concatenate(_split2(x), axis=1), tmat2)


def _suffix_matrix(tk, inclusive):
    rr, cc = _iota((2 * tk, tk), 0) & (tk - 1), _iota((2 * tk, tk), 1)
    return _ones_where(rr >= cc) if inclusive else _ones_where(rr > cc)


def _sb_scores(qh, kb, causal, tmat2, r_runs):
    heads = range(2)
    zs = [_dot_nt(qh[h], kb) for h in heads]
    nsps = [jnp.minimum(-z, 0.0) - jnp.log(1.0 + jnp.exp(-jnp.abs(z))) for z in zs]
    lbs = nsps if causal is None else [jnp.where(causal, n, 0.0) for n in nsps]
    rins = [_suffix_sums(lb, tmat2) for lb in lbs]
    args = [zs[h] + lbs[h] + (rins[h] + r_runs[h]) for h in heads]
    a_s = [jnp.exp(arg if causal is None else jnp.where(causal, arg, NEG)) for arg in args]
    return zs, nsps, lbs, a_s


def _sb_walk_left(nfull, tk, block, carry, running_sums):
    def alive(state):
        jj, c = state
        ra, rb = running_sums(c)
        return (jj < nfull) & (jnp.max(jnp.maximum(ra, rb)) >= EXP_DEAD)

    def step(state):
        jj, c = state
        return jj + 1, block(pl.multiple_of((nfull - 1 - jj) * tk, tk), 0, c, False)

    return lax.while_loop(alive, step, (jnp.int32(0), carry))[1]


def _sb_fwd(sq, sk, sv, projm, *, tq, tk):
    S = sq.shape[0]
    npair = SB_HEADS // 2

    def body(q_ref, k_ref, v_ref, sg_ref, o_ref, sm_ref):
        qi = pl.program_id(1)
        lane = _iota((1, LANES), 1)
        ma = lane < HEAD_DIM
        qh = _pair_masks(q_ref[...])
        tmat2 = _suffix_matrix(tk, inclusive=False)
        nfull = (qi * tq) // tk

        def block(k0, r0, carry, masked):
            nr = tq - r0
            kb = k_ref[pl.ds(k0, tk), :]
            vb = v_ref[pl.ds(k0, tk), :]
            causal = (k0 + _iota((nr, tk), 1)) < (qi * tq + r0 + _iota((nr, tk), 0)) if masked else None
            _, _, lbs, a_s = _sb_scores([q[r0:] for q in qh], kb, causal, tmat2, [carry[h][0][r0:] for h in range(2)])
            pv = _dot(jnp.concatenate([a.astype(BF16) for a in a_s], axis=0), vb)
            return tuple((_put_rows(carry[h][0], carry[h][0][r0:] + jnp.sum(lbs[h], axis=1, keepdims=True), r0),
                          _put_rows(carry[h][1], carry[h][1][r0:] + pv[h * nr:(h + 1) * nr], r0)) for h in range(2))

        carry = tuple((jnp.zeros((tq, 1), F32), jnp.zeros((tq, LANES), F32)) for _ in range(2))
        for off, size in reversed(_diag_tiles(tq)):
            assert size == tk
            carry = block(pl.multiple_of(qi * tq + off, tk), off, carry, True)
        (_, acca), (_, accb) = _sb_walk_left(nfull, tk, block, carry, lambda c: (c[0][0], c[1][0]))
        o = jnp.where(ma, acca, accb)
        o_ref[...] = o
        sg = sg_ref[...]
        sm_ref[...] = (o * (sg * _sigmoid(sg))).astype(BF16)

    qblk = pl.BlockSpec((tq, LANES), lambda p, i: (i, p))
    kvblk = pl.BlockSpec((S, LANES), lambda p, i: (0, p))
    return pl.pallas_call(
        body, name="sb_fwd", grid=(npair, S // tq),
        in_specs=[qblk, kvblk, kvblk, pl.BlockSpec((tq, LANES), lambda p, i: (i, C_SG // LANES + p))],
        out_specs=[qblk, qblk],
        out_shape=[jax.ShapeDtypeStruct((S, SB_W), F32), jax.ShapeDtypeStruct((S, SB_W), BF16)],
        compiler_params=_cparams(dimension_semantics=("arbitrary", "arbitrary")),
    )(sq, sk, sv, projm)


def _outproj(x, fm, pm, sm, w_out, layer, *, tm):
    S, D = x.shape

    def body(x_ref, fm_ref, pm_ref, sm_ref, w_ref, y_ref):
        y = x_ref[...] + _dot(fm_ref[...], w_ref[0:FOX_W, :])
        y = y + _dot(pm_ref[...], w_ref[FOX_W:FOX_W + POOL_W, :])
        y_ref[...] = y + _dot(sm_ref[...], w_ref[FOX_W + POOL_W:D_MIX, :])

    row = lambda w: pl.BlockSpec((tm, w), lambda i: (i, 0))
    return pl.pallas_call(
        body, name="outproj", grid=(S // tm,),
        in_specs=[row(D), row(FOX_W), row(POOL_W), row(SB_W), pl.BlockSpec((None, D_MIX, D), lambda i: (layer, 0, 0))],
        out_specs=row(D), out_shape=jax.ShapeDtypeStruct((S, D), F32),
        compiler_params=_cparams(dimension_semantics=("arbitrary",)),
    )(x, fm, pm, sm, w_out)


def _loss_head(y, target, *, tm):
    S, D = y.shape

    def body(y_ref, t_ref, dy_ref, sq_ref):
        @pl.when(pl.program_id(0) == 0)
        def _():
            sq_ref[...] = jnp.zeros_like(sq_ref)

        d = y_ref[...] - t_ref[...]
        dy_ref[...] = d * (1.0 / D)
        sq_ref[...] += jnp.sum(d * d, axis=0, keepdims=True)

    row = pl.BlockSpec((tm, D), lambda i: (i, 0))
    return pl.pallas_call(
        body, name="loss_head", grid=(S // tm,),
        in_specs=[row, row], out_specs=[row, pl.BlockSpec((1, D), lambda i: (0, 0))],
        out_shape=[jax.ShapeDtypeStruct((S, D), F32), jax.ShapeDtypeStruct((1, D), F32)],
        compiler_params=_cparams(dimension_semantics=("arbitrary",)),
    )(y, target)


def _outproj_bwd(dy, fm, pm, sm, w_out, layer, stacks, *, tm):
    S, D = dy.shape

    def body(dy_ref, fm_ref, pm_ref, sm_ref, w_ref, dm_ref, dw_ref):
        @pl.when(pl.program_id(0) == 0)
        def _():
            dw_ref[...] = jnp.zeros_like(dw_ref)

        dyb = dy_ref[...].astype(BF16)
        dm_ref[...] = _dot_nt(dyb, w_ref[...])
        dw_ref[0:FOX_W, :] += _dot_tn(fm_ref[...], dyb)
        dw_ref[FOX_W:FOX_W + POOL_W, :] += _dot_tn(pm_ref[...], dyb)
        dw_ref[FOX_W + POOL_W:D_MIX, :] += _dot_tn(sm_ref[...], dyb)

    row = lambda w: pl.BlockSpec((tm, w), lambda i: (i, 0))
    wspec = pl.BlockSpec((None, D_MIX, D), lambda i: (layer, 0, 0))
    return _stack_call(
        body, "outproj_bwd", (S // tm,), [row(D), row(FOX_W), row(POOL_W), row(SB_W), wspec], (dy, fm, pm, sm, w_out),
        [pl.BlockSpec((None, D_MIX, D), lambda i: (layer, 0, 0))], [(D_MIX, D)], stacks,
        plain_specs=[row(D_MIX)], plain_shapes=[jax.ShapeDtypeStruct((S, D_MIX), F32)],
        compiler_params=_cparams(dimension_semantics=("arbitrary",)))


def _fox_bwd(qn, ka, kb, v, o, lse, dmix, projm, qkb, *, tq, tk):
    S = qn.shape[0]
    npair = FOX_HEADS // 2

    def body(q_ref, ka_ref, kb_ref, v_ref, o_ref, lse_ref, dm_ref, fg_ref, qkb_ref,
             dq_ref, dk_ref, dv_ref, dfg_ref, dct_ref, dcr_ref):
        qi = pl.program_id(1)

        @pl.when(qi == 0)
        def _():
            dk_ref[...] = jnp.zeros_like(dk_ref)
            dv_ref[...] = jnp.zeros_like(dv_ref)
            dct_ref[...] = jnp.zeros_like(dct_ref)

        lane = _iota((1, LANES), 1)
        ma = lane < HEAD_DIM
        qh = _pair_masks(q_ref[...])
        qaug = _aug_queries(q_ref[...])
        k_refs = (ka_ref, kb_ref)
        lsev = lse_ref[...]
        lse = (_lane_pick(lsev, lane, 0), _lane_pick(lsev, lane, HEAD_DIM))
        fg = fg_ref[...]
        silu, dsilu = _silu_pair(fg)
        dm = dm_ref[...]
        ov = o_ref[...]
        do = dm * silu
        dfg_ref[...] = dm * ov * dsilu
        dd = do * ov
        dsum = (jnp.sum(jnp.where(ma, dd, 0.0), axis=1, keepdims=True), jnp.sum(jnp.where(ma, 0.0, dd), axis=1, keepdims=True))
        doh = _pair_masks(do.astype(BF16))

        def block(k0, tkl, r0, carry, masked, heads=(0, 1)):
            vb = v_ref[pl.ds(k0, tkl), :]
            if masked:
                mask = (k0 + _iota((tq - r0, tkl), 1)) <= (qi * tq + r0 + _iota((tq - r0, tkl), 0))
            kaugs = {h: k_refs[h][pl.ds(k0, tkl), :] for h in heads}
            scores = {h: _dot_nt(qaug[h][r0:], kaugs[h]) for h in heads}
            dps = {h: _dot_nt(doh[h][r0:], vb) for h in heads}
            ps, dss = [], []
            rows = [carry[1], carry[2]]
            for h in heads:
                s = jnp.where(mask, scores[h], NEG) if masked else scores[h]
                p = jnp.exp(s - lse[h][r0:])
                dsf = p * (dps[h] - dsum[h][r0:])
                dct_ref[0, h:h + 1, pl.ds(k0, tkl)] -= jnp.sum(dsf, axis=0, keepdims=True)
                rows[h] = _put_rows(carry[1 + h], carry[1 + h][r0:] + jnp.sum(dsf, axis=1, keepdims=True), r0)
                ps.append(p.astype(BF16))
                dss.append(dsf.astype(BF16))
            dv_ref[pl.ds(k0, tkl), :] += _dot_tn(jnp.concatenate(ps, axis=0), jnp.concatenate([doh[h][r0:] for h in heads], axis=0))
            dk_ref[pl.ds(k0, tkl), :] += _dot_tn(jnp.concatenate(dss, axis=0), jnp.concatenate([qh[h][r0:] for h in heads], axis=0))
            kh = jnp.concatenate([_pair_masks(kaugs[h])[h] for h in heads], axis=0)
            dq = _put_rows(carry[0], carry[0][r0:] + _dot(jnp.concatenate(dss, axis=1), kh), r0)
            return (dq, rows[0], rows[1])

        zcol = jnp.zeros((tq, 1), F32)
        carry = (jnp.zeros((tq, LANES), F32), zcol, zcol)
        for off, size in _diag_tiles(tq):
            carry = block(pl.multiple_of(qi * tq + off, size), size, off, carry, True)
        floors = (jnp.min(lse[0]), jnp.min(lse[1]))
        dq, rowa, rowb = _fox_walk_left((qi * tq) // tk, tk, block, carry, k_refs, jnp.max(qkb_ref[...]), lambda c: floors)
        dq_ref[...] = dq * QK_SCALE
        dcr_ref[0] = jnp.where(ma, rowa, rowb)

    qblk = pl.BlockSpec((tq, LANES), lambda p, i: (i, p))
    kvblk = pl.BlockSpec((S, LANES), lambda p, i: (0, p))
    f32out = jax.ShapeDtypeStruct((S, FOX_W), F32)
    ctblk = pl.BlockSpec((1, FF_STRIDE, S), lambda p, i: (p, 0, 0))
    return pl.pallas_call(
        body, name="fox_bwd", grid=(npair, S // tq),
        in_specs=[qblk, kvblk, kvblk, kvblk, qblk, qblk, qblk,
                  pl.BlockSpec((tq, LANES), lambda p, i: (i, C_FG // LANES + p)),
                  pl.BlockSpec((1, LANES), lambda p, i: (0, 0))],
        out_specs=[qblk, kvblk, kvblk, qblk, ctblk, pl.BlockSpec((1, tq, LANES), lambda p, i: (p, i, 0))],
        out_shape=[f32out, f32out, f32out, f32out, jax.ShapeDtypeStruct((npair, FF_STRIDE, S), F32),
                   jax.ShapeDtypeStruct((npair, S, LANES), F32)],
        compiler_params=_cparams(dimension_semantics=("arbitrary", "arbitrary")),
    )(qn, ka, kb, v, o, lse, dmix, projm, qkb)


def _sb_bwd(sq, sk, sv, o, dmix, projm, *, tq, tk):
    S = sq.shape[0]
    npair = SB_HEADS // 2
    mix0 = (FOX_W + POOL_W) // LANES

    def body(q_ref, k_ref, v_ref, o_ref, dm_ref, sg_ref, dq_ref, dk_ref, dv_ref, dsg_ref):
        qi = pl.program_id(1)

        @pl.when(qi == 0)
        def _():
            dk_ref[...] = jnp.zeros_like(dk_ref)
            dv_ref[...] = jnp.zeros_like(dv_ref)

        lane = _iota((1, LANES), 1)
        ma = lane < HEAD_DIM
        qh = _pair_masks(q_ref[...])
        sg = sg_ref[...]
        silu, dsilu = _silu_pair(sg)
        dm = dm_ref[...]
        ov = o_ref[...]
        do = dm * silu
        dsg_ref[...] = dm * ov * dsilu
        dob = do.astype(BF16)
        dd = dob.astype(F32) * ov
        dsum = (jnp.sum(jnp.where(ma, dd, 0.0), axis=1, keepdims=True), jnp.sum(jnp.where(ma, 0.0, dd), axis=1, keepdims=True))
        doh = _pair_masks(dob)
        tmat2 = _suffix_matrix(tk, inclusive=False)
        tmat2_inc = _suffix_matrix(tk, inclusive=True)
        nfull = (qi * tq) // tk

        def block(k0, r0, carry, masked):
            nr = tq - r0
            kb = k_ref[pl.ds(k0, tk), :]
            vb = v_ref[pl.ds(k0, tk), :]
            kh = _pair_masks(kb)
            causal = (k0 + _iota((nr, tk), 1)) < (qi * tq + r0 + _iota((nr, tk), 0)) if masked else None
            heads = range(2)
            qs = [q[r0:] for q in qh]
            dos = [d[r0:] for d in doh]
            das = [_dot_nt(dos[h], vb) for h in heads]
            zs, nsps, lbs, a_s = _sb_scores(qs, kb, causal, tmat2, [carry[h][0][r0:] for h in heads])
            abs_ = [a.astype(BF16) for a in a_s]
            us = [abs_[h].astype(F32) * das[h] for h in heads]
            uins = [_suffix_sums(u, tmat2_inc) for u in us]
            dzs = []
            for h in heads:
                cum_u = dsum[h][r0:] - (uins[h] + carry[h][1][r0:])
                dz = us[h] * jnp.exp(nsps[h]) - jnp.exp(zs[h] + nsps[h]) * cum_u
                if masked:
                    dz = jnp.where(causal, dz, 0.0)
                dzs.append(dz.astype(BF16))
            dv_ref[pl.ds(k0, tk), :] += _dot_tn(jnp.concatenate(abs_, axis=0), jnp.concatenate(dos, axis=0))
            dk_ref[pl.ds(k0, tk), :] += _dot_tn(jnp.concatenate(dzs, axis=0), jnp.concatenate(qs, axis=0))
            dq = _put_rows(carry[2], carry[2][r0:] + _dot(jnp.concatenate(dzs, axis=1), jnp.concatenate(kh, axis=0)), r0)
            new = [(_put_rows(carry[h][0], carry[h][0][r0:] + jnp.sum(lbs[h], axis=1, keepdims=True), r0),
                    _put_rows(carry[h][1], carry[h][1][r0:] + jnp.sum(us[h], axis=1, keepdims=True), r0)) for h in heads]
            return (new[0], new[1], dq)

        zcol = jnp.zeros((tq, 1), F32)
        carry = ((zcol, zcol), (zcol, zcol), jnp.zeros((tq, LANES), F32))
        for off, size in reversed(_diag_tiles(tq)):
            assert size == tk
            carry = block(pl.multiple_of(qi * tq + off, tk), off, carry, True)
        dq = _sb_walk_left(nfull, tk, block, carry, lambda c: (c[0][0], c[1][0]))[2]
        dq_ref[...] = dq * QK_SCALE

    qblk = pl.BlockSpec((tq, LANES), lambda p, i: (i, p))
    kvblk = pl.BlockSpec((S, LANES), lambda p, i: (0, p))
    f32out = jax.ShapeDtypeStruct((S, SB_W), F32)
    return pl.pallas_call(
        body, name="sb_bwd", grid=(npair, S // tq),
        in_specs=[qblk, kvblk, kvblk, qblk,
                  pl.BlockSpec((tq, LANES), lambda p, i: (i, mix0 + p)),
                  pl.BlockSpec((tq, LANES), lambda p, i: (i, C_SG // LANES + p))],
        out_specs=[qblk, kvblk, kvblk, qblk],
        out_shape=[f32out, f32out, f32out, f32out],
        compiler_params=_cparams(dimension_semantics=("arbitrary", "arbitrary")),
    )(sq, sk, sv, o, dmix, projm)


def _prep_bwd(projm, ffo, dqn, dkn, dct, dcr, dv, dfg, dsq, dsk, dsv, dsg, dmix, pooled, yp, qg, kg, bfp, wpd, ps, *, ts):
    S = projm.shape[0]
    nb = S // ts
    hb = ts // POOL_HALO
    npair = FOX_HEADS // 2
    last_halo = S // POOL_HALO - 1

    def body(fq_ref, fk_ref, pp_ref, pph_ref, ff_ref,
             dqn_ref, dkn_ref, dct_ref, dcr_ref, dv_ref, dfg_ref, dsq_ref, dsk_ref, dsv_ref, dsg_ref,
             dmp_ref, dmh_ref, pooled_ref, yp_ref, qg_ref, kg_ref, bf_ref, wpd_ref, ps_ref,
             dp_ref, dqg_ref, dkg_ref, dbf_ref, dwp_ref, dps_ref,
             carry_ref, dl_ref, buf_ref, dct_s):
        i = pl.program_id(0)
        blk = nb - 1 - i

        @pl.when(i == 0)
        def _():
            carry_ref[...] = jnp.zeros_like(carry_ref)
            dqg_ref[...] = jnp.zeros_like(dqg_ref)
            dkg_ref[...] = jnp.zeros_like(dkg_ref)
            dbf_ref[...] = jnp.zeros_like(dbf_ref)
            dwp_ref[...] = jnp.zeros_like(dwp_ref)
            dps_ref[...] = jnp.zeros_like(dps_ref)

        bd = _head_blockdiag()
        for raw_ref, g_ref, dn, dg_ref, col in ((fq_ref, qg_ref, dqn_ref[...], dqg_ref, C_FQ), (fk_ref, kg_ref, dkn_ref[...], dkg_ref, C_FK)):
            q = raw_ref[...]
            rstd = lax.rsqrt(_group_sum(q * q, bd) * (1.0 / HEAD_DIM) + EPS)
            xhat = q * rstd
            dg_ref[...] += jnp.sum(dn * xhat, axis=0, keepdims=True)
            dyg = dn * g_ref[...]
            mean = _group_sum(dyg * xhat, bd) * (1.0 / HEAD_DIM)
            dp_ref[:, col:col + FOX_W] = (rstd * (dyg - xhat * mean)).astype(BF16)
        dp_ref[:, C_FV:C_FV + FOX_W] = dv_ref[...].astype(BF16)
        dp_ref[:, C_FG:C_FG + FOX_W] = dfg_ref[...].astype(BF16)
        dp_ref[:, C_SQ:C_SQ + SB_W] = dsq_ref[...].astype(BF16)
        dp_ref[:, C_SK:C_SK + SB_W] = dsk_ref[...].astype(BF16)
        dp_ref[:, C_SV:C_SV + SB_W] = dsv_ref[...].astype(BF16)
        dp_ref[:, C_SG:C_SG + SB_W] = dsg_ref[...].astype(BF16)

        dct_s[...] = jnp.zeros_like(dct_s)
        for p in range(npair):
            dct_s[FF_STRIDE * p:FF_STRIDE * (p + 1), :] = dct_ref[p]
        dc = dct_s[...].T
        lane = _iota((1, LANES), 1)
        for p in range(npair):
            dcr = dcr_ref[p]
            dc = dc + jnp.where(lane == FF_STRIDE * p, _lane_pick(dcr, lane, 0), 0.0)
            dc = dc + jnp.where(lane == FF_STRIDE * p + 1, _lane_pick(dcr, lane, HEAD_DIM), 0.0)
        triu = _ones_where(_iota((ts, ts), 1) >= _iota((ts, ts), 0))
        dlf = _dot_exact_lhs(triu, dc) + carry_ref[...]
        dl_ref[...] = dlf
        carry_ref[...] = dl_ref[0:1, :]
        z = ff_ref[...] + bf_ref[...]
        dff = dlf * (1.0 / (1.0 + jnp.exp(z)))
        dbf_ref[...] += jnp.sum(dff, axis=0, keepdims=True)
        dp_ref[:, PM:PW] = dff.astype(BF16)

        psv = ps_ref[...]
        wpdv = wpd_ref[...]
        lane_group = _iota((1, POOL_W), 1) >> HEAD_SHIFT
        wlen = _pool_group_select(lane_group, [float(w) for w in POOL_WINDOWS])
        pg = pp_ref[:, POOL_W:2 * POOL_W]
        silu, dsilu = _silu_pair(pg)
        dmp = dmp_ref[...]
        ypv = yp_ref[...]
        dp_ref[:, C_PG:C_PG + POOL_W] = (dmp * (ypv * psv) * dsilu).astype(BF16)
        dps_ref[...] += jnp.sum(dmp * silu * ypv, axis=0, keepdims=True)
        dyp = (dmp * psv * silu).astype(BF16)
        dwp_ref[...] += _dot_tn(pooled_ref[...], dyp)
        dpooled = _dot_nt(dyp, wpdv)
        pgh = pph_ref[:, POOL_W:2 * POOL_W]
        dyph = (dmh_ref[...] * psv * (pgh * _sigmoid(pgh))).astype(BF16)
        dpooled_h = jnp.where(blk < nb - 1, _dot_nt(dyph, wpdv), 0.0)
        tpos = (blk * ts + _iota((ts, 1), 0) + 1).astype(F32)
        ev = dpooled / jnp.minimum(tpos, wlen)
        buf_ref[0:ts, :] = ev
        buf_ref[ts:ts + POOL_HALO, :] = dpooled_h / wlen
        acc = ev
        snaps = []
        for d in range(1, POOL_HALO):
            acc = acc + buf_ref[pl.ds(d, ts), :]
            if d + 1 in POOL_WINDOWS:
                snaps.append(acc)
        dp_ref[:, C_PX:C_PX + POOL_W] = (_pool_group_select(lane_group, snaps) - dpooled).astype(BF16)

    rblk = lambda w, c: pl.BlockSpec((ts, w), lambda i: (nb - 1 - i, c))
    full = lambda a: pl.BlockSpec(a.shape, lambda i: (0,) * a.ndim)
    halo = lambda w, c: pl.BlockSpec((POOL_HALO, w), lambda i: (jnp.minimum((nb - i) * hb, last_halo), c))
    acc_spec = lambda r, w: pl.BlockSpec((r, w), lambda i: (0, 0))
    return pl.pallas_call(
        body, name="prep_bwd", grid=(nb,),
        in_specs=[rblk(FOX_W, C_FQ // FOX_W), rblk(FOX_W, C_FK // FOX_W), rblk(2 * POOL_W, C_PX // (2 * POOL_W)),
                  halo(2 * POOL_W, C_PX // (2 * POOL_W)), rblk(LANES, 0),
                  rblk(FOX_W, 0), rblk(FOX_W, 0), pl.BlockSpec((npair, FF_STRIDE, ts), lambda i: (0, 0, nb - 1 - i)),
                  pl.BlockSpec((npair, ts, LANES), lambda i: (0, nb - 1 - i, 0)), rblk(FOX_W, 0), rblk(FOX_W, 0),
                  rblk(SB_W, 0), rblk(SB_W, 0), rblk(SB_W, 0), rblk(SB_W, 0),
                  rblk(POOL_W, FOX_W // POOL_W), halo(POOL_W, FOX_W // POOL_W), rblk(POOL_W, 0), rblk(POOL_W, 0),
                  full(qg), full(kg), full(bfp), full(wpd), full(ps)],
        out_specs=[rblk(PW, 0), acc_spec(1, FOX_W), acc_spec(1, FOX_W), acc_spec(1, LANES), acc_spec(POOL_W, POOL_W), acc_spec(1, POOL_W)],
        out_shape=[jax.ShapeDtypeStruct((S, PW), BF16), jax.ShapeDtypeStruct((1, FOX_W), F32), jax.ShapeDtypeStruct((1, FOX_W), F32),
                   jax.ShapeDtypeStruct((1, LANES), F32), jax.ShapeDtypeStruct((POOL_W, POOL_W), F32), jax.ShapeDtypeStruct((1, POOL_W), F32)],
        scratch_shapes=[pltpu.VMEM((1, LANES), F32), pltpu.VMEM((ts, LANES), F32), pltpu.VMEM((ts + POOL_HALO, POOL_W), F32),
                        pltpu.VMEM((LANES, ts), F32)],
        compiler_params=_cparams(dimension_semantics=("arbitrary",)),
    )(projm, projm, projm, projm, ffo, dqn, dkn, dct, dcr, dv, dfg, dsq, dsk, dsv, dsg, dmix, dmix, pooled, yp, qg, kg, bfp, wpd, ps)


def _stack_call(body, name, grid, in_specs, operands, slot_specs, slot_shapes, stacks, plain_specs=(), plain_shapes=(), **kw):
    out_specs = list(plain_specs) + list(slot_specs)
    out_shape = list(plain_shapes) + [jax.ShapeDtypeStruct((DEPTH,) + s, F32) for s in slot_shapes]
    if stacks is None:
        return pl.pallas_call(body, name=name, grid=grid, in_specs=in_specs, out_specs=out_specs, out_shape=out_shape, **kw)(*operands)
    n = len(operands)

    def aliased_body(*refs):
        body(*refs[:n], *refs[n + len(stacks):])

    return pl.pallas_call(
        aliased_body, name=name, grid=grid, in_specs=list(in_specs) + [pl.BlockSpec(memory_space=pl.ANY)] * len(stacks),
        out_specs=out_specs, out_shape=out_shape,
        input_output_aliases={n + k: len(plain_specs) + k for k in range(len(stacks))}, **kw)(*operands, *stacks)


def _inproj_dw(h, dproj, layer, stacks, *, ts, tn):
    S, D = h.shape
    nj = PM // tn

    def body(h_ref, dp_ref, dpf_ref, dw_ref, dwf_ref):
        s = pl.program_id(1)

        @pl.when(s == 0)
        def _():
            dw_ref[...] = jnp.zeros_like(dw_ref)

        @pl.when((s == 0) & (pl.program_id(0) == 0))
        def _():
            dwf_ref[...] = jnp.zeros_like(dwf_ref)

        hv = h_ref[...]
        dw_ref[...] += _dot_tn(dp_ref[...], hv)

        @pl.when(pl.program_id(0) == 0)
        def _():
            dwf_ref[...] += _dot_tn(dpf_ref[...], hv)

    return _stack_call(
        body, "inproj_dw", (nj, S // ts),
        [pl.BlockSpec((ts, D), lambda j, s: (s, 0)),
         pl.BlockSpec((ts, tn), lambda j, s: (s, j)),
         pl.BlockSpec((ts, LANES), lambda j, s: (s, PM // LANES))],
        (h, dproj, dproj),
        [pl.BlockSpec((None, tn, D), lambda j, s: (layer, j, 0)), pl.BlockSpec((None, LANES, D), lambda j, s: (layer, 0, 0))],
        [(PM, D), (LANES, D)], stacks,
        compiler_params=_cparams(dimension_semantics=("arbitrary", "arbitrary")))


def _inproj_dx(dproj, wt_all, layer, x, g, dy, *, tm):
    S, D = x.shape

    def body(dp_ref, w_ref, x_ref, g_ref, dy_ref, dx_ref, dg_ref):
        @pl.when(pl.program_id(0) == 0)
        def _():
            dg_ref[...] = jnp.zeros_like(dg_ref)

        dh = _dot(dp_ref[...], w_ref[...])
        xf = x_ref[...]
        rstd = lax.rsqrt(jnp.mean(xf * xf, axis=-1, keepdims=True) + EPS)
        xhat = xf * rstd
        dg_ref[...] += jnp.sum(dh * xhat, axis=0, keepdims=True)
        dyg = dh * g_ref[...]
        mean = jnp.mean(dyg * xhat, axis=-1, keepdims=True)
        dx_ref[...] = rstd * (dyg - xhat * mean) + dy_ref[...]

    row = lambda w: pl.BlockSpec((tm, w), lambda i: (i, 0))
    return pl.pallas_call(
        body, name="inproj_dx", grid=(S // tm,),
        in_specs=[row(PW), pl.BlockSpec((None, PW, D), lambda i: (layer, 0, 0)), row(D), pl.BlockSpec((1, D), lambda i: (0, 0)), row(D)],
        out_specs=[row(D), pl.BlockSpec((1, D), lambda i: (0, 0))],
        out_shape=[jax.ShapeDtypeStruct((S, D), F32), jax.ShapeDtypeStruct((1, D), F32)],
        compiler_params=_cparams(dimension_semantics=("arbitrary",)),
    )(dproj, wt_all, x, g, dy)


def _adam_update(w, g, m, v):
    nm = ADAM_B1 * m + (1.0 - ADAM_B1) * g
    nv = ADAM_B2 * v + (1.0 - ADAM_B2) * (g * g)
    m_hat = nm / (1.0 - ADAM_B1 ** ADAM_STEP)
    v_hat = nv / (1.0 - ADAM_B2 ** ADAM_STEP)
    return -ADAM_LR * (m_hat / (jnp.sqrt(v_hat) + ADAM_EPS) + ADAM_WD * w), nm, nv


def _adamw(w, g, m, v):
    L, R, C = w.shape
    tr = R if R <= 512 else 256

    def body(w_ref, g_ref, m_ref, v_ref, d_ref, nm_ref, nv_ref):
        d_ref[...], nm_ref[...], nv_ref[...] = _adam_update(w_ref[...], g_ref[...], m_ref[...], v_ref[...])

    spec = pl.BlockSpec((1, tr, C), lambda l, i: (l, i, 0))
    shp = jax.ShapeDtypeStruct((L, R, C), F32)
    return pl.pallas_call(
        body, name="adamw", grid=(L, R // tr), in_specs=[spec] * 4, out_specs=[spec] * 3, out_shape=[shp] * 3,
        compiler_params=_cparams(dimension_semantics=("arbitrary", "arbitrary")),
    )(w, g, m, v)


def _adamw_nd(w, g, m, v):
    shape = w.shape
    view = (1,) + shape if w.ndim == 2 else (shape[0], -1, shape[-1])
    outs = _adamw(w.reshape(view), g.reshape(view), m.reshape(view), v.reshape(view))
    return tuple(o.reshape(shape) for o in outs)


FLIP_C = (0, 0, 1)
FLIP_X = (1, 0, 0)
FLIP_Y = (0, 1, 0)
FLIP_XY = (1, 1, 0)
MESH = pl.DeviceIdType.MESH


def _peer(flip):
    me = (lax.axis_index("x"), lax.axis_index("y"), lax.axis_index("c"))
    return tuple(1 - a if f else a for a, f in zip(me, flip))


def _exchange(name, arrays, flips):
    n = len(arrays)

    def body(*refs):
        srcs, dsts = refs[:n], refs[n:2 * n]
        send_sems, recv_sems = refs[2 * n:]
        copies = [pltpu.make_async_remote_copy(src_ref=srcs[k], dst_ref=dsts[k], send_sem=send_sems.at[k], recv_sem=recv_sems.at[k],
                                               device_id=_peer(flips[k]), device_id_type=MESH) for k in range(n)]
        for cp in copies:
            cp.start()
        for cp in copies:
            cp.wait()

    anyspec = pl.BlockSpec(memory_space=pl.ANY)
    return pl.pallas_call(
        body, name=name, in_specs=[anyspec] * n, out_specs=[anyspec] * n,
        out_shape=[jax.ShapeDtypeStruct(a.shape, a.dtype) for a in arrays],
        scratch_shapes=[pltpu.SemaphoreType.DMA((n,)), pltpu.SemaphoreType.DMA((n,))],
    )(*arrays)


def _exchange_add(name, x, flip):
    def body(x_ref, o_ref, buf_ref, send_sem, recv_sem):
        cp = pltpu.make_async_remote_copy(src_ref=x_ref, dst_ref=buf_ref, send_sem=send_sem, recv_sem=recv_sem,
                                          device_id=_peer(flip), device_id_type=MESH)
        cp.start()
        cp.wait()
        o_ref[...] = x_ref[...] + buf_ref[...]

    vspec = pl.BlockSpec(memory_space=pltpu.VMEM)
    return pl.pallas_call(
        body, name=name, in_specs=[vspec], out_specs=vspec, out_shape=jax.ShapeDtypeStruct(x.shape, x.dtype),
        scratch_shapes=[pltpu.VMEM(x.shape, x.dtype), pltpu.SemaphoreType.DMA, pltpu.SemaphoreType.DMA],
    )(x)


def _chip_index():
    return 2 * lax.axis_index("x") + lax.axis_index("y")


def _gather_weights(w_in_t, w_out):
    wi = w_in_t.astype(BF16)
    wo = jnp.swapaxes(w_out, 0, 1).astype(BF16)
    halves = (wi.shape[0] // 2, wo.shape[0] // 2)
    ARR = 2
    TO_X, TO_Y, ON_Y, ON_X, SIB_X, SIB_Y, SIB_D0, SIB_D1, OWN = [ARR * k for k in range(9)]
    n_sems = ARR * 9

    def body(wi_ref, wo_ref, gi_ref, go_ref, send_sems, recv_sems):
        c = lax.axis_index("c")
        j = _chip_index()
        srcs = (wi_ref, wo_ref)
        dsts = (gi_ref, go_ref)
        def cuts(core):
            return [(pl.ds(h * core, h), pl.ds(h * core, h // 2), pl.ds(h * core + h // 2, h - h // 2)) for h in halves]
        mine, theirs = cuts(c), cuts(1 - c)
        HALF, Q0, Q1 = 0, 1, 2

        def copy(idx, src, dst, flip):
            return pltpu.make_async_remote_copy(src_ref=src, dst_ref=dst, send_sem=send_sems.at[idx], recv_sem=recv_sems.at[idx],
                                                device_id=_peer(flip), device_id_type=MESH)

        def slot(a, shard, cut):
            return dsts[a].at[shard, cut]

        jx, jy, jd = j ^ 2, j ^ 1, j ^ 3
        sends = []

        def start(cp):
            cp.start()
            sends.append(cp)

        for a in range(ARR):
            start(copy(TO_X + a, srcs[a].at[mine[a][HALF]], slot(a, j, mine[a][HALF]), FLIP_X))
            start(copy(TO_Y + a, srcs[a].at[mine[a][HALF]], slot(a, j, mine[a][HALF]), FLIP_Y))
        own = [copy(OWN + a, srcs[a], dsts[a].at[j], FLIP_C) for a in range(ARR)]
        for cp in own:
            cp.start()
        for a in range(ARR):
            copy(TO_X + a, slot(a, jx, mine[a][HALF]), slot(a, jx, mine[a][HALF]), FLIP_X).wait_recv()
            start(copy(ON_Y + a, slot(a, jx, mine[a][Q0]), slot(a, jx, mine[a][Q0]), FLIP_Y))
            start(copy(SIB_X + a, slot(a, jx, mine[a][HALF]), slot(a, jx, mine[a][HALF]), FLIP_C))
        for a in range(ARR):
            copy(TO_Y + a, slot(a, jy, mine[a][HALF]), slot(a, jy, mine[a][HALF]), FLIP_Y).wait_recv()
            start(copy(ON_X + a, slot(a, jy, mine[a][Q1]), slot(a, jy, mine[a][Q1]), FLIP_X))
            start(copy(SIB_Y + a, slot(a, jy, mine[a][HALF]), slot(a, jy, mine[a][HALF]), FLIP_C))
        for a in range(ARR):
            copy(ON_Y + a, slot(a, jd, mine[a][Q0]), slot(a, jd, mine[a][Q0]), FLIP_Y).wait_recv()
            start(copy(SIB_D0 + a, slot(a, jd, mine[a][Q0]), slot(a, jd, mine[a][Q0]), FLIP_C))
        for a in range(ARR):
            copy(ON_X + a, slot(a, jd, mine[a][Q1]), slot(a, jd, mine[a][Q1]), FLIP_X).wait_recv()
            start(copy(SIB_D1 + a, slot(a, jd, mine[a][Q1]), slot(a, jd, mine[a][Q1]), FLIP_C))
        for a in range(ARR):
            for idx, shard, cut in ((SIB_X, jx, HALF), (SIB_Y, jy, HALF), (SIB_D0, jd, Q0), (SIB_D1, jd, Q1)):
                copy(idx + a, slot(a, shard, theirs[a][cut]), slot(a, shard, theirs[a][cut]), FLIP_C).wait_recv()
        for cp in own:
            cp.wait()
        for cp in sends:
            cp.wait_send()

    anyspec = pl.BlockSpec(memory_space=pl.ANY)
    gi, go = pl.pallas_call(
        body, name="gather_weights", in_specs=[anyspec] * 2, out_specs=[anyspec] * 2,
        out_shape=[jax.ShapeDtypeStruct((4,) + wi.shape, BF16), jax.ShapeDtypeStruct((4,) + wo.shape, BF16)],
        scratch_shapes=[pltpu.SemaphoreType.DMA((n_sems,)), pltpu.SemaphoreType.DMA((n_sems,))],
    )(wi, wo)
    w_in_t_full = gi.reshape((4 * wi.shape[0],) + wi.shape[1:])
    w_out_full = jnp.swapaxes(go.reshape((4 * wo.shape[0],) + wo.shape[1:]), 0, 1)
    return w_in_t_full, w_out_full


def _to_aligned(w_t):
    _, L, D = w_t.shape
    npair = FOX_HEADS // 2
    ff = w_t[ORIG_FF:ORIG_REST].reshape(npair, 2, L, D)
    ff = jnp.pad(ff, ((0, 0), (0, FF_STRIDE - 2), (0, 0), (0, 0))).reshape(npair * FF_STRIDE, L, D)
    ff = jnp.pad(ff, ((0, LANES - npair * FF_STRIDE), (0, 0), (0, 0)))
    return jnp.swapaxes(jnp.concatenate([w_t[:ORIG_FOX], w_t[ORIG_REST:], ff], axis=0), 0, 1)


def _from_aligned(dw_t):
    n, _, D = dw_t.shape
    npair = FOX_HEADS // 2
    ff = dw_t[:, PM:PM + npair * FF_STRIDE].reshape(n, npair, FF_STRIDE, D)[:, :, :2].reshape(n, FOX_HEADS, D)
    return jnp.swapaxes(jnp.concatenate([dw_t[:, :ORIG_FOX], ff, dw_t[:, ORIG_FOX:PM]], axis=1), 0, 1)


RELAY_ROWS = 256


def _rows_first(m, f):
    n, _, D = m.shape
    npair = FOX_HEADS // 2
    first_late = ORIG_FOX // RELAY_ROWS

    def body(m_ref, f_ref, out_ref, buf_ref, ff_ref, sem, ff_sem):
        i = pl.program_id(0)
        for l in range(n):
            buf_ref[:, l, :] = m_ref[l]
        start = pl.multiple_of(i * RELAY_ROWS, FOX_HEADS) + jnp.where(i >= first_late, FOX_HEADS, 0)
        main = pltpu.make_async_copy(buf_ref, out_ref.at[pl.ds(start, RELAY_ROWS)], sem)
        main.start()

        @pl.when(i == 0)
        def _():
            for l in range(n):
                for p in range(npair):
                    ff_ref[2 * p:2 * p + 2, l, :] = f_ref[l, FF_STRIDE * p:FF_STRIDE * p + 2, :]
            ff = pltpu.make_async_copy(ff_ref, out_ref.at[pl.ds(ORIG_FF, FOX_HEADS)], ff_sem)
            ff.start()
            ff.wait()

        main.wait()

    return pl.pallas_call(
        body, name="rs_rows_first", grid=(PM // RELAY_ROWS,),
        in_specs=[pl.BlockSpec((n, RELAY_ROWS, D), lambda i: (0, i, 0)), pl.BlockSpec((n, LANES, D), lambda i: (0, 0, 0))],
        out_specs=pl.BlockSpec(memory_space=pl.ANY), out_shape=jax.ShapeDtypeStruct((D_IN, n, D), F32),
        scratch_shapes=[pltpu.VMEM((RELAY_ROWS, n, D), F32), pltpu.VMEM((FOX_HEADS, n, D), F32),
                        pltpu.SemaphoreType.DMA, pltpu.SemaphoreType.DMA],
        compiler_params=_cparams(dimension_semantics=("arbitrary",)),
    )(m, f)


def _hand_sibling_its_layers(stacks):
    n = len(stacks)
    half = DEPTH // 2

    def body(*refs):
        srcs, dsts = refs[:n], refs[n:2 * n]
        send_sems, recv_sems = refs[2 * n:]
        theirs = pl.ds(half * (1 - lax.axis_index("c")), half)
        copies = [pltpu.make_async_remote_copy(src_ref=srcs[k].at[theirs], dst_ref=dsts[k], send_sem=send_sems.at[k],
                                               recv_sem=recv_sems.at[k], device_id=_peer(FLIP_C), device_id_type=MESH) for k in range(n)]
        for cp in copies:
            cp.start()
        for cp in copies:
            cp.wait()

    anyspec = pl.BlockSpec(memory_space=pl.ANY)
    return pl.pallas_call(
        body, name="rs_d2d", in_specs=[anyspec] * n, out_specs=[anyspec] * n,
        out_shape=[jax.ShapeDtypeStruct((half,) + s.shape[1:], s.dtype) for s in stacks],
        scratch_shapes=[pltpu.SemaphoreType.DMA((n,)), pltpu.SemaphoreType.DMA((n,))],
    )(*stacks)


def _half_layers(name, stack, got, also_bf16=True):
    L, R, C = stack.shape
    half = L // 2
    tr = min(256, R)

    def body(c_ref, x_ref, g_ref, *out_refs):
        acc = x_ref[...] + g_ref[...]
        out_refs[0][...] = acc
        if also_bf16:
            out_refs[1][...] = acc.astype(BF16)

    plain = pl.BlockSpec((1, tr, C), lambda l, i, c_ref: (l, i, 0))
    picked = pl.BlockSpec((1, tr, C), lambda l, i, c_ref: (c_ref[0] * half + l, i, 0))
    shp = lambda dt: jax.ShapeDtypeStruct((half, R, C), dt)
    out_shape = [shp(F32)] + ([shp(BF16)] if also_bf16 else [])
    grid_spec = pltpu.PrefetchScalarGridSpec(
        num_scalar_prefetch=1, grid=(half, R // tr), in_specs=[picked, plain], out_specs=[plain] * len(out_shape))
    return pl.pallas_call(
        body, name=name, grid_spec=grid_spec, out_shape=out_shape,
        compiler_params=_cparams(dimension_semantics=("arbitrary", "arbitrary")),
    )(lax.axis_index("c").astype(jnp.int32).reshape(1), stack, got)


def _reduce_scatter(stack_m, stack_f, stack_o, shard_cols, shard_rows):
    j = _chip_index()
    half = DEPTH // 2
    stacks = (stack_m, stack_f, stack_o)
    got = _hand_sibling_its_layers(stacks)
    (m32,), (f32_,) = [_half_layers("rs_add_chip", s, g, also_bf16=False) for s, g in zip(stacks[:2], got[:2])]
    o32, obf = _half_layers("rs_add_chip", stack_o, got[2])
    d_model = stack_m.shape[2]
    in32 = _rows_first(m32, f32_).reshape(4, shard_cols, half, d_model)

    def out_shards(o):
        return jnp.moveaxis(o.reshape(half, 4, shard_rows, o.shape[-1]), 1, 0)

    chip = [(in32, in32.astype(BF16), 0), (out_shards(o32), out_shards(obf), 1)]
    shard = lambda a, idx: lax.dynamic_index_in_dim(a, idx, axis=0, keepdims=False)
    via = []
    for _, bf, axis in chip:
        diag = shard(bf, j ^ 3)
        cut = diag.shape[axis] // 2
        via += [lax.slice_in_dim(diag, 0, cut, axis=axis), lax.slice_in_dim(diag, cut, 2 * cut, axis=axis)]
    handed = _exchange("rs_via", via, (FLIP_X, FLIP_Y) * len(chip))
    sends = []
    for a, (f32_sum, _, axis) in enumerate(chip):
        sends.append(_add_half_along("rs_add_via", f32_sum, handed[2 * a + 1], axis, 1, pick=j ^ 2))
        sends.append(_add_half_along("rs_add_via", f32_sum, handed[2 * a], axis, 0, pick=j ^ 1))
    got = _exchange("rs_ici", sends, (FLIP_X, FLIP_Y) * len(chip))
    mine_in = _add_rows("rs_add_in", chip[0][0], list(got[0:2]), pick=j)
    mine_out = _add_into_half("rs_add_out", shard(chip[1][0], j), list(got[2:4]))
    sib_in, g_out = _share_halves(mine_in, mine_out)
    return (mine_in, sib_in), g_out


def _picked(spec, pick):
    return pl.BlockSpec((None,) + tuple(spec.block_shape), lambda *a: (a[-1][0],) + tuple(spec.index_map(*a[:-1])))


def _add_half_along(name, base, extra, axis, which, pick=None):
    shape = base.shape if pick is None else base.shape[1:]
    lanes = min(ROW_LANE_CHUNK, shape[2])
    assert shape[axis] == 2 * extra.shape[axis]
    blk = tuple(shape[d] // 2 if d == axis else shape[d] for d in range(2)) + (lanes,)

    def body(*refs):
        b_ref, e_ref, o_ref = refs[-3:]
        x = b_ref[...]
        o_ref[...] = jnp.where(pl.program_id(0) == which, x + e_ref[...].astype(F32), x).astype(BF16)

    at = lambda i, k, *_: (i, 0, k) if axis == 0 else (0, i, k)
    bspec, espec = pl.BlockSpec(blk, at), pl.BlockSpec(blk, lambda i, k, *_: (0, 0, k))
    kw = dict(out_shape=jax.ShapeDtypeStruct(shape, BF16), name=name, compiler_params=_cparams(dimension_semantics=("arbitrary", "arbitrary")))
    grid = (2, shape[2] // lanes)
    if pick is None:
        return pl.pallas_call(body, grid=grid, in_specs=[bspec, espec], out_specs=bspec, **kw)(base, extra)
    grid_spec = pltpu.PrefetchScalarGridSpec(num_scalar_prefetch=1, grid=grid, in_specs=[_picked(bspec, pick), espec], out_specs=bspec)
    return pl.pallas_call(body, grid_spec=grid_spec, **kw)(pick.astype(jnp.int32).reshape(1), base, extra)


def _add_rows(name, first, others, pick=None):
    n = len(others)
    shape = first.shape if pick is None else first.shape[1:]

    def body(*refs):
        refs = refs[-(n + 2):]
        acc = refs[0][...]
        for r in refs[1:1 + n]:
            acc = acc + r[...].astype(F32)
        refs[1 + n][...] = acc

    grid, spec = _row_lane_blocks(shape)
    sp = spec(shape[1])
    kw = dict(out_shape=jax.ShapeDtypeStruct(shape, F32), name=name, compiler_params=_cparams(dimension_semantics=("arbitrary", "arbitrary")))
    if pick is None:
        return pl.pallas_call(body, grid=grid, in_specs=[sp] * (1 + n), out_specs=sp, **kw)(first, *others)
    grid_spec = pltpu.PrefetchScalarGridSpec(num_scalar_prefetch=1, grid=grid, in_specs=[_picked(sp, pick)] + [sp] * n, out_specs=sp)
    return pl.pallas_call(body, grid_spec=grid_spec, **kw)(pick.astype(jnp.int32).reshape(1), first, *others)


ROW_LANE_CHUNK = 256


def _row_lane_blocks(shape):
    rows, _, C = shape
    tr = rows // 2 if rows % 2 == 0 and rows > 64 else rows
    lanes = min(ROW_LANE_CHUNK, C)
    return (rows // tr, C // lanes), lambda n_mid: pl.BlockSpec((tr, n_mid, lanes), lambda i, k, *_: (i, 0, k))


def _add_into_half(name, first, others):
    half, rows, C = first.shape
    tr = min(256, rows)
    n = len(others)

    def body(c_ref, *refs):
        acc = refs[0][...]
        for r in refs[1:1 + n]:
            acc = acc + r[...].astype(F32)
        refs[1 + n][...] = acc

    grid_spec = pltpu.PrefetchScalarGridSpec(
        num_scalar_prefetch=1, grid=(half, rows // tr),
        in_specs=[pl.BlockSpec((1, tr, C), lambda l, i, c_ref: (l, i, 0))] * (1 + n),
        out_specs=pl.BlockSpec((1, tr, C), lambda l, i, c_ref: (c_ref[0] * half + l, i, 0)))
    return pl.pallas_call(
        body, name=name, grid_spec=grid_spec, out_shape=jax.ShapeDtypeStruct((2 * half, rows, C), F32),
        compiler_params=_cparams(dimension_semantics=("arbitrary", "arbitrary")),
    )(lax.axis_index("c").astype(jnp.int32).reshape(1), first, *others)


def _share_halves(mine, buf):
    half = DEPTH // 2

    def body(mine_ref, buf_in, sib_ref, buf_ref, send_sems, recv_sems):
        lay = pl.ds(half * lax.axis_index("c"), half)
        copies = [pltpu.make_async_remote_copy(src_ref=src, dst_ref=dst, send_sem=send_sems.at[k], recv_sem=recv_sems.at[k],
                                               device_id=_peer(FLIP_C), device_id_type=MESH)
                  for k, (src, dst) in enumerate(((mine_ref, sib_ref), (buf_ref.at[lay], buf_ref.at[lay])))]
        for cp in copies:
            cp.start()
        for cp in copies:
            cp.wait()

    anyspec = pl.BlockSpec(memory_space=pl.ANY)
    return pl.pallas_call(
        body, name="rs_share", in_specs=[anyspec] * 2, out_specs=[anyspec] * 2,
        out_shape=[jax.ShapeDtypeStruct(mine.shape, mine.dtype), jax.ShapeDtypeStruct(buf.shape, buf.dtype)],
        input_output_aliases={1: 1},
        scratch_shapes=[pltpu.SemaphoreType.DMA((2,)), pltpu.SemaphoreType.DMA((2,))],
    )(mine, buf)


def _adamw_halves(w, g_mine, g_sib, m, v):
    half = g_mine.shape[1]

    def body(c_ref, w_ref, gm_ref, gs_ref, m_ref, v_ref, g_ref, d_ref, nm_ref, nv_ref):
        first = c_ref[0] == 0
        gm, gs = gm_ref[...], gs_ref[...]
        for h, gv in enumerate((jnp.where(first, gm, gs), jnp.where(first, gs, gm))):
            lay = slice(half * h, half * (h + 1))
            g_ref[:, lay, :] = gv
            d_ref[:, lay, :], nm_ref[:, lay, :], nv_ref[:, lay, :] = _adam_update(w_ref[:, lay, :], gv, m_ref[:, lay, :], v_ref[:, lay, :])

    grid, spec = _row_lane_blocks(w.shape)
    full, part = spec(w.shape[1]), spec(half)
    grid_spec = pltpu.PrefetchScalarGridSpec(num_scalar_prefetch=1, grid=grid, in_specs=[full, part, part, full, full], out_specs=[full] * 4)
    return pl.pallas_call(
        body, name="adamw_halves", grid_spec=grid_spec, out_shape=[jax.ShapeDtypeStruct(w.shape, F32)] * 4,
        compiler_params=_cparams(dimension_semantics=("arbitrary", "arbitrary")),
    )(lax.axis_index("c").astype(jnp.int32).reshape(1), w, g_mine, g_sib, m, v)


def _all_reduce_small(x):
    x = _exchange_add("ar_c", x, FLIP_C)
    x = _exchange_add("ar_y", x, FLIP_Y)
    return _exchange_add("ar_x", x, FLIP_X)


def _blocks(S):
    return dict(tm=min(512, S), tm_proj=min(1024, S), ts=min(512, S), tq=min(512, S), tq_big=min(1024, S), tk=min(512, S), tks=min(256, S))


def _pair_pad(vec):
    npair = FOX_HEADS // 2
    v = jnp.pad(vec.reshape(npair, 2), ((0, 0), (0, FF_STRIDE - 2))).reshape(1, npair * FF_STRIDE)
    return jnp.pad(v, ((0, 0), (0, LANES - npair * FF_STRIDE)))


def _pair_unpad(row):
    npair = FOX_HEADS // 2
    return row[0, :npair * FF_STRIDE].reshape(npair, FF_STRIDE)[:, :2].reshape(FOX_HEADS)


def _pool_blockdiag(w_pool):
    g, cg, _ = w_pool.shape
    eye = jnp.eye(g, dtype=w_pool.dtype)
    return jnp.einsum("gh,gcd->gchd", eye, w_pool).reshape(g * cg, g * cg)


QK_BOUND_SLACK = 1.05


def _layer_params(norm_g, b_f, q_norm_g, k_norm_g, w_pool, pool_scale):
    qk_bound = QK_BOUND_SLACK * HEAD_DIM * QK_SCALE * jnp.max(jnp.abs(q_norm_g)) * jnp.max(jnp.abs(k_norm_g))
    return dict(g=norm_g.reshape(1, -1), qg=jnp.tile(q_norm_g, FOX_HEADS).reshape(1, FOX_W), kg=jnp.tile(k_norm_g, FOX_HEADS).reshape(1, FOX_W),
                bfp=_pair_pad(b_f), wpd=_pool_blockdiag(w_pool).astype(BF16), ps=pool_scale.reshape(1, POOL_W),
                qkb=jnp.full((1, LANES), qk_bound, F32))


def _layer_fwd(x, wt_all, w_out, layer, prm, bs):
    projm, ffo, h = _inproj(x, prm["g"], wt_all, layer, tm=bs["tm_proj"], tn=PROJ_TN)
    qn, ka, kb, v, sq, sk, sv, pooled, yp, pm = _prep(projm, ffo, prm["qg"], prm["kg"], prm["bfp"], prm["wpd"], prm["ps"], ts=bs["ts"])
    o, lse, fm = _fox_fwd(qn, ka, kb, v, projm, prm["qkb"], tq=bs["tq"], tk=bs["tk"])
    so, sm = _sb_fwd(sq, sk, sv, projm, tq=bs["tq"], tk=bs["tks"])
    y = _outproj(x, fm, pm, sm, w_out, layer, tm=bs["tm_proj"])
    saved = dict(x=x, projm=projm, ffo=ffo, h=h, qn=qn, ka=ka, kb=kb, v=v, sq=sq, sk=sk, sv=sv, pooled=pooled, yp=yp,
                 o=o, lse=lse, so=so, fm=fm, pm=pm, sm=sm)
    return y, saved


def _layer_bwd(dy, wt_all, w_out, prm, sv_, bs, layer, stacks):
    dmix, stack_o = _outproj_bwd(dy, sv_["fm"], sv_["pm"], sv_["sm"], w_out, layer, None if stacks is None else stacks[2:], tm=bs["tm_proj"])
    dqn, dkn, dv, dfg, dct, dcr = _fox_bwd(sv_["qn"], sv_["ka"], sv_["kb"], sv_["v"], sv_["o"], sv_["lse"], dmix, sv_["projm"],
                                      prm["qkb"], tq=bs["tq_big"], tk=bs["tk"])
    dsq, dsk, dsv, dsg = _sb_bwd(sv_["sq"], sv_["sk"], sv_["sv"], sv_["so"], dmix, sv_["projm"], tq=bs["tks"], tk=bs["tks"])
    dproj, dqg, dkg, dbf, dwp, dps = _prep_bwd(sv_["projm"], sv_["ffo"], dqn, dkn, dct, dcr, dv, dfg, dsq, dsk, dsv, dsg, dmix,
                                               sv_["pooled"], sv_["yp"], prm["qg"], prm["kg"], prm["bfp"], prm["wpd"], prm["ps"], ts=bs["ts"])
    stack_m, stack_f = _inproj_dw(sv_["h"], dproj, layer, None if stacks is None else stacks[:2], ts=bs["tm_proj"], tn=PROJ_TN)
    dx, dg = _inproj_dx(dproj, wt_all, layer, sv_["x"], prm["g"], dy, tm=bs["tm"])
    grads = dict(
        norm_g=dg[0],
        b_f=_pair_unpad(dbf), q_norm_g=dqg.reshape(FOX_HEADS, HEAD_DIM).sum(0), k_norm_g=dkg.reshape(FOX_HEADS, HEAD_DIM).sum(0),
        w_pool=jnp.stack([dwp[HEAD_DIM * g:HEAD_DIM * (g + 1), HEAD_DIM * g:HEAD_DIM * (g + 1)] for g in range(4)]),
        pool_scale=dps[0])
    return dx, grads, (stack_m, stack_f, stack_o)


def _local_step(x, target, wt_all, w_out, norm_g, b_f, q_norm_g, k_norm_g, w_pool, pool_scale):
    S, D = x.shape
    bs = _blocks(S)
    prms = [_layer_params(norm_g[l], b_f[l], q_norm_g[l], k_norm_g[l], w_pool[l], pool_scale[l]) for l in range(DEPTH)]
    saved = []
    y = x
    for l in range(DEPTH):
        y, s_ = _layer_fwd(y, wt_all, w_out, l, prms[l], bs)
        saved.append(s_)
    dy, sq = _loss_head(y, target, tm=bs["tm"])
    loss = 0.5 * jnp.sum(sq) / D
    grads = [None] * DEPTH
    stacks = None
    for l in reversed(range(DEPTH)):
        dy, grads[l], stacks = _layer_bwd(dy, wt_all, w_out, prms[l], saved[l], bs, l, stacks)
    stacked = {k: jnp.stack([g[k] for g in grads]) for k in grads[0]}
    return loss, dy, stacked, stacks


SMALL = ("norm_g", "b_f", "q_norm_g", "k_norm_g", "w_pool", "pool_scale")


def _pack_small(gr):
    flat = jnp.concatenate([gr[k].reshape(-1) for k in SMALL])
    pad = (-flat.shape[0]) % (8 * LANES)
    return jnp.pad(flat, (0, pad)).reshape(-1, LANES)


def _unpack_small(packed, like):
    flat = packed.reshape(-1)
    out, off = {}, 0
    for k in SMALL:
        n = like[k].size
        out[k] = flat[off:off + n].reshape(like[k].shape)
        off += n
    return out


def kernel(x, norm_g, w_in, b_f, q_norm_g, k_norm_g, w_pool, pool_scale, w_out, loss_target, m_norm_g, m_w_in, m_b_f, m_q_norm_g, m_k_norm_g, m_w_pool, m_pool_scale, m_w_out, v_norm_g, v_w_in, v_b_f, v_q_norm_g, v_k_norm_g, v_w_pool, v_pool_scale, v_w_out):
    weights = dict(norm_g=norm_g, w_in=w_in, b_f=b_f, q_norm_g=q_norm_g, k_norm_g=k_norm_g, w_pool=w_pool, pool_scale=pool_scale, w_out=w_out)
    mom_m = dict(norm_g=m_norm_g, w_in=m_w_in, b_f=m_b_f, q_norm_g=m_q_norm_g, k_norm_g=m_k_norm_g, w_pool=m_w_pool, pool_scale=m_pool_scale, w_out=m_w_out)
    mom_v = dict(norm_g=v_norm_g, w_in=v_w_in, b_f=v_b_f, q_norm_g=v_q_norm_g, k_norm_g=v_k_norm_g, w_pool=v_w_pool, pool_scale=v_pool_scale, w_out=v_w_out)
    shard_cols = w_in.shape[2]
    shard_rows = w_out.shape[1]

    cols_first = lambda a: jnp.transpose(a, (2, 0, 1))
    w_in_t = cols_first(w_in)
    w_in_t_full, w_out_full = _gather_weights(w_in_t, w_out)
    wt_all = _to_aligned(w_in_t_full)
    loss, dx, gr, stacks = _local_step(x[0], loss_target[0], wt_all, w_out_full, norm_g, b_f, q_norm_g, k_norm_g, w_pool, pool_scale)
    loss = lax.psum(loss, ("x", "y", "c"))

    (g_in_mine, g_in_sib), g_w_out = _reduce_scatter(*stacks, shard_cols, shard_rows)
    small = _unpack_small(_all_reduce_small(_pack_small(gr)), {k: weights[k] for k in SMALL})
    grad_w = dict(small, w_out=g_w_out)

    names = ("norm_g", "w_in", "b_f", "q_norm_g", "k_norm_g", "w_pool", "pool_scale", "w_out")
    upd = {k: _adamw_nd(weights[k], grad_w[k], mom_m[k], mom_v[k]) for k in names if k != "w_in"}
    in_t = _adamw_halves(w_in_t, g_in_mine, g_in_sib, cols_first(mom_m["w_in"]), cols_first(mom_v["w_in"]))
    grad_w["w_in"], *upd["w_in"] = [jnp.transpose(a, (1, 2, 0)) for a in in_t]
    return (loss, dx[None], *[grad_w[k] for k in names], *[upd[k][0] for k in names], *[upd[k][1] for k in names], *[upd[k][2] for k in names])
```

```python
import functools

import jax
import jax.numpy as jnp
from jax import lax
from jax.experimental import pallas as pl
from jax.experimental.pallas import tpu as pltpu

F32 = jnp.float32
BF16 = jnp.bfloat16

DEPTH = 4
HEAD_DIM = 64
FOX_HEADS = 8
SB_HEADS = 4
FOX_W = FOX_HEADS * HEAD_DIM
SB_W = SB_HEADS * HEAD_DIM
POOL_W = 256
POOL_WINDOWS = (2, 4, 8, 16)
POOL_HALO = 16
D_MIX = FOX_W + POOL_W + SB_W
EPS = 1e-6
NEG = -1e30
QK_SCALE = HEAD_DIM ** -0.5

ORIG_FOX = 4 * FOX_W
ORIG_FF = ORIG_FOX
ORIG_REST = ORIG_FF + FOX_HEADS
D_IN = ORIG_REST + 2 * POOL_W + 4 * SB_W

C_FQ, C_FK, C_FV, C_FG = 0, FOX_W, 2 * FOX_W, 3 * FOX_W
C_PX = 4 * FOX_W
C_PG = C_PX + POOL_W
C_SQ = C_PG + POOL_W
C_SK, C_SV, C_SG = C_SQ + SB_W, C_SQ + 2 * SB_W, C_SQ + 3 * SB_W
PM = C_SG + SB_W
LANES = 128
LANE_SHIFT = 7
HEAD_SHIFT = 6
PW = PM + LANES
FF_STRIDE = 8
AUG = 3

ADAM_LR = 0.001
ADAM_B1 = 0.9
ADAM_B2 = 0.999
ADAM_EPS = 1e-08
ADAM_WD = 0.01
ADAM_STEP = 10

VMEM_LIMIT = 48 * 1024 * 1024
PROJ_TN = PM // 2


def _cparams(**kw):
    return pltpu.CompilerParams(vmem_limit_bytes=VMEM_LIMIT, **kw)


def _dot(a, b):
    return jnp.dot(a, b, preferred_element_type=F32)


def _dot_nt(a, b):
    return lax.dot_general(a, b, (((1,), (1,)), ((), ())), preferred_element_type=F32)


def _dot_tn(a, b):
    return lax.dot_general(a, b, (((0,), (0,)), ((), ())), preferred_element_type=F32)


def _split2(x):
    hi = x.astype(BF16)
    lo = (x - hi.astype(F32)).astype(BF16)
    return hi, lo


def _split3(x):
    hi = x.astype(BF16)
    r = x - hi.astype(F32)
    mid = r.astype(BF16)
    lo = (r - mid.astype(F32)).astype(BF16)
    return hi, mid, lo


def _dot_exact_rhs(x, m):
    hi, mid, lo = _split3(x)
    return _dot(hi, m) + _dot(mid, m) + _dot(lo, m)


def _dot_exact_lhs(m, x):
    hi, mid, lo = _split3(x)
    return _dot(m, hi) + _dot(m, mid) + _dot(m, lo)


def _sigmoid(x):
    return 1.0 / (1.0 + jnp.exp(-x))


def _silu_pair(x):
    s = _sigmoid(x)
    return x * s, s * (1.0 + x * (1.0 - s))


def _iota(shape, dim):
    return lax.broadcasted_iota(jnp.int32, shape, dim)


def _ones_where(cond):
    return jnp.where(cond, 1.0, 0.0).astype(BF16)


GROUP_SLAB = 256


def _head_blockdiag():
    rows, cols = _iota((2 * GROUP_SLAB, GROUP_SLAB), 0) & (GROUP_SLAB - 1), _iota((2 * GROUP_SLAB, GROUP_SLAB), 1)
    return _ones_where((rows >> HEAD_SHIFT) == (cols >> HEAD_SHIFT))


def _group_sum(x, bd):
    hi, lo = _split2(x)
    slabs = [_dot(jnp.concatenate([hi[:, s:s + GROUP_SLAB], lo[:, s:s + GROUP_SLAB]], axis=1), bd) for s in range(0, x.shape[1], GROUP_SLAB)]
    return jnp.concatenate(slabs, axis=1)


def _lane_pick(x, lane_idx, lane):
    return jnp.sum(jnp.where(lane_idx == lane, x, 0.0), axis=1, keepdims=True)


def _inproj(x, g, wt_all, layer, *, tm, tn):
    S, D = x.shape
    nj = PM // tn

    def body(x_ref, g_ref, w_ref, wff_ref, proj_ref, ff_ref, h_ref):
        @pl.when(pl.program_id(1) == 0)
        def _():
            xf = x_ref[...]
            ms = jnp.mean(xf * xf, axis=-1, keepdims=True)
            h = (xf * lax.rsqrt(ms + EPS) * g_ref[...]).astype(BF16)
            h_ref[...] = h
            ff_ref[...] = _dot_nt(h, wff_ref[...])

        proj_ref[...] = _dot_nt(h_ref[...], w_ref[...])

    return pl.pallas_call(
        body, name="inproj", grid=(S // tm, nj),
        in_specs=[pl.BlockSpec((tm, D), lambda i, j: (i, 0)),
                  pl.BlockSpec((1, D), lambda i, j: (0, 0)),
                  pl.BlockSpec((None, tn, D), lambda i, j: (layer, j, 0)),
                  pl.BlockSpec((None, LANES, D), lambda i, j: (layer, PM // LANES, 0))],
        out_specs=[pl.BlockSpec((tm, tn), lambda i, j: (i, j)),
                   pl.BlockSpec((tm, LANES), lambda i, j: (i, 0)),
                   pl.BlockSpec((tm, D), lambda i, j: (i, 0))],
        out_shape=[jax.ShapeDtypeStruct((S, PM), F32), jax.ShapeDtypeStruct((S, LANES), F32),
                   jax.ShapeDtypeStruct((S, D), BF16)],
        compiler_params=_cparams(dimension_semantics=("arbitrary", "arbitrary")),
    )(x, g, wt_all, wt_all)


def _pool_group_select(lane_group, vals):
    return jnp.where(lane_group == 0, vals[0], jnp.where(lane_group == 1, vals[1], jnp.where(lane_group == 2, vals[2], vals[3])))


def _prep(projm, ffo, qg, kg, bfp, wpd, ps, *, ts):
    S = projm.shape[0]
    nb = S // ts
    hb = ts // POOL_HALO

    def body(fq_ref, fk_ref, fv_ref, pp_ref, halo_ref, ff_ref, sq_ref, sk_ref, sv_ref,
             qg_ref, kg_ref, bf_ref, wpd_ref, ps_ref,
             qn_ref, ka_ref, kb_ref, v_ref, sqo_ref, sko_ref, svo_ref, pooled_ref, yp_ref, pm_ref,
             carry_ref, c_ref, buf_ref):
        i = pl.program_id(0)
        bd = _head_blockdiag()
        normed = []
        for src, g_ref in ((fq_ref, qg_ref), (fk_ref, kg_ref)):
            q = src[...]
            ss = _group_sum(q * q, bd)
            normed.append(q * lax.rsqrt(ss * (1.0 / HEAD_DIM) + EPS) * g_ref[...])
        qn_ref[...] = (normed[0] * QK_SCALE).astype(BF16)
        kn = normed[1]
        v_ref[...] = fv_ref[...].astype(BF16)
        sqo_ref[...] = (sq_ref[...] * QK_SCALE).astype(BF16)
        sko_ref[...] = sk_ref[...].astype(BF16)
        svo_ref[...] = sv_ref[...].astype(BF16)

        @pl.when(i == 0)
        def _():
            carry_ref[...] = jnp.zeros_like(carry_ref)

        z = ff_ref[...] + bf_ref[...]
        lf = jnp.minimum(z, 0.0) - jnp.log(1.0 + jnp.exp(-jnp.abs(z)))
        tri = _ones_where(_iota((ts, ts), 1) <= _iota((ts, ts), 0))
        c = _dot_exact_lhs(tri, lf) + carry_ref[...]
        c_ref[...] = c
        carry_ref[...] = c_ref[ts - 1:ts, :]
        parts = jnp.concatenate(_split3(-c), axis=1)
        row = _iota((AUG * LANES, FOX_W), 0)
        col = _iota((AUG * LANES, FOX_W), 1)
        part, src = row >> LANE_SHIFT, row & (LANES - 1)
        pair, off = col >> LANE_SHIFT, col & (LANES - 1)
        sel_a = _ones_where((src == FF_STRIDE * pair) & (off == HEAD_DIM + part))
        sel_b = _ones_where((src == FF_STRIDE * pair + 1) & (off == part))
        first_half = (_iota((1, FOX_W), 1) & HEAD_DIM) == 0
        ka_ref[...] = jnp.where(first_half, kn, _dot(parts, sel_a)).astype(BF16)
        kb_ref[...] = jnp.where(first_half, _dot(parts, sel_b), kn).astype(BF16)

        x = pp_ref[:, 0:POOL_W]
        pg = pp_ref[:, POOL_W:2 * POOL_W]
        halo = jnp.where(i > 0, halo_ref[:, 0:POOL_W], 0.0)
        buf_ref[0:POOL_HALO, :] = halo
        buf_ref[POOL_HALO:POOL_HALO + ts, :] = x
        acc = x
        snaps = []
        for d in range(1, POOL_HALO):
            acc = acc + buf_ref[pl.ds(POOL_HALO - d, ts), :]
            if d + 1 in POOL_WINDOWS:
                snaps.append(acc)
        lane_group = _iota((1, POOL_W), 1) >> HEAD_SHIFT
        wsum = _pool_group_select(lane_group, snaps)
        wlen = _pool_group_select(lane_group, [float(w) for w in POOL_WINDOWS])
        tpos = (i * ts + _iota((ts, 1), 0) + 1).astype(F32)
        pooled = wsum / jnp.minimum(tpos, wlen) - x
        pb = pooled.astype(BF16)
        pooled_ref[...] = pb
        yp = _dot(pb, wpd_ref[...])
        yp_ref[...] = yp
        pm_ref[...] = (yp * ps_ref[...] * (pg * _sigmoid(pg))).astype(BF16)

    blk = lambda w, c: pl.BlockSpec((ts, w), lambda i: (i, c))
    full = lambda a: pl.BlockSpec(a.shape, lambda i: (0,) * a.ndim)
    out_shapes = [
        jax.ShapeDtypeStruct((S, FOX_W), BF16), jax.ShapeDtypeStruct((S, FOX_W), BF16), jax.ShapeDtypeStruct((S, FOX_W), BF16),
        jax.ShapeDtypeStruct((S, FOX_W), BF16),
        jax.ShapeDtypeStruct((S, SB_W), BF16), jax.ShapeDtypeStruct((S, SB_W), BF16), jax.ShapeDtypeStruct((S, SB_W), BF16),
        jax.ShapeDtypeStruct((S, POOL_W), BF16), jax.ShapeDtypeStruct((S, POOL_W), F32), jax.ShapeDtypeStruct((S, POOL_W), BF16),
    ]
    out_specs = [
        blk(FOX_W, 0), blk(FOX_W, 0), blk(FOX_W, 0), blk(FOX_W, 0),
        blk(SB_W, 0), blk(SB_W, 0), blk(SB_W, 0),
        blk(POOL_W, 0), blk(POOL_W, 0), blk(POOL_W, 0),
    ]
    return pl.pallas_call(
        body, name="prep", grid=(nb,),
        in_specs=[blk(FOX_W, C_FQ // FOX_W), blk(FOX_W, C_FK // FOX_W), blk(FOX_W, C_FV // FOX_W), blk(2 * POOL_W, C_PX // (2 * POOL_W)),
                  pl.BlockSpec((POOL_HALO, 2 * POOL_W), lambda i: (jnp.maximum(i * hb - 1, 0), C_PX // (2 * POOL_W))),
                  blk(LANES, 0),
                  blk(SB_W, C_SQ // SB_W), blk(SB_W, C_SK // SB_W), blk(SB_W, C_SV // SB_W),
                  full(qg), full(kg), full(bfp), full(wpd), full(ps)],
        out_specs=out_specs, out_shape=out_shapes,
        scratch_shapes=[pltpu.VMEM((1, LANES), F32), pltpu.VMEM((ts, LANES), F32), pltpu.VMEM((ts + POOL_HALO, POOL_W), F32)],
        compiler_params=_cparams(dimension_semantics=("arbitrary",)),
    )(projm, projm, projm, projm, projm, ffo, projm, projm, projm, qg, kg, bfp, wpd, ps)


def _pair_masks(x):
    ma = _iota((1, LANES), 1) < HEAD_DIM
    zero = jnp.zeros_like(x)
    return jnp.where(ma, x, zero), jnp.where(ma, zero, x)


DIAG_TILE = 256


def _diag_tiles(tq, size=DIAG_TILE):
    size = min(tq, size)
    return [(t * size, size) for t in range(tq // size)]


def _put_rows(old, new, r0):
    return new if r0 == 0 else jnp.concatenate([old[:r0], new], axis=0)


def _aug_queries(q):
    lane = _iota((1, LANES), 1)
    one = jnp.ones_like(q)
    zero = jnp.zeros_like(q)
    qa = jnp.where(lane < HEAD_DIM, q, jnp.where(lane < HEAD_DIM + AUG, one, zero))
    qb = jnp.where(lane >= HEAD_DIM, q, jnp.where(lane < AUG, one, zero))
    return qa, qb


EXP_DEAD = -105.0
PACK = 16


def _fox_walk_left(nfull, tk, block, carry, k_refs, qk_bound, row_floor):
    lane = _iota((1, LANES), 1)

    def alive(h, jj, c):
        k0 = pl.multiple_of(jnp.maximum(nfull - 1 - jj, 0) * tk + tk - PACK, PACK)
        last = k_refs[h][pl.ds(k0, PACK), :].astype(F32)
        lo = HEAD_DIM if h == 0 else 0
        negc = jnp.sum(jnp.where((lane >= lo) & (lane < lo + AUG), last, 0.0), axis=1, keepdims=True)
        return qk_bound + jnp.max(negc) - row_floor(c)[h] >= EXP_DEAD

    def walk(heads, jj0, c0):
        def go_on(state):
            jj, c = state
            ok = jj < nfull
            for h in heads:
                ok = ok & alive(h, jj, c)
            return ok

        def step(state):
            jj, c = state
            return jj + 1, block(pl.multiple_of((nfull - 1 - jj) * tk, tk), tk, 0, c, False, heads)

        return lax.while_loop(go_on, step, (jj0, c0))

    jj_pair, carry = walk((0, 1), jnp.int32(0), carry)
    carry = walk((0,), jj_pair, carry)[1]
    return walk((1,), jj_pair, carry)[1]


def _fox_fwd(qn, ka, kb, v, projm, qkb, *, tq, tk):
    S = qn.shape[0]
    npair = FOX_HEADS // 2

    def body(q_ref, ka_ref, kb_ref, v_ref, fg_ref, qkb_ref, o_ref, lse_ref, fm_ref):
        qi = pl.program_id(1)
        lane = _iota((1, LANES), 1)
        ma = lane < HEAD_DIM
        qaug = _aug_queries(q_ref[...])
        k_refs = (ka_ref, kb_ref)

        def block(k0, tkl, r0, carry, masked, heads=(0, 1)):
            vb = v_ref[pl.ds(k0, tkl), :]
            if masked:
                mask = (k0 + _iota((tq - r0, tkl), 1)) <= (qi * tq + r0 + _iota((tq - r0, tkl), 0))
            scores = {h: _dot_nt(qaug[h][r0:], k_refs[h][pl.ds(k0, tkl), :]) for h in heads}
            new = list(carry)
            for h in heads:
                m, l, acc = [x[r0:] for x in carry[h]]
                s = jnp.where(mask, scores[h], NEG) if masked else scores[h]
                m_new = jnp.maximum(m, jnp.max(s, axis=1, keepdims=True))
                alpha = jnp.exp(m - m_new)
                p = jnp.exp(s - m_new)
                sub = (m_new, alpha * l + jnp.sum(p, axis=1, keepdims=True), alpha * acc + _dot(p.astype(BF16), vb))
                new[h] = tuple(_put_rows(old, x, r0) for old, x in zip(carry[h], sub))
            return tuple(new)

        carry = tuple((jnp.full((tq, 1), NEG, F32), jnp.zeros((tq, 1), F32), jnp.zeros((tq, LANES), F32)) for _ in range(2))
        for off, size in _diag_tiles(tq, tq):
            carry = block(pl.multiple_of(qi * tq + off, size), size, off, carry, True)
        carry = _fox_walk_left((qi * tq) // tk, tk, block, carry, k_refs, jnp.max(qkb_ref[...]),
                               lambda c: (jnp.min(c[0][0]), jnp.min(c[1][0])))
        (ma_, la, acca), (mb_, lb, accb) = carry
        o = jnp.where(ma, acca / la, accb / lb)
        o_ref[...] = o
        lse_ref[...] = jnp.where(ma, ma_ + jnp.log(la), mb_ + jnp.log(lb))
        fg = fg_ref[...]
        fm_ref[...] = (o * (fg * _sigmoid(fg))).astype(BF16)

    qblk = pl.BlockSpec((tq, LANES), lambda p, i: (i, p))
    kvblk = pl.BlockSpec((S, LANES), lambda p, i: (0, p))
    return pl.pallas_call(
        body, name="fox_fwd", grid=(npair, S // tq),
        in_specs=[qblk, kvblk, kvblk, kvblk,
                  pl.BlockSpec((tq, LANES), lambda p, i: (i, C_FG // LANES + p)),
                  pl.BlockSpec((1, LANES), lambda p, i: (0, 0))],
        out_specs=[qblk, qblk, qblk],
        out_shape=[jax.ShapeDtypeStruct((S, FOX_W), F32), jax.ShapeDtypeStruct((S, FOX_W), F32), jax.ShapeDtypeStruct((S, FOX_W), BF16)],
        compiler_params=_cparams(dimension_semantics=("arbitrary", "arbitrary")),
    )(qn, ka, kb, v, projm, qkb)


def _suffix_sums(x, tmat2):
    return _dot(jnp.concatenate(_split2(x), axis=1), tmat2)


def _suffix_matrix(tk, inclusive):
    rr, cc = _iota((2 * tk, tk), 0) & (tk - 1), _iota((2 * tk, tk), 1)
    return _ones_where(rr >= cc) if inclusive else _ones_where(rr > cc)


def _sb_scores(qh, kb, causal, tmat2, r_runs):
    heads = range(2)
    zs = [_dot_nt(qh[h], kb) for h in heads]
    nsps = [jnp.minimum(-z, 0.0) - jnp.log(1.0 + jnp.exp(-jnp.abs(z))) for z in zs]
    lbs = nsps if causal is None else [jnp.where(causal, n, 0.0) for n in nsps]
    rins = [_suffix_sums(lb, tmat2) for lb in lbs]
    args = [zs[h] + lbs[h] + (rins[h] + r_runs[h]) for h in heads]
    a_s = [jnp.exp(arg if causal is None else jnp.where(causal, arg, NEG)) for arg in args]
    return zs, nsps, lbs, a_s


def _sb_walk_left(nfull, tk, block, carry, running_sums):
    def alive(state):
        jj, c = state
        ra, rb = running_sums(c)
        return (jj < nfull) & (jnp.max(jnp.maximum(ra, rb)) >= EXP_DEAD)

    def step(state):
        jj, c = state
        return jj + 1, block(pl.multiple_of((nfull - 1 - jj) * tk, tk), 0, c, False)

    return lax.while_loop(alive, step, (jnp.int32(0), carry))[1]


def _sb_fwd(sq, sk, sv, projm, *, tq, tk):
    S = sq.shape[0]
    npair = SB_HEADS // 2

    def body(q_ref, k_ref, v_ref, sg_ref, o_ref, sm_ref):
        qi = pl.program_id(1)
        lane = _iota((1, LANES), 1)
        ma = lane < HEAD_DIM
        qh = _pair_masks(q_ref[...])
        tmat2 = _suffix_matrix(tk, inclusive=False)
        nfull = (qi * tq) // tk

        def block(k0, r0, carry, masked):
            nr = tq - r0
            kb = k_ref[pl.ds(k0, tk), :]
            vb = v_ref[pl.ds(k0, tk), :]
            causal = (k0 + _iota((nr, tk), 1)) < (qi * tq + r0 + _iota((nr, tk), 0)) if masked else None
            _, _, lbs, a_s = _sb_scores([q[r0:] for q in qh], kb, causal, tmat2, [carry[h][0][r0:] for h in range(2)])
            pv = _dot(jnp.concatenate([a.astype(BF16) for a in a_s], axis=0), vb)
            return tuple((_put_rows(carry[h][0], carry[h][0][r0:] + jnp.sum(lbs[h], axis=1, keepdims=True), r0),
                          _put_rows(carry[h][1], carry[h][1][r0:] + pv[h * nr:(h + 1) * nr], r0)) for h in range(2))

        carry = tuple((jnp.zeros((tq, 1), F32), jnp.zeros((tq, LANES), F32)) for _ in range(2))
        for off, size in reversed(_diag_tiles(tq)):
            assert size == tk
            carry = block(pl.multiple_of(qi * tq + off, tk), off, carry, True)
        (_, acca), (_, accb) = _sb_walk_left(nfull, tk, block, carry, lambda c: (c[0][0], c[1][0]))
        o = jnp.where(ma, acca, accb)
        o_ref[...] = o
        sg = sg_ref[...]
        sm_ref[...] = (o * (sg * _sigmoid(sg))).astype(BF16)

    qblk = pl.BlockSpec((tq, LANES), lambda p, i: (i, p))
    kvblk = pl.BlockSpec((S, LANES), lambda p, i: (0, p))
    return pl.pallas_call(
        body, name="sb_fwd", grid=(npair, S // tq),
        in_specs=[qblk, kvblk, kvblk, pl.BlockSpec((tq, LANES), lambda p, i: (i, C_SG // LANES + p))],
        out_specs=[qblk, qblk],
        out_shape=[jax.ShapeDtypeStruct((S, SB_W), F32), jax.ShapeDtypeStruct((S, SB_W), BF16)],
        compiler_params=_cparams(dimension_semantics=("arbitrary", "arbitrary")),
    )(sq, sk, sv, projm)


def _outproj(x, fm, pm, sm, w_out, layer, *, tm):
    S, D = x.shape

    def body(x_ref, fm_ref, pm_ref, sm_ref, w_ref, y_ref):
        y = x_ref[...] + _dot(fm_ref[...], w_ref[0:FOX_W, :])
        y = y + _dot(pm_ref[...], w_ref[FOX_W:FOX_W + POOL_W, :])
        y_ref[...] = y + _dot(sm_ref[...], w_ref[FOX_W + POOL_W:D_MIX, :])

    row = lambda w: pl.BlockSpec((tm, w), lambda i: (i, 0))
    return pl.pallas_call(
        body, name="outproj", grid=(S // tm,),
        in_specs=[row(D), row(FOX_W), row(POOL_W), row(SB_W), pl.BlockSpec((None, D_MIX, D), lambda i: (layer, 0, 0))],
        out_specs=row(D), out_shape=jax.ShapeDtypeStruct((S, D), F32),
        compiler_params=_cparams(dimension_semantics=("arbitrary",)),
    )(x, fm, pm, sm, w_out)


def _loss_head(y, target, *, tm):
    S, D = y.shape

    def body(y_ref, t_ref, dy_ref, sq_ref):
        @pl.when(pl.program_id(0) == 0)
        def _():
            sq_ref[...] = jnp.zeros_like(sq_ref)

        d = y_ref[...] - t_ref[...]
        dy_ref[...] = d * (1.0 / D)
        sq_ref[...] += jnp.sum(d * d, axis=0, keepdims=True)

    row = pl.BlockSpec((tm, D), lambda i: (i, 0))
    return pl.pallas_call(
        body, name="loss_head", grid=(S // tm,),
        in_specs=[row, row], out_specs=[row, pl.BlockSpec((1, D), lambda i: (0, 0))],
        out_shape=[jax.ShapeDtypeStruct((S, D), F32), jax.ShapeDtypeStruct((1, D), F32)],
        compiler_params=_cparams(dimension_semantics=("arbitrary",)),
    )(y, target)


def _outproj_bwd(dy, fm, pm, sm, w_out, layer, stacks, *, tm):
    S, D = dy.shape

    def body(dy_ref, fm_ref, pm_ref, sm_ref, w_ref, dm_ref, dw_ref):
        @pl.when(pl.program_id(0) == 0)
        def _():
            dw_ref[...] = jnp.zeros_like(dw_ref)

        dyb = dy_ref[...].astype(BF16)
        dm_ref[...] = _dot_nt(dyb, w_ref[...])
        dw_ref[0:FOX_W, :] += _dot_tn(fm_ref[...], dyb)
        dw_ref[FOX_W:FOX_W + POOL_W, :] += _dot_tn(pm_ref[...], dyb)
        dw_ref[FOX_W + POOL_W:D_MIX, :] += _dot_tn(sm_ref[...], dyb)

    row = lambda w: pl.BlockSpec((tm, w), lambda i: (i, 0))
    wspec = pl.BlockSpec((None, D_MIX, D), lambda i: (layer, 0, 0))
    return _stack_call(
        body, "outproj_bwd", (S // tm,), [row(D), row(FOX_W), row(POOL_W), row(SB_W), wspec], (dy, fm, pm, sm, w_out),
        [pl.BlockSpec((None, D_MIX, D), lambda i: (layer, 0, 0))], [(D_MIX, D)], stacks,
        plain_specs=[row(D_MIX)], plain_shapes=[jax.ShapeDtypeStruct((S, D_MIX), F32)],
        compiler_params=_cparams(dimension_semantics=("arbitrary",)))


def _fox_bwd(qn, ka, kb, v, o, lse, dmix, projm, qkb, *, tq, tk):
    S = qn.shape[0]
    npair = FOX_HEADS // 2

    def body(q_ref, ka_ref, kb_ref, v_ref, o_ref, lse_ref, dm_ref, fg_ref, qkb_ref,
             dq_ref, dk_ref, dv_ref, dfg_ref, dct_ref, dcr_ref):
        qi = pl.program_id(1)

        @pl.when(qi == 0)
        def _():
            dk_ref[...] = jnp.zeros_like(dk_ref)
            dv_ref[...] = jnp.zeros_like(dv_ref)
            dct_ref[...] = jnp.zeros_like(dct_ref)

        lane = _iota((1, LANES), 1)
        ma = lane < HEAD_DIM
        qh = _pair_masks(q_ref[...])
        qaug = _aug_queries(q_ref[...])
        k_refs = (ka_ref, kb_ref)
        lsev = lse_ref[...]
        lse = (_lane_pick(lsev, lane, 0), _lane_pick(lsev, lane, HEAD_DIM))
        fg = fg_ref[...]
        silu, dsilu = _silu_pair(fg)
        dm = dm_ref[...]
        ov = o_ref[...]
        do = dm * silu
        dfg_ref[...] = dm * ov * dsilu
        dd = do * ov
        dsum = (jnp.sum(jnp.where(ma, dd, 0.0), axis=1, keepdims=True), jnp.sum(jnp.where(ma, 0.0, dd), axis=1, keepdims=True))
        doh = _pair_masks(do.astype(BF16))

        def block(k0, tkl, r0, carry, masked, heads=(0, 1)):
            vb = v_ref[pl.ds(k0, tkl), :]
            if masked:
                mask = (k0 + _iota((tq - r0, tkl), 1)) <= (qi * tq + r0 + _iota((tq - r0, tkl), 0))
            kaugs = {h: k_refs[h][pl.ds(k0, tkl), :] for h in heads}
            scores = {h: _dot_nt(qaug[h][r0:], kaugs[h]) for h in heads}
            dps = {h: _dot_nt(doh[h][r0:], vb) for h in heads}
            ps, dss = [], []
            rows = [carry[1], carry[2]]
            for h in heads:
                s = jnp.where(mask, scores[h], NEG) if masked else scores[h]
                p = jnp.exp(s - lse[h][r0:])
                dsf = p * (dps[h] - dsum[h][r0:])
                dct_ref[0, h:h + 1, pl.ds(k0, tkl)] -= jnp.sum(dsf, axis=0, keepdims=True)
                rows[h] = _put_rows(carry[1 + h], carry[1 + h][r0:] + jnp.sum(dsf, axis=1, keepdims=True), r0)
                ps.append(p.astype(BF16))
                dss.append(dsf.astype(BF16))
            dv_ref[pl.ds(k0, tkl), :] += _dot_tn(jnp.concatenate(ps, axis=0), jnp.concatenate([doh[h][r0:] for h in heads], axis=0))
            dk_ref[pl.ds(k0, tkl), :] += _dot_tn(jnp.concatenate(dss, axis=0), jnp.concatenate([qh[h][r0:] for h in heads], axis=0))
            kh = jnp.concatenate([_pair_masks(kaugs[h])[h] for h in heads], axis=0)
            dq = _put_rows(carry[0], carry[0][r0:] + _dot(jnp.concatenate(dss, axis=1), kh), r0)
            return (dq, rows[0], rows[1])

        zcol = jnp.zeros((tq, 1), F32)
        carry = (jnp.zeros((tq, LANES), F32), zcol, zcol)
        for off, size in _diag_tiles(tq):
            carry = block(pl.multiple_of(qi * tq + off, size), size, off, carry, True)
        floors = (jnp.min(lse[0]), jnp.min(lse[1]))
        dq, rowa, rowb = _fox_walk_left((qi * tq) // tk, tk, block, carry, k_refs, jnp.max(qkb_ref[...]), lambda c: floors)
        dq_ref[...] = dq * QK_SCALE
        dcr_ref[0] = jnp.where(ma, rowa, rowb)

    qblk = pl.BlockSpec((tq, LANES), lambda p, i: (i, p))
    kvblk = pl.BlockSpec((S, LANES), lambda p, i: (0, p))
    f32out = jax.ShapeDtypeStruct((S, FOX_W), F32)
    ctblk = pl.BlockSpec((1, FF_STRIDE, S), lambda p, i: (p, 0, 0))
    return pl.pallas_call(
        body, name="fox_bwd", grid=(npair, S // tq),
        in_specs=[qblk, kvblk, kvblk, kvblk, qblk, qblk, qblk,
                  pl.BlockSpec((tq, LANES), lambda p, i: (i, C_FG // LANES + p)),
                  pl.BlockSpec((1, LANES), lambda p, i: (0, 0))],
        out_specs=[qblk, kvblk, kvblk, qblk, ctblk, pl.BlockSpec((1, tq, LANES), lambda p, i: (p, i, 0))],
        out_shape=[f32out, f32out, f32out, f32out, jax.ShapeDtypeStruct((npair, FF_STRIDE, S), F32),
                   jax.ShapeDtypeStruct((npair, S, LANES), F32)],
        compiler_params=_cparams(dimension_semantics=("arbitrary", "arbitrary")),
    )(qn, ka, kb, v, o, lse, dmix, projm, qkb)


def _sb_bwd(sq, sk, sv, o, dmix, projm, *, tq, tk):
    S = sq.shape[0]
    npair = SB_HEADS // 2
    mix0 = (FOX_W + POOL_W) // LANES

    def body(q_ref, k_ref, v_ref, o_ref, dm_ref, sg_ref, dq_ref, dk_ref, dv_ref, dsg_ref):
        qi = pl.program_id(1)

        @pl.when(qi == 0)
        def _():
            dk_ref[...] = jnp.zeros_like(dk_ref)
            dv_ref[...] = jnp.zeros_like(dv_ref)

        lane = _iota((1, LANES), 1)
        ma = lane < HEAD_DIM
        qh = _pair_masks(q_ref[...])
        sg = sg_ref[...]
        silu, dsilu = _silu_pair(sg)
        dm = dm_ref[...]
        ov = o_ref[...]
        do = dm * silu
        dsg_ref[...] = dm * ov * dsilu
        dob = do.astype(BF16)
        dd = dob.astype(F32) * ov
        dsum = (jnp.sum(jnp.where(ma, dd, 0.0), axis=1, keepdims=True), jnp.sum(jnp.where(ma, 0.0, dd), axis=1, keepdims=True))
        doh = _pair_masks(dob)
        tmat2 = _suffix_matrix(tk, inclusive=False)
        tmat2_inc = _suffix_matrix(tk, inclusive=True)
        nfull = (qi * tq) // tk

        def block(k0, r0, carry, masked):
            nr = tq - r0
            kb = k_ref[pl.ds(k0, tk), :]
            vb = v_ref[pl.ds(k0, tk), :]
            kh = _pair_masks(kb)
            causal = (k0 + _iota((nr, tk), 1)) < (qi * tq + r0 + _iota((nr, tk), 0)) if masked else None
            heads = range(2)
            qs = [q[r0:] for q in qh]
            dos = [d[r0:] for d in doh]
            das = [_dot_nt(dos[h], vb) for h in heads]
            zs, nsps, lbs, a_s = _sb_scores(qs, kb, causal, tmat2, [carry[h][0][r0:] for h in heads])
            abs_ = [a.astype(BF16) for a in a_s]
            us = [abs_[h].astype(F32) * das[h] for h in heads]
            uins = [_suffix_sums(u, tmat2_inc) for u in us]
            dzs = []
            for h in heads:
                cum_u = dsum[h][r0:] - (uins[h] + carry[h][1][r0:])
                dz = us[h] * jnp.exp(nsps[h]) - jnp.exp(zs[h] + nsps[h]) * cum_u
                if masked:
                    dz = jnp.where(causal, dz, 0.0)
                dzs.append(dz.astype(BF16))
            dv_ref[pl.ds(k0, tk), :] += _dot_tn(jnp.concatenate(abs_, axis=0), jnp.concatenate(dos, axis=0))
            dk_ref[pl.ds(k0, tk), :] += _dot_tn(jnp.concatenate(dzs, axis=0), jnp.concatenate(qs, axis=0))
            dq = _put_rows(carry[2], carry[2][r0:] + _dot(jnp.concatenate(dzs, axis=1), jnp.concatenate(kh, axis=0)), r0)
            new = [(_put_rows(carry[h][0], carry[h][0][r0:] + jnp.sum(lbs[h], axis=1, keepdims=True), r0),
                    _put_rows(carry[h][1], carry[h][1][r0:] + jnp.sum(us[h], axis=1, keepdims=True), r0)) for h in heads]
            return (new[0], new[1], dq)

        zcol = jnp.zeros((tq, 1), F32)
        carry = ((zcol, zcol), (zcol, zcol), jnp.zeros((tq, LANES), F32))
        for off, size in reversed(_diag_tiles(tq)):
            assert size == tk
            carry = block(pl.multiple_of(qi * tq + off, tk), off, carry, True)
        dq = _sb_walk_left(nfull, tk, block, carry, lambda c: (c[0][0], c[1][0]))[2]
        dq_ref[...] = dq * QK_SCALE

    qblk = pl.BlockSpec((tq, LANES), lambda p, i: (i, p))
    kvblk = pl.BlockSpec((S, LANES), lambda p, i: (0, p))
    f32out = jax.ShapeDtypeStruct((S, SB_W), F32)
    return pl.pallas_call(
        body, name="sb_bwd", grid=(npair, S // tq),
        in_specs=[qblk, kvblk, kvblk, qblk,
                  pl.BlockSpec((tq, LANES), lambda p, i: (i, mix0 + p)),
                  pl.BlockSpec((tq, LANES), lambda p, i: (i, C_SG // LANES + p))],
        out_specs=[qblk, kvblk, kvblk, qblk],
        out_shape=[f32out, f32out, f32out, f32out],
        compiler_params=_cparams(dimension_semantics=("arbitrary", "arbitrary")),
    )(sq, sk, sv, o, dmix, projm)


def _prep_bwd(projm, ffo, dqn, dkn, dct, dcr, dv, dfg, dsq, dsk, dsv, dsg, dmix, pooled, yp, qg, kg, bfp, wpd, ps, *, ts):
    S = projm.shape[0]
    nb = S // ts
    hb = ts // POOL_HALO
    npair = FOX_HEADS // 2
    last_halo = S // POOL_HALO - 1

    def body(fq_ref, fk_ref, pp_ref, pph_ref, ff_ref,
             dqn_ref, dkn_ref, dct_ref, dcr_ref, dv_ref, dfg_ref, dsq_ref, dsk_ref, dsv_ref, dsg_ref,
             dmp_ref, dmh_ref, pooled_ref, yp_ref, qg_ref, kg_ref, bf_ref, wpd_ref, ps_ref,
             dp_ref, dqg_ref, dkg_ref, dbf_ref, dwp_ref, dps_ref,
             carry_ref, dl_ref, buf_ref, dct_s):
        i = pl.program_id(0)
        blk = nb - 1 - i

        @pl.when(i == 0)
        def _():
            carry_ref[...] = jnp.zeros_like(carry_ref)
            dqg_ref[...] = jnp.zeros_like(dqg_ref)
            dkg_ref[...] = jnp.zeros_like(dkg_ref)
            dbf_ref[...] = jnp.zeros_like(dbf_ref)
            dwp_ref[...] = jnp.zeros_like(dwp_ref)
            dps_ref[...] = jnp.zeros_like(dps_ref)

        bd = _head_blockdiag()
        for raw_ref, g_ref, dn, dg_ref, col in ((fq_ref, qg_ref, dqn_ref[...], dqg_ref, C_FQ), (fk_ref, kg_ref, dkn_ref[...], dkg_ref, C_FK)):
            q = raw_ref[...]
            rstd = lax.rsqrt(_group_sum(q * q, bd) * (1.0 / HEAD_DIM) + EPS)
            xhat = q * rstd
            dg_ref[...] += jnp.sum(dn * xhat, axis=0, keepdims=True)
            dyg = dn * g_ref[...]
            mean = _group_sum(dyg * xhat, bd) * (1.0 / HEAD_DIM)
            dp_ref[:, col:col + FOX_W] = (rstd * (dyg - xhat * mean)).astype(BF16)
        dp_ref[:, C_FV:C_FV + FOX_W] = dv_ref[...].astype(BF16)
        dp_ref[:, C_FG:C_FG + FOX_W] = dfg_ref[...].astype(BF16)
        dp_ref[:, C_SQ:C_SQ + SB_W] = dsq_ref[...].astype(BF16)
        dp_ref[:, C_SK:C_SK + SB_W] = dsk_ref[...].astype(BF16)
        dp_ref[:, C_SV:C_SV + SB_W] = dsv_ref[...].astype(BF16)
        dp_ref[:, C_SG:C_SG + SB_W] = dsg_ref[...].astype(BF16)

        dct_s[...] = jnp.zeros_like(dct_s)
        for p in range(npair):
            dct_s[FF_STRIDE * p:FF_STRIDE * (p + 1), :] = dct_ref[p]
        dc = dct_s[...].T
        lane = _iota((1, LANES), 1)
        for p in range(npair):
            dcr = dcr_ref[p]
            dc = dc + jnp.where(lane == FF_STRIDE * p, _lane_pick(dcr, lane, 0), 0.0)
            dc = dc + jnp.where(lane == FF_STRIDE * p + 1, _lane_pick(dcr, lane, HEAD_DIM), 0.0)
        triu = _ones_where(_iota((ts, ts), 1) >= _iota((ts, ts), 0))
        dlf = _dot_exact_lhs(triu, dc) + carry_ref[...]
        dl_ref[...] = dlf
        carry_ref[...] = dl_ref[0:1, :]
        z = ff_ref[...] + bf_ref[...]
        dff = dlf * (1.0 / (1.0 + jnp.exp(z)))
        dbf_ref[...] += jnp.sum(dff, axis=0, keepdims=True)
        dp_ref[:, PM:PW] = dff.astype(BF16)

        psv = ps_ref[...]
        wpdv = wpd_ref[...]
        lane_group = _iota((1, POOL_W), 1) >> HEAD_SHIFT
        wlen = _pool_group_select(lane_group, [float(w) for w in POOL_WINDOWS])
        pg = pp_ref[:, POOL_W:2 * POOL_W]
        silu, dsilu = _silu_pair(pg)
        dmp = dmp_ref[...]
        ypv = yp_ref[...]
        dp_ref[:, C_PG:C_PG + POOL_W] = (dmp * (ypv * psv) * dsilu).astype(BF16)
        dps_ref[...] += jnp.sum(dmp * silu * ypv, axis=0, keepdims=True)
        dyp = (dmp * psv * silu).astype(BF16)
        dwp_ref[...] += _dot_tn(pooled_ref[...], dyp)
        dpooled = _dot_nt(dyp, wpdv)
        pgh = pph_ref[:, POOL_W:2 * POOL_W]
        dyph = (dmh_ref[...] * psv * (pgh * _sigmoid(pgh))).astype(BF16)
        dpooled_h = jnp.where(blk < nb - 1, _dot_nt(dyph, wpdv), 0.0)
        tpos = (blk * ts + _iota((ts, 1), 0) + 1).astype(F32)
        ev = dpooled / jnp.minimum(tpos, wlen)
        buf_ref[0:ts, :] = ev
        buf_ref[ts:ts + POOL_HALO, :] = dpooled_h / wlen
        acc = ev
        snaps = []
        for d in range(1, POOL_HALO):
            acc = acc + buf_ref[pl.ds(d, ts), :]
            if d + 1 in POOL_WINDOWS:
                snaps.append(acc)
        dp_ref[:, C_PX:C_PX + POOL_W] = (_pool_group_select(lane_group, snaps) - dpooled).astype(BF16)

    rblk = lambda w, c: pl.BlockSpec((ts, w), lambda i: (nb - 1 - i, c))
    full = lambda a: pl.BlockSpec(a.shape, lambda i: (0,) * a.ndim)
    halo = lambda w, c: pl.BlockSpec((POOL_HALO, w), lambda i: (jnp.minimum((nb - i) * hb, last_halo), c))
    acc_spec = lambda r, w: pl.BlockSpec((r, w), lambda i: (0, 0))
    return pl.pallas_call(
        body, name="prep_bwd", grid=(nb,),
        in_specs=[rblk(FOX_W, C_FQ // FOX_W), rblk(FOX_W, C_FK // FOX_W), rblk(2 * POOL_W, C_PX // (2 * POOL_W)),
                  halo(2 * POOL_W, C_PX // (2 * POOL_W)), rblk(LANES, 0),
                  rblk(FOX_W, 0), rblk(FOX_W, 0), pl.BlockSpec((npair, FF_STRIDE, ts), lambda i: (0, 0, nb - 1 - i)),
                  pl.BlockSpec((npair, ts, LANES), lambda i: (0, nb - 1 - i, 0)), rblk(FOX_W, 0), rblk(FOX_W, 0),
                  rblk(SB_W, 0), rblk(SB_W, 0), rblk(SB_W, 0), rblk(SB_W, 0),
                  rblk(POOL_W, FOX_W // POOL_W), halo(POOL_W, FOX_W // POOL_W), rblk(POOL_W, 0), rblk(POOL_W, 0),
                  full(qg), full(kg), full(bfp), full(wpd), full(ps)],
        out_specs=[rblk(PW, 0), acc_spec(1, FOX_W), acc_spec(1, FOX_W), acc_spec(1, LANES), acc_spec(POOL_W, POOL_W), acc_spec(1, POOL_W)],
        out_shape=[jax.ShapeDtypeStruct((S, PW), BF16), jax.ShapeDtypeStruct((1, FOX_W), F32), jax.ShapeDtypeStruct((1, FOX_W), F32),
                   jax.ShapeDtypeStruct((1, LANES), F32), jax.ShapeDtypeStruct((POOL_W, POOL_W), F32), jax.ShapeDtypeStruct((1, POOL_W), F32)],
        scratch_shapes=[pltpu.VMEM((1, LANES), F32), pltpu.VMEM((ts, LANES), F32), pltpu.VMEM((ts + POOL_HALO, POOL_W), F32),
                        pltpu.VMEM((LANES, ts), F32)],
        compiler_params=_cparams(dimension_semantics=("arbitrary",)),
    )(projm, projm, projm, projm, ffo, dqn, dkn, dct, dcr, dv, dfg, dsq, dsk, dsv, dsg, dmix, dmix, pooled, yp, qg, kg, bfp, wpd, ps)


def _stack_call(body, name, grid, in_specs, operands, slot_specs, slot_shapes, stacks, plain_specs=(), plain_shapes=(), **kw):
    out_specs = list(plain_specs) + list(slot_specs)
    out_shape = list(plain_shapes) + [jax.ShapeDtypeStruct((DEPTH,) + s, F32) for s in slot_shapes]
    if stacks is None:
        return pl.pallas_call(body, name=name, grid=grid, in_specs=in_specs, out_specs=out_specs, out_shape=out_shape, **kw)(*operands)
    n = len(operands)

    def aliased_body(*refs):
        body(*refs[:n], *refs[n + len(stacks):])

    return pl.pallas_call(
        aliased_body, name=name, grid=grid, in_specs=list(in_specs) + [pl.BlockSpec(memory_space=pl.ANY)] * len(stacks),
        out_specs=out_specs, out_shape=out_shape,
        input_output_aliases={n + k: len(plain_specs) + k for k in range(len(stacks))}, **kw)(*operands, *stacks)


def _inproj_dw(h, dproj, layer, stacks, *, ts, tn):
    S, D = h.shape
    nj = PM // tn

    def body(h_ref, dp_ref, dpf_ref, dw_ref, dwf_ref):
        s = pl.program_id(1)

        @pl.when(s == 0)
        def _():
            dw_ref[...] = jnp.zeros_like(dw_ref)

        @pl.when((s == 0) & (pl.program_id(0) == 0))
        def _():
            dwf_ref[...] = jnp.zeros_like(dwf_ref)

        hv = h_ref[...]
        dw_ref[...] += _dot_tn(dp_ref[...], hv)

        @pl.when(pl.program_id(0) == 0)
        def _():
            dwf_ref[...] += _dot_tn(dpf_ref[...], hv)

    return _stack_call(
        body, "inproj_dw", (nj, S // ts),
        [pl.BlockSpec((ts, D), lambda j, s: (s, 0)),
         pl.BlockSpec((ts, tn), lambda j, s: (s, j)),
         pl.BlockSpec((ts, LANES), lambda j, s: (s, PM // LANES))],
        (h, dproj, dproj),
        [pl.BlockSpec((None, tn, D), lambda j, s: (layer, j, 0)), pl.BlockSpec((None, LANES, D), lambda j, s: (layer, 0, 0))],
        [(PM, D), (LANES, D)], stacks,
        compiler_params=_cparams(dimension_semantics=("arbitrary", "arbitrary")))


def _inproj_dx(dproj, wt_all, layer, x, g, dy, *, tm):
    S, D = x.shape

    def body(dp_ref, w_ref, x_ref, g_ref, dy_ref, dx_ref, dg_ref):
        @pl.when(pl.program_id(0) == 0)
        def _():
            dg_ref[...] = jnp.zeros_like(dg_ref)

        dh = _dot(dp_ref[...], w_ref[...])
        xf = x_ref[...]
        rstd = lax.rsqrt(jnp.mean(xf * xf, axis=-1, keepdims=True) + EPS)
        xhat = xf * rstd
        dg_ref[...] += jnp.sum(dh * xhat, axis=0, keepdims=True)
        dyg = dh * g_ref[...]
        mean = jnp.mean(dyg * xhat, axis=-1, keepdims=True)
        dx_ref[...] = rstd * (dyg - xhat * mean) + dy_ref[...]

    row = lambda w: pl.BlockSpec((tm, w), lambda i: (i, 0))
    return pl.pallas_call(
        body, name="inproj_dx", grid=(S // tm,),
        in_specs=[row(PW), pl.BlockSpec((None, PW, D), lambda i: (layer, 0, 0)), row(D), pl.BlockSpec((1, D), lambda i: (0, 0)), row(D)],
        out_specs=[row(D), pl.BlockSpec((1, D), lambda i: (0, 0))],
        out_shape=[jax.ShapeDtypeStruct((S, D), F32), jax.ShapeDtypeStruct((1, D), F32)],
        compiler_params=_cparams(dimension_semantics=("arbitrary",)),
    )(dproj, wt_all, x, g, dy)


def _adam_update(w, g, m, v):
    nm = ADAM_B1 * m + (1.0 - ADAM_B1) * g
    nv = ADAM_B2 * v + (1.0 - ADAM_B2) * (g * g)
    m_hat = nm / (1.0 - ADAM_B1 ** ADAM_STEP)
    v_hat = nv / (1.0 - ADAM_B2 ** ADAM_STEP)
    return -ADAM_LR * (m_hat / (jnp.sqrt(v_hat) + ADAM_EPS) + ADAM_WD * w), nm, nv


def _adamw(w, g, m, v):
    L, R, C = w.shape
    tr = R if R <= 512 else 256

    def body(w_ref, g_ref, m_ref, v_ref, d_ref, nm_ref, nv_ref):
        d_ref[...], nm_ref[...], nv_ref[...] = _adam_update(w_ref[...], g_ref[...], m_ref[...], v_ref[...])

    spec = pl.BlockSpec((1, tr, C), lambda l, i: (l, i, 0))
    shp = jax.ShapeDtypeStruct((L, R, C), F32)
    return pl.pallas_call(
        body, name="adamw", grid=(L, R // tr), in_specs=[spec] * 4, out_specs=[spec] * 3, out_shape=[shp] * 3,
        compiler_params=_cparams(dimension_semantics=("arbitrary", "arbitrary")),
    )(w, g, m, v)


def _adamw_nd(w, g, m, v):
    shape = w.shape
    view = (1,) + shape if w.ndim == 2 else (shape[0], -1, shape[-1])
    outs = _adamw(w.reshape(view), g.reshape(view), m.reshape(view), v.reshape(view))
    return tuple(o.reshape(shape) for o in outs)


FLIP_C = (0, 0, 1)
FLIP_X = (1, 0, 0)
FLIP_Y = (0, 1, 0)
FLIP_XY = (1, 1, 0)
MESH = pl.DeviceIdType.MESH


def _peer(flip):
    me = (lax.axis_index("x"), lax.axis_index("y"), lax.axis_index("c"))
    return tuple(1 - a if f else a for a, f in zip(me, flip))


def _exchange(name, arrays, flips):
    n = len(arrays)

    def body(*refs):
        srcs, dsts = refs[:n], refs[n:2 * n]
        send_sems, recv_sems = refs[2 * n:]
        copies = [pltpu.make_async_remote_copy(src_ref=srcs[k], dst_ref=dsts[k], send_sem=send_sems.at[k], recv_sem=recv_sems.at[k],
                                               device_id=_peer(flips[k]), device_id_type=MESH) for k in range(n)]
        for cp in copies:
            cp.start()
        for cp in copies:
            cp.wait()

    anyspec = pl.BlockSpec(memory_space=pl.ANY)
    return pl.pallas_call(
        body, name=name, in_specs=[anyspec] * n, out_specs=[anyspec] * n,
        out_shape=[jax.ShapeDtypeStruct(a.shape, a.dtype) for a in arrays],
        scratch_shapes=[pltpu.SemaphoreType.DMA((n,)), pltpu.SemaphoreType.DMA((n,))],
    )(*arrays)


def _exchange_add(name, x, flip):
    def body(x_ref, o_ref, buf_ref, send_sem, recv_sem):
        cp = pltpu.make_async_remote_copy(src_ref=x_ref, dst_ref=buf_ref, send_sem=send_sem, recv_sem=recv_sem,
                                          device_id=_peer(flip), device_id_type=MESH)
        cp.start()
        cp.wait()
        o_ref[...] = x_ref[...] + buf_ref[...]

    vspec = pl.BlockSpec(memory_space=pltpu.VMEM)
    return pl.pallas_call(
        body, name=name, in_specs=[vspec], out_specs=vspec, out_shape=jax.ShapeDtypeStruct(x.shape, x.dtype),
        scratch_shapes=[pltpu.VMEM(x.shape, x.dtype), pltpu.SemaphoreType.DMA, pltpu.SemaphoreType.DMA],
    )(x)


def _chip_index():
    return 2 * lax.axis_index("x") + lax.axis_index("y")


def _gather_weights(w_in_t, w_out):
    wi = w_in_t.astype(BF16)
    wo = jnp.swapaxes(w_out, 0, 1).astype(BF16)
    halves = (wi.shape[0] // 2, wo.shape[0] // 2)
    ARR = 2
    TO_X, TO_Y, ON_Y, ON_X, SIB_X, SIB_Y, SIB_D0, SIB_D1, OWN = [ARR * k for k in range(9)]
    n_sems = ARR * 9

    def body(wi_ref, wo_ref, gi_ref, go_ref, send_sems, recv_sems):
        c = lax.axis_index("c")
        j = _chip_index()
        srcs = (wi_ref, wo_ref)
        dsts = (gi_ref, go_ref)
        def cuts(core):
            return [(pl.ds(h * core, h), pl.ds(h * core, h // 2), pl.ds(h * core + h // 2, h - h // 2)) for h in halves]
        mine, theirs = cuts(c), cuts(1 - c)
        HALF, Q0, Q1 = 0, 1, 2

        def copy(idx, src, dst, flip):
            return pltpu.make_async_remote_copy(src_ref=src, dst_ref=dst, send_sem=send_sems.at[idx], recv_sem=recv_sems.at[idx],
                                                device_id=_peer(flip), device_id_type=MESH)

        def slot(a, shard, cut):
            return dsts[a].at[shard, cut]

        jx, jy, jd = j ^ 2, j ^ 1, j ^ 3
        sends = []

        def start(cp):
            cp.start()
            sends.append(cp)

        for a in range(ARR):
            start(copy(TO_X + a, srcs[a].at[mine[a][HALF]], slot(a, j, mine[a][HALF]), FLIP_X))
            start(copy(TO_Y + a, srcs[a].at[mine[a][HALF]], slot(a, j, mine[a][HALF]), FLIP_Y))
        own = [copy(OWN + a, srcs[a], dsts[a].at[j], FLIP_C) for a in range(ARR)]
        for cp in own:
            cp.start()
        for a in range(ARR):
            copy(TO_X + a, slot(a, jx, mine[a][HALF]), slot(a, jx, mine[a][HALF]), FLIP_X).wait_recv()
            start(copy(ON_Y + a, slot(a, jx, mine[a][Q0]), slot(a, jx, mine[a][Q0]), FLIP_Y))
            start(copy(SIB_X + a, slot(a, jx, mine[a][HALF]), slot(a, jx, mine[a][HALF]), FLIP_C))
        for a in range(ARR):
            copy(TO_Y + a, slot(a, jy, mine[a][HALF]), slot(a, jy, mine[a][HALF]), FLIP_Y).wait_recv()
            start(copy(ON_X + a, slot(a, jy, mine[a][Q1]), slot(a, jy, mine[a][Q1]), FLIP_X))
            start(copy(SIB_Y + a, slot(a, jy, mine[a][HALF]), slot(a, jy, mine[a][HALF]), FLIP_C))
        for a in range(ARR):
            copy(ON_Y + a, slot(a, jd, mine[a][Q0]), slot(a, jd, mine[a][Q0]), FLIP_Y).wait_recv()
            start(copy(SIB_D0 + a, slot(a, jd, mine[a][Q0]), slot(a, jd, mine[a][Q0]), FLIP_C))
        for a in range(ARR):
            copy(ON_X + a, slot(a, jd, mine[a][Q1]), slot(a, jd, mine[a][Q1]), FLIP_X).wait_recv()
            start(copy(SIB_D1 + a, slot(a, jd, mine[a][Q1]), slot(a, jd, mine[a][Q1]), FLIP_C))
        for a in range(ARR):
            for idx, shard, cut in ((SIB_X, jx, HALF), (SIB_Y, jy, HALF), (SIB_D0, jd, Q0), (SIB_D1, jd, Q1)):
                copy(idx + a, slot(a, shard, theirs[a][cut]), slot(a, shard, theirs[a][cut]), FLIP_C).wait_recv()
        for cp in own:
            cp.wait()
        for cp in sends:
            cp.wait_send()

    anyspec = pl.BlockSpec(memory_space=pl.ANY)
    gi, go = pl.pallas_call(
        body, name="gather_weights", in_specs=[anyspec] * 2, out_specs=[anyspec] * 2,
        out_shape=[jax.ShapeDtypeStruct((4,) + wi.shape, BF16), jax.ShapeDtypeStruct((4,) + wo.shape, BF16)],
        scratch_shapes=[pltpu.SemaphoreType.DMA((n_sems,)), pltpu.SemaphoreType.DMA((n_sems,))],
    )(wi, wo)
    w_in_t_full = gi.reshape((4 * wi.shape[0],) + wi.shape[1:])
    w_out_full = jnp.swapaxes(go.reshape((4 * wo.shape[0],) + wo.shape[1:]), 0, 1)
    return w_in_t_full, w_out_full


def _to_aligned(w_t):
    _, L, D = w_t.shape
    npair = FOX_HEADS // 2
    ff = w_t[ORIG_FF:ORIG_REST].reshape(npair, 2, L, D)
    ff = jnp.pad(ff, ((0, 0), (0, FF_STRIDE - 2), (0, 0), (0, 0))).reshape(npair * FF_STRIDE, L, D)
    ff = jnp.pad(ff, ((0, LANES - npair * FF_STRIDE), (0, 0), (0, 0)))
    return jnp.swapaxes(jnp.concatenate([w_t[:ORIG_FOX], w_t[ORIG_REST:], ff], axis=0), 0, 1)


def _from_aligned(dw_t):
    n, _, D = dw_t.shape
    npair = FOX_HEADS // 2
    ff = dw_t[:, PM:PM + npair * FF_STRIDE].reshape(n, npair, FF_STRIDE, D)[:, :, :2].reshape(n, FOX_HEADS, D)
    return jnp.swapaxes(jnp.concatenate([dw_t[:, :ORIG_FOX], ff, dw_t[:, ORIG_FOX:PM]], axis=1), 0, 1)


RELAY_ROWS = 256


def _rows_first(stack_m, stack_f, got_m, got_f):
    n, _, D = got_m.shape
    npair = FOX_HEADS // 2
    first_late = ORIG_FOX // RELAY_ROWS

    def body(c_ref, m_ref, f_ref, gm_ref, gf_ref, out_ref, buf_ref, ff_ref, sem, ff_sem):
        i = pl.program_id(0)
        for l in range(n):
            buf_ref[:, l, :] = m_ref[l] + gm_ref[l].astype(F32)
        start = pl.multiple_of(i * RELAY_ROWS, FOX_HEADS) + jnp.where(i >= first_late, FOX_HEADS, 0)
        main = pltpu.make_async_copy(buf_ref, out_ref.at[pl.ds(start, RELAY_ROWS)], sem)
        main.start()

        @pl.when(i == 0)
        def _():
            for l in range(n):
                for p in range(npair):
                    rows = slice(FF_STRIDE * p, FF_STRIDE * p + 2)
                    ff_ref[2 * p:2 * p + 2, l, :] = f_ref[l, rows, :] + gf_ref[l, rows, :].astype(F32)
            ff = pltpu.make_async_copy(ff_ref, out_ref.at[pl.ds(ORIG_FF, FOX_HEADS)], ff_sem)
            ff.start()
            ff.wait()

        main.wait()

    grid_spec = pltpu.PrefetchScalarGridSpec(
        num_scalar_prefetch=1, grid=(PM // RELAY_ROWS,),
        in_specs=[pl.BlockSpec((n, RELAY_ROWS, D), lambda i, c: (c[0], i, 0)), pl.BlockSpec((n, LANES, D), lambda i, c: (c[0], 0, 0)),
                  pl.BlockSpec((n, RELAY_ROWS, D), lambda i, c: (0, i, 0)), pl.BlockSpec((n, LANES, D), lambda i, c: (0, 0, 0))],
        out_specs=pl.BlockSpec(memory_space=pl.ANY),
        scratch_shapes=[pltpu.VMEM((RELAY_ROWS, n, D), F32), pltpu.VMEM((FOX_HEADS, n, D), F32),
                        pltpu.SemaphoreType.DMA, pltpu.SemaphoreType.DMA])
    return pl.pallas_call(
        body, name="rs_rows_first", grid_spec=grid_spec, out_shape=jax.ShapeDtypeStruct((D_IN, n, D), F32),
        compiler_params=_cparams(dimension_semantics=("arbitrary",)),
    )(lax.axis_index("c").astype(jnp.int32).reshape(1), stack_m, stack_f, got_m, got_f)


def _half_layers(name, stack, got, also_bf16=True):
    L, R, C = stack.shape
    half = L // 2
    tr = min(256, R)
    c = lax.axis_index("c")
    which = ((1 - c) if got is None else c).astype(jnp.int32).reshape(1)

    def body(c_ref, x_ref, *refs):
        if got is None:
            refs[0][...] = x_ref[...].astype(BF16)
        else:
            acc = x_ref[...] + refs[0][...].astype(F32)
            refs[1][...] = acc
            if also_bf16:
                refs[2][...] = acc.astype(BF16)

    plain = pl.BlockSpec((1, tr, C), lambda l, i, c_ref: (l, i, 0))
    picked = pl.BlockSpec((1, tr, C), lambda l, i, c_ref: (c_ref[0] * half + l, i, 0))
    shp = lambda dt: jax.ShapeDtypeStruct((half, R, C), dt)
    out_shape = [shp(BF16)] if got is None else [shp(F32)] + ([shp(BF16)] if also_bf16 else [])
    grid_spec = pltpu.PrefetchScalarGridSpec(
        num_scalar_prefetch=1, grid=(half, R // tr),
        in_specs=[picked] + ([] if got is None else [plain]), out_specs=[plain] * len(out_shape))
    return pl.pallas_call(
        body, name=name, grid_spec=grid_spec, out_shape=out_shape,
        compiler_params=_cparams(dimension_semantics=("arbitrary", "arbitrary")),
    )(which, stack, *([] if got is None else [got]))


def _reduce_scatter(stack_m, stack_f, stack_o, shard_cols, shard_rows):
    j = _chip_index()
    half = DEPTH // 2
    stacks = (stack_m, stack_f, stack_o)
    give = [_half_layers("rs_give", s, None)[0] for s in stacks]
    got = _exchange("rs_d2d", give, (FLIP_C,) * len(stacks))
    o32, obf = _half_layers("rs_add_chip", stack_o, got[2])
    d_model = stack_m.shape[2]
    in32 = _rows_first(stack_m, stack_f, got[0], got[1]).reshape(4, shard_cols, half, d_model)

    def out_shards(o):
        return jnp.moveaxis(o.reshape(half, 4, shard_rows, o.shape[-1]), 1, 0)

    chip = [(in32, in32.astype(BF16), 0), (out_shards(o32), out_shards(obf), 1)]
    shard = lambda a, idx: lax.dynamic_index_in_dim(a, idx, axis=0, keepdims=False)
    via = []
    for _, bf, axis in chip:
        diag = shard(bf, j ^ 3)
        cut = diag.shape[axis] // 2
        via += [lax.slice_in_dim(diag, 0, cut, axis=axis), lax.slice_in_dim(diag, cut, 2 * cut, axis=axis)]
    handed = _exchange("rs_via", via, (FLIP_X, FLIP_Y) * len(chip))
    sends = []
    for a, (f32_sum, _, axis) in enumerate(chip):
        sends.append(_add_half_along("rs_add_via", f32_sum, handed[2 * a + 1], axis, 1, pick=j ^ 2))
        sends.append(_add_half_along("rs_add_via", f32_sum, handed[2 * a], axis, 0, pick=j ^ 1))
    got = _exchange("rs_ici", sends, (FLIP_X, FLIP_Y) * len(chip))
    mine_in = _add_rows("rs_add_in", chip[0][0], list(got[0:2]), pick=j)
    mine_out = _add_into_half("rs_add_out", shard(chip[1][0], j), list(got[2:4]))
    sib_in, g_out = _share_halves(mine_in, mine_out)
    return (mine_in, sib_in), g_out


def _picked(spec, pick):
    return pl.BlockSpec((None,) + tuple(spec.block_shape), lambda *a: (a[-1][0],) + tuple(spec.index_map(*a[:-1])))


def _add_half_along(name, base, extra, axis, which, pick=None):
    shape = base.shape if pick is None else base.shape[1:]
    lanes = min(ROW_LANE_CHUNK, shape[2])
    assert shape[axis] == 2 * extra.shape[axis]
    blk = tuple(shape[d] // 2 if d == axis else shape[d] for d in range(2)) + (lanes,)

    def body(*refs):
        b_ref, e_ref, o_ref = refs[-3:]
        x = b_ref[...]
        o_ref[...] = jnp.where(pl.program_id(0) == which, x + e_ref[...].astype(F32), x).astype(BF16)

    at = lambda i, k, *_: (i, 0, k) if axis == 0 else (0, i, k)
    bspec, espec = pl.BlockSpec(blk, at), pl.BlockSpec(blk, lambda i, k, *_: (0, 0, k))
    kw = dict(out_shape=jax.ShapeDtypeStruct(shape, BF16), name=name, compiler_params=_cparams(dimension_semantics=("arbitrary", "arbitrary")))
    grid = (2, shape[2] // lanes)
    if pick is None:
        return pl.pallas_call(body, grid=grid, in_specs=[bspec, espec], out_specs=bspec, **kw)(base, extra)
    grid_spec = pltpu.PrefetchScalarGridSpec(num_scalar_prefetch=1, grid=grid, in_specs=[_picked(bspec, pick), espec], out_specs=bspec)
    return pl.pallas_call(body, grid_spec=grid_spec, **kw)(pick.astype(jnp.int32).reshape(1), base, extra)


def _add_rows(name, first, others, pick=None):
    n = len(others)
    shape = first.shape if pick is None else first.shape[1:]

    def body(*refs):
        refs = refs[-(n + 2):]
        acc = refs[0][...]
        for r in refs[1:1 + n]:
            acc = acc + r[...].astype(F32)
        refs[1 + n][...] = acc

    grid, spec = _row_lane_blocks(shape)
    sp = spec(shape[1])
    kw = dict(out_shape=jax.ShapeDtypeStruct(shape, F32), name=name, compiler_params=_cparams(dimension_semantics=("arbitrary", "arbitrary")))
    if pick is None:
        return pl.pallas_call(body, grid=grid, in_specs=[sp] * (1 + n), out_specs=sp, **kw)(first, *others)
    grid_spec = pltpu.PrefetchScalarGridSpec(num_scalar_prefetch=1, grid=grid, in_specs=[_picked(sp, pick)] + [sp] * n, out_specs=sp)
    return pl.pallas_call(body, grid_spec=grid_spec, **kw)(pick.astype(jnp.int32).reshape(1), first, *others)


ROW_LANE_CHUNK = 256


def _row_lane_blocks(shape):
    rows, _, C = shape
    tr = rows // 2 if rows % 2 == 0 and rows > 64 else rows
    lanes = min(ROW_LANE_CHUNK, C)
    return (rows // tr, C // lanes), lambda n_mid: pl.BlockSpec((tr, n_mid, lanes), lambda i, k, *_: (i, 0, k))


def _add_into_half(name, first, others):
    half, rows, C = first.shape
    tr = min(256, rows)
    n = len(others)

    def body(c_ref, *refs):
        acc = refs[0][...]
        for r in refs[1:1 + n]:
            acc = acc + r[...].astype(F32)
        refs[1 + n][...] = acc

    grid_spec = pltpu.PrefetchScalarGridSpec(
        num_scalar_prefetch=1, grid=(half, rows // tr),
        in_specs=[pl.BlockSpec((1, tr, C), lambda l, i, c_ref: (l, i, 0))] * (1 + n),
        out_specs=pl.BlockSpec((1, tr, C), lambda l, i, c_ref: (c_ref[0] * half + l, i, 0)))
    return pl.pallas_call(
        body, name=name, grid_spec=grid_spec, out_shape=jax.ShapeDtypeStruct((2 * half, rows, C), F32),
        compiler_params=_cparams(dimension_semantics=("arbitrary", "arbitrary")),
    )(lax.axis_index("c").astype(jnp.int32).reshape(1), first, *others)


def _share_halves(mine, buf):
    half = DEPTH // 2

    def body(mine_ref, buf_in, sib_ref, buf_ref, send_sems, recv_sems):
        lay = pl.ds(half * lax.axis_index("c"), half)
        copies = [pltpu.make_async_remote_copy(src_ref=src, dst_ref=dst, send_sem=send_sems.at[k], recv_sem=recv_sems.at[k],
                                               device_id=_peer(FLIP_C), device_id_type=MESH)
                  for k, (src, dst) in enumerate(((mine_ref, sib_ref), (buf_ref.at[lay], buf_ref.at[lay])))]
        for cp in copies:
            cp.start()
        for cp in copies:
            cp.wait()

    anyspec = pl.BlockSpec(memory_space=pl.ANY)
    return pl.pallas_call(
        body, name="rs_share", in_specs=[anyspec] * 2, out_specs=[anyspec] * 2,
        out_shape=[jax.ShapeDtypeStruct(mine.shape, mine.dtype), jax.ShapeDtypeStruct(buf.shape, buf.dtype)],
        input_output_aliases={1: 1},
        scratch_shapes=[pltpu.SemaphoreType.DMA((2,)), pltpu.SemaphoreType.DMA((2,))],
    )(mine, buf)


def _adamw_halves(w, g_mine, g_sib, m, v):
    half = g_mine.shape[1]

    def body(c_ref, w_ref, gm_ref, gs_ref, m_ref, v_ref, g_ref, d_ref, nm_ref, nv_ref):
        first = c_ref[0] == 0
        gm, gs = gm_ref[...], gs_ref[...]
        for h, gv in enumerate((jnp.where(first, gm, gs), jnp.where(first, gs, gm))):
            lay = slice(half * h, half * (h + 1))
            g_ref[:, lay, :] = gv
            d_ref[:, lay, :], nm_ref[:, lay, :], nv_ref[:, lay, :] = _adam_update(w_ref[:, lay, :], gv, m_ref[:, lay, :], v_ref[:, lay, :])

    grid, spec = _row_lane_blocks(w.shape)
    full, part = spec(w.shape[1]), spec(half)
    grid_spec = pltpu.PrefetchScalarGridSpec(num_scalar_prefetch=1, grid=grid, in_specs=[full, part, part, full, full], out_specs=[full] * 4)
    return pl.pallas_call(
        body, name="adamw_halves", grid_spec=grid_spec, out_shape=[jax.ShapeDtypeStruct(w.shape, F32)] * 4,
        compiler_params=_cparams(dimension_semantics=("arbitrary", "arbitrary")),
    )(lax.axis_index("c").astype(jnp.int32).reshape(1), w, g_mine, g_sib, m, v)


def _all_reduce_small(x):
    x = _exchange_add("ar_c", x, FLIP_C)
    x = _exchange_add("ar_y", x, FLIP_Y)
    return _exchange_add("ar_x", x, FLIP_X)


def _blocks(S):
    return dict(tm=min(512, S), tm_proj=min(1024, S), ts=min(512, S), tq=min(512, S), tq_big=min(1024, S), tk=min(512, S), tks=min(256, S))


def _pair_pad(vec):
    npair = FOX_HEADS // 2
    v = jnp.pad(vec.reshape(npair, 2), ((0, 0), (0, FF_STRIDE - 2))).reshape(1, npair * FF_STRIDE)
    return jnp.pad(v, ((0, 0), (0, LANES - npair * FF_STRIDE)))


def _pair_unpad(row):
    npair = FOX_HEADS // 2
    return row[0, :npair * FF_STRIDE].reshape(npair, FF_STRIDE)[:, :2].reshape(FOX_HEADS)


def _pool_blockdiag(w_pool):
    g, cg, _ = w_pool.shape
    eye = jnp.eye(g, dtype=w_pool.dtype)
    return jnp.einsum("gh,gcd->gchd", eye, w_pool).reshape(g * cg, g * cg)


QK_BOUND_SLACK = 1.05


def _layer_params(norm_g, b_f, q_norm_g, k_norm_g, w_pool, pool_scale):
    qk_bound = QK_BOUND_SLACK * HEAD_DIM * QK_SCALE * jnp.max(jnp.abs(q_norm_g)) * jnp.max(jnp.abs(k_norm_g))
    return dict(g=norm_g.reshape(1, -1), qg=jnp.tile(q_norm_g, FOX_HEADS).reshape(1, FOX_W), kg=jnp.tile(k_norm_g, FOX_HEADS).reshape(1, FOX_W),
                bfp=_pair_pad(b_f), wpd=_pool_blockdiag(w_pool).astype(BF16), ps=pool_scale.reshape(1, POOL_W),
                qkb=jnp.full((1, LANES), qk_bound, F32))


def _layer_fwd(x, wt_all, w_out, layer, prm, bs):
    projm, ffo, h = _inproj(x, prm["g"], wt_all, layer, tm=bs["tm_proj"], tn=PROJ_TN)
    qn, ka, kb, v, sq, sk, sv, pooled, yp, pm = _prep(projm, ffo, prm["qg"], prm["kg"], prm["bfp"], prm["wpd"], prm["ps"], ts=bs["ts"])
    o, lse, fm = _fox_fwd(qn, ka, kb, v, projm, prm["qkb"], tq=bs["tq"], tk=bs["tk"])
    so, sm = _sb_fwd(sq, sk, sv, projm, tq=bs["tq"], tk=bs["tks"])
    y = _outproj(x, fm, pm, sm, w_out, layer, tm=bs["tm_proj"])
    saved = dict(x=x, projm=projm, ffo=ffo, h=h, qn=qn, ka=ka, kb=kb, v=v, sq=sq, sk=sk, sv=sv, pooled=pooled, yp=yp,
                 o=o, lse=lse, so=so, fm=fm, pm=pm, sm=sm)
    return y, saved


def _layer_bwd(dy, wt_all, w_out, prm, sv_, bs, layer, stacks):
    dmix, stack_o = _outproj_bwd(dy, sv_["fm"], sv_["pm"], sv_["sm"], w_out, layer, None if stacks is None else stacks[2:], tm=bs["tm_proj"])
    dqn, dkn, dv, dfg, dct, dcr = _fox_bwd(sv_["qn"], sv_["ka"], sv_["kb"], sv_["v"], sv_["o"], sv_["lse"], dmix, sv_["projm"],
                                      prm["qkb"], tq=bs["tq_big"], tk=bs["tk"])
    dsq, dsk, dsv, dsg = _sb_bwd(sv_["sq"], sv_["sk"], sv_["sv"], sv_["so"], dmix, sv_["projm"], tq=bs["tks"], tk=bs["tks"])
    dproj, dqg, dkg, dbf, dwp, dps = _prep_bwd(sv_["projm"], sv_["ffo"], dqn, dkn, dct, dcr, dv, dfg, dsq, dsk, dsv, dsg, dmix,
                                               sv_["pooled"], sv_["yp"], prm["qg"], prm["kg"], prm["bfp"], prm["wpd"], prm["ps"], ts=bs["ts"])
    stack_m, stack_f = _inproj_dw(sv_["h"], dproj, layer, None if stacks is None else stacks[:2], ts=bs["tm_proj"], tn=PROJ_TN)
    dx, dg = _inproj_dx(dproj, wt_all, layer, sv_["x"], prm["g"], dy, tm=bs["tm"])
    grads = dict(
        norm_g=dg[0],
        b_f=_pair_unpad(dbf), q_norm_g=dqg.reshape(FOX_HEADS, HEAD_DIM).sum(0), k_norm_g=dkg.reshape(FOX_HEADS, HEAD_DIM).sum(0),
        w_pool=jnp.stack([dwp[HEAD_DIM * g:HEAD_DIM * (g + 1), HEAD_DIM * g:HEAD_DIM * (g + 1)] for g in range(4)]),
        pool_scale=dps[0])
    return dx, grads, (stack_m, stack_f, stack_o)


def _local_step(x, target, wt_all, w_out, norm_g, b_f, q_norm_g, k_norm_g, w_pool, pool_scale):
    S, D = x.shape
    bs = _blocks(S)
    prms = [_layer_params(norm_g[l], b_f[l], q_norm_g[l], k_norm_g[l], w_pool[l], pool_scale[l]) for l in range(DEPTH)]
    saved = []
    y = x
    for l in range(DEPTH):
        y, s_ = _layer_fwd(y, wt_all, w_out, l, prms[l], bs)
        saved.append(s_)
    dy, sq = _loss_head(y, target, tm=bs["tm"])
    loss = 0.5 * jnp.sum(sq) / D
    grads = [None] * DEPTH
    stacks = None
    for l in reversed(range(DEPTH)):
        dy, grads[l], stacks = _layer_bwd(dy, wt_all, w_out, prms[l], saved[l], bs, l, stacks)
    stacked = {k: jnp.stack([g[k] for g in grads]) for k in grads[0]}
    return loss, dy, stacked, stacks


SMALL = ("norm_g", "b_f", "q_norm_g", "k_norm_g", "w_pool", "pool_scale")


def _pack_small(gr):
    flat = jnp.concatenate([gr[k].reshape(-1) for k in SMALL])
    pad = (-flat.shape[0]) % (8 * LANES)
    return jnp.pad(flat, (0, pad)).reshape(-1, LANES)


def _unpack_small(packed, like):
    flat = packed.reshape(-1)
    out, off = {}, 0
    for k in SMALL:
        n = like[k].size
        out[k] = flat[off:off + n].reshape(like[k].shape)
        off += n
    return out


def kernel(x, norm_g, w_in, b_f, q_norm_g, k_norm_g, w_pool, pool_scale, w_out, loss_target, m_norm_g, m_w_in, m_b_f, m_q_norm_g, m_k_norm_g, m_w_pool, m_pool_scale, m_w_out, v_norm_g, v_w_in, v_b_f, v_q_norm_g, v_k_norm_g, v_w_pool, v_pool_scale, v_w_out):
    weights = dict(norm_g=norm_g, w_in=w_in, b_f=b_f, q_norm_g=q_norm_g, k_norm_g=k_norm_g, w_pool=w_pool, pool_scale=pool_scale, w_out=w_out)
    mom_m = dict(norm_g=m_norm_g, w_in=m_w_in, b_f=m_b_f, q_norm_g=m_q_norm_g, k_norm_g=m_k_norm_g, w_pool=m_w_pool, pool_scale=m_pool_scale, w_out=m_w_out)
    mom_v = dict(norm_g=v_norm_g, w_in=v_w_in, b_f=v_b_f, q_norm_g=v_q_norm_g, k_norm_g=v_k_norm_g, w_pool=v_w_pool, pool_scale=v_pool_scale, w_out=v_w_out)
    shard_cols = w_in.shape[2]
    shard_rows = w_out.shape[1]

    cols_first = lambda a: jnp.transpose(a, (2, 0, 1))
    w_in_t = cols_first(w_in)
    w_in_t_full, w_out_full = _gather_weights(w_in_t, w_out)
    wt_all = _to_aligned(w_in_t_full)
    loss, dx, gr, stacks = _local_step(x[0], loss_target[0], wt_all, w_out_full, norm_g, b_f, q_norm_g, k_norm_g, w_pool, pool_scale)
    loss = lax.psum(loss, ("x", "y", "c"))

    (g_in_mine, g_in_sib), g_w_out = _reduce_scatter(*stacks, shard_cols, shard_rows)
    small = _unpack_small(_all_reduce_small(_pack_small(gr)), {k: weights[k] for k in SMALL})
    grad_w = dict(small, w_out=g_w_out)

    names = ("norm_g", "w_in", "b_f", "q_norm_g", "k_norm_g", "w_pool", "pool_scale", "w_out")
    upd = {k: _adamw_nd(weights[k], grad_w[k], mom_m[k], mom_v[k]) for k in names if k != "w_in"}
    in_t = _adamw_halves(w_in_t, g_in_mine, g_in_sib, cols_first(mom_m["w_in"]), cols_first(mom_v["w_in"]))
    grad_w["w_in"], *upd["w_in"] = [jnp.transpose(a, (1, 2, 0)) for a in in_t]
    return (loss, dx[None], *[grad_w[k] for k in names], *[upd[k][0] for k in names], *[upd[k][1] for k in names], *[upd[k][2] for k in names])
```

```python
import functools

import jax
import jax.numpy as jnp
from jax import lax
from jax.experimental import pallas as pl
from jax.experimental.pallas import tpu as pltpu

F32 = jnp.float32
BF16 = jnp.bfloat16

DEPTH = 4
HEAD_DIM = 64
FOX_HEADS = 8
SB_HEADS = 4
FOX_W = FOX_HEADS * HEAD_DIM
SB_W = SB_HEADS * HEAD_DIM
POOL_W = 256
POOL_WINDOWS = (2, 4, 8, 16)
POOL_HALO = 16
D_MIX = FOX_W + POOL_W + SB_W
EPS = 1e-6
NEG = -1e30
QK_SCALE = HEAD_DIM ** -0.5

ORIG_FOX = 4 * FOX_W
ORIG_FF = ORIG_FOX
ORIG_REST = ORIG_FF + FOX_HEADS
D_IN = ORIG_REST + 2 * POOL_W + 4 * SB_W

C_FQ, C_FK, C_FV, C_FG = 0, FOX_W, 2 * FOX_W, 3 * FOX_W
C_PX = 4 * FOX_W
C_PG = C_PX + POOL_W
C_SQ = C_PG + POOL_W
C_SK, C_SV, C_SG = C_SQ + SB_W, C_SQ + 2 * SB_W, C_SQ + 3 * SB_W
PM = C_SG + SB_W
LANES = 128
LANE_SHIFT = 7
HEAD_SHIFT = 6
PW = PM + LANES
FF_STRIDE = 8
AUG = 3

ADAM_LR = 0.001
ADAM_B1 = 0.9
ADAM_B2 = 0.999
ADAM_EPS = 1e-08
ADAM_WD = 0.01
ADAM_STEP = 10

VMEM_LIMIT = 48 * 1024 * 1024
PROJ_TN = PM // 2


def _cparams(**kw):
    return pltpu.CompilerParams(vmem_limit_bytes=VMEM_LIMIT, **kw)


def _dot(a, b):
    return jnp.dot(a, b, preferred_element_type=F32)


def _dot_nt(a, b):
    return lax.dot_general(a, b, (((1,), (1,)), ((), ())), preferred_element_type=F32)


def _dot_tn(a, b):
    return lax.dot_general(a, b, (((0,), (0,)), ((), ())), preferred_element_type=F32)


def _split2(x):
    hi = x.astype(BF16)
    lo = (x - hi.astype(F32)).astype(BF16)
    return hi, lo


def _split3(x):
    hi = x.astype(BF16)
    r = x - hi.astype(F32)
    mid = r.astype(BF16)
    lo = (r - mid.astype(F32)).astype(BF16)
    return hi, mid, lo


def _dot_exact_rhs(x, m):
    hi, mid, lo = _split3(x)
    return _dot(hi, m) + _dot(mid, m) + _dot(lo, m)


def _dot_exact_lhs(m, x):
    hi, mid, lo = _split3(x)
    return _dot(m, hi) + _dot(m, mid) + _dot(m, lo)


def _sigmoid(x):
    return 1.0 / (1.0 + jnp.exp(-x))


def _silu_pair(x):
    s = _sigmoid(x)
    return x * s, s * (1.0 + x * (1.0 - s))


def _iota(shape, dim):
    return lax.broadcasted_iota(jnp.int32, shape, dim)


def _ones_where(cond):
    return jnp.where(cond, 1.0, 0.0).astype(BF16)


GROUP_SLAB = 256


def _head_blockdiag():
    rows, cols = _iota((2 * GROUP_SLAB, GROUP_SLAB), 0) & (GROUP_SLAB - 1), _iota((2 * GROUP_SLAB, GROUP_SLAB), 1)
    return _ones_where((rows >> HEAD_SHIFT) == (cols >> HEAD_SHIFT))


def _group_sum(x, bd):
    hi, lo = _split2(x)
    slabs = [_dot(jnp.concatenate([hi[:, s:s + GROUP_SLAB], lo[:, s:s + GROUP_SLAB]], axis=1), bd) for s in range(0, x.shape[1], GROUP_SLAB)]
    return jnp.concatenate(slabs, axis=1)


def _lane_pick(x, lane_idx, lane):
    return jnp.sum(jnp.where(lane_idx == lane, x, 0.0), axis=1, keepdims=True)


def _inproj(x, g, wt_all, layer, *, tm, tn):
    S, D = x.shape
    nj = PM // tn

    def body(x_ref, g_ref, w_ref, wff_ref, proj_ref, ff_ref, h_ref):
        @pl.when(pl.program_id(1) == 0)
        def _():
            xf = x_ref[...]
            ms = jnp.mean(xf * xf, axis=-1, keepdims=True)
            h = (xf * lax.rsqrt(ms + EPS) * g_ref[...]).astype(BF16)
            h_ref[...] = h
            ff_ref[...] = _dot_nt(h, wff_ref[...])

        proj_ref[...] = _dot_nt(h_ref[...], w_ref[...])

    return pl.pallas_call(
        body, name="inproj", grid=(S // tm, nj),
        in_specs=[pl.BlockSpec((tm, D), lambda i, j: (i, 0)),
                  pl.BlockSpec((1, D), lambda i, j: (0, 0)),
                  pl.BlockSpec((None, tn, D), lambda i, j: (layer, j, 0)),
                  pl.BlockSpec((None, LANES, D), lambda i, j: (layer, PM // LANES, 0))],
        out_specs=[pl.BlockSpec((tm, tn), lambda i, j: (i, j)),
                   pl.BlockSpec((tm, LANES), lambda i, j: (i, 0)),
                   pl.BlockSpec((tm, D), lambda i, j: (i, 0))],
        out_shape=[jax.ShapeDtypeStruct((S, PM), F32), jax.ShapeDtypeStruct((S, LANES), F32),
                   jax.ShapeDtypeStruct((S, D), BF16)],
        compiler_params=_cparams(dimension_semantics=("arbitrary", "arbitrary")),
    )(x, g, wt_all, wt_all)


def _pool_group_select(lane_group, vals):
    return jnp.where(lane_group == 0, vals[0], jnp.where(lane_group == 1, vals[1], jnp.where(lane_group == 2, vals[2], vals[3])))


def _prep(projm, ffo, qg, kg, bfp, wpd, ps, *, ts):
    S = projm.shape[0]
    nb = S // ts
    hb = ts // POOL_HALO

    def body(fq_ref, fk_ref, fv_ref, pp_ref, halo_ref, ff_ref, sq_ref, sk_ref, sv_ref,
             qg_ref, kg_ref, bf_ref, wpd_ref, ps_ref,
             qn_ref, ka_ref, kb_ref, v_ref, sqo_ref, sko_ref, svo_ref, pooled_ref, yp_ref, pm_ref,
             carry_ref, c_ref, buf_ref):
        i = pl.program_id(0)
        bd = _head_blockdiag()
        normed = []
        for src, g_ref in ((fq_ref, qg_ref), (fk_ref, kg_ref)):
            q = src[...]
            ss = _group_sum(q * q, bd)
            normed.append(q * lax.rsqrt(ss * (1.0 / HEAD_DIM) + EPS) * g_ref[...])
        qn_ref[...] = (normed[0] * QK_SCALE).astype(BF16)
        kn = normed[1]
        v_ref[...] = fv_ref[...].astype(BF16)
        sqo_ref[...] = (sq_ref[...] * QK_SCALE).astype(BF16)
        sko_ref[...] = sk_ref[...].astype(BF16)
        svo_ref[...] = sv_ref[...].astype(BF16)

        @pl.when(i == 0)
        def _():
            carry_ref[...] = jnp.zeros_like(carry_ref)

        z = ff_ref[...] + bf_ref[...]
        lf = jnp.minimum(z, 0.0) - jnp.log(1.0 + jnp.exp(-jnp.abs(z)))
        tri = _ones_where(_iota((ts, ts), 1) <= _iota((ts, ts), 0))
        c = _dot_exact_lhs(tri, lf) + carry_ref[...]
        c_ref[...] = c
        carry_ref[...] = c_ref[ts - 1:ts, :]
        parts = jnp.concatenate(_split3(-c), axis=1)
        row = _iota((AUG * LANES, FOX_W), 0)
        col = _iota((AUG * LANES, FOX_W), 1)
        part, src = row >> LANE_SHIFT, row & (LANES - 1)
        pair, off = col >> LANE_SHIFT, col & (LANES - 1)
        sel_a = _ones_where((src == FF_STRIDE * pair) & (off == HEAD_DIM + part))
        sel_b = _ones_where((src == FF_STRIDE * pair + 1) & (off == part))
        first_half = (_iota((1, FOX_W), 1) & HEAD_DIM) == 0
        ka_ref[...] = jnp.where(first_half, kn, _dot(parts, sel_a)).astype(BF16)
        kb_ref[...] = jnp.where(first_half, _dot(parts, sel_b), kn).astype(BF16)

        x = pp_ref[:, 0:POOL_W]
        pg = pp_ref[:, POOL_W:2 * POOL_W]
        halo = jnp.where(i > 0, halo_ref[:, 0:POOL_W], 0.0)
        buf_ref[0:POOL_HALO, :] = halo
        buf_ref[POOL_HALO:POOL_HALO + ts, :] = x
        acc = x
        snaps = []
        for d in range(1, POOL_HALO):
            acc = acc + buf_ref[pl.ds(POOL_HALO - d, ts), :]
            if d + 1 in POOL_WINDOWS:
                snaps.append(acc)
        lane_group = _iota((1, POOL_W), 1) >> HEAD_SHIFT
        wsum = _pool_group_select(lane_group, snaps)
        wlen = _pool_group_select(lane_group, [float(w) for w in POOL_WINDOWS])
        tpos = (i * ts + _iota((ts, 1), 0) + 1).astype(F32)
        pooled = wsum / jnp.minimum(tpos, wlen) - x
        pb = pooled.astype(BF16)
        pooled_ref[...] = pb
        yp = _dot(pb, wpd_ref[...])
        yp_ref[...] = yp
        pm_ref[...] = (yp * ps_ref[...] * (pg * _sigmoid(pg))).astype(BF16)

    blk = lambda w, c: pl.BlockSpec((ts, w), lambda i: (i, c))
    full = lambda a: pl.BlockSpec(a.shape, lambda i: (0,) * a.ndim)
    out_shapes = [
        jax.ShapeDtypeStruct((S, FOX_W), BF16), jax.ShapeDtypeStruct((S, FOX_W), BF16), jax.ShapeDtypeStruct((S, FOX_W), BF16),
        jax.ShapeDtypeStruct((S, FOX_W), BF16),
        jax.ShapeDtypeStruct((S, SB_W), BF16), jax.ShapeDtypeStruct((S, SB_W), BF16), jax.ShapeDtypeStruct((S, SB_W), BF16),
        jax.ShapeDtypeStruct((S, POOL_W), BF16), jax.ShapeDtypeStruct((S, POOL_W), F32), jax.ShapeDtypeStruct((S, POOL_W), BF16),
    ]
    out_specs = [
        blk(FOX_W, 0), blk(FOX_W, 0), blk(FOX_W, 0), blk(FOX_W, 0),
        blk(SB_W, 0), blk(SB_W, 0), blk(SB_W, 0),
        blk(POOL_W, 0), blk(POOL_W, 0), blk(POOL_W, 0),
    ]
    return pl.pallas_call(
        body, name="prep", grid=(nb,),
        in_specs=[blk(FOX_W, C_FQ // FOX_W), blk(FOX_W, C_FK // FOX_W), blk(FOX_W, C_FV // FOX_W), blk(2 * POOL_W, C_PX // (2 * POOL_W)),
                  pl.BlockSpec((POOL_HALO, 2 * POOL_W), lambda i: (jnp.maximum(i * hb - 1, 0), C_PX // (2 * POOL_W))),
                  blk(LANES, 0),
                  blk(SB_W, C_SQ // SB_W), blk(SB_W, C_SK // SB_W), blk(SB_W, C_SV // SB_W),
                  full(qg), full(kg), full(bfp), full(wpd), full(ps)],
        out_specs=out_specs, out_shape=out_shapes,
        scratch_shapes=[pltpu.VMEM((1, LANES), F32), pltpu.VMEM((ts, LANES), F32), pltpu.VMEM((ts + POOL_HALO, POOL_W), F32)],
        compiler_params=_cparams(dimension_semantics=("arbitrary",)),
    )(projm, projm, projm, projm, projm, ffo, projm, projm, projm, qg, kg, bfp, wpd, ps)


def _pair_masks(x):
    ma = _iota((1, LANES), 1) < HEAD_DIM
    zero = jnp.zeros_like(x)
    return jnp.where(ma, x, zero), jnp.where(ma, zero, x)


DIAG_TILE = 256


def _diag_tiles(tq, size=DIAG_TILE):
    size = min(tq, size)
    return [(t * size, size) for t in range(tq // size)]


def _put_rows(old, new, r0):
    return new if r0 == 0 else jnp.concatenate([old[:r0], new], axis=0)


def _aug_queries(q):
    lane = _iota((1, LANES), 1)
    one = jnp.ones_like(q)
    zero = jnp.zeros_like(q)
    qa = jnp.where(lane < HEAD_DIM, q, jnp.where(lane < HEAD_DIM + AUG, one, zero))
    qb = jnp.where(lane >= HEAD_DIM, q, jnp.where(lane < AUG, one, zero))
    return qa, qb


EXP_DEAD = -105.0
PACK = 16


def _fox_walk_left(nfull, tk, block, carry, k_refs, qk_bound, row_floor):
    lane = _iota((1, LANES), 1)

    def alive(h, jj, c):
        k0 = pl.multiple_of(jnp.maximum(nfull - 1 - jj, 0) * tk + tk - PACK, PACK)
        last = k_refs[h][pl.ds(k0, PACK), :].astype(F32)
        lo = HEAD_DIM if h == 0 else 0
        negc = jnp.sum(jnp.where((lane >= lo) & (lane < lo + AUG), last, 0.0), axis=1, keepdims=True)
        return qk_bound + jnp.max(negc) - row_floor(c)[h] >= EXP_DEAD

    def walk(heads, jj0, c0):
        def go_on(state):
            jj, c = state
            ok = jj < nfull
            for h in heads:
                ok = ok & alive(h, jj, c)
            return ok

        def step(state):
            jj, c = state
            return jj + 1, block(pl.multiple_of((nfull - 1 - jj) * tk, tk), tk, 0, c, False, heads)

        return lax.while_loop(go_on, step, (jj0, c0))

    jj_pair, carry = walk((0, 1), jnp.int32(0), carry)
    carry = walk((0,), jj_pair, carry)[1]
    return walk((1,), jj_pair, carry)[1]


def _fox_fwd(qn, ka, kb, v, projm, qkb, *, tq, tk):
    S = qn.shape[0]
    npair = FOX_HEADS // 2

    def body(q_ref, ka_ref, kb_ref, v_ref, fg_ref, qkb_ref, o_ref, lse_ref, fm_ref):
        qi = pl.program_id(1)
        lane = _iota((1, LANES), 1)
        ma = lane < HEAD_DIM
        qaug = _aug_queries(q_ref[...])
        k_refs = (ka_ref, kb_ref)

        def block(k0, tkl, r0, carry, masked, heads=(0, 1)):
            vb = v_ref[pl.ds(k0, tkl), :]
            if masked:
                mask = (k0 + _iota((tq - r0, tkl), 1)) <= (qi * tq + r0 + _iota((tq - r0, tkl), 0))
            scores = {h: _dot_nt(qaug[h][r0:], k_refs[h][pl.ds(k0, tkl), :]) for h in heads}
            new = list(carry)
            for h in heads:
                m, l, acc = [x[r0:] for x in carry[h]]
                s = jnp.where(mask, scores[h], NEG) if masked else scores[h]
                m_new = jnp.maximum(m, jnp.max(s, axis=1, keepdims=True))
                alpha = jnp.exp(m - m_new)
                p = jnp.exp(s - m_new)
                sub = (m_new, alpha * l + jnp.sum(p, axis=1, keepdims=True), alpha * acc + _dot(p.astype(BF16), vb))
                new[h] = tuple(_put_rows(old, x, r0) for old, x in zip(carry[h], sub))
            return tuple(new)

        carry = tuple((jnp.full((tq, 1), NEG, F32), jnp.zeros((tq, 1), F32), jnp.zeros((tq, LANES), F32)) for _ in range(2))
        for off, size in _diag_tiles(tq, tq):
            carry = block(pl.multiple_of(qi * tq + off, size), size, off, carry, True)
        carry = _fox_walk_left((qi * tq) // tk, tk, block, carry, k_refs, jnp.max(qkb_ref[...]),
                               lambda c: (jnp.min(c[0][0]), jnp.min(c[1][0])))
        (ma_, la, acca), (mb_, lb, accb) = carry
        o = jnp.where(ma, acca / la, accb / lb)
        o_ref[...] = o
        lse_ref[...] = jnp.where(ma, ma_ + jnp.log(la), mb_ + jnp.log(lb))
        fg = fg_ref[...]
        fm_ref[...] = (o * (fg * _sigmoid(fg))).astype(BF16)

    qblk = pl.BlockSpec((tq, LANES), lambda p, i: (i, p))
    kvblk = pl.BlockSpec((S, LANES), lambda p, i: (0, p))
    return pl.pallas_call(
        body, name="fox_fwd", grid=(npair, S // tq),
        in_specs=[qblk, kvblk, kvblk, kvblk,
                  pl.BlockSpec((tq, LANES), lambda p, i: (i, C_FG // LANES + p)),
                  pl.BlockSpec((1, LANES), lambda p, i: (0, 0))],
        out_specs=[qblk, qblk, qblk],
        out_shape=[jax.ShapeDtypeStruct((S, FOX_W), F32), jax.ShapeDtypeStruct((S, FOX_W), F32), jax.ShapeDtypeStruct((S, FOX_W), BF16)],
        compiler_params=_cparams(dimension_semantics=("arbitrary", "arbitrary")),
    )(qn, ka, kb, v, projm, qkb)


def _suffix_sums(x, tmat2):
    return _dot(jnp.concatenate(_split2(x), axis=1), tmat2)


def _suffix_matrix(tk, inclusive):
    rr, cc = _iota((2 * tk, tk), 0) & (tk - 1), _iota((2 * tk, tk), 1)
    return _ones_where(rr >= cc) if inclusive else _ones_where(rr > cc)


def _sb_scores(qh, kb, causal, tmat2, r_runs):
    heads = range(2)
    zs = [_dot_nt(qh[h], kb) for h in heads]
    nsps = [jnp.minimum(-z, 0.0) - jnp.log(1.0 + jnp.exp(-jnp.abs(z))) for z in zs]
    lbs = nsps if causal is None else [jnp.where(causal, n, 0.0) for n in nsps]
    rins = [_suffix_sums(lb, tmat2) for lb in lbs]
    args = [zs[h] + lbs[h] + (rins[h] + r_runs[h]) for h in heads]
    a_s = [jnp.exp(arg if causal is None else jnp.where(causal, arg, NEG)) for arg in args]
    return zs, nsps, lbs, a_s


def _sb_walk_left(nfull, tk, block, carry, running_sums):
    def alive(state):
        jj, c = state
        ra, rb = running_sums(c)
        return (jj < nfull) & (jnp.max(jnp.maximum(ra, rb)) >= EXP_DEAD)

    def step(state):
        jj, c = state
        return jj + 1, block(pl.multiple_of((nfull - 1 - jj) * tk, tk), 0, c, False)

    return lax.while_loop(alive, step, (jnp.int32(0), carry))[1]


def _sb_fwd(sq, sk, sv, projm, *, tq, tk):
    S = sq.shape[0]
    npair = SB_HEADS // 2

    def body(q_ref, k_ref, v_ref, sg_ref, o_ref, sm_ref):
        qi = pl.program_id(1)
        lane = _iota((1, LANES), 1)
        ma = lane < HEAD_DIM
        qh = _pair_masks(q_ref[...])
        tmat2 = _suffix_matrix(tk, inclusive=False)
        nfull = (qi * tq) // tk

        def block(k0, r0, carry, masked):
            nr = tq - r0
            kb = k_ref[pl.ds(k0, tk), :]
            vb = v_ref[pl.ds(k0, tk), :]
            causal = (k0 + _iota((nr, tk), 1)) < (qi * tq + r0 + _iota((nr, tk), 0)) if masked else None
            _, _, lbs, a_s = _sb_scores([q[r0:] for q in qh], kb, causal, tmat2, [carry[h][0][r0:] for h in range(2)])
            pv = _dot(jnp.concatenate([a.astype(BF16) for a in a_s], axis=0), vb)
            return tuple((_put_rows(carry[h][0], carry[h][0][r0:] + jnp.sum(lbs[h], axis=1, keepdims=True), r0),
                          _put_rows(carry[h][1], carry[h][1][r0:] + pv[h * nr:(h + 1) * nr], r0)) for h in range(2))

        carry = tuple((jnp.zeros((tq, 1), F32), jnp.zeros((tq, LANES), F32)) for _ in range(2))
        for off, size in reversed(_diag_tiles(tq)):
            assert size == tk
            carry = block(pl.multiple_of(qi * tq + off, tk), off, carry, True)
        (_, acca), (_, accb) = _sb_walk_left(nfull, tk, block, carry, lambda c: (c[0][0], c[1][0]))
        o = jnp.where(ma, acca, accb)
        o_ref[...] = o
        sg = sg_ref[...]
        sm_ref[...] = (o * (sg * _sigmoid(sg))).astype(BF16)

    qblk = pl.BlockSpec((tq, LANES), lambda p, i: (i, p))
    kvblk = pl.BlockSpec((S, LANES), lambda p, i: (0, p))
    return pl.pallas_call(
        body, name="sb_fwd", grid=(npair, S // tq),
        in_specs=[qblk, kvblk, kvblk, pl.BlockSpec((tq, LANES), lambda p, i: (i, C_SG // LANES + p))],
        out_specs=[qblk, qblk],
        out_shape=[jax.ShapeDtypeStruct((S, SB_W), F32), jax.ShapeDtypeStruct((S, SB_W), BF16)],
        compiler_params=_cparams(dimension_semantics=("arbitrary", "arbitrary")),
    )(sq, sk, sv, projm)


def _outproj(x, fm, pm, sm, w_out, layer, *, tm):
    S, D = x.shape

    def body(x_ref, fm_ref, pm_ref, sm_ref, w_ref, y_ref):
        y = x_ref[...] + _dot(fm_ref[...], w_ref[0:FOX_W, :])
        y = y + _dot(pm_ref[...], w_ref[FOX_W:FOX_W + POOL_W, :])
        y_ref[...] = y + _dot(sm_ref[...], w_ref[FOX_W + POOL_W:D_MIX, :])

    row = lambda w: pl.BlockSpec((tm, w), lambda i: (i, 0))
    return pl.pallas_call(
        body, name="outproj", grid=(S // tm,),
        in_specs=[row(D), row(FOX_W), row(POOL_W), row(SB_W), pl.BlockSpec((None, D_MIX, D), lambda i: (layer, 0, 0))],
        out_specs=row(D), out_shape=jax.ShapeDtypeStruct((S, D), F32),
        compiler_params=_cparams(dimension_semantics=("arbitrary",)),
    )(x, fm, pm, sm, w_out)


def _loss_head(y, target, *, tm):
    S, D = y.shape

    def body(y_ref, t_ref, dy_ref, sq_ref):
        @pl.when(pl.program_id(0) == 0)
        def _():
            sq_ref[...] = jnp.zeros_like(sq_ref)

        d = y_ref[...] - t_ref[...]
        dy_ref[...] = d * (1.0 / D)
        sq_ref[...] += jnp.sum(d * d, axis=0, keepdims=True)

    row = pl.BlockSpec((tm, D), lambda i: (i, 0))
    return pl.pallas_call(
        body, name="loss_head", grid=(S // tm,),
        in_specs=[row, row], out_specs=[row, pl.BlockSpec((1, D), lambda i: (0, 0))],
        out_shape=[jax.ShapeDtypeStruct((S, D), F32), jax.ShapeDtypeStruct((1, D), F32)],
        compiler_params=_cparams(dimension_semantics=("arbitrary",)),
    )(y, target)


def _outproj_bwd(dy, fm, pm, sm, w_out, layer, stacks, *, tm):
    S, D = dy.shape

    def body(dy_ref, fm_ref, pm_ref, sm_ref, w_ref, dm_ref, dw_ref):
        @pl.when(pl.program_id(0) == 0)
        def _():
            dw_ref[...] = jnp.zeros_like(dw_ref)

        dyb = dy_ref[...].astype(BF16)
        dm_ref[...] = _dot_nt(dyb, w_ref[...])
        dw_ref[0:FOX_W, :] += _dot_tn(fm_ref[...], dyb)
        dw_ref[FOX_W:FOX_W + POOL_W, :] += _dot_tn(pm_ref[...], dyb)
        dw_ref[FOX_W + POOL_W:D_MIX, :] += _dot_tn(sm_ref[...], dyb)

    row = lambda w: pl.BlockSpec((tm, w), lambda i: (i, 0))
    wspec = pl.BlockSpec((None, D_MIX, D), lambda i: (layer, 0, 0))
    return _stack_call(
        body, "outproj_bwd", (S // tm,), [row(D), row(FOX_W), row(POOL_W), row(SB_W), wspec], (dy, fm, pm, sm, w_out),
        [pl.BlockSpec((None, D_MIX, D), lambda i: (layer, 0, 0))], [(D_MIX, D)], stacks,
        plain_specs=[row(D_MIX)], plain_shapes=[jax.ShapeDtypeStruct((S, D_MIX), F32)],
        compiler_params=_cparams(dimension_semantics=("arbitrary",)))


def _fox_bwd(qn, ka, kb, v, o, lse, dmix, projm, qkb, *, tq, tk):
    S = qn.shape[0]
    npair = FOX_HEADS // 2

    def body(q_ref, ka_ref, kb_ref, v_ref, o_ref, lse_ref, dm_ref, fg_ref, qkb_ref,
             dq_ref, dk_ref, dv_ref, dfg_ref, dct_ref, dcr_ref):
        qi = pl.program_id(1)

        @pl.when(qi == 0)
        def _():
            dk_ref[...] = jnp.zeros_like(dk_ref)
            dv_ref[...] = jnp.zeros_like(dv_ref)
            dct_ref[...] = jnp.zeros_like(dct_ref)

        lane = _iota((1, LANES), 1)
        ma = lane < HEAD_DIM
        qh = _pair_masks(q_ref[...])
        qaug = _aug_queries(q_ref[...])
        k_refs = (ka_ref, kb_ref)
        lsev = lse_ref[...]
        lse = (_lane_pick(lsev, lane, 0), _lane_pick(lsev, lane, HEAD_DIM))
        fg = fg_ref[...]
        silu, dsilu = _silu_pair(fg)
        dm = dm_ref[...]
        ov = o_ref[...]
        do = dm * silu
        dfg_ref[...] = dm * ov * dsilu
        dd = do * ov
        dsum = (jnp.sum(jnp.where(ma, dd, 0.0), axis=1, keepdims=True), jnp.sum(jnp.where(ma, 0.0, dd), axis=1, keepdims=True))
        doh = _pair_masks(do.astype(BF16))

        def block(k0, tkl, r0, carry, masked, heads=(0, 1)):
            vb = v_ref[pl.ds(k0, tkl), :]
            if masked:
                mask = (k0 + _iota((tq - r0, tkl), 1)) <= (qi * tq + r0 + _iota((tq - r0, tkl), 0))
            kaugs = {h: k_refs[h][pl.ds(k0, tkl), :] for h in heads}
            scores = {h: _dot_nt(qaug[h][r0:], kaugs[h]) for h in heads}
            dps = {h: _dot_nt(doh[h][r0:], vb) for h in heads}
            ps, dss = [], []
            rows = [carry[1], carry[2]]
            for h in heads:
                s = jnp.where(mask, scores[h], NEG) if masked else scores[h]
                p = jnp.exp(s - lse[h][r0:])
                dsf = p * (dps[h] - dsum[h][r0:])
                dct_ref[0, h:h + 1, pl.ds(k0, tkl)] -= jnp.sum(dsf, axis=0, keepdims=True)
                rows[h] = _put_rows(carry[1 + h], carry[1 + h][r0:] + jnp.sum(dsf, axis=1, keepdims=True), r0)
                ps.append(p.astype(BF16))
                dss.append(dsf.astype(BF16))
            dv_ref[pl.ds(k0, tkl), :] += _dot_tn(jnp.concatenate(ps, axis=0), jnp.concatenate([doh[h][r0:] for h in heads], axis=0))
            dk_ref[pl.ds(k0, tkl), :] += _dot_tn(jnp.concatenate(dss, axis=0), jnp.concatenate([qh[h][r0:] for h in heads], axis=0))
            kh = jnp.concatenate([_pair_masks(kaugs[h])[h] for h in heads], axis=0)
            dq = _put_rows(carry[0], carry[0][r0:] + _dot(jnp.concatenate(dss, axis=1), kh), r0)
            return (dq, rows[0], rows[1])

        zcol = jnp.zeros((tq, 1), F32)
        carry = (jnp.zeros((tq, LANES), F32), zcol, zcol)
        for off, size in _diag_tiles(tq):
            carry = block(pl.multiple_of(qi * tq + off, size), size, off, carry, True)
        floors = (jnp.min(lse[0]), jnp.min(lse[1]))
        dq, rowa, rowb = _fox_walk_left((qi * tq) // tk, tk, block, carry, k_refs, jnp.max(qkb_ref[...]), lambda c: floors)
        dq_ref[...] = dq * QK_SCALE
        dcr_ref[0] = jnp.where(ma, rowa, rowb)

    qblk = pl.BlockSpec((tq, LANES), lambda p, i: (i, p))
    kvblk = pl.BlockSpec((S, LANES), lambda p, i: (0, p))
    f32out = jax.ShapeDtypeStruct((S, FOX_W), F32)
    ctblk = pl.BlockSpec((1, FF_STRIDE, S), lambda p, i: (p, 0, 0))
    return pl.pallas_call(
        body, name="fox_bwd", grid=(npair, S // tq),
        in_specs=[qblk, kvblk, kvblk, kvblk, qblk, qblk, qblk,
                  pl.BlockSpec((tq, LANES), lambda p, i: (i, C_FG // LANES + p)),
                  pl.BlockSpec((1, LANES), lambda p, i: (0, 0))],
        out_specs=[qblk, kvblk, kvblk, qblk, ctblk, pl.BlockSpec((1, tq, LANES), lambda p, i: (p, i, 0))],
        out_shape=[f32out, f32out, f32out, f32out, jax.ShapeDtypeStruct((npair, FF_STRIDE, S), F32),
                   jax.ShapeDtypeStruct((npair, S, LANES), F32)],
        compiler_params=_cparams(dimension_semantics=("arbitrary", "arbitrary")),
    )(qn, ka, kb, v, o, lse, dmix, projm, qkb)


def _sb_bwd(sq, sk, sv, o, dmix, projm, *, tq, tk):
    S = sq.shape[0]
    npair = SB_HEADS // 2
    mix0 = (FOX_W + POOL_W) // LANES

    def body(q_ref, k_ref, v_ref, o_ref, dm_ref, sg_ref, dq_ref, dk_ref, dv_ref, dsg_ref):
        qi = pl.program_id(1)

        @pl.when(qi == 0)
        def _():
            dk_ref[...] = jnp.zeros_like(dk_ref)
            dv_ref[...] = jnp.zeros_like(dv_ref)

        lane = _iota((1, LANES), 1)
        ma = lane < HEAD_DIM
        qh = _pair_masks(q_ref[...])
        sg = sg_ref[...]
        silu, dsilu = _silu_pair(sg)
        dm = dm_ref[...]
        ov = o_ref[...]
        do = dm * silu
        dsg_ref[...] = dm * ov * dsilu
        dob = do.astype(BF16)
        dd = dob.astype(F32) * ov
        dsum = (jnp.sum(jnp.where(ma, dd, 0.0), axis=1, keepdims=True), jnp.sum(jnp.where(ma, 0.0, dd), axis=1, keepdims=True))
        doh = _pair_masks(dob)
        tmat2 = _suffix_matrix(tk, inclusive=False)
        tmat2_inc = _suffix_matrix(tk, inclusive=True)
        nfull = (qi * tq) // tk

        def block(k0, r0, carry, masked):
            nr = tq - r0
            kb = k_ref[pl.ds(k0, tk), :]
            vb = v_ref[pl.ds(k0, tk), :]
            kh = _pair_masks(kb)
            causal = (k0 + _iota((nr, tk), 1)) < (qi * tq + r0 + _iota((nr, tk), 0)) if masked else None
            heads = range(2)
            qs = [q[r0:] for q in qh]
            dos = [d[r0:] for d in doh]
            das = [_dot_nt(dos[h], vb) for h in heads]
            zs, nsps, lbs, a_s = _sb_scores(qs, kb, causal, tmat2, [carry[h][0][r0:] for h in heads])
            abs_ = [a.astype(BF16) for a in a_s]
            us = [abs_[h].astype(F32) * das[h] for h in heads]
            uins = [_suffix_sums(u, tmat2_inc) for u in us]
            dzs = []
            for h in heads:
                cum_u = dsum[h][r0:] - (uins[h] + carry[h][1][r0:])
                dz = us[h] * jnp.exp(nsps[h]) - jnp.exp(zs[h] + nsps[h]) * cum_u
                if masked:
                    dz = jnp.where(causal, dz, 0.0)
                dzs.append(dz.astype(BF16))
            dv_ref[pl.ds(k0, tk), :] += _dot_tn(jnp.concatenate(abs_, axis=0), jnp.concatenate(dos, axis=0))
            dk_ref[pl.ds(k0, tk), :] += _dot_tn(jnp.concatenate(dzs, axis=0), jnp.concatenate(qs, axis=0))
            dq = _put_rows(carry[2], carry[2][r0:] + _dot(jnp.concatenate(dzs, axis=1), jnp.concatenate(kh, axis=0)), r0)
            new = [(_put_rows(carry[h][0], carry[h][0][r0:] + jnp.sum(lbs[h], axis=1, keepdims=True), r0),
                    _put_rows(carry[h][1], carry[h][1][r0:] + jnp.sum(us[h], axis=1, keepdims=True), r0)) for h in heads]
            return (new[0], new[1], dq)

        zcol = jnp.zeros((tq, 1), F32)
        carry = ((zcol, zcol), (zcol, zcol), jnp.zeros((tq, LANES), F32))
        for off, size in reversed(_diag_tiles(tq)):
            assert size == tk
            carry = block(pl.multiple_of(qi * tq + off, tk), off, carry, True)
        dq = _sb_walk_left(nfull, tk, block, carry, lambda c: (c[0][0], c[1][0]))[2]
        dq_ref[...] = dq * QK_SCALE

    qblk = pl.BlockSpec((tq, LANES), lambda p, i: (i, p))
    kvblk = pl.BlockSpec((S, LANES), lambda p, i: (0, p))
    f32out = jax.ShapeDtypeStruct((S, SB_W), F32)
    return pl.pallas_call(
        body, name="sb_bwd", grid=(npair, S // tq),
        in_specs=[qblk, kvblk, kvblk, qblk,
                  pl.BlockSpec((tq, LANES), lambda p, i: (i, mix0 + p)),
                  pl.BlockSpec((tq, LANES), lambda p, i: (i, C_SG // LANES + p))],
        out_specs=[qblk, kvblk, kvblk, qblk],
        out_shape=[f32out, f32out, f32out, f32out],
        compiler_params=_cparams(dimension_semantics=("arbitrary", "arbitrary")),
    )(sq, sk, sv, o, dmix, projm)


def _prep_bwd(projm, ffo, dqn, dkn, dct, dcr, dv, dfg, dsq, dsk, dsv, dsg, dmix, pooled, yp, qg, kg, bfp, wpd, ps, *, ts):
    S = projm.shape[0]
    nb = S // ts
    hb = ts // POOL_HALO
    npair = FOX_HEADS // 2
    last_halo = S // POOL_HALO - 1

    def body(fq_ref, fk_ref, pp_ref, pph_ref, ff_ref,
             dqn_ref, dkn_ref, dct_ref, dcr_ref, dv_ref, dfg_ref, dsq_ref, dsk_ref, dsv_ref, dsg_ref,
             dmp_ref, dmh_ref, pooled_ref, yp_ref, qg_ref, kg_ref, bf_ref, wpd_ref, ps_ref,
             dp_ref, dqg_ref, dkg_ref, dbf_ref, dwp_ref, dps_ref,
             carry_ref, dl_ref, buf_ref, dct_s):
        i = pl.program_id(0)
        blk = nb - 1 - i

        @pl.when(i == 0)
        def _():
            carry_ref[...] = jnp.zeros_like(carry_ref)
            dqg_ref[...] = jnp.zeros_like(dqg_ref)
            dkg_ref[...] = jnp.zeros_like(dkg_ref)
            dbf_ref[...] = jnp.zeros_like(dbf_ref)
            dwp_ref[...] = jnp.zeros_like(dwp_ref)
            dps_ref[...] = jnp.zeros_like(dps_ref)

        bd = _head_blockdiag()
        for raw_ref, g_ref, dn, dg_ref, col in ((fq_ref, qg_ref, dqn_ref[...], dqg_ref, C_FQ), (fk_ref, kg_ref, dkn_ref[...], dkg_ref, C_FK)):
            q = raw_ref[...]
            rstd = lax.rsqrt(_group_sum(q * q, bd) * (1.0 / HEAD_DIM) + EPS)
            xhat = q * rstd
            dg_ref[...] += jnp.sum(dn * xhat, axis=0, keepdims=True)
            dyg = dn * g_ref[...]
            mean = _group_sum(dyg * xhat, bd) * (1.0 / HEAD_DIM)
            dp_ref[:, col:col + FOX_W] = (rstd * (dyg - xhat * mean)).astype(BF16)
        dp_ref[:, C_FV:C_FV + FOX_W] = dv_ref[...].astype(BF16)
        dp_ref[:, C_FG:C_FG + FOX_W] = dfg_ref[...].astype(BF16)
        dp_ref[:, C_SQ:C_SQ + SB_W] = dsq_ref[...].astype(BF16)
        dp_ref[:, C_SK:C_SK + SB_W] = dsk_ref[...].astype(BF16)
        dp_ref[:, C_SV:C_SV + SB_W] = dsv_ref[...].astype(BF16)
        dp_ref[:, C_SG:C_SG + SB_W] = dsg_ref[...].astype(BF16)

        dct_s[...] = jnp.zeros_like(dct_s)
        for p in range(npair):
            dct_s[FF_STRIDE * p:FF_STRIDE * (p + 1), :] = dct_ref[p]
        dc = dct_s[...].T
        lane = _iota((1, LANES), 1)
        for p in range(npair):
            dcr = dcr_ref[p]
            dc = dc + jnp.where(lane == FF_STRIDE * p, _lane_pick(dcr, lane, 0), 0.0)
            dc = dc + jnp.where(lane == FF_STRIDE * p + 1, _lane_pick(dcr, lane, HEAD_DIM), 0.0)
        triu = _ones_where(_iota((ts, ts), 1) >= _iota((ts, ts), 0))
        dlf = _dot_exact_lhs(triu, dc) + carry_ref[...]
        dl_ref[...] = dlf
        carry_ref[...] = dl_ref[0:1, :]
        z = ff_ref[...] + bf_ref[...]
        dff = dlf * (1.0 / (1.0 + jnp.exp(z)))
        dbf_ref[...] += jnp.sum(dff, axis=0, keepdims=True)
        dp_ref[:, PM:PW] = dff.astype(BF16)

        psv = ps_ref[...]
        wpdv = wpd_ref[...]
        lane_group = _iota((1, POOL_W), 1) >> HEAD_SHIFT
        wlen = _pool_group_select(lane_group, [float(w) for w in POOL_WINDOWS])
        pg = pp_ref[:, POOL_W:2 * POOL_W]
        silu, dsilu = _silu_pair(pg)
        dmp = dmp_ref[...]
        ypv = yp_ref[...]
        dp_ref[:, C_PG:C_PG + POOL_W] = (dmp * (ypv * psv) * dsilu).astype(BF16)
        dps_ref[...] += jnp.sum(dmp * silu * ypv, axis=0, keepdims=True)
        dyp = (dmp * psv * silu).astype(BF16)
        dwp_ref[...] += _dot_tn(pooled_ref[...], dyp)
        dpooled = _dot_nt(dyp, wpdv)
        pgh = pph_ref[:, POOL_W:2 * POOL_W]
        dyph = (dmh_ref[...] * psv * (pgh * _sigmoid(pgh))).astype(BF16)
        dpooled_h = jnp.where(blk < nb - 1, _dot_nt(dyph, wpdv), 0.0)
        tpos = (blk * ts + _iota((ts, 1), 0) + 1).astype(F32)
        ev = dpooled / jnp.minimum(tpos, wlen)
        buf_ref[0:ts, :] = ev
        buf_ref[ts:ts + POOL_HALO, :] = dpooled_h / wlen
        acc = ev
        snaps = []
        for d in range(1, POOL_HALO):
            acc = acc + buf_ref[pl.ds(d, ts), :]
            if d + 1 in POOL_WINDOWS:
                snaps.append(acc)
        dp_ref[:, C_PX:C_PX + POOL_W] = (_pool_group_select(lane_group, snaps) - dpooled).astype(BF16)

    rblk = lambda w, c: pl.BlockSpec((ts, w), lambda i: (nb - 1 - i, c))
    full = lambda a: pl.BlockSpec(a.shape, lambda i: (0,) * a.ndim)
    halo = lambda w, c: pl.BlockSpec((POOL_HALO, w), lambda i: (jnp.minimum((nb - i) * hb, last_halo), c))
    acc_spec = lambda r, w: pl.BlockSpec((r, w), lambda i: (0, 0))
    return pl.pallas_call(
        body, name="prep_bwd", grid=(nb,),
        in_specs=[rblk(FOX_W, C_FQ // FOX_W), rblk(FOX_W, C_FK // FOX_W), rblk(2 * POOL_W, C_PX // (2 * POOL_W)),
                  halo(2 * POOL_W, C_PX // (2 * POOL_W)), rblk(LANES, 0),
                  rblk(FOX_W, 0), rblk(FOX_W, 0), pl.BlockSpec((npair, FF_STRIDE, ts), lambda i: (0, 0, nb - 1 - i)),
                  pl.BlockSpec((npair, ts, LANES), lambda i: (0, nb - 1 - i, 0)), rblk(FOX_W, 0), rblk(FOX_W, 0),
                  rblk(SB_W, 0), rblk(SB_W, 0), rblk(SB_W, 0), rblk(SB_W, 0),
                  rblk(POOL_W, FOX_W // POOL_W), halo(POOL_W, FOX_W // POOL_W), rblk(POOL_W, 0), rblk(POOL_W, 0),
                  full(qg), full(kg), full(bfp), full(wpd), full(ps)],
        out_specs=[rblk(PW, 0), acc_spec(1, FOX_W), acc_spec(1, FOX_W), acc_spec(1, LANES), acc_spec(POOL_W, POOL_W), acc_spec(1, POOL_W)],
        out_shape=[jax.ShapeDtypeStruct((S, PW), BF16), jax.ShapeDtypeStruct((1, FOX_W), F32), jax.ShapeDtypeStruct((1, FOX_W), F32),
                   jax.ShapeDtypeStruct((1, LANES), F32), jax.ShapeDtypeStruct((POOL_W, POOL_W), F32), jax.ShapeDtypeStruct((1, POOL_W), F32)],
        scratch_shapes=[pltpu.VMEM((1, LANES), F32), pltpu.VMEM((ts, LANES), F32), pltpu.VMEM((ts + POOL_HALO, POOL_W), F32),
                        pltpu.VMEM((LANES, ts), F32)],
        compiler_params=_cparams(dimension_semantics=("arbitrary",)),
    )(projm, projm, projm, projm, ffo, dqn, dkn, dct, dcr, dv, dfg, dsq, dsk, dsv, dsg, dmix, dmix, pooled, yp, qg, kg, bfp, wpd, ps)


def _stack_call(body, name, grid, in_specs, operands, slot_specs, slot_shapes, stacks, plain_specs=(), plain_shapes=(), **kw):
    out_specs = list(plain_specs) + list(slot_specs)
    out_shape = list(plain_shapes) + [jax.ShapeDtypeStruct((DEPTH,) + s, F32) for s in slot_shapes]
    if stacks is None:
        return pl.pallas_call(body, name=name, grid=grid, in_specs=in_specs, out_specs=out_specs, out_shape=out_shape, **kw)(*operands)
    n = len(operands)

    def aliased_body(*refs):
        body(*refs[:n], *refs[n + len(stacks):])

    return pl.pallas_call(
        aliased_body, name=name, grid=grid, in_specs=list(in_specs) + [pl.BlockSpec(memory_space=pl.ANY)] * len(stacks),
        out_specs=out_specs, out_shape=out_shape,
        input_output_aliases={n + k: len(plain_specs) + k for k in range(len(stacks))}, **kw)(*operands, *stacks)


def _inproj_dw(h, dproj, layer, stacks, *, ts, tn):
    S, D = h.shape
    nj = PM // tn

    def body(h_ref, dp_ref, dpf_ref, dw_ref, dwf_ref):
        s = pl.program_id(1)

        @pl.when(s == 0)
        def _():
            dw_ref[...] = jnp.zeros_like(dw_ref)

        @pl.when((s == 0) & (pl.program_id(0) == 0))
        def _():
            dwf_ref[...] = jnp.zeros_like(dwf_ref)

        hv = h_ref[...]
        dw_ref[...] += _dot_tn(dp_ref[...], hv)

        @pl.when(pl.program_id(0) == 0)
        def _():
            dwf_ref[...] += _dot_tn(dpf_ref[...], hv)

    return _stack_call(
        body, "inproj_dw", (nj, S // ts),
        [pl.BlockSpec((ts, D), lambda j, s: (s, 0)),
         pl.BlockSpec((ts, tn), lambda j, s: (s, j)),
         pl.BlockSpec((ts, LANES), lambda j, s: (s, PM // LANES))],
        (h, dproj, dproj),
        [pl.BlockSpec((None, tn, D), lambda j, s: (layer, j, 0)), pl.BlockSpec((None, LANES, D), lambda j, s: (layer, 0, 0))],
        [(PM, D), (LANES, D)], stacks,
        compiler_params=_cparams(dimension_semantics=("arbitrary", "arbitrary")))


def _inproj_dx(dproj, wt_all, layer, x, g, dy, *, tm):
    S, D = x.shape

    def body(dp_ref, w_ref, x_ref, g_ref, dy_ref, dx_ref, dg_ref):
        @pl.when(pl.program_id(0) == 0)
        def _():
            dg_ref[...] = jnp.zeros_like(dg_ref)

        dh = _dot(dp_ref[...], w_ref[...])
        xf = x_ref[...]
        rstd = lax.rsqrt(jnp.mean(xf * xf, axis=-1, keepdims=True) + EPS)
        xhat = xf * rstd
        dg_ref[...] += jnp.sum(dh * xhat, axis=0, keepdims=True)
        dyg = dh * g_ref[...]
        mean = jnp.mean(dyg * xhat, axis=-1, keepdims=True)
        dx_ref[...] = rstd * (dyg - xhat * mean) + dy_ref[...]

    row = lambda w: pl.BlockSpec((tm, w), lambda i: (i, 0))
    return pl.pallas_call(
        body, name="inproj_dx", grid=(S // tm,),
        in_specs=[row(PW), pl.BlockSpec((None, PW, D), lambda i: (layer, 0, 0)), row(D), pl.BlockSpec((1, D), lambda i: (0, 0)), row(D)],
        out_specs=[row(D), pl.BlockSpec((1, D), lambda i: (0, 0))],
        out_shape=[jax.ShapeDtypeStruct((S, D), F32), jax.ShapeDtypeStruct((1, D), F32)],
        compiler_params=_cparams(dimension_semantics=("arbitrary",)),
    )(dproj, wt_all, x, g, dy)


def _adam_update(w, g, m, v):
    nm = ADAM_B1 * m + (1.0 - ADAM_B1) * g
    nv = ADAM_B2 * v + (1.0 - ADAM_B2) * (g * g)
    m_hat = nm / (1.0 - ADAM_B1 ** ADAM_STEP)
    v_hat = nv / (1.0 - ADAM_B2 ** ADAM_STEP)
    return -ADAM_LR * (m_hat / (jnp.sqrt(v_hat) + ADAM_EPS) + ADAM_WD * w), nm, nv


def _adamw(w, g, m, v):
    L, R, C = w.shape
    tr = R if R <= 512 else 256

    def body(w_ref, g_ref, m_ref, v_ref, d_ref, nm_ref, nv_ref):
        d_ref[...], nm_ref[...], nv_ref[...] = _adam_update(w_ref[...], g_ref[...], m_ref[...], v_ref[...])

    spec = pl.BlockSpec((1, tr, C), lambda l, i: (l, i, 0))
    shp = jax.ShapeDtypeStruct((L, R, C), F32)
    return pl.pallas_call(
        body, name="adamw", grid=(L, R // tr), in_specs=[spec] * 4, out_specs=[spec] * 3, out_shape=[shp] * 3,
        compiler_params=_cparams(dimension_semantics=("arbitrary", "arbitrary")),
    )(w, g, m, v)


def _adamw_nd(w, g, m, v):
    shape = w.shape
    view = (1,) + shape if w.ndim == 2 else (shape[0], -1, shape[-1])
    outs = _adamw(w.reshape(view), g.reshape(view), m.reshape(view), v.reshape(view))
    return tuple(o.reshape(shape) for o in outs)


FLIP_C = (0, 0, 1)
FLIP_X = (1, 0, 0)
FLIP_Y = (0, 1, 0)
FLIP_XY = (1, 1, 0)
MESH = pl.DeviceIdType.MESH


def _peer(flip):
    me = (lax.axis_index("x"), lax.axis_index("y"), lax.axis_index("c"))
    return tuple(1 - a if f else a for a, f in zip(me, flip))


def _exchange(name, arrays, flips):
    n = len(arrays)

    def body(*refs):
        srcs, dsts = refs[:n], refs[n:2 * n]
        send_sems, recv_sems = refs[2 * n:]
        copies = [pltpu.make_async_remote_copy(src_ref=srcs[k], dst_ref=dsts[k], send_sem=send_sems.at[k], recv_sem=recv_sems.at[k],
                                               device_id=_peer(flips[k]), device_id_type=MESH) for k in range(n)]
        for cp in copies:
            cp.start()
        for cp in copies:
            cp.wait()

    anyspec = pl.BlockSpec(memory_space=pl.ANY)
    return pl.pallas_call(
        body, name=name, in_specs=[anyspec] * n, out_specs=[anyspec] * n,
        out_shape=[jax.ShapeDtypeStruct(a.shape, a.dtype) for a in arrays],
        scratch_shapes=[pltpu.SemaphoreType.DMA((n,)), pltpu.SemaphoreType.DMA((n,))],
    )(*arrays)


def _exchange_add(name, x, flip):
    def body(x_ref, o_ref, buf_ref, send_sem, recv_sem):
        cp = pltpu.make_async_remote_copy(src_ref=x_ref, dst_ref=buf_ref, send_sem=send_sem, recv_sem=recv_sem,
                                          device_id=_peer(flip), device_id_type=MESH)
        cp.start()
        cp.wait()
        o_ref[...] = x_ref[...] + buf_ref[...]

    vspec = pl.BlockSpec(memory_space=pltpu.VMEM)
    return pl.pallas_call(
        body, name=name, in_specs=[vspec], out_specs=vspec, out_shape=jax.ShapeDtypeStruct(x.shape, x.dtype),
        scratch_shapes=[pltpu.VMEM(x.shape, x.dtype), pltpu.SemaphoreType.DMA, pltpu.SemaphoreType.DMA],
    )(x)


def _chip_index():
    return 2 * lax.axis_index("x") + lax.axis_index("y")


def _gather_weights(w_in_t, w_out):
    wi = w_in_t.astype(BF16)
    wo = w_out.astype(BF16)
    halves = (wi.shape[0] // 2, wo.shape[0] // 2)
    ARR = 2
    TO_X, TO_Y, ON_Y, ON_X, SIB_X, SIB_Y, SIB_D0, SIB_D1, OWN = [ARR * k for k in range(9)]
    n_sems = ARR * 9

    def body(wi_ref, wo_ref, gi_ref, go_ref, send_sems, recv_sems):
        c = lax.axis_index("c")
        j = _chip_index()
        srcs = (wi_ref, wo_ref)
        dsts = (gi_ref, go_ref)
        def cuts(core):
            return [(pl.ds(h * core, h), pl.ds(h * core, h // 2), pl.ds(h * core + h // 2, h - h // 2)) for h in halves]
        mine, theirs = cuts(c), cuts(1 - c)
        HALF, Q0, Q1 = 0, 1, 2

        def copy(idx, src, dst, flip):
            return pltpu.make_async_remote_copy(src_ref=src, dst_ref=dst, send_sem=send_sems.at[idx], recv_sem=recv_sems.at[idx],
                                                device_id=_peer(flip), device_id_type=MESH)

        def slot(a, shard, cut):
            return dsts[a].at[shard, cut]

        jx, jy, jd = j ^ 2, j ^ 1, j ^ 3
        sends = []

        def start(cp):
            cp.start()
            sends.append(cp)

        for a in range(ARR):
            start(copy(TO_X + a, srcs[a].at[mine[a][HALF]], slot(a, j, mine[a][HALF]), FLIP_X))
            start(copy(TO_Y + a, srcs[a].at[mine[a][HALF]], slot(a, j, mine[a][HALF]), FLIP_Y))
        own = [copy(OWN + a, srcs[a], dsts[a].at[j], FLIP_C) for a in range(ARR)]
        for cp in own:
            cp.start()
        for a in range(ARR):
            copy(TO_X + a, slot(a, jx, mine[a][HALF]), slot(a, jx, mine[a][HALF]), FLIP_X).wait_recv()
            start(copy(ON_Y + a, slot(a, jx, mine[a][Q0]), slot(a, jx, mine[a][Q0]), FLIP_Y))
            start(copy(SIB_X + a, slot(a, jx, mine[a][HALF]), slot(a, jx, mine[a][HALF]), FLIP_C))
        for a in range(ARR):
            copy(TO_Y + a, slot(a, jy, mine[a][HALF]), slot(a, jy, mine[a][HALF]), FLIP_Y).wait_recv()
            start(copy(ON_X + a, slot(a, jy, mine[a][Q1]), slot(a, jy, mine[a][Q1]), FLIP_X))
            start(copy(SIB_Y + a, slot(a, jy, mine[a][HALF]), slot(a, jy, mine[a][HALF]), FLIP_C))
        for a in range(ARR):
            copy(ON_Y + a, slot(a, jd, mine[a][Q0]), slot(a, jd, mine[a][Q0]), FLIP_Y).wait_recv()
            start(copy(SIB_D0 + a, slot(a, jd, mine[a][Q0]), slot(a, jd, mine[a][Q0]), FLIP_C))
        for a in range(ARR):
            copy(ON_X + a, slot(a, jd, mine[a][Q1]), slot(a, jd, mine[a][Q1]), FLIP_X).wait_recv()
            start(copy(SIB_D1 + a, slot(a, jd, mine[a][Q1]), slot(a, jd, mine[a][Q1]), FLIP_C))
        for a in range(ARR):
            for idx, shard, cut in ((SIB_X, jx, HALF), (SIB_Y, jy, HALF), (SIB_D0, jd, Q0), (SIB_D1, jd, Q1)):
                copy(idx + a, slot(a, shard, theirs[a][cut]), slot(a, shard, theirs[a][cut]), FLIP_C).wait_recv()
        for cp in own:
            cp.wait()
        for cp in sends:
            cp.wait_send()

    anyspec = pl.BlockSpec(memory_space=pl.ANY)
    gi, go = pl.pallas_call(
        body, name="gather_weights", in_specs=[anyspec] * 2, out_specs=[anyspec] * 2,
        out_shape=[jax.ShapeDtypeStruct((4,) + wi.shape, BF16), jax.ShapeDtypeStruct((4,) + wo.shape, BF16)],
        scratch_shapes=[pltpu.SemaphoreType.DMA((n_sems,)), pltpu.SemaphoreType.DMA((n_sems,))],
    )(wi, wo)
    w_in_t_full = gi.reshape((4 * wi.shape[0],) + wi.shape[1:])
    w_out_full = jnp.swapaxes(go, 0, 1).reshape(wo.shape[0], 4 * wo.shape[1], wo.shape[2])
    return w_in_t_full, w_out_full


def _to_aligned(w_t):
    _, L, D = w_t.shape
    npair = FOX_HEADS // 2
    ff = w_t[ORIG_FF:ORIG_REST].reshape(npair, 2, L, D)
    ff = jnp.pad(ff, ((0, 0), (0, FF_STRIDE - 2), (0, 0), (0, 0))).reshape(npair * FF_STRIDE, L, D)
    ff = jnp.pad(ff, ((0, LANES - npair * FF_STRIDE), (0, 0), (0, 0)))
    return jnp.swapaxes(jnp.concatenate([w_t[:ORIG_FOX], w_t[ORIG_REST:], ff], axis=0), 0, 1)


def _from_aligned(dw_t):
    n, _, D = dw_t.shape
    npair = FOX_HEADS // 2
    ff = dw_t[:, PM:PM + npair * FF_STRIDE].reshape(n, npair, FF_STRIDE, D)[:, :, :2].reshape(n, FOX_HEADS, D)
    return jnp.swapaxes(jnp.concatenate([dw_t[:, :ORIG_FOX], ff, dw_t[:, ORIG_FOX:PM]], axis=1), 0, 1)


RELAY_ROWS = 256


def _rows_first(stack_m, stack_f, got_m, got_f):
    n, _, D = got_m.shape
    npair = FOX_HEADS // 2
    first_late = ORIG_FOX // RELAY_ROWS

    def body(c_ref, m_ref, f_ref, gm_ref, gf_ref, out_ref, buf_ref, ff_ref, sem, ff_sem):
        i = pl.program_id(0)
        for l in range(n):
            buf_ref[:, l, :] = m_ref[l] + gm_ref[l].astype(F32)
        start = pl.multiple_of(i * RELAY_ROWS, FOX_HEADS) + jnp.where(i >= first_late, FOX_HEADS, 0)
        main = pltpu.make_async_copy(buf_ref, out_ref.at[pl.ds(start, RELAY_ROWS)], sem)
        main.start()

        @pl.when(i == 0)
        def _():
            for l in range(n):
                for p in range(npair):
                    rows = slice(FF_STRIDE * p, FF_STRIDE * p + 2)
                    ff_ref[2 * p:2 * p + 2, l, :] = f_ref[l, rows, :] + gf_ref[l, rows, :].astype(F32)
            ff = pltpu.make_async_copy(ff_ref, out_ref.at[pl.ds(ORIG_FF, FOX_HEADS)], ff_sem)
            ff.start()
            ff.wait()

        main.wait()

    grid_spec = pltpu.PrefetchScalarGridSpec(
        num_scalar_prefetch=1, grid=(PM // RELAY_ROWS,),
        in_specs=[pl.BlockSpec((n, RELAY_ROWS, D), lambda i, c: (c[0], i, 0)), pl.BlockSpec((n, LANES, D), lambda i, c: (c[0], 0, 0)),
                  pl.BlockSpec((n, RELAY_ROWS, D), lambda i, c: (0, i, 0)), pl.BlockSpec((n, LANES, D), lambda i, c: (0, 0, 0))],
        out_specs=pl.BlockSpec(memory_space=pl.ANY),
        scratch_shapes=[pltpu.VMEM((RELAY_ROWS, n, D), F32), pltpu.VMEM((FOX_HEADS, n, D), F32),
                        pltpu.SemaphoreType.DMA, pltpu.SemaphoreType.DMA])
    return pl.pallas_call(
        body, name="rs_rows_first", grid_spec=grid_spec, out_shape=jax.ShapeDtypeStruct((D_IN, n, D), F32),
        compiler_params=_cparams(dimension_semantics=("arbitrary",)),
    )(lax.axis_index("c").astype(jnp.int32).reshape(1), stack_m, stack_f, got_m, got_f)


def _half_layers(name, stack, got, also_bf16=True):
    L, R, C = stack.shape
    half = L // 2
    tr = min(256, R)
    c = lax.axis_index("c")
    which = ((1 - c) if got is None else c).astype(jnp.int32).reshape(1)

    def body(c_ref, x_ref, *refs):
        if got is None:
            refs[0][...] = x_ref[...].astype(BF16)
        else:
            acc = x_ref[...] + refs[0][...].astype(F32)
            refs[1][...] = acc
            if also_bf16:
                refs[2][...] = acc.astype(BF16)

    plain = pl.BlockSpec((1, tr, C), lambda l, i, c_ref: (l, i, 0))
    picked = pl.BlockSpec((1, tr, C), lambda l, i, c_ref: (c_ref[0] * half + l, i, 0))
    shp = lambda dt: jax.ShapeDtypeStruct((half, R, C), dt)
    out_shape = [shp(BF16)] if got is None else [shp(F32)] + ([shp(BF16)] if also_bf16 else [])
    grid_spec = pltpu.PrefetchScalarGridSpec(
        num_scalar_prefetch=1, grid=(half, R // tr),
        in_specs=[picked] + ([] if got is None else [plain]), out_specs=[plain] * len(out_shape))
    return pl.pallas_call(
        body, name=name, grid_spec=grid_spec, out_shape=out_shape,
        compiler_params=_cparams(dimension_semantics=("arbitrary", "arbitrary")),
    )(which, stack, *([] if got is None else [got]))


def _reduce_scatter(stack_m, stack_f, stack_o, shard_cols, shard_rows):
    j = _chip_index()
    half = DEPTH // 2
    stacks = (stack_m, stack_f, stack_o)
    give = [_half_layers("rs_give", s, None)[0] for s in stacks]
    got = _exchange("rs_d2d", give, (FLIP_C,) * len(stacks))
    o32, obf = _half_layers("rs_add_chip", stack_o, got[2])
    d_model = stack_m.shape[2]
    in32 = _rows_first(stack_m, stack_f, got[0], got[1]).reshape(4, shard_cols, half, d_model)

    def out_shards(o):
        return jnp.moveaxis(o.reshape(half, 4, shard_rows, o.shape[-1]), 1, 0)

    chip = [(in32, in32.astype(BF16), 0), (out_shards(o32), out_shards(obf), 1)]
    shard = lambda a, idx: lax.dynamic_index_in_dim(a, idx, axis=0, keepdims=False)
    via = []
    for _, bf, axis in chip:
        diag = shard(bf, j ^ 3)
        cut = diag.shape[axis] // 2
        via += [lax.slice_in_dim(diag, 0, cut, axis=axis), lax.slice_in_dim(diag, cut, 2 * cut, axis=axis)]
    handed = _exchange("rs_via", via, (FLIP_X, FLIP_Y) * len(chip))
    sends = []
    for a, (f32_sum, _, axis) in enumerate(chip):
        sends.append(_add_half_along("rs_add_via", f32_sum, handed[2 * a + 1], axis, 1, pick=j ^ 2))
        sends.append(_add_half_along("rs_add_via", f32_sum, handed[2 * a], axis, 0, pick=j ^ 1))
    got = _exchange("rs_ici", sends, (FLIP_X, FLIP_Y) * len(chip))
    mine_in = _add_rows("rs_add_in", chip[0][0], list(got[0:2]), pick=j)
    mine_out = _add_into_half("rs_add_out", shard(chip[1][0], j), list(got[2:4]))
    sib_in, g_out = _share_halves(mine_in, mine_out)
    return (mine_in, sib_in), g_out


def _picked(spec, pick):
    return pl.BlockSpec((None,) + tuple(spec.block_shape), lambda *a: (a[-1][0],) + tuple(spec.index_map(*a[:-1])))


def _add_half_along(name, base, extra, axis, which, pick=None):
    shape = base.shape if pick is None else base.shape[1:]
    lanes = min(ROW_LANE_CHUNK, shape[2])
    assert shape[axis] == 2 * extra.shape[axis]
    blk = tuple(shape[d] // 2 if d == axis else shape[d] for d in range(2)) + (lanes,)

    def body(*refs):
        b_ref, e_ref, o_ref = refs[-3:]
        x = b_ref[...]
        o_ref[...] = jnp.where(pl.program_id(0) == which, x + e_ref[...].astype(F32), x).astype(BF16)

    at = lambda i, k, *_: (i, 0, k) if axis == 0 else (0, i, k)
    bspec, espec = pl.BlockSpec(blk, at), pl.BlockSpec(blk, lambda i, k, *_: (0, 0, k))
    kw = dict(out_shape=jax.ShapeDtypeStruct(shape, BF16), name=name, compiler_params=_cparams(dimension_semantics=("arbitrary", "arbitrary")))
    grid = (2, shape[2] // lanes)
    if pick is None:
        return pl.pallas_call(body, grid=grid, in_specs=[bspec, espec], out_specs=bspec, **kw)(base, extra)
    grid_spec = pltpu.PrefetchScalarGridSpec(num_scalar_prefetch=1, grid=grid, in_specs=[_picked(bspec, pick), espec], out_specs=bspec)
    return pl.pallas_call(body, grid_spec=grid_spec, **kw)(pick.astype(jnp.int32).reshape(1), base, extra)


def _add_rows(name, first, others, pick=None):
    n = len(others)
    shape = first.shape if pick is None else first.shape[1:]

    def body(*refs):
        refs = refs[-(n + 2):]
        acc = refs[0][...]
        for r in refs[1:1 + n]:
            acc = acc + r[...].astype(F32)
        refs[1 + n][...] = acc

    grid, spec = _row_lane_blocks(shape)
    sp = spec(shape[1])
    kw = dict(out_shape=jax.ShapeDtypeStruct(shape, F32), name=name, compiler_params=_cparams(dimension_semantics=("arbitrary", "arbitrary")))
    if pick is None:
        return pl.pallas_call(body, grid=grid, in_specs=[sp] * (1 + n), out_specs=sp, **kw)(first, *others)
    grid_spec = pltpu.PrefetchScalarGridSpec(num_scalar_prefetch=1, grid=grid, in_specs=[_picked(sp, pick)] + [sp] * n, out_specs=sp)
    return pl.pallas_call(body, grid_spec=grid_spec, **kw)(pick.astype(jnp.int32).reshape(1), first, *others)


ROW_LANE_CHUNK = 256


def _row_lane_blocks(shape):
    rows, _, C = shape
    tr = rows // 2 if rows % 2 == 0 and rows > 64 else rows
    lanes = min(ROW_LANE_CHUNK, C)
    return (rows // tr, C // lanes), lambda n_mid: pl.BlockSpec((tr, n_mid, lanes), lambda i, k, *_: (i, 0, k))


def _add_into_half(name, first, others):
    half, rows, C = first.shape
    tr = min(256, rows)
    n = len(others)

    def body(c_ref, *refs):
        acc = refs[0][...]
        for r in refs[1:1 + n]:
            acc = acc + r[...].astype(F32)
        refs[1 + n][...] = acc

    grid_spec = pltpu.PrefetchScalarGridSpec(
        num_scalar_prefetch=1, grid=(half, rows // tr),
        in_specs=[pl.BlockSpec((1, tr, C), lambda l, i, c_ref: (l, i, 0))] * (1 + n),
        out_specs=pl.BlockSpec((1, tr, C), lambda l, i, c_ref: (c_ref[0] * half + l, i, 0)))
    return pl.pallas_call(
        body, name=name, grid_spec=grid_spec, out_shape=jax.ShapeDtypeStruct((2 * half, rows, C), F32),
        compiler_params=_cparams(dimension_semantics=("arbitrary", "arbitrary")),
    )(lax.axis_index("c").astype(jnp.int32).reshape(1), first, *others)


def _share_halves(mine, buf):
    half = DEPTH // 2

    def body(mine_ref, buf_in, sib_ref, buf_ref, send_sems, recv_sems):
        lay = pl.ds(half * lax.axis_index("c"), half)
        copies = [pltpu.make_async_remote_copy(src_ref=src, dst_ref=dst, send_sem=send_sems.at[k], recv_sem=recv_sems.at[k],
                                               device_id=_peer(FLIP_C), device_id_type=MESH)
                  for k, (src, dst) in enumerate(((mine_ref, sib_ref), (buf_ref.at[lay], buf_ref.at[lay])))]
        for cp in copies:
            cp.start()
        for cp in copies:
            cp.wait()

    anyspec = pl.BlockSpec(memory_space=pl.ANY)
    return pl.pallas_call(
        body, name="rs_share", in_specs=[anyspec] * 2, out_specs=[anyspec] * 2,
        out_shape=[jax.ShapeDtypeStruct(mine.shape, mine.dtype), jax.ShapeDtypeStruct(buf.shape, buf.dtype)],
        input_output_aliases={1: 1},
        scratch_shapes=[pltpu.SemaphoreType.DMA((2,)), pltpu.SemaphoreType.DMA((2,))],
    )(mine, buf)


def _adamw_halves(w, g_mine, g_sib, m, v):
    half = g_mine.shape[1]

    def body(c_ref, w_ref, gm_ref, gs_ref, m_ref, v_ref, g_ref, d_ref, nm_ref, nv_ref):
        first = c_ref[0] == 0
        gm, gs = gm_ref[...], gs_ref[...]
        for h, gv in enumerate((jnp.where(first, gm, gs), jnp.where(first, gs, gm))):
            lay = slice(half * h, half * (h + 1))
            g_ref[:, lay, :] = gv
            d_ref[:, lay, :], nm_ref[:, lay, :], nv_ref[:, lay, :] = _adam_update(w_ref[:, lay, :], gv, m_ref[:, lay, :], v_ref[:, lay, :])

    grid, spec = _row_lane_blocks(w.shape)
    full, part = spec(w.shape[1]), spec(half)
    grid_spec = pltpu.PrefetchScalarGridSpec(num_scalar_prefetch=1, grid=grid, in_specs=[full, part, part, full, full], out_specs=[full] * 4)
    return pl.pallas_call(
        body, name="adamw_halves", grid_spec=grid_spec, out_shape=[jax.ShapeDtypeStruct(w.shape, F32)] * 4,
        compiler_params=_cparams(dimension_semantics=("arbitrary", "arbitrary")),
    )(lax.axis_index("c").astype(jnp.int32).reshape(1), w, g_mine, g_sib, m, v)


def _all_reduce_small(x):
    x = _exchange_add("ar_c", x, FLIP_C)
    x = _exchange_add("ar_y", x, FLIP_Y)
    return _exchange_add("ar_x", x, FLIP_X)


def _blocks(S):
    return dict(tm=min(512, S), tm_proj=min(1024, S), ts=min(512, S), tq=min(512, S), tq_big=min(1024, S), tk=min(512, S), tks=min(256, S))


def _pair_pad(vec):
    npair = FOX_HEADS // 2
    v = jnp.pad(vec.reshape(npair, 2), ((0, 0), (0, FF_STRIDE - 2))).reshape(1, npair * FF_STRIDE)
    return jnp.pad(v, ((0, 0), (0, LANES - npair * FF_STRIDE)))


def _pair_unpad(row):
    npair = FOX_HEADS // 2
    return row[0, :npair * FF_STRIDE].reshape(npair, FF_STRIDE)[:, :2].reshape(FOX_HEADS)


def _pool_blockdiag(w_pool):
    g, cg, _ = w_pool.shape
    eye = jnp.eye(g, dtype=w_pool.dtype)
    return jnp.einsum("gh,gcd->gchd", eye, w_pool).reshape(g * cg, g * cg)


QK_BOUND_SLACK = 1.05


def _layer_params(norm_g, b_f, q_norm_g, k_norm_g, w_pool, pool_scale):
    qk_bound = QK_BOUND_SLACK * HEAD_DIM * QK_SCALE * jnp.max(jnp.abs(q_norm_g)) * jnp.max(jnp.abs(k_norm_g))
    return dict(g=norm_g.reshape(1, -1), qg=jnp.tile(q_norm_g, FOX_HEADS).reshape(1, FOX_W), kg=jnp.tile(k_norm_g, FOX_HEADS).reshape(1, FOX_W),
                bfp=_pair_pad(b_f), wpd=_pool_blockdiag(w_pool).astype(BF16), ps=pool_scale.reshape(1, POOL_W),
                qkb=jnp.full((1, LANES), qk_bound, F32))


def _layer_fwd(x, wt_all, w_out, layer, prm, bs):
    projm, ffo, h = _inproj(x, prm["g"], wt_all, layer, tm=bs["tm_proj"], tn=PROJ_TN)
    qn, ka, kb, v, sq, sk, sv, pooled, yp, pm = _prep(projm, ffo, prm["qg"], prm["kg"], prm["bfp"], prm["wpd"], prm["ps"], ts=bs["ts"])
    o, lse, fm = _fox_fwd(qn, ka, kb, v, projm, prm["qkb"], tq=bs["tq"], tk=bs["tk"])
    so, sm = _sb_fwd(sq, sk, sv, projm, tq=bs["tq"], tk=bs["tks"])
    y = _outproj(x, fm, pm, sm, w_out, layer, tm=bs["tm_proj"])
    saved = dict(x=x, projm=projm, ffo=ffo, h=h, qn=qn, ka=ka, kb=kb, v=v, sq=sq, sk=sk, sv=sv, pooled=pooled, yp=yp,
                 o=o, lse=lse, so=so, fm=fm, pm=pm, sm=sm)
    return y, saved


def _layer_bwd(dy, wt_all, w_out, prm, sv_, bs, layer, stacks):
    dmix, stack_o = _outproj_bwd(dy, sv_["fm"], sv_["pm"], sv_["sm"], w_out, layer, None if stacks is None else stacks[2:], tm=bs["tm_proj"])
    dqn, dkn, dv, dfg, dct, dcr = _fox_bwd(sv_["qn"], sv_["ka"], sv_["kb"], sv_["v"], sv_["o"], sv_["lse"], dmix, sv_["projm"],
                                      prm["qkb"], tq=bs["tq_big"], tk=bs["tk"])
    dsq, dsk, dsv, dsg = _sb_bwd(sv_["sq"], sv_["sk"], sv_["sv"], sv_["so"], dmix, sv_["projm"], tq=bs["tks"], tk=bs["tks"])
    dproj, dqg, dkg, dbf, dwp, dps = _prep_bwd(sv_["projm"], sv_["ffo"], dqn, dkn, dct, dcr, dv, dfg, dsq, dsk, dsv, dsg, dmix,
                                               sv_["pooled"], sv_["yp"], prm["qg"], prm["kg"], prm["bfp"], prm["wpd"], prm["ps"], ts=bs["ts"])
    stack_m, stack_f = _inproj_dw(sv_["h"], dproj, layer, None if stacks is None else stacks[:2], ts=bs["tm_proj"], tn=PROJ_TN)
    dx, dg = _inproj_dx(dproj, wt_all, layer, sv_["x"], prm["g"], dy, tm=bs["tm"])
    grads = dict(
        norm_g=dg[0],
        b_f=_pair_unpad(dbf), q_norm_g=dqg.reshape(FOX_HEADS, HEAD_DIM).sum(0), k_norm_g=dkg.reshape(FOX_HEADS, HEAD_DIM).sum(0),
        w_pool=jnp.stack([dwp[HEAD_DIM * g:HEAD_DIM * (g + 1), HEAD_DIM * g:HEAD_DIM * (g + 1)] for g in range(4)]),
        pool_scale=dps[0])
    return dx, grads, (stack_m, stack_f, stack_o)


def _local_step(x, target, wt_all, w_out, norm_g, b_f, q_norm_g, k_norm_g, w_pool, pool_scale):
    S, D = x.shape
    bs = _blocks(S)
    prms = [_layer_params(norm_g[l], b_f[l], q_norm_g[l], k_norm_g[l], w_pool[l], pool_scale[l]) for l in range(DEPTH)]
    saved = []
    y = x
    for l in range(DEPTH):
        y, s_ = _layer_fwd(y, wt_all, w_out, l, prms[l], bs)
        saved.append(s_)
    dy, sq = _loss_head(y, target, tm=bs["tm"])
    loss = 0.5 * jnp.sum(sq) / D
    grads = [None] * DEPTH
    stacks = None
    for l in reversed(range(DEPTH)):
        dy, grads[l], stacks = _layer_bwd(dy, wt_all, w_out, prms[l], saved[l], bs, l, stacks)
    stacked = {k: jnp.stack([g[k] for g in grads]) for k in grads[0]}
    return loss, dy, stacked, stacks


SMALL = ("norm_g", "b_f", "q_norm_g", "k_norm_g", "w_pool", "pool_scale")


def _pack_small(gr):
    flat = jnp.concatenate([gr[k].reshape(-1) for k in SMALL])
    pad = (-flat.shape[0]) % (8 * LANES)
    return jnp.pad(flat, (0, pad)).reshape(-1, LANES)


def _unpack_small(packed, like):
    flat = packed.reshape(-1)
    out, off = {}, 0
    for k in SMALL:
        n = like[k].size
        out[k] = flat[off:off + n].reshape(like[k].shape)
        off += n
    return out


def kernel(x, norm_g, w_in, b_f, q_norm_g, k_norm_g, w_pool, pool_scale, w_out, loss_target, m_norm_g, m_w_in, m_b_f, m_q_norm_g, m_k_norm_g, m_w_pool, m_pool_scale, m_w_out, v_norm_g, v_w_in, v_b_f, v_q_norm_g, v_k_norm_g, v_w_pool, v_pool_scale, v_w_out):
    weights = dict(norm_g=norm_g, w_in=w_in, b_f=b_f, q_norm_g=q_norm_g, k_norm_g=k_norm_g, w_pool=w_pool, pool_scale=pool_scale, w_out=w_out)
    mom_m = dict(norm_g=m_norm_g, w_in=m_w_in, b_f=m_b_f, q_norm_g=m_q_norm_g, k_norm_g=m_k_norm_g, w_pool=m_w_pool, pool_scale=m_pool_scale, w_out=m_w_out)
    mom_v = dict(norm_g=v_norm_g, w_in=v_w_in, b_f=v_b_f, q_norm_g=v_q_norm_g, k_norm_g=v_k_norm_g, w_pool=v_w_pool, pool_scale=v_pool_scale, w_out=v_w_out)
    shard_cols = w_in.shape[2]
    shard_rows = w_out.shape[1]

    cols_first = lambda a: jnp.transpose(a, (2, 0, 1))
    w_in_t = cols_first(w_in)
    w_in_t_full, w_out_full = _gather_weights(w_in_t, w_out)
    wt_all = _to_aligned(w_in_t_full)
    loss, dx, gr, stacks = _local_step(x[0], loss_target[0], wt_all, w_out_full, norm_g, b_f, q_norm_g, k_norm_g, w_pool, pool_scale)
    loss = lax.psum(loss, ("x", "y", "c"))

    (g_in_mine, g_in_sib), g_w_out = _reduce_scatter(*stacks, shard_cols, shard_rows)
    small = _unpack_small(_all_reduce_small(_pack_small(gr)), {k: weights[k] for k in SMALL})
    grad_w = dict(small, w_out=g_w_out)

    names = ("norm_g", "w_in", "b_f", "q_norm_g", "k_norm_g", "w_pool", "pool_scale", "w_out")
    upd = {k: _adamw_nd(weights[k], grad_w[k], mom_m[k], mom_v[k]) for k in names if k != "w_in"}
    in_t = _adamw_halves(w_in_t, g_in_mine, g_in_sib, cols_first(mom_m["w_in"]), cols_first(mom_v["w_in"]))
    grad_w["w_in"], *upd["w_in"] = [jnp.transpose(a, (1, 2, 0)) for a in in_t]
    return (loss, dx[None], *[grad_w[k] for k in names], *[upd[k][0] for k in names], *[upd[k][1] for k in names], *[upd[k][2] for k in names])
```

```python
import functools

import jax
import jax.numpy as jnp
from jax import lax
from jax.experimental import pallas as pl
from jax.experimental.pallas import tpu as pltpu

F32 = jnp.float32
BF16 = jnp.bfloat16

DEPTH = 4
HEAD_DIM = 64
FOX_HEADS = 8
SB_HEADS = 4
FOX_W = FOX_HEADS * HEAD_DIM
SB_W = SB_HEADS * HEAD_DIM
POOL_W = 256
POOL_WINDOWS = (2, 4, 8, 16)
POOL_HALO = 16
D_MIX = FOX_W + POOL_W + SB_W
EPS = 1e-6
NEG = -1e30
QK_SCALE = HEAD_DIM ** -0.5

ORIG_FOX = 4 * FOX_W
ORIG_FF = ORIG_FOX
ORIG_REST = ORIG_FF + FOX_HEADS
D_IN = ORIG_REST + 2 * POOL_W + 4 * SB_W

C_FQ, C_FK, C_FV, C_FG = 0, FOX_W, 2 * FOX_W, 3 * FOX_W
C_PX = 4 * FOX_W
C_PG = C_PX + POOL_W
C_SQ = C_PG + POOL_W
C_SK, C_SV, C_SG = C_SQ + SB_W, C_SQ + 2 * SB_W, C_SQ + 3 * SB_W
PM = C_SG + SB_W
LANES = 128
LANE_SHIFT = 7
HEAD_SHIFT = 6
PW = PM + LANES
FF_STRIDE = 8
AUG = 3

ADAM_LR = 0.001
ADAM_B1 = 0.9
ADAM_B2 = 0.999
ADAM_EPS = 1e-08
ADAM_WD = 0.01
ADAM_STEP = 10

VMEM_LIMIT = 48 * 1024 * 1024
PROJ_TN = PM // 2


def _cparams(**kw):
    return pltpu.CompilerParams(vmem_limit_bytes=VMEM_LIMIT, **kw)


def _dot(a, b):
    return jnp.dot(a, b, preferred_element_type=F32)


def _dot_nt(a, b):
    return lax.dot_general(a, b, (((1,), (1,)), ((), ())), preferred_element_type=F32)


def _dot_tn(a, b):
    return lax.dot_general(a, b, (((0,), (0,)), ((), ())), preferred_element_type=F32)


def _split2(x):
    hi = x.astype(BF16)
    lo = (x - hi.astype(F32)).astype(BF16)
    return hi, lo


def _split3(x):
    hi = x.astype(BF16)
    r = x - hi.astype(F32)
    mid = r.astype(BF16)
    lo = (r - mid.astype(F32)).astype(BF16)
    return hi, mid, lo


def _dot_exact_rhs(x, m):
    hi, mid, lo = _split3(x)
    return _dot(hi, m) + _dot(mid, m) + _dot(lo, m)


def _dot_exact_lhs(m, x):
    hi, mid, lo = _split3(x)
    return _dot(m, hi) + _dot(m, mid) + _dot(m, lo)


def _sigmoid(x):
    return 1.0 / (1.0 + jnp.exp(-x))


def _silu_pair(x):
    s = _sigmoid(x)
    return x * s, s * (1.0 + x * (1.0 - s))


def _iota(shape, dim):
    return lax.broadcasted_iota(jnp.int32, shape, dim)


def _ones_where(cond):
    return jnp.where(cond, 1.0, 0.0).astype(BF16)


GROUP_SLAB = 256


def _head_blockdiag():
    rows, cols = _iota((2 * GROUP_SLAB, GROUP_SLAB), 0) & (GROUP_SLAB - 1), _iota((2 * GROUP_SLAB, GROUP_SLAB), 1)
    return _ones_where((rows >> HEAD_SHIFT) == (cols >> HEAD_SHIFT))


def _group_sum(x, bd):
    hi, lo = _split2(x)
    slabs = [_dot(jnp.concatenate([hi[:, s:s + GROUP_SLAB], lo[:, s:s + GROUP_SLAB]], axis=1), bd) for s in range(0, x.shape[1], GROUP_SLAB)]
    return jnp.concatenate(slabs, axis=1)


def _lane_pick(x, lane_idx, lane):
    return jnp.sum(jnp.where(lane_idx == lane, x, 0.0), axis=1, keepdims=True)


def _inproj(x, g, wt_all, layer, *, tm, tn):
    S, D = x.shape
    nj = PM // tn

    def body(x_ref, g_ref, w_ref, wff_ref, proj_ref, ff_ref, h_ref):
        @pl.when(pl.program_id(1) == 0)
        def _():
            xf = x_ref[...]
            ms = jnp.mean(xf * xf, axis=-1, keepdims=True)
            h = (xf * lax.rsqrt(ms + EPS) * g_ref[...]).astype(BF16)
            h_ref[...] = h
            ff_ref[...] = _dot_nt(h, wff_ref[...])

        proj_ref[...] = _dot_nt(h_ref[...], w_ref[...])

    return pl.pallas_call(
        body, name="inproj", grid=(S // tm, nj),
        in_specs=[pl.BlockSpec((tm, D), lambda i, j: (i, 0)),
                  pl.BlockSpec((1, D), lambda i, j: (0, 0)),
                  pl.BlockSpec((None, tn, D), lambda i, j: (layer, j, 0)),
                  pl.BlockSpec((None, LANES, D), lambda i, j: (layer, PM // LANES, 0))],
        out_specs=[pl.BlockSpec((tm, tn), lambda i, j: (i, j)),
                   pl.BlockSpec((tm, LANES), lambda i, j: (i, 0)),
                   pl.BlockSpec((tm, D), lambda i, j: (i, 0))],
        out_shape=[jax.ShapeDtypeStruct((S, PM), F32), jax.ShapeDtypeStruct((S, LANES), F32),
                   jax.ShapeDtypeStruct((S, D), BF16)],
        compiler_params=_cparams(dimension_semantics=("arbitrary", "arbitrary")),
    )(x, g, wt_all, wt_all)


def _pool_group_select(lane_group, vals):
    return jnp.where(lane_group == 0, vals[0], jnp.where(lane_group == 1, vals[1], jnp.where(lane_group == 2, vals[2], vals[3])))


def _prep(projm, ffo, qg, kg, bfp, wpd, ps, *, ts):
    S = projm.shape[0]
    nb = S // ts
    hb = ts // POOL_HALO

    def body(fq_ref, fk_ref, fv_ref, pp_ref, halo_ref, ff_ref, sq_ref, sk_ref, sv_ref,
             qg_ref, kg_ref, bf_ref, wpd_ref, ps_ref,
             qn_ref, ka_ref, kb_ref, v_ref, sqo_ref, sko_ref, svo_ref, pooled_ref, yp_ref, pm_ref,
             carry_ref, c_ref, buf_ref):
        i = pl.program_id(0)
        bd = _head_blockdiag()
        normed = []
        for src, g_ref in ((fq_ref, qg_ref), (fk_ref, kg_ref)):
            q = src[...]
            ss = _group_sum(q * q, bd)
            normed.append(q * lax.rsqrt(ss * (1.0 / HEAD_DIM) + EPS) * g_ref[...])
        qn_ref[...] = (normed[0] * QK_SCALE).astype(BF16)
        kn = normed[1]
        v_ref[...] = fv_ref[...].astype(BF16)
        sqo_ref[...] = (sq_ref[...] * QK_SCALE).astype(BF16)
        sko_ref[...] = sk_ref[...].astype(BF16)
        svo_ref[...] = sv_ref[...].astype(BF16)

        @pl.when(i == 0)
        def _():
            carry_ref[...] = jnp.zeros_like(carry_ref)

        z = ff_ref[...] + bf_ref[...]
        lf = jnp.minimum(z, 0.0) - jnp.log(1.0 + jnp.exp(-jnp.abs(z)))
        tri = _ones_where(_iota((ts, ts), 1) <= _iota((ts, ts), 0))
        c = _dot_exact_lhs(tri, lf) + carry_ref[...]
        c_ref[...] = c
        carry_ref[...] = c_ref[ts - 1:ts, :]
        parts = jnp.concatenate(_split3(-c), axis=1)
        row = _iota((AUG * LANES, FOX_W), 0)
        col = _iota((AUG * LANES, FOX_W), 1)
        part, src = row >> LANE_SHIFT, row & (LANES - 1)
        pair, off = col >> LANE_SHIFT, col & (LANES - 1)
        sel_a = _ones_where((src == FF_STRIDE * pair) & (off == HEAD_DIM + part))
        sel_b = _ones_where((src == FF_STRIDE * pair + 1) & (off == part))
        first_half = (_iota((1, FOX_W), 1) & HEAD_DIM) == 0
        ka_ref[...] = jnp.where(first_half, kn, _dot(parts, sel_a)).astype(BF16)
        kb_ref[...] = jnp.where(first_half, _dot(parts, sel_b), kn).astype(BF16)

        x = pp_ref[:, 0:POOL_W]
        pg = pp_ref[:, POOL_W:2 * POOL_W]
        halo = jnp.where(i > 0, halo_ref[:, 0:POOL_W], 0.0)
        buf_ref[0:POOL_HALO, :] = halo
        buf_ref[POOL_HALO:POOL_HALO + ts, :] = x
        acc = x
        snaps = []
        for d in range(1, POOL_HALO):
            acc = acc + buf_ref[pl.ds(POOL_HALO - d, ts), :]
            if d + 1 in POOL_WINDOWS:
                snaps.append(acc)
        lane_group = _iota((1, POOL_W), 1) >> HEAD_SHIFT
        wsum = _pool_group_select(lane_group, snaps)
        wlen = _pool_group_select(lane_group, [float(w) for w in POOL_WINDOWS])
        tpos = (i * ts + _iota((ts, 1), 0) + 1).astype(F32)
        pooled = wsum / jnp.minimum(tpos, wlen) - x
        pb = pooled.astype(BF16)
        pooled_ref[...] = pb
        yp = _dot(pb, wpd_ref[...])
        yp_ref[...] = yp
        pm_ref[...] = (yp * ps_ref[...] * (pg * _sigmoid(pg))).astype(BF16)

    blk = lambda w, c: pl.BlockSpec((ts, w), lambda i: (i, c))
    full = lambda a: pl.BlockSpec(a.shape, lambda i: (0,) * a.ndim)
    out_shapes = [
        jax.ShapeDtypeStruct((S, FOX_W), BF16), jax.ShapeDtypeStruct((S, FOX_W), BF16), jax.ShapeDtypeStruct((S, FOX_W), BF16),
        jax.ShapeDtypeStruct((S, FOX_W), BF16),
        jax.ShapeDtypeStruct((S, SB_W), BF16), jax.ShapeDtypeStruct((S, SB_W), BF16), jax.ShapeDtypeStruct((S, SB_W), BF16),
        jax.ShapeDtypeStruct((S, POOL_W), BF16), jax.ShapeDtypeStruct((S, POOL_W), F32), jax.ShapeDtypeStruct((S, POOL_W), BF16),
    ]
    out_specs = [
        blk(FOX_W, 0), blk(FOX_W, 0), blk(FOX_W, 0), blk(FOX_W, 0),
        blk(SB_W, 0), blk(SB_W, 0), blk(SB_W, 0),
        blk(POOL_W, 0), blk(POOL_W, 0), blk(POOL_W, 0),
    ]
    return pl.pallas_call(
        body, name="prep", grid=(nb,),
        in_specs=[blk(FOX_W, C_FQ // FOX_W), blk(FOX_W, C_FK // FOX_W), blk(FOX_W, C_FV // FOX_W), blk(2 * POOL_W, C_PX // (2 * POOL_W)),
                  pl.BlockSpec((POOL_HALO, 2 * POOL_W), lambda i: (jnp.maximum(i * hb - 1, 0), C_PX // (2 * POOL_W))),
                  blk(LANES, 0),
                  blk(SB_W, C_SQ // SB_W), blk(SB_W, C_SK // SB_W), blk(SB_W, C_SV // SB_W),
                  full(qg), full(kg), full(bfp), full(wpd), full(ps)],
        out_specs=out_specs, out_shape=out_shapes,
        scratch_shapes=[pltpu.VMEM((1, LANES), F32), pltpu.VMEM((ts, LANES), F32), pltpu.VMEM((ts + POOL_HALO, POOL_W), F32)],
        compiler_params=_cparams(dimension_semantics=("arbitrary",)),
    )(projm, projm, projm, projm, projm, ffo, projm, projm, projm, qg, kg, bfp, wpd, ps)


def _pair_masks(x):
    ma = _iota((1, LANES), 1) < HEAD_DIM
    zero = jnp.zeros_like(x)
    return jnp.where(ma, x, zero), jnp.where(ma, zero, x)


DIAG_TILE = 256


def _diag_tiles(tq, size=DIAG_TILE):
    size = min(tq, size)
    return [(t * size, size) for t in range(tq // size)]


def _put_rows(old, new, r0):
    return new if r0 == 0 else jnp.concatenate([old[:r0], new], axis=0)


def _aug_queries(q):
    lane = _iota((1, LANES), 1)
    one = jnp.ones_like(q)
    zero = jnp.zeros_like(q)
    qa = jnp.where(lane < HEAD_DIM, q, jnp.where(lane < HEAD_DIM + AUG, one, zero))
    qb = jnp.where(lane >= HEAD_DIM, q, jnp.where(lane < AUG, one, zero))
    return qa, qb


EXP_DEAD = -105.0
PACK = 16


def _fox_walk_left(nfull, tk, block, carry, k_refs, qk_bound, row_floor):
    lane = _iota((1, LANES), 1)

    def alive(h, jj, c):
        k0 = pl.multiple_of(jnp.maximum(nfull - 1 - jj, 0) * tk + tk - PACK, PACK)
        last = k_refs[h][pl.ds(k0, PACK), :].astype(F32)
        lo = HEAD_DIM if h == 0 else 0
        negc = jnp.sum(jnp.where((lane >= lo) & (lane < lo + AUG), last, 0.0), axis=1, keepdims=True)
        return qk_bound + jnp.max(negc) - row_floor(c)[h] >= EXP_DEAD

    def walk(heads, jj0, c0):
        def go_on(state):
            jj, c = state
            ok = jj < nfull
            for h in heads:
                ok = ok & alive(h, jj, c)
            return ok

        def step(state):
            jj, c = state
            return jj + 1, block(pl.multiple_of((nfull - 1 - jj) * tk, tk), tk, 0, c, False, heads)

        return lax.while_loop(go_on, step, (jj0, c0))

    jj_pair, carry = walk((0, 1), jnp.int32(0), carry)
    carry = walk((0,), jj_pair, carry)[1]
    return walk((1,), jj_pair, carry)[1]


def _fox_fwd(qn, ka, kb, v, projm, qkb, *, tq, tk):
    S = qn.shape[0]
    npair = FOX_HEADS // 2

    def body(q_ref, ka_ref, kb_ref, v_ref, fg_ref, qkb_ref, o_ref, lse_ref, fm_ref):
        qi = pl.program_id(1)
        lane = _iota((1, LANES), 1)
        ma = lane < HEAD_DIM
        qaug = _aug_queries(q_ref[...])
        k_refs = (ka_ref, kb_ref)

        def block(k0, tkl, r0, carry, masked, heads=(0, 1)):
            vb = v_ref[pl.ds(k0, tkl), :]
            if masked:
                mask = (k0 + _iota((tq - r0, tkl), 1)) <= (qi * tq + r0 + _iota((tq - r0, tkl), 0))
            scores = {h: _dot_nt(qaug[h][r0:], k_refs[h][pl.ds(k0, tkl), :]) for h in heads}
            new = list(carry)
            for h in heads:
                m, l, acc = [x[r0:] for x in carry[h]]
                s = jnp.where(mask, scores[h], NEG) if masked else scores[h]
                m_new = jnp.maximum(m, jnp.max(s, axis=1, keepdims=True))
                alpha = jnp.exp(m - m_new)
                p = jnp.exp(s - m_new)
                sub = (m_new, alpha * l + jnp.sum(p, axis=1, keepdims=True), alpha * acc + _dot(p.astype(BF16), vb))
                new[h] = tuple(_put_rows(old, x, r0) for old, x in zip(carry[h], sub))
            return tuple(new)

        carry = tuple((jnp.full((tq, 1), NEG, F32), jnp.zeros((tq, 1), F32), jnp.zeros((tq, LANES), F32)) for _ in range(2))
        for off, size in _diag_tiles(tq, tq):
            carry = block(pl.multiple_of(qi * tq + off, size), size, off, carry, True)
        carry = _fox_walk_left((qi * tq) // tk, tk, block, carry, k_refs, jnp.max(qkb_ref[...]),
                               lambda c: (jnp.min(c[0][0]), jnp.min(c[1][0])))
        (ma_, la, acca), (mb_, lb, accb) = carry
        o = jnp.where(ma, acca / la, accb / lb)
        o_ref[...] = o
        lse_ref[...] = jnp.where(ma, ma_ + jnp.log(la), mb_ + jnp.log(lb))
        fg = fg_ref[...]
        fm_ref[...] = (o * (fg * _sigmoid(fg))).astype(BF16)

    qblk = pl.BlockSpec((tq, LANES), lambda p, i: (i, p))
    kvblk = pl.BlockSpec((S, LANES), lambda p, i: (0, p))
    return pl.pallas_call(
        body, name="fox_fwd", grid=(npair, S // tq),
        in_specs=[qblk, kvblk, kvblk, kvblk,
                  pl.BlockSpec((tq, LANES), lambda p, i: (i, C_FG // LANES + p)),
                  pl.BlockSpec((1, LANES), lambda p, i: (0, 0))],
        out_specs=[qblk, qblk, qblk],
        out_shape=[jax.ShapeDtypeStruct((S, FOX_W), F32), jax.ShapeDtypeStruct((S, FOX_W), F32), jax.ShapeDtypeStruct((S, FOX_W), BF16)],
        compiler_params=_cparams(dimension_semantics=("arbitrary", "arbitrary")),
    )(qn, ka, kb, v, projm, qkb)


def _suffix_sums(x, tmat2):
    return _dot(jnp.concatenate(_split2(x), axis=1), tmat2)


def _suffix_matrix(tk, inclusive):
    rr, cc = _iota((2 * tk, tk), 0) & (tk - 1), _iota((2 * tk, tk), 1)
    return _ones_where(rr >= cc) if inclusive else _ones_where(rr > cc)


def _sb_scores(qh, kb, causal, tmat2, r_runs):
    heads = range(2)
    zs = [_dot_nt(qh[h], kb) for h in heads]
    nsps = [jnp.minimum(-z, 0.0) - jnp.log(1.0 + jnp.exp(-jnp.abs(z))) for z in zs]
    lbs = nsps if causal is None else [jnp.where(causal, n, 0.0) for n in nsps]
    rins = [_suffix_sums(lb, tmat2) for lb in lbs]
    args = [zs[h] + lbs[h] + (rins[h] + r_runs[h]) for h in heads]
    a_s = [jnp.exp(arg if causal is None else jnp.where(causal, arg, NEG)) for arg in args]
    return zs, nsps, lbs, a_s


def _sb_walk_left(nfull, tk, block, carry, running_sums):
    def alive(state):
        jj, c = state
        ra, rb = running_sums(c)
        return (jj < nfull) & (jnp.max(jnp.maximum(ra, rb)) >= EXP_DEAD)

    def step(state):
        jj, c = state
        return jj + 1, block(pl.multiple_of((nfull - 1 - jj) * tk, tk), 0, c, False)

    return lax.while_loop(alive, step, (jnp.int32(0), carry))[1]


def _sb_fwd(sq, sk, sv, projm, *, tq, tk):
    S = sq.shape[0]
    npair = SB_HEADS // 2

    def body(q_ref, k_ref, v_ref, sg_ref, o_ref, sm_ref):
        qi = pl.program_id(1)
        lane = _iota((1, LANES), 1)
        ma = lane < HEAD_DIM
        qh = _pair_masks(q_ref[...])
        tmat2 = _suffix_matrix(tk, inclusive=False)
        nfull = (qi * tq) // tk

        def block(k0, r0, carry, masked):
            nr = tq - r0
            kb = k_ref[pl.ds(k0, tk), :]
            vb = v_ref[pl.ds(k0, tk), :]
            causal = (k0 + _iota((nr, tk), 1)) < (qi * tq + r0 + _iota((nr, tk), 0)) if masked else None
            _, _, lbs, a_s = _sb_scores([q[r0:] for q in qh], kb, causal, tmat2, [carry[h][0][r0:] for h in range(2)])
            pv = _dot(jnp.concatenate([a.astype(BF16) for a in a_s], axis=0), vb)
            return tuple((_put_rows(carry[h][0], carry[h][0][r0:] + jnp.sum(lbs[h], axis=1, keepdims=True), r0),
                          _put_rows(carry[h][1], carry[h][1][r0:] + pv[h * nr:(h + 1) * nr], r0)) for h in range(2))

        carry = tuple((jnp.zeros((tq, 1), F32), jnp.zeros((tq, LANES), F32)) for _ in range(2))
        for off, size in reversed(_diag_tiles(tq)):
            assert size == tk
            carry = block(pl.multiple_of(qi * tq + off, tk), off, carry, True)
        (_, acca), (_, accb) = _sb_walk_left(nfull, tk, block, carry, lambda c: (c[0][0], c[1][0]))
        o = jnp.where(ma, acca, accb)
        o_ref[...] = o
        sg = sg_ref[...]
        sm_ref[...] = (o * (sg * _sigmoid(sg))).astype(BF16)

    qblk = pl.BlockSpec((tq, LANES), lambda p, i: (i, p))
    kvblk = pl.BlockSpec((S, LANES), lambda p, i: (0, p))
    return pl.pallas_call(
        body, name="sb_fwd", grid=(npair, S // tq),
        in_specs=[qblk, kvblk, kvblk, pl.BlockSpec((tq, LANES), lambda p, i: (i, C_SG // LANES + p))],
        out_specs=[qblk, qblk],
        out_shape=[jax.ShapeDtypeStruct((S, SB_W), F32), jax.ShapeDtypeStruct((S, SB_W), BF16)],
        compiler_params=_cparams(dimension_semantics=("arbitrary", "arbitrary")),
    )(sq, sk, sv, projm)


def _outproj(x, fm, pm, sm, w_out, layer, *, tm, target=None):
    S, D = x.shape

    def body(x_ref, fm_ref, pm_ref, sm_ref, w_ref, *refs):
        y = x_ref[...] + _dot(fm_ref[...], w_ref[0:FOX_W, :])
        y = y + _dot(pm_ref[...], w_ref[FOX_W:FOX_W + POOL_W, :])
        y = y + _dot(sm_ref[...], w_ref[FOX_W + POOL_W:D_MIX, :])
        if target is None:
            refs[0][...] = y
            return
        t_ref, dy_ref, sq_ref = refs

        @pl.when(pl.program_id(0) == 0)
        def _():
            sq_ref[...] = jnp.zeros_like(sq_ref)

        d = y - t_ref[...]
        dy_ref[...] = d * (1.0 / D)
        sq_ref[...] += jnp.sum(d * d, axis=0, keepdims=True)

    row = lambda w: pl.BlockSpec((tm, w), lambda i: (i, 0))
    in_specs = [row(D), row(FOX_W), row(POOL_W), row(SB_W), pl.BlockSpec((None, D_MIX, D), lambda i: (layer, 0, 0))]
    kw = dict(name="outproj", grid=(S // tm,), compiler_params=_cparams(dimension_semantics=("arbitrary",)))
    if target is None:
        return pl.pallas_call(body, in_specs=in_specs, out_specs=row(D), out_shape=jax.ShapeDtypeStruct((S, D), F32), **kw)(x, fm, pm, sm, w_out)
    return pl.pallas_call(
        body, in_specs=in_specs + [row(D)], out_specs=[row(D), pl.BlockSpec((1, D), lambda i: (0, 0))],
        out_shape=[jax.ShapeDtypeStruct((S, D), F32), jax.ShapeDtypeStruct((1, D), F32)], **kw)(x, fm, pm, sm, w_out, target)


def _outproj_bwd(dy, fm, pm, sm, w_out, layer, stacks, *, tm):
    S, D = dy.shape

    def body(dy_ref, fm_ref, pm_ref, sm_ref, w_ref, dm_ref, dw_ref):
        @pl.when(pl.program_id(0) == 0)
        def _():
            dw_ref[...] = jnp.zeros_like(dw_ref)

        dyb = dy_ref[...].astype(BF16)
        dm_ref[...] = _dot_nt(dyb, w_ref[...])
        dw_ref[0:FOX_W, :] += _dot_tn(fm_ref[...], dyb)
        dw_ref[FOX_W:FOX_W + POOL_W, :] += _dot_tn(pm_ref[...], dyb)
        dw_ref[FOX_W + POOL_W:D_MIX, :] += _dot_tn(sm_ref[...], dyb)

    row = lambda w: pl.BlockSpec((tm, w), lambda i: (i, 0))
    wspec = pl.BlockSpec((None, D_MIX, D), lambda i: (layer, 0, 0))
    return _stack_call(
        body, "outproj_bwd", (S // tm,), [row(D), row(FOX_W), row(POOL_W), row(SB_W), wspec], (dy, fm, pm, sm, w_out),
        [pl.BlockSpec((None, D_MIX, D), lambda i: (layer, 0, 0))], [(D_MIX, D)], stacks,
        plain_specs=[row(D_MIX)], plain_shapes=[jax.ShapeDtypeStruct((S, D_MIX), F32)],
        compiler_params=_cparams(dimension_semantics=("arbitrary",)))


def _fox_bwd(qn, ka, kb, v, o, lse, dmix, projm, qkb, *, tq, tk):
    S = qn.shape[0]
    npair = FOX_HEADS // 2

    def body(q_ref, ka_ref, kb_ref, v_ref, o_ref, lse_ref, dm_ref, fg_ref, qkb_ref,
             dq_ref, dk_ref, dv_ref, dfg_ref, dct_ref, dcr_ref):
        qi = pl.program_id(1)

        @pl.when(qi == 0)
        def _():
            dk_ref[...] = jnp.zeros_like(dk_ref)
            dv_ref[...] = jnp.zeros_like(dv_ref)
            dct_ref[...] = jnp.zeros_like(dct_ref)

        lane = _iota((1, LANES), 1)
        ma = lane < HEAD_DIM
        qh = _pair_masks(q_ref[...])
        qaug = _aug_queries(q_ref[...])
        k_refs = (ka_ref, kb_ref)
        lsev = lse_ref[...]
        lse = (_lane_pick(lsev, lane, 0), _lane_pick(lsev, lane, HEAD_DIM))
        fg = fg_ref[...]
        silu, dsilu = _silu_pair(fg)
        dm = dm_ref[...]
        ov = o_ref[...]
        do = dm * silu
        dfg_ref[...] = dm * ov * dsilu
        dd = do * ov
        dsum = (jnp.sum(jnp.where(ma, dd, 0.0), axis=1, keepdims=True), jnp.sum(jnp.where(ma, 0.0, dd), axis=1, keepdims=True))
        doh = _pair_masks(do.astype(BF16))

        def block(k0, tkl, r0, carry, masked, heads=(0, 1)):
            vb = v_ref[pl.ds(k0, tkl), :]
            if masked:
                mask = (k0 + _iota((tq - r0, tkl), 1)) <= (qi * tq + r0 + _iota((tq - r0, tkl), 0))
            kaugs = {h: k_refs[h][pl.ds(k0, tkl), :] for h in heads}
            scores = {h: _dot_nt(qaug[h][r0:], kaugs[h]) for h in heads}
            dps = {h: _dot_nt(doh[h][r0:], vb) for h in heads}
            ps, dss = [], []
            rows = [carry[1], carry[2]]
            for h in heads:
                s = jnp.where(mask, scores[h], NEG) if masked else scores[h]
                p = jnp.exp(s - lse[h][r0:])
                dsf = p * (dps[h] - dsum[h][r0:])
                dct_ref[0, h:h + 1, pl.ds(k0, tkl)] -= jnp.sum(dsf, axis=0, keepdims=True)
                rows[h] = _put_rows(carry[1 + h], carry[1 + h][r0:] + jnp.sum(dsf, axis=1, keepdims=True), r0)
                ps.append(p.astype(BF16))
                dss.append(dsf.astype(BF16))
            dv_ref[pl.ds(k0, tkl), :] += _dot_tn(jnp.concatenate(ps, axis=0), jnp.concatenate([doh[h][r0:] for h in heads], axis=0))
            dk_ref[pl.ds(k0, tkl), :] += _dot_tn(jnp.concatenate(dss, axis=0), jnp.concatenate([qh[h][r0:] for h in heads], axis=0))
            kh = jnp.concatenate([_pair_masks(kaugs[h])[h] for h in heads], axis=0)
            dq = _put_rows(carry[0], carry[0][r0:] + _dot(jnp.concatenate(dss, axis=1), kh), r0)
            return (dq, rows[0], rows[1])

        zcol = jnp.zeros((tq, 1), F32)
        carry = (jnp.zeros((tq, LANES), F32), zcol, zcol)
        for off, size in _diag_tiles(tq):
            carry = block(pl.multiple_of(qi * tq + off, size), size, off, carry, True)
        floors = (jnp.min(lse[0]), jnp.min(lse[1]))
        dq, rowa, rowb = _fox_walk_left((qi * tq) // tk, tk, block, carry, k_refs, jnp.max(qkb_ref[...]), lambda c: floors)
        dq_ref[...] = dq * QK_SCALE
        dcr_ref[0] = jnp.where(ma, rowa, rowb)

    qblk = pl.BlockSpec((tq, LANES), lambda p, i: (i, p))
    kvblk = pl.BlockSpec((S, LANES), lambda p, i: (0, p))
    f32out = jax.ShapeDtypeStruct((S, FOX_W), F32)
    ctblk = pl.BlockSpec((1, FF_STRIDE, S), lambda p, i: (p, 0, 0))
    return pl.pallas_call(
        body, name="fox_bwd", grid=(npair, S // tq),
        in_specs=[qblk, kvblk, kvblk, kvblk, qblk, qblk, qblk,
                  pl.BlockSpec((tq, LANES), lambda p, i: (i, C_FG // LANES + p)),
                  pl.BlockSpec((1, LANES), lambda p, i: (0, 0))],
        out_specs=[qblk, kvblk, kvblk, qblk, ctblk, pl.BlockSpec((1, tq, LANES), lambda p, i: (p, i, 0))],
        out_shape=[f32out, f32out, f32out, f32out, jax.ShapeDtypeStruct((npair, FF_STRIDE, S), F32),
                   jax.ShapeDtypeStruct((npair, S, LANES), F32)],
        compiler_params=_cparams(dimension_semantics=("arbitrary", "arbitrary")),
    )(qn, ka, kb, v, o, lse, dmix, projm, qkb)


def _sb_bwd(sq, sk, sv, o, dmix, projm, *, tq, tk):
    S = sq.shape[0]
    npair = SB_HEADS // 2
    mix0 = (FOX_W + POOL_W) // LANES

    def body(q_ref, k_ref, v_ref, o_ref, dm_ref, sg_ref, dq_ref, dk_ref, dv_ref, dsg_ref):
        qi = pl.program_id(1)

        @pl.when(qi == 0)
        def _():
            dk_ref[...] = jnp.zeros_like(dk_ref)
            dv_ref[...] = jnp.zeros_like(dv_ref)

        lane = _iota((1, LANES), 1)
        ma = lane < HEAD_DIM
        qh = _pair_masks(q_ref[...])
        sg = sg_ref[...]
        silu, dsilu = _silu_pair(sg)
        dm = dm_ref[...]
        ov = o_ref[...]
        do = dm * silu
        dsg_ref[...] = dm * ov * dsilu
        dob = do.astype(BF16)
        dd = dob.astype(F32) * ov
        dsum = (jnp.sum(jnp.where(ma, dd, 0.0), axis=1, keepdims=True), jnp.sum(jnp.where(ma, 0.0, dd), axis=1, keepdims=True))
        doh = _pair_masks(dob)
        tmat2 = _suffix_matrix(tk, inclusive=False)
        tmat2_inc = _suffix_matrix(tk, inclusive=True)
        nfull = (qi * tq) // tk

        def block(k0, r0, carry, masked):
            nr = tq - r0
            kb = k_ref[pl.ds(k0, tk), :]
            vb = v_ref[pl.ds(k0, tk), :]
            kh = _pair_masks(kb)
            causal = (k0 + _iota((nr, tk), 1)) < (qi * tq + r0 + _iota((nr, tk), 0)) if masked else None
            heads = range(2)
            qs = [q[r0:] for q in qh]
            dos = [d[r0:] for d in doh]
            das = [_dot_nt(dos[h], vb) for h in heads]
            zs, nsps, lbs, a_s = _sb_scores(qs, kb, causal, tmat2, [carry[h][0][r0:] for h in heads])
            abs_ = [a.astype(BF16) for a in a_s]
            us = [abs_[h].astype(F32) * das[h] for h in heads]
            uins = [_suffix_sums(u, tmat2_inc) for u in us]
            dzs = []
            for h in heads:
                cum_u = dsum[h][r0:] - (uins[h] + carry[h][1][r0:])
                dz = us[h] * jnp.exp(nsps[h]) - jnp.exp(zs[h] + nsps[h]) * cum_u
                if masked:
                    dz = jnp.where(causal, dz, 0.0)
                dzs.append(dz.astype(BF16))
            dv_ref[pl.ds(k0, tk), :] += _dot_tn(jnp.concatenate(abs_, axis=0), jnp.concatenate(dos, axis=0))
            dk_ref[pl.ds(k0, tk), :] += _dot_tn(jnp.concatenate(dzs, axis=0), jnp.concatenate(qs, axis=0))
            dq = _put_rows(carry[2], carry[2][r0:] + _dot(jnp.concatenate(dzs, axis=1), jnp.concatenate(kh, axis=0)), r0)
            new = [(_put_rows(carry[h][0], carry[h][0][r0:] + jnp.sum(lbs[h], axis=1, keepdims=True), r0),
                    _put_rows(carry[h][1], carry[h][1][r0:] + jnp.sum(us[h], axis=1, keepdims=True), r0)) for h in heads]
            return (new[0], new[1], dq)

        zcol = jnp.zeros((tq, 1), F32)
        carry = ((zcol, zcol), (zcol, zcol), jnp.zeros((tq, LANES), F32))
        for off, size in reversed(_diag_tiles(tq)):
            assert size == tk
            carry = block(pl.multiple_of(qi * tq + off, tk), off, carry, True)
        dq = _sb_walk_left(nfull, tk, block, carry, lambda c: (c[0][0], c[1][0]))[2]
        dq_ref[...] = dq * QK_SCALE

    qblk = pl.BlockSpec((tq, LANES), lambda p, i: (i, p))
    kvblk = pl.BlockSpec((S, LANES), lambda p, i: (0, p))
    f32out = jax.ShapeDtypeStruct((S, SB_W), F32)
    return pl.pallas_call(
        body, name="sb_bwd", grid=(npair, S // tq),
        in_specs=[qblk, kvblk, kvblk, qblk,
                  pl.BlockSpec((tq, LANES), lambda p, i: (i, mix0 + p)),
                  pl.BlockSpec((tq, LANES), lambda p, i: (i, C_SG // LANES + p))],
        out_specs=[qblk, kvblk, kvblk, qblk],
        out_shape=[f32out, f32out, f32out, f32out],
        compiler_params=_cparams(dimension_semantics=("arbitrary", "arbitrary")),
    )(sq, sk, sv, o, dmix, projm)


def _prep_bwd(projm, ffo, dqn, dkn, dct, dcr, dv, dfg, dsq, dsk, dsv, dsg, dmix, pooled, yp, qg, kg, bfp, wpd, ps, *, ts):
    S = projm.shape[0]
    nb = S // ts
    hb = ts // POOL_HALO
    npair = FOX_HEADS // 2
    last_halo = S // POOL_HALO - 1

    def body(fq_ref, fk_ref, pp_ref, pph_ref, ff_ref,
             dqn_ref, dkn_ref, dct_ref, dcr_ref, dv_ref, dfg_ref, dsq_ref, dsk_ref, dsv_ref, dsg_ref,
             dmp_ref, dmh_ref, pooled_ref, yp_ref, qg_ref, kg_ref, bf_ref, wpd_ref, ps_ref,
             dp_ref, dqg_ref, dkg_ref, dbf_ref, dwp_ref, dps_ref,
             carry_ref, dl_ref, buf_ref, dct_s):
        i = pl.program_id(0)
        blk = nb - 1 - i

        @pl.when(i == 0)
        def _():
            carry_ref[...] = jnp.zeros_like(carry_ref)
            dqg_ref[...] = jnp.zeros_like(dqg_ref)
            dkg_ref[...] = jnp.zeros_like(dkg_ref)
            dbf_ref[...] = jnp.zeros_like(dbf_ref)
            dwp_ref[...] = jnp.zeros_like(dwp_ref)
            dps_ref[...] = jnp.zeros_like(dps_ref)

        bd = _head_blockdiag()
        for raw_ref, g_ref, dn, dg_ref, col in ((fq_ref, qg_ref, dqn_ref[...], dqg_ref, C_FQ), (fk_ref, kg_ref, dkn_ref[...], dkg_ref, C_FK)):
            q = raw_ref[...]
            rstd = lax.rsqrt(_group_sum(q * q, bd) * (1.0 / HEAD_DIM) + EPS)
            xhat = q * rstd
            dg_ref[...] += jnp.sum(dn * xhat, axis=0, keepdims=True)
            dyg = dn * g_ref[...]
            mean = _group_sum(dyg * xhat, bd) * (1.0 / HEAD_DIM)
            dp_ref[:, col:col + FOX_W] = (rstd * (dyg - xhat * mean)).astype(BF16)
        dp_ref[:, C_FV:C_FV + FOX_W] = dv_ref[...].astype(BF16)
        dp_ref[:, C_FG:C_FG + FOX_W] = dfg_ref[...].astype(BF16)
        dp_ref[:, C_SQ:C_SQ + SB_W] = dsq_ref[...].astype(BF16)
        dp_ref[:, C_SK:C_SK + SB_W] = dsk_ref[...].astype(BF16)
        dp_ref[:, C_SV:C_SV + SB_W] = dsv_ref[...].astype(BF16)
        dp_ref[:, C_SG:C_SG + SB_W] = dsg_ref[...].astype(BF16)

        dct_s[...] = jnp.zeros_like(dct_s)
        for p in range(npair):
            dct_s[FF_STRIDE * p:FF_STRIDE * (p + 1), :] = dct_ref[p]
        dc = dct_s[...].T
        lane = _iota((1, LANES), 1)
        for p in range(npair):
            dcr = dcr_ref[p]
            dc = dc + jnp.where(lane == FF_STRIDE * p, _lane_pick(dcr, lane, 0), 0.0)
            dc = dc + jnp.where(lane == FF_STRIDE * p + 1, _lane_pick(dcr, lane, HEAD_DIM), 0.0)
        triu = _ones_where(_iota((ts, ts), 1) >= _iota((ts, ts), 0))
        dlf = _dot_exact_lhs(triu, dc) + carry_ref[...]
        dl_ref[...] = dlf
        carry_ref[...] = dl_ref[0:1, :]
        z = ff_ref[...] + bf_ref[...]
        dff = dlf * (1.0 / (1.0 + jnp.exp(z)))
        dbf_ref[...] += jnp.sum(dff, axis=0, keepdims=True)
        dp_ref[:, PM:PW] = dff.astype(BF16)

        psv = ps_ref[...]
        wpdv = wpd_ref[...]
        lane_group = _iota((1, POOL_W), 1) >> HEAD_SHIFT
        wlen = _pool_group_select(lane_group, [float(w) for w in POOL_WINDOWS])
        pg = pp_ref[:, POOL_W:2 * POOL_W]
        silu, dsilu = _silu_pair(pg)
        dmp = dmp_ref[...]
        ypv = yp_ref[...]
        dp_ref[:, C_PG:C_PG + POOL_W] = (dmp * (ypv * psv) * dsilu).astype(BF16)
        dps_ref[...] += jnp.sum(dmp * silu * ypv, axis=0, keepdims=True)
        dyp = (dmp * psv * silu).astype(BF16)
        dwp_ref[...] += _dot_tn(pooled_ref[...], dyp)
        dpooled = _dot_nt(dyp, wpdv)
        pgh = pph_ref[:, POOL_W:2 * POOL_W]
        dyph = (dmh_ref[...] * psv * (pgh * _sigmoid(pgh))).astype(BF16)
        dpooled_h = jnp.where(blk < nb - 1, _dot_nt(dyph, wpdv), 0.0)
        tpos = (blk * ts + _iota((ts, 1), 0) + 1).astype(F32)
        ev = dpooled / jnp.minimum(tpos, wlen)
        buf_ref[0:ts, :] = ev
        buf_ref[ts:ts + POOL_HALO, :] = dpooled_h / wlen
        acc = ev
        snaps = []
        for d in range(1, POOL_HALO):
            acc = acc + buf_ref[pl.ds(d, ts), :]
            if d + 1 in POOL_WINDOWS:
                snaps.append(acc)
        dp_ref[:, C_PX:C_PX + POOL_W] = (_pool_group_select(lane_group, snaps) - dpooled).astype(BF16)

    rblk = lambda w, c: pl.BlockSpec((ts, w), lambda i: (nb - 1 - i, c))
    full = lambda a: pl.BlockSpec(a.shape, lambda i: (0,) * a.ndim)
    halo = lambda w, c: pl.BlockSpec((POOL_HALO, w), lambda i: (jnp.minimum((nb - i) * hb, last_halo), c))
    acc_spec = lambda r, w: pl.BlockSpec((r, w), lambda i: (0, 0))
    return pl.pallas_call(
        body, name="prep_bwd", grid=(nb,),
        in_specs=[rblk(FOX_W, C_FQ // FOX_W), rblk(FOX_W, C_FK // FOX_W), rblk(2 * POOL_W, C_PX // (2 * POOL_W)),
                  halo(2 * POOL_W, C_PX // (2 * POOL_W)), rblk(LANES, 0),
                  rblk(FOX_W, 0), rblk(FOX_W, 0), pl.BlockSpec((npair, FF_STRIDE, ts), lambda i: (0, 0, nb - 1 - i)),
                  pl.BlockSpec((npair, ts, LANES), lambda i: (0, nb - 1 - i, 0)), rblk(FOX_W, 0), rblk(FOX_W, 0),
                  rblk(SB_W, 0), rblk(SB_W, 0), rblk(SB_W, 0), rblk(SB_W, 0),
                  rblk(POOL_W, FOX_W // POOL_W), halo(POOL_W, FOX_W // POOL_W), rblk(POOL_W, 0), rblk(POOL_W, 0),
                  full(qg), full(kg), full(bfp), full(wpd), full(ps)],
        out_specs=[rblk(PW, 0), acc_spec(1, FOX_W), acc_spec(1, FOX_W), acc_spec(1, LANES), acc_spec(POOL_W, POOL_W), acc_spec(1, POOL_W)],
        out_shape=[jax.ShapeDtypeStruct((S, PW), BF16), jax.ShapeDtypeStruct((1, FOX_W), F32), jax.ShapeDtypeStruct((1, FOX_W), F32),
                   jax.ShapeDtypeStruct((1, LANES), F32), jax.ShapeDtypeStruct((POOL_W, POOL_W), F32), jax.ShapeDtypeStruct((1, POOL_W), F32)],
        scratch_shapes=[pltpu.VMEM((1, LANES), F32), pltpu.VMEM((ts, LANES), F32), pltpu.VMEM((ts + POOL_HALO, POOL_W), F32),
                        pltpu.VMEM((LANES, ts), F32)],
        compiler_params=_cparams(dimension_semantics=("arbitrary",)),
    )(projm, projm, projm, projm, ffo, dqn, dkn, dct, dcr, dv, dfg, dsq, dsk, dsv, dsg, dmix, dmix, pooled, yp, qg, kg, bfp, wpd, ps)


def _stack_call(body, name, grid, in_specs, operands, slot_specs, slot_shapes, stacks, plain_specs=(), plain_shapes=(), **kw):
    out_specs = list(plain_specs) + list(slot_specs)
    out_shape = list(plain_shapes) + [jax.ShapeDtypeStruct((DEPTH,) + s, F32) for s in slot_shapes]
    if stacks is None:
        return pl.pallas_call(body, name=name, grid=grid, in_specs=in_specs, out_specs=out_specs, out_shape=out_shape, **kw)(*operands)
    n = len(operands)

    def aliased_body(*refs):
        body(*refs[:n], *refs[n + len(stacks):])

    return pl.pallas_call(
        aliased_body, name=name, grid=grid, in_specs=list(in_specs) + [pl.BlockSpec(memory_space=pl.ANY)] * len(stacks),
        out_specs=out_specs, out_shape=out_shape,
        input_output_aliases={n + k: len(plain_specs) + k for k in range(len(stacks))}, **kw)(*operands, *stacks)


def _inproj_dw(h, dproj, layer, stacks, *, ts, tn):
    S, D = h.shape
    nj = PM // tn

    def body(h_ref, dp_ref, dpf_ref, dw_ref, dwf_ref):
        s = pl.program_id(1)

        @pl.when(s == 0)
        def _():
            dw_ref[...] = jnp.zeros_like(dw_ref)

        @pl.when((s == 0) & (pl.program_id(0) == 0))
        def _():
            dwf_ref[...] = jnp.zeros_like(dwf_ref)

        hv = h_ref[...]
        dw_ref[...] += _dot_tn(dp_ref[...], hv)

        @pl.when(pl.program_id(0) == 0)
        def _():
            dwf_ref[...] += _dot_tn(dpf_ref[...], hv)

    return _stack_call(
        body, "inproj_dw", (nj, S // ts),
        [pl.BlockSpec((ts, D), lambda j, s: (s, 0)),
         pl.BlockSpec((ts, tn), lambda j, s: (s, j)),
         pl.BlockSpec((ts, LANES), lambda j, s: (s, PM // LANES))],
        (h, dproj, dproj),
        [pl.BlockSpec((None, tn, D), lambda j, s: (layer, j, 0)), pl.BlockSpec((None, LANES, D), lambda j, s: (layer, 0, 0))],
        [(PM, D), (LANES, D)], stacks,
        compiler_params=_cparams(dimension_semantics=("arbitrary", "arbitrary")))


def _inproj_dx(dproj, wt_all, layer, x, g, dy, *, tm):
    S, D = x.shape

    def body(dp_ref, w_ref, x_ref, g_ref, dy_ref, dx_ref, dg_ref):
        @pl.when(pl.program_id(0) == 0)
        def _():
            dg_ref[...] = jnp.zeros_like(dg_ref)

        dh = _dot(dp_ref[...], w_ref[...])
        xf = x_ref[...]
        rstd = lax.rsqrt(jnp.mean(xf * xf, axis=-1, keepdims=True) + EPS)
        xhat = xf * rstd
        dg_ref[...] += jnp.sum(dh * xhat, axis=0, keepdims=True)
        dyg = dh * g_ref[...]
        mean = jnp.mean(dyg * xhat, axis=-1, keepdims=True)
        dx_ref[...] = rstd * (dyg - xhat * mean) + dy_ref[...]

    row = lambda w: pl.BlockSpec((tm, w), lambda i: (i, 0))
    return pl.pallas_call(
        body, name="inproj_dx", grid=(S // tm,),
        in_specs=[row(PW), pl.BlockSpec((None, PW, D), lambda i: (layer, 0, 0)), row(D), pl.BlockSpec((1, D), lambda i: (0, 0)), row(D)],
        out_specs=[row(D), pl.BlockSpec((1, D), lambda i: (0, 0))],
        out_shape=[jax.ShapeDtypeStruct((S, D), F32), jax.ShapeDtypeStruct((1, D), F32)],
        compiler_params=_cparams(dimension_semantics=("arbitrary",)),
    )(dproj, wt_all, x, g, dy)


def _adam_update(w, g, m, v):
    nm = ADAM_B1 * m + (1.0 - ADAM_B1) * g
    nv = ADAM_B2 * v + (1.0 - ADAM_B2) * (g * g)
    m_hat = nm / (1.0 - ADAM_B1 ** ADAM_STEP)
    v_hat = nv / (1.0 - ADAM_B2 ** ADAM_STEP)
    return -ADAM_LR * (m_hat / (jnp.sqrt(v_hat) + ADAM_EPS) + ADAM_WD * w), nm, nv


def _adamw(w, g, m, v):
    L, R, C = w.shape
    tr = R if R <= 512 else 256

    def body(w_ref, g_ref, m_ref, v_ref, d_ref, nm_ref, nv_ref):
        d_ref[...], nm_ref[...], nv_ref[...] = _adam_update(w_ref[...], g_ref[...], m_ref[...], v_ref[...])

    spec = pl.BlockSpec((1, tr, C), lambda l, i: (l, i, 0))
    shp = jax.ShapeDtypeStruct((L, R, C), F32)
    return pl.pallas_call(
        body, name="adamw", grid=(L, R // tr), in_specs=[spec] * 4, out_specs=[spec] * 3, out_shape=[shp] * 3,
        compiler_params=_cparams(dimension_semantics=("arbitrary", "arbitrary")),
    )(w, g, m, v)


def _adamw_nd(w, g, m, v):
    shape = w.shape
    view = (1,) + shape if w.ndim == 2 else (shape[0], -1, shape[-1])
    outs = _adamw(w.reshape(view), g.reshape(view), m.reshape(view), v.reshape(view))
    return tuple(o.reshape(shape) for o in outs)


FLIP_C = (0, 0, 1)
FLIP_X = (1, 0, 0)
FLIP_Y = (0, 1, 0)
FLIP_XY = (1, 1, 0)
MESH = pl.DeviceIdType.MESH


def _peer(flip):
    me = (lax.axis_index("x"), lax.axis_index("y"), lax.axis_index("c"))
    return tuple(1 - a if f else a for a, f in zip(me, flip))


def _exchange(name, arrays, flips):
    n = len(arrays)

    def body(*refs):
        srcs, dsts = refs[:n], refs[n:2 * n]
        send_sems, recv_sems = refs[2 * n:]
        copies = [pltpu.make_async_remote_copy(src_ref=srcs[k], dst_ref=dsts[k], send_sem=send_sems.at[k], recv_sem=recv_sems.at[k],
                                               device_id=_peer(flips[k]), device_id_type=MESH) for k in range(n)]
        for cp in copies:
            cp.start()
        for cp in copies:
            cp.wait()

    anyspec = pl.BlockSpec(memory_space=pl.ANY)
    return pl.pallas_call(
        body, name=name, in_specs=[anyspec] * n, out_specs=[anyspec] * n,
        out_shape=[jax.ShapeDtypeStruct(a.shape, a.dtype) for a in arrays],
        scratch_shapes=[pltpu.SemaphoreType.DMA((n,)), pltpu.SemaphoreType.DMA((n,))],
    )(*arrays)


def _exchange_add(name, x, flip):
    def body(x_ref, o_ref, buf_ref, send_sem, recv_sem):
        cp = pltpu.make_async_remote_copy(src_ref=x_ref, dst_ref=buf_ref, send_sem=send_sem, recv_sem=recv_sem,
                                          device_id=_peer(flip), device_id_type=MESH)
        cp.start()
        cp.wait()
        o_ref[...] = x_ref[...] + buf_ref[...]

    vspec = pl.BlockSpec(memory_space=pltpu.VMEM)
    return pl.pallas_call(
        body, name=name, in_specs=[vspec], out_specs=vspec, out_shape=jax.ShapeDtypeStruct(x.shape, x.dtype),
        scratch_shapes=[pltpu.VMEM(x.shape, x.dtype), pltpu.SemaphoreType.DMA, pltpu.SemaphoreType.DMA],
    )(x)


def _chip_index():
    return 2 * lax.axis_index("x") + lax.axis_index("y")


def _gather_weights(w_in_t, w_out):
    wi = w_in_t.astype(BF16)
    wo = jnp.swapaxes(w_out, 0, 1).astype(BF16)
    halves = (wi.shape[0] // 2, wo.shape[0] // 2)
    ARR = 2
    TO_X, TO_Y, ON_Y, ON_X, SIB_X, SIB_Y, SIB_D0, SIB_D1, OWN = [ARR * k for k in range(9)]
    n_sems = ARR * 9

    def body(wi_ref, wo_ref, gi_ref, go_ref, send_sems, recv_sems):
        c = lax.axis_index("c")
        j = _chip_index()
        srcs = (wi_ref, wo_ref)
        dsts = (gi_ref, go_ref)
        def cuts(core):
            return [(pl.ds(h * core, h), pl.ds(h * core, h // 2), pl.ds(h * core + h // 2, h - h // 2)) for h in halves]
        mine, theirs = cuts(c), cuts(1 - c)
        HALF, Q0, Q1 = 0, 1, 2

        def copy(idx, src, dst, flip):
            return pltpu.make_async_remote_copy(src_ref=src, dst_ref=dst, send_sem=send_sems.at[idx], recv_sem=recv_sems.at[idx],
                                                device_id=_peer(flip), device_id_type=MESH)

        def slot(a, shard, cut):
            return dsts[a].at[shard, cut]

        jx, jy, jd = j ^ 2, j ^ 1, j ^ 3
        sends = []

        def start(cp):
            cp.start()
            sends.append(cp)

        for a in range(ARR):
            start(copy(TO_X + a, srcs[a].at[mine[a][HALF]], slot(a, j, mine[a][HALF]), FLIP_X))
            start(copy(TO_Y + a, srcs[a].at[mine[a][HALF]], slot(a, j, mine[a][HALF]), FLIP_Y))
        own = [copy(OWN + a, srcs[a], dsts[a].at[j], FLIP_C) for a in range(ARR)]
        for cp in own:
            cp.start()
        for a in range(ARR):
            copy(TO_X + a, slot(a, jx, mine[a][HALF]), slot(a, jx, mine[a][HALF]), FLIP_X).wait_recv()
            start(copy(ON_Y + a, slot(a, jx, mine[a][Q0]), slot(a, jx, mine[a][Q0]), FLIP_Y))
            start(copy(SIB_X + a, slot(a, jx, mine[a][HALF]), slot(a, jx, mine[a][HALF]), FLIP_C))
        for a in range(ARR):
            copy(TO_Y + a, slot(a, jy, mine[a][HALF]), slot(a, jy, mine[a][HALF]), FLIP_Y).wait_recv()
            start(copy(ON_X + a, slot(a, jy, mine[a][Q1]), slot(a, jy, mine[a][Q1]), FLIP_X))
            start(copy(SIB_Y + a, slot(a, jy, mine[a][HALF]), slot(a, jy, mine[a][HALF]), FLIP_C))
        for a in range(ARR):
            copy(ON_Y + a, slot(a, jd, mine[a][Q0]), slot(a, jd, mine[a][Q0]), FLIP_Y).wait_recv()
            start(copy(SIB_D0 + a, slot(a, jd, mine[a][Q0]), slot(a, jd, mine[a][Q0]), FLIP_C))
        for a in range(ARR):
            copy(ON_X + a, slot(a, jd, mine[a][Q1]), slot(a, jd, mine[a][Q1]), FLIP_X).wait_recv()
            start(copy(SIB_D1 + a, slot(a, jd, mine[a][Q1]), slot(a, jd, mine[a][Q1]), FLIP_C))
        for a in range(ARR):
            for idx, shard, cut in ((SIB_X, jx, HALF), (SIB_Y, jy, HALF), (SIB_D0, jd, Q0), (SIB_D1, jd, Q1)):
                copy(idx + a, slot(a, shard, theirs[a][cut]), slot(a, shard, theirs[a][cut]), FLIP_C).wait_recv()
        for cp in own:
            cp.wait()
        for cp in sends:
            cp.wait_send()

    anyspec = pl.BlockSpec(memory_space=pl.ANY)
    gi, go = pl.pallas_call(
        body, name="gather_weights", in_specs=[anyspec] * 2, out_specs=[anyspec] * 2,
        out_shape=[jax.ShapeDtypeStruct((4,) + wi.shape, BF16), jax.ShapeDtypeStruct((4,) + wo.shape, BF16)],
        scratch_shapes=[pltpu.SemaphoreType.DMA((n_sems,)), pltpu.SemaphoreType.DMA((n_sems,))],
    )(wi, wo)
    w_in_t_full = gi.reshape((4 * wi.shape[0],) + wi.shape[1:])
    w_out_full = jnp.swapaxes(go.reshape((4 * wo.shape[0],) + wo.shape[1:]), 0, 1)
    return w_in_t_full, w_out_full


def _to_aligned(w_t):
    _, L, D = w_t.shape
    npair = FOX_HEADS // 2
    ff = w_t[ORIG_FF:ORIG_REST].reshape(npair, 2, L, D)
    ff = jnp.pad(ff, ((0, 0), (0, FF_STRIDE - 2), (0, 0), (0, 0))).reshape(npair * FF_STRIDE, L, D)
    ff = jnp.pad(ff, ((0, LANES - npair * FF_STRIDE), (0, 0), (0, 0)))
    return jnp.swapaxes(jnp.concatenate([w_t[:ORIG_FOX], w_t[ORIG_REST:], ff], axis=0), 0, 1)


def _from_aligned(dw_t):
    n, _, D = dw_t.shape
    npair = FOX_HEADS // 2
    ff = dw_t[:, PM:PM + npair * FF_STRIDE].reshape(n, npair, FF_STRIDE, D)[:, :, :2].reshape(n, FOX_HEADS, D)
    return jnp.swapaxes(jnp.concatenate([dw_t[:, :ORIG_FOX], ff, dw_t[:, ORIG_FOX:PM]], axis=1), 0, 1)


RELAY_ROWS = 256


def _rows_first(stack_m, stack_f, got_m, got_f):
    n, _, D = got_m.shape
    npair = FOX_HEADS // 2
    first_late = ORIG_FOX // RELAY_ROWS

    def body(c_ref, m_ref, f_ref, gm_ref, gf_ref, out_ref, buf_ref, ff_ref, sem, ff_sem):
        i = pl.program_id(0)
        for l in range(n):
            buf_ref[:, l, :] = m_ref[l] + gm_ref[l].astype(F32)
        start = pl.multiple_of(i * RELAY_ROWS, FOX_HEADS) + jnp.where(i >= first_late, FOX_HEADS, 0)
        main = pltpu.make_async_copy(buf_ref, out_ref.at[pl.ds(start, RELAY_ROWS)], sem)
        main.start()

        @pl.when(i == 0)
        def _():
            for l in range(n):
                for p in range(npair):
                    rows = slice(FF_STRIDE * p, FF_STRIDE * p + 2)
                    ff_ref[2 * p:2 * p + 2, l, :] = f_ref[l, rows, :] + gf_ref[l, rows, :].astype(F32)
            ff = pltpu.make_async_copy(ff_ref, out_ref.at[pl.ds(ORIG_FF, FOX_HEADS)], ff_sem)
            ff.start()
            ff.wait()

        main.wait()

    grid_spec = pltpu.PrefetchScalarGridSpec(
        num_scalar_prefetch=1, grid=(PM // RELAY_ROWS,),
        in_specs=[pl.BlockSpec((n, RELAY_ROWS, D), lambda i, c: (c[0], i, 0)), pl.BlockSpec((n, LANES, D), lambda i, c: (c[0], 0, 0)),
                  pl.BlockSpec((n, RELAY_ROWS, D), lambda i, c: (0, i, 0)), pl.BlockSpec((n, LANES, D), lambda i, c: (0, 0, 0))],
        out_specs=pl.BlockSpec(memory_space=pl.ANY),
        scratch_shapes=[pltpu.VMEM((RELAY_ROWS, n, D), F32), pltpu.VMEM((FOX_HEADS, n, D), F32),
                        pltpu.SemaphoreType.DMA, pltpu.SemaphoreType.DMA])
    return pl.pallas_call(
        body, name="rs_rows_first", grid_spec=grid_spec, out_shape=jax.ShapeDtypeStruct((D_IN, n, D), F32),
        compiler_params=_cparams(dimension_semantics=("arbitrary",)),
    )(lax.axis_index("c").astype(jnp.int32).reshape(1), stack_m, stack_f, got_m, got_f)


def _half_layers(name, stack, got, also_bf16=True):
    L, R, C = stack.shape
    half = L // 2
    tr = min(256, R)
    c = lax.axis_index("c")
    which = ((1 - c) if got is None else c).astype(jnp.int32).reshape(1)

    def body(c_ref, x_ref, *refs):
        if got is None:
            refs[0][...] = x_ref[...].astype(BF16)
        else:
            acc = x_ref[...] + refs[0][...].astype(F32)
            refs[1][...] = acc
            if also_bf16:
                refs[2][...] = acc.astype(BF16)

    plain = pl.BlockSpec((1, tr, C), lambda l, i, c_ref: (l, i, 0))
    picked = pl.BlockSpec((1, tr, C), lambda l, i, c_ref: (c_ref[0] * half + l, i, 0))
    shp = lambda dt: jax.ShapeDtypeStruct((half, R, C), dt)
    out_shape = [shp(BF16)] if got is None else [shp(F32)] + ([shp(BF16)] if also_bf16 else [])
    grid_spec = pltpu.PrefetchScalarGridSpec(
        num_scalar_prefetch=1, grid=(half, R // tr),
        in_specs=[picked] + ([] if got is None else [plain]), out_specs=[plain] * len(out_shape))
    return pl.pallas_call(
        body, name=name, grid_spec=grid_spec, out_shape=out_shape,
        compiler_params=_cparams(dimension_semantics=("arbitrary", "arbitrary")),
    )(which, stack, *([] if got is None else [got]))


def _reduce_scatter(stack_m, stack_f, stack_o, shard_cols, shard_rows):
    j = _chip_index()
    half = DEPTH // 2
    stacks = (stack_m, stack_f, stack_o)
    give = [_half_layers("rs_give", s, None)[0] for s in stacks]
    got = _exchange("rs_d2d", give, (FLIP_C,) * len(stacks))
    o32, obf = _half_layers("rs_add_chip", stack_o, got[2])
    d_model = stack_m.shape[2]
    in32 = _rows_first(stack_m, stack_f, got[0], got[1]).reshape(4, shard_cols, half, d_model)

    def out_shards(o):
        return jnp.moveaxis(o.reshape(half, 4, shard_rows, o.shape[-1]), 1, 0)

    chip = [(in32, in32.astype(BF16), 0), (out_shards(o32), out_shards(obf), 1)]
    shard = lambda a, idx: lax.dynamic_index_in_dim(a, idx, axis=0, keepdims=False)
    via = []
    for _, bf, axis in chip:
        diag = shard(bf, j ^ 3)
        cut = diag.shape[axis] // 2
        via += [lax.slice_in_dim(diag, 0, cut, axis=axis), lax.slice_in_dim(diag, cut, 2 * cut, axis=axis)]
    handed = _exchange("rs_via", via, (FLIP_X, FLIP_Y) * len(chip))
    sends = []
    for a, (f32_sum, _, axis) in enumerate(chip):
        sends.append(_add_half_along("rs_add_via", f32_sum, handed[2 * a + 1], axis, 1, pick=j ^ 2))
        sends.append(_add_half_along("rs_add_via", f32_sum, handed[2 * a], axis, 0, pick=j ^ 1))
    got = _exchange("rs_ici", sends, (FLIP_X, FLIP_Y) * len(chip))
    mine_in = _add_rows("rs_add_in", chip[0][0], list(got[0:2]), pick=j)
    mine_out = _add_into_half("rs_add_out", shard(chip[1][0], j), list(got[2:4]))
    sib_in, g_out = _share_halves(mine_in, mine_out)
    return (mine_in, sib_in), g_out


def _picked(spec, pick):
    return pl.BlockSpec((None,) + tuple(spec.block_shape), lambda *a: (a[-1][0],) + tuple(spec.index_map(*a[:-1])))


def _add_half_along(name, base, extra, axis, which, pick=None):
    shape = base.shape if pick is None else base.shape[1:]
    lanes = min(ROW_LANE_CHUNK, shape[2])
    assert shape[axis] == 2 * extra.shape[axis]
    blk = tuple(shape[d] // 2 if d == axis else shape[d] for d in range(2)) + (lanes,)

    def body(*refs):
        b_ref, e_ref, o_ref = refs[-3:]
        x = b_ref[...]
        o_ref[...] = jnp.where(pl.program_id(0) == which, x + e_ref[...].astype(F32), x).astype(BF16)

    at = lambda i, k, *_: (i, 0, k) if axis == 0 else (0, i, k)
    bspec, espec = pl.BlockSpec(blk, at), pl.BlockSpec(blk, lambda i, k, *_: (0, 0, k))
    kw = dict(out_shape=jax.ShapeDtypeStruct(shape, BF16), name=name, compiler_params=_cparams(dimension_semantics=("arbitrary", "arbitrary")))
    grid = (2, shape[2] // lanes)
    if pick is None:
        return pl.pallas_call(body, grid=grid, in_specs=[bspec, espec], out_specs=bspec, **kw)(base, extra)
    grid_spec = pltpu.PrefetchScalarGridSpec(num_scalar_prefetch=1, grid=grid, in_specs=[_picked(bspec, pick), espec], out_specs=bspec)
    return pl.pallas_call(body, grid_spec=grid_spec, **kw)(pick.astype(jnp.int32).reshape(1), base, extra)


def _add_rows(name, first, others, pick=None):
    n = len(others)
    shape = first.shape if pick is None else first.shape[1:]

    def body(*refs):
        refs = refs[-(n + 2):]
        acc = refs[0][...]
        for r in refs[1:1 + n]:
            acc = acc + r[...].astype(F32)
        refs[1 + n][...] = acc

    grid, spec = _row_lane_blocks(shape)
    sp = spec(shape[1])
    kw = dict(out_shape=jax.ShapeDtypeStruct(shape, F32), name=name, compiler_params=_cparams(dimension_semantics=("arbitrary", "arbitrary")))
    if pick is None:
        return pl.pallas_call(body, grid=grid, in_specs=[sp] * (1 + n), out_specs=sp, **kw)(first, *others)
    grid_spec = pltpu.PrefetchScalarGridSpec(num_scalar_prefetch=1, grid=grid, in_specs=[_picked(sp, pick)] + [sp] * n, out_specs=sp)
    return pl.pallas_call(body, grid_spec=grid_spec, **kw)(pick.astype(jnp.int32).reshape(1), first, *others)


ROW_LANE_CHUNK = 256


def _row_lane_blocks(shape):
    rows, _, C = shape
    tr = rows // 2 if rows % 2 == 0 and rows > 64 else rows
    lanes = min(ROW_LANE_CHUNK, C)
    return (rows // tr, C // lanes), lambda n_mid: pl.BlockSpec((tr, n_mid, lanes), lambda i, k, *_: (i, 0, k))


def _add_into_half(name, first, others):
    half, rows, C = first.shape
    tr = min(256, rows)
    n = len(others)

    def body(c_ref, *refs):
        acc = refs[0][...]
        for r in refs[1:1 + n]:
            acc = acc + r[...].astype(F32)
        refs[1 + n][...] = acc

    grid_spec = pltpu.PrefetchScalarGridSpec(
        num_scalar_prefetch=1, grid=(half, rows // tr),
        in_specs=[pl.BlockSpec((1, tr, C), lambda l, i, c_ref: (l, i, 0))] * (1 + n),
        out_specs=pl.BlockSpec((1, tr, C), lambda l, i, c_ref: (c_ref[0] * half + l, i, 0)))
    return pl.pallas_call(
        body, name=name, grid_spec=grid_spec, out_shape=jax.ShapeDtypeStruct((2 * half, rows, C), F32),
        compiler_params=_cparams(dimension_semantics=("arbitrary", "arbitrary")),
    )(lax.axis_index("c").astype(jnp.int32).reshape(1), first, *others)


def _share_halves(mine, buf):
    half = DEPTH // 2

    def body(mine_ref, buf_in, sib_ref, buf_ref, send_sems, recv_sems):
        lay = pl.ds(half * lax.axis_index("c"), half)
        copies = [pltpu.make_async_remote_copy(src_ref=src, dst_ref=dst, send_sem=send_sems.at[k], recv_sem=recv_sems.at[k],
                                               device_id=_peer(FLIP_C), device_id_type=MESH)
                  for k, (src, dst) in enumerate(((mine_ref, sib_ref), (buf_ref.at[lay], buf_ref.at[lay])))]
        for cp in copies:
            cp.start()
        for cp in copies:
            cp.wait()

    anyspec = pl.BlockSpec(memory_space=pl.ANY)
    return pl.pallas_call(
        body, name="rs_share", in_specs=[anyspec] * 2, out_specs=[anyspec] * 2,
        out_shape=[jax.ShapeDtypeStruct(mine.shape, mine.dtype), jax.ShapeDtypeStruct(buf.shape, buf.dtype)],
        input_output_aliases={1: 1},
        scratch_shapes=[pltpu.SemaphoreType.DMA((2,)), pltpu.SemaphoreType.DMA((2,))],
    )(mine, buf)


def _adamw_halves(w, g_mine, g_sib, m, v):
    half = g_mine.shape[1]

    def body(c_ref, w_ref, gm_ref, gs_ref, m_ref, v_ref, g_ref, d_ref, nm_ref, nv_ref):
        first = c_ref[0] == 0
        gm, gs = gm_ref[...], gs_ref[...]
        for h, gv in enumerate((jnp.where(first, gm, gs), jnp.where(first, gs, gm))):
            lay = slice(half * h, half * (h + 1))
            g_ref[:, lay, :] = gv
            d_ref[:, lay, :], nm_ref[:, lay, :], nv_ref[:, lay, :] = _adam_update(w_ref[:, lay, :], gv, m_ref[:, lay, :], v_ref[:, lay, :])

    grid, spec = _row_lane_blocks(w.shape)
    full, part = spec(w.shape[1]), spec(half)
    grid_spec = pltpu.PrefetchScalarGridSpec(num_scalar_prefetch=1, grid=grid, in_specs=[full, part, part, full, full], out_specs=[full] * 4)
    return pl.pallas_call(
        body, name="adamw_halves", grid_spec=grid_spec, out_shape=[jax.ShapeDtypeStruct(w.shape, F32)] * 4,
        compiler_params=_cparams(dimension_semantics=("arbitrary", "arbitrary")),
    )(lax.axis_index("c").astype(jnp.int32).reshape(1), w, g_mine, g_sib, m, v)


def _all_reduce_small(x):
    x = _exchange_add("ar_c", x, FLIP_C)
    x = _exchange_add("ar_y", x, FLIP_Y)
    return _exchange_add("ar_x", x, FLIP_X)


def _blocks(S):
    return dict(tm=min(512, S), tm_proj=min(1024, S), ts=min(512, S), tq=min(512, S), tq_big=min(1024, S), tk=min(512, S), tks=min(256, S))


def _pair_pad(vec):
    npair = FOX_HEADS // 2
    v = jnp.pad(vec.reshape(npair, 2), ((0, 0), (0, FF_STRIDE - 2))).reshape(1, npair * FF_STRIDE)
    return jnp.pad(v, ((0, 0), (0, LANES - npair * FF_STRIDE)))


def _pair_unpad(row):
    npair = FOX_HEADS // 2
    return row[0, :npair * FF_STRIDE].reshape(npair, FF_STRIDE)[:, :2].reshape(FOX_HEADS)


def _pool_blockdiag(w_pool):
    g, cg, _ = w_pool.shape
    eye = jnp.eye(g, dtype=w_pool.dtype)
    return jnp.einsum("gh,gcd->gchd", eye, w_pool).reshape(g * cg, g * cg)


QK_BOUND_SLACK = 1.05


def _layer_params(norm_g, b_f, q_norm_g, k_norm_g, w_pool, pool_scale):
    qk_bound = QK_BOUND_SLACK * HEAD_DIM * QK_SCALE * jnp.max(jnp.abs(q_norm_g)) * jnp.max(jnp.abs(k_norm_g))
    return dict(g=norm_g.reshape(1, -1), qg=jnp.tile(q_norm_g, FOX_HEADS).reshape(1, FOX_W), kg=jnp.tile(k_norm_g, FOX_HEADS).reshape(1, FOX_W),
                bfp=_pair_pad(b_f), wpd=_pool_blockdiag(w_pool).astype(BF16), ps=pool_scale.reshape(1, POOL_W),
                qkb=jnp.full((1, LANES), qk_bound, F32))


def _layer_fwd(x, wt_all, w_out, layer, prm, bs, target=None):
    projm, ffo, h = _inproj(x, prm["g"], wt_all, layer, tm=bs["tm_proj"], tn=PROJ_TN)
    qn, ka, kb, v, sq, sk, sv, pooled, yp, pm = _prep(projm, ffo, prm["qg"], prm["kg"], prm["bfp"], prm["wpd"], prm["ps"], ts=bs["ts"])
    o, lse, fm = _fox_fwd(qn, ka, kb, v, projm, prm["qkb"], tq=bs["tq"], tk=bs["tk"])
    so, sm = _sb_fwd(sq, sk, sv, projm, tq=bs["tq"], tk=bs["tks"])
    y = _outproj(x, fm, pm, sm, w_out, layer, tm=bs["tm_proj"], target=target)
    saved = dict(x=x, projm=projm, ffo=ffo, h=h, qn=qn, ka=ka, kb=kb, v=v, sq=sq, sk=sk, sv=sv, pooled=pooled, yp=yp,
                 o=o, lse=lse, so=so, fm=fm, pm=pm, sm=sm)
    return y, saved


def _layer_bwd(dy, wt_all, w_out, prm, sv_, bs, layer, stacks):
    dmix, stack_o = _outproj_bwd(dy, sv_["fm"], sv_["pm"], sv_["sm"], w_out, layer, None if stacks is None else stacks[2:], tm=bs["tm_proj"])
    dqn, dkn, dv, dfg, dct, dcr = _fox_bwd(sv_["qn"], sv_["ka"], sv_["kb"], sv_["v"], sv_["o"], sv_["lse"], dmix, sv_["projm"],
                                      prm["qkb"], tq=bs["tq_big"], tk=bs["tk"])
    dsq, dsk, dsv, dsg = _sb_bwd(sv_["sq"], sv_["sk"], sv_["sv"], sv_["so"], dmix, sv_["projm"], tq=bs["tks"], tk=bs["tks"])
    dproj, dqg, dkg, dbf, dwp, dps = _prep_bwd(sv_["projm"], sv_["ffo"], dqn, dkn, dct, dcr, dv, dfg, dsq, dsk, dsv, dsg, dmix,
                                               sv_["pooled"], sv_["yp"], prm["qg"], prm["kg"], prm["bfp"], prm["wpd"], prm["ps"], ts=bs["ts"])
    stack_m, stack_f = _inproj_dw(sv_["h"], dproj, layer, None if stacks is None else stacks[:2], ts=bs["tm_proj"], tn=PROJ_TN)
    dx, dg = _inproj_dx(dproj, wt_all, layer, sv_["x"], prm["g"], dy, tm=bs["tm"])
    grads = dict(
        norm_g=dg[0],
        b_f=_pair_unpad(dbf), q_norm_g=dqg.reshape(FOX_HEADS, HEAD_DIM).sum(0), k_norm_g=dkg.reshape(FOX_HEADS, HEAD_DIM).sum(0),
        w_pool=jnp.stack([dwp[HEAD_DIM * g:HEAD_DIM * (g + 1), HEAD_DIM * g:HEAD_DIM * (g + 1)] for g in range(4)]),
        pool_scale=dps[0])
    return dx, grads, (stack_m, stack_f, stack_o)


def _local_step(x, target, wt_all, w_out, norm_g, b_f, q_norm_g, k_norm_g, w_pool, pool_scale):
    S, D = x.shape
    bs = _blocks(S)
    prms = [_layer_params(norm_g[l], b_f[l], q_norm_g[l], k_norm_g[l], w_pool[l], pool_scale[l]) for l in range(DEPTH)]
    saved = []
    y = x
    for l in range(DEPTH):
        y, s_ = _layer_fwd(y, wt_all, w_out, l, prms[l], bs, target if l == DEPTH - 1 else None)
        saved.append(s_)
    dy, sq = y
    loss = 0.5 * jnp.sum(sq) / D
    grads = [None] * DEPTH
    stacks = None
    for l in reversed(range(DEPTH)):
        dy, grads[l], stacks = _layer_bwd(dy, wt_all, w_out, prms[l], saved[l], bs, l, stacks)
    stacked = {k: jnp.stack([g[k] for g in grads]) for k in grads[0]}
    return loss, dy, stacked, stacks


SMALL = ("norm_g", "b_f", "q_norm_g", "k_norm_g", "w_pool", "pool_scale")


def _pack_small(gr):
    flat = jnp.concatenate([gr[k].reshape(-1) for k in SMALL])
    pad = (-flat.shape[0]) % (8 * LANES)
    return jnp.pad(flat, (0, pad)).reshape(-1, LANES)


def _unpack_small(packed, like):
    flat = packed.reshape(-1)
    out, off = {}, 0
    for k in SMALL:
        n = like[k].size
        out[k] = flat[off:off + n].reshape(like[k].shape)
        off += n
    return out


def kernel(x, norm_g, w_in, b_f, q_norm_g, k_norm_g, w_pool, pool_scale, w_out, loss_target, m_norm_g, m_w_in, m_b_f, m_q_norm_g, m_k_norm_g, m_w_pool, m_pool_scale, m_w_out, v_norm_g, v_w_in, v_b_f, v_q_norm_g, v_k_norm_g, v_w_pool, v_pool_scale, v_w_out):
    weights = dict(norm_g=norm_g, w_in=w_in, b_f=b_f, q_norm_g=q_norm_g, k_norm_g=k_norm_g, w_pool=w_pool, pool_scale=pool_scale, w_out=w_out)
    mom_m = dict(norm_g=m_norm_g, w_in=m_w_in, b_f=m_b_f, q_norm_g=m_q_norm_g, k_norm_g=m_k_norm_g, w_pool=m_w_pool, pool_scale=m_pool_scale, w_out=m_w_out)
    mom_v = dict(norm_g=v_norm_g, w_in=v_w_in, b_f=v_b_f, q_norm_g=v_q_norm_g, k_norm_g=v_k_norm_g, w_pool=v_w_pool, pool_scale=v_pool_scale, w_out=v_w_out)
    shard_cols = w_in.shape[2]
    shard_rows = w_out.shape[1]

    cols_first = lambda a: jnp.transpose(a, (2, 0, 1))
    w_in_t = cols_first(w_in)
    w_in_t_full, w_out_full = _gather_weights(w_in_t, w_out)
    wt_all = _to_aligned(w_in_t_full)
    loss, dx, gr, stacks = _local_step(x[0], loss_target[0], wt_all, w_out_full, norm_g, b_f, q_norm_g, k_norm_g, w_pool, pool_scale)
    loss = lax.psum(loss, ("x", "y", "c"))

    (g_in_mine, g_in_sib), g_w_out = _reduce_scatter(*stacks, shard_cols, shard_rows)
    small = _unpack_small(_all_reduce_small(_pack_small(gr)), {k: weights[k] for k in SMALL})
    grad_w = dict(small, w_out=g_w_out)

    names = ("norm_g", "w_in", "b_f", "q_norm_g", "k_norm_g", "w_pool", "pool_scale", "w_out")
    upd = {k: _adamw_nd(weights[k], grad_w[k], mom_m[k], mom_v[k]) for k in names if k != "w_in"}
    in_t = _adamw_halves(w_in_t, g_in_mine, g_in_sib, cols_first(mom_m["w_in"]), cols_first(mom_v["w_in"]))
    grad_w["w_in"], *upd["w_in"] = [jnp.transpose(a, (1, 2, 0)) for a in in_t]
    return (loss, dx[None], *[grad_w[k] for k in names], *[upd[k][0] for k in names], *[upd[k][1] for k in names], *[upd[k][2] for k in names])
```

```python
import functools

import jax
import jax.numpy as jnp
from jax import lax
from jax.experimental import pallas as pl
from jax.experimental.pallas import tpu as pltpu

F32 = jnp.float32
BF16 = jnp.bfloat16

DEPTH = 4
HEAD_DIM = 64
FOX_HEADS = 8
SB_HEADS = 4
FOX_W = FOX_HEADS * HEAD_DIM
SB_W = SB_HEADS * HEAD_DIM
POOL_W = 256
POOL_WINDOWS = (2, 4, 8, 16)
POOL_HALO = 16
D_MIX = FOX_W + POOL_W + SB_W
EPS = 1e-6
NEG = -1e30
QK_SCALE = HEAD_DIM ** -0.5

ORIG_FOX = 4 * FOX_W
ORIG_FF = ORIG_FOX
ORIG_REST = ORIG_FF + FOX_HEADS
D_IN = ORIG_REST + 2 * POOL_W + 4 * SB_W

C_FQ, C_FK, C_FV, C_FG = 0, FOX_W, 2 * FOX_W, 3 * FOX_W
C_PX = 4 * FOX_W
C_PG = C_PX + POOL_W
C_SQ = C_PG + POOL_W
C_SK, C_SV, C_SG = C_SQ + SB_W, C_SQ + 2 * SB_W, C_SQ + 3 * SB_W
PM = C_SG + SB_W
LANES = 128
LANE_SHIFT = 7
HEAD_SHIFT = 6
PW = PM + LANES
FF_STRIDE = 8
AUG = 3

ADAM_LR = 0.001
ADAM_B1 = 0.9
ADAM_B2 = 0.999
ADAM_EPS = 1e-08
ADAM_WD = 0.01
ADAM_STEP = 10

VMEM_LIMIT = 48 * 1024 * 1024
PROJ_TN = PM // 2


def _cparams(**kw):
    return pltpu.CompilerParams(vmem_limit_bytes=VMEM_LIMIT, **kw)


def _dot(a, b):
    return jnp.dot(a, b, preferred_element_type=F32)


def _dot_nt(a, b):
    return lax.dot_general(a, b, (((1,), (1,)), ((), ())), preferred_element_type=F32)


def _dot_tn(a, b):
    return lax.dot_general(a, b, (((0,), (0,)), ((), ())), preferred_element_type=F32)


def _split2(x):
    hi = x.astype(BF16)
    lo = (x - hi.astype(F32)).astype(BF16)
    return hi, lo


def _split3(x):
    hi = x.astype(BF16)
    r = x - hi.astype(F32)
    mid = r.astype(BF16)
    lo = (r - mid.astype(F32)).astype(BF16)
    return hi, mid, lo


def _dot_exact_rhs(x, m):
    hi, mid, lo = _split3(x)
    return _dot(hi, m) + _dot(mid, m) + _dot(lo, m)


def _dot_exact_lhs(m, x):
    hi, mid, lo = _split3(x)
    return _dot(m, hi) + _dot(m, mid) + _dot(m, lo)


def _sigmoid(x):
    return 1.0 / (1.0 + jnp.exp(-x))


def _silu_pair(x):
    s = _sigmoid(x)
    return x * s, s * (1.0 + x * (1.0 - s))


def _iota(shape, dim):
    return lax.broadcasted_iota(jnp.int32, shape, dim)


def _ones_where(cond):
    return jnp.where(cond, 1.0, 0.0).astype(BF16)


GROUP_SLAB = 256


def _head_blockdiag():
    rows, cols = _iota((2 * GROUP_SLAB, GROUP_SLAB), 0) & (GROUP_SLAB - 1), _iota((2 * GROUP_SLAB, GROUP_SLAB), 1)
    return _ones_where((rows >> HEAD_SHIFT) == (cols >> HEAD_SHIFT))


def _group_sum(x, bd):
    hi, lo = _split2(x)
    slabs = [_dot(jnp.concatenate([hi[:, s:s + GROUP_SLAB], lo[:, s:s + GROUP_SLAB]], axis=1), bd) for s in range(0, x.shape[1], GROUP_SLAB)]
    return jnp.concatenate(slabs, axis=1)


def _lane_pick(x, lane_idx, lane):
    return jnp.sum(jnp.where(lane_idx == lane, x, 0.0), axis=1, keepdims=True)


def _inproj(x, g, wt_all, layer, *, tm, tn):
    S, D = x.shape
    nj = PM // tn

    def body(x_ref, g_ref, w_ref, wff_ref, proj_ref, ff_ref, h_ref):
        @pl.when(pl.program_id(1) == 0)
        def _():
            xf = x_ref[...]
            ms = jnp.mean(xf * xf, axis=-1, keepdims=True)
            h = (xf * lax.rsqrt(ms + EPS) * g_ref[...]).astype(BF16)
            h_ref[...] = h
            ff_ref[...] = _dot_nt(h, wff_ref[...])

        proj_ref[...] = _dot_nt(h_ref[...], w_ref[...])

    return pl.pallas_call(
        body, name="inproj", grid=(S // tm, nj),
        in_specs=[pl.BlockSpec((tm, D), lambda i, j: (i, 0)),
                  pl.BlockSpec((1, D), lambda i, j: (0, 0)),
                  pl.BlockSpec((None, tn, D), lambda i, j: (layer, j, 0)),
                  pl.BlockSpec((None, LANES, D), lambda i, j: (layer, PM // LANES, 0))],
        out_specs=[pl.BlockSpec((tm, tn), lambda i, j: (i, j)),
                   pl.BlockSpec((tm, LANES), lambda i, j: (i, 0)),
                   pl.BlockSpec((tm, D), lambda i, j: (i, 0))],
        out_shape=[jax.ShapeDtypeStruct((S, PM), F32), jax.ShapeDtypeStruct((S, LANES), F32),
                   jax.ShapeDtypeStruct((S, D), BF16)],
        compiler_params=_cparams(dimension_semantics=("arbitrary", "arbitrary")),
    )(x, g, wt_all, wt_all)


def _pool_group_select(lane_group, vals):
    return jnp.where(lane_group == 0, vals[0], jnp.where(lane_group == 1, vals[1], jnp.where(lane_group == 2, vals[2], vals[3])))


def _prep(projm, ffo, qg, kg, bfp, wpd, ps, *, ts):
    S = projm.shape[0]
    nb = S // ts
    hb = ts // POOL_HALO

    def body(fq_ref, fk_ref, fv_ref, pp_ref, halo_ref, ff_ref, sq_ref, sk_ref, sv_ref,
             qg_ref, kg_ref, bf_ref, wpd_ref, ps_ref,
             qn_ref, ka_ref, kb_ref, v_ref, sqo_ref, sko_ref, svo_ref, pooled_ref, yp_ref, pm_ref,
             carry_ref, c_ref, buf_ref):
        i = pl.program_id(0)
        bd = _head_blockdiag()
        normed = []
        for src, g_ref in ((fq_ref, qg_ref), (fk_ref, kg_ref)):
            q = src[...]
            ss = _group_sum(q * q, bd)
            normed.append(q * lax.rsqrt(ss * (1.0 / HEAD_DIM) + EPS) * g_ref[...])
        qn_ref[...] = (normed[0] * QK_SCALE).astype(BF16)
        kn = normed[1]
        v_ref[...] = fv_ref[...].astype(BF16)
        sqo_ref[...] = (sq_ref[...] * QK_SCALE).astype(BF16)
        sko_ref[...] = sk_ref[...].astype(BF16)
        svo_ref[...] = sv_ref[...].astype(BF16)

        @pl.when(i == 0)
        def _():
            carry_ref[...] = jnp.zeros_like(carry_ref)

        z = ff_ref[...] + bf_ref[...]
        lf = jnp.minimum(z, 0.0) - jnp.log(1.0 + jnp.exp(-jnp.abs(z)))
        tri = _ones_where(_iota((ts, ts), 1) <= _iota((ts, ts), 0))
        c = _dot_exact_lhs(tri, lf) + carry_ref[...]
        c_ref[...] = c
        carry_ref[...] = c_ref[ts - 1:ts, :]
        parts = jnp.concatenate(_split3(-c), axis=1)
        row = _iota((AUG * LANES, FOX_W), 0)
        col = _iota((AUG * LANES, FOX_W), 1)
        part, src = row >> LANE_SHIFT, row & (LANES - 1)
        pair, off = col >> LANE_SHIFT, col & (LANES - 1)
        sel_a = _ones_where((src == FF_STRIDE * pair) & (off == HEAD_DIM + part))
        sel_b = _ones_where((src == FF_STRIDE * pair + 1) & (off == part))
        first_half = (_iota((1, FOX_W), 1) & HEAD_DIM) == 0
        ka_ref[...] = jnp.where(first_half, kn, _dot(parts, sel_a)).astype(BF16)
        kb_ref[...] = jnp.where(first_half, _dot(parts, sel_b), kn).astype(BF16)

        x = pp_ref[:, 0:POOL_W]
        pg = pp_ref[:, POOL_W:2 * POOL_W]
        halo = jnp.where(i > 0, halo_ref[:, 0:POOL_W], 0.0)
        buf_ref[0:POOL_HALO, :] = halo
        buf_ref[POOL_HALO:POOL_HALO + ts, :] = x
        acc = x
        snaps = []
        for d in range(1, POOL_HALO):
            acc = acc + buf_ref[pl.ds(POOL_HALO - d, ts), :]
            if d + 1 in POOL_WINDOWS:
                snaps.append(acc)
        lane_group = _iota((1, POOL_W), 1) >> HEAD_SHIFT
        wsum = _pool_group_select(lane_group, snaps)
        wlen = _pool_group_select(lane_group, [float(w) for w in POOL_WINDOWS])
        tpos = (i * ts + _iota((ts, 1), 0) + 1).astype(F32)
        pooled = wsum / jnp.minimum(tpos, wlen) - x
        pb = pooled.astype(BF16)
        pooled_ref[...] = pb
        yp = _dot(pb, wpd_ref[...])
        yp_ref[...] = yp
        pm_ref[...] = (yp * ps_ref[...] * (pg * _sigmoid(pg))).astype(BF16)

    blk = lambda w, c: pl.BlockSpec((ts, w), lambda i: (i, c))
    full = lambda a: pl.BlockSpec(a.shape, lambda i: (0,) * a.ndim)
    out_shapes = [
        jax.ShapeDtypeStruct((S, FOX_W), BF16), jax.ShapeDtypeStruct((S, FOX_W), BF16), jax.ShapeDtypeStruct((S, FOX_W), BF16),
        jax.ShapeDtypeStruct((S, FOX_W), BF16),
        jax.ShapeDtypeStruct((S, SB_W), BF16), jax.ShapeDtypeStruct((S, SB_W), BF16), jax.ShapeDtypeStruct((S, SB_W), BF16),
        jax.ShapeDtypeStruct((S, POOL_W), BF16), jax.ShapeDtypeStruct((S, POOL_W), F32), jax.ShapeDtypeStruct((S, POOL_W), BF16),
    ]
    out_specs = [
        blk(FOX_W, 0), blk(FOX_W, 0), blk(FOX_W, 0), blk(FOX_W, 0),
        blk(SB_W, 0), blk(SB_W, 0), blk(SB_W, 0),
        blk(POOL_W, 0), blk(POOL_W, 0), blk(POOL_W, 0),
    ]
    return pl.pallas_call(
        body, name="prep", grid=(nb,),
        in_specs=[blk(FOX_W, C_FQ // FOX_W), blk(FOX_W, C_FK // FOX_W), blk(FOX_W, C_FV // FOX_W), blk(2 * POOL_W, C_PX // (2 * POOL_W)),
                  pl.BlockSpec((POOL_HALO, 2 * POOL_W), lambda i: (jnp.maximum(i * hb - 1, 0), C_PX // (2 * POOL_W))),
                  blk(LANES, 0),
                  blk(SB_W, C_SQ // SB_W), blk(SB_W, C_SK // SB_W), blk(SB_W, C_SV // SB_W),
                  full(qg), full(kg), full(bfp), full(wpd), full(ps)],
        out_specs=out_specs, out_shape=out_shapes,
        scratch_shapes=[pltpu.VMEM((1, LANES), F32), pltpu.VMEM((ts, LANES), F32), pltpu.VMEM((ts + POOL_HALO, POOL_W), F32)],
        compiler_params=_cparams(dimension_semantics=("arbitrary",)),
    )(projm, projm, projm, projm, projm, ffo, projm, projm, projm, qg, kg, bfp, wpd, ps)


def _pair_masks(x):
    ma = _iota((1, LANES), 1) < HEAD_DIM
    zero = jnp.zeros_like(x)
    return jnp.where(ma, x, zero), jnp.where(ma, zero, x)


DIAG_TILE = 256


def _diag_tiles(tq, size=DIAG_TILE):
    size = min(tq, size)
    return [(t * size, size) for t in range(tq // size)]


def _put_rows(old, new, r0):
    return new if r0 == 0 else jnp.concatenate([old[:r0], new], axis=0)


def _aug_queries(q):
    lane = _iota((1, LANES), 1)
    one = jnp.ones_like(q)
    zero = jnp.zeros_like(q)
    qa = jnp.where(lane < HEAD_DIM, q, jnp.where(lane < HEAD_DIM + AUG, one, zero))
    qb = jnp.where(lane >= HEAD_DIM, q, jnp.where(lane < AUG, one, zero))
    return qa, qb


EXP_DEAD = -105.0
PACK = 16


def _fox_walk_left(nfull, tk, block, carry, k_refs, qk_bound, row_floor):
    lane = _iota((1, LANES), 1)

    def alive(h, jj, c):
        k0 = pl.multiple_of(jnp.maximum(nfull - 1 - jj, 0) * tk + tk - PACK, PACK)
        last = k_refs[h][pl.ds(k0, PACK), :].astype(F32)
        lo = HEAD_DIM if h == 0 else 0
        negc = jnp.sum(jnp.where((lane >= lo) & (lane < lo + AUG), last, 0.0), axis=1, keepdims=True)
        return qk_bound + jnp.max(negc) - row_floor(c)[h] >= EXP_DEAD

    def walk(heads, jj0, c0):
        def go_on(state):
            jj, c = state
            ok = jj < nfull
            for h in heads:
                ok = ok & alive(h, jj, c)
            return ok

        def step(state):
            jj, c = state
            return jj + 1, block(pl.multiple_of((nfull - 1 - jj) * tk, tk), tk, 0, c, False, heads)

        return lax.while_loop(go_on, step, (jj0, c0))

    jj_pair, carry = walk((0, 1), jnp.int32(0), carry)
    carry = walk((0,), jj_pair, carry)[1]
    return walk((1,), jj_pair, carry)[1]


def _fox_fwd(qn, ka, kb, v, projm, qkb, *, tq, tk):
    S = qn.shape[0]
    npair = FOX_HEADS // 2

    def body(q_ref, ka_ref, kb_ref, v_ref, fg_ref, qkb_ref, o_ref, lse_ref, fm_ref):
        qi = pl.program_id(1)
        lane = _iota((1, LANES), 1)
        ma = lane < HEAD_DIM
        qaug = _aug_queries(q_ref[...])
        k_refs = (ka_ref, kb_ref)

        def block(k0, tkl, r0, carry, masked, heads=(0, 1)):
            vb = v_ref[pl.ds(k0, tkl), :]
            if masked:
                mask = (k0 + _iota((tq - r0, tkl), 1)) <= (qi * tq + r0 + _iota((tq - r0, tkl), 0))
            scores = {h: _dot_nt(qaug[h][r0:], k_refs[h][pl.ds(k0, tkl), :]) for h in heads}
            new = list(carry)
            for h in heads:
                m, l, acc = [x[r0:] for x in carry[h]]
                s = jnp.where(mask, scores[h], NEG) if masked else scores[h]
                m_new = jnp.maximum(m, jnp.max(s, axis=1, keepdims=True))
                alpha = jnp.exp(m - m_new)
                p = jnp.exp(s - m_new)
                sub = (m_new, alpha * l + jnp.sum(p, axis=1, keepdims=True), alpha * acc + _dot(p.astype(BF16), vb))
                new[h] = tuple(_put_rows(old, x, r0) for old, x in zip(carry[h], sub))
            return tuple(new)

        carry = tuple((jnp.full((tq, 1), NEG, F32), jnp.zeros((tq, 1), F32), jnp.zeros((tq, LANES), F32)) for _ in range(2))
        for off, size in _diag_tiles(tq, tq):
            carry = block(pl.multiple_of(qi * tq + off, size), size, off, carry, True)
        carry = _fox_walk_left((qi * tq) // tk, tk, block, carry, k_refs, jnp.max(qkb_ref[...]),
                               lambda c: (jnp.min(c[0][0]), jnp.min(c[1][0])))
        (ma_, la, acca), (mb_, lb, accb) = carry
        o = jnp.where(ma, acca / la, accb / lb)
        o_ref[...] = o
        lse_ref[...] = jnp.where(ma, ma_ + jnp.log(la), mb_ + jnp.log(lb))
        fg = fg_ref[...]
        fm_ref[...] = (o * (fg * _sigmoid(fg))).astype(BF16)

    qblk = pl.BlockSpec((tq, LANES), lambda p, i: (i, p))
    kvblk = pl.BlockSpec((S, LANES), lambda p, i: (0, p))
    return pl.pallas_call(
        body, name="fox_fwd", grid=(npair, S // tq),
        in_specs=[qblk, kvblk, kvblk, kvblk,
                  pl.BlockSpec((tq, LANES), lambda p, i: (i, C_FG // LANES + p)),
                  pl.BlockSpec((1, LANES), lambda p, i: (0, 0))],
        out_specs=[qblk, qblk, qblk],
        out_shape=[jax.ShapeDtypeStruct((S, FOX_W), F32), jax.ShapeDtypeStruct((S, FOX_W), F32), jax.ShapeDtypeStruct((S, FOX_W), BF16)],
        compiler_params=_cparams(dimension_semantics=("arbitrary", "arbitrary")),
    )(qn, ka, kb, v, projm, qkb)


def _suffix_sums(x, tmat2):
    return _dot(jnp.concatenate(_split2(x), axis=1), tmat2)


def _suffix_matrix(tk, inclusive):
    rr, cc = _iota((2 * tk, tk), 0) & (tk - 1), _iota((2 * tk, tk), 1)
    return _ones_where(rr >= cc) if inclusive else _ones_where(rr > cc)


def _sb_scores(qh, kb, causal, tmat2, r_runs):
    heads = range(2)
    zs = [_dot_nt(qh[h], kb) for h in heads]
    nsps = [jnp.minimum(-z, 0.0) - jnp.log(1.0 + jnp.exp(-jnp.abs(z))) for z in zs]
    lbs = nsps if causal is None else [jnp.where(causal, n, 0.0) for n in nsps]
    rins = [_suffix_sums(lb, tmat2) for lb in lbs]
    args = [zs[h] + lbs[h] + (rins[h] + r_runs[h]) for h in heads]
    a_s = [jnp.exp(arg if causal is None else jnp.where(causal, arg, NEG)) for arg in args]
    return zs, nsps, lbs, a_s


def _sb_walk_left(nfull, tk, block, carry, running_sums):
    def alive(state):
        jj, c = state
        ra, rb = running_sums(c)
        return (jj < nfull) & (jnp.max(jnp.maximum(ra, rb)) >= EXP_DEAD)

    def step(state):
        jj, c = state
        return jj + 1, block(pl.multiple_of((nfull - 1 - jj) * tk, tk), 0, c, False)

    return lax.while_loop(alive, step, (jnp.int32(0), carry))[1]


def _sb_fwd(sq, sk, sv, projm, *, tq, tk):
    S = sq.shape[0]
    npair = SB_HEADS // 2

    def body(q_ref, k_ref, v_ref, sg_ref, o_ref, sm_ref):
        qi = pl.program_id(1)
        lane = _iota((1, LANES), 1)
        ma = lane < HEAD_DIM
        qh = _pair_masks(q_ref[...])
        tmat2 = _suffix_matrix(tk, inclusive=False)
        nfull = (qi * tq) // tk

        def block(k0, r0, carry, masked):
            nr = tq - r0
            kb = k_ref[pl.ds(k0, tk), :]
            vb = v_ref[pl.ds(k0, tk), :]
            causal = (k0 + _iota((nr, tk), 1)) < (qi * tq + r0 + _iota((nr, tk), 0)) if masked else None
            _, _, lbs, a_s = _sb_scores([q[r0:] for q in qh], kb, causal, tmat2, [carry[h][0][r0:] for h in range(2)])
            pv = _dot(jnp.concatenate([a.astype(BF16) for a in a_s], axis=0), vb)
            return tuple((_put_rows(carry[h][0], carry[h][0][r0:] + jnp.sum(lbs[h], axis=1, keepdims=True), r0),
                          _put_rows(carry[h][1], carry[h][1][r0:] + pv[h * nr:(h + 1) * nr], r0)) for h in range(2))

        carry = tuple((jnp.zeros((tq, 1), F32), jnp.zeros((tq, LANES), F32)) for _ in range(2))
        for off, size in reversed(_diag_tiles(tq)):
            assert size == tk
            carry = block(pl.multiple_of(qi * tq + off, tk), off, carry, True)
        (_, acca), (_, accb) = _sb_walk_left(nfull, tk, block, carry, lambda c: (c[0][0], c[1][0]))
        o = jnp.where(ma, acca, accb)
        o_ref[...] = o
        sg = sg_ref[...]
        sm_ref[...] = (o * (sg * _sigmoid(sg))).astype(BF16)

    qblk = pl.BlockSpec((tq, LANES), lambda p, i: (i, p))
    kvblk = pl.BlockSpec((S, LANES), lambda p, i: (0, p))
    return pl.pallas_call(
        body, name="sb_fwd", grid=(npair, S // tq),
        in_specs=[qblk, kvblk, kvblk, pl.BlockSpec((tq, LANES), lambda p, i: (i, C_SG // LANES + p))],
        out_specs=[qblk, qblk],
        out_shape=[jax.ShapeDtypeStruct((S, SB_W), F32), jax.ShapeDtypeStruct((S, SB_W), BF16)],
        compiler_params=_cparams(dimension_semantics=("arbitrary", "arbitrary")),
    )(sq, sk, sv, projm)


def _outproj(x, fm, pm, sm, w_out, layer, *, tm, target=None):
    S, D = x.shape

    def body(x_ref, fm_ref, pm_ref, sm_ref, w_ref, *refs):
        y = x_ref[...] + _dot(fm_ref[...], w_ref[0:FOX_W, :])
        y = y + _dot(pm_ref[...], w_ref[FOX_W:FOX_W + POOL_W, :])
        y = y + _dot(sm_ref[...], w_ref[FOX_W + POOL_W:D_MIX, :])
        if target is None:
            refs[0][...] = y
            return
        t_ref, dy_ref, sq_ref = refs

        @pl.when(pl.program_id(0) == 0)
        def _():
            sq_ref[...] = jnp.zeros_like(sq_ref)

        d = y - t_ref[...]
        dy_ref[...] = d * (1.0 / D)
        sq_ref[...] += jnp.sum(d * d, axis=0, keepdims=True)

    row = lambda w: pl.BlockSpec((tm, w), lambda i: (i, 0))
    in_specs = [row(D), row(FOX_W), row(POOL_W), row(SB_W), pl.BlockSpec((None, D_MIX, D), lambda i: (layer, 0, 0))]
    kw = dict(name="outproj", grid=(S // tm,), compiler_params=_cparams(dimension_semantics=("arbitrary",)))
    if target is None:
        return pl.pallas_call(body, in_specs=in_specs, out_specs=row(D), out_shape=jax.ShapeDtypeStruct((S, D), F32), **kw)(x, fm, pm, sm, w_out)
    return pl.pallas_call(
        body, in_specs=in_specs + [row(D)], out_specs=[row(D), pl.BlockSpec((1, D), lambda i: (0, 0))],
        out_shape=[jax.ShapeDtypeStruct((S, D), F32), jax.ShapeDtypeStruct((1, D), F32)], **kw)(x, fm, pm, sm, w_out, target)


def _outproj_bwd(dy, fm, pm, sm, w_out, layer, stacks, *, tm):
    S, D = dy.shape

    def body(dy_ref, fm_ref, pm_ref, sm_ref, w_ref, dm_ref, dw_ref):
        @pl.when(pl.program_id(0) == 0)
        def _():
            dw_ref[...] = jnp.zeros_like(dw_ref)

        dyb = dy_ref[...].astype(BF16)
        dm_ref[...] = _dot_nt(dyb, w_ref[...])
        dw_ref[0:FOX_W, :] += _dot_tn(fm_ref[...], dyb)
        dw_ref[FOX_W:FOX_W + POOL_W, :] += _dot_tn(pm_ref[...], dyb)
        dw_ref[FOX_W + POOL_W:D_MIX, :] += _dot_tn(sm_ref[...], dyb)

    row = lambda w: pl.BlockSpec((tm, w), lambda i: (i, 0))
    wspec = pl.BlockSpec((None, D_MIX, D), lambda i: (layer, 0, 0))
    return _stack_call(
        body, "outproj_bwd", (S // tm,), [row(D), row(FOX_W), row(POOL_W), row(SB_W), wspec], (dy, fm, pm, sm, w_out),
        [pl.BlockSpec((None, D_MIX, D), lambda i: (layer, 0, 0))], [(D_MIX, D)], stacks,
        plain_specs=[row(D_MIX)], plain_shapes=[jax.ShapeDtypeStruct((S, D_MIX), F32)],
        compiler_params=_cparams(dimension_semantics=("arbitrary",)))


def _fox_bwd(qn, ka, kb, v, o, lse, dmix, projm, qkb, *, tq, tk):
    S = qn.shape[0]
    npair = FOX_HEADS // 2

    def body(q_ref, ka_ref, kb_ref, v_ref, o_ref, lse_ref, dm_ref, fg_ref, qkb_ref,
             dq_ref, dk_ref, dv_ref, dfg_ref, dct_ref, dcr_ref):
        qi = pl.program_id(1)

        @pl.when(qi == 0)
        def _():
            dk_ref[...] = jnp.zeros_like(dk_ref)
            dv_ref[...] = jnp.zeros_like(dv_ref)
            dct_ref[...] = jnp.zeros_like(dct_ref)

        lane = _iota((1, LANES), 1)
        ma = lane < HEAD_DIM
        qh = _pair_masks(q_ref[...])
        qaug = _aug_queries(q_ref[...])
        k_refs = (ka_ref, kb_ref)
        lsev = lse_ref[...]
        lse = (_lane_pick(lsev, lane, 0), _lane_pick(lsev, lane, HEAD_DIM))
        fg = fg_ref[...]
        silu, dsilu = _silu_pair(fg)
        dm = dm_ref[...]
        ov = o_ref[...]
        do = dm * silu
        dfg_ref[...] = dm * ov * dsilu
        dd = do * ov
        dsum = (jnp.sum(jnp.where(ma, dd, 0.0), axis=1, keepdims=True), jnp.sum(jnp.where(ma, 0.0, dd), axis=1, keepdims=True))
        doh = _pair_masks(do.astype(BF16))

        def block(k0, tkl, r0, carry, masked, heads=(0, 1)):
            vb = v_ref[pl.ds(k0, tkl), :]
            if masked:
                mask = (k0 + _iota((tq - r0, tkl), 1)) <= (qi * tq + r0 + _iota((tq - r0, tkl), 0))
            kaugs = {h: k_refs[h][pl.ds(k0, tkl), :] for h in heads}
            scores = {h: _dot_nt(qaug[h][r0:], kaugs[h]) for h in heads}
            dps = {h: _dot_nt(doh[h][r0:], vb) for h in heads}
            ps, dss = [], []
            rows = [carry[1], carry[2]]
            for h in heads:
                s = jnp.where(mask, scores[h], NEG) if masked else scores[h]
                p = jnp.exp(s - lse[h][r0:])
                dsf = p * (dps[h] - dsum[h][r0:])
                dct_ref[0, h:h + 1, pl.ds(k0, tkl)] -= jnp.sum(dsf, axis=0, keepdims=True)
                rows[h] = _put_rows(carry[1 + h], carry[1 + h][r0:] + jnp.sum(dsf, axis=1, keepdims=True), r0)
                ps.append(p.astype(BF16))
                dss.append(dsf.astype(BF16))
            dv_ref[pl.ds(k0, tkl), :] += _dot_tn(jnp.concatenate(ps, axis=0), jnp.concatenate([doh[h][r0:] for h in heads], axis=0))
            dk_ref[pl.ds(k0, tkl), :] += _dot_tn(jnp.concatenate(dss, axis=0), jnp.concatenate([qh[h][r0:] for h in heads], axis=0))
            kh = jnp.concatenate([_pair_masks(kaugs[h])[h] for h in heads], axis=0)
            dq = _put_rows(carry[0], carry[0][r0:] + _dot(jnp.concatenate(dss, axis=1), kh), r0)
            return (dq, rows[0], rows[1])

        zcol = jnp.zeros((tq, 1), F32)
        carry = (jnp.zeros((tq, LANES), F32), zcol, zcol)
        for off, size in _diag_tiles(tq):
            carry = block(pl.multiple_of(qi * tq + off, size), size, off, carry, True)
        floors = (jnp.min(lse[0]), jnp.min(lse[1]))
        dq, rowa, rowb = _fox_walk_left((qi * tq) // tk, tk, block, carry, k_refs, jnp.max(qkb_ref[...]), lambda c: floors)
        dq_ref[...] = dq * QK_SCALE
        dcr_ref[0] = jnp.where(ma, rowa, rowb)

    qblk = pl.BlockSpec((tq, LANES), lambda p, i: (i, p))
    kvblk = pl.BlockSpec((S, LANES), lambda p, i: (0, p))
    f32out = jax.ShapeDtypeStruct((S, FOX_W), F32)
    ctblk = pl.BlockSpec((1, FF_STRIDE, S), lambda p, i: (p, 0, 0))
    return pl.pallas_call(
        body, name="fox_bwd", grid=(npair, S // tq),
        in_specs=[qblk, kvblk, kvblk, kvblk, qblk, qblk, qblk,
                  pl.BlockSpec((tq, LANES), lambda p, i: (i, C_FG // LANES + p)),
                  pl.BlockSpec((1, LANES), lambda p, i: (0, 0))],
        out_specs=[qblk, kvblk, kvblk, qblk, ctblk, pl.BlockSpec((1, tq, LANES), lambda p, i: (p, i, 0))],
        out_shape=[f32out, f32out, f32out, f32out, jax.ShapeDtypeStruct((npair, FF_STRIDE, S), F32),
                   jax.ShapeDtypeStruct((npair, S, LANES), F32)],
        compiler_params=_cparams(dimension_semantics=("arbitrary", "arbitrary")),
    )(qn, ka, kb, v, o, lse, dmix, projm, qkb)


def _sb_bwd(sq, sk, sv, o, dmix, projm, *, tq, tk):
    S = sq.shape[0]
    npair = SB_HEADS // 2
    mix0 = (FOX_W + POOL_W) // LANES

    def body(q_ref, k_ref, v_ref, o_ref, dm_ref, sg_ref, dq_ref, dk_ref, dv_ref, dsg_ref):
        qi = pl.program_id(1)

        @pl.when(qi == 0)
        def _():
            dk_ref[...] = jnp.zeros_like(dk_ref)
            dv_ref[...] = jnp.zeros_like(dv_ref)

        lane = _iota((1, LANES), 1)
        ma = lane < HEAD_DIM
        qh = _pair_masks(q_ref[...])
        sg = sg_ref[...]
        silu, dsilu = _silu_pair(sg)
        dm = dm_ref[...]
        ov = o_ref[...]
        do = dm * silu
        dsg_ref[...] = dm * ov * dsilu
        dob = do.astype(BF16)
        dd = dob.astype(F32) * ov
        dsum = (jnp.sum(jnp.where(ma, dd, 0.0), axis=1, keepdims=True), jnp.sum(jnp.where(ma, 0.0, dd), axis=1, keepdims=True))
        doh = _pair_masks(dob)
        tmat2 = _suffix_matrix(tk, inclusive=False)
        tmat2_inc = _suffix_matrix(tk, inclusive=True)
        nfull = (qi * tq) // tk

        def block(k0, r0, carry, masked):
            nr = tq - r0
            kb = k_ref[pl.ds(k0, tk), :]
            vb = v_ref[pl.ds(k0, tk), :]
            kh = _pair_masks(kb)
            causal = (k0 + _iota((nr, tk), 1)) < (qi * tq + r0 + _iota((nr, tk), 0)) if masked else None
            heads = range(2)
            qs = [q[r0:] for q in qh]
            dos = [d[r0:] for d in doh]
            das = [_dot_nt(dos[h], vb) for h in heads]
            zs, nsps, lbs, a_s = _sb_scores(qs, kb, causal, tmat2, [carry[h][0][r0:] for h in heads])
            abs_ = [a.astype(BF16) for a in a_s]
            us = [abs_[h].astype(F32) * das[h] for h in heads]
            uins = [_suffix_sums(u, tmat2_inc) for u in us]
            dzs = []
            for h in heads:
                cum_u = dsum[h][r0:] - (uins[h] + carry[h][1][r0:])
                dz = us[h] * jnp.exp(nsps[h]) - jnp.exp(zs[h] + nsps[h]) * cum_u
                if masked:
                    dz = jnp.where(causal, dz, 0.0)
                dzs.append(dz.astype(BF16))
            dv_ref[pl.ds(k0, tk), :] += _dot_tn(jnp.concatenate(abs_, axis=0), jnp.concatenate(dos, axis=0))
            dk_ref[pl.ds(k0, tk), :] += _dot_tn(jnp.concatenate(dzs, axis=0), jnp.concatenate(qs, axis=0))
            dq = _put_rows(carry[2], carry[2][r0:] + _dot(jnp.concatenate(dzs, axis=1), jnp.concatenate(kh, axis=0)), r0)
            new = [(_put_rows(carry[h][0], carry[h][0][r0:] + jnp.sum(lbs[h], axis=1, keepdims=True), r0),
                    _put_rows(carry[h][1], carry[h][1][r0:] + jnp.sum(us[h], axis=1, keepdims=True), r0)) for h in heads]
            return (new[0], new[1], dq)

        zcol = jnp.zeros((tq, 1), F32)
        carry = ((zcol, zcol), (zcol, zcol), jnp.zeros((tq, LANES), F32))
        for off, size in reversed(_diag_tiles(tq)):
            assert size == tk
            carry = block(pl.multiple_of(qi * tq + off, tk), off, carry, True)
        dq = _sb_walk_left(nfull, tk, block, carry, lambda c: (c[0][0], c[1][0]))[2]
        dq_ref[...] = dq * QK_SCALE

    qblk = pl.BlockSpec((tq, LANES), lambda p, i: (i, p))
    kvblk = pl.BlockSpec((S, LANES), lambda p, i: (0, p))
    f32out = jax.ShapeDtypeStruct((S, SB_W), F32)
    return pl.pallas_call(
        body, name="sb_bwd", grid=(npair, S // tq),
        in_specs=[qblk, kvblk, kvblk, qblk,
                  pl.BlockSpec((tq, LANES), lambda p, i: (i, mix0 + p)),
                  pl.BlockSpec((tq, LANES), lambda p, i: (i, C_SG // LANES + p))],
        out_specs=[qblk, kvblk, kvblk, qblk],
        out_shape=[f32out, f32out, f32out, f32out],
        compiler_params=_cparams(dimension_semantics=("arbitrary", "arbitrary")),
    )(sq, sk, sv, o, dmix, projm)


def _prep_bwd(projm, ffo, dqn, dkn, dct, dcr, dv, dfg, dsq, dsk, dsv, dsg, dmix, pooled, yp, qg, kg, bfp, wpd, ps, *, ts):
    S = projm.shape[0]
    nb = S // ts
    hb = ts // POOL_HALO
    npair = FOX_HEADS // 2
    last_halo = S // POOL_HALO - 1

    def body(fq_ref, fk_ref, pp_ref, pph_ref, ff_ref,
             dqn_ref, dkn_ref, dct_ref, dcr_ref, dv_ref, dfg_ref, dsq_ref, dsk_ref, dsv_ref, dsg_ref,
             dmp_ref, dmh_ref, pooled_ref, yp_ref, qg_ref, kg_ref, bf_ref, wpd_ref, ps_ref,
             dp_ref, dqg_ref, dkg_ref, dbf_ref, dwp_ref, dps_ref,
             carry_ref, dl_ref, buf_ref, dct_s):
        i = pl.program_id(0)
        blk = nb - 1 - i

        @pl.when(i == 0)
        def _():
            carry_ref[...] = jnp.zeros_like(carry_ref)
            dqg_ref[...] = jnp.zeros_like(dqg_ref)
            dkg_ref[...] = jnp.zeros_like(dkg_ref)
            dbf_ref[...] = jnp.zeros_like(dbf_ref)
            dwp_ref[...] = jnp.zeros_like(dwp_ref)
            dps_ref[...] = jnp.zeros_like(dps_ref)

        bd = _head_blockdiag()
        for raw_ref, g_ref, dn, dg_ref, col in ((fq_ref, qg_ref, dqn_ref[...], dqg_ref, C_FQ), (fk_ref, kg_ref, dkn_ref[...], dkg_ref, C_FK)):
            q = raw_ref[...]
            rstd = lax.rsqrt(_group_sum(q * q, bd) * (1.0 / HEAD_DIM) + EPS)
            xhat = q * rstd
            dg_ref[...] += jnp.sum(dn * xhat, axis=0, keepdims=True)
            dyg = dn * g_ref[...]
            mean = _group_sum(dyg * xhat, bd) * (1.0 / HEAD_DIM)
            dp_ref[:, col:col + FOX_W] = (rstd * (dyg - xhat * mean)).astype(BF16)
        dp_ref[:, C_FV:C_FV + FOX_W] = dv_ref[...].astype(BF16)
        dp_ref[:, C_FG:C_FG + FOX_W] = dfg_ref[...].astype(BF16)
        dp_ref[:, C_SQ:C_SQ + SB_W] = dsq_ref[...].astype(BF16)
        dp_ref[:, C_SK:C_SK + SB_W] = dsk_ref[...].astype(BF16)
        dp_ref[:, C_SV:C_SV + SB_W] = dsv_ref[...].astype(BF16)
        dp_ref[:, C_SG:C_SG + SB_W] = dsg_ref[...].astype(BF16)

        dct_s[...] = jnp.zeros_like(dct_s)
        for p in range(npair):
            dct_s[FF_STRIDE * p:FF_STRIDE * (p + 1), :] = dct_ref[p]
        dc = dct_s[...].T
        lane = _iota((1, LANES), 1)
        for p in range(npair):
            dcr = dcr_ref[p]
            dc = dc + jnp.where(lane == FF_STRIDE * p, _lane_pick(dcr, lane, 0), 0.0)
            dc = dc + jnp.where(lane == FF_STRIDE * p + 1, _lane_pick(dcr, lane, HEAD_DIM), 0.0)
        triu = _ones_where(_iota((ts, ts), 1) >= _iota((ts, ts), 0))
        dlf = _dot_exact_lhs(triu, dc) + carry_ref[...]
        dl_ref[...] = dlf
        carry_ref[...] = dl_ref[0:1, :]
        z = ff_ref[...] + bf_ref[...]
        dff = dlf * (1.0 / (1.0 + jnp.exp(z)))
        dbf_ref[...] += jnp.sum(dff, axis=0, keepdims=True)
        dp_ref[:, PM:PW] = dff.astype(BF16)

        psv = ps_ref[...]
        wpdv = wpd_ref[...]
        lane_group = _iota((1, POOL_W), 1) >> HEAD_SHIFT
        wlen = _pool_group_select(lane_group, [float(w) for w in POOL_WINDOWS])
        pg = pp_ref[:, POOL_W:2 * POOL_W]
        silu, dsilu = _silu_pair(pg)
        dmp = dmp_ref[...]
        ypv = yp_ref[...]
        dp_ref[:, C_PG:C_PG + POOL_W] = (dmp * (ypv * psv) * dsilu).astype(BF16)
        dps_ref[...] += jnp.sum(dmp * silu * ypv, axis=0, keepdims=True)
        dyp = (dmp * psv * silu).astype(BF16)
        dwp_ref[...] += _dot_tn(pooled_ref[...], dyp)
        dpooled = _dot_nt(dyp, wpdv)
        pgh = pph_ref[:, POOL_W:2 * POOL_W]
        dyph = (dmh_ref[...] * psv * (pgh * _sigmoid(pgh))).astype(BF16)
        dpooled_h = jnp.where(blk < nb - 1, _dot_nt(dyph, wpdv), 0.0)
        tpos = (blk * ts + _iota((ts, 1), 0) + 1).astype(F32)
        ev = dpooled / jnp.minimum(tpos, wlen)
        buf_ref[0:ts, :] = ev
        buf_ref[ts:ts + POOL_HALO, :] = dpooled_h / wlen
        acc = ev
        snaps = []
        for d in range(1, POOL_HALO):
            acc = acc + buf_ref[pl.ds(d, ts), :]
            if d + 1 in POOL_WINDOWS:
                snaps.append(acc)
        dp_ref[:, C_PX:C_PX + POOL_W] = (_pool_group_select(lane_group, snaps) - dpooled).astype(BF16)

    rblk = lambda w, c: pl.BlockSpec((ts, w), lambda i: (nb - 1 - i, c))
    full = lambda a: pl.BlockSpec(a.shape, lambda i: (0,) * a.ndim)
    halo = lambda w, c: pl.BlockSpec((POOL_HALO, w), lambda i: (jnp.minimum((nb - i) * hb, last_halo), c))
    acc_spec = lambda r, w: pl.BlockSpec((r, w), lambda i: (0, 0))
    return pl.pallas_call(
        body, name="prep_bwd", grid=(nb,),
        in_specs=[rblk(FOX_W, C_FQ // FOX_W), rblk(FOX_W, C_FK // FOX_W), rblk(2 * POOL_W, C_PX // (2 * POOL_W)),
                  halo(2 * POOL_W, C_PX // (2 * POOL_W)), rblk(LANES, 0),
                  rblk(FOX_W, 0), rblk(FOX_W, 0), pl.BlockSpec((npair, FF_STRIDE, ts), lambda i: (0, 0, nb - 1 - i)),
                  pl.BlockSpec((npair, ts, LANES), lambda i: (0, nb - 1 - i, 0)), rblk(FOX_W, 0), rblk(FOX_W, 0),
                  rblk(SB_W, 0), rblk(SB_W, 0), rblk(SB_W, 0), rblk(SB_W, 0),
                  rblk(POOL_W, FOX_W // POOL_W), halo(POOL_W, FOX_W // POOL_W), rblk(POOL_W, 0), rblk(POOL_W, 0),
                  full(qg), full(kg), full(bfp), full(wpd), full(ps)],
        out_specs=[rblk(PW, 0), acc_spec(1, FOX_W), acc_spec(1, FOX_W), acc_spec(1, LANES), acc_spec(POOL_W, POOL_W), acc_spec(1, POOL_W)],
        out_shape=[jax.ShapeDtypeStruct((S, PW), BF16), jax.ShapeDtypeStruct((1, FOX_W), F32), jax.ShapeDtypeStruct((1, FOX_W), F32),
                   jax.ShapeDtypeStruct((1, LANES), F32), jax.ShapeDtypeStruct((POOL_W, POOL_W), F32), jax.ShapeDtypeStruct((1, POOL_W), F32)],
        scratch_shapes=[pltpu.VMEM((1, LANES), F32), pltpu.VMEM((ts, LANES), F32), pltpu.VMEM((ts + POOL_HALO, POOL_W), F32),
                        pltpu.VMEM((LANES, ts), F32)],
        compiler_params=_cparams(dimension_semantics=("arbitrary",)),
    )(projm, projm, projm, projm, ffo, dqn, dkn, dct, dcr, dv, dfg, dsq, dsk, dsv, dsg, dmix, dmix, pooled, yp, qg, kg, bfp, wpd, ps)


def _stack_call(body, name, grid, in_specs, operands, slot_specs, slot_shapes, stacks, plain_specs=(), plain_shapes=(), **kw):
    out_specs = list(plain_specs) + list(slot_specs)
    out_shape = list(plain_shapes) + [jax.ShapeDtypeStruct((DEPTH,) + s, F32) for s in slot_shapes]
    if stacks is None:
        return pl.pallas_call(body, name=name, grid=grid, in_specs=in_specs, out_specs=out_specs, out_shape=out_shape, **kw)(*operands)
    n = len(operands)

    def aliased_body(*refs):
        body(*refs[:n], *refs[n + len(stacks):])

    return pl.pallas_call(
        aliased_body, name=name, grid=grid, in_specs=list(in_specs) + [pl.BlockSpec(memory_space=pl.ANY)] * len(stacks),
        out_specs=out_specs, out_shape=out_shape,
        input_output_aliases={n + k: len(plain_specs) + k for k in range(len(stacks))}, **kw)(*operands, *stacks)


def _inproj_dw(h, dproj, layer, stacks, *, ts, tn):
    S, D = h.shape
    nj = PM // tn

    def body(h_ref, dp_ref, dpf_ref, dw_ref, dwf_ref):
        s = pl.program_id(1)

        @pl.when(s == 0)
        def _():
            dw_ref[...] = jnp.zeros_like(dw_ref)

        @pl.when((s == 0) & (pl.program_id(0) == 0))
        def _():
            dwf_ref[...] = jnp.zeros_like(dwf_ref)

        hv = h_ref[...]
        dw_ref[...] += _dot_tn(dp_ref[...], hv)

        @pl.when(pl.program_id(0) == 0)
        def _():
            dwf_ref[...] += _dot_tn(dpf_ref[...], hv)

    return _stack_call(
        body, "inproj_dw", (nj, S // ts),
        [pl.BlockSpec((ts, D), lambda j, s: (s, 0)),
         pl.BlockSpec((ts, tn), lambda j, s: (s, j)),
         pl.BlockSpec((ts, LANES), lambda j, s: (s, PM // LANES))],
        (h, dproj, dproj),
        [pl.BlockSpec((None, tn, D), lambda j, s: (layer, j, 0)), pl.BlockSpec((None, LANES, D), lambda j, s: (layer, 0, 0))],
        [(PM, D), (LANES, D)], stacks,
        compiler_params=_cparams(dimension_semantics=("arbitrary", "arbitrary")))


def _inproj_dx(dproj, wt_all, layer, x, g, dy, *, tm):
    S, D = x.shape

    def body(dp_ref, w_ref, x_ref, g_ref, dy_ref, dx_ref, dg_ref):
        @pl.when(pl.program_id(0) == 0)
        def _():
            dg_ref[...] = jnp.zeros_like(dg_ref)

        dh = _dot(dp_ref[...], w_ref[...])
        xf = x_ref[...]
        rstd = lax.rsqrt(jnp.mean(xf * xf, axis=-1, keepdims=True) + EPS)
        xhat = xf * rstd
        dg_ref[...] += jnp.sum(dh * xhat, axis=0, keepdims=True)
        dyg = dh * g_ref[...]
        mean = jnp.mean(dyg * xhat, axis=-1, keepdims=True)
        dx_ref[...] = rstd * (dyg - xhat * mean) + dy_ref[...]

    row = lambda w: pl.BlockSpec((tm, w), lambda i: (i, 0))
    return pl.pallas_call(
        body, name="inproj_dx", grid=(S // tm,),
        in_specs=[row(PW), pl.BlockSpec((None, PW, D), lambda i: (layer, 0, 0)), row(D), pl.BlockSpec((1, D), lambda i: (0, 0)), row(D)],
        out_specs=[row(D), pl.BlockSpec((1, D), lambda i: (0, 0))],
        out_shape=[jax.ShapeDtypeStruct((S, D), F32), jax.ShapeDtypeStruct((1, D), F32)],
        compiler_params=_cparams(dimension_semantics=("arbitrary",)),
    )(dproj, wt_all, x, g, dy)


def _adam_update(w, g, m, v):
    nm = ADAM_B1 * m + (1.0 - ADAM_B1) * g
    nv = ADAM_B2 * v + (1.0 - ADAM_B2) * (g * g)
    m_hat = nm / (1.0 - ADAM_B1 ** ADAM_STEP)
    v_hat = nv / (1.0 - ADAM_B2 ** ADAM_STEP)
    return -ADAM_LR * (m_hat / (jnp.sqrt(v_hat) + ADAM_EPS) + ADAM_WD * w), nm, nv


def _adamw(w, g, m, v):
    L, R, C = w.shape
    tr = R if R <= 512 else 256

    def body(w_ref, g_ref, m_ref, v_ref, d_ref, nm_ref, nv_ref):
        d_ref[...], nm_ref[...], nv_ref[...] = _adam_update(w_ref[...], g_ref[...], m_ref[...], v_ref[...])

    spec = pl.BlockSpec((1, tr, C), lambda l, i: (l, i, 0))
    shp = jax.ShapeDtypeStruct((L, R, C), F32)
    return pl.pallas_call(
        body, name="adamw", grid=(L, R // tr), in_specs=[spec] * 4, out_specs=[spec] * 3, out_shape=[shp] * 3,
        compiler_params=_cparams(dimension_semantics=("arbitrary", "arbitrary")),
    )(w, g, m, v)


def _adamw_nd(w, g, m, v):
    shape = w.shape
    view = (1,) + shape if w.ndim == 2 else (shape[0], -1, shape[-1])
    outs = _adamw(w.reshape(view), g.reshape(view), m.reshape(view), v.reshape(view))
    return tuple(o.reshape(shape) for o in outs)


FLIP_C = (0, 0, 1)
FLIP_X = (1, 0, 0)
FLIP_Y = (0, 1, 0)
FLIP_XY = (1, 1, 0)
MESH = pl.DeviceIdType.MESH


def _peer(flip):
    me = (lax.axis_index("x"), lax.axis_index("y"), lax.axis_index("c"))
    return tuple(1 - a if f else a for a, f in zip(me, flip))


def _exchange(name, arrays, flips):
    n = len(arrays)

    def body(*refs):
        srcs, dsts = refs[:n], refs[n:2 * n]
        send_sems, recv_sems = refs[2 * n:]
        copies = [pltpu.make_async_remote_copy(src_ref=srcs[k], dst_ref=dsts[k], send_sem=send_sems.at[k], recv_sem=recv_sems.at[k],
                                               device_id=_peer(flips[k]), device_id_type=MESH) for k in range(n)]
        for cp in copies:
            cp.start()
        for cp in copies:
            cp.wait()

    anyspec = pl.BlockSpec(memory_space=pl.ANY)
    return pl.pallas_call(
        body, name=name, in_specs=[anyspec] * n, out_specs=[anyspec] * n,
        out_shape=[jax.ShapeDtypeStruct(a.shape, a.dtype) for a in arrays],
        scratch_shapes=[pltpu.SemaphoreType.DMA((n,)), pltpu.SemaphoreType.DMA((n,))],
    )(*arrays)


def _all_reduce_small(x):
    flips = (FLIP_C, FLIP_Y, FLIP_X)
    n = len(flips)

    def body(x_ref, o_ref, sum_ref, buf_ref, send_sems, recv_sems):
        src = x_ref
        for k, flip in enumerate(flips):
            cp = pltpu.make_async_remote_copy(src_ref=src, dst_ref=buf_ref.at[k], send_sem=send_sems.at[k], recv_sem=recv_sems.at[k],
                                              device_id=_peer(flip), device_id_type=MESH)
            cp.start()
            cp.wait()
            dst = o_ref if k == n - 1 else sum_ref.at[k]
            dst[...] = src[...] + buf_ref[k]
            src = dst

    vspec = pl.BlockSpec(memory_space=pltpu.VMEM)
    return pl.pallas_call(
        body, name="ar_small", in_specs=[vspec], out_specs=vspec, out_shape=jax.ShapeDtypeStruct(x.shape, x.dtype),
        scratch_shapes=[pltpu.VMEM((n - 1,) + x.shape, x.dtype), pltpu.VMEM((n,) + x.shape, x.dtype),
                        pltpu.SemaphoreType.DMA((n,)), pltpu.SemaphoreType.DMA((n,))],
    )(x)


def _chip_index():
    return 2 * lax.axis_index("x") + lax.axis_index("y")


def _gather_weights(w_in_t, w_out):
    wi = w_in_t.astype(BF16)
    wo = jnp.swapaxes(w_out, 0, 1).astype(BF16)
    halves = (wi.shape[0] // 2, wo.shape[0] // 2)
    ARR = 2
    TO_X, TO_Y, ON_Y, ON_X, SIB_X, SIB_Y, SIB_D0, SIB_D1, OWN = [ARR * k for k in range(9)]
    n_sems = ARR * 9

    def body(wi_ref, wo_ref, gi_ref, go_ref, send_sems, recv_sems):
        c = lax.axis_index("c")
        j = _chip_index()
        srcs = (wi_ref, wo_ref)
        dsts = (gi_ref, go_ref)
        def cuts(core):
            return [(pl.ds(h * core, h), pl.ds(h * core, h // 2), pl.ds(h * core + h // 2, h - h // 2)) for h in halves]
        mine, theirs = cuts(c), cuts(1 - c)
        HALF, Q0, Q1 = 0, 1, 2

        def copy(idx, src, dst, flip):
            return pltpu.make_async_remote_copy(src_ref=src, dst_ref=dst, send_sem=send_sems.at[idx], recv_sem=recv_sems.at[idx],
                                                device_id=_peer(flip), device_id_type=MESH)

        def slot(a, shard, cut):
            return dsts[a].at[shard, cut]

        jx, jy, jd = j ^ 2, j ^ 1, j ^ 3
        sends = []

        def start(cp):
            cp.start()
            sends.append(cp)

        for a in range(ARR):
            start(copy(TO_X + a, srcs[a].at[mine[a][HALF]], slot(a, j, mine[a][HALF]), FLIP_X))
            start(copy(TO_Y + a, srcs[a].at[mine[a][HALF]], slot(a, j, mine[a][HALF]), FLIP_Y))
        own = [copy(OWN + a, srcs[a], dsts[a].at[j], FLIP_C) for a in range(ARR)]
        for cp in own:
            cp.start()
        for a in range(ARR):
            copy(TO_X + a, slot(a, jx, mine[a][HALF]), slot(a, jx, mine[a][HALF]), FLIP_X).wait_recv()
            start(copy(ON_Y + a, slot(a, jx, mine[a][Q0]), slot(a, jx, mine[a][Q0]), FLIP_Y))
            start(copy(SIB_X + a, slot(a, jx, mine[a][HALF]), slot(a, jx, mine[a][HALF]), FLIP_C))
        for a in range(ARR):
            copy(TO_Y + a, slot(a, jy, mine[a][HALF]), slot(a, jy, mine[a][HALF]), FLIP_Y).wait_recv()
            start(copy(ON_X + a, slot(a, jy, mine[a][Q1]), slot(a, jy, mine[a][Q1]), FLIP_X))
            start(copy(SIB_Y + a, slot(a, jy, mine[a][HALF]), slot(a, jy, mine[a][HALF]), FLIP_C))
        for a in range(ARR):
            copy(ON_Y + a, slot(a, jd, mine[a][Q0]), slot(a, jd, mine[a][Q0]), FLIP_Y).wait_recv()
            start(copy(SIB_D0 + a, slot(a, jd, mine[a][Q0]), slot(a, jd, mine[a][Q0]), FLIP_C))
        for a in range(ARR):
            copy(ON_X + a, slot(a, jd, mine[a][Q1]), slot(a, jd, mine[a][Q1]), FLIP_X).wait_recv()
            start(copy(SIB_D1 + a, slot(a, jd, mine[a][Q1]), slot(a, jd, mine[a][Q1]), FLIP_C))
        for a in range(ARR):
            for idx, shard, cut in ((SIB_X, jx, HALF), (SIB_Y, jy, HALF), (SIB_D0, jd, Q0), (SIB_D1, jd, Q1)):
                copy(idx + a, slot(a, shard, theirs[a][cut]), slot(a, shard, theirs[a][cut]), FLIP_C).wait_recv()
        for cp in own:
            cp.wait()
        for cp in sends:
            cp.wait_send()

    anyspec = pl.BlockSpec(memory_space=pl.ANY)
    gi, go = pl.pallas_call(
        body, name="gather_weights", in_specs=[anyspec] * 2, out_specs=[anyspec] * 2,
        out_shape=[jax.ShapeDtypeStruct((4,) + wi.shape, BF16), jax.ShapeDtypeStruct((4,) + wo.shape, BF16)],
        scratch_shapes=[pltpu.SemaphoreType.DMA((n_sems,)), pltpu.SemaphoreType.DMA((n_sems,))],
    )(wi, wo)
    w_in_t_full = gi.reshape((4 * wi.shape[0],) + wi.shape[1:])
    w_out_full = jnp.swapaxes(go.reshape((4 * wo.shape[0],) + wo.shape[1:]), 0, 1)
    return w_in_t_full, w_out_full


def _to_aligned(w_t):
    _, L, D = w_t.shape
    npair = FOX_HEADS // 2
    ff = w_t[ORIG_FF:ORIG_REST].reshape(npair, 2, L, D)
    ff = jnp.pad(ff, ((0, 0), (0, FF_STRIDE - 2), (0, 0), (0, 0))).reshape(npair * FF_STRIDE, L, D)
    ff = jnp.pad(ff, ((0, LANES - npair * FF_STRIDE), (0, 0), (0, 0)))
    return jnp.swapaxes(jnp.concatenate([w_t[:ORIG_FOX], w_t[ORIG_REST:], ff], axis=0), 0, 1)


def _from_aligned(dw_t):
    n, _, D = dw_t.shape
    npair = FOX_HEADS // 2
    ff = dw_t[:, PM:PM + npair * FF_STRIDE].reshape(n, npair, FF_STRIDE, D)[:, :, :2].reshape(n, FOX_HEADS, D)
    return jnp.swapaxes(jnp.concatenate([dw_t[:, :ORIG_FOX], ff, dw_t[:, ORIG_FOX:PM]], axis=1), 0, 1)


RELAY_ROWS = 256


def _rows_first(stack_m, stack_f, got_m, got_f):
    n, _, D = got_m.shape
    npair = FOX_HEADS // 2
    first_late = ORIG_FOX // RELAY_ROWS

    def body(c_ref, m_ref, f_ref, gm_ref, gf_ref, out_ref, buf_ref, ff_ref, sem, ff_sem):
        i = pl.program_id(0)
        for l in range(n):
            buf_ref[:, l, :] = m_ref[l] + gm_ref[l].astype(F32)
        start = pl.multiple_of(i * RELAY_ROWS, FOX_HEADS) + jnp.where(i >= first_late, FOX_HEADS, 0)
        main = pltpu.make_async_copy(buf_ref, out_ref.at[pl.ds(start, RELAY_ROWS)], sem)
        main.start()

        @pl.when(i == 0)
        def _():
            for l in range(n):
                for p in range(npair):
                    rows = slice(FF_STRIDE * p, FF_STRIDE * p + 2)
                    ff_ref[2 * p:2 * p + 2, l, :] = f_ref[l, rows, :] + gf_ref[l, rows, :].astype(F32)
            ff = pltpu.make_async_copy(ff_ref, out_ref.at[pl.ds(ORIG_FF, FOX_HEADS)], ff_sem)
            ff.start()
            ff.wait()

        main.wait()

    grid_spec = pltpu.PrefetchScalarGridSpec(
        num_scalar_prefetch=1, grid=(PM // RELAY_ROWS,),
        in_specs=[pl.BlockSpec((n, RELAY_ROWS, D), lambda i, c: (c[0], i, 0)), pl.BlockSpec((n, LANES, D), lambda i, c: (c[0], 0, 0)),
                  pl.BlockSpec((n, RELAY_ROWS, D), lambda i, c: (0, i, 0)), pl.BlockSpec((n, LANES, D), lambda i, c: (0, 0, 0))],
        out_specs=pl.BlockSpec(memory_space=pl.ANY),
        scratch_shapes=[pltpu.VMEM((RELAY_ROWS, n, D), F32), pltpu.VMEM((FOX_HEADS, n, D), F32),
                        pltpu.SemaphoreType.DMA, pltpu.SemaphoreType.DMA])
    return pl.pallas_call(
        body, name="rs_rows_first", grid_spec=grid_spec, out_shape=jax.ShapeDtypeStruct((D_IN, n, D), F32),
        compiler_params=_cparams(dimension_semantics=("arbitrary",)),
    )(lax.axis_index("c").astype(jnp.int32).reshape(1), stack_m, stack_f, got_m, got_f)


def _half_layers(name, stack, got, also_bf16=True):
    L, R, C = stack.shape
    half = L // 2
    tr = min(256, R)
    c = lax.axis_index("c")
    which = ((1 - c) if got is None else c).astype(jnp.int32).reshape(1)

    def body(c_ref, x_ref, *refs):
        if got is None:
            refs[0][...] = x_ref[...].astype(BF16)
        else:
            acc = x_ref[...] + refs[0][...].astype(F32)
            refs[1][...] = acc
            if also_bf16:
                refs[2][...] = acc.astype(BF16)

    plain = pl.BlockSpec((1, tr, C), lambda l, i, c_ref: (l, i, 0))
    picked = pl.BlockSpec((1, tr, C), lambda l, i, c_ref: (c_ref[0] * half + l, i, 0))
    shp = lambda dt: jax.ShapeDtypeStruct((half, R, C), dt)
    out_shape = [shp(BF16)] if got is None else [shp(F32)] + ([shp(BF16)] if also_bf16 else [])
    grid_spec = pltpu.PrefetchScalarGridSpec(
        num_scalar_prefetch=1, grid=(half, R // tr),
        in_specs=[picked] + ([] if got is None else [plain]), out_specs=[plain] * len(out_shape))
    return pl.pallas_call(
        body, name=name, grid_spec=grid_spec, out_shape=out_shape,
        compiler_params=_cparams(dimension_semantics=("arbitrary", "arbitrary")),
    )(which, stack, *([] if got is None else [got]))


def _reduce_scatter(stack_m, stack_f, stack_o, shard_cols, shard_rows):
    j = _chip_index()
    half = DEPTH // 2
    stacks = (stack_m, stack_f, stack_o)
    give = [_half_layers("rs_give", s, None)[0] for s in stacks]
    got = _exchange("rs_d2d", give, (FLIP_C,) * len(stacks))
    o32, obf = _half_layers("rs_add_chip", stack_o, got[2])
    d_model = stack_m.shape[2]
    in32 = _rows_first(stack_m, stack_f, got[0], got[1]).reshape(4, shard_cols, half, d_model)

    def out_shards(o):
        return jnp.moveaxis(o.reshape(half, 4, shard_rows, o.shape[-1]), 1, 0)

    chip = [(in32, in32.astype(BF16), 0), (out_shards(o32), out_shards(obf), 1)]
    shard = lambda a, idx: lax.dynamic_index_in_dim(a, idx, axis=0, keepdims=False)
    via = []
    for _, bf, axis in chip:
        diag = shard(bf, j ^ 3)
        cut = diag.shape[axis] // 2
        via += [lax.slice_in_dim(diag, 0, cut, axis=axis), lax.slice_in_dim(diag, cut, 2 * cut, axis=axis)]
    handed = _exchange("rs_via", via, (FLIP_X, FLIP_Y) * len(chip))
    sends = []
    for a, (f32_sum, _, axis) in enumerate(chip):
        sends.append(_add_half_along("rs_add_via", f32_sum, handed[2 * a + 1], axis, 1, pick=j ^ 2))
        sends.append(_add_half_along("rs_add_via", f32_sum, handed[2 * a], axis, 0, pick=j ^ 1))
    got = _exchange("rs_ici", sends, (FLIP_X, FLIP_Y) * len(chip))
    mine_in = _add_rows("rs_add_in", chip[0][0], list(got[0:2]), pick=j)
    mine_out = _add_into_half("rs_add_out", shard(chip[1][0], j), list(got[2:4]))
    sib_in, g_out = _share_halves(mine_in, mine_out)
    return (mine_in, sib_in), g_out


def _picked(spec, pick):
    return pl.BlockSpec((None,) + tuple(spec.block_shape), lambda *a: (a[-1][0],) + tuple(spec.index_map(*a[:-1])))


def _add_half_along(name, base, extra, axis, which, pick=None):
    shape = base.shape if pick is None else base.shape[1:]
    lanes = min(ROW_LANE_CHUNK, shape[2])
    assert shape[axis] == 2 * extra.shape[axis]
    blk = tuple(shape[d] // 2 if d == axis else shape[d] for d in range(2)) + (lanes,)

    def body(*refs):
        b_ref, e_ref, o_ref = refs[-3:]
        x = b_ref[...]
        o_ref[...] = jnp.where(pl.program_id(0) == which, x + e_ref[...].astype(F32), x).astype(BF16)

    at = lambda i, k, *_: (i, 0, k) if axis == 0 else (0, i, k)
    bspec, espec = pl.BlockSpec(blk, at), pl.BlockSpec(blk, lambda i, k, *_: (0, 0, k))
    kw = dict(out_shape=jax.ShapeDtypeStruct(shape, BF16), name=name, compiler_params=_cparams(dimension_semantics=("arbitrary", "arbitrary")))
    grid = (2, shape[2] // lanes)
    if pick is None:
        return pl.pallas_call(body, grid=grid, in_specs=[bspec, espec], out_specs=bspec, **kw)(base, extra)
    grid_spec = pltpu.PrefetchScalarGridSpec(num_scalar_prefetch=1, grid=grid, in_specs=[_picked(bspec, pick), espec], out_specs=bspec)
    return pl.pallas_call(body, grid_spec=grid_spec, **kw)(pick.astype(jnp.int32).reshape(1), base, extra)


def _add_rows(name, first, others, pick=None):
    n = len(others)
    shape = first.shape if pick is None else first.shape[1:]

    def body(*refs):
        refs = refs[-(n + 2):]
        acc = refs[0][...]
        for r in refs[1:1 + n]:
            acc = acc + r[...].astype(F32)
        refs[1 + n][...] = acc

    grid, spec = _row_lane_blocks(shape)
    sp = spec(shape[1])
    kw = dict(out_shape=jax.ShapeDtypeStruct(shape, F32), name=name, compiler_params=_cparams(dimension_semantics=("arbitrary", "arbitrary")))
    if pick is None:
        return pl.pallas_call(body, grid=grid, in_specs=[sp] * (1 + n), out_specs=sp, **kw)(first, *others)
    grid_spec = pltpu.PrefetchScalarGridSpec(num_scalar_prefetch=1, grid=grid, in_specs=[_picked(sp, pick)] + [sp] * n, out_specs=sp)
    return pl.pallas_call(body, grid_spec=grid_spec, **kw)(pick.astype(jnp.int32).reshape(1), first, *others)


ROW_LANE_CHUNK = 256


def _row_lane_blocks(shape):
    rows, _, C = shape
    tr = rows // 2 if rows % 2 == 0 and rows > 64 else rows
    lanes = min(ROW_LANE_CHUNK, C)
    return (rows // tr, C // lanes), lambda n_mid: pl.BlockSpec((tr, n_mid, lanes), lambda i, k, *_: (i, 0, k))


def _add_into_half(name, first, others):
    half, rows, C = first.shape
    tr = min(256, rows)
    n = len(others)

    def body(c_ref, *refs):
        acc = refs[0][...]
        for r in refs[1:1 + n]:
            acc = acc + r[...].astype(F32)
        refs[1 + n][...] = acc

    grid_spec = pltpu.PrefetchScalarGridSpec(
        num_scalar_prefetch=1, grid=(half, rows // tr),
        in_specs=[pl.BlockSpec((1, tr, C), lambda l, i, c_ref: (l, i, 0))] * (1 + n),
        out_specs=pl.BlockSpec((1, tr, C), lambda l, i, c_ref: (c_ref[0] * half + l, i, 0)))
    return pl.pallas_call(
        body, name=name, grid_spec=grid_spec, out_shape=jax.ShapeDtypeStruct((2 * half, rows, C), F32),
        compiler_params=_cparams(dimension_semantics=("arbitrary", "arbitrary")),
    )(lax.axis_index("c").astype(jnp.int32).reshape(1), first, *others)


def _share_halves(mine, buf):
    half = DEPTH // 2

    def body(mine_ref, buf_in, sib_ref, buf_ref, send_sems, recv_sems):
        lay = pl.ds(half * lax.axis_index("c"), half)
        copies = [pltpu.make_async_remote_copy(src_ref=src, dst_ref=dst, send_sem=send_sems.at[k], recv_sem=recv_sems.at[k],
                                               device_id=_peer(FLIP_C), device_id_type=MESH)
                  for k, (src, dst) in enumerate(((mine_ref, sib_ref), (buf_ref.at[lay], buf_ref.at[lay])))]
        for cp in copies:
            cp.start()
        for cp in copies:
            cp.wait()

    anyspec = pl.BlockSpec(memory_space=pl.ANY)
    return pl.pallas_call(
        body, name="rs_share", in_specs=[anyspec] * 2, out_specs=[anyspec] * 2,
        out_shape=[jax.ShapeDtypeStruct(mine.shape, mine.dtype), jax.ShapeDtypeStruct(buf.shape, buf.dtype)],
        input_output_aliases={1: 1},
        scratch_shapes=[pltpu.SemaphoreType.DMA((2,)), pltpu.SemaphoreType.DMA((2,))],
    )(mine, buf)


def _adamw_halves(w, g_mine, g_sib, m, v):
    half = g_mine.shape[1]

    def body(c_ref, w_ref, gm_ref, gs_ref, m_ref, v_ref, g_ref, d_ref, nm_ref, nv_ref):
        first = c_ref[0] == 0
        gm, gs = gm_ref[...], gs_ref[...]
        for h, gv in enumerate((jnp.where(first, gm, gs), jnp.where(first, gs, gm))):
            lay = slice(half * h, half * (h + 1))
            g_ref[:, lay, :] = gv
            d_ref[:, lay, :], nm_ref[:, lay, :], nv_ref[:, lay, :] = _adam_update(w_ref[:, lay, :], gv, m_ref[:, lay, :], v_ref[:, lay, :])

    grid, spec = _row_lane_blocks(w.shape)
    full, part = spec(w.shape[1]), spec(half)
    grid_spec = pltpu.PrefetchScalarGridSpec(num_scalar_prefetch=1, grid=grid, in_specs=[full, part, part, full, full], out_specs=[full] * 4)
    return pl.pallas_call(
        body, name="adamw_halves", grid_spec=grid_spec, out_shape=[jax.ShapeDtypeStruct(w.shape, F32)] * 4,
        compiler_params=_cparams(dimension_semantics=("arbitrary", "arbitrary")),
    )(lax.axis_index("c").astype(jnp.int32).reshape(1), w, g_mine, g_sib, m, v)


def _blocks(S):
    return dict(tm=min(512, S), tm_proj=min(1024, S), ts=min(512, S), tq=min(512, S), tq_big=min(1024, S), tk=min(512, S), tks=min(256, S))


def _pair_pad(vec):
    npair = FOX_HEADS // 2
    v = jnp.pad(vec.reshape(npair, 2), ((0, 0), (0, FF_STRIDE - 2))).reshape(1, npair * FF_STRIDE)
    return jnp.pad(v, ((0, 0), (0, LANES - npair * FF_STRIDE)))


def _pair_unpad(row):
    npair = FOX_HEADS // 2
    return row[0, :npair * FF_STRIDE].reshape(npair, FF_STRIDE)[:, :2].reshape(FOX_HEADS)


def _pool_blockdiag(w_pool):
    g, cg, _ = w_pool.shape
    eye = jnp.eye(g, dtype=w_pool.dtype)
    return jnp.einsum("gh,gcd->gchd", eye, w_pool).reshape(g * cg, g * cg)


QK_BOUND_SLACK = 1.05


def _layer_params(norm_g, b_f, q_norm_g, k_norm_g, w_pool, pool_scale):
    qk_bound = QK_BOUND_SLACK * HEAD_DIM * QK_SCALE * jnp.max(jnp.abs(q_norm_g)) * jnp.max(jnp.abs(k_norm_g))
    return dict(g=norm_g.reshape(1, -1), qg=jnp.tile(q_norm_g, FOX_HEADS).reshape(1, FOX_W), kg=jnp.tile(k_norm_g, FOX_HEADS).reshape(1, FOX_W),
                bfp=_pair_pad(b_f), wpd=_pool_blockdiag(w_pool).astype(BF16), ps=pool_scale.reshape(1, POOL_W),
                qkb=jnp.full((1, LANES), qk_bound, F32))


def _layer_fwd(x, wt_all, w_out, layer, prm, bs, target=None):
    projm, ffo, h = _inproj(x, prm["g"], wt_all, layer, tm=bs["tm_proj"], tn=PROJ_TN)
    qn, ka, kb, v, sq, sk, sv, pooled, yp, pm = _prep(projm, ffo, prm["qg"], prm["kg"], prm["bfp"], prm["wpd"], prm["ps"], ts=bs["ts"])
    o, lse, fm = _fox_fwd(qn, ka, kb, v, projm, prm["qkb"], tq=bs["tq"], tk=bs["tk"])
    so, sm = _sb_fwd(sq, sk, sv, projm, tq=bs["tq"], tk=bs["tks"])
    y = _outproj(x, fm, pm, sm, w_out, layer, tm=bs["tm_proj"], target=target)
    saved = dict(x=x, projm=projm, ffo=ffo, h=h, qn=qn, ka=ka, kb=kb, v=v, sq=sq, sk=sk, sv=sv, pooled=pooled, yp=yp,
                 o=o, lse=lse, so=so, fm=fm, pm=pm, sm=sm)
    return y, saved


def _layer_bwd(dy, wt_all, w_out, prm, sv_, bs, layer, stacks):
    dmix, stack_o = _outproj_bwd(dy, sv_["fm"], sv_["pm"], sv_["sm"], w_out, layer, None if stacks is None else stacks[2:], tm=bs["tm_proj"])
    dqn, dkn, dv, dfg, dct, dcr = _fox_bwd(sv_["qn"], sv_["ka"], sv_["kb"], sv_["v"], sv_["o"], sv_["lse"], dmix, sv_["projm"],
                                      prm["qkb"], tq=bs["tq_big"], tk=bs["tk"])
    dsq, dsk, dsv, dsg = _sb_bwd(sv_["sq"], sv_["sk"], sv_["sv"], sv_["so"], dmix, sv_["projm"], tq=bs["tks"], tk=bs["tks"])
    dproj, dqg, dkg, dbf, dwp, dps = _prep_bwd(sv_["projm"], sv_["ffo"], dqn, dkn, dct, dcr, dv, dfg, dsq, dsk, dsv, dsg, dmix,
                                               sv_["pooled"], sv_["yp"], prm["qg"], prm["kg"], prm["bfp"], prm["wpd"], prm["ps"], ts=bs["ts"])
    stack_m, stack_f = _inproj_dw(sv_["h"], dproj, layer, None if stacks is None else stacks[:2], ts=bs["tm_proj"], tn=PROJ_TN)
    dx, dg = _inproj_dx(dproj, wt_all, layer, sv_["x"], prm["g"], dy, tm=bs["tm"])
    grads = dict(
        norm_g=dg[0],
        b_f=_pair_unpad(dbf), q_norm_g=dqg.reshape(FOX_HEADS, HEAD_DIM).sum(0), k_norm_g=dkg.reshape(FOX_HEADS, HEAD_DIM).sum(0),
        w_pool=jnp.stack([dwp[HEAD_DIM * g:HEAD_DIM * (g + 1), HEAD_DIM * g:HEAD_DIM * (g + 1)] for g in range(4)]),
        pool_scale=dps[0])
    return dx, grads, (stack_m, stack_f, stack_o)


def _local_step(x, target, wt_all, w_out, norm_g, b_f, q_norm_g, k_norm_g, w_pool, pool_scale):
    S, D = x.shape
    bs = _blocks(S)
    prms = [_layer_params(norm_g[l], b_f[l], q_norm_g[l], k_norm_g[l], w_pool[l], pool_scale[l]) for l in range(DEPTH)]
    saved = []
    y = x
    for l in range(DEPTH):
        y, s_ = _layer_fwd(y, wt_all, w_out, l, prms[l], bs, target if l == DEPTH - 1 else None)
        saved.append(s_)
    dy, sq = y
    loss = 0.5 * jnp.sum(sq) / D
    grads = [None] * DEPTH
    stacks = None
    for l in reversed(range(DEPTH)):
        dy, grads[l], stacks = _layer_bwd(dy, wt_all, w_out, prms[l], saved[l], bs, l, stacks)
    stacked = {k: jnp.stack([g[k] for g in grads]) for k in grads[0]}
    return loss, dy, stacked, stacks


SMALL = ("norm_g", "b_f", "q_norm_g", "k_norm_g", "w_pool", "pool_scale")


def _pack_small(gr):
    flat = jnp.concatenate([gr[k].reshape(-1) for k in SMALL])
    pad = (-flat.shape[0]) % (8 * LANES)
    return jnp.pad(flat, (0, pad)).reshape(-1, LANES)


def _unpack_small(packed, like):
    flat = packed.reshape(-1)
    out, off = {}, 0
    for k in SMALL:
        n = like[k].size
        out[k] = flat[off:off + n].reshape(like[k].shape)
        off += n
    return out


def kernel(x, norm_g, w_in, b_f, q_norm_g, k_norm_g, w_pool, pool_scale, w_out, loss_target, m_norm_g, m_w_in, m_b_f, m_q_norm_g, m_k_norm_g, m_w_pool, m_pool_scale, m_w_out, v_norm_g, v_w_in, v_b_f, v_q_norm_g, v_k_norm_g, v_w_pool, v_pool_scale, v_w_out):
    weights = dict(norm_g=norm_g, w_in=w_in, b_f=b_f, q_norm_g=q_norm_g, k_norm_g=k_norm_g, w_pool=w_pool, pool_scale=pool_scale, w_out=w_out)
    mom_m = dict(norm_g=m_norm_g, w_in=m_w_in, b_f=m_b_f, q_norm_g=m_q_norm_g, k_norm_g=m_k_norm_g, w_pool=m_w_pool, pool_scale=m_pool_scale, w_out=m_w_out)
    mom_v = dict(norm_g=v_norm_g, w_in=v_w_in, b_f=v_b_f, q_norm_g=v_q_norm_g, k_norm_g=v_k_norm_g, w_pool=v_w_pool, pool_scale=v_pool_scale, w_out=v_w_out)
    shard_cols = w_in.shape[2]
    shard_rows = w_out.shape[1]

    cols_first = lambda a: jnp.transpose(a, (2, 0, 1))
    w_in_t = cols_first(w_in)
    w_in_t_full, w_out_full = _gather_weights(w_in_t, w_out)
    wt_all = _to_aligned(w_in_t_full)
    loss, dx, gr, stacks = _local_step(x[0], loss_target[0], wt_all, w_out_full, norm_g, b_f, q_norm_g, k_norm_g, w_pool, pool_scale)
    loss = lax.psum(loss, ("x", "y", "c"))

    (g_in_mine, g_in_sib), g_w_out = _reduce_scatter(*stacks, shard_cols, shard_rows)
    small = _unpack_small(_all_reduce_small(_pack_small(gr)), {k: weights[k] for k in SMALL})
    grad_w = dict(small, w_out=g_w_out)

    names = ("norm_g", "w_in", "b_f", "q_norm_g", "k_norm_g", "w_pool", "pool_scale", "w_out")
    upd = {k: _adamw_nd(weights[k], grad_w[k], mom_m[k], mom_v[k]) for k in names if k != "w_in"}
    in_t = _adamw_halves(w_in_t, g_in_mine, g_in_sib, cols_first(mom_m["w_in"]), cols_first(mom_v["w_in"]))
    grad_w["w_in"], *upd["w_in"] = [jnp.transpose(a, (1, 2, 0)) for a in in_t]
    return (loss, dx[None], *[grad_w[k] for k in names], *[upd[k][0] for k in names], *[upd[k][1] for k in names], *[upd[k][2] for k in names])
```

```python
import functools

import jax
import jax.numpy as jnp
from jax import lax
from jax.experimental import pallas as pl
from jax.experimental.pallas import tpu as pltpu

F32 = jnp.float32
BF16 = jnp.bfloat16

DEPTH = 4
HEAD_DIM = 64
FOX_HEADS = 8
SB_HEADS = 4
FOX_W = FOX_HEADS * HEAD_DIM
SB_W = SB_HEADS * HEAD_DIM
POOL_W = 256
POOL_WINDOWS = (2, 4, 8, 16)
POOL_HALO = 16
D_MIX = FOX_W + POOL_W + SB_W
EPS = 1e-6
NEG = -1e30
QK_SCALE = HEAD_DIM ** -0.5

ORIG_FOX = 4 * FOX_W
ORIG_FF = ORIG_FOX
ORIG_REST = ORIG_FF + FOX_HEADS
D_IN = ORIG_REST + 2 * POOL_W + 4 * SB_W

C_FQ, C_FK, C_FV, C_FG = 0, FOX_W, 2 * FOX_W, 3 * FOX_W
C_PX = 4 * FOX_W
C_PG = C_PX + POOL_W
C_SQ = C_PG + POOL_W
C_SK, C_SV, C_SG = C_SQ + SB_W, C_SQ + 2 * SB_W, C_SQ + 3 * SB_W
PM = C_SG + SB_W
LANES = 128
LANE_SHIFT = 7
HEAD_SHIFT = 6
PW = PM + LANES
FF_STRIDE = 8
AUG = 3

ADAM_LR = 0.001
ADAM_B1 = 0.9
ADAM_B2 = 0.999
ADAM_EPS = 1e-08
ADAM_WD = 0.01
ADAM_STEP = 10

VMEM_LIMIT = 48 * 1024 * 1024
PROJ_TN = PM // 2


def _cparams(**kw):
    return pltpu.CompilerParams(vmem_limit_bytes=VMEM_LIMIT, **kw)


def _dot(a, b):
    return jnp.dot(a, b, preferred_element_type=F32)


def _dot_nt(a, b):
    return lax.dot_general(a, b, (((1,), (1,)), ((), ())), preferred_element_type=F32)


def _dot_tn(a, b):
    return lax.dot_general(a, b, (((0,), (0,)), ((), ())), preferred_element_type=F32)


def _split2(x):
    hi = x.astype(BF16)
    lo = (x - hi.astype(F32)).astype(BF16)
    return hi, lo


def _split3(x):
    hi = x.astype(BF16)
    r = x - hi.astype(F32)
    mid = r.astype(BF16)
    lo = (r - mid.astype(F32)).astype(BF16)
    return hi, mid, lo


def _dot_exact_rhs(x, m):
    hi, mid, lo = _split3(x)
    return _dot(hi, m) + _dot(mid, m) + _dot(lo, m)


def _dot_exact_lhs(m, x):
    hi, mid, lo = _split3(x)
    return _dot(m, hi) + _dot(m, mid) + _dot(m, lo)


def _sigmoid(x):
    return 1.0 / (1.0 + jnp.exp(-x))


def _silu_pair(x):
    s = _sigmoid(x)
    return x * s, s * (1.0 + x * (1.0 - s))


def _iota(shape, dim):
    return lax.broadcasted_iota(jnp.int32, shape, dim)


def _ones_where(cond):
    return jnp.where(cond, 1.0, 0.0).astype(BF16)


GROUP_SLAB = 256


def _head_blockdiag():
    rows, cols = _iota((2 * GROUP_SLAB, GROUP_SLAB), 0) & (GROUP_SLAB - 1), _iota((2 * GROUP_SLAB, GROUP_SLAB), 1)
    return _ones_where((rows >> HEAD_SHIFT) == (cols >> HEAD_SHIFT))


def _group_sum(x, bd):
    hi, lo = _split2(x)
    slabs = [_dot(jnp.concatenate([hi[:, s:s + GROUP_SLAB], lo[:, s:s + GROUP_SLAB]], axis=1), bd) for s in range(0, x.shape[1], GROUP_SLAB)]
    return jnp.concatenate(slabs, axis=1)


def _lane_pick(x, lane_idx, lane):
    return jnp.sum(jnp.where(lane_idx == lane, x, 0.0), axis=1, keepdims=True)


def _inproj(x, g, wt_all, layer, *, tm, tn):
    S, D = x.shape
    nj = PM // tn

    def body(x_ref, g_ref, w_ref, wff_ref, proj_ref, ff_ref, h_ref):
        @pl.when(pl.program_id(1) == 0)
        def _():
            xf = x_ref[...]
            ms = jnp.mean(xf * xf, axis=-1, keepdims=True)
            h = (xf * lax.rsqrt(ms + EPS) * g_ref[...]).astype(BF16)
            h_ref[...] = h
            ff_ref[...] = _dot_nt(h, wff_ref[...])

        proj_ref[...] = _dot_nt(h_ref[...], w_ref[...])

    return pl.pallas_call(
        body, name="inproj", grid=(S // tm, nj),
        in_specs=[pl.BlockSpec((tm, D), lambda i, j: (i, 0)),
                  pl.BlockSpec((1, D), lambda i, j: (0, 0)),
                  pl.BlockSpec((None, tn, D), lambda i, j: (layer, j, 0)),
                  pl.BlockSpec((None, LANES, D), lambda i, j: (layer, PM // LANES, 0))],
        out_specs=[pl.BlockSpec((tm, tn), lambda i, j: (i, j)),
                   pl.BlockSpec((tm, LANES), lambda i, j: (i, 0)),
                   pl.BlockSpec((tm, D), lambda i, j: (i, 0))],
        out_shape=[jax.ShapeDtypeStruct((S, PM), F32), jax.ShapeDtypeStruct((S, LANES), F32),
                   jax.ShapeDtypeStruct((S, D), BF16)],
        compiler_params=_cparams(dimension_semantics=("arbitrary", "arbitrary")),
    )(x, g, wt_all, wt_all)


def _pool_group_select(lane_group, vals):
    return jnp.where(lane_group == 0, vals[0], jnp.where(lane_group == 1, vals[1], jnp.where(lane_group == 2, vals[2], vals[3])))


def _prep(projm, ffo, qg, kg, bfp, wpd, ps, *, ts):
    S = projm.shape[0]
    nb = S // ts
    hb = ts // POOL_HALO

    def body(fq_ref, fk_ref, fv_ref, pp_ref, halo_ref, ff_ref, sq_ref, sk_ref, sv_ref,
             qg_ref, kg_ref, bf_ref, wpd_ref, ps_ref,
             qn_ref, ka_ref, kb_ref, v_ref, sqo_ref, sko_ref, svo_ref, pooled_ref, yp_ref, pm_ref,
             carry_ref, c_ref, buf_ref):
        i = pl.program_id(0)
        bd = _head_blockdiag()
        normed = []
        for src, g_ref in ((fq_ref, qg_ref), (fk_ref, kg_ref)):
            q = src[...]
            ss = _group_sum(q * q, bd)
            normed.append(q * lax.rsqrt(ss * (1.0 / HEAD_DIM) + EPS) * g_ref[...])
        qn_ref[...] = (normed[0] * QK_SCALE).astype(BF16)
        kn = normed[1]
        v_ref[...] = fv_ref[...].astype(BF16)
        sqo_ref[...] = (sq_ref[...] * QK_SCALE).astype(BF16)
        sko_ref[...] = sk_ref[...].astype(BF16)
        svo_ref[...] = sv_ref[...].astype(BF16)

        @pl.when(i == 0)
        def _():
            carry_ref[...] = jnp.zeros_like(carry_ref)

        z = ff_ref[...] + bf_ref[...]
        lf = jnp.minimum(z, 0.0) - jnp.log(1.0 + jnp.exp(-jnp.abs(z)))
        tri = _ones_where(_iota((ts, ts), 1) <= _iota((ts, ts), 0))
        c = _dot_exact_lhs(tri, lf) + carry_ref[...]
        c_ref[...] = c
        carry_ref[...] = c_ref[ts - 1:ts, :]
        parts = jnp.concatenate(_split3(-c), axis=1)
        row = _iota((AUG * LANES, FOX_W), 0)
        col = _iota((AUG * LANES, FOX_W), 1)
        part, src = row >> LANE_SHIFT, row & (LANES - 1)
        pair, off = col >> LANE_SHIFT, col & (LANES - 1)
        sel_a = _ones_where((src == FF_STRIDE * pair) & (off == HEAD_DIM + part))
        sel_b = _ones_where((src == FF_STRIDE * pair + 1) & (off == part))
        first_half = (_iota((1, FOX_W), 1) & HEAD_DIM) == 0
        ka_ref[...] = jnp.where(first_half, kn, _dot(parts, sel_a)).astype(BF16)
        kb_ref[...] = jnp.where(first_half, _dot(parts, sel_b), kn).astype(BF16)

        x = pp_ref[:, 0:POOL_W]
        pg = pp_ref[:, POOL_W:2 * POOL_W]
        halo = jnp.where(i > 0, halo_ref[:, 0:POOL_W], 0.0)
        buf_ref[0:POOL_HALO, :] = halo
        buf_ref[POOL_HALO:POOL_HALO + ts, :] = x
        acc = x
        snaps = []
        for d in range(1, POOL_HALO):
            acc = acc + buf_ref[pl.ds(POOL_HALO - d, ts), :]
            if d + 1 in POOL_WINDOWS:
                snaps.append(acc)
        lane_group = _iota((1, POOL_W), 1) >> HEAD_SHIFT
        wsum = _pool_group_select(lane_group, snaps)
        wlen = _pool_group_select(lane_group, [float(w) for w in POOL_WINDOWS])
        tpos = (i * ts + _iota((ts, 1), 0) + 1).astype(F32)
        pooled = wsum / jnp.minimum(tpos, wlen) - x
        pb = pooled.astype(BF16)
        pooled_ref[...] = pb
        yp = _dot(pb, wpd_ref[...])
        yp_ref[...] = yp
        pm_ref[...] = (yp * ps_ref[...] * (pg * _sigmoid(pg))).astype(BF16)

    blk = lambda w, c: pl.BlockSpec((ts, w), lambda i: (i, c))
    full = lambda a: pl.BlockSpec(a.shape, lambda i: (0,) * a.ndim)
    out_shapes = [
        jax.ShapeDtypeStruct((S, FOX_W), BF16), jax.ShapeDtypeStruct((S, FOX_W), BF16), jax.ShapeDtypeStruct((S, FOX_W), BF16),
        jax.ShapeDtypeStruct((S, FOX_W), BF16),
        jax.ShapeDtypeStruct((S, SB_W), BF16), jax.ShapeDtypeStruct((S, SB_W), BF16), jax.ShapeDtypeStruct((S, SB_W), BF16),
        jax.ShapeDtypeStruct((S, POOL_W), BF16), jax.ShapeDtypeStruct((S, POOL_W), F32), jax.ShapeDtypeStruct((S, POOL_W), BF16),
    ]
    out_specs = [
        blk(FOX_W, 0), blk(FOX_W, 0), blk(FOX_W, 0), blk(FOX_W, 0),
        blk(SB_W, 0), blk(SB_W, 0), blk(SB_W, 0),
        blk(POOL_W, 0), blk(POOL_W, 0), blk(POOL_W, 0),
    ]
    return pl.pallas_call(
        body, name="prep", grid=(nb,),
        in_specs=[blk(FOX_W, C_FQ // FOX_W), blk(FOX_W, C_FK // FOX_W), blk(FOX_W, C_FV // FOX_W), blk(2 * POOL_W, C_PX // (2 * POOL_W)),
                  pl.BlockSpec((POOL_HALO, 2 * POOL_W), lambda i: (jnp.maximum(i * hb - 1, 0), C_PX // (2 * POOL_W))),
                  blk(LANES, 0),
                  blk(SB_W, C_SQ // SB_W), blk(SB_W, C_SK // SB_W), blk(SB_W, C_SV // SB_W),
                  full(qg), full(kg), full(bfp), full(wpd), full(ps)],
        out_specs=out_specs, out_shape=out_shapes,
        scratch_shapes=[pltpu.VMEM((1, LANES), F32), pltpu.VMEM((ts, LANES), F32), pltpu.VMEM((ts + POOL_HALO, POOL_W), F32)],
        compiler_params=_cparams(dimension_semantics=("arbitrary",)),
    )(projm, projm, projm, projm, projm, ffo, projm, projm, projm, qg, kg, bfp, wpd, ps)


def _pair_masks(x):
    ma = _iota((1, LANES), 1) < HEAD_DIM
    zero = jnp.zeros_like(x)
    return jnp.where(ma, x, zero), jnp.where(ma, zero, x)


DIAG_TILE = 256


def _diag_tiles(tq, size=DIAG_TILE):
    size = min(tq, size)
    return [(t * size, size) for t in range(tq // size)]


def _put_rows(old, new, r0):
    return new if r0 == 0 else jnp.concatenate([old[:r0], new], axis=0)


def _aug_queries(q):
    lane = _iota((1, LANES), 1)
    one = jnp.ones_like(q)
    zero = jnp.zeros_like(q)
    qa = jnp.where(lane < HEAD_DIM, q, jnp.where(lane < HEAD_DIM + AUG, one, zero))
    qb = jnp.where(lane >= HEAD_DIM, q, jnp.where(lane < AUG, one, zero))
    return qa, qb


EXP_DEAD = -105.0
PACK = 16


def _fox_walk_left(nfull, tk, block, carry, k_refs, qk_bound, row_floor):
    lane = _iota((1, LANES), 1)

    def alive(h, jj, c):
        k0 = pl.multiple_of(jnp.maximum(nfull - 1 - jj, 0) * tk + tk - PACK, PACK)
        last = k_refs[h][pl.ds(k0, PACK), :].astype(F32)
        lo = HEAD_DIM if h == 0 else 0
        negc = jnp.sum(jnp.where((lane >= lo) & (lane < lo + AUG), last, 0.0), axis=1, keepdims=True)
        return qk_bound + jnp.max(negc) - row_floor(c)[h] >= EXP_DEAD

    def walk(heads, jj0, c0):
        def go_on(state):
            jj, c = state
            ok = jj < nfull
            for h in heads:
                ok = ok & alive(h, jj, c)
            return ok

        def step(state):
            jj, c = state
            return jj + 1, block(pl.multiple_of((nfull - 1 - jj) * tk, tk), tk, 0, c, False, heads)

        return lax.while_loop(go_on, step, (jj0, c0))

    jj_pair, carry = walk((0, 1), jnp.int32(0), carry)
    carry = walk((0,), jj_pair, carry)[1]
    return walk((1,), jj_pair, carry)[1]


def _fox_fwd(qn, ka, kb, v, projm, qkb, *, tq, tk):
    S = qn.shape[0]
    npair = FOX_HEADS // 2

    def body(q_ref, ka_ref, kb_ref, v_ref, fg_ref, qkb_ref, o_ref, lse_ref, fm_ref):
        qi = pl.program_id(1)
        lane = _iota((1, LANES), 1)
        ma = lane < HEAD_DIM
        qaug = _aug_queries(q_ref[...])
        k_refs = (ka_ref, kb_ref)

        def block(k0, tkl, r0, carry, masked, heads=(0, 1)):
            vb = v_ref[pl.ds(k0, tkl), :]
            if masked:
                mask = (k0 + _iota((tq - r0, tkl), 1)) <= (qi * tq + r0 + _iota((tq - r0, tkl), 0))
            scores = {h: _dot_nt(qaug[h][r0:], k_refs[h][pl.ds(k0, tkl), :]) for h in heads}
            new = list(carry)
            for h in heads:
                m, l, acc = [x[r0:] for x in carry[h]]
                s = jnp.where(mask, scores[h], NEG) if masked else scores[h]
                m_new = jnp.maximum(m, jnp.max(s, axis=1, keepdims=True))
                alpha = jnp.exp(m - m_new)
                p = jnp.exp(s - m_new)
                sub = (m_new, alpha * l + jnp.sum(p, axis=1, keepdims=True), alpha * acc + _dot(p.astype(BF16), vb))
                new[h] = tuple(_put_rows(old, x, r0) for old, x in zip(carry[h], sub))
            return tuple(new)

        carry = tuple((jnp.full((tq, 1), NEG, F32), jnp.zeros((tq, 1), F32), jnp.zeros((tq, LANES), F32)) for _ in range(2))
        for off, size in _diag_tiles(tq, tq):
            carry = block(pl.multiple_of(qi * tq + off, size), size, off, carry, True)
        carry = _fox_walk_left((qi * tq) // tk, tk, block, carry, k_refs, jnp.max(qkb_ref[...]),
                               lambda c: (jnp.min(c[0][0]), jnp.min(c[1][0])))
        (ma_, la, acca), (mb_, lb, accb) = carry
        o = jnp.where(ma, acca / la, accb / lb)
        o_ref[...] = o
        lse_ref[...] = jnp.where(ma, ma_ + jnp.log(la), mb_ + jnp.log(lb))
        fg = fg_ref[...]
        fm_ref[...] = (o * (fg * _sigmoid(fg))).astype(BF16)

    qblk = pl.BlockSpec((tq, LANES), lambda p, i: (i, p))
    kvblk = pl.BlockSpec((S, LANES), lambda p, i: (0, p))
    return pl.pallas_call(
        body, name="fox_fwd", grid=(npair, S // tq),
        in_specs=[qblk, kvblk, kvblk, kvblk,
                  pl.BlockSpec((tq, LANES), lambda p, i: (i, C_FG // LANES + p)),
                  pl.BlockSpec((1, LANES), lambda p, i: (0, 0))],
        out_specs=[qblk, qblk, qblk],
        out_shape=[jax.ShapeDtypeStruct((S, FOX_W), F32), jax.ShapeDtypeStruct((S, FOX_W), F32), jax.ShapeDtypeStruct((S, FOX_W), BF16)],
        compiler_params=_cparams(dimension_semantics=("arbitrary", "arbitrary")),
    )(qn, ka, kb, v, projm, qkb)


def _suffix_sums(x, tmat2):
    return _dot(jnp.concatenate(_split2(x), axis=1), tmat2)


def _suffix_matrix(tk, inclusive):
    rr, cc = _iota((2 * tk, tk), 0) & (tk - 1), _iota((2 * tk, tk), 1)
    return _ones_where(rr >= cc) if inclusive else _ones_where(rr > cc)


def _sb_scores(qh, kb, causal, tmat2, r_runs):
    heads = range(2)
    zs = [_dot_nt(qh[h], kb) for h in heads]
    nsps = [jnp.minimum(-z, 0.0) - jnp.log(1.0 + jnp.exp(-jnp.abs(z))) for z in zs]
    lbs = nsps if causal is None else [jnp.where(causal, n, 0.0) for n in nsps]
    rins = [_suffix_sums(lb, tmat2) for lb in lbs]
    args = [zs[h] + lbs[h] + (rins[h] + r_runs[h]) for h in heads]
    a_s = [jnp.exp(arg if causal is None else jnp.where(causal, arg, NEG)) for arg in args]
    return zs, nsps, lbs, a_s


def _sb_walk_left(nfull, tk, block, carry, running_sums):
    def alive(state):
        jj, c = state
        ra, rb = running_sums(c)
        return (jj < nfull) & (jnp.max(jnp.maximum(ra, rb)) >= EXP_DEAD)

    def step(state):
        jj, c = state
        return jj + 1, block(pl.multiple_of((nfull - 1 - jj) * tk, tk), 0, c, False)

    return lax.while_loop(alive, step, (jnp.int32(0), carry))[1]


def _sb_fwd(sq, sk, sv, projm, *, tq, tk):
    S = sq.shape[0]
    npair = SB_HEADS // 2

    def body(q_ref, k_ref, v_ref, sg_ref, o_ref, sm_ref):
        qi = pl.program_id(1)
        lane = _iota((1, LANES), 1)
        ma = lane < HEAD_DIM
        qh = _pair_masks(q_ref[...])
        tmat2 = _suffix_matrix(tk, inclusive=False)
        nfull = (qi * tq) // tk

        def block(k0, r0, carry, masked):
            nr = tq - r0
            kb = k_ref[pl.ds(k0, tk), :]
            vb = v_ref[pl.ds(k0, tk), :]
            causal = (k0 + _iota((nr, tk), 1)) < (qi * tq + r0 + _iota((nr, tk), 0)) if masked else None
            _, _, lbs, a_s = _sb_scores([q[r0:] for q in qh], kb, causal, tmat2, [carry[h][0][r0:] for h in range(2)])
            pv = _dot(jnp.concatenate([a.astype(BF16) for a in a_s], axis=0), vb)
            return tuple((_put_rows(carry[h][0], carry[h][0][r0:] + jnp.sum(lbs[h], axis=1, keepdims=True), r0),
                          _put_rows(carry[h][1], carry[h][1][r0:] + pv[h * nr:(h + 1) * nr], r0)) for h in range(2))

        carry = tuple((jnp.zeros((tq, 1), F32), jnp.zeros((tq, LANES), F32)) for _ in range(2))
        for off, size in reversed(_diag_tiles(tq)):
            assert size == tk
            carry = block(pl.multiple_of(qi * tq + off, tk), off, carry, True)
        (_, acca), (_, accb) = _sb_walk_left(nfull, tk, block, carry, lambda c: (c[0][0], c[1][0]))
        o = jnp.where(ma, acca, accb)
        o_ref[...] = o
        sg = sg_ref[...]
        sm_ref[...] = (o * (sg * _sigmoid(sg))).astype(BF16)

    qblk = pl.BlockSpec((tq, LANES), lambda p, i: (i, p))
    kvblk = pl.BlockSpec((S, LANES), lambda p, i: (0, p))
    return pl.pallas_call(
        body, name="sb_fwd", grid=(npair, S // tq),
        in_specs=[qblk, kvblk, kvblk, pl.BlockSpec((tq, LANES), lambda p, i: (i, C_SG // LANES + p))],
        out_specs=[qblk, qblk],
        out_shape=[jax.ShapeDtypeStruct((S, SB_W), F32), jax.ShapeDtypeStruct((S, SB_W), BF16)],
        compiler_params=_cparams(dimension_semantics=("arbitrary", "arbitrary")),
    )(sq, sk, sv, projm)


def _outproj(x, fm, pm, sm, w_out, layer, *, tm, target=None):
    S, D = x.shape

    def body(x_ref, fm_ref, pm_ref, sm_ref, w_ref, *refs):
        y = x_ref[...] + _dot(fm_ref[...], w_ref[0:FOX_W, :])
        y = y + _dot(pm_ref[...], w_ref[FOX_W:FOX_W + POOL_W, :])
        y = y + _dot(sm_ref[...], w_ref[FOX_W + POOL_W:D_MIX, :])
        if target is None:
            refs[0][...] = y
            return
        t_ref, dy_ref, sq_ref = refs

        @pl.when(pl.program_id(0) == 0)
        def _():
            sq_ref[...] = jnp.zeros_like(sq_ref)

        d = y - t_ref[...]
        dy_ref[...] = d * (1.0 / D)
        sq_ref[...] += jnp.sum(d * d, axis=0, keepdims=True)

    row = lambda w: pl.BlockSpec((tm, w), lambda i: (i, 0))
    in_specs = [row(D), row(FOX_W), row(POOL_W), row(SB_W), pl.BlockSpec((None, D_MIX, D), lambda i: (layer, 0, 0))]
    kw = dict(name="outproj", grid=(S // tm,), compiler_params=_cparams(dimension_semantics=("arbitrary",)))
    if target is None:
        return pl.pallas_call(body, in_specs=in_specs, out_specs=row(D), out_shape=jax.ShapeDtypeStruct((S, D), F32), **kw)(x, fm, pm, sm, w_out)
    return pl.pallas_call(
        body, in_specs=in_specs + [row(D)], out_specs=[row(D), pl.BlockSpec((1, D), lambda i: (0, 0))],
        out_shape=[jax.ShapeDtypeStruct((S, D), F32), jax.ShapeDtypeStruct((1, D), F32)], **kw)(x, fm, pm, sm, w_out, target)


def _outproj_bwd(dy, fm, pm, sm, w_out, layer, stacks, *, tm):
    S, D = dy.shape

    def body(dy_ref, fm_ref, pm_ref, sm_ref, w_ref, dm_ref, dw_ref):
        @pl.when(pl.program_id(0) == 0)
        def _():
            dw_ref[...] = jnp.zeros_like(dw_ref)

        dyb = dy_ref[...].astype(BF16)
        dm_ref[...] = _dot_nt(dyb, w_ref[...])
        dw_ref[0:FOX_W, :] += _dot_tn(fm_ref[...], dyb)
        dw_ref[FOX_W:FOX_W + POOL_W, :] += _dot_tn(pm_ref[...], dyb)
        dw_ref[FOX_W + POOL_W:D_MIX, :] += _dot_tn(sm_ref[...], dyb)

    row = lambda w: pl.BlockSpec((tm, w), lambda i: (i, 0))
    wspec = pl.BlockSpec((None, D_MIX, D), lambda i: (layer, 0, 0))
    return _stack_call(
        body, "outproj_bwd", (S // tm,), [row(D), row(FOX_W), row(POOL_W), row(SB_W), wspec], (dy, fm, pm, sm, w_out),
        [pl.BlockSpec((None, D_MIX, D), lambda i: (layer, 0, 0))], [(D_MIX, D)], stacks,
        plain_specs=[row(D_MIX)], plain_shapes=[jax.ShapeDtypeStruct((S, D_MIX), F32)],
        compiler_params=_cparams(dimension_semantics=("arbitrary",)))


def _fox_bwd(qn, ka, kb, v, o, lse, dmix, projm, qkb, *, tq, tk):
    S = qn.shape[0]
    npair = FOX_HEADS // 2

    def body(q_ref, ka_ref, kb_ref, v_ref, o_ref, lse_ref, dm_ref, fg_ref, qkb_ref,
             dq_ref, dk_ref, dv_ref, dfg_ref, dct_ref, dcr_ref):
        qi = pl.program_id(1)

        @pl.when(qi == 0)
        def _():
            dk_ref[...] = jnp.zeros_like(dk_ref)
            dv_ref[...] = jnp.zeros_like(dv_ref)
            dct_ref[...] = jnp.zeros_like(dct_ref)

        lane = _iota((1, LANES), 1)
        ma = lane < HEAD_DIM
        qh = _pair_masks(q_ref[...])
        qaug = _aug_queries(q_ref[...])
        k_refs = (ka_ref, kb_ref)
        lsev = lse_ref[...]
        lse = (_lane_pick(lsev, lane, 0), _lane_pick(lsev, lane, HEAD_DIM))
        fg = fg_ref[...]
        silu, dsilu = _silu_pair(fg)
        dm = dm_ref[...]
        ov = o_ref[...]
        do = dm * silu
        dfg_ref[...] = dm * ov * dsilu
        dd = do * ov
        dsum = (jnp.sum(jnp.where(ma, dd, 0.0), axis=1, keepdims=True), jnp.sum(jnp.where(ma, 0.0, dd), axis=1, keepdims=True))
        doh = _pair_masks(do.astype(BF16))

        def block(k0, tkl, r0, carry, masked, heads=(0, 1)):
            vb = v_ref[pl.ds(k0, tkl), :]
            if masked:
                mask = (k0 + _iota((tq - r0, tkl), 1)) <= (qi * tq + r0 + _iota((tq - r0, tkl), 0))
            kaugs = {h: k_refs[h][pl.ds(k0, tkl), :] for h in heads}
            scores = {h: _dot_nt(qaug[h][r0:], kaugs[h]) for h in heads}
            dps = {h: _dot_nt(doh[h][r0:], vb) for h in heads}
            ps, dss = [], []
            rows = [carry[1], carry[2]]
            for h in heads:
                s = jnp.where(mask, scores[h], NEG) if masked else scores[h]
                p = jnp.exp(s - lse[h][r0:])
                dsf = p * (dps[h] - dsum[h][r0:])
                dct_ref[0, h:h + 1, pl.ds(k0, tkl)] -= jnp.sum(dsf, axis=0, keepdims=True)
                rows[h] = _put_rows(carry[1 + h], carry[1 + h][r0:] + jnp.sum(dsf, axis=1, keepdims=True), r0)
                ps.append(p.astype(BF16))
                dss.append(dsf.astype(BF16))
            dv_ref[pl.ds(k0, tkl), :] += _dot_tn(jnp.concatenate(ps, axis=0), jnp.concatenate([doh[h][r0:] for h in heads], axis=0))
            dk_ref[pl.ds(k0, tkl), :] += _dot_tn(jnp.concatenate(dss, axis=0), jnp.concatenate([qh[h][r0:] for h in heads], axis=0))
            kh = jnp.concatenate([_pair_masks(kaugs[h])[h] for h in heads], axis=0)
            dq = _put_rows(carry[0], carry[0][r0:] + _dot(jnp.concatenate(dss, axis=1), kh), r0)
            return (dq, rows[0], rows[1])

        zcol = jnp.zeros((tq, 1), F32)
        carry = (jnp.zeros((tq, LANES), F32), zcol, zcol)
        for off, size in _diag_tiles(tq):
            carry = block(pl.multiple_of(qi * tq + off, size), size, off, carry, True)
        floors = (jnp.min(lse[0]), jnp.min(lse[1]))
        dq, rowa, rowb = _fox_walk_left((qi * tq) // tk, tk, block, carry, k_refs, jnp.max(qkb_ref[...]), lambda c: floors)
        dq_ref[...] = dq * QK_SCALE
        dcr_ref[0] = jnp.where(ma, rowa, rowb)

    qblk = pl.BlockSpec((tq, LANES), lambda p, i: (i, p))
    kvblk = pl.BlockSpec((S, LANES), lambda p, i: (0, p))
    f32out = jax.ShapeDtypeStruct((S, FOX_W), F32)
    ctblk = pl.BlockSpec((1, FF_STRIDE, S), lambda p, i: (p, 0, 0))
    return pl.pallas_call(
        body, name="fox_bwd", grid=(npair, S // tq),
        in_specs=[qblk, kvblk, kvblk, kvblk, qblk, qblk, qblk,
                  pl.BlockSpec((tq, LANES), lambda p, i: (i, C_FG // LANES + p)),
                  pl.BlockSpec((1, LANES), lambda p, i: (0, 0))],
        out_specs=[qblk, kvblk, kvblk, qblk, ctblk, pl.BlockSpec((1, tq, LANES), lambda p, i: (p, i, 0))],
        out_shape=[f32out, f32out, f32out, f32out, jax.ShapeDtypeStruct((npair, FF_STRIDE, S), F32),
                   jax.ShapeDtypeStruct((npair, S, LANES), F32)],
        compiler_params=_cparams(dimension_semantics=("arbitrary", "arbitrary")),
    )(qn, ka, kb, v, o, lse, dmix, projm, qkb)


def _sb_bwd(sq, sk, sv, o, dmix, projm, *, tq, tk):
    S = sq.shape[0]
    npair = SB_HEADS // 2
    mix0 = (FOX_W + POOL_W) // LANES

    def body(q_ref, k_ref, v_ref, o_ref, dm_ref, sg_ref, dq_ref, dk_ref, dv_ref, dsg_ref):
        qi = pl.program_id(1)

        @pl.when(qi == 0)
        def _():
            dk_ref[...] = jnp.zeros_like(dk_ref)
            dv_ref[...] = jnp.zeros_like(dv_ref)

        lane = _iota((1, LANES), 1)
        ma = lane < HEAD_DIM
        qh = _pair_masks(q_ref[...])
        sg = sg_ref[...]
        silu, dsilu = _silu_pair(sg)
        dm = dm_ref[...]
        ov = o_ref[...]
        do = dm * silu
        dsg_ref[...] = dm * ov * dsilu
        dob = do.astype(BF16)
        dd = dob.astype(F32) * ov
        dsum = (jnp.sum(jnp.where(ma, dd, 0.0), axis=1, keepdims=True), jnp.sum(jnp.where(ma, 0.0, dd), axis=1, keepdims=True))
        doh = _pair_masks(dob)
        tmat2 = _suffix_matrix(tk, inclusive=False)
        tmat2_inc = _suffix_matrix(tk, inclusive=True)
        nfull = (qi * tq) // tk

        def block(k0, r0, carry, masked):
            nr = tq - r0
            kb = k_ref[pl.ds(k0, tk), :]
            vb = v_ref[pl.ds(k0, tk), :]
            kh = _pair_masks(kb)
            causal = (k0 + _iota((nr, tk), 1)) < (qi * tq + r0 + _iota((nr, tk), 0)) if masked else None
            heads = range(2)
            qs = [q[r0:] for q in qh]
            dos = [d[r0:] for d in doh]
            das = [_dot_nt(dos[h], vb) for h in heads]
            zs, nsps, lbs, a_s = _sb_scores(qs, kb, causal, tmat2, [carry[h][0][r0:] for h in heads])
            abs_ = [a.astype(BF16) for a in a_s]
            us = [abs_[h].astype(F32) * das[h] for h in heads]
            uins = [_suffix_sums(u, tmat2_inc) for u in us]
            dzs = []
            for h in heads:
                cum_u = dsum[h][r0:] - (uins[h] + carry[h][1][r0:])
                dz = us[h] * jnp.exp(nsps[h]) - jnp.exp(zs[h] + nsps[h]) * cum_u
                if masked:
                    dz = jnp.where(causal, dz, 0.0)
                dzs.append(dz.astype(BF16))
            dv_ref[pl.ds(k0, tk), :] += _dot_tn(jnp.concatenate(abs_, axis=0), jnp.concatenate(dos, axis=0))
            dk_ref[pl.ds(k0, tk), :] += _dot_tn(jnp.concatenate(dzs, axis=0), jnp.concatenate(qs, axis=0))
            dq = _put_rows(carry[2], carry[2][r0:] + _dot(jnp.concatenate(dzs, axis=1), jnp.concatenate(kh, axis=0)), r0)
            new = [(_put_rows(carry[h][0], carry[h][0][r0:] + jnp.sum(lbs[h], axis=1, keepdims=True), r0),
                    _put_rows(carry[h][1], carry[h][1][r0:] + jnp.sum(us[h], axis=1, keepdims=True), r0)) for h in heads]
            return (new[0], new[1], dq)

        zcol = jnp.zeros((tq, 1), F32)
        carry = ((zcol, zcol), (zcol, zcol), jnp.zeros((tq, LANES), F32))
        for off, size in reversed(_diag_tiles(tq)):
            assert size == tk
            carry = block(pl.multiple_of(qi * tq + off, tk), off, carry, True)
        dq = _sb_walk_left(nfull, tk, block, carry, lambda c: (c[0][0], c[1][0]))[2]
        dq_ref[...] = dq * QK_SCALE

    qblk = pl.BlockSpec((tq, LANES), lambda p, i: (i, p))
    kvblk = pl.BlockSpec((S, LANES), lambda p, i: (0, p))
    f32out = jax.ShapeDtypeStruct((S, SB_W), F32)
    return pl.pallas_call(
        body, name="sb_bwd", grid=(npair, S // tq),
        in_specs=[qblk, kvblk, kvblk, qblk,
                  pl.BlockSpec((tq, LANES), lambda p, i: (i, mix0 + p)),
                  pl.BlockSpec((tq, LANES), lambda p, i: (i, C_SG // LANES + p))],
        out_specs=[qblk, kvblk, kvblk, qblk],
        out_shape=[f32out, f32out, f32out, f32out],
        compiler_params=_cparams(dimension_semantics=("arbitrary", "arbitrary")),
    )(sq, sk, sv, o, dmix, projm)


def _prep_bwd(projm, ffo, dqn, dkn, dct, dcr, dv, dfg, dsq, dsk, dsv, dsg, dmix, pooled, yp, qg, kg, bfp, wpd, ps, *, ts):
    S = projm.shape[0]
    nb = S // ts
    hb = ts // POOL_HALO
    npair = FOX_HEADS // 2
    last_halo = S // POOL_HALO - 1

    def body(fq_ref, fk_ref, pp_ref, pph_ref, ff_ref,
             dqn_ref, dkn_ref, dct_ref, dcr_ref, dv_ref, dfg_ref, dsq_ref, dsk_ref, dsv_ref, dsg_ref,
             dmp_ref, dmh_ref, pooled_ref, yp_ref, qg_ref, kg_ref, bf_ref, wpd_ref, ps_ref,
             dp_ref, dqg_ref, dkg_ref, dbf_ref, dwp_ref, dps_ref,
             carry_ref, dl_ref, buf_ref, dct_s):
        i = pl.program_id(0)
        blk = nb - 1 - i

        @pl.when(i == 0)
        def _():
            carry_ref[...] = jnp.zeros_like(carry_ref)
            dqg_ref[...] = jnp.zeros_like(dqg_ref)
            dkg_ref[...] = jnp.zeros_like(dkg_ref)
            dbf_ref[...] = jnp.zeros_like(dbf_ref)
            dwp_ref[...] = jnp.zeros_like(dwp_ref)
            dps_ref[...] = jnp.zeros_like(dps_ref)

        bd = _head_blockdiag()
        for raw_ref, g_ref, dn, dg_ref, col in ((fq_ref, qg_ref, dqn_ref[...], dqg_ref, C_FQ), (fk_ref, kg_ref, dkn_ref[...], dkg_ref, C_FK)):
            q = raw_ref[...]
            rstd = lax.rsqrt(_group_sum(q * q, bd) * (1.0 / HEAD_DIM) + EPS)
            xhat = q * rstd
            dg_ref[...] += jnp.sum(dn * xhat, axis=0, keepdims=True)
            dyg = dn * g_ref[...]
            mean = _group_sum(dyg * xhat, bd) * (1.0 / HEAD_DIM)
            dp_ref[:, col:col + FOX_W] = (rstd * (dyg - xhat * mean)).astype(BF16)
        dp_ref[:, C_FV:C_FV + FOX_W] = dv_ref[...].astype(BF16)
        dp_ref[:, C_FG:C_FG + FOX_W] = dfg_ref[...].astype(BF16)
        dp_ref[:, C_SQ:C_SQ + SB_W] = dsq_ref[...].astype(BF16)
        dp_ref[:, C_SK:C_SK + SB_W] = dsk_ref[...].astype(BF16)
        dp_ref[:, C_SV:C_SV + SB_W] = dsv_ref[...].astype(BF16)
        dp_ref[:, C_SG:C_SG + SB_W] = dsg_ref[...].astype(BF16)

        dct_s[...] = jnp.zeros_like(dct_s)
        for p in range(npair):
            dct_s[FF_STRIDE * p:FF_STRIDE * (p + 1), :] = dct_ref[p]
        dc = dct_s[...].T
        lane = _iota((1, LANES), 1)
        for p in range(npair):
            dcr = dcr_ref[p]
            dc = dc + jnp.where(lane == FF_STRIDE * p, _lane_pick(dcr, lane, 0), 0.0)
            dc = dc + jnp.where(lane == FF_STRIDE * p + 1, _lane_pick(dcr, lane, HEAD_DIM), 0.0)
        triu = _ones_where(_iota((ts, ts), 1) >= _iota((ts, ts), 0))
        dlf = _dot_exact_lhs(triu, dc) + carry_ref[...]
        dl_ref[...] = dlf
        carry_ref[...] = dl_ref[0:1, :]
        z = ff_ref[...] + bf_ref[...]
        dff = dlf * (1.0 / (1.0 + jnp.exp(z)))
        dbf_ref[...] += jnp.sum(dff, axis=0, keepdims=True)
        dp_ref[:, PM:PW] = dff.astype(BF16)

        psv = ps_ref[...]
        wpdv = wpd_ref[...]
        lane_group = _iota((1, POOL_W), 1) >> HEAD_SHIFT
        wlen = _pool_group_select(lane_group, [float(w) for w in POOL_WINDOWS])
        pg = pp_ref[:, POOL_W:2 * POOL_W]
        silu, dsilu = _silu_pair(pg)
        dmp = dmp_ref[...]
        ypv = yp_ref[...]
        dp_ref[:, C_PG:C_PG + POOL_W] = (dmp * (ypv * psv) * dsilu).astype(BF16)
        dps_ref[...] += jnp.sum(dmp * silu * ypv, axis=0, keepdims=True)
        dyp = (dmp * psv * silu).astype(BF16)
        dwp_ref[...] += _dot_tn(pooled_ref[...], dyp)
        dpooled = _dot_nt(dyp, wpdv)
        pgh = pph_ref[:, POOL_W:2 * POOL_W]
        dyph = (dmh_ref[...] * psv * (pgh * _sigmoid(pgh))).astype(BF16)
        dpooled_h = jnp.where(blk < nb - 1, _dot_nt(dyph, wpdv), 0.0)
        tpos = (blk * ts + _iota((ts, 1), 0) + 1).astype(F32)
        ev = dpooled / jnp.minimum(tpos, wlen)
        buf_ref[0:ts, :] = ev
        buf_ref[ts:ts + POOL_HALO, :] = dpooled_h / wlen
        acc = ev
        snaps = []
        for d in range(1, POOL_HALO):
            acc = acc + buf_ref[pl.ds(d, ts), :]
            if d + 1 in POOL_WINDOWS:
                snaps.append(acc)
        dp_ref[:, C_PX:C_PX + POOL_W] = (_pool_group_select(lane_group, snaps) - dpooled).astype(BF16)

    rblk = lambda w, c: pl.BlockSpec((ts, w), lambda i: (nb - 1 - i, c))
    full = lambda a: pl.BlockSpec(a.shape, lambda i: (0,) * a.ndim)
    halo = lambda w, c: pl.BlockSpec((POOL_HALO, w), lambda i: (jnp.minimum((nb - i) * hb, last_halo), c))
    acc_spec = lambda r, w: pl.BlockSpec((r, w), lambda i: (0, 0))
    return pl.pallas_call(
        body, name="prep_bwd", grid=(nb,),
        in_specs=[rblk(FOX_W, C_FQ // FOX_W), rblk(FOX_W, C_FK // FOX_W), rblk(2 * POOL_W, C_PX // (2 * POOL_W)),
                  halo(2 * POOL_W, C_PX // (2 * POOL_W)), rblk(LANES, 0),
                  rblk(FOX_W, 0), rblk(FOX_W, 0), pl.BlockSpec((npair, FF_STRIDE, ts), lambda i: (0, 0, nb - 1 - i)),
                  pl.BlockSpec((npair, ts, LANES), lambda i: (0, nb - 1 - i, 0)), rblk(FOX_W, 0), rblk(FOX_W, 0),
                  rblk(SB_W, 0), rblk(SB_W, 0), rblk(SB_W, 0), rblk(SB_W, 0),
                  rblk(POOL_W, FOX_W // POOL_W), halo(POOL_W, FOX_W // POOL_W), rblk(POOL_W, 0), rblk(POOL_W, 0),
                  full(qg), full(kg), full(bfp), full(wpd), full(ps)],
        out_specs=[rblk(PW, 0), acc_spec(1, FOX_W), acc_spec(1, FOX_W), acc_spec(1, LANES), acc_spec(POOL_W, POOL_W), acc_spec(1, POOL_W)],
        out_shape=[jax.ShapeDtypeStruct((S, PW), BF16), jax.ShapeDtypeStruct((1, FOX_W), F32), jax.ShapeDtypeStruct((1, FOX_W), F32),
                   jax.ShapeDtypeStruct((1, LANES), F32), jax.ShapeDtypeStruct((POOL_W, POOL_W), F32), jax.ShapeDtypeStruct((1, POOL_W), F32)],
        scratch_shapes=[pltpu.VMEM((1, LANES), F32), pltpu.VMEM((ts, LANES), F32), pltpu.VMEM((ts + POOL_HALO, POOL_W), F32),
                        pltpu.VMEM((LANES, ts), F32)],
        compiler_params=_cparams(dimension_semantics=("arbitrary",)),
    )(projm, projm, projm, projm, ffo, dqn, dkn, dct, dcr, dv, dfg, dsq, dsk, dsv, dsg, dmix, dmix, pooled, yp, qg, kg, bfp, wpd, ps)


def _stack_call(body, name, grid, in_specs, operands, slot_specs, slot_shapes, stacks, plain_specs=(), plain_shapes=(), **kw):
    out_specs = list(plain_specs) + list(slot_specs)
    out_shape = list(plain_shapes) + [jax.ShapeDtypeStruct((DEPTH,) + s, F32) for s in slot_shapes]
    if stacks is None:
        return pl.pallas_call(body, name=name, grid=grid, in_specs=in_specs, out_specs=out_specs, out_shape=out_shape, **kw)(*operands)
    n = len(operands)

    def aliased_body(*refs):
        body(*refs[:n], *refs[n + len(stacks):])

    return pl.pallas_call(
        aliased_body, name=name, grid=grid, in_specs=list(in_specs) + [pl.BlockSpec(memory_space=pl.ANY)] * len(stacks),
        out_specs=out_specs, out_shape=out_shape,
        input_output_aliases={n + k: len(plain_specs) + k for k in range(len(stacks))}, **kw)(*operands, *stacks)


def _inproj_dw(h, dproj, layer, stacks, *, ts, tn):
    S, D = h.shape
    nj = PM // tn

    def body(h_ref, dp_ref, dpf_ref, dw_ref, dwf_ref):
        s = pl.program_id(1)

        @pl.when(s == 0)
        def _():
            dw_ref[...] = jnp.zeros_like(dw_ref)

        @pl.when((s == 0) & (pl.program_id(0) == 0))
        def _():
            dwf_ref[...] = jnp.zeros_like(dwf_ref)

        hv = h_ref[...]
        dw_ref[...] += _dot_tn(dp_ref[...], hv)

        @pl.when(pl.program_id(0) == 0)
        def _():
            dwf_ref[...] += _dot_tn(dpf_ref[...], hv)

    return _stack_call(
        body, "inproj_dw", (nj, S // ts),
        [pl.BlockSpec((ts, D), lambda j, s: (s, 0)),
         pl.BlockSpec((ts, tn), lambda j, s: (s, j)),
         pl.BlockSpec((ts, LANES), lambda j, s: (s, PM // LANES))],
        (h, dproj, dproj),
        [pl.BlockSpec((None, tn, D), lambda j, s: (layer, j, 0)), pl.BlockSpec((None, LANES, D), lambda j, s: (layer, 0, 0))],
        [(PM, D), (LANES, D)], stacks,
        compiler_params=_cparams(dimension_semantics=("arbitrary", "arbitrary")))


def _inproj_dx(dproj, wt_all, layer, x, g, dy, *, tm):
    S, D = x.shape

    def body(dp_ref, w_ref, x_ref, g_ref, dy_ref, dx_ref, dg_ref):
        @pl.when(pl.program_id(0) == 0)
        def _():
            dg_ref[...] = jnp.zeros_like(dg_ref)

        dh = _dot(dp_ref[...], w_ref[...])
        xf = x_ref[...]
        rstd = lax.rsqrt(jnp.mean(xf * xf, axis=-1, keepdims=True) + EPS)
        xhat = xf * rstd
        dg_ref[...] += jnp.sum(dh * xhat, axis=0, keepdims=True)
        dyg = dh * g_ref[...]
        mean = jnp.mean(dyg * xhat, axis=-1, keepdims=True)
        dx_ref[...] = rstd * (dyg - xhat * mean) + dy_ref[...]

    row = lambda w: pl.BlockSpec((tm, w), lambda i: (i, 0))
    return pl.pallas_call(
        body, name="inproj_dx", grid=(S // tm,),
        in_specs=[row(PW), pl.BlockSpec((None, PW, D), lambda i: (layer, 0, 0)), row(D), pl.BlockSpec((1, D), lambda i: (0, 0)), row(D)],
        out_specs=[row(D), pl.BlockSpec((1, D), lambda i: (0, 0))],
        out_shape=[jax.ShapeDtypeStruct((S, D), F32), jax.ShapeDtypeStruct((1, D), F32)],
        compiler_params=_cparams(dimension_semantics=("arbitrary",)),
    )(dproj, wt_all, x, g, dy)


def _adam_update(w, g, m, v):
    nm = ADAM_B1 * m + (1.0 - ADAM_B1) * g
    nv = ADAM_B2 * v + (1.0 - ADAM_B2) * (g * g)
    m_hat = nm / (1.0 - ADAM_B1 ** ADAM_STEP)
    v_hat = nv / (1.0 - ADAM_B2 ** ADAM_STEP)
    return -ADAM_LR * (m_hat / (jnp.sqrt(v_hat) + ADAM_EPS) + ADAM_WD * w), nm, nv


def _adamw(w, g, m, v):
    L, R, C = w.shape
    tr = R if R <= 512 else 256

    def body(w_ref, g_ref, m_ref, v_ref, d_ref, nm_ref, nv_ref):
        d_ref[...], nm_ref[...], nv_ref[...] = _adam_update(w_ref[...], g_ref[...], m_ref[...], v_ref[...])

    spec = pl.BlockSpec((1, tr, C), lambda l, i: (l, i, 0))
    shp = jax.ShapeDtypeStruct((L, R, C), F32)
    return pl.pallas_call(
        body, name="adamw", grid=(L, R // tr), in_specs=[spec] * 4, out_specs=[spec] * 3, out_shape=[shp] * 3,
        compiler_params=_cparams(dimension_semantics=("arbitrary", "arbitrary")),
    )(w, g, m, v)


def _adamw_nd(w, g, m, v):
    shape = w.shape
    view = (1,) + shape if w.ndim == 2 else (shape[0], -1, shape[-1])
    outs = _adamw(w.reshape(view), g.reshape(view), m.reshape(view), v.reshape(view))
    return tuple(o.reshape(shape) for o in outs)


FLIP_C = (0, 0, 1)
FLIP_X = (1, 0, 0)
FLIP_Y = (0, 1, 0)
FLIP_XY = (1, 1, 0)
MESH = pl.DeviceIdType.MESH


def _peer(flip):
    me = (lax.axis_index("x"), lax.axis_index("y"), lax.axis_index("c"))
    return tuple(1 - a if f else a for a, f in zip(me, flip))


def _exchange(name, arrays, flips):
    n = len(arrays)

    def body(*refs):
        srcs, dsts = refs[:n], refs[n:2 * n]
        send_sems, recv_sems = refs[2 * n:]
        copies = [pltpu.make_async_remote_copy(src_ref=srcs[k], dst_ref=dsts[k], send_sem=send_sems.at[k], recv_sem=recv_sems.at[k],
                                               device_id=_peer(flips[k]), device_id_type=MESH) for k in range(n)]
        for cp in copies:
            cp.start()
        for cp in copies:
            cp.wait()

    anyspec = pl.BlockSpec(memory_space=pl.ANY)
    return pl.pallas_call(
        body, name=name, in_specs=[anyspec] * n, out_specs=[anyspec] * n,
        out_shape=[jax.ShapeDtypeStruct(a.shape, a.dtype) for a in arrays],
        scratch_shapes=[pltpu.SemaphoreType.DMA((n,)), pltpu.SemaphoreType.DMA((n,))],
    )(*arrays)


def _all_reduce_small(x):
    flips = (FLIP_C, FLIP_Y, FLIP_X)
    n = len(flips)

    def body(x_ref, o_ref, sum_ref, buf_ref, send_sems, recv_sems):
        src = x_ref
        for k, flip in enumerate(flips):
            cp = pltpu.make_async_remote_copy(src_ref=src, dst_ref=buf_ref.at[k], send_sem=send_sems.at[k], recv_sem=recv_sems.at[k],
                                              device_id=_peer(flip), device_id_type=MESH)
            cp.start()
            cp.wait()
            dst = o_ref if k == n - 1 else sum_ref.at[k]
            dst[...] = src[...] + buf_ref[k]
            src = dst

    vspec = pl.BlockSpec(memory_space=pltpu.VMEM)
    return pl.pallas_call(
        body, name="ar_small", in_specs=[vspec], out_specs=vspec, out_shape=jax.ShapeDtypeStruct(x.shape, x.dtype),
        scratch_shapes=[pltpu.VMEM((n - 1,) + x.shape, x.dtype), pltpu.VMEM((n,) + x.shape, x.dtype),
                        pltpu.SemaphoreType.DMA((n,)), pltpu.SemaphoreType.DMA((n,))],
    )(x)


def _chip_index():
    return 2 * lax.axis_index("x") + lax.axis_index("y")


def _gather_weights(w_in_t, w_out):
    wi = w_in_t.astype(BF16)
    wo = jnp.swapaxes(w_out, 0, 1).astype(BF16)
    halves = (wi.shape[0] // 2, wo.shape[0] // 2)
    ARR = 2
    TO_X, TO_Y, ON_Y, ON_X, SIB_X, SIB_Y, SIB_D0, SIB_D1, OWN = [ARR * k for k in range(9)]
    n_sems = ARR * 9

    def body(wi_ref, wo_ref, gi_ref, go_ref, send_sems, recv_sems):
        c = lax.axis_index("c")
        j = _chip_index()
        srcs = (wi_ref, wo_ref)
        dsts = (gi_ref, go_ref)
        def cuts(core):
            return [(pl.ds(h * core, h), pl.ds(h * core, h // 2), pl.ds(h * core + h // 2, h - h // 2)) for h in halves]
        mine, theirs = cuts(c), cuts(1 - c)
        HALF, Q0, Q1 = 0, 1, 2

        def copy(idx, src, dst, flip):
            return pltpu.make_async_remote_copy(src_ref=src, dst_ref=dst, send_sem=send_sems.at[idx], recv_sem=recv_sems.at[idx],
                                                device_id=_peer(flip), device_id_type=MESH)

        def slot(a, shard, cut):
            return dsts[a].at[shard, cut]

        jx, jy, jd = j ^ 2, j ^ 1, j ^ 3
        sends = []

        def start(cp):
            cp.start()
            sends.append(cp)

        for a in range(ARR):
            start(copy(TO_X + a, srcs[a].at[mine[a][HALF]], slot(a, j, mine[a][HALF]), FLIP_X))
            start(copy(TO_Y + a, srcs[a].at[mine[a][HALF]], slot(a, j, mine[a][HALF]), FLIP_Y))
        own = [copy(OWN + a, srcs[a], dsts[a].at[j], FLIP_C) for a in range(ARR)]
        for cp in own:
            cp.start()
        for a in range(ARR):
            copy(TO_X + a, slot(a, jx, mine[a][HALF]), slot(a, jx, mine[a][HALF]), FLIP_X).wait_recv()
            start(copy(ON_Y + a, slot(a, jx, mine[a][Q0]), slot(a, jx, mine[a][Q0]), FLIP_Y))
            start(copy(SIB_X + a, slot(a, jx, mine[a][HALF]), slot(a, jx, mine[a][HALF]), FLIP_C))
        for a in range(ARR):
            copy(TO_Y + a, slot(a, jy, mine[a][HALF]), slot(a, jy, mine[a][HALF]), FLIP_Y).wait_recv()
            start(copy(ON_X + a, slot(a, jy, mine[a][Q1]), slot(a, jy, mine[a][Q1]), FLIP_X))
            start(copy(SIB_Y + a, slot(a, jy, mine[a][HALF]), slot(a, jy, mine[a][HALF]), FLIP_C))
        for a in range(ARR):
            copy(ON_Y + a, slot(a, jd, mine[a][Q0]), slot(a, jd, mine[a][Q0]), FLIP_Y).wait_recv()
            start(copy(SIB_D0 + a, slot(a, jd, mine[a][Q0]), slot(a, jd, mine[a][Q0]), FLIP_C))
        for a in range(ARR):
            copy(ON_X + a, slot(a, jd, mine[a][Q1]), slot(a, jd, mine[a][Q1]), FLIP_X).wait_recv()
            start(copy(SIB_D1 + a, slot(a, jd, mine[a][Q1]), slot(a, jd, mine[a][Q1]), FLIP_C))
        for a in range(ARR):
            for idx, shard, cut in ((SIB_X, jx, HALF), (SIB_Y, jy, HALF), (SIB_D0, jd, Q0), (SIB_D1, jd, Q1)):
                copy(idx + a, slot(a, shard, theirs[a][cut]), slot(a, shard, theirs[a][cut]), FLIP_C).wait_recv()
        for cp in own:
            cp.wait()
        for cp in sends:
            cp.wait_send()

    anyspec = pl.BlockSpec(memory_space=pl.ANY)
    gi, go = pl.pallas_call(
        body, name="gather_weights", in_specs=[anyspec] * 2, out_specs=[anyspec] * 2,
        out_shape=[jax.ShapeDtypeStruct((4,) + wi.shape, BF16), jax.ShapeDtypeStruct((4,) + wo.shape, BF16)],
        scratch_shapes=[pltpu.SemaphoreType.DMA((n_sems,)), pltpu.SemaphoreType.DMA((n_sems,))],
    )(wi, wo)
    w_in_t_full = gi.reshape((4 * wi.shape[0],) + wi.shape[1:])
    w_out_full = jnp.swapaxes(go.reshape((4 * wo.shape[0],) + wo.shape[1:]), 0, 1)
    return w_in_t_full, w_out_full


def _to_aligned(w_t):
    _, L, D = w_t.shape
    npair = FOX_HEADS // 2
    ff = w_t[ORIG_FF:ORIG_REST].reshape(npair, 2, L, D)
    ff = jnp.pad(ff, ((0, 0), (0, FF_STRIDE - 2), (0, 0), (0, 0))).reshape(npair * FF_STRIDE, L, D)
    ff = jnp.pad(ff, ((0, LANES - npair * FF_STRIDE), (0, 0), (0, 0)))
    return jnp.swapaxes(jnp.concatenate([w_t[:ORIG_FOX], w_t[ORIG_REST:], ff], axis=0), 0, 1)


def _from_aligned(dw_t):
    n, _, D = dw_t.shape
    npair = FOX_HEADS // 2
    ff = dw_t[:, PM:PM + npair * FF_STRIDE].reshape(n, npair, FF_STRIDE, D)[:, :, :2].reshape(n, FOX_HEADS, D)
    return jnp.swapaxes(jnp.concatenate([dw_t[:, :ORIG_FOX], ff, dw_t[:, ORIG_FOX:PM]], axis=1), 0, 1)


RELAY_ROWS = 256


def _rows_first(stack_m, stack_f, got_m, got_f):
    n, _, D = got_m.shape
    npair = FOX_HEADS // 2
    first_late = ORIG_FOX // RELAY_ROWS

    def body(c_ref, m_ref, f_ref, gm_ref, gf_ref, out_ref, buf_ref, ff_ref, sem, ff_sem):
        i = pl.program_id(0)
        for l in range(n):
            buf_ref[:, l, :] = m_ref[l] + gm_ref[l].astype(F32)
        start = pl.multiple_of(i * RELAY_ROWS, FOX_HEADS) + jnp.where(i >= first_late, FOX_HEADS, 0)
        main = pltpu.make_async_copy(buf_ref, out_ref.at[pl.ds(start, RELAY_ROWS)], sem)
        main.start()

        @pl.when(i == 0)
        def _():
            for l in range(n):
                for p in range(npair):
                    rows = slice(FF_STRIDE * p, FF_STRIDE * p + 2)
                    ff_ref[2 * p:2 * p + 2, l, :] = f_ref[l, rows, :] + gf_ref[l, rows, :].astype(F32)
            ff = pltpu.make_async_copy(ff_ref, out_ref.at[pl.ds(ORIG_FF, FOX_HEADS)], ff_sem)
            ff.start()
            ff.wait()

        main.wait()

    grid_spec = pltpu.PrefetchScalarGridSpec(
        num_scalar_prefetch=1, grid=(PM // RELAY_ROWS,),
        in_specs=[pl.BlockSpec((n, RELAY_ROWS, D), lambda i, c: (c[0], i, 0)), pl.BlockSpec((n, LANES, D), lambda i, c: (c[0], 0, 0)),
                  pl.BlockSpec((n, RELAY_ROWS, D), lambda i, c: (0, i, 0)), pl.BlockSpec((n, LANES, D), lambda i, c: (0, 0, 0))],
        out_specs=pl.BlockSpec(memory_space=pl.ANY),
        scratch_shapes=[pltpu.VMEM((RELAY_ROWS, n, D), F32), pltpu.VMEM((FOX_HEADS, n, D), F32),
                        pltpu.SemaphoreType.DMA, pltpu.SemaphoreType.DMA])
    return pl.pallas_call(
        body, name="rs_rows_first", grid_spec=grid_spec, out_shape=jax.ShapeDtypeStruct((D_IN, n, D), F32),
        compiler_params=_cparams(dimension_semantics=("arbitrary",)),
    )(lax.axis_index("c").astype(jnp.int32).reshape(1), stack_m, stack_f, got_m, got_f)


def _half_layers(name, stack, got, also_bf16=True):
    L, R, C = stack.shape
    half = L // 2
    tr = min(256, R)
    c = lax.axis_index("c")
    which = ((1 - c) if got is None else c).astype(jnp.int32).reshape(1)

    def body(c_ref, x_ref, *refs):
        if got is None:
            refs[0][...] = x_ref[...].astype(BF16)
        else:
            acc = x_ref[...] + refs[0][...].astype(F32)
            refs[1][...] = acc
            if also_bf16:
                refs[2][...] = acc.astype(BF16)

    plain = pl.BlockSpec((1, tr, C), lambda l, i, c_ref: (l, i, 0))
    picked = pl.BlockSpec((1, tr, C), lambda l, i, c_ref: (c_ref[0] * half + l, i, 0))
    shp = lambda dt: jax.ShapeDtypeStruct((half, R, C), dt)
    out_shape = [shp(BF16)] if got is None else [shp(F32)] + ([shp(BF16)] if also_bf16 else [])
    grid_spec = pltpu.PrefetchScalarGridSpec(
        num_scalar_prefetch=1, grid=(half, R // tr),
        in_specs=[picked] + ([] if got is None else [plain]), out_specs=[plain] * len(out_shape))
    return pl.pallas_call(
        body, name=name, grid_spec=grid_spec, out_shape=out_shape,
        compiler_params=_cparams(dimension_semantics=("arbitrary", "arbitrary")),
    )(which, stack, *([] if got is None else [got]))


def _reduce_scatter(stack_m, stack_f, stack_o, shard_cols, shard_rows):
    j = _chip_index()
    half = DEPTH // 2
    stacks = (stack_m, stack_f, stack_o)
    give = [_half_layers("rs_give", s, None)[0] for s in stacks]
    got = _exchange("rs_d2d", give, (FLIP_C,) * len(stacks))
    o32, obf = _half_layers("rs_add_chip", stack_o, got[2])
    d_model = stack_m.shape[2]
    in32 = _rows_first(stack_m, stack_f, got[0], got[1]).reshape(4, shard_cols, half, d_model)

    def out_shards(o):
        return jnp.moveaxis(o.reshape(half, 4, shard_rows, o.shape[-1]), 1, 0)

    chip = [(in32, in32.astype(BF16), 0), (out_shards(o32), out_shards(obf), 1)]
    shard = lambda a, idx: lax.dynamic_index_in_dim(a, idx, axis=0, keepdims=False)
    via = []
    for _, bf, axis in chip:
        diag = shard(bf, j ^ 3)
        cut = diag.shape[axis] // 2
        via += [lax.slice_in_dim(diag, 0, cut, axis=axis), lax.slice_in_dim(diag, cut, 2 * cut, axis=axis)]
    handed = _exchange("rs_via", via, (FLIP_X, FLIP_Y) * len(chip))
    sends = []
    for a, (f32_sum, _, axis) in enumerate(chip):
        sends.append(_add_half_along("rs_add_via", f32_sum, handed[2 * a + 1], axis, 1, pick=j ^ 2))
        sends.append(_add_half_along("rs_add_via", f32_sum, handed[2 * a], axis, 0, pick=j ^ 1))
    got = _exchange("rs_ici", sends, (FLIP_X, FLIP_Y) * len(chip))
    mine_in = _add_rows("rs_add_in", chip[0][0], list(got[0:2]), pick=j)
    mine_out = _add_into_half("rs_add_out", shard(chip[1][0], j), list(got[2:4]))
    sib_in, g_out = _share_halves(mine_in, mine_out)
    return (mine_in, sib_in), g_out


def _picked(spec, pick):
    return pl.BlockSpec((None,) + tuple(spec.block_shape), lambda *a: (a[-1][0],) + tuple(spec.index_map(*a[:-1])))


def _add_half_along(name, base, extra, axis, which, pick=None):
    shape = base.shape if pick is None else base.shape[1:]
    lanes = min(ROW_LANE_CHUNK, shape[2])
    assert shape[axis] == 2 * extra.shape[axis]
    blk = tuple(shape[d] // 2 if d == axis else shape[d] for d in range(2)) + (lanes,)

    def body(*refs):
        b_ref, e_ref, o_ref = refs[-3:]
        x = b_ref[...]
        o_ref[...] = jnp.where(pl.program_id(0) == which, x + e_ref[...].astype(F32), x).astype(BF16)

    at = lambda i, k, *_: (i, 0, k) if axis == 0 else (0, i, k)
    bspec, espec = pl.BlockSpec(blk, at), pl.BlockSpec(blk, lambda i, k, *_: (0, 0, k))
    kw = dict(out_shape=jax.ShapeDtypeStruct(shape, BF16), name=name, compiler_params=_cparams(dimension_semantics=("arbitrary", "arbitrary")))
    grid = (2, shape[2] // lanes)
    if pick is None:
        return pl.pallas_call(body, grid=grid, in_specs=[bspec, espec], out_specs=bspec, **kw)(base, extra)
    grid_spec = pltpu.PrefetchScalarGridSpec(num_scalar_prefetch=1, grid=grid, in_specs=[_picked(bspec, pick), espec], out_specs=bspec)
    return pl.pallas_call(body, grid_spec=grid_spec, **kw)(pick.astype(jnp.int32).reshape(1), base, extra)


def _add_rows(name, first, others, pick=None):
    n = len(others)
    shape = first.shape if pick is None else first.shape[1:]

    def body(*refs):
        refs = refs[-(n + 2):]
        acc = refs[0][...]
        for r in refs[1:1 + n]:
            acc = acc + r[...].astype(F32)
        refs[1 + n][...] = acc

    grid, spec = _row_lane_blocks(shape)
    sp = spec(shape[1])
    kw = dict(out_shape=jax.ShapeDtypeStruct(shape, F32), name=name, compiler_params=_cparams(dimension_semantics=("arbitrary", "arbitrary")))
    if pick is None:
        return pl.pallas_call(body, grid=grid, in_specs=[sp] * (1 + n), out_specs=sp, **kw)(first, *others)
    grid_spec = pltpu.PrefetchScalarGridSpec(num_scalar_prefetch=1, grid=grid, in_specs=[_picked(sp, pick)] + [sp] * n, out_specs=sp)
    return pl.pallas_call(body, grid_spec=grid_spec, **kw)(pick.astype(jnp.int32).reshape(1), first, *others)


ROW_LANE_CHUNK = 256


def _row_lane_blocks(shape):
    rows, _, C = shape
    tr = rows // 2 if rows % 2 == 0 and rows > 64 else rows
    lanes = min(ROW_LANE_CHUNK, C)
    return (rows // tr, C // lanes), lambda n_mid: pl.BlockSpec((tr, n_mid, lanes), lambda i, k, *_: (i, 0, k))


def _add_into_half(name, first, others):
    half, rows, C = first.shape
    tr = min(256, rows)
    n = len(others)

    def body(c_ref, *refs):
        acc = refs[0][...]
        for r in refs[1:1 + n]:
            acc = acc + r[...].astype(F32)
        refs[1 + n][...] = acc

    grid_spec = pltpu.PrefetchScalarGridSpec(
        num_scalar_prefetch=1, grid=(half, rows // tr),
        in_specs=[pl.BlockSpec((1, tr, C), lambda l, i, c_ref: (l, i, 0))] * (1 + n),
        out_specs=pl.BlockSpec((1, tr, C), lambda l, i, c_ref: (c_ref[0] * half + l, i, 0)))
    return pl.pallas_call(
        body, name=name, grid_spec=grid_spec, out_shape=jax.ShapeDtypeStruct((2 * half, rows, C), F32),
        compiler_params=_cparams(dimension_semantics=("arbitrary", "arbitrary")),
    )(lax.axis_index("c").astype(jnp.int32).reshape(1), first, *others)


def _share_halves(mine, buf):
    half = DEPTH // 2

    def body(mine_ref, buf_in, sib_ref, buf_ref, send_sems, recv_sems):
        lay = pl.ds(half * lax.axis_index("c"), half)
        copies = [pltpu.make_async_remote_copy(src_ref=src, dst_ref=dst, send_sem=send_sems.at[k], recv_sem=recv_sems.at[k],
                                               device_id=_peer(FLIP_C), device_id_type=MESH)
                  for k, (src, dst) in enumerate(((mine_ref, sib_ref), (buf_ref.at[lay], buf_ref.at[lay])))]
        for cp in copies:
            cp.start()
        for cp in copies:
            cp.wait()

    anyspec = pl.BlockSpec(memory_space=pl.ANY)
    return pl.pallas_call(
        body, name="rs_share", in_specs=[anyspec] * 2, out_specs=[anyspec] * 2,
        out_shape=[jax.ShapeDtypeStruct(mine.shape, mine.dtype), jax.ShapeDtypeStruct(buf.shape, buf.dtype)],
        input_output_aliases={1: 1},
        scratch_shapes=[pltpu.SemaphoreType.DMA((2,)), pltpu.SemaphoreType.DMA((2,))],
    )(mine, buf)


def _adamw_halves(w, g_mine, g_sib, m, v):
    half = g_mine.shape[1]

    def body(c_ref, w_ref, gm_ref, gs_ref, m_ref, v_ref, g_ref, d_ref, nm_ref, nv_ref):
        first = c_ref[0] == 0
        gm, gs = gm_ref[...], gs_ref[...]
        for h, gv in enumerate((jnp.where(first, gm, gs), jnp.where(first, gs, gm))):
            lay = slice(half * h, half * (h + 1))
            g_ref[:, lay, :] = gv
            d_ref[:, lay, :], nm_ref[:, lay, :], nv_ref[:, lay, :] = _adam_update(w_ref[:, lay, :], gv, m_ref[:, lay, :], v_ref[:, lay, :])

    grid, spec = _row_lane_blocks(w.shape)
    full, part = spec(w.shape[1]), spec(half)
    grid_spec = pltpu.PrefetchScalarGridSpec(num_scalar_prefetch=1, grid=grid, in_specs=[full, part, part, full, full], out_specs=[full] * 4)
    return pl.pallas_call(
        body, name="adamw_halves", grid_spec=grid_spec, out_shape=[jax.ShapeDtypeStruct(w.shape, F32)] * 4,
        compiler_params=_cparams(dimension_semantics=("arbitrary", "arbitrary")),
    )(lax.axis_index("c").astype(jnp.int32).reshape(1), w, g_mine, g_sib, m, v)


def _blocks(S):
    return dict(tm=min(512, S), tm_proj=min(1024, S), ts=min(512, S), tq=min(512, S), tq_big=min(1024, S), tk=min(512, S), tks=min(256, S))


def _pair_pad(vec):
    npair = FOX_HEADS // 2
    v = jnp.pad(vec.reshape(npair, 2), ((0, 0), (0, FF_STRIDE - 2))).reshape(1, npair * FF_STRIDE)
    return jnp.pad(v, ((0, 0), (0, LANES - npair * FF_STRIDE)))


def _pair_unpad(row):
    npair = FOX_HEADS // 2
    return row[0, :npair * FF_STRIDE].reshape(npair, FF_STRIDE)[:, :2].reshape(FOX_HEADS)


def _pool_blockdiag(w_pool):
    g, cg, _ = w_pool.shape
    eye = jnp.eye(g, dtype=w_pool.dtype)
    return jnp.einsum("gh,gcd->gchd", eye, w_pool).reshape(g * cg, g * cg)


QK_BOUND_SLACK = 1.05


def _layer_params(norm_g, b_f, q_norm_g, k_norm_g, w_pool, pool_scale):
    qk_bound = QK_BOUND_SLACK * HEAD_DIM * QK_SCALE * jnp.max(jnp.abs(q_norm_g)) * jnp.max(jnp.abs(k_norm_g))
    return dict(g=norm_g.reshape(1, -1), qg=jnp.tile(q_norm_g, FOX_HEADS).reshape(1, FOX_W), kg=jnp.tile(k_norm_g, FOX_HEADS).reshape(1, FOX_W),
                bfp=_pair_pad(b_f), wpd=_pool_blockdiag(w_pool).astype(BF16), ps=pool_scale.reshape(1, POOL_W),
                qkb=jnp.full((1, LANES), qk_bound, F32))


def _layer_fwd(x, wt_all, w_out, layer, prm, bs, target=None):
    projm, ffo, h = _inproj(x, prm["g"], wt_all, layer, tm=bs["tm_proj"], tn=PROJ_TN)
    qn, ka, kb, v, sq, sk, sv, pooled, yp, pm = _prep(projm, ffo, prm["qg"], prm["kg"], prm["bfp"], prm["wpd"], prm["ps"], ts=bs["ts"])
    o, lse, fm = _fox_fwd(qn, ka, kb, v, projm, prm["qkb"], tq=bs["tq"], tk=bs["tk"])
    so, sm = _sb_fwd(sq, sk, sv, projm, tq=bs["tq"], tk=bs["tks"])
    y = _outproj(x, fm, pm, sm, w_out, layer, tm=bs["tm_proj"], target=target)
    saved = dict(x=x, projm=projm, ffo=ffo, h=h, qn=qn, ka=ka, kb=kb, v=v, sq=sq, sk=sk, sv=sv, pooled=pooled, yp=yp,
                 o=o, lse=lse, so=so, fm=fm, pm=pm, sm=sm)
    return y, saved


def _layer_bwd(dy, wt_all, w_out, prm, sv_, bs, layer, stacks):
    dmix, stack_o = _outproj_bwd(dy, sv_["fm"], sv_["pm"], sv_["sm"], w_out, layer, None if stacks is None else stacks[2:], tm=bs["tm_proj"])
    dqn, dkn, dv, dfg, dct, dcr = _fox_bwd(sv_["qn"], sv_["ka"], sv_["kb"], sv_["v"], sv_["o"], sv_["lse"], dmix, sv_["projm"],
                                      prm["qkb"], tq=bs["tq_big"], tk=bs["tk"])
    dsq, dsk, dsv, dsg = _sb_bwd(sv_["sq"], sv_["sk"], sv_["sv"], sv_["so"], dmix, sv_["projm"], tq=bs["tks"], tk=bs["tks"])
    dproj, dqg, dkg, dbf, dwp, dps = _prep_bwd(sv_["projm"], sv_["ffo"], dqn, dkn, dct, dcr, dv, dfg, dsq, dsk, dsv, dsg, dmix,
                                               sv_["pooled"], sv_["yp"], prm["qg"], prm["kg"], prm["bfp"], prm["wpd"], prm["ps"], ts=bs["ts"])
    stack_m, stack_f = _inproj_dw(sv_["h"], dproj, layer, None if stacks is None else stacks[:2], ts=bs["tm_proj"], tn=PROJ_TN)
    dx, dg = _inproj_dx(dproj, wt_all, layer, sv_["x"], prm["g"], dy, tm=bs["tm"])
    grads = dict(
        norm_g=dg[0],
        b_f=_pair_unpad(dbf), q_norm_g=dqg.reshape(FOX_HEADS, HEAD_DIM).sum(0), k_norm_g=dkg.reshape(FOX_HEADS, HEAD_DIM).sum(0),
        w_pool=jnp.stack([dwp[HEAD_DIM * g:HEAD_DIM * (g + 1), HEAD_DIM * g:HEAD_DIM * (g + 1)] for g in range(4)]),
        pool_scale=dps[0])
    return dx, grads, (stack_m, stack_f, stack_o)


def _local_step(x, target, wt_all, w_out, norm_g, b_f, q_norm_g, k_norm_g, w_pool, pool_scale):
    S, D = x.shape
    bs = _blocks(S)
    prms = [_layer_params(norm_g[l], b_f[l], q_norm_g[l], k_norm_g[l], w_pool[l], pool_scale[l]) for l in range(DEPTH)]
    saved = []
    y = x
    for l in range(DEPTH):
        y, s_ = _layer_fwd(y, wt_all, w_out, l, prms[l], bs, target if l == DEPTH - 1 else None)
        saved.append(s_)
    dy, sq = y
    loss = 0.5 * jnp.sum(sq) / D
    grads = [None] * DEPTH
    stacks = None
    for l in reversed(range(DEPTH)):
        dy, grads[l], stacks = _layer_bwd(dy, wt_all, w_out, prms[l], saved[l], bs, l, stacks)
    stacked = {k: jnp.stack([g[k] for g in grads]) for k in grads[0]}
    return loss, dy, stacked, stacks


SMALL = ("norm_g", "b_f", "q_norm_g", "k_norm_g", "w_pool", "pool_scale")


def _pack_small(gr, loss):
    flat = jnp.concatenate([gr[k].reshape(-1) for k in SMALL] + [loss.reshape(1)])
    pad = (-flat.shape[0]) % (8 * LANES)
    return jnp.pad(flat, (0, pad)).reshape(-1, LANES)


def _unpack_small(packed, like):
    flat = packed.reshape(-1)
    out, off = {}, 0
    for k in SMALL:
        n = like[k].size
        out[k] = flat[off:off + n].reshape(like[k].shape)
        off += n
    return out, flat[off]


def kernel(x, norm_g, w_in, b_f, q_norm_g, k_norm_g, w_pool, pool_scale, w_out, loss_target, m_norm_g, m_w_in, m_b_f, m_q_norm_g, m_k_norm_g, m_w_pool, m_pool_scale, m_w_out, v_norm_g, v_w_in, v_b_f, v_q_norm_g, v_k_norm_g, v_w_pool, v_pool_scale, v_w_out):
    weights = dict(norm_g=norm_g, w_in=w_in, b_f=b_f, q_norm_g=q_norm_g, k_norm_g=k_norm_g, w_pool=w_pool, pool_scale=pool_scale, w_out=w_out)
    mom_m = dict(norm_g=m_norm_g, w_in=m_w_in, b_f=m_b_f, q_norm_g=m_q_norm_g, k_norm_g=m_k_norm_g, w_pool=m_w_pool, pool_scale=m_pool_scale, w_out=m_w_out)
    mom_v = dict(norm_g=v_norm_g, w_in=v_w_in, b_f=v_b_f, q_norm_g=v_q_norm_g, k_norm_g=v_k_norm_g, w_pool=v_w_pool, pool_scale=v_pool_scale, w_out=v_w_out)
    shard_cols = w_in.shape[2]
    shard_rows = w_out.shape[1]

    cols_first = lambda a: jnp.transpose(a, (2, 0, 1))
    w_in_t = cols_first(w_in)
    w_in_t_full, w_out_full = _gather_weights(w_in_t, w_out)
    wt_all = _to_aligned(w_in_t_full)
    loss, dx, gr, stacks = _local_step(x[0], loss_target[0], wt_all, w_out_full, norm_g, b_f, q_norm_g, k_norm_g, w_pool, pool_scale)

    (g_in_mine, g_in_sib), g_w_out = _reduce_scatter(*stacks, shard_cols, shard_rows)
    small, loss = _unpack_small(_all_reduce_small(_pack_small(gr, loss)), {k: weights[k] for k in SMALL})
    grad_w = dict(small, w_out=g_w_out)

    names = ("norm_g", "w_in", "b_f", "q_norm_g", "k_norm_g", "w_pool", "pool_scale", "w_out")
    upd = {k: _adamw_nd(weights[k], grad_w[k], mom_m[k], mom_v[k]) for k in names if k != "w_in"}
    in_t = _adamw_halves(w_in_t, g_in_mine, g_in_sib, cols_first(mom_m["w_in"]), cols_first(mom_v["w_in"]))
    grad_w["w_in"], *upd["w_in"] = [jnp.transpose(a, (1, 2, 0)) for a in in_t]
    return (loss, dx[None], *[grad_w[k] for k in names], *[upd[k][0] for k in names], *[upd[k][1] for k in names], *[upd[k][2] for k in names])
```
